```python
import math
import jax, jax.numpy as jnp
from jax import lax
import numpy as np

D_MODEL = 1024
BATCH = 8
SEQ = 8192
DEPTH = 2

N_ATTN_LAYERS = (DEPTH + 1) // 2
N_REC_LAYERS = DEPTH // 2
HEAD_DIM = 64
BLOCK = 128
A_Q_HEADS = 8
A_KV_HEADS = 2
A_WINDOW = 128
B_HEADS = 8
B_BRANCHES = ((128, 1), (512, 4), (2048, 16))
N_ATTN_HEADS = A_Q_HEADS + B_HEADS
ATTN_SPLITS = [A_Q_HEADS * HEAD_DIM, A_KV_HEADS * HEAD_DIM, A_KV_HEADS * HEAD_DIM,
               B_HEADS * HEAD_DIM, B_HEADS * HEAD_DIM, B_HEADS * HEAD_DIM]
ATTN_IN = sum(ATTN_SPLITS)
ATTN_OUT = N_ATTN_HEADS * HEAD_DIM
S5_GROUP = 16
S5_GROUPS = 16
S5_WIDTH = S5_GROUP * S5_GROUPS
S5_STATE = 64
DN_HEADS = 6
DN_DK = 128
DN_DV = 128
DN_CONV = 4
DN_CHUNK = 64
REC_SPLITS = [S5_WIDTH, DN_HEADS * DN_DK, DN_HEADS * DN_DK, DN_HEADS * DN_DV,
              DN_HEADS * DN_DV, DN_HEADS, DN_HEADS]
REC_IN = sum(REC_SPLITS)
REC_OUT = S5_WIDTH + DN_HEADS * DN_DV
D_FF = 2816
FFN_CONV = 3
EPS = 1e-6

kernel_name = "hybrid_swa_dilated_s5_deltanet_block"


def split_cols(t, sizes):
    offs = np.cumsum(sizes)[:-1]
    return jnp.split(t, [int(o) for o in offs], axis=-1)


def rms_norm(x, w):
    xf = x.astype(jnp.float32)
    y = xf * lax.rsqrt(jnp.mean(xf * xf, axis=-1, keepdims=True) + EPS)
    return (y * w.astype(jnp.float32)).astype(x.dtype)


def causal_dwconv(x, w):
    width, ch = w.shape
    xp = jnp.pad(x, ((0, 0), (width - 1, 0), (0, 0)))
    return lax.conv_general_dilated(xp, w[:, None, :].astype(x.dtype), window_strides=(1,),
                                    padding="VALID", dimension_numbers=("NWC", "WIO", "NWC"),
                                    feature_group_count=ch)


def alibi_slopes(n):
    return jnp.asarray(2.0 ** (-8.0 * np.arange(1, n + 1) / n), dtype=jnp.float32)


def banded_attention(q, k, v, slopes, step, max_dist):
    b, L, K, R, hd = q.shape
    nb = L // BLOCK
    qb = q.reshape(b, nb, BLOCK, K, R, hd)
    pad = ((0, 0), (BLOCK, 0), (0, 0), (0, 0))
    kb = jnp.pad(k, pad).reshape(b, nb + 1, BLOCK, K, hd)
    vb = jnp.pad(v, pad).reshape(b, nb + 1, BLOCK, K, hd)
    kw = jnp.concatenate([kb[:, :-1], kb[:, 1:]], axis=2)
    vw = jnp.concatenate([vb[:, :-1], vb[:, 1:]], axis=2)
    s = jnp.einsum("bnqkrd,bnskd->bnkrqs", qb, kw,
                   preferred_element_type=jnp.float32) * (hd ** -0.5)
    dist = BLOCK + jnp.arange(BLOCK)[:, None] - jnp.arange(2 * BLOCK)[None, :]
    after_start = (jnp.arange(nb)[:, None, None] > 0) | (jnp.arange(2 * BLOCK)[None, None, :] >= BLOCK)
    valid = (dist >= 0) & (dist <= max_dist) & after_start
    bias = -slopes.astype(jnp.float32)[:, :, None, None] * (step * dist).astype(jnp.float32)
    s = jnp.where(valid[None, :, None, None], s + bias, -jnp.inf)
    m = jnp.max(s, axis=-1, keepdims=True)
    p = jnp.exp(s - m)
    l = jnp.sum(p, axis=-1)
    o = jnp.einsum("bnkrqs,bnskd->bnqkrd", p.astype(v.dtype), vw,
                   preferred_element_type=jnp.float32)
    o = o / jnp.moveaxis(l, -1, 2)[..., None]
    lse = jnp.moveaxis(m[..., 0] + jnp.log(l), -1, 2)
    return o.reshape(b, L, K, R, hd).astype(q.dtype), lse.reshape(b, L, K, R)


def dilated_branch(q, k, v, slopes, window, dilation):
    b, L, H, hd = q.shape
    span = dilation * BLOCK
    Lp = -(-L // span) * span
    pad = ((0, 0), (0, Lp - L), (0, 0), (0, 0))

    def strided(t):
        return jnp.pad(t, pad).reshape(b, Lp // dilation, dilation * H, hd)

    o, lse = banded_attention(strided(q)[:, :, :, None, :], strided(k), strided(v),
                              jnp.tile(slopes, dilation)[:, None], dilation, window // dilation)
    o = o[:, :, :, 0].reshape(b, Lp, H, hd)[:, :L]
    lse = lse[..., 0].reshape(b, Lp, H)[:, :L]
    return o, lse


def attention_mixer(h, w_in, q_norm_a, k_norm_a, q_norm_b, k_norm_b, sinks, w_out):
    b, L, _ = h.shape
    rep = A_Q_HEADS // A_KV_HEADS
    qa, ka, va, qb, kb, vb = split_cols(h @ w_in, ATTN_SPLITS)
    slopes = alibi_slopes(N_ATTN_HEADS)
    qa = rms_norm(qa.reshape(b, L, A_KV_HEADS, rep, HEAD_DIM), q_norm_a)
    ka = rms_norm(ka.reshape(b, L, A_KV_HEADS, HEAD_DIM), k_norm_a)
    va = va.reshape(b, L, A_KV_HEADS, HEAD_DIM)
    oa, lse_a = banded_attention(qa, ka, va, slopes[:A_Q_HEADS].reshape(A_KV_HEADS, rep),
                                 1, A_WINDOW - 1)
    keep = jax.nn.sigmoid(lse_a - sinks.astype(jnp.float32).reshape(A_KV_HEADS, rep))
    oa = (oa.astype(jnp.float32) * keep[..., None]).reshape(b, L, A_Q_HEADS * HEAD_DIM)
    qb = rms_norm(qb.reshape(b, L, B_HEADS, HEAD_DIM), q_norm_b)
    kb = rms_norm(kb.reshape(b, L, B_HEADS, HEAD_DIM), k_norm_b)
    vb = vb.reshape(b, L, B_HEADS, HEAD_DIM)
    outs, lses = [], []
    for window, dilation in B_BRANCHES:
        o, l = dilated_branch(qb, kb, vb, slopes[A_Q_HEADS:], window, dilation)
        outs.append(o)
        lses.append(l)
    wts = jax.nn.softmax(jnp.stack(lses), axis=0)
    ob = jnp.einsum("gblh,gblhd->blhd", wts, jnp.stack(outs).astype(jnp.float32))
    ob = ob.reshape(b, L, B_HEADS * HEAD_DIM)
    return jnp.concatenate([oa, ob], axis=-1).astype(h.dtype) @ w_out


def s5_mixer(u, lam_re, lam_im, log_dt, b_re, b_im, c_re, c_im, d_skip, glu_w, glu_b):
    f32 = jnp.float32
    bsz, L, _ = u.shape
    uf = u.astype(f32).reshape(bsz, L, S5_GROUPS, S5_GROUP)
    lr, li = lam_re.astype(f32), lam_im.astype(f32)
    dt = jnp.exp(log_dt.astype(f32))[:, None]
    mag, ang = jnp.exp(lr * dt), li * dt
    ab_re, ab_im = mag * jnp.cos(ang), mag * jnp.sin(ang)
    nr, ni = ab_re - 1.0, ab_im
    den = lr * lr + li * li
    f_re = (nr * lr + ni * li) / den
    f_im = (ni * lr - nr * li) / den
    bu_re = jnp.einsum("blgi,gpi->blgp", uf, b_re.astype(f32))
    bu_im = jnp.einsum("blgi,gpi->blgp", uf, b_im.astype(f32))
    e_re = f_re * bu_re - f_im * bu_im
    e_im = f_re * bu_im + f_im * bu_re
    a_re = jnp.broadcast_to(ab_re, e_re.shape)
    a_im = jnp.broadcast_to(ab_im, e_im.shape)

    def combine(e1, e2):
        a1r, a1i, b1r, b1i = e1
        a2r, a2i, b2r, b2i = e2
        return (a2r * a1r - a2i * a1i, a2r * a1i + a2i * a1r,
                a2r * b1r - a2i * b1i + b2r, a2r * b1i + a2i * b1r + b2i)

    _, _, x_re, x_im = lax.associative_scan(combine, (a_re, a_im, e_re, e_im), axis=1)
    y = (jnp.einsum("blgp,gip->blgi", x_re, c_re.astype(f32))
         - jnp.einsum("blgp,gip->blgi", x_im, c_im.astype(f32))
         + d_skip.astype(f32).reshape(S5_GROUPS, S5_GROUP) * uf)
    g = jax.nn.gelu(y.reshape(bsz, L, S5_WIDTH))
    return (g * jax.nn.sigmoid(g @ glu_w.astype(f32) + glu_b.astype(f32))).astype(u.dtype)


def chunk_gated_delta_rule(q, k, v, g, beta):
    b, L, H, dk = q.shape
    dv = v.shape[-1]
    n, C = L // DN_CHUNK, DN_CHUNK

    def chunks(t):
        return jnp.moveaxis(t.reshape((b, n, C) + t.shape[2:]), 3, 2)

    q, k, v, g, beta = chunks(q), chunks(k), chunks(v), chunks(g), chunks(beta)
    G = jnp.cumsum(g, axis=-1)
    causal = jnp.tril(jnp.ones((C, C), bool))
    strict = jnp.tril(jnp.ones((C, C), bool), -1)
    diff = G[..., :, None] - G[..., None, :]
    gamma = jnp.where(causal, jnp.exp(jnp.where(causal, diff, 0.0)), 0.0)
    kk = jnp.einsum("bnhid,bnhjd->bnhij", k, k)
    n_mat = jnp.where(strict, beta[..., :, None] * kk * gamma, 0.0)
    rhs = jnp.concatenate([v * beta[..., None], k * (beta * jnp.exp(G))[..., None]], axis=-1)
    sol = lax.linalg.triangular_solve(n_mat + jnp.eye(C, dtype=jnp.float32), rhs,
                                      left_side=True, lower=True, unit_diagonal=True)
    u, w = sol[..., :dv], sol[..., dv:]
    qk = jnp.einsum("bnhid,bnhjd->bnhij", q, k) * gamma
    q_dec = q * jnp.exp(G)[..., None]
    k_dec = k * jnp.exp(G[..., -1:] - G)[..., None]
    g_last = jnp.exp(G[..., -1])

    def step(S, xs):
        u_c, w_c, qk_c, qd_c, kd_c, gl_c = xs
        v_new = u_c - jnp.einsum("bhcd,bhde->bhce", w_c, S)
        o = jnp.einsum("bhcd,bhde->bhce", qd_c, S) + jnp.einsum("bhij,bhje->bhie", qk_c, v_new)
        S = S * gl_c[..., None, None] + jnp.einsum("bhcd,bhce->bhde", kd_c, v_new)
        return S, o

    xs = tuple(jnp.moveaxis(t, 1, 0) for t in (u, w, qk, q_dec, k_dec, g_last))
    _, o = lax.scan(step, jnp.zeros((b, H, dk, dv), jnp.float32), xs)
    return jnp.moveaxis(jnp.moveaxis(o, 0, 1), 2, 3).reshape(b, L, H, dv)


def gated_deltanet_mixer(q, k, v, z, a, beta_raw, conv_w, a_log, dt_bias, out_norm):
    f32 = jnp.float32
    bsz, L, _ = q.shape
    qkv = jax.nn.silu(causal_dwconv(jnp.concatenate([q, k, v], axis=-1), conv_w)).astype(f32)
    q, k, v = split_cols(qkv, [DN_HEADS * DN_DK, DN_HEADS * DN_DK, DN_HEADS * DN_DV])
    q = q.reshape(bsz, L, DN_HEADS, DN_DK)
    k = k.reshape(bsz, L, DN_HEADS, DN_DK)
    v = v.reshape(bsz, L, DN_HEADS, DN_DV)
    q = q * lax.rsqrt(jnp.sum(q * q, axis=-1, keepdims=True) + EPS) * (DN_DK ** -0.5)
    k = k * lax.rsqrt(jnp.sum(k * k, axis=-1, keepdims=True) + EPS)
    beta = jax.nn.sigmoid(beta_raw.astype(f32))
    g = -jnp.exp(a_log.astype(f32)) * jax.nn.softplus(a.astype(f32) + dt_bias.astype(f32))
    o = chunk_gated_delta_rule(q, k, v, g, beta)
    o = rms_norm(o, out_norm) * jax.nn.silu(z.astype(f32).reshape(bsz, L, DN_HEADS, DN_DV))
    return o.reshape(bsz, L, DN_HEADS * DN_DV)


def recurrent_mixer(h, w_in, lam_re, lam_im, log_dt, b_re, b_im, c_re, c_im, d_skip, glu_w,
                    glu_b, dn_conv, a_log, dt_bias, out_norm, w_out):
    u, q, k, v, z, a, beta_raw = split_cols(h @ w_in, REC_SPLITS)
    yc = s5_mixer(u, lam_re, lam_im, log_dt, b_re, b_im, c_re, c_im, d_skip, glu_w, glu_b)
    yd = gated_deltanet_mixer(q, k, v, z, a, beta_raw, dn_conv, a_log, dt_bias, out_norm)
    return jnp.concatenate([yc.astype(h.dtype), yd.astype(h.dtype)], axis=-1) @ w_out


def conv_ffn(h, w_up, conv_w, w_down):
    up = causal_dwconv(h @ w_up, conv_w)
    a, b = jnp.split(up, 2, axis=-1)
    return (jax.nn.silu(a) * b) @ w_down


def modulate(x, norm_w, shift, scale):
    return rms_norm(x, norm_w) * (1.0 + scale[:, None, :]) + shift[:, None, :]


def _fwd_setup_inputs(seed: int = 0) -> dict:
    key = jax.random.key(seed)
    ks = iter(jax.random.split(key, 40))
    f32 = jnp.float32
    na, nr = N_ATTN_LAYERS, N_REC_LAYERS

    def nrm(shape, scale):
        return jax.random.normal(next(ks), shape, f32) * scale

    def gain(shape):
        return 1.0 + nrm(shape, 0.02)

    x = nrm((BATCH, SEQ, D_MODEL), 1.0)
    c = nrm((BATCH, D_MODEL), 1.0)
    ada_w = nrm((DEPTH, D_MODEL, 6 * D_MODEL), 0.5 * D_MODEL ** -0.5)
    ada_b = nrm((DEPTH, 6 * D_MODEL), 0.02)
    norm_mix = gain((DEPTH, D_MODEL))
    norm_ffn = gain((DEPTH, D_MODEL))
    attn_w_in = nrm((na, D_MODEL, ATTN_IN), D_MODEL ** -0.5)
    attn_q_norm_a = gain((na, HEAD_DIM))
    attn_k_norm_a = gain((na, HEAD_DIM))
    attn_q_norm_b = gain((na, HEAD_DIM))
    attn_k_norm_b = gain((na, HEAD_DIM))
    attn_sinks = nrm((na, A_Q_HEADS), 1.0)
    attn_w_out = nrm((na, ATTN_OUT, D_MODEL), ATTN_OUT ** -0.5)
    rec_w_in = nrm((nr, D_MODEL, REC_IN), D_MODEL ** -0.5)
    s5_lambda_re = -0.5 + nrm((nr, S5_GROUPS, S5_STATE), 0.01)
    s5_lambda_im = jnp.pi * jnp.arange(S5_STATE, dtype=f32) + nrm((nr, S5_GROUPS, S5_STATE), 0.01)
    s5_log_dt = jax.random.uniform(next(ks), (nr, S5_GROUPS), f32, math.log(1e-3), math.log(1e-1))
    s5_b_re = nrm((nr, S5_GROUPS, S5_STATE, S5_GROUP), (2 * S5_GROUP) ** -0.5)
    s5_b_im = nrm((nr, S5_GROUPS, S5_STATE, S5_GROUP), (2 * S5_GROUP) ** -0.5)
    s5_c_re = nrm((nr, S5_GROUPS, S5_GROUP, S5_STATE), S5_STATE ** -0.5)
    s5_c_im = nrm((nr, S5_GROUPS, S5_GROUP, S5_STATE), S5_STATE ** -0.5)
    s5_d = nrm((nr, S5_WIDTH), 1.0)
    s5_glu_w = nrm((nr, S5_WIDTH, S5_WIDTH), S5_WIDTH ** -0.5)
    s5_glu_b = nrm((nr, S5_WIDTH), 0.02)
    dn_conv = nrm((nr, DN_CONV, DN_HEADS * (2 * DN_DK + DN_DV)), DN_CONV ** -0.5)
    dn_a_log = jnp.log(jax.random.uniform(next(ks), (nr, DN_HEADS), f32, 1.0, 16.0))
    dt0 = jnp.exp(jax.random.uniform(next(ks), (nr, DN_HEADS), f32, math.log(1e-3), math.log(1e-1)))
    dn_dt_bias = dt0 + jnp.log(-jnp.expm1(-dt0))
    dn_out_norm = gain((nr, DN_DV))
    rec_w_out = nrm((nr, REC_OUT, D_MODEL), REC_OUT ** -0.5)
    ffn_w_up = nrm((DEPTH, D_MODEL, 2 * D_FF), D_MODEL ** -0.5)
    ffn_conv = nrm((DEPTH, FFN_CONV, 2 * D_FF), FFN_CONV ** -0.5)
    ffn_w_down = nrm((DEPTH, D_FF, D_MODEL), D_FF ** -0.5)
    return {"x": x, "c": c, "ada_w": ada_w, "ada_b": ada_b, "norm_mix": norm_mix,
            "norm_ffn": norm_ffn, "attn_w_in": attn_w_in, "attn_q_norm_a": attn_q_norm_a,
            "attn_k_norm_a": attn_k_norm_a, "attn_q_norm_b": attn_q_norm_b,
            "attn_k_norm_b": attn_k_norm_b, "attn_sinks": attn_sinks, "attn_w_out": attn_w_out,
            "rec_w_in": rec_w_in, "s5_lambda_re": s5_lambda_re, "s5_lambda_im": s5_lambda_im,
            "s5_log_dt": s5_log_dt, "s5_b_re": s5_b_re, "s5_b_im": s5_b_im, "s5_c_re": s5_c_re,
            "s5_c_im": s5_c_im, "s5_d": s5_d, "s5_glu_w": s5_glu_w, "s5_glu_b": s5_glu_b,
            "dn_conv": dn_conv, "dn_a_log": dn_a_log, "dn_dt_bias": dn_dt_bias,
            "dn_out_norm": dn_out_norm, "rec_w_out": rec_w_out, "ffn_w_up": ffn_w_up,
            "ffn_conv": ffn_conv, "ffn_w_down": ffn_w_down}


def _fwd_reference(x, c, ada_w, ada_b, norm_mix, norm_ffn, attn_w_in, attn_q_norm_a, attn_k_norm_a,
              attn_q_norm_b, attn_k_norm_b, attn_sinks, attn_w_out, rec_w_in, s5_lambda_re,
              s5_lambda_im, s5_log_dt, s5_b_re, s5_b_im, s5_c_re, s5_c_im, s5_d, s5_glu_w,
              s5_glu_b, dn_conv, dn_a_log, dn_dt_bias, dn_out_norm, rec_w_out, ffn_w_up,
              ffn_conv, ffn_w_down):
    cond = jax.nn.silu(c)
    for layer in range(DEPTH):
        mod = cond @ ada_w[layer] + ada_b[layer]
        sh1, sc1, g1, sh2, sc2, g2 = jnp.split(mod, 6, axis=-1)
        h = modulate(x, norm_mix[layer], sh1, sc1)
        i = layer // 2
        if layer % 2 == 0:
            y = attention_mixer(h, attn_w_in[i], attn_q_norm_a[i], attn_k_norm_a[i],
                                attn_q_norm_b[i], attn_k_norm_b[i], attn_sinks[i], attn_w_out[i])
        else:
            y = recurrent_mixer(h, rec_w_in[i], s5_lambda_re[i], s5_lambda_im[i], s5_log_dt[i],
                                s5_b_re[i], s5_b_im[i], s5_c_re[i], s5_c_im[i], s5_d[i],
                                s5_glu_w[i], s5_glu_b[i], dn_conv[i], dn_a_log[i], dn_dt_bias[i],
                                dn_out_norm[i], rec_w_out[i])
        x = x + g1[:, None, :] * y
        h = modulate(x, norm_ffn[layer], sh2, sc2)
        x = x + g2[:, None, :] * conv_ffn(h, ffn_w_up[layer], ffn_conv[layer], ffn_w_down[layer])
    return x


import jax as _jax
import jax.numpy as _jnp

TWIN_FORMAT = 'train_step'
FWD_PARAMS = ['x', 'c', 'ada_w', 'ada_b', 'norm_mix', 'norm_ffn', 'attn_w_in', 'attn_q_norm_a', 'attn_k_norm_a', 'attn_q_norm_b', 'attn_k_norm_b', 'attn_sinks', 'attn_w_out', 'rec_w_in', 's5_lambda_re', 's5_lambda_im', 's5_log_dt', 's5_b_re', 's5_b_im', 's5_c_re', 's5_c_im', 's5_d', 's5_glu_w', 's5_glu_b', 'dn_conv', 'dn_a_log', 'dn_dt_bias', 'dn_out_norm', 'rec_w_out', 'ffn_w_up', 'ffn_conv', 'ffn_w_down']
TWIN_WEIGHTS = ['ada_w', 'ada_b', 'norm_mix', 'norm_ffn', 'attn_w_in', 'attn_q_norm_a', 'attn_k_norm_a', 'attn_q_norm_b', 'attn_k_norm_b', 'attn_sinks', 'attn_w_out', 'rec_w_in', 's5_lambda_re', 's5_lambda_im', 's5_log_dt', 's5_b_re', 's5_b_im', 's5_c_re', 's5_c_im', 's5_d', 's5_glu_w', 's5_glu_b', 'dn_conv', 'dn_a_log', 'dn_dt_bias', 'dn_out_norm', 'rec_w_out', 'ffn_w_up', 'ffn_conv', 'ffn_w_down']
TWIN_DIFF_INPUT = 'x'
TWIN_INPUTS = ['x', 'c', 'ada_w', 'ada_b', 'norm_mix', 'norm_ffn', 'attn_w_in', 'attn_q_norm_a', 'attn_k_norm_a', 'attn_q_norm_b', 'attn_k_norm_b', 'attn_sinks', 'attn_w_out', 'rec_w_in', 's5_lambda_re', 's5_lambda_im', 's5_log_dt', 's5_b_re', 's5_b_im', 's5_c_re', 's5_c_im', 's5_d', 's5_glu_w', 's5_glu_b', 'dn_conv', 'dn_a_log', 'dn_dt_bias', 'dn_out_norm', 'rec_w_out', 'ffn_w_up', 'ffn_conv', 'ffn_w_down', 'loss_target', 'm_ada_w', 'm_ada_b', 'm_norm_mix', 'm_norm_ffn', 'm_attn_w_in', 'm_attn_q_norm_a', 'm_attn_k_norm_a', 'm_attn_q_norm_b', 'm_attn_k_norm_b', 'm_attn_sinks', 'm_attn_w_out', 'm_rec_w_in', 'm_s5_lambda_re', 'm_s5_lambda_im', 'm_s5_log_dt', 'm_s5_b_re', 'm_s5_b_im', 'm_s5_c_re', 'm_s5_c_im', 'm_s5_d', 'm_s5_glu_w', 'm_s5_glu_b', 'm_dn_conv', 'm_dn_a_log', 'm_dn_dt_bias', 'm_dn_out_norm', 'm_rec_w_out', 'm_ffn_w_up', 'm_ffn_conv', 'm_ffn_w_down', 'v_ada_w', 'v_ada_b', 'v_norm_mix', 'v_norm_ffn', 'v_attn_w_in', 'v_attn_q_norm_a', 'v_attn_k_norm_a', 'v_attn_q_norm_b', 'v_attn_k_norm_b', 'v_attn_sinks', 'v_attn_w_out', 'v_rec_w_in', 'v_s5_lambda_re', 'v_s5_lambda_im', 'v_s5_log_dt', 'v_s5_b_re', 'v_s5_b_im', 'v_s5_c_re', 'v_s5_c_im', 'v_s5_d', 'v_s5_glu_w', 'v_s5_glu_b', 'v_dn_conv', 'v_dn_a_log', 'v_dn_dt_bias', 'v_dn_out_norm', 'v_rec_w_out', 'v_ffn_w_up', 'v_ffn_conv', 'v_ffn_w_down']
TWIN_OUTPUTS = ['loss', 'grad_x', 'grad_ada_w', 'grad_ada_b', 'grad_norm_mix', 'grad_norm_ffn', 'grad_attn_w_in', 'grad_attn_q_norm_a', 'grad_attn_k_norm_a', 'grad_attn_q_norm_b', 'grad_attn_k_norm_b', 'grad_attn_sinks', 'grad_attn_w_out', 'grad_rec_w_in', 'grad_s5_lambda_re', 'grad_s5_lambda_im', 'grad_s5_log_dt', 'grad_s5_b_re', 'grad_s5_b_im', 'grad_s5_c_re', 'grad_s5_c_im', 'grad_s5_d', 'grad_s5_glu_w', 'grad_s5_glu_b', 'grad_dn_conv', 'grad_dn_a_log', 'grad_dn_dt_bias', 'grad_dn_out_norm', 'grad_rec_w_out', 'grad_ffn_w_up', 'grad_ffn_conv', 'grad_ffn_w_down', 'delta_ada_w', 'delta_ada_b', 'delta_norm_mix', 'delta_norm_ffn', 'delta_attn_w_in', 'delta_attn_q_norm_a', 'delta_attn_k_norm_a', 'delta_attn_q_norm_b', 'delta_attn_k_norm_b', 'delta_attn_sinks', 'delta_attn_w_out', 'delta_rec_w_in', 'delta_s5_lambda_re', 'delta_s5_lambda_im', 'delta_s5_log_dt', 'delta_s5_b_re', 'delta_s5_b_im', 'delta_s5_c_re', 'delta_s5_c_im', 'delta_s5_d', 'delta_s5_glu_w', 'delta_s5_glu_b', 'delta_dn_conv', 'delta_dn_a_log', 'delta_dn_dt_bias', 'delta_dn_out_norm', 'delta_rec_w_out', 'delta_ffn_w_up', 'delta_ffn_conv', 'delta_ffn_w_down', 'new_m_ada_w', 'new_m_ada_b', 'new_m_norm_mix', 'new_m_norm_ffn', 'new_m_attn_w_in', 'new_m_attn_q_norm_a', 'new_m_attn_k_norm_a', 'new_m_attn_q_norm_b', 'new_m_attn_k_norm_b', 'new_m_attn_sinks', 'new_m_attn_w_out', 'new_m_rec_w_in', 'new_m_s5_lambda_re', 'new_m_s5_lambda_im', 'new_m_s5_log_dt', 'new_m_s5_b_re', 'new_m_s5_b_im', 'new_m_s5_c_re', 'new_m_s5_c_im', 'new_m_s5_d', 'new_m_s5_glu_w', 'new_m_s5_glu_b', 'new_m_dn_conv', 'new_m_dn_a_log', 'new_m_dn_dt_bias', 'new_m_dn_out_norm', 'new_m_rec_w_out', 'new_m_ffn_w_up', 'new_m_ffn_conv', 'new_m_ffn_w_down', 'new_v_ada_w', 'new_v_ada_b', 'new_v_norm_mix', 'new_v_norm_ffn', 'new_v_attn_w_in', 'new_v_attn_q_norm_a', 'new_v_attn_k_norm_a', 'new_v_attn_q_norm_b', 'new_v_attn_k_norm_b', 'new_v_attn_sinks', 'new_v_attn_w_out', 'new_v_rec_w_in', 'new_v_s5_lambda_re', 'new_v_s5_lambda_im', 'new_v_s5_log_dt', 'new_v_s5_b_re', 'new_v_s5_b_im', 'new_v_s5_c_re', 'new_v_s5_c_im', 'new_v_s5_d', 'new_v_s5_glu_w', 'new_v_s5_glu_b', 'new_v_dn_conv', 'new_v_dn_a_log', 'new_v_dn_dt_bias', 'new_v_dn_out_norm', 'new_v_rec_w_out', 'new_v_ffn_w_up', 'new_v_ffn_conv', 'new_v_ffn_w_down']
TWIN_LEAF_KINDS = {'loss': 'loss', 'grad_x': 'grad_x', 'grad_ada_w': 'grad_w', 'grad_ada_b': 'grad_w', 'grad_norm_mix': 'grad_w', 'grad_norm_ffn': 'grad_w', 'grad_attn_w_in': 'grad_w', 'grad_attn_q_norm_a': 'grad_w', 'grad_attn_k_norm_a': 'grad_w', 'grad_attn_q_norm_b': 'grad_w', 'grad_attn_k_norm_b': 'grad_w', 'grad_attn_sinks': 'grad_w', 'grad_attn_w_out': 'grad_w', 'grad_rec_w_in': 'grad_w', 'grad_s5_lambda_re': 'grad_w', 'grad_s5_lambda_im': 'grad_w', 'grad_s5_log_dt': 'grad_w', 'grad_s5_b_re': 'grad_w', 'grad_s5_b_im': 'grad_w', 'grad_s5_c_re': 'grad_w', 'grad_s5_c_im': 'grad_w', 'grad_s5_d': 'grad_w', 'grad_s5_glu_w': 'grad_w', 'grad_s5_glu_b': 'grad_w', 'grad_dn_conv': 'grad_w', 'grad_dn_a_log': 'grad_w', 'grad_dn_dt_bias': 'grad_w', 'grad_dn_out_norm': 'grad_w', 'grad_rec_w_out': 'grad_w', 'grad_ffn_w_up': 'grad_w', 'grad_ffn_conv': 'grad_w', 'grad_ffn_w_down': 'grad_w', 'delta_ada_w': 'delta_w', 'delta_ada_b': 'delta_w', 'delta_norm_mix': 'delta_w', 'delta_norm_ffn': 'delta_w', 'delta_attn_w_in': 'delta_w', 'delta_attn_q_norm_a': 'delta_w', 'delta_attn_k_norm_a': 'delta_w', 'delta_attn_q_norm_b': 'delta_w', 'delta_attn_k_norm_b': 'delta_w', 'delta_attn_sinks': 'delta_w', 'delta_attn_w_out': 'delta_w', 'delta_rec_w_in': 'delta_w', 'delta_s5_lambda_re': 'delta_w', 'delta_s5_lambda_im': 'delta_w', 'delta_s5_log_dt': 'delta_w', 'delta_s5_b_re': 'delta_w', 'delta_s5_b_im': 'delta_w', 'delta_s5_c_re': 'delta_w', 'delta_s5_c_im': 'delta_w', 'delta_s5_d': 'delta_w', 'delta_s5_glu_w': 'delta_w', 'delta_s5_glu_b': 'delta_w', 'delta_dn_conv': 'delta_w', 'delta_dn_a_log': 'delta_w', 'delta_dn_dt_bias': 'delta_w', 'delta_dn_out_norm': 'delta_w', 'delta_rec_w_out': 'delta_w', 'delta_ffn_w_up': 'delta_w', 'delta_ffn_conv': 'delta_w', 'delta_ffn_w_down': 'delta_w', 'new_m_ada_w': 'new_m', 'new_m_ada_b': 'new_m', 'new_m_norm_mix': 'new_m', 'new_m_norm_ffn': 'new_m', 'new_m_attn_w_in': 'new_m', 'new_m_attn_q_norm_a': 'new_m', 'new_m_attn_k_norm_a': 'new_m', 'new_m_attn_q_norm_b': 'new_m', 'new_m_attn_k_norm_b': 'new_m', 'new_m_attn_sinks': 'new_m', 'new_m_attn_w_out': 'new_m', 'new_m_rec_w_in': 'new_m', 'new_m_s5_lambda_re': 'new_m', 'new_m_s5_lambda_im': 'new_m', 'new_m_s5_log_dt': 'new_m', 'new_m_s5_b_re': 'new_m', 'new_m_s5_b_im': 'new_m', 'new_m_s5_c_re': 'new_m', 'new_m_s5_c_im': 'new_m', 'new_m_s5_d': 'new_m', 'new_m_s5_glu_w': 'new_m', 'new_m_s5_glu_b': 'new_m', 'new_m_dn_conv': 'new_m', 'new_m_dn_a_log': 'new_m', 'new_m_dn_dt_bias': 'new_m', 'new_m_dn_out_norm': 'new_m', 'new_m_rec_w_out': 'new_m', 'new_m_ffn_w_up': 'new_m', 'new_m_ffn_conv': 'new_m', 'new_m_ffn_w_down': 'new_m', 'new_v_ada_w': 'new_v', 'new_v_ada_b': 'new_v', 'new_v_norm_mix': 'new_v', 'new_v_norm_ffn': 'new_v', 'new_v_attn_w_in': 'new_v', 'new_v_attn_q_norm_a': 'new_v', 'new_v_attn_k_norm_a': 'new_v', 'new_v_attn_q_norm_b': 'new_v', 'new_v_attn_k_norm_b': 'new_v', 'new_v_attn_sinks': 'new_v', 'new_v_attn_w_out': 'new_v', 'new_v_rec_w_in': 'new_v', 'new_v_s5_lambda_re': 'new_v', 'new_v_s5_lambda_im': 'new_v', 'new_v_s5_log_dt': 'new_v', 'new_v_s5_b_re': 'new_v', 'new_v_s5_b_im': 'new_v', 'new_v_s5_c_re': 'new_v', 'new_v_s5_c_im': 'new_v', 'new_v_s5_d': 'new_v', 'new_v_s5_glu_w': 'new_v', 'new_v_s5_glu_b': 'new_v', 'new_v_dn_conv': 'new_v', 'new_v_dn_a_log': 'new_v', 'new_v_dn_dt_bias': 'new_v', 'new_v_dn_out_norm': 'new_v', 'new_v_rec_w_out': 'new_v', 'new_v_ffn_w_up': 'new_v', 'new_v_ffn_conv': 'new_v', 'new_v_ffn_w_down': 'new_v'}


def _forward(args):
    return _fwd_reference(*[args[k] for k in FWD_PARAMS])


def _output_shape():
    def fwd():
        inp = _fwd_setup_inputs(0)
        return _fwd_reference(*[inp[k] for k in FWD_PARAMS])
    out = _jax.eval_shape(fwd)
    return out.shape, out.dtype

N_MICROBATCH = 1
ADAM_LR = 0.001
ADAM_B1 = 0.9
ADAM_B2 = 0.999
ADAM_EPS = 1e-08
ADAM_WD = 0.01
ADAM_STEP = 10
PER_EXAMPLE_BATCH_AXIS = {'x': 0, 'c': 0, 'loss_target': 0}
SHARED_INPUTS = []
_WEIGHT_DTYPES = {'ada_w': _jnp.float32, 'ada_b': _jnp.float32, 'norm_mix': _jnp.float32, 'norm_ffn': _jnp.float32, 'attn_w_in': _jnp.float32, 'attn_q_norm_a': _jnp.float32, 'attn_k_norm_a': _jnp.float32, 'attn_q_norm_b': _jnp.float32, 'attn_k_norm_b': _jnp.float32, 'attn_sinks': _jnp.float32, 'attn_w_out': _jnp.float32, 'rec_w_in': _jnp.float32, 's5_lambda_re': _jnp.float32, 's5_lambda_im': _jnp.float32, 's5_log_dt': _jnp.float32, 's5_b_re': _jnp.float32, 's5_b_im': _jnp.float32, 's5_c_re': _jnp.float32, 's5_c_im': _jnp.float32, 's5_d': _jnp.float32, 's5_glu_w': _jnp.float32, 's5_glu_b': _jnp.float32, 'dn_conv': _jnp.float32, 'dn_a_log': _jnp.float32, 'dn_dt_bias': _jnp.float32, 'dn_out_norm': _jnp.float32, 'rec_w_out': _jnp.float32, 'ffn_w_up': _jnp.float32, 'ffn_conv': _jnp.float32, 'ffn_w_down': _jnp.float32}
MOMENT_SCALE = {'ada_w': 1.344596e+00, 'ada_b': 3.603827e+00, 'norm_mix': 1.656275e+00, 'norm_ffn': 6.414651e+00, 'attn_w_in': 2.535068e-01, 'attn_q_norm_a': 3.066501e+00, 'attn_k_norm_a': 3.065816e+00, 'attn_q_norm_b': 7.554958e-01, 'attn_k_norm_b': 7.584088e-01, 'attn_sinks': 1.532865e+01, 'attn_w_out': 3.268383e-01, 'rec_w_in': 2.191892e-01, 's5_lambda_re': 1.879821e-02, 's5_lambda_im': 1.973505e-02, 's5_log_dt': 3.757624e+00, 's5_b_re': 1.864540e-02, 's5_b_im': 1.455694e-02, 's5_c_re': 2.302495e-02, 's5_c_im': 2.138687e-02, 's5_d': 1.165437e+00, 's5_glu_w': 3.162489e-01, 's5_glu_b': 6.827419e-01, 'dn_conv': 2.266534e-01, 'dn_a_log': 6.372116e+00, 'dn_dt_bias': 6.023792e+00, 'dn_out_norm': 1.584714e+01, 'rec_w_out': 2.077215e-01, 'ffn_w_up': 1.500118e-01, 'ffn_conv': 9.438273e-01, 'ffn_w_down': 1.263125e-01}


def _to_microbatches(a, axis):
    t = _jnp.moveaxis(a, axis, 0)
    t = t.reshape((N_MICROBATCH, t.shape[0] // N_MICROBATCH) + t.shape[1:])
    return _jnp.moveaxis(t, 1, axis + 1)


def setup_inputs(seed: int = 0) -> dict:
    inp = _fwd_setup_inputs(seed)
    key = _jax.random.fold_in(_jax.random.key(seed), 7919)
    shape, _ = _output_shape()
    out = dict(inp)
    out["loss_target"] = _jax.random.normal(_jax.random.fold_in(key, 0), shape, _jnp.float32)
    for i, name in enumerate(TWIN_WEIGHTS):
        w = inp[name].astype(_jnp.float32)
        if MOMENT_SCALE is None:
            s = _jnp.sqrt(_jnp.mean(_jnp.square(w)) + 1e-30)
        else:
            s = MOMENT_SCALE[name]
        km, kv = _jax.random.split(_jax.random.fold_in(key, i + 1))
        out[name] = w
        out["m_" + name] = s * _jax.random.normal(km, w.shape, _jnp.float32)
        out["v_" + name] = (s * s) * _jax.random.uniform(kv, w.shape, _jnp.float32, 0.5, 1.5)
    if N_MICROBATCH > 1:
        for name, axis in PER_EXAMPLE_BATCH_AXIS.items():
            out[name] = _to_microbatches(out[name], axis)
    return {'x': out['x'], 'c': out['c'], 'ada_w': out['ada_w'], 'ada_b': out['ada_b'], 'norm_mix': out['norm_mix'], 'norm_ffn': out['norm_ffn'], 'attn_w_in': out['attn_w_in'], 'attn_q_norm_a': out['attn_q_norm_a'], 'attn_k_norm_a': out['attn_k_norm_a'], 'attn_q_norm_b': out['attn_q_norm_b'], 'attn_k_norm_b': out['attn_k_norm_b'], 'attn_sinks': out['attn_sinks'], 'attn_w_out': out['attn_w_out'], 'rec_w_in': out['rec_w_in'], 's5_lambda_re': out['s5_lambda_re'], 's5_lambda_im': out['s5_lambda_im'], 's5_log_dt': out['s5_log_dt'], 's5_b_re': out['s5_b_re'], 's5_b_im': out['s5_b_im'], 's5_c_re': out['s5_c_re'], 's5_c_im': out['s5_c_im'], 's5_d': out['s5_d'], 's5_glu_w': out['s5_glu_w'], 's5_glu_b': out['s5_glu_b'], 'dn_conv': out['dn_conv'], 'dn_a_log': out['dn_a_log'], 'dn_dt_bias': out['dn_dt_bias'], 'dn_out_norm': out['dn_out_norm'], 'rec_w_out': out['rec_w_out'], 'ffn_w_up': out['ffn_w_up'], 'ffn_conv': out['ffn_conv'], 'ffn_w_down': out['ffn_w_down'], 'loss_target': out['loss_target'], 'm_ada_w': out['m_ada_w'], 'm_ada_b': out['m_ada_b'], 'm_norm_mix': out['m_norm_mix'], 'm_norm_ffn': out['m_norm_ffn'], 'm_attn_w_in': out['m_attn_w_in'], 'm_attn_q_norm_a': out['m_attn_q_norm_a'], 'm_attn_k_norm_a': out['m_attn_k_norm_a'], 'm_attn_q_norm_b': out['m_attn_q_norm_b'], 'm_attn_k_norm_b': out['m_attn_k_norm_b'], 'm_attn_sinks': out['m_attn_sinks'], 'm_attn_w_out': out['m_attn_w_out'], 'm_rec_w_in': out['m_rec_w_in'], 'm_s5_lambda_re': out['m_s5_lambda_re'], 'm_s5_lambda_im': out['m_s5_lambda_im'], 'm_s5_log_dt': out['m_s5_log_dt'], 'm_s5_b_re': out['m_s5_b_re'], 'm_s5_b_im': out['m_s5_b_im'], 'm_s5_c_re': out['m_s5_c_re'], 'm_s5_c_im': out['m_s5_c_im'], 'm_s5_d': out['m_s5_d'], 'm_s5_glu_w': out['m_s5_glu_w'], 'm_s5_glu_b': out['m_s5_glu_b'], 'm_dn_conv': out['m_dn_conv'], 'm_dn_a_log': out['m_dn_a_log'], 'm_dn_dt_bias': out['m_dn_dt_bias'], 'm_dn_out_norm': out['m_dn_out_norm'], 'm_rec_w_out': out['m_rec_w_out'], 'm_ffn_w_up': out['m_ffn_w_up'], 'm_ffn_conv': out['m_ffn_conv'], 'm_ffn_w_down': out['m_ffn_w_down'], 'v_ada_w': out['v_ada_w'], 'v_ada_b': out['v_ada_b'], 'v_norm_mix': out['v_norm_mix'], 'v_norm_ffn': out['v_norm_ffn'], 'v_attn_w_in': out['v_attn_w_in'], 'v_attn_q_norm_a': out['v_attn_q_norm_a'], 'v_attn_k_norm_a': out['v_attn_k_norm_a'], 'v_attn_q_norm_b': out['v_attn_q_norm_b'], 'v_attn_k_norm_b': out['v_attn_k_norm_b'], 'v_attn_sinks': out['v_attn_sinks'], 'v_attn_w_out': out['v_attn_w_out'], 'v_rec_w_in': out['v_rec_w_in'], 'v_s5_lambda_re': out['v_s5_lambda_re'], 'v_s5_lambda_im': out['v_s5_lambda_im'], 'v_s5_log_dt': out['v_s5_log_dt'], 'v_s5_b_re': out['v_s5_b_re'], 'v_s5_b_im': out['v_s5_b_im'], 'v_s5_c_re': out['v_s5_c_re'], 'v_s5_c_im': out['v_s5_c_im'], 'v_s5_d': out['v_s5_d'], 'v_s5_glu_w': out['v_s5_glu_w'], 'v_s5_glu_b': out['v_s5_glu_b'], 'v_dn_conv': out['v_dn_conv'], 'v_dn_a_log': out['v_dn_a_log'], 'v_dn_dt_bias': out['v_dn_dt_bias'], 'v_dn_out_norm': out['v_dn_out_norm'], 'v_rec_w_out': out['v_rec_w_out'], 'v_ffn_w_up': out['v_ffn_w_up'], 'v_ffn_conv': out['v_ffn_conv'], 'v_ffn_w_down': out['v_ffn_w_down']}


def _loss(weights, diff, rest, loss_target):
    with _jax.named_scope("forward"):
        args = {**rest, TWIN_DIFF_INPUT: diff, **{k: w.astype(_WEIGHT_DTYPES[k]) for k, w in weights.items()}}
        y = _forward(args)
    with _jax.named_scope("loss_head"):
        err = _jnp.square(y.astype(_jnp.float32) - loss_target)
        return 0.5 * _jnp.sum(_jnp.mean(err, axis=-1)) if err.ndim else 0.5 * err


def _adamw(w, g, m, v):
    m = ADAM_B1 * m + (1.0 - ADAM_B1) * g
    v = ADAM_B2 * v + (1.0 - ADAM_B2) * _jnp.square(g)
    m_hat = m / (1.0 - ADAM_B1 ** ADAM_STEP)
    v_hat = v / (1.0 - ADAM_B2 ** ADAM_STEP)
    delta = -ADAM_LR * (m_hat / (_jnp.sqrt(v_hat) + ADAM_EPS) + ADAM_WD * w)
    return delta, m, v


def reference(x, c, ada_w, ada_b, norm_mix, norm_ffn, attn_w_in, attn_q_norm_a, attn_k_norm_a, attn_q_norm_b, attn_k_norm_b, attn_sinks, attn_w_out, rec_w_in, s5_lambda_re, s5_lambda_im, s5_log_dt, s5_b_re, s5_b_im, s5_c_re, s5_c_im, s5_d, s5_glu_w, s5_glu_b, dn_conv, dn_a_log, dn_dt_bias, dn_out_norm, rec_w_out, ffn_w_up, ffn_conv, ffn_w_down, loss_target, m_ada_w, m_ada_b, m_norm_mix, m_norm_ffn, m_attn_w_in, m_attn_q_norm_a, m_attn_k_norm_a, m_attn_q_norm_b, m_attn_k_norm_b, m_attn_sinks, m_attn_w_out, m_rec_w_in, m_s5_lambda_re, m_s5_lambda_im, m_s5_log_dt, m_s5_b_re, m_s5_b_im, m_s5_c_re, m_s5_c_im, m_s5_d, m_s5_glu_w, m_s5_glu_b, m_dn_conv, m_dn_a_log, m_dn_dt_bias, m_dn_out_norm, m_rec_w_out, m_ffn_w_up, m_ffn_conv, m_ffn_w_down, v_ada_w, v_ada_b, v_norm_mix, v_norm_ffn, v_attn_w_in, v_attn_q_norm_a, v_attn_k_norm_a, v_attn_q_norm_b, v_attn_k_norm_b, v_attn_sinks, v_attn_w_out, v_rec_w_in, v_s5_lambda_re, v_s5_lambda_im, v_s5_log_dt, v_s5_b_re, v_s5_b_im, v_s5_c_re, v_s5_c_im, v_s5_d, v_s5_glu_w, v_s5_glu_b, v_dn_conv, v_dn_a_log, v_dn_dt_bias, v_dn_out_norm, v_rec_w_out, v_ffn_w_up, v_ffn_conv, v_ffn_w_down):
    given = dict(x=x, c=c, ada_w=ada_w, ada_b=ada_b, norm_mix=norm_mix, norm_ffn=norm_ffn, attn_w_in=attn_w_in, attn_q_norm_a=attn_q_norm_a, attn_k_norm_a=attn_k_norm_a, attn_q_norm_b=attn_q_norm_b, attn_k_norm_b=attn_k_norm_b, attn_sinks=attn_sinks, attn_w_out=attn_w_out, rec_w_in=rec_w_in, s5_lambda_re=s5_lambda_re, s5_lambda_im=s5_lambda_im, s5_log_dt=s5_log_dt, s5_b_re=s5_b_re, s5_b_im=s5_b_im, s5_c_re=s5_c_re, s5_c_im=s5_c_im, s5_d=s5_d, s5_glu_w=s5_glu_w, s5_glu_b=s5_glu_b, dn_conv=dn_conv, dn_a_log=dn_a_log, dn_dt_bias=dn_dt_bias, dn_out_norm=dn_out_norm, rec_w_out=rec_w_out, ffn_w_up=ffn_w_up, ffn_conv=ffn_conv, ffn_w_down=ffn_w_down, loss_target=loss_target, m_ada_w=m_ada_w, m_ada_b=m_ada_b, m_norm_mix=m_norm_mix, m_norm_ffn=m_norm_ffn, m_attn_w_in=m_attn_w_in, m_attn_q_norm_a=m_attn_q_norm_a, m_attn_k_norm_a=m_attn_k_norm_a, m_attn_q_norm_b=m_attn_q_norm_b, m_attn_k_norm_b=m_attn_k_norm_b, m_attn_sinks=m_attn_sinks, m_attn_w_out=m_attn_w_out, m_rec_w_in=m_rec_w_in, m_s5_lambda_re=m_s5_lambda_re, m_s5_lambda_im=m_s5_lambda_im, m_s5_log_dt=m_s5_log_dt, m_s5_b_re=m_s5_b_re, m_s5_b_im=m_s5_b_im, m_s5_c_re=m_s5_c_re, m_s5_c_im=m_s5_c_im, m_s5_d=m_s5_d, m_s5_glu_w=m_s5_glu_w, m_s5_glu_b=m_s5_glu_b, m_dn_conv=m_dn_conv, m_dn_a_log=m_dn_a_log, m_dn_dt_bias=m_dn_dt_bias, m_dn_out_norm=m_dn_out_norm, m_rec_w_out=m_rec_w_out, m_ffn_w_up=m_ffn_w_up, m_ffn_conv=m_ffn_conv, m_ffn_w_down=m_ffn_w_down, v_ada_w=v_ada_w, v_ada_b=v_ada_b, v_norm_mix=v_norm_mix, v_norm_ffn=v_norm_ffn, v_attn_w_in=v_attn_w_in, v_attn_q_norm_a=v_attn_q_norm_a, v_attn_k_norm_a=v_attn_k_norm_a, v_attn_q_norm_b=v_attn_q_norm_b, v_attn_k_norm_b=v_attn_k_norm_b, v_attn_sinks=v_attn_sinks, v_attn_w_out=v_attn_w_out, v_rec_w_in=v_rec_w_in, v_s5_lambda_re=v_s5_lambda_re, v_s5_lambda_im=v_s5_lambda_im, v_s5_log_dt=v_s5_log_dt, v_s5_b_re=v_s5_b_re, v_s5_b_im=v_s5_b_im, v_s5_c_re=v_s5_c_re, v_s5_c_im=v_s5_c_im, v_s5_d=v_s5_d, v_s5_glu_w=v_s5_glu_w, v_s5_glu_b=v_s5_glu_b, v_dn_conv=v_dn_conv, v_dn_a_log=v_dn_a_log, v_dn_dt_bias=v_dn_dt_bias, v_dn_out_norm=v_dn_out_norm, v_rec_w_out=v_rec_w_out, v_ffn_w_up=v_ffn_w_up, v_ffn_conv=v_ffn_conv, v_ffn_w_down=v_ffn_w_down)
    weights = {n: given[n] for n in TWIN_WEIGHTS}
    shared = {n: given[n] for n in SHARED_INPUTS}
    per_example = {n: given[n] for n in ['x', 'c']}
    grad_fn = _jax.value_and_grad(_loss, argnums=(0, 1))

    def one_microbatch(ex, loss_target):
        ex = dict(ex)
        diff = ex.pop(TWIN_DIFF_INPUT)
        return grad_fn(weights, diff, {**shared, **ex}, loss_target)

    if N_MICROBATCH == 1:
        loss, (grad_w, grad_x) = one_microbatch(per_example, given["loss_target"])
    else:
        def body(carry, xs):
            loss_sum, grad_sum = carry
            l_k, (gw_k, gx_k) = one_microbatch(xs[0], xs[1])
            with _jax.named_scope("update"):
                return (loss_sum + l_k, _jax.tree.map(_jnp.add, grad_sum, gw_k)), gx_k

        init = (_jnp.zeros((), _jnp.float32), _jax.tree.map(_jnp.zeros_like, weights))
        (loss, grad_w), grad_x = _jax.lax.scan(body, init, (per_example, given["loss_target"]))
    with _jax.named_scope("update"):
        delta_w, new_m, new_v = {}, {}, {}
        for n in TWIN_WEIGHTS:
            delta_w[n], new_m[n], new_v[n] = _adamw(weights[n], grad_w[n], given["m_" + n], given["v_" + n])
    return (loss, grad_x, *[grad_w[n] for n in TWIN_WEIGHTS], *[delta_w[n] for n in TWIN_WEIGHTS],
            *[new_m[n] for n in TWIN_WEIGHTS], *[new_v[n] for n in TWIN_WEIGHTS])
```

```python
import functools
import math

import numpy as np
import jax
import jax.numpy as jnp
from jax import lax
from jax.experimental import pallas as pl
from jax.experimental.pallas import tpu as pltpu

F32 = jnp.float32
BF16 = jnp.bfloat16

N_DEV = 8
D = 1024
HD = 64
BLK = 128
ATTN_IN = 2304
CB = ATTN_IN // 128
B_BRANCHES = ((128, 1), (512, 4), (2048, 16))
S5_W = 256
S5_P = 1024
DN_H = 6
DN_DK = 128
DN_C = 64
REC_IN = 3340
REC_PAD = 3456
D_FF = 2816
EPS = 1e-6
ADAM_LR, ADAM_B1, ADAM_B2, ADAM_EPS, ADAM_WD, ADAM_STEP = 0.001, 0.9, 0.999, 1e-8, 0.01, 10
VMEM_LIMIT = 48 * 1024 * 1024

ALIBI = np.asarray(2.0 ** (-8.0 * np.arange(1, 17) / 16), dtype=np.float32)


def _cparams(*sem):
    return pltpu.CompilerParams(dimension_semantics=tuple(sem), vmem_limit_bytes=VMEM_LIMIT)


def _tile(n, target):
    if n <= target:
        return n
    best = None
    for t in range(128, target + 1, 128):
        if n % t == 0:
            best = t
    assert best is not None, (n, target)
    return best


def _rtile(n, target):
    if n <= target:
        return n
    best = None
    for t in range(8, target + 1, 8):
        if n % t == 0:
            best = t
    assert best is not None, (n, target)
    return best


def _fold8(x):
    r, c = x.shape
    return x.reshape(r // 8, 8, c).sum(axis=0)


def _sigmoid(x):
    return 1.0 / (1.0 + jnp.exp(-x))


_DIMS = {"nn": (((1,), (0,)), ((), ())), "nt": (((1,), (1,)), ((), ())), "tn": (((0,), (0,)), ((), ()))}


def matmul(pairs, mode, name, out_dtype=F32, tm=512, tn=512, tk=1024):
    a0, b0 = pairs[0]
    if mode == "nn":
        (M, K), N = a0.shape, b0.shape[1]
    elif mode == "nt":
        (M, K), N = a0.shape, b0.shape[0]
    else:
        (K, M), N = a0.shape, b0.shape[1]
    tm = _rtile(M, tm) if M % 128 else _tile(M, tm)
    tn, tk = _tile(N, tn), _tile(K, tk)
    nk = K // tk
    npair = len(pairs)
    dims = _DIMS[mode]

    def body(*refs):
        o_ref, acc_ref = refs[2 * npair], refs[2 * npair + 1]
        k = pl.program_id(2)

        @pl.when(k == 0)
        def _():
            acc_ref[...] = jnp.zeros_like(acc_ref)

        tot = None
        for p in range(npair):
            part = lax.dot_general(refs[2 * p][...].astype(BF16), refs[2 * p + 1][...].astype(BF16),
                                   dims, preferred_element_type=F32)
            tot = part if tot is None else tot + part
        acc_ref[...] += tot

        @pl.when(k == nk - 1)
        def _():
            o_ref[...] = acc_ref[...].astype(o_ref.dtype)

    if mode == "nn":
        a_spec = pl.BlockSpec((tm, tk), lambda i, j, k: (i, k))
        b_spec = pl.BlockSpec((tk, tn), lambda i, j, k: (k, j))
    elif mode == "nt":
        a_spec = pl.BlockSpec((tm, tk), lambda i, j, k: (i, k))
        b_spec = pl.BlockSpec((tn, tk), lambda i, j, k: (j, k))
    else:
        a_spec = pl.BlockSpec((tk, tm), lambda i, j, k: (k, i))
        b_spec = pl.BlockSpec((tk, tn), lambda i, j, k: (k, j))
    flat = [t for pr in pairs for t in pr]
    return pl.pallas_call(
        body, name=name, grid=(M // tm, N // tn, nk),
        in_specs=[a_spec, b_spec] * npair,
        out_specs=pl.BlockSpec((tm, tn), lambda i, j, k: (i, j)),
        out_shape=jax.ShapeDtypeStruct((M, N), out_dtype),
        scratch_shapes=[pltpu.VMEM((tm, tn), F32)],
        compiler_params=_cparams("parallel", "parallel", "arbitrary"),
    )(*flat)


def gate_norm_fwd(x, y, gate, nw, sh, sc, name):
    L, C = x.shape
    tl = _rtile(L, 512)
    has_gate = y is not None

    def body(*refs):
        if has_gate:
            x_ref, y_ref, g_ref, nw_ref, sh_ref, sc_ref, xn_ref, h_ref = refs
            xn = x_ref[...] + g_ref[...] * y_ref[...]
            xn_ref[...] = xn
        else:
            x_ref, nw_ref, sh_ref, sc_ref, h_ref = refs
            xn = x_ref[...]
        r = lax.rsqrt(jnp.mean(xn * xn, axis=-1, keepdims=True) + EPS)
        h = (xn * r * nw_ref[...]) * (1.0 + sc_ref[...]) + sh_ref[...]
        h_ref[...] = h.astype(BF16)

    big = pl.BlockSpec((tl, C), lambda i: (i, 0))
    vec = pl.BlockSpec((1, C), lambda i: (0, 0))
    if has_gate:
        ins, in_specs = (x, y, gate, nw, sh, sc), [big, big, vec, vec, vec, vec]
        out_shape = (jax.ShapeDtypeStruct((L, C), F32), jax.ShapeDtypeStruct((L, C), BF16))
        out_specs = (big, big)
    else:
        ins, in_specs = (x, nw, sh, sc), [big, vec, vec, vec]
        out_shape = jax.ShapeDtypeStruct((L, C), BF16)
        out_specs = big
    return pl.pallas_call(body, name=name, grid=(L // tl,), in_specs=in_specs, out_specs=out_specs,
                          out_shape=out_shape, compiler_params=_cparams("parallel"))(*ins)


def gate_norm_bwd(xn, y, gate, nw, sc, dxn_direct, dh, name):
    L, C = xn.shape
    tl = _rtile(L, 256)
    has_gate = y is not None
    has_direct = dxn_direct is not None

    def body(*refs):
        refs = list(refs)
        xn_ref = refs.pop(0)
        y_ref = refs.pop(0) if has_gate else None
        g_ref = refs.pop(0) if has_gate else None
        nw_ref = refs.pop(0)
        sc_ref = refs.pop(0)
        dd_ref = refs.pop(0) if has_direct else None
        dh_ref = refs.pop(0)
        dxn_ref = refs.pop(0)
        dy_ref = refs.pop(0) if has_gate else None
        sums_ref = refs.pop(0)

        @pl.when(pl.program_id(0) == 0)
        def _():
            sums_ref[...] = jnp.zeros_like(sums_ref)

        xv = xn_ref[...]
        dh_v = dh_ref[...]
        r = lax.rsqrt(jnp.mean(xv * xv, axis=-1, keepdims=True) + EPS)
        n = xv * r
        a = nw_ref[...] * (1.0 + sc_ref[...])
        dn = dh_v * a
        dx = r * (dn - n * jnp.mean(dn * n, axis=-1, keepdims=True))
        if has_direct:
            dx = dx + dd_ref[...]
        dxn_ref[...] = dx
        sums_ref[8:16, :] += _fold8(dh_v * n)
        sums_ref[16:24, :] += _fold8(dh_v)
        if has_gate:
            dy_ref[...] = dx * g_ref[...]
            sums_ref[0:8, :] += _fold8(dx * y_ref[...])

    big = pl.BlockSpec((tl, C), lambda i: (i, 0))
    vec = pl.BlockSpec((1, C), lambda i: (0, 0))
    ins, in_specs = [xn], [big]
    if has_gate:
        ins += [y, gate]
        in_specs += [big, vec]
    ins += [nw, sc]
    in_specs += [vec, vec]
    if has_direct:
        ins.append(dxn_direct)
        in_specs.append(big)
    ins.append(dh)
    in_specs.append(big)
    out_shape = [jax.ShapeDtypeStruct((L, C), F32)]
    out_specs = [big]
    if has_gate:
        out_shape.append(jax.ShapeDtypeStruct((L, C), F32))
        out_specs.append(big)
    out_shape.append(jax.ShapeDtypeStruct((32, C), F32))
    out_specs.append(pl.BlockSpec((32, C), lambda i: (0, 0)))
    return pl.pallas_call(body, name=name, grid=(L // tl,), in_specs=in_specs, out_specs=tuple(out_specs),
                          out_shape=tuple(out_shape), compiler_params=_cparams("arbitrary"))(*ins)


def final_loss(x, f, gate, target, name):
    L, C = x.shape
    tl = _rtile(L, 256)

    def body(x_ref, f_ref, g_ref, t_ref, dy_ref, df_ref, sums_ref):
        @pl.when(pl.program_id(0) == 0)
        def _():
            sums_ref[...] = jnp.zeros_like(sums_ref)

        fv = f_ref[...]
        err = x_ref[...] + g_ref[...] * fv - t_ref[...]
        dy = err * (1.0 / C)
        dy_ref[...] = dy
        df_ref[...] = dy * g_ref[...]
        sums_ref[0:8, :] += _fold8(err * err)
        sums_ref[8:16, :] += _fold8(dy * fv)

    big = pl.BlockSpec((tl, C), lambda i: (i, 0))
    vec = pl.BlockSpec((1, C), lambda i: (0, 0))
    return pl.pallas_call(
        body, name=name, grid=(L // tl,), in_specs=[big, big, vec, big],
        out_specs=(big, big, pl.BlockSpec((16, C), lambda i: (0, 0))),
        out_shape=(jax.ShapeDtypeStruct((L, C), F32), jax.ShapeDtypeStruct((L, C), F32),
                   jax.ShapeDtypeStruct((16, C), F32)),
        compiler_params=_cparams("arbitrary"))(x, f, gate, target)


def _seg_ones(seg):
    r = lax.broadcasted_iota(jnp.int32, (128, 128), 0) // seg
    c = lax.broadcasted_iota(jnp.int32, (128, 128), 1) // seg
    return (r == c).astype(BF16)


def _segsum(t, ones):
    hi = t.astype(BF16)
    lo = (t - hi.astype(F32)).astype(BF16)
    return (jnp.dot(hi, ones, preferred_element_type=F32) + jnp.dot(lo, ones, preferred_element_type=F32))


_NORMED_TILES = tuple(list(range(0, 5)) + list(range(6, 14)))


def qknorm_fwd(qkv, wvec, name):
    L, C = qkv.shape
    tl = _rtile(L, 256)

    def body(x_ref, w_ref, o_ref):
        ones = _seg_ones(HD)
        for t in range(CB):
            cs = slice(t * 128, (t + 1) * 128)
            x = x_ref[:, cs]
            if t in _NORMED_TILES:
                ms = _segsum(x * x, ones) * (1.0 / HD)
                x = x * lax.rsqrt(ms + EPS) * w_ref[:, cs]
            o_ref[:, cs] = x.astype(BF16)

    return pl.pallas_call(
        body, name=name, grid=(L // tl,),
        in_specs=[pl.BlockSpec((tl, C), lambda i: (i, 0)), pl.BlockSpec((1, C), lambda i: (0, 0))],
        out_specs=pl.BlockSpec((tl, C), lambda i: (i, 0)),
        out_shape=jax.ShapeDtypeStruct((L, C), BF16), compiler_params=_cparams("parallel"))(qkv, wvec)


def qknorm_bwd(qkv, wvec, dy, name):
    L, C = qkv.shape
    tl = _rtile(L, 256)

    def body(x_ref, w_ref, dy_ref, dx_ref, sums_ref):
        @pl.when(pl.program_id(0) == 0)
        def _():
            sums_ref[...] = jnp.zeros_like(sums_ref)

        ones = _seg_ones(HD)
        for t in range(CB):
            cs = slice(t * 128, (t + 1) * 128)
            d = dy_ref[:, cs]
            if t in _NORMED_TILES:
                x = x_ref[:, cs]
                r = lax.rsqrt(_segsum(x * x, ones) * (1.0 / HD) + EPS)
                n = x * r
                dn = d * w_ref[:, cs]
                dx_ref[:, cs] = r * (dn - n * (_segsum(dn * n, ones) * (1.0 / HD)))
                sums_ref[:, cs] += _fold8(d * n)
            else:
                dx_ref[:, cs] = d

    big = pl.BlockSpec((tl, C), lambda i: (i, 0))
    return pl.pallas_call(
        body, name=name, grid=(L // tl,),
        in_specs=[big, pl.BlockSpec((1, C), lambda i: (0, 0)), big],
        out_specs=(big, pl.BlockSpec((8, C), lambda i: (0, 0))),
        out_shape=(jax.ShapeDtypeStruct((L, C), F32), jax.ShapeDtypeStruct((8, C), F32)),
        compiler_params=_cparams("arbitrary"))(qkv, wvec, dy)


def _attn_scores(q, kw, n, slope, step, maxdist):
    s = lax.dot_general(q, kw, (((1,), (1,)), ((), ())), preferred_element_type=F32) * (HD ** -0.5)
    qi = lax.broadcasted_iota(jnp.int32, (BLK, 2 * BLK), 0)
    sj = lax.broadcasted_iota(jnp.int32, (BLK, 2 * BLK), 1)
    dist = BLK + qi - sj
    valid = (dist >= 0) & (dist <= maxdist) & ((n > 0) | (sj >= BLK))
    bias = (-slope) * (step * dist).astype(F32)
    return jnp.where(valid, s + bias, -jnp.inf), valid


def attn_fwd(X, d, q_off, k_off, v_off, gqa, slope0, maxdist, name):
    Ls = X.shape[0]
    nb = Ls // BLK
    slopes = jnp.asarray(ALIBI)

    def body(sl_ref, q_ref, kp_ref, kc_ref, vp_ref, vc_ref, o_ref, lse_ref):
        hp, n = pl.program_id(1), pl.program_id(2)
        for e in range(2):
            slope = sl_ref[slope0 + 2 * hp + e]
            if gqa:
                ksel = lambda ref: jnp.where(hp >= 2, ref[:, 64:128], ref[:, 0:64])
            else:
                ksel = lambda ref: ref[:, e * 64:(e + 1) * 64]
            q = q_ref[:, e * 64:(e + 1) * 64]
            kw = jnp.concatenate([ksel(kp_ref), ksel(kc_ref)], axis=0)
            vw = jnp.concatenate([ksel(vp_ref), ksel(vc_ref)], axis=0)
            s, _ = _attn_scores(q, kw, n, slope, d, maxdist)
            m = jnp.max(s, axis=-1, keepdims=True)
            p = jnp.exp(s - m)
            l = jnp.sum(p, axis=-1, keepdims=True)
            o = jnp.dot(p.astype(BF16), vw, preferred_element_type=F32) / l
            o_ref[:, e * 64:(e + 1) * 64] = o
            lse_ref[:, e * 64:(e + 1) * 64] = jnp.broadcast_to(m + jnp.log(l), (BLK, HD))

    kcol = (lambda r, hp: r * CB + k_off) if gqa else (lambda r, hp: r * CB + k_off + hp)
    vcol = (lambda r, hp: r * CB + v_off) if gqa else (lambda r, hp: r * CB + v_off + hp)
    blk = (BLK, 128)
    in_specs = [
        pl.BlockSpec(memory_space=pltpu.SMEM),
        pl.BlockSpec(blk, lambda r, hp, n: (n, r * CB + q_off + hp)),
        pl.BlockSpec(blk, lambda r, hp, n: (jnp.maximum(n - 1, 0), kcol(r, hp))),
        pl.BlockSpec(blk, lambda r, hp, n: (n, kcol(r, hp))),
        pl.BlockSpec(blk, lambda r, hp, n: (jnp.maximum(n - 1, 0), vcol(r, hp))),
        pl.BlockSpec(blk, lambda r, hp, n: (n, vcol(r, hp))),
    ]
    out_spec = pl.BlockSpec(blk, lambda r, hp, n: (n, r * 4 + hp))
    out = jax.ShapeDtypeStruct((Ls, d * 512), F32)
    return pl.pallas_call(
        body, name=name, grid=(d, 4, nb), in_specs=in_specs, out_specs=(out_spec, out_spec),
        out_shape=(out, out), compiler_params=_cparams("parallel", "parallel", "arbitrary"),
    )(slopes, X, X, X, X, X)


def attn_bwd(X, o, lse, do, dlse, d, q_off, k_off, v_off, gqa, slope0, maxdist, name):
    Ls = X.shape[0]
    nb = Ls // BLK
    slopes = jnp.asarray(ALIBI)

    def body(sl_ref, q_ref, kp_ref, kc_ref, vp_ref, vc_ref, o_ref, lse_ref, do_ref, dlse_ref,
             dq_ref, dk_ref, dv_ref, ck_ref, cv_ref):
        hp, n = pl.program_id(1), pl.program_id(2)

        @pl.when(n == 0)
        def _():
            ck_ref[...] = jnp.zeros_like(ck_ref)
            cv_ref[...] = jnp.zeros_like(cv_ref)

        @pl.when(n < nb)
        def _():
            for e in range(2):
                cs = slice(e * 64, (e + 1) * 64)
                slope = sl_ref[slope0 + 2 * hp + e]
                if gqa:
                    ksel = lambda ref: jnp.where(hp >= 2, ref[:, 64:128], ref[:, 0:64])
                else:
                    ksel = lambda ref: ref[:, cs]
                q = q_ref[:, cs]
                kw = jnp.concatenate([ksel(kp_ref), ksel(kc_ref)], axis=0)
                vw = jnp.concatenate([ksel(vp_ref), ksel(vc_ref)], axis=0)
                s, valid = _attn_scores(q, kw, n, slope, d, maxdist)
                p = jnp.where(valid, jnp.exp(s - lse_ref[:, e * 64:e * 64 + 1]), 0.0)
                dov = do_ref[:, cs]
                delta = jnp.sum(dov * o_ref[:, cs], axis=-1, keepdims=True)
                dob = dov.astype(BF16)
                dp = lax.dot_general(dob, vw, (((1,), (1,)), ((), ())), preferred_element_type=F32)
                ds = p * (dp - delta + dlse_ref[:, e * 64:e * 64 + 1])
                dsb = ds.astype(BF16)
                dq_ref[:, cs] = jnp.dot(dsb, kw, preferred_element_type=F32) * (HD ** -0.5)
                dkw = lax.dot_general(dsb, q, (((0,), (0,)), ((), ())), preferred_element_type=F32) * (HD ** -0.5)
                dvw = lax.dot_general(p.astype(BF16), dob, (((0,), (0,)), ((), ())), preferred_element_type=F32)
                dk_ref[:, cs] = ck_ref[:, cs] + dkw[0:BLK]
                dv_ref[:, cs] = cv_ref[:, cs] + dvw[0:BLK]
                ck_ref[:, cs] = dkw[BLK:]
                cv_ref[:, cs] = dvw[BLK:]

        @pl.when(n == nb)
        def _():
            dk_ref[...] = ck_ref[...]
            dv_ref[...] = cv_ref[...]

    kcol = (lambda r, hp: r * CB + k_off) if gqa else (lambda r, hp: r * CB + k_off + hp)
    vcol = (lambda r, hp: r * CB + v_off) if gqa else (lambda r, hp: r * CB + v_off + hp)
    blk = (BLK, 128)
    cur = lambda n: jnp.minimum(n, nb - 1)
    prev = lambda n: jnp.maximum(jnp.minimum(n, nb - 1) - 1, 0)
    ospec = pl.BlockSpec(blk, lambda r, hp, n: (cur(n), r * 4 + hp))
    in_specs = [
        pl.BlockSpec(memory_space=pltpu.SMEM),
        pl.BlockSpec(blk, lambda r, hp, n: (cur(n), r * CB + q_off + hp)),
        pl.BlockSpec(blk, lambda r, hp, n: (prev(n), kcol(r, hp))),
        pl.BlockSpec(blk, lambda r, hp, n: (cur(n), kcol(r, hp))),
        pl.BlockSpec(blk, lambda r, hp, n: (prev(n), vcol(r, hp))),
        pl.BlockSpec(blk, lambda r, hp, n: (cur(n), vcol(r, hp))),
        ospec, ospec, ospec, ospec,
    ]
    shifted = pl.BlockSpec(blk, lambda r, hp, n: (jnp.maximum(n - 1, 0), r * 4 + hp))
    out = jax.ShapeDtypeStruct((Ls, d * 512), F32)
    return pl.pallas_call(
        body, name=name, grid=(d, 4, nb + 1), in_specs=in_specs, out_specs=(ospec, shifted, shifted),
        out_shape=(out, out, out),
        scratch_shapes=[pltpu.VMEM((BLK, 128), F32), pltpu.VMEM((BLK, 128), F32)],
        compiler_params=_cparams("parallel", "parallel", "arbitrary"),
    )(slopes, X, X, X, X, X, o, lse, do, dlse)


def attn_merge_fwd(oa, la, sink, obs, lbs, name):
    L = oa.shape[0]
    tl = _rtile(L, 256)

    def body(oa_ref, la_ref, sk_ref, o1, o2, o3, l1, l2, l3, m_ref):
        m_ref[:, 0:512] = (oa_ref[...] * _sigmoid(la_ref[...] - sk_ref[...])).astype(BF16)
        a, b, c = l1[...], l2[...], l3[...]
        mx = jnp.maximum(jnp.maximum(a, b), c)
        ea, eb, ec = jnp.exp(a - mx), jnp.exp(b - mx), jnp.exp(c - mx)
        inv = 1.0 / (ea + eb + ec)
        m_ref[:, 512:1024] = ((ea * inv) * o1[...] + (eb * inv) * o2[...] + (ec * inv) * o3[...]).astype(BF16)

    big = pl.BlockSpec((tl, 512), lambda i: (i, 0))
    return pl.pallas_call(
        body, name=name, grid=(L // tl,),
        in_specs=[big, big, pl.BlockSpec((1, 512), lambda i: (0, 0))] + [big] * 6,
        out_specs=pl.BlockSpec((tl, 1024), lambda i: (i, 0)),
        out_shape=jax.ShapeDtypeStruct((L, 1024), BF16), compiler_params=_cparams("parallel"),
    )(oa, la, sink, *obs, *lbs)


def attn_merge_bwd(dm, oa, la, sink, obs, lbs, name):
    L = oa.shape[0]
    tl = _rtile(L, 256)

    def body(dm_ref, oa_ref, la_ref, sk_ref, o1, o2, o3, l1, l2, l3,
             doa_ref, dla_ref, d1, d2, d3, g1, g2, g3, sums_ref):
        @pl.when(pl.program_id(0) == 0)
        def _():
            sums_ref[...] = jnp.zeros_like(sums_ref)

        ones = _seg_ones(HD)
        for t in range(4):
            cs = slice(t * 128, (t + 1) * 128)
            dma = dm_ref[:, cs]
            keep = _sigmoid(la_ref[:, cs] - sk_ref[:, cs])
            doa_ref[:, cs] = dma * keep
            tt = dma * oa_ref[:, cs] * keep * (1.0 - keep)
            dla_ref[:, cs] = _segsum(tt, ones)
            sums_ref[:, cs] += _fold8(-tt)
            dmb = dm_ref[:, 512 + t * 128:512 + (t + 1) * 128]
            a, b, c = l1[:, cs], l2[:, cs], l3[:, cs]
            mx = jnp.maximum(jnp.maximum(a, b), c)
            ea, eb, ec = jnp.exp(a - mx), jnp.exp(b - mx), jnp.exp(c - mx)
            inv = 1.0 / (ea + eb + ec)
            wa, wb, wc = ea * inv, eb * inv, ec * inv
            d1[:, cs] = wa * dmb
            d2[:, cs] = wb * dmb
            d3[:, cs] = wc * dmb
            sa = _segsum(dmb * o1[:, cs], ones)
            sb = _segsum(dmb * o2[:, cs], ones)
            sc_ = _segsum(dmb * o3[:, cs], ones)
            mean = wa * sa + wb * sb + wc * sc_
            g1[:, cs] = wa * (sa - mean)
            g2[:, cs] = wb * (sb - mean)
            g3[:, cs] = wc * (sc_ - mean)

    big = pl.BlockSpec((tl, 512), lambda i: (i, 0))
    o512 = jax.ShapeDtypeStruct((L, 512), F32)
    return pl.pallas_call(
        body, name=name, grid=(L // tl,),
        in_specs=[pl.BlockSpec((tl, 1024), lambda i: (i, 0)), big, big,
                  pl.BlockSpec((1, 512), lambda i: (0, 0))] + [big] * 6,
        out_specs=tuple([big] * 8 + [pl.BlockSpec((8, 512), lambda i: (0, 0))]),
        out_shape=tuple([o512] * 8 + [jax.ShapeDtypeStruct((8, 512), F32)]),
        compiler_params=_cparams("arbitrary"),
    )(dm, oa, la, sink, *obs, *lbs)


def _shift_down(x, halo, k, first):
    tl = x.shape[0]
    rows = lax.broadcasted_iota(jnp.int32, x.shape, 0)
    out = pltpu.roll(x, k, axis=0)
    for j in range(k):
        hrow = jnp.where(first, 0.0, halo[8 - k + j:8 - k + j + 1, :])
        out = jnp.where(rows == j, hrow, out)
    return out


def _shift_up(x, nxt, k):
    tl = x.shape[0]
    rows = lax.broadcasted_iota(jnp.int32, x.shape, 0)
    out = pltpu.roll(x, tl - k, axis=0)
    for j in range(k):
        out = jnp.where(rows == tl - k + j, nxt[j:j + 1, :], out)
    return out


def _silu(x):
    return x * _sigmoid(x)


def _dsilu(x):
    s = _sigmoid(x)
    return s * (1.0 + x * (1.0 - s))


def ffn_act_fwd(ua, ub, cw, name):
    L, F = ua.shape
    tl = _rtile(L, 256)
    tc = _tile(F, 1408)
    hb = tl // 8

    def body(ua_ref, uah_ref, ub_ref, ubh_ref, wa_ref, wb_ref, o_ref):
        first = pl.program_id(1) == 0

        def conv(x_ref, h_ref, w_ref):
            x = x_ref[...]
            h = h_ref[...]
            return (w_ref[2:3, :] * x + w_ref[1:2, :] * _shift_down(x, h, 1, first)
                    + w_ref[0:1, :] * _shift_down(x, h, 2, first))

        a = conv(ua_ref, uah_ref, wa_ref)
        b = conv(ub_ref, ubh_ref, wb_ref)
        o_ref[...] = (_silu(a) * b).astype(BF16)

    main = pl.BlockSpec((tl, tc), lambda j, i: (i, j))
    halo = pl.BlockSpec((8, tc), lambda j, i: (jnp.maximum(i * hb - 1, 0), j))
    wa = pl.BlockSpec((3, tc), lambda j, i: (0, j))
    wb = pl.BlockSpec((3, tc), lambda j, i: (0, j + F // tc))
    return pl.pallas_call(
        body, name=name, grid=(F // tc, L // tl), in_specs=[main, halo, main, halo, wa, wb],
        out_specs=main, out_shape=jax.ShapeDtypeStruct((L, F), BF16),
        compiler_params=_cparams("parallel", "parallel"))(ua, ua, ub, ub, cw, cw)


def ffn_act_bwd(ua, ub, cw, dact, name):
    L, F = ua.shape
    tl = _rtile(L, 256)
    tc = _tile(F, 1408)
    hb = tl // 8
    nrt = L // tl

    def body(ua_ref, uah_ref, ub_ref, ubh_ref, wa_ref, wb_ref, da_ref, dua_ref, dub_ref, sums_ref, ca_ref, cb_ref):
        i = pl.program_id(1)
        first = i == nrt - 1

        @pl.when(i == 0)
        def _():
            sums_ref[...] = jnp.zeros_like(sums_ref)
            ca_ref[...] = jnp.zeros_like(ca_ref)
            cb_ref[...] = jnp.zeros_like(cb_ref)

        def taps(x_ref, h_ref):
            x = x_ref[...]
            h = h_ref[...]
            return x, _shift_down(x, h, 1, first), _shift_down(x, h, 2, first)

        a0, a1, a2 = taps(ua_ref, uah_ref)
        b0, b1, b2 = taps(ub_ref, ubh_ref)
        a = wa_ref[2:3, :] * a0 + wa_ref[1:2, :] * a1 + wa_ref[0:1, :] * a2
        b = wb_ref[2:3, :] * b0 + wb_ref[1:2, :] * b1 + wb_ref[0:1, :] * b2
        dact_v = da_ref[...]
        dya = dact_v * b * _dsilu(a)
        dyb = dact_v * _silu(a)
        for (dy, w_ref, c_ref, d_ref, xs, base) in ((dya, wa_ref, ca_ref, dua_ref, (a2, a1, a0), 0),
                                                     (dyb, wb_ref, cb_ref, dub_ref, (b2, b1, b0), 24)):
            nxt = c_ref[...]
            d_ref[...] = (w_ref[2:3, :] * dy + w_ref[1:2, :] * _shift_up(dy, nxt, 1)
                          + w_ref[0:1, :] * _shift_up(dy, nxt, 2))
            c_ref[...] = dy[0:8, :]
            for j in range(3):
                sums_ref[base + 8 * j:base + 8 * j + 8, :] += _fold8(dy * xs[j])

    rev = lambda i: nrt - 1 - i
    main = pl.BlockSpec((tl, tc), lambda j, i: (rev(i), j))
    halo = pl.BlockSpec((8, tc), lambda j, i: (jnp.maximum(rev(i) * hb - 1, 0), j))
    wa = pl.BlockSpec((3, tc), lambda j, i: (0, j))
    wb = pl.BlockSpec((3, tc), lambda j, i: (0, j + F // tc))
    o = jax.ShapeDtypeStruct((L, F), F32)
    return pl.pallas_call(
        body, name=name, grid=(F // tc, nrt), in_specs=[main, halo, main, halo, wa, wb, main],
        out_specs=(main, main, pl.BlockSpec((48, tc), lambda j, i: (0, j))),
        out_shape=(o, o, jax.ShapeDtypeStruct((48, F), F32)),
        scratch_shapes=[pltpu.VMEM((8, tc), F32), pltpu.VMEM((8, tc), F32)],
        compiler_params=_cparams("parallel", "arbitrary"))(ua, ua, ub, ub, cw, cw, dact)


def attn_vectors(qna, kna, qnb, knb, sinks):
    ones = jnp.ones((128,), F32)
    wvec = jnp.concatenate([jnp.tile(qna, 8), jnp.tile(kna, 2), ones, jnp.tile(qnb, 8), jnp.tile(knb, 8),
                            jnp.tile(ones, 4)]).reshape(1, ATTN_IN)
    return wvec, jnp.repeat(sinks, HD).reshape(1, 512)


def attention_block_fwd(h, w_in, wvec, sinkvec, w_out, tag):
    L = h.shape[0]
    qkv = matmul([(h, w_in)], "nn", tag + "_qkv")
    X = qknorm_fwd(qkv, wvec, tag + "_qknorm")
    oa, la = attn_fwd(X, 1, 0, 4, 5, True, 0, BLK - 1, tag + "_swa")
    obs, lbs = [], []
    for window, d in B_BRANCHES:
        o, l = attn_fwd(X.reshape(L // d, d * ATTN_IN), d, 6, 10, 14, False, 8, window // d, tag + f"_dil{d}")
        obs.append(o.reshape(L, 512))
        lbs.append(l.reshape(L, 512))
    m = attn_merge_fwd(oa, la, sinkvec, obs, lbs, tag + "_merge")
    y = matmul([(m, w_out)], "nn", tag + "_out")
    return y, (h, qkv, X, oa, la, obs, lbs, m)


def attention_block_bwd(dy, res, w_in, wvec, sinkvec, w_out, tag):
    h, qkv, X, oa, la, obs, lbs, m = res
    L = h.shape[0]
    g_w_out = matmul([(m, dy)], "tn", tag + "_dwout")
    dm = matmul([(dy, w_out)], "nt", tag + "_dm")
    doa, dla, d1, d2, d3, g1, g2, g3, sinksums = attn_merge_bwd(dm, oa, la, sinkvec, obs, lbs, tag + "_dmerge")
    dqa, dka, dva = attn_bwd(X, oa, la, doa, dla, 1, 0, 4, 5, True, 0, BLK - 1, tag + "_dswa")
    dqb = dkb = dvb = None
    for (window, d), o, l, do, dl in zip(B_BRANCHES, obs, lbs, (d1, d2, d3), (g1, g2, g3)):
        shp = (L // d, d * 512)
        dq, dk, dv = attn_bwd(X.reshape(L // d, d * ATTN_IN), o.reshape(shp), l.reshape(shp), do.reshape(shp),
                              dl.reshape(shp), d, 6, 10, 14, False, 8, window // d, tag + f"_ddil{d}")
        dq, dk, dv = dq.reshape(L, 512), dk.reshape(L, 512), dv.reshape(L, 512)
        dqb, dkb, dvb = (dq, dk, dv) if dqb is None else (dqb + dq, dkb + dk, dvb + dv)
    fold = lambda t: t.reshape(L, 2, 4, HD).sum(axis=2).reshape(L, 128)
    dX = jnp.concatenate([dqa, fold(dka), fold(dva), dqb, dkb, dvb], axis=1)
    dqkv, wsums = qknorm_bwd(qkv, wvec, dX, tag + "_dqknorm")
    g_w_in = matmul([(h, dqkv)], "tn", tag + "_dwin")
    dh = matmul([(dqkv, w_in)], "nt", tag + "_dh")
    ws = wsums.sum(axis=0)
    grads = dict(
        w_in=g_w_in, w_out=g_w_out,
        q_norm_a=ws[0:512].reshape(8, HD).sum(axis=0), k_norm_a=ws[512:640].reshape(2, HD).sum(axis=0),
        q_norm_b=ws[768:1280].reshape(8, HD).sum(axis=0), k_norm_b=ws[1280:1792].reshape(8, HD).sum(axis=0),
        sinks=sinksums.sum(axis=0).reshape(8, HD).sum(axis=1))
    return dh, grads


def ffn_block_fwd(h, w_up_a, w_up_b, cw, w_down, tag):
    ua = matmul([(h, w_up_a)], "nn", tag + "_upa")
    ub = matmul([(h, w_up_b)], "nn", tag + "_upb")
    act = ffn_act_fwd(ua, ub, cw, tag + "_act")
    f = matmul([(act, w_down)], "nn", tag + "_down")
    return f, (h, ua, ub, act)


def ffn_block_bwd(df, res, w_up_a, w_up_b, cw, w_down, tag):
    h, ua, ub, act = res
    g_down = matmul([(act, df)], "tn", tag + "_dwdown")
    dact = matmul([(df, w_down)], "nt", tag + "_dact")
    dua, dub, sums = ffn_act_bwd(ua, ub, cw, dact, tag + "_dactk")
    g_up = jnp.concatenate([matmul([(h, dua)], "tn", tag + "_dwupa"), matmul([(h, dub)], "tn", tag + "_dwupb")], axis=1)
    dh = matmul([(dua, w_up_a), (dub, w_up_b)], "nt", tag + "_dh")
    s = sums.reshape(2, 3, 8, D_FF).sum(axis=2)
    g_conv = jnp.concatenate([s[0], s[1]], axis=1)
    return dh, dict(w_up=g_up, conv=g_conv, w_down=g_down)


def s5_params(lam_re, lam_im, log_dt, b_re, b_im, c_re, c_im):
    dt = jnp.exp(log_dt)[:, None]
    mag, ang = jnp.exp(lam_re * dt), lam_im * dt
    a_re, a_im = mag * jnp.cos(ang), mag * jnp.sin(ang)
    nr, ni = a_re - 1.0, a_im
    den = lam_re * lam_re + lam_im * lam_im
    f_re = (nr * lam_re + ni * lam_im) / den
    f_im = (ni * lam_re - nr * lam_im) / den
    eye = jnp.eye(16, dtype=F32)[:, None, :, None]
    bd = lambda b: (eye * jnp.transpose(b, (0, 2, 1))[:, :, None, :]).reshape(S5_W, S5_P)
    cd = lambda c: (eye * jnp.transpose(c, (0, 2, 1))[:, :, None, :]).reshape(S5_P, S5_W)
    flat = lambda t: t.reshape(1, S5_P)
    return flat(a_re), flat(a_im), flat(f_re), flat(f_im), bd(b_re), bd(b_im), cd(c_re), cd(c_im)


def _scan_tables(a_re, a_im, reverse):
    pows = [(a_re, a_im)]
    for _ in range(7):
        pr, pi = pows[-1]
        pows.append((pr * a_re - pi * a_im, pr * a_im + pi * a_re))
    order = list(range(7, -1, -1)) if reverse else list(range(8))
    z = jnp.zeros_like(a_re)
    rows = [pows[0][0], pows[0][1], pows[1][0], pows[1][1], pows[3][0], pows[3][1], z, z]
    rows += [pows[k][0] for k in order] + [pows[k][1] for k in order]
    return jnp.concatenate(rows, axis=0)


def _block_scan(er, ei, tab_ref, cr, ci, reverse):
    rows = lax.broadcasted_iota(jnp.int32, er.shape, 0)
    for idx, s in enumerate((1, 2, 4)):
        if reverse:
            sr, si, keep = pltpu.roll(er, 8 - s, axis=0), pltpu.roll(ei, 8 - s, axis=0), rows < 8 - s
        else:
            sr, si, keep = pltpu.roll(er, s, axis=0), pltpu.roll(ei, s, axis=0), rows >= s
        sr, si = jnp.where(keep, sr, 0.0), jnp.where(keep, si, 0.0)
        ar, ai = tab_ref[2 * idx:2 * idx + 1, :], tab_ref[2 * idx + 1:2 * idx + 2, :]
        er, ei = er + ar * sr - ai * si, ei + ar * si + ai * sr
    pr, pi_ = tab_ref[8:16, :], tab_ref[16:24, :]
    er, ei = er + pr * cr - pi_ * ci, ei + pr * ci + pi_ * cr
    return er, ei


def s5_scan_fwd(bu_re, bu_im, a_re, a_im, f_re, f_im, name):
    L, P = bu_re.shape
    tl = _rtile(L, 512)
    tab = _scan_tables(a_re, a_im, False)
    fvec = jnp.concatenate([f_re, f_im] + [jnp.zeros_like(f_re)] * 6, axis=0)

    def body(br_ref, bi_ref, tab_ref, f_ref, xr_ref, xi_ref, c_ref):
        @pl.when(pl.program_id(0) == 0)
        def _():
            c_ref[...] = jnp.zeros_like(c_ref)

        def blk(i, carry):
            cr, ci = carry
            rows = pl.ds(pl.multiple_of(i * 8, 8), 8)
            br, bi = br_ref[rows, :], bi_ref[rows, :]
            fr, fi = f_ref[0:1, :], f_ref[1:2, :]
            er, ei = _block_scan(fr * br - fi * bi, fr * bi + fi * br, tab_ref, cr, ci, False)
            xr_ref[rows, :] = er
            xi_ref[rows, :] = ei
            return er[7:8, :], ei[7:8, :]

        cr, ci = lax.fori_loop(0, tl // 8, blk, (c_ref[0:1, :], c_ref[1:2, :]))
        c_ref[0:1, :] = cr
        c_ref[1:2, :] = ci

    big = pl.BlockSpec((tl, P), lambda i: (i, 0))
    out = jax.ShapeDtypeStruct((L, P), F32)
    return pl.pallas_call(
        body, name=name, grid=(L // tl,),
        in_specs=[big, big, pl.BlockSpec((24, P), lambda i: (0, 0)), pl.BlockSpec((8, P), lambda i: (0, 0))],
        out_specs=(big, big), out_shape=(out, out), scratch_shapes=[pltpu.VMEM((8, P), F32)],
        compiler_params=_cparams("arbitrary"))(bu_re, bu_im, tab, fvec)


def s5_scan_bwd(dx_re, dx_im, x_re, x_im, bu_re, bu_im, a_re, a_im, f_re, f_im, name):
    L, P = dx_re.shape
    tl = _rtile(L, 256)
    nt = L // tl
    tab = _scan_tables(a_re, -a_im, True)
    fvec = jnp.concatenate([f_re, f_im] + [jnp.zeros_like(f_re)] * 6, axis=0)

    def body(gr_ref, gi_ref, xr_ref, xi_ref, br_ref, bi_ref, tab_ref, f_ref, dbr_ref, dbi_ref, s_ref, c_ref):
        @pl.when(pl.program_id(0) == 0)
        def _():
            c_ref[...] = jnp.zeros_like(c_ref)
            s_ref[...] = jnp.zeros_like(s_ref)

        def blk(k, carry):
            cr, ci = carry
            i = tl // 8 - 1 - k
            rows = pl.ds(pl.multiple_of(i * 8, 8), 8)
            er, ei = _block_scan(gr_ref[rows, :], gi_ref[rows, :], tab_ref, cr, ci, True)
            rid = lax.broadcasted_iota(jnp.int32, er.shape, 0)
            sr = jnp.where(rid == 7, cr, pltpu.roll(er, 7, axis=0))
            si = jnp.where(rid == 7, ci, pltpu.roll(ei, 7, axis=0))
            xr, xi = xr_ref[rows, :], xi_ref[rows, :]
            s_ref[0:8, :] += sr * xr + si * xi
            s_ref[8:16, :] += si * xr - sr * xi
            br, bi = br_ref[rows, :], bi_ref[rows, :]
            s_ref[16:24, :] += er * br + ei * bi
            s_ref[24:32, :] += ei * br - er * bi
            fr, fi = f_ref[0:1, :], f_ref[1:2, :]
            dbr_ref[rows, :] = fr * er + fi * ei
            dbi_ref[rows, :] = fr * ei - fi * er
            return er[0:1, :], ei[0:1, :]

        cr, ci = lax.fori_loop(0, tl // 8, blk, (c_ref[0:1, :], c_ref[1:2, :]))
        c_ref[0:1, :] = cr
        c_ref[1:2, :] = ci

    big = pl.BlockSpec((tl, P), lambda i: (nt - 1 - i, 0))
    out = jax.ShapeDtypeStruct((L, P), F32)
    return pl.pallas_call(
        body, name=name, grid=(nt,),
        in_specs=[big] * 6 + [pl.BlockSpec((24, P), lambda i: (0, 0)), pl.BlockSpec((8, P), lambda i: (0, 0))],
        out_specs=(big, big, pl.BlockSpec((32, P), lambda i: (0, 0))),
        out_shape=(out, out, jax.ShapeDtypeStruct((32, P), F32)), scratch_shapes=[pltpu.VMEM((8, P), F32)],
        compiler_params=_cparams("arbitrary"))(dx_re, dx_im, x_re, x_im, bu_re, bu_im, tab, fvec)


_GK, _GC = math.sqrt(2.0 / math.pi), 0.044715


def _gelu(y):
    return 0.5 * y * (1.0 + jnp.tanh(_GK * (y + _GC * y * y * y)))


def _dgelu(y):
    t = jnp.tanh(_GK * (y + _GC * y * y * y))
    return 0.5 * (1.0 + t) + 0.5 * y * (1.0 - t * t) * _GK * (1.0 + 3.0 * _GC * y * y)


def s5_out_fwd(x_re, x_im, u, cd_re, cd_im, dskip, glu_w, glu_b, name):
    L = u.shape[0]
    tl = _rtile(L, 512)

    def body(xr_ref, xi_ref, u_ref, cr_ref, ci_ref, d_ref, w_ref, b_ref, y_ref, o_ref):
        y = (jnp.dot(xr_ref[...].astype(BF16), cr_ref[...], preferred_element_type=F32)
             - jnp.dot(xi_ref[...].astype(BF16), ci_ref[...], preferred_element_type=F32)
             + d_ref[...] * u_ref[...])
        y_ref[...] = y
        g = _gelu(y)
        z = jnp.dot(g.astype(BF16), w_ref[...], preferred_element_type=F32) + b_ref[...]
        o_ref[...] = (g * _sigmoid(z)).astype(BF16)

    big = pl.BlockSpec((tl, S5_P), lambda i: (i, 0))
    sm = pl.BlockSpec((tl, S5_W), lambda i: (i, 0))
    full = lambda r, c: pl.BlockSpec((r, c), lambda i: (0, 0))
    return pl.pallas_call(
        body, name=name, grid=(L // tl,),
        in_specs=[big, big, sm, full(S5_P, S5_W), full(S5_P, S5_W), full(1, S5_W), full(S5_W, S5_W), full(1, S5_W)],
        out_specs=(sm, sm),
        out_shape=(jax.ShapeDtypeStruct((L, S5_W), F32), jax.ShapeDtypeStruct((L, S5_W), BF16)),
        compiler_params=_cparams("parallel"))(x_re, x_im, u, cd_re, cd_im, dskip, glu_w, glu_b)


def s5_out_bwd(dout, y, u, x_re, x_im, cd_re, cd_im, dskip, glu_w, glu_b, name):
    L = u.shape[0]
    tl = _rtile(L, 256)
    nt_dims = (((1,), (1,)), ((), ()))
    tn_dims = (((0,), (0,)), ((), ()))

    def body(do_ref, y_ref, u_ref, xr_ref, xi_ref, cr_ref, ci_ref, d_ref, w_ref, b_ref,
             dxr_ref, dxi_ref, du_ref, dcr_ref, dci_ref, dw_ref, s_ref):
        @pl.when(pl.program_id(0) == 0)
        def _():
            dcr_ref[...] = jnp.zeros_like(dcr_ref)
            dci_ref[...] = jnp.zeros_like(dci_ref)
            dw_ref[...] = jnp.zeros_like(dw_ref)
            s_ref[...] = jnp.zeros_like(s_ref)

        yv, dov = y_ref[...], do_ref[...]
        g = _gelu(yv)
        gb = g.astype(BF16)
        sg = _sigmoid(jnp.dot(gb, w_ref[...], preferred_element_type=F32) + b_ref[...])
        dz = dov * g * sg * (1.0 - sg)
        dzb = dz.astype(BF16)
        dg = dov * sg + lax.dot_general(dzb, w_ref[...], nt_dims, preferred_element_type=F32)
        dw_ref[...] += lax.dot_general(gb, dzb, tn_dims, preferred_element_type=F32)
        dy = dg * _dgelu(yv)
        dyb = dy.astype(BF16)
        s_ref[0:8, :] += _fold8(dy * u_ref[...])
        s_ref[8:16, :] += _fold8(dz)
        du_ref[...] = dy * d_ref[...]
        dxr_ref[...] = lax.dot_general(dyb, cr_ref[...], nt_dims, preferred_element_type=F32)
        dxi_ref[...] = -lax.dot_general(dyb, ci_ref[...], nt_dims, preferred_element_type=F32)
        dcr_ref[...] += lax.dot_general(xr_ref[...].astype(BF16), dyb, tn_dims, preferred_element_type=F32)
        dci_ref[...] -= lax.dot_general(xi_ref[...].astype(BF16), dyb, tn_dims, preferred_element_type=F32)

    big = pl.BlockSpec((tl, S5_P), lambda i: (i, 0))
    sm = pl.BlockSpec((tl, S5_W), lambda i: (i, 0))
    full = lambda r, c: pl.BlockSpec((r, c), lambda i: (0, 0))
    sd = jax.ShapeDtypeStruct
    return pl.pallas_call(
        body, name=name, grid=(L // tl,),
        in_specs=[sm, sm, sm, big, big, full(S5_P, S5_W), full(S5_P, S5_W), full(1, S5_W), full(S5_W, S5_W),
                  full(1, S5_W)],
        out_specs=(big, big, sm, full(S5_P, S5_W), full(S5_P, S5_W), full(S5_W, S5_W), full(16, S5_W)),
        out_shape=(sd((L, S5_P), F32), sd((L, S5_P), F32), sd((L, S5_W), F32), sd((S5_P, S5_W), F32),
                   sd((S5_P, S5_W), F32), sd((S5_W, S5_W), F32), sd((16, S5_W), F32)),
        compiler_params=_cparams("arbitrary"))(dout, y, u, x_re, x_im, cd_re, cd_im, dskip, glu_w, glu_b)


def s5_block_fwd(u, params, dskip, glu_w, glu_b, tag):
    a_re, a_im, f_re, f_im, bd_re, bd_im, cd_re, cd_im = params
    bu_re = matmul([(u, bd_re.astype(BF16))], "nn", tag + "_bure")
    bu_im = matmul([(u, bd_im.astype(BF16))], "nn", tag + "_buim")
    x_re, x_im = s5_scan_fwd(bu_re, bu_im, a_re, a_im, f_re, f_im, tag + "_scan")
    y, out = s5_out_fwd(x_re, x_im, u, cd_re.astype(BF16), cd_im.astype(BF16), dskip, glu_w, glu_b, tag + "_out")
    return out, (u, bu_re, bu_im, x_re, x_im, y)


def s5_block_bwd(dout, res, params, dskip, glu_w, glu_b, tag):
    u, bu_re, bu_im, x_re, x_im, y = res
    a_re, a_im, f_re, f_im, bd_re, bd_im, cd_re, cd_im = params
    dxr, dxi, du, dcr, dci, dglu_w, sums = s5_out_bwd(dout, y, u, x_re, x_im, cd_re.astype(BF16), cd_im.astype(BF16),
                                                      dskip, glu_w, glu_b, tag + "_dout")
    dbr, dbi, acc = s5_scan_bwd(dxr, dxi, x_re, x_im, bu_re, bu_im, a_re, a_im, f_re, f_im, tag + "_dscan")
    du = du + matmul([(dbr, bd_re.astype(BF16)), (dbi, bd_im.astype(BF16))], "nt", tag + "_du")
    dbd_re = matmul([(u, dbr)], "tn", tag + "_dbdre")
    dbd_im = matmul([(u, dbi)], "tn", tag + "_dbdim")
    acc = acc.reshape(4, 8, S5_P).sum(axis=1)
    s = sums.reshape(2, 8, S5_W).sum(axis=1)
    cot = (acc[0:1], acc[1:2], acc[2:3], acc[3:4], dbd_re, dbd_im, dcr, dci)
    return du, cot, dict(dskip=s[0], glu_w=dglu_w, glu_b=s[1])


DN_QKV0, DN_Z0, DN_NT = 2, 20, 18


def _l2n(s, j):
    r = lax.rsqrt(jnp.sum(s * s, axis=-1, keepdims=True) + EPS)
    scale = jnp.where(j < DN_H, DN_DK ** -0.5, 1.0)
    return r, scale


def dn_prep_fwd(rin, cw, name):
    L = rin.shape[0]
    tl = _rtile(L, 512)
    hb = tl // 8

    def body(x_ref, h_ref, w_ref, o_ref):
        j = pl.program_id(0)
        first = pl.program_id(1) == 0
        x, h = x_ref[...], h_ref[...]
        xc = w_ref[3:4, :] * x
        for k in range(1, 4):
            xc = xc + w_ref[3 - k:4 - k, :] * _shift_down(x, h, k, first)
        s = _silu(xc)
        r, scale = _l2n(s, j)
        o_ref[...] = jnp.where(j < 2 * DN_H, s * r * scale, s)

    main = pl.BlockSpec((tl, 128), lambda j, i: (i, DN_QKV0 + j))
    halo = pl.BlockSpec((8, 128), lambda j, i: (jnp.maximum(i * hb - 1, 0), DN_QKV0 + j))
    return pl.pallas_call(
        body, name=name, grid=(DN_NT, L // tl),
        in_specs=[main, halo, pl.BlockSpec((4, 128), lambda j, i: (0, j))],
        out_specs=pl.BlockSpec((tl, 128), lambda j, i: (i, j)),
        out_shape=jax.ShapeDtypeStruct((L, DN_NT * 128), F32),
        compiler_params=_cparams("parallel", "parallel"))(rin, rin, cw)


def dn_prep_bwd(rin, cw, dout, name):
    L = rin.shape[0]
    tl = _rtile(L, 512)
    hb = tl // 8
    nrt = L // tl

    def body(x_ref, h_ref, w_ref, d_ref, dx_ref, s_ref, c_ref):
        j = pl.program_id(0)
        i = pl.program_id(1)
        first = i == nrt - 1

        @pl.when(i == 0)
        def _():
            s_ref[...] = jnp.zeros_like(s_ref)
            c_ref[...] = jnp.zeros_like(c_ref)

        x, h = x_ref[...], h_ref[...]
        taps = [x] + [_shift_down(x, h, k, first) for k in range(1, 4)]
        xc = w_ref[3:4, :] * x
        for k in range(1, 4):
            xc = xc + w_ref[3 - k:4 - k, :] * taps[k]
        s = _silu(xc)
        r, scale = _l2n(s, j)
        n = s * r
        dn = d_ref[...] * scale
        ds_norm = r * (dn - n * jnp.sum(dn * n, axis=-1, keepdims=True))
        ds = jnp.where(j < 2 * DN_H, ds_norm, d_ref[...])
        dxc = ds * _dsilu(xc)
        nxt = c_ref[...]
        dx = w_ref[3:4, :] * dxc
        for k in range(1, 4):
            dx = dx + w_ref[3 - k:4 - k, :] * _shift_up(dxc, nxt, k)
        dx_ref[...] = dx
        c_ref[...] = dxc[0:8, :]
        for k in range(4):
            s_ref[8 * (3 - k):8 * (3 - k) + 8, :] += _fold8(dxc * taps[k])

    rev = lambda i: nrt - 1 - i
    main = pl.BlockSpec((tl, 128), lambda j, i: (rev(i), DN_QKV0 + j))
    halo = pl.BlockSpec((8, 128), lambda j, i: (jnp.maximum(rev(i) * hb - 1, 0), DN_QKV0 + j))
    own = pl.BlockSpec((tl, 128), lambda j, i: (rev(i), j))
    return pl.pallas_call(
        body, name=name, grid=(DN_NT, nrt),
        in_specs=[main, halo, pl.BlockSpec((4, 128), lambda j, i: (0, j)), own],
        out_specs=(own, pl.BlockSpec((32, 128), lambda j, i: (0, j))),
        out_shape=(jax.ShapeDtypeStruct((L, DN_NT * 128), F32), jax.ShapeDtypeStruct((32, DN_NT * 128), F32)),
        scratch_shapes=[pltpu.VMEM((8, 128), F32)],
        compiler_params=_cparams("parallel", "arbitrary"))(rin, rin, cw, dout)


_HI = lax.Precision.HIGHEST
_NT = (((1,), (1,)), ((), ()))
_TN = (((0,), (0,)), ((), ()))


def _mm(a, b, dims=(((1,), (0,)), ((), ())), hi=False):
    if hi:
        return lax.dot_general(a, b, dims, precision=_HI, preferred_element_type=F32)
    return lax.dot_general(a.astype(BF16), b.astype(BF16), dims, preferred_element_type=F32)


def _dn_chunk_common(q, k, v, gc, gr, beta):
    C = DN_C
    ri = lax.broadcasted_iota(jnp.int32, (C, C), 0)
    ci = lax.broadcasted_iota(jnp.int32, (C, C), 1)
    causal, strict = ri >= ci, ri > ci
    gam = jnp.where(causal, jnp.exp(jnp.where(causal, gc - gr, 0.0)), 0.0)
    kk = _mm(k, k, _NT)
    nmat = jnp.where(strict, beta * kk * gam, 0.0)
    eye = (ri == ci).astype(F32)
    t = eye - nmat
    m = _mm(nmat, nmat, hi=True)
    for step in range(5):
        t = t + _mm(t, m, hi=True)
        if step < 4:
            m = _mm(m, m, hi=True)
    eg = jnp.exp(gc)
    glast = gc[C - 1:C, :]
    el = jnp.exp(glast - gc)
    rhs = jnp.concatenate([v * beta, k * (beta * eg)], axis=1)
    sol = _mm(t, rhs, hi=True)
    qk_raw = _mm(q, k, _NT)
    return dict(causal=causal, strict=strict, gam=gam, kk=kk, t=t, eg=eg, el=el, gl=jnp.exp(glast),
                sol=sol, qk_raw=qk_raw)


def dn_chunk_fwd(qkv, gcol, grow, bcol, name):
    L = qkv.shape[0]
    C, W = DN_C, DN_H * DN_DK
    ncb = 8
    tl = ncb * C
    nchunks = L // C

    def body(q_ref, k_ref, v_ref, gc_ref, gr_ref, b_ref, o_ref, sh_ref, s_ref):
        @pl.when(pl.program_id(0) == 0)
        def _():
            s_ref[...] = jnp.zeros_like(s_ref)

        def chunk(c, _):
            rows = pl.ds(pl.multiple_of(c * C, C), C)
            grow_c = gr_ref[c]
            for h in range(DN_H):
                cs = slice(h * 128, (h + 1) * 128)
                q, k, v = q_ref[rows, cs], k_ref[rows, cs], v_ref[rows, cs]
                gc, beta = gc_ref[rows, h:h + 1], b_ref[rows, h:h + 1]
                w_ = _dn_chunk_common(q, k, v, gc, grow_c[h:h + 1, :], beta)
                S = s_ref[cs, :]
                sh_ref[c, cs, :] = S
                u, w = w_["sol"][:, :128], w_["sol"][:, 128:]
                vn = u - _mm(w, S)
                o_ref[rows, cs] = _mm(q * w_["eg"], S) + _mm(w_["qk_raw"] * w_["gam"], vn)
                s_ref[cs, :] = S * w_["gl"] + _mm(k * w_["el"], vn, _TN)
            return 0

        lax.fori_loop(0, ncb, chunk, 0)

    col = lambda b: pl.BlockSpec((tl, W), lambda i: (i, b))
    small = pl.BlockSpec((tl, 8), lambda i: (i, 0))
    return pl.pallas_call(
        body, name=name, grid=(L // tl,),
        in_specs=[col(0), col(1), col(2), small, pl.BlockSpec((ncb, 8, C), lambda i: (i, 0, 0)), small],
        out_specs=(pl.BlockSpec((tl, W), lambda i: (i, 0)), pl.BlockSpec((ncb, W, 128), lambda i: (i, 0, 0))),
        out_shape=(jax.ShapeDtypeStruct((L, W), F32), jax.ShapeDtypeStruct((nchunks, W, 128), F32)),
        scratch_shapes=[pltpu.VMEM((W, 128), F32)],
        compiler_params=_cparams("arbitrary"))(qkv, qkv, qkv, gcol, grow, bcol)


def dn_chunk_bwd(qkv, gcol, grow, bcol, shist, do, name):
    L = qkv.shape[0]
    C, W = DN_C, DN_H * DN_DK
    ncb = 8
    tl = ncb * C
    nchunks = L // C
    nt = L // tl

    def body(q_ref, k_ref, v_ref, gc_ref, gr_ref, b_ref, sh_ref, do_ref,
             dq_ref, dk_ref, dv_ref, dgc_ref, dgr_ref, db_ref, ds_ref):
        @pl.when(pl.program_id(0) == 0)
        def _():
            ds_ref[...] = jnp.zeros_like(ds_ref)

        lane8 = lax.broadcasted_iota(jnp.int32, (C, 8), 1)
        sub8 = lax.broadcasted_iota(jnp.int32, (8, C), 0)
        rowid = lax.broadcasted_iota(jnp.int32, (C, 1), 0)

        def chunk(cc, _):
            c = ncb - 1 - cc
            rows = pl.ds(pl.multiple_of(c * C, C), C)
            grow_c = gr_ref[c]
            dgc_all = jnp.zeros((C, 8), F32)
            db_all = jnp.zeros((C, 8), F32)
            dgr_all = jnp.zeros((8, C), F32)
            for h in range(DN_H):
                cs = slice(h * 128, (h + 1) * 128)
                q, k, v = q_ref[rows, cs], k_ref[rows, cs], v_ref[rows, cs]
                gc, beta = gc_ref[rows, h:h + 1], b_ref[rows, h:h + 1]
                w_ = _dn_chunk_common(q, k, v, gc, grow_c[h:h + 1, :], beta)
                gam, kk, t, eg, el, gl, sol = (w_[n] for n in ("gam", "kk", "t", "eg", "el", "gl", "sol"))
                S = sh_ref[c, cs, :]
                dS = ds_ref[cs, :]
                dov = do_ref[rows, cs]
                u, w = sol[:, :128], sol[:, 128:]
                qd, kd = q * eg, k * el
                qk = w_["qk_raw"] * gam
                vn = u - _mm(w, S)
                dvn = _mm(qk, dov, _TN) + _mm(kd, dS)
                dqd = _mm(dov, S, _NT)
                dqk = jnp.where(w_["causal"], _mm(dov, vn, _NT), 0.0)
                dkd = _mm(vn, dS, _NT)
                dgl = jnp.sum(jnp.sum(dS * S, axis=1, keepdims=True), axis=0, keepdims=True)
                dw = -_mm(dvn, S, _NT)
                ds_ref[cs, :] = dS * gl + _mm(qd, dov, _TN) - _mm(w, dvn, _TN)
                drhs = _mm(t, jnp.concatenate([dvn, dw], axis=1), _TN, hi=True)
                dn = jnp.where(w_["strict"], -_mm(drhs, sol, _NT, hi=True), 0.0)
                drv, drk = drhs[:, :128], drhs[:, 128:]
                t2 = jnp.sum(drk * k, axis=1, keepdims=True)
                dbeta = jnp.sum(drv * v, axis=1, keepdims=True) + t2 * eg
                deg = t2 * beta
                x = dn * gam
                dbeta = dbeta + jnp.sum(x * kk, axis=1, keepdims=True)
                dkk = x * beta
                draw = dqk * gam
                dgam = dn * beta * kk + dqk * w_["qk_raw"]
                mm_ = dgam * gam
                dgc = jnp.sum(mm_, axis=1, keepdims=True)
                dgr = -jnp.sum(mm_, axis=0, keepdims=True)
                deg = deg + jnp.sum(dqd * q, axis=1, keepdims=True)
                r_ = jnp.sum(dkd * k, axis=1, keepdims=True) * el
                dglast = jnp.sum(r_, axis=0, keepdims=True) + dgl * gl
                dgc = dgc + deg * eg - r_ + jnp.where(rowid == C - 1, dglast, 0.0)
                dq_ref[rows, cs] = _mm(draw, k) + dqd * eg
                dk_ref[rows, cs] = (drk * (beta * eg) + _mm(dkk, k) + _mm(dkk, k, _TN) + _mm(draw, q, _TN)
                                    + dkd * el)
                dv_ref[rows, cs] = drv * beta
                dgc_all = dgc_all + jnp.where(lane8 == h, dgc, 0.0)
                db_all = db_all + jnp.where(lane8 == h, dbeta, 0.0)
                dgr_all = dgr_all + jnp.where(sub8 == h, dgr, 0.0)
            dgc_ref[rows, :] = dgc_all
            db_ref[rows, :] = db_all
            dgr_ref[c] = dgr_all
            return 0

        lax.fori_loop(0, ncb, chunk, 0)

    rev = lambda i: nt - 1 - i
    col = lambda b: pl.BlockSpec((tl, W), lambda i: (rev(i), b))
    small = pl.BlockSpec((tl, 8), lambda i: (rev(i), 0))
    g3 = pl.BlockSpec((ncb, 8, C), lambda i: (rev(i), 0, 0))
    sd = jax.ShapeDtypeStruct
    dq, dk, dv, dgc, dgr, db = pl.pallas_call(
        body, name=name, grid=(nt,),
        in_specs=[col(0), col(1), col(2), small, g3, small,
                  pl.BlockSpec((ncb, W, 128), lambda i: (rev(i), 0, 0)), col(0)],
        out_specs=(col(0), col(0), col(0), small, g3, small),
        out_shape=(sd((L, W), F32), sd((L, W), F32), sd((L, W), F32), sd((L, 8), F32), sd((nchunks, 8, C), F32),
                   sd((L, 8), F32)),
        scratch_shapes=[pltpu.VMEM((W, 128), F32)],
        compiler_params=_cparams("arbitrary"))(qkv, qkv, qkv, gcol, grow, bcol, shist, do)
    return jnp.concatenate([dq, dk, dv], axis=1), dgc, dgr, db


def dn_out_fwd(o, rin, nw, name):
    L = o.shape[0]
    tl = _rtile(L, 512)

    def body(o_ref, z_ref, w_ref, y_ref):
        ov = o_ref[...]
        r = lax.rsqrt(jnp.mean(ov * ov, axis=-1, keepdims=True) + EPS)
        y_ref[...] = (ov * r * w_ref[...] * _silu(z_ref[...])).astype(BF16)

    own = pl.BlockSpec((tl, 128), lambda j, i: (i, j))
    return pl.pallas_call(
        body, name=name, grid=(DN_H, L // tl),
        in_specs=[own, pl.BlockSpec((tl, 128), lambda j, i: (i, DN_Z0 + j)), pl.BlockSpec((1, 128), lambda j, i: (0, 0))],
        out_specs=own, out_shape=jax.ShapeDtypeStruct((L, DN_H * 128), BF16),
        compiler_params=_cparams("parallel", "parallel"))(o, rin, nw)


def dn_out_bwd(dy, o, rin, nw, name):
    L = o.shape[0]
    tl = _rtile(L, 512)

    def body(dy_ref, o_ref, z_ref, w_ref, do_ref, dz_ref, s_ref):
        @pl.when(pl.program_id(1) == 0)
        def _():
            s_ref[...] = jnp.zeros_like(s_ref)

        ov, zv, d = o_ref[...], z_ref[...], dy_ref[...]
        r = lax.rsqrt(jnp.mean(ov * ov, axis=-1, keepdims=True) + EPS)
        n = ov * r
        dnw = d * _silu(zv)
        dz_ref[...] = d * n * w_ref[...] * _dsilu(zv)
        dn = dnw * w_ref[...]
        do_ref[...] = r * (dn - n * jnp.mean(dn * n, axis=-1, keepdims=True))
        s_ref[...] += _fold8(dnw * n)

    own = pl.BlockSpec((tl, 128), lambda j, i: (i, j))
    sd = jax.ShapeDtypeStruct
    return pl.pallas_call(
        body, name=name, grid=(DN_H, L // tl),
        in_specs=[own, own, pl.BlockSpec((tl, 128), lambda j, i: (i, DN_Z0 + j)),
                  pl.BlockSpec((1, 128), lambda j, i: (0, 0))],
        out_specs=(own, own, pl.BlockSpec((8, 128), lambda j, i: (0, j))),
        out_shape=(sd((L, DN_H * 128), F32), sd((L, DN_H * 128), F32), sd((8, DN_H * 128), F32)),
        compiler_params=_cparams("parallel", "arbitrary"))(dy, o, rin, nw)


def dn_gates(a, beta_raw, a_log, dt_bias):
    L = a.shape[0]
    beta = jax.nn.sigmoid(beta_raw)
    g = -jnp.exp(a_log) * jax.nn.softplus(a + dt_bias)
    G = jnp.cumsum(g.reshape(L // DN_C, DN_C, DN_H), axis=1)
    pad = lambda t: jnp.pad(t, ((0, 0), (0, 8 - DN_H)))
    gcol = pad(G.reshape(L, DN_H))
    grow = jnp.pad(jnp.transpose(G, (0, 2, 1)), ((0, 0), (0, 8 - DN_H), (0, 0)))
    return gcol, grow, pad(beta)


def dn_block_fwd(rin, cw, a_log, dt_bias, out_norm, tag):
    gates, gates_vjp = jax.vjp(dn_gates, rin[:, 3328:3334], rin[:, 3334:3340], a_log, dt_bias)
    qkv = dn_prep_fwd(rin, cw, tag + "_prep")
    o, shist = dn_chunk_fwd(qkv, *gates, tag + "_chunk")
    yd = dn_out_fwd(o, rin, out_norm.reshape(1, 128), tag + "_onorm")
    return yd, (qkv, gates, gates_vjp, o, shist)


def dn_block_bwd(dyd, res, rin, cw, out_norm, tag):
    qkv, gates, gates_vjp, o, shist = res
    do, dz, nsum = dn_out_bwd(dyd, o, rin, out_norm.reshape(1, 128), tag + "_donorm")
    dqkv, dgc, dgr, db = dn_chunk_bwd(qkv, *gates, shist, do, tag + "_dchunk")
    da, dbraw, g_alog, g_dtb = gates_vjp((dgc, dgr, db))
    dx, csum = dn_prep_bwd(rin, cw, dqkv, tag + "_dprep")
    grads = dict(conv=csum.reshape(4, 8, DN_NT * 128).sum(axis=1), a_log=g_alog, dt_bias=g_dtb,
                 out_norm=nsum.sum(axis=0).reshape(DN_H, 128).sum(axis=0))
    return dx, dz, da, dbraw, grads


_HBM = pl.BlockSpec(memory_space=pltpu.HBM)


def _mesh_pos():
    xi, yi, ci = lax.axis_index("x"), lax.axis_index("y"), lax.axis_index("c")
    return xi, yi, ci, 4 * xi + 2 * yi + ci


def _peer(xi, yi, ci, k):
    px = 1 - xi if (k >> 2) & 1 else xi
    py = 1 - yi if (k >> 1) & 1 else yi
    pc = 1 - ci if k & 1 else ci
    return (px, py, pc), 4 * px + 2 * py + pc


def _exchange(x, gather, name):
    shp = x.shape[-2:]

    def body(x_ref, o_ref, send_sems, recv_sems, lsem):
        xi, yi, ci, me = _mesh_pos()
        src_of = (lambda lin: x_ref) if gather else (lambda lin: x_ref.at[lin])
        local = pltpu.make_async_copy(src_of(me), o_ref.at[me], lsem)
        local.start()
        copies = []
        for k in range(1, N_DEV):
            peer, lin = _peer(xi, yi, ci, k)
            cp = pltpu.make_async_remote_copy(
                src_ref=src_of(lin), dst_ref=o_ref.at[me], send_sem=send_sems.at[k - 1],
                recv_sem=recv_sems.at[k - 1], device_id=peer, device_id_type=pl.DeviceIdType.MESH)
            cp.start()
            copies.append(cp)
        for cp in copies:
            cp.wait()
        local.wait()

    return pl.pallas_call(
        body, name=name, in_specs=[_HBM], out_specs=_HBM,
        out_shape=jax.ShapeDtypeStruct((N_DEV,) + shp, x.dtype),
        scratch_shapes=[pltpu.SemaphoreType.DMA((N_DEV - 1,)), pltpu.SemaphoreType.DMA((N_DEV - 1,)),
                        pltpu.SemaphoreType.DMA],
    )(x)


def all_gather(x, name):
    return _exchange(x, True, name)


def all_to_all(x, name):
    return _exchange(x, False, name)


def reduce_adamw(gsrc, w, m, v, name):
    S, R, C = gsrc.shape
    tr = _rtile(R, 128)
    c1 = 1.0 - ADAM_B1 ** ADAM_STEP
    c2 = 1.0 - ADAM_B2 ** ADAM_STEP

    def body(g_ref, w_ref, m_ref, v_ref, go_ref, d_ref, mo_ref, vo_ref):
        g = g_ref[0]
        for s in range(1, S):
            g = g + g_ref[s]
        go_ref[...] = g
        mn = ADAM_B1 * m_ref[...] + (1.0 - ADAM_B1) * g
        vn = ADAM_B2 * v_ref[...] + (1.0 - ADAM_B2) * (g * g)
        mo_ref[...] = mn
        vo_ref[...] = vn
        d_ref[...] = -ADAM_LR * ((mn / c1) / (jnp.sqrt(vn / c2) + ADAM_EPS) + ADAM_WD * w_ref[...])

    big = pl.BlockSpec((tr, C), lambda i: (i, 0))
    o = jax.ShapeDtypeStruct((R, C), F32)
    return pl.pallas_call(
        body, name=name, grid=(R // tr,),
        in_specs=[pl.BlockSpec((S, tr, C), lambda i: (0, i, 0)), big, big, big],
        out_specs=(big, big, big, big), out_shape=(o, o, o, o),
        compiler_params=_cparams("parallel"))(gsrc, w, m, v)


def _to_slabs(g, ax):
    shp = g.shape
    g = g.reshape(shp[:ax] + (N_DEV, shp[ax] // N_DEV) + shp[ax + 1:])
    return jnp.moveaxis(g, ax, 0).reshape(N_DEV, -1)


def _from_slabs(s, ax, shp):
    s = s.reshape((N_DEV,) + shp[:ax] + (shp[ax] // N_DEV,) + shp[ax + 1:])
    return jnp.moveaxis(s, 0, ax).reshape(shp)


def _pack_rows(flat, width, row_mult):
    n = flat.shape[-1]
    per = width * row_mult
    tot = -(-n // per) * per
    flat = jnp.pad(flat, [(0, 0)] * (flat.ndim - 1) + [(0, tot - n)])
    return flat.reshape(flat.shape[:-1] + (tot // width, width))


def _offsets(sizes):
    offs, o = [], 0
    for s in sizes:
        offs.append(o)
        o += s
    return offs


WEIGHTS = ['ada_w', 'ada_b', 'norm_mix', 'norm_ffn', 'attn_w_in', 'attn_q_norm_a', 'attn_k_norm_a', 'attn_q_norm_b',
           'attn_k_norm_b', 'attn_sinks', 'attn_w_out', 'rec_w_in', 's5_lambda_re', 's5_lambda_im', 's5_log_dt',
           's5_b_re', 's5_b_im', 's5_c_re', 's5_c_im', 's5_d', 's5_glu_w', 's5_glu_b', 'dn_conv', 'dn_a_log',
           'dn_dt_bias', 'dn_out_norm', 'rec_w_out', 'ffn_w_up', 'ffn_conv', 'ffn_w_down']
BIG = [('attn_w_in', 2, (1, D, ATTN_IN)), ('attn_w_out', 1, (1, D, D)), ('rec_w_in', 1, (1, D, REC_IN)),
       ('s5_glu_w', 1, (1, S5_W, S5_W)), ('rec_w_out', 1, (1, D, D)), ('ffn_w_up', 2, (2, D, 2 * D_FF)),
       ('ffn_w_down', 1, (2, D_FF, D))]
SMALL_SHARDED = [('s5_d', 1, (1, S5_W)), ('s5_glu_b', 1, (1, S5_W)), ('dn_conv', 2, (1, 4, 2304)),
                 ('ffn_conv', 2, (2, 3, 2 * D_FF))]
REPLICATED = [('ada_b', (2, 6 * D)), ('norm_mix', (2, D)), ('norm_ffn', (2, D)), ('attn_q_norm_a', (1, HD)),
              ('attn_k_norm_a', (1, HD)), ('attn_q_norm_b', (1, HD)), ('attn_k_norm_b', (1, HD)),
              ('attn_sinks', (1, 8)), ('s5_lambda_re', (1, 16, 64)), ('s5_lambda_im', (1, 16, 64)),
              ('s5_log_dt', (1, 16)), ('s5_b_re', (1, 16, 64, 16)), ('s5_b_im', (1, 16, 64, 16)),
              ('s5_c_re', (1, 16, 16, 64)), ('s5_c_im', (1, 16, 16, 64)), ('dn_a_log', (1, DN_H)),
              ('dn_dt_bias', (1, DN_H)), ('dn_out_norm', (1, 128))]


def _numel(shp):
    return int(np.prod(shp))


def kernel(x, c, ada_w, ada_b, norm_mix, norm_ffn, attn_w_in, attn_q_norm_a, attn_k_norm_a, attn_q_norm_b, attn_k_norm_b, attn_sinks, attn_w_out, rec_w_in, s5_lambda_re, s5_lambda_im, s5_log_dt, s5_b_re, s5_b_im, s5_c_re, s5_c_im, s5_d, s5_glu_w, s5_glu_b, dn_conv, dn_a_log, dn_dt_bias, dn_out_norm, rec_w_out, ffn_w_up, ffn_conv, ffn_w_down, loss_target, m_ada_w, m_ada_b, m_norm_mix, m_norm_ffn, m_attn_w_in, m_attn_q_norm_a, m_attn_k_norm_a, m_attn_q_norm_b, m_attn_k_norm_b, m_attn_sinks, m_attn_w_out, m_rec_w_in, m_s5_lambda_re, m_s5_lambda_im, m_s5_log_dt, m_s5_b_re, m_s5_b_im, m_s5_c_re, m_s5_c_im, m_s5_d, m_s5_glu_w, m_s5_glu_b, m_dn_conv, m_dn_a_log, m_dn_dt_bias, m_dn_out_norm, m_rec_w_out, m_ffn_w_up, m_ffn_conv, m_ffn_w_down, v_ada_w, v_ada_b, v_norm_mix, v_norm_ffn, v_attn_w_in, v_attn_q_norm_a, v_attn_k_norm_a, v_attn_q_norm_b, v_attn_k_norm_b, v_attn_sinks, v_attn_w_out, v_rec_w_in, v_s5_lambda_re, v_s5_lambda_im, v_s5_log_dt, v_s5_b_re, v_s5_b_im, v_s5_c_re, v_s5_c_im, v_s5_d, v_s5_glu_w, v_s5_glu_b, v_dn_conv, v_dn_a_log, v_dn_dt_bias, v_dn_out_norm, v_rec_w_out, v_ffn_w_up, v_ffn_conv, v_ffn_w_down):
    loc = locals()
    W = {n: loc[n] for n in WEIGHTS}
    M = {n: loc["m_" + n] for n in WEIGHTS}
    V = {n: loc["v_" + n] for n in WEIGHTS}
    _, _, _, me = _mesh_pos()
    L = x.shape[1]
    x0, tgt = x[0], loss_target[0]

    small_in = jnp.concatenate([c.reshape(-1)] + [W[n].reshape(-1) for n, _, _ in SMALL_SHARDED])
    si = all_gather(_pack_rows(small_in, 1024, 8), "gather_small_in").reshape(N_DEV, -1)
    c_all = si[:, :D]
    off = D
    small_full = {}
    for n, ax, shp in SMALL_SHARDED:
        k = _numel(shp) // N_DEV
        small_full[n] = _from_slabs(si[:, off:off + k], ax, shp)
        off += k

    cond_all = jax.nn.silu(c_all)
    modp = jnp.concatenate([matmul([(cond_all, ada_w[l].astype(BF16))], "nn", f"ada{l}") for l in range(2)], axis=0)
    modp_all = all_gather(modp, "gather_mod")
    mods = []
    for l in range(2):
        row = lax.dynamic_index_in_dim(modp_all, l * N_DEV + me, axis=1, keepdims=False)
        mod = row.reshape(1, 6 * D) + ada_b[l].reshape(1, 6 * D)
        mods.append([mod[:, i * D:(i + 1) * D] for i in range(6)])

    sizes = [_numel(shp) // N_DEV for _, _, shp in BIG]
    offs = _offsets(sizes)
    mine = jnp.concatenate([W[n].reshape(-1) for n, _, _ in BIG]).astype(BF16)
    wall = all_gather(_pack_rows(mine, 1024, 16), "gather_weights").reshape(N_DEV, -1)
    full = {n: _from_slabs(wall[:, o:o + k], ax, shp) for (n, ax, shp), o, k in zip(BIG, offs, sizes)}
    w_att_in, w_att_out = full['attn_w_in'][0], full['attn_w_out'][0]
    w_rec_in = jnp.pad(full['rec_w_in'][0], ((0, 0), (0, REC_PAD - REC_IN)))
    w_rec_out, glu_w = full['rec_w_out'][0], full['s5_glu_w'][0]
    w_up = [(full['ffn_w_up'][l][:, :D_FF], full['ffn_w_up'][l][:, D_FF:]) for l in range(2)]
    w_down = [full['ffn_w_down'][l] for l in range(2)]
    ffn_cw = [small_full['ffn_conv'][l] for l in range(2)]
    dn_cw = small_full['dn_conv'][0]
    s5_dskip, glu_b = small_full['s5_d'], small_full['s5_glu_b']
    row = lambda t: t.reshape(1, -1)

    sh1, sc1, g1, sh2, sc2, g2 = mods[0]
    h1 = gate_norm_fwd(x0, None, None, row(norm_mix[0]), sh1, sc1, "l0_norm1")
    wvec, sinkvec = attn_vectors(attn_q_norm_a[0], attn_k_norm_a[0], attn_q_norm_b[0], attn_k_norm_b[0], attn_sinks[0])
    y0, res_att = attention_block_fwd(h1, w_att_in, wvec, sinkvec, w_att_out, "att")
    x1, h2 = gate_norm_fwd(x0, y0, g1, row(norm_ffn[0]), sh2, sc2, "l0_norm2")
    f0, res_f0 = ffn_block_fwd(h2, w_up[0][0], w_up[0][1], ffn_cw[0], w_down[0], "ffn0")
    t1, tc1, tg1, t2, tc2, tg2 = mods[1]
    x2, h3 = gate_norm_fwd(x1, f0, g2, row(norm_mix[1]), t1, tc1, "l1_norm1")
    rin = matmul([(h3, w_rec_in)], "nn", "rec_in")
    s5p, s5p_vjp = jax.vjp(s5_params, s5_lambda_re[0], s5_lambda_im[0], s5_log_dt[0], s5_b_re[0], s5_b_im[0],
                           s5_c_re[0], s5_c_im[0])
    u = rin[:, :S5_W]
    yc, res_s5 = s5_block_fwd(u, s5p, s5_dskip, glu_w, glu_b, "s5")
    yd, res_dn = dn_block_fwd(rin, dn_cw, dn_a_log[0], dn_dt_bias[0], dn_out_norm[0], "dn")
    ycat = jnp.concatenate([yc, yd], axis=1)
    y1 = matmul([(ycat, w_rec_out)], "nn", "rec_out")
    x3, h4 = gate_norm_fwd(x2, y1, tg1, row(norm_ffn[1]), t2, tc2, "l1_norm2")
    f1, res_f1 = ffn_block_fwd(h4, w_up[1][0], w_up[1][1], ffn_cw[1], w_down[1], "ffn1")
    dx4, df1, lsum = final_loss(x3, f1, tg2, tgt, "loss")

    G = {}
    d_tg2 = lsum[8:16].sum(axis=0)
    dh4, gf1 = ffn_block_bwd(df1, res_f1, w_up[1][0], w_up[1][1], ffn_cw[1], w_down[1], "ffn1")
    dx3, dy1, s = gate_norm_bwd(x3, y1, tg1, row(norm_ffn[1]), tc2, dx4, dh4, "l1_dnorm2")
    s = s.reshape(4, 8, D).sum(axis=1)
    d_tg1, d_nffn1, d_t2, d_tc2 = s[0], s[1] * (1.0 + tc2[0]), s[2], s[1] * norm_ffn[1]
    G['rec_w_out'] = matmul([(ycat, dy1)], "tn", "rec_out_dw")[None]
    dycat = matmul([(dy1, w_rec_out)], "nt", "rec_out_dx")
    du, s5cot, gs5 = s5_block_bwd(dycat[:, :S5_W], res_s5, s5p, s5_dskip, glu_w, glu_b, "s5")
    s5g = s5p_vjp(s5cot)
    dqkv, dz, da, dbraw, gdn = dn_block_bwd(dycat[:, S5_W:], res_dn, rin, dn_cw, dn_out_norm[0], "dn")
    drin = jnp.concatenate([du, dqkv, dz, da, dbraw, jnp.zeros((L, REC_PAD - REC_IN), F32)], axis=1)
    G['rec_w_in'] = matmul([(h3, drin)], "tn", "rec_in_dw")[:, :REC_IN][None]
    dh3 = matmul([(drin, w_rec_in)], "nt", "rec_in_dx")
    dx2, df0, s = gate_norm_bwd(x2, f0, g2, row(norm_mix[1]), tc1, dx3, dh3, "l1_dnorm1")
    s = s.reshape(4, 8, D).sum(axis=1)
    d_g2, d_nmix1, d_t1, d_tc1 = s[0], s[1] * (1.0 + tc1[0]), s[2], s[1] * norm_mix[1]
    dh2, gf0 = ffn_block_bwd(df0, res_f0, w_up[0][0], w_up[0][1], ffn_cw[0], w_down[0], "ffn0")
    dx1, dy0, s = gate_norm_bwd(x1, y0, g1, row(norm_ffn[0]), sc2, dx2, dh2, "l0_dnorm2")
    s = s.reshape(4, 8, D).sum(axis=1)
    d_g1, d_nffn0, d_sh2, d_sc2 = s[0], s[1] * (1.0 + sc2[0]), s[2], s[1] * norm_ffn[0]
    dh1, gatt = attention_block_bwd(dy0, res_att, w_att_in, wvec, sinkvec, w_att_out, "att")
    grad_x, s = gate_norm_bwd(x0, None, None, row(norm_mix[0]), sc1, dx1, dh1, "l0_dnorm1")
    s = s.reshape(4, 8, D).sum(axis=1)
    d_nmix0, d_sh1, d_sc1 = s[1] * (1.0 + sc1[0]), s[2], s[1] * norm_mix[0]
    dmod = jnp.stack([jnp.concatenate([d_sh1, d_sc1, d_g1, d_sh2, d_sc2, d_g2]),
                      jnp.concatenate([d_t1, d_tc1, d_tg1, d_t2, d_tc2, d_tg2])])

    G['attn_w_in'], G['attn_w_out'] = gatt['w_in'][None], gatt['w_out'][None]
    G['s5_glu_w'] = gs5['glu_w'][None]
    G['ffn_w_up'] = jnp.stack([gf0['w_up'], gf1['w_up']])
    G['ffn_w_down'] = jnp.stack([gf0['w_down'], gf1['w_down']])
    P = {'ada_b': dmod, 'norm_mix': jnp.stack([d_nmix0, d_nmix1]), 'norm_ffn': jnp.stack([d_nffn0, d_nffn1]),
         'attn_q_norm_a': gatt['q_norm_a'], 'attn_k_norm_a': gatt['k_norm_a'], 'attn_q_norm_b': gatt['q_norm_b'],
         'attn_k_norm_b': gatt['k_norm_b'], 'attn_sinks': gatt['sinks'],
         's5_lambda_re': s5g[0], 's5_lambda_im': s5g[1], 's5_log_dt': s5g[2], 's5_b_re': s5g[3], 's5_b_im': s5g[4],
         's5_c_re': s5g[5], 's5_c_im': s5g[6], 'dn_a_log': gdn['a_log'], 'dn_dt_bias': gdn['dt_bias'],
         'dn_out_norm': gdn['out_norm'],
         's5_d': gs5['dskip'], 's5_glu_b': gs5['glu_b'], 'dn_conv': gdn['conv'],
         'ffn_conv': jnp.stack([gf0['conv'], gf1['conv']])}

    gpack = _pack_rows(jnp.concatenate([_to_slabs(G[n].reshape(shp), ax) for n, ax, shp in BIG], axis=1), 1024, 128)
    grecv = all_to_all(gpack, "exchange_grads")
    pk = lambda d: _pack_rows(jnp.concatenate([d[n].reshape(-1) for n, _, _ in BIG]), 1024, 128)
    bg, bd, bm, bv = [t.reshape(-1) for t in reduce_adamw(grecv, pk(W), pk(M), pk(V), "adamw_big")]

    rep_sizes = [_numel(shp) for _, shp in REPLICATED]
    ss_sizes = [_numel(shp) for _, _, shp in SMALL_SHARDED]
    rep_offs = _offsets(rep_sizes + ss_sizes + [1])
    parts = [P[n].reshape(-1) for n, _ in REPLICATED] + [P[n].reshape(-1) for n, _, _ in SMALL_SHARDED]
    parts.append(lsum[0:8].sum().reshape(1))
    spack = _pack_rows(jnp.concatenate(parts), 1024, 8)
    sall = all_gather(spack, "gather_small_grads")
    n_rest = sum(ss_sizes) + 1
    pk = lambda d: _pack_rows(jnp.concatenate([d[n].reshape(-1) for n, _ in REPLICATED]
                                              + [jnp.zeros((n_rest,), F32)]), 1024, 8)
    sg, sd_, sm, sv = [t.reshape(-1) for t in reduce_adamw(sall, pk(W), pk(M), pk(V), "adamw_small")]
    loss = 0.5 * sg[rep_offs[-1]] / D

    dmod_all = sall.reshape(N_DEV, -1)[:, :2 * 6 * D].reshape(N_DEV, 2, 6 * D)
    dmod_mine = lax.dynamic_slice_in_dim(dmod_all, me * (6 * D // N_DEV), 6 * D // N_DEV, axis=2)
    g_ada = jnp.stack([matmul([(cond_all, dmod_mine[:, l])], "tn", f"ada{l}_dw") for l in range(2)])
    own = [g_ada.reshape(-1)]
    for (n, ax, shp), o in zip(SMALL_SHARDED, rep_offs[len(REPLICATED):]):
        slabs = _to_slabs(sg[o:o + _numel(shp)].reshape(shp), ax)
        own.append(lax.dynamic_index_in_dim(slabs, me, axis=0, keepdims=False))
    own_names = ['ada_w'] + [n for n, _, _ in SMALL_SHARDED]
    pk = lambda d: _pack_rows(jnp.concatenate([d[n].reshape(-1) for n in own_names]), 1024, 128)
    og, od, om, ov = [t.reshape(-1) for t in reduce_adamw(_pack_rows(jnp.concatenate(own), 1024, 128)[None],
                                                          pk(W), pk(M), pk(V), "adamw_own")]

    out = {k: {} for k in ("g", "d", "m", "v")}

    def unpack(names_shapes, bufs):
        o = 0
        for n, shp in names_shapes:
            k = _numel(shp)
            for key, buf in zip(("g", "d", "m", "v"), bufs):
                out[key][n] = buf[o:o + k].reshape(shp)
            o += k

    unpack([(n, W[n].shape) for n, _, _ in BIG], (bg, bd, bm, bv))
    unpack(REPLICATED, (sg, sd_, sm, sv))
    unpack([(n, W[n].shape) for n in own_names], (og, od, om, ov))
    return (loss, grad_x[None], *[out["g"][n] for n in WEIGHTS], *[out["d"][n] for n in WEIGHTS],
            *[out["m"][n] for n in WEIGHTS], *[out["v"][n] for n in WEIGHTS])
```

```python
import functools
import math

import numpy as np
import jax
import jax.numpy as jnp
from jax import lax
from jax.experimental import pallas as pl
from jax.experimental.pallas import tpu as pltpu

F32 = jnp.float32
BF16 = jnp.bfloat16

N_DEV = 8
D = 1024
HD = 64
BLK = 128
ATTN_IN = 2304
CB = ATTN_IN // 128
B_BRANCHES = ((128, 1), (512, 4), (2048, 16))
S5_W = 256
S5_P = 1024
DN_H = 6
DN_DK = 128
DN_C = 64
REC_IN = 3340
REC_PAD = 3456
D_FF = 2816
EPS = 1e-6
ADAM_LR, ADAM_B1, ADAM_B2, ADAM_EPS, ADAM_WD, ADAM_STEP = 0.001, 0.9, 0.999, 1e-8, 0.01, 10
VMEM_LIMIT = 48 * 1024 * 1024

ALIBI = np.asarray(2.0 ** (-8.0 * np.arange(1, 17) / 16), dtype=np.float32)


def _cparams(*sem):
    return pltpu.CompilerParams(dimension_semantics=tuple(sem), vmem_limit_bytes=VMEM_LIMIT)


def _tile(n, target):
    if n <= target:
        return n
    best = None
    for t in range(128, target + 1, 128):
        if n % t == 0:
            best = t
    assert best is not None, (n, target)
    return best


def _rtile(n, target):
    if n <= target:
        return n
    best = None
    for t in range(8, target + 1, 8):
        if n % t == 0:
            best = t
    assert best is not None, (n, target)
    return best


def _fold8(x):
    r, c = x.shape
    return x.reshape(r // 8, 8, c).sum(axis=0)


def _sigmoid(x):
    return 1.0 / (1.0 + jnp.exp(-x))


_DIMS = {"nn": (((1,), (0,)), ((), ())), "nt": (((1,), (1,)), ((), ())), "tn": (((0,), (0,)), ((), ()))}


MM_FULL_K = 3584


def matmul(pairs, mode, name, out_dtype=F32, tm=512, tn=1536, tk=1024):
    a0, b0 = pairs[0]
    if mode == "nn":
        (M, K), N = a0.shape, b0.shape[1]
    elif mode == "nt":
        (M, K), N = a0.shape, b0.shape[0]
    else:
        (K, M), N = a0.shape, b0.shape[1]
    tm = _rtile(M, tm) if M % 128 else _tile(M, tm)
    tn = _tile(N, tn)
    tk = K if K <= MM_FULL_K else _tile(K, tk)
    nk = K // tk
    npair = len(pairs)
    dims = _DIMS[mode]

    def body(*refs):
        o_ref = refs[2 * npair]
        tot = None
        for p in range(npair):
            part = lax.dot_general(refs[2 * p][...].astype(BF16), refs[2 * p + 1][...].astype(BF16),
                                   dims, preferred_element_type=F32)
            tot = part if tot is None else tot + part
        if nk == 1:
            o_ref[...] = tot.astype(o_ref.dtype)
            return
        acc_ref = refs[2 * npair + 1]
        k = pl.program_id(2)

        @pl.when(k == 0)
        def _():
            acc_ref[...] = tot

        @pl.when(k > 0)
        def _():
            acc_ref[...] += tot

        @pl.when(k == nk - 1)
        def _():
            o_ref[...] = acc_ref[...].astype(o_ref.dtype)

    if mode == "nn":
        a_spec = pl.BlockSpec((tm, tk), lambda i, j, k: (i, k))
        b_spec = pl.BlockSpec((tk, tn), lambda i, j, k: (k, j))
    elif mode == "nt":
        a_spec = pl.BlockSpec((tm, tk), lambda i, j, k: (i, k))
        b_spec = pl.BlockSpec((tn, tk), lambda i, j, k: (j, k))
    else:
        a_spec = pl.BlockSpec((tk, tm), lambda i, j, k: (k, i))
        b_spec = pl.BlockSpec((tk, tn), lambda i, j, k: (k, j))
    flat = [t for pr in pairs for t in pr]
    return pl.pallas_call(
        body, name=name, grid=(M // tm, N // tn, nk),
        in_specs=[a_spec, b_spec] * npair,
        out_specs=pl.BlockSpec((tm, tn), lambda i, j, k: (i, j)),
        out_shape=jax.ShapeDtypeStruct((M, N), out_dtype),
        scratch_shapes=[pltpu.VMEM((tm, tn), F32)] if nk > 1 else [],
        compiler_params=_cparams("parallel", "parallel", "arbitrary"),
    )(*flat)


def gate_norm_fwd(x, y, gate, nw, sh, sc, name):
    L, C = x.shape
    tl = _rtile(L, 512)
    has_gate = y is not None

    def body(*refs):
        if has_gate:
            x_ref, y_ref, g_ref, nw_ref, sh_ref, sc_ref, xn_ref, h_ref = refs
            xn = x_ref[...] + g_ref[...] * y_ref[...]
            xn_ref[...] = xn
        else:
            x_ref, nw_ref, sh_ref, sc_ref, h_ref = refs
            xn = x_ref[...]
        r = lax.rsqrt(jnp.mean(xn * xn, axis=-1, keepdims=True) + EPS)
        h = (xn * r * nw_ref[...]) * (1.0 + sc_ref[...]) + sh_ref[...]
        h_ref[...] = h.astype(BF16)

    big = pl.BlockSpec((tl, C), lambda i: (i, 0))
    vec = pl.BlockSpec((1, C), lambda i: (0, 0))
    if has_gate:
        ins, in_specs = (x, y, gate, nw, sh, sc), [big, big, vec, vec, vec, vec]
        out_shape = (jax.ShapeDtypeStruct((L, C), F32), jax.ShapeDtypeStruct((L, C), BF16))
        out_specs = (big, big)
    else:
        ins, in_specs = (x, nw, sh, sc), [big, vec, vec, vec]
        out_shape = jax.ShapeDtypeStruct((L, C), BF16)
        out_specs = big
    return pl.pallas_call(body, name=name, grid=(L // tl,), in_specs=in_specs, out_specs=out_specs,
                          out_shape=out_shape, compiler_params=_cparams("parallel"))(*ins)


def gate_norm_bwd(xn, y, gate, nw, sc, dxn_direct, dh, name):
    L, C = xn.shape
    tl = _rtile(L, 256)
    has_gate = y is not None
    has_direct = dxn_direct is not None

    def body(*refs):
        refs = list(refs)
        xn_ref = refs.pop(0)
        y_ref = refs.pop(0) if has_gate else None
        g_ref = refs.pop(0) if has_gate else None
        nw_ref = refs.pop(0)
        sc_ref = refs.pop(0)
        dd_ref = refs.pop(0) if has_direct else None
        dh_ref = refs.pop(0)
        dxn_ref = refs.pop(0)
        dy_ref = refs.pop(0) if has_gate else None
        sums_ref = refs.pop(0)

        @pl.when(pl.program_id(0) == 0)
        def _():
            sums_ref[...] = jnp.zeros_like(sums_ref)

        xv = xn_ref[...]
        dh_v = dh_ref[...]
        r = lax.rsqrt(jnp.mean(xv * xv, axis=-1, keepdims=True) + EPS)
        n = xv * r
        a = nw_ref[...] * (1.0 + sc_ref[...])
        dn = dh_v * a
        dx = r * (dn - n * jnp.mean(dn * n, axis=-1, keepdims=True))
        if has_direct:
            dx = dx + dd_ref[...]
        dxn_ref[...] = dx
        sums_ref[8:16, :] += _fold8(dh_v * n)
        sums_ref[16:24, :] += _fold8(dh_v)
        if has_gate:
            dy_ref[...] = (dx * g_ref[...]).astype(BF16)
            sums_ref[0:8, :] += _fold8(dx * y_ref[...])

    big = pl.BlockSpec((tl, C), lambda i: (i, 0))
    vec = pl.BlockSpec((1, C), lambda i: (0, 0))
    ins, in_specs = [xn], [big]
    if has_gate:
        ins += [y, gate]
        in_specs += [big, vec]
    ins += [nw, sc]
    in_specs += [vec, vec]
    if has_direct:
        ins.append(dxn_direct)
        in_specs.append(big)
    ins.append(dh)
    in_specs.append(big)
    out_shape = [jax.ShapeDtypeStruct((L, C), F32)]
    out_specs = [big]
    if has_gate:
        out_shape.append(jax.ShapeDtypeStruct((L, C), BF16))
        out_specs.append(big)
    out_shape.append(jax.ShapeDtypeStruct((32, C), F32))
    out_specs.append(pl.BlockSpec((32, C), lambda i: (0, 0)))
    return pl.pallas_call(body, name=name, grid=(L // tl,), in_specs=in_specs, out_specs=tuple(out_specs),
                          out_shape=tuple(out_shape), compiler_params=_cparams("arbitrary"))(*ins)


def final_loss(x, f, gate, target, name):
    L, C = x.shape
    tl = _rtile(L, 256)

    def body(x_ref, f_ref, g_ref, t_ref, dy_ref, df_ref, sums_ref):
        @pl.when(pl.program_id(0) == 0)
        def _():
            sums_ref[...] = jnp.zeros_like(sums_ref)

        fv = f_ref[...]
        err = x_ref[...] + g_ref[...] * fv - t_ref[...]
        dy = err * (1.0 / C)
        dy_ref[...] = dy
        df_ref[...] = (dy * g_ref[...]).astype(BF16)
        sums_ref[0:8, :] += _fold8(err * err)
        sums_ref[8:16, :] += _fold8(dy * fv)

    big = pl.BlockSpec((tl, C), lambda i: (i, 0))
    vec = pl.BlockSpec((1, C), lambda i: (0, 0))
    return pl.pallas_call(
        body, name=name, grid=(L // tl,), in_specs=[big, big, vec, big],
        out_specs=(big, big, pl.BlockSpec((16, C), lambda i: (0, 0))),
        out_shape=(jax.ShapeDtypeStruct((L, C), F32), jax.ShapeDtypeStruct((L, C), BF16),
                   jax.ShapeDtypeStruct((16, C), F32)),
        compiler_params=_cparams("arbitrary"))(x, f, gate, target)


def _seg_ones(seg):
    r = lax.broadcasted_iota(jnp.int32, (128, 128), 0) // seg
    c = lax.broadcasted_iota(jnp.int32, (128, 128), 1) // seg
    return (r == c).astype(BF16)


def _segsum(t, ones):
    hi = t.astype(BF16)
    lo = (t - hi.astype(F32)).astype(BF16)
    return (jnp.dot(hi, ones, preferred_element_type=F32) + jnp.dot(lo, ones, preferred_element_type=F32))


_NORMED_TILES = tuple(list(range(0, 5)) + list(range(6, 14)))


def qknorm_fwd(qkv, wvec, name):
    L, C = qkv.shape
    tl = _rtile(L, 256)

    def body(x_ref, w_ref, o_ref):
        ones = _seg_ones(HD)
        for t in range(CB):
            cs = slice(t * 128, (t + 1) * 128)
            x = x_ref[:, cs]
            if t in _NORMED_TILES:
                ms = _segsum(x * x, ones) * (1.0 / HD)
                x = x * lax.rsqrt(ms + EPS) * w_ref[:, cs]
            o_ref[:, cs] = x.astype(BF16)

    return pl.pallas_call(
        body, name=name, grid=(L // tl,),
        in_specs=[pl.BlockSpec((tl, C), lambda i: (i, 0)), pl.BlockSpec((1, C), lambda i: (0, 0))],
        out_specs=pl.BlockSpec((tl, C), lambda i: (i, 0)),
        out_shape=jax.ShapeDtypeStruct((L, C), BF16), compiler_params=_cparams("parallel"))(qkv, wvec)


def qknorm_bwd(qkv, wvec, dy, name):
    L, C = qkv.shape
    tl = _rtile(L, 256)

    def body(x_ref, w_ref, dy_ref, dx_ref, sums_ref):
        @pl.when(pl.program_id(0) == 0)
        def _():
            sums_ref[...] = jnp.zeros_like(sums_ref)

        ones = _seg_ones(HD)
        for t in range(CB):
            cs = slice(t * 128, (t + 1) * 128)
            d = dy_ref[:, cs]
            if t in _NORMED_TILES:
                x = x_ref[:, cs]
                r = lax.rsqrt(_segsum(x * x, ones) * (1.0 / HD) + EPS)
                n = x * r
                dn = d * w_ref[:, cs]
                dx_ref[:, cs] = (r * (dn - n * (_segsum(dn * n, ones) * (1.0 / HD)))).astype(BF16)
                sums_ref[:, cs] += _fold8(d * n)
            else:
                dx_ref[:, cs] = d.astype(BF16)

    big = pl.BlockSpec((tl, C), lambda i: (i, 0))
    return pl.pallas_call(
        body, name=name, grid=(L // tl,),
        in_specs=[big, pl.BlockSpec((1, C), lambda i: (0, 0)), big],
        out_specs=(big, pl.BlockSpec((8, C), lambda i: (0, 0))),
        out_shape=(jax.ShapeDtypeStruct((L, C), BF16), jax.ShapeDtypeStruct((8, C), F32)),
        compiler_params=_cparams("arbitrary"))(qkv, wvec, dy)


def _attn_scores(q, kw, n, slope, step, maxdist):
    s = lax.dot_general(q, kw, (((1,), (1,)), ((), ())), preferred_element_type=F32) * (HD ** -0.5)
    qi = lax.broadcasted_iota(jnp.int32, (BLK, 2 * BLK), 0)
    sj = lax.broadcasted_iota(jnp.int32, (BLK, 2 * BLK), 1)
    dist = BLK + qi - sj
    valid = (dist >= 0) & (dist <= maxdist) & ((n > 0) | (sj >= BLK))
    bias = (-slope) * (step * dist).astype(F32)
    return jnp.where(valid, s + bias, -jnp.inf), valid


def attn_fwd(X, d, q_off, k_off, v_off, gqa, slope0, maxdist, name):
    Ls = X.shape[0]
    nb = Ls // BLK
    slopes = jnp.asarray(ALIBI)

    def body(sl_ref, q_ref, kp_ref, kc_ref, vp_ref, vc_ref, o_ref, lse_ref):
        hp, n = pl.program_id(1), pl.program_id(2)
        for e in range(2):
            slope = sl_ref[slope0 + 2 * hp + e]
            if gqa:
                ksel = lambda ref: jnp.where(hp >= 2, ref[:, 64:128], ref[:, 0:64])
            else:
                ksel = lambda ref: ref[:, e * 64:(e + 1) * 64]
            q = q_ref[:, e * 64:(e + 1) * 64]
            kw = jnp.concatenate([ksel(kp_ref), ksel(kc_ref)], axis=0)
            vw = jnp.concatenate([ksel(vp_ref), ksel(vc_ref)], axis=0)
            s, _ = _attn_scores(q, kw, n, slope, d, maxdist)
            m = jnp.max(s, axis=-1, keepdims=True)
            p = jnp.exp(s - m)
            l = jnp.sum(p, axis=-1, keepdims=True)
            o = jnp.dot(p.astype(BF16), vw, preferred_element_type=F32) / l
            o_ref[:, e * 64:(e + 1) * 64] = o
            lse_ref[:, e * 64:(e + 1) * 64] = jnp.broadcast_to(m + jnp.log(l), (BLK, HD))

    kcol = (lambda r, hp: r * CB + k_off) if gqa else (lambda r, hp: r * CB + k_off + hp)
    vcol = (lambda r, hp: r * CB + v_off) if gqa else (lambda r, hp: r * CB + v_off + hp)
    blk = (BLK, 128)
    in_specs = [
        pl.BlockSpec(memory_space=pltpu.SMEM),
        pl.BlockSpec(blk, lambda r, hp, n: (n, r * CB + q_off + hp)),
        pl.BlockSpec(blk, lambda r, hp, n: (jnp.maximum(n - 1, 0), kcol(r, hp))),
        pl.BlockSpec(blk, lambda r, hp, n: (n, kcol(r, hp))),
        pl.BlockSpec(blk, lambda r, hp, n: (jnp.maximum(n - 1, 0), vcol(r, hp))),
        pl.BlockSpec(blk, lambda r, hp, n: (n, vcol(r, hp))),
    ]
    out_spec = pl.BlockSpec(blk, lambda r, hp, n: (n, r * 4 + hp))
    out = jax.ShapeDtypeStruct((Ls, d * 512), F32)
    return pl.pallas_call(
        body, name=name, grid=(d, 4, nb), in_specs=in_specs, out_specs=(out_spec, out_spec),
        out_shape=(out, out), compiler_params=_cparams("parallel", "parallel", "arbitrary"),
    )(slopes, X, X, X, X, X)


def attn_bwd(X, o, lse, do, dlse, d, q_off, k_off, v_off, gqa, slope0, maxdist, name):
    Ls = X.shape[0]
    nb = Ls // BLK
    slopes = jnp.asarray(ALIBI)

    def body(sl_ref, q_ref, kp_ref, kc_ref, vp_ref, vc_ref, o_ref, lse_ref, do_ref, dlse_ref,
             dq_ref, dk_ref, dv_ref, ck_ref, cv_ref):
        hp, n = pl.program_id(1), pl.program_id(2)

        @pl.when(n == 0)
        def _():
            ck_ref[...] = jnp.zeros_like(ck_ref)
            cv_ref[...] = jnp.zeros_like(cv_ref)

        @pl.when(n < nb)
        def _():
            for e in range(2):
                cs = slice(e * 64, (e + 1) * 64)
                slope = sl_ref[slope0 + 2 * hp + e]
                if gqa:
                    ksel = lambda ref: jnp.where(hp >= 2, ref[:, 64:128], ref[:, 0:64])
                else:
                    ksel = lambda ref: ref[:, cs]
                q = q_ref[:, cs]
                kw = jnp.concatenate([ksel(kp_ref), ksel(kc_ref)], axis=0)
                vw = jnp.concatenate([ksel(vp_ref), ksel(vc_ref)], axis=0)
                s, valid = _attn_scores(q, kw, n, slope, d, maxdist)
                p = jnp.where(valid, jnp.exp(s - lse_ref[:, e * 64:e * 64 + 1]), 0.0)
                dov = do_ref[:, cs]
                delta = jnp.sum(dov * o_ref[:, cs], axis=-1, keepdims=True)
                dob = dov.astype(BF16)
                dp = lax.dot_general(dob, vw, (((1,), (1,)), ((), ())), preferred_element_type=F32)
                ds = p * (dp - delta + dlse_ref[:, e * 64:e * 64 + 1])
                dsb = ds.astype(BF16)
                dq_ref[:, cs] = jnp.dot(dsb, kw, preferred_element_type=F32) * (HD ** -0.5)
                dkw = lax.dot_general(dsb, q, (((0,), (0,)), ((), ())), preferred_element_type=F32) * (HD ** -0.5)
                dvw = lax.dot_general(p.astype(BF16), dob, (((0,), (0,)), ((), ())), preferred_element_type=F32)
                dk_ref[:, cs] = ck_ref[:, cs] + dkw[0:BLK]
                dv_ref[:, cs] = cv_ref[:, cs] + dvw[0:BLK]
                ck_ref[:, cs] = dkw[BLK:]
                cv_ref[:, cs] = dvw[BLK:]

        @pl.when(n == nb)
        def _():
            dk_ref[...] = ck_ref[...]
            dv_ref[...] = cv_ref[...]

    kcol = (lambda r, hp: r * CB + k_off) if gqa else (lambda r, hp: r * CB + k_off + hp)
    vcol = (lambda r, hp: r * CB + v_off) if gqa else (lambda r, hp: r * CB + v_off + hp)
    blk = (BLK, 128)
    cur = lambda n: jnp.minimum(n, nb - 1)
    prev = lambda n: jnp.maximum(jnp.minimum(n, nb - 1) - 1, 0)
    ospec = pl.BlockSpec(blk, lambda r, hp, n: (cur(n), r * 4 + hp))
    in_specs = [
        pl.BlockSpec(memory_space=pltpu.SMEM),
        pl.BlockSpec(blk, lambda r, hp, n: (cur(n), r * CB + q_off + hp)),
        pl.BlockSpec(blk, lambda r, hp, n: (prev(n), kcol(r, hp))),
        pl.BlockSpec(blk, lambda r, hp, n: (cur(n), kcol(r, hp))),
        pl.BlockSpec(blk, lambda r, hp, n: (prev(n), vcol(r, hp))),
        pl.BlockSpec(blk, lambda r, hp, n: (cur(n), vcol(r, hp))),
        ospec, ospec, ospec, ospec,
    ]
    shifted = pl.BlockSpec(blk, lambda r, hp, n: (jnp.maximum(n - 1, 0), r * 4 + hp))
    out = jax.ShapeDtypeStruct((Ls, d * 512), F32)
    return pl.pallas_call(
        body, name=name, grid=(d, 4, nb + 1), in_specs=in_specs, out_specs=(ospec, shifted, shifted),
        out_shape=(out, out, out),
        scratch_shapes=[pltpu.VMEM((BLK, 128), F32), pltpu.VMEM((BLK, 128), F32)],
        compiler_params=_cparams("parallel", "parallel", "arbitrary"),
    )(slopes, X, X, X, X, X, o, lse, do, dlse)


def attn_merge_fwd(oa, la, sink, obs, lbs, name):
    L = oa.shape[0]
    tl = _rtile(L, 256)

    def body(oa_ref, la_ref, sk_ref, o1, o2, o3, l1, l2, l3, m_ref):
        m_ref[:, 0:512] = (oa_ref[...] * _sigmoid(la_ref[...] - sk_ref[...])).astype(BF16)
        a, b, c = l1[...], l2[...], l3[...]
        mx = jnp.maximum(jnp.maximum(a, b), c)
        ea, eb, ec = jnp.exp(a - mx), jnp.exp(b - mx), jnp.exp(c - mx)
        inv = 1.0 / (ea + eb + ec)
        m_ref[:, 512:1024] = ((ea * inv) * o1[...] + (eb * inv) * o2[...] + (ec * inv) * o3[...]).astype(BF16)

    big = pl.BlockSpec((tl, 512), lambda i: (i, 0))
    return pl.pallas_call(
        body, name=name, grid=(L // tl,),
        in_specs=[big, big, pl.BlockSpec((1, 512), lambda i: (0, 0))] + [big] * 6,
        out_specs=pl.BlockSpec((tl, 1024), lambda i: (i, 0)),
        out_shape=jax.ShapeDtypeStruct((L, 1024), BF16), compiler_params=_cparams("parallel"),
    )(oa, la, sink, *obs, *lbs)


def attn_merge_bwd(dm, oa, la, sink, obs, lbs, name):
    L = oa.shape[0]
    tl = _rtile(L, 256)

    def body(dm_ref, oa_ref, la_ref, sk_ref, o1, o2, o3, l1, l2, l3,
             doa_ref, dla_ref, d1, d2, d3, g1, g2, g3, sums_ref):
        @pl.when(pl.program_id(0) == 0)
        def _():
            sums_ref[...] = jnp.zeros_like(sums_ref)

        ones = _seg_ones(HD)
        for t in range(4):
            cs = slice(t * 128, (t + 1) * 128)
            dma = dm_ref[:, cs]
            keep = _sigmoid(la_ref[:, cs] - sk_ref[:, cs])
            doa_ref[:, cs] = dma * keep
            tt = dma * oa_ref[:, cs] * keep * (1.0 - keep)
            dla_ref[:, cs] = _segsum(tt, ones)
            sums_ref[:, cs] += _fold8(-tt)
            dmb = dm_ref[:, 512 + t * 128:512 + (t + 1) * 128]
            a, b, c = l1[:, cs], l2[:, cs], l3[:, cs]
            mx = jnp.maximum(jnp.maximum(a, b), c)
            ea, eb, ec = jnp.exp(a - mx), jnp.exp(b - mx), jnp.exp(c - mx)
            inv = 1.0 / (ea + eb + ec)
            wa, wb, wc = ea * inv, eb * inv, ec * inv
            d1[:, cs] = wa * dmb
            d2[:, cs] = wb * dmb
            d3[:, cs] = wc * dmb
            sa = _segsum(dmb * o1[:, cs], ones)
            sb = _segsum(dmb * o2[:, cs], ones)
            sc_ = _segsum(dmb * o3[:, cs], ones)
            mean = wa * sa + wb * sb + wc * sc_
            g1[:, cs] = wa * (sa - mean)
            g2[:, cs] = wb * (sb - mean)
            g3[:, cs] = wc * (sc_ - mean)

    big = pl.BlockSpec((tl, 512), lambda i: (i, 0))
    o512 = jax.ShapeDtypeStruct((L, 512), F32)
    return pl.pallas_call(
        body, name=name, grid=(L // tl,),
        in_specs=[pl.BlockSpec((tl, 1024), lambda i: (i, 0)), big, big,
                  pl.BlockSpec((1, 512), lambda i: (0, 0))] + [big] * 6,
        out_specs=tuple([big] * 8 + [pl.BlockSpec((8, 512), lambda i: (0, 0))]),
        out_shape=tuple([o512] * 8 + [jax.ShapeDtypeStruct((8, 512), F32)]),
        compiler_params=_cparams("arbitrary"),
    )(dm, oa, la, sink, *obs, *lbs)


def _shift_down(x, halo, k, first):
    tl = x.shape[0]
    rows = lax.broadcasted_iota(jnp.int32, x.shape, 0)
    out = pltpu.roll(x, k, axis=0)
    for j in range(k):
        hrow = jnp.where(first, 0.0, halo[8 - k + j:8 - k + j + 1, :])
        out = jnp.where(rows == j, hrow, out)
    return out


def _shift_up(x, nxt, k):
    tl = x.shape[0]
    rows = lax.broadcasted_iota(jnp.int32, x.shape, 0)
    out = pltpu.roll(x, tl - k, axis=0)
    for j in range(k):
        out = jnp.where(rows == tl - k + j, nxt[j:j + 1, :], out)
    return out


def _silu(x):
    return x * _sigmoid(x)


def _dsilu(x):
    s = _sigmoid(x)
    return s * (1.0 + x * (1.0 - s))


def ffn_act_fwd(ua, ub, cw, name):
    L, F = ua.shape
    tl = _rtile(L, 256)
    tc = _tile(F, 1408)
    hb = tl // 8

    def body(ua_ref, uah_ref, ub_ref, ubh_ref, wa_ref, wb_ref, o_ref):
        first = pl.program_id(1) == 0

        def conv(x_ref, h_ref, w_ref):
            x = x_ref[...]
            h = h_ref[...]
            return (w_ref[2:3, :] * x + w_ref[1:2, :] * _shift_down(x, h, 1, first)
                    + w_ref[0:1, :] * _shift_down(x, h, 2, first))

        a = conv(ua_ref, uah_ref, wa_ref)
        b = conv(ub_ref, ubh_ref, wb_ref)
        o_ref[...] = (_silu(a) * b).astype(BF16)

    main = pl.BlockSpec((tl, tc), lambda j, i: (i, j))
    halo = pl.BlockSpec((8, tc), lambda j, i: (jnp.maximum(i * hb - 1, 0), j))
    wa = pl.BlockSpec((3, tc), lambda j, i: (0, j))
    wb = pl.BlockSpec((3, tc), lambda j, i: (0, j + F // tc))
    return pl.pallas_call(
        body, name=name, grid=(F // tc, L // tl), in_specs=[main, halo, main, halo, wa, wb],
        out_specs=main, out_shape=jax.ShapeDtypeStruct((L, F), BF16),
        compiler_params=_cparams("parallel", "parallel"))(ua, ua, ub, ub, cw, cw)


def ffn_act_bwd(ua, ub, cw, dact, name):
    L, F = ua.shape
    tl = _rtile(L, 256)
    tc = _tile(F, 1408)
    hb = tl // 8
    nrt = L // tl

    def body(ua_ref, uah_ref, ub_ref, ubh_ref, wa_ref, wb_ref, da_ref, dua_ref, dub_ref, sums_ref, ca_ref, cb_ref):
        i = pl.program_id(1)
        first = i == nrt - 1

        @pl.when(i == 0)
        def _():
            sums_ref[...] = jnp.zeros_like(sums_ref)
            ca_ref[...] = jnp.zeros_like(ca_ref)
            cb_ref[...] = jnp.zeros_like(cb_ref)

        def taps(x_ref, h_ref):
            x = x_ref[...]
            h = h_ref[...]
            return x, _shift_down(x, h, 1, first), _shift_down(x, h, 2, first)

        a0, a1, a2 = taps(ua_ref, uah_ref)
        b0, b1, b2 = taps(ub_ref, ubh_ref)
        a = wa_ref[2:3, :] * a0 + wa_ref[1:2, :] * a1 + wa_ref[0:1, :] * a2
        b = wb_ref[2:3, :] * b0 + wb_ref[1:2, :] * b1 + wb_ref[0:1, :] * b2
        dact_v = da_ref[...]
        dya = dact_v * b * _dsilu(a)
        dyb = dact_v * _silu(a)
        for (dy, w_ref, c_ref, d_ref, xs, base) in ((dya, wa_ref, ca_ref, dua_ref, (a2, a1, a0), 0),
                                                     (dyb, wb_ref, cb_ref, dub_ref, (b2, b1, b0), 24)):
            nxt = c_ref[...]
            d_ref[...] = (w_ref[2:3, :] * dy + w_ref[1:2, :] * _shift_up(dy, nxt, 1)
                          + w_ref[0:1, :] * _shift_up(dy, nxt, 2)).astype(BF16)
            c_ref[...] = dy[0:8, :]
            for j in range(3):
                sums_ref[base + 8 * j:base + 8 * j + 8, :] += _fold8(dy * xs[j])

    rev = lambda i: nrt - 1 - i
    main = pl.BlockSpec((tl, tc), lambda j, i: (rev(i), j))
    halo = pl.BlockSpec((8, tc), lambda j, i: (jnp.maximum(rev(i) * hb - 1, 0), j))
    wa = pl.BlockSpec((3, tc), lambda j, i: (0, j))
    wb = pl.BlockSpec((3, tc), lambda j, i: (0, j + F // tc))
    ob = jax.ShapeDtypeStruct((L, F), BF16)
    return pl.pallas_call(
        body, name=name, grid=(F // tc, nrt), in_specs=[main, halo, main, halo, wa, wb, main],
        out_specs=(main, main, pl.BlockSpec((48, tc), lambda j, i: (0, j))),
        out_shape=(ob, ob, jax.ShapeDtypeStruct((48, F), F32)),
        scratch_shapes=[pltpu.VMEM((8, tc), F32), pltpu.VMEM((8, tc), F32)],
        compiler_params=_cparams("parallel", "arbitrary"))(ua, ua, ub, ub, cw, cw, dact)


def attn_vectors(qna, kna, qnb, knb, sinks):
    ones = jnp.ones((128,), F32)
    wvec = jnp.concatenate([jnp.tile(qna, 8), jnp.tile(kna, 2), ones, jnp.tile(qnb, 8), jnp.tile(knb, 8),
                            jnp.tile(ones, 4)]).reshape(1, ATTN_IN)
    return wvec, jnp.repeat(sinks, HD).reshape(1, 512)


def attention_block_fwd(h, w_in, wvec, sinkvec, w_out, tag):
    L = h.shape[0]
    qkv = matmul([(h, w_in)], "nn", tag + "_qkv")
    X = qknorm_fwd(qkv, wvec, tag + "_qknorm")
    oa, la = attn_fwd(X, 1, 0, 4, 5, True, 0, BLK - 1, tag + "_swa")
    obs, lbs = [], []
    for window, d in B_BRANCHES:
        o, l = attn_fwd(X.reshape(L // d, d * ATTN_IN), d, 6, 10, 14, False, 8, window // d, tag + f"_dil{d}")
        obs.append(o.reshape(L, 512))
        lbs.append(l.reshape(L, 512))
    m = attn_merge_fwd(oa, la, sinkvec, obs, lbs, tag + "_merge")
    y = matmul([(m, w_out)], "nn", tag + "_out")
    return y, (h, qkv, X, oa, la, obs, lbs, m)


def attention_block_bwd(dy, res, w_in, wvec, sinkvec, w_out, tag):
    h, qkv, X, oa, la, obs, lbs, m = res
    L = h.shape[0]
    g_w_out = matmul([(m, dy)], "tn", tag + "_dwout")
    dm = matmul([(dy, w_out)], "nt", tag + "_dm")
    doa, dla, d1, d2, d3, g1, g2, g3, sinksums = attn_merge_bwd(dm, oa, la, sinkvec, obs, lbs, tag + "_dmerge")
    dqa, dka, dva = attn_bwd(X, oa, la, doa, dla, 1, 0, 4, 5, True, 0, BLK - 1, tag + "_dswa")
    dqb = dkb = dvb = None
    for (window, d), o, l, do, dl in zip(B_BRANCHES, obs, lbs, (d1, d2, d3), (g1, g2, g3)):
        shp = (L // d, d * 512)
        dq, dk, dv = attn_bwd(X.reshape(L // d, d * ATTN_IN), o.reshape(shp), l.reshape(shp), do.reshape(shp),
                              dl.reshape(shp), d, 6, 10, 14, False, 8, window // d, tag + f"_ddil{d}")
        dq, dk, dv = dq.reshape(L, 512), dk.reshape(L, 512), dv.reshape(L, 512)
        dqb, dkb, dvb = (dq, dk, dv) if dqb is None else (dqb + dq, dkb + dk, dvb + dv)
    fold = lambda t: t.reshape(L, 2, 4, HD).sum(axis=2).reshape(L, 128)
    dX = jnp.concatenate([dqa, fold(dka), fold(dva), dqb, dkb, dvb], axis=1)
    dqkv, wsums = qknorm_bwd(qkv, wvec, dX, tag + "_dqknorm")
    g_w_in = matmul([(h, dqkv)], "tn", tag + "_dwin")
    dh = matmul([(dqkv, w_in)], "nt", tag + "_dh")
    ws = wsums.sum(axis=0)
    grads = dict(
        w_in=g_w_in, w_out=g_w_out,
        q_norm_a=ws[0:512].reshape(8, HD).sum(axis=0), k_norm_a=ws[512:640].reshape(2, HD).sum(axis=0),
        q_norm_b=ws[768:1280].reshape(8, HD).sum(axis=0), k_norm_b=ws[1280:1792].reshape(8, HD).sum(axis=0),
        sinks=sinksums.sum(axis=0).reshape(8, HD).sum(axis=1))
    return dh, grads


def ffn_block_fwd(h, w_up_a, w_up_b, cw, w_down, tag):
    ua = matmul([(h, w_up_a)], "nn", tag + "_upa")
    ub = matmul([(h, w_up_b)], "nn", tag + "_upb")
    act = ffn_act_fwd(ua, ub, cw, tag + "_act")
    f = matmul([(act, w_down)], "nn", tag + "_down")
    return f, (h, ua, ub, act)


def ffn_block_bwd(df, res, w_up_a, w_up_b, cw, w_down, tag):
    h, ua, ub, act = res
    g_down = matmul([(act, df)], "tn", tag + "_dwdown")
    dact = matmul([(df, w_down)], "nt", tag + "_dact")
    dua, dub, sums = ffn_act_bwd(ua, ub, cw, dact, tag + "_dactk")
    g_up = jnp.concatenate([matmul([(h, dua)], "tn", tag + "_dwupa"), matmul([(h, dub)], "tn", tag + "_dwupb")], axis=1)
    dh = matmul([(dua, w_up_a), (dub, w_up_b)], "nt", tag + "_dh")
    s = sums.reshape(2, 3, 8, D_FF).sum(axis=2)
    g_conv = jnp.concatenate([s[0], s[1]], axis=1)
    return dh, dict(w_up=g_up, conv=g_conv, w_down=g_down)


def s5_params(lam_re, lam_im, log_dt, b_re, b_im, c_re, c_im):
    dt = jnp.exp(log_dt)[:, None]
    mag, ang = jnp.exp(lam_re * dt), lam_im * dt
    a_re, a_im = mag * jnp.cos(ang), mag * jnp.sin(ang)
    nr, ni = a_re - 1.0, a_im
    den = lam_re * lam_re + lam_im * lam_im
    f_re = (nr * lam_re + ni * lam_im) / den
    f_im = (ni * lam_re - nr * lam_im) / den
    eye = jnp.eye(16, dtype=F32)[:, None, :, None]
    bd = lambda b: (eye * jnp.transpose(b, (0, 2, 1))[:, :, None, :]).reshape(S5_W, S5_P)
    cd = lambda c: (eye * jnp.transpose(c, (0, 2, 1))[:, :, None, :]).reshape(S5_P, S5_W)
    flat = lambda t: t.reshape(1, S5_P)
    return flat(a_re), flat(a_im), flat(f_re), flat(f_im), bd(b_re), bd(b_im), cd(c_re), cd(c_im)


def _scan_tables(a_re, a_im, reverse):
    pows = [(a_re, a_im)]
    for _ in range(7):
        pr, pi = pows[-1]
        pows.append((pr * a_re - pi * a_im, pr * a_im + pi * a_re))
    order = list(range(7, -1, -1)) if reverse else list(range(8))
    z = jnp.zeros_like(a_re)
    rows = [pows[0][0], pows[0][1], pows[1][0], pows[1][1], pows[3][0], pows[3][1], z, z]
    rows += [pows[k][0] for k in order] + [pows[k][1] for k in order]
    return jnp.concatenate(rows, axis=0)


def _block_scan(er, ei, tab_ref, cr, ci, reverse):
    rows = lax.broadcasted_iota(jnp.int32, er.shape, 0)
    for idx, s in enumerate((1, 2, 4)):
        if reverse:
            sr, si, keep = pltpu.roll(er, 8 - s, axis=0), pltpu.roll(ei, 8 - s, axis=0), rows < 8 - s
        else:
            sr, si, keep = pltpu.roll(er, s, axis=0), pltpu.roll(ei, s, axis=0), rows >= s
        sr, si = jnp.where(keep, sr, 0.0), jnp.where(keep, si, 0.0)
        ar, ai = tab_ref[2 * idx:2 * idx + 1, :], tab_ref[2 * idx + 1:2 * idx + 2, :]
        er, ei = er + ar * sr - ai * si, ei + ar * si + ai * sr
    pr, pi_ = tab_ref[8:16, :], tab_ref[16:24, :]
    er, ei = er + pr * cr - pi_ * ci, ei + pr * ci + pi_ * cr
    return er, ei


def s5_scan_fwd(bu_re, bu_im, a_re, a_im, f_re, f_im, name):
    L, P = bu_re.shape
    tl = _rtile(L, 512)
    tab = _scan_tables(a_re, a_im, False)
    fvec = jnp.concatenate([f_re, f_im] + [jnp.zeros_like(f_re)] * 6, axis=0)

    def body(br_ref, bi_ref, tab_ref, f_ref, xr_ref, xi_ref, c_ref):
        @pl.when(pl.program_id(0) == 0)
        def _():
            c_ref[...] = jnp.zeros_like(c_ref)

        def blk(i, carry):
            cr, ci = carry
            rows = pl.ds(pl.multiple_of(i * 8, 8), 8)
            br, bi = br_ref[rows, :], bi_ref[rows, :]
            fr, fi = f_ref[0:1, :], f_ref[1:2, :]
            er, ei = _block_scan(fr * br - fi * bi, fr * bi + fi * br, tab_ref, cr, ci, False)
            xr_ref[rows, :] = er
            xi_ref[rows, :] = ei
            return er[7:8, :], ei[7:8, :]

        cr, ci = lax.fori_loop(0, tl // 8, blk, (c_ref[0:1, :], c_ref[1:2, :]))
        c_ref[0:1, :] = cr
        c_ref[1:2, :] = ci

    big = pl.BlockSpec((tl, P), lambda i: (i, 0))
    out = jax.ShapeDtypeStruct((L, P), F32)
    return pl.pallas_call(
        body, name=name, grid=(L // tl,),
        in_specs=[big, big, pl.BlockSpec((24, P), lambda i: (0, 0)), pl.BlockSpec((8, P), lambda i: (0, 0))],
        out_specs=(big, big), out_shape=(out, out), scratch_shapes=[pltpu.VMEM((8, P), F32)],
        compiler_params=_cparams("arbitrary"))(bu_re, bu_im, tab, fvec)


def s5_scan_bwd(dx_re, dx_im, x_re, x_im, bu_re, bu_im, a_re, a_im, f_re, f_im, name):
    L, P = dx_re.shape
    tl = _rtile(L, 256)
    nt = L // tl
    tab = _scan_tables(a_re, -a_im, True)
    fvec = jnp.concatenate([f_re, f_im] + [jnp.zeros_like(f_re)] * 6, axis=0)

    def body(gr_ref, gi_ref, xr_ref, xi_ref, br_ref, bi_ref, tab_ref, f_ref, dbr_ref, dbi_ref, s_ref, c_ref):
        @pl.when(pl.program_id(0) == 0)
        def _():
            c_ref[...] = jnp.zeros_like(c_ref)
            s_ref[...] = jnp.zeros_like(s_ref)

        def blk(k, carry):
            cr, ci = carry
            i = tl // 8 - 1 - k
            rows = pl.ds(pl.multiple_of(i * 8, 8), 8)
            er, ei = _block_scan(gr_ref[rows, :], gi_ref[rows, :], tab_ref, cr, ci, True)
            rid = lax.broadcasted_iota(jnp.int32, er.shape, 0)
            sr = jnp.where(rid == 7, cr, pltpu.roll(er, 7, axis=0))
            si = jnp.where(rid == 7, ci, pltpu.roll(ei, 7, axis=0))
            xr, xi = xr_ref[rows, :], xi_ref[rows, :]
            s_ref[0:8, :] += sr * xr + si * xi
            s_ref[8:16, :] += si * xr - sr * xi
            br, bi = br_ref[rows, :], bi_ref[rows, :]
            s_ref[16:24, :] += er * br + ei * bi
            s_ref[24:32, :] += ei * br - er * bi
            fr, fi = f_ref[0:1, :], f_ref[1:2, :]
            dbr_ref[rows, :] = fr * er + fi * ei
            dbi_ref[rows, :] = fr * ei - fi * er
            return er[0:1, :], ei[0:1, :]

        cr, ci = lax.fori_loop(0, tl // 8, blk, (c_ref[0:1, :], c_ref[1:2, :]))
        c_ref[0:1, :] = cr
        c_ref[1:2, :] = ci

    big = pl.BlockSpec((tl, P), lambda i: (nt - 1 - i, 0))
    out = jax.ShapeDtypeStruct((L, P), F32)
    return pl.pallas_call(
        body, name=name, grid=(nt,),
        in_specs=[big] * 6 + [pl.BlockSpec((24, P), lambda i: (0, 0)), pl.BlockSpec((8, P), lambda i: (0, 0))],
        out_specs=(big, big, pl.BlockSpec((32, P), lambda i: (0, 0))),
        out_shape=(out, out, jax.ShapeDtypeStruct((32, P), F32)), scratch_shapes=[pltpu.VMEM((8, P), F32)],
        compiler_params=_cparams("arbitrary"))(dx_re, dx_im, x_re, x_im, bu_re, bu_im, tab, fvec)


_GK, _GC = math.sqrt(2.0 / math.pi), 0.044715


def _gelu(y):
    return 0.5 * y * (1.0 + jnp.tanh(_GK * (y + _GC * y * y * y)))


def _dgelu(y):
    t = jnp.tanh(_GK * (y + _GC * y * y * y))
    return 0.5 * (1.0 + t) + 0.5 * y * (1.0 - t * t) * _GK * (1.0 + 3.0 * _GC * y * y)


def s5_out_fwd(x_re, x_im, u, cd_re, cd_im, dskip, glu_w, glu_b, name):
    L = u.shape[0]
    tl = _rtile(L, 512)

    def body(xr_ref, xi_ref, u_ref, cr_ref, ci_ref, d_ref, w_ref, b_ref, y_ref, o_ref):
        y = (jnp.dot(xr_ref[...].astype(BF16), cr_ref[...], preferred_element_type=F32)
             - jnp.dot(xi_ref[...].astype(BF16), ci_ref[...], preferred_element_type=F32)
             + d_ref[...] * u_ref[...])
        y_ref[...] = y
        g = _gelu(y)
        z = jnp.dot(g.astype(BF16), w_ref[...], preferred_element_type=F32) + b_ref[...]
        o_ref[...] = (g * _sigmoid(z)).astype(BF16)

    big = pl.BlockSpec((tl, S5_P), lambda i: (i, 0))
    sm = pl.BlockSpec((tl, S5_W), lambda i: (i, 0))
    full = lambda r, c: pl.BlockSpec((r, c), lambda i: (0, 0))
    return pl.pallas_call(
        body, name=name, grid=(L // tl,),
        in_specs=[big, big, sm, full(S5_P, S5_W), full(S5_P, S5_W), full(1, S5_W), full(S5_W, S5_W), full(1, S5_W)],
        out_specs=(sm, sm),
        out_shape=(jax.ShapeDtypeStruct((L, S5_W), F32), jax.ShapeDtypeStruct((L, S5_W), BF16)),
        compiler_params=_cparams("parallel"))(x_re, x_im, u, cd_re, cd_im, dskip, glu_w, glu_b)


def s5_out_bwd(dout, y, u, x_re, x_im, cd_re, cd_im, dskip, glu_w, glu_b, name):
    L = u.shape[0]
    tl = _rtile(L, 256)
    nt_dims = (((1,), (1,)), ((), ()))
    tn_dims = (((0,), (0,)), ((), ()))

    def body(do_ref, y_ref, u_ref, xr_ref, xi_ref, cr_ref, ci_ref, d_ref, w_ref, b_ref,
             dxr_ref, dxi_ref, du_ref, dcr_ref, dci_ref, dw_ref, s_ref):
        @pl.when(pl.program_id(0) == 0)
        def _():
            dcr_ref[...] = jnp.zeros_like(dcr_ref)
            dci_ref[...] = jnp.zeros_like(dci_ref)
            dw_ref[...] = jnp.zeros_like(dw_ref)
            s_ref[...] = jnp.zeros_like(s_ref)

        yv, dov = y_ref[...], do_ref[...]
        g = _gelu(yv)
        gb = g.astype(BF16)
        sg = _sigmoid(jnp.dot(gb, w_ref[...], preferred_element_type=F32) + b_ref[...])
        dz = dov * g * sg * (1.0 - sg)
        dzb = dz.astype(BF16)
        dg = dov * sg + lax.dot_general(dzb, w_ref[...], nt_dims, preferred_element_type=F32)
        dw_ref[...] += lax.dot_general(gb, dzb, tn_dims, preferred_element_type=F32)
        dy = dg * _dgelu(yv)
        dyb = dy.astype(BF16)
        s_ref[0:8, :] += _fold8(dy * u_ref[...])
        s_ref[8:16, :] += _fold8(dz)
        du_ref[...] = dy * d_ref[...]
        dxr_ref[...] = lax.dot_general(dyb, cr_ref[...], nt_dims, preferred_element_type=F32)
        dxi_ref[...] = -lax.dot_general(dyb, ci_ref[...], nt_dims, preferred_element_type=F32)
        dcr_ref[...] += lax.dot_general(xr_ref[...].astype(BF16), dyb, tn_dims, preferred_element_type=F32)
        dci_ref[...] -= lax.dot_general(xi_ref[...].astype(BF16), dyb, tn_dims, preferred_element_type=F32)

    big = pl.BlockSpec((tl, S5_P), lambda i: (i, 0))
    sm = pl.BlockSpec((tl, S5_W), lambda i: (i, 0))
    full = lambda r, c: pl.BlockSpec((r, c), lambda i: (0, 0))
    sd = jax.ShapeDtypeStruct
    return pl.pallas_call(
        body, name=name, grid=(L // tl,),
        in_specs=[sm, sm, sm, big, big, full(S5_P, S5_W), full(S5_P, S5_W), full(1, S5_W), full(S5_W, S5_W),
                  full(1, S5_W)],
        out_specs=(big, big, sm, full(S5_P, S5_W), full(S5_P, S5_W), full(S5_W, S5_W), full(16, S5_W)),
        out_shape=(sd((L, S5_P), F32), sd((L, S5_P), F32), sd((L, S5_W), F32), sd((S5_P, S5_W), F32),
                   sd((S5_P, S5_W), F32), sd((S5_W, S5_W), F32), sd((16, S5_W), F32)),
        compiler_params=_cparams("arbitrary"))(dout, y, u, x_re, x_im, cd_re, cd_im, dskip, glu_w, glu_b)


def s5_block_fwd(u, params, dskip, glu_w, glu_b, tag):
    a_re, a_im, f_re, f_im, bd_re, bd_im, cd_re, cd_im = params
    bu_re = matmul([(u, bd_re.astype(BF16))], "nn", tag + "_bure")
    bu_im = matmul([(u, bd_im.astype(BF16))], "nn", tag + "_buim")
    x_re, x_im = s5_scan_fwd(bu_re, bu_im, a_re, a_im, f_re, f_im, tag + "_scan")
    y, out = s5_out_fwd(x_re, x_im, u, cd_re.astype(BF16), cd_im.astype(BF16), dskip, glu_w, glu_b, tag + "_out")
    return out, (u, bu_re, bu_im, x_re, x_im, y)


def s5_block_bwd(dout, res, params, dskip, glu_w, glu_b, tag):
    u, bu_re, bu_im, x_re, x_im, y = res
    a_re, a_im, f_re, f_im, bd_re, bd_im, cd_re, cd_im = params
    dxr, dxi, du, dcr, dci, dglu_w, sums = s5_out_bwd(dout, y, u, x_re, x_im, cd_re.astype(BF16), cd_im.astype(BF16),
                                                      dskip, glu_w, glu_b, tag + "_dout")
    dbr, dbi, acc = s5_scan_bwd(dxr, dxi, x_re, x_im, bu_re, bu_im, a_re, a_im, f_re, f_im, tag + "_dscan")
    du = du + matmul([(dbr, bd_re.astype(BF16)), (dbi, bd_im.astype(BF16))], "nt", tag + "_du")
    dbd_re = matmul([(u, dbr)], "tn", tag + "_dbdre")
    dbd_im = matmul([(u, dbi)], "tn", tag + "_dbdim")
    acc = acc.reshape(4, 8, S5_P).sum(axis=1)
    s = sums.reshape(2, 8, S5_W).sum(axis=1)
    cot = (acc[0:1], acc[1:2], acc[2:3], acc[3:4], dbd_re, dbd_im, dcr, dci)
    return du, cot, dict(dskip=s[0], glu_w=dglu_w, glu_b=s[1])


DN_QKV0, DN_Z0, DN_NT = 2, 20, 18


def _l2n(s, j):
    r = lax.rsqrt(jnp.sum(s * s, axis=-1, keepdims=True) + EPS)
    scale = jnp.where(j < DN_H, DN_DK ** -0.5, 1.0)
    return r, scale


def dn_prep_fwd(rin, cw, name):
    L = rin.shape[0]
    tl = _rtile(L, 512)
    hb = tl // 8

    def body(x_ref, h_ref, w_ref, o_ref):
        j = pl.program_id(0)
        first = pl.program_id(1) == 0
        x, h = x_ref[...], h_ref[...]
        xc = w_ref[3:4, :] * x
        for k in range(1, 4):
            xc = xc + w_ref[3 - k:4 - k, :] * _shift_down(x, h, k, first)
        s = _silu(xc)
        r, scale = _l2n(s, j)
        o_ref[...] = jnp.where(j < 2 * DN_H, s * r * scale, s)

    main = pl.BlockSpec((tl, 128), lambda j, i: (i, DN_QKV0 + j))
    halo = pl.BlockSpec((8, 128), lambda j, i: (jnp.maximum(i * hb - 1, 0), DN_QKV0 + j))
    return pl.pallas_call(
        body, name=name, grid=(DN_NT, L // tl),
        in_specs=[main, halo, pl.BlockSpec((4, 128), lambda j, i: (0, j))],
        out_specs=pl.BlockSpec((tl, 128), lambda j, i: (i, j)),
        out_shape=jax.ShapeDtypeStruct((L, DN_NT * 128), F32),
        compiler_params=_cparams("parallel", "parallel"))(rin, rin, cw)


def dn_prep_bwd(rin, cw, dout, name):
    L = rin.shape[0]
    tl = _rtile(L, 512)
    hb = tl // 8
    nrt = L // tl

    def body(x_ref, h_ref, w_ref, d_ref, dx_ref, s_ref, c_ref):
        j = pl.program_id(0)
        i = pl.program_id(1)
        first = i == nrt - 1

        @pl.when(i == 0)
        def _():
            s_ref[...] = jnp.zeros_like(s_ref)
            c_ref[...] = jnp.zeros_like(c_ref)

        x, h = x_ref[...], h_ref[...]
        taps = [x] + [_shift_down(x, h, k, first) for k in range(1, 4)]
        xc = w_ref[3:4, :] * x
        for k in range(1, 4):
            xc = xc + w_ref[3 - k:4 - k, :] * taps[k]
        s = _silu(xc)
        r, scale = _l2n(s, j)
        n = s * r
        dn = d_ref[...] * scale
        ds_norm = r * (dn - n * jnp.sum(dn * n, axis=-1, keepdims=True))
        ds = jnp.where(j < 2 * DN_H, ds_norm, d_ref[...])
        dxc = ds * _dsilu(xc)
        nxt = c_ref[...]
        dx = w_ref[3:4, :] * dxc
        for k in range(1, 4):
            dx = dx + w_ref[3 - k:4 - k, :] * _shift_up(dxc, nxt, k)
        dx_ref[...] = dx
        c_ref[...] = dxc[0:8, :]
        for k in range(4):
            s_ref[8 * (3 - k):8 * (3 - k) + 8, :] += _fold8(dxc * taps[k])

    rev = lambda i: nrt - 1 - i
    main = pl.BlockSpec((tl, 128), lambda j, i: (rev(i), DN_QKV0 + j))
    halo = pl.BlockSpec((8, 128), lambda j, i: (jnp.maximum(rev(i) * hb - 1, 0), DN_QKV0 + j))
    own = pl.BlockSpec((tl, 128), lambda j, i: (rev(i), j))
    return pl.pallas_call(
        body, name=name, grid=(DN_NT, nrt),
        in_specs=[main, halo, pl.BlockSpec((4, 128), lambda j, i: (0, j)), own],
        out_specs=(own, pl.BlockSpec((32, 128), lambda j, i: (0, j))),
        out_shape=(jax.ShapeDtypeStruct((L, DN_NT * 128), F32), jax.ShapeDtypeStruct((32, DN_NT * 128), F32)),
        scratch_shapes=[pltpu.VMEM((8, 128), F32)],
        compiler_params=_cparams("parallel", "arbitrary"))(rin, rin, cw, dout)


_HI = lax.Precision.HIGH
_NT = (((1,), (1,)), ((), ()))
_TN = (((0,), (0,)), ((), ()))
_HEADS = tuple(range(DN_H))


def _mm(a, b, dims=(((1,), (0,)), ((), ())), hi=False):
    if hi:
        return lax.dot_general(a, b, dims, precision=_HI, preferred_element_type=F32)
    return lax.dot_general(a.astype(BF16), b.astype(BF16), dims, preferred_element_type=F32)


def _dn_masks():
    ri = lax.broadcasted_iota(jnp.int32, (DN_C, DN_C), 0)
    ci = lax.broadcasted_iota(jnp.int32, (DN_C, DN_C), 1)
    return ri >= ci, ri > ci, (ri == ci).astype(F32)


def _dn_decay(gc, gr, causal):
    gam = [jnp.where(causal, jnp.exp(jnp.where(causal, gc[h] - gr[h], 0.0)), 0.0) for h in _HEADS]
    eg = [jnp.exp(gc[h]) for h in _HEADS]
    el = [jnp.exp(gc[h][DN_C - 1:DN_C, :] - gc[h]) for h in _HEADS]
    gl = [jnp.exp(gc[h][DN_C - 1:DN_C, :]) for h in _HEADS]
    return gam, eg, el, gl


def _dn_solve(k, v, beta, gam, eg, kk, strict, eye):
    nmat = [jnp.where(strict, beta[h] * kk[h] * gam[h], 0.0) for h in _HEADS]
    t = [eye - nmat[h] for h in _HEADS]
    m = [_mm(nmat[h], nmat[h], hi=True) for h in _HEADS]
    for step in range(5):
        t = [t[h] + _mm(t[h], m[h], hi=True) for h in _HEADS]
        if step < 4:
            m = [_mm(m[h], m[h], hi=True) for h in _HEADS]
    rhs = [jnp.concatenate([v[h] * beta[h], k[h] * (beta[h] * eg[h])], axis=1) for h in _HEADS]
    sol = [_mm(t[h], rhs[h], hi=True) for h in _HEADS]
    return t, sol


def dn_chunk_fwd(qkv, gcol, grow, bcol, name):
    L = qkv.shape[0]
    C, W = DN_C, DN_H * DN_DK
    ncb = 8
    tl = ncb * C
    nchunks = L // C

    def body(q_ref, k_ref, v_ref, gc_ref, gr_ref, b_ref, o_ref, sh_ref, t_ref, sol_ref, s_ref):
        @pl.when(pl.program_id(0) == 0)
        def _():
            s_ref[...] = jnp.zeros_like(s_ref)

        causal, strict, eye = _dn_masks()

        def chunk(c, _):
            rows = pl.ds(pl.multiple_of(c * C, C), C)
            grow_c = gr_ref[c]
            hs = lambda h: slice(h * 128, (h + 1) * 128)
            q = [q_ref[rows, hs(h)] for h in _HEADS]
            k = [k_ref[rows, hs(h)] for h in _HEADS]
            v = [v_ref[rows, hs(h)] for h in _HEADS]
            gc = [gc_ref[rows, h:h + 1] for h in _HEADS]
            gr = [grow_c[h:h + 1, :] for h in _HEADS]
            beta = [b_ref[rows, h:h + 1] for h in _HEADS]
            gam, eg, el, gl = _dn_decay(gc, gr, causal)
            kk = [_mm(k[h], k[h], _NT) for h in _HEADS]
            t, sol = _dn_solve(k, v, beta, gam, eg, kk, strict, eye)
            qk = [_mm(q[h], k[h], _NT) * gam[h] for h in _HEADS]
            S = [s_ref[hs(h), :] for h in _HEADS]
            vn = [sol[h][:, :128] - _mm(sol[h][:, 128:], S[h]) for h in _HEADS]
            o = [_mm(q[h] * eg[h], S[h]) + _mm(qk[h], vn[h]) for h in _HEADS]
            Sn = [S[h] * gl[h] + _mm(k[h] * el[h], vn[h], _TN) for h in _HEADS]
            for h in _HEADS:
                sh_ref[c, hs(h), :] = S[h]
                s_ref[hs(h), :] = Sn[h]
                o_ref[rows, hs(h)] = o[h]
                t_ref[rows, h * C:(h + 1) * C] = t[h]
                sol_ref[rows, h * 256:(h + 1) * 256] = sol[h]
            return 0

        lax.fori_loop(0, ncb, chunk, 0)

    col = lambda b: pl.BlockSpec((tl, W), lambda i: (i, b))
    small = pl.BlockSpec((tl, 8), lambda i: (i, 0))
    rowblk = lambda w: pl.BlockSpec((tl, w), lambda i: (i, 0))
    sd = jax.ShapeDtypeStruct
    return pl.pallas_call(
        body, name=name, grid=(L // tl,),
        in_specs=[col(0), col(1), col(2), small, pl.BlockSpec((ncb, 8, C), lambda i: (i, 0, 0)), small],
        out_specs=(rowblk(W), pl.BlockSpec((ncb, W, 128), lambda i: (i, 0, 0)), rowblk(DN_H * C), rowblk(DN_H * 256)),
        out_shape=(sd((L, W), F32), sd((nchunks, W, 128), F32), sd((L, DN_H * C), F32), sd((L, DN_H * 256), F32)),
        scratch_shapes=[pltpu.VMEM((W, 128), F32)],
        compiler_params=_cparams("arbitrary"))(qkv, qkv, qkv, gcol, grow, bcol)


def dn_chunk_bwd(qkv, gcol, grow, bcol, shist, thist, solhist, do, name):
    L = qkv.shape[0]
    C, W = DN_C, DN_H * DN_DK
    ncb = 8
    tl = ncb * C
    nchunks = L // C
    nt = L // tl

    def body(q_ref, k_ref, v_ref, gc_ref, gr_ref, b_ref, sh_ref, t_ref, sol_ref, do_ref,
             dqkv_ref, dgc_ref, dgr_ref, db_ref, ds_ref):
        @pl.when(pl.program_id(0) == 0)
        def _():
            ds_ref[...] = jnp.zeros_like(ds_ref)

        lane8 = lax.broadcasted_iota(jnp.int32, (C, 8), 1)
        sub8 = lax.broadcasted_iota(jnp.int32, (8, C), 0)
        rowid = lax.broadcasted_iota(jnp.int32, (C, 1), 0)
        causal, strict, _ = _dn_masks()
        rsum = lambda a: jnp.sum(a, axis=1, keepdims=True)

        def chunk(cc, _):
            c = ncb - 1 - cc
            rows = pl.ds(pl.multiple_of(c * C, C), C)
            grow_c = gr_ref[c]
            hs = lambda h: slice(h * 128, (h + 1) * 128)
            q = [q_ref[rows, hs(h)] for h in _HEADS]
            k = [k_ref[rows, hs(h)] for h in _HEADS]
            v = [v_ref[rows, hs(h)] for h in _HEADS]
            gc = [gc_ref[rows, h:h + 1] for h in _HEADS]
            gr = [grow_c[h:h + 1, :] for h in _HEADS]
            beta = [b_ref[rows, h:h + 1] for h in _HEADS]
            t = [t_ref[rows, h * C:(h + 1) * C] for h in _HEADS]
            sol = [sol_ref[rows, h * 256:(h + 1) * 256] for h in _HEADS]
            S = [sh_ref[c, hs(h), :] for h in _HEADS]
            dS = [ds_ref[hs(h), :] for h in _HEADS]
            dov = [do_ref[rows, hs(h)] for h in _HEADS]
            gam, eg, el, gl = _dn_decay(gc, gr, causal)
            kk = [_mm(k[h], k[h], _NT) for h in _HEADS]
            qk_raw = [_mm(q[h], k[h], _NT) for h in _HEADS]
            w = [sol[h][:, 128:] for h in _HEADS]
            kd = [k[h] * el[h] for h in _HEADS]
            vn = [sol[h][:, :128] - _mm(w[h], S[h]) for h in _HEADS]
            dvn = [_mm(qk_raw[h] * gam[h], dov[h], _TN) + _mm(kd[h], dS[h]) for h in _HEADS]
            dqd = [_mm(dov[h], S[h], _NT) for h in _HEADS]
            dqk = [jnp.where(causal, _mm(dov[h], vn[h], _NT), 0.0) for h in _HEADS]
            dkd = [_mm(vn[h], dS[h], _NT) for h in _HEADS]
            dgl = [jnp.sum(rsum(dS[h] * S[h]), axis=0, keepdims=True) for h in _HEADS]
            dw = [-_mm(dvn[h], S[h], _NT) for h in _HEADS]
            dSn = [dS[h] * gl[h] + _mm(q[h] * eg[h], dov[h], _TN) - _mm(w[h], dvn[h], _TN) for h in _HEADS]
            drhs = [_mm(t[h], jnp.concatenate([dvn[h], dw[h]], axis=1), _TN, hi=True) for h in _HEADS]
            dn = [jnp.where(strict, -_mm(drhs[h], sol[h], _NT, hi=True), 0.0) for h in _HEADS]
            dgc_all = jnp.zeros((C, 8), F32)
            db_all = jnp.zeros((C, 8), F32)
            dgr_all = jnp.zeros((8, C), F32)
            for h in _HEADS:
                drv, drk = drhs[h][:, :128], drhs[h][:, 128:]
                t2 = rsum(drk * k[h])
                x = dn[h] * gam[h]
                dbeta = rsum(drv * v[h]) + t2 * eg[h] + rsum(x * kk[h])
                dkk = x * beta[h]
                draw = dqk[h] * gam[h]
                mm_ = (dn[h] * beta[h] * kk[h] + dqk[h] * qk_raw[h]) * gam[h]
                deg = t2 * beta[h] + rsum(dqd[h] * q[h])
                r_ = rsum(dkd[h] * k[h]) * el[h]
                dglast = jnp.sum(r_, axis=0, keepdims=True) + dgl[h] * gl[h]
                dgc = rsum(mm_) + deg * eg[h] - r_ + jnp.where(rowid == C - 1, dglast, 0.0)
                dgr = -jnp.sum(mm_, axis=0, keepdims=True)
                dqkv_ref[rows, hs(h)] = _mm(draw, k[h]) + dqd[h] * eg[h]
                dqkv_ref[rows, hs(DN_H + h)] = (drk * (beta[h] * eg[h]) + _mm(dkk, k[h]) + _mm(dkk, k[h], _TN)
                                                + _mm(draw, q[h], _TN) + dkd[h] * el[h])
                dqkv_ref[rows, hs(2 * DN_H + h)] = drv * beta[h]
                ds_ref[hs(h), :] = dSn[h]
                dgc_all = dgc_all + jnp.where(lane8 == h, dgc, 0.0)
                db_all = db_all + jnp.where(lane8 == h, dbeta, 0.0)
                dgr_all = dgr_all + jnp.where(sub8 == h, dgr, 0.0)
            dgc_ref[rows, :] = dgc_all
            db_ref[rows, :] = db_all
            dgr_ref[c] = dgr_all
            return 0

        lax.fori_loop(0, ncb, chunk, 0)

    rev = lambda i: nt - 1 - i
    col = lambda b: pl.BlockSpec((tl, W), lambda i: (rev(i), b))
    rowblk = lambda w: pl.BlockSpec((tl, w), lambda i: (rev(i), 0))
    small = pl.BlockSpec((tl, 8), lambda i: (rev(i), 0))
    g3 = pl.BlockSpec((ncb, 8, C), lambda i: (rev(i), 0, 0))
    sd = jax.ShapeDtypeStruct
    return pl.pallas_call(
        body, name=name, grid=(nt,),
        in_specs=[col(0), col(1), col(2), small, g3, small,
                  pl.BlockSpec((ncb, W, 128), lambda i: (rev(i), 0, 0)), rowblk(DN_H * C), rowblk(DN_H * 256), col(0)],
        out_specs=(rowblk(3 * W), small, g3, small),
        out_shape=(sd((L, 3 * W), F32), sd((L, 8), F32), sd((nchunks, 8, C), F32), sd((L, 8), F32)),
        scratch_shapes=[pltpu.VMEM((W, 128), F32)],
        compiler_params=_cparams("arbitrary"))(qkv, qkv, qkv, gcol, grow, bcol, shist, thist, solhist, do)


def dn_out_fwd(o, rin, nw, name):
    L = o.shape[0]
    tl = _rtile(L, 512)

    def body(o_ref, z_ref, w_ref, y_ref):
        ov = o_ref[...]
        r = lax.rsqrt(jnp.mean(ov * ov, axis=-1, keepdims=True) + EPS)
        y_ref[...] = (ov * r * w_ref[...] * _silu(z_ref[...])).astype(BF16)

    own = pl.BlockSpec((tl, 128), lambda j, i: (i, j))
    return pl.pallas_call(
        body, name=name, grid=(DN_H, L // tl),
        in_specs=[own, pl.BlockSpec((tl, 128), lambda j, i: (i, DN_Z0 + j)), pl.BlockSpec((1, 128), lambda j, i: (0, 0))],
        out_specs=own, out_shape=jax.ShapeDtypeStruct((L, DN_H * 128), BF16),
        compiler_params=_cparams("parallel", "parallel"))(o, rin, nw)


def dn_out_bwd(dy, o, rin, nw, name):
    L = o.shape[0]
    tl = _rtile(L, 512)

    def body(dy_ref, o_ref, z_ref, w_ref, do_ref, dz_ref, s_ref):
        @pl.when(pl.program_id(1) == 0)
        def _():
            s_ref[...] = jnp.zeros_like(s_ref)

        ov, zv, d = o_ref[...], z_ref[...], dy_ref[...]
        r = lax.rsqrt(jnp.mean(ov * ov, axis=-1, keepdims=True) + EPS)
        n = ov * r
        dnw = d * _silu(zv)
        dz_ref[...] = d * n * w_ref[...] * _dsilu(zv)
        dn = dnw * w_ref[...]
        do_ref[...] = r * (dn - n * jnp.mean(dn * n, axis=-1, keepdims=True))
        s_ref[...] += _fold8(dnw * n)

    own = pl.BlockSpec((tl, 128), lambda j, i: (i, j))
    sd = jax.ShapeDtypeStruct
    return pl.pallas_call(
        body, name=name, grid=(DN_H, L // tl),
        in_specs=[own, own, pl.BlockSpec((tl, 128), lambda j, i: (i, DN_Z0 + j)),
                  pl.BlockSpec((1, 128), lambda j, i: (0, 0))],
        out_specs=(own, own, pl.BlockSpec((8, 128), lambda j, i: (0, j))),
        out_shape=(sd((L, DN_H * 128), F32), sd((L, DN_H * 128), F32), sd((8, DN_H * 128), F32)),
        compiler_params=_cparams("parallel", "arbitrary"))(dy, o, rin, nw)


def dn_gates(a, beta_raw, a_log, dt_bias):
    L = a.shape[0]
    beta = jax.nn.sigmoid(beta_raw)
    g = -jnp.exp(a_log) * jax.nn.softplus(a + dt_bias)
    G = jnp.cumsum(g.reshape(L // DN_C, DN_C, DN_H), axis=1)
    pad = lambda t: jnp.pad(t, ((0, 0), (0, 8 - DN_H)))
    gcol = pad(G.reshape(L, DN_H))
    grow = jnp.pad(jnp.transpose(G, (0, 2, 1)), ((0, 0), (0, 8 - DN_H), (0, 0)))
    return gcol, grow, pad(beta)


def dn_block_fwd(rin, cw, a_log, dt_bias, out_norm, tag):
    gates, gates_vjp = jax.vjp(dn_gates, rin[:, 3328:3334], rin[:, 3334:3340], a_log, dt_bias)
    qkv = dn_prep_fwd(rin, cw, tag + "_prep")
    o, shist, thist, solhist = dn_chunk_fwd(qkv, *gates, tag + "_chunk")
    yd = dn_out_fwd(o, rin, out_norm.reshape(1, 128), tag + "_onorm")
    return yd, (qkv, gates, gates_vjp, o, shist, thist, solhist)


def dn_block_bwd(dyd, res, rin, cw, out_norm, tag):
    qkv, gates, gates_vjp, o, shist, thist, solhist = res
    do, dz, nsum = dn_out_bwd(dyd, o, rin, out_norm.reshape(1, 128), tag + "_donorm")
    dqkv, dgc, dgr, db = dn_chunk_bwd(qkv, *gates, shist, thist, solhist, do, tag + "_dchunk")
    da, dbraw, g_alog, g_dtb = gates_vjp((dgc, dgr, db))
    dx, csum = dn_prep_bwd(rin, cw, dqkv, tag + "_dprep")
    grads = dict(conv=csum.reshape(4, 8, DN_NT * 128).sum(axis=1), a_log=g_alog, dt_bias=g_dtb,
                 out_norm=nsum.sum(axis=0).reshape(DN_H, 128).sum(axis=0))
    return dx, dz, da, dbraw, grads


_HBM = pl.BlockSpec(memory_space=pltpu.HBM)


def _mesh_pos():
    xi, yi, ci = lax.axis_index("x"), lax.axis_index("y"), lax.axis_index("c")
    return xi, yi, ci, 4 * xi + 2 * yi + ci


def _peer(xi, yi, ci, k):
    px = 1 - xi if (k >> 2) & 1 else xi
    py = 1 - yi if (k >> 1) & 1 else yi
    pc = 1 - ci if k & 1 else ci
    return (px, py, pc), 4 * px + 2 * py + pc


def _exchange(xs, gather, name):
    n = len(xs)

    def body(*refs):
        x_refs, o_refs = refs[:n], refs[n:2 * n]
        send_sems, recv_sems, lsems = refs[2 * n:]
        xi, yi, ci, me = _mesh_pos()
        copies = []
        for t in range(n):
            src_of = (lambda lin, t=t: x_refs[t]) if gather else (lambda lin, t=t: x_refs[t].at[lin])
            local = pltpu.make_async_copy(src_of(me), o_refs[t].at[me], lsems.at[t])
            local.start()
            copies.append(local)
            for k in range(1, N_DEV):
                peer, lin = _peer(xi, yi, ci, k)
                s = t * (N_DEV - 1) + k - 1
                cp = pltpu.make_async_remote_copy(
                    src_ref=src_of(lin), dst_ref=o_refs[t].at[me], send_sem=send_sems.at[s],
                    recv_sem=recv_sems.at[s], device_id=peer, device_id_type=pl.DeviceIdType.MESH)
                cp.start()
                copies.append(cp)
        for cp in copies:
            cp.wait()

    return pl.pallas_call(
        body, name=name, in_specs=[_HBM] * n, out_specs=tuple([_HBM] * n),
        out_shape=tuple(jax.ShapeDtypeStruct((N_DEV,) + x.shape[-2:], x.dtype) for x in xs),
        scratch_shapes=[pltpu.SemaphoreType.DMA((n * (N_DEV - 1),)), pltpu.SemaphoreType.DMA((n * (N_DEV - 1),)),
                        pltpu.SemaphoreType.DMA((n,))],
    )(*xs)


def all_gather(x, name):
    return _exchange([x], True, name)[0]


def all_gather_many(xs, name):
    return _exchange(xs, True, name)


def all_to_all_many(xs, name):
    return _exchange(xs, False, name)


def reduce_adamw(gsrc, w, m, v, name):
    S, R, C = gsrc.shape
    tr = _rtile(R, max(8, min(256, (4 << 20) // (S * C * 4) // 8 * 8)))
    c1 = 1.0 - ADAM_B1 ** ADAM_STEP
    c2 = 1.0 - ADAM_B2 ** ADAM_STEP

    def body(g_ref, w_ref, m_ref, v_ref, go_ref, d_ref, mo_ref, vo_ref):
        g = g_ref[0]
        for s in range(1, S):
            g = g + g_ref[s]
        go_ref[...] = g
        mn = ADAM_B1 * m_ref[...] + (1.0 - ADAM_B1) * g
        vn = ADAM_B2 * v_ref[...] + (1.0 - ADAM_B2) * (g * g)
        mo_ref[...] = mn
        vo_ref[...] = vn
        d_ref[...] = -ADAM_LR * ((mn / c1) / (jnp.sqrt(vn / c2) + ADAM_EPS) + ADAM_WD * w_ref[...])

    big = pl.BlockSpec((tr, C), lambda i: (i, 0))
    o = jax.ShapeDtypeStruct((R, C), F32)
    return pl.pallas_call(
        body, name=name, grid=(R // tr,),
        in_specs=[pl.BlockSpec((S, tr, C), lambda i: (0, i, 0)), big, big, big],
        out_specs=(big, big, big, big), out_shape=(o, o, o, o),
        compiler_params=_cparams("parallel"))(gsrc, w, m, v)


def _to_slabs(g, ax):
    shp = g.shape
    g = g.reshape(shp[:ax] + (N_DEV, shp[ax] // N_DEV) + shp[ax + 1:])
    return jnp.moveaxis(g, ax, 0).reshape(N_DEV, -1)


def _from_slabs(s, ax, shp):
    s = s.reshape((N_DEV,) + shp[:ax] + (shp[ax] // N_DEV,) + shp[ax + 1:])
    return jnp.moveaxis(s, 0, ax).reshape(shp)


def _pack_rows(flat, width, row_mult):
    n = flat.shape[-1]
    per = width * row_mult
    tot = -(-n // per) * per
    flat = jnp.pad(flat, [(0, 0)] * (flat.ndim - 1) + [(0, tot - n)])
    return flat.reshape(flat.shape[:-1] + (tot // width, width))


def _offsets(sizes):
    offs, o = [], 0
    for s in sizes:
        offs.append(o)
        o += s
    return offs


WEIGHTS = ['ada_w', 'ada_b', 'norm_mix', 'norm_ffn', 'attn_w_in', 'attn_q_norm_a', 'attn_k_norm_a', 'attn_q_norm_b',
           'attn_k_norm_b', 'attn_sinks', 'attn_w_out', 'rec_w_in', 's5_lambda_re', 's5_lambda_im', 's5_log_dt',
           's5_b_re', 's5_b_im', 's5_c_re', 's5_c_im', 's5_d', 's5_glu_w', 's5_glu_b', 'dn_conv', 'dn_a_log',
           'dn_dt_bias', 'dn_out_norm', 'rec_w_out', 'ffn_w_up', 'ffn_conv', 'ffn_w_down']
BIG = [('attn_w_in', (D, ATTN_IN // N_DEV)), ('attn_w_out', (D // N_DEV, D)), ('rec_w_in', (D // N_DEV, REC_PAD)),
       ('s5_glu_w', (S5_W // N_DEV, S5_W)), ('rec_w_out', (D // N_DEV, D)), ('ffn_w_up', (2 * D, 2 * D_FF // N_DEV)),
       ('ffn_w_down', (2 * D_FF // N_DEV, D))]


def _shard2d(name, t):
    if name == 'rec_w_in':
        return jnp.pad(t[0], ((0, 0), (0, REC_PAD - REC_IN)))
    return t.reshape((-1, t.shape[-1]))


def _cols_to_slabs(g):
    r, n = g.shape
    return jnp.transpose(g.reshape(r, N_DEV, n // N_DEV), (1, 0, 2))


def _slabs_to_cols(s):
    k, r, c_ = s.shape
    return jnp.transpose(s, (1, 0, 2)).reshape(r, k * c_)
SMALL_SHARDED = [('s5_d', 1, (1, S5_W)), ('s5_glu_b', 1, (1, S5_W)), ('dn_conv', 2, (1, 4, 2304)),
                 ('ffn_conv', 2, (2, 3, 2 * D_FF))]
REPLICATED = [('ada_b', (2, 6 * D)), ('norm_mix', (2, D)), ('norm_ffn', (2, D)), ('attn_q_norm_a', (1, HD)),
              ('attn_k_norm_a', (1, HD)), ('attn_q_norm_b', (1, HD)), ('attn_k_norm_b', (1, HD)),
              ('attn_sinks', (1, 8)), ('s5_lambda_re', (1, 16, 64)), ('s5_lambda_im', (1, 16, 64)),
              ('s5_log_dt', (1, 16)), ('s5_b_re', (1, 16, 64, 16)), ('s5_b_im', (1, 16, 64, 16)),
              ('s5_c_re', (1, 16, 16, 64)), ('s5_c_im', (1, 16, 16, 64)), ('dn_a_log', (1, DN_H)),
              ('dn_dt_bias', (1, DN_H)), ('dn_out_norm', (1, 128))]


def _numel(shp):
    return int(np.prod(shp))


def kernel(x, c, ada_w, ada_b, norm_mix, norm_ffn, attn_w_in, attn_q_norm_a, attn_k_norm_a, attn_q_norm_b, attn_k_norm_b, attn_sinks, attn_w_out, rec_w_in, s5_lambda_re, s5_lambda_im, s5_log_dt, s5_b_re, s5_b_im, s5_c_re, s5_c_im, s5_d, s5_glu_w, s5_glu_b, dn_conv, dn_a_log, dn_dt_bias, dn_out_norm, rec_w_out, ffn_w_up, ffn_conv, ffn_w_down, loss_target, m_ada_w, m_ada_b, m_norm_mix, m_norm_ffn, m_attn_w_in, m_attn_q_norm_a, m_attn_k_norm_a, m_attn_q_norm_b, m_attn_k_norm_b, m_attn_sinks, m_attn_w_out, m_rec_w_in, m_s5_lambda_re, m_s5_lambda_im, m_s5_log_dt, m_s5_b_re, m_s5_b_im, m_s5_c_re, m_s5_c_im, m_s5_d, m_s5_glu_w, m_s5_glu_b, m_dn_conv, m_dn_a_log, m_dn_dt_bias, m_dn_out_norm, m_rec_w_out, m_ffn_w_up, m_ffn_conv, m_ffn_w_down, v_ada_w, v_ada_b, v_norm_mix, v_norm_ffn, v_attn_w_in, v_attn_q_norm_a, v_attn_k_norm_a, v_attn_q_norm_b, v_attn_k_norm_b, v_attn_sinks, v_attn_w_out, v_rec_w_in, v_s5_lambda_re, v_s5_lambda_im, v_s5_log_dt, v_s5_b_re, v_s5_b_im, v_s5_c_re, v_s5_c_im, v_s5_d, v_s5_glu_w, v_s5_glu_b, v_dn_conv, v_dn_a_log, v_dn_dt_bias, v_dn_out_norm, v_rec_w_out, v_ffn_w_up, v_ffn_conv, v_ffn_w_down):
    loc = locals()
    W = {n: loc[n] for n in WEIGHTS}
    M = {n: loc["m_" + n] for n in WEIGHTS}
    V = {n: loc["v_" + n] for n in WEIGHTS}
    _, _, _, me = _mesh_pos()
    L = x.shape[1]
    x0, tgt = x[0], loss_target[0]

    small_in = jnp.concatenate([c.reshape(-1)] + [W[n].reshape(-1) for n, _, _ in SMALL_SHARDED])
    si = all_gather(_pack_rows(small_in, 1024, 8), "gather_small_in").reshape(N_DEV, -1)
    c_all = si[:, :D]
    off = D
    small_full = {}
    for n, ax, shp in SMALL_SHARDED:
        k = _numel(shp) // N_DEV
        small_full[n] = _from_slabs(si[:, off:off + k], ax, shp)
        off += k

    cond_all = jax.nn.silu(c_all)
    modp = jnp.concatenate([matmul([(cond_all, ada_w[l].astype(BF16))], "nn", f"ada{l}") for l in range(2)], axis=0)
    modp_all = all_gather(modp, "gather_mod")
    mods = []
    for l in range(2):
        row = lax.dynamic_index_in_dim(modp_all, l * N_DEV + me, axis=1, keepdims=False)
        mod = row.reshape(1, 6 * D) + ada_b[l].reshape(1, 6 * D)
        mods.append([mod[:, i * D:(i + 1) * D] for i in range(6)])

    gathered = all_gather_many([_shard2d(n, W[n]).astype(BF16) for n, _ in BIG], "gather_weights")
    full = dict(zip([n for n, _ in BIG], gathered))
    w_att_in = _slabs_to_cols(full['attn_w_in'])
    w_att_out, w_rec_out = full['attn_w_out'].reshape(D, D), full['rec_w_out'].reshape(D, D)
    w_rec_in, glu_w = full['rec_w_in'].reshape(D, REC_PAD), full['s5_glu_w'].reshape(S5_W, S5_W)
    up = full['ffn_w_up'].reshape(N_DEV, 2, D, 2 * D_FF // N_DEV)
    w_up = [(_slabs_to_cols(up[:4, l]), _slabs_to_cols(up[4:, l])) for l in range(2)]
    down = full['ffn_w_down'].reshape(N_DEV, 2, D_FF // N_DEV, D)
    w_down = [down[:, l].reshape(D_FF, D) for l in range(2)]
    ffn_cw = [small_full['ffn_conv'][l] for l in range(2)]
    dn_cw = small_full['dn_conv'][0]
    s5_dskip, glu_b = small_full['s5_d'], small_full['s5_glu_b']
    row = lambda t: t.reshape(1, -1)

    sh1, sc1, g1, sh2, sc2, g2 = mods[0]
    h1 = gate_norm_fwd(x0, None, None, row(norm_mix[0]), sh1, sc1, "l0_norm1")
    wvec, sinkvec = attn_vectors(attn_q_norm_a[0], attn_k_norm_a[0], attn_q_norm_b[0], attn_k_norm_b[0], attn_sinks[0])
    y0, res_att = attention_block_fwd(h1, w_att_in, wvec, sinkvec, w_att_out, "att")
    x1, h2 = gate_norm_fwd(x0, y0, g1, row(norm_ffn[0]), sh2, sc2, "l0_norm2")
    f0, res_f0 = ffn_block_fwd(h2, w_up[0][0], w_up[0][1], ffn_cw[0], w_down[0], "ffn0")
    t1, tc1, tg1, t2, tc2, tg2 = mods[1]
    x2, h3 = gate_norm_fwd(x1, f0, g2, row(norm_mix[1]), t1, tc1, "l1_norm1")
    rin = matmul([(h3, w_rec_in)], "nn", "rec_in")
    s5p, s5p_vjp = jax.vjp(s5_params, s5_lambda_re[0], s5_lambda_im[0], s5_log_dt[0], s5_b_re[0], s5_b_im[0],
                           s5_c_re[0], s5_c_im[0])
    u = rin[:, :S5_W]
    yc, res_s5 = s5_block_fwd(u, s5p, s5_dskip, glu_w, glu_b, "s5")
    yd, res_dn = dn_block_fwd(rin, dn_cw, dn_a_log[0], dn_dt_bias[0], dn_out_norm[0], "dn")
    ycat = jnp.concatenate([yc, yd], axis=1)
    y1 = matmul([(ycat, w_rec_out)], "nn", "rec_out")
    x3, h4 = gate_norm_fwd(x2, y1, tg1, row(norm_ffn[1]), t2, tc2, "l1_norm2")
    f1, res_f1 = ffn_block_fwd(h4, w_up[1][0], w_up[1][1], ffn_cw[1], w_down[1], "ffn1")
    dx4, df1, lsum = final_loss(x3, f1, tg2, tgt, "loss")

    G = {}
    d_tg2 = lsum[8:16].sum(axis=0)
    dh4, gf1 = ffn_block_bwd(df1, res_f1, w_up[1][0], w_up[1][1], ffn_cw[1], w_down[1], "ffn1")
    dx3, dy1, s = gate_norm_bwd(x3, y1, tg1, row(norm_ffn[1]), tc2, dx4, dh4, "l1_dnorm2")
    s = s.reshape(4, 8, D).sum(axis=1)
    d_tg1, d_nffn1, d_t2, d_tc2 = s[0], s[1] * (1.0 + tc2[0]), s[2], s[1] * norm_ffn[1]
    G['rec_w_out'] = matmul([(ycat, dy1)], "tn", "rec_out_dw").reshape(N_DEV, D // N_DEV, D)
    dycat = matmul([(dy1, w_rec_out)], "nt", "rec_out_dx")
    du, s5cot, gs5 = s5_block_bwd(dycat[:, :S5_W], res_s5, s5p, s5_dskip, glu_w, glu_b, "s5")
    s5g = s5p_vjp(s5cot)
    dqkv, dz, da, dbraw, gdn = dn_block_bwd(dycat[:, S5_W:], res_dn, rin, dn_cw, dn_out_norm[0], "dn")
    drin = jnp.concatenate([du, dqkv, dz, da, dbraw, jnp.zeros((L, REC_PAD - REC_IN), F32)], axis=1).astype(BF16)
    G['rec_w_in'] = matmul([(h3, drin)], "tn", "rec_in_dw").reshape(N_DEV, D // N_DEV, REC_PAD)
    dh3 = matmul([(drin, w_rec_in)], "nt", "rec_in_dx")
    dx2, df0, s = gate_norm_bwd(x2, f0, g2, row(norm_mix[1]), tc1, dx3, dh3, "l1_dnorm1")
    s = s.reshape(4, 8, D).sum(axis=1)
    d_g2, d_nmix1, d_t1, d_tc1 = s[0], s[1] * (1.0 + tc1[0]), s[2], s[1] * norm_mix[1]
    dh2, gf0 = ffn_block_bwd(df0, res_f0, w_up[0][0], w_up[0][1], ffn_cw[0], w_down[0], "ffn0")
    dx1, dy0, s = gate_norm_bwd(x1, y0, g1, row(norm_ffn[0]), sc2, dx2, dh2, "l0_dnorm2")
    s = s.reshape(4, 8, D).sum(axis=1)
    d_g1, d_nffn0, d_sh2, d_sc2 = s[0], s[1] * (1.0 + sc2[0]), s[2], s[1] * norm_ffn[0]
    dh1, gatt = attention_block_bwd(dy0, res_att, w_att_in, wvec, sinkvec, w_att_out, "att")
    grad_x, s = gate_norm_bwd(x0, None, None, row(norm_mix[0]), sc1, dx1, dh1, "l0_dnorm1")
    s = s.reshape(4, 8, D).sum(axis=1)
    d_nmix0, d_sh1, d_sc1 = s[1] * (1.0 + sc1[0]), s[2], s[1] * norm_mix[0]
    dmod = jnp.stack([jnp.concatenate([d_sh1, d_sc1, d_g1, d_sh2, d_sc2, d_g2]),
                      jnp.concatenate([d_t1, d_tc1, d_tg1, d_t2, d_tc2, d_tg2])])

    G['attn_w_in'] = _cols_to_slabs(gatt['w_in'])
    G['attn_w_out'] = gatt['w_out'].reshape(N_DEV, D // N_DEV, D)
    G['s5_glu_w'] = gs5['glu_w'].reshape(N_DEV, S5_W // N_DEV, S5_W)
    G['ffn_w_up'] = jnp.concatenate([_cols_to_slabs(gf0['w_up']), _cols_to_slabs(gf1['w_up'])], axis=1)
    G['ffn_w_down'] = jnp.concatenate([gf0['w_down'].reshape(N_DEV, D_FF // N_DEV, D),
                                       gf1['w_down'].reshape(N_DEV, D_FF // N_DEV, D)], axis=1)
    P = {'ada_b': dmod, 'norm_mix': jnp.stack([d_nmix0, d_nmix1]), 'norm_ffn': jnp.stack([d_nffn0, d_nffn1]),
         'attn_q_norm_a': gatt['q_norm_a'], 'attn_k_norm_a': gatt['k_norm_a'], 'attn_q_norm_b': gatt['q_norm_b'],
         'attn_k_norm_b': gatt['k_norm_b'], 'attn_sinks': gatt['sinks'],
         's5_lambda_re': s5g[0], 's5_lambda_im': s5g[1], 's5_log_dt': s5g[2], 's5_b_re': s5g[3], 's5_b_im': s5g[4],
         's5_c_re': s5g[5], 's5_c_im': s5g[6], 'dn_a_log': gdn['a_log'], 'dn_dt_bias': gdn['dt_bias'],
         'dn_out_norm': gdn['out_norm'],
         's5_d': gs5['dskip'], 's5_glu_b': gs5['glu_b'], 'dn_conv': gdn['conv'],
         'ffn_conv': jnp.stack([gf0['conv'], gf1['conv']])}

    out = {k: {} for k in ("g", "d", "m", "v")}
    grecv = all_to_all_many([G[n] for n, _ in BIG], "exchange_grads")
    for (n, _), gr_ in zip(BIG, grecv):
        res4 = reduce_adamw(gr_, _shard2d(n, W[n]), _shard2d(n, M[n]), _shard2d(n, V[n]), "adamw_" + n)
        for key, t in zip(("g", "d", "m", "v"), res4):
            out[key][n] = (t[:, :REC_IN] if n == 'rec_w_in' else t).reshape(W[n].shape)

    rep_sizes = [_numel(shp) for _, shp in REPLICATED]
    ss_sizes = [_numel(shp) for _, _, shp in SMALL_SHARDED]
    rep_offs = _offsets(rep_sizes + ss_sizes + [1])
    parts = [P[n].reshape(-1) for n, _ in REPLICATED] + [P[n].reshape(-1) for n, _, _ in SMALL_SHARDED]
    parts.append(lsum[0:8].sum().reshape(1))
    spack = _pack_rows(jnp.concatenate(parts), 1024, 8)
    sall = all_gather(spack, "gather_small_grads")
    n_rest = sum(ss_sizes) + 1
    pk = lambda d: _pack_rows(jnp.concatenate([d[n].reshape(-1) for n, _ in REPLICATED]
                                              + [jnp.zeros((n_rest,), F32)]), 1024, 8)
    sg, sd_, sm, sv = [t.reshape(-1) for t in reduce_adamw(sall, pk(W), pk(M), pk(V), "adamw_small")]
    loss = 0.5 * sg[rep_offs[-1]] / D

    dmod_all = sall.reshape(N_DEV, -1)[:, :2 * 6 * D].reshape(N_DEV, 2, 6 * D)
    dmod_mine = lax.dynamic_slice_in_dim(dmod_all, me * (6 * D // N_DEV), 6 * D // N_DEV, axis=2)
    g_ada = jnp.stack([matmul([(cond_all, dmod_mine[:, l])], "tn", f"ada{l}_dw") for l in range(2)])
    ada2d = lambda t: t.reshape(2 * D, 6 * D // N_DEV)
    for key, t in zip(("g", "d", "m", "v"), reduce_adamw(ada2d(g_ada)[None], ada2d(ada_w), ada2d(m_ada_w),
                                                          ada2d(v_ada_w), "adamw_ada_w")):
        out[key]['ada_w'] = t.reshape(ada_w.shape)
    own = []
    for (n, ax, shp), o in zip(SMALL_SHARDED, rep_offs[len(REPLICATED):]):
        slabs = _to_slabs(sg[o:o + _numel(shp)].reshape(shp), ax)
        own.append(lax.dynamic_index_in_dim(slabs, me, axis=0, keepdims=False))
    own_names = [n for n, _, _ in SMALL_SHARDED]
    pk = lambda d: _pack_rows(jnp.concatenate([d[n].reshape(-1) for n in own_names]), 1024, 8)
    og, od, om, ov = [t.reshape(-1) for t in reduce_adamw(_pack_rows(jnp.concatenate(own), 1024, 8)[None],
                                                          pk(W), pk(M), pk(V), "adamw_own")]

    def unpack(names_shapes, bufs):
        o = 0
        for n, shp in names_shapes:
            k = _numel(shp)
            for key, buf in zip(("g", "d", "m", "v"), bufs):
                out[key][n] = buf[o:o + k].reshape(shp)
            o += k

    unpack(REPLICATED, (sg, sd_, sm, sv))
    unpack([(n, W[n].shape) for n in own_names], (og, od, om, ov))
    return (loss, grad_x[None], *[out["g"][n] for n in WEIGHTS], *[out["d"][n] for n in WEIGHTS],
            *[out["m"][n] for n in WEIGHTS], *[out["v"][n] for n in WEIGHTS])
```

```python
import functools
import math

import numpy as np
import jax
import jax.numpy as jnp
from jax import lax
from jax.experimental import pallas as pl
from jax.experimental.pallas import tpu as pltpu

F32 = jnp.float32
BF16 = jnp.bfloat16

N_DEV = 8
D = 1024
HD = 64
BLK = 128
ATTN_IN = 2304
CB = ATTN_IN // 128
B_BRANCHES = ((128, 1), (512, 4), (2048, 16))
S5_W = 256
S5_P = 1024
DN_H = 6
DN_DK = 128
DN_C = 64
REC_IN = 3340
REC_PAD = 3456
D_FF = 2816
EPS = 1e-6
ADAM_LR, ADAM_B1, ADAM_B2, ADAM_EPS, ADAM_WD, ADAM_STEP = 0.001, 0.9, 0.999, 1e-8, 0.01, 10
VMEM_LIMIT = 48 * 1024 * 1024

ALIBI = np.asarray(2.0 ** (-8.0 * np.arange(1, 17) / 16), dtype=np.float32)


def _cparams(*sem):
    return pltpu.CompilerParams(dimension_semantics=tuple(sem), vmem_limit_bytes=VMEM_LIMIT)


def _tile(n, target):
    if n <= target:
        return n
    best = None
    for t in range(128, target + 1, 128):
        if n % t == 0:
            best = t
    assert best is not None, (n, target)
    return best


def _rtile(n, target, mult=8):
    if n <= target:
        return n
    best = None
    for t in range(mult, target + 1, mult):
        if n % t == 0:
            best = t
    assert best is not None, (n, target)
    return best


def _fold8(x):
    r, c = x.shape
    return x.reshape(r // 8, 8, c).sum(axis=0)


def _sigmoid(x):
    return 1.0 / (1.0 + jnp.exp(-x))


_DIMS = {"nn": (((1,), (0,)), ((), ())), "nt": (((1,), (1,)), ((), ())), "tn": (((0,), (0,)), ((), ()))}


MM_FULL_K = 3584


def matmul(pairs, mode, name, out_dtype=F32, tm=512, tn=1536, tk=1024):
    a0, b0 = pairs[0]
    if mode == "nn":
        (M, K), N = a0.shape, b0.shape[1]
    elif mode == "nt":
        (M, K), N = a0.shape, b0.shape[0]
    else:
        (K, M), N = a0.shape, b0.shape[1]
    tm = _rtile(M, tm) if M % 128 else _tile(M, tm)
    tn = _tile(N, tn)
    tk = K if K <= MM_FULL_K else _tile(K, tk)
    nk = K // tk
    npair = len(pairs)
    dims = _DIMS[mode]

    def body(*refs):
        o_ref = refs[2 * npair]
        tot = None
        for p in range(npair):
            part = lax.dot_general(refs[2 * p][...].astype(BF16), refs[2 * p + 1][...].astype(BF16),
                                   dims, preferred_element_type=F32)
            tot = part if tot is None else tot + part
        if nk == 1:
            o_ref[...] = tot.astype(o_ref.dtype)
            return
        acc_ref = refs[2 * npair + 1]
        k = pl.program_id(2)

        @pl.when(k == 0)
        def _():
            acc_ref[...] = tot

        @pl.when(k > 0)
        def _():
            acc_ref[...] += tot

        @pl.when(k == nk - 1)
        def _():
            o_ref[...] = acc_ref[...].astype(o_ref.dtype)

    if mode == "nn":
        a_spec = pl.BlockSpec((tm, tk), lambda i, j, k: (i, k))
        b_spec = pl.BlockSpec((tk, tn), lambda i, j, k: (k, j))
    elif mode == "nt":
        a_spec = pl.BlockSpec((tm, tk), lambda i, j, k: (i, k))
        b_spec = pl.BlockSpec((tn, tk), lambda i, j, k: (j, k))
    else:
        a_spec = pl.BlockSpec((tk, tm), lambda i, j, k: (k, i))
        b_spec = pl.BlockSpec((tk, tn), lambda i, j, k: (k, j))
    flat = [t for pr in pairs for t in pr]
    return pl.pallas_call(
        body, name=name, grid=(M // tm, N // tn, nk),
        in_specs=[a_spec, b_spec] * npair,
        out_specs=pl.BlockSpec((tm, tn), lambda i, j, k: (i, j)),
        out_shape=jax.ShapeDtypeStruct((M, N), out_dtype),
        scratch_shapes=[pltpu.VMEM((tm, tn), F32)] if nk > 1 else [],
        compiler_params=_cparams("parallel", "parallel", "arbitrary"),
    )(*flat)


def gate_norm_fwd(x, y, gate, nw, sh, sc, name):
    L, C = x.shape
    tl = _rtile(L, 512)
    has_gate = y is not None

    def body(*refs):
        if has_gate:
            x_ref, y_ref, g_ref, nw_ref, sh_ref, sc_ref, xn_ref, h_ref = refs
            xn = x_ref[...] + g_ref[...] * y_ref[...]
            xn_ref[...] = xn
        else:
            x_ref, nw_ref, sh_ref, sc_ref, h_ref = refs
            xn = x_ref[...]
        r = lax.rsqrt(jnp.mean(xn * xn, axis=-1, keepdims=True) + EPS)
        h = (xn * r * nw_ref[...]) * (1.0 + sc_ref[...]) + sh_ref[...]
        h_ref[...] = h.astype(BF16)

    big = pl.BlockSpec((tl, C), lambda i: (i, 0))
    vec = pl.BlockSpec((1, C), lambda i: (0, 0))
    if has_gate:
        ins, in_specs = (x, y, gate, nw, sh, sc), [big, big, vec, vec, vec, vec]
        out_shape = (jax.ShapeDtypeStruct((L, C), F32), jax.ShapeDtypeStruct((L, C), BF16))
        out_specs = (big, big)
    else:
        ins, in_specs = (x, nw, sh, sc), [big, vec, vec, vec]
        out_shape = jax.ShapeDtypeStruct((L, C), BF16)
        out_specs = big
    return pl.pallas_call(body, name=name, grid=(L // tl,), in_specs=in_specs, out_specs=out_specs,
                          out_shape=out_shape, compiler_params=_cparams("parallel"))(*ins)


def gate_norm_bwd(xn, y, gate, nw, sc, dxn_direct, dh, name, comm=None):
    L, C = xn.shape
    tl = _rtile(L, 256)
    has_gate = y is not None
    has_direct = dxn_direct is not None

    def body(*refs):
        refs = list(refs)
        xn_ref = refs.pop(0)
        y_ref = refs.pop(0) if has_gate else None
        g_ref = refs.pop(0) if has_gate else None
        nw_ref = refs.pop(0)
        sc_ref = refs.pop(0)
        dd_ref = refs.pop(0) if has_direct else None
        dh_ref = refs.pop(0)
        dxn_ref = refs.pop(0)
        dy_ref = refs.pop(0) if has_gate else None
        sums_ref = refs.pop(0)

        @pl.when(pl.program_id(0) == 0)
        def _():
            sums_ref[...] = jnp.zeros_like(sums_ref)

        xv = xn_ref[...]
        dh_v = dh_ref[...]
        r = lax.rsqrt(jnp.mean(xv * xv, axis=-1, keepdims=True) + EPS)
        n = xv * r
        a = nw_ref[...] * (1.0 + sc_ref[...])
        dn = dh_v * a
        dx = r * (dn - n * jnp.mean(dn * n, axis=-1, keepdims=True))
        if has_direct:
            dx = dx + dd_ref[...]
        dxn_ref[...] = dx
        sums_ref[8:16, :] += _fold8(dh_v * n)
        sums_ref[16:24, :] += _fold8(dh_v)
        if has_gate:
            dy_ref[...] = (dx * g_ref[...]).astype(BF16)
            sums_ref[0:8, :] += _fold8(dx * y_ref[...])

    big = pl.BlockSpec((tl, C), lambda i: (i, 0))
    vec = pl.BlockSpec((1, C), lambda i: (0, 0))
    ins, in_specs = [xn], [big]
    if has_gate:
        ins += [y, gate]
        in_specs += [big, vec]
    ins += [nw, sc]
    in_specs += [vec, vec]
    if has_direct:
        ins.append(dxn_direct)
        in_specs.append(big)
    ins.append(dh)
    in_specs.append(big)
    out_shape = [jax.ShapeDtypeStruct((L, C), F32)]
    out_specs = [big]
    if has_gate:
        out_shape.append(jax.ShapeDtypeStruct((L, C), BF16))
        out_specs.append(big)
    out_shape.append(jax.ShapeDtypeStruct((32, C), F32))
    out_specs.append(pl.BlockSpec((32, C), lambda i: (0, 0)))
    return _call(body, ins, name=name, grid=(L // tl,), in_specs=in_specs, out_specs=tuple(out_specs),
                 out_shape=tuple(out_shape), sem=("arbitrary",), comm=comm)


def final_loss(x, f, gate, target, name):
    L, C = x.shape
    tl = _rtile(L, 256)

    def body(x_ref, f_ref, g_ref, t_ref, dy_ref, df_ref, sums_ref):
        @pl.when(pl.program_id(0) == 0)
        def _():
            sums_ref[...] = jnp.zeros_like(sums_ref)

        fv = f_ref[...]
        err = x_ref[...] + g_ref[...] * fv - t_ref[...]
        dy = err * (1.0 / C)
        dy_ref[...] = dy
        df_ref[...] = (dy * g_ref[...]).astype(BF16)
        sums_ref[0:8, :] += _fold8(err * err)
        sums_ref[8:16, :] += _fold8(dy * fv)

    big = pl.BlockSpec((tl, C), lambda i: (i, 0))
    vec = pl.BlockSpec((1, C), lambda i: (0, 0))
    return pl.pallas_call(
        body, name=name, grid=(L // tl,), in_specs=[big, big, vec, big],
        out_specs=(big, big, pl.BlockSpec((16, C), lambda i: (0, 0))),
        out_shape=(jax.ShapeDtypeStruct((L, C), F32), jax.ShapeDtypeStruct((L, C), BF16),
                   jax.ShapeDtypeStruct((16, C), F32)),
        compiler_params=_cparams("arbitrary"))(x, f, gate, target)


def _seg_ones(seg):
    r = lax.broadcasted_iota(jnp.int32, (128, 128), 0) // seg
    c = lax.broadcasted_iota(jnp.int32, (128, 128), 1) // seg
    return (r == c).astype(BF16)


def _segsum(t, ones):
    hi = t.astype(BF16)
    lo = (t - hi.astype(F32)).astype(BF16)
    return (jnp.dot(hi, ones, preferred_element_type=F32) + jnp.dot(lo, ones, preferred_element_type=F32))


_NORMED_TILES = tuple(list(range(0, 5)) + list(range(6, 14)))


def qknorm_fwd(qkv, wvec, name):
    L, C = qkv.shape
    tl = _rtile(L, 256)

    def body(x_ref, w_ref, o_ref):
        ones = _seg_ones(HD)
        for t in range(CB):
            cs = slice(t * 128, (t + 1) * 128)
            x = x_ref[:, cs]
            if t in _NORMED_TILES:
                ms = _segsum(x * x, ones) * (1.0 / HD)
                x = x * lax.rsqrt(ms + EPS) * w_ref[:, cs]
            o_ref[:, cs] = x.astype(BF16)

    return pl.pallas_call(
        body, name=name, grid=(L // tl,),
        in_specs=[pl.BlockSpec((tl, C), lambda i: (i, 0)), pl.BlockSpec((1, C), lambda i: (0, 0))],
        out_specs=pl.BlockSpec((tl, C), lambda i: (i, 0)),
        out_shape=jax.ShapeDtypeStruct((L, C), BF16), compiler_params=_cparams("parallel"))(qkv, wvec)


def qknorm_bwd(qkv, wvec, dy, name):
    L, C = qkv.shape
    tl = _rtile(L, 256)

    def body(x_ref, w_ref, dy_ref, dx_ref, sums_ref):
        @pl.when(pl.program_id(0) == 0)
        def _():
            sums_ref[...] = jnp.zeros_like(sums_ref)

        ones = _seg_ones(HD)
        for t in range(CB):
            cs = slice(t * 128, (t + 1) * 128)
            d = dy_ref[:, cs]
            if t in _NORMED_TILES:
                x = x_ref[:, cs]
                r = lax.rsqrt(_segsum(x * x, ones) * (1.0 / HD) + EPS)
                n = x * r
                dn = d * w_ref[:, cs]
                dx_ref[:, cs] = (r * (dn - n * (_segsum(dn * n, ones) * (1.0 / HD)))).astype(BF16)
                sums_ref[:, cs] += _fold8(d * n)
            else:
                dx_ref[:, cs] = d.astype(BF16)

    big = pl.BlockSpec((tl, C), lambda i: (i, 0))
    return pl.pallas_call(
        body, name=name, grid=(L // tl,),
        in_specs=[big, pl.BlockSpec((1, C), lambda i: (0, 0)), big],
        out_specs=(big, pl.BlockSpec((8, C), lambda i: (0, 0))),
        out_shape=(jax.ShapeDtypeStruct((L, C), BF16), jax.ShapeDtypeStruct((8, C), F32)),
        compiler_params=_cparams("arbitrary"))(qkv, wvec, dy)


def _attn_scores(q, kw, n, slope, step, maxdist):
    s = lax.dot_general(q, kw, (((1,), (1,)), ((), ())), preferred_element_type=F32) * (HD ** -0.5)
    qi = lax.broadcasted_iota(jnp.int32, (BLK, 2 * BLK), 0)
    sj = lax.broadcasted_iota(jnp.int32, (BLK, 2 * BLK), 1)
    dist = BLK + qi - sj
    valid = (dist >= 0) & (dist <= maxdist) & ((n > 0) | (sj >= BLK))
    bias = (-slope) * (step * dist).astype(F32)
    return jnp.where(valid, s + bias, -jnp.inf), valid


ATT_NQ = 4


def _attn_operands(hp, gqa, q_ref, kh_ref, kc_ref, vh_ref, vc_ref):
    ops = []
    for b in range(ATT_NQ):
        rows = slice(b * BLK, (b + 1) * BLK)
        prev = slice((b - 1) * BLK, b * BLK)
        for e in range(2):
            cs = slice(e * HD, (e + 1) * HD)
            if gqa:
                ksel = lambda ref, r: jnp.where(hp >= 2, ref[r, 64:128], ref[r, 0:64])
            else:
                ksel = lambda ref, r, cs=cs: ref[r, cs]
            kprev = ksel(kh_ref, slice(0, BLK)) if b == 0 else ksel(kc_ref, prev)
            vprev = ksel(vh_ref, slice(0, BLK)) if b == 0 else ksel(vc_ref, prev)
            ops.append((b, e, rows, cs, q_ref[rows, cs], jnp.concatenate([kprev, ksel(kc_ref, rows)], axis=0),
                        jnp.concatenate([vprev, ksel(vc_ref, rows)], axis=0)))
    return ops


def _attn_specs(d, q_off, k_off, v_off, gqa):
    kcol = (lambda r, hp: r * CB + k_off) if gqa else (lambda r, hp: r * CB + k_off + hp)
    vcol = (lambda r, hp: r * CB + v_off) if gqa else (lambda r, hp: r * CB + v_off + hp)
    return kcol, vcol


def attn_fwd(X, d, q_off, k_off, v_off, gqa, slope0, maxdist, name, comm=None):
    Ls = X.shape[0]
    TQ = ATT_NQ * BLK
    nt = Ls // TQ
    slopes = jnp.asarray(ALIBI)

    def body(sl_ref, q_ref, kh_ref, kc_ref, vh_ref, vc_ref, o_ref, lse_ref):
        hp, t = pl.program_id(1), pl.program_id(2)
        ops = _attn_operands(hp, gqa, q_ref, kh_ref, kc_ref, vh_ref, vc_ref)
        s = [_attn_scores(q, kw, t if b == 0 else 1, sl_ref[slope0 + 2 * hp + e], d, maxdist)[0]
             for (b, e, rows, cs, q, kw, vw) in ops]
        m = [jnp.max(x, axis=-1, keepdims=True) for x in s]
        p = [jnp.exp(x - mm) for x, mm in zip(s, m)]
        l = [jnp.sum(x, axis=-1, keepdims=True) for x in p]
        o = [jnp.dot(x.astype(BF16), op[6], preferred_element_type=F32) / ll for x, op, ll in zip(p, ops, l)]
        for (b, e, rows, cs, q, kw, vw), oo, mm, ll in zip(ops, o, m, l):
            o_ref[rows, cs] = oo
            lse_ref[rows, cs] = jnp.broadcast_to(mm + jnp.log(ll), (BLK, HD))

    kcol, vcol = _attn_specs(d, q_off, k_off, v_off, gqa)
    tile, blk = (TQ, 128), (BLK, 128)
    halo = lambda t: jnp.maximum(t * ATT_NQ - 1, 0)
    in_specs = [
        pl.BlockSpec(memory_space=pltpu.SMEM),
        pl.BlockSpec(tile, lambda r, hp, t: (t, r * CB + q_off + hp)),
        pl.BlockSpec(blk, lambda r, hp, t: (halo(t), kcol(r, hp))),
        pl.BlockSpec(tile, lambda r, hp, t: (t, kcol(r, hp))),
        pl.BlockSpec(blk, lambda r, hp, t: (halo(t), vcol(r, hp))),
        pl.BlockSpec(tile, lambda r, hp, t: (t, vcol(r, hp))),
    ]
    out_spec = pl.BlockSpec(tile, lambda r, hp, t: (t, r * 4 + hp))
    out = jax.ShapeDtypeStruct((Ls, d * 512), F32)
    return _call(body, (slopes, X, X, X, X, X), name=name, grid=(d, 4, nt), in_specs=in_specs,
                 out_specs=(out_spec, out_spec), out_shape=(out, out),
                 sem=("parallel", "parallel", "arbitrary"), comm=comm)


def attn_bwd(X, o, lse, do, dlse, d, q_off, k_off, v_off, gqa, slope0, maxdist, name, comm=None):
    Ls = X.shape[0]
    slopes = jnp.asarray(ALIBI)

    TQ = ATT_NQ * BLK
    nt = Ls // TQ
    nt_dims, tn_dims = (((1,), (1,)), ((), ())), (((0,), (0,)), ((), ()))

    def body(sl_ref, q_ref, kh_ref, kc_ref, vh_ref, vc_ref, o_ref, lse_ref, do_ref, dlse_ref,
             dq_ref, dk_ref, dv_ref, ak_ref, av_ref, pk_ref, pv_ref):
        hp, t = pl.program_id(1), pl.program_id(2)

        @pl.when(t == 0)
        def _():
            pk_ref[...] = jnp.zeros_like(pk_ref)
            pv_ref[...] = jnp.zeros_like(pv_ref)

        @pl.when(t < nt)
        def _():
            ops = _attn_operands(hp, gqa, q_ref, kh_ref, kc_ref, vh_ref, vc_ref)
            sv = [_attn_scores(q, kw, t if b == 0 else 1, sl_ref[slope0 + 2 * hp + e], d, maxdist)
                  for (b, e, rows, cs, q, kw, vw) in ops]
            p = [jnp.where(valid, jnp.exp(s - lse_ref[op[2], op[1] * HD:op[1] * HD + 1]), 0.0)
                 for (s, valid), op in zip(sv, ops)]
            dov = [do_ref[op[2], op[3]] for op in ops]
            delta = [jnp.sum(dd * o_ref[op[2], op[3]], axis=-1, keepdims=True) for dd, op in zip(dov, ops)]
            dob = [dd.astype(BF16) for dd in dov]
            dp = [lax.dot_general(dd, op[6], nt_dims, preferred_element_type=F32) for dd, op in zip(dob, ops)]
            ds = [(pp * (x - dl + dlse_ref[op[2], op[1] * HD:op[1] * HD + 1])).astype(BF16)
                  for pp, x, dl, op in zip(p, dp, delta, ops)]
            dq = [jnp.dot(x, op[5], preferred_element_type=F32) * (HD ** -0.5) for x, op in zip(ds, ops)]
            dkw = [lax.dot_general(x, op[4], tn_dims, preferred_element_type=F32) * (HD ** -0.5)
                   for x, op in zip(ds, ops)]
            dvw = [lax.dot_general(pp.astype(BF16), dd, tn_dims, preferred_element_type=F32)
                   for pp, dd in zip(p, dob)]
            ak_ref[...] = jnp.zeros_like(ak_ref)
            av_ref[...] = jnp.zeros_like(av_ref)
            for (b, e, rows, cs, q, kw, vw), x, yk, yv in zip(ops, dq, dkw, dvw):
                dq_ref[rows, cs] = x
                ak_ref[b * BLK:(b + 2) * BLK, cs] += yk
                av_ref[b * BLK:(b + 2) * BLK, cs] += yv
            last = slice(TQ - BLK, TQ)
            dk_ref[...] = pk_ref[...]
            dv_ref[...] = pv_ref[...]
            dk_ref[last, :] += ak_ref[0:BLK, :]
            dv_ref[last, :] += av_ref[0:BLK, :]
            pk_ref[...] = ak_ref[BLK:, :]
            pv_ref[...] = av_ref[BLK:, :]

        @pl.when(t == nt)
        def _():
            dk_ref[...] = pk_ref[...]
            dv_ref[...] = pv_ref[...]

    kcol, vcol = _attn_specs(d, q_off, k_off, v_off, gqa)
    tile, blk = (TQ, 128), (BLK, 128)
    cur = lambda t: jnp.minimum(t, nt - 1)
    halo = lambda t: jnp.maximum(cur(t) * ATT_NQ - 1, 0)
    ospec = pl.BlockSpec(tile, lambda r, hp, t: (cur(t), r * 4 + hp))
    in_specs = [
        pl.BlockSpec(memory_space=pltpu.SMEM),
        pl.BlockSpec(tile, lambda r, hp, t: (cur(t), r * CB + q_off + hp)),
        pl.BlockSpec(blk, lambda r, hp, t: (halo(t), kcol(r, hp))),
        pl.BlockSpec(tile, lambda r, hp, t: (cur(t), kcol(r, hp))),
        pl.BlockSpec(blk, lambda r, hp, t: (halo(t), vcol(r, hp))),
        pl.BlockSpec(tile, lambda r, hp, t: (cur(t), vcol(r, hp))),
        ospec, ospec, ospec, ospec,
    ]
    shifted = pl.BlockSpec(tile, lambda r, hp, t: (jnp.maximum(t - 1, 0), r * 4 + hp))
    out = jax.ShapeDtypeStruct((Ls, d * 512), F32)
    return _call(body, (slopes, X, X, X, X, X, o, lse, do, dlse), name=name, grid=(d, 4, nt + 1),
                 in_specs=in_specs, out_specs=(ospec, shifted, shifted), out_shape=(out, out, out),
                 scratch_shapes=[pltpu.VMEM((TQ + BLK, 128), F32), pltpu.VMEM((TQ + BLK, 128), F32),
                                 pltpu.VMEM((TQ, 128), F32), pltpu.VMEM((TQ, 128), F32)],
                 sem=("parallel", "parallel", "arbitrary"), comm=comm)


def attn_merge_fwd(oa, la, sink, obs, lbs, name):
    L = oa.shape[0]
    tl = _rtile(L, 256)

    def body(oa_ref, la_ref, sk_ref, o1, o2, o3, l1, l2, l3, m_ref):
        m_ref[:, 0:512] = (oa_ref[...] * _sigmoid(la_ref[...] - sk_ref[...])).astype(BF16)
        a, b, c = l1[...], l2[...], l3[...]
        mx = jnp.maximum(jnp.maximum(a, b), c)
        ea, eb, ec = jnp.exp(a - mx), jnp.exp(b - mx), jnp.exp(c - mx)
        inv = 1.0 / (ea + eb + ec)
        m_ref[:, 512:1024] = ((ea * inv) * o1[...] + (eb * inv) * o2[...] + (ec * inv) * o3[...]).astype(BF16)

    big = pl.BlockSpec((tl, 512), lambda i: (i, 0))
    return pl.pallas_call(
        body, name=name, grid=(L // tl,),
        in_specs=[big, big, pl.BlockSpec((1, 512), lambda i: (0, 0))] + [big] * 6,
        out_specs=pl.BlockSpec((tl, 1024), lambda i: (i, 0)),
        out_shape=jax.ShapeDtypeStruct((L, 1024), BF16), compiler_params=_cparams("parallel"),
    )(oa, la, sink, *obs, *lbs)


def attn_merge_bwd(dm, oa, la, sink, obs, lbs, name):
    L = oa.shape[0]
    tl = _rtile(L, 256)

    def body(dm_ref, oa_ref, la_ref, sk_ref, o1, o2, o3, l1, l2, l3,
             doa_ref, dla_ref, d1, d2, d3, g1, g2, g3, sums_ref):
        @pl.when(pl.program_id(0) == 0)
        def _():
            sums_ref[...] = jnp.zeros_like(sums_ref)

        ones = _seg_ones(HD)
        for t in range(4):
            cs = slice(t * 128, (t + 1) * 128)
            dma = dm_ref[:, cs]
            keep = _sigmoid(la_ref[:, cs] - sk_ref[:, cs])
            doa_ref[:, cs] = dma * keep
            tt = dma * oa_ref[:, cs] * keep * (1.0 - keep)
            dla_ref[:, cs] = _segsum(tt, ones)
            sums_ref[:, cs] += _fold8(-tt)
            dmb = dm_ref[:, 512 + t * 128:512 + (t + 1) * 128]
            a, b, c = l1[:, cs], l2[:, cs], l3[:, cs]
            mx = jnp.maximum(jnp.maximum(a, b), c)
            ea, eb, ec = jnp.exp(a - mx), jnp.exp(b - mx), jnp.exp(c - mx)
            inv = 1.0 / (ea + eb + ec)
            wa, wb, wc = ea * inv, eb * inv, ec * inv
            d1[:, cs] = wa * dmb
            d2[:, cs] = wb * dmb
            d3[:, cs] = wc * dmb
            sa = _segsum(dmb * o1[:, cs], ones)
            sb = _segsum(dmb * o2[:, cs], ones)
            sc_ = _segsum(dmb * o3[:, cs], ones)
            mean = wa * sa + wb * sb + wc * sc_
            g1[:, cs] = wa * (sa - mean)
            g2[:, cs] = wb * (sb - mean)
            g3[:, cs] = wc * (sc_ - mean)

    big = pl.BlockSpec((tl, 512), lambda i: (i, 0))
    o512 = jax.ShapeDtypeStruct((L, 512), F32)
    return pl.pallas_call(
        body, name=name, grid=(L // tl,),
        in_specs=[pl.BlockSpec((tl, 1024), lambda i: (i, 0)), big, big,
                  pl.BlockSpec((1, 512), lambda i: (0, 0))] + [big] * 6,
        out_specs=tuple([big] * 8 + [pl.BlockSpec((8, 512), lambda i: (0, 0))]),
        out_shape=tuple([o512] * 8 + [jax.ShapeDtypeStruct((8, 512), F32)]),
        compiler_params=_cparams("arbitrary"),
    )(dm, oa, la, sink, *obs, *lbs)


def _shift_down(x, halo, k, first):
    tl = x.shape[0]
    rows = lax.broadcasted_iota(jnp.int32, x.shape, 0)
    out = pltpu.roll(x, k, axis=0)
    for j in range(k):
        hrow = jnp.where(first, 0.0, halo[8 - k + j:8 - k + j + 1, :])
        out = jnp.where(rows == j, hrow, out)
    return out


def _shift_up(x, nxt, k):
    tl = x.shape[0]
    rows = lax.broadcasted_iota(jnp.int32, x.shape, 0)
    out = pltpu.roll(x, tl - k, axis=0)
    for j in range(k):
        out = jnp.where(rows == tl - k + j, nxt[j:j + 1, :], out)
    return out


def _silu(x):
    return x * _sigmoid(x)


def _dsilu(x):
    s = _sigmoid(x)
    return s * (1.0 + x * (1.0 - s))


def ffn_act_fwd(ua, ub, cw, name):
    L, F = ua.shape
    tl = _rtile(L, 256)
    tc = _tile(F, 1408)
    hb = tl // 8

    def body(ua_ref, uah_ref, ub_ref, ubh_ref, wa_ref, wb_ref, o_ref):
        first = pl.program_id(1) == 0

        def conv(x_ref, h_ref, w_ref):
            x = x_ref[...]
            h = h_ref[...]
            return (w_ref[2:3, :] * x + w_ref[1:2, :] * _shift_down(x, h, 1, first)
                    + w_ref[0:1, :] * _shift_down(x, h, 2, first))

        a = conv(ua_ref, uah_ref, wa_ref)
        b = conv(ub_ref, ubh_ref, wb_ref)
        o_ref[...] = (_silu(a) * b).astype(BF16)

    main = pl.BlockSpec((tl, tc), lambda j, i: (i, j))
    halo = pl.BlockSpec((8, tc), lambda j, i: (jnp.maximum(i * hb - 1, 0), j))
    wa = pl.BlockSpec((3, tc), lambda j, i: (0, j))
    wb = pl.BlockSpec((3, tc), lambda j, i: (0, j + F // tc))
    return pl.pallas_call(
        body, name=name, grid=(F // tc, L // tl), in_specs=[main, halo, main, halo, wa, wb],
        out_specs=main, out_shape=jax.ShapeDtypeStruct((L, F), BF16),
        compiler_params=_cparams("parallel", "parallel"))(ua, ua, ub, ub, cw, cw)


def ffn_act_bwd(ua, ub, cw, dact, name, comm=None):
    L, F = ua.shape
    tl = _rtile(L, 256)
    tc = _tile(F, 1408)
    hb = tl // 8
    nrt = L // tl

    def body(ua_ref, uah_ref, ub_ref, ubh_ref, wa_ref, wb_ref, da_ref, dua_ref, dub_ref, sums_ref, ca_ref, cb_ref):
        i = pl.program_id(1)
        first = i == nrt - 1

        @pl.when(i == 0)
        def _():
            sums_ref[...] = jnp.zeros_like(sums_ref)
            ca_ref[...] = jnp.zeros_like(ca_ref)
            cb_ref[...] = jnp.zeros_like(cb_ref)

        def taps(x_ref, h_ref):
            x = x_ref[...]
            h = h_ref[...]
            return x, _shift_down(x, h, 1, first), _shift_down(x, h, 2, first)

        a0, a1, a2 = taps(ua_ref, uah_ref)
        b0, b1, b2 = taps(ub_ref, ubh_ref)
        a = wa_ref[2:3, :] * a0 + wa_ref[1:2, :] * a1 + wa_ref[0:1, :] * a2
        b = wb_ref[2:3, :] * b0 + wb_ref[1:2, :] * b1 + wb_ref[0:1, :] * b2
        dact_v = da_ref[...]
        dya = dact_v * b * _dsilu(a)
        dyb = dact_v * _silu(a)
        for (dy, w_ref, c_ref, d_ref, xs, base) in ((dya, wa_ref, ca_ref, dua_ref, (a2, a1, a0), 0),
                                                     (dyb, wb_ref, cb_ref, dub_ref, (b2, b1, b0), 24)):
            nxt = c_ref[...]
            d_ref[...] = (w_ref[2:3, :] * dy + w_ref[1:2, :] * _shift_up(dy, nxt, 1)
                          + w_ref[0:1, :] * _shift_up(dy, nxt, 2)).astype(BF16)
            c_ref[...] = dy[0:8, :]
            for j in range(3):
                sums_ref[base + 8 * j:base + 8 * j + 8, :] += _fold8(dy * xs[j])

    rev = lambda i: nrt - 1 - i
    main = pl.BlockSpec((tl, tc), lambda j, i: (rev(i), j))
    halo = pl.BlockSpec((8, tc), lambda j, i: (jnp.maximum(rev(i) * hb - 1, 0), j))
    wa = pl.BlockSpec((3, tc), lambda j, i: (0, j))
    wb = pl.BlockSpec((3, tc), lambda j, i: (0, j + F // tc))
    ob = jax.ShapeDtypeStruct((L, F), BF16)
    return _call(body, (ua, ua, ub, ub, cw, cw, dact), name=name, grid=(F // tc, nrt),
                 in_specs=[main, halo, main, halo, wa, wb, main],
                 out_specs=(main, main, pl.BlockSpec((48, tc), lambda j, i: (0, j))),
                 out_shape=(ob, ob, jax.ShapeDtypeStruct((48, F), F32)),
                 scratch_shapes=[pltpu.VMEM((8, tc), F32), pltpu.VMEM((8, tc), F32)],
                 sem=("parallel", "arbitrary"), comm=comm)


def attn_vectors(qna, kna, qnb, knb, sinks):
    ones = jnp.ones((128,), F32)
    wvec = jnp.concatenate([jnp.tile(qna, 8), jnp.tile(kna, 2), ones, jnp.tile(qnb, 8), jnp.tile(knb, 8),
                            jnp.tile(ones, 4)]).reshape(1, ATTN_IN)
    return wvec, jnp.repeat(sinks, HD).reshape(1, 512)


def _with_comm(result, comm):
    return result if comm is not None else (result, None)


def attention_block_fwd(h, w_in, wvec, sinkvec, w_out, tag, comms=None):
    L = h.shape[0]
    comms = comms or {}
    got = {}
    qkv = matmul([(h, w_in)], "nn", tag + "_qkv")
    X = qknorm_fwd(qkv, wvec, tag + "_qknorm")
    (oa, la), got['swa'] = _with_comm(attn_fwd(X, 1, 0, 4, 5, True, 0, BLK - 1, tag + "_swa",
                                               comm=comms.get('swa')), comms.get('swa'))
    obs, lbs = [], []
    for window, d in B_BRANCHES:
        (o, l), got[d] = _with_comm(attn_fwd(X.reshape(L // d, d * ATTN_IN), d, 6, 10, 14, False, 8, window // d,
                                             tag + f"_dil{d}", comm=comms.get(d)), comms.get(d))
        obs.append(o.reshape(L, 512))
        lbs.append(l.reshape(L, 512))
    m = attn_merge_fwd(oa, la, sinkvec, obs, lbs, tag + "_merge")
    y = matmul([(m, w_out)], "nn", tag + "_out")
    return y, (h, qkv, X, oa, la, obs, lbs, m), got


def attention_block_bwd(dy, res, w_in, wvec, sinkvec, w_out, tag, comm=None):
    h, qkv, X, oa, la, obs, lbs, m = res
    L = h.shape[0]
    g_w_out = matmul([(m, dy)], "tn", tag + "_dwout", out_dtype=BF16)
    dm = matmul([(dy, w_out)], "nt", tag + "_dm")
    doa, dla, d1, d2, d3, g1, g2, g3, sinksums = attn_merge_bwd(dm, oa, la, sinkvec, obs, lbs, tag + "_dmerge")
    (dqa, dka, dva), got = _with_comm(attn_bwd(X, oa, la, doa, dla, 1, 0, 4, 5, True, 0, BLK - 1, tag + "_dswa",
                                               comm=comm), comm)
    dqb = dkb = dvb = None
    for (window, d), o, l, do, dl in zip(B_BRANCHES, obs, lbs, (d1, d2, d3), (g1, g2, g3)):
        shp = (L // d, d * 512)
        dq, dk, dv = attn_bwd(X.reshape(L // d, d * ATTN_IN), o.reshape(shp), l.reshape(shp), do.reshape(shp),
                              dl.reshape(shp), d, 6, 10, 14, False, 8, window // d, tag + f"_ddil{d}")
        dq, dk, dv = dq.reshape(L, 512), dk.reshape(L, 512), dv.reshape(L, 512)
        dqb, dkb, dvb = (dq, dk, dv) if dqb is None else (dqb + dq, dkb + dk, dvb + dv)
    fold = lambda t: t.reshape(L, 2, 4, HD).sum(axis=2).reshape(L, 128)
    dX = jnp.concatenate([dqa, fold(dka), fold(dva), dqb, dkb, dvb], axis=1)
    dqkv, wsums = qknorm_bwd(qkv, wvec, dX, tag + "_dqknorm")
    g_w_in = matmul([(h, dqkv)], "tn", tag + "_dwin", out_dtype=BF16)
    dh = matmul([(dqkv, w_in)], "nt", tag + "_dh")
    ws = wsums.sum(axis=0)
    grads = dict(
        w_in=g_w_in, w_out=g_w_out,
        q_norm_a=ws[0:512].reshape(8, HD).sum(axis=0), k_norm_a=ws[512:640].reshape(2, HD).sum(axis=0),
        q_norm_b=ws[768:1280].reshape(8, HD).sum(axis=0), k_norm_b=ws[1280:1792].reshape(8, HD).sum(axis=0),
        sinks=sinksums.sum(axis=0).reshape(8, HD).sum(axis=1))
    return dh, grads, got


def ffn_block_fwd(h, w_up_a, w_up_b, cw, w_down, tag):
    ua = matmul([(h, w_up_a)], "nn", tag + "_upa")
    ub = matmul([(h, w_up_b)], "nn", tag + "_upb")
    act = ffn_act_fwd(ua, ub, cw, tag + "_act")
    f = matmul([(act, w_down)], "nn", tag + "_down")
    return f, (h, ua, ub, act)


def ffn_block_bwd(df, res, w_up_a, w_up_b, cw, w_down, tag, comm=None):
    h, ua, ub, act = res
    g_down = matmul([(act, df)], "tn", tag + "_dwdown", out_dtype=BF16)
    dact = matmul([(df, w_down)], "nt", tag + "_dact")
    (dua, dub, sums), got = _with_comm(ffn_act_bwd(ua, ub, cw, dact, tag + "_dactk", comm=comm), comm)
    g_up = jnp.concatenate([_cols_to_slabs(matmul([(h, dua)], "tn", tag + "_dwupa", out_dtype=BF16), N_DEV // 2),
                            _cols_to_slabs(matmul([(h, dub)], "tn", tag + "_dwupb", out_dtype=BF16), N_DEV // 2)],
                           axis=0)
    dh = matmul([(dua, w_up_a), (dub, w_up_b)], "nt", tag + "_dh")
    s = sums.reshape(2, 3, 8, D_FF).sum(axis=2)
    g_conv = jnp.concatenate([s[0], s[1]], axis=1)
    return dh, dict(w_up=g_up, conv=g_conv, w_down=g_down), got


def s5_params(lam_re, lam_im, log_dt, b_re, b_im, c_re, c_im):
    dt = jnp.exp(log_dt)[:, None]
    mag, ang = jnp.exp(lam_re * dt), lam_im * dt
    a_re, a_im = mag * jnp.cos(ang), mag * jnp.sin(ang)
    nr, ni = a_re - 1.0, a_im
    den = lam_re * lam_re + lam_im * lam_im
    f_re = (nr * lam_re + ni * lam_im) / den
    f_im = (ni * lam_re - nr * lam_im) / den
    eye = jnp.eye(16, dtype=F32)[:, None, :, None]
    bd = lambda b: (eye * jnp.transpose(b, (0, 2, 1))[:, :, None, :]).reshape(S5_W, S5_P)
    cd = lambda c: (eye * jnp.transpose(c, (0, 2, 1))[:, :, None, :]).reshape(S5_P, S5_W)
    flat = lambda t: t.reshape(1, S5_P)
    return flat(a_re), flat(a_im), flat(f_re), flat(f_im), bd(b_re), bd(b_im), cd(c_re), cd(c_im)


def _scan_tables(a_re, a_im, reverse):
    pows = [(a_re, a_im)]
    for _ in range(7):
        pr, pi = pows[-1]
        pows.append((pr * a_re - pi * a_im, pr * a_im + pi * a_re))
    order = list(range(7, -1, -1)) if reverse else list(range(8))
    z = jnp.zeros_like(a_re)
    rows = [pows[0][0], pows[0][1], pows[1][0], pows[1][1], pows[3][0], pows[3][1], z, z]
    rows += [pows[k][0] for k in order] + [pows[k][1] for k in order]
    return jnp.concatenate(rows, axis=0)


def _block_scan(er, ei, tab_ref, cr, ci, reverse):
    rows = lax.broadcasted_iota(jnp.int32, er.shape, 0)
    for idx, s in enumerate((1, 2, 4)):
        if reverse:
            sr, si, keep = pltpu.roll(er, 8 - s, axis=0), pltpu.roll(ei, 8 - s, axis=0), rows < 8 - s
        else:
            sr, si, keep = pltpu.roll(er, s, axis=0), pltpu.roll(ei, s, axis=0), rows >= s
        sr, si = jnp.where(keep, sr, 0.0), jnp.where(keep, si, 0.0)
        ar, ai = tab_ref[2 * idx:2 * idx + 1, :], tab_ref[2 * idx + 1:2 * idx + 2, :]
        er, ei = er + ar * sr - ai * si, ei + ar * si + ai * sr
    pr, pi_ = tab_ref[8:16, :], tab_ref[16:24, :]
    er, ei = er + pr * cr - pi_ * ci, ei + pr * ci + pi_ * cr
    return er, ei


def s5_scan_fwd(bu_re, bu_im, a_re, a_im, f_re, f_im, name):
    L, P = bu_re.shape
    tl = _rtile(L, 512)
    tab = _scan_tables(a_re, a_im, False)
    fvec = jnp.concatenate([f_re, f_im] + [jnp.zeros_like(f_re)] * 6, axis=0)

    def body(br_ref, bi_ref, tab_ref, f_ref, xr_ref, xi_ref, c_ref):
        @pl.when(pl.program_id(0) == 0)
        def _():
            c_ref[...] = jnp.zeros_like(c_ref)

        def blk(i, carry):
            cr, ci = carry
            rows = pl.ds(pl.multiple_of(i * 8, 8), 8)
            br, bi = br_ref[rows, :], bi_ref[rows, :]
            fr, fi = f_ref[0:1, :], f_ref[1:2, :]
            er, ei = _block_scan(fr * br - fi * bi, fr * bi + fi * br, tab_ref, cr, ci, False)
            xr_ref[rows, :] = er
            xi_ref[rows, :] = ei
            return er[7:8, :], ei[7:8, :]

        cr, ci = lax.fori_loop(0, tl // 8, blk, (c_ref[0:1, :], c_ref[1:2, :]))
        c_ref[0:1, :] = cr
        c_ref[1:2, :] = ci

    big = pl.BlockSpec((tl, P), lambda i: (i, 0))
    out = jax.ShapeDtypeStruct((L, P), F32)
    return pl.pallas_call(
        body, name=name, grid=(L // tl,),
        in_specs=[big, big, pl.BlockSpec((24, P), lambda i: (0, 0)), pl.BlockSpec((8, P), lambda i: (0, 0))],
        out_specs=(big, big), out_shape=(out, out), scratch_shapes=[pltpu.VMEM((8, P), F32)],
        compiler_params=_cparams("arbitrary"))(bu_re, bu_im, tab, fvec)


def s5_scan_bwd(dx_re, dx_im, x_re, x_im, bu_re, bu_im, a_re, a_im, f_re, f_im, name):
    L, P = dx_re.shape
    tl = _rtile(L, 256)
    nt = L // tl
    tab = _scan_tables(a_re, -a_im, True)
    fvec = jnp.concatenate([f_re, f_im] + [jnp.zeros_like(f_re)] * 6, axis=0)

    def body(gr_ref, gi_ref, xr_ref, xi_ref, br_ref, bi_ref, tab_ref, f_ref, dbr_ref, dbi_ref, s_ref, c_ref):
        @pl.when(pl.program_id(0) == 0)
        def _():
            c_ref[...] = jnp.zeros_like(c_ref)
            s_ref[...] = jnp.zeros_like(s_ref)

        def blk(k, carry):
            cr, ci = carry
            i = tl // 8 - 1 - k
            rows = pl.ds(pl.multiple_of(i * 8, 8), 8)
            er, ei = _block_scan(gr_ref[rows, :], gi_ref[rows, :], tab_ref, cr, ci, True)
            rid = lax.broadcasted_iota(jnp.int32, er.shape, 0)
            sr = jnp.where(rid == 7, cr, pltpu.roll(er, 7, axis=0))
            si = jnp.where(rid == 7, ci, pltpu.roll(ei, 7, axis=0))
            xr, xi = xr_ref[rows, :], xi_ref[rows, :]
            s_ref[0:8, :] += sr * xr + si * xi
            s_ref[8:16, :] += si * xr - sr * xi
            br, bi = br_ref[rows, :], bi_ref[rows, :]
            s_ref[16:24, :] += er * br + ei * bi
            s_ref[24:32, :] += ei * br - er * bi
            fr, fi = f_ref[0:1, :], f_ref[1:2, :]
            dbr_ref[rows, :] = fr * er + fi * ei
            dbi_ref[rows, :] = fr * ei - fi * er
            return er[0:1, :], ei[0:1, :]

        cr, ci = lax.fori_loop(0, tl // 8, blk, (c_ref[0:1, :], c_ref[1:2, :]))
        c_ref[0:1, :] = cr
        c_ref[1:2, :] = ci

    big = pl.BlockSpec((tl, P), lambda i: (nt - 1 - i, 0))
    out = jax.ShapeDtypeStruct((L, P), F32)
    return pl.pallas_call(
        body, name=name, grid=(nt,),
        in_specs=[big] * 6 + [pl.BlockSpec((24, P), lambda i: (0, 0)), pl.BlockSpec((8, P), lambda i: (0, 0))],
        out_specs=(big, big, pl.BlockSpec((32, P), lambda i: (0, 0))),
        out_shape=(out, out, jax.ShapeDtypeStruct((32, P), F32)), scratch_shapes=[pltpu.VMEM((8, P), F32)],
        compiler_params=_cparams("arbitrary"))(dx_re, dx_im, x_re, x_im, bu_re, bu_im, tab, fvec)


_GK, _GC = math.sqrt(2.0 / math.pi), 0.044715


def _gelu(y):
    return 0.5 * y * (1.0 + jnp.tanh(_GK * (y + _GC * y * y * y)))


def _dgelu(y):
    t = jnp.tanh(_GK * (y + _GC * y * y * y))
    return 0.5 * (1.0 + t) + 0.5 * y * (1.0 - t * t) * _GK * (1.0 + 3.0 * _GC * y * y)


def s5_out_fwd(x_re, x_im, u, cd_re, cd_im, dskip, glu_w, glu_b, name):
    L = u.shape[0]
    tl = _rtile(L, 512)

    def body(xr_ref, xi_ref, u_ref, cr_ref, ci_ref, d_ref, w_ref, b_ref, y_ref, o_ref):
        y = (jnp.dot(xr_ref[...].astype(BF16), cr_ref[...], preferred_element_type=F32)
             - jnp.dot(xi_ref[...].astype(BF16), ci_ref[...], preferred_element_type=F32)
             + d_ref[...] * u_ref[...])
        y_ref[...] = y
        g = _gelu(y)
        z = jnp.dot(g.astype(BF16), w_ref[...], preferred_element_type=F32) + b_ref[...]
        o_ref[...] = (g * _sigmoid(z)).astype(BF16)

    big = pl.BlockSpec((tl, S5_P), lambda i: (i, 0))
    sm = pl.BlockSpec((tl, S5_W), lambda i: (i, 0))
    full = lambda r, c: pl.BlockSpec((r, c), lambda i: (0, 0))
    return pl.pallas_call(
        body, name=name, grid=(L // tl,),
        in_specs=[big, big, sm, full(S5_P, S5_W), full(S5_P, S5_W), full(1, S5_W), full(S5_W, S5_W), full(1, S5_W)],
        out_specs=(sm, sm),
        out_shape=(jax.ShapeDtypeStruct((L, S5_W), F32), jax.ShapeDtypeStruct((L, S5_W), BF16)),
        compiler_params=_cparams("parallel"))(x_re, x_im, u, cd_re, cd_im, dskip, glu_w, glu_b)


def s5_out_bwd(dout, y, u, x_re, x_im, cd_re, cd_im, dskip, glu_w, glu_b, name):
    L = u.shape[0]
    tl = _rtile(L, 256)
    nt_dims = (((1,), (1,)), ((), ()))
    tn_dims = (((0,), (0,)), ((), ()))

    def body(do_ref, y_ref, u_ref, xr_ref, xi_ref, cr_ref, ci_ref, d_ref, w_ref, b_ref,
             dxr_ref, dxi_ref, du_ref, dcr_ref, dci_ref, dw_ref, s_ref):
        @pl.when(pl.program_id(0) == 0)
        def _():
            dcr_ref[...] = jnp.zeros_like(dcr_ref)
            dci_ref[...] = jnp.zeros_like(dci_ref)
            dw_ref[...] = jnp.zeros_like(dw_ref)
            s_ref[...] = jnp.zeros_like(s_ref)

        yv, dov = y_ref[...], do_ref[...]
        g = _gelu(yv)
        gb = g.astype(BF16)
        sg = _sigmoid(jnp.dot(gb, w_ref[...], preferred_element_type=F32) + b_ref[...])
        dz = dov * g * sg * (1.0 - sg)
        dzb = dz.astype(BF16)
        dg = dov * sg + lax.dot_general(dzb, w_ref[...], nt_dims, preferred_element_type=F32)
        dw_ref[...] += lax.dot_general(gb, dzb, tn_dims, preferred_element_type=F32)
        dy = dg * _dgelu(yv)
        dyb = dy.astype(BF16)
        s_ref[0:8, :] += _fold8(dy * u_ref[...])
        s_ref[8:16, :] += _fold8(dz)
        du_ref[...] = dy * d_ref[...]
        dxr_ref[...] = lax.dot_general(dyb, cr_ref[...], nt_dims, preferred_element_type=F32)
        dxi_ref[...] = -lax.dot_general(dyb, ci_ref[...], nt_dims, preferred_element_type=F32)
        dcr_ref[...] += lax.dot_general(xr_ref[...].astype(BF16), dyb, tn_dims, preferred_element_type=F32)
        dci_ref[...] -= lax.dot_general(xi_ref[...].astype(BF16), dyb, tn_dims, preferred_element_type=F32)

    big = pl.BlockSpec((tl, S5_P), lambda i: (i, 0))
    sm = pl.BlockSpec((tl, S5_W), lambda i: (i, 0))
    full = lambda r, c: pl.BlockSpec((r, c), lambda i: (0, 0))
    sd = jax.ShapeDtypeStruct
    return pl.pallas_call(
        body, name=name, grid=(L // tl,),
        in_specs=[sm, sm, sm, big, big, full(S5_P, S5_W), full(S5_P, S5_W), full(1, S5_W), full(S5_W, S5_W),
                  full(1, S5_W)],
        out_specs=(big, big, sm, full(S5_P, S5_W), full(S5_P, S5_W), full(S5_W, S5_W), full(16, S5_W)),
        out_shape=(sd((L, S5_P), F32), sd((L, S5_P), F32), sd((L, S5_W), F32), sd((S5_P, S5_W), F32),
                   sd((S5_P, S5_W), F32), sd((S5_W, S5_W), F32), sd((16, S5_W), F32)),
        compiler_params=_cparams("arbitrary"))(dout, y, u, x_re, x_im, cd_re, cd_im, dskip, glu_w, glu_b)


def s5_block_fwd(u, params, dskip, glu_w, glu_b, tag):
    a_re, a_im, f_re, f_im, bd_re, bd_im, cd_re, cd_im = params
    bu_re = matmul([(u, bd_re.astype(BF16))], "nn", tag + "_bure")
    bu_im = matmul([(u, bd_im.astype(BF16))], "nn", tag + "_buim")
    x_re, x_im = s5_scan_fwd(bu_re, bu_im, a_re, a_im, f_re, f_im, tag + "_scan")
    y, out = s5_out_fwd(x_re, x_im, u, cd_re.astype(BF16), cd_im.astype(BF16), dskip, glu_w, glu_b, tag + "_out")
    return out, (u, bu_re, bu_im, x_re, x_im, y)


def s5_block_bwd(dout, res, params, dskip, glu_w, glu_b, tag):
    u, bu_re, bu_im, x_re, x_im, y = res
    a_re, a_im, f_re, f_im, bd_re, bd_im, cd_re, cd_im = params
    dxr, dxi, du, dcr, dci, dglu_w, sums = s5_out_bwd(dout, y, u, x_re, x_im, cd_re.astype(BF16), cd_im.astype(BF16),
                                                      dskip, glu_w, glu_b, tag + "_dout")
    dbr, dbi, acc = s5_scan_bwd(dxr, dxi, x_re, x_im, bu_re, bu_im, a_re, a_im, f_re, f_im, tag + "_dscan")
    du = du + matmul([(dbr, bd_re.astype(BF16)), (dbi, bd_im.astype(BF16))], "nt", tag + "_du")
    dbd_re = matmul([(u, dbr)], "tn", tag + "_dbdre")
    dbd_im = matmul([(u, dbi)], "tn", tag + "_dbdim")
    acc = acc.reshape(4, 8, S5_P).sum(axis=1)
    s = sums.reshape(2, 8, S5_W).sum(axis=1)
    cot = (acc[0:1], acc[1:2], acc[2:3], acc[3:4], dbd_re, dbd_im, dcr, dci)
    return du, cot, dict(dskip=s[0], glu_w=dglu_w, glu_b=s[1])


DN_QKV0, DN_Z0, DN_NT = 2, 20, 18


def _l2n(s, j):
    r = lax.rsqrt(jnp.sum(s * s, axis=-1, keepdims=True) + EPS)
    scale = jnp.where(j < DN_H, DN_DK ** -0.5, 1.0)
    return r, scale


def dn_prep_fwd(rin, cw, name):
    L = rin.shape[0]
    tl = _rtile(L, 512)
    hb = tl // 8

    def body(x_ref, h_ref, w_ref, o_ref):
        j = pl.program_id(0)
        first = pl.program_id(1) == 0
        x, h = x_ref[...], h_ref[...]
        xc = w_ref[3:4, :] * x
        for k in range(1, 4):
            xc = xc + w_ref[3 - k:4 - k, :] * _shift_down(x, h, k, first)
        s = _silu(xc)
        r, scale = _l2n(s, j)
        o_ref[...] = jnp.where(j < 2 * DN_H, s * r * scale, s)

    main = pl.BlockSpec((tl, 128), lambda j, i: (i, DN_QKV0 + j))
    halo = pl.BlockSpec((8, 128), lambda j, i: (jnp.maximum(i * hb - 1, 0), DN_QKV0 + j))
    return pl.pallas_call(
        body, name=name, grid=(DN_NT, L // tl),
        in_specs=[main, halo, pl.BlockSpec((4, 128), lambda j, i: (0, j))],
        out_specs=pl.BlockSpec((tl, 128), lambda j, i: (i, j)),
        out_shape=jax.ShapeDtypeStruct((L, DN_NT * 128), F32),
        compiler_params=_cparams("parallel", "parallel"))(rin, rin, cw)


def dn_prep_bwd(rin, cw, dout, name):
    L = rin.shape[0]
    tl = _rtile(L, 512)
    hb = tl // 8
    nrt = L // tl

    def body(x_ref, h_ref, w_ref, d_ref, dx_ref, s_ref, c_ref):
        j = pl.program_id(0)
        i = pl.program_id(1)
        first = i == nrt - 1

        @pl.when(i == 0)
        def _():
            s_ref[...] = jnp.zeros_like(s_ref)
            c_ref[...] = jnp.zeros_like(c_ref)

        x, h = x_ref[...], h_ref[...]
        taps = [x] + [_shift_down(x, h, k, first) for k in range(1, 4)]
        xc = w_ref[3:4, :] * x
        for k in range(1, 4):
            xc = xc + w_ref[3 - k:4 - k, :] * taps[k]
        s = _silu(xc)
        r, scale = _l2n(s, j)
        n = s * r
        dn = d_ref[...] * scale
        ds_norm = r * (dn - n * jnp.sum(dn * n, axis=-1, keepdims=True))
        ds = jnp.where(j < 2 * DN_H, ds_norm, d_ref[...])
        dxc = ds * _dsilu(xc)
        nxt = c_ref[...]
        dx = w_ref[3:4, :] * dxc
        for k in range(1, 4):
            dx = dx + w_ref[3 - k:4 - k, :] * _shift_up(dxc, nxt, k)
        dx_ref[...] = dx
        c_ref[...] = dxc[0:8, :]
        for k in range(4):
            s_ref[8 * (3 - k):8 * (3 - k) + 8, :] += _fold8(dxc * taps[k])

    rev = lambda i: nrt - 1 - i
    main = pl.BlockSpec((tl, 128), lambda j, i: (rev(i), DN_QKV0 + j))
    halo = pl.BlockSpec((8, 128), lambda j, i: (jnp.maximum(rev(i) * hb - 1, 0), DN_QKV0 + j))
    own = pl.BlockSpec((tl, 128), lambda j, i: (rev(i), j))
    return pl.pallas_call(
        body, name=name, grid=(DN_NT, nrt),
        in_specs=[main, halo, pl.BlockSpec((4, 128), lambda j, i: (0, j)), own],
        out_specs=(own, pl.BlockSpec((32, 128), lambda j, i: (0, j))),
        out_shape=(jax.ShapeDtypeStruct((L, DN_NT * 128), F32), jax.ShapeDtypeStruct((32, DN_NT * 128), F32)),
        scratch_shapes=[pltpu.VMEM((8, 128), F32)],
        compiler_params=_cparams("parallel", "arbitrary"))(rin, rin, cw, dout)


_HI = lax.Precision.HIGH
_NT = (((1,), (1,)), ((), ()))
_TN = (((0,), (0,)), ((), ()))
_HEADS = tuple(range(DN_H))


def _mm(a, b, dims=(((1,), (0,)), ((), ())), hi=False):
    if hi:
        return lax.dot_general(a, b, dims, precision=_HI, preferred_element_type=F32)
    return lax.dot_general(a.astype(BF16), b.astype(BF16), dims, preferred_element_type=F32)


def _dn_masks():
    ri = lax.broadcasted_iota(jnp.int32, (DN_C, DN_C), 0)
    ci = lax.broadcasted_iota(jnp.int32, (DN_C, DN_C), 1)
    return ri >= ci, ri > ci, (ri == ci).astype(F32)


def _dn_decay(gc, gr, causal):
    gam = [jnp.where(causal, jnp.exp(jnp.where(causal, gc[h] - gr[h], 0.0)), 0.0) for h in _HEADS]
    eg = [jnp.exp(gc[h]) for h in _HEADS]
    el = [jnp.exp(gc[h][DN_C - 1:DN_C, :] - gc[h]) for h in _HEADS]
    gl = [jnp.exp(gc[h][DN_C - 1:DN_C, :]) for h in _HEADS]
    return gam, eg, el, gl


def _dn_solve(k, v, beta, gam, eg, kk, strict, eye):
    nmat = [jnp.where(strict, beta[h] * kk[h] * gam[h], 0.0) for h in _HEADS]
    t = [eye - nmat[h] for h in _HEADS]
    m = [_mm(nmat[h], nmat[h], hi=True) for h in _HEADS]
    for step in range(5):
        t = [t[h] + _mm(t[h], m[h], hi=True) for h in _HEADS]
        if step < 4:
            m = [_mm(m[h], m[h], hi=True) for h in _HEADS]
    rhs = [jnp.concatenate([v[h] * beta[h], k[h] * (beta[h] * eg[h])], axis=1) for h in _HEADS]
    sol = [_mm(t[h], rhs[h], hi=True) for h in _HEADS]
    return t, sol


def dn_chunk_fwd(qkv, gcol, grow, bcol, name):
    L = qkv.shape[0]
    C, W = DN_C, DN_H * DN_DK
    ncb = 8
    tl = ncb * C
    nchunks = L // C

    def body(q_ref, k_ref, v_ref, gc_ref, gr_ref, b_ref, o_ref, sh_ref, t_ref, sol_ref, s_ref):
        @pl.when(pl.program_id(0) == 0)
        def _():
            s_ref[...] = jnp.zeros_like(s_ref)

        causal, strict, eye = _dn_masks()

        def chunk(c, _):
            rows = pl.ds(pl.multiple_of(c * C, C), C)
            grow_c = gr_ref[c]
            hs = lambda h: slice(h * 128, (h + 1) * 128)
            q = [q_ref[rows, hs(h)] for h in _HEADS]
            k = [k_ref[rows, hs(h)] for h in _HEADS]
            v = [v_ref[rows, hs(h)] for h in _HEADS]
            gc = [gc_ref[rows, h:h + 1] for h in _HEADS]
            gr = [grow_c[h:h + 1, :] for h in _HEADS]
            beta = [b_ref[rows, h:h + 1] for h in _HEADS]
            gam, eg, el, gl = _dn_decay(gc, gr, causal)
            kk = [_mm(k[h], k[h], _NT) for h in _HEADS]
            t, sol = _dn_solve(k, v, beta, gam, eg, kk, strict, eye)
            qk = [_mm(q[h], k[h], _NT) * gam[h] for h in _HEADS]
            S = [s_ref[hs(h), :] for h in _HEADS]
            vn = [sol[h][:, :128] - _mm(sol[h][:, 128:], S[h]) for h in _HEADS]
            o = [_mm(q[h] * eg[h], S[h]) + _mm(qk[h], vn[h]) for h in _HEADS]
            Sn = [S[h] * gl[h] + _mm(k[h] * el[h], vn[h], _TN) for h in _HEADS]
            for h in _HEADS:
                sh_ref[c, hs(h), :] = S[h]
                s_ref[hs(h), :] = Sn[h]
                o_ref[rows, hs(h)] = o[h]
                t_ref[rows, h * C:(h + 1) * C] = t[h]
                sol_ref[rows, h * 256:(h + 1) * 256] = sol[h]
            return 0

        lax.fori_loop(0, ncb, chunk, 0)

    col = lambda b: pl.BlockSpec((tl, W), lambda i: (i, b))
    small = pl.BlockSpec((tl, 8), lambda i: (i, 0))
    rowblk = lambda w: pl.BlockSpec((tl, w), lambda i: (i, 0))
    sd = jax.ShapeDtypeStruct
    return pl.pallas_call(
        body, name=name, grid=(L // tl,),
        in_specs=[col(0), col(1), col(2), small, pl.BlockSpec((ncb, 8, C), lambda i: (i, 0, 0)), small],
        out_specs=(rowblk(W), pl.BlockSpec((ncb, W, 128), lambda i: (i, 0, 0)), rowblk(DN_H * C), rowblk(DN_H * 256)),
        out_shape=(sd((L, W), F32), sd((nchunks, W, 128), F32), sd((L, DN_H * C), F32), sd((L, DN_H * 256), F32)),
        scratch_shapes=[pltpu.VMEM((W, 128), F32)],
        compiler_params=_cparams("arbitrary"))(qkv, qkv, qkv, gcol, grow, bcol)


def dn_chunk_bwd(qkv, gcol, grow, bcol, shist, thist, solhist, do, name, comm=None):
    L = qkv.shape[0]
    C, W = DN_C, DN_H * DN_DK
    ncb = 8
    tl = ncb * C
    nchunks = L // C
    nt = L // tl

    def body(q_ref, k_ref, v_ref, gc_ref, gr_ref, b_ref, sh_ref, t_ref, sol_ref, do_ref,
             dqkv_ref, dgc_ref, dgr_ref, db_ref, ds_ref):
        @pl.when(pl.program_id(0) == 0)
        def _():
            ds_ref[...] = jnp.zeros_like(ds_ref)

        lane8 = lax.broadcasted_iota(jnp.int32, (C, 8), 1)
        sub8 = lax.broadcasted_iota(jnp.int32, (8, C), 0)
        rowid = lax.broadcasted_iota(jnp.int32, (C, 1), 0)
        causal, strict, _ = _dn_masks()
        rsum = lambda a: jnp.sum(a, axis=1, keepdims=True)

        def chunk(cc, _):
            c = ncb - 1 - cc
            rows = pl.ds(pl.multiple_of(c * C, C), C)
            grow_c = gr_ref[c]
            hs = lambda h: slice(h * 128, (h + 1) * 128)
            q = [q_ref[rows, hs(h)] for h in _HEADS]
            k = [k_ref[rows, hs(h)] for h in _HEADS]
            v = [v_ref[rows, hs(h)] for h in _HEADS]
            gc = [gc_ref[rows, h:h + 1] for h in _HEADS]
            gr = [grow_c[h:h + 1, :] for h in _HEADS]
            beta = [b_ref[rows, h:h + 1] for h in _HEADS]
            t = [t_ref[rows, h * C:(h + 1) * C] for h in _HEADS]
            sol = [sol_ref[rows, h * 256:(h + 1) * 256] for h in _HEADS]
            S = [sh_ref[c, hs(h), :] for h in _HEADS]
            dS = [ds_ref[hs(h), :] for h in _HEADS]
            dov = [do_ref[rows, hs(h)] for h in _HEADS]
            gam, eg, el, gl = _dn_decay(gc, gr, causal)
            kk = [_mm(k[h], k[h], _NT) for h in _HEADS]
            qk_raw = [_mm(q[h], k[h], _NT) for h in _HEADS]
            w = [sol[h][:, 128:] for h in _HEADS]
            kd = [k[h] * el[h] for h in _HEADS]
            vn = [sol[h][:, :128] - _mm(w[h], S[h]) for h in _HEADS]
            dvn = [_mm(qk_raw[h] * gam[h], dov[h], _TN) + _mm(kd[h], dS[h]) for h in _HEADS]
            dqd = [_mm(dov[h], S[h], _NT) for h in _HEADS]
            dqk = [jnp.where(causal, _mm(dov[h], vn[h], _NT), 0.0) for h in _HEADS]
            dkd = [_mm(vn[h], dS[h], _NT) for h in _HEADS]
            dgl = [jnp.sum(rsum(dS[h] * S[h]), axis=0, keepdims=True) for h in _HEADS]
            dw = [-_mm(dvn[h], S[h], _NT) for h in _HEADS]
            dSn = [dS[h] * gl[h] + _mm(q[h] * eg[h], dov[h], _TN) - _mm(w[h], dvn[h], _TN) for h in _HEADS]
            drhs = [_mm(t[h], jnp.concatenate([dvn[h], dw[h]], axis=1), _TN, hi=True) for h in _HEADS]
            dn = [jnp.where(strict, -_mm(drhs[h], sol[h], _NT, hi=True), 0.0) for h in _HEADS]
            dgc_all = jnp.zeros((C, 8), F32)
            db_all = jnp.zeros((C, 8), F32)
            dgr_all = jnp.zeros((8, C), F32)
            for h in _HEADS:
                drv, drk = drhs[h][:, :128], drhs[h][:, 128:]
                t2 = rsum(drk * k[h])
                x = dn[h] * gam[h]
                dbeta = rsum(drv * v[h]) + t2 * eg[h] + rsum(x * kk[h])
                dkk = x * beta[h]
                draw = dqk[h] * gam[h]
                mm_ = (dn[h] * beta[h] * kk[h] + dqk[h] * qk_raw[h]) * gam[h]
                deg = t2 * beta[h] + rsum(dqd[h] * q[h])
                r_ = rsum(dkd[h] * k[h]) * el[h]
                dglast = jnp.sum(r_, axis=0, keepdims=True) + dgl[h] * gl[h]
                dgc = rsum(mm_) + deg * eg[h] - r_ + jnp.where(rowid == C - 1, dglast, 0.0)
                dgr = -jnp.sum(mm_, axis=0, keepdims=True)
                dqkv_ref[rows, hs(h)] = _mm(draw, k[h]) + dqd[h] * eg[h]
                dqkv_ref[rows, hs(DN_H + h)] = (drk * (beta[h] * eg[h]) + _mm(dkk, k[h]) + _mm(dkk, k[h], _TN)
                                                + _mm(draw, q[h], _TN) + dkd[h] * el[h])
                dqkv_ref[rows, hs(2 * DN_H + h)] = drv * beta[h]
                ds_ref[hs(h), :] = dSn[h]
                dgc_all = dgc_all + jnp.where(lane8 == h, dgc, 0.0)
                db_all = db_all + jnp.where(lane8 == h, dbeta, 0.0)
                dgr_all = dgr_all + jnp.where(sub8 == h, dgr, 0.0)
            dgc_ref[rows, :] = dgc_all
            db_ref[rows, :] = db_all
            dgr_ref[c] = dgr_all
            return 0

        lax.fori_loop(0, ncb, chunk, 0)

    rev = lambda i: nt - 1 - i
    col = lambda b: pl.BlockSpec((tl, W), lambda i: (rev(i), b))
    rowblk = lambda w: pl.BlockSpec((tl, w), lambda i: (rev(i), 0))
    small = pl.BlockSpec((tl, 8), lambda i: (rev(i), 0))
    g3 = pl.BlockSpec((ncb, 8, C), lambda i: (rev(i), 0, 0))
    sd = jax.ShapeDtypeStruct
    return _call(body, (qkv, qkv, qkv, gcol, grow, bcol, shist, thist, solhist, do), name=name, grid=(nt,),
                 in_specs=[col(0), col(1), col(2), small, g3, small,
                           pl.BlockSpec((ncb, W, 128), lambda i: (rev(i), 0, 0)), rowblk(DN_H * C),
                           rowblk(DN_H * 256), col(0)],
                 out_specs=(rowblk(3 * W), small, g3, small),
                 out_shape=(sd((L, 3 * W), F32), sd((L, 8), F32), sd((nchunks, 8, C), F32), sd((L, 8), F32)),
                 scratch_shapes=[pltpu.VMEM((W, 128), F32)], sem=("arbitrary",), comm=comm)


def dn_out_fwd(o, rin, nw, name):
    L = o.shape[0]
    tl = _rtile(L, 512)

    def body(o_ref, z_ref, w_ref, y_ref):
        ov = o_ref[...]
        r = lax.rsqrt(jnp.mean(ov * ov, axis=-1, keepdims=True) + EPS)
        y_ref[...] = (ov * r * w_ref[...] * _silu(z_ref[...])).astype(BF16)

    own = pl.BlockSpec((tl, 128), lambda j, i: (i, j))
    return pl.pallas_call(
        body, name=name, grid=(DN_H, L // tl),
        in_specs=[own, pl.BlockSpec((tl, 128), lambda j, i: (i, DN_Z0 + j)), pl.BlockSpec((1, 128), lambda j, i: (0, 0))],
        out_specs=own, out_shape=jax.ShapeDtypeStruct((L, DN_H * 128), BF16),
        compiler_params=_cparams("parallel", "parallel"))(o, rin, nw)


def dn_out_bwd(dy, o, rin, nw, name):
    L = o.shape[0]
    tl = _rtile(L, 512)

    def body(dy_ref, o_ref, z_ref, w_ref, do_ref, dz_ref, s_ref):
        @pl.when(pl.program_id(1) == 0)
        def _():
            s_ref[...] = jnp.zeros_like(s_ref)

        ov, zv, d = o_ref[...], z_ref[...], dy_ref[...]
        r = lax.rsqrt(jnp.mean(ov * ov, axis=-1, keepdims=True) + EPS)
        n = ov * r
        dnw = d * _silu(zv)
        dz_ref[...] = d * n * w_ref[...] * _dsilu(zv)
        dn = dnw * w_ref[...]
        do_ref[...] = r * (dn - n * jnp.mean(dn * n, axis=-1, keepdims=True))
        s_ref[...] += _fold8(dnw * n)

    own = pl.BlockSpec((tl, 128), lambda j, i: (i, j))
    sd = jax.ShapeDtypeStruct
    return pl.pallas_call(
        body, name=name, grid=(DN_H, L // tl),
        in_specs=[own, own, pl.BlockSpec((tl, 128), lambda j, i: (i, DN_Z0 + j)),
                  pl.BlockSpec((1, 128), lambda j, i: (0, 0))],
        out_specs=(own, own, pl.BlockSpec((8, 128), lambda j, i: (0, j))),
        out_shape=(sd((L, DN_H * 128), F32), sd((L, DN_H * 128), F32), sd((8, DN_H * 128), F32)),
        compiler_params=_cparams("parallel", "arbitrary"))(dy, o, rin, nw)


def dn_gates(a, beta_raw, a_log, dt_bias):
    L = a.shape[0]
    beta = jax.nn.sigmoid(beta_raw)
    g = -jnp.exp(a_log) * jax.nn.softplus(a + dt_bias)
    G = jnp.cumsum(g.reshape(L // DN_C, DN_C, DN_H), axis=1)
    pad = lambda t: jnp.pad(t, ((0, 0), (0, 8 - DN_H)))
    gcol = pad(G.reshape(L, DN_H))
    grow = jnp.pad(jnp.transpose(G, (0, 2, 1)), ((0, 0), (0, 8 - DN_H), (0, 0)))
    return gcol, grow, pad(beta)


def dn_block_fwd(rin, cw, a_log, dt_bias, out_norm, tag):
    gates, gates_vjp = jax.vjp(dn_gates, rin[:, 3328:3334], rin[:, 3334:3340], a_log, dt_bias)
    qkv = dn_prep_fwd(rin, cw, tag + "_prep")
    o, shist, thist, solhist = dn_chunk_fwd(qkv, *gates, tag + "_chunk")
    yd = dn_out_fwd(o, rin, out_norm.reshape(1, 128), tag + "_onorm")
    return yd, (qkv, gates, gates_vjp, o, shist, thist, solhist)


def dn_block_bwd(dyd, res, rin, cw, out_norm, tag, comm=None):
    qkv, gates, gates_vjp, o, shist, thist, solhist = res
    do, dz, nsum = dn_out_bwd(dyd, o, rin, out_norm.reshape(1, 128), tag + "_donorm")
    (dqkv, dgc, dgr, db), got = _with_comm(dn_chunk_bwd(qkv, *gates, shist, thist, solhist, do, tag + "_dchunk",
                                                        comm=comm), comm)
    da, dbraw, g_alog, g_dtb = gates_vjp((dgc, dgr, db))
    dx, csum = dn_prep_bwd(rin, cw, dqkv, tag + "_dprep")
    grads = dict(conv=csum.reshape(4, 8, DN_NT * 128).sum(axis=1), a_log=g_alog, dt_bias=g_dtb,
                 out_norm=nsum.sum(axis=0).reshape(DN_H, 128).sum(axis=0))
    return dx, dz, da, dbraw, grads, got


_HBM = pl.BlockSpec(memory_space=pltpu.HBM)


def _mesh_pos():
    xi, yi, ci = lax.axis_index("x"), lax.axis_index("y"), lax.axis_index("c")
    return xi, yi, ci, 4 * xi + 2 * yi + ci


def _peer(xi, yi, ci, k):
    px = 1 - xi if (k >> 2) & 1 else xi
    py = 1 - yi if (k >> 1) & 1 else yi
    pc = 1 - ci if k & 1 else ci
    return (px, py, pc), 4 * px + 2 * py + pc


def _exchange(xs, gather, name):
    n = len(xs)

    def body(*refs):
        copies = _comm_copies(refs[:n], refs[n:2 * n], *refs[2 * n:], gather)
        for cp in copies:
            cp.start()
        for cp in copies:
            cp.wait()

    return pl.pallas_call(
        body, name=name, in_specs=[_HBM] * n, out_specs=tuple([_HBM] * n),
        out_shape=_comm_out_shapes(xs), scratch_shapes=_comm_sems(n))(*xs)


def _comm_out_shapes(xs):
    return tuple(jax.ShapeDtypeStruct((N_DEV,) + x.shape[-2:], x.dtype) for x in xs)


def _comm_sems(n):
    return [pltpu.SemaphoreType.DMA((n * (N_DEV - 1),)), pltpu.SemaphoreType.DMA((n * (N_DEV - 1),)),
            pltpu.SemaphoreType.DMA((n,))]


def _comm_copies(x_refs, o_refs, send_sems, recv_sems, lsems, gather):
    xi, yi, ci, me = _mesh_pos()
    copies = []
    for t in range(len(x_refs)):
        src_of = (lambda lin, t=t: x_refs[t]) if gather else (lambda lin, t=t: x_refs[t].at[lin])
        copies.append(pltpu.make_async_copy(src_of(me), o_refs[t].at[me], lsems.at[t]))
        for k in range(1, N_DEV):
            peer, lin = _peer(xi, yi, ci, k)
            s = t * (N_DEV - 1) + k - 1
            copies.append(pltpu.make_async_remote_copy(
                src_ref=src_of(lin), dst_ref=o_refs[t].at[me], send_sem=send_sems.at[s],
                recv_sem=recv_sems.at[s], device_id=peer, device_id_type=pl.DeviceIdType.MESH))
    return copies


def _call(body, args, *, name, grid, in_specs, out_specs, out_shape, scratch_shapes=(), sem, comm=None):
    if comm is None:
        return pl.pallas_call(body, name=name, grid=grid, in_specs=in_specs, out_specs=out_specs,
                              out_shape=out_shape, scratch_shapes=list(scratch_shapes),
                              compiler_params=_cparams(*sem))(*args)
    xs, gather = comm
    n = len(xs)
    single = not isinstance(out_shape, (tuple, list))
    outs_shape = (out_shape,) if single else tuple(out_shape)
    outs_specs = (out_specs,) if single else tuple(out_specs)
    n_in, n_out, n_scr = len(in_specs), len(outs_shape), len(scratch_shapes)

    def body2(*refs):
        ins, cx = refs[:n_in], refs[n_in:n_in + n]
        outs = refs[n_in + n:n_in + n + n_out]
        co = refs[n_in + n + n_out:n_in + 2 * n + n_out]
        scr = refs[n_in + 2 * n + n_out:n_in + 2 * n + n_out + n_scr]
        sems = refs[n_in + 2 * n + n_out + n_scr:]
        first = functools.reduce(jnp.logical_and, [pl.program_id(a) == 0 for a in range(len(grid))])
        last = functools.reduce(jnp.logical_and, [pl.program_id(a) == grid[a] - 1 for a in range(len(grid))])

        @pl.when(first)
        def _():
            for cp in _comm_copies(cx, co, *sems, gather):
                cp.start()

        body(*ins, *outs, *scr)

        @pl.when(last)
        def _():
            for cp in _comm_copies(cx, co, *sems, gather):
                cp.wait()

    res = pl.pallas_call(
        body2, name=name, grid=grid, in_specs=list(in_specs) + [_HBM] * n,
        out_specs=outs_specs + tuple([_HBM] * n), out_shape=outs_shape + _comm_out_shapes(xs),
        scratch_shapes=list(scratch_shapes) + _comm_sems(n),
        compiler_params=_cparams(*(["arbitrary"] * len(grid))))(*args, *xs)
    main = res[0] if single else tuple(res[:n_out])
    return main, list(res[n_out:])


def all_gather(x, name):
    return _exchange([x], True, name)[0]


def all_gather_many(xs, name):
    return _exchange(xs, True, name)


def all_to_all_many(xs, name):
    return _exchange(xs, False, name)


def reduce_adamw(gsrc, w, m, v, name):
    S, R, C = gsrc.shape
    tr = _rtile(R, max(16, min(256, (4 << 20) // (S * C * 4) // 16 * 16)), 16 if R % 16 == 0 else 8)
    c1 = 1.0 - ADAM_B1 ** ADAM_STEP
    c2 = 1.0 - ADAM_B2 ** ADAM_STEP

    def body(g_ref, w_ref, m_ref, v_ref, go_ref, d_ref, mo_ref, vo_ref):
        g = g_ref[0].astype(F32)
        for s in range(1, S):
            g = g + g_ref[s].astype(F32)
        go_ref[...] = g
        mn = ADAM_B1 * m_ref[...] + (1.0 - ADAM_B1) * g
        vn = ADAM_B2 * v_ref[...] + (1.0 - ADAM_B2) * (g * g)
        mo_ref[...] = mn
        vo_ref[...] = vn
        d_ref[...] = -ADAM_LR * ((mn / c1) / (jnp.sqrt(vn / c2) + ADAM_EPS) + ADAM_WD * w_ref[...])

    big = pl.BlockSpec((tr, C), lambda i: (i, 0))
    o = jax.ShapeDtypeStruct((R, C), F32)
    return pl.pallas_call(
        body, name=name, grid=(R // tr,),
        in_specs=[pl.BlockSpec((S, tr, C), lambda i: (0, i, 0)), big, big, big],
        out_specs=(big, big, big, big), out_shape=(o, o, o, o),
        compiler_params=_cparams("parallel"))(gsrc, w, m, v)


def _to_slabs(g, ax):
    shp = g.shape
    g = g.reshape(shp[:ax] + (N_DEV, shp[ax] // N_DEV) + shp[ax + 1:])
    return jnp.moveaxis(g, ax, 0).reshape(N_DEV, -1)


def _from_slabs(s, ax, shp):
    s = s.reshape((N_DEV,) + shp[:ax] + (shp[ax] // N_DEV,) + shp[ax + 1:])
    return jnp.moveaxis(s, 0, ax).reshape(shp)


def _pack_rows(flat, width, row_mult):
    n = flat.shape[-1]
    per = width * row_mult
    tot = -(-n // per) * per
    flat = jnp.pad(flat, [(0, 0)] * (flat.ndim - 1) + [(0, tot - n)])
    return flat.reshape(flat.shape[:-1] + (tot // width, width))


def _offsets(sizes):
    offs, o = [], 0
    for s in sizes:
        offs.append(o)
        o += s
    return offs


WEIGHTS = ['ada_w', 'ada_b', 'norm_mix', 'norm_ffn', 'attn_w_in', 'attn_q_norm_a', 'attn_k_norm_a', 'attn_q_norm_b',
           'attn_k_norm_b', 'attn_sinks', 'attn_w_out', 'rec_w_in', 's5_lambda_re', 's5_lambda_im', 's5_log_dt',
           's5_b_re', 's5_b_im', 's5_c_re', 's5_c_im', 's5_d', 's5_glu_w', 's5_glu_b', 'dn_conv', 'dn_a_log',
           'dn_dt_bias', 'dn_out_norm', 'rec_w_out', 'ffn_w_up', 'ffn_conv', 'ffn_w_down']
BIG = [('attn_w_in', (D, ATTN_IN // N_DEV)), ('attn_w_out', (D // N_DEV, D)), ('rec_w_in', (D // N_DEV, REC_PAD)),
       ('s5_glu_w', (S5_W // N_DEV, S5_W)), ('rec_w_out', (D // N_DEV, D)), ('ffn_w_up', (2 * D, 2 * D_FF // N_DEV)),
       ('ffn_w_down', (2 * D_FF // N_DEV, D))]


def _shard2d(name, t):
    if name == 'rec_w_in':
        return jnp.pad(t[0], ((0, 0), (0, REC_PAD - REC_IN)))
    return t.reshape((-1, t.shape[-1]))


def _cols_to_slabs(g, k=N_DEV):
    r, n = g.shape
    return jnp.transpose(g.reshape(r, k, n // k), (1, 0, 2))


def _slabs_to_cols(s):
    k, r, c_ = s.shape
    return jnp.transpose(s, (1, 0, 2)).reshape(r, k * c_)
SMALL_SHARDED = [('s5_d', 1, (1, S5_W)), ('s5_glu_b', 1, (1, S5_W)), ('dn_conv', 2, (1, 4, 2304)),
                 ('ffn_conv', 2, (2, 3, 2 * D_FF))]
REPLICATED = [('ada_b', (2, 6 * D)), ('norm_mix', (2, D)), ('norm_ffn', (2, D)), ('attn_q_norm_a', (1, HD)),
              ('attn_k_norm_a', (1, HD)), ('attn_q_norm_b', (1, HD)), ('attn_k_norm_b', (1, HD)),
              ('attn_sinks', (1, 8)), ('s5_lambda_re', (1, 16, 64)), ('s5_lambda_im', (1, 16, 64)),
              ('s5_log_dt', (1, 16)), ('s5_b_re', (1, 16, 64, 16)), ('s5_b_im', (1, 16, 64, 16)),
              ('s5_c_re', (1, 16, 16, 64)), ('s5_c_im', (1, 16, 16, 64)), ('dn_a_log', (1, DN_H)),
              ('dn_dt_bias', (1, DN_H)), ('dn_out_norm', (1, 128))]


def _numel(shp):
    return int(np.prod(shp))


def kernel(x, c, ada_w, ada_b, norm_mix, norm_ffn, attn_w_in, attn_q_norm_a, attn_k_norm_a, attn_q_norm_b, attn_k_norm_b, attn_sinks, attn_w_out, rec_w_in, s5_lambda_re, s5_lambda_im, s5_log_dt, s5_b_re, s5_b_im, s5_c_re, s5_c_im, s5_d, s5_glu_w, s5_glu_b, dn_conv, dn_a_log, dn_dt_bias, dn_out_norm, rec_w_out, ffn_w_up, ffn_conv, ffn_w_down, loss_target, m_ada_w, m_ada_b, m_norm_mix, m_norm_ffn, m_attn_w_in, m_attn_q_norm_a, m_attn_k_norm_a, m_attn_q_norm_b, m_attn_k_norm_b, m_attn_sinks, m_attn_w_out, m_rec_w_in, m_s5_lambda_re, m_s5_lambda_im, m_s5_log_dt, m_s5_b_re, m_s5_b_im, m_s5_c_re, m_s5_c_im, m_s5_d, m_s5_glu_w, m_s5_glu_b, m_dn_conv, m_dn_a_log, m_dn_dt_bias, m_dn_out_norm, m_rec_w_out, m_ffn_w_up, m_ffn_conv, m_ffn_w_down, v_ada_w, v_ada_b, v_norm_mix, v_norm_ffn, v_attn_w_in, v_attn_q_norm_a, v_attn_k_norm_a, v_attn_q_norm_b, v_attn_k_norm_b, v_attn_sinks, v_attn_w_out, v_rec_w_in, v_s5_lambda_re, v_s5_lambda_im, v_s5_log_dt, v_s5_b_re, v_s5_b_im, v_s5_c_re, v_s5_c_im, v_s5_d, v_s5_glu_w, v_s5_glu_b, v_dn_conv, v_dn_a_log, v_dn_dt_bias, v_dn_out_norm, v_rec_w_out, v_ffn_w_up, v_ffn_conv, v_ffn_w_down):
    loc = locals()
    W = {n: loc[n] for n in WEIGHTS}
    M = {n: loc["m_" + n] for n in WEIGHTS}
    V = {n: loc["v_" + n] for n in WEIGHTS}
    _, _, _, me = _mesh_pos()
    L = x.shape[1]
    x0, tgt = x[0], loss_target[0]

    small_in = jnp.concatenate([c.reshape(-1)] + [W[n].reshape(-1) for n, _, _ in SMALL_SHARDED])
    si, att_in_all, att_out_all = all_gather_many(
        [_pack_rows(small_in, 1024, 8), attn_w_in[0].astype(BF16), attn_w_out[0].astype(BF16)], "gather_first")
    si = si.reshape(N_DEV, -1)
    c_all = si[:, :D]
    off = D
    small_full = {}
    for n, ax, shp in SMALL_SHARDED:
        k = _numel(shp) // N_DEV
        small_full[n] = _from_slabs(si[:, off:off + k], ax, shp)
        off += k

    cond_all = jax.nn.silu(c_all)
    modp = jnp.concatenate([matmul([(cond_all, ada_w[l].astype(BF16))], "nn", f"ada{l}") for l in range(2)], axis=0)
    modp_all = all_gather(modp, "gather_mod")
    mods = []
    for l in range(2):
        row = lax.dynamic_index_in_dim(modp_all, l * N_DEV + me, axis=1, keepdims=False)
        mod = row.reshape(1, 6 * D) + ada_b[l].reshape(1, 6 * D)
        mods.append([mod[:, i * D:(i + 1) * D] for i in range(6)])

    w_att_in, w_att_out = _slabs_to_cols(att_in_all), att_out_all.reshape(D, D)
    bf = lambda t: t.astype(BF16)
    ffn_shards = [[bf(ffn_w_up[l]), bf(ffn_w_down[l])] for l in range(2)]
    rec_shards = [bf(_shard2d('rec_w_in', rec_w_in)), bf(s5_glu_w[0]), bf(rec_w_out[0])]
    ffn_cw = [small_full['ffn_conv'][l] for l in range(2)]
    dn_cw = small_full['dn_conv'][0]
    s5_dskip, glu_b = small_full['s5_d'], small_full['s5_glu_b']
    row = lambda t: t.reshape(1, -1)

    sh1, sc1, g1, sh2, sc2, g2 = mods[0]
    h1 = gate_norm_fwd(x0, None, None, row(norm_mix[0]), sh1, sc1, "l0_norm1")
    wvec, sinkvec = attn_vectors(attn_q_norm_a[0], attn_k_norm_a[0], attn_q_norm_b[0], attn_k_norm_b[0], attn_sinks[0])
    y0, res_att, got = attention_block_fwd(
        h1, w_att_in, wvec, sinkvec, w_att_out, "att",
        comms={'swa': (ffn_shards[0], True), 1: (rec_shards, True), 4: (ffn_shards[1], True)})
    w_up, w_down = [], []
    for up_all, down_all in (got['swa'], got[4]):
        w_up.append((_slabs_to_cols(up_all[:4]), _slabs_to_cols(up_all[4:])))
        w_down.append(down_all.reshape(D_FF, D))
    w_rec_in, glu_w, w_rec_out = got[1][0].reshape(D, REC_PAD), got[1][1].reshape(S5_W, S5_W), got[1][2].reshape(D, D)
    x1, h2 = gate_norm_fwd(x0, y0, g1, row(norm_ffn[0]), sh2, sc2, "l0_norm2")
    f0, res_f0 = ffn_block_fwd(h2, w_up[0][0], w_up[0][1], ffn_cw[0], w_down[0], "ffn0")
    t1, tc1, tg1, t2, tc2, tg2 = mods[1]
    x2, h3 = gate_norm_fwd(x1, f0, g2, row(norm_mix[1]), t1, tc1, "l1_norm1")
    rin = matmul([(h3, w_rec_in)], "nn", "rec_in")
    s5p, s5p_vjp = jax.vjp(s5_params, s5_lambda_re[0], s5_lambda_im[0], s5_log_dt[0], s5_b_re[0], s5_b_im[0],
                           s5_c_re[0], s5_c_im[0])
    u = rin[:, :S5_W]
    yc, res_s5 = s5_block_fwd(u, s5p, s5_dskip, glu_w, glu_b, "s5")
    yd, res_dn = dn_block_fwd(rin, dn_cw, dn_a_log[0], dn_dt_bias[0], dn_out_norm[0], "dn")
    ycat = jnp.concatenate([yc, yd], axis=1)
    y1 = matmul([(ycat, w_rec_out)], "nn", "rec_out")
    x3, h4 = gate_norm_fwd(x2, y1, tg1, row(norm_ffn[1]), t2, tc2, "l1_norm2")
    f1, res_f1 = ffn_block_fwd(h4, w_up[1][0], w_up[1][1], ffn_cw[1], w_down[1], "ffn1")
    dx4, df1, lsum = final_loss(x3, f1, tg2, tgt, "loss")

    G = {}
    d_tg2 = lsum[8:16].sum(axis=0)
    dh4, gf1, _ = ffn_block_bwd(df1, res_f1, w_up[1][0], w_up[1][1], ffn_cw[1], w_down[1], "ffn1")
    ffn_slabs = lambda g: [g['w_up'], g['w_down'].reshape(N_DEV, D_FF // N_DEV, D)]
    dx3, dy1, s = gate_norm_bwd(x3, y1, tg1, row(norm_ffn[1]), tc2, dx4, dh4, "l1_dnorm2")
    s = s.reshape(4, 8, D).sum(axis=1)
    d_tg1, d_nffn1, d_t2, d_tc2 = s[0], s[1] * (1.0 + tc2[0]), s[2], s[1] * norm_ffn[1]
    g_rec_out = matmul([(ycat, dy1)], "tn", "rec_out_dw", out_dtype=BF16).reshape(N_DEV, D // N_DEV, D)
    dycat = matmul([(dy1, w_rec_out)], "nt", "rec_out_dx")
    du, s5cot, gs5 = s5_block_bwd(dycat[:, :S5_W], res_s5, s5p, s5_dskip, glu_w, glu_b, "s5")
    s5g = s5p_vjp(s5cot)
    dqkv, dz, da, dbraw, gdn, recv_ffn1 = dn_block_bwd(dycat[:, S5_W:], res_dn, rin, dn_cw, dn_out_norm[0], "dn",
                                                       comm=(ffn_slabs(gf1), False))
    drin = jnp.concatenate([du, dqkv, dz, da, dbraw, jnp.zeros((L, REC_PAD - REC_IN), F32)], axis=1).astype(BF16)
    g_rec_in = matmul([(h3, drin)], "tn", "rec_in_dw", out_dtype=BF16).reshape(N_DEV, D // N_DEV, REC_PAD)
    g_glu = gs5['glu_w'].astype(BF16).reshape(N_DEV, S5_W // N_DEV, S5_W)
    dh3 = matmul([(drin, w_rec_in)], "nt", "rec_in_dx")
    dx2, df0, s = gate_norm_bwd(x2, f0, g2, row(norm_mix[1]), tc1, dx3, dh3, "l1_dnorm1")
    s = s.reshape(4, 8, D).sum(axis=1)
    d_g2, d_nmix1, d_t1, d_tc1 = s[0], s[1] * (1.0 + tc1[0]), s[2], s[1] * norm_mix[1]
    dh2, gf0, recv_rec = ffn_block_bwd(df0, res_f0, w_up[0][0], w_up[0][1], ffn_cw[0], w_down[0], "ffn0",
                                       comm=([g_rec_in, g_glu, g_rec_out], False))
    dx1, dy0, s = gate_norm_bwd(x1, y0, g1, row(norm_ffn[0]), sc2, dx2, dh2, "l0_dnorm2")
    s = s.reshape(4, 8, D).sum(axis=1)
    d_g1, d_nffn0, d_sh2, d_sc2 = s[0], s[1] * (1.0 + sc2[0]), s[2], s[1] * norm_ffn[0]
    dh1, gatt, recv_ffn0 = attention_block_bwd(dy0, res_att, w_att_in, wvec, sinkvec, w_att_out, "att",
                                               comm=(ffn_slabs(gf0), False))
    att_slabs = [_cols_to_slabs(gatt['w_in']), gatt['w_out'].reshape(N_DEV, D // N_DEV, D)]
    (grad_x, s), recv_att = gate_norm_bwd(x0, None, None, row(norm_mix[0]), sc1, dx1, dh1, "l0_dnorm1",
                                          comm=(att_slabs, False))
    s = s.reshape(4, 8, D).sum(axis=1)
    d_nmix0, d_sh1, d_sc1 = s[1] * (1.0 + sc1[0]), s[2], s[1] * norm_mix[0]
    dmod = jnp.stack([jnp.concatenate([d_sh1, d_sc1, d_g1, d_sh2, d_sc2, d_g2]),
                      jnp.concatenate([d_t1, d_tc1, d_tg1, d_t2, d_tc2, d_tg2])])

    P = {'ada_b': dmod, 'norm_mix': jnp.stack([d_nmix0, d_nmix1]), 'norm_ffn': jnp.stack([d_nffn0, d_nffn1]),
         'attn_q_norm_a': gatt['q_norm_a'], 'attn_k_norm_a': gatt['k_norm_a'], 'attn_q_norm_b': gatt['q_norm_b'],
         'attn_k_norm_b': gatt['k_norm_b'], 'attn_sinks': gatt['sinks'],
         's5_lambda_re': s5g[0], 's5_lambda_im': s5g[1], 's5_log_dt': s5g[2], 's5_b_re': s5g[3], 's5_b_im': s5g[4],
         's5_c_re': s5g[5], 's5_c_im': s5g[6], 'dn_a_log': gdn['a_log'], 'dn_dt_bias': gdn['dt_bias'],
         'dn_out_norm': gdn['out_norm'],
         's5_d': gs5['dskip'], 's5_glu_b': gs5['glu_b'], 'dn_conv': gdn['conv'],
         'ffn_conv': jnp.stack([gf0['conv'], gf1['conv']])}

    out = {k: {} for k in ("g", "d", "m", "v")}
    keys = ("g", "d", "m", "v")
    recv = {'attn_w_in': recv_att[0], 'attn_w_out': recv_att[1], 'rec_w_in': recv_rec[0], 's5_glu_w': recv_rec[1],
            'rec_w_out': recv_rec[2]}
    for n, gr_ in recv.items():
        res4 = reduce_adamw(gr_, _shard2d(n, W[n]), _shard2d(n, M[n]), _shard2d(n, V[n]), "adamw_" + n)
        for key, t in zip(keys, res4):
            out[key][n] = (t[:, :REC_IN] if n == 'rec_w_in' else t).reshape(W[n].shape)
    for n, idx in (('ffn_w_up', 0), ('ffn_w_down', 1)):
        per_layer = [reduce_adamw(r_[idx], W[n][l], M[n][l], V[n][l], f"adamw_{n}{l}")
                     for l, r_ in enumerate((recv_ffn0, recv_ffn1))]
        for i, key in enumerate(keys):
            out[key][n] = jnp.stack([per_layer[0][i], per_layer[1][i]])

    rep_sizes = [_numel(shp) for _, shp in REPLICATED]
    ss_sizes = [_numel(shp) for _, _, shp in SMALL_SHARDED]
    rep_offs = _offsets(rep_sizes + ss_sizes + [1])
    parts = [P[n].reshape(-1) for n, _ in REPLICATED] + [P[n].reshape(-1) for n, _, _ in SMALL_SHARDED]
    parts.append(lsum[0:8].sum().reshape(1))
    spack = _pack_rows(jnp.concatenate(parts), 1024, 8)
    sall = all_gather(spack, "gather_small_grads")
    n_rest = sum(ss_sizes) + 1
    pk = lambda d: _pack_rows(jnp.concatenate([d[n].reshape(-1) for n, _ in REPLICATED]
                                              + [jnp.zeros((n_rest,), F32)]), 1024, 8)
    sg, sd_, sm, sv = [t.reshape(-1) for t in reduce_adamw(sall, pk(W), pk(M), pk(V), "adamw_small")]
    loss = 0.5 * sg[rep_offs[-1]] / D

    dmod_all = sall.reshape(N_DEV, -1)[:, :2 * 6 * D].reshape(N_DEV, 2, 6 * D)
    dmod_mine = lax.dynamic_slice_in_dim(dmod_all, me * (6 * D // N_DEV), 6 * D // N_DEV, axis=2)
    g_ada = jnp.stack([matmul([(cond_all, dmod_mine[:, l])], "tn", f"ada{l}_dw") for l in range(2)])
    ada2d = lambda t: t.reshape(2 * D, 6 * D // N_DEV)
    for key, t in zip(("g", "d", "m", "v"), reduce_adamw(ada2d(g_ada)[None], ada2d(ada_w), ada2d(m_ada_w),
                                                          ada2d(v_ada_w), "adamw_ada_w")):
        out[key]['ada_w'] = t.reshape(ada_w.shape)
    own = []
    for (n, ax, shp), o in zip(SMALL_SHARDED, rep_offs[len(REPLICATED):]):
        slabs = _to_slabs(sg[o:o + _numel(shp)].reshape(shp), ax)
        own.append(lax.dynamic_index_in_dim(slabs, me, axis=0, keepdims=False))
    own_names = [n for n, _, _ in SMALL_SHARDED]
    pk = lambda d: _pack_rows(jnp.concatenate([d[n].reshape(-1) for n in own_names]), 1024, 8)
    og, od, om, ov = [t.reshape(-1) for t in reduce_adamw(_pack_rows(jnp.concatenate(own), 1024, 8)[None],
                                                          pk(W), pk(M), pk(V), "adamw_own")]

    def unpack(names_shapes, bufs):
        o = 0
        for n, shp in names_shapes:
            k = _numel(shp)
            for key, buf in zip(("g", "d", "m", "v"), bufs):
                out[key][n] = buf[o:o + k].reshape(shp)
            o += k

    unpack(REPLICATED, (sg, sd_, sm, sv))
    unpack([(n, W[n].shape) for n in own_names], (og, od, om, ov))
    return (loss, grad_x[None], *[out["g"][n] for n in WEIGHTS], *[out["d"][n] for n in WEIGHTS],
            *[out["m"][n] for n in WEIGHTS], *[out["v"][n] for n in WEIGHTS])
```

```python
import functools
import math

import numpy as np
import jax
import jax.numpy as jnp
from jax import lax
from jax.experimental import pallas as pl
from jax.experimental.pallas import tpu as pltpu

F32 = jnp.float32
BF16 = jnp.bfloat16

N_DEV = 8
D = 1024
HD = 64
BLK = 128
ATTN_IN = 2304
CB = ATTN_IN // 128
B_BRANCHES = ((128, 1), (512, 4), (2048, 16))
S5_W = 256
S5_P = 1024
DN_H = 6
DN_DK = 128
DN_C = 64
REC_IN = 3340
REC_PAD = 3456
D_FF = 2816
EPS = 1e-6
ADAM_LR, ADAM_B1, ADAM_B2, ADAM_EPS, ADAM_WD, ADAM_STEP = 0.001, 0.9, 0.999, 1e-8, 0.01, 10
VMEM_LIMIT = 48 * 1024 * 1024

ALIBI = np.asarray(2.0 ** (-8.0 * np.arange(1, 17) / 16), dtype=np.float32)


def _cparams(*sem):
    return pltpu.CompilerParams(dimension_semantics=tuple(sem), vmem_limit_bytes=VMEM_LIMIT)


def _tile(n, target):
    if n <= target:
        return n
    best = None
    for t in range(128, target + 1, 128):
        if n % t == 0:
            best = t
    assert best is not None, (n, target)
    return best


def _rtile(n, target, mult=8):
    if n <= target:
        return n
    best = None
    for t in range(mult, target + 1, mult):
        if n % t == 0:
            best = t
    assert best is not None, (n, target)
    return best


def _fold8(x):
    r, c = x.shape
    return x.reshape(r // 8, 8, c).sum(axis=0)


def _sigmoid(x):
    return 1.0 / (1.0 + jnp.exp(-x))


_DIMS = {"nn": (((1,), (0,)), ((), ())), "nt": (((1,), (1,)), ((), ())), "tn": (((0,), (0,)), ((), ()))}


MM_FULL_K = 3584


def matmul(pairs, mode, name, out_dtype=F32, tm=512, tn=1536, tk=1024):
    a0, b0 = pairs[0]
    if mode == "nn":
        (M, K), N = a0.shape, b0.shape[1]
    elif mode == "nt":
        (M, K), N = a0.shape, b0.shape[0]
    else:
        (K, M), N = a0.shape, b0.shape[1]
    tm = _rtile(M, tm) if M % 128 else _tile(M, tm)
    tn = _tile(N, tn)
    tk = K if K <= MM_FULL_K else _tile(K, tk)
    nk = K // tk
    npair = len(pairs)
    dims = _DIMS[mode]

    def body(*refs):
        o_ref = refs[2 * npair]
        tot = None
        for p in range(npair):
            part = lax.dot_general(refs[2 * p][...].astype(BF16), refs[2 * p + 1][...].astype(BF16),
                                   dims, preferred_element_type=F32)
            tot = part if tot is None else tot + part
        if nk == 1:
            o_ref[...] = tot.astype(o_ref.dtype)
            return
        acc_ref = refs[2 * npair + 1]
        k = pl.program_id(2)

        @pl.when(k == 0)
        def _():
            acc_ref[...] = tot

        @pl.when(k > 0)
        def _():
            acc_ref[...] += tot

        @pl.when(k == nk - 1)
        def _():
            o_ref[...] = acc_ref[...].astype(o_ref.dtype)

    if mode == "nn":
        a_spec = pl.BlockSpec((tm, tk), lambda i, j, k: (i, k))
        b_spec = pl.BlockSpec((tk, tn), lambda i, j, k: (k, j))
    elif mode == "nt":
        a_spec = pl.BlockSpec((tm, tk), lambda i, j, k: (i, k))
        b_spec = pl.BlockSpec((tn, tk), lambda i, j, k: (j, k))
    else:
        a_spec = pl.BlockSpec((tk, tm), lambda i, j, k: (k, i))
        b_spec = pl.BlockSpec((tk, tn), lambda i, j, k: (k, j))
    flat = [t for pr in pairs for t in pr]
    return pl.pallas_call(
        body, name=name, grid=(M // tm, N // tn, nk),
        in_specs=[a_spec, b_spec] * npair,
        out_specs=pl.BlockSpec((tm, tn), lambda i, j, k: (i, j)),
        out_shape=jax.ShapeDtypeStruct((M, N), out_dtype),
        scratch_shapes=[pltpu.VMEM((tm, tn), F32)] if nk > 1 else [],
        compiler_params=_cparams("parallel", "parallel", "arbitrary"),
    )(*flat)


def gate_norm_fwd(x, y, gate, nw, sh, sc, name):
    L, C = x.shape
    tl = _rtile(L, 512)
    has_gate = y is not None

    def body(*refs):
        if has_gate:
            x_ref, y_ref, g_ref, nw_ref, sh_ref, sc_ref, xn_ref, h_ref = refs
            xn = x_ref[...] + g_ref[...] * y_ref[...]
            xn_ref[...] = xn
        else:
            x_ref, nw_ref, sh_ref, sc_ref, h_ref = refs
            xn = x_ref[...]
        r = lax.rsqrt(jnp.mean(xn * xn, axis=-1, keepdims=True) + EPS)
        h = (xn * r * nw_ref[...]) * (1.0 + sc_ref[...]) + sh_ref[...]
        h_ref[...] = h.astype(BF16)

    big = pl.BlockSpec((tl, C), lambda i: (i, 0))
    vec = pl.BlockSpec((1, C), lambda i: (0, 0))
    if has_gate:
        ins, in_specs = (x, y, gate, nw, sh, sc), [big, big, vec, vec, vec, vec]
        out_shape = (jax.ShapeDtypeStruct((L, C), F32), jax.ShapeDtypeStruct((L, C), BF16))
        out_specs = (big, big)
    else:
        ins, in_specs = (x, nw, sh, sc), [big, vec, vec, vec]
        out_shape = jax.ShapeDtypeStruct((L, C), BF16)
        out_specs = big
    return pl.pallas_call(body, name=name, grid=(L // tl,), in_specs=in_specs, out_specs=out_specs,
                          out_shape=out_shape, compiler_params=_cparams("parallel"))(*ins)


def gate_norm_bwd(xn, y, gate, nw, sc, dxn_direct, dh, name, comm=None):
    L, C = xn.shape
    tl = _rtile(L, 256)
    has_gate = y is not None
    has_direct = dxn_direct is not None

    def body(*refs):
        refs = list(refs)
        xn_ref = refs.pop(0)
        y_ref = refs.pop(0) if has_gate else None
        g_ref = refs.pop(0) if has_gate else None
        nw_ref = refs.pop(0)
        sc_ref = refs.pop(0)
        dd_ref = refs.pop(0) if has_direct else None
        dh_ref = refs.pop(0)
        dxn_ref = refs.pop(0)
        dy_ref = refs.pop(0) if has_gate else None
        sums_ref = refs.pop(0)

        @pl.when(pl.program_id(0) == 0)
        def _():
            sums_ref[...] = jnp.zeros_like(sums_ref)

        xv = xn_ref[...]
        dh_v = dh_ref[...]
        r = lax.rsqrt(jnp.mean(xv * xv, axis=-1, keepdims=True) + EPS)
        n = xv * r
        a = nw_ref[...] * (1.0 + sc_ref[...])
        dn = dh_v * a
        dx = r * (dn - n * jnp.mean(dn * n, axis=-1, keepdims=True))
        if has_direct:
            dx = dx + dd_ref[...]
        dxn_ref[...] = dx
        sums_ref[8:16, :] += _fold8(dh_v * n)
        sums_ref[16:24, :] += _fold8(dh_v)
        if has_gate:
            dy_ref[...] = (dx * g_ref[...]).astype(BF16)
            sums_ref[0:8, :] += _fold8(dx * y_ref[...])

    big = pl.BlockSpec((tl, C), lambda i: (i, 0))
    vec = pl.BlockSpec((1, C), lambda i: (0, 0))
    ins, in_specs = [xn], [big]
    if has_gate:
        ins += [y, gate]
        in_specs += [big, vec]
    ins += [nw, sc]
    in_specs += [vec, vec]
    if has_direct:
        ins.append(dxn_direct)
        in_specs.append(big)
    ins.append(dh)
    in_specs.append(big)
    out_shape = [jax.ShapeDtypeStruct((L, C), F32)]
    out_specs = [big]
    if has_gate:
        out_shape.append(jax.ShapeDtypeStruct((L, C), BF16))
        out_specs.append(big)
    out_shape.append(jax.ShapeDtypeStruct((32, C), F32))
    out_specs.append(pl.BlockSpec((32, C), lambda i: (0, 0)))
    return _call(body, ins, name=name, grid=(L // tl,), in_specs=in_specs, out_specs=tuple(out_specs),
                 out_shape=tuple(out_shape), sem=("arbitrary",), comm=comm)


def final_loss(x, f, gate, target, name):
    L, C = x.shape
    tl = _rtile(L, 256)

    def body(x_ref, f_ref, g_ref, t_ref, dy_ref, df_ref, sums_ref):
        @pl.when(pl.program_id(0) == 0)
        def _():
            sums_ref[...] = jnp.zeros_like(sums_ref)

        fv = f_ref[...]
        err = x_ref[...] + g_ref[...] * fv - t_ref[...]
        dy = err * (1.0 / C)
        dy_ref[...] = dy
        df_ref[...] = (dy * g_ref[...]).astype(BF16)
        sums_ref[0:8, :] += _fold8(err * err)
        sums_ref[8:16, :] += _fold8(dy * fv)

    big = pl.BlockSpec((tl, C), lambda i: (i, 0))
    vec = pl.BlockSpec((1, C), lambda i: (0, 0))
    return pl.pallas_call(
        body, name=name, grid=(L // tl,), in_specs=[big, big, vec, big],
        out_specs=(big, big, pl.BlockSpec((16, C), lambda i: (0, 0))),
        out_shape=(jax.ShapeDtypeStruct((L, C), F32), jax.ShapeDtypeStruct((L, C), BF16),
                   jax.ShapeDtypeStruct((16, C), F32)),
        compiler_params=_cparams("arbitrary"))(x, f, gate, target)


def _seg_ones(seg):
    r = lax.broadcasted_iota(jnp.int32, (128, 128), 0) // seg
    c = lax.broadcasted_iota(jnp.int32, (128, 128), 1) // seg
    return (r == c).astype(BF16)


def _segsum(t, ones):
    hi = t.astype(BF16)
    lo = (t - hi.astype(F32)).astype(BF16)
    return (jnp.dot(hi, ones, preferred_element_type=F32) + jnp.dot(lo, ones, preferred_element_type=F32))


_NORMED_TILES = tuple(list(range(0, 5)) + list(range(6, 14)))


def qknorm_fwd(qkv, wvec, name):
    L, C = qkv.shape
    tl = _rtile(L, 256)

    def body(x_ref, w_ref, o_ref):
        ones = _seg_ones(HD)
        for t in range(CB):
            cs = slice(t * 128, (t + 1) * 128)
            x = x_ref[:, cs]
            if t in _NORMED_TILES:
                ms = _segsum(x * x, ones) * (1.0 / HD)
                x = x * lax.rsqrt(ms + EPS) * w_ref[:, cs]
            o_ref[:, cs] = x.astype(BF16)

    return pl.pallas_call(
        body, name=name, grid=(L // tl,),
        in_specs=[pl.BlockSpec((tl, C), lambda i: (i, 0)), pl.BlockSpec((1, C), lambda i: (0, 0))],
        out_specs=pl.BlockSpec((tl, C), lambda i: (i, 0)),
        out_shape=jax.ShapeDtypeStruct((L, C), BF16), compiler_params=_cparams("parallel"))(qkv, wvec)


def qknorm_bwd(qkv, wvec, dy, name):
    L, C = qkv.shape
    tl = _rtile(L, 256)

    def body(x_ref, w_ref, dy_ref, dx_ref, sums_ref):
        @pl.when(pl.program_id(0) == 0)
        def _():
            sums_ref[...] = jnp.zeros_like(sums_ref)

        ones = _seg_ones(HD)
        for t in range(CB):
            cs = slice(t * 128, (t + 1) * 128)
            d = dy_ref[:, cs]
            if t in _NORMED_TILES:
                x = x_ref[:, cs]
                r = lax.rsqrt(_segsum(x * x, ones) * (1.0 / HD) + EPS)
                n = x * r
                dn = d * w_ref[:, cs]
                dx_ref[:, cs] = (r * (dn - n * (_segsum(dn * n, ones) * (1.0 / HD)))).astype(BF16)
                sums_ref[:, cs] += _fold8(d * n)
            else:
                dx_ref[:, cs] = d.astype(BF16)

    big = pl.BlockSpec((tl, C), lambda i: (i, 0))
    return pl.pallas_call(
        body, name=name, grid=(L // tl,),
        in_specs=[big, pl.BlockSpec((1, C), lambda i: (0, 0)), big],
        out_specs=(big, pl.BlockSpec((8, C), lambda i: (0, 0))),
        out_shape=(jax.ShapeDtypeStruct((L, C), BF16), jax.ShapeDtypeStruct((8, C), F32)),
        compiler_params=_cparams("arbitrary"))(qkv, wvec, dy)


def _attn_scores(q, kw, n, slope, step, maxdist):
    s = lax.dot_general(q, kw, (((1,), (1,)), ((), ())), preferred_element_type=F32) * (HD ** -0.5)
    qi = lax.broadcasted_iota(jnp.int32, (BLK, 2 * BLK), 0)
    sj = lax.broadcasted_iota(jnp.int32, (BLK, 2 * BLK), 1)
    dist = BLK + qi - sj
    valid = (dist >= 0) & (dist <= maxdist) & ((n > 0) | (sj >= BLK))
    bias = (-slope) * (step * dist).astype(F32)
    return jnp.where(valid, s + bias, -jnp.inf), valid


ATT_NQ = 4


def _attn_operands(hp, gqa, q_ref, kh_ref, kc_ref, vh_ref, vc_ref):
    ops = []
    for b in range(ATT_NQ):
        rows = slice(b * BLK, (b + 1) * BLK)
        prev = slice((b - 1) * BLK, b * BLK)
        for e in range(2):
            cs = slice(e * HD, (e + 1) * HD)
            if gqa:
                ksel = lambda ref, r: jnp.where(hp >= 2, ref[r, 64:128], ref[r, 0:64])
            else:
                ksel = lambda ref, r, cs=cs: ref[r, cs]
            kprev = ksel(kh_ref, slice(0, BLK)) if b == 0 else ksel(kc_ref, prev)
            vprev = ksel(vh_ref, slice(0, BLK)) if b == 0 else ksel(vc_ref, prev)
            ops.append((b, e, rows, cs, q_ref[rows, cs], jnp.concatenate([kprev, ksel(kc_ref, rows)], axis=0),
                        jnp.concatenate([vprev, ksel(vc_ref, rows)], axis=0)))
    return ops


def _attn_specs(d, q_off, k_off, v_off, gqa):
    kcol = (lambda r, hp: r * CB + k_off) if gqa else (lambda r, hp: r * CB + k_off + hp)
    vcol = (lambda r, hp: r * CB + v_off) if gqa else (lambda r, hp: r * CB + v_off + hp)
    return kcol, vcol


def attn_fwd(X, d, q_off, k_off, v_off, gqa, slope0, maxdist, name, comm=None):
    Ls = X.shape[0]
    TQ = ATT_NQ * BLK
    nt = Ls // TQ
    slopes = jnp.asarray(ALIBI)

    def body(sl_ref, q_ref, kh_ref, kc_ref, vh_ref, vc_ref, o_ref, lse_ref):
        hp, t = pl.program_id(1), pl.program_id(2)
        ops = _attn_operands(hp, gqa, q_ref, kh_ref, kc_ref, vh_ref, vc_ref)
        s = [_attn_scores(q, kw, t if b == 0 else 1, sl_ref[slope0 + 2 * hp + e], d, maxdist)[0]
             for (b, e, rows, cs, q, kw, vw) in ops]
        m = [jnp.max(x, axis=-1, keepdims=True) for x in s]
        p = [jnp.exp(x - mm) for x, mm in zip(s, m)]
        l = [jnp.sum(x, axis=-1, keepdims=True) for x in p]
        o = [jnp.dot(x.astype(BF16), op[6], preferred_element_type=F32) / ll for x, op, ll in zip(p, ops, l)]
        for (b, e, rows, cs, q, kw, vw), oo, mm, ll in zip(ops, o, m, l):
            o_ref[rows, cs] = oo
            lse_ref[rows, cs] = jnp.broadcast_to(mm + jnp.log(ll), (BLK, HD))

    kcol, vcol = _attn_specs(d, q_off, k_off, v_off, gqa)
    tile, blk = (TQ, 128), (BLK, 128)
    halo = lambda t: jnp.maximum(t * ATT_NQ - 1, 0)
    in_specs = [
        pl.BlockSpec(memory_space=pltpu.SMEM),
        pl.BlockSpec(tile, lambda r, hp, t: (t, r * CB + q_off + hp)),
        pl.BlockSpec(blk, lambda r, hp, t: (halo(t), kcol(r, hp))),
        pl.BlockSpec(tile, lambda r, hp, t: (t, kcol(r, hp))),
        pl.BlockSpec(blk, lambda r, hp, t: (halo(t), vcol(r, hp))),
        pl.BlockSpec(tile, lambda r, hp, t: (t, vcol(r, hp))),
    ]
    out_spec = pl.BlockSpec(tile, lambda r, hp, t: (t, r * 4 + hp))
    out = jax.ShapeDtypeStruct((Ls, d * 512), F32)
    return _call(body, (slopes, X, X, X, X, X), name=name, grid=(d, 4, nt), in_specs=in_specs,
                 out_specs=(out_spec, out_spec), out_shape=(out, out),
                 sem=("parallel", "parallel", "arbitrary"), comm=comm)


def attn_bwd(X, o, lse, do, dlse, d, q_off, k_off, v_off, gqa, slope0, maxdist, name, comm=None):
    Ls = X.shape[0]
    slopes = jnp.asarray(ALIBI)

    TQ = ATT_NQ * BLK
    nt = Ls // TQ
    nt_dims, tn_dims = (((1,), (1,)), ((), ())), (((0,), (0,)), ((), ()))

    def body(sl_ref, q_ref, kh_ref, kc_ref, vh_ref, vc_ref, o_ref, lse_ref, do_ref, dlse_ref,
             dq_ref, dk_ref, dv_ref, ak_ref, av_ref, pk_ref, pv_ref):
        hp, t = pl.program_id(1), pl.program_id(2)

        @pl.when(t == 0)
        def _():
            pk_ref[...] = jnp.zeros_like(pk_ref)
            pv_ref[...] = jnp.zeros_like(pv_ref)

        @pl.when(t < nt)
        def _():
            ops = _attn_operands(hp, gqa, q_ref, kh_ref, kc_ref, vh_ref, vc_ref)
            sv = [_attn_scores(q, kw, t if b == 0 else 1, sl_ref[slope0 + 2 * hp + e], d, maxdist)
                  for (b, e, rows, cs, q, kw, vw) in ops]
            p = [jnp.where(valid, jnp.exp(s - lse_ref[op[2], op[1] * HD:op[1] * HD + 1]), 0.0)
                 for (s, valid), op in zip(sv, ops)]
            dov = [do_ref[op[2], op[3]] for op in ops]
            delta = [jnp.sum(dd * o_ref[op[2], op[3]], axis=-1, keepdims=True) for dd, op in zip(dov, ops)]
            dob = [dd.astype(BF16) for dd in dov]
            dp = [lax.dot_general(dd, op[6], nt_dims, preferred_element_type=F32) for dd, op in zip(dob, ops)]
            ds = [(pp * (x - dl + dlse_ref[op[2], op[1] * HD:op[1] * HD + 1])).astype(BF16)
                  for pp, x, dl, op in zip(p, dp, delta, ops)]
            dq = [jnp.dot(x, op[5], preferred_element_type=F32) * (HD ** -0.5) for x, op in zip(ds, ops)]
            dkw = [lax.dot_general(x, op[4], tn_dims, preferred_element_type=F32) * (HD ** -0.5)
                   for x, op in zip(ds, ops)]
            dvw = [lax.dot_general(pp.astype(BF16), dd, tn_dims, preferred_element_type=F32)
                   for pp, dd in zip(p, dob)]
            ak_ref[...] = jnp.zeros_like(ak_ref)
            av_ref[...] = jnp.zeros_like(av_ref)
            for (b, e, rows, cs, q, kw, vw), x, yk, yv in zip(ops, dq, dkw, dvw):
                dq_ref[rows, cs] = x
                ak_ref[b * BLK:(b + 2) * BLK, cs] += yk
                av_ref[b * BLK:(b + 2) * BLK, cs] += yv
            if nt == 1:
                dk_ref[...] = ak_ref[BLK:, :]
                dv_ref[...] = av_ref[BLK:, :]
                return
            last = slice(TQ - BLK, TQ)
            dk_ref[...] = pk_ref[...]
            dv_ref[...] = pv_ref[...]
            dk_ref[last, :] += ak_ref[0:BLK, :]
            dv_ref[last, :] += av_ref[0:BLK, :]
            pk_ref[...] = ak_ref[BLK:, :]
            pv_ref[...] = av_ref[BLK:, :]

        @pl.when(t == nt)
        def _():
            dk_ref[...] = pk_ref[...]
            dv_ref[...] = pv_ref[...]

    kcol, vcol = _attn_specs(d, q_off, k_off, v_off, gqa)
    tile, blk = (TQ, 128), (BLK, 128)
    cur = lambda t: jnp.minimum(t, nt - 1)
    halo = lambda t: jnp.maximum(cur(t) * ATT_NQ - 1, 0)
    ospec = pl.BlockSpec(tile, lambda r, hp, t: (cur(t), r * 4 + hp))
    in_specs = [
        pl.BlockSpec(memory_space=pltpu.SMEM),
        pl.BlockSpec(tile, lambda r, hp, t: (cur(t), r * CB + q_off + hp)),
        pl.BlockSpec(blk, lambda r, hp, t: (halo(t), kcol(r, hp))),
        pl.BlockSpec(tile, lambda r, hp, t: (cur(t), kcol(r, hp))),
        pl.BlockSpec(blk, lambda r, hp, t: (halo(t), vcol(r, hp))),
        pl.BlockSpec(tile, lambda r, hp, t: (cur(t), vcol(r, hp))),
        ospec, ospec, ospec, ospec,
    ]
    shifted = pl.BlockSpec(tile, lambda r, hp, t: (jnp.maximum(t - 1, 0), r * 4 + hp))
    out = jax.ShapeDtypeStruct((Ls, d * 512), F32)
    return _call(body, (slopes, X, X, X, X, X, o, lse, do, dlse), name=name, grid=(d, 4, nt + 1 if nt > 1 else 1),
                 in_specs=in_specs, out_specs=(ospec, shifted, shifted), out_shape=(out, out, out),
                 scratch_shapes=[pltpu.VMEM((TQ + BLK, 128), F32), pltpu.VMEM((TQ + BLK, 128), F32),
                                 pltpu.VMEM((TQ, 128), F32), pltpu.VMEM((TQ, 128), F32)],
                 sem=("parallel", "parallel", "arbitrary"), comm=comm)


def attn_merge_fwd(oa, la, sink, obs, lbs, name):
    L = oa.shape[0]
    tl = _rtile(L, 256)

    def body(oa_ref, la_ref, sk_ref, o1, o2, o3, l1, l2, l3, m_ref):
        m_ref[:, 0:512] = (oa_ref[...] * _sigmoid(la_ref[...] - sk_ref[...])).astype(BF16)
        a, b, c = l1[...], l2[...], l3[...]
        mx = jnp.maximum(jnp.maximum(a, b), c)
        ea, eb, ec = jnp.exp(a - mx), jnp.exp(b - mx), jnp.exp(c - mx)
        inv = 1.0 / (ea + eb + ec)
        m_ref[:, 512:1024] = ((ea * inv) * o1[...] + (eb * inv) * o2[...] + (ec * inv) * o3[...]).astype(BF16)

    big = pl.BlockSpec((tl, 512), lambda i: (i, 0))
    return pl.pallas_call(
        body, name=name, grid=(L // tl,),
        in_specs=[big, big, pl.BlockSpec((1, 512), lambda i: (0, 0))] + [big] * 6,
        out_specs=pl.BlockSpec((tl, 1024), lambda i: (i, 0)),
        out_shape=jax.ShapeDtypeStruct((L, 1024), BF16), compiler_params=_cparams("parallel"),
    )(oa, la, sink, *obs, *lbs)


def attn_merge_bwd(dm, oa, la, sink, obs, lbs, name):
    L = oa.shape[0]
    tl = _rtile(L, 256)

    def body(dm_ref, oa_ref, la_ref, sk_ref, o1, o2, o3, l1, l2, l3,
             doa_ref, dla_ref, d1, d2, d3, g1, g2, g3, sums_ref):
        @pl.when(pl.program_id(0) == 0)
        def _():
            sums_ref[...] = jnp.zeros_like(sums_ref)

        ones = _seg_ones(HD)
        for t in range(4):
            cs = slice(t * 128, (t + 1) * 128)
            dma = dm_ref[:, cs]
            keep = _sigmoid(la_ref[:, cs] - sk_ref[:, cs])
            doa_ref[:, cs] = dma * keep
            tt = dma * oa_ref[:, cs] * keep * (1.0 - keep)
            dla_ref[:, cs] = _segsum(tt, ones)
            sums_ref[:, cs] += _fold8(-tt)
            dmb = dm_ref[:, 512 + t * 128:512 + (t + 1) * 128]
            a, b, c = l1[:, cs], l2[:, cs], l3[:, cs]
            mx = jnp.maximum(jnp.maximum(a, b), c)
            ea, eb, ec = jnp.exp(a - mx), jnp.exp(b - mx), jnp.exp(c - mx)
            inv = 1.0 / (ea + eb + ec)
            wa, wb, wc = ea * inv, eb * inv, ec * inv
            d1[:, cs] = wa * dmb
            d2[:, cs] = wb * dmb
            d3[:, cs] = wc * dmb
            sa = _segsum(dmb * o1[:, cs], ones)
            sb = _segsum(dmb * o2[:, cs], ones)
            sc_ = _segsum(dmb * o3[:, cs], ones)
            mean = wa * sa + wb * sb + wc * sc_
            g1[:, cs] = wa * (sa - mean)
            g2[:, cs] = wb * (sb - mean)
            g3[:, cs] = wc * (sc_ - mean)

    big = pl.BlockSpec((tl, 512), lambda i: (i, 0))
    o512 = jax.ShapeDtypeStruct((L, 512), F32)
    return pl.pallas_call(
        body, name=name, grid=(L // tl,),
        in_specs=[pl.BlockSpec((tl, 1024), lambda i: (i, 0)), big, big,
                  pl.BlockSpec((1, 512), lambda i: (0, 0))] + [big] * 6,
        out_specs=tuple([big] * 8 + [pl.BlockSpec((8, 512), lambda i: (0, 0))]),
        out_shape=tuple([o512] * 8 + [jax.ShapeDtypeStruct((8, 512), F32)]),
        compiler_params=_cparams("arbitrary"),
    )(dm, oa, la, sink, *obs, *lbs)


def _shift_down(x, halo, k, first):
    rows = lax.broadcasted_iota(jnp.int32, (8, x.shape[1]), 0)
    out = pltpu.roll(x, k, axis=0)
    hrows = jnp.where(first, 0.0, pltpu.roll(halo, k, axis=0))
    top = jnp.where(rows < k, hrows, out[0:8, :])
    return jnp.concatenate([top, out[8:, :]], axis=0)


def _shift_up(x, nxt, k):
    tl = x.shape[0]
    rows = lax.broadcasted_iota(jnp.int32, (8, x.shape[1]), 0)
    out = pltpu.roll(x, tl - k, axis=0)
    bottom = jnp.where(rows >= 8 - k, pltpu.roll(nxt, 8 - k, axis=0), out[tl - 8:, :])
    return jnp.concatenate([out[:tl - 8, :], bottom], axis=0)


def _silu(x):
    return x * _sigmoid(x)


def _dsilu(x):
    s = _sigmoid(x)
    return s * (1.0 + x * (1.0 - s))


def ffn_act_fwd(ua, ub, cw, name):
    L, F = ua.shape
    tl = _rtile(L, 256)
    tc = _tile(F, 1408)
    hb = tl // 8

    def body(ua_ref, uah_ref, ub_ref, ubh_ref, wa_ref, wb_ref, o_ref):
        first = pl.program_id(1) == 0

        def conv(x_ref, h_ref, w_ref):
            x = x_ref[...]
            h = h_ref[...]
            return (w_ref[2:3, :] * x + w_ref[1:2, :] * _shift_down(x, h, 1, first)
                    + w_ref[0:1, :] * _shift_down(x, h, 2, first))

        a = conv(ua_ref, uah_ref, wa_ref)
        b = conv(ub_ref, ubh_ref, wb_ref)
        o_ref[...] = (_silu(a) * b).astype(BF16)

    main = pl.BlockSpec((tl, tc), lambda j, i: (i, j))
    halo = pl.BlockSpec((8, tc), lambda j, i: (jnp.maximum(i * hb - 1, 0), j))
    wa = pl.BlockSpec((3, tc), lambda j, i: (0, j))
    wb = pl.BlockSpec((3, tc), lambda j, i: (0, j + F // tc))
    return pl.pallas_call(
        body, name=name, grid=(F // tc, L // tl), in_specs=[main, halo, main, halo, wa, wb],
        out_specs=main, out_shape=jax.ShapeDtypeStruct((L, F), BF16),
        compiler_params=_cparams("parallel", "parallel"))(ua, ua, ub, ub, cw, cw)


def ffn_act_bwd(ua, ub, cw, dact, name, comm=None):
    L, F = ua.shape
    tl = _rtile(L, 256)
    tc = _tile(F, 1408)
    hb = tl // 8
    nrt = L // tl

    def body(ua_ref, uah_ref, ub_ref, ubh_ref, wa_ref, wb_ref, da_ref, dua_ref, dub_ref, sums_ref, ca_ref, cb_ref):
        i = pl.program_id(1)
        first = i == nrt - 1

        @pl.when(i == 0)
        def _():
            sums_ref[...] = jnp.zeros_like(sums_ref)
            ca_ref[...] = jnp.zeros_like(ca_ref)
            cb_ref[...] = jnp.zeros_like(cb_ref)

        def taps(x_ref, h_ref):
            x = x_ref[...]
            h = h_ref[...]
            return x, _shift_down(x, h, 1, first), _shift_down(x, h, 2, first)

        a0, a1, a2 = taps(ua_ref, uah_ref)
        b0, b1, b2 = taps(ub_ref, ubh_ref)
        a = wa_ref[2:3, :] * a0 + wa_ref[1:2, :] * a1 + wa_ref[0:1, :] * a2
        b = wb_ref[2:3, :] * b0 + wb_ref[1:2, :] * b1 + wb_ref[0:1, :] * b2
        dact_v = da_ref[...]
        dya = dact_v * b * _dsilu(a)
        dyb = dact_v * _silu(a)
        for (dy, w_ref, c_ref, d_ref, xs, base) in ((dya, wa_ref, ca_ref, dua_ref, (a2, a1, a0), 0),
                                                     (dyb, wb_ref, cb_ref, dub_ref, (b2, b1, b0), 24)):
            nxt = c_ref[...]
            d_ref[...] = (w_ref[2:3, :] * dy + w_ref[1:2, :] * _shift_up(dy, nxt, 1)
                          + w_ref[0:1, :] * _shift_up(dy, nxt, 2)).astype(BF16)
            c_ref[...] = dy[0:8, :]
            for j in range(3):
                sums_ref[base + 8 * j:base + 8 * j + 8, :] += _fold8(dy * xs[j])

    rev = lambda i: nrt - 1 - i
    main = pl.BlockSpec((tl, tc), lambda j, i: (rev(i), j))
    halo = pl.BlockSpec((8, tc), lambda j, i: (jnp.maximum(rev(i) * hb - 1, 0), j))
    wa = pl.BlockSpec((3, tc), lambda j, i: (0, j))
    wb = pl.BlockSpec((3, tc), lambda j, i: (0, j + F // tc))
    ob = jax.ShapeDtypeStruct((L, F), BF16)
    return _call(body, (ua, ua, ub, ub, cw, cw, dact), name=name, grid=(F // tc, nrt),
                 in_specs=[main, halo, main, halo, wa, wb, main],
                 out_specs=(main, main, pl.BlockSpec((48, tc), lambda j, i: (0, j))),
                 out_shape=(ob, ob, jax.ShapeDtypeStruct((48, F), F32)),
                 scratch_shapes=[pltpu.VMEM((8, tc), F32), pltpu.VMEM((8, tc), F32)],
                 sem=("parallel", "arbitrary"), comm=comm)


def attn_vectors(qna, kna, qnb, knb, sinks):
    ones = jnp.ones((128,), F32)
    wvec = jnp.concatenate([jnp.tile(qna, 8), jnp.tile(kna, 2), ones, jnp.tile(qnb, 8), jnp.tile(knb, 8),
                            jnp.tile(ones, 4)]).reshape(1, ATTN_IN)
    return wvec, jnp.repeat(sinks, HD).reshape(1, 512)


def _with_comm(result, comm):
    return result if comm is not None else (result, None)


def attention_block_fwd(h, w_in, wvec, sinkvec, w_out, tag, comms=None):
    L = h.shape[0]
    comms = comms or {}
    got = {}
    qkv = matmul([(h, w_in)], "nn", tag + "_qkv")
    X = qknorm_fwd(qkv, wvec, tag + "_qknorm")
    (oa, la), got['swa'] = _with_comm(attn_fwd(X, 1, 0, 4, 5, True, 0, BLK - 1, tag + "_swa",
                                               comm=comms.get('swa')), comms.get('swa'))
    obs, lbs = [], []
    for window, d in B_BRANCHES:
        (o, l), got[d] = _with_comm(attn_fwd(X.reshape(L // d, d * ATTN_IN), d, 6, 10, 14, False, 8, window // d,
                                             tag + f"_dil{d}", comm=comms.get(d)), comms.get(d))
        obs.append(o.reshape(L, 512))
        lbs.append(l.reshape(L, 512))
    m = attn_merge_fwd(oa, la, sinkvec, obs, lbs, tag + "_merge")
    y = matmul([(m, w_out)], "nn", tag + "_out")
    return y, (h, qkv, X, oa, la, obs, lbs, m), got


def attention_block_bwd(dy, res, w_in, wvec, sinkvec, w_out, tag, comms=None):
    h, qkv, X, oa, la, obs, lbs, m = res
    L = h.shape[0]
    comms = comms or {}
    got = {}
    g_w_out = matmul([(m, dy)], "tn", tag + "_dwout", out_dtype=BF16)
    dm = matmul([(dy, w_out)], "nt", tag + "_dm")
    doa, dla, d1, d2, d3, g1, g2, g3, sinksums = attn_merge_bwd(dm, oa, la, sinkvec, obs, lbs, tag + "_dmerge")
    (dqa, dka, dva), got['swa'] = _with_comm(attn_bwd(X, oa, la, doa, dla, 1, 0, 4, 5, True, 0, BLK - 1,
                                                      tag + "_dswa", comm=comms.get('swa')), comms.get('swa'))
    dqb = dkb = dvb = None
    for (window, d), o, l, do, dl in zip(B_BRANCHES, obs, lbs, (d1, d2, d3), (g1, g2, g3)):
        shp = (L // d, d * 512)
        (dq, dk, dv), got[d] = _with_comm(
            attn_bwd(X.reshape(L // d, d * ATTN_IN), o.reshape(shp), l.reshape(shp), do.reshape(shp), dl.reshape(shp),
                     d, 6, 10, 14, False, 8, window // d, tag + f"_ddil{d}", comm=comms.get(d)), comms.get(d))
        dq, dk, dv = dq.reshape(L, 512), dk.reshape(L, 512), dv.reshape(L, 512)
        dqb, dkb, dvb = (dq, dk, dv) if dqb is None else (dqb + dq, dkb + dk, dvb + dv)
    fold = lambda t: t.reshape(L, 2, 4, HD).sum(axis=2).reshape(L, 128)
    dX = jnp.concatenate([dqa, fold(dka), fold(dva), dqb, dkb, dvb], axis=1)
    dqkv, wsums = qknorm_bwd(qkv, wvec, dX, tag + "_dqknorm")
    g_w_in = matmul([(h, dqkv)], "tn", tag + "_dwin", out_dtype=BF16)
    dh = matmul([(dqkv, w_in)], "nt", tag + "_dh")
    ws = wsums.sum(axis=0)
    grads = dict(
        w_in=g_w_in, w_out=g_w_out,
        q_norm_a=ws[0:512].reshape(8, HD).sum(axis=0), k_norm_a=ws[512:640].reshape(2, HD).sum(axis=0),
        q_norm_b=ws[768:1280].reshape(8, HD).sum(axis=0), k_norm_b=ws[1280:1792].reshape(8, HD).sum(axis=0),
        sinks=sinksums.sum(axis=0).reshape(8, HD).sum(axis=1))
    return dh, grads, got


def ffn_block_fwd(h, w_up_a, w_up_b, cw, w_down, tag):
    ua = matmul([(h, w_up_a)], "nn", tag + "_upa")
    ub = matmul([(h, w_up_b)], "nn", tag + "_upb")
    act = ffn_act_fwd(ua, ub, cw, tag + "_act")
    f = matmul([(act, w_down)], "nn", tag + "_down")
    return f, (h, ua, ub, act)


def ffn_block_bwd(df, res, w_up_a, w_up_b, cw, w_down, tag, comm=None):
    h, ua, ub, act = res
    g_down = matmul([(act, df)], "tn", tag + "_dwdown", out_dtype=BF16)
    dact = matmul([(df, w_down)], "nt", tag + "_dact")
    (dua, dub, sums), got = _with_comm(ffn_act_bwd(ua, ub, cw, dact, tag + "_dactk", comm=comm), comm)
    g_up = jnp.concatenate([_cols_to_slabs(matmul([(h, dua)], "tn", tag + "_dwupa", out_dtype=BF16), N_DEV // 2),
                            _cols_to_slabs(matmul([(h, dub)], "tn", tag + "_dwupb", out_dtype=BF16), N_DEV // 2)],
                           axis=0)
    dh = matmul([(dua, w_up_a), (dub, w_up_b)], "nt", tag + "_dh")
    s = sums.reshape(2, 3, 8, D_FF).sum(axis=2)
    g_conv = jnp.concatenate([s[0], s[1]], axis=1)
    return dh, dict(w_up=g_up, conv=g_conv, w_down=g_down), got


def s5_params(lam_re, lam_im, log_dt, b_re, b_im, c_re, c_im):
    dt = jnp.exp(log_dt)[:, None]
    mag, ang = jnp.exp(lam_re * dt), lam_im * dt
    a_re, a_im = mag * jnp.cos(ang), mag * jnp.sin(ang)
    nr, ni = a_re - 1.0, a_im
    den = lam_re * lam_re + lam_im * lam_im
    f_re = (nr * lam_re + ni * lam_im) / den
    f_im = (ni * lam_re - nr * lam_im) / den
    eye = jnp.eye(16, dtype=F32)[:, None, :, None]
    bd = lambda b: (eye * jnp.transpose(b, (0, 2, 1))[:, :, None, :]).reshape(S5_W, S5_P)
    cd = lambda c: (eye * jnp.transpose(c, (0, 2, 1))[:, :, None, :]).reshape(S5_P, S5_W)
    flat = lambda t: t.reshape(1, S5_P)
    return flat(a_re), flat(a_im), flat(f_re), flat(f_im), bd(b_re), bd(b_im), cd(c_re), cd(c_im)


def _scan_tables(a_re, a_im, reverse):
    pows = [(a_re, a_im)]
    for _ in range(7):
        pr, pi = pows[-1]
        pows.append((pr * a_re - pi * a_im, pr * a_im + pi * a_re))
    order = list(range(7, -1, -1)) if reverse else list(range(8))
    z = jnp.zeros_like(a_re)
    rows = [pows[0][0], pows[0][1], pows[1][0], pows[1][1], pows[3][0], pows[3][1], z, z]
    rows += [pows[k][0] for k in order] + [pows[k][1] for k in order]
    return jnp.concatenate(rows, axis=0)


def _block_scan(er, ei, tab_ref, cr, ci, reverse):
    rows = lax.broadcasted_iota(jnp.int32, er.shape, 0)
    for idx, s in enumerate((1, 2, 4)):
        if reverse:
            sr, si, keep = pltpu.roll(er, 8 - s, axis=0), pltpu.roll(ei, 8 - s, axis=0), rows < 8 - s
        else:
            sr, si, keep = pltpu.roll(er, s, axis=0), pltpu.roll(ei, s, axis=0), rows >= s
        sr, si = jnp.where(keep, sr, 0.0), jnp.where(keep, si, 0.0)
        ar, ai = tab_ref[2 * idx:2 * idx + 1, :], tab_ref[2 * idx + 1:2 * idx + 2, :]
        er, ei = er + ar * sr - ai * si, ei + ar * si + ai * sr
    pr, pi_ = tab_ref[8:16, :], tab_ref[16:24, :]
    er, ei = er + pr * cr - pi_ * ci, ei + pr * ci + pi_ * cr
    return er, ei


def s5_scan_fwd(bu_re, bu_im, a_re, a_im, f_re, f_im, name):
    L, P = bu_re.shape
    tl = _rtile(L, 512)
    tab = _scan_tables(a_re, a_im, False)
    fvec = jnp.concatenate([f_re, f_im] + [jnp.zeros_like(f_re)] * 6, axis=0)

    def body(br_ref, bi_ref, tab_ref, f_ref, xr_ref, xi_ref, c_ref):
        @pl.when(pl.program_id(0) == 0)
        def _():
            c_ref[...] = jnp.zeros_like(c_ref)

        def blk(i, carry):
            cr, ci = carry
            rows = pl.ds(pl.multiple_of(i * 8, 8), 8)
            br, bi = br_ref[rows, :], bi_ref[rows, :]
            fr, fi = f_ref[0:1, :], f_ref[1:2, :]
            er, ei = _block_scan(fr * br - fi * bi, fr * bi + fi * br, tab_ref, cr, ci, False)
            xr_ref[rows, :] = er
            xi_ref[rows, :] = ei
            return er[7:8, :], ei[7:8, :]

        cr, ci = lax.fori_loop(0, tl // 8, blk, (c_ref[0:1, :], c_ref[1:2, :]))
        c_ref[0:1, :] = cr
        c_ref[1:2, :] = ci

    big = pl.BlockSpec((tl, P), lambda i: (i, 0))
    out = jax.ShapeDtypeStruct((L, P), F32)
    return pl.pallas_call(
        body, name=name, grid=(L // tl,),
        in_specs=[big, big, pl.BlockSpec((24, P), lambda i: (0, 0)), pl.BlockSpec((8, P), lambda i: (0, 0))],
        out_specs=(big, big), out_shape=(out, out), scratch_shapes=[pltpu.VMEM((8, P), F32)],
        compiler_params=_cparams("arbitrary"))(bu_re, bu_im, tab, fvec)


def s5_scan_bwd(dx_re, dx_im, x_re, x_im, bu_re, bu_im, a_re, a_im, f_re, f_im, name):
    L, P = dx_re.shape
    tl = _rtile(L, 256)
    nt = L // tl
    tab = _scan_tables(a_re, -a_im, True)
    fvec = jnp.concatenate([f_re, f_im] + [jnp.zeros_like(f_re)] * 6, axis=0)

    def body(gr_ref, gi_ref, xr_ref, xi_ref, br_ref, bi_ref, tab_ref, f_ref, dbr_ref, dbi_ref, s_ref, c_ref):
        @pl.when(pl.program_id(0) == 0)
        def _():
            c_ref[...] = jnp.zeros_like(c_ref)
            s_ref[...] = jnp.zeros_like(s_ref)

        def blk(k, carry):
            cr, ci = carry
            i = tl // 8 - 1 - k
            rows = pl.ds(pl.multiple_of(i * 8, 8), 8)
            er, ei = _block_scan(gr_ref[rows, :], gi_ref[rows, :], tab_ref, cr, ci, True)
            rid = lax.broadcasted_iota(jnp.int32, er.shape, 0)
            sr = jnp.where(rid == 7, cr, pltpu.roll(er, 7, axis=0))
            si = jnp.where(rid == 7, ci, pltpu.roll(ei, 7, axis=0))
            xr, xi = xr_ref[rows, :], xi_ref[rows, :]
            s_ref[0:8, :] += sr * xr + si * xi
            s_ref[8:16, :] += si * xr - sr * xi
            br, bi = br_ref[rows, :], bi_ref[rows, :]
            s_ref[16:24, :] += er * br + ei * bi
            s_ref[24:32, :] += ei * br - er * bi
            fr, fi = f_ref[0:1, :], f_ref[1:2, :]
            dbr_ref[rows, :] = fr * er + fi * ei
            dbi_ref[rows, :] = fr * ei - fi * er
            return er[0:1, :], ei[0:1, :]

        cr, ci = lax.fori_loop(0, tl // 8, blk, (c_ref[0:1, :], c_ref[1:2, :]))
        c_ref[0:1, :] = cr
        c_ref[1:2, :] = ci

    big = pl.BlockSpec((tl, P), lambda i: (nt - 1 - i, 0))
    out = jax.ShapeDtypeStruct((L, P), F32)
    return pl.pallas_call(
        body, name=name, grid=(nt,),
        in_specs=[big] * 6 + [pl.BlockSpec((24, P), lambda i: (0, 0)), pl.BlockSpec((8, P), lambda i: (0, 0))],
        out_specs=(big, big, pl.BlockSpec((32, P), lambda i: (0, 0))),
        out_shape=(out, out, jax.ShapeDtypeStruct((32, P), F32)), scratch_shapes=[pltpu.VMEM((8, P), F32)],
        compiler_params=_cparams("arbitrary"))(dx_re, dx_im, x_re, x_im, bu_re, bu_im, tab, fvec)


_GK, _GC = math.sqrt(2.0 / math.pi), 0.044715


def _gelu(y):
    return 0.5 * y * (1.0 + jnp.tanh(_GK * (y + _GC * y * y * y)))


def _dgelu(y):
    t = jnp.tanh(_GK * (y + _GC * y * y * y))
    return 0.5 * (1.0 + t) + 0.5 * y * (1.0 - t * t) * _GK * (1.0 + 3.0 * _GC * y * y)


def s5_out_fwd(x_re, x_im, u, cd_re, cd_im, dskip, glu_w, glu_b, name):
    L = u.shape[0]
    tl = _rtile(L, 512)

    def body(xr_ref, xi_ref, u_ref, cr_ref, ci_ref, d_ref, w_ref, b_ref, y_ref, o_ref):
        y = (jnp.dot(xr_ref[...].astype(BF16), cr_ref[...], preferred_element_type=F32)
             - jnp.dot(xi_ref[...].astype(BF16), ci_ref[...], preferred_element_type=F32)
             + d_ref[...] * u_ref[...])
        y_ref[...] = y
        g = _gelu(y)
        z = jnp.dot(g.astype(BF16), w_ref[...], preferred_element_type=F32) + b_ref[...]
        o_ref[...] = (g * _sigmoid(z)).astype(BF16)

    big = pl.BlockSpec((tl, S5_P), lambda i: (i, 0))
    sm = pl.BlockSpec((tl, S5_W), lambda i: (i, 0))
    full = lambda r, c: pl.BlockSpec((r, c), lambda i: (0, 0))
    return pl.pallas_call(
        body, name=name, grid=(L // tl,),
        in_specs=[big, big, sm, full(S5_P, S5_W), full(S5_P, S5_W), full(1, S5_W), full(S5_W, S5_W), full(1, S5_W)],
        out_specs=(sm, sm),
        out_shape=(jax.ShapeDtypeStruct((L, S5_W), F32), jax.ShapeDtypeStruct((L, S5_W), BF16)),
        compiler_params=_cparams("parallel"))(x_re, x_im, u, cd_re, cd_im, dskip, glu_w, glu_b)


def s5_out_bwd(dout, y, u, x_re, x_im, cd_re, cd_im, dskip, glu_w, glu_b, name, dout_col=0):
    L = u.shape[0]
    tl = _rtile(L, 256)
    nt_dims = (((1,), (1,)), ((), ()))
    tn_dims = (((0,), (0,)), ((), ()))

    def body(do_ref, y_ref, u_ref, xr_ref, xi_ref, cr_ref, ci_ref, d_ref, w_ref, b_ref,
             dxr_ref, dxi_ref, du_ref, dcr_ref, dci_ref, dw_ref, s_ref):
        @pl.when(pl.program_id(0) == 0)
        def _():
            dcr_ref[...] = jnp.zeros_like(dcr_ref)
            dci_ref[...] = jnp.zeros_like(dci_ref)
            dw_ref[...] = jnp.zeros_like(dw_ref)
            s_ref[...] = jnp.zeros_like(s_ref)

        yv, dov = y_ref[...], do_ref[...]
        g = _gelu(yv)
        gb = g.astype(BF16)
        sg = _sigmoid(jnp.dot(gb, w_ref[...], preferred_element_type=F32) + b_ref[...])
        dz = dov * g * sg * (1.0 - sg)
        dzb = dz.astype(BF16)
        dg = dov * sg + lax.dot_general(dzb, w_ref[...], nt_dims, preferred_element_type=F32)
        dw_ref[...] += lax.dot_general(gb, dzb, tn_dims, preferred_element_type=F32)
        dy = dg * _dgelu(yv)
        dyb = dy.astype(BF16)
        s_ref[0:8, :] += _fold8(dy * u_ref[...])
        s_ref[8:16, :] += _fold8(dz)
        du_ref[...] = dy * d_ref[...]
        dxr_ref[...] = lax.dot_general(dyb, cr_ref[...], nt_dims, preferred_element_type=F32)
        dxi_ref[...] = -lax.dot_general(dyb, ci_ref[...], nt_dims, preferred_element_type=F32)
        dcr_ref[...] += lax.dot_general(xr_ref[...].astype(BF16), dyb, tn_dims, preferred_element_type=F32)
        dci_ref[...] -= lax.dot_general(xi_ref[...].astype(BF16), dyb, tn_dims, preferred_element_type=F32)

    big = pl.BlockSpec((tl, S5_P), lambda i: (i, 0))
    sm = pl.BlockSpec((tl, S5_W), lambda i: (i, 0))
    full = lambda r, c: pl.BlockSpec((r, c), lambda i: (0, 0))
    sd = jax.ShapeDtypeStruct
    return pl.pallas_call(
        body, name=name, grid=(L // tl,),
        in_specs=[pl.BlockSpec((tl, S5_W), lambda i: (i, dout_col)), sm, sm, big, big, full(S5_P, S5_W),
                  full(S5_P, S5_W), full(1, S5_W), full(S5_W, S5_W), full(1, S5_W)],
        out_specs=(big, big, sm, full(S5_P, S5_W), full(S5_P, S5_W), full(S5_W, S5_W), full(16, S5_W)),
        out_shape=(sd((L, S5_P), F32), sd((L, S5_P), F32), sd((L, S5_W), F32), sd((S5_P, S5_W), F32),
                   sd((S5_P, S5_W), F32), sd((S5_W, S5_W), F32), sd((16, S5_W), F32)),
        compiler_params=_cparams("arbitrary"))(dout, y, u, x_re, x_im, cd_re, cd_im, dskip, glu_w, glu_b)


def s5_block_fwd(u, params, dskip, glu_w, glu_b, tag):
    a_re, a_im, f_re, f_im, bd_re, bd_im, cd_re, cd_im = params
    bu_re = matmul([(u, bd_re.astype(BF16))], "nn", tag + "_bure")
    bu_im = matmul([(u, bd_im.astype(BF16))], "nn", tag + "_buim")
    x_re, x_im = s5_scan_fwd(bu_re, bu_im, a_re, a_im, f_re, f_im, tag + "_scan")
    y, out = s5_out_fwd(x_re, x_im, u, cd_re.astype(BF16), cd_im.astype(BF16), dskip, glu_w, glu_b, tag + "_out")
    return out, (u, bu_re, bu_im, x_re, x_im, y)


def s5_block_bwd(dout, res, params, dskip, glu_w, glu_b, tag, dout_col=0):
    u, bu_re, bu_im, x_re, x_im, y = res
    a_re, a_im, f_re, f_im, bd_re, bd_im, cd_re, cd_im = params
    dxr, dxi, du, dcr, dci, dglu_w, sums = s5_out_bwd(dout, y, u, x_re, x_im, cd_re.astype(BF16), cd_im.astype(BF16),
                                                      dskip, glu_w, glu_b, tag + "_dout", dout_col=dout_col)
    dbr, dbi, acc = s5_scan_bwd(dxr, dxi, x_re, x_im, bu_re, bu_im, a_re, a_im, f_re, f_im, tag + "_dscan")
    du = du + matmul([(dbr, bd_re.astype(BF16)), (dbi, bd_im.astype(BF16))], "nt", tag + "_du")
    dbd_re = matmul([(u, dbr)], "tn", tag + "_dbdre")
    dbd_im = matmul([(u, dbi)], "tn", tag + "_dbdim")
    acc = acc.reshape(4, 8, S5_P).sum(axis=1)
    s = sums.reshape(2, 8, S5_W).sum(axis=1)
    cot = (acc[0:1], acc[1:2], acc[2:3], acc[3:4], dbd_re, dbd_im, dcr, dci)
    return du, cot, dict(dskip=s[0], glu_w=dglu_w, glu_b=s[1])


DN_Z0, DN_NT = 18, 18
REC_U0, REC_A0 = 3072, 3328


def rec_cols_permute(w):
    return jnp.concatenate([w[..., S5_W:REC_A0], w[..., :S5_W], w[..., REC_A0:]], axis=-1)


def rec_cols_restore(w):
    return jnp.concatenate([w[..., REC_U0:REC_A0], w[..., :REC_U0], w[..., REC_A0:]], axis=-1)


DN_W = DN_H * DN_DK


def _dn_conv4(taps, w_ref):
    xc = w_ref[3:4, :] * taps[0]
    for k in range(1, 4):
        xc = xc + w_ref[3 - k:4 - k, :] * taps[k]
    return xc


def dn_prep_fwd(rin, cw, name):
    L = rin.shape[0]
    tl = _rtile(L, 256)
    hb = tl // 8

    def body(x_ref, h_ref, w_ref, o_ref):
        j = pl.program_id(0)
        first = pl.program_id(1) == 0
        x, h = x_ref[...], h_ref[...]
        s = _silu(_dn_conv4([x] + [_shift_down(x, h, k, first) for k in range(1, 4)], w_ref))
        scale = jnp.where(j == 0, DN_DK ** -0.5, 1.0)
        for hd in _HEADS:
            cs = slice(hd * 128, (hd + 1) * 128)
            sh = s[:, cs]
            r = lax.rsqrt(jnp.sum(sh * sh, axis=-1, keepdims=True) + EPS)
            o_ref[:, cs] = jnp.where(j < 2, sh * r * scale, sh)

    main = pl.BlockSpec((tl, DN_W), lambda j, i: (i, j))
    halo = pl.BlockSpec((8, DN_W), lambda j, i: (jnp.maximum(i * hb - 1, 0), j))
    return pl.pallas_call(
        body, name=name, grid=(3, L // tl),
        in_specs=[main, halo, pl.BlockSpec((4, DN_W), lambda j, i: (0, j))],
        out_specs=main, out_shape=jax.ShapeDtypeStruct((L, 3 * DN_W), F32),
        compiler_params=_cparams("parallel", "parallel"))(rin, rin, cw)


def dn_prep_bwd(rin, cw, dout, name):
    L = rin.shape[0]
    tl = _rtile(L, 256)
    hb = tl // 8
    nrt = L // tl

    def body(x_ref, h_ref, w_ref, d_ref, dx_ref, s_ref, c_ref):
        j = pl.program_id(0)
        i = pl.program_id(1)
        first = i == nrt - 1

        @pl.when(i == 0)
        def _():
            s_ref[...] = jnp.zeros_like(s_ref)
            c_ref[...] = jnp.zeros_like(c_ref)

        x, h = x_ref[...], h_ref[...]
        taps = [x] + [_shift_down(x, h, k, first) for k in range(1, 4)]
        xc = _dn_conv4(taps, w_ref)
        s = _silu(xc)
        scale = jnp.where(j == 0, DN_DK ** -0.5, 1.0)
        pieces = []
        for hd in _HEADS:
            cs = slice(hd * 128, (hd + 1) * 128)
            sh, d = s[:, cs], d_ref[:, cs]
            r = lax.rsqrt(jnp.sum(sh * sh, axis=-1, keepdims=True) + EPS)
            n = sh * r
            dn = d * scale
            pieces.append(jnp.where(j < 2, r * (dn - n * jnp.sum(dn * n, axis=-1, keepdims=True)), d))
        dxc = jnp.concatenate(pieces, axis=1) * _dsilu(xc)
        nxt = c_ref[...]
        dx_ref[...] = _dn_conv4([dxc] + [_shift_up(dxc, nxt, k) for k in range(1, 4)], w_ref).astype(BF16)
        c_ref[...] = dxc[0:8, :]
        for k in range(4):
            s_ref[8 * (3 - k):8 * (3 - k) + 8, :] += _fold8(dxc * taps[k])

    rev = lambda i: nrt - 1 - i
    main = pl.BlockSpec((tl, DN_W), lambda j, i: (rev(i), j))
    halo = pl.BlockSpec((8, DN_W), lambda j, i: (jnp.maximum(rev(i) * hb - 1, 0), j))
    return pl.pallas_call(
        body, name=name, grid=(3, nrt),
        in_specs=[main, halo, pl.BlockSpec((4, DN_W), lambda j, i: (0, j)), main],
        out_specs=(main, pl.BlockSpec((32, DN_W), lambda j, i: (0, j))),
        out_shape=(jax.ShapeDtypeStruct((L, 3 * DN_W), BF16), jax.ShapeDtypeStruct((32, 3 * DN_W), F32)),
        scratch_shapes=[pltpu.VMEM((8, DN_W), F32)],
        compiler_params=_cparams("parallel", "arbitrary"))(rin, rin, cw, dout)


_HI = lax.Precision.HIGH
_NT = (((1,), (1,)), ((), ()))
_TN = (((0,), (0,)), ((), ()))
_HEADS = tuple(range(DN_H))


def _mm(a, b, dims=(((1,), (0,)), ((), ())), hi=False):
    if hi:
        return lax.dot_general(a, b, dims, precision=_HI, preferred_element_type=F32)
    return lax.dot_general(a.astype(BF16), b.astype(BF16), dims, preferred_element_type=F32)


def _dn_masks():
    ri = lax.broadcasted_iota(jnp.int32, (DN_C, DN_C), 0)
    ci = lax.broadcasted_iota(jnp.int32, (DN_C, DN_C), 1)
    return ri >= ci, ri > ci, (ri == ci).astype(F32)


def _dn_decay(gc, gr, causal):
    gam = [jnp.where(causal, jnp.exp(jnp.where(causal, gc[h] - gr[h], 0.0)), 0.0) for h in _HEADS]
    eg = [jnp.exp(gc[h]) for h in _HEADS]
    el = [jnp.exp(gc[h][DN_C - 1:DN_C, :] - gc[h]) for h in _HEADS]
    gl = [jnp.exp(gc[h][DN_C - 1:DN_C, :]) for h in _HEADS]
    return gam, eg, el, gl


def _dn_solve(k, v, beta, gam, eg, kk, strict, eye):
    nmat = [jnp.where(strict, beta[h] * kk[h] * gam[h], 0.0) for h in _HEADS]
    t = [eye - nmat[h] for h in _HEADS]
    m = [_mm(nmat[h], nmat[h], hi=True) for h in _HEADS]
    for step in range(5):
        t = [t[h] + _mm(t[h], m[h], hi=True) for h in _HEADS]
        if step < 4:
            m = [_mm(m[h], m[h], hi=True) for h in _HEADS]
    rhs = [jnp.concatenate([v[h] * beta[h], k[h] * (beta[h] * eg[h])], axis=1) for h in _HEADS]
    sol = [_mm(t[h], rhs[h], hi=True) for h in _HEADS]
    return t, sol


def dn_chunk_fwd(qkv, gcol, grow, bcol, name, comm=None):
    L = qkv.shape[0]
    C, W = DN_C, DN_H * DN_DK
    ncb = 8
    tl = ncb * C
    nchunks = L // C

    def body(q_ref, k_ref, v_ref, gc_ref, gr_ref, b_ref, o_ref, sh_ref, t_ref, sol_ref, s_ref):
        @pl.when(pl.program_id(0) == 0)
        def _():
            s_ref[...] = jnp.zeros_like(s_ref)

        causal, strict, eye = _dn_masks()

        def chunk(c, _):
            rows = pl.ds(pl.multiple_of(c * C, C), C)
            grow_c = gr_ref[c]
            hs = lambda h: slice(h * 128, (h + 1) * 128)
            q = [q_ref[rows, hs(h)] for h in _HEADS]
            k = [k_ref[rows, hs(h)] for h in _HEADS]
            v = [v_ref[rows, hs(h)] for h in _HEADS]
            gc = [gc_ref[rows, h:h + 1] for h in _HEADS]
            gr = [grow_c[h:h + 1, :] for h in _HEADS]
            beta = [b_ref[rows, h:h + 1] for h in _HEADS]
            gam, eg, el, gl = _dn_decay(gc, gr, causal)
            kk = [_mm(k[h], k[h], _NT) for h in _HEADS]
            t, sol = _dn_solve(k, v, beta, gam, eg, kk, strict, eye)
            qk = [_mm(q[h], k[h], _NT) * gam[h] for h in _HEADS]
            S = [s_ref[hs(h), :] for h in _HEADS]
            vn = [sol[h][:, :128] - _mm(sol[h][:, 128:], S[h]) for h in _HEADS]
            o = [_mm(q[h] * eg[h], S[h]) + _mm(qk[h], vn[h]) for h in _HEADS]
            Sn = [S[h] * gl[h] + _mm(k[h] * el[h], vn[h], _TN) for h in _HEADS]
            for h in _HEADS:
                sh_ref[c, hs(h), :] = S[h]
                s_ref[hs(h), :] = Sn[h]
                o_ref[rows, hs(h)] = o[h]
                t_ref[rows, h * C:(h + 1) * C] = t[h]
                sol_ref[rows, h * 256:(h + 1) * 256] = sol[h]
            return 0

        lax.fori_loop(0, ncb, chunk, 0)

    col = lambda b: pl.BlockSpec((tl, W), lambda i: (i, b))
    small = pl.BlockSpec((tl, 8), lambda i: (i, 0))
    rowblk = lambda w: pl.BlockSpec((tl, w), lambda i: (i, 0))
    sd = jax.ShapeDtypeStruct
    return _call(body, (qkv, qkv, qkv, gcol, grow, bcol), name=name, grid=(L // tl,),
                 in_specs=[col(0), col(1), col(2), small, pl.BlockSpec((ncb, 8, C), lambda i: (i, 0, 0)), small],
                 out_specs=(rowblk(W), pl.BlockSpec((ncb, W, 128), lambda i: (i, 0, 0)), rowblk(DN_H * C),
                            rowblk(DN_H * 256)),
                 out_shape=(sd((L, W), F32), sd((nchunks, W, 128), F32), sd((L, DN_H * C), F32),
                            sd((L, DN_H * 256), F32)),
                 scratch_shapes=[pltpu.VMEM((W, 128), F32)], sem=("arbitrary",), comm=comm)


def dn_chunk_bwd(qkv, gcol, grow, bcol, shist, thist, solhist, do, name, comm=None):
    L = qkv.shape[0]
    C, W = DN_C, DN_H * DN_DK
    ncb = 8
    tl = ncb * C
    nchunks = L // C
    nt = L // tl

    def body(q_ref, k_ref, v_ref, gc_ref, gr_ref, b_ref, sh_ref, t_ref, sol_ref, do_ref,
             dqkv_ref, dgc_ref, dgr_ref, db_ref, ds_ref):
        @pl.when(pl.program_id(0) == 0)
        def _():
            ds_ref[...] = jnp.zeros_like(ds_ref)

        lane8 = lax.broadcasted_iota(jnp.int32, (C, 8), 1)
        sub8 = lax.broadcasted_iota(jnp.int32, (8, C), 0)
        rowid = lax.broadcasted_iota(jnp.int32, (C, 1), 0)
        causal, strict, _ = _dn_masks()
        rsum = lambda a: jnp.sum(a, axis=1, keepdims=True)

        def chunk(cc, _):
            c = ncb - 1 - cc
            rows = pl.ds(pl.multiple_of(c * C, C), C)
            grow_c = gr_ref[c]
            hs = lambda h: slice(h * 128, (h + 1) * 128)
            q = [q_ref[rows, hs(h)] for h in _HEADS]
            k = [k_ref[rows, hs(h)] for h in _HEADS]
            v = [v_ref[rows, hs(h)] for h in _HEADS]
            gc = [gc_ref[rows, h:h + 1] for h in _HEADS]
            gr = [grow_c[h:h + 1, :] for h in _HEADS]
            beta = [b_ref[rows, h:h + 1] for h in _HEADS]
            t = [t_ref[rows, h * C:(h + 1) * C] for h in _HEADS]
            sol = [sol_ref[rows, h * 256:(h + 1) * 256] for h in _HEADS]
            S = [sh_ref[c, hs(h), :] for h in _HEADS]
            dS = [ds_ref[hs(h), :] for h in _HEADS]
            dov = [do_ref[rows, hs(h)] for h in _HEADS]
            gam, eg, el, gl = _dn_decay(gc, gr, causal)
            kk = [_mm(k[h], k[h], _NT) for h in _HEADS]
            qk_raw = [_mm(q[h], k[h], _NT) for h in _HEADS]
            w = [sol[h][:, 128:] for h in _HEADS]
            kd = [k[h] * el[h] for h in _HEADS]
            vn = [sol[h][:, :128] - _mm(w[h], S[h]) for h in _HEADS]
            dvn = [_mm(qk_raw[h] * gam[h], dov[h], _TN) + _mm(kd[h], dS[h]) for h in _HEADS]
            dqd = [_mm(dov[h], S[h], _NT) for h in _HEADS]
            dqk = [jnp.where(causal, _mm(dov[h], vn[h], _NT), 0.0) for h in _HEADS]
            dkd = [_mm(vn[h], dS[h], _NT) for h in _HEADS]
            dgl = [jnp.sum(rsum(dS[h] * S[h]), axis=0, keepdims=True) for h in _HEADS]
            dw = [-_mm(dvn[h], S[h], _NT) for h in _HEADS]
            dSn = [dS[h] * gl[h] + _mm(q[h] * eg[h], dov[h], _TN) - _mm(w[h], dvn[h], _TN) for h in _HEADS]
            drhs = [_mm(t[h], jnp.concatenate([dvn[h], dw[h]], axis=1), _TN, hi=True) for h in _HEADS]
            dn = [jnp.where(strict, -_mm(drhs[h], sol[h], _NT, hi=True), 0.0) for h in _HEADS]
            dgc_all = jnp.zeros((C, 8), F32)
            db_all = jnp.zeros((C, 8), F32)
            dgr_all = jnp.zeros((8, C), F32)
            for h in _HEADS:
                drv, drk = drhs[h][:, :128], drhs[h][:, 128:]
                t2 = rsum(drk * k[h])
                x = dn[h] * gam[h]
                dbeta = rsum(drv * v[h]) + t2 * eg[h] + rsum(x * kk[h])
                dkk = x * beta[h]
                draw = dqk[h] * gam[h]
                mm_ = (dn[h] * beta[h] * kk[h] + dqk[h] * qk_raw[h]) * gam[h]
                deg = t2 * beta[h] + rsum(dqd[h] * q[h])
                r_ = rsum(dkd[h] * k[h]) * el[h]
                dglast = jnp.sum(r_, axis=0, keepdims=True) + dgl[h] * gl[h]
                dgc = rsum(mm_) + deg * eg[h] - r_ + jnp.where(rowid == C - 1, dglast, 0.0)
                dgr = -jnp.sum(mm_, axis=0, keepdims=True)
                dqkv_ref[rows, hs(h)] = _mm(draw, k[h]) + dqd[h] * eg[h]
                dqkv_ref[rows, hs(DN_H + h)] = (drk * (beta[h] * eg[h]) + _mm(dkk, k[h]) + _mm(dkk, k[h], _TN)
                                                + _mm(draw, q[h], _TN) + dkd[h] * el[h])
                dqkv_ref[rows, hs(2 * DN_H + h)] = drv * beta[h]
                ds_ref[hs(h), :] = dSn[h]
                dgc_all = dgc_all + jnp.where(lane8 == h, dgc, 0.0)
                db_all = db_all + jnp.where(lane8 == h, dbeta, 0.0)
                dgr_all = dgr_all + jnp.where(sub8 == h, dgr, 0.0)
            dgc_ref[rows, :] = dgc_all
            db_ref[rows, :] = db_all
            dgr_ref[c] = dgr_all
            return 0

        lax.fori_loop(0, ncb, chunk, 0)

    rev = lambda i: nt - 1 - i
    col = lambda b: pl.BlockSpec((tl, W), lambda i: (rev(i), b))
    rowblk = lambda w: pl.BlockSpec((tl, w), lambda i: (rev(i), 0))
    small = pl.BlockSpec((tl, 8), lambda i: (rev(i), 0))
    g3 = pl.BlockSpec((ncb, 8, C), lambda i: (rev(i), 0, 0))
    sd = jax.ShapeDtypeStruct
    return _call(body, (qkv, qkv, qkv, gcol, grow, bcol, shist, thist, solhist, do), name=name, grid=(nt,),
                 in_specs=[col(0), col(1), col(2), small, g3, small,
                           pl.BlockSpec((ncb, W, 128), lambda i: (rev(i), 0, 0)), rowblk(DN_H * C),
                           rowblk(DN_H * 256), col(0)],
                 out_specs=(rowblk(3 * W), small, g3, small),
                 out_shape=(sd((L, 3 * W), F32), sd((L, 8), F32), sd((nchunks, 8, C), F32), sd((L, 8), F32)),
                 scratch_shapes=[pltpu.VMEM((W, 128), F32)], sem=("arbitrary",), comm=comm)


def dn_out_fwd(o, rin, nw, name):
    L = o.shape[0]
    tl = _rtile(L, 256)

    def body(o_ref, z_ref, w_ref, y_ref):
        for hd in _HEADS:
            cs = slice(hd * 128, (hd + 1) * 128)
            ov = o_ref[:, cs]
            r = lax.rsqrt(jnp.mean(ov * ov, axis=-1, keepdims=True) + EPS)
            y_ref[:, cs] = (ov * r * w_ref[...] * _silu(z_ref[:, cs])).astype(BF16)

    return pl.pallas_call(
        body, name=name, grid=(L // tl,),
        in_specs=[pl.BlockSpec((tl, DN_W), lambda i: (i, 0)), pl.BlockSpec((tl, DN_W), lambda i: (i, 3)),
                  pl.BlockSpec((1, 128), lambda i: (0, 0))],
        out_specs=pl.BlockSpec((tl, DN_W), lambda i: (i, 0)), out_shape=jax.ShapeDtypeStruct((L, DN_W), BF16),
        compiler_params=_cparams("parallel"))(o, rin, nw)


def dn_out_bwd(dycat, o, rin, nw, name):
    L = o.shape[0]
    tl = _rtile(L, 256)

    def body(dy_ref, o_ref, z_ref, w_ref, do_ref, dz_ref, s_ref):
        @pl.when(pl.program_id(0) == 0)
        def _():
            s_ref[...] = jnp.zeros_like(s_ref)

        for hd in _HEADS:
            cs = slice(hd * 128, (hd + 1) * 128)
            ov, zv, d = o_ref[:, cs], z_ref[:, cs], dy_ref[:, cs]
            r = lax.rsqrt(jnp.mean(ov * ov, axis=-1, keepdims=True) + EPS)
            n = ov * r
            dnw = d * _silu(zv)
            dz_ref[:, cs] = (d * n * w_ref[...] * _dsilu(zv)).astype(BF16)
            dn = dnw * w_ref[...]
            do_ref[:, cs] = r * (dn - n * jnp.mean(dn * n, axis=-1, keepdims=True))
            s_ref[:, cs] += _fold8(dnw * n)

    own = pl.BlockSpec((tl, DN_W), lambda i: (i, 0))
    sd = jax.ShapeDtypeStruct
    return pl.pallas_call(
        body, name=name, grid=(L // tl,),
        in_specs=[own, own, pl.BlockSpec((tl, DN_W), lambda i: (i, 3)), pl.BlockSpec((1, 128), lambda i: (0, 0))],
        out_specs=(own, own, pl.BlockSpec((8, DN_W), lambda i: (0, 0))),
        out_shape=(sd((L, DN_W), F32), sd((L, DN_W), BF16), sd((8, DN_W), F32)),
        compiler_params=_cparams("arbitrary"))(dycat, o, rin, nw)


def dn_gates(a, beta_raw, a_log, dt_bias):
    L = a.shape[0]
    beta = jax.nn.sigmoid(beta_raw)
    g = -jnp.exp(a_log) * jax.nn.softplus(a + dt_bias)
    G = jnp.cumsum(g.reshape(L // DN_C, DN_C, DN_H), axis=1)
    pad = lambda t: jnp.pad(t, ((0, 0), (0, 8 - DN_H)))
    gcol = pad(G.reshape(L, DN_H))
    grow = jnp.pad(jnp.transpose(G, (0, 2, 1)), ((0, 0), (0, 8 - DN_H), (0, 0)))
    return gcol, grow, pad(beta)


def dn_block_fwd(rin, cw, a_log, dt_bias, out_norm, tag, comm=None):
    gates, gates_vjp = jax.vjp(dn_gates, rin[:, REC_A0:REC_A0 + DN_H], rin[:, REC_A0 + DN_H:REC_IN], a_log, dt_bias)
    qkv = dn_prep_fwd(rin, cw, tag + "_prep")
    (o, shist, thist, solhist), got = _with_comm(dn_chunk_fwd(qkv, *gates, tag + "_chunk", comm=comm), comm)
    yd = dn_out_fwd(o, rin, out_norm.reshape(1, 128), tag + "_onorm")
    return yd, (qkv, gates, gates_vjp, o, shist, thist, solhist), got


def dn_block_bwd(dyd, res, rin, cw, out_norm, tag, comm=None):
    qkv, gates, gates_vjp, o, shist, thist, solhist = res
    do, dz, nsum = dn_out_bwd(dyd, o, rin, out_norm.reshape(1, 128), tag + "_donorm")
    (dqkv, dgc, dgr, db), got = _with_comm(dn_chunk_bwd(qkv, *gates, shist, thist, solhist, do, tag + "_dchunk",
                                                        comm=comm), comm)
    da, dbraw, g_alog, g_dtb = gates_vjp((dgc, dgr, db))
    dx, csum = dn_prep_bwd(rin, cw, dqkv, tag + "_dprep")
    grads = dict(conv=csum.reshape(4, 8, DN_NT * 128).sum(axis=1), a_log=g_alog, dt_bias=g_dtb,
                 out_norm=nsum.sum(axis=0).reshape(DN_H, 128).sum(axis=0))
    return dx, dz, da, dbraw, grads, got


_HBM = pl.BlockSpec(memory_space=pltpu.HBM)


def _mesh_pos():
    xi, yi, ci = lax.axis_index("x"), lax.axis_index("y"), lax.axis_index("c")
    return xi, yi, ci, 4 * xi + 2 * yi + ci


def _peer(xi, yi, ci, k):
    px = 1 - xi if (k >> 2) & 1 else xi
    py = 1 - yi if (k >> 1) & 1 else yi
    pc = 1 - ci if k & 1 else ci
    return (px, py, pc), 4 * px + 2 * py + pc


def _exchange(xs, gather, name):
    n = len(xs)

    def body(*refs):
        copies = _comm_copies(refs[:n], refs[n:2 * n], *refs[2 * n:], gather)
        for cp in copies:
            cp.start()
        for cp in copies:
            cp.wait()

    return pl.pallas_call(
        body, name=name, in_specs=[_HBM] * n, out_specs=tuple([_HBM] * n),
        out_shape=_comm_out_shapes(xs), scratch_shapes=_comm_sems(n))(*xs)


def _comm_out_shapes(xs):
    return tuple(jax.ShapeDtypeStruct((N_DEV,) + x.shape[-2:], x.dtype) for x in xs)


def _comm_sems(n):
    return [pltpu.SemaphoreType.DMA((n * (N_DEV - 1),)), pltpu.SemaphoreType.DMA((n * (N_DEV - 1),)),
            pltpu.SemaphoreType.DMA((n,))]


def _comm_copies(x_refs, o_refs, send_sems, recv_sems, lsems, gather):
    xi, yi, ci, me = _mesh_pos()
    copies = []
    for t in range(len(x_refs)):
        src_of = (lambda lin, t=t: x_refs[t]) if gather else (lambda lin, t=t: x_refs[t].at[lin])
        copies.append(pltpu.make_async_copy(src_of(me), o_refs[t].at[me], lsems.at[t]))
        for k in range(1, N_DEV):
            peer, lin = _peer(xi, yi, ci, k)
            s = t * (N_DEV - 1) + k - 1
            copies.append(pltpu.make_async_remote_copy(
                src_ref=src_of(lin), dst_ref=o_refs[t].at[me], send_sem=send_sems.at[s],
                recv_sem=recv_sems.at[s], device_id=peer, device_id_type=pl.DeviceIdType.MESH))
    return copies


def _call(body, args, *, name, grid, in_specs, out_specs, out_shape, scratch_shapes=(), sem, comm=None):
    if comm is None:
        return pl.pallas_call(body, name=name, grid=grid, in_specs=in_specs, out_specs=out_specs,
                              out_shape=out_shape, scratch_shapes=list(scratch_shapes),
                              compiler_params=_cparams(*sem))(*args)
    xs, gather = comm
    n = len(xs)
    single = not isinstance(out_shape, (tuple, list))
    outs_shape = (out_shape,) if single else tuple(out_shape)
    outs_specs = (out_specs,) if single else tuple(out_specs)
    n_in, n_out, n_scr = len(in_specs), len(outs_shape), len(scratch_shapes)

    def body2(*refs):
        ins, cx = refs[:n_in], refs[n_in:n_in + n]
        outs = refs[n_in + n:n_in + n + n_out]
        co = refs[n_in + n + n_out:n_in + 2 * n + n_out]
        scr = refs[n_in + 2 * n + n_out:n_in + 2 * n + n_out + n_scr]
        sems = refs[n_in + 2 * n + n_out + n_scr:]
        first = functools.reduce(jnp.logical_and, [pl.program_id(a) == 0 for a in range(len(grid))])
        last = functools.reduce(jnp.logical_and, [pl.program_id(a) == grid[a] - 1 for a in range(len(grid))])

        @pl.when(first)
        def _():
            for cp in _comm_copies(cx, co, *sems, gather):
                cp.start()

        body(*ins, *outs, *scr)

        @pl.when(last)
        def _():
            for cp in _comm_copies(cx, co, *sems, gather):
                cp.wait()

    res = pl.pallas_call(
        body2, name=name, grid=grid, in_specs=list(in_specs) + [_HBM] * n,
        out_specs=outs_specs + tuple([_HBM] * n), out_shape=outs_shape + _comm_out_shapes(xs),
        scratch_shapes=list(scratch_shapes) + _comm_sems(n),
        compiler_params=_cparams(*(["arbitrary"] * len(grid))))(*args, *xs)
    main = res[0] if single else tuple(res[:n_out])
    return main, list(res[n_out:])


def all_gather(x, name):
    return _exchange([x], True, name)[0]


def all_gather_many(xs, name):
    return _exchange(xs, True, name)


def all_to_all_many(xs, name):
    return _exchange(xs, False, name)


def reduce_adamw(gsrc, w, m, v, name):
    S, R, C = gsrc.shape
    tr = _rtile(R, max(16, min(256, (4 << 20) // (S * C * 4) // 16 * 16)), 16 if R % 16 == 0 else 8)
    c1 = 1.0 - ADAM_B1 ** ADAM_STEP
    c2 = 1.0 - ADAM_B2 ** ADAM_STEP

    def body(g_ref, w_ref, m_ref, v_ref, go_ref, d_ref, mo_ref, vo_ref):
        g = g_ref[0].astype(F32)
        for s in range(1, S):
            g = g + g_ref[s].astype(F32)
        go_ref[...] = g
        mn = ADAM_B1 * m_ref[...] + (1.0 - ADAM_B1) * g
        vn = ADAM_B2 * v_ref[...] + (1.0 - ADAM_B2) * (g * g)
        mo_ref[...] = mn
        vo_ref[...] = vn
        d_ref[...] = -ADAM_LR * ((mn / c1) / (jnp.sqrt(vn / c2) + ADAM_EPS) + ADAM_WD * w_ref[...])

    big = pl.BlockSpec((tr, C), lambda i: (i, 0))
    o = jax.ShapeDtypeStruct((R, C), F32)
    return pl.pallas_call(
        body, name=name, grid=(R // tr,),
        in_specs=[pl.BlockSpec((S, tr, C), lambda i: (0, i, 0)), big, big, big],
        out_specs=(big, big, big, big), out_shape=(o, o, o, o),
        compiler_params=_cparams("parallel"))(gsrc, w, m, v)


def _to_slabs(g, ax):
    shp = g.shape
    g = g.reshape(shp[:ax] + (N_DEV, shp[ax] // N_DEV) + shp[ax + 1:])
    return jnp.moveaxis(g, ax, 0).reshape(N_DEV, -1)


def _from_slabs(s, ax, shp):
    s = s.reshape((N_DEV,) + shp[:ax] + (shp[ax] // N_DEV,) + shp[ax + 1:])
    return jnp.moveaxis(s, 0, ax).reshape(shp)


def _pack_rows(flat, width, row_mult):
    n = flat.shape[-1]
    per = width * row_mult
    tot = -(-n // per) * per
    flat = jnp.pad(flat, [(0, 0)] * (flat.ndim - 1) + [(0, tot - n)])
    return flat.reshape(flat.shape[:-1] + (tot // width, width))


def _offsets(sizes):
    offs, o = [], 0
    for s in sizes:
        offs.append(o)
        o += s
    return offs


WEIGHTS = ['ada_w', 'ada_b', 'norm_mix', 'norm_ffn', 'attn_w_in', 'attn_q_norm_a', 'attn_k_norm_a', 'attn_q_norm_b',
           'attn_k_norm_b', 'attn_sinks', 'attn_w_out', 'rec_w_in', 's5_lambda_re', 's5_lambda_im', 's5_log_dt',
           's5_b_re', 's5_b_im', 's5_c_re', 's5_c_im', 's5_d', 's5_glu_w', 's5_glu_b', 'dn_conv', 'dn_a_log',
           'dn_dt_bias', 'dn_out_norm', 'rec_w_out', 'ffn_w_up', 'ffn_conv', 'ffn_w_down']
BIG = [('attn_w_in', (D, ATTN_IN // N_DEV)), ('attn_w_out', (D // N_DEV, D)), ('rec_w_in', (D // N_DEV, REC_PAD)),
       ('s5_glu_w', (S5_W // N_DEV, S5_W)), ('rec_w_out', (D // N_DEV, D)), ('ffn_w_up', (2 * D, 2 * D_FF // N_DEV)),
       ('ffn_w_down', (2 * D_FF // N_DEV, D))]


def _shard2d(name, t):
    if name == 'rec_w_in':
        return jnp.pad(t[0], ((0, 0), (0, REC_PAD - REC_IN)))
    return t.reshape((-1, t.shape[-1]))


def _cols_to_slabs(g, k=N_DEV):
    r, n = g.shape
    return jnp.transpose(g.reshape(r, k, n // k), (1, 0, 2))


def _slabs_to_cols(s):
    k, r, c_ = s.shape
    return jnp.transpose(s, (1, 0, 2)).reshape(r, k * c_)
SMALL_SHARDED = [('s5_d', 1, (1, S5_W)), ('s5_glu_b', 1, (1, S5_W)), ('dn_conv', 2, (1, 4, 2304)),
                 ('ffn_conv', 2, (2, 3, 2 * D_FF))]
REPLICATED = [('ada_b', (2, 6 * D)), ('norm_mix', (2, D)), ('norm_ffn', (2, D)), ('attn_q_norm_a', (1, HD)),
              ('attn_k_norm_a', (1, HD)), ('attn_q_norm_b', (1, HD)), ('attn_k_norm_b', (1, HD)),
              ('attn_sinks', (1, 8)), ('s5_lambda_re', (1, 16, 64)), ('s5_lambda_im', (1, 16, 64)),
              ('s5_log_dt', (1, 16)), ('s5_b_re', (1, 16, 64, 16)), ('s5_b_im', (1, 16, 64, 16)),
              ('s5_c_re', (1, 16, 16, 64)), ('s5_c_im', (1, 16, 16, 64)), ('dn_a_log', (1, DN_H)),
              ('dn_dt_bias', (1, DN_H)), ('dn_out_norm', (1, 128))]


def _numel(shp):
    return int(np.prod(shp))


def kernel(x, c, ada_w, ada_b, norm_mix, norm_ffn, attn_w_in, attn_q_norm_a, attn_k_norm_a, attn_q_norm_b, attn_k_norm_b, attn_sinks, attn_w_out, rec_w_in, s5_lambda_re, s5_lambda_im, s5_log_dt, s5_b_re, s5_b_im, s5_c_re, s5_c_im, s5_d, s5_glu_w, s5_glu_b, dn_conv, dn_a_log, dn_dt_bias, dn_out_norm, rec_w_out, ffn_w_up, ffn_conv, ffn_w_down, loss_target, m_ada_w, m_ada_b, m_norm_mix, m_norm_ffn, m_attn_w_in, m_attn_q_norm_a, m_attn_k_norm_a, m_attn_q_norm_b, m_attn_k_norm_b, m_attn_sinks, m_attn_w_out, m_rec_w_in, m_s5_lambda_re, m_s5_lambda_im, m_s5_log_dt, m_s5_b_re, m_s5_b_im, m_s5_c_re, m_s5_c_im, m_s5_d, m_s5_glu_w, m_s5_glu_b, m_dn_conv, m_dn_a_log, m_dn_dt_bias, m_dn_out_norm, m_rec_w_out, m_ffn_w_up, m_ffn_conv, m_ffn_w_down, v_ada_w, v_ada_b, v_norm_mix, v_norm_ffn, v_attn_w_in, v_attn_q_norm_a, v_attn_k_norm_a, v_attn_q_norm_b, v_attn_k_norm_b, v_attn_sinks, v_attn_w_out, v_rec_w_in, v_s5_lambda_re, v_s5_lambda_im, v_s5_log_dt, v_s5_b_re, v_s5_b_im, v_s5_c_re, v_s5_c_im, v_s5_d, v_s5_glu_w, v_s5_glu_b, v_dn_conv, v_dn_a_log, v_dn_dt_bias, v_dn_out_norm, v_rec_w_out, v_ffn_w_up, v_ffn_conv, v_ffn_w_down):
    loc = locals()
    W = {n: loc[n] for n in WEIGHTS}
    M = {n: loc["m_" + n] for n in WEIGHTS}
    V = {n: loc["v_" + n] for n in WEIGHTS}
    _, _, _, me = _mesh_pos()
    L = x.shape[1]
    x0, tgt = x[0], loss_target[0]

    small_in = jnp.concatenate([c.reshape(-1)] + [W[n].reshape(-1) for n, _, _ in SMALL_SHARDED])
    si, att_in_all, att_out_all = all_gather_many(
        [_pack_rows(small_in, 1024, 8), attn_w_in[0].astype(BF16), attn_w_out[0].astype(BF16)], "gather_first")
    si = si.reshape(N_DEV, -1)
    c_all = si[:, :D]
    off = D
    small_full = {}
    for n, ax, shp in SMALL_SHARDED:
        k = _numel(shp) // N_DEV
        small_full[n] = _from_slabs(si[:, off:off + k], ax, shp)
        off += k

    cond_all = jax.nn.silu(c_all)
    modp = jnp.concatenate([matmul([(cond_all, ada_w[l].astype(BF16))], "nn", f"ada{l}") for l in range(2)], axis=0)
    modp_all = all_gather(modp, "gather_mod")
    mods = []
    for l in range(2):
        row = lax.dynamic_index_in_dim(modp_all, l * N_DEV + me, axis=1, keepdims=False)
        mod = row.reshape(1, 6 * D) + ada_b[l].reshape(1, 6 * D)
        mods.append([mod[:, i * D:(i + 1) * D] for i in range(6)])

    w_att_in, w_att_out = _slabs_to_cols(att_in_all), att_out_all.reshape(D, D)
    bf = lambda t: t.astype(BF16)
    ffn_shards = [[bf(ffn_w_up[l]), bf(ffn_w_down[l])] for l in range(2)]
    rec_shards = [bf(_shard2d('rec_w_in', rec_w_in)), bf(s5_glu_w[0]), bf(rec_w_out[0])]
    ffn_cw = [small_full['ffn_conv'][l] for l in range(2)]
    dn_cw = small_full['dn_conv'][0]
    s5_dskip, glu_b = small_full['s5_d'], small_full['s5_glu_b']
    row = lambda t: t.reshape(1, -1)

    sh1, sc1, g1, sh2, sc2, g2 = mods[0]
    h1 = gate_norm_fwd(x0, None, None, row(norm_mix[0]), sh1, sc1, "l0_norm1")
    wvec, sinkvec = attn_vectors(attn_q_norm_a[0], attn_k_norm_a[0], attn_q_norm_b[0], attn_k_norm_b[0], attn_sinks[0])
    y0, res_att, got = attention_block_fwd(
        h1, w_att_in, wvec, sinkvec, w_att_out, "att",
        comms={'swa': (ffn_shards[0][:1], True), 1: (ffn_shards[0][1:], True), 4: (rec_shards, True)})
    split_up = lambda up_all: (_slabs_to_cols(up_all[:4]), _slabs_to_cols(up_all[4:]))
    w_up, w_down = [split_up(got['swa'][0])], [got[1][0].reshape(D_FF, D)]
    w_rec_in = rec_cols_permute(got[4][0].reshape(D, REC_PAD))
    glu_w, w_rec_out = got[4][1].reshape(S5_W, S5_W), got[4][2].reshape(D, D)
    w_rec_out = jnp.concatenate([w_rec_out[S5_W:], w_rec_out[:S5_W]], axis=0)
    x1, h2 = gate_norm_fwd(x0, y0, g1, row(norm_ffn[0]), sh2, sc2, "l0_norm2")
    f0, res_f0 = ffn_block_fwd(h2, w_up[0][0], w_up[0][1], ffn_cw[0], w_down[0], "ffn0")
    t1, tc1, tg1, t2, tc2, tg2 = mods[1]
    x2, h3 = gate_norm_fwd(x1, f0, g2, row(norm_mix[1]), t1, tc1, "l1_norm1")
    rin = matmul([(h3, w_rec_in)], "nn", "rec_in")
    s5p, s5p_vjp = jax.vjp(s5_params, s5_lambda_re[0], s5_lambda_im[0], s5_log_dt[0], s5_b_re[0], s5_b_im[0],
                           s5_c_re[0], s5_c_im[0])
    u = rin[:, REC_U0:REC_A0]
    yc, res_s5 = s5_block_fwd(u, s5p, s5_dskip, glu_w, glu_b, "s5")
    yd, res_dn, got_ffn1 = dn_block_fwd(rin, dn_cw, dn_a_log[0], dn_dt_bias[0], dn_out_norm[0], "dn",
                                        comm=(ffn_shards[1], True))
    w_up.append(split_up(got_ffn1[0]))
    w_down.append(got_ffn1[1].reshape(D_FF, D))
    ycat = jnp.concatenate([yd, yc], axis=1)
    y1 = matmul([(ycat, w_rec_out)], "nn", "rec_out")
    x3, h4 = gate_norm_fwd(x2, y1, tg1, row(norm_ffn[1]), t2, tc2, "l1_norm2")
    f1, res_f1 = ffn_block_fwd(h4, w_up[1][0], w_up[1][1], ffn_cw[1], w_down[1], "ffn1")
    dx4, df1, lsum = final_loss(x3, f1, tg2, tgt, "loss")

    G = {}
    d_tg2 = lsum[8:16].sum(axis=0)
    dh4, gf1, _ = ffn_block_bwd(df1, res_f1, w_up[1][0], w_up[1][1], ffn_cw[1], w_down[1], "ffn1")
    ffn_slabs = lambda g: [g['w_up'], g['w_down'].reshape(N_DEV, D_FF // N_DEV, D)]
    dx3, dy1, s = gate_norm_bwd(x3, y1, tg1, row(norm_ffn[1]), tc2, dx4, dh4, "l1_dnorm2")
    s = s.reshape(4, 8, D).sum(axis=1)
    d_tg1, d_nffn1, d_t2, d_tc2 = s[0], s[1] * (1.0 + tc2[0]), s[2], s[1] * norm_ffn[1]
    g_rec_out = matmul([(ycat, dy1)], "tn", "rec_out_dw", out_dtype=BF16)
    g_rec_out = jnp.concatenate([g_rec_out[DN_W:], g_rec_out[:DN_W]], axis=0).reshape(N_DEV, D // N_DEV, D)
    dycat = matmul([(dy1, w_rec_out)], "nt", "rec_out_dx")
    du, s5cot, gs5 = s5_block_bwd(dycat, res_s5, s5p, s5_dskip, glu_w, glu_b, "s5", dout_col=DN_W // S5_W)
    s5g = s5p_vjp(s5cot)
    dqkv, dz, da, dbraw, gdn, recv_ffn1 = dn_block_bwd(dycat, res_dn, rin, dn_cw, dn_out_norm[0], "dn",
                                                       comm=(ffn_slabs(gf1), False))
    drin = jnp.concatenate([dqkv, dz, du.astype(BF16), da.astype(BF16), dbraw.astype(BF16),
                            jnp.zeros((L, REC_PAD - REC_IN), BF16)], axis=1)
    g_rec_in = rec_cols_restore(matmul([(h3, drin)], "tn", "rec_in_dw", out_dtype=BF16)).reshape(
        N_DEV, D // N_DEV, REC_PAD)
    g_glu = gs5['glu_w'].astype(BF16).reshape(N_DEV, S5_W // N_DEV, S5_W)
    dh3 = matmul([(drin, w_rec_in)], "nt", "rec_in_dx")
    dx2, df0, s = gate_norm_bwd(x2, f0, g2, row(norm_mix[1]), tc1, dx3, dh3, "l1_dnorm1")
    s = s.reshape(4, 8, D).sum(axis=1)
    d_g2, d_nmix1, d_t1, d_tc1 = s[0], s[1] * (1.0 + tc1[0]), s[2], s[1] * norm_mix[1]
    dh2, gf0, recv_rec = ffn_block_bwd(df0, res_f0, w_up[0][0], w_up[0][1], ffn_cw[0], w_down[0], "ffn0",
                                       comm=([g_rec_in, g_glu, g_rec_out], False))
    dx1, dy0, s = gate_norm_bwd(x1, y0, g1, row(norm_ffn[0]), sc2, dx2, dh2, "l0_dnorm2")
    s = s.reshape(4, 8, D).sum(axis=1)
    d_g1, d_nffn0, d_sh2, d_sc2 = s[0], s[1] * (1.0 + sc2[0]), s[2], s[1] * norm_ffn[0]
    dh1, gatt, got_b = attention_block_bwd(dy0, res_att, w_att_in, wvec, sinkvec, w_att_out, "att",
                                           comms={'swa': (ffn_slabs(gf0)[:1], False), 1: (ffn_slabs(gf0)[1:], False)})
    recv_ffn0 = [got_b['swa'][0], got_b[1][0]]
    att_slabs = [_cols_to_slabs(gatt['w_in']), gatt['w_out'].reshape(N_DEV, D // N_DEV, D)]
    (grad_x, s), recv_att = gate_norm_bwd(x0, None, None, row(norm_mix[0]), sc1, dx1, dh1, "l0_dnorm1",
                                          comm=(att_slabs, False))
    s = s.reshape(4, 8, D).sum(axis=1)
    d_nmix0, d_sh1, d_sc1 = s[1] * (1.0 + sc1[0]), s[2], s[1] * norm_mix[0]
    dmod = jnp.stack([jnp.concatenate([d_sh1, d_sc1, d_g1, d_sh2, d_sc2, d_g2]),
                      jnp.concatenate([d_t1, d_tc1, d_tg1, d_t2, d_tc2, d_tg2])])

    P = {'ada_b': dmod, 'norm_mix': jnp.stack([d_nmix0, d_nmix1]), 'norm_ffn': jnp.stack([d_nffn0, d_nffn1]),
         'attn_q_norm_a': gatt['q_norm_a'], 'attn_k_norm_a': gatt['k_norm_a'], 'attn_q_norm_b': gatt['q_norm_b'],
         'attn_k_norm_b': gatt['k_norm_b'], 'attn_sinks': gatt['sinks'],
         's5_lambda_re': s5g[0], 's5_lambda_im': s5g[1], 's5_log_dt': s5g[2], 's5_b_re': s5g[3], 's5_b_im': s5g[4],
         's5_c_re': s5g[5], 's5_c_im': s5g[6], 'dn_a_log': gdn['a_log'], 'dn_dt_bias': gdn['dt_bias'],
         'dn_out_norm': gdn['out_norm'],
         's5_d': gs5['dskip'], 's5_glu_b': gs5['glu_b'], 'dn_conv': gdn['conv'],
         'ffn_conv': jnp.stack([gf0['conv'], gf1['conv']])}

    out = {k: {} for k in ("g", "d", "m", "v")}
    keys = ("g", "d", "m", "v")
    recv = {'attn_w_in': recv_att[0], 'attn_w_out': recv_att[1], 'rec_w_in': recv_rec[0], 's5_glu_w': recv_rec[1],
            'rec_w_out': recv_rec[2]}
    for n, gr_ in recv.items():
        res4 = reduce_adamw(gr_, _shard2d(n, W[n]), _shard2d(n, M[n]), _shard2d(n, V[n]), "adamw_" + n)
        for key, t in zip(keys, res4):
            out[key][n] = (t[:, :REC_IN] if n == 'rec_w_in' else t).reshape(W[n].shape)
    for n, idx in (('ffn_w_up', 0), ('ffn_w_down', 1)):
        per_layer = [reduce_adamw(r_[idx], W[n][l], M[n][l], V[n][l], f"adamw_{n}{l}")
                     for l, r_ in enumerate((recv_ffn0, recv_ffn1))]
        for i, key in enumerate(keys):
            out[key][n] = jnp.stack([per_layer[0][i], per_layer[1][i]])

    rep_sizes = [_numel(shp) for _, shp in REPLICATED]
    ss_sizes = [_numel(shp) for _, _, shp in SMALL_SHARDED]
    rep_offs = _offsets(rep_sizes + ss_sizes + [1])
    parts = [P[n].reshape(-1) for n, _ in REPLICATED] + [P[n].reshape(-1) for n, _, _ in SMALL_SHARDED]
    parts.append(lsum[0:8].sum().reshape(1))
    spack = _pack_rows(jnp.concatenate(parts), 1024, 8)
    sall = all_gather(spack, "gather_small_grads")
    n_rest = sum(ss_sizes) + 1
    pk = lambda d: _pack_rows(jnp.concatenate([d[n].reshape(-1) for n, _ in REPLICATED]
                                              + [jnp.zeros((n_rest,), F32)]), 1024, 8)
    sg, sd_, sm, sv = [t.reshape(-1) for t in reduce_adamw(sall, pk(W), pk(M), pk(V), "adamw_small")]
    loss = 0.5 * sg[rep_offs[-1]] / D

    dmod_all = sall.reshape(N_DEV, -1)[:, :2 * 6 * D].reshape(N_DEV, 2, 6 * D)
    dmod_mine = lax.dynamic_slice_in_dim(dmod_all, me * (6 * D // N_DEV), 6 * D // N_DEV, axis=2)
    g_ada = jnp.stack([matmul([(cond_all, dmod_mine[:, l])], "tn", f"ada{l}_dw") for l in range(2)])
    ada2d = lambda t: t.reshape(2 * D, 6 * D // N_DEV)
    for key, t in zip(("g", "d", "m", "v"), reduce_adamw(ada2d(g_ada)[None], ada2d(ada_w), ada2d(m_ada_w),
                                                          ada2d(v_ada_w), "adamw_ada_w")):
        out[key]['ada_w'] = t.reshape(ada_w.shape)
    own = []
    for (n, ax, shp), o in zip(SMALL_SHARDED, rep_offs[len(REPLICATED):]):
        slabs = _to_slabs(sg[o:o + _numel(shp)].reshape(shp), ax)
        own.append(lax.dynamic_index_in_dim(slabs, me, axis=0, keepdims=False))
    own_names = [n for n, _, _ in SMALL_SHARDED]
    pk = lambda d: _pack_rows(jnp.concatenate([d[n].reshape(-1) for n in own_names]), 1024, 8)
    og, od, om, ov = [t.reshape(-1) for t in reduce_adamw(_pack_rows(jnp.concatenate(own), 1024, 8)[None],
                                                          pk(W), pk(M), pk(V), "adamw_own")]

    def unpack(names_shapes, bufs):
        o = 0
        for n, shp in names_shapes:
            k = _numel(shp)
            for key, buf in zip(("g", "d", "m", "v"), bufs):
                out[key][n] = buf[o:o + k].reshape(shp)
            o += k

    unpack(REPLICATED, (sg, sd_, sm, sv))
    unpack([(n, W[n].shape) for n in own_names], (og, od, om, ov))
    return (loss, grad_x[None], *[out["g"][n] for n in WEIGHTS], *[out["d"][n] for n in WEIGHTS],
            *[out["m"][n] for n in WEIGHTS], *[out["v"][n] for n in WEIGHTS])
```

```python
import functools
import math

import numpy as np
import jax
import jax.numpy as jnp
from jax import lax
from jax.experimental import pallas as pl
from jax.experimental.pallas import tpu as pltpu

F32 = jnp.float32
BF16 = jnp.bfloat16

N_DEV = 8
D = 1024
HD = 64
BLK = 128
ATTN_IN = 2304
CB = ATTN_IN // 128
B_BRANCHES = ((128, 1), (512, 4), (2048, 16))
S5_W = 256
S5_P = 1024
DN_H = 6
DN_DK = 128
DN_C = 64
REC_IN = 3340
REC_PAD = 3456
D_FF = 2816
EPS = 1e-6
ADAM_LR, ADAM_B1, ADAM_B2, ADAM_EPS, ADAM_WD, ADAM_STEP = 0.001, 0.9, 0.999, 1e-8, 0.01, 10
VMEM_LIMIT = 48 * 1024 * 1024

ALIBI = np.asarray(2.0 ** (-8.0 * np.arange(1, 17) / 16), dtype=np.float32)


def _cparams(*sem):
    return pltpu.CompilerParams(dimension_semantics=tuple(sem), vmem_limit_bytes=VMEM_LIMIT)


def _tile(n, target):
    if n <= target:
        return n
    best = None
    for t in range(128, target + 1, 128):
        if n % t == 0:
            best = t
    assert best is not None, (n, target)
    return best


def _rtile(n, target, mult=8):
    if n <= target:
        return n
    best = None
    for t in range(mult, target + 1, mult):
        if n % t == 0:
            best = t
    assert best is not None, (n, target)
    return best


def _fold8(x):
    r, c = x.shape
    return x.reshape(r // 8, 8, c).sum(axis=0)


def _sigmoid(x):
    return 1.0 / (1.0 + jnp.exp(-x))


_DIMS = {"nn": (((1,), (0,)), ((), ())), "nt": (((1,), (1,)), ((), ())), "tn": (((0,), (0,)), ((), ()))}


MM_FULL_K = 3584


MM_VMEM_BUDGET = 40 << 20


def matmul(pairs, mode, name, out_dtype=F32, tm=1024, tn=1536, tk=1024):
    a0, b0 = pairs[0]
    if mode == "nn":
        (M, K), N = a0.shape, b0.shape[1]
    elif mode == "nt":
        (M, K), N = a0.shape, b0.shape[0]
    else:
        (K, M), N = a0.shape, b0.shape[1]
        tm = 1536
    tn = _tile(N, tn)
    tk = K if K <= MM_FULL_K else _tile(K, tk)
    nk = K // tk
    npair = len(pairs)
    dims = _DIMS[mode]

    def planned(tm_):
        ab = sum(tm_ * tk * a.dtype.itemsize + tk * tn * b.dtype.itemsize for a, b in pairs)
        return 2 * ab + 2 * tm_ * tn * jnp.dtype(out_dtype).itemsize + (tm_ * tn * 4 if nk > 1 else 0)

    while True:
        tm_try = _rtile(M, tm) if M % 128 else _tile(M, tm)
        if planned(tm_try) <= MM_VMEM_BUDGET or tm <= 128:
            break
        tm //= 2
    tm = tm_try

    def body(*refs):
        o_ref = refs[2 * npair]
        tot = None
        for p in range(npair):
            part = lax.dot_general(refs[2 * p][...].astype(BF16), refs[2 * p + 1][...].astype(BF16),
                                   dims, preferred_element_type=F32)
            tot = part if tot is None else tot + part
        if nk == 1:
            o_ref[...] = tot.astype(o_ref.dtype)
            return
        acc_ref = refs[2 * npair + 1]
        k = pl.program_id(2)

        @pl.when(k == 0)
        def _():
            acc_ref[...] = tot

        @pl.when(k > 0)
        def _():
            acc_ref[...] += tot

        @pl.when(k == nk - 1)
        def _():
            o_ref[...] = acc_ref[...].astype(o_ref.dtype)

    if mode == "nn":
        a_spec = pl.BlockSpec((tm, tk), lambda j, i, k: (i, k))
        b_spec = pl.BlockSpec((tk, tn), lambda j, i, k: (k, j))
    elif mode == "nt":
        a_spec = pl.BlockSpec((tm, tk), lambda j, i, k: (i, k))
        b_spec = pl.BlockSpec((tn, tk), lambda j, i, k: (j, k))
    else:
        a_spec = pl.BlockSpec((tk, tm), lambda j, i, k: (k, i))
        b_spec = pl.BlockSpec((tk, tn), lambda j, i, k: (k, j))
    flat = [t for pr in pairs for t in pr]
    return pl.pallas_call(
        body, name=name, grid=(N // tn, M // tm, nk),
        in_specs=[a_spec, b_spec] * npair,
        out_specs=pl.BlockSpec((tm, tn), lambda j, i, k: (i, j)),
        out_shape=jax.ShapeDtypeStruct((M, N), out_dtype),
        scratch_shapes=[pltpu.VMEM((tm, tn), F32)] if nk > 1 else [],
        compiler_params=_cparams("parallel", "parallel", "arbitrary"),
    )(*flat)


def gate_norm_fwd(x, y, gate, nw, sh, sc, name):
    L, C = x.shape
    tl = _rtile(L, 512)
    has_gate = y is not None

    def body(*refs):
        if has_gate:
            x_ref, y_ref, g_ref, nw_ref, sh_ref, sc_ref, xn_ref, h_ref = refs
            xn = x_ref[...] + g_ref[...] * y_ref[...]
            xn_ref[...] = xn
        else:
            x_ref, nw_ref, sh_ref, sc_ref, h_ref = refs
            xn = x_ref[...]
        r = lax.rsqrt(jnp.mean(xn * xn, axis=-1, keepdims=True) + EPS)
        h = (xn * r * nw_ref[...]) * (1.0 + sc_ref[...]) + sh_ref[...]
        h_ref[...] = h.astype(BF16)

    big = pl.BlockSpec((tl, C), lambda i: (i, 0))
    vec = pl.BlockSpec((1, C), lambda i: (0, 0))
    if has_gate:
        ins, in_specs = (x, y, gate, nw, sh, sc), [big, big, vec, vec, vec, vec]
        out_shape = (jax.ShapeDtypeStruct((L, C), F32), jax.ShapeDtypeStruct((L, C), BF16))
        out_specs = (big, big)
    else:
        ins, in_specs = (x, nw, sh, sc), [big, vec, vec, vec]
        out_shape = jax.ShapeDtypeStruct((L, C), BF16)
        out_specs = big
    return pl.pallas_call(body, name=name, grid=(L // tl,), in_specs=in_specs, out_specs=out_specs,
                          out_shape=out_shape, compiler_params=_cparams("parallel"))(*ins)


def gate_norm_bwd(xn, y, gate, nw, sc, dxn_direct, dh, name, comm=None):
    L, C = xn.shape
    tl = _rtile(L, 256)
    has_gate = y is not None
    has_direct = dxn_direct is not None

    def body(*refs):
        refs = list(refs)
        xn_ref = refs.pop(0)
        y_ref = refs.pop(0) if has_gate else None
        g_ref = refs.pop(0) if has_gate else None
        nw_ref = refs.pop(0)
        sc_ref = refs.pop(0)
        dd_ref = refs.pop(0) if has_direct else None
        dh_ref = refs.pop(0)
        dxn_ref = refs.pop(0)
        dy_ref = refs.pop(0) if has_gate else None
        sums_ref = refs.pop(0)

        @pl.when(pl.program_id(0) == 0)
        def _():
            sums_ref[...] = jnp.zeros_like(sums_ref)

        xv = xn_ref[...]
        dh_v = dh_ref[...]
        r = lax.rsqrt(jnp.mean(xv * xv, axis=-1, keepdims=True) + EPS)
        n = xv * r
        a = nw_ref[...] * (1.0 + sc_ref[...])
        dn = dh_v * a
        dx = r * (dn - n * jnp.mean(dn * n, axis=-1, keepdims=True))
        if has_direct:
            dx = dx + dd_ref[...]
        dxn_ref[...] = dx
        sums_ref[8:16, :] += _fold8(dh_v * n)
        sums_ref[16:24, :] += _fold8(dh_v)
        if has_gate:
            dy_ref[...] = (dx * g_ref[...]).astype(BF16)
            sums_ref[0:8, :] += _fold8(dx * y_ref[...])

    big = pl.BlockSpec((tl, C), lambda i: (i, 0))
    vec = pl.BlockSpec((1, C), lambda i: (0, 0))
    ins, in_specs = [xn], [big]
    if has_gate:
        ins += [y, gate]
        in_specs += [big, vec]
    ins += [nw, sc]
    in_specs += [vec, vec]
    if has_direct:
        ins.append(dxn_direct)
        in_specs.append(big)
    ins.append(dh)
    in_specs.append(big)
    out_shape = [jax.ShapeDtypeStruct((L, C), F32)]
    out_specs = [big]
    if has_gate:
        out_shape.append(jax.ShapeDtypeStruct((L, C), BF16))
        out_specs.append(big)
    out_shape.append(jax.ShapeDtypeStruct((32, C), F32))
    out_specs.append(pl.BlockSpec((32, C), lambda i: (0, 0)))
    return _call(body, ins, name=name, grid=(L // tl,), in_specs=in_specs, out_specs=tuple(out_specs),
                 out_shape=tuple(out_shape), sem=("arbitrary",), comm=comm)


def final_loss(x, f, gate, target, name):
    L, C = x.shape
    tl = _rtile(L, 256)

    def body(x_ref, f_ref, g_ref, t_ref, dy_ref, df_ref, sums_ref):
        @pl.when(pl.program_id(0) == 0)
        def _():
            sums_ref[...] = jnp.zeros_like(sums_ref)

        fv = f_ref[...]
        err = x_ref[...] + g_ref[...] * fv - t_ref[...]
        dy = err * (1.0 / C)
        dy_ref[...] = dy
        df_ref[...] = (dy * g_ref[...]).astype(BF16)
        sums_ref[0:8, :] += _fold8(err * err)
        sums_ref[8:16, :] += _fold8(dy * fv)

    big = pl.BlockSpec((tl, C), lambda i: (i, 0))
    vec = pl.BlockSpec((1, C), lambda i: (0, 0))
    return pl.pallas_call(
        body, name=name, grid=(L // tl,), in_specs=[big, big, vec, big],
        out_specs=(big, big, pl.BlockSpec((16, C), lambda i: (0, 0))),
        out_shape=(jax.ShapeDtypeStruct((L, C), F32), jax.ShapeDtypeStruct((L, C), BF16),
                   jax.ShapeDtypeStruct((16, C), F32)),
        compiler_params=_cparams("arbitrary"))(x, f, gate, target)


def _seg_ones(seg):
    r = lax.broadcasted_iota(jnp.int32, (128, 128), 0) // seg
    c = lax.broadcasted_iota(jnp.int32, (128, 128), 1) // seg
    return (r == c).astype(BF16)


def _segsum(t, ones):
    hi = t.astype(BF16)
    lo = (t - hi.astype(F32)).astype(BF16)
    return (jnp.dot(hi, ones, preferred_element_type=F32) + jnp.dot(lo, ones, preferred_element_type=F32))


_NORMED_TILES = tuple(list(range(0, 5)) + list(range(6, 14)))


DIL = (4, 16)
B_COLS0, B_W = 768, 1536
DIL_TL = 256


def _to_dilated(scr_ref, out_ref, d, cast=None):
    nj, tl, _ = scr_ref.shape
    for r in range(d):
        for j in range(nj):
            piece = scr_ref[j, pl.ds(r, tl // d, stride=d), :]
            c0 = (r * nj + j) * 128
            out_ref[:, c0:c0 + 128] = piece if cast is None else piece.astype(cast)


def _from_dilated(in_ref, scr_ref, d):
    nj, tl, _ = scr_ref.shape
    for r in range(d):
        for j in range(nj):
            c0 = (r * nj + j) * 128
            scr_ref[j, pl.ds(r, tl // d, stride=d), :] = in_ref[:, c0:c0 + 128]


def _dil_spec(tl, d, width):
    return pl.BlockSpec((tl // d, d * width), lambda i: (i, 0))


def qknorm_fwd(qkv, wvec, name):
    L, C = qkv.shape
    tl = DIL_TL

    def body(x_ref, w_ref, o_ref, o4_ref, o16_ref, scr_ref):
        ones = _seg_ones(HD)
        for t in range(CB):
            cs = slice(t * 128, (t + 1) * 128)
            x = x_ref[:, cs]
            if t in _NORMED_TILES:
                ms = _segsum(x * x, ones) * (1.0 / HD)
                x = x * lax.rsqrt(ms + EPS) * w_ref[:, cs]
            o_ref[:, cs] = x.astype(BF16)
            if t * 128 >= B_COLS0:
                scr_ref[t - B_COLS0 // 128] = x
        _to_dilated(scr_ref, o4_ref, 4, BF16)
        _to_dilated(scr_ref, o16_ref, 16, BF16)

    return pl.pallas_call(
        body, name=name, grid=(L // tl,),
        in_specs=[pl.BlockSpec((tl, C), lambda i: (i, 0)), pl.BlockSpec((1, C), lambda i: (0, 0))],
        out_specs=(pl.BlockSpec((tl, C), lambda i: (i, 0)), _dil_spec(tl, 4, B_W), _dil_spec(tl, 16, B_W)),
        out_shape=(jax.ShapeDtypeStruct((L, C), BF16), jax.ShapeDtypeStruct((L // 4, 4 * B_W), BF16),
                   jax.ShapeDtypeStruct((L // 16, 16 * B_W), BF16)),
        scratch_shapes=[pltpu.VMEM((B_W // 128, tl, 128), F32)], compiler_params=_cparams("parallel"))(qkv, wvec)


def qknorm_bwd(qkv, wvec, d_a, d_b, name):
    L, C = qkv.shape
    tl = DIL_TL

    def body(x_ref, w_ref, dqa, dka, dva, q1, k1, v1, q4, k4, v4, q16, k16, v16, dx_ref, sums_ref,
             dy_ref, s4_ref, s16_ref):
        @pl.when(pl.program_id(0) == 0)
        def _():
            sums_ref[...] = jnp.zeros_like(sums_ref)

        dy_ref[:, 0:512] = dqa[...]
        for off, ref in ((512, dka), (640, dva)):
            for g in range(2):
                acc = ref[:, g * 256:g * 256 + HD]
                for h in range(1, 4):
                    acc = acc + ref[:, g * 256 + h * HD:g * 256 + (h + 1) * HD]
                dy_ref[:, off + g * HD:off + (g + 1) * HD] = acc
        for off, r1, r4, r16 in ((768, q1, q4, q16), (1280, k1, k4, k16), (1792, v1, v4, v16)):
            _from_dilated(r4, s4_ref, 4)
            _from_dilated(r16, s16_ref, 16)
            for j in range(4):
                dy_ref[:, off + j * 128:off + (j + 1) * 128] = r1[:, j * 128:(j + 1) * 128] + s4_ref[j] + s16_ref[j]

        ones = _seg_ones(HD)
        for t in range(CB):
            cs = slice(t * 128, (t + 1) * 128)
            d = dy_ref[:, cs]
            if t in _NORMED_TILES:
                x = x_ref[:, cs]
                r = lax.rsqrt(_segsum(x * x, ones) * (1.0 / HD) + EPS)
                n = x * r
                dn = d * w_ref[:, cs]
                dx_ref[:, cs] = (r * (dn - n * (_segsum(dn * n, ones) * (1.0 / HD)))).astype(BF16)
                sums_ref[:, cs] += _fold8(d * n)
            else:
                dx_ref[:, cs] = d.astype(BF16)

    big = pl.BlockSpec((tl, C), lambda i: (i, 0))
    p512 = pl.BlockSpec((tl, 512), lambda i: (i, 0))
    return pl.pallas_call(
        body, name=name, grid=(L // tl,),
        in_specs=[big, pl.BlockSpec((1, C), lambda i: (0, 0))] + [p512] * 6 + [_dil_spec(tl, 4, 512)] * 3
        + [_dil_spec(tl, 16, 512)] * 3,
        out_specs=(big, pl.BlockSpec((8, C), lambda i: (0, 0))),
        out_shape=(jax.ShapeDtypeStruct((L, C), BF16), jax.ShapeDtypeStruct((8, C), F32)),
        scratch_shapes=[pltpu.VMEM((tl, C), F32), pltpu.VMEM((4, tl, 128), F32), pltpu.VMEM((4, tl, 128), F32)],
        compiler_params=_cparams("arbitrary"))(qkv, wvec, *d_a, *d_b[0], *d_b[1], *d_b[2])


def _attn_scores(q, kw, n, slope, step, maxdist):
    s = lax.dot_general(q, kw, (((1,), (1,)), ((), ())), preferred_element_type=F32) * (HD ** -0.5)
    qi = lax.broadcasted_iota(jnp.int32, (BLK, 2 * BLK), 0)
    sj = lax.broadcasted_iota(jnp.int32, (BLK, 2 * BLK), 1)
    dist = BLK + qi - sj
    valid = (dist >= 0) & (dist <= maxdist) & ((n > 0) | (sj >= BLK))
    bias = (-slope) * (step * dist).astype(F32)
    return jnp.where(valid, s + bias, -jnp.inf), valid


ATT_NQ = 4


def _attn_operands(hp, gqa, q_ref, kh_ref, kc_ref, vh_ref, vc_ref):
    ops = []
    for b in range(ATT_NQ):
        rows = slice(b * BLK, (b + 1) * BLK)
        prev = slice((b - 1) * BLK, b * BLK)
        for e in range(2):
            cs = slice(e * HD, (e + 1) * HD)
            if gqa:
                ksel = lambda ref, r: jnp.where(hp >= 2, ref[r, 64:128], ref[r, 0:64])
            else:
                ksel = lambda ref, r, cs=cs: ref[r, cs]
            kprev = ksel(kh_ref, slice(0, BLK)) if b == 0 else ksel(kc_ref, prev)
            vprev = ksel(vh_ref, slice(0, BLK)) if b == 0 else ksel(vc_ref, prev)
            ops.append((b, e, rows, cs, q_ref[rows, cs], jnp.concatenate([kprev, ksel(kc_ref, rows)], axis=0),
                        jnp.concatenate([vprev, ksel(vc_ref, rows)], axis=0)))
    return ops


def _attn_specs(cb, q_off, k_off, v_off, gqa):
    kcol = (lambda r, hp: r * cb + k_off) if gqa else (lambda r, hp: r * cb + k_off + hp)
    vcol = (lambda r, hp: r * cb + v_off) if gqa else (lambda r, hp: r * cb + v_off + hp)
    return kcol, vcol


def attn_fwd(X, d, q_off, k_off, v_off, gqa, slope0, maxdist, name, comm=None):
    Ls = X.shape[0]
    TQ = ATT_NQ * BLK
    nt = Ls // TQ
    slopes = jnp.asarray(ALIBI)

    def body(sl_ref, q_ref, kh_ref, kc_ref, vh_ref, vc_ref, o_ref, lse_ref):
        hp, t = pl.program_id(1), pl.program_id(2)
        ops = _attn_operands(hp, gqa, q_ref, kh_ref, kc_ref, vh_ref, vc_ref)
        s = [_attn_scores(q, kw, t if b == 0 else 1, sl_ref[slope0 + 2 * hp + e], d, maxdist)[0]
             for (b, e, rows, cs, q, kw, vw) in ops]
        m = [jnp.max(x, axis=-1, keepdims=True) for x in s]
        p = [jnp.exp(x - mm) for x, mm in zip(s, m)]
        l = [jnp.sum(x, axis=-1, keepdims=True) for x in p]
        o = [jnp.dot(x.astype(BF16), op[6], preferred_element_type=F32) / ll for x, op, ll in zip(p, ops, l)]
        for (b, e, rows, cs, q, kw, vw), oo, mm, ll in zip(ops, o, m, l):
            o_ref[rows, cs] = oo
            lse_ref[rows, cs] = jnp.broadcast_to(mm + jnp.log(ll), (BLK, HD))

    cb = X.shape[1] // (d * 128)
    kcol, vcol = _attn_specs(cb, q_off, k_off, v_off, gqa)
    tile, blk = (TQ, 128), (BLK, 128)
    halo = lambda t: jnp.maximum(t * ATT_NQ - 1, 0)
    in_specs = [
        pl.BlockSpec(memory_space=pltpu.SMEM),
        pl.BlockSpec(tile, lambda r, hp, t: (t, r * cb + q_off + hp)),
        pl.BlockSpec(blk, lambda r, hp, t: (halo(t), kcol(r, hp))),
        pl.BlockSpec(tile, lambda r, hp, t: (t, kcol(r, hp))),
        pl.BlockSpec(blk, lambda r, hp, t: (halo(t), vcol(r, hp))),
        pl.BlockSpec(tile, lambda r, hp, t: (t, vcol(r, hp))),
    ]
    out_spec = pl.BlockSpec(tile, lambda r, hp, t: (t, r * 4 + hp))
    out = jax.ShapeDtypeStruct((Ls, d * 512), F32)
    return _call(body, (slopes, X, X, X, X, X), name=name, grid=(d, 4, nt), in_specs=in_specs,
                 out_specs=(out_spec, out_spec), out_shape=(out, out),
                 sem=("parallel", "parallel", "arbitrary"), comm=comm)


def attn_bwd(X, o, lse, do, dlse, d, q_off, k_off, v_off, gqa, slope0, maxdist, name, comm=None):
    Ls = X.shape[0]
    slopes = jnp.asarray(ALIBI)

    TQ = ATT_NQ * BLK
    nt = Ls // TQ
    nt_dims, tn_dims = (((1,), (1,)), ((), ())), (((0,), (0,)), ((), ()))

    def body(sl_ref, q_ref, kh_ref, kc_ref, vh_ref, vc_ref, o_ref, lse_ref, do_ref, dlse_ref,
             dq_ref, dk_ref, dv_ref, ak_ref, av_ref, pk_ref, pv_ref):
        hp, t = pl.program_id(1), pl.program_id(2)

        @pl.when(t == 0)
        def _():
            pk_ref[...] = jnp.zeros_like(pk_ref)
            pv_ref[...] = jnp.zeros_like(pv_ref)

        @pl.when(t < nt)
        def _():
            ops = _attn_operands(hp, gqa, q_ref, kh_ref, kc_ref, vh_ref, vc_ref)
            sv = [_attn_scores(q, kw, t if b == 0 else 1, sl_ref[slope0 + 2 * hp + e], d, maxdist)
                  for (b, e, rows, cs, q, kw, vw) in ops]
            p = [jnp.where(valid, jnp.exp(s - lse_ref[op[2], op[1] * HD:op[1] * HD + 1]), 0.0)
                 for (s, valid), op in zip(sv, ops)]
            dov = [do_ref[op[2], op[3]] for op in ops]
            delta = [jnp.sum(dd * o_ref[op[2], op[3]], axis=-1, keepdims=True) for dd, op in zip(dov, ops)]
            dob = [dd.astype(BF16) for dd in dov]
            dp = [lax.dot_general(dd, op[6], nt_dims, preferred_element_type=F32) for dd, op in zip(dob, ops)]
            ds = [(pp * (x - dl + dlse_ref[op[2], op[1] * HD:op[1] * HD + 1])).astype(BF16)
                  for pp, x, dl, op in zip(p, dp, delta, ops)]
            dq = [jnp.dot(x, op[5], preferred_element_type=F32) * (HD ** -0.5) for x, op in zip(ds, ops)]
            dkw = [lax.dot_general(x, op[4], tn_dims, preferred_element_type=F32) * (HD ** -0.5)
                   for x, op in zip(ds, ops)]
            dvw = [lax.dot_general(pp.astype(BF16), dd, tn_dims, preferred_element_type=F32)
                   for pp, dd in zip(p, dob)]
            ak_ref[...] = jnp.zeros_like(ak_ref)
            av_ref[...] = jnp.zeros_like(av_ref)
            for (b, e, rows, cs, q, kw, vw), x, yk, yv in zip(ops, dq, dkw, dvw):
                dq_ref[rows, cs] = x
                ak_ref[b * BLK:(b + 2) * BLK, cs] += yk
                av_ref[b * BLK:(b + 2) * BLK, cs] += yv
            if nt == 1:
                dk_ref[...] = ak_ref[BLK:, :]
                dv_ref[...] = av_ref[BLK:, :]
                return
            last = slice(TQ - BLK, TQ)
            dk_ref[...] = pk_ref[...]
            dv_ref[...] = pv_ref[...]
            dk_ref[last, :] += ak_ref[0:BLK, :]
            dv_ref[last, :] += av_ref[0:BLK, :]
            pk_ref[...] = ak_ref[BLK:, :]
            pv_ref[...] = av_ref[BLK:, :]

        @pl.when(t == nt)
        def _():
            dk_ref[...] = pk_ref[...]
            dv_ref[...] = pv_ref[...]

    cb = X.shape[1] // (d * 128)
    kcol, vcol = _attn_specs(cb, q_off, k_off, v_off, gqa)
    tile, blk = (TQ, 128), (BLK, 128)
    cur = lambda t: jnp.minimum(t, nt - 1)
    halo = lambda t: jnp.maximum(cur(t) * ATT_NQ - 1, 0)
    ospec = pl.BlockSpec(tile, lambda r, hp, t: (cur(t), r * 4 + hp))
    in_specs = [
        pl.BlockSpec(memory_space=pltpu.SMEM),
        pl.BlockSpec(tile, lambda r, hp, t: (cur(t), r * cb + q_off + hp)),
        pl.BlockSpec(blk, lambda r, hp, t: (halo(t), kcol(r, hp))),
        pl.BlockSpec(tile, lambda r, hp, t: (cur(t), kcol(r, hp))),
        pl.BlockSpec(blk, lambda r, hp, t: (halo(t), vcol(r, hp))),
        pl.BlockSpec(tile, lambda r, hp, t: (cur(t), vcol(r, hp))),
        ospec, ospec, ospec, ospec,
    ]
    shifted = pl.BlockSpec(tile, lambda r, hp, t: (jnp.maximum(t - 1, 0), r * 4 + hp))
    out = jax.ShapeDtypeStruct((Ls, d * 512), F32)
    return _call(body, (slopes, X, X, X, X, X, o, lse, do, dlse), name=name, grid=(d, 4, nt + 1 if nt > 1 else 1),
                 in_specs=in_specs, out_specs=(ospec, shifted, shifted), out_shape=(out, out, out),
                 scratch_shapes=[pltpu.VMEM((TQ + BLK, 128), F32), pltpu.VMEM((TQ + BLK, 128), F32),
                                 pltpu.VMEM((TQ, 128), F32), pltpu.VMEM((TQ, 128), F32)],
                 sem=("parallel", "parallel", "arbitrary"), comm=comm)


def attn_merge_fwd(oa, la, sink, obs, lbs, name):
    L = oa.shape[0]
    tl = DIL_TL

    def body(oa_ref, la_ref, sk_ref, o1, o4, o16, l1, l4, l16, m_ref, so4, so16, sl4, sl16):
        m_ref[:, 0:512] = (oa_ref[...] * _sigmoid(la_ref[...] - sk_ref[...])).astype(BF16)
        for src, dst, d in ((o4, so4, 4), (o16, so16, 16), (l4, sl4, 4), (l16, sl16, 16)):
            _from_dilated(src, dst, d)
        for j in range(4):
            cs = slice(j * 128, (j + 1) * 128)
            a, b, c = l1[:, cs], sl4[j], sl16[j]
            mx = jnp.maximum(jnp.maximum(a, b), c)
            ea, eb, ec = jnp.exp(a - mx), jnp.exp(b - mx), jnp.exp(c - mx)
            inv = 1.0 / (ea + eb + ec)
            m_ref[:, 512 + j * 128:512 + (j + 1) * 128] = (
                (ea * inv) * o1[:, cs] + (eb * inv) * so4[j] + (ec * inv) * so16[j]).astype(BF16)

    big = pl.BlockSpec((tl, 512), lambda i: (i, 0))
    dil = [big, _dil_spec(tl, 4, 512), _dil_spec(tl, 16, 512)]
    return pl.pallas_call(
        body, name=name, grid=(L // tl,),
        in_specs=[big, big, pl.BlockSpec((1, 512), lambda i: (0, 0))] + dil + dil,
        out_specs=pl.BlockSpec((tl, 1024), lambda i: (i, 0)),
        out_shape=jax.ShapeDtypeStruct((L, 1024), BF16), scratch_shapes=[pltpu.VMEM((4, tl, 128), F32)] * 4,
        compiler_params=_cparams("parallel"),
    )(oa, la, sink, *obs, *lbs)


def attn_merge_bwd(dm, oa, la, sink, obs, lbs, name):
    L = oa.shape[0]
    tl = DIL_TL

    def body(dm_ref, oa_ref, la_ref, sk_ref, o1, o4, o16, l1, l4, l16,
             doa_ref, dla_ref, d1, d4, d16, g1, g4, g16, sums_ref, so4, so16, sl4, sl16, sd4, sd16, sg4, sg16):
        @pl.when(pl.program_id(0) == 0)
        def _():
            sums_ref[...] = jnp.zeros_like(sums_ref)

        for src, dst, d in ((o4, so4, 4), (o16, so16, 16), (l4, sl4, 4), (l16, sl16, 16)):
            _from_dilated(src, dst, d)
        ones = _seg_ones(HD)
        for t in range(4):
            cs = slice(t * 128, (t + 1) * 128)
            dma = dm_ref[:, cs]
            keep = _sigmoid(la_ref[:, cs] - sk_ref[:, cs])
            doa_ref[:, cs] = dma * keep
            tt = dma * oa_ref[:, cs] * keep * (1.0 - keep)
            dla_ref[:, cs] = _segsum(tt, ones)
            sums_ref[:, cs] += _fold8(-tt)
            dmb = dm_ref[:, 512 + t * 128:512 + (t + 1) * 128]
            a, b, c = l1[:, cs], sl4[t], sl16[t]
            mx = jnp.maximum(jnp.maximum(a, b), c)
            ea, eb, ec = jnp.exp(a - mx), jnp.exp(b - mx), jnp.exp(c - mx)
            inv = 1.0 / (ea + eb + ec)
            wa, wb, wc = ea * inv, eb * inv, ec * inv
            d1[:, cs] = wa * dmb
            sd4[t] = wb * dmb
            sd16[t] = wc * dmb
            sa = _segsum(dmb * o1[:, cs], ones)
            sb = _segsum(dmb * so4[t], ones)
            sc_ = _segsum(dmb * so16[t], ones)
            mean = wa * sa + wb * sb + wc * sc_
            g1[:, cs] = wa * (sa - mean)
            sg4[t] = wb * (sb - mean)
            sg16[t] = wc * (sc_ - mean)
        for src, dst, d in ((sd4, d4, 4), (sd16, d16, 16), (sg4, g4, 4), (sg16, g16, 16)):
            _to_dilated(src, dst, d)

    big = pl.BlockSpec((tl, 512), lambda i: (i, 0))
    dil = [big, _dil_spec(tl, 4, 512), _dil_spec(tl, 16, 512)]
    sd = jax.ShapeDtypeStruct
    shp = [sd((L, 512), F32), sd((L // 4, 4 * 512), F32), sd((L // 16, 16 * 512), F32)]
    return pl.pallas_call(
        body, name=name, grid=(L // tl,),
        in_specs=[pl.BlockSpec((tl, 1024), lambda i: (i, 0)), big, big,
                  pl.BlockSpec((1, 512), lambda i: (0, 0))] + dil + dil,
        out_specs=tuple([big, big] + dil + dil + [pl.BlockSpec((8, 512), lambda i: (0, 0))]),
        out_shape=tuple([shp[0], shp[0]] + shp + shp + [sd((8, 512), F32)]),
        scratch_shapes=[pltpu.VMEM((4, tl, 128), F32)] * 8, compiler_params=_cparams("arbitrary"),
    )(dm, oa, la, sink, *obs, *lbs)


def _shift_down(x, halo, k, first):
    rows = lax.broadcasted_iota(jnp.int32, (8, x.shape[1]), 0)
    out = pltpu.roll(x, k, axis=0)
    hrows = jnp.where(first, 0.0, pltpu.roll(halo, k, axis=0))
    top = jnp.where(rows < k, hrows, out[0:8, :])
    return jnp.concatenate([top, out[8:, :]], axis=0)


def _shift_up(x, nxt, k):
    tl = x.shape[0]
    rows = lax.broadcasted_iota(jnp.int32, (8, x.shape[1]), 0)
    out = pltpu.roll(x, tl - k, axis=0)
    bottom = jnp.where(rows >= 8 - k, pltpu.roll(nxt, 8 - k, axis=0), out[tl - 8:, :])
    return jnp.concatenate([out[:tl - 8, :], bottom], axis=0)


def _silu(x):
    return x * _sigmoid(x)


def _dsilu(x):
    s = _sigmoid(x)
    return s * (1.0 + x * (1.0 - s))


def ffn_act_fwd(ua, ub, cw, name):
    L, F = ua.shape
    tl = _rtile(L, 256)
    tc = _tile(F, 1408)
    hb = tl // 8

    def body(ua_ref, uah_ref, ub_ref, ubh_ref, wa_ref, wb_ref, o_ref):
        first = pl.program_id(1) == 0

        def conv(x_ref, h_ref, w_ref):
            x = x_ref[...]
            h = h_ref[...]
            return (w_ref[2:3, :] * x + w_ref[1:2, :] * _shift_down(x, h, 1, first)
                    + w_ref[0:1, :] * _shift_down(x, h, 2, first))

        a = conv(ua_ref, uah_ref, wa_ref)
        b = conv(ub_ref, ubh_ref, wb_ref)
        o_ref[...] = (_silu(a) * b).astype(BF16)

    main = pl.BlockSpec((tl, tc), lambda j, i: (i, j))
    halo = pl.BlockSpec((8, tc), lambda j, i: (jnp.maximum(i * hb - 1, 0), j))
    wa = pl.BlockSpec((3, tc), lambda j, i: (0, j))
    wb = pl.BlockSpec((3, tc), lambda j, i: (0, j + F // tc))
    return pl.pallas_call(
        body, name=name, grid=(F // tc, L // tl), in_specs=[main, halo, main, halo, wa, wb],
        out_specs=main, out_shape=jax.ShapeDtypeStruct((L, F), BF16),
        compiler_params=_cparams("parallel", "parallel"))(ua, ua, ub, ub, cw, cw)


def ffn_act_bwd(ua, ub, cw, dact, name, comm=None):
    L, F = ua.shape
    tl = _rtile(L, 256)
    tc = _tile(F, 1408)
    hb = tl // 8
    nrt = L // tl

    def body(ua_ref, uah_ref, ub_ref, ubh_ref, wa_ref, wb_ref, da_ref, dua_ref, dub_ref, sums_ref, ca_ref, cb_ref):
        i = pl.program_id(1)
        first = i == nrt - 1

        @pl.when(i == 0)
        def _():
            sums_ref[...] = jnp.zeros_like(sums_ref)
            ca_ref[...] = jnp.zeros_like(ca_ref)
            cb_ref[...] = jnp.zeros_like(cb_ref)

        def taps(x_ref, h_ref):
            x = x_ref[...]
            h = h_ref[...]
            return x, _shift_down(x, h, 1, first), _shift_down(x, h, 2, first)

        a0, a1, a2 = taps(ua_ref, uah_ref)
        b0, b1, b2 = taps(ub_ref, ubh_ref)
        a = wa_ref[2:3, :] * a0 + wa_ref[1:2, :] * a1 + wa_ref[0:1, :] * a2
        b = wb_ref[2:3, :] * b0 + wb_ref[1:2, :] * b1 + wb_ref[0:1, :] * b2
        dact_v = da_ref[...]
        dya = dact_v * b * _dsilu(a)
        dyb = dact_v * _silu(a)
        for (dy, w_ref, c_ref, d_ref, xs, base) in ((dya, wa_ref, ca_ref, dua_ref, (a2, a1, a0), 0),
                                                     (dyb, wb_ref, cb_ref, dub_ref, (b2, b1, b0), 24)):
            nxt = c_ref[...]
            d_ref[...] = (w_ref[2:3, :] * dy + w_ref[1:2, :] * _shift_up(dy, nxt, 1)
                          + w_ref[0:1, :] * _shift_up(dy, nxt, 2)).astype(BF16)
            c_ref[...] = dy[0:8, :]
            for j in range(3):
                sums_ref[base + 8 * j:base + 8 * j + 8, :] += _fold8(dy * xs[j])

    rev = lambda i: nrt - 1 - i
    main = pl.BlockSpec((tl, tc), lambda j, i: (rev(i), j))
    halo = pl.BlockSpec((8, tc), lambda j, i: (jnp.maximum(rev(i) * hb - 1, 0), j))
    wa = pl.BlockSpec((3, tc), lambda j, i: (0, j))
    wb = pl.BlockSpec((3, tc), lambda j, i: (0, j + F // tc))
    ob = jax.ShapeDtypeStruct((L, F), BF16)
    return _call(body, (ua, ua, ub, ub, cw, cw, dact), name=name, grid=(F // tc, nrt),
                 in_specs=[main, halo, main, halo, wa, wb, main],
                 out_specs=(main, main, pl.BlockSpec((48, tc), lambda j, i: (0, j))),
                 out_shape=(ob, ob, jax.ShapeDtypeStruct((48, F), F32)),
                 scratch_shapes=[pltpu.VMEM((8, tc), F32), pltpu.VMEM((8, tc), F32)],
                 sem=("parallel", "arbitrary"), comm=comm)


def attn_vectors(qna, kna, qnb, knb, sinks):
    ones = jnp.ones((128,), F32)
    wvec = jnp.concatenate([jnp.tile(qna, 8), jnp.tile(kna, 2), ones, jnp.tile(qnb, 8), jnp.tile(knb, 8),
                            jnp.tile(ones, 4)]).reshape(1, ATTN_IN)
    return wvec, jnp.repeat(sinks, HD).reshape(1, 512)


def _with_comm(result, comm):
    return result if comm is not None else (result, None)


def attention_block_fwd(h, w_in, wvec, sinkvec, w_out, tag, comms=None):
    L = h.shape[0]
    comms = comms or {}
    got = {}
    qkv = matmul([(h, w_in)], "nn", tag + "_qkv")
    X, X4, X16 = qknorm_fwd(qkv, wvec, tag + "_qknorm")
    (oa, la), got['swa'] = _with_comm(attn_fwd(X, 1, 0, 4, 5, True, 0, BLK - 1, tag + "_swa",
                                               comm=comms.get('swa')), comms.get('swa'))
    views = {1: (X, 6, 10, 14), 4: (X4, 0, 4, 8), 16: (X16, 0, 4, 8)}
    obs, lbs = [], []
    for window, d in B_BRANCHES:
        xd, qo, ko, vo = views[d]
        (o, l), got[d] = _with_comm(attn_fwd(xd, d, qo, ko, vo, False, 8, window // d,
                                             tag + f"_dil{d}", comm=comms.get(d)), comms.get(d))
        obs.append(o)
        lbs.append(l)
    m = attn_merge_fwd(oa, la, sinkvec, obs, lbs, tag + "_merge")
    y = matmul([(m, w_out)], "nn", tag + "_out")
    return y, (h, qkv, views, oa, la, obs, lbs, m), got


def attention_block_bwd(dy, res, w_in, wvec, sinkvec, w_out, tag, comms=None):
    h, qkv, views, oa, la, obs, lbs, m = res
    comms = comms or {}
    got = {}
    g_w_out = matmul([(m, dy)], "tn", tag + "_dwout", out_dtype=BF16)
    dm = matmul([(dy, w_out)], "nt", tag + "_dm")
    doa, dla, d1, d2, d3, g1, g2, g3, sinksums = attn_merge_bwd(dm, oa, la, sinkvec, obs, lbs, tag + "_dmerge")
    d_a, got['swa'] = _with_comm(attn_bwd(views[1][0], oa, la, doa, dla, 1, 0, 4, 5, True, 0, BLK - 1,
                                          tag + "_dswa", comm=comms.get('swa')), comms.get('swa'))
    d_b = []
    for (window, d), o, l, do, dl in zip(B_BRANCHES, obs, lbs, (d1, d2, d3), (g1, g2, g3)):
        xd, qo, ko, vo = views[d]
        dqkv_d, got[d] = _with_comm(attn_bwd(xd, o, l, do, dl, d, qo, ko, vo, False, 8, window // d,
                                             tag + f"_ddil{d}", comm=comms.get(d)), comms.get(d))
        d_b.append(dqkv_d)
    dqkv, wsums = qknorm_bwd(qkv, wvec, d_a, d_b, tag + "_dqknorm")
    g_w_in = matmul([(h, dqkv)], "tn", tag + "_dwin", out_dtype=BF16)
    dh = matmul([(dqkv, w_in)], "nt", tag + "_dh")
    ws = wsums.sum(axis=0)
    grads = dict(
        w_in=g_w_in, w_out=g_w_out,
        q_norm_a=ws[0:512].reshape(8, HD).sum(axis=0), k_norm_a=ws[512:640].reshape(2, HD).sum(axis=0),
        q_norm_b=ws[768:1280].reshape(8, HD).sum(axis=0), k_norm_b=ws[1280:1792].reshape(8, HD).sum(axis=0),
        sinks=sinksums.sum(axis=0).reshape(8, HD).sum(axis=1))
    return dh, grads, got


def ffn_block_fwd(h, w_up_a, w_up_b, cw, w_down, tag):
    ua = matmul([(h, w_up_a)], "nn", tag + "_upa")
    ub = matmul([(h, w_up_b)], "nn", tag + "_upb")
    act = ffn_act_fwd(ua, ub, cw, tag + "_act")
    f = matmul([(act, w_down)], "nn", tag + "_down")
    return f, (h, ua, ub, act)


def ffn_block_bwd(df, res, w_up_a, w_up_b, cw, w_down, tag, comm=None):
    h, ua, ub, act = res
    g_down = matmul([(act, df)], "tn", tag + "_dwdown", out_dtype=BF16)
    dact = matmul([(df, w_down)], "nt", tag + "_dact")
    (dua, dub, sums), got = _with_comm(ffn_act_bwd(ua, ub, cw, dact, tag + "_dactk", comm=comm), comm)
    g_up = jnp.concatenate([_cols_to_slabs(matmul([(h, dua)], "tn", tag + "_dwupa", out_dtype=BF16), N_DEV // 2),
                            _cols_to_slabs(matmul([(h, dub)], "tn", tag + "_dwupb", out_dtype=BF16), N_DEV // 2)],
                           axis=0)
    dh = matmul([(dua, w_up_a), (dub, w_up_b)], "nt", tag + "_dh")
    s = sums.reshape(2, 3, 8, D_FF).sum(axis=2)
    g_conv = jnp.concatenate([s[0], s[1]], axis=1)
    return dh, dict(w_up=g_up, conv=g_conv, w_down=g_down), got


def s5_params(lam_re, lam_im, log_dt, b_re, b_im, c_re, c_im):
    dt = jnp.exp(log_dt)[:, None]
    mag, ang = jnp.exp(lam_re * dt), lam_im * dt
    a_re, a_im = mag * jnp.cos(ang), mag * jnp.sin(ang)
    nr, ni = a_re - 1.0, a_im
    den = lam_re * lam_re + lam_im * lam_im
    f_re = (nr * lam_re + ni * lam_im) / den
    f_im = (ni * lam_re - nr * lam_im) / den
    eye = jnp.eye(16, dtype=F32)[:, None, :, None]
    bd = lambda b: (eye * jnp.transpose(b, (0, 2, 1))[:, :, None, :]).reshape(S5_W, S5_P)
    cd = lambda c: (eye * jnp.transpose(c, (0, 2, 1))[:, :, None, :]).reshape(S5_P, S5_W)
    flat = lambda t: t.reshape(1, S5_P)
    return flat(a_re), flat(a_im), flat(f_re), flat(f_im), bd(b_re), bd(b_im), cd(c_re), cd(c_im)


def _scan_tables(a_re, a_im, reverse):
    pows = [(a_re, a_im)]
    for _ in range(7):
        pr, pi = pows[-1]
        pows.append((pr * a_re - pi * a_im, pr * a_im + pi * a_re))
    order = list(range(7, -1, -1)) if reverse else list(range(8))
    z = jnp.zeros_like(a_re)
    rows = [pows[0][0], pows[0][1], pows[1][0], pows[1][1], pows[3][0], pows[3][1], z, z]
    rows += [pows[k][0] for k in order] + [pows[k][1] for k in order]
    return jnp.concatenate(rows, axis=0)


def _block_scan(er, ei, tab_ref, cr, ci, reverse):
    rows = lax.broadcasted_iota(jnp.int32, er.shape, 0)
    for idx, s in enumerate((1, 2, 4)):
        if reverse:
            sr, si, keep = pltpu.roll(er, 8 - s, axis=0), pltpu.roll(ei, 8 - s, axis=0), rows < 8 - s
        else:
            sr, si, keep = pltpu.roll(er, s, axis=0), pltpu.roll(ei, s, axis=0), rows >= s
        sr, si = jnp.where(keep, sr, 0.0), jnp.where(keep, si, 0.0)
        ar, ai = tab_ref[2 * idx:2 * idx + 1, :], tab_ref[2 * idx + 1:2 * idx + 2, :]
        er, ei = er + ar * sr - ai * si, ei + ar * si + ai * sr
    pr, pi_ = tab_ref[8:16, :], tab_ref[16:24, :]
    er, ei = er + pr * cr - pi_ * ci, ei + pr * ci + pi_ * cr
    return er, ei


def s5_scan_fwd(bu_re, bu_im, a_re, a_im, f_re, f_im, name):
    L, P = bu_re.shape
    tl = _rtile(L, 512)
    tab = _scan_tables(a_re, a_im, False)
    fvec = jnp.concatenate([f_re, f_im] + [jnp.zeros_like(f_re)] * 6, axis=0)

    def body(br_ref, bi_ref, tab_ref, f_ref, xr_ref, xi_ref, c_ref):
        @pl.when(pl.program_id(0) == 0)
        def _():
            c_ref[...] = jnp.zeros_like(c_ref)

        def blk(i, carry):
            cr, ci = carry
            rows = pl.ds(pl.multiple_of(i * 8, 8), 8)
            br, bi = br_ref[rows, :], bi_ref[rows, :]
            fr, fi = f_ref[0:1, :], f_ref[1:2, :]
            er, ei = _block_scan(fr * br - fi * bi, fr * bi + fi * br, tab_ref, cr, ci, False)
            xr_ref[rows, :] = er
            xi_ref[rows, :] = ei
            return er[7:8, :], ei[7:8, :]

        cr, ci = lax.fori_loop(0, tl // 8, blk, (c_ref[0:1, :], c_ref[1:2, :]))
        c_ref[0:1, :] = cr
        c_ref[1:2, :] = ci

    big = pl.BlockSpec((tl, P), lambda i: (i, 0))
    out = jax.ShapeDtypeStruct((L, P), F32)
    return pl.pallas_call(
        body, name=name, grid=(L // tl,),
        in_specs=[big, big, pl.BlockSpec((24, P), lambda i: (0, 0)), pl.BlockSpec((8, P), lambda i: (0, 0))],
        out_specs=(big, big), out_shape=(out, out), scratch_shapes=[pltpu.VMEM((8, P), F32)],
        compiler_params=_cparams("arbitrary"))(bu_re, bu_im, tab, fvec)


def s5_scan_bwd(dx_re, dx_im, x_re, x_im, bu_re, bu_im, a_re, a_im, f_re, f_im, name):
    L, P = dx_re.shape
    tl = _rtile(L, 256)
    nt = L // tl
    tab = _scan_tables(a_re, -a_im, True)
    fvec = jnp.concatenate([f_re, f_im] + [jnp.zeros_like(f_re)] * 6, axis=0)

    def body(gr_ref, gi_ref, xr_ref, xi_ref, br_ref, bi_ref, tab_ref, f_ref, dbr_ref, dbi_ref, s_ref, c_ref):
        @pl.when(pl.program_id(0) == 0)
        def _():
            c_ref[...] = jnp.zeros_like(c_ref)
            s_ref[...] = jnp.zeros_like(s_ref)

        def blk(k, carry):
            cr, ci = carry
            i = tl // 8 - 1 - k
            rows = pl.ds(pl.multiple_of(i * 8, 8), 8)
            er, ei = _block_scan(gr_ref[rows, :], gi_ref[rows, :], tab_ref, cr, ci, True)
            rid = lax.broadcasted_iota(jnp.int32, er.shape, 0)
            sr = jnp.where(rid == 7, cr, pltpu.roll(er, 7, axis=0))
            si = jnp.where(rid == 7, ci, pltpu.roll(ei, 7, axis=0))
            xr, xi = xr_ref[rows, :], xi_ref[rows, :]
            s_ref[0:8, :] += sr * xr + si * xi
            s_ref[8:16, :] += si * xr - sr * xi
            br, bi = br_ref[rows, :], bi_ref[rows, :]
            s_ref[16:24, :] += er * br + ei * bi
            s_ref[24:32, :] += ei * br - er * bi
            fr, fi = f_ref[0:1, :], f_ref[1:2, :]
            dbr_ref[rows, :] = fr * er + fi * ei
            dbi_ref[rows, :] = fr * ei - fi * er
            return er[0:1, :], ei[0:1, :]

        cr, ci = lax.fori_loop(0, tl // 8, blk, (c_ref[0:1, :], c_ref[1:2, :]))
        c_ref[0:1, :] = cr
        c_ref[1:2, :] = ci

    big = pl.BlockSpec((tl, P), lambda i: (nt - 1 - i, 0))
    out = jax.ShapeDtypeStruct((L, P), F32)
    return pl.pallas_call(
        body, name=name, grid=(nt,),
        in_specs=[big] * 6 + [pl.BlockSpec((24, P), lambda i: (0, 0)), pl.BlockSpec((8, P), lambda i: (0, 0))],
        out_specs=(big, big, pl.BlockSpec((32, P), lambda i: (0, 0))),
        out_shape=(out, out, jax.ShapeDtypeStruct((32, P), F32)), scratch_shapes=[pltpu.VMEM((8, P), F32)],
        compiler_params=_cparams("arbitrary"))(dx_re, dx_im, x_re, x_im, bu_re, bu_im, tab, fvec)


_GK, _GC = math.sqrt(2.0 / math.pi), 0.044715


def _gelu(y):
    return 0.5 * y * (1.0 + jnp.tanh(_GK * (y + _GC * y * y * y)))


def _dgelu(y):
    t = jnp.tanh(_GK * (y + _GC * y * y * y))
    return 0.5 * (1.0 + t) + 0.5 * y * (1.0 - t * t) * _GK * (1.0 + 3.0 * _GC * y * y)


def s5_out_fwd(x_re, x_im, u, cd_re, cd_im, dskip, glu_w, glu_b, name):
    L = u.shape[0]
    tl = _rtile(L, 512)

    def body(xr_ref, xi_ref, u_ref, cr_ref, ci_ref, d_ref, w_ref, b_ref, y_ref, o_ref):
        y = (jnp.dot(xr_ref[...].astype(BF16), cr_ref[...], preferred_element_type=F32)
             - jnp.dot(xi_ref[...].astype(BF16), ci_ref[...], preferred_element_type=F32)
             + d_ref[...] * u_ref[...])
        y_ref[...] = y
        g = _gelu(y)
        z = jnp.dot(g.astype(BF16), w_ref[...], preferred_element_type=F32) + b_ref[...]
        o_ref[...] = (g * _sigmoid(z)).astype(BF16)

    big = pl.BlockSpec((tl, S5_P), lambda i: (i, 0))
    sm = pl.BlockSpec((tl, S5_W), lambda i: (i, 0))
    full = lambda r, c: pl.BlockSpec((r, c), lambda i: (0, 0))
    return pl.pallas_call(
        body, name=name, grid=(L // tl,),
        in_specs=[big, big, sm, full(S5_P, S5_W), full(S5_P, S5_W), full(1, S5_W), full(S5_W, S5_W), full(1, S5_W)],
        out_specs=(sm, sm),
        out_shape=(jax.ShapeDtypeStruct((L, S5_W), F32), jax.ShapeDtypeStruct((L, S5_W), BF16)),
        compiler_params=_cparams("parallel"))(x_re, x_im, u, cd_re, cd_im, dskip, glu_w, glu_b)


def s5_out_bwd(dout, y, u, x_re, x_im, cd_re, cd_im, dskip, glu_w, glu_b, name, dout_col=0):
    L = u.shape[0]
    tl = _rtile(L, 256)
    nt_dims = (((1,), (1,)), ((), ()))
    tn_dims = (((0,), (0,)), ((), ()))

    def body(do_ref, y_ref, u_ref, xr_ref, xi_ref, cr_ref, ci_ref, d_ref, w_ref, b_ref,
             dxr_ref, dxi_ref, du_ref, dcr_ref, dci_ref, dw_ref, s_ref):
        @pl.when(pl.program_id(0) == 0)
        def _():
            dcr_ref[...] = jnp.zeros_like(dcr_ref)
            dci_ref[...] = jnp.zeros_like(dci_ref)
            dw_ref[...] = jnp.zeros_like(dw_ref)
            s_ref[...] = jnp.zeros_like(s_ref)

        yv, dov = y_ref[...], do_ref[...]
        g = _gelu(yv)
        gb = g.astype(BF16)
        sg = _sigmoid(jnp.dot(gb, w_ref[...], preferred_element_type=F32) + b_ref[...])
        dz = dov * g * sg * (1.0 - sg)
        dzb = dz.astype(BF16)
        dg = dov * sg + lax.dot_general(dzb, w_ref[...], nt_dims, preferred_element_type=F32)
        dw_ref[...] += lax.dot_general(gb, dzb, tn_dims, preferred_element_type=F32)
        dy = dg * _dgelu(yv)
        dyb = dy.astype(BF16)
        s_ref[0:8, :] += _fold8(dy * u_ref[...])
        s_ref[8:16, :] += _fold8(dz)
        du_ref[...] = dy * d_ref[...]
        dxr_ref[...] = lax.dot_general(dyb, cr_ref[...], nt_dims, preferred_element_type=F32)
        dxi_ref[...] = -lax.dot_general(dyb, ci_ref[...], nt_dims, preferred_element_type=F32)
        dcr_ref[...] += lax.dot_general(xr_ref[...].astype(BF16), dyb, tn_dims, preferred_element_type=F32)
        dci_ref[...] -= lax.dot_general(xi_ref[...].astype(BF16), dyb, tn_dims, preferred_element_type=F32)

    big = pl.BlockSpec((tl, S5_P), lambda i: (i, 0))
    sm = pl.BlockSpec((tl, S5_W), lambda i: (i, 0))
    full = lambda r, c: pl.BlockSpec((r, c), lambda i: (0, 0))
    sd = jax.ShapeDtypeStruct
    return pl.pallas_call(
        body, name=name, grid=(L // tl,),
        in_specs=[pl.BlockSpec((tl, S5_W), lambda i: (i, dout_col)), sm, sm, big, big, full(S5_P, S5_W),
                  full(S5_P, S5_W), full(1, S5_W), full(S5_W, S5_W), full(1, S5_W)],
        out_specs=(big, big, sm, full(S5_P, S5_W), full(S5_P, S5_W), full(S5_W, S5_W), full(16, S5_W)),
        out_shape=(sd((L, S5_P), F32), sd((L, S5_P), F32), sd((L, S5_W), F32), sd((S5_P, S5_W), F32),
                   sd((S5_P, S5_W), F32), sd((S5_W, S5_W), F32), sd((16, S5_W), F32)),
        compiler_params=_cparams("arbitrary"))(dout, y, u, x_re, x_im, cd_re, cd_im, dskip, glu_w, glu_b)


def s5_block_fwd(u, params, dskip, glu_w, glu_b, tag):
    a_re, a_im, f_re, f_im, bd_re, bd_im, cd_re, cd_im = params
    bu_re = matmul([(u, bd_re.astype(BF16))], "nn", tag + "_bure")
    bu_im = matmul([(u, bd_im.astype(BF16))], "nn", tag + "_buim")
    x_re, x_im = s5_scan_fwd(bu_re, bu_im, a_re, a_im, f_re, f_im, tag + "_scan")
    y, out = s5_out_fwd(x_re, x_im, u, cd_re.astype(BF16), cd_im.astype(BF16), dskip, glu_w, glu_b, tag + "_out")
    return out, (u, bu_re, bu_im, x_re, x_im, y)


def s5_block_bwd(dout, res, params, dskip, glu_w, glu_b, tag, dout_col=0):
    u, bu_re, bu_im, x_re, x_im, y = res
    a_re, a_im, f_re, f_im, bd_re, bd_im, cd_re, cd_im = params
    dxr, dxi, du, dcr, dci, dglu_w, sums = s5_out_bwd(dout, y, u, x_re, x_im, cd_re.astype(BF16), cd_im.astype(BF16),
                                                      dskip, glu_w, glu_b, tag + "_dout", dout_col=dout_col)
    dbr, dbi, acc = s5_scan_bwd(dxr, dxi, x_re, x_im, bu_re, bu_im, a_re, a_im, f_re, f_im, tag + "_dscan")
    du = du + matmul([(dbr, bd_re.astype(BF16)), (dbi, bd_im.astype(BF16))], "nt", tag + "_du")
    dbd_re = matmul([(u, dbr)], "tn", tag + "_dbdre")
    dbd_im = matmul([(u, dbi)], "tn", tag + "_dbdim")
    acc = acc.reshape(4, 8, S5_P).sum(axis=1)
    s = sums.reshape(2, 8, S5_W).sum(axis=1)
    cot = (acc[0:1], acc[1:2], acc[2:3], acc[3:4], dbd_re, dbd_im, dcr, dci)
    return du, cot, dict(dskip=s[0], glu_w=dglu_w, glu_b=s[1])


DN_Z0, DN_NT = 18, 18
REC_U0, REC_A0 = 3072, 3328


def rec_cols_permute(w):
    return jnp.concatenate([w[..., S5_W:REC_A0], w[..., :S5_W], w[..., REC_A0:]], axis=-1)


def rec_cols_restore(w):
    return jnp.concatenate([w[..., REC_U0:REC_A0], w[..., :REC_U0], w[..., REC_A0:]], axis=-1)


DN_W = DN_H * DN_DK


def _dn_conv4(taps, w_ref):
    xc = w_ref[3:4, :] * taps[0]
    for k in range(1, 4):
        xc = xc + w_ref[3 - k:4 - k, :] * taps[k]
    return xc


def dn_prep_fwd(rin, cw, name):
    L = rin.shape[0]
    tl = _rtile(L, 256)
    hb = tl // 8

    def body(x_ref, h_ref, w_ref, o_ref):
        j = pl.program_id(0)
        first = pl.program_id(1) == 0
        x, h = x_ref[...], h_ref[...]
        s = _silu(_dn_conv4([x] + [_shift_down(x, h, k, first) for k in range(1, 4)], w_ref))
        scale = jnp.where(j == 0, DN_DK ** -0.5, 1.0)
        for hd in _HEADS:
            cs = slice(hd * 128, (hd + 1) * 128)
            sh = s[:, cs]
            r = lax.rsqrt(jnp.sum(sh * sh, axis=-1, keepdims=True) + EPS)
            o_ref[:, cs] = jnp.where(j < 2, sh * r * scale, sh)

    main = pl.BlockSpec((tl, DN_W), lambda j, i: (i, j))
    halo = pl.BlockSpec((8, DN_W), lambda j, i: (jnp.maximum(i * hb - 1, 0), j))
    return pl.pallas_call(
        body, name=name, grid=(3, L // tl),
        in_specs=[main, halo, pl.BlockSpec((4, DN_W), lambda j, i: (0, j))],
        out_specs=main, out_shape=jax.ShapeDtypeStruct((L, 3 * DN_W), F32),
        compiler_params=_cparams("parallel", "parallel"))(rin, rin, cw)


def dn_prep_bwd(rin, cw, dout, name):
    L = rin.shape[0]
    tl = _rtile(L, 256)
    hb = tl // 8
    nrt = L // tl

    def body(x_ref, h_ref, w_ref, d_ref, dx_ref, s_ref, c_ref):
        j = pl.program_id(0)
        i = pl.program_id(1)
        first = i == nrt - 1

        @pl.when(i == 0)
        def _():
            s_ref[...] = jnp.zeros_like(s_ref)
            c_ref[...] = jnp.zeros_like(c_ref)

        x, h = x_ref[...], h_ref[...]
        taps = [x] + [_shift_down(x, h, k, first) for k in range(1, 4)]
        xc = _dn_conv4(taps, w_ref)
        s = _silu(xc)
        scale = jnp.where(j == 0, DN_DK ** -0.5, 1.0)
        pieces = []
        for hd in _HEADS:
            cs = slice(hd * 128, (hd + 1) * 128)
            sh, d = s[:, cs], d_ref[:, cs]
            r = lax.rsqrt(jnp.sum(sh * sh, axis=-1, keepdims=True) + EPS)
            n = sh * r
            dn = d * scale
            pieces.append(jnp.where(j < 2, r * (dn - n * jnp.sum(dn * n, axis=-1, keepdims=True)), d))
        dxc = jnp.concatenate(pieces, axis=1) * _dsilu(xc)
        nxt = c_ref[...]
        dx_ref[...] = _dn_conv4([dxc] + [_shift_up(dxc, nxt, k) for k in range(1, 4)], w_ref).astype(BF16)
        c_ref[...] = dxc[0:8, :]
        for k in range(4):
            s_ref[8 * (3 - k):8 * (3 - k) + 8, :] += _fold8(dxc * taps[k])

    rev = lambda i: nrt - 1 - i
    main = pl.BlockSpec((tl, DN_W), lambda j, i: (rev(i), j))
    halo = pl.BlockSpec((8, DN_W), lambda j, i: (jnp.maximum(rev(i) * hb - 1, 0), j))
    return pl.pallas_call(
        body, name=name, grid=(3, nrt),
        in_specs=[main, halo, pl.BlockSpec((4, DN_W), lambda j, i: (0, j)), main],
        out_specs=(main, pl.BlockSpec((32, DN_W), lambda j, i: (0, j))),
        out_shape=(jax.ShapeDtypeStruct((L, 3 * DN_W), BF16), jax.ShapeDtypeStruct((32, 3 * DN_W), F32)),
        scratch_shapes=[pltpu.VMEM((8, DN_W), F32)],
        compiler_params=_cparams("parallel", "arbitrary"))(rin, rin, cw, dout)


_HI = lax.Precision.HIGH
_NT = (((1,), (1,)), ((), ()))
_TN = (((0,), (0,)), ((), ()))
_HEADS = tuple(range(DN_H))


def _mm(a, b, dims=(((1,), (0,)), ((), ())), hi=False):
    if hi:
        return lax.dot_general(a, b, dims, precision=_HI, preferred_element_type=F32)
    return lax.dot_general(a.astype(BF16), b.astype(BF16), dims, preferred_element_type=F32)


def _dn_masks():
    ri = lax.broadcasted_iota(jnp.int32, (DN_C, DN_C), 0)
    ci = lax.broadcasted_iota(jnp.int32, (DN_C, DN_C), 1)
    return ri >= ci, ri > ci, (ri == ci).astype(F32)


def _dn_decay(gc, gr, causal):
    gam = [jnp.where(causal, jnp.exp(jnp.where(causal, gc[h] - gr[h], 0.0)), 0.0) for h in _HEADS]
    eg = [jnp.exp(gc[h]) for h in _HEADS]
    el = [jnp.exp(gc[h][DN_C - 1:DN_C, :] - gc[h]) for h in _HEADS]
    gl = [jnp.exp(gc[h][DN_C - 1:DN_C, :]) for h in _HEADS]
    return gam, eg, el, gl


def _dn_solve(k, v, beta, gam, eg, kk, strict, eye):
    nmat = [jnp.where(strict, beta[h] * kk[h] * gam[h], 0.0) for h in _HEADS]
    t = [eye - nmat[h] for h in _HEADS]
    m = [_mm(nmat[h], nmat[h], hi=True) for h in _HEADS]
    for step in range(5):
        t = [t[h] + _mm(t[h], m[h], hi=True) for h in _HEADS]
        if step < 4:
            m = [_mm(m[h], m[h], hi=True) for h in _HEADS]
    rhs = [jnp.concatenate([v[h] * beta[h], k[h] * (beta[h] * eg[h])], axis=1) for h in _HEADS]
    sol = [_mm(t[h], rhs[h], hi=True) for h in _HEADS]
    return t, sol


def dn_chunk_fwd(qkv, gcol, grow, bcol, name, comm=None):
    L = qkv.shape[0]
    C, W = DN_C, DN_H * DN_DK
    ncb = 8
    tl = ncb * C
    nchunks = L // C

    def body(q_ref, k_ref, v_ref, gc_ref, gr_ref, b_ref, o_ref, sh_ref, t_ref, sol_ref, s_ref):
        @pl.when(pl.program_id(0) == 0)
        def _():
            s_ref[...] = jnp.zeros_like(s_ref)

        causal, strict, eye = _dn_masks()

        def chunk(c, _):
            rows = pl.ds(pl.multiple_of(c * C, C), C)
            grow_c = gr_ref[c]
            hs = lambda h: slice(h * 128, (h + 1) * 128)
            q = [q_ref[rows, hs(h)] for h in _HEADS]
            k = [k_ref[rows, hs(h)] for h in _HEADS]
            v = [v_ref[rows, hs(h)] for h in _HEADS]
            gc = [gc_ref[rows, h:h + 1] for h in _HEADS]
            gr = [grow_c[h:h + 1, :] for h in _HEADS]
            beta = [b_ref[rows, h:h + 1] for h in _HEADS]
            gam, eg, el, gl = _dn_decay(gc, gr, causal)
            kk = [_mm(k[h], k[h], _NT) for h in _HEADS]
            t, sol = _dn_solve(k, v, beta, gam, eg, kk, strict, eye)
            qk = [_mm(q[h], k[h], _NT) * gam[h] for h in _HEADS]
            S = [s_ref[hs(h), :] for h in _HEADS]
            vn = [sol[h][:, :128] - _mm(sol[h][:, 128:], S[h]) for h in _HEADS]
            o = [_mm(q[h] * eg[h], S[h]) + _mm(qk[h], vn[h]) for h in _HEADS]
            Sn = [S[h] * gl[h] + _mm(k[h] * el[h], vn[h], _TN) for h in _HEADS]
            for h in _HEADS:
                sh_ref[c, hs(h), :] = S[h]
                s_ref[hs(h), :] = Sn[h]
                o_ref[rows, hs(h)] = o[h]
                t_ref[rows, h * C:(h + 1) * C] = t[h]
                sol_ref[rows, h * 256:(h + 1) * 256] = sol[h]
            return 0

        lax.fori_loop(0, ncb, chunk, 0)

    col = lambda b: pl.BlockSpec((tl, W), lambda i: (i, b))
    small = pl.BlockSpec((tl, 8), lambda i: (i, 0))
    rowblk = lambda w: pl.BlockSpec((tl, w), lambda i: (i, 0))
    sd = jax.ShapeDtypeStruct
    return _call(body, (qkv, qkv, qkv, gcol, grow, bcol), name=name, grid=(L // tl,),
                 in_specs=[col(0), col(1), col(2), small, pl.BlockSpec((ncb, 8, C), lambda i: (i, 0, 0)), small],
                 out_specs=(rowblk(W), pl.BlockSpec((ncb, W, 128), lambda i: (i, 0, 0)), rowblk(DN_H * C),
                            rowblk(DN_H * 256)),
                 out_shape=(sd((L, W), F32), sd((nchunks, W, 128), F32), sd((L, DN_H * C), F32),
                            sd((L, DN_H * 256), F32)),
                 scratch_shapes=[pltpu.VMEM((W, 128), F32)], sem=("arbitrary",), comm=comm)


def dn_chunk_bwd(qkv, gcol, grow, bcol, shist, thist, solhist, do, name, comm=None):
    L = qkv.shape[0]
    C, W = DN_C, DN_H * DN_DK
    ncb = 8
    tl = ncb * C
    nchunks = L // C
    nt = L // tl

    def body(q_ref, k_ref, v_ref, gc_ref, gr_ref, b_ref, sh_ref, t_ref, sol_ref, do_ref,
             dqkv_ref, dgc_ref, dgr_ref, db_ref, ds_ref):
        @pl.when(pl.program_id(0) == 0)
        def _():
            ds_ref[...] = jnp.zeros_like(ds_ref)

        lane8 = lax.broadcasted_iota(jnp.int32, (C, 8), 1)
        sub8 = lax.broadcasted_iota(jnp.int32, (8, C), 0)
        rowid = lax.broadcasted_iota(jnp.int32, (C, 1), 0)
        causal, strict, _ = _dn_masks()
        rsum = lambda a: jnp.sum(a, axis=1, keepdims=True)

        def chunk(cc, _):
            c = ncb - 1 - cc
            rows = pl.ds(pl.multiple_of(c * C, C), C)
            grow_c = gr_ref[c]
            hs = lambda h: slice(h * 128, (h + 1) * 128)
            q = [q_ref[rows, hs(h)] for h in _HEADS]
            k = [k_ref[rows, hs(h)] for h in _HEADS]
            v = [v_ref[rows, hs(h)] for h in _HEADS]
            gc = [gc_ref[rows, h:h + 1] for h in _HEADS]
            gr = [grow_c[h:h + 1, :] for h in _HEADS]
            beta = [b_ref[rows, h:h + 1] for h in _HEADS]
            t = [t_ref[rows, h * C:(h + 1) * C] for h in _HEADS]
            sol = [sol_ref[rows, h * 256:(h + 1) * 256] for h in _HEADS]
            S = [sh_ref[c, hs(h), :] for h in _HEADS]
            dS = [ds_ref[hs(h), :] for h in _HEADS]
            dov = [do_ref[rows, hs(h)] for h in _HEADS]
            gam, eg, el, gl = _dn_decay(gc, gr, causal)
            kk = [_mm(k[h], k[h], _NT) for h in _HEADS]
            qk_raw = [_mm(q[h], k[h], _NT) for h in _HEADS]
            w = [sol[h][:, 128:] for h in _HEADS]
            kd = [k[h] * el[h] for h in _HEADS]
            vn = [sol[h][:, :128] - _mm(w[h], S[h]) for h in _HEADS]
            dvn = [_mm(qk_raw[h] * gam[h], dov[h], _TN) + _mm(kd[h], dS[h]) for h in _HEADS]
            dqd = [_mm(dov[h], S[h], _NT) for h in _HEADS]
            dqk = [jnp.where(causal, _mm(dov[h], vn[h], _NT), 0.0) for h in _HEADS]
            dkd = [_mm(vn[h], dS[h], _NT) for h in _HEADS]
            dgl = [jnp.sum(rsum(dS[h] * S[h]), axis=0, keepdims=True) for h in _HEADS]
            dw = [-_mm(dvn[h], S[h], _NT) for h in _HEADS]
            dSn = [dS[h] * gl[h] + _mm(q[h] * eg[h], dov[h], _TN) - _mm(w[h], dvn[h], _TN) for h in _HEADS]
            drhs = [_mm(t[h], jnp.concatenate([dvn[h], dw[h]], axis=1), _TN, hi=True) for h in _HEADS]
            dn = [jnp.where(strict, -_mm(drhs[h], sol[h], _NT, hi=True), 0.0) for h in _HEADS]
            dgc_all = jnp.zeros((C, 8), F32)
            db_all = jnp.zeros((C, 8), F32)
            dgr_all = jnp.zeros((8, C), F32)
            for h in _HEADS:
                drv, drk = drhs[h][:, :128], drhs[h][:, 128:]
                t2 = rsum(drk * k[h])
                x = dn[h] * gam[h]
                dbeta = rsum(drv * v[h]) + t2 * eg[h] + rsum(x * kk[h])
                dkk = x * beta[h]
                draw = dqk[h] * gam[h]
                mm_ = (dn[h] * beta[h] * kk[h] + dqk[h] * qk_raw[h]) * gam[h]
                deg = t2 * beta[h] + rsum(dqd[h] * q[h])
                r_ = rsum(dkd[h] * k[h]) * el[h]
                dglast = jnp.sum(r_, axis=0, keepdims=True) + dgl[h] * gl[h]
                dgc = rsum(mm_) + deg * eg[h] - r_ + jnp.where(rowid == C - 1, dglast, 0.0)
                dgr = -jnp.sum(mm_, axis=0, keepdims=True)
                dqkv_ref[rows, hs(h)] = _mm(draw, k[h]) + dqd[h] * eg[h]
                dqkv_ref[rows, hs(DN_H + h)] = (drk * (beta[h] * eg[h]) + _mm(dkk, k[h]) + _mm(dkk, k[h], _TN)
                                                + _mm(draw, q[h], _TN) + dkd[h] * el[h])
                dqkv_ref[rows, hs(2 * DN_H + h)] = drv * beta[h]
                ds_ref[hs(h), :] = dSn[h]
                dgc_all = dgc_all + jnp.where(lane8 == h, dgc, 0.0)
                db_all = db_all + jnp.where(lane8 == h, dbeta, 0.0)
                dgr_all = dgr_all + jnp.where(sub8 == h, dgr, 0.0)
            dgc_ref[rows, :] = dgc_all
            db_ref[rows, :] = db_all
            dgr_ref[c] = dgr_all
            return 0

        lax.fori_loop(0, ncb, chunk, 0)

    rev = lambda i: nt - 1 - i
    col = lambda b: pl.BlockSpec((tl, W), lambda i: (rev(i), b))
    rowblk = lambda w: pl.BlockSpec((tl, w), lambda i: (rev(i), 0))
    small = pl.BlockSpec((tl, 8), lambda i: (rev(i), 0))
    g3 = pl.BlockSpec((ncb, 8, C), lambda i: (rev(i), 0, 0))
    sd = jax.ShapeDtypeStruct
    return _call(body, (qkv, qkv, qkv, gcol, grow, bcol, shist, thist, solhist, do), name=name, grid=(nt,),
                 in_specs=[col(0), col(1), col(2), small, g3, small,
                           pl.BlockSpec((ncb, W, 128), lambda i: (rev(i), 0, 0)), rowblk(DN_H * C),
                           rowblk(DN_H * 256), col(0)],
                 out_specs=(rowblk(3 * W), small, g3, small),
                 out_shape=(sd((L, 3 * W), F32), sd((L, 8), F32), sd((nchunks, 8, C), F32), sd((L, 8), F32)),
                 scratch_shapes=[pltpu.VMEM((W, 128), F32)], sem=("arbitrary",), comm=comm)


def dn_out_fwd(o, rin, nw, name):
    L = o.shape[0]
    tl = _rtile(L, 256)

    def body(o_ref, z_ref, w_ref, y_ref):
        for hd in _HEADS:
            cs = slice(hd * 128, (hd + 1) * 128)
            ov = o_ref[:, cs]
            r = lax.rsqrt(jnp.mean(ov * ov, axis=-1, keepdims=True) + EPS)
            y_ref[:, cs] = (ov * r * w_ref[...] * _silu(z_ref[:, cs])).astype(BF16)

    return pl.pallas_call(
        body, name=name, grid=(L // tl,),
        in_specs=[pl.BlockSpec((tl, DN_W), lambda i: (i, 0)), pl.BlockSpec((tl, DN_W), lambda i: (i, 3)),
                  pl.BlockSpec((1, 128), lambda i: (0, 0))],
        out_specs=pl.BlockSpec((tl, DN_W), lambda i: (i, 0)), out_shape=jax.ShapeDtypeStruct((L, DN_W), BF16),
        compiler_params=_cparams("parallel"))(o, rin, nw)


def dn_out_bwd(dycat, o, rin, nw, name):
    L = o.shape[0]
    tl = _rtile(L, 256)

    def body(dy_ref, o_ref, z_ref, w_ref, do_ref, dz_ref, s_ref):
        @pl.when(pl.program_id(0) == 0)
        def _():
            s_ref[...] = jnp.zeros_like(s_ref)

        for hd in _HEADS:
            cs = slice(hd * 128, (hd + 1) * 128)
            ov, zv, d = o_ref[:, cs], z_ref[:, cs], dy_ref[:, cs]
            r = lax.rsqrt(jnp.mean(ov * ov, axis=-1, keepdims=True) + EPS)
            n = ov * r
            dnw = d * _silu(zv)
            dz_ref[:, cs] = (d * n * w_ref[...] * _dsilu(zv)).astype(BF16)
            dn = dnw * w_ref[...]
            do_ref[:, cs] = r * (dn - n * jnp.mean(dn * n, axis=-1, keepdims=True))
            s_ref[:, cs] += _fold8(dnw * n)

    own = pl.BlockSpec((tl, DN_W), lambda i: (i, 0))
    sd = jax.ShapeDtypeStruct
    return pl.pallas_call(
        body, name=name, grid=(L // tl,),
        in_specs=[own, own, pl.BlockSpec((tl, DN_W), lambda i: (i, 3)), pl.BlockSpec((1, 128), lambda i: (0, 0))],
        out_specs=(own, own, pl.BlockSpec((8, DN_W), lambda i: (0, 0))),
        out_shape=(sd((L, DN_W), F32), sd((L, DN_W), BF16), sd((8, DN_W), F32)),
        compiler_params=_cparams("arbitrary"))(dycat, o, rin, nw)


def dn_gates(a, beta_raw, a_log, dt_bias):
    L = a.shape[0]
    beta = jax.nn.sigmoid(beta_raw)
    g = -jnp.exp(a_log) * jax.nn.softplus(a + dt_bias)
    G = jnp.cumsum(g.reshape(L // DN_C, DN_C, DN_H), axis=1)
    pad = lambda t: jnp.pad(t, ((0, 0), (0, 8 - DN_H)))
    gcol = pad(G.reshape(L, DN_H))
    grow = jnp.pad(jnp.transpose(G, (0, 2, 1)), ((0, 0), (0, 8 - DN_H), (0, 0)))
    return gcol, grow, pad(beta)


def dn_block_fwd(rin, cw, a_log, dt_bias, out_norm, tag, comm=None):
    gates, gates_vjp = jax.vjp(dn_gates, rin[:, REC_A0:REC_A0 + DN_H], rin[:, REC_A0 + DN_H:REC_IN], a_log, dt_bias)
    qkv = dn_prep_fwd(rin, cw, tag + "_prep")
    (o, shist, thist, solhist), got = _with_comm(dn_chunk_fwd(qkv, *gates, tag + "_chunk", comm=comm), comm)
    yd = dn_out_fwd(o, rin, out_norm.reshape(1, 128), tag + "_onorm")
    return yd, (qkv, gates, gates_vjp, o, shist, thist, solhist), got


def dn_block_bwd(dyd, res, rin, cw, out_norm, tag, comm=None):
    qkv, gates, gates_vjp, o, shist, thist, solhist = res
    do, dz, nsum = dn_out_bwd(dyd, o, rin, out_norm.reshape(1, 128), tag + "_donorm")
    (dqkv, dgc, dgr, db), got = _with_comm(dn_chunk_bwd(qkv, *gates, shist, thist, solhist, do, tag + "_dchunk",
                                                        comm=comm), comm)
    da, dbraw, g_alog, g_dtb = gates_vjp((dgc, dgr, db))
    dx, csum = dn_prep_bwd(rin, cw, dqkv, tag + "_dprep")
    grads = dict(conv=csum.reshape(4, 8, DN_NT * 128).sum(axis=1), a_log=g_alog, dt_bias=g_dtb,
                 out_norm=nsum.sum(axis=0).reshape(DN_H, 128).sum(axis=0))
    return dx, dz, da, dbraw, grads, got


_HBM = pl.BlockSpec(memory_space=pltpu.HBM)


def _mesh_pos():
    xi, yi, ci = lax.axis_index("x"), lax.axis_index("y"), lax.axis_index("c")
    return xi, yi, ci, 4 * xi + 2 * yi + ci


def _peer(xi, yi, ci, k):
    px = 1 - xi if (k >> 2) & 1 else xi
    py = 1 - yi if (k >> 1) & 1 else yi
    pc = 1 - ci if k & 1 else ci
    return (px, py, pc), 4 * px + 2 * py + pc


def _exchange(xs, gather, name):
    n = len(xs)

    def body(*refs):
        copies = _comm_copies(refs[:n], refs[n:2 * n], *refs[2 * n:], gather)
        for cp in copies:
            cp.start()
        for cp in copies:
            cp.wait()

    return pl.pallas_call(
        body, name=name, in_specs=[_HBM] * n, out_specs=tuple([_HBM] * n),
        out_shape=_comm_out_shapes(xs), scratch_shapes=_comm_sems(n))(*xs)


def _comm_out_shapes(xs):
    return tuple(jax.ShapeDtypeStruct((N_DEV,) + x.shape[-2:], x.dtype) for x in xs)


def _comm_sems(n):
    return [pltpu.SemaphoreType.DMA((n * (N_DEV - 1),)), pltpu.SemaphoreType.DMA((n * (N_DEV - 1),)),
            pltpu.SemaphoreType.DMA((n,))]


def _comm_copies(x_refs, o_refs, send_sems, recv_sems, lsems, gather):
    xi, yi, ci, me = _mesh_pos()
    copies = []
    for t in range(len(x_refs)):
        src_of = (lambda lin, t=t: x_refs[t]) if gather else (lambda lin, t=t: x_refs[t].at[lin])
        copies.append(pltpu.make_async_copy(src_of(me), o_refs[t].at[me], lsems.at[t]))
        for k in range(1, N_DEV):
            peer, lin = _peer(xi, yi, ci, k)
            s = t * (N_DEV - 1) + k - 1
            copies.append(pltpu.make_async_remote_copy(
                src_ref=src_of(lin), dst_ref=o_refs[t].at[me], send_sem=send_sems.at[s],
                recv_sem=recv_sems.at[s], device_id=peer, device_id_type=pl.DeviceIdType.MESH))
    return copies


def _call(body, args, *, name, grid, in_specs, out_specs, out_shape, scratch_shapes=(), sem, comm=None):
    if comm is None:
        return pl.pallas_call(body, name=name, grid=grid, in_specs=in_specs, out_specs=out_specs,
                              out_shape=out_shape, scratch_shapes=list(scratch_shapes),
                              compiler_params=_cparams(*sem))(*args)
    xs, gather = comm
    n = len(xs)
    single = not isinstance(out_shape, (tuple, list))
    outs_shape = (out_shape,) if single else tuple(out_shape)
    outs_specs = (out_specs,) if single else tuple(out_specs)
    n_in, n_out, n_scr = len(in_specs), len(outs_shape), len(scratch_shapes)

    def body2(*refs):
        ins, cx = refs[:n_in], refs[n_in:n_in + n]
        outs = refs[n_in + n:n_in + n + n_out]
        co = refs[n_in + n + n_out:n_in + 2 * n + n_out]
        scr = refs[n_in + 2 * n + n_out:n_in + 2 * n + n_out + n_scr]
        sems = refs[n_in + 2 * n + n_out + n_scr:]
        first = functools.reduce(jnp.logical_and, [pl.program_id(a) == 0 for a in range(len(grid))])
        last = functools.reduce(jnp.logical_and, [pl.program_id(a) == grid[a] - 1 for a in range(len(grid))])

        @pl.when(first)
        def _():
            for cp in _comm_copies(cx, co, *sems, gather):
                cp.start()

        body(*ins, *outs, *scr)

        @pl.when(last)
        def _():
            for cp in _comm_copies(cx, co, *sems, gather):
                cp.wait()

    res = pl.pallas_call(
        body2, name=name, grid=grid, in_specs=list(in_specs) + [_HBM] * n,
        out_specs=outs_specs + tuple([_HBM] * n), out_shape=outs_shape + _comm_out_shapes(xs),
        scratch_shapes=list(scratch_shapes) + _comm_sems(n),
        compiler_params=_cparams(*(["arbitrary"] * len(grid))))(*args, *xs)
    main = res[0] if single else tuple(res[:n_out])
    return main, list(res[n_out:])


def all_gather(x, name):
    return _exchange([x], True, name)[0]


def all_gather_many(xs, name):
    return _exchange(xs, True, name)


def all_to_all_many(xs, name):
    return _exchange(xs, False, name)


def reduce_adamw(gsrc, w, m, v, name):
    S, R, C = gsrc.shape
    tr = _rtile(R, max(16, min(256, (4 << 20) // (S * C * 4) // 16 * 16)), 16 if R % 16 == 0 else 8)
    c1 = 1.0 - ADAM_B1 ** ADAM_STEP
    c2 = 1.0 - ADAM_B2 ** ADAM_STEP

    def body(g_ref, w_ref, m_ref, v_ref, go_ref, d_ref, mo_ref, vo_ref):
        g = g_ref[0].astype(F32)
        for s in range(1, S):
            g = g + g_ref[s].astype(F32)
        go_ref[...] = g
        mn = ADAM_B1 * m_ref[...] + (1.0 - ADAM_B1) * g
        vn = ADAM_B2 * v_ref[...] + (1.0 - ADAM_B2) * (g * g)
        mo_ref[...] = mn
        vo_ref[...] = vn
        d_ref[...] = -ADAM_LR * ((mn / c1) / (jnp.sqrt(vn / c2) + ADAM_EPS) + ADAM_WD * w_ref[...])

    big = pl.BlockSpec((tr, C), lambda i: (i, 0))
    o = jax.ShapeDtypeStruct((R, C), F32)
    return pl.pallas_call(
        body, name=name, grid=(R // tr,),
        in_specs=[pl.BlockSpec((S, tr, C), lambda i: (0, i, 0)), big, big, big],
        out_specs=(big, big, big, big), out_shape=(o, o, o, o),
        compiler_params=_cparams("parallel"))(gsrc, w, m, v)


def _to_slabs(g, ax):
    shp = g.shape
    g = g.reshape(shp[:ax] + (N_DEV, shp[ax] // N_DEV) + shp[ax + 1:])
    return jnp.moveaxis(g, ax, 0).reshape(N_DEV, -1)


def _from_slabs(s, ax, shp):
    s = s.reshape((N_DEV,) + shp[:ax] + (shp[ax] // N_DEV,) + shp[ax + 1:])
    return jnp.moveaxis(s, 0, ax).reshape(shp)


def _pack_rows(flat, width, row_mult):
    n = flat.shape[-1]
    per = width * row_mult
    tot = -(-n // per) * per
    flat = jnp.pad(flat, [(0, 0)] * (flat.ndim - 1) + [(0, tot - n)])
    return flat.reshape(flat.shape[:-1] + (tot // width, width))


def _offsets(sizes):
    offs, o = [], 0
    for s in sizes:
        offs.append(o)
        o += s
    return offs


WEIGHTS = ['ada_w', 'ada_b', 'norm_mix', 'norm_ffn', 'attn_w_in', 'attn_q_norm_a', 'attn_k_norm_a', 'attn_q_norm_b',
           'attn_k_norm_b', 'attn_sinks', 'attn_w_out', 'rec_w_in', 's5_lambda_re', 's5_lambda_im', 's5_log_dt',
           's5_b_re', 's5_b_im', 's5_c_re', 's5_c_im', 's5_d', 's5_glu_w', 's5_glu_b', 'dn_conv', 'dn_a_log',
           'dn_dt_bias', 'dn_out_norm', 'rec_w_out', 'ffn_w_up', 'ffn_conv', 'ffn_w_down']
BIG = [('attn_w_in', (D, ATTN_IN // N_DEV)), ('attn_w_out', (D // N_DEV, D)), ('rec_w_in', (D // N_DEV, REC_PAD)),
       ('s5_glu_w', (S5_W // N_DEV, S5_W)), ('rec_w_out', (D // N_DEV, D)), ('ffn_w_up', (2 * D, 2 * D_FF // N_DEV)),
       ('ffn_w_down', (2 * D_FF // N_DEV, D))]


def _shard2d(name, t):
    if name == 'rec_w_in':
        return jnp.pad(t[0], ((0, 0), (0, REC_PAD - REC_IN)))
    return t.reshape((-1, t.shape[-1]))


def _cols_to_slabs(g, k=N_DEV):
    r, n = g.shape
    return jnp.transpose(g.reshape(r, k, n // k), (1, 0, 2))


def _slabs_to_cols(s):
    k, r, c_ = s.shape
    return jnp.transpose(s, (1, 0, 2)).reshape(r, k * c_)
SMALL_SHARDED = [('s5_d', 1, (1, S5_W)), ('s5_glu_b', 1, (1, S5_W)), ('dn_conv', 2, (1, 4, 2304)),
                 ('ffn_conv', 2, (2, 3, 2 * D_FF))]
REPLICATED = [('ada_b', (2, 6 * D)), ('norm_mix', (2, D)), ('norm_ffn', (2, D)), ('attn_q_norm_a', (1, HD)),
              ('attn_k_norm_a', (1, HD)), ('attn_q_norm_b', (1, HD)), ('attn_k_norm_b', (1, HD)),
              ('attn_sinks', (1, 8)), ('s5_lambda_re', (1, 16, 64)), ('s5_lambda_im', (1, 16, 64)),
              ('s5_log_dt', (1, 16)), ('s5_b_re', (1, 16, 64, 16)), ('s5_b_im', (1, 16, 64, 16)),
              ('s5_c_re', (1, 16, 16, 64)), ('s5_c_im', (1, 16, 16, 64)), ('dn_a_log', (1, DN_H)),
              ('dn_dt_bias', (1, DN_H)), ('dn_out_norm', (1, 128))]


def _numel(shp):
    return int(np.prod(shp))


def kernel(x, c, ada_w, ada_b, norm_mix, norm_ffn, attn_w_in, attn_q_norm_a, attn_k_norm_a, attn_q_norm_b, attn_k_norm_b, attn_sinks, attn_w_out, rec_w_in, s5_lambda_re, s5_lambda_im, s5_log_dt, s5_b_re, s5_b_im, s5_c_re, s5_c_im, s5_d, s5_glu_w, s5_glu_b, dn_conv, dn_a_log, dn_dt_bias, dn_out_norm, rec_w_out, ffn_w_up, ffn_conv, ffn_w_down, loss_target, m_ada_w, m_ada_b, m_norm_mix, m_norm_ffn, m_attn_w_in, m_attn_q_norm_a, m_attn_k_norm_a, m_attn_q_norm_b, m_attn_k_norm_b, m_attn_sinks, m_attn_w_out, m_rec_w_in, m_s5_lambda_re, m_s5_lambda_im, m_s5_log_dt, m_s5_b_re, m_s5_b_im, m_s5_c_re, m_s5_c_im, m_s5_d, m_s5_glu_w, m_s5_glu_b, m_dn_conv, m_dn_a_log, m_dn_dt_bias, m_dn_out_norm, m_rec_w_out, m_ffn_w_up, m_ffn_conv, m_ffn_w_down, v_ada_w, v_ada_b, v_norm_mix, v_norm_ffn, v_attn_w_in, v_attn_q_norm_a, v_attn_k_norm_a, v_attn_q_norm_b, v_attn_k_norm_b, v_attn_sinks, v_attn_w_out, v_rec_w_in, v_s5_lambda_re, v_s5_lambda_im, v_s5_log_dt, v_s5_b_re, v_s5_b_im, v_s5_c_re, v_s5_c_im, v_s5_d, v_s5_glu_w, v_s5_glu_b, v_dn_conv, v_dn_a_log, v_dn_dt_bias, v_dn_out_norm, v_rec_w_out, v_ffn_w_up, v_ffn_conv, v_ffn_w_down):
    loc = locals()
    W = {n: loc[n] for n in WEIGHTS}
    M = {n: loc["m_" + n] for n in WEIGHTS}
    V = {n: loc["v_" + n] for n in WEIGHTS}
    _, _, _, me = _mesh_pos()
    L = x.shape[1]
    x0, tgt = x[0], loss_target[0]

    small_in = jnp.concatenate([c.reshape(-1)] + [W[n].reshape(-1) for n, _, _ in SMALL_SHARDED])
    si, att_in_all, att_out_all = all_gather_many(
        [_pack_rows(small_in, 1024, 8), attn_w_in[0].astype(BF16), attn_w_out[0].astype(BF16)], "gather_first")
    si = si.reshape(N_DEV, -1)
    c_all = si[:, :D]
    off = D
    small_full = {}
    for n, ax, shp in SMALL_SHARDED:
        k = _numel(shp) // N_DEV
        small_full[n] = _from_slabs(si[:, off:off + k], ax, shp)
        off += k

    cond_all = jax.nn.silu(c_all)
    modp = jnp.concatenate([matmul([(cond_all, ada_w[l].astype(BF16))], "nn", f"ada{l}") for l in range(2)], axis=0)
    modp_all = all_gather(modp, "gather_mod")
    mods = []
    for l in range(2):
        row = lax.dynamic_index_in_dim(modp_all, l * N_DEV + me, axis=1, keepdims=False)
        mod = row.reshape(1, 6 * D) + ada_b[l].reshape(1, 6 * D)
        mods.append([mod[:, i * D:(i + 1) * D] for i in range(6)])

    w_att_in, w_att_out = _slabs_to_cols(att_in_all), att_out_all.reshape(D, D)
    bf = lambda t: t.astype(BF16)
    ffn_shards = [[bf(ffn_w_up[l]), bf(ffn_w_down[l])] for l in range(2)]
    rec_shards = [bf(_shard2d('rec_w_in', rec_w_in)), bf(s5_glu_w[0]), bf(rec_w_out[0])]
    ffn_cw = [small_full['ffn_conv'][l] for l in range(2)]
    dn_cw = small_full['dn_conv'][0]
    s5_dskip, glu_b = small_full['s5_d'], small_full['s5_glu_b']
    row = lambda t: t.reshape(1, -1)

    sh1, sc1, g1, sh2, sc2, g2 = mods[0]
    h1 = gate_norm_fwd(x0, None, None, row(norm_mix[0]), sh1, sc1, "l0_norm1")
    wvec, sinkvec = attn_vectors(attn_q_norm_a[0], attn_k_norm_a[0], attn_q_norm_b[0], attn_k_norm_b[0], attn_sinks[0])
    y0, res_att, got = attention_block_fwd(
        h1, w_att_in, wvec, sinkvec, w_att_out, "att",
        comms={'swa': (ffn_shards[0][:1], True), 1: (ffn_shards[0][1:], True), 4: (rec_shards, True)})
    split_up = lambda up_all: (_slabs_to_cols(up_all[:4]), _slabs_to_cols(up_all[4:]))
    w_up, w_down = [split_up(got['swa'][0])], [got[1][0].reshape(D_FF, D)]
    w_rec_in = rec_cols_permute(got[4][0].reshape(D, REC_PAD))
    glu_w, w_rec_out = got[4][1].reshape(S5_W, S5_W), got[4][2].reshape(D, D)
    w_rec_out = jnp.concatenate([w_rec_out[S5_W:], w_rec_out[:S5_W]], axis=0)
    x1, h2 = gate_norm_fwd(x0, y0, g1, row(norm_ffn[0]), sh2, sc2, "l0_norm2")
    f0, res_f0 = ffn_block_fwd(h2, w_up[0][0], w_up[0][1], ffn_cw[0], w_down[0], "ffn0")
    t1, tc1, tg1, t2, tc2, tg2 = mods[1]
    x2, h3 = gate_norm_fwd(x1, f0, g2, row(norm_mix[1]), t1, tc1, "l1_norm1")
    rin = matmul([(h3, w_rec_in)], "nn", "rec_in")
    s5p, s5p_vjp = jax.vjp(s5_params, s5_lambda_re[0], s5_lambda_im[0], s5_log_dt[0], s5_b_re[0], s5_b_im[0],
                           s5_c_re[0], s5_c_im[0])
    u = rin[:, REC_U0:REC_A0]
    yc, res_s5 = s5_block_fwd(u, s5p, s5_dskip, glu_w, glu_b, "s5")
    yd, res_dn, got_ffn1 = dn_block_fwd(rin, dn_cw, dn_a_log[0], dn_dt_bias[0], dn_out_norm[0], "dn",
                                        comm=(ffn_shards[1], True))
    w_up.append(split_up(got_ffn1[0]))
    w_down.append(got_ffn1[1].reshape(D_FF, D))
    ycat = jnp.concatenate([yd, yc], axis=1)
    y1 = matmul([(ycat, w_rec_out)], "nn", "rec_out")
    x3, h4 = gate_norm_fwd(x2, y1, tg1, row(norm_ffn[1]), t2, tc2, "l1_norm2")
    f1, res_f1 = ffn_block_fwd(h4, w_up[1][0], w_up[1][1], ffn_cw[1], w_down[1], "ffn1")
    dx4, df1, lsum = final_loss(x3, f1, tg2, tgt, "loss")

    G = {}
    d_tg2 = lsum[8:16].sum(axis=0)
    dh4, gf1, _ = ffn_block_bwd(df1, res_f1, w_up[1][0], w_up[1][1], ffn_cw[1], w_down[1], "ffn1")
    ffn_slabs = lambda g: [g['w_up'], g['w_down'].reshape(N_DEV, D_FF // N_DEV, D)]
    dx3, dy1, s = gate_norm_bwd(x3, y1, tg1, row(norm_ffn[1]), tc2, dx4, dh4, "l1_dnorm2")
    s = s.reshape(4, 8, D).sum(axis=1)
    d_tg1, d_nffn1, d_t2, d_tc2 = s[0], s[1] * (1.0 + tc2[0]), s[2], s[1] * norm_ffn[1]
    g_rec_out = matmul([(ycat, dy1)], "tn", "rec_out_dw", out_dtype=BF16)
    g_rec_out = jnp.concatenate([g_rec_out[DN_W:], g_rec_out[:DN_W]], axis=0).reshape(N_DEV, D // N_DEV, D)
    dycat = matmul([(dy1, w_rec_out)], "nt", "rec_out_dx")
    du, s5cot, gs5 = s5_block_bwd(dycat, res_s5, s5p, s5_dskip, glu_w, glu_b, "s5", dout_col=DN_W // S5_W)
    s5g = s5p_vjp(s5cot)
    dqkv, dz, da, dbraw, gdn, recv_ffn1 = dn_block_bwd(dycat, res_dn, rin, dn_cw, dn_out_norm[0], "dn",
                                                       comm=(ffn_slabs(gf1), False))
    drin = jnp.concatenate([dqkv, dz, du.astype(BF16), da.astype(BF16), dbraw.astype(BF16),
                            jnp.zeros((L, REC_PAD - REC_IN), BF16)], axis=1)
    g_rec_in = rec_cols_restore(matmul([(h3, drin)], "tn", "rec_in_dw", out_dtype=BF16)).reshape(
        N_DEV, D // N_DEV, REC_PAD)
    g_glu = gs5['glu_w'].astype(BF16).reshape(N_DEV, S5_W // N_DEV, S5_W)
    dh3 = matmul([(drin, w_rec_in)], "nt", "rec_in_dx")
    dx2, df0, s = gate_norm_bwd(x2, f0, g2, row(norm_mix[1]), tc1, dx3, dh3, "l1_dnorm1")
    s = s.reshape(4, 8, D).sum(axis=1)
    d_g2, d_nmix1, d_t1, d_tc1 = s[0], s[1] * (1.0 + tc1[0]), s[2], s[1] * norm_mix[1]
    dh2, gf0, recv_rec = ffn_block_bwd(df0, res_f0, w_up[0][0], w_up[0][1], ffn_cw[0], w_down[0], "ffn0",
                                       comm=([g_rec_in, g_glu, g_rec_out], False))
    dx1, dy0, s = gate_norm_bwd(x1, y0, g1, row(norm_ffn[0]), sc2, dx2, dh2, "l0_dnorm2")
    s = s.reshape(4, 8, D).sum(axis=1)
    d_g1, d_nffn0, d_sh2, d_sc2 = s[0], s[1] * (1.0 + sc2[0]), s[2], s[1] * norm_ffn[0]
    dh1, gatt, got_b = attention_block_bwd(dy0, res_att, w_att_in, wvec, sinkvec, w_att_out, "att",
                                           comms={'swa': (ffn_slabs(gf0)[:1], False), 1: (ffn_slabs(gf0)[1:], False)})
    recv_ffn0 = [got_b['swa'][0], got_b[1][0]]
    att_slabs = [_cols_to_slabs(gatt['w_in']), gatt['w_out'].reshape(N_DEV, D // N_DEV, D)]
    (grad_x, s), recv_att = gate_norm_bwd(x0, None, None, row(norm_mix[0]), sc1, dx1, dh1, "l0_dnorm1",
                                          comm=(att_slabs, False))
    s = s.reshape(4, 8, D).sum(axis=1)
    d_nmix0, d_sh1, d_sc1 = s[1] * (1.0 + sc1[0]), s[2], s[1] * norm_mix[0]
    dmod = jnp.stack([jnp.concatenate([d_sh1, d_sc1, d_g1, d_sh2, d_sc2, d_g2]),
                      jnp.concatenate([d_t1, d_tc1, d_tg1, d_t2, d_tc2, d_tg2])])

    P = {'ada_b': dmod, 'norm_mix': jnp.stack([d_nmix0, d_nmix1]), 'norm_ffn': jnp.stack([d_nffn0, d_nffn1]),
         'attn_q_norm_a': gatt['q_norm_a'], 'attn_k_norm_a': gatt['k_norm_a'], 'attn_q_norm_b': gatt['q_norm_b'],
         'attn_k_norm_b': gatt['k_norm_b'], 'attn_sinks': gatt['sinks'],
         's5_lambda_re': s5g[0], 's5_lambda_im': s5g[1], 's5_log_dt': s5g[2], 's5_b_re': s5g[3], 's5_b_im': s5g[4],
         's5_c_re': s5g[5], 's5_c_im': s5g[6], 'dn_a_log': gdn['a_log'], 'dn_dt_bias': gdn['dt_bias'],
         'dn_out_norm': gdn['out_norm'],
         's5_d': gs5['dskip'], 's5_glu_b': gs5['glu_b'], 'dn_conv': gdn['conv'],
         'ffn_conv': jnp.stack([gf0['conv'], gf1['conv']])}

    out = {k: {} for k in ("g", "d", "m", "v")}
    keys = ("g", "d", "m", "v")
    recv = {'attn_w_in': recv_att[0], 'attn_w_out': recv_att[1], 'rec_w_in': recv_rec[0], 's5_glu_w': recv_rec[1],
            'rec_w_out': recv_rec[2]}
    for n, gr_ in recv.items():
        res4 = reduce_adamw(gr_, _shard2d(n, W[n]), _shard2d(n, M[n]), _shard2d(n, V[n]), "adamw_" + n)
        for key, t in zip(keys, res4):
            out[key][n] = (t[:, :REC_IN] if n == 'rec_w_in' else t).reshape(W[n].shape)
    for n, idx in (('ffn_w_up', 0), ('ffn_w_down', 1)):
        per_layer = [reduce_adamw(r_[idx], W[n][l], M[n][l], V[n][l], f"adamw_{n}{l}")
                     for l, r_ in enumerate((recv_ffn0, recv_ffn1))]
        for i, key in enumerate(keys):
            out[key][n] = jnp.stack([per_layer[0][i], per_layer[1][i]])

    rep_sizes = [_numel(shp) for _, shp in REPLICATED]
    ss_sizes = [_numel(shp) for _, _, shp in SMALL_SHARDED]
    rep_offs = _offsets(rep_sizes + ss_sizes + [1])
    parts = [P[n].reshape(-1) for n, _ in REPLICATED] + [P[n].reshape(-1) for n, _, _ in SMALL_SHARDED]
    parts.append(lsum[0:8].sum().reshape(1))
    spack = _pack_rows(jnp.concatenate(parts), 1024, 8)
    sall = all_gather(spack, "gather_small_grads")
    n_rest = sum(ss_sizes) + 1
    pk = lambda d: _pack_rows(jnp.concatenate([d[n].reshape(-1) for n, _ in REPLICATED]
                                              + [jnp.zeros((n_rest,), F32)]), 1024, 8)
    sg, sd_, sm, sv = [t.reshape(-1) for t in reduce_adamw(sall, pk(W), pk(M), pk(V), "adamw_small")]
    loss = 0.5 * sg[rep_offs[-1]] / D

    dmod_all = sall.reshape(N_DEV, -1)[:, :2 * 6 * D].reshape(N_DEV, 2, 6 * D)
    dmod_mine = lax.dynamic_slice_in_dim(dmod_all, me * (6 * D // N_DEV), 6 * D // N_DEV, axis=2)
    g_ada = jnp.stack([matmul([(cond_all, dmod_mine[:, l])], "tn", f"ada{l}_dw") for l in range(2)])
    ada2d = lambda t: t.reshape(2 * D, 6 * D // N_DEV)
    for key, t in zip(("g", "d", "m", "v"), reduce_adamw(ada2d(g_ada)[None], ada2d(ada_w), ada2d(m_ada_w),
                                                          ada2d(v_ada_w), "adamw_ada_w")):
        out[key]['ada_w'] = t.reshape(ada_w.shape)
    own = []
    for (n, ax, shp), o in zip(SMALL_SHARDED, rep_offs[len(REPLICATED):]):
        slabs = _to_slabs(sg[o:o + _numel(shp)].reshape(shp), ax)
        own.append(lax.dynamic_index_in_dim(slabs, me, axis=0, keepdims=False))
    own_names = [n for n, _, _ in SMALL_SHARDED]
    pk = lambda d: _pack_rows(jnp.concatenate([d[n].reshape(-1) for n in own_names]), 1024, 8)
    og, od, om, ov = [t.reshape(-1) for t in reduce_adamw(_pack_rows(jnp.concatenate(own), 1024, 8)[None],
                                                          pk(W), pk(M), pk(V), "adamw_own")]

    def unpack(names_shapes, bufs):
        o = 0
        for n, shp in names_shapes:
            k = _numel(shp)
            for key, buf in zip(("g", "d", "m", "v"), bufs):
                out[key][n] = buf[o:o + k].reshape(shp)
            o += k

    unpack(REPLICATED, (sg, sd_, sm, sv))
    unpack([(n, W[n].shape) for n in own_names], (og, od, om, ov))
    return (loss, grad_x[None], *[out["g"][n] for n in WEIGHTS], *[out["d"][n] for n in WEIGHTS],
            *[out["m"][n] for n in WEIGHTS], *[out["v"][n] for n in WEIGHTS])
```

```python
import functools
import math

import numpy as np
import jax
import jax.numpy as jnp
from jax import lax
from jax.experimental import pallas as pl
from jax.experimental.pallas import tpu as pltpu

F32 = jnp.float32
BF16 = jnp.bfloat16

N_DEV = 8
D = 1024
HD = 64
BLK = 128
ATTN_IN = 2304
CB = ATTN_IN // 128
B_BRANCHES = ((128, 1), (512, 4), (2048, 16))
S5_W = 256
S5_P = 1024
DN_H = 6
DN_DK = 128
DN_C = 64
REC_IN = 3340
REC_PAD = 3456
D_FF = 2816
EPS = 1e-6
ADAM_LR, ADAM_B1, ADAM_B2, ADAM_EPS, ADAM_WD, ADAM_STEP = 0.001, 0.9, 0.999, 1e-8, 0.01, 10
VMEM_LIMIT = 48 * 1024 * 1024

ALIBI = np.asarray(2.0 ** (-8.0 * np.arange(1, 17) / 16), dtype=np.float32)


def _cparams(*sem):
    return pltpu.CompilerParams(dimension_semantics=tuple(sem), vmem_limit_bytes=VMEM_LIMIT)


def _tile(n, target):
    if n <= target:
        return n
    best = None
    for t in range(128, target + 1, 128):
        if n % t == 0:
            best = t
    assert best is not None, (n, target)
    return best


def _rtile(n, target, mult=8):
    if n <= target:
        return n
    best = None
    for t in range(mult, target + 1, mult):
        if n % t == 0:
            best = t
    assert best is not None, (n, target)
    return best


def _fold8(x):
    r, c = x.shape
    return x.reshape(r // 8, 8, c).sum(axis=0)


def _sigmoid(x):
    return 1.0 / (1.0 + jnp.exp(-x))


_DIMS = {"nn": (((1,), (0,)), ((), ())), "nt": (((1,), (1,)), ((), ())), "tn": (((0,), (0,)), ((), ()))}


MM_FULL_K = 3584


MM_VMEM_BUDGET = 40 << 20


def matmul(pairs, mode, name, out_dtype=F32, tm=1024, tn=1536, tk=1024):
    a0, b0 = pairs[0]
    if mode == "nn":
        (M, K), N = a0.shape, b0.shape[1]
    elif mode == "nt":
        (M, K), N = a0.shape, b0.shape[0]
    else:
        (K, M), N = a0.shape, b0.shape[1]
        tm = 1536
    tn = _tile(N, tn)
    tk = K if K <= MM_FULL_K else _tile(K, tk)
    nk = K // tk
    npair = len(pairs)
    dims = _DIMS[mode]
    kdim = 0 if mode == "tn" else 1
    tks = [a.shape[kdim] for a, _ in pairs]
    assert all(t == K for t in tks) or (nk == 1 and max(tks) <= MM_FULL_K), tks
    if nk > 1:
        tks = [tk] * npair

    def planned(tm_):
        ab = sum(tm_ * t * a.dtype.itemsize + t * tn * b.dtype.itemsize for (a, b), t in zip(pairs, tks))
        return 2 * ab + 2 * tm_ * tn * jnp.dtype(out_dtype).itemsize + (tm_ * tn * 4 if nk > 1 else 0)

    while True:
        tm_try = _rtile(M, tm) if M % 128 else _tile(M, tm)
        if planned(tm_try) <= MM_VMEM_BUDGET or tm <= 128:
            break
        tm //= 2
    tm = tm_try

    def body(*refs):
        o_ref = refs[2 * npair]
        tot = None
        for p in range(npair):
            part = lax.dot_general(refs[2 * p][...].astype(BF16), refs[2 * p + 1][...].astype(BF16),
                                   dims, preferred_element_type=F32)
            tot = part if tot is None else tot + part
        if nk == 1:
            o_ref[...] = tot.astype(o_ref.dtype)
            return
        acc_ref = refs[2 * npair + 1]
        k = pl.program_id(2)

        @pl.when(k == 0)
        def _():
            acc_ref[...] = tot

        @pl.when(k > 0)
        def _():
            acc_ref[...] += tot

        @pl.when(k == nk - 1)
        def _():
            o_ref[...] = acc_ref[...].astype(o_ref.dtype)

    def specs(t):
        if mode == "nn":
            return [pl.BlockSpec((tm, t), lambda j, i, k: (i, k)), pl.BlockSpec((t, tn), lambda j, i, k: (k, j))]
        if mode == "nt":
            return [pl.BlockSpec((tm, t), lambda j, i, k: (i, k)), pl.BlockSpec((tn, t), lambda j, i, k: (j, k))]
        return [pl.BlockSpec((t, tm), lambda j, i, k: (k, i)), pl.BlockSpec((t, tn), lambda j, i, k: (k, j))]

    flat = [t for pr in pairs for t in pr]
    return pl.pallas_call(
        body, name=name, grid=(N // tn, M // tm, nk),
        in_specs=[s for t in tks for s in specs(t)],
        out_specs=pl.BlockSpec((tm, tn), lambda j, i, k: (i, j)),
        out_shape=jax.ShapeDtypeStruct((M, N), out_dtype),
        scratch_shapes=[pltpu.VMEM((tm, tn), F32)] if nk > 1 else [],
        compiler_params=_cparams("parallel", "parallel", "arbitrary"),
    )(*flat)


def gate_norm_fwd(x, y, gate, nw, sh, sc, name):
    L, C = x.shape
    tl = _rtile(L, 512)
    has_gate = y is not None

    def body(*refs):
        if has_gate:
            x_ref, y_ref, g_ref, nw_ref, sh_ref, sc_ref, xn_ref, h_ref = refs
            xn = x_ref[...] + g_ref[...] * y_ref[...]
            xn_ref[...] = xn
        else:
            x_ref, nw_ref, sh_ref, sc_ref, h_ref = refs
            xn = x_ref[...]
        r = lax.rsqrt(jnp.mean(xn * xn, axis=-1, keepdims=True) + EPS)
        h = (xn * r * nw_ref[...]) * (1.0 + sc_ref[...]) + sh_ref[...]
        h_ref[...] = h.astype(BF16)

    big = pl.BlockSpec((tl, C), lambda i: (i, 0))
    vec = pl.BlockSpec((1, C), lambda i: (0, 0))
    if has_gate:
        ins, in_specs = (x, y, gate, nw, sh, sc), [big, big, vec, vec, vec, vec]
        out_shape = (jax.ShapeDtypeStruct((L, C), F32), jax.ShapeDtypeStruct((L, C), BF16))
        out_specs = (big, big)
    else:
        ins, in_specs = (x, nw, sh, sc), [big, vec, vec, vec]
        out_shape = jax.ShapeDtypeStruct((L, C), BF16)
        out_specs = big
    return pl.pallas_call(body, name=name, grid=(L // tl,), in_specs=in_specs, out_specs=out_specs,
                          out_shape=out_shape, compiler_params=_cparams("parallel"))(*ins)


def gate_norm_bwd(xn, y, gate, nw, sc, dxn_direct, dh, name, comm=None):
    L, C = xn.shape
    tl = _rtile(L, 256)
    has_gate = y is not None
    has_direct = dxn_direct is not None

    def body(*refs):
        refs = list(refs)
        xn_ref = refs.pop(0)
        y_ref = refs.pop(0) if has_gate else None
        g_ref = refs.pop(0) if has_gate else None
        nw_ref = refs.pop(0)
        sc_ref = refs.pop(0)
        dd_ref = refs.pop(0) if has_direct else None
        dh_ref = refs.pop(0)
        dxn_ref = refs.pop(0)
        dy_ref = refs.pop(0) if has_gate else None
        sums_ref = refs.pop(0)

        @pl.when(pl.program_id(0) == 0)
        def _():
            sums_ref[...] = jnp.zeros_like(sums_ref)

        xv = xn_ref[...]
        dh_v = dh_ref[...]
        r = lax.rsqrt(jnp.mean(xv * xv, axis=-1, keepdims=True) + EPS)
        n = xv * r
        a = nw_ref[...] * (1.0 + sc_ref[...])
        dn = dh_v * a
        dx = r * (dn - n * jnp.mean(dn * n, axis=-1, keepdims=True))
        if has_direct:
            dx = dx + dd_ref[...]
        dxn_ref[...] = dx
        sums_ref[8:16, :] += _fold8(dh_v * n)
        sums_ref[16:24, :] += _fold8(dh_v)
        if has_gate:
            dy_ref[...] = (dx * g_ref[...]).astype(BF16)
            sums_ref[0:8, :] += _fold8(dx * y_ref[...])

    big = pl.BlockSpec((tl, C), lambda i: (i, 0))
    vec = pl.BlockSpec((1, C), lambda i: (0, 0))
    ins, in_specs = [xn], [big]
    if has_gate:
        ins += [y, gate]
        in_specs += [big, vec]
    ins += [nw, sc]
    in_specs += [vec, vec]
    if has_direct:
        ins.append(dxn_direct)
        in_specs.append(big)
    ins.append(dh)
    in_specs.append(big)
    out_shape = [jax.ShapeDtypeStruct((L, C), F32)]
    out_specs = [big]
    if has_gate:
        out_shape.append(jax.ShapeDtypeStruct((L, C), BF16))
        out_specs.append(big)
    out_shape.append(jax.ShapeDtypeStruct((32, C), F32))
    out_specs.append(pl.BlockSpec((32, C), lambda i: (0, 0)))
    return _call(body, ins, name=name, grid=(L // tl,), in_specs=in_specs, out_specs=tuple(out_specs),
                 out_shape=tuple(out_shape), sem=("arbitrary",), comm=comm)


def final_loss(x, f, gate, target, name):
    L, C = x.shape
    tl = _rtile(L, 256)

    def body(x_ref, f_ref, g_ref, t_ref, dy_ref, df_ref, sums_ref):
        @pl.when(pl.program_id(0) == 0)
        def _():
            sums_ref[...] = jnp.zeros_like(sums_ref)

        fv = f_ref[...]
        err = x_ref[...] + g_ref[...] * fv - t_ref[...]
        dy = err * (1.0 / C)
        dy_ref[...] = dy
        df_ref[...] = (dy * g_ref[...]).astype(BF16)
        sums_ref[0:8, :] += _fold8(err * err)
        sums_ref[8:16, :] += _fold8(dy * fv)

    big = pl.BlockSpec((tl, C), lambda i: (i, 0))
    vec = pl.BlockSpec((1, C), lambda i: (0, 0))
    return pl.pallas_call(
        body, name=name, grid=(L // tl,), in_specs=[big, big, vec, big],
        out_specs=(big, big, pl.BlockSpec((16, C), lambda i: (0, 0))),
        out_shape=(jax.ShapeDtypeStruct((L, C), F32), jax.ShapeDtypeStruct((L, C), BF16),
                   jax.ShapeDtypeStruct((16, C), F32)),
        compiler_params=_cparams("arbitrary"))(x, f, gate, target)


def _seg_ones(seg):
    r = lax.broadcasted_iota(jnp.int32, (128, 128), 0) // seg
    c = lax.broadcasted_iota(jnp.int32, (128, 128), 1) // seg
    return (r == c).astype(BF16)


def _segsum(t, ones):
    hi = t.astype(BF16)
    lo = (t - hi.astype(F32)).astype(BF16)
    return (jnp.dot(hi, ones, preferred_element_type=F32) + jnp.dot(lo, ones, preferred_element_type=F32))


_NORMED_TILES = tuple(list(range(0, 5)) + list(range(6, 14)))


DIL = (4, 16)
B_COLS0, B_W = 768, 1536
DIL_TL = 256


def _to_dilated(scr_ref, out_ref, d, cast=None):
    nj, tl, _ = scr_ref.shape
    for r in range(d):
        for j in range(nj):
            piece = scr_ref[j, pl.ds(r, tl // d, stride=d), :]
            c0 = (r * nj + j) * 128
            out_ref[:, c0:c0 + 128] = piece if cast is None else piece.astype(cast)


def _from_dilated(in_ref, scr_ref, d):
    nj, tl, _ = scr_ref.shape
    for r in range(d):
        for j in range(nj):
            c0 = (r * nj + j) * 128
            scr_ref[j, pl.ds(r, tl // d, stride=d), :] = in_ref[:, c0:c0 + 128]


def _dil_spec(tl, d, width):
    return pl.BlockSpec((tl // d, d * width), lambda i: (i, 0))


def qknorm_fwd(qkv, wvec, name):
    L, C = qkv.shape
    tl = DIL_TL

    def body(x_ref, w_ref, o_ref, o4_ref, o16_ref, scr_ref):
        ones = _seg_ones(HD)
        for t in range(CB):
            cs = slice(t * 128, (t + 1) * 128)
            x = x_ref[:, cs]
            if t in _NORMED_TILES:
                ms = _segsum(x * x, ones) * (1.0 / HD)
                x = x * lax.rsqrt(ms + EPS) * w_ref[:, cs]
            o_ref[:, cs] = x.astype(BF16)
            if t * 128 >= B_COLS0:
                scr_ref[t - B_COLS0 // 128] = x
        _to_dilated(scr_ref, o4_ref, 4, BF16)
        _to_dilated(scr_ref, o16_ref, 16, BF16)

    return pl.pallas_call(
        body, name=name, grid=(L // tl,),
        in_specs=[pl.BlockSpec((tl, C), lambda i: (i, 0)), pl.BlockSpec((1, C), lambda i: (0, 0))],
        out_specs=(pl.BlockSpec((tl, C), lambda i: (i, 0)), _dil_spec(tl, 4, B_W), _dil_spec(tl, 16, B_W)),
        out_shape=(jax.ShapeDtypeStruct((L, C), BF16), jax.ShapeDtypeStruct((L // 4, 4 * B_W), BF16),
                   jax.ShapeDtypeStruct((L // 16, 16 * B_W), BF16)),
        scratch_shapes=[pltpu.VMEM((B_W // 128, tl, 128), F32)], compiler_params=_cparams("parallel"))(qkv, wvec)


def qknorm_bwd(qkv, wvec, d_a, d_b, name):
    L, C = qkv.shape
    tl = DIL_TL

    def body(x_ref, w_ref, dqa, dka, dva, q1, k1, v1, q4, k4, v4, q16, k16, v16, dx_ref, sums_ref,
             dy_ref, s4_ref, s16_ref):
        @pl.when(pl.program_id(0) == 0)
        def _():
            sums_ref[...] = jnp.zeros_like(sums_ref)

        dy_ref[:, 0:512] = dqa[...]
        for off, ref in ((512, dka), (640, dva)):
            for g in range(2):
                acc = ref[:, g * 256:g * 256 + HD]
                for h in range(1, 4):
                    acc = acc + ref[:, g * 256 + h * HD:g * 256 + (h + 1) * HD]
                dy_ref[:, off + g * HD:off + (g + 1) * HD] = acc
        for off, r1, r4, r16 in ((768, q1, q4, q16), (1280, k1, k4, k16), (1792, v1, v4, v16)):
            _from_dilated(r4, s4_ref, 4)
            _from_dilated(r16, s16_ref, 16)
            for j in range(4):
                dy_ref[:, off + j * 128:off + (j + 1) * 128] = r1[:, j * 128:(j + 1) * 128] + s4_ref[j] + s16_ref[j]

        ones = _seg_ones(HD)
        for t in range(CB):
            cs = slice(t * 128, (t + 1) * 128)
            d = dy_ref[:, cs]
            if t in _NORMED_TILES:
                x = x_ref[:, cs]
                r = lax.rsqrt(_segsum(x * x, ones) * (1.0 / HD) + EPS)
                n = x * r
                dn = d * w_ref[:, cs]
                dx_ref[:, cs] = (r * (dn - n * (_segsum(dn * n, ones) * (1.0 / HD)))).astype(BF16)
                sums_ref[:, cs] += _fold8(d * n)
            else:
                dx_ref[:, cs] = d.astype(BF16)

    big = pl.BlockSpec((tl, C), lambda i: (i, 0))
    p512 = pl.BlockSpec((tl, 512), lambda i: (i, 0))
    return pl.pallas_call(
        body, name=name, grid=(L // tl,),
        in_specs=[big, pl.BlockSpec((1, C), lambda i: (0, 0))] + [p512] * 6 + [_dil_spec(tl, 4, 512)] * 3
        + [_dil_spec(tl, 16, 512)] * 3,
        out_specs=(big, pl.BlockSpec((8, C), lambda i: (0, 0))),
        out_shape=(jax.ShapeDtypeStruct((L, C), BF16), jax.ShapeDtypeStruct((8, C), F32)),
        scratch_shapes=[pltpu.VMEM((tl, C), F32), pltpu.VMEM((4, tl, 128), F32), pltpu.VMEM((4, tl, 128), F32)],
        compiler_params=_cparams("arbitrary"))(qkv, wvec, *d_a, *d_b[0], *d_b[1], *d_b[2])


def _attn_scores(q, kw, n, slope, step, maxdist):
    s = lax.dot_general(q, kw, (((1,), (1,)), ((), ())), preferred_element_type=F32) * (HD ** -0.5)
    qi = lax.broadcasted_iota(jnp.int32, (BLK, 2 * BLK), 0)
    sj = lax.broadcasted_iota(jnp.int32, (BLK, 2 * BLK), 1)
    dist = BLK + qi - sj
    valid = (dist >= 0) & (dist <= maxdist) & ((n > 0) | (sj >= BLK))
    bias = (-slope) * (step * dist).astype(F32)
    return jnp.where(valid, s + bias, -jnp.inf), valid


ATT_NQ = 4


def _attn_operands(hp, gqa, q_ref, kh_ref, kc_ref, vh_ref, vc_ref):
    ops = []
    for b in range(ATT_NQ):
        rows = slice(b * BLK, (b + 1) * BLK)
        prev = slice((b - 1) * BLK, b * BLK)
        for e in range(2):
            cs = slice(e * HD, (e + 1) * HD)
            if gqa:
                ksel = lambda ref, r: jnp.where(hp >= 2, ref[r, 64:128], ref[r, 0:64])
            else:
                ksel = lambda ref, r, cs=cs: ref[r, cs]
            kprev = ksel(kh_ref, slice(0, BLK)) if b == 0 else ksel(kc_ref, prev)
            vprev = ksel(vh_ref, slice(0, BLK)) if b == 0 else ksel(vc_ref, prev)
            ops.append((b, e, rows, cs, q_ref[rows, cs], jnp.concatenate([kprev, ksel(kc_ref, rows)], axis=0),
                        jnp.concatenate([vprev, ksel(vc_ref, rows)], axis=0)))
    return ops


def _attn_specs(cb, q_off, k_off, v_off, gqa):
    kcol = (lambda r, hp: r * cb + k_off) if gqa else (lambda r, hp: r * cb + k_off + hp)
    vcol = (lambda r, hp: r * cb + v_off) if gqa else (lambda r, hp: r * cb + v_off + hp)
    return kcol, vcol


def attn_fwd(X, d, q_off, k_off, v_off, gqa, slope0, maxdist, name, comm=None):
    Ls = X.shape[0]
    TQ = ATT_NQ * BLK
    nt = Ls // TQ
    slopes = jnp.asarray(ALIBI)

    def body(sl_ref, q_ref, kh_ref, kc_ref, vh_ref, vc_ref, o_ref, lse_ref):
        hp, t = pl.program_id(1), pl.program_id(2)
        ops = _attn_operands(hp, gqa, q_ref, kh_ref, kc_ref, vh_ref, vc_ref)
        s = [_attn_scores(q, kw, t if b == 0 else 1, sl_ref[slope0 + 2 * hp + e], d, maxdist)[0]
             for (b, e, rows, cs, q, kw, vw) in ops]
        m = [jnp.max(x, axis=-1, keepdims=True) for x in s]
        p = [jnp.exp(x - mm) for x, mm in zip(s, m)]
        l = [jnp.sum(x, axis=-1, keepdims=True) for x in p]
        o = [jnp.dot(x.astype(BF16), op[6], preferred_element_type=F32) / ll for x, op, ll in zip(p, ops, l)]
        for (b, e, rows, cs, q, kw, vw), oo, mm, ll in zip(ops, o, m, l):
            o_ref[rows, cs] = oo
            lse_ref[rows, cs] = jnp.broadcast_to(mm + jnp.log(ll), (BLK, HD))

    cb = X.shape[1] // (d * 128)
    kcol, vcol = _attn_specs(cb, q_off, k_off, v_off, gqa)
    tile, blk = (TQ, 128), (BLK, 128)
    halo = lambda t: jnp.maximum(t * ATT_NQ - 1, 0)
    in_specs = [
        pl.BlockSpec(memory_space=pltpu.SMEM),
        pl.BlockSpec(tile, lambda r, hp, t: (t, r * cb + q_off + hp)),
        pl.BlockSpec(blk, lambda r, hp, t: (halo(t), kcol(r, hp))),
        pl.BlockSpec(tile, lambda r, hp, t: (t, kcol(r, hp))),
        pl.BlockSpec(blk, lambda r, hp, t: (halo(t), vcol(r, hp))),
        pl.BlockSpec(tile, lambda r, hp, t: (t, vcol(r, hp))),
    ]
    out_spec = pl.BlockSpec(tile, lambda r, hp, t: (t, r * 4 + hp))
    out = jax.ShapeDtypeStruct((Ls, d * 512), F32)
    return _call(body, (slopes, X, X, X, X, X), name=name, grid=(d, 4, nt), in_specs=in_specs,
                 out_specs=(out_spec, out_spec), out_shape=(out, out),
                 sem=("parallel", "parallel", "arbitrary"), comm=comm)


def attn_bwd(X, o, lse, do, dlse, d, q_off, k_off, v_off, gqa, slope0, maxdist, name, comm=None):
    Ls = X.shape[0]
    slopes = jnp.asarray(ALIBI)

    TQ = ATT_NQ * BLK
    nt = Ls // TQ
    nt_dims, tn_dims = (((1,), (1,)), ((), ())), (((0,), (0,)), ((), ()))

    def body(sl_ref, q_ref, kh_ref, kc_ref, vh_ref, vc_ref, o_ref, lse_ref, do_ref, dlse_ref,
             dq_ref, dk_ref, dv_ref, ak_ref, av_ref, pk_ref, pv_ref):
        hp, t = pl.program_id(1), pl.program_id(2)

        @pl.when(t == 0)
        def _():
            pk_ref[...] = jnp.zeros_like(pk_ref)
            pv_ref[...] = jnp.zeros_like(pv_ref)

        @pl.when(t < nt)
        def _():
            ops = _attn_operands(hp, gqa, q_ref, kh_ref, kc_ref, vh_ref, vc_ref)
            sv = [_attn_scores(q, kw, t if b == 0 else 1, sl_ref[slope0 + 2 * hp + e], d, maxdist)
                  for (b, e, rows, cs, q, kw, vw) in ops]
            p = [jnp.where(valid, jnp.exp(s - lse_ref[op[2], op[1] * HD:op[1] * HD + 1]), 0.0)
                 for (s, valid), op in zip(sv, ops)]
            dov = [do_ref[op[2], op[3]] for op in ops]
            delta = [jnp.sum(dd * o_ref[op[2], op[3]], axis=-1, keepdims=True) for dd, op in zip(dov, ops)]
            dob = [dd.astype(BF16) for dd in dov]
            dp = [lax.dot_general(dd, op[6], nt_dims, preferred_element_type=F32) for dd, op in zip(dob, ops)]
            ds = [(pp * (x - dl + dlse_ref[op[2], op[1] * HD:op[1] * HD + 1])).astype(BF16)
                  for pp, x, dl, op in zip(p, dp, delta, ops)]
            dq = [jnp.dot(x, op[5], preferred_element_type=F32) * (HD ** -0.5) for x, op in zip(ds, ops)]
            dkw = [lax.dot_general(x, op[4], tn_dims, preferred_element_type=F32) * (HD ** -0.5)
                   for x, op in zip(ds, ops)]
            dvw = [lax.dot_general(pp.astype(BF16), dd, tn_dims, preferred_element_type=F32)
                   for pp, dd in zip(p, dob)]
            ak_ref[...] = jnp.zeros_like(ak_ref)
            av_ref[...] = jnp.zeros_like(av_ref)
            for (b, e, rows, cs, q, kw, vw), x, yk, yv in zip(ops, dq, dkw, dvw):
                dq_ref[rows, cs] = x
                ak_ref[b * BLK:(b + 2) * BLK, cs] += yk
                av_ref[b * BLK:(b + 2) * BLK, cs] += yv
            if nt == 1:
                dk_ref[...] = ak_ref[BLK:, :]
                dv_ref[...] = av_ref[BLK:, :]
                return
            last = slice(TQ - BLK, TQ)
            dk_ref[...] = pk_ref[...]
            dv_ref[...] = pv_ref[...]
            dk_ref[last, :] += ak_ref[0:BLK, :]
            dv_ref[last, :] += av_ref[0:BLK, :]
            pk_ref[...] = ak_ref[BLK:, :]
            pv_ref[...] = av_ref[BLK:, :]

        @pl.when(t == nt)
        def _():
            dk_ref[...] = pk_ref[...]
            dv_ref[...] = pv_ref[...]

    cb = X.shape[1] // (d * 128)
    kcol, vcol = _attn_specs(cb, q_off, k_off, v_off, gqa)
    tile, blk = (TQ, 128), (BLK, 128)
    cur = lambda t: jnp.minimum(t, nt - 1)
    halo = lambda t: jnp.maximum(cur(t) * ATT_NQ - 1, 0)
    ospec = pl.BlockSpec(tile, lambda r, hp, t: (cur(t), r * 4 + hp))
    in_specs = [
        pl.BlockSpec(memory_space=pltpu.SMEM),
        pl.BlockSpec(tile, lambda r, hp, t: (cur(t), r * cb + q_off + hp)),
        pl.BlockSpec(blk, lambda r, hp, t: (halo(t), kcol(r, hp))),
        pl.BlockSpec(tile, lambda r, hp, t: (cur(t), kcol(r, hp))),
        pl.BlockSpec(blk, lambda r, hp, t: (halo(t), vcol(r, hp))),
        pl.BlockSpec(tile, lambda r, hp, t: (cur(t), vcol(r, hp))),
        ospec, ospec, ospec, ospec,
    ]
    shifted = pl.BlockSpec(tile, lambda r, hp, t: (jnp.maximum(t - 1, 0), r * 4 + hp))
    out = jax.ShapeDtypeStruct((Ls, d * 512), F32)
    return _call(body, (slopes, X, X, X, X, X, o, lse, do, dlse), name=name, grid=(d, 4, nt + 1 if nt > 1 else 1),
                 in_specs=in_specs, out_specs=(ospec, shifted, shifted), out_shape=(out, out, out),
                 scratch_shapes=[pltpu.VMEM((TQ + BLK, 128), F32), pltpu.VMEM((TQ + BLK, 128), F32),
                                 pltpu.VMEM((TQ, 128), F32), pltpu.VMEM((TQ, 128), F32)],
                 sem=("parallel", "parallel", "arbitrary"), comm=comm)


def attn_merge_fwd(oa, la, sink, obs, lbs, name):
    L = oa.shape[0]
    tl = DIL_TL

    def body(oa_ref, la_ref, sk_ref, o1, o4, o16, l1, l4, l16, m_ref, so4, so16, sl4, sl16):
        m_ref[:, 0:512] = (oa_ref[...] * _sigmoid(la_ref[...] - sk_ref[...])).astype(BF16)
        for src, dst, d in ((o4, so4, 4), (o16, so16, 16), (l4, sl4, 4), (l16, sl16, 16)):
            _from_dilated(src, dst, d)
        for j in range(4):
            cs = slice(j * 128, (j + 1) * 128)
            a, b, c = l1[:, cs], sl4[j], sl16[j]
            mx = jnp.maximum(jnp.maximum(a, b), c)
            ea, eb, ec = jnp.exp(a - mx), jnp.exp(b - mx), jnp.exp(c - mx)
            inv = 1.0 / (ea + eb + ec)
            m_ref[:, 512 + j * 128:512 + (j + 1) * 128] = (
                (ea * inv) * o1[:, cs] + (eb * inv) * so4[j] + (ec * inv) * so16[j]).astype(BF16)

    big = pl.BlockSpec((tl, 512), lambda i: (i, 0))
    dil = [big, _dil_spec(tl, 4, 512), _dil_spec(tl, 16, 512)]
    return pl.pallas_call(
        body, name=name, grid=(L // tl,),
        in_specs=[big, big, pl.BlockSpec((1, 512), lambda i: (0, 0))] + dil + dil,
        out_specs=pl.BlockSpec((tl, 1024), lambda i: (i, 0)),
        out_shape=jax.ShapeDtypeStruct((L, 1024), BF16), scratch_shapes=[pltpu.VMEM((4, tl, 128), F32)] * 4,
        compiler_params=_cparams("parallel"),
    )(oa, la, sink, *obs, *lbs)


def attn_merge_bwd(dm, oa, la, sink, obs, lbs, name):
    L = oa.shape[0]
    tl = DIL_TL

    def body(dm_ref, oa_ref, la_ref, sk_ref, o1, o4, o16, l1, l4, l16,
             doa_ref, dla_ref, d1, d4, d16, g1, g4, g16, sums_ref, so4, so16, sl4, sl16, sd4, sd16, sg4, sg16):
        @pl.when(pl.program_id(0) == 0)
        def _():
            sums_ref[...] = jnp.zeros_like(sums_ref)

        for src, dst, d in ((o4, so4, 4), (o16, so16, 16), (l4, sl4, 4), (l16, sl16, 16)):
            _from_dilated(src, dst, d)
        ones = _seg_ones(HD)
        for t in range(4):
            cs = slice(t * 128, (t + 1) * 128)
            dma = dm_ref[:, cs]
            keep = _sigmoid(la_ref[:, cs] - sk_ref[:, cs])
            doa_ref[:, cs] = dma * keep
            tt = dma * oa_ref[:, cs] * keep * (1.0 - keep)
            dla_ref[:, cs] = _segsum(tt, ones)
            sums_ref[:, cs] += _fold8(-tt)
            dmb = dm_ref[:, 512 + t * 128:512 + (t + 1) * 128]
            a, b, c = l1[:, cs], sl4[t], sl16[t]
            mx = jnp.maximum(jnp.maximum(a, b), c)
            ea, eb, ec = jnp.exp(a - mx), jnp.exp(b - mx), jnp.exp(c - mx)
            inv = 1.0 / (ea + eb + ec)
            wa, wb, wc = ea * inv, eb * inv, ec * inv
            d1[:, cs] = wa * dmb
            sd4[t] = wb * dmb
            sd16[t] = wc * dmb
            sa = _segsum(dmb * o1[:, cs], ones)
            sb = _segsum(dmb * so4[t], ones)
            sc_ = _segsum(dmb * so16[t], ones)
            mean = wa * sa + wb * sb + wc * sc_
            g1[:, cs] = wa * (sa - mean)
            sg4[t] = wb * (sb - mean)
            sg16[t] = wc * (sc_ - mean)
        for src, dst, d in ((sd4, d4, 4), (sd16, d16, 16), (sg4, g4, 4), (sg16, g16, 16)):
            _to_dilated(src, dst, d)

    big = pl.BlockSpec((tl, 512), lambda i: (i, 0))
    dil = [big, _dil_spec(tl, 4, 512), _dil_spec(tl, 16, 512)]
    sd = jax.ShapeDtypeStruct
    shp = [sd((L, 512), F32), sd((L // 4, 4 * 512), F32), sd((L // 16, 16 * 512), F32)]
    return pl.pallas_call(
        body, name=name, grid=(L // tl,),
        in_specs=[pl.BlockSpec((tl, 1024), lambda i: (i, 0)), big, big,
                  pl.BlockSpec((1, 512), lambda i: (0, 0))] + dil + dil,
        out_specs=tuple([big, big] + dil + dil + [pl.BlockSpec((8, 512), lambda i: (0, 0))]),
        out_shape=tuple([shp[0], shp[0]] + shp + shp + [sd((8, 512), F32)]),
        scratch_shapes=[pltpu.VMEM((4, tl, 128), F32)] * 8, compiler_params=_cparams("arbitrary"),
    )(dm, oa, la, sink, *obs, *lbs)


def _shift_down(x, halo, k, first):
    rows = lax.broadcasted_iota(jnp.int32, (8, x.shape[1]), 0)
    out = pltpu.roll(x, k, axis=0)
    hrows = jnp.where(first, 0.0, pltpu.roll(halo, k, axis=0))
    top = jnp.where(rows < k, hrows, out[0:8, :])
    return jnp.concatenate([top, out[8:, :]], axis=0)


def _shift_up(x, nxt, k):
    tl = x.shape[0]
    rows = lax.broadcasted_iota(jnp.int32, (8, x.shape[1]), 0)
    out = pltpu.roll(x, tl - k, axis=0)
    bottom = jnp.where(rows >= 8 - k, pltpu.roll(nxt, 8 - k, axis=0), out[tl - 8:, :])
    return jnp.concatenate([out[:tl - 8, :], bottom], axis=0)


def _silu(x):
    return x * _sigmoid(x)


def _dsilu(x):
    s = _sigmoid(x)
    return s * (1.0 + x * (1.0 - s))


def ffn_act_fwd(ua, ub, cw, name):
    L, F = ua.shape
    tl = _rtile(L, 256)
    tc = _tile(F, 1408)
    hb = tl // 8

    def body(ua_ref, uah_ref, ub_ref, ubh_ref, wa_ref, wb_ref, o_ref):
        first = pl.program_id(1) == 0

        def conv(x_ref, h_ref, w_ref):
            x = x_ref[...]
            h = h_ref[...]
            return (w_ref[2:3, :] * x + w_ref[1:2, :] * _shift_down(x, h, 1, first)
                    + w_ref[0:1, :] * _shift_down(x, h, 2, first))

        a = conv(ua_ref, uah_ref, wa_ref)
        b = conv(ub_ref, ubh_ref, wb_ref)
        o_ref[...] = (_silu(a) * b).astype(BF16)

    main = pl.BlockSpec((tl, tc), lambda j, i: (i, j))
    halo = pl.BlockSpec((8, tc), lambda j, i: (jnp.maximum(i * hb - 1, 0), j))
    wa = pl.BlockSpec((3, tc), lambda j, i: (0, j))
    wb = pl.BlockSpec((3, tc), lambda j, i: (0, j + F // tc))
    return pl.pallas_call(
        body, name=name, grid=(F // tc, L // tl), in_specs=[main, halo, main, halo, wa, wb],
        out_specs=main, out_shape=jax.ShapeDtypeStruct((L, F), BF16),
        compiler_params=_cparams("parallel", "parallel"))(ua, ua, ub, ub, cw, cw)


def ffn_act_bwd(ua, ub, cw, dact, name, comm=None):
    L, F = ua.shape
    tl = _rtile(L, 256)
    tc = _tile(F, 1408)
    hb = tl // 8
    nrt = L // tl

    def body(ua_ref, uah_ref, ub_ref, ubh_ref, wa_ref, wb_ref, da_ref, dua_ref, dub_ref, sums_ref, ca_ref, cb_ref):
        i = pl.program_id(1)
        first = i == nrt - 1

        @pl.when(i == 0)
        def _():
            sums_ref[...] = jnp.zeros_like(sums_ref)
            ca_ref[...] = jnp.zeros_like(ca_ref)
            cb_ref[...] = jnp.zeros_like(cb_ref)

        def taps(x_ref, h_ref):
            x = x_ref[...]
            h = h_ref[...]
            return x, _shift_down(x, h, 1, first), _shift_down(x, h, 2, first)

        a0, a1, a2 = taps(ua_ref, uah_ref)
        b0, b1, b2 = taps(ub_ref, ubh_ref)
        a = wa_ref[2:3, :] * a0 + wa_ref[1:2, :] * a1 + wa_ref[0:1, :] * a2
        b = wb_ref[2:3, :] * b0 + wb_ref[1:2, :] * b1 + wb_ref[0:1, :] * b2
        dact_v = da_ref[...]
        dya = dact_v * b * _dsilu(a)
        dyb = dact_v * _silu(a)
        for (dy, w_ref, c_ref, d_ref, xs, base) in ((dya, wa_ref, ca_ref, dua_ref, (a2, a1, a0), 0),
                                                     (dyb, wb_ref, cb_ref, dub_ref, (b2, b1, b0), 24)):
            nxt = c_ref[...]
            d_ref[...] = (w_ref[2:3, :] * dy + w_ref[1:2, :] * _shift_up(dy, nxt, 1)
                          + w_ref[0:1, :] * _shift_up(dy, nxt, 2)).astype(BF16)
            c_ref[...] = dy[0:8, :]
            for j in range(3):
                sums_ref[base + 8 * j:base + 8 * j + 8, :] += _fold8(dy * xs[j])

    rev = lambda i: nrt - 1 - i
    main = pl.BlockSpec((tl, tc), lambda j, i: (rev(i), j))
    halo = pl.BlockSpec((8, tc), lambda j, i: (jnp.maximum(rev(i) * hb - 1, 0), j))
    wa = pl.BlockSpec((3, tc), lambda j, i: (0, j))
    wb = pl.BlockSpec((3, tc), lambda j, i: (0, j + F // tc))
    ob = jax.ShapeDtypeStruct((L, F), BF16)
    return _call(body, (ua, ua, ub, ub, cw, cw, dact), name=name, grid=(F // tc, nrt),
                 in_specs=[main, halo, main, halo, wa, wb, main],
                 out_specs=(main, main, pl.BlockSpec((48, tc), lambda j, i: (0, j))),
                 out_shape=(ob, ob, jax.ShapeDtypeStruct((48, F), F32)),
                 scratch_shapes=[pltpu.VMEM((8, tc), F32), pltpu.VMEM((8, tc), F32)],
                 sem=("parallel", "arbitrary"), comm=comm)


def attn_vectors(qna, kna, qnb, knb, sinks):
    ones = jnp.ones((128,), F32)
    wvec = jnp.concatenate([jnp.tile(qna, 8), jnp.tile(kna, 2), ones, jnp.tile(qnb, 8), jnp.tile(knb, 8),
                            jnp.tile(ones, 4)]).reshape(1, ATTN_IN)
    return wvec, jnp.repeat(sinks, HD).reshape(1, 512)


def _with_comm(result, comm):
    return result if comm is not None else (result, None)


def attention_block_fwd(h, w_in, wvec, sinkvec, w_out, tag, comms=None):
    L = h.shape[0]
    comms = comms or {}
    got = {}
    qkv = matmul([(h, w_in)], "nn", tag + "_qkv")
    X, X4, X16 = qknorm_fwd(qkv, wvec, tag + "_qknorm")
    (oa, la), got['swa'] = _with_comm(attn_fwd(X, 1, 0, 4, 5, True, 0, BLK - 1, tag + "_swa",
                                               comm=comms.get('swa')), comms.get('swa'))
    views = {1: (X, 6, 10, 14), 4: (X4, 0, 4, 8), 16: (X16, 0, 4, 8)}
    obs, lbs = [], []
    for window, d in B_BRANCHES:
        xd, qo, ko, vo = views[d]
        (o, l), got[d] = _with_comm(attn_fwd(xd, d, qo, ko, vo, False, 8, window // d,
                                             tag + f"_dil{d}", comm=comms.get(d)), comms.get(d))
        obs.append(o)
        lbs.append(l)
    m = attn_merge_fwd(oa, la, sinkvec, obs, lbs, tag + "_merge")
    y = matmul([(m, w_out)], "nn", tag + "_out")
    return y, (h, qkv, views, oa, la, obs, lbs, m), got


def attention_block_bwd(dy, res, w_in, wvec, sinkvec, w_out, tag, comms=None, send_w_out_on=None):
    h, qkv, views, oa, la, obs, lbs, m = res
    comms = dict(comms or {})
    got = {}
    g_w_out = matmul([(m, dy)], "tn", tag + "_dwout", out_dtype=BF16)
    if send_w_out_on is not None:
        comms[send_w_out_on] = ([g_w_out.reshape(N_DEV, D // N_DEV, D)], False)
    dm = matmul([(dy, w_out)], "nt", tag + "_dm")
    doa, dla, d1, d2, d3, g1, g2, g3, sinksums = attn_merge_bwd(dm, oa, la, sinkvec, obs, lbs, tag + "_dmerge")
    d_a, got['swa'] = _with_comm(attn_bwd(views[1][0], oa, la, doa, dla, 1, 0, 4, 5, True, 0, BLK - 1,
                                          tag + "_dswa", comm=comms.get('swa')), comms.get('swa'))
    d_b = []
    for (window, d), o, l, do, dl in zip(B_BRANCHES, obs, lbs, (d1, d2, d3), (g1, g2, g3)):
        xd, qo, ko, vo = views[d]
        dqkv_d, got[d] = _with_comm(attn_bwd(xd, o, l, do, dl, d, qo, ko, vo, False, 8, window // d,
                                             tag + f"_ddil{d}", comm=comms.get(d)), comms.get(d))
        d_b.append(dqkv_d)
    dqkv, wsums = qknorm_bwd(qkv, wvec, d_a, d_b, tag + "_dqknorm")
    g_w_in = matmul([(h, dqkv)], "tn", tag + "_dwin", out_dtype=BF16)
    dh = matmul([(dqkv, w_in)], "nt", tag + "_dh")
    ws = wsums.sum(axis=0)
    grads = dict(
        w_in=g_w_in, w_out=g_w_out,
        q_norm_a=ws[0:512].reshape(8, HD).sum(axis=0), k_norm_a=ws[512:640].reshape(2, HD).sum(axis=0),
        q_norm_b=ws[768:1280].reshape(8, HD).sum(axis=0), k_norm_b=ws[1280:1792].reshape(8, HD).sum(axis=0),
        sinks=sinksums.sum(axis=0).reshape(8, HD).sum(axis=1))
    return dh, grads, got


def ffn_block_fwd(h, w_up_a, w_up_b, cw, w_down, tag):
    ua = matmul([(h, w_up_a)], "nn", tag + "_upa")
    ub = matmul([(h, w_up_b)], "nn", tag + "_upb")
    act = ffn_act_fwd(ua, ub, cw, tag + "_act")
    f = matmul([(act, w_down)], "nn", tag + "_down")
    return f, (h, ua, ub, act)


def ffn_block_bwd(df, res, w_up_a, w_up_b, cw, w_down, tag, comm=None):
    h, ua, ub, act = res
    g_down = matmul([(act, df)], "tn", tag + "_dwdown", out_dtype=BF16)
    dact = matmul([(df, w_down)], "nt", tag + "_dact")
    (dua, dub, sums), got = _with_comm(ffn_act_bwd(ua, ub, cw, dact, tag + "_dactk", comm=comm), comm)
    g_up = jnp.concatenate([_cols_to_slabs(matmul([(h, dua)], "tn", tag + "_dwupa", out_dtype=BF16), N_DEV // 2),
                            _cols_to_slabs(matmul([(h, dub)], "tn", tag + "_dwupb", out_dtype=BF16), N_DEV // 2)],
                           axis=0)
    dh = matmul([(dua, w_up_a), (dub, w_up_b)], "nt", tag + "_dh")
    s = sums.reshape(2, 3, 8, D_FF).sum(axis=2)
    g_conv = jnp.concatenate([s[0], s[1]], axis=1)
    return dh, dict(w_up=g_up, conv=g_conv, w_down=g_down), got


def s5_params(lam_re, lam_im, log_dt, b_re, b_im, c_re, c_im):
    dt = jnp.exp(log_dt)[:, None]
    mag, ang = jnp.exp(lam_re * dt), lam_im * dt
    a_re, a_im = mag * jnp.cos(ang), mag * jnp.sin(ang)
    nr, ni = a_re - 1.0, a_im
    den = lam_re * lam_re + lam_im * lam_im
    f_re = (nr * lam_re + ni * lam_im) / den
    f_im = (ni * lam_re - nr * lam_im) / den
    eye = jnp.eye(16, dtype=F32)[:, None, :, None]
    bd = lambda b: (eye * jnp.transpose(b, (0, 2, 1))[:, :, None, :]).reshape(S5_W, S5_P)
    cd = lambda c: (eye * jnp.transpose(c, (0, 2, 1))[:, :, None, :]).reshape(S5_P, S5_W)
    flat = lambda t: t.reshape(1, S5_P)
    return flat(a_re), flat(a_im), flat(f_re), flat(f_im), bd(b_re), bd(b_im), cd(c_re), cd(c_im)


def _scan_tables(a_re, a_im, reverse):
    pows = [(a_re, a_im)]
    for _ in range(7):
        pr, pi = pows[-1]
        pows.append((pr * a_re - pi * a_im, pr * a_im + pi * a_re))
    order = list(range(7, -1, -1)) if reverse else list(range(8))
    z = jnp.zeros_like(a_re)
    rows = [pows[0][0], pows[0][1], pows[1][0], pows[1][1], pows[3][0], pows[3][1], z, z]
    rows += [pows[k][0] for k in order] + [pows[k][1] for k in order]
    return jnp.concatenate(rows, axis=0)


def _block_scan(er, ei, tab_ref, cr, ci, reverse):
    rows = lax.broadcasted_iota(jnp.int32, er.shape, 0)
    for idx, s in enumerate((1, 2, 4)):
        if reverse:
            sr, si, keep = pltpu.roll(er, 8 - s, axis=0), pltpu.roll(ei, 8 - s, axis=0), rows < 8 - s
        else:
            sr, si, keep = pltpu.roll(er, s, axis=0), pltpu.roll(ei, s, axis=0), rows >= s
        sr, si = jnp.where(keep, sr, 0.0), jnp.where(keep, si, 0.0)
        ar, ai = tab_ref[2 * idx:2 * idx + 1, :], tab_ref[2 * idx + 1:2 * idx + 2, :]
        er, ei = er + ar * sr - ai * si, ei + ar * si + ai * sr
    pr, pi_ = tab_ref[8:16, :], tab_ref[16:24, :]
    er, ei = er + pr * cr - pi_ * ci, ei + pr * ci + pi_ * cr
    return er, ei


def s5_scan_fwd(bu_re, bu_im, a_re, a_im, f_re, f_im, name):
    L, P = bu_re.shape
    tl = _rtile(L, 512)
    tab = _scan_tables(a_re, a_im, False)
    fvec = jnp.concatenate([f_re, f_im] + [jnp.zeros_like(f_re)] * 6, axis=0)

    def body(br_ref, bi_ref, tab_ref, f_ref, xr_ref, xi_ref, c_ref):
        @pl.when(pl.program_id(0) == 0)
        def _():
            c_ref[...] = jnp.zeros_like(c_ref)

        def blk(i, carry):
            cr, ci = carry
            rows = pl.ds(pl.multiple_of(i * 8, 8), 8)
            br, bi = br_ref[rows, :], bi_ref[rows, :]
            fr, fi = f_ref[0:1, :], f_ref[1:2, :]
            er, ei = _block_scan(fr * br - fi * bi, fr * bi + fi * br, tab_ref, cr, ci, False)
            xr_ref[rows, :] = er
            xi_ref[rows, :] = ei
            return er[7:8, :], ei[7:8, :]

        cr, ci = lax.fori_loop(0, tl // 8, blk, (c_ref[0:1, :], c_ref[1:2, :]))
        c_ref[0:1, :] = cr
        c_ref[1:2, :] = ci

    big = pl.BlockSpec((tl, P), lambda i: (i, 0))
    out = jax.ShapeDtypeStruct((L, P), F32)
    return pl.pallas_call(
        body, name=name, grid=(L // tl,),
        in_specs=[big, big, pl.BlockSpec((24, P), lambda i: (0, 0)), pl.BlockSpec((8, P), lambda i: (0, 0))],
        out_specs=(big, big), out_shape=(out, out), scratch_shapes=[pltpu.VMEM((8, P), F32)],
        compiler_params=_cparams("arbitrary"))(bu_re, bu_im, tab, fvec)


def s5_scan_bwd(dx_re, dx_im, x_re, x_im, bu_re, bu_im, a_re, a_im, f_re, f_im, name):
    L, P = dx_re.shape
    tl = _rtile(L, 256)
    nt = L // tl
    tab = _scan_tables(a_re, -a_im, True)
    fvec = jnp.concatenate([f_re, f_im] + [jnp.zeros_like(f_re)] * 6, axis=0)

    def body(gr_ref, gi_ref, xr_ref, xi_ref, br_ref, bi_ref, tab_ref, f_ref, dbr_ref, dbi_ref, s_ref, c_ref):
        @pl.when(pl.program_id(0) == 0)
        def _():
            c_ref[...] = jnp.zeros_like(c_ref)
            s_ref[...] = jnp.zeros_like(s_ref)

        def blk(k, carry):
            cr, ci = carry
            i = tl // 8 - 1 - k
            rows = pl.ds(pl.multiple_of(i * 8, 8), 8)
            er, ei = _block_scan(gr_ref[rows, :], gi_ref[rows, :], tab_ref, cr, ci, True)
            rid = lax.broadcasted_iota(jnp.int32, er.shape, 0)
            sr = jnp.where(rid == 7, cr, pltpu.roll(er, 7, axis=0))
            si = jnp.where(rid == 7, ci, pltpu.roll(ei, 7, axis=0))
            xr, xi = xr_ref[rows, :], xi_ref[rows, :]
            s_ref[0:8, :] += sr * xr + si * xi
            s_ref[8:16, :] += si * xr - sr * xi
            br, bi = br_ref[rows, :], bi_ref[rows, :]
            s_ref[16:24, :] += er * br + ei * bi
            s_ref[24:32, :] += ei * br - er * bi
            fr, fi = f_ref[0:1, :], f_ref[1:2, :]
            dbr_ref[rows, :] = fr * er + fi * ei
            dbi_ref[rows, :] = fr * ei - fi * er
            return er[0:1, :], ei[0:1, :]

        cr, ci = lax.fori_loop(0, tl // 8, blk, (c_ref[0:1, :], c_ref[1:2, :]))
        c_ref[0:1, :] = cr
        c_ref[1:2, :] = ci

    big = pl.BlockSpec((tl, P), lambda i: (nt - 1 - i, 0))
    out = jax.ShapeDtypeStruct((L, P), F32)
    return pl.pallas_call(
        body, name=name, grid=(nt,),
        in_specs=[big] * 6 + [pl.BlockSpec((24, P), lambda i: (0, 0)), pl.BlockSpec((8, P), lambda i: (0, 0))],
        out_specs=(big, big, pl.BlockSpec((32, P), lambda i: (0, 0))),
        out_shape=(out, out, jax.ShapeDtypeStruct((32, P), F32)), scratch_shapes=[pltpu.VMEM((8, P), F32)],
        compiler_params=_cparams("arbitrary"))(dx_re, dx_im, x_re, x_im, bu_re, bu_im, tab, fvec)


_GK, _GC = math.sqrt(2.0 / math.pi), 0.044715


def _gelu(y):
    return 0.5 * y * (1.0 + jnp.tanh(_GK * (y + _GC * y * y * y)))


def _dgelu(y):
    t = jnp.tanh(_GK * (y + _GC * y * y * y))
    return 0.5 * (1.0 + t) + 0.5 * y * (1.0 - t * t) * _GK * (1.0 + 3.0 * _GC * y * y)


def s5_out_fwd(x_re, x_im, u, cd_re, cd_im, dskip, glu_w, glu_b, name):
    L = u.shape[0]
    tl = _rtile(L, 512)

    def body(xr_ref, xi_ref, u_ref, cr_ref, ci_ref, d_ref, w_ref, b_ref, y_ref, o_ref):
        y = (jnp.dot(xr_ref[...].astype(BF16), cr_ref[...], preferred_element_type=F32)
             - jnp.dot(xi_ref[...].astype(BF16), ci_ref[...], preferred_element_type=F32)
             + d_ref[...] * u_ref[...])
        y_ref[...] = y
        g = _gelu(y)
        z = jnp.dot(g.astype(BF16), w_ref[...], preferred_element_type=F32) + b_ref[...]
        o_ref[...] = (g * _sigmoid(z)).astype(BF16)

    big = pl.BlockSpec((tl, S5_P), lambda i: (i, 0))
    sm = pl.BlockSpec((tl, S5_W), lambda i: (i, 0))
    full = lambda r, c: pl.BlockSpec((r, c), lambda i: (0, 0))
    return pl.pallas_call(
        body, name=name, grid=(L // tl,),
        in_specs=[big, big, sm, full(S5_P, S5_W), full(S5_P, S5_W), full(1, S5_W), full(S5_W, S5_W), full(1, S5_W)],
        out_specs=(sm, sm),
        out_shape=(jax.ShapeDtypeStruct((L, S5_W), F32), jax.ShapeDtypeStruct((L, S5_W), BF16)),
        compiler_params=_cparams("parallel"))(x_re, x_im, u, cd_re, cd_im, dskip, glu_w, glu_b)


def s5_out_bwd(dout, y, u, x_re, x_im, cd_re, cd_im, dskip, glu_w, glu_b, name, dout_col=0):
    L = u.shape[0]
    tl = _rtile(L, 256)
    nt_dims = (((1,), (1,)), ((), ()))
    tn_dims = (((0,), (0,)), ((), ()))

    def body(do_ref, y_ref, u_ref, xr_ref, xi_ref, cr_ref, ci_ref, d_ref, w_ref, b_ref,
             dxr_ref, dxi_ref, du_ref, dcr_ref, dci_ref, dw_ref, s_ref):
        @pl.when(pl.program_id(0) == 0)
        def _():
            dcr_ref[...] = jnp.zeros_like(dcr_ref)
            dci_ref[...] = jnp.zeros_like(dci_ref)
            dw_ref[...] = jnp.zeros_like(dw_ref)
            s_ref[...] = jnp.zeros_like(s_ref)

        yv, dov = y_ref[...], do_ref[...]
        g = _gelu(yv)
        gb = g.astype(BF16)
        sg = _sigmoid(jnp.dot(gb, w_ref[...], preferred_element_type=F32) + b_ref[...])
        dz = dov * g * sg * (1.0 - sg)
        dzb = dz.astype(BF16)
        dg = dov * sg + lax.dot_general(dzb, w_ref[...], nt_dims, preferred_element_type=F32)
        dw_ref[...] += lax.dot_general(gb, dzb, tn_dims, preferred_element_type=F32)
        dy = dg * _dgelu(yv)
        dyb = dy.astype(BF16)
        s_ref[0:8, :] += _fold8(dy * u_ref[...])
        s_ref[8:16, :] += _fold8(dz)
        du_ref[...] = dy * d_ref[...]
        dxr_ref[...] = lax.dot_general(dyb, cr_ref[...], nt_dims, preferred_element_type=F32)
        dxi_ref[...] = -lax.dot_general(dyb, ci_ref[...], nt_dims, preferred_element_type=F32)
        dcr_ref[...] += lax.dot_general(xr_ref[...].astype(BF16), dyb, tn_dims, preferred_element_type=F32)
        dci_ref[...] -= lax.dot_general(xi_ref[...].astype(BF16), dyb, tn_dims, preferred_element_type=F32)

    big = pl.BlockSpec((tl, S5_P), lambda i: (i, 0))
    sm = pl.BlockSpec((tl, S5_W), lambda i: (i, 0))
    full = lambda r, c: pl.BlockSpec((r, c), lambda i: (0, 0))
    sd = jax.ShapeDtypeStruct
    return pl.pallas_call(
        body, name=name, grid=(L // tl,),
        in_specs=[pl.BlockSpec((tl, S5_W), lambda i: (i, dout_col)), sm, sm, big, big, full(S5_P, S5_W),
                  full(S5_P, S5_W), full(1, S5_W), full(S5_W, S5_W), full(1, S5_W)],
        out_specs=(big, big, sm, full(S5_P, S5_W), full(S5_P, S5_W), full(S5_W, S5_W), full(16, S5_W)),
        out_shape=(sd((L, S5_P), F32), sd((L, S5_P), F32), sd((L, S5_W), F32), sd((S5_P, S5_W), F32),
                   sd((S5_P, S5_W), F32), sd((S5_W, S5_W), F32), sd((16, S5_W), F32)),
        compiler_params=_cparams("arbitrary"))(dout, y, u, x_re, x_im, cd_re, cd_im, dskip, glu_w, glu_b)


def s5_block_fwd(u, params, dskip, glu_w, glu_b, tag):
    a_re, a_im, f_re, f_im, bd_re, bd_im, cd_re, cd_im = params
    bu_re = matmul([(u, bd_re.astype(BF16))], "nn", tag + "_bure")
    bu_im = matmul([(u, bd_im.astype(BF16))], "nn", tag + "_buim")
    x_re, x_im = s5_scan_fwd(bu_re, bu_im, a_re, a_im, f_re, f_im, tag + "_scan")
    y, out = s5_out_fwd(x_re, x_im, u, cd_re.astype(BF16), cd_im.astype(BF16), dskip, glu_w, glu_b, tag + "_out")
    return out, (u, bu_re, bu_im, x_re, x_im, y)


def s5_block_bwd(dout, res, params, dskip, glu_w, glu_b, tag, dout_col=0):
    u, bu_re, bu_im, x_re, x_im, y = res
    a_re, a_im, f_re, f_im, bd_re, bd_im, cd_re, cd_im = params
    dxr, dxi, du, dcr, dci, dglu_w, sums = s5_out_bwd(dout, y, u, x_re, x_im, cd_re.astype(BF16), cd_im.astype(BF16),
                                                      dskip, glu_w, glu_b, tag + "_dout", dout_col=dout_col)
    dbr, dbi, acc = s5_scan_bwd(dxr, dxi, x_re, x_im, bu_re, bu_im, a_re, a_im, f_re, f_im, tag + "_dscan")
    du = du + matmul([(dbr, bd_re.astype(BF16)), (dbi, bd_im.astype(BF16))], "nt", tag + "_du")
    dbd_re = matmul([(u, dbr)], "tn", tag + "_dbdre")
    dbd_im = matmul([(u, dbi)], "tn", tag + "_dbdim")
    acc = acc.reshape(4, 8, S5_P).sum(axis=1)
    s = sums.reshape(2, 8, S5_W).sum(axis=1)
    cot = (acc[0:1], acc[1:2], acc[2:3], acc[3:4], dbd_re, dbd_im, dcr, dci)
    return du, cot, dict(dskip=s[0], glu_w=dglu_w, glu_b=s[1])


DN_Z0, DN_NT = 18, 18
REC_U0, REC_A0 = 3072, 3328


def rec_cols_permute(w):
    return jnp.concatenate([w[..., S5_W:REC_A0], w[..., :S5_W], w[..., REC_A0:]], axis=-1)


def rec_cols_restore(w):
    return jnp.concatenate([w[..., REC_U0:REC_A0], w[..., :REC_U0], w[..., REC_A0:]], axis=-1)


DN_W = DN_H * DN_DK


def _dn_conv4(taps, w_ref):
    xc = w_ref[3:4, :] * taps[0]
    for k in range(1, 4):
        xc = xc + w_ref[3 - k:4 - k, :] * taps[k]
    return xc


def dn_prep_fwd(rin, cw, name):
    L = rin.shape[0]
    tl = _rtile(L, 256)
    hb = tl // 8

    def body(x_ref, h_ref, w_ref, o_ref):
        j = pl.program_id(0)
        first = pl.program_id(1) == 0
        x, h = x_ref[...], h_ref[...]
        s = _silu(_dn_conv4([x] + [_shift_down(x, h, k, first) for k in range(1, 4)], w_ref))
        scale = jnp.where(j == 0, DN_DK ** -0.5, 1.0)
        for hd in _HEADS:
            cs = slice(hd * 128, (hd + 1) * 128)
            sh = s[:, cs]
            r = lax.rsqrt(jnp.sum(sh * sh, axis=-1, keepdims=True) + EPS)
            o_ref[:, cs] = jnp.where(j < 2, sh * r * scale, sh)

    main = pl.BlockSpec((tl, DN_W), lambda j, i: (i, j))
    halo = pl.BlockSpec((8, DN_W), lambda j, i: (jnp.maximum(i * hb - 1, 0), j))
    return pl.pallas_call(
        body, name=name, grid=(3, L // tl),
        in_specs=[main, halo, pl.BlockSpec((4, DN_W), lambda j, i: (0, j))],
        out_specs=main, out_shape=jax.ShapeDtypeStruct((L, 3 * DN_W), F32),
        compiler_params=_cparams("parallel", "parallel"))(rin, rin, cw)


def dn_prep_bwd(rin, cw, dout, name):
    L = rin.shape[0]
    tl = _rtile(L, 256)
    hb = tl // 8
    nrt = L // tl

    def body(x_ref, h_ref, w_ref, d_ref, dx_ref, s_ref, c_ref):
        j = pl.program_id(0)
        i = pl.program_id(1)
        first = i == nrt - 1

        @pl.when(i == 0)
        def _():
            s_ref[...] = jnp.zeros_like(s_ref)
            c_ref[...] = jnp.zeros_like(c_ref)

        x, h = x_ref[...], h_ref[...]
        taps = [x] + [_shift_down(x, h, k, first) for k in range(1, 4)]
        xc = _dn_conv4(taps, w_ref)
        s = _silu(xc)
        scale = jnp.where(j == 0, DN_DK ** -0.5, 1.0)
        pieces = []
        for hd in _HEADS:
            cs = slice(hd * 128, (hd + 1) * 128)
            sh, d = s[:, cs], d_ref[:, cs]
            r = lax.rsqrt(jnp.sum(sh * sh, axis=-1, keepdims=True) + EPS)
            n = sh * r
            dn = d * scale
            pieces.append(jnp.where(j < 2, r * (dn - n * jnp.sum(dn * n, axis=-1, keepdims=True)), d))
        dxc = jnp.concatenate(pieces, axis=1) * _dsilu(xc)
        nxt = c_ref[...]
        dx_ref[...] = _dn_conv4([dxc] + [_shift_up(dxc, nxt, k) for k in range(1, 4)], w_ref).astype(BF16)
        c_ref[...] = dxc[0:8, :]
        for k in range(4):
            s_ref[8 * (3 - k):8 * (3 - k) + 8, :] += _fold8(dxc * taps[k])

    rev = lambda i: nrt - 1 - i
    main = pl.BlockSpec((tl, DN_W), lambda j, i: (rev(i), j))
    halo = pl.BlockSpec((8, DN_W), lambda j, i: (jnp.maximum(rev(i) * hb - 1, 0), j))
    return pl.pallas_call(
        body, name=name, grid=(3, nrt),
        in_specs=[main, halo, pl.BlockSpec((4, DN_W), lambda j, i: (0, j)), main],
        out_specs=(main, pl.BlockSpec((32, DN_W), lambda j, i: (0, j))),
        out_shape=(jax.ShapeDtypeStruct((L, 3 * DN_W), BF16), jax.ShapeDtypeStruct((32, 3 * DN_W), F32)),
        scratch_shapes=[pltpu.VMEM((8, DN_W), F32)],
        compiler_params=_cparams("parallel", "arbitrary"))(rin, rin, cw, dout)


_HI = lax.Precision.HIGH
_NT = (((1,), (1,)), ((), ()))
_TN = (((0,), (0,)), ((), ()))
_HEADS = tuple(range(DN_H))


def _mm(a, b, dims=(((1,), (0,)), ((), ())), hi=False):
    if hi:
        return lax.dot_general(a, b, dims, precision=_HI, preferred_element_type=F32)
    return lax.dot_general(a.astype(BF16), b.astype(BF16), dims, preferred_element_type=F32)


def _dn_masks():
    ri = lax.broadcasted_iota(jnp.int32, (DN_C, DN_C), 0)
    ci = lax.broadcasted_iota(jnp.int32, (DN_C, DN_C), 1)
    return ri >= ci, ri > ci, (ri == ci).astype(F32)


def _dn_decay(gc, gr, causal):
    gam = [jnp.where(causal, jnp.exp(jnp.where(causal, gc[h] - gr[h], 0.0)), 0.0) for h in _HEADS]
    eg = [jnp.exp(gc[h]) for h in _HEADS]
    el = [jnp.exp(gc[h][DN_C - 1:DN_C, :] - gc[h]) for h in _HEADS]
    gl = [jnp.exp(gc[h][DN_C - 1:DN_C, :]) for h in _HEADS]
    return gam, eg, el, gl


def _dn_solve(k, v, beta, gam, eg, kk, strict, eye):
    nmat = [jnp.where(strict, beta[h] * kk[h] * gam[h], 0.0) for h in _HEADS]
    t = [eye - nmat[h] for h in _HEADS]
    m = [_mm(nmat[h], nmat[h], hi=True) for h in _HEADS]
    for step in range(5):
        t = [t[h] + _mm(t[h], m[h], hi=True) for h in _HEADS]
        if step < 4:
            m = [_mm(m[h], m[h], hi=True) for h in _HEADS]
    rhs = [jnp.concatenate([v[h] * beta[h], k[h] * (beta[h] * eg[h])], axis=1) for h in _HEADS]
    sol = [_mm(t[h], rhs[h], hi=True) for h in _HEADS]
    return t, sol


def dn_chunk_fwd(qkv, gcol, grow, bcol, name, comm=None):
    L = qkv.shape[0]
    C, W = DN_C, DN_H * DN_DK
    ncb = 8
    tl = ncb * C
    nchunks = L // C

    def body(q_ref, k_ref, v_ref, gc_ref, gr_ref, b_ref, o_ref, sh_ref, t_ref, sol_ref, s_ref):
        @pl.when(pl.program_id(0) == 0)
        def _():
            s_ref[...] = jnp.zeros_like(s_ref)

        causal, strict, eye = _dn_masks()

        def chunk(c, _):
            rows = pl.ds(pl.multiple_of(c * C, C), C)
            grow_c = gr_ref[c]
            hs = lambda h: slice(h * 128, (h + 1) * 128)
            q = [q_ref[rows, hs(h)] for h in _HEADS]
            k = [k_ref[rows, hs(h)] for h in _HEADS]
            v = [v_ref[rows, hs(h)] for h in _HEADS]
            gc = [gc_ref[rows, h:h + 1] for h in _HEADS]
            gr = [grow_c[h:h + 1, :] for h in _HEADS]
            beta = [b_ref[rows, h:h + 1] for h in _HEADS]
            gam, eg, el, gl = _dn_decay(gc, gr, causal)
            kk = [_mm(k[h], k[h], _NT) for h in _HEADS]
            t, sol = _dn_solve(k, v, beta, gam, eg, kk, strict, eye)
            qk = [_mm(q[h], k[h], _NT) * gam[h] for h in _HEADS]
            S = [s_ref[hs(h), :] for h in _HEADS]
            vn = [sol[h][:, :128] - _mm(sol[h][:, 128:], S[h]) for h in _HEADS]
            o = [_mm(q[h] * eg[h], S[h]) + _mm(qk[h], vn[h]) for h in _HEADS]
            Sn = [S[h] * gl[h] + _mm(k[h] * el[h], vn[h], _TN) for h in _HEADS]
            for h in _HEADS:
                sh_ref[c, hs(h), :] = S[h]
                s_ref[hs(h), :] = Sn[h]
                o_ref[rows, hs(h)] = o[h]
                t_ref[rows, h * C:(h + 1) * C] = t[h]
                sol_ref[rows, h * 256:(h + 1) * 256] = sol[h]
            return 0

        lax.fori_loop(0, ncb, chunk, 0)

    col = lambda b: pl.BlockSpec((tl, W), lambda i: (i, b))
    small = pl.BlockSpec((tl, 8), lambda i: (i, 0))
    rowblk = lambda w: pl.BlockSpec((tl, w), lambda i: (i, 0))
    sd = jax.ShapeDtypeStruct
    return _call(body, (qkv, qkv, qkv, gcol, grow, bcol), name=name, grid=(L // tl,),
                 in_specs=[col(0), col(1), col(2), small, pl.BlockSpec((ncb, 8, C), lambda i: (i, 0, 0)), small],
                 out_specs=(rowblk(W), pl.BlockSpec((ncb, W, 128), lambda i: (i, 0, 0)), rowblk(DN_H * C),
                            rowblk(DN_H * 256)),
                 out_shape=(sd((L, W), F32), sd((nchunks, W, 128), F32), sd((L, DN_H * C), F32),
                            sd((L, DN_H * 256), F32)),
                 scratch_shapes=[pltpu.VMEM((W, 128), F32)], sem=("arbitrary",), comm=comm)


def dn_chunk_bwd(qkv, gcol, grow, bcol, shist, thist, solhist, do, name, comm=None):
    L = qkv.shape[0]
    C, W = DN_C, DN_H * DN_DK
    ncb = 8
    tl = ncb * C
    nchunks = L // C
    nt = L // tl

    def body(q_ref, k_ref, v_ref, gc_ref, gr_ref, b_ref, sh_ref, t_ref, sol_ref, do_ref,
             dqkv_ref, dgc_ref, dgr_ref, db_ref, ds_ref):
        @pl.when(pl.program_id(0) == 0)
        def _():
            ds_ref[...] = jnp.zeros_like(ds_ref)

        lane8 = lax.broadcasted_iota(jnp.int32, (C, 8), 1)
        sub8 = lax.broadcasted_iota(jnp.int32, (8, C), 0)
        rowid = lax.broadcasted_iota(jnp.int32, (C, 1), 0)
        causal, strict, _ = _dn_masks()
        rsum = lambda a: jnp.sum(a, axis=1, keepdims=True)

        def chunk(cc, _):
            c = ncb - 1 - cc
            rows = pl.ds(pl.multiple_of(c * C, C), C)
            grow_c = gr_ref[c]
            hs = lambda h: slice(h * 128, (h + 1) * 128)
            q = [q_ref[rows, hs(h)] for h in _HEADS]
            k = [k_ref[rows, hs(h)] for h in _HEADS]
            v = [v_ref[rows, hs(h)] for h in _HEADS]
            gc = [gc_ref[rows, h:h + 1] for h in _HEADS]
            gr = [grow_c[h:h + 1, :] for h in _HEADS]
            beta = [b_ref[rows, h:h + 1] for h in _HEADS]
            t = [t_ref[rows, h * C:(h + 1) * C] for h in _HEADS]
            sol = [sol_ref[rows, h * 256:(h + 1) * 256] for h in _HEADS]
            S = [sh_ref[c, hs(h), :] for h in _HEADS]
            dS = [ds_ref[hs(h), :] for h in _HEADS]
            dov = [do_ref[rows, hs(h)] for h in _HEADS]
            gam, eg, el, gl = _dn_decay(gc, gr, causal)
            kk = [_mm(k[h], k[h], _NT) for h in _HEADS]
            qk_raw = [_mm(q[h], k[h], _NT) for h in _HEADS]
            w = [sol[h][:, 128:] for h in _HEADS]
            kd = [k[h] * el[h] for h in _HEADS]
            vn = [sol[h][:, :128] - _mm(w[h], S[h]) for h in _HEADS]
            dvn = [_mm(qk_raw[h] * gam[h], dov[h], _TN) + _mm(kd[h], dS[h]) for h in _HEADS]
            dqd = [_mm(dov[h], S[h], _NT) for h in _HEADS]
            dqk = [jnp.where(causal, _mm(dov[h], vn[h], _NT), 0.0) for h in _HEADS]
            dkd = [_mm(vn[h], dS[h], _NT) for h in _HEADS]
            dgl = [jnp.sum(rsum(dS[h] * S[h]), axis=0, keepdims=True) for h in _HEADS]
            dw = [-_mm(dvn[h], S[h], _NT) for h in _HEADS]
            dSn = [dS[h] * gl[h] + _mm(q[h] * eg[h], dov[h], _TN) - _mm(w[h], dvn[h], _TN) for h in _HEADS]
            drhs = [_mm(t[h], jnp.concatenate([dvn[h], dw[h]], axis=1), _TN, hi=True) for h in _HEADS]
            dn = [jnp.where(strict, -_mm(drhs[h], sol[h], _NT, hi=True), 0.0) for h in _HEADS]
            dgc_all = jnp.zeros((C, 8), F32)
            db_all = jnp.zeros((C, 8), F32)
            dgr_all = jnp.zeros((8, C), F32)
            for h in _HEADS:
                drv, drk = drhs[h][:, :128], drhs[h][:, 128:]
                t2 = rsum(drk * k[h])
                x = dn[h] * gam[h]
                dbeta = rsum(drv * v[h]) + t2 * eg[h] + rsum(x * kk[h])
                dkk = x * beta[h]
                draw = dqk[h] * gam[h]
                mm_ = (dn[h] * beta[h] * kk[h] + dqk[h] * qk_raw[h]) * gam[h]
                deg = t2 * beta[h] + rsum(dqd[h] * q[h])
                r_ = rsum(dkd[h] * k[h]) * el[h]
                dglast = jnp.sum(r_, axis=0, keepdims=True) + dgl[h] * gl[h]
                dgc = rsum(mm_) + deg * eg[h] - r_ + jnp.where(rowid == C - 1, dglast, 0.0)
                dgr = -jnp.sum(mm_, axis=0, keepdims=True)
                dqkv_ref[rows, hs(h)] = _mm(draw, k[h]) + dqd[h] * eg[h]
                dqkv_ref[rows, hs(DN_H + h)] = (drk * (beta[h] * eg[h]) + _mm(dkk, k[h]) + _mm(dkk, k[h], _TN)
                                                + _mm(draw, q[h], _TN) + dkd[h] * el[h])
                dqkv_ref[rows, hs(2 * DN_H + h)] = drv * beta[h]
                ds_ref[hs(h), :] = dSn[h]
                dgc_all = dgc_all + jnp.where(lane8 == h, dgc, 0.0)
                db_all = db_all + jnp.where(lane8 == h, dbeta, 0.0)
                dgr_all = dgr_all + jnp.where(sub8 == h, dgr, 0.0)
            dgc_ref[rows, :] = dgc_all
            db_ref[rows, :] = db_all
            dgr_ref[c] = dgr_all
            return 0

        lax.fori_loop(0, ncb, chunk, 0)

    rev = lambda i: nt - 1 - i
    col = lambda b: pl.BlockSpec((tl, W), lambda i: (rev(i), b))
    rowblk = lambda w: pl.BlockSpec((tl, w), lambda i: (rev(i), 0))
    small = pl.BlockSpec((tl, 8), lambda i: (rev(i), 0))
    g3 = pl.BlockSpec((ncb, 8, C), lambda i: (rev(i), 0, 0))
    sd = jax.ShapeDtypeStruct
    return _call(body, (qkv, qkv, qkv, gcol, grow, bcol, shist, thist, solhist, do), name=name, grid=(nt,),
                 in_specs=[col(0), col(1), col(2), small, g3, small,
                           pl.BlockSpec((ncb, W, 128), lambda i: (rev(i), 0, 0)), rowblk(DN_H * C),
                           rowblk(DN_H * 256), col(0)],
                 out_specs=(rowblk(3 * W), small, g3, small),
                 out_shape=(sd((L, 3 * W), F32), sd((L, 8), F32), sd((nchunks, 8, C), F32), sd((L, 8), F32)),
                 scratch_shapes=[pltpu.VMEM((W, 128), F32)], sem=("arbitrary",), comm=comm)


def dn_out_fwd(o, rin, nw, name):
    L = o.shape[0]
    tl = _rtile(L, 256)

    def body(o_ref, z_ref, w_ref, y_ref):
        for hd in _HEADS:
            cs = slice(hd * 128, (hd + 1) * 128)
            ov = o_ref[:, cs]
            r = lax.rsqrt(jnp.mean(ov * ov, axis=-1, keepdims=True) + EPS)
            y_ref[:, cs] = (ov * r * w_ref[...] * _silu(z_ref[:, cs])).astype(BF16)

    return pl.pallas_call(
        body, name=name, grid=(L // tl,),
        in_specs=[pl.BlockSpec((tl, DN_W), lambda i: (i, 0)), pl.BlockSpec((tl, DN_W), lambda i: (i, 3)),
                  pl.BlockSpec((1, 128), lambda i: (0, 0))],
        out_specs=pl.BlockSpec((tl, DN_W), lambda i: (i, 0)), out_shape=jax.ShapeDtypeStruct((L, DN_W), BF16),
        compiler_params=_cparams("parallel"))(o, rin, nw)


def dn_out_bwd(dycat, o, rin, nw, name):
    L = o.shape[0]
    tl = _rtile(L, 256)

    def body(dy_ref, o_ref, z_ref, w_ref, do_ref, dz_ref, s_ref):
        @pl.when(pl.program_id(0) == 0)
        def _():
            s_ref[...] = jnp.zeros_like(s_ref)

        for hd in _HEADS:
            cs = slice(hd * 128, (hd + 1) * 128)
            ov, zv, d = o_ref[:, cs], z_ref[:, cs], dy_ref[:, cs]
            r = lax.rsqrt(jnp.mean(ov * ov, axis=-1, keepdims=True) + EPS)
            n = ov * r
            dnw = d * _silu(zv)
            dz_ref[:, cs] = (d * n * w_ref[...] * _dsilu(zv)).astype(BF16)
            dn = dnw * w_ref[...]
            do_ref[:, cs] = r * (dn - n * jnp.mean(dn * n, axis=-1, keepdims=True))
            s_ref[:, cs] += _fold8(dnw * n)

    own = pl.BlockSpec((tl, DN_W), lambda i: (i, 0))
    sd = jax.ShapeDtypeStruct
    return pl.pallas_call(
        body, name=name, grid=(L // tl,),
        in_specs=[own, own, pl.BlockSpec((tl, DN_W), lambda i: (i, 3)), pl.BlockSpec((1, 128), lambda i: (0, 0))],
        out_specs=(own, own, pl.BlockSpec((8, DN_W), lambda i: (0, 0))),
        out_shape=(sd((L, DN_W), F32), sd((L, DN_W), BF16), sd((8, DN_W), F32)),
        compiler_params=_cparams("arbitrary"))(dycat, o, rin, nw)


def dn_gates(a, beta_raw, a_log, dt_bias):
    L = a.shape[0]
    beta = jax.nn.sigmoid(beta_raw)
    g = -jnp.exp(a_log) * jax.nn.softplus(a + dt_bias)
    G = jnp.cumsum(g.reshape(L // DN_C, DN_C, DN_H), axis=1)
    pad = lambda t: jnp.pad(t, ((0, 0), (0, 8 - DN_H)))
    gcol = pad(G.reshape(L, DN_H))
    grow = jnp.pad(jnp.transpose(G, (0, 2, 1)), ((0, 0), (0, 8 - DN_H), (0, 0)))
    return gcol, grow, pad(beta)


def dn_block_fwd(rin, cw, a_log, dt_bias, out_norm, tag, comm=None):
    gates, gates_vjp = jax.vjp(dn_gates, rin[:, REC_A0:REC_A0 + DN_H], rin[:, REC_A0 + DN_H:REC_IN], a_log, dt_bias)
    qkv = dn_prep_fwd(rin, cw, tag + "_prep")
    (o, shist, thist, solhist), got = _with_comm(dn_chunk_fwd(qkv, *gates, tag + "_chunk", comm=comm), comm)
    yd = dn_out_fwd(o, rin, out_norm.reshape(1, 128), tag + "_onorm")
    return yd, (qkv, gates, gates_vjp, o, shist, thist, solhist), got


def dn_block_bwd(dyd, res, rin, cw, out_norm, tag, comm=None):
    qkv, gates, gates_vjp, o, shist, thist, solhist = res
    do, dz, nsum = dn_out_bwd(dyd, o, rin, out_norm.reshape(1, 128), tag + "_donorm")
    (dqkv, dgc, dgr, db), got = _with_comm(dn_chunk_bwd(qkv, *gates, shist, thist, solhist, do, tag + "_dchunk",
                                                        comm=comm), comm)
    da, dbraw, g_alog, g_dtb = gates_vjp((dgc, dgr, db))
    dx, csum = dn_prep_bwd(rin, cw, dqkv, tag + "_dprep")
    grads = dict(conv=csum.reshape(4, 8, DN_NT * 128).sum(axis=1), a_log=g_alog, dt_bias=g_dtb,
                 out_norm=nsum.sum(axis=0).reshape(DN_H, 128).sum(axis=0))
    return dx, dz, da, dbraw, grads, got


_HBM = pl.BlockSpec(memory_space=pltpu.HBM)


def _mesh_pos():
    xi, yi, ci = lax.axis_index("x"), lax.axis_index("y"), lax.axis_index("c")
    return xi, yi, ci, 4 * xi + 2 * yi + ci


def _peer(xi, yi, ci, k):
    px = 1 - xi if (k >> 2) & 1 else xi
    py = 1 - yi if (k >> 1) & 1 else yi
    pc = 1 - ci if k & 1 else ci
    return (px, py, pc), 4 * px + 2 * py + pc


def _exchange(xs, gather, name):
    n = len(xs)

    def body(*refs):
        copies = _comm_copies(refs[:n], refs[n:2 * n], *refs[2 * n:], gather)
        for cp in copies:
            cp.start()
        for cp in copies:
            cp.wait()

    return pl.pallas_call(
        body, name=name, in_specs=[_HBM] * n, out_specs=tuple([_HBM] * n),
        out_shape=_comm_out_shapes(xs), scratch_shapes=_comm_sems(n))(*xs)


def _comm_out_shapes(xs):
    return tuple(jax.ShapeDtypeStruct((N_DEV,) + x.shape[-2:], x.dtype) for x in xs)


def _comm_sems(n):
    return [pltpu.SemaphoreType.DMA((n * (N_DEV - 1),)), pltpu.SemaphoreType.DMA((n * (N_DEV - 1),)),
            pltpu.SemaphoreType.DMA((n,))]


def _comm_copies(x_refs, o_refs, send_sems, recv_sems, lsems, gather):
    xi, yi, ci, me = _mesh_pos()
    copies = []
    for t in range(len(x_refs)):
        src_of = (lambda lin, t=t: x_refs[t]) if gather else (lambda lin, t=t: x_refs[t].at[lin])
        copies.append(pltpu.make_async_copy(src_of(me), o_refs[t].at[me], lsems.at[t]))
        for k in range(1, N_DEV):
            peer, lin = _peer(xi, yi, ci, k)
            s = t * (N_DEV - 1) + k - 1
            copies.append(pltpu.make_async_remote_copy(
                src_ref=src_of(lin), dst_ref=o_refs[t].at[me], send_sem=send_sems.at[s],
                recv_sem=recv_sems.at[s], device_id=peer, device_id_type=pl.DeviceIdType.MESH))
    return copies


def _call(body, args, *, name, grid, in_specs, out_specs, out_shape, scratch_shapes=(), sem, comm=None):
    if comm is None:
        return pl.pallas_call(body, name=name, grid=grid, in_specs=in_specs, out_specs=out_specs,
                              out_shape=out_shape, scratch_shapes=list(scratch_shapes),
                              compiler_params=_cparams(*sem))(*args)
    xs, gather = comm
    n = len(xs)
    single = not isinstance(out_shape, (tuple, list))
    outs_shape = (out_shape,) if single else tuple(out_shape)
    outs_specs = (out_specs,) if single else tuple(out_specs)
    n_in, n_out, n_scr = len(in_specs), len(outs_shape), len(scratch_shapes)

    def body2(*refs):
        ins, cx = refs[:n_in], refs[n_in:n_in + n]
        outs = refs[n_in + n:n_in + n + n_out]
        co = refs[n_in + n + n_out:n_in + 2 * n + n_out]
        scr = refs[n_in + 2 * n + n_out:n_in + 2 * n + n_out + n_scr]
        sems = refs[n_in + 2 * n + n_out + n_scr:]
        first = functools.reduce(jnp.logical_and, [pl.program_id(a) == 0 for a in range(len(grid))])
        last = functools.reduce(jnp.logical_and, [pl.program_id(a) == grid[a] - 1 for a in range(len(grid))])

        @pl.when(first)
        def _():
            for cp in _comm_copies(cx, co, *sems, gather):
                cp.start()

        body(*ins, *outs, *scr)

        @pl.when(last)
        def _():
            for cp in _comm_copies(cx, co, *sems, gather):
                cp.wait()

    res = pl.pallas_call(
        body2, name=name, grid=grid, in_specs=list(in_specs) + [_HBM] * n,
        out_specs=outs_specs + tuple([_HBM] * n), out_shape=outs_shape + _comm_out_shapes(xs),
        scratch_shapes=list(scratch_shapes) + _comm_sems(n),
        compiler_params=_cparams(*(["arbitrary"] * len(grid))))(*args, *xs)
    main = res[0] if single else tuple(res[:n_out])
    return main, list(res[n_out:])


def all_gather(x, name):
    return _exchange([x], True, name)[0]


def all_gather_many(xs, name):
    return _exchange(xs, True, name)


def all_to_all_many(xs, name):
    return _exchange(xs, False, name)


def reduce_adamw(gsrc, w, m, v, name, comm=None):
    parts = list(gsrc) if isinstance(gsrc, (list, tuple)) else [gsrc]
    S, R0, C = parts[0].shape
    R = R0 * len(parts)
    tr = _rtile(R0, max(16, min(256, (4 << 20) // (S * C * 4) // 16 * 16)), 16 if R0 % 16 == 0 else 8)
    n0 = R0 // tr
    c1 = 1.0 - ADAM_B1 ** ADAM_STEP
    c2 = 1.0 - ADAM_B2 ** ADAM_STEP

    def body(*refs):
        g_refs = refs[:len(parts)]
        w_ref, m_ref, v_ref, go_ref, d_ref, mo_ref, vo_ref = refs[len(parts):]
        for p, g_ref in enumerate(g_refs):
            @pl.when(pl.program_id(0) // n0 == p)
            def _(g_ref=g_ref):
                acc = g_ref[0].astype(F32)
                for s in range(1, S):
                    acc = acc + g_ref[s].astype(F32)
                go_ref[...] = acc
        g = go_ref[...]
        mn = ADAM_B1 * m_ref[...] + (1.0 - ADAM_B1) * g
        vn = ADAM_B2 * v_ref[...] + (1.0 - ADAM_B2) * (g * g)
        mo_ref[...] = mn
        vo_ref[...] = vn
        d_ref[...] = -ADAM_LR * ((mn / c1) / (jnp.sqrt(vn / c2) + ADAM_EPS) + ADAM_WD * w_ref[...])

    big = pl.BlockSpec((tr, C), lambda i: (i, 0))
    o = jax.ShapeDtypeStruct((R, C), F32)
    part_spec = lambda p: pl.BlockSpec((S, tr, C), lambda i: (0, jnp.clip(i - p * n0, 0, n0 - 1), 0))
    return _call(body, (*parts, w, m, v), name=name, grid=(R // tr,),
                 in_specs=[part_spec(p) for p in range(len(parts))] + [big, big, big],
                 out_specs=(big, big, big, big), out_shape=(o, o, o, o), sem=("parallel",), comm=comm)


def _to_slabs(g, ax):
    shp = g.shape
    g = g.reshape(shp[:ax] + (N_DEV, shp[ax] // N_DEV) + shp[ax + 1:])
    return jnp.moveaxis(g, ax, 0).reshape(N_DEV, -1)


def _from_slabs(s, ax, shp):
    s = s.reshape((N_DEV,) + shp[:ax] + (shp[ax] // N_DEV,) + shp[ax + 1:])
    return jnp.moveaxis(s, 0, ax).reshape(shp)


def _pack_rows(flat, width, row_mult):
    n = flat.shape[-1]
    per = width * row_mult
    tot = -(-n // per) * per
    flat = jnp.pad(flat, [(0, 0)] * (flat.ndim - 1) + [(0, tot - n)])
    return flat.reshape(flat.shape[:-1] + (tot // width, width))


def _offsets(sizes):
    offs, o = [], 0
    for s in sizes:
        offs.append(o)
        o += s
    return offs


WEIGHTS = ['ada_w', 'ada_b', 'norm_mix', 'norm_ffn', 'attn_w_in', 'attn_q_norm_a', 'attn_k_norm_a', 'attn_q_norm_b',
           'attn_k_norm_b', 'attn_sinks', 'attn_w_out', 'rec_w_in', 's5_lambda_re', 's5_lambda_im', 's5_log_dt',
           's5_b_re', 's5_b_im', 's5_c_re', 's5_c_im', 's5_d', 's5_glu_w', 's5_glu_b', 'dn_conv', 'dn_a_log',
           'dn_dt_bias', 'dn_out_norm', 'rec_w_out', 'ffn_w_up', 'ffn_conv', 'ffn_w_down']
BIG = [('attn_w_in', (D, ATTN_IN // N_DEV)), ('attn_w_out', (D // N_DEV, D)), ('rec_w_in', (D // N_DEV, REC_PAD)),
       ('s5_glu_w', (S5_W // N_DEV, S5_W)), ('rec_w_out', (D // N_DEV, D)), ('ffn_w_up', (2 * D, 2 * D_FF // N_DEV)),
       ('ffn_w_down', (2 * D_FF // N_DEV, D))]


def _shard2d(name, t):
    if name == 'rec_w_in':
        return jnp.pad(t[0], ((0, 0), (0, REC_PAD - REC_IN)))
    return t.reshape((-1, t.shape[-1]))


def _cols_to_slabs(g, k=N_DEV):
    r, n = g.shape
    return jnp.transpose(g.reshape(r, k, n // k), (1, 0, 2))


def _slabs_to_cols(s):
    k, r, c_ = s.shape
    return jnp.transpose(s, (1, 0, 2)).reshape(r, k * c_)
SMALL_SHARDED = [('s5_d', 1, (1, S5_W)), ('s5_glu_b', 1, (1, S5_W)), ('dn_conv', 2, (1, 4, 2304)),
                 ('ffn_conv', 2, (2, 3, 2 * D_FF))]
REPLICATED = [('ada_b', (2, 6 * D)), ('norm_mix', (2, D)), ('norm_ffn', (2, D)), ('attn_q_norm_a', (1, HD)),
              ('attn_k_norm_a', (1, HD)), ('attn_q_norm_b', (1, HD)), ('attn_k_norm_b', (1, HD)),
              ('attn_sinks', (1, 8)), ('s5_lambda_re', (1, 16, 64)), ('s5_lambda_im', (1, 16, 64)),
              ('s5_log_dt', (1, 16)), ('s5_b_re', (1, 16, 64, 16)), ('s5_b_im', (1, 16, 64, 16)),
              ('s5_c_re', (1, 16, 16, 64)), ('s5_c_im', (1, 16, 16, 64)), ('dn_a_log', (1, DN_H)),
              ('dn_dt_bias', (1, DN_H)), ('dn_out_norm', (1, 128))]


def _numel(shp):
    return int(np.prod(shp))


def kernel(x, c, ada_w, ada_b, norm_mix, norm_ffn, attn_w_in, attn_q_norm_a, attn_k_norm_a, attn_q_norm_b, attn_k_norm_b, attn_sinks, attn_w_out, rec_w_in, s5_lambda_re, s5_lambda_im, s5_log_dt, s5_b_re, s5_b_im, s5_c_re, s5_c_im, s5_d, s5_glu_w, s5_glu_b, dn_conv, dn_a_log, dn_dt_bias, dn_out_norm, rec_w_out, ffn_w_up, ffn_conv, ffn_w_down, loss_target, m_ada_w, m_ada_b, m_norm_mix, m_norm_ffn, m_attn_w_in, m_attn_q_norm_a, m_attn_k_norm_a, m_attn_q_norm_b, m_attn_k_norm_b, m_attn_sinks, m_attn_w_out, m_rec_w_in, m_s5_lambda_re, m_s5_lambda_im, m_s5_log_dt, m_s5_b_re, m_s5_b_im, m_s5_c_re, m_s5_c_im, m_s5_d, m_s5_glu_w, m_s5_glu_b, m_dn_conv, m_dn_a_log, m_dn_dt_bias, m_dn_out_norm, m_rec_w_out, m_ffn_w_up, m_ffn_conv, m_ffn_w_down, v_ada_w, v_ada_b, v_norm_mix, v_norm_ffn, v_attn_w_in, v_attn_q_norm_a, v_attn_k_norm_a, v_attn_q_norm_b, v_attn_k_norm_b, v_attn_sinks, v_attn_w_out, v_rec_w_in, v_s5_lambda_re, v_s5_lambda_im, v_s5_log_dt, v_s5_b_re, v_s5_b_im, v_s5_c_re, v_s5_c_im, v_s5_d, v_s5_glu_w, v_s5_glu_b, v_dn_conv, v_dn_a_log, v_dn_dt_bias, v_dn_out_norm, v_rec_w_out, v_ffn_w_up, v_ffn_conv, v_ffn_w_down):
    loc = locals()
    W = {n: loc[n] for n in WEIGHTS}
    M = {n: loc["m_" + n] for n in WEIGHTS}
    V = {n: loc["v_" + n] for n in WEIGHTS}
    _, _, _, me = _mesh_pos()
    L = x.shape[1]
    x0, tgt = x[0], loss_target[0]

    small_in = jnp.concatenate([c.reshape(-1)] + [W[n].reshape(-1) for n, _, _ in SMALL_SHARDED])
    si, att_in_all, att_out_all = all_gather_many(
        [_pack_rows(small_in, 1024, 8), attn_w_in[0].astype(BF16), attn_w_out[0].astype(BF16)], "gather_first")
    si = si.reshape(N_DEV, -1)
    c_all = si[:, :D]
    off = D
    small_full = {}
    for n, ax, shp in SMALL_SHARDED:
        k = _numel(shp) // N_DEV
        small_full[n] = _from_slabs(si[:, off:off + k], ax, shp)
        off += k

    cond_all = jax.nn.silu(c_all)
    modp = jnp.concatenate([matmul([(cond_all, ada_w[l].astype(BF16))], "nn", f"ada{l}") for l in range(2)], axis=0)
    modp_all = all_gather(modp, "gather_mod")
    mods = []
    for l in range(2):
        row = lax.dynamic_index_in_dim(modp_all, l * N_DEV + me, axis=1, keepdims=False)
        mod = row.reshape(1, 6 * D) + ada_b[l].reshape(1, 6 * D)
        mods.append([mod[:, i * D:(i + 1) * D] for i in range(6)])

    w_att_in, w_att_out = _slabs_to_cols(att_in_all), att_out_all.reshape(D, D)
    bf = lambda t: t.astype(BF16)
    ffn_shards = [[bf(ffn_w_up[l]), bf(ffn_w_down[l])] for l in range(2)]
    rec_shards = [bf(_shard2d('rec_w_in', rec_w_in)), bf(s5_glu_w[0]), bf(rec_w_out[0])]
    ffn_cw = [small_full['ffn_conv'][l] for l in range(2)]
    dn_cw = small_full['dn_conv'][0]
    s5_dskip, glu_b = small_full['s5_d'], small_full['s5_glu_b']
    row = lambda t: t.reshape(1, -1)

    sh1, sc1, g1, sh2, sc2, g2 = mods[0]
    h1 = gate_norm_fwd(x0, None, None, row(norm_mix[0]), sh1, sc1, "l0_norm1")
    wvec, sinkvec = attn_vectors(attn_q_norm_a[0], attn_k_norm_a[0], attn_q_norm_b[0], attn_k_norm_b[0], attn_sinks[0])
    y0, res_att, got = attention_block_fwd(
        h1, w_att_in, wvec, sinkvec, w_att_out, "att",
        comms={'swa': (ffn_shards[0][:1], True), 1: (ffn_shards[0][1:], True), 4: (rec_shards, True)})
    split_up = lambda up_all: (_slabs_to_cols(up_all[:4]), _slabs_to_cols(up_all[4:]))
    w_up, w_down = [split_up(got['swa'][0])], [got[1][0].reshape(D_FF, D)]
    w_rec_in = rec_cols_permute(got[4][0].reshape(D, REC_PAD))
    glu_w, w_rec_out = got[4][1].reshape(S5_W, S5_W), got[4][2].reshape(D, D)
    w_rec_out = jnp.concatenate([w_rec_out[S5_W:], w_rec_out[:S5_W]], axis=0)
    x1, h2 = gate_norm_fwd(x0, y0, g1, row(norm_ffn[0]), sh2, sc2, "l0_norm2")
    f0, res_f0 = ffn_block_fwd(h2, w_up[0][0], w_up[0][1], ffn_cw[0], w_down[0], "ffn0")
    t1, tc1, tg1, t2, tc2, tg2 = mods[1]
    x2, h3 = gate_norm_fwd(x1, f0, g2, row(norm_mix[1]), t1, tc1, "l1_norm1")
    rin = matmul([(h3, w_rec_in)], "nn", "rec_in")
    s5p, s5p_vjp = jax.vjp(s5_params, s5_lambda_re[0], s5_lambda_im[0], s5_log_dt[0], s5_b_re[0], s5_b_im[0],
                           s5_c_re[0], s5_c_im[0])
    u = rin[:, REC_U0:REC_A0]
    yc, res_s5 = s5_block_fwd(u, s5p, s5_dskip, glu_w, glu_b, "s5")
    yd, res_dn, got_ffn1 = dn_block_fwd(rin, dn_cw, dn_a_log[0], dn_dt_bias[0], dn_out_norm[0], "dn",
                                        comm=(ffn_shards[1], True))
    w_up.append(split_up(got_ffn1[0]))
    w_down.append(got_ffn1[1].reshape(D_FF, D))
    ycat = jnp.concatenate([yd, yc], axis=1)
    y1 = matmul([(ycat, w_rec_out)], "nn", "rec_out")
    x3, h4 = gate_norm_fwd(x2, y1, tg1, row(norm_ffn[1]), t2, tc2, "l1_norm2")
    f1, res_f1 = ffn_block_fwd(h4, w_up[1][0], w_up[1][1], ffn_cw[1], w_down[1], "ffn1")
    dx4, df1, lsum = final_loss(x3, f1, tg2, tgt, "loss")

    G = {}
    d_tg2 = lsum[8:16].sum(axis=0)
    dh4, gf1, _ = ffn_block_bwd(df1, res_f1, w_up[1][0], w_up[1][1], ffn_cw[1], w_down[1], "ffn1")
    ffn_slabs = lambda g: [g['w_up'], g['w_down'].reshape(N_DEV, D_FF // N_DEV, D)]
    dx3, dy1, s = gate_norm_bwd(x3, y1, tg1, row(norm_ffn[1]), tc2, dx4, dh4, "l1_dnorm2")
    s = s.reshape(4, 8, D).sum(axis=1)
    d_tg1, d_nffn1, d_t2, d_tc2 = s[0], s[1] * (1.0 + tc2[0]), s[2], s[1] * norm_ffn[1]
    g_rec_out = matmul([(ycat, dy1)], "tn", "rec_out_dw", out_dtype=BF16)
    g_rec_out = jnp.concatenate([g_rec_out[DN_W:], g_rec_out[:DN_W]], axis=0).reshape(N_DEV, D // N_DEV, D)
    dycat = matmul([(dy1, w_rec_out)], "nt", "rec_out_dx")
    du, s5cot, gs5 = s5_block_bwd(dycat, res_s5, s5p, s5_dskip, glu_w, glu_b, "s5", dout_col=DN_W // S5_W)
    s5g = s5p_vjp(s5cot)
    dqkv, dz, da, dbraw, gdn, recv_ffn1 = dn_block_bwd(dycat, res_dn, rin, dn_cw, dn_out_norm[0], "dn",
                                                       comm=(ffn_slabs(gf1), False))
    d_rest = jnp.concatenate([du.astype(BF16), da.astype(BF16), dbraw.astype(BF16),
                              jnp.zeros((L, REC_PAD - REC_IN), BF16)], axis=1)
    drin = ((dqkv, 0), (dz, 3 * DN_W), (d_rest, REC_U0))
    g_rec_in = jnp.concatenate([matmul([(h3, p)], "tn", f"rec_in_dw{i}", out_dtype=BF16)
                                for i, (p, _) in enumerate(drin)], axis=1)
    g_rec_in = rec_cols_restore(g_rec_in).reshape(N_DEV, D // N_DEV, REC_PAD)
    g_glu = gs5['glu_w'].astype(BF16).reshape(N_DEV, S5_W // N_DEV, S5_W)
    dh3 = matmul([(p, w_rec_in[:, c0:c0 + p.shape[1]]) for p, c0 in drin], "nt", "rec_in_dx")
    dx2, df0, s = gate_norm_bwd(x2, f0, g2, row(norm_mix[1]), tc1, dx3, dh3, "l1_dnorm1")
    s = s.reshape(4, 8, D).sum(axis=1)
    d_g2, d_nmix1, d_t1, d_tc1 = s[0], s[1] * (1.0 + tc1[0]), s[2], s[1] * norm_mix[1]
    dh2, gf0, recv_rec = ffn_block_bwd(df0, res_f0, w_up[0][0], w_up[0][1], ffn_cw[0], w_down[0], "ffn0",
                                       comm=([g_rec_in, g_glu, g_rec_out], False))
    dx1, dy0, s = gate_norm_bwd(x1, y0, g1, row(norm_ffn[0]), sc2, dx2, dh2, "l0_dnorm2")
    s = s.reshape(4, 8, D).sum(axis=1)
    d_g1, d_nffn0, d_sh2, d_sc2 = s[0], s[1] * (1.0 + sc2[0]), s[2], s[1] * norm_ffn[0]
    dh1, gatt, got_b = attention_block_bwd(dy0, res_att, w_att_in, wvec, sinkvec, w_att_out, "att",
                                           comms={'swa': (ffn_slabs(gf0)[:1], False), 1: (ffn_slabs(gf0)[1:], False)},
                                           send_w_out_on=4)
    recv_ffn0 = [got_b['swa'][0], got_b[1][0]]
    (grad_x, s), recv_w_in = gate_norm_bwd(x0, None, None, row(norm_mix[0]), sc1, dx1, dh1, "l0_dnorm1",
                                           comm=([_cols_to_slabs(gatt['w_in'])], False))
    recv_att = [recv_w_in[0], got_b[4][0]]
    s = s.reshape(4, 8, D).sum(axis=1)
    d_nmix0, d_sh1, d_sc1 = s[1] * (1.0 + sc1[0]), s[2], s[1] * norm_mix[0]
    dmod = jnp.stack([jnp.concatenate([d_sh1, d_sc1, d_g1, d_sh2, d_sc2, d_g2]),
                      jnp.concatenate([d_t1, d_tc1, d_tg1, d_t2, d_tc2, d_tg2])])

    P = {'ada_b': dmod, 'norm_mix': jnp.stack([d_nmix0, d_nmix1]), 'norm_ffn': jnp.stack([d_nffn0, d_nffn1]),
         'attn_q_norm_a': gatt['q_norm_a'], 'attn_k_norm_a': gatt['k_norm_a'], 'attn_q_norm_b': gatt['q_norm_b'],
         'attn_k_norm_b': gatt['k_norm_b'], 'attn_sinks': gatt['sinks'],
         's5_lambda_re': s5g[0], 's5_lambda_im': s5g[1], 's5_log_dt': s5g[2], 's5_b_re': s5g[3], 's5_b_im': s5g[4],
         's5_c_re': s5g[5], 's5_c_im': s5g[6], 'dn_a_log': gdn['a_log'], 'dn_dt_bias': gdn['dt_bias'],
         'dn_out_norm': gdn['out_norm'],
         's5_d': gs5['dskip'], 's5_glu_b': gs5['glu_b'], 'dn_conv': gdn['conv'],
         'ffn_conv': jnp.stack([gf0['conv'], gf1['conv']])}

    out = {k: {} for k in ("g", "d", "m", "v")}
    keys = ("g", "d", "m", "v")
    recv = {'attn_w_in': recv_att[0], 'attn_w_out': recv_att[1], 'rec_w_in': recv_rec[0], 's5_glu_w': recv_rec[1],
            'rec_w_out': recv_rec[2]}
    for n, gr_ in recv.items():
        res4 = reduce_adamw(gr_, _shard2d(n, W[n]), _shard2d(n, M[n]), _shard2d(n, V[n]), "adamw_" + n)
        for key, t in zip(keys, res4):
            out[key][n] = (t[:, :REC_IN] if n == 'rec_w_in' else t).reshape(W[n].shape)
    rep_sizes = [_numel(shp) for _, shp in REPLICATED]
    ss_sizes = [_numel(shp) for _, _, shp in SMALL_SHARDED]
    rep_offs = _offsets(rep_sizes + ss_sizes + [1])
    parts = [P[n].reshape(-1) for n, _ in REPLICATED] + [P[n].reshape(-1) for n, _, _ in SMALL_SHARDED]
    parts.append(lsum[0:8].sum().reshape(1))
    spack = _pack_rows(jnp.concatenate(parts), 1024, 8)
    flat2d = lambda t: t.reshape(-1, t.shape[-1])
    sall = None
    for n, idx in (('ffn_w_up', 0), ('ffn_w_down', 1)):
        comm = ([spack], True) if sall is None else None
        res4, got_s = _with_comm(reduce_adamw([recv_ffn0[idx], recv_ffn1[idx]], flat2d(W[n]), flat2d(M[n]),
                                              flat2d(V[n]), "adamw_" + n, comm=comm), comm)
        if got_s is not None:
            sall = got_s[0]
        for key, t in zip(keys, res4):
            out[key][n] = t.reshape(W[n].shape)
    n_rest = sum(ss_sizes) + 1
    pk = lambda d: _pack_rows(jnp.concatenate([d[n].reshape(-1) for n, _ in REPLICATED]
                                              + [jnp.zeros((n_rest,), F32)]), 1024, 8)
    sg, sd_, sm, sv = [t.reshape(-1) for t in reduce_adamw(sall, pk(W), pk(M), pk(V), "adamw_small")]
    loss = 0.5 * sg[rep_offs[-1]] / D

    dmod_all = sall.reshape(N_DEV, -1)[:, :2 * 6 * D].reshape(N_DEV, 2, 6 * D)
    dmod_mine = lax.dynamic_slice_in_dim(dmod_all, me * (6 * D // N_DEV), 6 * D // N_DEV, axis=2)
    g_ada = [matmul([(cond_all, dmod_mine[:, l])], "tn", f"ada{l}_dw")[None] for l in range(2)]
    ada2d = lambda t: t.reshape(2 * D, 6 * D // N_DEV)
    for key, t in zip(("g", "d", "m", "v"), reduce_adamw(g_ada, ada2d(ada_w), ada2d(m_ada_w),
                                                          ada2d(v_ada_w), "adamw_ada_w")):
        out[key]['ada_w'] = t.reshape(ada_w.shape)
    own = []
    for (n, ax, shp), o in zip(SMALL_SHARDED, rep_offs[len(REPLICATED):]):
        slabs = _to_slabs(sg[o:o + _numel(shp)].reshape(shp), ax)
        own.append(lax.dynamic_index_in_dim(slabs, me, axis=0, keepdims=False))
    own_names = [n for n, _, _ in SMALL_SHARDED]
    pk = lambda d: _pack_rows(jnp.concatenate([d[n].reshape(-1) for n in own_names]), 1024, 8)
    og, od, om, ov = [t.reshape(-1) for t in reduce_adamw(_pack_rows(jnp.concatenate(own), 1024, 8)[None],
                                                          pk(W), pk(M), pk(V), "adamw_own")]

    def unpack(names_shapes, bufs):
        o = 0
        for n, shp in names_shapes:
            k = _numel(shp)
            for key, buf in zip(("g", "d", "m", "v"), bufs):
                out[key][n] = buf[o:o + k].reshape(shp)
            o += k

    unpack(REPLICATED, (sg, sd_, sm, sv))
    unpack([(n, W[n].shape) for n in own_names], (og, od, om, ov))
    return (loss, grad_x[None], *[out["g"][n] for n in WEIGHTS], *[out["d"][n] for n in WEIGHTS],
            *[out["m"][n] for n in WEIGHTS], *[out["v"][n] for n in WEIGHTS])
```

```python
import functools
import math

import numpy as np
import jax
import jax.numpy as jnp
from jax import lax
from jax.experimental import pallas as pl
from jax.experimental.pallas import tpu as pltpu

F32 = jnp.float32
BF16 = jnp.bfloat16

N_DEV = 8
D = 1024
HD = 64
BLK = 128
ATTN_IN = 2304
CB = ATTN_IN // 128
B_BRANCHES = ((128, 1), (512, 4), (2048, 16))
S5_W = 256
S5_P = 1024
DN_H = 6
DN_DK = 128
DN_C = 64
REC_IN = 3340
REC_PAD = 3456
D_FF = 2816
EPS = 1e-6
ADAM_LR, ADAM_B1, ADAM_B2, ADAM_EPS, ADAM_WD, ADAM_STEP = 0.001, 0.9, 0.999, 1e-8, 0.01, 10
VMEM_LIMIT = 48 * 1024 * 1024

ALIBI = np.asarray(2.0 ** (-8.0 * np.arange(1, 17) / 16), dtype=np.float32)


def _cparams(*sem):
    return pltpu.CompilerParams(dimension_semantics=tuple(sem), vmem_limit_bytes=VMEM_LIMIT)


def _tile(n, target):
    if n <= target:
        return n
    best = None
    for t in range(128, target + 1, 128):
        if n % t == 0:
            best = t
    assert best is not None, (n, target)
    return best


def _rtile(n, target, mult=8):
    if n <= target:
        return n
    best = None
    for t in range(mult, target + 1, mult):
        if n % t == 0:
            best = t
    assert best is not None, (n, target)
    return best


def _fold8(x):
    r, c = x.shape
    return x.reshape(r // 8, 8, c).sum(axis=0)


def _sigmoid(x):
    return 1.0 / (1.0 + jnp.exp(-x))


_DIMS = {"nn": (((1,), (0,)), ((), ())), "nt": (((1,), (1,)), ((), ())), "tn": (((0,), (0,)), ((), ()))}


MM_FULL_K = 3584


MM_VMEM_BUDGET = 40 << 20


def matmul(pairs, mode, name, out_dtype=F32, tm=1024, tn=1536, tk=1024):
    a0, b0 = pairs[0]
    if mode == "nn":
        (M, K), N = a0.shape, b0.shape[1]
    elif mode == "nt":
        (M, K), N = a0.shape, b0.shape[0]
    else:
        (K, M), N = a0.shape, b0.shape[1]
        tm = 1536
    tn = _tile(N, tn)
    tk = K if K <= MM_FULL_K else _tile(K, tk)
    nk = K // tk
    npair = len(pairs)
    dims = _DIMS[mode]
    kdim = 0 if mode == "tn" else 1
    tks = [a.shape[kdim] for a, _ in pairs]
    assert all(t == K for t in tks) or (nk == 1 and max(tks) <= MM_FULL_K), tks
    if nk > 1:
        tks = [tk] * npair

    def planned(tm_):
        ab = sum(tm_ * t * a.dtype.itemsize + t * tn * b.dtype.itemsize for (a, b), t in zip(pairs, tks))
        return 2 * ab + 2 * tm_ * tn * jnp.dtype(out_dtype).itemsize + (tm_ * tn * 4 if nk > 1 else 0)

    while True:
        tm_try = _rtile(M, tm) if M % 128 else _tile(M, tm)
        if planned(tm_try) <= MM_VMEM_BUDGET or tm <= 128:
            break
        tm //= 2
    tm = tm_try

    def body(*refs):
        o_ref = refs[2 * npair]
        tot = None
        for p in range(npair):
            part = lax.dot_general(refs[2 * p][...].astype(BF16), refs[2 * p + 1][...].astype(BF16),
                                   dims, preferred_element_type=F32)
            tot = part if tot is None else tot + part
        if nk == 1:
            o_ref[...] = tot.astype(o_ref.dtype)
            return
        acc_ref = refs[2 * npair + 1]
        k = pl.program_id(2)

        @pl.when(k == 0)
        def _():
            acc_ref[...] = tot

        @pl.when(k > 0)
        def _():
            acc_ref[...] += tot

        @pl.when(k == nk - 1)
        def _():
            o_ref[...] = acc_ref[...].astype(o_ref.dtype)

    def specs(t):
        if mode == "nn":
            return [pl.BlockSpec((tm, t), lambda j, i, k: (i, k)), pl.BlockSpec((t, tn), lambda j, i, k: (k, j))]
        if mode == "nt":
            return [pl.BlockSpec((tm, t), lambda j, i, k: (i, k)), pl.BlockSpec((tn, t), lambda j, i, k: (j, k))]
        return [pl.BlockSpec((t, tm), lambda j, i, k: (k, i)), pl.BlockSpec((t, tn), lambda j, i, k: (k, j))]

    flat = [t for pr in pairs for t in pr]
    return pl.pallas_call(
        body, name=name, grid=(N // tn, M // tm, nk),
        in_specs=[s for t in tks for s in specs(t)],
        out_specs=pl.BlockSpec((tm, tn), lambda j, i, k: (i, j)),
        out_shape=jax.ShapeDtypeStruct((M, N), out_dtype),
        scratch_shapes=[pltpu.VMEM((tm, tn), F32)] if nk > 1 else [],
        compiler_params=_cparams("parallel", "parallel", "arbitrary"),
    )(*flat)


def gate_norm_fwd(x, y, gate, nw, sh, sc, name):
    L, C = x.shape
    tl = _rtile(L, 512)
    has_gate = y is not None

    def body(*refs):
        if has_gate:
            x_ref, y_ref, g_ref, nw_ref, sh_ref, sc_ref, xn_ref, h_ref = refs
            xn = x_ref[...] + g_ref[...] * y_ref[...]
            xn_ref[...] = xn
        else:
            x_ref, nw_ref, sh_ref, sc_ref, h_ref = refs
            xn = x_ref[...]
        r = lax.rsqrt(jnp.mean(xn * xn, axis=-1, keepdims=True) + EPS)
        h = (xn * r * nw_ref[...]) * (1.0 + sc_ref[...]) + sh_ref[...]
        h_ref[...] = h.astype(BF16)

    big = pl.BlockSpec((tl, C), lambda i: (i, 0))
    vec = pl.BlockSpec((1, C), lambda i: (0, 0))
    if has_gate:
        ins, in_specs = (x, y, gate, nw, sh, sc), [big, big, vec, vec, vec, vec]
        out_shape = (jax.ShapeDtypeStruct((L, C), F32), jax.ShapeDtypeStruct((L, C), BF16))
        out_specs = (big, big)
    else:
        ins, in_specs = (x, nw, sh, sc), [big, vec, vec, vec]
        out_shape = jax.ShapeDtypeStruct((L, C), BF16)
        out_specs = big
    return pl.pallas_call(body, name=name, grid=(L // tl,), in_specs=in_specs, out_specs=out_specs,
                          out_shape=out_shape, compiler_params=_cparams("parallel"))(*ins)


def gate_norm_bwd(xn, y, gate, nw, sc, dxn_direct, dh, name, comm=None):
    L, C = xn.shape
    tl = _rtile(L, 256)
    has_gate = y is not None
    has_direct = dxn_direct is not None

    def body(*refs):
        refs = list(refs)
        xn_ref = refs.pop(0)
        y_ref = refs.pop(0) if has_gate else None
        g_ref = refs.pop(0) if has_gate else None
        nw_ref = refs.pop(0)
        sc_ref = refs.pop(0)
        dd_ref = refs.pop(0) if has_direct else None
        dh_ref = refs.pop(0)
        dxn_ref = refs.pop(0)
        dy_ref = refs.pop(0) if has_gate else None
        sums_ref = refs.pop(0)

        @pl.when(pl.program_id(0) == 0)
        def _():
            sums_ref[...] = jnp.zeros_like(sums_ref)

        xv = xn_ref[...]
        dh_v = dh_ref[...]
        r = lax.rsqrt(jnp.mean(xv * xv, axis=-1, keepdims=True) + EPS)
        n = xv * r
        a = nw_ref[...] * (1.0 + sc_ref[...])
        dn = dh_v * a
        dx = r * (dn - n * jnp.mean(dn * n, axis=-1, keepdims=True))
        if has_direct:
            dx = dx + dd_ref[...]
        dxn_ref[...] = dx
        sums_ref[8:16, :] += _fold8(dh_v * n)
        sums_ref[16:24, :] += _fold8(dh_v)
        if has_gate:
            dy_ref[...] = (dx * g_ref[...]).astype(BF16)
            sums_ref[0:8, :] += _fold8(dx * y_ref[...])

    big = pl.BlockSpec((tl, C), lambda i: (i, 0))
    vec = pl.BlockSpec((1, C), lambda i: (0, 0))
    ins, in_specs = [xn], [big]
    if has_gate:
        ins += [y, gate]
        in_specs += [big, vec]
    ins += [nw, sc]
    in_specs += [vec, vec]
    if has_direct:
        ins.append(dxn_direct)
        in_specs.append(big)
    ins.append(dh)
    in_specs.append(big)
    out_shape = [jax.ShapeDtypeStruct((L, C), F32)]
    out_specs = [big]
    if has_gate:
        out_shape.append(jax.ShapeDtypeStruct((L, C), BF16))
        out_specs.append(big)
    out_shape.append(jax.ShapeDtypeStruct((32, C), F32))
    out_specs.append(pl.BlockSpec((32, C), lambda i: (0, 0)))
    return _call(body, ins, name=name, grid=(L // tl,), in_specs=in_specs, out_specs=tuple(out_specs),
                 out_shape=tuple(out_shape), sem=("arbitrary",), comm=comm)


def final_loss(x, f, gate, target, name):
    L, C = x.shape
    tl = _rtile(L, 256)

    def body(x_ref, f_ref, g_ref, t_ref, dy_ref, df_ref, sums_ref):
        @pl.when(pl.program_id(0) == 0)
        def _():
            sums_ref[...] = jnp.zeros_like(sums_ref)

        fv = f_ref[...]
        err = x_ref[...] + g_ref[...] * fv - t_ref[...]
        dy = err * (1.0 / C)
        dy_ref[...] = dy
        df_ref[...] = (dy * g_ref[...]).astype(BF16)
        sums_ref[0:8, :] += _fold8(err * err)
        sums_ref[8:16, :] += _fold8(dy * fv)

    big = pl.BlockSpec((tl, C), lambda i: (i, 0))
    vec = pl.BlockSpec((1, C), lambda i: (0, 0))
    return pl.pallas_call(
        body, name=name, grid=(L // tl,), in_specs=[big, big, vec, big],
        out_specs=(big, big, pl.BlockSpec((16, C), lambda i: (0, 0))),
        out_shape=(jax.ShapeDtypeStruct((L, C), F32), jax.ShapeDtypeStruct((L, C), BF16),
                   jax.ShapeDtypeStruct((16, C), F32)),
        compiler_params=_cparams("arbitrary"))(x, f, gate, target)


def _seg_ones(seg):
    r = lax.broadcasted_iota(jnp.int32, (128, 128), 0) // seg
    c = lax.broadcasted_iota(jnp.int32, (128, 128), 1) // seg
    return (r == c).astype(BF16)


def _segsum(t, ones):
    hi = t.astype(BF16)
    lo = (t - hi.astype(F32)).astype(BF16)
    return (jnp.dot(hi, ones, preferred_element_type=F32) + jnp.dot(lo, ones, preferred_element_type=F32))


_NORMED_TILES = tuple(list(range(0, 5)) + list(range(6, 14)))


DIL = (4, 16)
B_COLS0, B_W = 768, 1536
DIL_TL = 256


def _to_dilated(scr_ref, out_ref, d, cast=None):
    nj, tl, _ = scr_ref.shape
    for r in range(d):
        for j in range(nj):
            piece = scr_ref[j, pl.ds(r, tl // d, stride=d), :]
            c0 = (r * nj + j) * 128
            out_ref[:, c0:c0 + 128] = piece if cast is None else piece.astype(cast)


def _from_dilated(in_ref, scr_ref, d):
    nj, tl, _ = scr_ref.shape
    for r in range(d):
        for j in range(nj):
            c0 = (r * nj + j) * 128
            scr_ref[j, pl.ds(r, tl // d, stride=d), :] = in_ref[:, c0:c0 + 128]


def _dil_spec(tl, d, width):
    return pl.BlockSpec((tl // d, d * width), lambda i: (i, 0))


def qknorm_fwd(qkv, wvec, name):
    L, C = qkv.shape
    tl = DIL_TL

    def body(x_ref, w_ref, o_ref, o4_ref, o16_ref, scr_ref):
        ones = _seg_ones(HD)
        for t in range(CB):
            cs = slice(t * 128, (t + 1) * 128)
            x = x_ref[:, cs]
            if t in _NORMED_TILES:
                ms = _segsum(x * x, ones) * (1.0 / HD)
                x = x * lax.rsqrt(ms + EPS) * w_ref[:, cs]
            o_ref[:, cs] = x.astype(BF16)
            if t * 128 >= B_COLS0:
                scr_ref[t - B_COLS0 // 128] = x
        _to_dilated(scr_ref, o4_ref, 4, BF16)
        _to_dilated(scr_ref, o16_ref, 16, BF16)

    return pl.pallas_call(
        body, name=name, grid=(L // tl,),
        in_specs=[pl.BlockSpec((tl, C), lambda i: (i, 0)), pl.BlockSpec((1, C), lambda i: (0, 0))],
        out_specs=(pl.BlockSpec((tl, C), lambda i: (i, 0)), _dil_spec(tl, 4, B_W), _dil_spec(tl, 16, B_W)),
        out_shape=(jax.ShapeDtypeStruct((L, C), BF16), jax.ShapeDtypeStruct((L // 4, 4 * B_W), BF16),
                   jax.ShapeDtypeStruct((L // 16, 16 * B_W), BF16)),
        scratch_shapes=[pltpu.VMEM((B_W // 128, tl, 128), F32)], compiler_params=_cparams("parallel"))(qkv, wvec)


def qknorm_bwd(qkv, wvec, d_a, d_b, name):
    L, C = qkv.shape
    tl = DIL_TL

    def body(x_ref, w_ref, dqa, dka, dva, q1, k1, v1, q4, k4, v4, q16, k16, v16, dx_ref, sums_ref,
             dy_ref, s4_ref, s16_ref):
        @pl.when(pl.program_id(0) == 0)
        def _():
            sums_ref[...] = jnp.zeros_like(sums_ref)

        dy_ref[:, 0:512] = dqa[...]
        for off, ref in ((512, dka), (640, dva)):
            for g in range(2):
                acc = ref[:, g * 256:g * 256 + HD]
                for h in range(1, 4):
                    acc = acc + ref[:, g * 256 + h * HD:g * 256 + (h + 1) * HD]
                dy_ref[:, off + g * HD:off + (g + 1) * HD] = acc
        for off, r1, r4, r16 in ((768, q1, q4, q16), (1280, k1, k4, k16), (1792, v1, v4, v16)):
            _from_dilated(r4, s4_ref, 4)
            _from_dilated(r16, s16_ref, 16)
            for j in range(4):
                dy_ref[:, off + j * 128:off + (j + 1) * 128] = r1[:, j * 128:(j + 1) * 128] + s4_ref[j] + s16_ref[j]

        ones = _seg_ones(HD)
        for t in range(CB):
            cs = slice(t * 128, (t + 1) * 128)
            d = dy_ref[:, cs]
            if t in _NORMED_TILES:
                x = x_ref[:, cs]
                r = lax.rsqrt(_segsum(x * x, ones) * (1.0 / HD) + EPS)
                n = x * r
                dn = d * w_ref[:, cs]
                dx_ref[:, cs] = (r * (dn - n * (_segsum(dn * n, ones) * (1.0 / HD)))).astype(BF16)
                sums_ref[:, cs] += _fold8(d * n)
            else:
                dx_ref[:, cs] = d.astype(BF16)

    big = pl.BlockSpec((tl, C), lambda i: (i, 0))
    p512 = pl.BlockSpec((tl, 512), lambda i: (i, 0))
    return pl.pallas_call(
        body, name=name, grid=(L // tl,),
        in_specs=[big, pl.BlockSpec((1, C), lambda i: (0, 0))] + [p512] * 6 + [_dil_spec(tl, 4, 512)] * 3
        + [_dil_spec(tl, 16, 512)] * 3,
        out_specs=(big, pl.BlockSpec((8, C), lambda i: (0, 0))),
        out_shape=(jax.ShapeDtypeStruct((L, C), BF16), jax.ShapeDtypeStruct((8, C), F32)),
        scratch_shapes=[pltpu.VMEM((tl, C), F32), pltpu.VMEM((4, tl, 128), F32), pltpu.VMEM((4, tl, 128), F32)],
        compiler_params=_cparams("arbitrary"))(qkv, wvec, *d_a, *d_b[0], *d_b[1], *d_b[2])


def _attn_biases(t, slopes, step, maxdist):
    qi = lax.broadcasted_iota(jnp.int32, (BLK, 2 * BLK), 0)
    sj = lax.broadcasted_iota(jnp.int32, (BLK, 2 * BLK), 1)
    dist = BLK + qi - sj
    valid = (dist >= 0) & (dist <= maxdist)
    distf = (step * dist).astype(F32)
    inner = [jnp.where(valid, (-sl) * distf, -jnp.inf) for sl in slopes]
    first = [jnp.where((t > 0) | (sj >= BLK), b, -jnp.inf) for b in inner]
    return inner, first


def _attn_scores(q, kw, bias):
    return lax.dot_general(q, kw, (((1,), (1,)), ((), ())), preferred_element_type=F32) + bias


ATT_NQ = 8


def _attn_operands(nq, hp, gqa, q_ref, kh_ref, kc_ref, vh_ref, vc_ref):
    ops = []
    for b in range(nq):
        rows = slice(b * BLK, (b + 1) * BLK)
        prev = slice((b - 1) * BLK, b * BLK)
        for e in range(2):
            cs = slice(e * HD, (e + 1) * HD)
            if gqa:
                ksel = lambda ref, r: jnp.where(hp >= 2, ref[r, 64:128], ref[r, 0:64])
            else:
                ksel = lambda ref, r, cs=cs: ref[r, cs]
            kprev = ksel(kh_ref, slice(0, BLK)) if b == 0 else ksel(kc_ref, prev)
            vprev = ksel(vh_ref, slice(0, BLK)) if b == 0 else ksel(vc_ref, prev)
            ops.append((b, e, rows, cs, q_ref[rows, cs] * (HD ** -0.5),
                        jnp.concatenate([kprev, ksel(kc_ref, rows)], axis=0),
                        jnp.concatenate([vprev, ksel(vc_ref, rows)], axis=0)))
    return ops


def _attn_specs(cb, q_off, k_off, v_off, gqa):
    kcol = (lambda r, hp: r * cb + k_off) if gqa else (lambda r, hp: r * cb + k_off + hp)
    vcol = (lambda r, hp: r * cb + v_off) if gqa else (lambda r, hp: r * cb + v_off + hp)
    return kcol, vcol


def attn_fwd(X, d, q_off, k_off, v_off, gqa, slope0, maxdist, name, comm=None):
    Ls = X.shape[0]
    nq = min(ATT_NQ, Ls // BLK)
    TQ = nq * BLK
    nt = Ls // TQ
    slopes = jnp.asarray(ALIBI)

    def body(sl_ref, q_ref, kh_ref, kc_ref, vh_ref, vc_ref, o_ref, lse_ref):
        hp, t = pl.program_id(1), pl.program_id(2)
        ops = _attn_operands(nq, hp, gqa, q_ref, kh_ref, kc_ref, vh_ref, vc_ref)
        inner, first = _attn_biases(t, [sl_ref[slope0 + 2 * hp + e] for e in range(2)], d, maxdist)
        s = [_attn_scores(q, kw, first[e] if b == 0 else inner[e]) for (b, e, rows, cs, q, kw, vw) in ops]
        m = [jnp.max(x, axis=-1, keepdims=True) for x in s]
        p = [jnp.exp(x - mm) for x, mm in zip(s, m)]
        l = [jnp.sum(x, axis=-1, keepdims=True) for x in p]
        o = [jnp.dot(x.astype(BF16), op[6], preferred_element_type=F32) / ll for x, op, ll in zip(p, ops, l)]
        for (b, e, rows, cs, q, kw, vw), oo, mm, ll in zip(ops, o, m, l):
            o_ref[rows, cs] = oo
            lse_ref[rows, cs] = jnp.broadcast_to(mm + jnp.log(ll), (BLK, HD))

    cb = X.shape[1] // (d * 128)
    kcol, vcol = _attn_specs(cb, q_off, k_off, v_off, gqa)
    tile, blk = (TQ, 128), (BLK, 128)
    halo = lambda t: jnp.maximum(t * nq - 1, 0)
    in_specs = [
        pl.BlockSpec(memory_space=pltpu.SMEM),
        pl.BlockSpec(tile, lambda r, hp, t: (t, r * cb + q_off + hp)),
        pl.BlockSpec(blk, lambda r, hp, t: (halo(t), kcol(r, hp))),
        pl.BlockSpec(tile, lambda r, hp, t: (t, kcol(r, hp))),
        pl.BlockSpec(blk, lambda r, hp, t: (halo(t), vcol(r, hp))),
        pl.BlockSpec(tile, lambda r, hp, t: (t, vcol(r, hp))),
    ]
    out_spec = pl.BlockSpec(tile, lambda r, hp, t: (t, r * 4 + hp))
    out = jax.ShapeDtypeStruct((Ls, d * 512), F32)
    return _call(body, (slopes, X, X, X, X, X), name=name, grid=(d, 4, nt), in_specs=in_specs,
                 out_specs=(out_spec, out_spec), out_shape=(out, out),
                 sem=("parallel", "parallel", "arbitrary"), comm=comm)


def attn_bwd(X, o, lse, do, dlse, d, q_off, k_off, v_off, gqa, slope0, maxdist, name, comm=None):
    Ls = X.shape[0]
    slopes = jnp.asarray(ALIBI)

    nq = min(ATT_NQ, Ls // BLK)
    TQ = nq * BLK
    nt = Ls // TQ
    nt_dims, tn_dims = (((1,), (1,)), ((), ())), (((0,), (0,)), ((), ()))

    def body(sl_ref, q_ref, kh_ref, kc_ref, vh_ref, vc_ref, o_ref, lse_ref, do_ref, dlse_ref,
             dq_ref, dk_ref, dv_ref, ak_ref, av_ref, pk_ref, pv_ref):
        hp, t = pl.program_id(1), pl.program_id(2)

        @pl.when(t == 0)
        def _():
            pk_ref[...] = jnp.zeros_like(pk_ref)
            pv_ref[...] = jnp.zeros_like(pv_ref)

        @pl.when(t < nt)
        def _():
            ops = _attn_operands(nq, hp, gqa, q_ref, kh_ref, kc_ref, vh_ref, vc_ref)
            inner, first = _attn_biases(t, [sl_ref[slope0 + 2 * hp + e] for e in range(2)], d, maxdist)
            sv = [_attn_scores(q, kw, first[e] if b == 0 else inner[e]) for (b, e, rows, cs, q, kw, vw) in ops]
            p = [jnp.exp(s - lse_ref[op[2], op[1] * HD:op[1] * HD + 1]) for s, op in zip(sv, ops)]
            dov = [do_ref[op[2], op[3]] for op in ops]
            delta = [jnp.sum(dd * o_ref[op[2], op[3]], axis=-1, keepdims=True) for dd, op in zip(dov, ops)]
            dob = [dd.astype(BF16) for dd in dov]
            dp = [lax.dot_general(dd, op[6], nt_dims, preferred_element_type=F32) for dd, op in zip(dob, ops)]
            ds = [(pp * (x - dl + dlse_ref[op[2], op[1] * HD:op[1] * HD + 1])).astype(BF16)
                  for pp, x, dl, op in zip(p, dp, delta, ops)]
            dq = [jnp.dot(x, op[5], preferred_element_type=F32) * (HD ** -0.5) for x, op in zip(ds, ops)]
            dkw = [lax.dot_general(x, op[4], tn_dims, preferred_element_type=F32) for x, op in zip(ds, ops)]
            dvw = [lax.dot_general(pp.astype(BF16), dd, tn_dims, preferred_element_type=F32)
                   for pp, dd in zip(p, dob)]
            ak_ref[...] = jnp.zeros_like(ak_ref)
            av_ref[...] = jnp.zeros_like(av_ref)
            for (b, e, rows, cs, q, kw, vw), x, yk, yv in zip(ops, dq, dkw, dvw):
                dq_ref[rows, cs] = x
                ak_ref[b * BLK:(b + 2) * BLK, cs] += yk
                av_ref[b * BLK:(b + 2) * BLK, cs] += yv
            if nt == 1:
                dk_ref[...] = ak_ref[BLK:, :]
                dv_ref[...] = av_ref[BLK:, :]
                return
            last = slice(TQ - BLK, TQ)
            dk_ref[...] = pk_ref[...]
            dv_ref[...] = pv_ref[...]
            dk_ref[last, :] += ak_ref[0:BLK, :]
            dv_ref[last, :] += av_ref[0:BLK, :]
            pk_ref[...] = ak_ref[BLK:, :]
            pv_ref[...] = av_ref[BLK:, :]

        @pl.when(t == nt)
        def _():
            dk_ref[...] = pk_ref[...]
            dv_ref[...] = pv_ref[...]

    cb = X.shape[1] // (d * 128)
    kcol, vcol = _attn_specs(cb, q_off, k_off, v_off, gqa)
    tile, blk = (TQ, 128), (BLK, 128)
    cur = lambda t: jnp.minimum(t, nt - 1)
    halo = lambda t: jnp.maximum(cur(t) * nq - 1, 0)
    ospec = pl.BlockSpec(tile, lambda r, hp, t: (cur(t), r * 4 + hp))
    in_specs = [
        pl.BlockSpec(memory_space=pltpu.SMEM),
        pl.BlockSpec(tile, lambda r, hp, t: (cur(t), r * cb + q_off + hp)),
        pl.BlockSpec(blk, lambda r, hp, t: (halo(t), kcol(r, hp))),
        pl.BlockSpec(tile, lambda r, hp, t: (cur(t), kcol(r, hp))),
        pl.BlockSpec(blk, lambda r, hp, t: (halo(t), vcol(r, hp))),
        pl.BlockSpec(tile, lambda r, hp, t: (cur(t), vcol(r, hp))),
        ospec, ospec, ospec, ospec,
    ]
    shifted = pl.BlockSpec(tile, lambda r, hp, t: (jnp.maximum(t - 1, 0), r * 4 + hp))
    out = jax.ShapeDtypeStruct((Ls, d * 512), F32)
    return _call(body, (slopes, X, X, X, X, X, o, lse, do, dlse), name=name, grid=(d, 4, nt + 1 if nt > 1 else 1),
                 in_specs=in_specs, out_specs=(ospec, shifted, shifted), out_shape=(out, out, out),
                 scratch_shapes=[pltpu.VMEM((TQ + BLK, 128), F32), pltpu.VMEM((TQ + BLK, 128), F32),
                                 pltpu.VMEM((TQ, 128), F32), pltpu.VMEM((TQ, 128), F32)],
                 sem=("parallel", "parallel", "arbitrary"), comm=comm)


def attn_merge_fwd(oa, la, sink, obs, lbs, name):
    L = oa.shape[0]
    tl = DIL_TL

    def body(oa_ref, la_ref, sk_ref, o1, o4, o16, l1, l4, l16, m_ref, so4, so16, sl4, sl16):
        m_ref[:, 0:512] = (oa_ref[...] * _sigmoid(la_ref[...] - sk_ref[...])).astype(BF16)
        for src, dst, d in ((o4, so4, 4), (o16, so16, 16), (l4, sl4, 4), (l16, sl16, 16)):
            _from_dilated(src, dst, d)
        for j in range(4):
            cs = slice(j * 128, (j + 1) * 128)
            a, b, c = l1[:, cs], sl4[j], sl16[j]
            mx = jnp.maximum(jnp.maximum(a, b), c)
            ea, eb, ec = jnp.exp(a - mx), jnp.exp(b - mx), jnp.exp(c - mx)
            inv = 1.0 / (ea + eb + ec)
            m_ref[:, 512 + j * 128:512 + (j + 1) * 128] = (
                (ea * inv) * o1[:, cs] + (eb * inv) * so4[j] + (ec * inv) * so16[j]).astype(BF16)

    big = pl.BlockSpec((tl, 512), lambda i: (i, 0))
    dil = [big, _dil_spec(tl, 4, 512), _dil_spec(tl, 16, 512)]
    return pl.pallas_call(
        body, name=name, grid=(L // tl,),
        in_specs=[big, big, pl.BlockSpec((1, 512), lambda i: (0, 0))] + dil + dil,
        out_specs=pl.BlockSpec((tl, 1024), lambda i: (i, 0)),
        out_shape=jax.ShapeDtypeStruct((L, 1024), BF16), scratch_shapes=[pltpu.VMEM((4, tl, 128), F32)] * 4,
        compiler_params=_cparams("parallel"),
    )(oa, la, sink, *obs, *lbs)


def attn_merge_bwd(dm, oa, la, sink, obs, lbs, name):
    L = oa.shape[0]
    tl = DIL_TL

    def body(dm_ref, oa_ref, la_ref, sk_ref, o1, o4, o16, l1, l4, l16,
             doa_ref, dla_ref, d1, d4, d16, g1, g4, g16, sums_ref, so4, so16, sl4, sl16, sd4, sd16, sg4, sg16):
        @pl.when(pl.program_id(0) == 0)
        def _():
            sums_ref[...] = jnp.zeros_like(sums_ref)

        for src, dst, d in ((o4, so4, 4), (o16, so16, 16), (l4, sl4, 4), (l16, sl16, 16)):
            _from_dilated(src, dst, d)
        ones = _seg_ones(HD)
        for t in range(4):
            cs = slice(t * 128, (t + 1) * 128)
            dma = dm_ref[:, cs]
            keep = _sigmoid(la_ref[:, cs] - sk_ref[:, cs])
            doa_ref[:, cs] = dma * keep
            tt = dma * oa_ref[:, cs] * keep * (1.0 - keep)
            dla_ref[:, cs] = _segsum(tt, ones)
            sums_ref[:, cs] += _fold8(-tt)
            dmb = dm_ref[:, 512 + t * 128:512 + (t + 1) * 128]
            a, b, c = l1[:, cs], sl4[t], sl16[t]
            mx = jnp.maximum(jnp.maximum(a, b), c)
            ea, eb, ec = jnp.exp(a - mx), jnp.exp(b - mx), jnp.exp(c - mx)
            inv = 1.0 / (ea + eb + ec)
            wa, wb, wc = ea * inv, eb * inv, ec * inv
            d1[:, cs] = wa * dmb
            sd4[t] = wb * dmb
            sd16[t] = wc * dmb
            sa = _segsum(dmb * o1[:, cs], ones)
            sb = _segsum(dmb * so4[t], ones)
            sc_ = _segsum(dmb * so16[t], ones)
            mean = wa * sa + wb * sb + wc * sc_
            g1[:, cs] = wa * (sa - mean)
            sg4[t] = wb * (sb - mean)
            sg16[t] = wc * (sc_ - mean)
        for src, dst, d in ((sd4, d4, 4), (sd16, d16, 16), (sg4, g4, 4), (sg16, g16, 16)):
            _to_dilated(src, dst, d)

    big = pl.BlockSpec((tl, 512), lambda i: (i, 0))
    dil = [big, _dil_spec(tl, 4, 512), _dil_spec(tl, 16, 512)]
    sd = jax.ShapeDtypeStruct
    shp = [sd((L, 512), F32), sd((L // 4, 4 * 512), F32), sd((L // 16, 16 * 512), F32)]
    return pl.pallas_call(
        body, name=name, grid=(L // tl,),
        in_specs=[pl.BlockSpec((tl, 1024), lambda i: (i, 0)), big, big,
                  pl.BlockSpec((1, 512), lambda i: (0, 0))] + dil + dil,
        out_specs=tuple([big, big] + dil + dil + [pl.BlockSpec((8, 512), lambda i: (0, 0))]),
        out_shape=tuple([shp[0], shp[0]] + shp + shp + [sd((8, 512), F32)]),
        scratch_shapes=[pltpu.VMEM((4, tl, 128), F32)] * 8, compiler_params=_cparams("arbitrary"),
    )(dm, oa, la, sink, *obs, *lbs)


def _shift_down(x, halo, k, first):
    rows = lax.broadcasted_iota(jnp.int32, (8, x.shape[1]), 0)
    out = pltpu.roll(x, k, axis=0)
    hrows = jnp.where(first, 0.0, pltpu.roll(halo, k, axis=0))
    top = jnp.where(rows < k, hrows, out[0:8, :])
    return jnp.concatenate([top, out[8:, :]], axis=0)


def _shift_up(x, nxt, k):
    tl = x.shape[0]
    rows = lax.broadcasted_iota(jnp.int32, (8, x.shape[1]), 0)
    out = pltpu.roll(x, tl - k, axis=0)
    bottom = jnp.where(rows >= 8 - k, pltpu.roll(nxt, 8 - k, axis=0), out[tl - 8:, :])
    return jnp.concatenate([out[:tl - 8, :], bottom], axis=0)


def _silu(x):
    return x * _sigmoid(x)


def _dsilu(x):
    s = _sigmoid(x)
    return s * (1.0 + x * (1.0 - s))


def ffn_act_fwd(ua, ub, cw, name):
    L, F = ua.shape
    tl = _rtile(L, 256)
    tc = _tile(F, 1408)
    hb = tl // 8

    def body(ua_ref, uah_ref, ub_ref, ubh_ref, wa_ref, wb_ref, o_ref):
        first = pl.program_id(1) == 0

        def conv(x_ref, h_ref, w_ref):
            x = x_ref[...]
            h = h_ref[...]
            return (w_ref[2:3, :] * x + w_ref[1:2, :] * _shift_down(x, h, 1, first)
                    + w_ref[0:1, :] * _shift_down(x, h, 2, first))

        a = conv(ua_ref, uah_ref, wa_ref)
        b = conv(ub_ref, ubh_ref, wb_ref)
        o_ref[...] = (_silu(a) * b).astype(BF16)

    main = pl.BlockSpec((tl, tc), lambda j, i: (i, j))
    halo = pl.BlockSpec((8, tc), lambda j, i: (jnp.maximum(i * hb - 1, 0), j))
    wa = pl.BlockSpec((3, tc), lambda j, i: (0, j))
    wb = pl.BlockSpec((3, tc), lambda j, i: (0, j + F // tc))
    return pl.pallas_call(
        body, name=name, grid=(F // tc, L // tl), in_specs=[main, halo, main, halo, wa, wb],
        out_specs=main, out_shape=jax.ShapeDtypeStruct((L, F), BF16),
        compiler_params=_cparams("parallel", "parallel"))(ua, ua, ub, ub, cw, cw)


def ffn_act_bwd(ua, ub, cw, dact, name, comm=None):
    L, F = ua.shape
    tl = _rtile(L, 256)
    tc = _tile(F, 1408)
    hb = tl // 8
    nrt = L // tl

    def body(ua_ref, uah_ref, ub_ref, ubh_ref, wa_ref, wb_ref, da_ref, dua_ref, dub_ref, sums_ref, ca_ref, cb_ref):
        i = pl.program_id(1)
        first = i == nrt - 1

        @pl.when(i == 0)
        def _():
            sums_ref[...] = jnp.zeros_like(sums_ref)
            ca_ref[...] = jnp.zeros_like(ca_ref)
            cb_ref[...] = jnp.zeros_like(cb_ref)

        def taps(x_ref, h_ref):
            x = x_ref[...]
            h = h_ref[...]
            return x, _shift_down(x, h, 1, first), _shift_down(x, h, 2, first)

        a0, a1, a2 = taps(ua_ref, uah_ref)
        b0, b1, b2 = taps(ub_ref, ubh_ref)
        a = wa_ref[2:3, :] * a0 + wa_ref[1:2, :] * a1 + wa_ref[0:1, :] * a2
        b = wb_ref[2:3, :] * b0 + wb_ref[1:2, :] * b1 + wb_ref[0:1, :] * b2
        dact_v = da_ref[...]
        dya = dact_v * b * _dsilu(a)
        dyb = dact_v * _silu(a)
        for (dy, w_ref, c_ref, d_ref, xs, base) in ((dya, wa_ref, ca_ref, dua_ref, (a2, a1, a0), 0),
                                                     (dyb, wb_ref, cb_ref, dub_ref, (b2, b1, b0), 24)):
            nxt = c_ref[...]
            d_ref[...] = (w_ref[2:3, :] * dy + w_ref[1:2, :] * _shift_up(dy, nxt, 1)
                          + w_ref[0:1, :] * _shift_up(dy, nxt, 2)).astype(BF16)
            c_ref[...] = dy[0:8, :]
            for j in range(3):
                sums_ref[base + 8 * j:base + 8 * j + 8, :] += _fold8(dy * xs[j])

    rev = lambda i: nrt - 1 - i
    main = pl.BlockSpec((tl, tc), lambda j, i: (rev(i), j))
    halo = pl.BlockSpec((8, tc), lambda j, i: (jnp.maximum(rev(i) * hb - 1, 0), j))
    wa = pl.BlockSpec((3, tc), lambda j, i: (0, j))
    wb = pl.BlockSpec((3, tc), lambda j, i: (0, j + F // tc))
    ob = jax.ShapeDtypeStruct((L, F), BF16)
    return _call(body, (ua, ua, ub, ub, cw, cw, dact), name=name, grid=(F // tc, nrt),
                 in_specs=[main, halo, main, halo, wa, wb, main],
                 out_specs=(main, main, pl.BlockSpec((48, tc), lambda j, i: (0, j))),
                 out_shape=(ob, ob, jax.ShapeDtypeStruct((48, F), F32)),
                 scratch_shapes=[pltpu.VMEM((8, tc), F32), pltpu.VMEM((8, tc), F32)],
                 sem=("parallel", "arbitrary"), comm=comm)


def attn_vectors(qna, kna, qnb, knb, sinks):
    ones = jnp.ones((128,), F32)
    wvec = jnp.concatenate([jnp.tile(qna, 8), jnp.tile(kna, 2), ones, jnp.tile(qnb, 8), jnp.tile(knb, 8),
                            jnp.tile(ones, 4)]).reshape(1, ATTN_IN)
    return wvec, jnp.repeat(sinks, HD).reshape(1, 512)


def _with_comm(result, comm):
    return result if comm is not None else (result, None)


def attention_block_fwd(h, w_in, wvec, sinkvec, w_out, tag, comms=None):
    L = h.shape[0]
    comms = comms or {}
    got = {}
    qkv = matmul([(h, w_in)], "nn", tag + "_qkv")
    X, X4, X16 = qknorm_fwd(qkv, wvec, tag + "_qknorm")
    (oa, la), got['swa'] = _with_comm(attn_fwd(X, 1, 0, 4, 5, True, 0, BLK - 1, tag + "_swa",
                                               comm=comms.get('swa')), comms.get('swa'))
    views = {1: (X, 6, 10, 14), 4: (X4, 0, 4, 8), 16: (X16, 0, 4, 8)}
    obs, lbs = [], []
    for window, d in B_BRANCHES:
        xd, qo, ko, vo = views[d]
        (o, l), got[d] = _with_comm(attn_fwd(xd, d, qo, ko, vo, False, 8, window // d,
                                             tag + f"_dil{d}", comm=comms.get(d)), comms.get(d))
        obs.append(o)
        lbs.append(l)
    m = attn_merge_fwd(oa, la, sinkvec, obs, lbs, tag + "_merge")
    y = matmul([(m, w_out)], "nn", tag + "_out")
    return y, (h, qkv, views, oa, la, obs, lbs, m), got


def attention_block_bwd(dy, res, w_in, wvec, sinkvec, w_out, tag, comms=None, send_w_out_on=None):
    h, qkv, views, oa, la, obs, lbs, m = res
    comms = dict(comms or {})
    got = {}
    g_w_out = matmul([(m, dy)], "tn", tag + "_dwout", out_dtype=BF16)
    if send_w_out_on is not None:
        comms[send_w_out_on] = ([g_w_out.reshape(N_DEV, D // N_DEV, D)], False)
    dm = matmul([(dy, w_out)], "nt", tag + "_dm")
    doa, dla, d1, d2, d3, g1, g2, g3, sinksums = attn_merge_bwd(dm, oa, la, sinkvec, obs, lbs, tag + "_dmerge")
    d_a, got['swa'] = _with_comm(attn_bwd(views[1][0], oa, la, doa, dla, 1, 0, 4, 5, True, 0, BLK - 1,
                                          tag + "_dswa", comm=comms.get('swa')), comms.get('swa'))
    d_b = []
    for (window, d), o, l, do, dl in zip(B_BRANCHES, obs, lbs, (d1, d2, d3), (g1, g2, g3)):
        xd, qo, ko, vo = views[d]
        dqkv_d, got[d] = _with_comm(attn_bwd(xd, o, l, do, dl, d, qo, ko, vo, False, 8, window // d,
                                             tag + f"_ddil{d}", comm=comms.get(d)), comms.get(d))
        d_b.append(dqkv_d)
    dqkv, wsums = qknorm_bwd(qkv, wvec, d_a, d_b, tag + "_dqknorm")
    g_w_in = matmul([(h, dqkv)], "tn", tag + "_dwin", out_dtype=BF16)
    dh = matmul([(dqkv, w_in)], "nt", tag + "_dh")
    ws = wsums.sum(axis=0)
    grads = dict(
        w_in=g_w_in, w_out=g_w_out,
        q_norm_a=ws[0:512].reshape(8, HD).sum(axis=0), k_norm_a=ws[512:640].reshape(2, HD).sum(axis=0),
        q_norm_b=ws[768:1280].reshape(8, HD).sum(axis=0), k_norm_b=ws[1280:1792].reshape(8, HD).sum(axis=0),
        sinks=sinksums.sum(axis=0).reshape(8, HD).sum(axis=1))
    return dh, grads, got


def ffn_block_fwd(h, w_up_a, w_up_b, cw, w_down, tag):
    ua = matmul([(h, w_up_a)], "nn", tag + "_upa")
    ub = matmul([(h, w_up_b)], "nn", tag + "_upb")
    act = ffn_act_fwd(ua, ub, cw, tag + "_act")
    f = matmul([(act, w_down)], "nn", tag + "_down")
    return f, (h, ua, ub, act)


def ffn_block_bwd(df, res, w_up_a, w_up_b, cw, w_down, tag, comm=None):
    h, ua, ub, act = res
    g_down = matmul([(act, df)], "tn", tag + "_dwdown", out_dtype=BF16)
    dact = matmul([(df, w_down)], "nt", tag + "_dact")
    (dua, dub, sums), got = _with_comm(ffn_act_bwd(ua, ub, cw, dact, tag + "_dactk", comm=comm), comm)
    g_up = jnp.concatenate([_cols_to_slabs(matmul([(h, dua)], "tn", tag + "_dwupa", out_dtype=BF16), N_DEV // 2),
                            _cols_to_slabs(matmul([(h, dub)], "tn", tag + "_dwupb", out_dtype=BF16), N_DEV // 2)],
                           axis=0)
    dh = matmul([(dua, w_up_a), (dub, w_up_b)], "nt", tag + "_dh")
    s = sums.reshape(2, 3, 8, D_FF).sum(axis=2)
    g_conv = jnp.concatenate([s[0], s[1]], axis=1)
    return dh, dict(w_up=g_up, conv=g_conv, w_down=g_down), got


def s5_params(lam_re, lam_im, log_dt, b_re, b_im, c_re, c_im):
    dt = jnp.exp(log_dt)[:, None]
    mag, ang = jnp.exp(lam_re * dt), lam_im * dt
    a_re, a_im = mag * jnp.cos(ang), mag * jnp.sin(ang)
    nr, ni = a_re - 1.0, a_im
    den = lam_re * lam_re + lam_im * lam_im
    f_re = (nr * lam_re + ni * lam_im) / den
    f_im = (ni * lam_re - nr * lam_im) / den
    eye = jnp.eye(16, dtype=F32)[:, None, :, None]
    bd = lambda b: (eye * jnp.transpose(b, (0, 2, 1))[:, :, None, :]).reshape(S5_W, S5_P)
    cd = lambda c: (eye * jnp.transpose(c, (0, 2, 1))[:, :, None, :]).reshape(S5_P, S5_W)
    flat = lambda t: t.reshape(1, S5_P)
    return flat(a_re), flat(a_im), flat(f_re), flat(f_im), bd(b_re), bd(b_im), cd(c_re), cd(c_im)


def _scan_tables(a_re, a_im, reverse):
    pows = [(a_re, a_im)]
    for _ in range(7):
        pr, pi = pows[-1]
        pows.append((pr * a_re - pi * a_im, pr * a_im + pi * a_re))
    order = list(range(7, -1, -1)) if reverse else list(range(8))
    z = jnp.zeros_like(a_re)
    rows = [pows[0][0], pows[0][1], pows[1][0], pows[1][1], pows[3][0], pows[3][1], z, z]
    rows += [pows[k][0] for k in order] + [pows[k][1] for k in order]
    return jnp.concatenate(rows, axis=0)


def _block_scan(er, ei, tab_ref, cr, ci, reverse):
    rows = lax.broadcasted_iota(jnp.int32, er.shape, 0)
    for idx, s in enumerate((1, 2, 4)):
        if reverse:
            sr, si, keep = pltpu.roll(er, 8 - s, axis=0), pltpu.roll(ei, 8 - s, axis=0), rows < 8 - s
        else:
            sr, si, keep = pltpu.roll(er, s, axis=0), pltpu.roll(ei, s, axis=0), rows >= s
        sr, si = jnp.where(keep, sr, 0.0), jnp.where(keep, si, 0.0)
        ar, ai = tab_ref[2 * idx:2 * idx + 1, :], tab_ref[2 * idx + 1:2 * idx + 2, :]
        er, ei = er + ar * sr - ai * si, ei + ar * si + ai * sr
    pr, pi_ = tab_ref[8:16, :], tab_ref[16:24, :]
    er, ei = er + pr * cr - pi_ * ci, ei + pr * ci + pi_ * cr
    return er, ei


def s5_scan_fwd(bu_re, bu_im, a_re, a_im, f_re, f_im, name):
    L, P = bu_re.shape
    tl = _rtile(L, 512)
    tab = _scan_tables(a_re, a_im, False)
    fvec = jnp.concatenate([f_re, f_im] + [jnp.zeros_like(f_re)] * 6, axis=0)

    def body(br_ref, bi_ref, tab_ref, f_ref, xr_ref, xi_ref, c_ref):
        @pl.when(pl.program_id(0) == 0)
        def _():
            c_ref[...] = jnp.zeros_like(c_ref)

        def blk(i, carry):
            cr, ci = carry
            rows = pl.ds(pl.multiple_of(i * 8, 8), 8)
            br, bi = br_ref[rows, :], bi_ref[rows, :]
            fr, fi = f_ref[0:1, :], f_ref[1:2, :]
            er, ei = _block_scan(fr * br - fi * bi, fr * bi + fi * br, tab_ref, cr, ci, False)
            xr_ref[rows, :] = er
            xi_ref[rows, :] = ei
            return er[7:8, :], ei[7:8, :]

        cr, ci = lax.fori_loop(0, tl // 8, blk, (c_ref[0:1, :], c_ref[1:2, :]))
        c_ref[0:1, :] = cr
        c_ref[1:2, :] = ci

    big = pl.BlockSpec((tl, P), lambda i: (i, 0))
    out = jax.ShapeDtypeStruct((L, P), F32)
    return pl.pallas_call(
        body, name=name, grid=(L // tl,),
        in_specs=[big, big, pl.BlockSpec((24, P), lambda i: (0, 0)), pl.BlockSpec((8, P), lambda i: (0, 0))],
        out_specs=(big, big), out_shape=(out, out), scratch_shapes=[pltpu.VMEM((8, P), F32)],
        compiler_params=_cparams("arbitrary"))(bu_re, bu_im, tab, fvec)


def s5_scan_bwd(dx_re, dx_im, x_re, x_im, bu_re, bu_im, a_re, a_im, f_re, f_im, name):
    L, P = dx_re.shape
    tl = _rtile(L, 256)
    nt = L // tl
    tab = _scan_tables(a_re, -a_im, True)
    fvec = jnp.concatenate([f_re, f_im] + [jnp.zeros_like(f_re)] * 6, axis=0)

    def body(gr_ref, gi_ref, xr_ref, xi_ref, br_ref, bi_ref, tab_ref, f_ref, dbr_ref, dbi_ref, s_ref, c_ref):
        @pl.when(pl.program_id(0) == 0)
        def _():
            c_ref[...] = jnp.zeros_like(c_ref)
            s_ref[...] = jnp.zeros_like(s_ref)

        def blk(k, carry):
            cr, ci = carry
            i = tl // 8 - 1 - k
            rows = pl.ds(pl.multiple_of(i * 8, 8), 8)
            er, ei = _block_scan(gr_ref[rows, :], gi_ref[rows, :], tab_ref, cr, ci, True)
            rid = lax.broadcasted_iota(jnp.int32, er.shape, 0)
            sr = jnp.where(rid == 7, cr, pltpu.roll(er, 7, axis=0))
            si = jnp.where(rid == 7, ci, pltpu.roll(ei, 7, axis=0))
            xr, xi = xr_ref[rows, :], xi_ref[rows, :]
            s_ref[0:8, :] += sr * xr + si * xi
            s_ref[8:16, :] += si * xr - sr * xi
            br, bi = br_ref[rows, :], bi_ref[rows, :]
            s_ref[16:24, :] += er * br + ei * bi
            s_ref[24:32, :] += ei * br - er * bi
            fr, fi = f_ref[0:1, :], f_ref[1:2, :]
            dbr_ref[rows, :] = fr * er + fi * ei
            dbi_ref[rows, :] = fr * ei - fi * er
            return er[0:1, :], ei[0:1, :]

        cr, ci = lax.fori_loop(0, tl // 8, blk, (c_ref[0:1, :], c_ref[1:2, :]))
        c_ref[0:1, :] = cr
        c_ref[1:2, :] = ci

    big = pl.BlockSpec((tl, P), lambda i: (nt - 1 - i, 0))
    out = jax.ShapeDtypeStruct((L, P), F32)
    return pl.pallas_call(
        body, name=name, grid=(nt,),
        in_specs=[big] * 6 + [pl.BlockSpec((24, P), lambda i: (0, 0)), pl.BlockSpec((8, P), lambda i: (0, 0))],
        out_specs=(big, big, pl.BlockSpec((32, P), lambda i: (0, 0))),
        out_shape=(out, out, jax.ShapeDtypeStruct((32, P), F32)), scratch_shapes=[pltpu.VMEM((8, P), F32)],
        compiler_params=_cparams("arbitrary"))(dx_re, dx_im, x_re, x_im, bu_re, bu_im, tab, fvec)


_GK, _GC = math.sqrt(2.0 / math.pi), 0.044715


def _gelu(y):
    return 0.5 * y * (1.0 + jnp.tanh(_GK * (y + _GC * y * y * y)))


def _dgelu(y):
    t = jnp.tanh(_GK * (y + _GC * y * y * y))
    return 0.5 * (1.0 + t) + 0.5 * y * (1.0 - t * t) * _GK * (1.0 + 3.0 * _GC * y * y)


def s5_out_fwd(x_re, x_im, u, cd_re, cd_im, dskip, glu_w, glu_b, name):
    L = u.shape[0]
    tl = _rtile(L, 512)

    def body(xr_ref, xi_ref, u_ref, cr_ref, ci_ref, d_ref, w_ref, b_ref, y_ref, o_ref):
        y = (jnp.dot(xr_ref[...].astype(BF16), cr_ref[...], preferred_element_type=F32)
             - jnp.dot(xi_ref[...].astype(BF16), ci_ref[...], preferred_element_type=F32)
             + d_ref[...] * u_ref[...])
        y_ref[...] = y
        g = _gelu(y)
        z = jnp.dot(g.astype(BF16), w_ref[...], preferred_element_type=F32) + b_ref[...]
        o_ref[...] = (g * _sigmoid(z)).astype(BF16)

    big = pl.BlockSpec((tl, S5_P), lambda i: (i, 0))
    sm = pl.BlockSpec((tl, S5_W), lambda i: (i, 0))
    full = lambda r, c: pl.BlockSpec((r, c), lambda i: (0, 0))
    return pl.pallas_call(
        body, name=name, grid=(L // tl,),
        in_specs=[big, big, sm, full(S5_P, S5_W), full(S5_P, S5_W), full(1, S5_W), full(S5_W, S5_W), full(1, S5_W)],
        out_specs=(sm, sm),
        out_shape=(jax.ShapeDtypeStruct((L, S5_W), F32), jax.ShapeDtypeStruct((L, S5_W), BF16)),
        compiler_params=_cparams("parallel"))(x_re, x_im, u, cd_re, cd_im, dskip, glu_w, glu_b)


def s5_out_bwd(dout, y, u, x_re, x_im, cd_re, cd_im, dskip, glu_w, glu_b, name, dout_col=0):
    L = u.shape[0]
    tl = _rtile(L, 256)
    nt_dims = (((1,), (1,)), ((), ()))
    tn_dims = (((0,), (0,)), ((), ()))

    def body(do_ref, y_ref, u_ref, xr_ref, xi_ref, cr_ref, ci_ref, d_ref, w_ref, b_ref,
             dxr_ref, dxi_ref, du_ref, dcr_ref, dci_ref, dw_ref, s_ref):
        @pl.when(pl.program_id(0) == 0)
        def _():
            dcr_ref[...] = jnp.zeros_like(dcr_ref)
            dci_ref[...] = jnp.zeros_like(dci_ref)
            dw_ref[...] = jnp.zeros_like(dw_ref)
            s_ref[...] = jnp.zeros_like(s_ref)

        yv, dov = y_ref[...], do_ref[...]
        g = _gelu(yv)
        gb = g.astype(BF16)
        sg = _sigmoid(jnp.dot(gb, w_ref[...], preferred_element_type=F32) + b_ref[...])
        dz = dov * g * sg * (1.0 - sg)
        dzb = dz.astype(BF16)
        dg = dov * sg + lax.dot_general(dzb, w_ref[...], nt_dims, preferred_element_type=F32)
        dw_ref[...] += lax.dot_general(gb, dzb, tn_dims, preferred_element_type=F32)
        dy = dg * _dgelu(yv)
        dyb = dy.astype(BF16)
        s_ref[0:8, :] += _fold8(dy * u_ref[...])
        s_ref[8:16, :] += _fold8(dz)
        du_ref[...] = dy * d_ref[...]
        dxr_ref[...] = lax.dot_general(dyb, cr_ref[...], nt_dims, preferred_element_type=F32)
        dxi_ref[...] = -lax.dot_general(dyb, ci_ref[...], nt_dims, preferred_element_type=F32)
        dcr_ref[...] += lax.dot_general(xr_ref[...].astype(BF16), dyb, tn_dims, preferred_element_type=F32)
        dci_ref[...] -= lax.dot_general(xi_ref[...].astype(BF16), dyb, tn_dims, preferred_element_type=F32)

    big = pl.BlockSpec((tl, S5_P), lambda i: (i, 0))
    sm = pl.BlockSpec((tl, S5_W), lambda i: (i, 0))
    full = lambda r, c: pl.BlockSpec((r, c), lambda i: (0, 0))
    sd = jax.ShapeDtypeStruct
    return pl.pallas_call(
        body, name=name, grid=(L // tl,),
        in_specs=[pl.BlockSpec((tl, S5_W), lambda i: (i, dout_col)), sm, sm, big, big, full(S5_P, S5_W),
                  full(S5_P, S5_W), full(1, S5_W), full(S5_W, S5_W), full(1, S5_W)],
        out_specs=(big, big, sm, full(S5_P, S5_W), full(S5_P, S5_W), full(S5_W, S5_W), full(16, S5_W)),
        out_shape=(sd((L, S5_P), F32), sd((L, S5_P), F32), sd((L, S5_W), F32), sd((S5_P, S5_W), F32),
                   sd((S5_P, S5_W), F32), sd((S5_W, S5_W), F32), sd((16, S5_W), F32)),
        compiler_params=_cparams("arbitrary"))(dout, y, u, x_re, x_im, cd_re, cd_im, dskip, glu_w, glu_b)


def s5_block_fwd(u, params, dskip, glu_w, glu_b, tag):
    a_re, a_im, f_re, f_im, bd_re, bd_im, cd_re, cd_im = params
    bu_re = matmul([(u, bd_re.astype(BF16))], "nn", tag + "_bure")
    bu_im = matmul([(u, bd_im.astype(BF16))], "nn", tag + "_buim")
    x_re, x_im = s5_scan_fwd(bu_re, bu_im, a_re, a_im, f_re, f_im, tag + "_scan")
    y, out = s5_out_fwd(x_re, x_im, u, cd_re.astype(BF16), cd_im.astype(BF16), dskip, glu_w, glu_b, tag + "_out")
    return out, (u, bu_re, bu_im, x_re, x_im, y)


def s5_block_bwd(dout, res, params, dskip, glu_w, glu_b, tag, dout_col=0):
    u, bu_re, bu_im, x_re, x_im, y = res
    a_re, a_im, f_re, f_im, bd_re, bd_im, cd_re, cd_im = params
    dxr, dxi, du, dcr, dci, dglu_w, sums = s5_out_bwd(dout, y, u, x_re, x_im, cd_re.astype(BF16), cd_im.astype(BF16),
                                                      dskip, glu_w, glu_b, tag + "_dout", dout_col=dout_col)
    dbr, dbi, acc = s5_scan_bwd(dxr, dxi, x_re, x_im, bu_re, bu_im, a_re, a_im, f_re, f_im, tag + "_dscan")
    du = du + matmul([(dbr, bd_re.astype(BF16)), (dbi, bd_im.astype(BF16))], "nt", tag + "_du")
    dbd_re = matmul([(u, dbr)], "tn", tag + "_dbdre")
    dbd_im = matmul([(u, dbi)], "tn", tag + "_dbdim")
    acc = acc.reshape(4, 8, S5_P).sum(axis=1)
    s = sums.reshape(2, 8, S5_W).sum(axis=1)
    cot = (acc[0:1], acc[1:2], acc[2:3], acc[3:4], dbd_re, dbd_im, dcr, dci)
    return du, cot, dict(dskip=s[0], glu_w=dglu_w, glu_b=s[1])


DN_Z0, DN_NT = 18, 18
REC_U0, REC_A0 = 3072, 3328


def rec_cols_permute(w):
    return jnp.concatenate([w[..., S5_W:REC_A0], w[..., :S5_W], w[..., REC_A0:]], axis=-1)


def rec_cols_restore(w):
    return jnp.concatenate([w[..., REC_U0:REC_A0], w[..., :REC_U0], w[..., REC_A0:]], axis=-1)


DN_W = DN_H * DN_DK


def _dn_conv4(taps, w_ref):
    xc = w_ref[3:4, :] * taps[0]
    for k in range(1, 4):
        xc = xc + w_ref[3 - k:4 - k, :] * taps[k]
    return xc


def dn_prep_fwd(rin, cw, name):
    L = rin.shape[0]
    tl = _rtile(L, 256)
    hb = tl // 8

    def body(x_ref, h_ref, w_ref, o_ref):
        j = pl.program_id(0)
        first = pl.program_id(1) == 0
        x, h = x_ref[...], h_ref[...]
        s = _silu(_dn_conv4([x] + [_shift_down(x, h, k, first) for k in range(1, 4)], w_ref))
        scale = jnp.where(j == 0, DN_DK ** -0.5, 1.0)
        for hd in _HEADS:
            cs = slice(hd * 128, (hd + 1) * 128)
            sh = s[:, cs]
            r = lax.rsqrt(jnp.sum(sh * sh, axis=-1, keepdims=True) + EPS)
            o_ref[:, cs] = jnp.where(j < 2, sh * r * scale, sh)

    main = pl.BlockSpec((tl, DN_W), lambda j, i: (i, j))
    halo = pl.BlockSpec((8, DN_W), lambda j, i: (jnp.maximum(i * hb - 1, 0), j))
    return pl.pallas_call(
        body, name=name, grid=(3, L // tl),
        in_specs=[main, halo, pl.BlockSpec((4, DN_W), lambda j, i: (0, j))],
        out_specs=main, out_shape=jax.ShapeDtypeStruct((L, 3 * DN_W), F32),
        compiler_params=_cparams("parallel", "parallel"))(rin, rin, cw)


def dn_prep_bwd(rin, cw, dout, name):
    L = rin.shape[0]
    tl = _rtile(L, 256)
    hb = tl // 8
    nrt = L // tl

    def body(x_ref, h_ref, w_ref, d_ref, dx_ref, s_ref, c_ref):
        j = pl.program_id(0)
        i = pl.program_id(1)
        first = i == nrt - 1

        @pl.when(i == 0)
        def _():
            s_ref[...] = jnp.zeros_like(s_ref)
            c_ref[...] = jnp.zeros_like(c_ref)

        x, h = x_ref[...], h_ref[...]
        taps = [x] + [_shift_down(x, h, k, first) for k in range(1, 4)]
        xc = _dn_conv4(taps, w_ref)
        s = _silu(xc)
        scale = jnp.where(j == 0, DN_DK ** -0.5, 1.0)
        pieces = []
        for hd in _HEADS:
            cs = slice(hd * 128, (hd + 1) * 128)
            sh, d = s[:, cs], d_ref[:, cs]
            r = lax.rsqrt(jnp.sum(sh * sh, axis=-1, keepdims=True) + EPS)
            n = sh * r
            dn = d * scale
            pieces.append(jnp.where(j < 2, r * (dn - n * jnp.sum(dn * n, axis=-1, keepdims=True)), d))
        dxc = jnp.concatenate(pieces, axis=1) * _dsilu(xc)
        nxt = c_ref[...]
        dx_ref[...] = _dn_conv4([dxc] + [_shift_up(dxc, nxt, k) for k in range(1, 4)], w_ref).astype(BF16)
        c_ref[...] = dxc[0:8, :]
        for k in range(4):
            s_ref[8 * (3 - k):8 * (3 - k) + 8, :] += _fold8(dxc * taps[k])

    rev = lambda i: nrt - 1 - i
    main = pl.BlockSpec((tl, DN_W), lambda j, i: (rev(i), j))
    halo = pl.BlockSpec((8, DN_W), lambda j, i: (jnp.maximum(rev(i) * hb - 1, 0), j))
    return pl.pallas_call(
        body, name=name, grid=(3, nrt),
        in_specs=[main, halo, pl.BlockSpec((4, DN_W), lambda j, i: (0, j)), main],
        out_specs=(main, pl.BlockSpec((32, DN_W), lambda j, i: (0, j))),
        out_shape=(jax.ShapeDtypeStruct((L, 3 * DN_W), BF16), jax.ShapeDtypeStruct((32, 3 * DN_W), F32)),
        scratch_shapes=[pltpu.VMEM((8, DN_W), F32)],
        compiler_params=_cparams("parallel", "arbitrary"))(rin, rin, cw, dout)


_HI = lax.Precision.HIGH
_NT = (((1,), (1,)), ((), ()))
_TN = (((0,), (0,)), ((), ()))
_HEADS = tuple(range(DN_H))


def _mm(a, b, dims=(((1,), (0,)), ((), ())), hi=False):
    if hi:
        return lax.dot_general(a, b, dims, precision=_HI, preferred_element_type=F32)
    return lax.dot_general(a.astype(BF16), b.astype(BF16), dims, preferred_element_type=F32)


def _dn_masks():
    ri = lax.broadcasted_iota(jnp.int32, (DN_C, DN_C), 0)
    ci = lax.broadcasted_iota(jnp.int32, (DN_C, DN_C), 1)
    return ri >= ci, ri > ci, (ri == ci).astype(F32)


def _dn_decay(gc, gr, causal):
    gam = [jnp.where(causal, jnp.exp(jnp.where(causal, gc[h] - gr[h], 0.0)), 0.0) for h in _HEADS]
    eg = [jnp.exp(gc[h]) for h in _HEADS]
    el = [jnp.exp(gc[h][DN_C - 1:DN_C, :] - gc[h]) for h in _HEADS]
    gl = [jnp.exp(gc[h][DN_C - 1:DN_C, :]) for h in _HEADS]
    return gam, eg, el, gl


def _dn_solve(k, v, beta, gam, eg, kk, strict, eye):
    nmat = [jnp.where(strict, beta[h] * kk[h] * gam[h], 0.0) for h in _HEADS]
    t = [eye - nmat[h] for h in _HEADS]
    m = [_mm(nmat[h], nmat[h], hi=True) for h in _HEADS]
    for step in range(5):
        t = [t[h] + _mm(t[h], m[h], hi=True) for h in _HEADS]
        if step < 4:
            m = [_mm(m[h], m[h], hi=True) for h in _HEADS]
    rhs = [jnp.concatenate([v[h] * beta[h], k[h] * (beta[h] * eg[h])], axis=1) for h in _HEADS]
    sol = [_mm(t[h], rhs[h], hi=True) for h in _HEADS]
    return t, sol


def dn_chunk_fwd(qkv, gcol, grow, bcol, name, comm=None):
    L = qkv.shape[0]
    C, W = DN_C, DN_H * DN_DK
    ncb = 8
    tl = ncb * C
    nchunks = L // C

    def body(q_ref, k_ref, v_ref, gc_ref, gr_ref, b_ref, o_ref, sh_ref, t_ref, sol_ref, s_ref):
        @pl.when(pl.program_id(0) == 0)
        def _():
            s_ref[...] = jnp.zeros_like(s_ref)

        causal, strict, eye = _dn_masks()

        def chunk(c, _):
            rows = pl.ds(pl.multiple_of(c * C, C), C)
            grow_c = gr_ref[c]
            hs = lambda h: slice(h * 128, (h + 1) * 128)
            q = [q_ref[rows, hs(h)] for h in _HEADS]
            k = [k_ref[rows, hs(h)] for h in _HEADS]
            v = [v_ref[rows, hs(h)] for h in _HEADS]
            gc = [gc_ref[rows, h:h + 1] for h in _HEADS]
            gr = [grow_c[h:h + 1, :] for h in _HEADS]
            beta = [b_ref[rows, h:h + 1] for h in _HEADS]
            gam, eg, el, gl = _dn_decay(gc, gr, causal)
            kk = [_mm(k[h], k[h], _NT) for h in _HEADS]
            t, sol = _dn_solve(k, v, beta, gam, eg, kk, strict, eye)
            qk = [_mm(q[h], k[h], _NT) * gam[h] for h in _HEADS]
            S = [s_ref[hs(h), :] for h in _HEADS]
            vn = [sol[h][:, :128] - _mm(sol[h][:, 128:], S[h]) for h in _HEADS]
            o = [_mm(q[h] * eg[h], S[h]) + _mm(qk[h], vn[h]) for h in _HEADS]
            Sn = [S[h] * gl[h] + _mm(k[h] * el[h], vn[h], _TN) for h in _HEADS]
            for h in _HEADS:
                sh_ref[c, hs(h), :] = S[h]
                s_ref[hs(h), :] = Sn[h]
                o_ref[rows, hs(h)] = o[h]
                t_ref[rows, h * C:(h + 1) * C] = t[h]
                sol_ref[rows, h * 256:(h + 1) * 256] = sol[h]
            return 0

        lax.fori_loop(0, ncb, chunk, 0)

    col = lambda b: pl.BlockSpec((tl, W), lambda i: (i, b))
    small = pl.BlockSpec((tl, 8), lambda i: (i, 0))
    rowblk = lambda w: pl.BlockSpec((tl, w), lambda i: (i, 0))
    sd = jax.ShapeDtypeStruct
    return _call(body, (qkv, qkv, qkv, gcol, grow, bcol), name=name, grid=(L // tl,),
                 in_specs=[col(0), col(1), col(2), small, pl.BlockSpec((ncb, 8, C), lambda i: (i, 0, 0)), small],
                 out_specs=(rowblk(W), pl.BlockSpec((ncb, W, 128), lambda i: (i, 0, 0)), rowblk(DN_H * C),
                            rowblk(DN_H * 256)),
                 out_shape=(sd((L, W), F32), sd((nchunks, W, 128), F32), sd((L, DN_H * C), F32),
                            sd((L, DN_H * 256), F32)),
                 scratch_shapes=[pltpu.VMEM((W, 128), F32)], sem=("arbitrary",), comm=comm)


def dn_chunk_bwd(qkv, gcol, grow, bcol, shist, thist, solhist, do, name, comm=None):
    L = qkv.shape[0]
    C, W = DN_C, DN_H * DN_DK
    ncb = 8
    tl = ncb * C
    nchunks = L // C
    nt = L // tl

    def body(q_ref, k_ref, v_ref, gc_ref, gr_ref, b_ref, sh_ref, t_ref, sol_ref, do_ref,
             dqkv_ref, dgc_ref, dgr_ref, db_ref, ds_ref):
        @pl.when(pl.program_id(0) == 0)
        def _():
            ds_ref[...] = jnp.zeros_like(ds_ref)

        lane8 = lax.broadcasted_iota(jnp.int32, (C, 8), 1)
        sub8 = lax.broadcasted_iota(jnp.int32, (8, C), 0)
        rowid = lax.broadcasted_iota(jnp.int32, (C, 1), 0)
        causal, strict, _ = _dn_masks()
        rsum = lambda a: jnp.sum(a, axis=1, keepdims=True)

        def chunk(cc, _):
            c = ncb - 1 - cc
            rows = pl.ds(pl.multiple_of(c * C, C), C)
            grow_c = gr_ref[c]
            hs = lambda h: slice(h * 128, (h + 1) * 128)
            q = [q_ref[rows, hs(h)] for h in _HEADS]
            k = [k_ref[rows, hs(h)] for h in _HEADS]
            v = [v_ref[rows, hs(h)] for h in _HEADS]
            gc = [gc_ref[rows, h:h + 1] for h in _HEADS]
            gr = [grow_c[h:h + 1, :] for h in _HEADS]
            beta = [b_ref[rows, h:h + 1] for h in _HEADS]
            t = [t_ref[rows, h * C:(h + 1) * C] for h in _HEADS]
            sol = [sol_ref[rows, h * 256:(h + 1) * 256] for h in _HEADS]
            S = [sh_ref[c, hs(h), :] for h in _HEADS]
            dS = [ds_ref[hs(h), :] for h in _HEADS]
            dov = [do_ref[rows, hs(h)] for h in _HEADS]
            gam, eg, el, gl = _dn_decay(gc, gr, causal)
            kk = [_mm(k[h], k[h], _NT) for h in _HEADS]
            qk_raw = [_mm(q[h], k[h], _NT) for h in _HEADS]
            w = [sol[h][:, 128:] for h in _HEADS]
            kd = [k[h] * el[h] for h in _HEADS]
            vn = [sol[h][:, :128] - _mm(w[h], S[h]) for h in _HEADS]
            dvn = [_mm(qk_raw[h] * gam[h], dov[h], _TN) + _mm(kd[h], dS[h]) for h in _HEADS]
            dqd = [_mm(dov[h], S[h], _NT) for h in _HEADS]
            dqk = [jnp.where(causal, _mm(dov[h], vn[h], _NT), 0.0) for h in _HEADS]
            dkd = [_mm(vn[h], dS[h], _NT) for h in _HEADS]
            dgl = [jnp.sum(rsum(dS[h] * S[h]), axis=0, keepdims=True) for h in _HEADS]
            dw = [-_mm(dvn[h], S[h], _NT) for h in _HEADS]
            dSn = [dS[h] * gl[h] + _mm(q[h] * eg[h], dov[h], _TN) - _mm(w[h], dvn[h], _TN) for h in _HEADS]
            drhs = [_mm(t[h], jnp.concatenate([dvn[h], dw[h]], axis=1), _TN, hi=True) for h in _HEADS]
            dn = [jnp.where(strict, -_mm(drhs[h], sol[h], _NT, hi=True), 0.0) for h in _HEADS]
            dgc_all = jnp.zeros((C, 8), F32)
            db_all = jnp.zeros((C, 8), F32)
            dgr_all = jnp.zeros((8, C), F32)
            for h in _HEADS:
                drv, drk = drhs[h][:, :128], drhs[h][:, 128:]
                t2 = rsum(drk * k[h])
                x = dn[h] * gam[h]
                dbeta = rsum(drv * v[h]) + t2 * eg[h] + rsum(x * kk[h])
                dkk = x * beta[h]
                draw = dqk[h] * gam[h]
                mm_ = (dn[h] * beta[h] * kk[h] + dqk[h] * qk_raw[h]) * gam[h]
                deg = t2 * beta[h] + rsum(dqd[h] * q[h])
                r_ = rsum(dkd[h] * k[h]) * el[h]
                dglast = jnp.sum(r_, axis=0, keepdims=True) + dgl[h] * gl[h]
                dgc = rsum(mm_) + deg * eg[h] - r_ + jnp.where(rowid == C - 1, dglast, 0.0)
                dgr = -jnp.sum(mm_, axis=0, keepdims=True)
                dqkv_ref[rows, hs(h)] = _mm(draw, k[h]) + dqd[h] * eg[h]
                dqkv_ref[rows, hs(DN_H + h)] = (drk * (beta[h] * eg[h]) + _mm(dkk, k[h]) + _mm(dkk, k[h], _TN)
                                                + _mm(draw, q[h], _TN) + dkd[h] * el[h])
                dqkv_ref[rows, hs(2 * DN_H + h)] = drv * beta[h]
                ds_ref[hs(h), :] = dSn[h]
                dgc_all = dgc_all + jnp.where(lane8 == h, dgc, 0.0)
                db_all = db_all + jnp.where(lane8 == h, dbeta, 0.0)
                dgr_all = dgr_all + jnp.where(sub8 == h, dgr, 0.0)
            dgc_ref[rows, :] = dgc_all
            db_ref[rows, :] = db_all
            dgr_ref[c] = dgr_all
            return 0

        lax.fori_loop(0, ncb, chunk, 0)

    rev = lambda i: nt - 1 - i
    col = lambda b: pl.BlockSpec((tl, W), lambda i: (rev(i), b))
    rowblk = lambda w: pl.BlockSpec((tl, w), lambda i: (rev(i), 0))
    small = pl.BlockSpec((tl, 8), lambda i: (rev(i), 0))
    g3 = pl.BlockSpec((ncb, 8, C), lambda i: (rev(i), 0, 0))
    sd = jax.ShapeDtypeStruct
    return _call(body, (qkv, qkv, qkv, gcol, grow, bcol, shist, thist, solhist, do), name=name, grid=(nt,),
                 in_specs=[col(0), col(1), col(2), small, g3, small,
                           pl.BlockSpec((ncb, W, 128), lambda i: (rev(i), 0, 0)), rowblk(DN_H * C),
                           rowblk(DN_H * 256), col(0)],
                 out_specs=(rowblk(3 * W), small, g3, small),
                 out_shape=(sd((L, 3 * W), F32), sd((L, 8), F32), sd((nchunks, 8, C), F32), sd((L, 8), F32)),
                 scratch_shapes=[pltpu.VMEM((W, 128), F32)], sem=("arbitrary",), comm=comm)


def dn_out_fwd(o, rin, nw, name):
    L = o.shape[0]
    tl = _rtile(L, 256)

    def body(o_ref, z_ref, w_ref, y_ref):
        for hd in _HEADS:
            cs = slice(hd * 128, (hd + 1) * 128)
            ov = o_ref[:, cs]
            r = lax.rsqrt(jnp.mean(ov * ov, axis=-1, keepdims=True) + EPS)
            y_ref[:, cs] = (ov * r * w_ref[...] * _silu(z_ref[:, cs])).astype(BF16)

    return pl.pallas_call(
        body, name=name, grid=(L // tl,),
        in_specs=[pl.BlockSpec((tl, DN_W), lambda i: (i, 0)), pl.BlockSpec((tl, DN_W), lambda i: (i, 3)),
                  pl.BlockSpec((1, 128), lambda i: (0, 0))],
        out_specs=pl.BlockSpec((tl, DN_W), lambda i: (i, 0)), out_shape=jax.ShapeDtypeStruct((L, DN_W), BF16),
        compiler_params=_cparams("parallel"))(o, rin, nw)


def dn_out_bwd(dycat, o, rin, nw, name):
    L = o.shape[0]
    tl = _rtile(L, 256)

    def body(dy_ref, o_ref, z_ref, w_ref, do_ref, dz_ref, s_ref):
        @pl.when(pl.program_id(0) == 0)
        def _():
            s_ref[...] = jnp.zeros_like(s_ref)

        for hd in _HEADS:
            cs = slice(hd * 128, (hd + 1) * 128)
            ov, zv, d = o_ref[:, cs], z_ref[:, cs], dy_ref[:, cs]
            r = lax.rsqrt(jnp.mean(ov * ov, axis=-1, keepdims=True) + EPS)
            n = ov * r
            dnw = d * _silu(zv)
            dz_ref[:, cs] = (d * n * w_ref[...] * _dsilu(zv)).astype(BF16)
            dn = dnw * w_ref[...]
            do_ref[:, cs] = r * (dn - n * jnp.mean(dn * n, axis=-1, keepdims=True))
            s_ref[:, cs] += _fold8(dnw * n)

    own = pl.BlockSpec((tl, DN_W), lambda i: (i, 0))
    sd = jax.ShapeDtypeStruct
    return pl.pallas_call(
        body, name=name, grid=(L // tl,),
        in_specs=[own, own, pl.BlockSpec((tl, DN_W), lambda i: (i, 3)), pl.BlockSpec((1, 128), lambda i: (0, 0))],
        out_specs=(own, own, pl.BlockSpec((8, DN_W), lambda i: (0, 0))),
        out_shape=(sd((L, DN_W), F32), sd((L, DN_W), BF16), sd((8, DN_W), F32)),
        compiler_params=_cparams("arbitrary"))(dycat, o, rin, nw)


def dn_gates(a, beta_raw, a_log, dt_bias):
    L = a.shape[0]
    beta = jax.nn.sigmoid(beta_raw)
    g = -jnp.exp(a_log) * jax.nn.softplus(a + dt_bias)
    G = jnp.cumsum(g.reshape(L // DN_C, DN_C, DN_H), axis=1)
    pad = lambda t: jnp.pad(t, ((0, 0), (0, 8 - DN_H)))
    gcol = pad(G.reshape(L, DN_H))
    grow = jnp.pad(jnp.transpose(G, (0, 2, 1)), ((0, 0), (0, 8 - DN_H), (0, 0)))
    return gcol, grow, pad(beta)


def dn_block_fwd(rin, cw, a_log, dt_bias, out_norm, tag, comm=None):
    gates, gates_vjp = jax.vjp(dn_gates, rin[:, REC_A0:REC_A0 + DN_H], rin[:, REC_A0 + DN_H:REC_IN], a_log, dt_bias)
    qkv = dn_prep_fwd(rin, cw, tag + "_prep")
    (o, shist, thist, solhist), got = _with_comm(dn_chunk_fwd(qkv, *gates, tag + "_chunk", comm=comm), comm)
    yd = dn_out_fwd(o, rin, out_norm.reshape(1, 128), tag + "_onorm")
    return yd, (qkv, gates, gates_vjp, o, shist, thist, solhist), got


def dn_block_bwd(dyd, res, rin, cw, out_norm, tag, comm=None):
    qkv, gates, gates_vjp, o, shist, thist, solhist = res
    do, dz, nsum = dn_out_bwd(dyd, o, rin, out_norm.reshape(1, 128), tag + "_donorm")
    (dqkv, dgc, dgr, db), got = _with_comm(dn_chunk_bwd(qkv, *gates, shist, thist, solhist, do, tag + "_dchunk",
                                                        comm=comm), comm)
    da, dbraw, g_alog, g_dtb = gates_vjp((dgc, dgr, db))
    dx, csum = dn_prep_bwd(rin, cw, dqkv, tag + "_dprep")
    grads = dict(conv=csum.reshape(4, 8, DN_NT * 128).sum(axis=1), a_log=g_alog, dt_bias=g_dtb,
                 out_norm=nsum.sum(axis=0).reshape(DN_H, 128).sum(axis=0))
    return dx, dz, da, dbraw, grads, got


_HBM = pl.BlockSpec(memory_space=pltpu.HBM)


def _mesh_pos():
    xi, yi, ci = lax.axis_index("x"), lax.axis_index("y"), lax.axis_index("c")
    return xi, yi, ci, 4 * xi + 2 * yi + ci


def _peer(xi, yi, ci, k):
    px = 1 - xi if (k >> 2) & 1 else xi
    py = 1 - yi if (k >> 1) & 1 else yi
    pc = 1 - ci if k & 1 else ci
    return (px, py, pc), 4 * px + 2 * py + pc


def _exchange(xs, gather, name):
    n = len(xs)

    def body(*refs):
        copies = _comm_copies(refs[:n], refs[n:2 * n], *refs[2 * n:], gather)
        for cp in copies:
            cp.start()
        for cp in copies:
            cp.wait()

    return pl.pallas_call(
        body, name=name, in_specs=[_HBM] * n, out_specs=tuple([_HBM] * n),
        out_shape=_comm_out_shapes(xs), scratch_shapes=_comm_sems(n))(*xs)


def _comm_out_shapes(xs):
    return tuple(jax.ShapeDtypeStruct((N_DEV,) + x.shape[-2:], x.dtype) for x in xs)


def _comm_sems(n):
    return [pltpu.SemaphoreType.DMA((n * (N_DEV - 1),)), pltpu.SemaphoreType.DMA((n * (N_DEV - 1),)),
            pltpu.SemaphoreType.DMA((n,))]


def _comm_copies(x_refs, o_refs, send_sems, recv_sems, lsems, gather):
    xi, yi, ci, me = _mesh_pos()
    copies = []
    for t in range(len(x_refs)):
        src_of = (lambda lin, t=t: x_refs[t]) if gather else (lambda lin, t=t: x_refs[t].at[lin])
        copies.append(pltpu.make_async_copy(src_of(me), o_refs[t].at[me], lsems.at[t]))
        for k in range(1, N_DEV):
            peer, lin = _peer(xi, yi, ci, k)
            s = t * (N_DEV - 1) + k - 1
            copies.append(pltpu.make_async_remote_copy(
                src_ref=src_of(lin), dst_ref=o_refs[t].at[me], send_sem=send_sems.at[s],
                recv_sem=recv_sems.at[s], device_id=peer, device_id_type=pl.DeviceIdType.MESH))
    return copies


def _call(body, args, *, name, grid, in_specs, out_specs, out_shape, scratch_shapes=(), sem, comm=None):
    if comm is None:
        return pl.pallas_call(body, name=name, grid=grid, in_specs=in_specs, out_specs=out_specs,
                              out_shape=out_shape, scratch_shapes=list(scratch_shapes),
                              compiler_params=_cparams(*sem))(*args)
    xs, gather = comm
    n = len(xs)
    single = not isinstance(out_shape, (tuple, list))
    outs_shape = (out_shape,) if single else tuple(out_shape)
    outs_specs = (out_specs,) if single else tuple(out_specs)
    n_in, n_out, n_scr = len(in_specs), len(outs_shape), len(scratch_shapes)

    def body2(*refs):
        ins, cx = refs[:n_in], refs[n_in:n_in + n]
        outs = refs[n_in + n:n_in + n + n_out]
        co = refs[n_in + n + n_out:n_in + 2 * n + n_out]
        scr = refs[n_in + 2 * n + n_out:n_in + 2 * n + n_out + n_scr]
        sems = refs[n_in + 2 * n + n_out + n_scr:]
        first = functools.reduce(jnp.logical_and, [pl.program_id(a) == 0 for a in range(len(grid))])
        last = functools.reduce(jnp.logical_and, [pl.program_id(a) == grid[a] - 1 for a in range(len(grid))])

        @pl.when(first)
        def _():
            for cp in _comm_copies(cx, co, *sems, gather):
                cp.start()

        body(*ins, *outs, *scr)

        @pl.when(last)
        def _():
            for cp in _comm_copies(cx, co, *sems, gather):
                cp.wait()

    res = pl.pallas_call(
        body2, name=name, grid=grid, in_specs=list(in_specs) + [_HBM] * n,
        out_specs=outs_specs + tuple([_HBM] * n), out_shape=outs_shape + _comm_out_shapes(xs),
        scratch_shapes=list(scratch_shapes) + _comm_sems(n),
        compiler_params=_cparams(*(["arbitrary"] * len(grid))))(*args, *xs)
    main = res[0] if single else tuple(res[:n_out])
    return main, list(res[n_out:])


def all_gather(x, name):
    return _exchange([x], True, name)[0]


def all_gather_many(xs, name):
    return _exchange(xs, True, name)


def all_to_all_many(xs, name):
    return _exchange(xs, False, name)


def reduce_adamw(gsrc, w, m, v, name, comm=None):
    parts = list(gsrc) if isinstance(gsrc, (list, tuple)) else [gsrc]
    S, R0, C = parts[0].shape
    R = R0 * len(parts)
    tr = _rtile(R0, max(16, min(256, (4 << 20) // (S * C * 4) // 16 * 16)), 16 if R0 % 16 == 0 else 8)
    n0 = R0 // tr
    c1 = 1.0 - ADAM_B1 ** ADAM_STEP
    c2 = 1.0 - ADAM_B2 ** ADAM_STEP

    def body(*refs):
        g_refs = refs[:len(parts)]
        w_ref, m_ref, v_ref, go_ref, d_ref, mo_ref, vo_ref = refs[len(parts):]
        for p, g_ref in enumerate(g_refs):
            @pl.when(pl.program_id(0) // n0 == p)
            def _(g_ref=g_ref):
                acc = g_ref[0].astype(F32)
                for s in range(1, S):
                    acc = acc + g_ref[s].astype(F32)
                go_ref[...] = acc
        g = go_ref[...]
        mn = ADAM_B1 * m_ref[...] + (1.0 - ADAM_B1) * g
        vn = ADAM_B2 * v_ref[...] + (1.0 - ADAM_B2) * (g * g)
        mo_ref[...] = mn
        vo_ref[...] = vn
        d_ref[...] = -ADAM_LR * ((mn / c1) / (jnp.sqrt(vn / c2) + ADAM_EPS) + ADAM_WD * w_ref[...])

    big = pl.BlockSpec((tr, C), lambda i: (i, 0))
    o = jax.ShapeDtypeStruct((R, C), F32)
    part_spec = lambda p: pl.BlockSpec((S, tr, C), lambda i: (0, jnp.clip(i - p * n0, 0, n0 - 1), 0))
    return _call(body, (*parts, w, m, v), name=name, grid=(R // tr,),
                 in_specs=[part_spec(p) for p in range(len(parts))] + [big, big, big],
                 out_specs=(big, big, big, big), out_shape=(o, o, o, o), sem=("parallel",), comm=comm)


def _to_slabs(g, ax):
    shp = g.shape
    g = g.reshape(shp[:ax] + (N_DEV, shp[ax] // N_DEV) + shp[ax + 1:])
    return jnp.moveaxis(g, ax, 0).reshape(N_DEV, -1)


def _from_slabs(s, ax, shp):
    s = s.reshape((N_DEV,) + shp[:ax] + (shp[ax] // N_DEV,) + shp[ax + 1:])
    return jnp.moveaxis(s, 0, ax).reshape(shp)


def _pack_rows(flat, width, row_mult):
    n = flat.shape[-1]
    per = width * row_mult
    tot = -(-n // per) * per
    flat = jnp.pad(flat, [(0, 0)] * (flat.ndim - 1) + [(0, tot - n)])
    return flat.reshape(flat.shape[:-1] + (tot // width, width))


def _offsets(sizes):
    offs, o = [], 0
    for s in sizes:
        offs.append(o)
        o += s
    return offs


WEIGHTS = ['ada_w', 'ada_b', 'norm_mix', 'norm_ffn', 'attn_w_in', 'attn_q_norm_a', 'attn_k_norm_a', 'attn_q_norm_b',
           'attn_k_norm_b', 'attn_sinks', 'attn_w_out', 'rec_w_in', 's5_lambda_re', 's5_lambda_im', 's5_log_dt',
           's5_b_re', 's5_b_im', 's5_c_re', 's5_c_im', 's5_d', 's5_glu_w', 's5_glu_b', 'dn_conv', 'dn_a_log',
           'dn_dt_bias', 'dn_out_norm', 'rec_w_out', 'ffn_w_up', 'ffn_conv', 'ffn_w_down']
BIG = [('attn_w_in', (D, ATTN_IN // N_DEV)), ('attn_w_out', (D // N_DEV, D)), ('rec_w_in', (D // N_DEV, REC_PAD)),
       ('s5_glu_w', (S5_W // N_DEV, S5_W)), ('rec_w_out', (D // N_DEV, D)), ('ffn_w_up', (2 * D, 2 * D_FF // N_DEV)),
       ('ffn_w_down', (2 * D_FF // N_DEV, D))]


def _shard2d(name, t):
    if name == 'rec_w_in':
        return jnp.pad(t[0], ((0, 0), (0, REC_PAD - REC_IN)))
    return t.reshape((-1, t.shape[-1]))


def _cols_to_slabs(g, k=N_DEV):
    r, n = g.shape
    return jnp.transpose(g.reshape(r, k, n // k), (1, 0, 2))


def _slabs_to_cols(s):
    k, r, c_ = s.shape
    return jnp.transpose(s, (1, 0, 2)).reshape(r, k * c_)
SMALL_SHARDED = [('s5_d', 1, (1, S5_W)), ('s5_glu_b', 1, (1, S5_W)), ('dn_conv', 2, (1, 4, 2304)),
                 ('ffn_conv', 2, (2, 3, 2 * D_FF))]
REPLICATED = [('ada_b', (2, 6 * D)), ('norm_mix', (2, D)), ('norm_ffn', (2, D)), ('attn_q_norm_a', (1, HD)),
              ('attn_k_norm_a', (1, HD)), ('attn_q_norm_b', (1, HD)), ('attn_k_norm_b', (1, HD)),
              ('attn_sinks', (1, 8)), ('s5_lambda_re', (1, 16, 64)), ('s5_lambda_im', (1, 16, 64)),
              ('s5_log_dt', (1, 16)), ('s5_b_re', (1, 16, 64, 16)), ('s5_b_im', (1, 16, 64, 16)),
              ('s5_c_re', (1, 16, 16, 64)), ('s5_c_im', (1, 16, 16, 64)), ('dn_a_log', (1, DN_H)),
              ('dn_dt_bias', (1, DN_H)), ('dn_out_norm', (1, 128))]


def _numel(shp):
    return int(np.prod(shp))


def kernel(x, c, ada_w, ada_b, norm_mix, norm_ffn, attn_w_in, attn_q_norm_a, attn_k_norm_a, attn_q_norm_b, attn_k_norm_b, attn_sinks, attn_w_out, rec_w_in, s5_lambda_re, s5_lambda_im, s5_log_dt, s5_b_re, s5_b_im, s5_c_re, s5_c_im, s5_d, s5_glu_w, s5_glu_b, dn_conv, dn_a_log, dn_dt_bias, dn_out_norm, rec_w_out, ffn_w_up, ffn_conv, ffn_w_down, loss_target, m_ada_w, m_ada_b, m_norm_mix, m_norm_ffn, m_attn_w_in, m_attn_q_norm_a, m_attn_k_norm_a, m_attn_q_norm_b, m_attn_k_norm_b, m_attn_sinks, m_attn_w_out, m_rec_w_in, m_s5_lambda_re, m_s5_lambda_im, m_s5_log_dt, m_s5_b_re, m_s5_b_im, m_s5_c_re, m_s5_c_im, m_s5_d, m_s5_glu_w, m_s5_glu_b, m_dn_conv, m_dn_a_log, m_dn_dt_bias, m_dn_out_norm, m_rec_w_out, m_ffn_w_up, m_ffn_conv, m_ffn_w_down, v_ada_w, v_ada_b, v_norm_mix, v_norm_ffn, v_attn_w_in, v_attn_q_norm_a, v_attn_k_norm_a, v_attn_q_norm_b, v_attn_k_norm_b, v_attn_sinks, v_attn_w_out, v_rec_w_in, v_s5_lambda_re, v_s5_lambda_im, v_s5_log_dt, v_s5_b_re, v_s5_b_im, v_s5_c_re, v_s5_c_im, v_s5_d, v_s5_glu_w, v_s5_glu_b, v_dn_conv, v_dn_a_log, v_dn_dt_bias, v_dn_out_norm, v_rec_w_out, v_ffn_w_up, v_ffn_conv, v_ffn_w_down):
    loc = locals()
    W = {n: loc[n] for n in WEIGHTS}
    M = {n: loc["m_" + n] for n in WEIGHTS}
    V = {n: loc["v_" + n] for n in WEIGHTS}
    _, _, _, me = _mesh_pos()
    L = x.shape[1]
    x0, tgt = x[0], loss_target[0]

    small_in = jnp.concatenate([c.reshape(-1)] + [W[n].reshape(-1) for n, _, _ in SMALL_SHARDED])
    si, att_in_all, att_out_all = all_gather_many(
        [_pack_rows(small_in, 1024, 8), attn_w_in[0].astype(BF16), attn_w_out[0].astype(BF16)], "gather_first")
    si = si.reshape(N_DEV, -1)
    c_all = si[:, :D]
    off = D
    small_full = {}
    for n, ax, shp in SMALL_SHARDED:
        k = _numel(shp) // N_DEV
        small_full[n] = _from_slabs(si[:, off:off + k], ax, shp)
        off += k

    cond_all = jax.nn.silu(c_all)
    modp = jnp.concatenate([matmul([(cond_all, ada_w[l].astype(BF16))], "nn", f"ada{l}") for l in range(2)], axis=0)
    modp_all = all_gather(modp, "gather_mod")
    mods = []
    for l in range(2):
        row = lax.dynamic_index_in_dim(modp_all, l * N_DEV + me, axis=1, keepdims=False)
        mod = row.reshape(1, 6 * D) + ada_b[l].reshape(1, 6 * D)
        mods.append([mod[:, i * D:(i + 1) * D] for i in range(6)])

    w_att_in, w_att_out = _slabs_to_cols(att_in_all), att_out_all.reshape(D, D)
    bf = lambda t: t.astype(BF16)
    ffn_shards = [[bf(ffn_w_up[l]), bf(ffn_w_down[l])] for l in range(2)]
    rec_shards = [bf(_shard2d('rec_w_in', rec_w_in)), bf(s5_glu_w[0]), bf(rec_w_out[0])]
    ffn_cw = [small_full['ffn_conv'][l] for l in range(2)]
    dn_cw = small_full['dn_conv'][0]
    s5_dskip, glu_b = small_full['s5_d'], small_full['s5_glu_b']
    row = lambda t: t.reshape(1, -1)

    sh1, sc1, g1, sh2, sc2, g2 = mods[0]
    h1 = gate_norm_fwd(x0, None, None, row(norm_mix[0]), sh1, sc1, "l0_norm1")
    wvec, sinkvec = attn_vectors(attn_q_norm_a[0], attn_k_norm_a[0], attn_q_norm_b[0], attn_k_norm_b[0], attn_sinks[0])
    y0, res_att, got = attention_block_fwd(
        h1, w_att_in, wvec, sinkvec, w_att_out, "att",
        comms={'swa': (ffn_shards[0][:1], True), 1: (ffn_shards[0][1:], True), 4: (rec_shards, True)})
    split_up = lambda up_all: (_slabs_to_cols(up_all[:4]), _slabs_to_cols(up_all[4:]))
    w_up, w_down = [split_up(got['swa'][0])], [got[1][0].reshape(D_FF, D)]
    w_rec_in = rec_cols_permute(got[4][0].reshape(D, REC_PAD))
    glu_w, w_rec_out = got[4][1].reshape(S5_W, S5_W), got[4][2].reshape(D, D)
    w_rec_out = jnp.concatenate([w_rec_out[S5_W:], w_rec_out[:S5_W]], axis=0)
    x1, h2 = gate_norm_fwd(x0, y0, g1, row(norm_ffn[0]), sh2, sc2, "l0_norm2")
    f0, res_f0 = ffn_block_fwd(h2, w_up[0][0], w_up[0][1], ffn_cw[0], w_down[0], "ffn0")
    t1, tc1, tg1, t2, tc2, tg2 = mods[1]
    x2, h3 = gate_norm_fwd(x1, f0, g2, row(norm_mix[1]), t1, tc1, "l1_norm1")
    rin = matmul([(h3, w_rec_in)], "nn", "rec_in")
    s5p, s5p_vjp = jax.vjp(s5_params, s5_lambda_re[0], s5_lambda_im[0], s5_log_dt[0], s5_b_re[0], s5_b_im[0],
                           s5_c_re[0], s5_c_im[0])
    u = rin[:, REC_U0:REC_A0]
    yc, res_s5 = s5_block_fwd(u, s5p, s5_dskip, glu_w, glu_b, "s5")
    yd, res_dn, got_ffn1 = dn_block_fwd(rin, dn_cw, dn_a_log[0], dn_dt_bias[0], dn_out_norm[0], "dn",
                                        comm=(ffn_shards[1], True))
    w_up.append(split_up(got_ffn1[0]))
    w_down.append(got_ffn1[1].reshape(D_FF, D))
    ycat = jnp.concatenate([yd, yc], axis=1)
    y1 = matmul([(ycat, w_rec_out)], "nn", "rec_out")
    x3, h4 = gate_norm_fwd(x2, y1, tg1, row(norm_ffn[1]), t2, tc2, "l1_norm2")
    f1, res_f1 = ffn_block_fwd(h4, w_up[1][0], w_up[1][1], ffn_cw[1], w_down[1], "ffn1")
    dx4, df1, lsum = final_loss(x3, f1, tg2, tgt, "loss")

    G = {}
    d_tg2 = lsum[8:16].sum(axis=0)
    dh4, gf1, _ = ffn_block_bwd(df1, res_f1, w_up[1][0], w_up[1][1], ffn_cw[1], w_down[1], "ffn1")
    ffn_slabs = lambda g: [g['w_up'], g['w_down'].reshape(N_DEV, D_FF // N_DEV, D)]
    dx3, dy1, s = gate_norm_bwd(x3, y1, tg1, row(norm_ffn[1]), tc2, dx4, dh4, "l1_dnorm2")
    s = s.reshape(4, 8, D).sum(axis=1)
    d_tg1, d_nffn1, d_t2, d_tc2 = s[0], s[1] * (1.0 + tc2[0]), s[2], s[1] * norm_ffn[1]
    g_rec_out = matmul([(ycat, dy1)], "tn", "rec_out_dw", out_dtype=BF16)
    g_rec_out = jnp.concatenate([g_rec_out[DN_W:], g_rec_out[:DN_W]], axis=0).reshape(N_DEV, D // N_DEV, D)
    dycat = matmul([(dy1, w_rec_out)], "nt", "rec_out_dx")
    du, s5cot, gs5 = s5_block_bwd(dycat, res_s5, s5p, s5_dskip, glu_w, glu_b, "s5", dout_col=DN_W // S5_W)
    s5g = s5p_vjp(s5cot)
    dqkv, dz, da, dbraw, gdn, recv_ffn1 = dn_block_bwd(dycat, res_dn, rin, dn_cw, dn_out_norm[0], "dn",
                                                       comm=(ffn_slabs(gf1), False))
    d_rest = jnp.concatenate([du.astype(BF16), da.astype(BF16), dbraw.astype(BF16),
                              jnp.zeros((L, REC_PAD - REC_IN), BF16)], axis=1)
    drin = ((dqkv, 0), (dz, 3 * DN_W), (d_rest, REC_U0))
    g_rec_in = jnp.concatenate([matmul([(h3, p)], "tn", f"rec_in_dw{i}", out_dtype=BF16)
                                for i, (p, _) in enumerate(drin)], axis=1)
    g_rec_in = rec_cols_restore(g_rec_in).reshape(N_DEV, D // N_DEV, REC_PAD)
    g_glu = gs5['glu_w'].astype(BF16).reshape(N_DEV, S5_W // N_DEV, S5_W)
    dh3 = matmul([(p, w_rec_in[:, c0:c0 + p.shape[1]]) for p, c0 in drin], "nt", "rec_in_dx")
    dx2, df0, s = gate_norm_bwd(x2, f0, g2, row(norm_mix[1]), tc1, dx3, dh3, "l1_dnorm1")
    s = s.reshape(4, 8, D).sum(axis=1)
    d_g2, d_nmix1, d_t1, d_tc1 = s[0], s[1] * (1.0 + tc1[0]), s[2], s[1] * norm_mix[1]
    dh2, gf0, recv_rec = ffn_block_bwd(df0, res_f0, w_up[0][0], w_up[0][1], ffn_cw[0], w_down[0], "ffn0",
                                       comm=([g_rec_in, g_glu, g_rec_out], False))
    dx1, dy0, s = gate_norm_bwd(x1, y0, g1, row(norm_ffn[0]), sc2, dx2, dh2, "l0_dnorm2")
    s = s.reshape(4, 8, D).sum(axis=1)
    d_g1, d_nffn0, d_sh2, d_sc2 = s[0], s[1] * (1.0 + sc2[0]), s[2], s[1] * norm_ffn[0]
    dh1, gatt, got_b = attention_block_bwd(dy0, res_att, w_att_in, wvec, sinkvec, w_att_out, "att",
                                           comms={'swa': (ffn_slabs(gf0)[:1], False), 1: (ffn_slabs(gf0)[1:], False)},
                                           send_w_out_on=4)
    recv_ffn0 = [got_b['swa'][0], got_b[1][0]]
    (grad_x, s), recv_w_in = gate_norm_bwd(x0, None, None, row(norm_mix[0]), sc1, dx1, dh1, "l0_dnorm1",
                                           comm=([_cols_to_slabs(gatt['w_in'])], False))
    recv_att = [recv_w_in[0], got_b[4][0]]
    s = s.reshape(4, 8, D).sum(axis=1)
    d_nmix0, d_sh1, d_sc1 = s[1] * (1.0 + sc1[0]), s[2], s[1] * norm_mix[0]
    dmod = jnp.stack([jnp.concatenate([d_sh1, d_sc1, d_g1, d_sh2, d_sc2, d_g2]),
                      jnp.concatenate([d_t1, d_tc1, d_tg1, d_t2, d_tc2, d_tg2])])

    P = {'ada_b': dmod, 'norm_mix': jnp.stack([d_nmix0, d_nmix1]), 'norm_ffn': jnp.stack([d_nffn0, d_nffn1]),
         'attn_q_norm_a': gatt['q_norm_a'], 'attn_k_norm_a': gatt['k_norm_a'], 'attn_q_norm_b': gatt['q_norm_b'],
         'attn_k_norm_b': gatt['k_norm_b'], 'attn_sinks': gatt['sinks'],
         's5_lambda_re': s5g[0], 's5_lambda_im': s5g[1], 's5_log_dt': s5g[2], 's5_b_re': s5g[3], 's5_b_im': s5g[4],
         's5_c_re': s5g[5], 's5_c_im': s5g[6], 'dn_a_log': gdn['a_log'], 'dn_dt_bias': gdn['dt_bias'],
         'dn_out_norm': gdn['out_norm'],
         's5_d': gs5['dskip'], 's5_glu_b': gs5['glu_b'], 'dn_conv': gdn['conv'],
         'ffn_conv': jnp.stack([gf0['conv'], gf1['conv']])}

    out = {k: {} for k in ("g", "d", "m", "v")}
    keys = ("g", "d", "m", "v")
    recv = {'attn_w_in': recv_att[0], 'attn_w_out': recv_att[1], 'rec_w_in': recv_rec[0], 's5_glu_w': recv_rec[1],
            'rec_w_out': recv_rec[2]}
    for n, gr_ in recv.items():
        res4 = reduce_adamw(gr_, _shard2d(n, W[n]), _shard2d(n, M[n]), _shard2d(n, V[n]), "adamw_" + n)
        for key, t in zip(keys, res4):
            out[key][n] = (t[:, :REC_IN] if n == 'rec_w_in' else t).reshape(W[n].shape)
    rep_sizes = [_numel(shp) for _, shp in REPLICATED]
    ss_sizes = [_numel(shp) for _, _, shp in SMALL_SHARDED]
    rep_offs = _offsets(rep_sizes + ss_sizes + [1])
    parts = [P[n].reshape(-1) for n, _ in REPLICATED] + [P[n].reshape(-1) for n, _, _ in SMALL_SHARDED]
    parts.append(lsum[0:8].sum().reshape(1))
    spack = _pack_rows(jnp.concatenate(parts), 1024, 8)
    flat2d = lambda t: t.reshape(-1, t.shape[-1])
    sall = None
    for n, idx in (('ffn_w_up', 0), ('ffn_w_down', 1)):
        comm = ([spack], True) if sall is None else None
        res4, got_s = _with_comm(reduce_adamw([recv_ffn0[idx], recv_ffn1[idx]], flat2d(W[n]), flat2d(M[n]),
                                              flat2d(V[n]), "adamw_" + n, comm=comm), comm)
        if got_s is not None:
            sall = got_s[0]
        for key, t in zip(keys, res4):
            out[key][n] = t.reshape(W[n].shape)
    n_rest = sum(ss_sizes) + 1
    pk = lambda d: _pack_rows(jnp.concatenate([d[n].reshape(-1) for n, _ in REPLICATED]
                                              + [jnp.zeros((n_rest,), F32)]), 1024, 8)
    sg, sd_, sm, sv = [t.reshape(-1) for t in reduce_adamw(sall, pk(W), pk(M), pk(V), "adamw_small")]
    loss = 0.5 * sg[rep_offs[-1]] / D

    dmod_all = sall.reshape(N_DEV, -1)[:, :2 * 6 * D].reshape(N_DEV, 2, 6 * D)
    dmod_mine = lax.dynamic_slice_in_dim(dmod_all, me * (6 * D // N_DEV), 6 * D // N_DEV, axis=2)
    g_ada = [matmul([(cond_all, dmod_mine[:, l])], "tn", f"ada{l}_dw")[None] for l in range(2)]
    ada2d = lambda t: t.reshape(2 * D, 6 * D // N_DEV)
    for key, t in zip(("g", "d", "m", "v"), reduce_adamw(g_ada, ada2d(ada_w), ada2d(m_ada_w),
                                                          ada2d(v_ada_w), "adamw_ada_w")):
        out[key]['ada_w'] = t.reshape(ada_w.shape)
    own = []
    for (n, ax, shp), o in zip(SMALL_SHARDED, rep_offs[len(REPLICATED):]):
        slabs = _to_slabs(sg[o:o + _numel(shp)].reshape(shp), ax)
        own.append(lax.dynamic_index_in_dim(slabs, me, axis=0, keepdims=False))
    own_names = [n for n, _, _ in SMALL_SHARDED]
    pk = lambda d: _pack_rows(jnp.concatenate([d[n].reshape(-1) for n in own_names]), 1024, 8)
    og, od, om, ov = [t.reshape(-1) for t in reduce_adamw(_pack_rows(jnp.concatenate(own), 1024, 8)[None],
                                                          pk(W), pk(M), pk(V), "adamw_own")]

    def unpack(names_shapes, bufs):
        o = 0
        for n, shp in names_shapes:
            k = _numel(shp)
            for key, buf in zip(("g", "d", "m", "v"), bufs):
                out[key][n] = buf[o:o + k].reshape(shp)
            o += k

    unpack(REPLICATED, (sg, sd_, sm, sv))
    unpack([(n, W[n].shape) for n in own_names], (og, od, om, ov))
    return (loss, grad_x[None], *[out["g"][n] for n in WEIGHTS], *[out["d"][n] for n in WEIGHTS],
            *[out["m"][n] for n in WEIGHTS], *[out["v"][n] for n in WEIGHTS])
```

```python
import functools
import math

import numpy as np
import jax
import jax.numpy as jnp
from jax import lax
from jax.experimental import pallas as pl
from jax.experimental.pallas import tpu as pltpu

F32 = jnp.float32
BF16 = jnp.bfloat16

N_DEV = 8
D = 1024
HD = 64
BLK = 128
ATTN_IN = 2304
CB = ATTN_IN // 128
B_BRANCHES = ((128, 1), (512, 4), (2048, 16))
S5_W = 256
S5_P = 1024
DN_H = 6
DN_DK = 128
DN_C = 64
REC_IN = 3340
REC_PAD = 3456
D_FF = 2816
EPS = 1e-6
ADAM_LR, ADAM_B1, ADAM_B2, ADAM_EPS, ADAM_WD, ADAM_STEP = 0.001, 0.9, 0.999, 1e-8, 0.01, 10
VMEM_LIMIT = 48 * 1024 * 1024

ALIBI = np.asarray(2.0 ** (-8.0 * np.arange(1, 17) / 16), dtype=np.float32)


def _cparams(*sem):
    return pltpu.CompilerParams(dimension_semantics=tuple(sem), vmem_limit_bytes=VMEM_LIMIT)


def _tile(n, target):
    if n <= target:
        return n
    best = None
    for t in range(128, target + 1, 128):
        if n % t == 0:
            best = t
    assert best is not None, (n, target)
    return best


def _rtile(n, target, mult=8):
    if n <= target:
        return n
    best = None
    for t in range(mult, target + 1, mult):
        if n % t == 0:
            best = t
    assert best is not None, (n, target)
    return best


def _fold8(x):
    r, c = x.shape
    return x.reshape(r // 8, 8, c).sum(axis=0)


def _sigmoid(x):
    return 1.0 / (1.0 + jnp.exp(-x))


_DIMS = {"nn": (((1,), (0,)), ((), ())), "nt": (((1,), (1,)), ((), ())), "tn": (((0,), (0,)), ((), ()))}


MM_FULL_K = 3584


MM_VMEM_BUDGET = 40 << 20


def matmul(pairs, mode, name, out_dtype=F32, tm=1024, tn=1536, tk=1024):
    a0, b0 = pairs[0]
    if mode == "nn":
        (M, K), N = a0.shape, b0.shape[1]
    elif mode == "nt":
        (M, K), N = a0.shape, b0.shape[0]
    else:
        (K, M), N = a0.shape, b0.shape[1]
        tm = 1536
    tn = _tile(N, tn)
    tk = K if K <= MM_FULL_K else _tile(K, tk)
    nk = K // tk
    npair = len(pairs)
    dims = _DIMS[mode]
    kdim = 0 if mode == "tn" else 1
    tks = [a.shape[kdim] for a, _ in pairs]
    assert all(t == K for t in tks) or (nk == 1 and max(tks) <= MM_FULL_K), tks
    if nk > 1:
        tks = [tk] * npair

    def planned(tm_):
        ab = sum(tm_ * t * a.dtype.itemsize + t * tn * b.dtype.itemsize for (a, b), t in zip(pairs, tks))
        return 2 * ab + 2 * tm_ * tn * jnp.dtype(out_dtype).itemsize + (tm_ * tn * 4 if nk > 1 else 0)

    while True:
        tm_try = _rtile(M, tm) if M % 128 else _tile(M, tm)
        if planned(tm_try) <= MM_VMEM_BUDGET or tm <= 128:
            break
        tm //= 2
    tm = tm_try

    def body(*refs):
        o_ref = refs[2 * npair]
        tot = None
        for p in range(npair):
            part = lax.dot_general(refs[2 * p][...].astype(BF16), refs[2 * p + 1][...].astype(BF16),
                                   dims, preferred_element_type=F32)
            tot = part if tot is None else tot + part
        if nk == 1:
            o_ref[...] = tot.astype(o_ref.dtype)
            return
        acc_ref = refs[2 * npair + 1]
        k = pl.program_id(2)

        @pl.when(k == 0)
        def _():
            acc_ref[...] = tot

        @pl.when(k > 0)
        def _():
            acc_ref[...] += tot

        @pl.when(k == nk - 1)
        def _():
            o_ref[...] = acc_ref[...].astype(o_ref.dtype)

    def specs(t):
        if mode == "nn":
            return [pl.BlockSpec((tm, t), lambda j, i, k: (i, k)), pl.BlockSpec((t, tn), lambda j, i, k: (k, j))]
        if mode == "nt":
            return [pl.BlockSpec((tm, t), lambda j, i, k: (i, k)), pl.BlockSpec((tn, t), lambda j, i, k: (j, k))]
        return [pl.BlockSpec((t, tm), lambda j, i, k: (k, i)), pl.BlockSpec((t, tn), lambda j, i, k: (k, j))]

    flat = [t for pr in pairs for t in pr]
    return pl.pallas_call(
        body, name=name, grid=(N // tn, M // tm, nk),
        in_specs=[s for t in tks for s in specs(t)],
        out_specs=pl.BlockSpec((tm, tn), lambda j, i, k: (i, j)),
        out_shape=jax.ShapeDtypeStruct((M, N), out_dtype),
        scratch_shapes=[pltpu.VMEM((tm, tn), F32)] if nk > 1 else [],
        compiler_params=_cparams("parallel", "parallel", "arbitrary"),
    )(*flat)


def gate_norm_fwd(x, y, gate, nw, sh, sc, name):
    L, C = x.shape
    tl = _rtile(L, 512)
    has_gate = y is not None

    def body(*refs):
        if has_gate:
            x_ref, y_ref, g_ref, nw_ref, sh_ref, sc_ref, xn_ref, h_ref = refs
            xn = x_ref[...] + g_ref[...] * y_ref[...]
            xn_ref[...] = xn
        else:
            x_ref, nw_ref, sh_ref, sc_ref, h_ref = refs
            xn = x_ref[...]
        r = lax.rsqrt(jnp.mean(xn * xn, axis=-1, keepdims=True) + EPS)
        h = (xn * r * nw_ref[...]) * (1.0 + sc_ref[...]) + sh_ref[...]
        h_ref[...] = h.astype(BF16)

    big = pl.BlockSpec((tl, C), lambda i: (i, 0))
    vec = pl.BlockSpec((1, C), lambda i: (0, 0))
    if has_gate:
        ins, in_specs = (x, y, gate, nw, sh, sc), [big, big, vec, vec, vec, vec]
        out_shape = (jax.ShapeDtypeStruct((L, C), F32), jax.ShapeDtypeStruct((L, C), BF16))
        out_specs = (big, big)
    else:
        ins, in_specs = (x, nw, sh, sc), [big, vec, vec, vec]
        out_shape = jax.ShapeDtypeStruct((L, C), BF16)
        out_specs = big
    return pl.pallas_call(body, name=name, grid=(L // tl,), in_specs=in_specs, out_specs=out_specs,
                          out_shape=out_shape, compiler_params=_cparams("parallel"))(*ins)


def gate_norm_bwd(xn, y, gate, nw, sc, dxn_direct, dh, name, comm=None):
    L, C = xn.shape
    tl = _rtile(L, 256)
    has_gate = y is not None
    has_direct = dxn_direct is not None

    def body(*refs):
        refs = list(refs)
        xn_ref = refs.pop(0)
        y_ref = refs.pop(0) if has_gate else None
        g_ref = refs.pop(0) if has_gate else None
        nw_ref = refs.pop(0)
        sc_ref = refs.pop(0)
        dd_ref = refs.pop(0) if has_direct else None
        dh_ref = refs.pop(0)
        dxn_ref = refs.pop(0)
        dy_ref = refs.pop(0) if has_gate else None
        sums_ref = refs.pop(0)

        @pl.when(pl.program_id(0) == 0)
        def _():
            sums_ref[...] = jnp.zeros_like(sums_ref)

        xv = xn_ref[...]
        dh_v = dh_ref[...]
        r = lax.rsqrt(jnp.mean(xv * xv, axis=-1, keepdims=True) + EPS)
        n = xv * r
        a = nw_ref[...] * (1.0 + sc_ref[...])
        dn = dh_v * a
        dx = r * (dn - n * jnp.mean(dn * n, axis=-1, keepdims=True))
        if has_direct:
            dx = dx + dd_ref[...]
        dxn_ref[...] = dx
        sums_ref[8:16, :] += _fold8(dh_v * n)
        sums_ref[16:24, :] += _fold8(dh_v)
        if has_gate:
            dy_ref[...] = (dx * g_ref[...]).astype(BF16)
            sums_ref[0:8, :] += _fold8(dx * y_ref[...])

    big = pl.BlockSpec((tl, C), lambda i: (i, 0))
    vec = pl.BlockSpec((1, C), lambda i: (0, 0))
    ins, in_specs = [xn], [big]
    if has_gate:
        ins += [y, gate]
        in_specs += [big, vec]
    ins += [nw, sc]
    in_specs += [vec, vec]
    if has_direct:
        ins.append(dxn_direct)
        in_specs.append(big)
    ins.append(dh)
    in_specs.append(big)
    out_shape = [jax.ShapeDtypeStruct((L, C), F32)]
    out_specs = [big]
    if has_gate:
        out_shape.append(jax.ShapeDtypeStruct((L, C), BF16))
        out_specs.append(big)
    out_shape.append(jax.ShapeDtypeStruct((32, C), F32))
    out_specs.append(pl.BlockSpec((32, C), lambda i: (0, 0)))
    return _call(body, ins, name=name, grid=(L // tl,), in_specs=in_specs, out_specs=tuple(out_specs),
                 out_shape=tuple(out_shape), sem=("arbitrary",), comm=comm)


def final_loss(x, f, gate, target, name):
    L, C = x.shape
    tl = _rtile(L, 256)

    def body(x_ref, f_ref, g_ref, t_ref, dy_ref, df_ref, sums_ref):
        @pl.when(pl.program_id(0) == 0)
        def _():
            sums_ref[...] = jnp.zeros_like(sums_ref)

        fv = f_ref[...]
        err = x_ref[...] + g_ref[...] * fv - t_ref[...]
        dy = err * (1.0 / C)
        dy_ref[...] = dy
        df_ref[...] = (dy * g_ref[...]).astype(BF16)
        sums_ref[0:8, :] += _fold8(err * err)
        sums_ref[8:16, :] += _fold8(dy * fv)

    big = pl.BlockSpec((tl, C), lambda i: (i, 0))
    vec = pl.BlockSpec((1, C), lambda i: (0, 0))
    return pl.pallas_call(
        body, name=name, grid=(L // tl,), in_specs=[big, big, vec, big],
        out_specs=(big, big, pl.BlockSpec((16, C), lambda i: (0, 0))),
        out_shape=(jax.ShapeDtypeStruct((L, C), F32), jax.ShapeDtypeStruct((L, C), BF16),
                   jax.ShapeDtypeStruct((16, C), F32)),
        compiler_params=_cparams("arbitrary"))(x, f, gate, target)


def _seg_ones(seg):
    r = lax.broadcasted_iota(jnp.int32, (128, 128), 0) // seg
    c = lax.broadcasted_iota(jnp.int32, (128, 128), 1) // seg
    return (r == c).astype(BF16)


def _segsum(t, ones):
    hi = t.astype(BF16)
    lo = (t - hi.astype(F32)).astype(BF16)
    return (jnp.dot(hi, ones, preferred_element_type=F32) + jnp.dot(lo, ones, preferred_element_type=F32))


_NORMED_TILES = tuple(list(range(0, 5)) + list(range(6, 14)))


DIL = (4, 16)
B_COLS0, B_W = 768, 1536
DIL_TL = 256


def _to_dilated(scr_ref, out_ref, d, cast=None):
    nj, tl, _ = scr_ref.shape
    for r in range(d):
        for j in range(nj):
            piece = scr_ref[j, pl.ds(r, tl // d, stride=d), :]
            c0 = (r * nj + j) * 128
            out_ref[:, c0:c0 + 128] = piece if cast is None else piece.astype(cast)


def _from_dilated(in_ref, scr_ref, d):
    nj, tl, _ = scr_ref.shape
    for r in range(d):
        for j in range(nj):
            c0 = (r * nj + j) * 128
            scr_ref[j, pl.ds(r, tl // d, stride=d), :] = in_ref[:, c0:c0 + 128]


def _dil_spec(tl, d, width):
    return pl.BlockSpec((tl // d, d * width), lambda i: (i, 0))


def qknorm_fwd(qkv, wvec, name):
    L, C = qkv.shape
    tl = DIL_TL

    def body(x_ref, w_ref, o_ref, o4_ref, o16_ref, scr_ref):
        ones = _seg_ones(HD)
        for t in range(CB):
            cs = slice(t * 128, (t + 1) * 128)
            x = x_ref[:, cs]
            if t in _NORMED_TILES:
                ms = _segsum(x * x, ones) * (1.0 / HD)
                x = x * lax.rsqrt(ms + EPS) * w_ref[:, cs]
            o_ref[:, cs] = x.astype(BF16)
            if t * 128 >= B_COLS0:
                scr_ref[t - B_COLS0 // 128] = x
        _to_dilated(scr_ref, o4_ref, 4, BF16)
        _to_dilated(scr_ref, o16_ref, 16, BF16)

    return pl.pallas_call(
        body, name=name, grid=(L // tl,),
        in_specs=[pl.BlockSpec((tl, C), lambda i: (i, 0)), pl.BlockSpec((1, C), lambda i: (0, 0))],
        out_specs=(pl.BlockSpec((tl, C), lambda i: (i, 0)), _dil_spec(tl, 4, B_W), _dil_spec(tl, 16, B_W)),
        out_shape=(jax.ShapeDtypeStruct((L, C), BF16), jax.ShapeDtypeStruct((L // 4, 4 * B_W), BF16),
                   jax.ShapeDtypeStruct((L // 16, 16 * B_W), BF16)),
        scratch_shapes=[pltpu.VMEM((B_W // 128, tl, 128), F32)], compiler_params=_cparams("parallel"))(qkv, wvec)


def qknorm_bwd(qkv, wvec, d_a, d_b, name):
    L, C = qkv.shape
    tl = DIL_TL

    def body(x_ref, w_ref, dqa, dka, dva, q1, k1, v1, q4, k4, v4, q16, k16, v16, dx_ref, sums_ref,
             dy_ref, s4_ref, s16_ref):
        @pl.when(pl.program_id(0) == 0)
        def _():
            sums_ref[...] = jnp.zeros_like(sums_ref)

        dy_ref[:, 0:512] = dqa[...]
        for off, ref in ((512, dka), (640, dva)):
            for g in range(2):
                acc = ref[:, g * 256:g * 256 + HD]
                for h in range(1, 4):
                    acc = acc + ref[:, g * 256 + h * HD:g * 256 + (h + 1) * HD]
                dy_ref[:, off + g * HD:off + (g + 1) * HD] = acc
        for off, r1, r4, r16 in ((768, q1, q4, q16), (1280, k1, k4, k16), (1792, v1, v4, v16)):
            _from_dilated(r4, s4_ref, 4)
            _from_dilated(r16, s16_ref, 16)
            for j in range(4):
                dy_ref[:, off + j * 128:off + (j + 1) * 128] = r1[:, j * 128:(j + 1) * 128] + s4_ref[j] + s16_ref[j]

        ones = _seg_ones(HD)
        for t in range(CB):
            cs = slice(t * 128, (t + 1) * 128)
            d = dy_ref[:, cs]
            if t in _NORMED_TILES:
                x = x_ref[:, cs]
                r = lax.rsqrt(_segsum(x * x, ones) * (1.0 / HD) + EPS)
                n = x * r
                dn = d * w_ref[:, cs]
                dx_ref[:, cs] = (r * (dn - n * (_segsum(dn * n, ones) * (1.0 / HD)))).astype(BF16)
                sums_ref[:, cs] += _fold8(d * n)
            else:
                dx_ref[:, cs] = d.astype(BF16)

    big = pl.BlockSpec((tl, C), lambda i: (i, 0))
    p512 = pl.BlockSpec((tl, 512), lambda i: (i, 0))
    return pl.pallas_call(
        body, name=name, grid=(L // tl,),
        in_specs=[big, pl.BlockSpec((1, C), lambda i: (0, 0))] + [p512] * 6 + [_dil_spec(tl, 4, 512)] * 3
        + [_dil_spec(tl, 16, 512)] * 3,
        out_specs=(big, pl.BlockSpec((8, C), lambda i: (0, 0))),
        out_shape=(jax.ShapeDtypeStruct((L, C), BF16), jax.ShapeDtypeStruct((8, C), F32)),
        scratch_shapes=[pltpu.VMEM((tl, C), F32), pltpu.VMEM((4, tl, 128), F32), pltpu.VMEM((4, tl, 128), F32)],
        compiler_params=_cparams("arbitrary"))(qkv, wvec, *d_a, *d_b[0], *d_b[1], *d_b[2])


def _attn_biases(t, slopes, step, maxdist):
    qi = lax.broadcasted_iota(jnp.int32, (BLK, 2 * BLK), 0)
    sj = lax.broadcasted_iota(jnp.int32, (BLK, 2 * BLK), 1)
    dist = BLK + qi - sj
    valid = (dist >= 0) & (dist <= maxdist)
    distf = (step * dist).astype(F32)
    inner = [jnp.where(valid, (-sl) * distf, -jnp.inf) for sl in slopes]
    first = [jnp.where((t > 0) | (sj >= BLK), b, -jnp.inf) for b in inner]
    return inner, first


def _attn_scores(q, kw, bias):
    return lax.dot_general(q, kw, (((1,), (1,)), ((), ())), preferred_element_type=F32) + bias


ATT_NQ = 8


def _attn_operands(nq, hp, gqa, q_ref, kh_ref, kc_ref, vh_ref, vc_ref):
    ops = []
    for b in range(nq):
        rows = slice(b * BLK, (b + 1) * BLK)
        prev = slice((b - 1) * BLK, b * BLK)
        for e in range(2):
            cs = slice(e * HD, (e + 1) * HD)
            if gqa:
                ksel = lambda ref, r: jnp.where(hp >= 2, ref[r, 64:128], ref[r, 0:64])
            else:
                ksel = lambda ref, r, cs=cs: ref[r, cs]
            kprev = ksel(kh_ref, slice(0, BLK)) if b == 0 else ksel(kc_ref, prev)
            vprev = ksel(vh_ref, slice(0, BLK)) if b == 0 else ksel(vc_ref, prev)
            ops.append((b, e, rows, cs, q_ref[rows, cs] * (HD ** -0.5),
                        jnp.concatenate([kprev, ksel(kc_ref, rows)], axis=0),
                        jnp.concatenate([vprev, ksel(vc_ref, rows)], axis=0)))
    return ops


def _attn_specs(cb, q_off, k_off, v_off, gqa):
    kcol = (lambda r, hp: r * cb + k_off) if gqa else (lambda r, hp: r * cb + k_off + hp)
    vcol = (lambda r, hp: r * cb + v_off) if gqa else (lambda r, hp: r * cb + v_off + hp)
    return kcol, vcol


def attn_fwd(X, d, q_off, k_off, v_off, gqa, slope0, maxdist, name, comm=None):
    Ls = X.shape[0]
    nq = min(ATT_NQ, Ls // BLK)
    TQ = nq * BLK
    nt = Ls // TQ
    slopes = jnp.asarray(ALIBI)

    def body(sl_ref, q_ref, kh_ref, kc_ref, vh_ref, vc_ref, o_ref, lse_ref):
        hp, t = pl.program_id(1), pl.program_id(2)
        ops = _attn_operands(nq, hp, gqa, q_ref, kh_ref, kc_ref, vh_ref, vc_ref)
        inner, first = _attn_biases(t, [sl_ref[slope0 + 2 * hp + e] for e in range(2)], d, maxdist)
        s = [_attn_scores(q, kw, first[e] if b == 0 else inner[e]) for (b, e, rows, cs, q, kw, vw) in ops]
        m = [jnp.max(x, axis=-1, keepdims=True) for x in s]
        p = [jnp.exp(x - mm) for x, mm in zip(s, m)]
        l = [jnp.sum(x, axis=-1, keepdims=True) for x in p]
        o = [jnp.dot(x.astype(BF16), op[6], preferred_element_type=F32) / ll for x, op, ll in zip(p, ops, l)]
        for (b, e, rows, cs, q, kw, vw), oo, mm, ll in zip(ops, o, m, l):
            o_ref[rows, cs] = oo
            lse_ref[rows, cs] = jnp.broadcast_to(mm + jnp.log(ll), (BLK, HD))

    cb = X.shape[1] // (d * 128)
    kcol, vcol = _attn_specs(cb, q_off, k_off, v_off, gqa)
    tile, blk = (TQ, 128), (BLK, 128)
    halo = lambda t: jnp.maximum(t * nq - 1, 0)
    in_specs = [
        pl.BlockSpec(memory_space=pltpu.SMEM),
        pl.BlockSpec(tile, lambda r, hp, t: (t, r * cb + q_off + hp)),
        pl.BlockSpec(blk, lambda r, hp, t: (halo(t), kcol(r, hp))),
        pl.BlockSpec(tile, lambda r, hp, t: (t, kcol(r, hp))),
        pl.BlockSpec(blk, lambda r, hp, t: (halo(t), vcol(r, hp))),
        pl.BlockSpec(tile, lambda r, hp, t: (t, vcol(r, hp))),
    ]
    out_spec = pl.BlockSpec(tile, lambda r, hp, t: (t, r * 4 + hp))
    out = jax.ShapeDtypeStruct((Ls, d * 512), F32)
    return _call(body, (slopes, X, X, X, X, X), name=name, grid=(d, 4, nt), in_specs=in_specs,
                 out_specs=(out_spec, out_spec), out_shape=(out, out),
                 sem=("parallel", "parallel", "arbitrary"), comm=comm)


def attn_bwd(X, o, lse, do, dlse, d, q_off, k_off, v_off, gqa, slope0, maxdist, name, comm=None):
    Ls = X.shape[0]
    slopes = jnp.asarray(ALIBI)

    nq = min(ATT_NQ, Ls // BLK)
    TQ = nq * BLK
    nt = Ls // TQ
    nt_dims, tn_dims = (((1,), (1,)), ((), ())), (((0,), (0,)), ((), ()))

    def body(sl_ref, q_ref, kh_ref, kc_ref, vh_ref, vc_ref, o_ref, lse_ref, do_ref, dlse_ref,
             dq_ref, dk_ref, dv_ref, ak_ref, av_ref, pk_ref, pv_ref):
        hp, t = pl.program_id(1), pl.program_id(2)

        @pl.when(t == 0)
        def _():
            pk_ref[...] = jnp.zeros_like(pk_ref)
            pv_ref[...] = jnp.zeros_like(pv_ref)

        @pl.when(t < nt)
        def _():
            ops = _attn_operands(nq, hp, gqa, q_ref, kh_ref, kc_ref, vh_ref, vc_ref)
            inner, first = _attn_biases(t, [sl_ref[slope0 + 2 * hp + e] for e in range(2)], d, maxdist)
            sv = [_attn_scores(q, kw, first[e] if b == 0 else inner[e]) for (b, e, rows, cs, q, kw, vw) in ops]
            p = [jnp.exp(s - lse_ref[op[2], op[1] * HD:op[1] * HD + 1]) for s, op in zip(sv, ops)]
            dov = [do_ref[op[2], op[3]] for op in ops]
            delta = [jnp.sum(dd * o_ref[op[2], op[3]], axis=-1, keepdims=True) for dd, op in zip(dov, ops)]
            dob = [dd.astype(BF16) for dd in dov]
            dp = [lax.dot_general(dd, op[6], nt_dims, preferred_element_type=F32) for dd, op in zip(dob, ops)]
            ds = [(pp * (x - dl + dlse_ref[op[2], op[1] * HD:op[1] * HD + 1])).astype(BF16)
                  for pp, x, dl, op in zip(p, dp, delta, ops)]
            dq = [jnp.dot(x, op[5], preferred_element_type=F32) * (HD ** -0.5) for x, op in zip(ds, ops)]
            dkw = [lax.dot_general(x, op[4], tn_dims, preferred_element_type=F32) for x, op in zip(ds, ops)]
            dvw = [lax.dot_general(pp.astype(BF16), dd, tn_dims, preferred_element_type=F32)
                   for pp, dd in zip(p, dob)]
            ak_ref[...] = jnp.zeros_like(ak_ref)
            av_ref[...] = jnp.zeros_like(av_ref)
            for (b, e, rows, cs, q, kw, vw), x, yk, yv in zip(ops, dq, dkw, dvw):
                dq_ref[rows, cs] = x
                ak_ref[b * BLK:(b + 2) * BLK, cs] += yk
                av_ref[b * BLK:(b + 2) * BLK, cs] += yv
            if nt == 1:
                dk_ref[...] = ak_ref[BLK:, :]
                dv_ref[...] = av_ref[BLK:, :]
                return
            last = slice(TQ - BLK, TQ)
            dk_ref[...] = pk_ref[...]
            dv_ref[...] = pv_ref[...]
            dk_ref[last, :] += ak_ref[0:BLK, :]
            dv_ref[last, :] += av_ref[0:BLK, :]
            pk_ref[...] = ak_ref[BLK:, :]
            pv_ref[...] = av_ref[BLK:, :]

        @pl.when(t == nt)
        def _():
            dk_ref[...] = pk_ref[...]
            dv_ref[...] = pv_ref[...]

    cb = X.shape[1] // (d * 128)
    kcol, vcol = _attn_specs(cb, q_off, k_off, v_off, gqa)
    tile, blk = (TQ, 128), (BLK, 128)
    cur = lambda t: jnp.minimum(t, nt - 1)
    halo = lambda t: jnp.maximum(cur(t) * nq - 1, 0)
    ospec = pl.BlockSpec(tile, lambda r, hp, t: (cur(t), r * 4 + hp))
    in_specs = [
        pl.BlockSpec(memory_space=pltpu.SMEM),
        pl.BlockSpec(tile, lambda r, hp, t: (cur(t), r * cb + q_off + hp)),
        pl.BlockSpec(blk, lambda r, hp, t: (halo(t), kcol(r, hp))),
        pl.BlockSpec(tile, lambda r, hp, t: (cur(t), kcol(r, hp))),
        pl.BlockSpec(blk, lambda r, hp, t: (halo(t), vcol(r, hp))),
        pl.BlockSpec(tile, lambda r, hp, t: (cur(t), vcol(r, hp))),
        ospec, ospec, ospec, ospec,
    ]
    shifted = pl.BlockSpec(tile, lambda r, hp, t: (jnp.maximum(t - 1, 0), r * 4 + hp))
    out = jax.ShapeDtypeStruct((Ls, d * 512), F32)
    return _call(body, (slopes, X, X, X, X, X, o, lse, do, dlse), name=name, grid=(d, 4, nt + 1 if nt > 1 else 1),
                 in_specs=in_specs, out_specs=(ospec, shifted, shifted), out_shape=(out, out, out),
                 scratch_shapes=[pltpu.VMEM((TQ + BLK, 128), F32), pltpu.VMEM((TQ + BLK, 128), F32),
                                 pltpu.VMEM((TQ, 128), F32), pltpu.VMEM((TQ, 128), F32)],
                 sem=("parallel", "parallel", "arbitrary"), comm=comm)


def attn_merge_fwd(oa, la, sink, obs, lbs, name):
    L = oa.shape[0]
    tl = DIL_TL

    def body(oa_ref, la_ref, sk_ref, o1, o4, o16, l1, l4, l16, m_ref, so4, so16, sl4, sl16):
        m_ref[:, 0:512] = (oa_ref[...] * _sigmoid(la_ref[...] - sk_ref[...])).astype(BF16)
        for src, dst, d in ((o4, so4, 4), (o16, so16, 16), (l4, sl4, 4), (l16, sl16, 16)):
            _from_dilated(src, dst, d)
        for j in range(4):
            cs = slice(j * 128, (j + 1) * 128)
            a, b, c = l1[:, cs], sl4[j], sl16[j]
            mx = jnp.maximum(jnp.maximum(a, b), c)
            ea, eb, ec = jnp.exp(a - mx), jnp.exp(b - mx), jnp.exp(c - mx)
            inv = 1.0 / (ea + eb + ec)
            m_ref[:, 512 + j * 128:512 + (j + 1) * 128] = (
                (ea * inv) * o1[:, cs] + (eb * inv) * so4[j] + (ec * inv) * so16[j]).astype(BF16)

    big = pl.BlockSpec((tl, 512), lambda i: (i, 0))
    dil = [big, _dil_spec(tl, 4, 512), _dil_spec(tl, 16, 512)]
    return pl.pallas_call(
        body, name=name, grid=(L // tl,),
        in_specs=[big, big, pl.BlockSpec((1, 512), lambda i: (0, 0))] + dil + dil,
        out_specs=pl.BlockSpec((tl, 1024), lambda i: (i, 0)),
        out_shape=jax.ShapeDtypeStruct((L, 1024), BF16), scratch_shapes=[pltpu.VMEM((4, tl, 128), F32)] * 4,
        compiler_params=_cparams("parallel"),
    )(oa, la, sink, *obs, *lbs)


def attn_merge_bwd(dm, oa, la, sink, obs, lbs, name):
    L = oa.shape[0]
    tl = DIL_TL

    def body(dm_ref, oa_ref, la_ref, sk_ref, o1, o4, o16, l1, l4, l16,
             doa_ref, dla_ref, d1, d4, d16, g1, g4, g16, sums_ref, so4, so16, sl4, sl16, sd4, sd16, sg4, sg16):
        @pl.when(pl.program_id(0) == 0)
        def _():
            sums_ref[...] = jnp.zeros_like(sums_ref)

        for src, dst, d in ((o4, so4, 4), (o16, so16, 16), (l4, sl4, 4), (l16, sl16, 16)):
            _from_dilated(src, dst, d)
        ones = _seg_ones(HD)
        for t in range(4):
            cs = slice(t * 128, (t + 1) * 128)
            dma = dm_ref[:, cs]
            keep = _sigmoid(la_ref[:, cs] - sk_ref[:, cs])
            doa_ref[:, cs] = dma * keep
            tt = dma * oa_ref[:, cs] * keep * (1.0 - keep)
            dla_ref[:, cs] = _segsum(tt, ones)
            sums_ref[:, cs] += _fold8(-tt)
            dmb = dm_ref[:, 512 + t * 128:512 + (t + 1) * 128]
            a, b, c = l1[:, cs], sl4[t], sl16[t]
            mx = jnp.maximum(jnp.maximum(a, b), c)
            ea, eb, ec = jnp.exp(a - mx), jnp.exp(b - mx), jnp.exp(c - mx)
            inv = 1.0 / (ea + eb + ec)
            wa, wb, wc = ea * inv, eb * inv, ec * inv
            d1[:, cs] = wa * dmb
            sd4[t] = wb * dmb
            sd16[t] = wc * dmb
            sa = _segsum(dmb * o1[:, cs], ones)
            sb = _segsum(dmb * so4[t], ones)
            sc_ = _segsum(dmb * so16[t], ones)
            mean = wa * sa + wb * sb + wc * sc_
            g1[:, cs] = wa * (sa - mean)
            sg4[t] = wb * (sb - mean)
            sg16[t] = wc * (sc_ - mean)
        for src, dst, d in ((sd4, d4, 4), (sd16, d16, 16), (sg4, g4, 4), (sg16, g16, 16)):
            _to_dilated(src, dst, d)

    big = pl.BlockSpec((tl, 512), lambda i: (i, 0))
    dil = [big, _dil_spec(tl, 4, 512), _dil_spec(tl, 16, 512)]
    sd = jax.ShapeDtypeStruct
    shp = [sd((L, 512), F32), sd((L // 4, 4 * 512), F32), sd((L // 16, 16 * 512), F32)]
    return pl.pallas_call(
        body, name=name, grid=(L // tl,),
        in_specs=[pl.BlockSpec((tl, 1024), lambda i: (i, 0)), big, big,
                  pl.BlockSpec((1, 512), lambda i: (0, 0))] + dil + dil,
        out_specs=tuple([big, big] + dil + dil + [pl.BlockSpec((8, 512), lambda i: (0, 0))]),
        out_shape=tuple([shp[0], shp[0]] + shp + shp + [sd((8, 512), F32)]),
        scratch_shapes=[pltpu.VMEM((4, tl, 128), F32)] * 8, compiler_params=_cparams("arbitrary"),
    )(dm, oa, la, sink, *obs, *lbs)


def _shift_down(x, halo, k, first):
    rows = lax.broadcasted_iota(jnp.int32, (8, x.shape[1]), 0)
    out = pltpu.roll(x, k, axis=0)
    hrows = jnp.where(first, 0.0, pltpu.roll(halo, k, axis=0))
    top = jnp.where(rows < k, hrows, out[0:8, :])
    return jnp.concatenate([top, out[8:, :]], axis=0)


def _shift_up(x, nxt, k):
    tl = x.shape[0]
    rows = lax.broadcasted_iota(jnp.int32, (8, x.shape[1]), 0)
    out = pltpu.roll(x, tl - k, axis=0)
    bottom = jnp.where(rows >= 8 - k, pltpu.roll(nxt, 8 - k, axis=0), out[tl - 8:, :])
    return jnp.concatenate([out[:tl - 8, :], bottom], axis=0)


def _silu(x):
    return x * _sigmoid(x)


def _dsilu(x):
    s = _sigmoid(x)
    return s * (1.0 + x * (1.0 - s))


def ffn_act_fwd(ua, ub, cw, name, comm=None):
    L, F = ua.shape
    tl = _rtile(L, 256)
    tc = _tile(F, 1408)
    hb = tl // 8

    def body(ua_ref, uah_ref, ub_ref, ubh_ref, wa_ref, wb_ref, o_ref):
        first = pl.program_id(1) == 0

        def conv(x_ref, h_ref, w_ref):
            x = x_ref[...]
            h = h_ref[...]
            return (w_ref[2:3, :] * x + w_ref[1:2, :] * _shift_down(x, h, 1, first)
                    + w_ref[0:1, :] * _shift_down(x, h, 2, first))

        a = conv(ua_ref, uah_ref, wa_ref)
        b = conv(ub_ref, ubh_ref, wb_ref)
        o_ref[...] = (_silu(a) * b).astype(BF16)

    main = pl.BlockSpec((tl, tc), lambda j, i: (i, j))
    halo = pl.BlockSpec((8, tc), lambda j, i: (jnp.maximum(i * hb - 1, 0), j))
    wa = pl.BlockSpec((3, tc), lambda j, i: (0, j))
    wb = pl.BlockSpec((3, tc), lambda j, i: (0, j + F // tc))
    return _call(body, (ua, ua, ub, ub, cw, cw), name=name, grid=(F // tc, L // tl),
                 in_specs=[main, halo, main, halo, wa, wb], out_specs=main,
                 out_shape=jax.ShapeDtypeStruct((L, F), BF16), sem=("parallel", "parallel"), comm=comm)


def ffn_act_bwd(ua, ub, cw, dact, name, comm=None):
    L, F = ua.shape
    tl = _rtile(L, 256)
    tc = _tile(F, 1408)
    hb = tl // 8
    nrt = L // tl

    def body(ua_ref, uah_ref, ub_ref, ubh_ref, wa_ref, wb_ref, da_ref, dua_ref, dub_ref, sums_ref, ca_ref, cb_ref):
        i = pl.program_id(1)
        first = i == nrt - 1

        @pl.when(i == 0)
        def _():
            sums_ref[...] = jnp.zeros_like(sums_ref)
            ca_ref[...] = jnp.zeros_like(ca_ref)
            cb_ref[...] = jnp.zeros_like(cb_ref)

        def taps(x_ref, h_ref):
            x = x_ref[...]
            h = h_ref[...]
            return x, _shift_down(x, h, 1, first), _shift_down(x, h, 2, first)

        a0, a1, a2 = taps(ua_ref, uah_ref)
        b0, b1, b2 = taps(ub_ref, ubh_ref)
        a = wa_ref[2:3, :] * a0 + wa_ref[1:2, :] * a1 + wa_ref[0:1, :] * a2
        b = wb_ref[2:3, :] * b0 + wb_ref[1:2, :] * b1 + wb_ref[0:1, :] * b2
        dact_v = da_ref[...]
        dya = dact_v * b * _dsilu(a)
        dyb = dact_v * _silu(a)
        for (dy, w_ref, c_ref, d_ref, xs, base) in ((dya, wa_ref, ca_ref, dua_ref, (a2, a1, a0), 0),
                                                     (dyb, wb_ref, cb_ref, dub_ref, (b2, b1, b0), 24)):
            nxt = c_ref[...]
            d_ref[...] = (w_ref[2:3, :] * dy + w_ref[1:2, :] * _shift_up(dy, nxt, 1)
                          + w_ref[0:1, :] * _shift_up(dy, nxt, 2)).astype(BF16)
            c_ref[...] = dy[0:8, :]
            for j in range(3):
                sums_ref[base + 8 * j:base + 8 * j + 8, :] += _fold8(dy * xs[j])

    rev = lambda i: nrt - 1 - i
    main = pl.BlockSpec((tl, tc), lambda j, i: (rev(i), j))
    halo = pl.BlockSpec((8, tc), lambda j, i: (jnp.maximum(rev(i) * hb - 1, 0), j))
    wa = pl.BlockSpec((3, tc), lambda j, i: (0, j))
    wb = pl.BlockSpec((3, tc), lambda j, i: (0, j + F // tc))
    ob = jax.ShapeDtypeStruct((L, F), BF16)
    return _call(body, (ua, ua, ub, ub, cw, cw, dact), name=name, grid=(F // tc, nrt),
                 in_specs=[main, halo, main, halo, wa, wb, main],
                 out_specs=(main, main, pl.BlockSpec((48, tc), lambda j, i: (0, j))),
                 out_shape=(ob, ob, jax.ShapeDtypeStruct((48, F), F32)),
                 scratch_shapes=[pltpu.VMEM((8, tc), F32), pltpu.VMEM((8, tc), F32)],
                 sem=("parallel", "arbitrary"), comm=comm)


def attn_vectors(qna, kna, qnb, knb, sinks):
    ones = jnp.ones((128,), F32)
    wvec = jnp.concatenate([jnp.tile(qna, 8), jnp.tile(kna, 2), ones, jnp.tile(qnb, 8), jnp.tile(knb, 8),
                            jnp.tile(ones, 4)]).reshape(1, ATTN_IN)
    return wvec, jnp.repeat(sinks, HD).reshape(1, 512)


def _with_comm(result, comm):
    return result if comm is not None else (result, None)


def attention_block_fwd(h, w_in, wvec, sinkvec, w_out, tag, comms=None):
    L = h.shape[0]
    comms = comms or {}
    got = {}
    qkv = matmul([(h, w_in)], "nn", tag + "_qkv")
    X, X4, X16 = qknorm_fwd(qkv, wvec, tag + "_qknorm")
    (oa, la), got['swa'] = _with_comm(attn_fwd(X, 1, 0, 4, 5, True, 0, BLK - 1, tag + "_swa",
                                               comm=comms.get('swa')), comms.get('swa'))
    views = {1: (X, 6, 10, 14), 4: (X4, 0, 4, 8), 16: (X16, 0, 4, 8)}
    obs, lbs = [], []
    for window, d in B_BRANCHES:
        xd, qo, ko, vo = views[d]
        (o, l), got[d] = _with_comm(attn_fwd(xd, d, qo, ko, vo, False, 8, window // d,
                                             tag + f"_dil{d}", comm=comms.get(d)), comms.get(d))
        obs.append(o)
        lbs.append(l)
    m = attn_merge_fwd(oa, la, sinkvec, obs, lbs, tag + "_merge")
    y = matmul([(m, w_out)], "nn", tag + "_out")
    return y, (h, qkv, views, oa, la, obs, lbs, m), got


def attention_block_bwd(dy, res, w_in, wvec, sinkvec, w_out, tag, comms=None, send_w_out_on=None):
    h, qkv, views, oa, la, obs, lbs, m = res
    comms = dict(comms or {})
    got = {}
    g_w_out = matmul([(m, dy)], "tn", tag + "_dwout", out_dtype=BF16)
    if send_w_out_on is not None:
        comms[send_w_out_on] = ([g_w_out.reshape(N_DEV, D // N_DEV, D)], False)
    dm = matmul([(dy, w_out)], "nt", tag + "_dm")
    doa, dla, d1, d2, d3, g1, g2, g3, sinksums = attn_merge_bwd(dm, oa, la, sinkvec, obs, lbs, tag + "_dmerge")
    d_a, got['swa'] = _with_comm(attn_bwd(views[1][0], oa, la, doa, dla, 1, 0, 4, 5, True, 0, BLK - 1,
                                          tag + "_dswa", comm=comms.get('swa')), comms.get('swa'))
    d_b = []
    for (window, d), o, l, do, dl in zip(B_BRANCHES, obs, lbs, (d1, d2, d3), (g1, g2, g3)):
        xd, qo, ko, vo = views[d]
        dqkv_d, got[d] = _with_comm(attn_bwd(xd, o, l, do, dl, d, qo, ko, vo, False, 8, window // d,
                                             tag + f"_ddil{d}", comm=comms.get(d)), comms.get(d))
        d_b.append(dqkv_d)
    dqkv, wsums = qknorm_bwd(qkv, wvec, d_a, d_b, tag + "_dqknorm")
    g_w_in = matmul([(h, dqkv)], "tn", tag + "_dwin", out_dtype=BF16)
    dh = matmul([(dqkv, w_in)], "nt", tag + "_dh")
    ws = wsums.sum(axis=0)
    grads = dict(
        w_in=g_w_in, w_out=g_w_out,
        q_norm_a=ws[0:512].reshape(8, HD).sum(axis=0), k_norm_a=ws[512:640].reshape(2, HD).sum(axis=0),
        q_norm_b=ws[768:1280].reshape(8, HD).sum(axis=0), k_norm_b=ws[1280:1792].reshape(8, HD).sum(axis=0),
        sinks=sinksums.sum(axis=0).reshape(8, HD).sum(axis=1))
    return dh, grads, got


def ffn_block_fwd(h, w_up_a, w_up_b, cw, w_down, tag, comm=None):
    ua = matmul([(h, w_up_a)], "nn", tag + "_upa")
    ub = matmul([(h, w_up_b)], "nn", tag + "_upb")
    act, got = _with_comm(ffn_act_fwd(ua, ub, cw, tag + "_act", comm=comm), comm)
    f = matmul([(act, w_down)], "nn", tag + "_down")
    return f, (h, ua, ub, act), got


def ffn_block_bwd(df, res, w_up_a, w_up_b, cw, w_down, tag, comm=None):
    h, ua, ub, act = res
    g_down = matmul([(act, df)], "tn", tag + "_dwdown", out_dtype=BF16)
    dact = matmul([(df, w_down)], "nt", tag + "_dact")
    (dua, dub, sums), got = _with_comm(ffn_act_bwd(ua, ub, cw, dact, tag + "_dactk", comm=comm), comm)
    g_up = jnp.concatenate([_cols_to_slabs(matmul([(h, dua)], "tn", tag + "_dwupa", out_dtype=BF16), N_DEV // 2),
                            _cols_to_slabs(matmul([(h, dub)], "tn", tag + "_dwupb", out_dtype=BF16), N_DEV // 2)],
                           axis=0)
    dh = matmul([(dua, w_up_a), (dub, w_up_b)], "nt", tag + "_dh")
    s = sums.reshape(2, 3, 8, D_FF).sum(axis=2)
    g_conv = jnp.concatenate([s[0], s[1]], axis=1)
    return dh, dict(w_up=g_up, conv=g_conv, w_down=g_down), got


def s5_params(lam_re, lam_im, log_dt, b_re, b_im, c_re, c_im):
    dt = jnp.exp(log_dt)[:, None]
    mag, ang = jnp.exp(lam_re * dt), lam_im * dt
    a_re, a_im = mag * jnp.cos(ang), mag * jnp.sin(ang)
    nr, ni = a_re - 1.0, a_im
    den = lam_re * lam_re + lam_im * lam_im
    f_re = (nr * lam_re + ni * lam_im) / den
    f_im = (ni * lam_re - nr * lam_im) / den
    eye = jnp.eye(16, dtype=F32)[:, None, :, None]
    bd = lambda b: (eye * jnp.transpose(b, (0, 2, 1))[:, :, None, :]).reshape(S5_W, S5_P)
    cd = lambda c: (eye * jnp.transpose(c, (0, 2, 1))[:, :, None, :]).reshape(S5_P, S5_W)
    flat = lambda t: t.reshape(1, S5_P)
    return flat(a_re), flat(a_im), flat(f_re), flat(f_im), bd(b_re), bd(b_im), cd(c_re), cd(c_im)


def _scan_tables(a_re, a_im, reverse):
    pows = [(a_re, a_im)]
    for _ in range(7):
        pr, pi = pows[-1]
        pows.append((pr * a_re - pi * a_im, pr * a_im + pi * a_re))
    order = list(range(7, -1, -1)) if reverse else list(range(8))
    z = jnp.zeros_like(a_re)
    rows = [pows[0][0], pows[0][1], pows[1][0], pows[1][1], pows[3][0], pows[3][1], z, z]
    rows += [pows[k][0] for k in order] + [pows[k][1] for k in order]
    return jnp.concatenate(rows, axis=0)


def _block_scan(er, ei, tab_ref, cr, ci, reverse):
    rows = lax.broadcasted_iota(jnp.int32, er.shape, 0)
    for idx, s in enumerate((1, 2, 4)):
        if reverse:
            sr, si, keep = pltpu.roll(er, 8 - s, axis=0), pltpu.roll(ei, 8 - s, axis=0), rows < 8 - s
        else:
            sr, si, keep = pltpu.roll(er, s, axis=0), pltpu.roll(ei, s, axis=0), rows >= s
        sr, si = jnp.where(keep, sr, 0.0), jnp.where(keep, si, 0.0)
        ar, ai = tab_ref[2 * idx:2 * idx + 1, :], tab_ref[2 * idx + 1:2 * idx + 2, :]
        er, ei = er + ar * sr - ai * si, ei + ar * si + ai * sr
    pr, pi_ = tab_ref[8:16, :], tab_ref[16:24, :]
    er, ei = er + pr * cr - pi_ * ci, ei + pr * ci + pi_ * cr
    return er, ei


def s5_scan_fwd(bu_re, bu_im, a_re, a_im, f_re, f_im, name):
    L, P = bu_re.shape
    tl = _rtile(L, 512)
    tab = _scan_tables(a_re, a_im, False)
    fvec = jnp.concatenate([f_re, f_im] + [jnp.zeros_like(f_re)] * 6, axis=0)

    def body(br_ref, bi_ref, tab_ref, f_ref, xr_ref, xi_ref, c_ref):
        @pl.when(pl.program_id(0) == 0)
        def _():
            c_ref[...] = jnp.zeros_like(c_ref)

        def blk(i, carry):
            cr, ci = carry
            rows = pl.ds(pl.multiple_of(i * 8, 8), 8)
            br, bi = br_ref[rows, :], bi_ref[rows, :]
            fr, fi = f_ref[0:1, :], f_ref[1:2, :]
            er, ei = _block_scan(fr * br - fi * bi, fr * bi + fi * br, tab_ref, cr, ci, False)
            xr_ref[rows, :] = er
            xi_ref[rows, :] = ei
            return er[7:8, :], ei[7:8, :]

        cr, ci = lax.fori_loop(0, tl // 8, blk, (c_ref[0:1, :], c_ref[1:2, :]))
        c_ref[0:1, :] = cr
        c_ref[1:2, :] = ci

    big = pl.BlockSpec((tl, P), lambda i: (i, 0))
    out = jax.ShapeDtypeStruct((L, P), F32)
    return pl.pallas_call(
        body, name=name, grid=(L // tl,),
        in_specs=[big, big, pl.BlockSpec((24, P), lambda i: (0, 0)), pl.BlockSpec((8, P), lambda i: (0, 0))],
        out_specs=(big, big), out_shape=(out, out), scratch_shapes=[pltpu.VMEM((8, P), F32)],
        compiler_params=_cparams("arbitrary"))(bu_re, bu_im, tab, fvec)


def s5_scan_bwd(dx_re, dx_im, x_re, x_im, bu_re, bu_im, a_re, a_im, f_re, f_im, name):
    L, P = dx_re.shape
    tl = _rtile(L, 256)
    nt = L // tl
    tab = _scan_tables(a_re, -a_im, True)
    fvec = jnp.concatenate([f_re, f_im] + [jnp.zeros_like(f_re)] * 6, axis=0)

    def body(gr_ref, gi_ref, xr_ref, xi_ref, br_ref, bi_ref, tab_ref, f_ref, dbr_ref, dbi_ref, s_ref, c_ref):
        @pl.when(pl.program_id(0) == 0)
        def _():
            c_ref[...] = jnp.zeros_like(c_ref)
            s_ref[...] = jnp.zeros_like(s_ref)

        def blk(k, carry):
            cr, ci = carry
            i = tl // 8 - 1 - k
            rows = pl.ds(pl.multiple_of(i * 8, 8), 8)
            er, ei = _block_scan(gr_ref[rows, :], gi_ref[rows, :], tab_ref, cr, ci, True)
            rid = lax.broadcasted_iota(jnp.int32, er.shape, 0)
            sr = jnp.where(rid == 7, cr, pltpu.roll(er, 7, axis=0))
            si = jnp.where(rid == 7, ci, pltpu.roll(ei, 7, axis=0))
            xr, xi = xr_ref[rows, :], xi_ref[rows, :]
            s_ref[0:8, :] += sr * xr + si * xi
            s_ref[8:16, :] += si * xr - sr * xi
            br, bi = br_ref[rows, :], bi_ref[rows, :]
            s_ref[16:24, :] += er * br + ei * bi
            s_ref[24:32, :] += ei * br - er * bi
            fr, fi = f_ref[0:1, :], f_ref[1:2, :]
            dbr_ref[rows, :] = fr * er + fi * ei
            dbi_ref[rows, :] = fr * ei - fi * er
            return er[0:1, :], ei[0:1, :]

        cr, ci = lax.fori_loop(0, tl // 8, blk, (c_ref[0:1, :], c_ref[1:2, :]))
        c_ref[0:1, :] = cr
        c_ref[1:2, :] = ci

    big = pl.BlockSpec((tl, P), lambda i: (nt - 1 - i, 0))
    out = jax.ShapeDtypeStruct((L, P), F32)
    return pl.pallas_call(
        body, name=name, grid=(nt,),
        in_specs=[big] * 6 + [pl.BlockSpec((24, P), lambda i: (0, 0)), pl.BlockSpec((8, P), lambda i: (0, 0))],
        out_specs=(big, big, pl.BlockSpec((32, P), lambda i: (0, 0))),
        out_shape=(out, out, jax.ShapeDtypeStruct((32, P), F32)), scratch_shapes=[pltpu.VMEM((8, P), F32)],
        compiler_params=_cparams("arbitrary"))(dx_re, dx_im, x_re, x_im, bu_re, bu_im, tab, fvec)


_GK, _GC = math.sqrt(2.0 / math.pi), 0.044715


def _gelu(y):
    return 0.5 * y * (1.0 + jnp.tanh(_GK * (y + _GC * y * y * y)))


def _dgelu(y):
    t = jnp.tanh(_GK * (y + _GC * y * y * y))
    return 0.5 * (1.0 + t) + 0.5 * y * (1.0 - t * t) * _GK * (1.0 + 3.0 * _GC * y * y)


def s5_out_fwd(x_re, x_im, u, cd_re, cd_im, dskip, glu_w, glu_b, name):
    L = u.shape[0]
    tl = _rtile(L, 512)

    def body(xr_ref, xi_ref, u_ref, cr_ref, ci_ref, d_ref, w_ref, b_ref, y_ref, o_ref):
        y = (jnp.dot(xr_ref[...].astype(BF16), cr_ref[...], preferred_element_type=F32)
             - jnp.dot(xi_ref[...].astype(BF16), ci_ref[...], preferred_element_type=F32)
             + d_ref[...] * u_ref[...])
        y_ref[...] = y
        g = _gelu(y)
        z = jnp.dot(g.astype(BF16), w_ref[...], preferred_element_type=F32) + b_ref[...]
        o_ref[...] = (g * _sigmoid(z)).astype(BF16)

    big = pl.BlockSpec((tl, S5_P), lambda i: (i, 0))
    sm = pl.BlockSpec((tl, S5_W), lambda i: (i, 0))
    full = lambda r, c: pl.BlockSpec((r, c), lambda i: (0, 0))
    return pl.pallas_call(
        body, name=name, grid=(L // tl,),
        in_specs=[big, big, sm, full(S5_P, S5_W), full(S5_P, S5_W), full(1, S5_W), full(S5_W, S5_W), full(1, S5_W)],
        out_specs=(sm, sm),
        out_shape=(jax.ShapeDtypeStruct((L, S5_W), F32), jax.ShapeDtypeStruct((L, S5_W), BF16)),
        compiler_params=_cparams("parallel"))(x_re, x_im, u, cd_re, cd_im, dskip, glu_w, glu_b)


def s5_out_bwd(dout, y, u, x_re, x_im, cd_re, cd_im, dskip, glu_w, glu_b, name, dout_col=0):
    L = u.shape[0]
    tl = _rtile(L, 256)
    nt_dims = (((1,), (1,)), ((), ()))
    tn_dims = (((0,), (0,)), ((), ()))

    def body(do_ref, y_ref, u_ref, xr_ref, xi_ref, cr_ref, ci_ref, d_ref, w_ref, b_ref,
             dxr_ref, dxi_ref, du_ref, dcr_ref, dci_ref, dw_ref, s_ref):
        @pl.when(pl.program_id(0) == 0)
        def _():
            dcr_ref[...] = jnp.zeros_like(dcr_ref)
            dci_ref[...] = jnp.zeros_like(dci_ref)
            dw_ref[...] = jnp.zeros_like(dw_ref)
            s_ref[...] = jnp.zeros_like(s_ref)

        yv, dov = y_ref[...], do_ref[...]
        g = _gelu(yv)
        gb = g.astype(BF16)
        sg = _sigmoid(jnp.dot(gb, w_ref[...], preferred_element_type=F32) + b_ref[...])
        dz = dov * g * sg * (1.0 - sg)
        dzb = dz.astype(BF16)
        dg = dov * sg + lax.dot_general(dzb, w_ref[...], nt_dims, preferred_element_type=F32)
        dw_ref[...] += lax.dot_general(gb, dzb, tn_dims, preferred_element_type=F32)
        dy = dg * _dgelu(yv)
        dyb = dy.astype(BF16)
        s_ref[0:8, :] += _fold8(dy * u_ref[...])
        s_ref[8:16, :] += _fold8(dz)
        du_ref[...] = dy * d_ref[...]
        dxr_ref[...] = lax.dot_general(dyb, cr_ref[...], nt_dims, preferred_element_type=F32)
        dxi_ref[...] = -lax.dot_general(dyb, ci_ref[...], nt_dims, preferred_element_type=F32)
        dcr_ref[...] += lax.dot_general(xr_ref[...].astype(BF16), dyb, tn_dims, preferred_element_type=F32)
        dci_ref[...] -= lax.dot_general(xi_ref[...].astype(BF16), dyb, tn_dims, preferred_element_type=F32)

    big = pl.BlockSpec((tl, S5_P), lambda i: (i, 0))
    sm = pl.BlockSpec((tl, S5_W), lambda i: (i, 0))
    full = lambda r, c: pl.BlockSpec((r, c), lambda i: (0, 0))
    sd = jax.ShapeDtypeStruct
    return pl.pallas_call(
        body, name=name, grid=(L // tl,),
        in_specs=[pl.BlockSpec((tl, S5_W), lambda i: (i, dout_col)), sm, sm, big, big, full(S5_P, S5_W),
                  full(S5_P, S5_W), full(1, S5_W), full(S5_W, S5_W), full(1, S5_W)],
        out_specs=(big, big, sm, full(S5_P, S5_W), full(S5_P, S5_W), full(S5_W, S5_W), full(16, S5_W)),
        out_shape=(sd((L, S5_P), F32), sd((L, S5_P), F32), sd((L, S5_W), F32), sd((S5_P, S5_W), F32),
                   sd((S5_P, S5_W), F32), sd((S5_W, S5_W), F32), sd((16, S5_W), F32)),
        compiler_params=_cparams("arbitrary"))(dout, y, u, x_re, x_im, cd_re, cd_im, dskip, glu_w, glu_b)


def s5_block_fwd(u, params, dskip, glu_w, glu_b, tag):
    a_re, a_im, f_re, f_im, bd_re, bd_im, cd_re, cd_im = params
    bu_re = matmul([(u, bd_re.astype(BF16))], "nn", tag + "_bure")
    bu_im = matmul([(u, bd_im.astype(BF16))], "nn", tag + "_buim")
    x_re, x_im = s5_scan_fwd(bu_re, bu_im, a_re, a_im, f_re, f_im, tag + "_scan")
    y, out = s5_out_fwd(x_re, x_im, u, cd_re.astype(BF16), cd_im.astype(BF16), dskip, glu_w, glu_b, tag + "_out")
    return out, (u, bu_re, bu_im, x_re, x_im, y)


def s5_block_bwd(dout, res, params, dskip, glu_w, glu_b, tag, dout_col=0):
    u, bu_re, bu_im, x_re, x_im, y = res
    a_re, a_im, f_re, f_im, bd_re, bd_im, cd_re, cd_im = params
    dxr, dxi, du, dcr, dci, dglu_w, sums = s5_out_bwd(dout, y, u, x_re, x_im, cd_re.astype(BF16), cd_im.astype(BF16),
                                                      dskip, glu_w, glu_b, tag + "_dout", dout_col=dout_col)
    dbr, dbi, acc = s5_scan_bwd(dxr, dxi, x_re, x_im, bu_re, bu_im, a_re, a_im, f_re, f_im, tag + "_dscan")
    du = du + matmul([(dbr, bd_re.astype(BF16)), (dbi, bd_im.astype(BF16))], "nt", tag + "_du")
    dbd_re = matmul([(u, dbr)], "tn", tag + "_dbdre")
    dbd_im = matmul([(u, dbi)], "tn", tag + "_dbdim")
    acc = acc.reshape(4, 8, S5_P).sum(axis=1)
    s = sums.reshape(2, 8, S5_W).sum(axis=1)
    cot = (acc[0:1], acc[1:2], acc[2:3], acc[3:4], dbd_re, dbd_im, dcr, dci)
    return du, cot, dict(dskip=s[0], glu_w=dglu_w, glu_b=s[1])


DN_Z0, DN_NT = 18, 18
REC_U0, REC_A0 = 3072, 3328


def rec_cols_permute(w):
    return jnp.concatenate([w[..., S5_W:REC_A0], w[..., :S5_W], w[..., REC_A0:]], axis=-1)


def rec_cols_restore(w):
    return jnp.concatenate([w[..., REC_U0:REC_A0], w[..., :REC_U0], w[..., REC_A0:]], axis=-1)


DN_W = DN_H * DN_DK


def _dn_conv4(taps, w_ref):
    xc = w_ref[3:4, :] * taps[0]
    for k in range(1, 4):
        xc = xc + w_ref[3 - k:4 - k, :] * taps[k]
    return xc


def dn_prep_fwd(rin, cw, name):
    L = rin.shape[0]
    tl = _rtile(L, 256)
    hb = tl // 8

    def body(x_ref, h_ref, w_ref, o_ref):
        j = pl.program_id(0)
        first = pl.program_id(1) == 0
        x, h = x_ref[...], h_ref[...]
        s = _silu(_dn_conv4([x] + [_shift_down(x, h, k, first) for k in range(1, 4)], w_ref))
        scale = jnp.where(j == 0, DN_DK ** -0.5, 1.0)
        for hd in _HEADS:
            cs = slice(hd * 128, (hd + 1) * 128)
            sh = s[:, cs]
            r = lax.rsqrt(jnp.sum(sh * sh, axis=-1, keepdims=True) + EPS)
            o_ref[:, cs] = jnp.where(j < 2, sh * r * scale, sh)

    main = pl.BlockSpec((tl, DN_W), lambda j, i: (i, j))
    halo = pl.BlockSpec((8, DN_W), lambda j, i: (jnp.maximum(i * hb - 1, 0), j))
    return pl.pallas_call(
        body, name=name, grid=(3, L // tl),
        in_specs=[main, halo, pl.BlockSpec((4, DN_W), lambda j, i: (0, j))],
        out_specs=main, out_shape=jax.ShapeDtypeStruct((L, 3 * DN_W), F32),
        compiler_params=_cparams("parallel", "parallel"))(rin, rin, cw)


def dn_prep_bwd(rin, cw, dout, name):
    L = rin.shape[0]
    tl = _rtile(L, 256)
    hb = tl // 8
    nrt = L // tl

    def body(x_ref, h_ref, w_ref, d_ref, dx_ref, s_ref, c_ref):
        j = pl.program_id(0)
        i = pl.program_id(1)
        first = i == nrt - 1

        @pl.when(i == 0)
        def _():
            s_ref[...] = jnp.zeros_like(s_ref)
            c_ref[...] = jnp.zeros_like(c_ref)

        x, h = x_ref[...], h_ref[...]
        taps = [x] + [_shift_down(x, h, k, first) for k in range(1, 4)]
        xc = _dn_conv4(taps, w_ref)
        s = _silu(xc)
        scale = jnp.where(j == 0, DN_DK ** -0.5, 1.0)
        pieces = []
        for hd in _HEADS:
            cs = slice(hd * 128, (hd + 1) * 128)
            sh, d = s[:, cs], d_ref[:, cs]
            r = lax.rsqrt(jnp.sum(sh * sh, axis=-1, keepdims=True) + EPS)
            n = sh * r
            dn = d * scale
            pieces.append(jnp.where(j < 2, r * (dn - n * jnp.sum(dn * n, axis=-1, keepdims=True)), d))
        dxc = jnp.concatenate(pieces, axis=1) * _dsilu(xc)
        nxt = c_ref[...]
        dx_ref[...] = _dn_conv4([dxc] + [_shift_up(dxc, nxt, k) for k in range(1, 4)], w_ref).astype(BF16)
        c_ref[...] = dxc[0:8, :]
        for k in range(4):
            s_ref[8 * (3 - k):8 * (3 - k) + 8, :] += _fold8(dxc * taps[k])

    rev = lambda i: nrt - 1 - i
    main = pl.BlockSpec((tl, DN_W), lambda j, i: (rev(i), j))
    halo = pl.BlockSpec((8, DN_W), lambda j, i: (jnp.maximum(rev(i) * hb - 1, 0), j))
    return pl.pallas_call(
        body, name=name, grid=(3, nrt),
        in_specs=[main, halo, pl.BlockSpec((4, DN_W), lambda j, i: (0, j)), main],
        out_specs=(main, pl.BlockSpec((32, DN_W), lambda j, i: (0, j))),
        out_shape=(jax.ShapeDtypeStruct((L, 3 * DN_W), BF16), jax.ShapeDtypeStruct((32, 3 * DN_W), F32)),
        scratch_shapes=[pltpu.VMEM((8, DN_W), F32)],
        compiler_params=_cparams("parallel", "arbitrary"))(rin, rin, cw, dout)


_HI = lax.Precision.HIGH
_NT = (((1,), (1,)), ((), ()))
_TN = (((0,), (0,)), ((), ()))
_HEADS = tuple(range(DN_H))


def _mm(a, b, dims=(((1,), (0,)), ((), ())), hi=False):
    if hi:
        return lax.dot_general(a, b, dims, precision=_HI, preferred_element_type=F32)
    return lax.dot_general(a.astype(BF16), b.astype(BF16), dims, preferred_element_type=F32)


def _dn_masks():
    ri = lax.broadcasted_iota(jnp.int32, (DN_C, DN_C), 0)
    ci = lax.broadcasted_iota(jnp.int32, (DN_C, DN_C), 1)
    return ri >= ci, ri > ci, (ri == ci).astype(F32)


def _dn_decay(gc, gr, causal):
    gam = [jnp.where(causal, jnp.exp(jnp.where(causal, gc[h] - gr[h], 0.0)), 0.0) for h in _HEADS]
    eg = [jnp.exp(gc[h]) for h in _HEADS]
    el = [jnp.exp(gc[h][DN_C - 1:DN_C, :] - gc[h]) for h in _HEADS]
    gl = [jnp.exp(gc[h][DN_C - 1:DN_C, :]) for h in _HEADS]
    return gam, eg, el, gl


def _dn_solve(k, v, beta, gam, eg, kk, strict, eye):
    nmat = [jnp.where(strict, beta[h] * kk[h] * gam[h], 0.0) for h in _HEADS]
    t = [eye - nmat[h] for h in _HEADS]
    m = [_mm(nmat[h], nmat[h], hi=True) for h in _HEADS]
    for step in range(5):
        t = [t[h] + _mm(t[h], m[h], hi=True) for h in _HEADS]
        if step < 4:
            m = [_mm(m[h], m[h], hi=True) for h in _HEADS]
    rhs = [jnp.concatenate([v[h] * beta[h], k[h] * (beta[h] * eg[h])], axis=1) for h in _HEADS]
    sol = [_mm(t[h], rhs[h], hi=True) for h in _HEADS]
    return t, sol


def dn_chunk_fwd(qkv, gcol, grow, bcol, name, comm=None):
    L = qkv.shape[0]
    C, W = DN_C, DN_H * DN_DK
    ncb = 8
    tl = ncb * C
    nchunks = L // C

    def body(q_ref, k_ref, v_ref, gc_ref, gr_ref, b_ref, o_ref, sh_ref, t_ref, sol_ref, s_ref):
        @pl.when(pl.program_id(0) == 0)
        def _():
            s_ref[...] = jnp.zeros_like(s_ref)

        causal, strict, eye = _dn_masks()

        def chunk(c, _):
            rows = pl.ds(pl.multiple_of(c * C, C), C)
            grow_c = gr_ref[c]
            hs = lambda h: slice(h * 128, (h + 1) * 128)
            q = [q_ref[rows, hs(h)] for h in _HEADS]
            k = [k_ref[rows, hs(h)] for h in _HEADS]
            v = [v_ref[rows, hs(h)] for h in _HEADS]
            gc = [gc_ref[rows, h:h + 1] for h in _HEADS]
            gr = [grow_c[h:h + 1, :] for h in _HEADS]
            beta = [b_ref[rows, h:h + 1] for h in _HEADS]
            gam, eg, el, gl = _dn_decay(gc, gr, causal)
            kk = [_mm(k[h], k[h], _NT) for h in _HEADS]
            t, sol = _dn_solve(k, v, beta, gam, eg, kk, strict, eye)
            qk = [_mm(q[h], k[h], _NT) * gam[h] for h in _HEADS]
            S = [s_ref[hs(h), :] for h in _HEADS]
            vn = [sol[h][:, :128] - _mm(sol[h][:, 128:], S[h]) for h in _HEADS]
            o = [_mm(q[h] * eg[h], S[h]) + _mm(qk[h], vn[h]) for h in _HEADS]
            Sn = [S[h] * gl[h] + _mm(k[h] * el[h], vn[h], _TN) for h in _HEADS]
            for h in _HEADS:
                sh_ref[c, hs(h), :] = S[h]
                s_ref[hs(h), :] = Sn[h]
                o_ref[rows, hs(h)] = o[h]
                t_ref[rows, h * C:(h + 1) * C] = t[h]
                sol_ref[rows, h * 256:(h + 1) * 256] = sol[h]
            return 0

        lax.fori_loop(0, ncb, chunk, 0)

    col = lambda b: pl.BlockSpec((tl, W), lambda i: (i, b))
    small = pl.BlockSpec((tl, 8), lambda i: (i, 0))
    rowblk = lambda w: pl.BlockSpec((tl, w), lambda i: (i, 0))
    sd = jax.ShapeDtypeStruct
    return _call(body, (qkv, qkv, qkv, gcol, grow, bcol), name=name, grid=(L // tl,),
                 in_specs=[col(0), col(1), col(2), small, pl.BlockSpec((ncb, 8, C), lambda i: (i, 0, 0)), small],
                 out_specs=(rowblk(W), pl.BlockSpec((ncb, W, 128), lambda i: (i, 0, 0)), rowblk(DN_H * C),
                            rowblk(DN_H * 256)),
                 out_shape=(sd((L, W), F32), sd((nchunks, W, 128), F32), sd((L, DN_H * C), F32),
                            sd((L, DN_H * 256), F32)),
                 scratch_shapes=[pltpu.VMEM((W, 128), F32)], sem=("arbitrary",), comm=comm)


def dn_chunk_bwd(qkv, gcol, grow, bcol, shist, thist, solhist, do, name, comm=None):
    L = qkv.shape[0]
    C, W = DN_C, DN_H * DN_DK
    ncb = 8
    tl = ncb * C
    nchunks = L // C
    nt = L // tl

    def body(q_ref, k_ref, v_ref, gc_ref, gr_ref, b_ref, sh_ref, t_ref, sol_ref, do_ref,
             dqkv_ref, dgc_ref, dgr_ref, db_ref, ds_ref):
        @pl.when(pl.program_id(0) == 0)
        def _():
            ds_ref[...] = jnp.zeros_like(ds_ref)

        lane8 = lax.broadcasted_iota(jnp.int32, (C, 8), 1)
        sub8 = lax.broadcasted_iota(jnp.int32, (8, C), 0)
        rowid = lax.broadcasted_iota(jnp.int32, (C, 1), 0)
        causal, strict, _ = _dn_masks()
        rsum = lambda a: jnp.sum(a, axis=1, keepdims=True)

        def chunk(cc, _):
            c = ncb - 1 - cc
            rows = pl.ds(pl.multiple_of(c * C, C), C)
            grow_c = gr_ref[c]
            hs = lambda h: slice(h * 128, (h + 1) * 128)
            q = [q_ref[rows, hs(h)] for h in _HEADS]
            k = [k_ref[rows, hs(h)] for h in _HEADS]
            v = [v_ref[rows, hs(h)] for h in _HEADS]
            gc = [gc_ref[rows, h:h + 1] for h in _HEADS]
            gr = [grow_c[h:h + 1, :] for h in _HEADS]
            beta = [b_ref[rows, h:h + 1] for h in _HEADS]
            t = [t_ref[rows, h * C:(h + 1) * C] for h in _HEADS]
            sol = [sol_ref[rows, h * 256:(h + 1) * 256] for h in _HEADS]
            S = [sh_ref[c, hs(h), :] for h in _HEADS]
            dS = [ds_ref[hs(h), :] for h in _HEADS]
            dov = [do_ref[rows, hs(h)] for h in _HEADS]
            gam, eg, el, gl = _dn_decay(gc, gr, causal)
            kk = [_mm(k[h], k[h], _NT) for h in _HEADS]
            qk_raw = [_mm(q[h], k[h], _NT) for h in _HEADS]
            w = [sol[h][:, 128:] for h in _HEADS]
            kd = [k[h] * el[h] for h in _HEADS]
            vn = [sol[h][:, :128] - _mm(w[h], S[h]) for h in _HEADS]
            dvn = [_mm(qk_raw[h] * gam[h], dov[h], _TN) + _mm(kd[h], dS[h]) for h in _HEADS]
            dqd = [_mm(dov[h], S[h], _NT) for h in _HEADS]
            dqk = [jnp.where(causal, _mm(dov[h], vn[h], _NT), 0.0) for h in _HEADS]
            dkd = [_mm(vn[h], dS[h], _NT) for h in _HEADS]
            dgl = [jnp.sum(rsum(dS[h] * S[h]), axis=0, keepdims=True) for h in _HEADS]
            dw = [-_mm(dvn[h], S[h], _NT) for h in _HEADS]
            dSn = [dS[h] * gl[h] + _mm(q[h] * eg[h], dov[h], _TN) - _mm(w[h], dvn[h], _TN) for h in _HEADS]
            drhs = [_mm(t[h], jnp.concatenate([dvn[h], dw[h]], axis=1), _TN, hi=True) for h in _HEADS]
            dn = [jnp.where(strict, -_mm(drhs[h], sol[h], _NT, hi=True), 0.0) for h in _HEADS]
            dgc_all = jnp.zeros((C, 8), F32)
            db_all = jnp.zeros((C, 8), F32)
            dgr_all = jnp.zeros((8, C), F32)
            for h in _HEADS:
                drv, drk = drhs[h][:, :128], drhs[h][:, 128:]
                t2 = rsum(drk * k[h])
                x = dn[h] * gam[h]
                dbeta = rsum(drv * v[h]) + t2 * eg[h] + rsum(x * kk[h])
                dkk = x * beta[h]
                draw = dqk[h] * gam[h]
                mm_ = (dn[h] * beta[h] * kk[h] + dqk[h] * qk_raw[h]) * gam[h]
                deg = t2 * beta[h] + rsum(dqd[h] * q[h])
                r_ = rsum(dkd[h] * k[h]) * el[h]
                dglast = jnp.sum(r_, axis=0, keepdims=True) + dgl[h] * gl[h]
                dgc = rsum(mm_) + deg * eg[h] - r_ + jnp.where(rowid == C - 1, dglast, 0.0)
                dgr = -jnp.sum(mm_, axis=0, keepdims=True)
                dqkv_ref[rows, hs(h)] = _mm(draw, k[h]) + dqd[h] * eg[h]
                dqkv_ref[rows, hs(DN_H + h)] = (drk * (beta[h] * eg[h]) + _mm(dkk, k[h]) + _mm(dkk, k[h], _TN)
                                                + _mm(draw, q[h], _TN) + dkd[h] * el[h])
                dqkv_ref[rows, hs(2 * DN_H + h)] = drv * beta[h]
                ds_ref[hs(h), :] = dSn[h]
                dgc_all = dgc_all + jnp.where(lane8 == h, dgc, 0.0)
                db_all = db_all + jnp.where(lane8 == h, dbeta, 0.0)
                dgr_all = dgr_all + jnp.where(sub8 == h, dgr, 0.0)
            dgc_ref[rows, :] = dgc_all
            db_ref[rows, :] = db_all
            dgr_ref[c] = dgr_all
            return 0

        lax.fori_loop(0, ncb, chunk, 0)

    rev = lambda i: nt - 1 - i
    col = lambda b: pl.BlockSpec((tl, W), lambda i: (rev(i), b))
    rowblk = lambda w: pl.BlockSpec((tl, w), lambda i: (rev(i), 0))
    small = pl.BlockSpec((tl, 8), lambda i: (rev(i), 0))
    g3 = pl.BlockSpec((ncb, 8, C), lambda i: (rev(i), 0, 0))
    sd = jax.ShapeDtypeStruct
    return _call(body, (qkv, qkv, qkv, gcol, grow, bcol, shist, thist, solhist, do), name=name, grid=(nt,),
                 in_specs=[col(0), col(1), col(2), small, g3, small,
                           pl.BlockSpec((ncb, W, 128), lambda i: (rev(i), 0, 0)), rowblk(DN_H * C),
                           rowblk(DN_H * 256), col(0)],
                 out_specs=(rowblk(3 * W), small, g3, small),
                 out_shape=(sd((L, 3 * W), F32), sd((L, 8), F32), sd((nchunks, 8, C), F32), sd((L, 8), F32)),
                 scratch_shapes=[pltpu.VMEM((W, 128), F32)], sem=("arbitrary",), comm=comm)


def dn_out_fwd(o, rin, nw, name):
    L = o.shape[0]
    tl = _rtile(L, 256)

    def body(o_ref, z_ref, w_ref, y_ref):
        for hd in _HEADS:
            cs = slice(hd * 128, (hd + 1) * 128)
            ov = o_ref[:, cs]
            r = lax.rsqrt(jnp.mean(ov * ov, axis=-1, keepdims=True) + EPS)
            y_ref[:, cs] = (ov * r * w_ref[...] * _silu(z_ref[:, cs])).astype(BF16)

    return pl.pallas_call(
        body, name=name, grid=(L // tl,),
        in_specs=[pl.BlockSpec((tl, DN_W), lambda i: (i, 0)), pl.BlockSpec((tl, DN_W), lambda i: (i, 3)),
                  pl.BlockSpec((1, 128), lambda i: (0, 0))],
        out_specs=pl.BlockSpec((tl, DN_W), lambda i: (i, 0)), out_shape=jax.ShapeDtypeStruct((L, DN_W), BF16),
        compiler_params=_cparams("parallel"))(o, rin, nw)


def dn_out_bwd(dycat, o, rin, nw, name):
    L = o.shape[0]
    tl = _rtile(L, 256)

    def body(dy_ref, o_ref, z_ref, w_ref, do_ref, dz_ref, s_ref):
        @pl.when(pl.program_id(0) == 0)
        def _():
            s_ref[...] = jnp.zeros_like(s_ref)

        for hd in _HEADS:
            cs = slice(hd * 128, (hd + 1) * 128)
            ov, zv, d = o_ref[:, cs], z_ref[:, cs], dy_ref[:, cs]
            r = lax.rsqrt(jnp.mean(ov * ov, axis=-1, keepdims=True) + EPS)
            n = ov * r
            dnw = d * _silu(zv)
            dz_ref[:, cs] = (d * n * w_ref[...] * _dsilu(zv)).astype(BF16)
            dn = dnw * w_ref[...]
            do_ref[:, cs] = r * (dn - n * jnp.mean(dn * n, axis=-1, keepdims=True))
            s_ref[:, cs] += _fold8(dnw * n)

    own = pl.BlockSpec((tl, DN_W), lambda i: (i, 0))
    sd = jax.ShapeDtypeStruct
    return pl.pallas_call(
        body, name=name, grid=(L // tl,),
        in_specs=[own, own, pl.BlockSpec((tl, DN_W), lambda i: (i, 3)), pl.BlockSpec((1, 128), lambda i: (0, 0))],
        out_specs=(own, own, pl.BlockSpec((8, DN_W), lambda i: (0, 0))),
        out_shape=(sd((L, DN_W), F32), sd((L, DN_W), BF16), sd((8, DN_W), F32)),
        compiler_params=_cparams("arbitrary"))(dycat, o, rin, nw)


def dn_gates(a, beta_raw, a_log, dt_bias):
    L = a.shape[0]
    beta = jax.nn.sigmoid(beta_raw)
    g = -jnp.exp(a_log) * jax.nn.softplus(a + dt_bias)
    G = jnp.cumsum(g.reshape(L // DN_C, DN_C, DN_H), axis=1)
    pad = lambda t: jnp.pad(t, ((0, 0), (0, 8 - DN_H)))
    gcol = pad(G.reshape(L, DN_H))
    grow = jnp.pad(jnp.transpose(G, (0, 2, 1)), ((0, 0), (0, 8 - DN_H), (0, 0)))
    return gcol, grow, pad(beta)


def dn_block_fwd(rin, cw, a_log, dt_bias, out_norm, tag, comm=None):
    gates, gates_vjp = jax.vjp(dn_gates, rin[:, REC_A0:REC_A0 + DN_H], rin[:, REC_A0 + DN_H:REC_IN], a_log, dt_bias)
    qkv = dn_prep_fwd(rin, cw, tag + "_prep")
    (o, shist, thist, solhist), got = _with_comm(dn_chunk_fwd(qkv, *gates, tag + "_chunk", comm=comm), comm)
    yd = dn_out_fwd(o, rin, out_norm.reshape(1, 128), tag + "_onorm")
    return yd, (qkv, gates, gates_vjp, o, shist, thist, solhist), got


def dn_block_bwd(dyd, res, rin, cw, out_norm, tag, comm=None):
    qkv, gates, gates_vjp, o, shist, thist, solhist = res
    do, dz, nsum = dn_out_bwd(dyd, o, rin, out_norm.reshape(1, 128), tag + "_donorm")
    (dqkv, dgc, dgr, db), got = _with_comm(dn_chunk_bwd(qkv, *gates, shist, thist, solhist, do, tag + "_dchunk",
                                                        comm=comm), comm)
    da, dbraw, g_alog, g_dtb = gates_vjp((dgc, dgr, db))
    dx, csum = dn_prep_bwd(rin, cw, dqkv, tag + "_dprep")
    grads = dict(conv=csum.reshape(4, 8, DN_NT * 128).sum(axis=1), a_log=g_alog, dt_bias=g_dtb,
                 out_norm=nsum.sum(axis=0).reshape(DN_H, 128).sum(axis=0))
    return dx, dz, da, dbraw, grads, got


_HBM = pl.BlockSpec(memory_space=pltpu.HBM)


def _mesh_pos():
    xi, yi, ci = lax.axis_index("x"), lax.axis_index("y"), lax.axis_index("c")
    return xi, yi, ci, 4 * xi + 2 * yi + ci


def _peer(xi, yi, ci, k):
    px = 1 - xi if (k >> 2) & 1 else xi
    py = 1 - yi if (k >> 1) & 1 else yi
    pc = 1 - ci if k & 1 else ci
    return (px, py, pc), 4 * px + 2 * py + pc


def _exchange(xs, gather, name):
    n = len(xs)

    def body(*refs):
        copies = _comm_copies(refs[:n], refs[n:2 * n], *refs[2 * n:], gather)
        for cp in copies:
            cp.start()
        for cp in copies:
            cp.wait()

    return pl.pallas_call(
        body, name=name, in_specs=[_HBM] * n, out_specs=tuple([_HBM] * n),
        out_shape=_comm_out_shapes(xs), scratch_shapes=_comm_sems(n))(*xs)


def _comm_out_shapes(xs):
    return tuple(jax.ShapeDtypeStruct((N_DEV,) + x.shape[-2:], x.dtype) for x in xs)


def _comm_sems(n):
    return [pltpu.SemaphoreType.DMA((n * (N_DEV - 1),)), pltpu.SemaphoreType.DMA((n * (N_DEV - 1),)),
            pltpu.SemaphoreType.DMA((n,))]


def _comm_copies(x_refs, o_refs, send_sems, recv_sems, lsems, gather):
    xi, yi, ci, me = _mesh_pos()
    copies = []
    for t in range(len(x_refs)):
        src_of = (lambda lin, t=t: x_refs[t]) if gather else (lambda lin, t=t: x_refs[t].at[lin])
        copies.append(pltpu.make_async_copy(src_of(me), o_refs[t].at[me], lsems.at[t]))
        for k in range(1, N_DEV):
            peer, lin = _peer(xi, yi, ci, k)
            s = t * (N_DEV - 1) + k - 1
            copies.append(pltpu.make_async_remote_copy(
                src_ref=src_of(lin), dst_ref=o_refs[t].at[me], send_sem=send_sems.at[s],
                recv_sem=recv_sems.at[s], device_id=peer, device_id_type=pl.DeviceIdType.MESH))
    return copies


def _call(body, args, *, name, grid, in_specs, out_specs, out_shape, scratch_shapes=(), sem, comm=None):
    if comm is None:
        return pl.pallas_call(body, name=name, grid=grid, in_specs=in_specs, out_specs=out_specs,
                              out_shape=out_shape, scratch_shapes=list(scratch_shapes),
                              compiler_params=_cparams(*sem))(*args)
    xs, gather = comm
    n = len(xs)
    single = not isinstance(out_shape, (tuple, list))
    outs_shape = (out_shape,) if single else tuple(out_shape)
    outs_specs = (out_specs,) if single else tuple(out_specs)
    n_in, n_out, n_scr = len(in_specs), len(outs_shape), len(scratch_shapes)

    def body2(*refs):
        ins, cx = refs[:n_in], refs[n_in:n_in + n]
        outs = refs[n_in + n:n_in + n + n_out]
        co = refs[n_in + n + n_out:n_in + 2 * n + n_out]
        scr = refs[n_in + 2 * n + n_out:n_in + 2 * n + n_out + n_scr]
        sems = refs[n_in + 2 * n + n_out + n_scr:]
        first = functools.reduce(jnp.logical_and, [pl.program_id(a) == 0 for a in range(len(grid))])
        last = functools.reduce(jnp.logical_and, [pl.program_id(a) == grid[a] - 1 for a in range(len(grid))])

        @pl.when(first)
        def _():
            for cp in _comm_copies(cx, co, *sems, gather):
                cp.start()

        body(*ins, *outs, *scr)

        @pl.when(last)
        def _():
            for cp in _comm_copies(cx, co, *sems, gather):
                cp.wait()

    res = pl.pallas_call(
        body2, name=name, grid=grid, in_specs=list(in_specs) + [_HBM] * n,
        out_specs=outs_specs + tuple([_HBM] * n), out_shape=outs_shape + _comm_out_shapes(xs),
        scratch_shapes=list(scratch_shapes) + _comm_sems(n),
        compiler_params=_cparams(*(["arbitrary"] * len(grid))))(*args, *xs)
    main = res[0] if single else tuple(res[:n_out])
    return main, list(res[n_out:])


def all_gather(x, name):
    return _exchange([x], True, name)[0]


def all_gather_many(xs, name):
    return _exchange(xs, True, name)


def all_to_all_many(xs, name):
    return _exchange(xs, False, name)


def reduce_adamw(gsrc, w, m, v, name, comm=None):
    parts = list(gsrc) if isinstance(gsrc, (list, tuple)) else [gsrc]
    S, R0, C = parts[0].shape
    R = R0 * len(parts)
    tr = _rtile(R0, max(16, min(256, (4 << 20) // (S * C * 4) // 16 * 16)), 16 if R0 % 16 == 0 else 8)
    n0 = R0 // tr
    c1 = 1.0 - ADAM_B1 ** ADAM_STEP
    c2 = 1.0 - ADAM_B2 ** ADAM_STEP

    def body(*refs):
        g_refs = refs[:len(parts)]
        w_ref, m_ref, v_ref, go_ref, d_ref, mo_ref, vo_ref = refs[len(parts):]
        for p, g_ref in enumerate(g_refs):
            @pl.when(pl.program_id(0) // n0 == p)
            def _(g_ref=g_ref):
                acc = g_ref[0].astype(F32)
                for s in range(1, S):
                    acc = acc + g_ref[s].astype(F32)
                go_ref[...] = acc
        g = go_ref[...]
        mn = ADAM_B1 * m_ref[...] + (1.0 - ADAM_B1) * g
        vn = ADAM_B2 * v_ref[...] + (1.0 - ADAM_B2) * (g * g)
        mo_ref[...] = mn
        vo_ref[...] = vn
        d_ref[...] = -ADAM_LR * ((mn / c1) / (jnp.sqrt(vn / c2) + ADAM_EPS) + ADAM_WD * w_ref[...])

    big = pl.BlockSpec((tr, C), lambda i: (i, 0))
    o = jax.ShapeDtypeStruct((R, C), F32)
    part_spec = lambda p: pl.BlockSpec((S, tr, C), lambda i: (0, jnp.clip(i - p * n0, 0, n0 - 1), 0))
    return _call(body, (*parts, w, m, v), name=name, grid=(R // tr,),
                 in_specs=[part_spec(p) for p in range(len(parts))] + [big, big, big],
                 out_specs=(big, big, big, big), out_shape=(o, o, o, o), sem=("parallel",), comm=comm)


def _to_slabs(g, ax):
    shp = g.shape
    g = g.reshape(shp[:ax] + (N_DEV, shp[ax] // N_DEV) + shp[ax + 1:])
    return jnp.moveaxis(g, ax, 0).reshape(N_DEV, -1)


def _from_slabs(s, ax, shp):
    s = s.reshape((N_DEV,) + shp[:ax] + (shp[ax] // N_DEV,) + shp[ax + 1:])
    return jnp.moveaxis(s, 0, ax).reshape(shp)


def _pack_rows(flat, width, row_mult):
    n = flat.shape[-1]
    per = width * row_mult
    tot = -(-n // per) * per
    flat = jnp.pad(flat, [(0, 0)] * (flat.ndim - 1) + [(0, tot - n)])
    return flat.reshape(flat.shape[:-1] + (tot // width, width))


def _offsets(sizes):
    offs, o = [], 0
    for s in sizes:
        offs.append(o)
        o += s
    return offs


WEIGHTS = ['ada_w', 'ada_b', 'norm_mix', 'norm_ffn', 'attn_w_in', 'attn_q_norm_a', 'attn_k_norm_a', 'attn_q_norm_b',
           'attn_k_norm_b', 'attn_sinks', 'attn_w_out', 'rec_w_in', 's5_lambda_re', 's5_lambda_im', 's5_log_dt',
           's5_b_re', 's5_b_im', 's5_c_re', 's5_c_im', 's5_d', 's5_glu_w', 's5_glu_b', 'dn_conv', 'dn_a_log',
           'dn_dt_bias', 'dn_out_norm', 'rec_w_out', 'ffn_w_up', 'ffn_conv', 'ffn_w_down']
BIG = [('attn_w_in', (D, ATTN_IN // N_DEV)), ('attn_w_out', (D // N_DEV, D)), ('rec_w_in', (D // N_DEV, REC_PAD)),
       ('s5_glu_w', (S5_W // N_DEV, S5_W)), ('rec_w_out', (D // N_DEV, D)), ('ffn_w_up', (2 * D, 2 * D_FF // N_DEV)),
       ('ffn_w_down', (2 * D_FF // N_DEV, D))]


def _shard2d(name, t):
    if name == 'rec_w_in':
        return jnp.pad(t[0], ((0, 0), (0, REC_PAD - REC_IN)))
    return t.reshape((-1, t.shape[-1]))


def _cols_to_slabs(g, k=N_DEV):
    r, n = g.shape
    return jnp.transpose(g.reshape(r, k, n // k), (1, 0, 2))


def _slabs_to_cols(s):
    k, r, c_ = s.shape
    return jnp.transpose(s, (1, 0, 2)).reshape(r, k * c_)
SMALL_SHARDED = [('s5_d', 1, (1, S5_W)), ('s5_glu_b', 1, (1, S5_W)), ('dn_conv', 2, (1, 4, 2304)),
                 ('ffn_conv', 2, (2, 3, 2 * D_FF))]
REPLICATED = [('ada_b', (2, 6 * D)), ('norm_mix', (2, D)), ('norm_ffn', (2, D)), ('attn_q_norm_a', (1, HD)),
              ('attn_k_norm_a', (1, HD)), ('attn_q_norm_b', (1, HD)), ('attn_k_norm_b', (1, HD)),
              ('attn_sinks', (1, 8)), ('s5_lambda_re', (1, 16, 64)), ('s5_lambda_im', (1, 16, 64)),
              ('s5_log_dt', (1, 16)), ('s5_b_re', (1, 16, 64, 16)), ('s5_b_im', (1, 16, 64, 16)),
              ('s5_c_re', (1, 16, 16, 64)), ('s5_c_im', (1, 16, 16, 64)), ('dn_a_log', (1, DN_H)),
              ('dn_dt_bias', (1, DN_H)), ('dn_out_norm', (1, 128))]


def _numel(shp):
    return int(np.prod(shp))


def kernel(x, c, ada_w, ada_b, norm_mix, norm_ffn, attn_w_in, attn_q_norm_a, attn_k_norm_a, attn_q_norm_b, attn_k_norm_b, attn_sinks, attn_w_out, rec_w_in, s5_lambda_re, s5_lambda_im, s5_log_dt, s5_b_re, s5_b_im, s5_c_re, s5_c_im, s5_d, s5_glu_w, s5_glu_b, dn_conv, dn_a_log, dn_dt_bias, dn_out_norm, rec_w_out, ffn_w_up, ffn_conv, ffn_w_down, loss_target, m_ada_w, m_ada_b, m_norm_mix, m_norm_ffn, m_attn_w_in, m_attn_q_norm_a, m_attn_k_norm_a, m_attn_q_norm_b, m_attn_k_norm_b, m_attn_sinks, m_attn_w_out, m_rec_w_in, m_s5_lambda_re, m_s5_lambda_im, m_s5_log_dt, m_s5_b_re, m_s5_b_im, m_s5_c_re, m_s5_c_im, m_s5_d, m_s5_glu_w, m_s5_glu_b, m_dn_conv, m_dn_a_log, m_dn_dt_bias, m_dn_out_norm, m_rec_w_out, m_ffn_w_up, m_ffn_conv, m_ffn_w_down, v_ada_w, v_ada_b, v_norm_mix, v_norm_ffn, v_attn_w_in, v_attn_q_norm_a, v_attn_k_norm_a, v_attn_q_norm_b, v_attn_k_norm_b, v_attn_sinks, v_attn_w_out, v_rec_w_in, v_s5_lambda_re, v_s5_lambda_im, v_s5_log_dt, v_s5_b_re, v_s5_b_im, v_s5_c_re, v_s5_c_im, v_s5_d, v_s5_glu_w, v_s5_glu_b, v_dn_conv, v_dn_a_log, v_dn_dt_bias, v_dn_out_norm, v_rec_w_out, v_ffn_w_up, v_ffn_conv, v_ffn_w_down):
    loc = locals()
    W = {n: loc[n] for n in WEIGHTS}
    M = {n: loc["m_" + n] for n in WEIGHTS}
    V = {n: loc["v_" + n] for n in WEIGHTS}
    _, _, _, me = _mesh_pos()
    L = x.shape[1]
    x0, tgt = x[0], loss_target[0]

    small_in = jnp.concatenate([c.reshape(-1)] + [W[n].reshape(-1) for n, _, _ in SMALL_SHARDED])
    si, att_in_all, att_out_all = all_gather_many(
        [_pack_rows(small_in, 1024, 8), attn_w_in[0].astype(BF16), attn_w_out[0].astype(BF16)], "gather_first")
    si = si.reshape(N_DEV, -1)
    c_all = si[:, :D]
    off = D
    small_full = {}
    for n, ax, shp in SMALL_SHARDED:
        k = _numel(shp) // N_DEV
        small_full[n] = _from_slabs(si[:, off:off + k], ax, shp)
        off += k

    cond_all = jax.nn.silu(c_all)
    modp = jnp.concatenate([matmul([(cond_all, ada_w[l].astype(BF16))], "nn", f"ada{l}") for l in range(2)], axis=0)
    modp_all = all_gather(modp, "gather_mod")
    mods = []
    for l in range(2):
        row = lax.dynamic_index_in_dim(modp_all, l * N_DEV + me, axis=1, keepdims=False)
        mod = row.reshape(1, 6 * D) + ada_b[l].reshape(1, 6 * D)
        mods.append([mod[:, i * D:(i + 1) * D] for i in range(6)])

    w_att_in, w_att_out = _slabs_to_cols(att_in_all), att_out_all.reshape(D, D)
    bf = lambda t: t.astype(BF16)
    ffn_shards = [[bf(ffn_w_up[l]), bf(ffn_w_down[l])] for l in range(2)]
    rec_shards = [bf(_shard2d('rec_w_in', rec_w_in)), bf(s5_glu_w[0]), bf(rec_w_out[0])]
    ffn_cw = [small_full['ffn_conv'][l] for l in range(2)]
    dn_cw = small_full['dn_conv'][0]
    s5_dskip, glu_b = small_full['s5_d'], small_full['s5_glu_b']
    row = lambda t: t.reshape(1, -1)

    sh1, sc1, g1, sh2, sc2, g2 = mods[0]
    h1 = gate_norm_fwd(x0, None, None, row(norm_mix[0]), sh1, sc1, "l0_norm1")
    wvec, sinkvec = attn_vectors(attn_q_norm_a[0], attn_k_norm_a[0], attn_q_norm_b[0], attn_k_norm_b[0], attn_sinks[0])
    y0, res_att, got = attention_block_fwd(
        h1, w_att_in, wvec, sinkvec, w_att_out, "att",
        comms={'swa': ([ffn_shards[0][0][:D // 2]], True), 1: ([ffn_shards[0][0][D // 2:]], True),
               4: (ffn_shards[0][1:], True)})
    split_up = lambda up_all: (_slabs_to_cols(up_all[:4]), _slabs_to_cols(up_all[4:]))
    w_up = [split_up(jnp.concatenate([got['swa'][0], got[1][0]], axis=1))]
    w_down = [got[4][0].reshape(D_FF, D)]
    x1, h2 = gate_norm_fwd(x0, y0, g1, row(norm_ffn[0]), sh2, sc2, "l0_norm2")
    f0, res_f0, got_rec = ffn_block_fwd(h2, w_up[0][0], w_up[0][1], ffn_cw[0], w_down[0], "ffn0",
                                        comm=(rec_shards, True))
    w_rec_in = rec_cols_permute(got_rec[0].reshape(D, REC_PAD))
    glu_w, w_rec_out = got_rec[1].reshape(S5_W, S5_W), got_rec[2].reshape(D, D)
    w_rec_out = jnp.concatenate([w_rec_out[S5_W:], w_rec_out[:S5_W]], axis=0)
    t1, tc1, tg1, t2, tc2, tg2 = mods[1]
    x2, h3 = gate_norm_fwd(x1, f0, g2, row(norm_mix[1]), t1, tc1, "l1_norm1")
    rin = matmul([(h3, w_rec_in)], "nn", "rec_in")
    s5p, s5p_vjp = jax.vjp(s5_params, s5_lambda_re[0], s5_lambda_im[0], s5_log_dt[0], s5_b_re[0], s5_b_im[0],
                           s5_c_re[0], s5_c_im[0])
    u = rin[:, REC_U0:REC_A0]
    yc, res_s5 = s5_block_fwd(u, s5p, s5_dskip, glu_w, glu_b, "s5")
    yd, res_dn, got_ffn1 = dn_block_fwd(rin, dn_cw, dn_a_log[0], dn_dt_bias[0], dn_out_norm[0], "dn",
                                        comm=(ffn_shards[1], True))
    w_up.append(split_up(got_ffn1[0]))
    w_down.append(got_ffn1[1].reshape(D_FF, D))
    ycat = jnp.concatenate([yd, yc], axis=1)
    y1 = matmul([(ycat, w_rec_out)], "nn", "rec_out")
    x3, h4 = gate_norm_fwd(x2, y1, tg1, row(norm_ffn[1]), t2, tc2, "l1_norm2")
    f1, res_f1, _ = ffn_block_fwd(h4, w_up[1][0], w_up[1][1], ffn_cw[1], w_down[1], "ffn1")
    dx4, df1, lsum = final_loss(x3, f1, tg2, tgt, "loss")

    G = {}
    d_tg2 = lsum[8:16].sum(axis=0)
    dh4, gf1, _ = ffn_block_bwd(df1, res_f1, w_up[1][0], w_up[1][1], ffn_cw[1], w_down[1], "ffn1")
    ffn_slabs = lambda g: [g['w_up'], g['w_down'].reshape(N_DEV, D_FF // N_DEV, D)]
    dx3, dy1, s = gate_norm_bwd(x3, y1, tg1, row(norm_ffn[1]), tc2, dx4, dh4, "l1_dnorm2")
    s = s.reshape(4, 8, D).sum(axis=1)
    d_tg1, d_nffn1, d_t2, d_tc2 = s[0], s[1] * (1.0 + tc2[0]), s[2], s[1] * norm_ffn[1]
    g_rec_out = matmul([(ycat, dy1)], "tn", "rec_out_dw", out_dtype=BF16)
    g_rec_out = jnp.concatenate([g_rec_out[DN_W:], g_rec_out[:DN_W]], axis=0).reshape(N_DEV, D // N_DEV, D)
    dycat = matmul([(dy1, w_rec_out)], "nt", "rec_out_dx")
    du, s5cot, gs5 = s5_block_bwd(dycat, res_s5, s5p, s5_dskip, glu_w, glu_b, "s5", dout_col=DN_W // S5_W)
    s5g = s5p_vjp(s5cot)
    dqkv, dz, da, dbraw, gdn, recv_ffn1 = dn_block_bwd(dycat, res_dn, rin, dn_cw, dn_out_norm[0], "dn",
                                                       comm=(ffn_slabs(gf1), False))
    d_rest = jnp.concatenate([du.astype(BF16), da.astype(BF16), dbraw.astype(BF16),
                              jnp.zeros((L, REC_PAD - REC_IN), BF16)], axis=1)
    drin = ((dqkv, 0), (dz, 3 * DN_W), (d_rest, REC_U0))
    g_rec_in = jnp.concatenate([matmul([(h3, p)], "tn", f"rec_in_dw{i}", out_dtype=BF16)
                                for i, (p, _) in enumerate(drin)], axis=1)
    g_rec_in = rec_cols_restore(g_rec_in).reshape(N_DEV, D // N_DEV, REC_PAD)
    g_glu = gs5['glu_w'].astype(BF16).reshape(N_DEV, S5_W // N_DEV, S5_W)
    dh3 = matmul([(p, w_rec_in[:, c0:c0 + p.shape[1]]) for p, c0 in drin], "nt", "rec_in_dx")
    dx2, df0, s = gate_norm_bwd(x2, f0, g2, row(norm_mix[1]), tc1, dx3, dh3, "l1_dnorm1")
    s = s.reshape(4, 8, D).sum(axis=1)
    d_g2, d_nmix1, d_t1, d_tc1 = s[0], s[1] * (1.0 + tc1[0]), s[2], s[1] * norm_mix[1]
    dh2, gf0, recv_rec = ffn_block_bwd(df0, res_f0, w_up[0][0], w_up[0][1], ffn_cw[0], w_down[0], "ffn0",
                                       comm=([g_rec_in, g_glu, g_rec_out], False))
    dx1, dy0, s = gate_norm_bwd(x1, y0, g1, row(norm_ffn[0]), sc2, dx2, dh2, "l0_dnorm2")
    s = s.reshape(4, 8, D).sum(axis=1)
    d_g1, d_nffn0, d_sh2, d_sc2 = s[0], s[1] * (1.0 + sc2[0]), s[2], s[1] * norm_ffn[0]
    dh1, gatt, got_b = attention_block_bwd(dy0, res_att, w_att_in, wvec, sinkvec, w_att_out, "att",
                                           comms={'swa': ([gf0['w_up'][:, :D // 2]], False),
                                                  16: ([gf0['w_up'][:, D // 2:]], False),
                                                  1: (ffn_slabs(gf0)[1:], False)},
                                           send_w_out_on=4)
    recv_ffn0 = [jnp.concatenate([got_b['swa'][0], got_b[16][0]], axis=1), got_b[1][0]]
    (grad_x, s), recv_w_in = gate_norm_bwd(x0, None, None, row(norm_mix[0]), sc1, dx1, dh1, "l0_dnorm1",
                                           comm=([_cols_to_slabs(gatt['w_in'])], False))
    recv_att = [recv_w_in[0], got_b[4][0]]
    s = s.reshape(4, 8, D).sum(axis=1)
    d_nmix0, d_sh1, d_sc1 = s[1] * (1.0 + sc1[0]), s[2], s[1] * norm_mix[0]
    dmod = jnp.stack([jnp.concatenate([d_sh1, d_sc1, d_g1, d_sh2, d_sc2, d_g2]),
                      jnp.concatenate([d_t1, d_tc1, d_tg1, d_t2, d_tc2, d_tg2])])

    P = {'ada_b': dmod, 'norm_mix': jnp.stack([d_nmix0, d_nmix1]), 'norm_ffn': jnp.stack([d_nffn0, d_nffn1]),
         'attn_q_norm_a': gatt['q_norm_a'], 'attn_k_norm_a': gatt['k_norm_a'], 'attn_q_norm_b': gatt['q_norm_b'],
         'attn_k_norm_b': gatt['k_norm_b'], 'attn_sinks': gatt['sinks'],
         's5_lambda_re': s5g[0], 's5_lambda_im': s5g[1], 's5_log_dt': s5g[2], 's5_b_re': s5g[3], 's5_b_im': s5g[4],
         's5_c_re': s5g[5], 's5_c_im': s5g[6], 'dn_a_log': gdn['a_log'], 'dn_dt_bias': gdn['dt_bias'],
         'dn_out_norm': gdn['out_norm'],
         's5_d': gs5['dskip'], 's5_glu_b': gs5['glu_b'], 'dn_conv': gdn['conv'],
         'ffn_conv': jnp.stack([gf0['conv'], gf1['conv']])}

    out = {k: {} for k in ("g", "d", "m", "v")}
    keys = ("g", "d", "m", "v")
    recv = {'attn_w_in': recv_att[0], 'attn_w_out': recv_att[1], 'rec_w_in': recv_rec[0], 's5_glu_w': recv_rec[1],
            'rec_w_out': recv_rec[2]}
    for n, gr_ in recv.items():
        res4 = reduce_adamw(gr_, _shard2d(n, W[n]), _shard2d(n, M[n]), _shard2d(n, V[n]), "adamw_" + n)
        for key, t in zip(keys, res4):
            out[key][n] = (t[:, :REC_IN] if n == 'rec_w_in' else t).reshape(W[n].shape)
    rep_sizes = [_numel(shp) for _, shp in REPLICATED]
    ss_sizes = [_numel(shp) for _, _, shp in SMALL_SHARDED]
    rep_offs = _offsets(rep_sizes + ss_sizes + [1])
    parts = [P[n].reshape(-1) for n, _ in REPLICATED] + [P[n].reshape(-1) for n, _, _ in SMALL_SHARDED]
    parts.append(lsum[0:8].sum().reshape(1))
    spack = _pack_rows(jnp.concatenate(parts), 1024, 8)
    flat2d = lambda t: t.reshape(-1, t.shape[-1])
    sall = None
    for n, idx in (('ffn_w_up', 0), ('ffn_w_down', 1)):
        comm = ([spack], True) if sall is None else None
        res4, got_s = _with_comm(reduce_adamw([recv_ffn0[idx], recv_ffn1[idx]], flat2d(W[n]), flat2d(M[n]),
                                              flat2d(V[n]), "adamw_" + n, comm=comm), comm)
        if got_s is not None:
            sall = got_s[0]
        for key, t in zip(keys, res4):
            out[key][n] = t.reshape(W[n].shape)
    n_rest = sum(ss_sizes) + 1
    pk = lambda d: _pack_rows(jnp.concatenate([d[n].reshape(-1) for n, _ in REPLICATED]
                                              + [jnp.zeros((n_rest,), F32)]), 1024, 8)
    sg, sd_, sm, sv = [t.reshape(-1) for t in reduce_adamw(sall, pk(W), pk(M), pk(V), "adamw_small")]
    loss = 0.5 * sg[rep_offs[-1]] / D

    dmod_all = sall.reshape(N_DEV, -1)[:, :2 * 6 * D].reshape(N_DEV, 2, 6 * D)
    dmod_mine = lax.dynamic_slice_in_dim(dmod_all, me * (6 * D // N_DEV), 6 * D // N_DEV, axis=2)
    g_ada = [matmul([(cond_all, dmod_mine[:, l])], "tn", f"ada{l}_dw")[None] for l in range(2)]
    ada2d = lambda t: t.reshape(2 * D, 6 * D // N_DEV)
    for key, t in zip(("g", "d", "m", "v"), reduce_adamw(g_ada, ada2d(ada_w), ada2d(m_ada_w),
                                                          ada2d(v_ada_w), "adamw_ada_w")):
        out[key]['ada_w'] = t.reshape(ada_w.shape)
    own = []
    for (n, ax, shp), o in zip(SMALL_SHARDED, rep_offs[len(REPLICATED):]):
        slabs = _to_slabs(sg[o:o + _numel(shp)].reshape(shp), ax)
        own.append(lax.dynamic_index_in_dim(slabs, me, axis=0, keepdims=False))
    own_names = [n for n, _, _ in SMALL_SHARDED]
    pk = lambda d: _pack_rows(jnp.concatenate([d[n].reshape(-1) for n in own_names]), 1024, 8)
    og, od, om, ov = [t.reshape(-1) for t in reduce_adamw(_pack_rows(jnp.concatenate(own), 1024, 8)[None],
                                                          pk(W), pk(M), pk(V), "adamw_own")]

    def unpack(names_shapes, bufs):
        o = 0
        for n, shp in names_shapes:
            k = _numel(shp)
            for key, buf in zip(("g", "d", "m", "v"), bufs):
                out[key][n] = buf[o:o + k].reshape(shp)
            o += k

    unpack(REPLICATED, (sg, sd_, sm, sv))
    unpack([(n, W[n].shape) for n in own_names], (og, od, om, ov))
    return (loss, grad_x[None], *[out["g"][n] for n in WEIGHTS], *[out["d"][n] for n in WEIGHTS],
            *[out["m"][n] for n in WEIGHTS], *[out["v"][n] for n in WEIGHTS])
```

```python
import functools
import math

import numpy as np
import jax
import jax.numpy as jnp
from jax import lax
from jax.experimental import pallas as pl
from jax.experimental.pallas import tpu as pltpu

F32 = jnp.float32
BF16 = jnp.bfloat16

N_DEV = 8
D = 1024
HD = 64
BLK = 128
ATTN_IN = 2304
CB = ATTN_IN // 128
B_BRANCHES = ((128, 1), (512, 4), (2048, 16))
S5_W = 256
S5_P = 1024
DN_H = 6
DN_DK = 128
DN_C = 64
REC_IN = 3340
REC_PAD = 3456
D_FF = 2816
EPS = 1e-6
ADAM_LR, ADAM_B1, ADAM_B2, ADAM_EPS, ADAM_WD, ADAM_STEP = 0.001, 0.9, 0.999, 1e-8, 0.01, 10
VMEM_LIMIT = 48 * 1024 * 1024

ALIBI = np.asarray(2.0 ** (-8.0 * np.arange(1, 17) / 16), dtype=np.float32)


def _cparams(*sem):
    return pltpu.CompilerParams(dimension_semantics=tuple(sem), vmem_limit_bytes=VMEM_LIMIT)


def _tile(n, target):
    if n <= target:
        return n
    best = None
    for t in range(128, target + 1, 128):
        if n % t == 0:
            best = t
    assert best is not None, (n, target)
    return best


def _rtile(n, target, mult=8):
    if n <= target:
        return n
    best = None
    for t in range(mult, target + 1, mult):
        if n % t == 0:
            best = t
    assert best is not None, (n, target)
    return best


def _fold8(x):
    r, c = x.shape
    return x.reshape(r // 8, 8, c).sum(axis=0)


def _sigmoid(x):
    return 1.0 / (1.0 + jnp.exp(-x))


_DIMS = {"nn": (((1,), (0,)), ((), ())), "nt": (((1,), (1,)), ((), ())), "tn": (((0,), (0,)), ((), ()))}


MM_FULL_K = 3584


MM_VMEM_BUDGET = 40 << 20


def matmul(pairs, mode, name, out_dtype=F32, tm=1024, tn=1536, tk=1024):
    a0, b0 = pairs[0]
    if mode == "nn":
        (M, K), N = a0.shape, b0.shape[1]
    elif mode == "nt":
        (M, K), N = a0.shape, b0.shape[0]
    else:
        (K, M), N = a0.shape, b0.shape[1]
        tm = 1536
    tn = _tile(N, tn)
    tk = K if K <= MM_FULL_K else _tile(K, tk)
    nk = K // tk
    npair = len(pairs)
    dims = _DIMS[mode]
    kdim = 0 if mode == "tn" else 1
    tks = [a.shape[kdim] for a, _ in pairs]
    assert all(t == K for t in tks) or (nk == 1 and max(tks) <= MM_FULL_K), tks
    if nk > 1:
        tks = [tk] * npair

    def planned(tm_):
        ab = sum(tm_ * t * a.dtype.itemsize + t * tn * b.dtype.itemsize for (a, b), t in zip(pairs, tks))
        return 2 * ab + 2 * tm_ * tn * jnp.dtype(out_dtype).itemsize + (tm_ * tn * 4 if nk > 1 else 0)

    while True:
        tm_try = _rtile(M, tm) if M % 128 else _tile(M, tm)
        if planned(tm_try) <= MM_VMEM_BUDGET or tm <= 128:
            break
        tm //= 2
    tm = tm_try

    def body(*refs):
        o_ref = refs[2 * npair]
        tot = None
        for p in range(npair):
            part = lax.dot_general(refs[2 * p][...].astype(BF16), refs[2 * p + 1][...].astype(BF16),
                                   dims, preferred_element_type=F32)
            tot = part if tot is None else tot + part
        if nk == 1:
            o_ref[...] = tot.astype(o_ref.dtype)
            return
        acc_ref = refs[2 * npair + 1]
        k = pl.program_id(2)

        @pl.when(k == 0)
        def _():
            acc_ref[...] = tot

        @pl.when(k > 0)
        def _():
            acc_ref[...] += tot

        @pl.when(k == nk - 1)
        def _():
            o_ref[...] = acc_ref[...].astype(o_ref.dtype)

    def specs(t):
        if mode == "nn":
            return [pl.BlockSpec((tm, t), lambda j, i, k: (i, k)), pl.BlockSpec((t, tn), lambda j, i, k: (k, j))]
        if mode == "nt":
            return [pl.BlockSpec((tm, t), lambda j, i, k: (i, k)), pl.BlockSpec((tn, t), lambda j, i, k: (j, k))]
        return [pl.BlockSpec((t, tm), lambda j, i, k: (k, i)), pl.BlockSpec((t, tn), lambda j, i, k: (k, j))]

    flat = [t for pr in pairs for t in pr]
    return pl.pallas_call(
        body, name=name, grid=(N // tn, M // tm, nk),
        in_specs=[s for t in tks for s in specs(t)],
        out_specs=pl.BlockSpec((tm, tn), lambda j, i, k: (i, j)),
        out_shape=jax.ShapeDtypeStruct((M, N), out_dtype),
        scratch_shapes=[pltpu.VMEM((tm, tn), F32)] if nk > 1 else [],
        compiler_params=_cparams("parallel", "parallel", "arbitrary"),
    )(*flat)


def gate_norm_fwd(x, y, gate, nw, sh, sc, name):
    L, C = x.shape
    tl = _rtile(L, 512)
    has_gate = y is not None

    def body(*refs):
        if has_gate:
            x_ref, y_ref, g_ref, nw_ref, sh_ref, sc_ref, xn_ref, h_ref = refs
            xn = x_ref[...] + g_ref[...] * y_ref[...]
            xn_ref[...] = xn
        else:
            x_ref, nw_ref, sh_ref, sc_ref, h_ref = refs
            xn = x_ref[...]
        r = lax.rsqrt(jnp.mean(xn * xn, axis=-1, keepdims=True) + EPS)
        h = (xn * r * nw_ref[...]) * (1.0 + sc_ref[...]) + sh_ref[...]
        h_ref[...] = h.astype(BF16)

    big = pl.BlockSpec((tl, C), lambda i: (i, 0))
    vec = pl.BlockSpec((1, C), lambda i: (0, 0))
    if has_gate:
        ins, in_specs = (x, y, gate, nw, sh, sc), [big, big, vec, vec, vec, vec]
        out_shape = (jax.ShapeDtypeStruct((L, C), F32), jax.ShapeDtypeStruct((L, C), BF16))
        out_specs = (big, big)
    else:
        ins, in_specs = (x, nw, sh, sc), [big, vec, vec, vec]
        out_shape = jax.ShapeDtypeStruct((L, C), BF16)
        out_specs = big
    return pl.pallas_call(body, name=name, grid=(L // tl,), in_specs=in_specs, out_specs=out_specs,
                          out_shape=out_shape, compiler_params=_cparams("parallel"))(*ins)


def gate_norm_bwd(xn, y, gate, nw, sc, dxn_direct, dh, name, comm=None):
    L, C = xn.shape
    tl = _rtile(L, 256)
    has_gate = y is not None
    has_direct = dxn_direct is not None

    def body(*refs):
        refs = list(refs)
        xn_ref = refs.pop(0)
        y_ref = refs.pop(0) if has_gate else None
        g_ref = refs.pop(0) if has_gate else None
        nw_ref = refs.pop(0)
        sc_ref = refs.pop(0)
        dd_ref = refs.pop(0) if has_direct else None
        dh_ref = refs.pop(0)
        dxn_ref = refs.pop(0)
        dy_ref = refs.pop(0) if has_gate else None
        sums_ref = refs.pop(0)

        @pl.when(pl.program_id(0) == 0)
        def _():
            sums_ref[...] = jnp.zeros_like(sums_ref)

        xv = xn_ref[...]
        dh_v = dh_ref[...]
        r = lax.rsqrt(jnp.mean(xv * xv, axis=-1, keepdims=True) + EPS)
        n = xv * r
        a = nw_ref[...] * (1.0 + sc_ref[...])
        dn = dh_v * a
        dx = r * (dn - n * jnp.mean(dn * n, axis=-1, keepdims=True))
        if has_direct:
            dx = dx + dd_ref[...]
        dxn_ref[...] = dx
        sums_ref[8:16, :] += _fold8(dh_v * n)
        sums_ref[16:24, :] += _fold8(dh_v)
        if has_gate:
            dy_ref[...] = (dx * g_ref[...]).astype(BF16)
            sums_ref[0:8, :] += _fold8(dx * y_ref[...])

    big = pl.BlockSpec((tl, C), lambda i: (i, 0))
    vec = pl.BlockSpec((1, C), lambda i: (0, 0))
    ins, in_specs = [xn], [big]
    if has_gate:
        ins += [y, gate]
        in_specs += [big, vec]
    ins += [nw, sc]
    in_specs += [vec, vec]
    if has_direct:
        ins.append(dxn_direct)
        in_specs.append(big)
    ins.append(dh)
    in_specs.append(big)
    out_shape = [jax.ShapeDtypeStruct((L, C), F32)]
    out_specs = [big]
    if has_gate:
        out_shape.append(jax.ShapeDtypeStruct((L, C), BF16))
        out_specs.append(big)
    out_shape.append(jax.ShapeDtypeStruct((32, C), F32))
    out_specs.append(pl.BlockSpec((32, C), lambda i: (0, 0)))
    return _call(body, ins, name=name, grid=(L // tl,), in_specs=in_specs, out_specs=tuple(out_specs),
                 out_shape=tuple(out_shape), sem=("arbitrary",), comm=comm)


def final_loss(x, f, gate, target, name):
    L, C = x.shape
    tl = _rtile(L, 256)

    def body(x_ref, f_ref, g_ref, t_ref, dy_ref, df_ref, sums_ref):
        @pl.when(pl.program_id(0) == 0)
        def _():
            sums_ref[...] = jnp.zeros_like(sums_ref)

        fv = f_ref[...]
        err = x_ref[...] + g_ref[...] * fv - t_ref[...]
        dy = err * (1.0 / C)
        dy_ref[...] = dy
        df_ref[...] = (dy * g_ref[...]).astype(BF16)
        sums_ref[0:8, :] += _fold8(err * err)
        sums_ref[8:16, :] += _fold8(dy * fv)

    big = pl.BlockSpec((tl, C), lambda i: (i, 0))
    vec = pl.BlockSpec((1, C), lambda i: (0, 0))
    return pl.pallas_call(
        body, name=name, grid=(L // tl,), in_specs=[big, big, vec, big],
        out_specs=(big, big, pl.BlockSpec((16, C), lambda i: (0, 0))),
        out_shape=(jax.ShapeDtypeStruct((L, C), F32), jax.ShapeDtypeStruct((L, C), BF16),
                   jax.ShapeDtypeStruct((16, C), F32)),
        compiler_params=_cparams("arbitrary"))(x, f, gate, target)


def _seg_ones(seg):
    r = lax.broadcasted_iota(jnp.int32, (128, 128), 0) // seg
    c = lax.broadcasted_iota(jnp.int32, (128, 128), 1) // seg
    return (r == c).astype(BF16)


def _segsum(t, ones):
    hi = t.astype(BF16)
    lo = (t - hi.astype(F32)).astype(BF16)
    return (jnp.dot(hi, ones, preferred_element_type=F32) + jnp.dot(lo, ones, preferred_element_type=F32))


_NORMED_TILES = tuple(list(range(0, 5)) + list(range(6, 14)))


DIL = (4, 16)
B_COLS0, B_W = 768, 1536
DIL_TL = 256


def _to_dilated(scr_ref, out_ref, d, cast=None):
    nj, tl, _ = scr_ref.shape
    for r in range(d):
        for j in range(nj):
            piece = scr_ref[j, pl.ds(r, tl // d, stride=d), :]
            c0 = (r * nj + j) * 128
            out_ref[:, c0:c0 + 128] = piece if cast is None else piece.astype(cast)


def _from_dilated(in_ref, scr_ref, d):
    nj, tl, _ = scr_ref.shape
    for r in range(d):
        for j in range(nj):
            c0 = (r * nj + j) * 128
            scr_ref[j, pl.ds(r, tl // d, stride=d), :] = in_ref[:, c0:c0 + 128]


def _dil_spec(tl, d, width):
    return pl.BlockSpec((tl // d, d * width), lambda i: (i, 0))


def qknorm_fwd(qkv, wvec, name):
    L, C = qkv.shape
    tl = DIL_TL

    def body(x_ref, w_ref, o_ref, o4_ref, o16_ref, scr_ref):
        ones = _seg_ones(HD)
        for t in range(CB):
            cs = slice(t * 128, (t + 1) * 128)
            x = x_ref[:, cs]
            if t in _NORMED_TILES:
                ms = _segsum(x * x, ones) * (1.0 / HD)
                x = x * lax.rsqrt(ms + EPS) * w_ref[:, cs]
            o_ref[:, cs] = x.astype(BF16)
            if t * 128 >= B_COLS0:
                scr_ref[t - B_COLS0 // 128] = x
        _to_dilated(scr_ref, o4_ref, 4, BF16)
        _to_dilated(scr_ref, o16_ref, 16, BF16)

    return pl.pallas_call(
        body, name=name, grid=(L // tl,),
        in_specs=[pl.BlockSpec((tl, C), lambda i: (i, 0)), pl.BlockSpec((1, C), lambda i: (0, 0))],
        out_specs=(pl.BlockSpec((tl, C), lambda i: (i, 0)), _dil_spec(tl, 4, B_W), _dil_spec(tl, 16, B_W)),
        out_shape=(jax.ShapeDtypeStruct((L, C), BF16), jax.ShapeDtypeStruct((L // 4, 4 * B_W), BF16),
                   jax.ShapeDtypeStruct((L // 16, 16 * B_W), BF16)),
        scratch_shapes=[pltpu.VMEM((B_W // 128, tl, 128), F32)], compiler_params=_cparams("parallel"))(qkv, wvec)


def qknorm_bwd(qkv, wvec, d_a, d_b, name):
    L, C = qkv.shape
    tl = DIL_TL

    def body(x_ref, w_ref, dqa, dka, dva, q1, k1, v1, q4, k4, v4, q16, k16, v16, dx_ref, sums_ref,
             dy_ref, s4_ref, s16_ref):
        @pl.when(pl.program_id(0) == 0)
        def _():
            sums_ref[...] = jnp.zeros_like(sums_ref)

        dy_ref[:, 0:512] = dqa[...]
        for off, ref in ((512, dka), (640, dva)):
            for g in range(2):
                acc = ref[:, g * 256:g * 256 + HD]
                for h in range(1, 4):
                    acc = acc + ref[:, g * 256 + h * HD:g * 256 + (h + 1) * HD]
                dy_ref[:, off + g * HD:off + (g + 1) * HD] = acc
        for off, r1, r4, r16 in ((768, q1, q4, q16), (1280, k1, k4, k16), (1792, v1, v4, v16)):
            _from_dilated(r4, s4_ref, 4)
            _from_dilated(r16, s16_ref, 16)
            for j in range(4):
                dy_ref[:, off + j * 128:off + (j + 1) * 128] = r1[:, j * 128:(j + 1) * 128] + s4_ref[j] + s16_ref[j]

        ones = _seg_ones(HD)
        for t in range(CB):
            cs = slice(t * 128, (t + 1) * 128)
            d = dy_ref[:, cs]
            if t in _NORMED_TILES:
                x = x_ref[:, cs]
                r = lax.rsqrt(_segsum(x * x, ones) * (1.0 / HD) + EPS)
                n = x * r
                dn = d * w_ref[:, cs]
                dx_ref[:, cs] = (r * (dn - n * (_segsum(dn * n, ones) * (1.0 / HD)))).astype(BF16)
                sums_ref[:, cs] += _fold8(d * n)
            else:
                dx_ref[:, cs] = d.astype(BF16)

    big = pl.BlockSpec((tl, C), lambda i: (i, 0))
    p512 = pl.BlockSpec((tl, 512), lambda i: (i, 0))
    return pl.pallas_call(
        body, name=name, grid=(L // tl,),
        in_specs=[big, pl.BlockSpec((1, C), lambda i: (0, 0))] + [p512] * 6 + [_dil_spec(tl, 4, 512)] * 3
        + [_dil_spec(tl, 16, 512)] * 3,
        out_specs=(big, pl.BlockSpec((8, C), lambda i: (0, 0))),
        out_shape=(jax.ShapeDtypeStruct((L, C), BF16), jax.ShapeDtypeStruct((8, C), F32)),
        scratch_shapes=[pltpu.VMEM((tl, C), F32), pltpu.VMEM((4, tl, 128), F32), pltpu.VMEM((4, tl, 128), F32)],
        compiler_params=_cparams("arbitrary"))(qkv, wvec, *d_a, *d_b[0], *d_b[1], *d_b[2])


def _attn_biases(t, slopes, step, maxdist):
    qi = lax.broadcasted_iota(jnp.int32, (BLK, 2 * BLK), 0)
    sj = lax.broadcasted_iota(jnp.int32, (BLK, 2 * BLK), 1)
    dist = BLK + qi - sj
    valid = (dist >= 0) & (dist <= maxdist)
    distf = (step * dist).astype(F32)
    inner = [jnp.where(valid, (-sl) * distf, -jnp.inf) for sl in slopes]
    first = [jnp.where((t > 0) | (sj >= BLK), b, -jnp.inf) for b in inner]
    return inner, first


def _attn_scores(q, kw, bias):
    return lax.dot_general(q, kw, (((1,), (1,)), ((), ())), preferred_element_type=F32) + bias


ATT_NQ = 8


def _attn_operands(nq, hp, gqa, q_ref, kh_ref, kc_ref, vh_ref, vc_ref):
    ops = []
    for b in range(nq):
        rows = slice(b * BLK, (b + 1) * BLK)
        prev = slice((b - 1) * BLK, b * BLK)
        for e in range(2):
            cs = slice(e * HD, (e + 1) * HD)
            if gqa:
                ksel = lambda ref, r: jnp.where(hp >= 2, ref[r, 64:128], ref[r, 0:64])
            else:
                ksel = lambda ref, r, cs=cs: ref[r, cs]
            kprev = ksel(kh_ref, slice(0, BLK)) if b == 0 else ksel(kc_ref, prev)
            vprev = ksel(vh_ref, slice(0, BLK)) if b == 0 else ksel(vc_ref, prev)
            ops.append((b, e, rows, cs, q_ref[rows, cs] * (HD ** -0.5),
                        jnp.concatenate([kprev, ksel(kc_ref, rows)], axis=0),
                        jnp.concatenate([vprev, ksel(vc_ref, rows)], axis=0)))
    return ops


def _attn_specs(cb, q_off, k_off, v_off, gqa):
    kcol = (lambda r, hp: r * cb + k_off) if gqa else (lambda r, hp: r * cb + k_off + hp)
    vcol = (lambda r, hp: r * cb + v_off) if gqa else (lambda r, hp: r * cb + v_off + hp)
    return kcol, vcol


def attn_fwd(X, d, q_off, k_off, v_off, gqa, slope0, maxdist, name, comm=None):
    Ls = X.shape[0]
    nq = min(ATT_NQ, Ls // BLK)
    TQ = nq * BLK
    nt = Ls // TQ
    slopes = jnp.asarray(ALIBI)

    def body(sl_ref, q_ref, kh_ref, kc_ref, vh_ref, vc_ref, o_ref, lse_ref):
        hp, t = pl.program_id(1), pl.program_id(2)
        ops = _attn_operands(nq, hp, gqa, q_ref, kh_ref, kc_ref, vh_ref, vc_ref)
        inner, first = _attn_biases(t, [sl_ref[slope0 + 2 * hp + e] for e in range(2)], d, maxdist)
        s = [_attn_scores(q, kw, first[e] if b == 0 else inner[e]) for (b, e, rows, cs, q, kw, vw) in ops]
        m = [jnp.max(x, axis=-1, keepdims=True) for x in s]
        p = [jnp.exp(x - mm) for x, mm in zip(s, m)]
        l = [jnp.sum(x, axis=-1, keepdims=True) for x in p]
        o = [jnp.dot(x.astype(BF16), op[6], preferred_element_type=F32) / ll for x, op, ll in zip(p, ops, l)]
        for (b, e, rows, cs, q, kw, vw), oo, mm, ll in zip(ops, o, m, l):
            o_ref[rows, cs] = oo
            lse_ref[rows, cs] = jnp.broadcast_to(mm + jnp.log(ll), (BLK, HD))

    cb = X.shape[1] // (d * 128)
    kcol, vcol = _attn_specs(cb, q_off, k_off, v_off, gqa)
    tile, blk = (TQ, 128), (BLK, 128)
    halo = lambda t: jnp.maximum(t * nq - 1, 0)
    in_specs = [
        pl.BlockSpec(memory_space=pltpu.SMEM),
        pl.BlockSpec(tile, lambda r, hp, t: (t, r * cb + q_off + hp)),
        pl.BlockSpec(blk, lambda r, hp, t: (halo(t), kcol(r, hp))),
        pl.BlockSpec(tile, lambda r, hp, t: (t, kcol(r, hp))),
        pl.BlockSpec(blk, lambda r, hp, t: (halo(t), vcol(r, hp))),
        pl.BlockSpec(tile, lambda r, hp, t: (t, vcol(r, hp))),
    ]
    out_spec = pl.BlockSpec(tile, lambda r, hp, t: (t, r * 4 + hp))
    out = jax.ShapeDtypeStruct((Ls, d * 512), F32)
    return _call(body, (slopes, X, X, X, X, X), name=name, grid=(d, 4, nt), in_specs=in_specs,
                 out_specs=(out_spec, out_spec), out_shape=(out, out),
                 sem=("parallel", "parallel", "arbitrary"), comm=comm)


def attn_bwd(X, o, lse, do, dlse, d, q_off, k_off, v_off, gqa, slope0, maxdist, name, comm=None):
    Ls = X.shape[0]
    slopes = jnp.asarray(ALIBI)

    nq = min(ATT_NQ, Ls // BLK)
    TQ = nq * BLK
    nt = Ls // TQ
    nt_dims, tn_dims = (((1,), (1,)), ((), ())), (((0,), (0,)), ((), ()))

    def body(sl_ref, q_ref, kh_ref, kc_ref, vh_ref, vc_ref, o_ref, lse_ref, do_ref, dlse_ref,
             dq_ref, dk_ref, dv_ref, ak_ref, av_ref, pk_ref, pv_ref):
        hp, t = pl.program_id(1), pl.program_id(2)

        @pl.when(t == 0)
        def _():
            pk_ref[...] = jnp.zeros_like(pk_ref)
            pv_ref[...] = jnp.zeros_like(pv_ref)

        @pl.when(t < nt)
        def _():
            ops = _attn_operands(nq, hp, gqa, q_ref, kh_ref, kc_ref, vh_ref, vc_ref)
            inner, first = _attn_biases(t, [sl_ref[slope0 + 2 * hp + e] for e in range(2)], d, maxdist)
            sv = [_attn_scores(q, kw, first[e] if b == 0 else inner[e]) for (b, e, rows, cs, q, kw, vw) in ops]
            p = [jnp.exp(s - lse_ref[op[2], op[1] * HD:op[1] * HD + 1]) for s, op in zip(sv, ops)]
            dov = [do_ref[op[2], op[3]] for op in ops]
            delta = [jnp.sum(dd * o_ref[op[2], op[3]], axis=-1, keepdims=True) for dd, op in zip(dov, ops)]
            dob = [dd.astype(BF16) for dd in dov]
            dp = [lax.dot_general(dd, op[6], nt_dims, preferred_element_type=F32) for dd, op in zip(dob, ops)]
            ds = [(pp * (x - dl + dlse_ref[op[2], op[1] * HD:op[1] * HD + 1])).astype(BF16)
                  for pp, x, dl, op in zip(p, dp, delta, ops)]
            dq = [jnp.dot(x, op[5], preferred_element_type=F32) * (HD ** -0.5) for x, op in zip(ds, ops)]
            dkw = [lax.dot_general(x, op[4], tn_dims, preferred_element_type=F32) for x, op in zip(ds, ops)]
            dvw = [lax.dot_general(pp.astype(BF16), dd, tn_dims, preferred_element_type=F32)
                   for pp, dd in zip(p, dob)]
            ak_ref[...] = jnp.zeros_like(ak_ref)
            av_ref[...] = jnp.zeros_like(av_ref)
            for (b, e, rows, cs, q, kw, vw), x, yk, yv in zip(ops, dq, dkw, dvw):
                dq_ref[rows, cs] = x
                ak_ref[b * BLK:(b + 2) * BLK, cs] += yk
                av_ref[b * BLK:(b + 2) * BLK, cs] += yv
            if nt == 1:
                dk_ref[...] = ak_ref[BLK:, :]
                dv_ref[...] = av_ref[BLK:, :]
                return
            last = slice(TQ - BLK, TQ)
            dk_ref[...] = pk_ref[...]
            dv_ref[...] = pv_ref[...]
            dk_ref[last, :] += ak_ref[0:BLK, :]
            dv_ref[last, :] += av_ref[0:BLK, :]
            pk_ref[...] = ak_ref[BLK:, :]
            pv_ref[...] = av_ref[BLK:, :]

        @pl.when(t == nt)
        def _():
            dk_ref[...] = pk_ref[...]
            dv_ref[...] = pv_ref[...]

    cb = X.shape[1] // (d * 128)
    kcol, vcol = _attn_specs(cb, q_off, k_off, v_off, gqa)
    tile, blk = (TQ, 128), (BLK, 128)
    cur = lambda t: jnp.minimum(t, nt - 1)
    halo = lambda t: jnp.maximum(cur(t) * nq - 1, 0)
    ospec = pl.BlockSpec(tile, lambda r, hp, t: (cur(t), r * 4 + hp))
    in_specs = [
        pl.BlockSpec(memory_space=pltpu.SMEM),
        pl.BlockSpec(tile, lambda r, hp, t: (cur(t), r * cb + q_off + hp)),
        pl.BlockSpec(blk, lambda r, hp, t: (halo(t), kcol(r, hp))),
        pl.BlockSpec(tile, lambda r, hp, t: (cur(t), kcol(r, hp))),
        pl.BlockSpec(blk, lambda r, hp, t: (halo(t), vcol(r, hp))),
        pl.BlockSpec(tile, lambda r, hp, t: (cur(t), vcol(r, hp))),
        ospec, ospec, ospec, ospec,
    ]
    shifted = pl.BlockSpec(tile, lambda r, hp, t: (jnp.maximum(t - 1, 0), r * 4 + hp))
    out = jax.ShapeDtypeStruct((Ls, d * 512), F32)
    return _call(body, (slopes, X, X, X, X, X, o, lse, do, dlse), name=name, grid=(d, 4, nt + 1 if nt > 1 else 1),
                 in_specs=in_specs, out_specs=(ospec, shifted, shifted), out_shape=(out, out, out),
                 scratch_shapes=[pltpu.VMEM((TQ + BLK, 128), F32), pltpu.VMEM((TQ + BLK, 128), F32),
                                 pltpu.VMEM((TQ, 128), F32), pltpu.VMEM((TQ, 128), F32)],
                 sem=("parallel", "parallel", "arbitrary"), comm=comm)


def attn_merge_fwd(oa, la, sink, obs, lbs, name):
    L = oa.shape[0]
    tl = DIL_TL

    def body(oa_ref, la_ref, sk_ref, o1, o4, o16, l1, l4, l16, m_ref, so4, so16, sl4, sl16):
        m_ref[:, 0:512] = (oa_ref[...] * _sigmoid(la_ref[...] - sk_ref[...])).astype(BF16)
        for src, dst, d in ((o4, so4, 4), (o16, so16, 16), (l4, sl4, 4), (l16, sl16, 16)):
            _from_dilated(src, dst, d)
        for j in range(4):
            cs = slice(j * 128, (j + 1) * 128)
            a, b, c = l1[:, cs], sl4[j], sl16[j]
            mx = jnp.maximum(jnp.maximum(a, b), c)
            ea, eb, ec = jnp.exp(a - mx), jnp.exp(b - mx), jnp.exp(c - mx)
            inv = 1.0 / (ea + eb + ec)
            m_ref[:, 512 + j * 128:512 + (j + 1) * 128] = (
                (ea * inv) * o1[:, cs] + (eb * inv) * so4[j] + (ec * inv) * so16[j]).astype(BF16)

    big = pl.BlockSpec((tl, 512), lambda i: (i, 0))
    dil = [big, _dil_spec(tl, 4, 512), _dil_spec(tl, 16, 512)]
    return pl.pallas_call(
        body, name=name, grid=(L // tl,),
        in_specs=[big, big, pl.BlockSpec((1, 512), lambda i: (0, 0))] + dil + dil,
        out_specs=pl.BlockSpec((tl, 1024), lambda i: (i, 0)),
        out_shape=jax.ShapeDtypeStruct((L, 1024), BF16), scratch_shapes=[pltpu.VMEM((4, tl, 128), F32)] * 4,
        compiler_params=_cparams("parallel"),
    )(oa, la, sink, *obs, *lbs)


def attn_merge_bwd(dm, oa, la, sink, obs, lbs, name):
    L = oa.shape[0]
    tl = DIL_TL

    def body(dm_ref, oa_ref, la_ref, sk_ref, o1, o4, o16, l1, l4, l16,
             doa_ref, dla_ref, d1, d4, d16, g1, g4, g16, sums_ref, so4, so16, sl4, sl16, sd4, sd16, sg4, sg16):
        @pl.when(pl.program_id(0) == 0)
        def _():
            sums_ref[...] = jnp.zeros_like(sums_ref)

        for src, dst, d in ((o4, so4, 4), (o16, so16, 16), (l4, sl4, 4), (l16, sl16, 16)):
            _from_dilated(src, dst, d)
        ones = _seg_ones(HD)
        for t in range(4):
            cs = slice(t * 128, (t + 1) * 128)
            dma = dm_ref[:, cs]
            keep = _sigmoid(la_ref[:, cs] - sk_ref[:, cs])
            doa_ref[:, cs] = dma * keep
            tt = dma * oa_ref[:, cs] * keep * (1.0 - keep)
            dla_ref[:, cs] = _segsum(tt, ones)
            sums_ref[:, cs] += _fold8(-tt)
            dmb = dm_ref[:, 512 + t * 128:512 + (t + 1) * 128]
            a, b, c = l1[:, cs], sl4[t], sl16[t]
            mx = jnp.maximum(jnp.maximum(a, b), c)
            ea, eb, ec = jnp.exp(a - mx), jnp.exp(b - mx), jnp.exp(c - mx)
            inv = 1.0 / (ea + eb + ec)
            wa, wb, wc = ea * inv, eb * inv, ec * inv
            d1[:, cs] = wa * dmb
            sd4[t] = wb * dmb
            sd16[t] = wc * dmb
            sa = _segsum(dmb * o1[:, cs], ones)
            sb = _segsum(dmb * so4[t], ones)
            sc_ = _segsum(dmb * so16[t], ones)
            mean = wa * sa + wb * sb + wc * sc_
            g1[:, cs] = wa * (sa - mean)
            sg4[t] = wb * (sb - mean)
            sg16[t] = wc * (sc_ - mean)
        for src, dst, d in ((sd4, d4, 4), (sd16, d16, 16), (sg4, g4, 4), (sg16, g16, 16)):
            _to_dilated(src, dst, d)

    big = pl.BlockSpec((tl, 512), lambda i: (i, 0))
    dil = [big, _dil_spec(tl, 4, 512), _dil_spec(tl, 16, 512)]
    sd = jax.ShapeDtypeStruct
    shp = [sd((L, 512), F32), sd((L // 4, 4 * 512), F32), sd((L // 16, 16 * 512), F32)]
    return pl.pallas_call(
        body, name=name, grid=(L // tl,),
        in_specs=[pl.BlockSpec((tl, 1024), lambda i: (i, 0)), big, big,
                  pl.BlockSpec((1, 512), lambda i: (0, 0))] + dil + dil,
        out_specs=tuple([big, big] + dil + dil + [pl.BlockSpec((8, 512), lambda i: (0, 0))]),
        out_shape=tuple([shp[0], shp[0]] + shp + shp + [sd((8, 512), F32)]),
        scratch_shapes=[pltpu.VMEM((4, tl, 128), F32)] * 8, compiler_params=_cparams("arbitrary"),
    )(dm, oa, la, sink, *obs, *lbs)


def _shift_down(x, halo, k, first):
    rows = lax.broadcasted_iota(jnp.int32, (8, x.shape[1]), 0)
    out = pltpu.roll(x, k, axis=0)
    hrows = jnp.where(first, 0.0, pltpu.roll(halo, k, axis=0))
    top = jnp.where(rows < k, hrows, out[0:8, :])
    return jnp.concatenate([top, out[8:, :]], axis=0)


def _shift_up(x, nxt, k):
    tl = x.shape[0]
    rows = lax.broadcasted_iota(jnp.int32, (8, x.shape[1]), 0)
    out = pltpu.roll(x, tl - k, axis=0)
    bottom = jnp.where(rows >= 8 - k, pltpu.roll(nxt, 8 - k, axis=0), out[tl - 8:, :])
    return jnp.concatenate([out[:tl - 8, :], bottom], axis=0)


def _silu(x):
    return x * _sigmoid(x)


def _dsilu(x):
    s = _sigmoid(x)
    return s * (1.0 + x * (1.0 - s))


def ffn_act_fwd(ua, ub, cw, name, comm=None):
    L, F = ua.shape
    tl = _rtile(L, 256)
    tc = _tile(F, 1408)
    hb = tl // 8

    def body(ua_ref, uah_ref, ub_ref, ubh_ref, wa_ref, wb_ref, o_ref, ac_ref, bc_ref):
        first = pl.program_id(1) == 0

        def conv(x_ref, h_ref, w_ref):
            x = x_ref[...]
            h = h_ref[...]
            return (w_ref[2:3, :] * x + w_ref[1:2, :] * _shift_down(x, h, 1, first)
                    + w_ref[0:1, :] * _shift_down(x, h, 2, first))

        a = conv(ua_ref, uah_ref, wa_ref)
        b = conv(ub_ref, ubh_ref, wb_ref)
        ac_ref[...] = a
        bc_ref[...] = b
        o_ref[...] = (_silu(a) * b).astype(BF16)

    main = pl.BlockSpec((tl, tc), lambda j, i: (i, j))
    halo = pl.BlockSpec((8, tc), lambda j, i: (jnp.maximum(i * hb - 1, 0), j))
    wa = pl.BlockSpec((3, tc), lambda j, i: (0, j))
    wb = pl.BlockSpec((3, tc), lambda j, i: (0, j + F // tc))
    f32 = jax.ShapeDtypeStruct((L, F), F32)
    return _call(body, (ua, ua, ub, ub, cw, cw), name=name, grid=(F // tc, L // tl),
                 in_specs=[main, halo, main, halo, wa, wb], out_specs=(main, main, main),
                 out_shape=(jax.ShapeDtypeStruct((L, F), BF16), f32, f32), sem=("parallel", "parallel"), comm=comm)


def ffn_act_bwd(ua, ub, ac, bc, cw, dact, name, comm=None):
    L, F = ua.shape
    tl = _rtile(L, 256)
    tc = _tile(F, 1408)
    nrt = L // tl

    def body(ua_ref, ub_ref, ac_ref, bc_ref, wa_ref, wb_ref, da_ref, dua_ref, dub_ref, sums_ref, ca_ref, cb_ref):
        i = pl.program_id(1)

        @pl.when(i == 0)
        def _():
            sums_ref[...] = jnp.zeros_like(sums_ref)
            ca_ref[...] = jnp.zeros_like(ca_ref)
            cb_ref[...] = jnp.zeros_like(cb_ref)

        a, b = ac_ref[...], bc_ref[...]
        dact_v = da_ref[...]
        dya = dact_v * b * _dsilu(a)
        dyb = dact_v * _silu(a)
        for (dy, w_ref, c_ref, d_ref, x_ref, base) in ((dya, wa_ref, ca_ref, dua_ref, ua_ref, 0),
                                                        (dyb, wb_ref, cb_ref, dub_ref, ub_ref, 24)):
            nxt = c_ref[...]
            ups = (dy, _shift_up(dy, nxt, 1), _shift_up(dy, nxt, 2))
            d_ref[...] = (w_ref[2:3, :] * ups[0] + w_ref[1:2, :] * ups[1] + w_ref[0:1, :] * ups[2]).astype(BF16)
            c_ref[...] = dy[0:8, :]
            x = x_ref[...]
            for k in range(3):
                sums_ref[base + 8 * (2 - k):base + 8 * (2 - k) + 8, :] += _fold8(ups[k] * x)

    rev = lambda i: nrt - 1 - i
    main = pl.BlockSpec((tl, tc), lambda j, i: (rev(i), j))
    wa = pl.BlockSpec((3, tc), lambda j, i: (0, j))
    wb = pl.BlockSpec((3, tc), lambda j, i: (0, j + F // tc))
    ob = jax.ShapeDtypeStruct((L, F), BF16)
    return _call(body, (ua, ub, ac, bc, cw, cw, dact), name=name, grid=(F // tc, nrt),
                 in_specs=[main, main, main, main, wa, wb, main],
                 out_specs=(main, main, pl.BlockSpec((48, tc), lambda j, i: (0, j))),
                 out_shape=(ob, ob, jax.ShapeDtypeStruct((48, F), F32)),
                 scratch_shapes=[pltpu.VMEM((8, tc), F32), pltpu.VMEM((8, tc), F32)],
                 sem=("parallel", "arbitrary"), comm=comm)


def attn_vectors(qna, kna, qnb, knb, sinks):
    ones = jnp.ones((128,), F32)
    wvec = jnp.concatenate([jnp.tile(qna, 8), jnp.tile(kna, 2), ones, jnp.tile(qnb, 8), jnp.tile(knb, 8),
                            jnp.tile(ones, 4)]).reshape(1, ATTN_IN)
    return wvec, jnp.repeat(sinks, HD).reshape(1, 512)


def _with_comm(result, comm):
    return result if comm is not None else (result, None)


def attention_block_fwd(h, w_in, wvec, sinkvec, w_out, tag, comms=None):
    L = h.shape[0]
    comms = comms or {}
    got = {}
    qkv = matmul([(h, w_in)], "nn", tag + "_qkv")
    X, X4, X16 = qknorm_fwd(qkv, wvec, tag + "_qknorm")
    (oa, la), got['swa'] = _with_comm(attn_fwd(X, 1, 0, 4, 5, True, 0, BLK - 1, tag + "_swa",
                                               comm=comms.get('swa')), comms.get('swa'))
    views = {1: (X, 6, 10, 14), 4: (X4, 0, 4, 8), 16: (X16, 0, 4, 8)}
    obs, lbs = [], []
    for window, d in B_BRANCHES:
        xd, qo, ko, vo = views[d]
        (o, l), got[d] = _with_comm(attn_fwd(xd, d, qo, ko, vo, False, 8, window // d,
                                             tag + f"_dil{d}", comm=comms.get(d)), comms.get(d))
        obs.append(o)
        lbs.append(l)
    m = attn_merge_fwd(oa, la, sinkvec, obs, lbs, tag + "_merge")
    if w_out is None:
        w_out = got[16][0].reshape(D, D)
        got['w_out'] = w_out
    y = matmul([(m, w_out)], "nn", tag + "_out")
    return y, (h, qkv, views, oa, la, obs, lbs, m), got


def attention_block_bwd(dy, res, w_in, wvec, sinkvec, w_out, tag, comms=None, send_w_out_on=None):
    h, qkv, views, oa, la, obs, lbs, m = res
    comms = dict(comms or {})
    got = {}
    g_w_out = matmul([(m, dy)], "tn", tag + "_dwout", out_dtype=BF16)
    if send_w_out_on is not None:
        comms[send_w_out_on] = ([g_w_out.reshape(N_DEV, D // N_DEV, D)], False)
    dm = matmul([(dy, w_out)], "nt", tag + "_dm")
    doa, dla, d1, d2, d3, g1, g2, g3, sinksums = attn_merge_bwd(dm, oa, la, sinkvec, obs, lbs, tag + "_dmerge")
    d_a, got['swa'] = _with_comm(attn_bwd(views[1][0], oa, la, doa, dla, 1, 0, 4, 5, True, 0, BLK - 1,
                                          tag + "_dswa", comm=comms.get('swa')), comms.get('swa'))
    d_b = []
    for (window, d), o, l, do, dl in zip(B_BRANCHES, obs, lbs, (d1, d2, d3), (g1, g2, g3)):
        xd, qo, ko, vo = views[d]
        dqkv_d, got[d] = _with_comm(attn_bwd(xd, o, l, do, dl, d, qo, ko, vo, False, 8, window // d,
                                             tag + f"_ddil{d}", comm=comms.get(d)), comms.get(d))
        d_b.append(dqkv_d)
    dqkv, wsums = qknorm_bwd(qkv, wvec, d_a, d_b, tag + "_dqknorm")
    g_w_in = matmul([(h, dqkv)], "tn", tag + "_dwin", out_dtype=BF16)
    dh = matmul([(dqkv, w_in)], "nt", tag + "_dh")
    ws = wsums.sum(axis=0)
    grads = dict(
        w_in=g_w_in, w_out=g_w_out,
        q_norm_a=ws[0:512].reshape(8, HD).sum(axis=0), k_norm_a=ws[512:640].reshape(2, HD).sum(axis=0),
        q_norm_b=ws[768:1280].reshape(8, HD).sum(axis=0), k_norm_b=ws[1280:1792].reshape(8, HD).sum(axis=0),
        sinks=sinksums.sum(axis=0).reshape(8, HD).sum(axis=1))
    return dh, grads, got


def ffn_block_fwd(h, w_up_a, w_up_b, cw, w_down, tag, comm=None):
    ua = matmul([(h, w_up_a)], "nn", tag + "_upa")
    ub = matmul([(h, w_up_b)], "nn", tag + "_upb")
    (act, ac, bc), got = _with_comm(ffn_act_fwd(ua, ub, cw, tag + "_act", comm=comm), comm)
    f = matmul([(act, w_down)], "nn", tag + "_down")
    return f, (h, ua, ub, ac, bc, act), got


def ffn_block_bwd(df, res, w_up_a, w_up_b, cw, w_down, tag, comm=None):
    h, ua, ub, ac, bc, act = res
    g_down = matmul([(act, df)], "tn", tag + "_dwdown", out_dtype=BF16)
    dact = matmul([(df, w_down)], "nt", tag + "_dact")
    (dua, dub, sums), got = _with_comm(ffn_act_bwd(ua, ub, ac, bc, cw, dact, tag + "_dactk", comm=comm), comm)
    g_up = jnp.concatenate([_cols_to_slabs(matmul([(h, dua)], "tn", tag + "_dwupa", out_dtype=BF16), N_DEV // 2),
                            _cols_to_slabs(matmul([(h, dub)], "tn", tag + "_dwupb", out_dtype=BF16), N_DEV // 2)],
                           axis=0)
    dh = matmul([(dua, w_up_a), (dub, w_up_b)], "nt", tag + "_dh")
    s = sums.reshape(2, 3, 8, D_FF).sum(axis=2)
    g_conv = jnp.concatenate([s[0], s[1]], axis=1)
    return dh, dict(w_up=g_up, conv=g_conv, w_down=g_down), got


def s5_params(lam_re, lam_im, log_dt, b_re, b_im, c_re, c_im):
    dt = jnp.exp(log_dt)[:, None]
    mag, ang = jnp.exp(lam_re * dt), lam_im * dt
    a_re, a_im = mag * jnp.cos(ang), mag * jnp.sin(ang)
    nr, ni = a_re - 1.0, a_im
    den = lam_re * lam_re + lam_im * lam_im
    f_re = (nr * lam_re + ni * lam_im) / den
    f_im = (ni * lam_re - nr * lam_im) / den
    eye = jnp.eye(16, dtype=F32)[:, None, :, None]
    bd = lambda b: (eye * jnp.transpose(b, (0, 2, 1))[:, :, None, :]).reshape(S5_W, S5_P)
    cd = lambda c: (eye * jnp.transpose(c, (0, 2, 1))[:, :, None, :]).reshape(S5_P, S5_W)
    flat = lambda t: t.reshape(1, S5_P)
    return flat(a_re), flat(a_im), flat(f_re), flat(f_im), bd(b_re), bd(b_im), cd(c_re), cd(c_im)


def _scan_tables(a_re, a_im, reverse):
    pows = [(a_re, a_im)]
    for _ in range(7):
        pr, pi = pows[-1]
        pows.append((pr * a_re - pi * a_im, pr * a_im + pi * a_re))
    order = list(range(7, -1, -1)) if reverse else list(range(8))
    z = jnp.zeros_like(a_re)
    rows = [pows[0][0], pows[0][1], pows[1][0], pows[1][1], pows[3][0], pows[3][1], z, z]
    rows += [pows[k][0] for k in order] + [pows[k][1] for k in order]
    return jnp.concatenate(rows, axis=0)


def _block_scan(er, ei, tab_ref, cr, ci, reverse):
    rows = lax.broadcasted_iota(jnp.int32, er.shape, 0)
    for idx, s in enumerate((1, 2, 4)):
        if reverse:
            sr, si, keep = pltpu.roll(er, 8 - s, axis=0), pltpu.roll(ei, 8 - s, axis=0), rows < 8 - s
        else:
            sr, si, keep = pltpu.roll(er, s, axis=0), pltpu.roll(ei, s, axis=0), rows >= s
        sr, si = jnp.where(keep, sr, 0.0), jnp.where(keep, si, 0.0)
        ar, ai = tab_ref[2 * idx:2 * idx + 1, :], tab_ref[2 * idx + 1:2 * idx + 2, :]
        er, ei = er + ar * sr - ai * si, ei + ar * si + ai * sr
    pr, pi_ = tab_ref[8:16, :], tab_ref[16:24, :]
    er, ei = er + pr * cr - pi_ * ci, ei + pr * ci + pi_ * cr
    return er, ei


def s5_scan_fwd(bu_re, bu_im, a_re, a_im, f_re, f_im, name):
    L, P = bu_re.shape
    tl = _rtile(L, 512)
    tab = _scan_tables(a_re, a_im, False)
    fvec = jnp.concatenate([f_re, f_im] + [jnp.zeros_like(f_re)] * 6, axis=0)

    def body(br_ref, bi_ref, tab_ref, f_ref, xr_ref, xi_ref, c_ref):
        @pl.when(pl.program_id(0) == 0)
        def _():
            c_ref[...] = jnp.zeros_like(c_ref)

        def blk(i, carry):
            cr, ci = carry
            rows = pl.ds(pl.multiple_of(i * 8, 8), 8)
            br, bi = br_ref[rows, :], bi_ref[rows, :]
            fr, fi = f_ref[0:1, :], f_ref[1:2, :]
            er, ei = _block_scan(fr * br - fi * bi, fr * bi + fi * br, tab_ref, cr, ci, False)
            xr_ref[rows, :] = er
            xi_ref[rows, :] = ei
            return er[7:8, :], ei[7:8, :]

        cr, ci = lax.fori_loop(0, tl // 8, blk, (c_ref[0:1, :], c_ref[1:2, :]))
        c_ref[0:1, :] = cr
        c_ref[1:2, :] = ci

    big = pl.BlockSpec((tl, P), lambda i: (i, 0))
    out = jax.ShapeDtypeStruct((L, P), F32)
    return pl.pallas_call(
        body, name=name, grid=(L // tl,),
        in_specs=[big, big, pl.BlockSpec((24, P), lambda i: (0, 0)), pl.BlockSpec((8, P), lambda i: (0, 0))],
        out_specs=(big, big), out_shape=(out, out), scratch_shapes=[pltpu.VMEM((8, P), F32)],
        compiler_params=_cparams("arbitrary"))(bu_re, bu_im, tab, fvec)


def s5_scan_bwd(dx_re, dx_im, x_re, x_im, bu_re, bu_im, a_re, a_im, f_re, f_im, name):
    L, P = dx_re.shape
    tl = _rtile(L, 256)
    nt = L // tl
    tab = _scan_tables(a_re, -a_im, True)
    fvec = jnp.concatenate([f_re, f_im] + [jnp.zeros_like(f_re)] * 6, axis=0)

    def body(gr_ref, gi_ref, xr_ref, xi_ref, br_ref, bi_ref, tab_ref, f_ref, dbr_ref, dbi_ref, s_ref, c_ref):
        @pl.when(pl.program_id(0) == 0)
        def _():
            c_ref[...] = jnp.zeros_like(c_ref)
            s_ref[...] = jnp.zeros_like(s_ref)

        def blk(k, carry):
            cr, ci = carry
            i = tl // 8 - 1 - k
            rows = pl.ds(pl.multiple_of(i * 8, 8), 8)
            er, ei = _block_scan(gr_ref[rows, :], gi_ref[rows, :], tab_ref, cr, ci, True)
            rid = lax.broadcasted_iota(jnp.int32, er.shape, 0)
            sr = jnp.where(rid == 7, cr, pltpu.roll(er, 7, axis=0))
            si = jnp.where(rid == 7, ci, pltpu.roll(ei, 7, axis=0))
            xr, xi = xr_ref[rows, :], xi_ref[rows, :]
            s_ref[0:8, :] += sr * xr + si * xi
            s_ref[8:16, :] += si * xr - sr * xi
            br, bi = br_ref[rows, :], bi_ref[rows, :]
            s_ref[16:24, :] += er * br + ei * bi
            s_ref[24:32, :] += ei * br - er * bi
            fr, fi = f_ref[0:1, :], f_ref[1:2, :]
            dbr_ref[rows, :] = fr * er + fi * ei
            dbi_ref[rows, :] = fr * ei - fi * er
            return er[0:1, :], ei[0:1, :]

        cr, ci = lax.fori_loop(0, tl // 8, blk, (c_ref[0:1, :], c_ref[1:2, :]))
        c_ref[0:1, :] = cr
        c_ref[1:2, :] = ci

    big = pl.BlockSpec((tl, P), lambda i: (nt - 1 - i, 0))
    out = jax.ShapeDtypeStruct((L, P), F32)
    return pl.pallas_call(
        body, name=name, grid=(nt,),
        in_specs=[big] * 6 + [pl.BlockSpec((24, P), lambda i: (0, 0)), pl.BlockSpec((8, P), lambda i: (0, 0))],
        out_specs=(big, big, pl.BlockSpec((32, P), lambda i: (0, 0))),
        out_shape=(out, out, jax.ShapeDtypeStruct((32, P), F32)), scratch_shapes=[pltpu.VMEM((8, P), F32)],
        compiler_params=_cparams("arbitrary"))(dx_re, dx_im, x_re, x_im, bu_re, bu_im, tab, fvec)


_GK, _GC = math.sqrt(2.0 / math.pi), 0.044715


def _gelu(y):
    return 0.5 * y * (1.0 + jnp.tanh(_GK * (y + _GC * y * y * y)))


def _dgelu(y):
    t = jnp.tanh(_GK * (y + _GC * y * y * y))
    return 0.5 * (1.0 + t) + 0.5 * y * (1.0 - t * t) * _GK * (1.0 + 3.0 * _GC * y * y)


def s5_out_fwd(x_re, x_im, u, cd_re, cd_im, dskip, glu_w, glu_b, name):
    L = u.shape[0]
    tl = _rtile(L, 512)

    def body(xr_ref, xi_ref, u_ref, cr_ref, ci_ref, d_ref, w_ref, b_ref, y_ref, o_ref):
        y = (jnp.dot(xr_ref[...].astype(BF16), cr_ref[...], preferred_element_type=F32)
             - jnp.dot(xi_ref[...].astype(BF16), ci_ref[...], preferred_element_type=F32)
             + d_ref[...] * u_ref[...])
        y_ref[...] = y
        g = _gelu(y)
        z = jnp.dot(g.astype(BF16), w_ref[...], preferred_element_type=F32) + b_ref[...]
        o_ref[...] = (g * _sigmoid(z)).astype(BF16)

    big = pl.BlockSpec((tl, S5_P), lambda i: (i, 0))
    sm = pl.BlockSpec((tl, S5_W), lambda i: (i, 0))
    full = lambda r, c: pl.BlockSpec((r, c), lambda i: (0, 0))
    return pl.pallas_call(
        body, name=name, grid=(L // tl,),
        in_specs=[big, big, sm, full(S5_P, S5_W), full(S5_P, S5_W), full(1, S5_W), full(S5_W, S5_W), full(1, S5_W)],
        out_specs=(sm, sm),
        out_shape=(jax.ShapeDtypeStruct((L, S5_W), F32), jax.ShapeDtypeStruct((L, S5_W), BF16)),
        compiler_params=_cparams("parallel"))(x_re, x_im, u, cd_re, cd_im, dskip, glu_w, glu_b)


def s5_out_bwd(dout, y, u, x_re, x_im, cd_re, cd_im, dskip, glu_w, glu_b, name, dout_col=0):
    L = u.shape[0]
    tl = _rtile(L, 256)
    nt_dims = (((1,), (1,)), ((), ()))
    tn_dims = (((0,), (0,)), ((), ()))

    def body(do_ref, y_ref, u_ref, xr_ref, xi_ref, cr_ref, ci_ref, d_ref, w_ref, b_ref,
             dxr_ref, dxi_ref, du_ref, dcr_ref, dci_ref, dw_ref, s_ref):
        @pl.when(pl.program_id(0) == 0)
        def _():
            dcr_ref[...] = jnp.zeros_like(dcr_ref)
            dci_ref[...] = jnp.zeros_like(dci_ref)
            dw_ref[...] = jnp.zeros_like(dw_ref)
            s_ref[...] = jnp.zeros_like(s_ref)

        yv, dov = y_ref[...], do_ref[...]
        g = _gelu(yv)
        gb = g.astype(BF16)
        sg = _sigmoid(jnp.dot(gb, w_ref[...], preferred_element_type=F32) + b_ref[...])
        dz = dov * g * sg * (1.0 - sg)
        dzb = dz.astype(BF16)
        dg = dov * sg + lax.dot_general(dzb, w_ref[...], nt_dims, preferred_element_type=F32)
        dw_ref[...] += lax.dot_general(gb, dzb, tn_dims, preferred_element_type=F32)
        dy = dg * _dgelu(yv)
        dyb = dy.astype(BF16)
        s_ref[0:8, :] += _fold8(dy * u_ref[...])
        s_ref[8:16, :] += _fold8(dz)
        du_ref[...] = dy * d_ref[...]
        dxr_ref[...] = lax.dot_general(dyb, cr_ref[...], nt_dims, preferred_element_type=F32)
        dxi_ref[...] = -lax.dot_general(dyb, ci_ref[...], nt_dims, preferred_element_type=F32)
        dcr_ref[...] += lax.dot_general(xr_ref[...].astype(BF16), dyb, tn_dims, preferred_element_type=F32)
        dci_ref[...] -= lax.dot_general(xi_ref[...].astype(BF16), dyb, tn_dims, preferred_element_type=F32)

    big = pl.BlockSpec((tl, S5_P), lambda i: (i, 0))
    sm = pl.BlockSpec((tl, S5_W), lambda i: (i, 0))
    full = lambda r, c: pl.BlockSpec((r, c), lambda i: (0, 0))
    sd = jax.ShapeDtypeStruct
    return pl.pallas_call(
        body, name=name, grid=(L // tl,),
        in_specs=[pl.BlockSpec((tl, S5_W), lambda i: (i, dout_col)), sm, sm, big, big, full(S5_P, S5_W),
                  full(S5_P, S5_W), full(1, S5_W), full(S5_W, S5_W), full(1, S5_W)],
        out_specs=(big, big, sm, full(S5_P, S5_W), full(S5_P, S5_W), full(S5_W, S5_W), full(16, S5_W)),
        out_shape=(sd((L, S5_P), F32), sd((L, S5_P), F32), sd((L, S5_W), F32), sd((S5_P, S5_W), F32),
                   sd((S5_P, S5_W), F32), sd((S5_W, S5_W), F32), sd((16, S5_W), F32)),
        compiler_params=_cparams("arbitrary"))(dout, y, u, x_re, x_im, cd_re, cd_im, dskip, glu_w, glu_b)


def s5_block_fwd(u, params, dskip, glu_w, glu_b, tag):
    a_re, a_im, f_re, f_im, bd_re, bd_im, cd_re, cd_im = params
    bu_re = matmul([(u, bd_re.astype(BF16))], "nn", tag + "_bure")
    bu_im = matmul([(u, bd_im.astype(BF16))], "nn", tag + "_buim")
    x_re, x_im = s5_scan_fwd(bu_re, bu_im, a_re, a_im, f_re, f_im, tag + "_scan")
    y, out = s5_out_fwd(x_re, x_im, u, cd_re.astype(BF16), cd_im.astype(BF16), dskip, glu_w, glu_b, tag + "_out")
    return out, (u, bu_re, bu_im, x_re, x_im, y)


def s5_block_bwd(dout, res, params, dskip, glu_w, glu_b, tag, dout_col=0):
    u, bu_re, bu_im, x_re, x_im, y = res
    a_re, a_im, f_re, f_im, bd_re, bd_im, cd_re, cd_im = params
    dxr, dxi, du, dcr, dci, dglu_w, sums = s5_out_bwd(dout, y, u, x_re, x_im, cd_re.astype(BF16), cd_im.astype(BF16),
                                                      dskip, glu_w, glu_b, tag + "_dout", dout_col=dout_col)
    dbr, dbi, acc = s5_scan_bwd(dxr, dxi, x_re, x_im, bu_re, bu_im, a_re, a_im, f_re, f_im, tag + "_dscan")
    du = du + matmul([(dbr, bd_re.astype(BF16)), (dbi, bd_im.astype(BF16))], "nt", tag + "_du")
    dbd_re = matmul([(u, dbr)], "tn", tag + "_dbdre")
    dbd_im = matmul([(u, dbi)], "tn", tag + "_dbdim")
    acc = acc.reshape(4, 8, S5_P).sum(axis=1)
    s = sums.reshape(2, 8, S5_W).sum(axis=1)
    cot = (acc[0:1], acc[1:2], acc[2:3], acc[3:4], dbd_re, dbd_im, dcr, dci)
    return du, cot, dict(dskip=s[0], glu_w=dglu_w, glu_b=s[1])


DN_Z0, DN_NT = 18, 18
REC_U0, REC_A0 = 3072, 3328


def rec_cols_permute(w):
    return jnp.concatenate([w[..., S5_W:REC_A0], w[..., :S5_W], w[..., REC_A0:]], axis=-1)


def rec_cols_restore(w):
    return jnp.concatenate([w[..., REC_U0:REC_A0], w[..., :REC_U0], w[..., REC_A0:]], axis=-1)


DN_W = DN_H * DN_DK


def _dn_conv4(taps, w_ref):
    xc = w_ref[3:4, :] * taps[0]
    for k in range(1, 4):
        xc = xc + w_ref[3 - k:4 - k, :] * taps[k]
    return xc


def dn_prep_fwd(rin, cw, name):
    L = rin.shape[0]
    tl = _rtile(L, 256)
    hb = tl // 8

    def body(x_ref, h_ref, w_ref, o_ref):
        j = pl.program_id(0)
        first = pl.program_id(1) == 0
        x, h = x_ref[...], h_ref[...]
        s = _silu(_dn_conv4([x] + [_shift_down(x, h, k, first) for k in range(1, 4)], w_ref))
        scale = jnp.where(j == 0, DN_DK ** -0.5, 1.0)
        for hd in _HEADS:
            cs = slice(hd * 128, (hd + 1) * 128)
            sh = s[:, cs]
            r = lax.rsqrt(jnp.sum(sh * sh, axis=-1, keepdims=True) + EPS)
            o_ref[:, cs] = jnp.where(j < 2, sh * r * scale, sh)

    main = pl.BlockSpec((tl, DN_W), lambda j, i: (i, j))
    halo = pl.BlockSpec((8, DN_W), lambda j, i: (jnp.maximum(i * hb - 1, 0), j))
    return pl.pallas_call(
        body, name=name, grid=(3, L // tl),
        in_specs=[main, halo, pl.BlockSpec((4, DN_W), lambda j, i: (0, j))],
        out_specs=main, out_shape=jax.ShapeDtypeStruct((L, 3 * DN_W), F32),
        compiler_params=_cparams("parallel", "parallel"))(rin, rin, cw)


def dn_prep_bwd(rin, cw, dout, name):
    L = rin.shape[0]
    tl = _rtile(L, 256)
    hb = tl // 8
    nrt = L // tl

    def body(x_ref, h_ref, w_ref, d_ref, dx_ref, s_ref, c_ref):
        j = pl.program_id(0)
        i = pl.program_id(1)
        first = i == nrt - 1

        @pl.when(i == 0)
        def _():
            s_ref[...] = jnp.zeros_like(s_ref)
            c_ref[...] = jnp.zeros_like(c_ref)

        x, h = x_ref[...], h_ref[...]
        taps = [x] + [_shift_down(x, h, k, first) for k in range(1, 4)]
        xc = _dn_conv4(taps, w_ref)
        s = _silu(xc)
        scale = jnp.where(j == 0, DN_DK ** -0.5, 1.0)
        pieces = []
        for hd in _HEADS:
            cs = slice(hd * 128, (hd + 1) * 128)
            sh, d = s[:, cs], d_ref[:, cs]
            r = lax.rsqrt(jnp.sum(sh * sh, axis=-1, keepdims=True) + EPS)
            n = sh * r
            dn = d * scale
            pieces.append(jnp.where(j < 2, r * (dn - n * jnp.sum(dn * n, axis=-1, keepdims=True)), d))
        dxc = jnp.concatenate(pieces, axis=1) * _dsilu(xc)
        nxt = c_ref[...]
        dx_ref[...] = _dn_conv4([dxc] + [_shift_up(dxc, nxt, k) for k in range(1, 4)], w_ref).astype(BF16)
        c_ref[...] = dxc[0:8, :]
        for k in range(4):
            s_ref[8 * (3 - k):8 * (3 - k) + 8, :] += _fold8(dxc * taps[k])

    rev = lambda i: nrt - 1 - i
    main = pl.BlockSpec((tl, DN_W), lambda j, i: (rev(i), j))
    halo = pl.BlockSpec((8, DN_W), lambda j, i: (jnp.maximum(rev(i) * hb - 1, 0), j))
    return pl.pallas_call(
        body, name=name, grid=(3, nrt),
        in_specs=[main, halo, pl.BlockSpec((4, DN_W), lambda j, i: (0, j)), main],
        out_specs=(main, pl.BlockSpec((32, DN_W), lambda j, i: (0, j))),
        out_shape=(jax.ShapeDtypeStruct((L, 3 * DN_W), BF16), jax.ShapeDtypeStruct((32, 3 * DN_W), F32)),
        scratch_shapes=[pltpu.VMEM((8, DN_W), F32)],
        compiler_params=_cparams("parallel", "arbitrary"))(rin, rin, cw, dout)


_HI = lax.Precision.HIGH
_NT = (((1,), (1,)), ((), ()))
_TN = (((0,), (0,)), ((), ()))
_HEADS = tuple(range(DN_H))


def _mm(a, b, dims=(((1,), (0,)), ((), ())), hi=False):
    if hi:
        return lax.dot_general(a, b, dims, precision=_HI, preferred_element_type=F32)
    return lax.dot_general(a.astype(BF16), b.astype(BF16), dims, preferred_element_type=F32)


def _dn_masks():
    ri = lax.broadcasted_iota(jnp.int32, (DN_C, DN_C), 0)
    ci = lax.broadcasted_iota(jnp.int32, (DN_C, DN_C), 1)
    return ri >= ci, ri > ci, (ri == ci).astype(F32)


def _dn_decay(gc, gr, causal):
    gam = [jnp.where(causal, jnp.exp(jnp.where(causal, gc[h] - gr[h], 0.0)), 0.0) for h in _HEADS]
    eg = [jnp.exp(gc[h]) for h in _HEADS]
    el = [jnp.exp(gc[h][DN_C - 1:DN_C, :] - gc[h]) for h in _HEADS]
    gl = [jnp.exp(gc[h][DN_C - 1:DN_C, :]) for h in _HEADS]
    return gam, eg, el, gl


def _dn_solve(k, v, beta, gam, eg, kk, strict, eye):
    nmat = [jnp.where(strict, beta[h] * kk[h] * gam[h], 0.0) for h in _HEADS]
    t = [eye - nmat[h] for h in _HEADS]
    m = [_mm(nmat[h], nmat[h], hi=True) for h in _HEADS]
    for step in range(5):
        t = [t[h] + _mm(t[h], m[h], hi=True) for h in _HEADS]
        if step < 4:
            m = [_mm(m[h], m[h], hi=True) for h in _HEADS]
    rhs = [jnp.concatenate([v[h] * beta[h], k[h] * (beta[h] * eg[h])], axis=1) for h in _HEADS]
    sol = [_mm(t[h], rhs[h], hi=True) for h in _HEADS]
    return t, sol


def dn_chunk_fwd(qkv, gcol, grow, bcol, name, comm=None):
    L = qkv.shape[0]
    C, W = DN_C, DN_H * DN_DK
    ncb = 8
    tl = ncb * C
    nchunks = L // C

    def body(q_ref, k_ref, v_ref, gc_ref, gr_ref, b_ref, o_ref, sh_ref, t_ref, sol_ref, s_ref):
        @pl.when(pl.program_id(0) == 0)
        def _():
            s_ref[...] = jnp.zeros_like(s_ref)

        causal, strict, eye = _dn_masks()

        def chunk(c, _):
            rows = pl.ds(pl.multiple_of(c * C, C), C)
            grow_c = gr_ref[c]
            hs = lambda h: slice(h * 128, (h + 1) * 128)
            q = [q_ref[rows, hs(h)] for h in _HEADS]
            k = [k_ref[rows, hs(h)] for h in _HEADS]
            v = [v_ref[rows, hs(h)] for h in _HEADS]
            gc = [gc_ref[rows, h:h + 1] for h in _HEADS]
            gr = [grow_c[h:h + 1, :] for h in _HEADS]
            beta = [b_ref[rows, h:h + 1] for h in _HEADS]
            gam, eg, el, gl = _dn_decay(gc, gr, causal)
            kk = [_mm(k[h], k[h], _NT) for h in _HEADS]
            t, sol = _dn_solve(k, v, beta, gam, eg, kk, strict, eye)
            qk = [_mm(q[h], k[h], _NT) * gam[h] for h in _HEADS]
            S = [s_ref[hs(h), :] for h in _HEADS]
            vn = [sol[h][:, :128] - _mm(sol[h][:, 128:], S[h]) for h in _HEADS]
            o = [_mm(q[h] * eg[h], S[h]) + _mm(qk[h], vn[h]) for h in _HEADS]
            Sn = [S[h] * gl[h] + _mm(k[h] * el[h], vn[h], _TN) for h in _HEADS]
            for h in _HEADS:
                sh_ref[c, hs(h), :] = S[h]
                s_ref[hs(h), :] = Sn[h]
                o_ref[rows, hs(h)] = o[h]
                t_ref[rows, h * C:(h + 1) * C] = t[h]
                sol_ref[rows, h * 256:(h + 1) * 256] = sol[h]
            return 0

        lax.fori_loop(0, ncb, chunk, 0)

    col = lambda b: pl.BlockSpec((tl, W), lambda i: (i, b))
    small = pl.BlockSpec((tl, 8), lambda i: (i, 0))
    rowblk = lambda w: pl.BlockSpec((tl, w), lambda i: (i, 0))
    sd = jax.ShapeDtypeStruct
    return _call(body, (qkv, qkv, qkv, gcol, grow, bcol), name=name, grid=(L // tl,),
                 in_specs=[col(0), col(1), col(2), small, pl.BlockSpec((ncb, 8, C), lambda i: (i, 0, 0)), small],
                 out_specs=(rowblk(W), pl.BlockSpec((ncb, W, 128), lambda i: (i, 0, 0)), rowblk(DN_H * C),
                            rowblk(DN_H * 256)),
                 out_shape=(sd((L, W), F32), sd((nchunks, W, 128), F32), sd((L, DN_H * C), F32),
                            sd((L, DN_H * 256), F32)),
                 scratch_shapes=[pltpu.VMEM((W, 128), F32)], sem=("arbitrary",), comm=comm)


def dn_chunk_bwd(qkv, gcol, grow, bcol, shist, thist, solhist, do, name, comm=None):
    L = qkv.shape[0]
    C, W = DN_C, DN_H * DN_DK
    ncb = 8
    tl = ncb * C
    nchunks = L // C
    nt = L // tl

    def body(q_ref, k_ref, v_ref, gc_ref, gr_ref, b_ref, sh_ref, t_ref, sol_ref, do_ref,
             dqkv_ref, dgc_ref, dgr_ref, db_ref, ds_ref):
        @pl.when(pl.program_id(0) == 0)
        def _():
            ds_ref[...] = jnp.zeros_like(ds_ref)

        lane8 = lax.broadcasted_iota(jnp.int32, (C, 8), 1)
        sub8 = lax.broadcasted_iota(jnp.int32, (8, C), 0)
        rowid = lax.broadcasted_iota(jnp.int32, (C, 1), 0)
        causal, strict, _ = _dn_masks()
        rsum = lambda a: jnp.sum(a, axis=1, keepdims=True)

        def chunk(cc, _):
            c = ncb - 1 - cc
            rows = pl.ds(pl.multiple_of(c * C, C), C)
            grow_c = gr_ref[c]
            hs = lambda h: slice(h * 128, (h + 1) * 128)
            q = [q_ref[rows, hs(h)] for h in _HEADS]
            k = [k_ref[rows, hs(h)] for h in _HEADS]
            v = [v_ref[rows, hs(h)] for h in _HEADS]
            gc = [gc_ref[rows, h:h + 1] for h in _HEADS]
            gr = [grow_c[h:h + 1, :] for h in _HEADS]
            beta = [b_ref[rows, h:h + 1] for h in _HEADS]
            t = [t_ref[rows, h * C:(h + 1) * C] for h in _HEADS]
            sol = [sol_ref[rows, h * 256:(h + 1) * 256] for h in _HEADS]
            S = [sh_ref[c, hs(h), :] for h in _HEADS]
            dS = [ds_ref[hs(h), :] for h in _HEADS]
            dov = [do_ref[rows, hs(h)] for h in _HEADS]
            gam, eg, el, gl = _dn_decay(gc, gr, causal)
            kk = [_mm(k[h], k[h], _NT) for h in _HEADS]
            qk_raw = [_mm(q[h], k[h], _NT) for h in _HEADS]
            w = [sol[h][:, 128:] for h in _HEADS]
            kd = [k[h] * el[h] for h in _HEADS]
            vn = [sol[h][:, :128] - _mm(w[h], S[h]) for h in _HEADS]
            dvn = [_mm(qk_raw[h] * gam[h], dov[h], _TN) + _mm(kd[h], dS[h]) for h in _HEADS]
            dqd = [_mm(dov[h], S[h], _NT) for h in _HEADS]
            dqk = [jnp.where(causal, _mm(dov[h], vn[h], _NT), 0.0) for h in _HEADS]
            dkd = [_mm(vn[h], dS[h], _NT) for h in _HEADS]
            dgl = [jnp.sum(rsum(dS[h] * S[h]), axis=0, keepdims=True) for h in _HEADS]
            dw = [-_mm(dvn[h], S[h], _NT) for h in _HEADS]
            dSn = [dS[h] * gl[h] + _mm(q[h] * eg[h], dov[h], _TN) - _mm(w[h], dvn[h], _TN) for h in _HEADS]
            drhs = [_mm(t[h], jnp.concatenate([dvn[h], dw[h]], axis=1), _TN, hi=True) for h in _HEADS]
            dn = [jnp.where(strict, -_mm(drhs[h], sol[h], _NT, hi=True), 0.0) for h in _HEADS]
            dgc_all = jnp.zeros((C, 8), F32)
            db_all = jnp.zeros((C, 8), F32)
            dgr_all = jnp.zeros((8, C), F32)
            for h in _HEADS:
                drv, drk = drhs[h][:, :128], drhs[h][:, 128:]
                t2 = rsum(drk * k[h])
                x = dn[h] * gam[h]
                dbeta = rsum(drv * v[h]) + t2 * eg[h] + rsum(x * kk[h])
                dkk = x * beta[h]
                draw = dqk[h] * gam[h]
                mm_ = (dn[h] * beta[h] * kk[h] + dqk[h] * qk_raw[h]) * gam[h]
                deg = t2 * beta[h] + rsum(dqd[h] * q[h])
                r_ = rsum(dkd[h] * k[h]) * el[h]
                dglast = jnp.sum(r_, axis=0, keepdims=True) + dgl[h] * gl[h]
                dgc = rsum(mm_) + deg * eg[h] - r_ + jnp.where(rowid == C - 1, dglast, 0.0)
                dgr = -jnp.sum(mm_, axis=0, keepdims=True)
                dqkv_ref[rows, hs(h)] = _mm(draw, k[h]) + dqd[h] * eg[h]
                dqkv_ref[rows, hs(DN_H + h)] = (drk * (beta[h] * eg[h]) + _mm(dkk, k[h]) + _mm(dkk, k[h], _TN)
                                                + _mm(draw, q[h], _TN) + dkd[h] * el[h])
                dqkv_ref[rows, hs(2 * DN_H + h)] = drv * beta[h]
                ds_ref[hs(h), :] = dSn[h]
                dgc_all = dgc_all + jnp.where(lane8 == h, dgc, 0.0)
                db_all = db_all + jnp.where(lane8 == h, dbeta, 0.0)
                dgr_all = dgr_all + jnp.where(sub8 == h, dgr, 0.0)
            dgc_ref[rows, :] = dgc_all
            db_ref[rows, :] = db_all
            dgr_ref[c] = dgr_all
            return 0

        lax.fori_loop(0, ncb, chunk, 0)

    rev = lambda i: nt - 1 - i
    col = lambda b: pl.BlockSpec((tl, W), lambda i: (rev(i), b))
    rowblk = lambda w: pl.BlockSpec((tl, w), lambda i: (rev(i), 0))
    small = pl.BlockSpec((tl, 8), lambda i: (rev(i), 0))
    g3 = pl.BlockSpec((ncb, 8, C), lambda i: (rev(i), 0, 0))
    sd = jax.ShapeDtypeStruct
    return _call(body, (qkv, qkv, qkv, gcol, grow, bcol, shist, thist, solhist, do), name=name, grid=(nt,),
                 in_specs=[col(0), col(1), col(2), small, g3, small,
                           pl.BlockSpec((ncb, W, 128), lambda i: (rev(i), 0, 0)), rowblk(DN_H * C),
                           rowblk(DN_H * 256), col(0)],
                 out_specs=(rowblk(3 * W), small, g3, small),
                 out_shape=(sd((L, 3 * W), F32), sd((L, 8), F32), sd((nchunks, 8, C), F32), sd((L, 8), F32)),
                 scratch_shapes=[pltpu.VMEM((W, 128), F32)], sem=("arbitrary",), comm=comm)


def dn_out_fwd(o, rin, nw, name):
    L = o.shape[0]
    tl = _rtile(L, 256)

    def body(o_ref, z_ref, w_ref, y_ref):
        for hd in _HEADS:
            cs = slice(hd * 128, (hd + 1) * 128)
            ov = o_ref[:, cs]
            r = lax.rsqrt(jnp.mean(ov * ov, axis=-1, keepdims=True) + EPS)
            y_ref[:, cs] = (ov * r * w_ref[...] * _silu(z_ref[:, cs])).astype(BF16)

    return pl.pallas_call(
        body, name=name, grid=(L // tl,),
        in_specs=[pl.BlockSpec((tl, DN_W), lambda i: (i, 0)), pl.BlockSpec((tl, DN_W), lambda i: (i, 3)),
                  pl.BlockSpec((1, 128), lambda i: (0, 0))],
        out_specs=pl.BlockSpec((tl, DN_W), lambda i: (i, 0)), out_shape=jax.ShapeDtypeStruct((L, DN_W), BF16),
        compiler_params=_cparams("parallel"))(o, rin, nw)


def dn_out_bwd(dycat, o, rin, nw, name):
    L = o.shape[0]
    tl = _rtile(L, 256)

    def body(dy_ref, o_ref, z_ref, w_ref, do_ref, dz_ref, s_ref):
        @pl.when(pl.program_id(0) == 0)
        def _():
            s_ref[...] = jnp.zeros_like(s_ref)

        for hd in _HEADS:
            cs = slice(hd * 128, (hd + 1) * 128)
            ov, zv, d = o_ref[:, cs], z_ref[:, cs], dy_ref[:, cs]
            r = lax.rsqrt(jnp.mean(ov * ov, axis=-1, keepdims=True) + EPS)
            n = ov * r
            dnw = d * _silu(zv)
            dz_ref[:, cs] = (d * n * w_ref[...] * _dsilu(zv)).astype(BF16)
            dn = dnw * w_ref[...]
            do_ref[:, cs] = r * (dn - n * jnp.mean(dn * n, axis=-1, keepdims=True))
            s_ref[:, cs] += _fold8(dnw * n)

    own = pl.BlockSpec((tl, DN_W), lambda i: (i, 0))
    sd = jax.ShapeDtypeStruct
    return pl.pallas_call(
        body, name=name, grid=(L // tl,),
        in_specs=[own, own, pl.BlockSpec((tl, DN_W), lambda i: (i, 3)), pl.BlockSpec((1, 128), lambda i: (0, 0))],
        out_specs=(own, own, pl.BlockSpec((8, DN_W), lambda i: (0, 0))),
        out_shape=(sd((L, DN_W), F32), sd((L, DN_W), BF16), sd((8, DN_W), F32)),
        compiler_params=_cparams("arbitrary"))(dycat, o, rin, nw)


def dn_gates(a, beta_raw, a_log, dt_bias):
    L = a.shape[0]
    beta = jax.nn.sigmoid(beta_raw)
    g = -jnp.exp(a_log) * jax.nn.softplus(a + dt_bias)
    G = jnp.cumsum(g.reshape(L // DN_C, DN_C, DN_H), axis=1)
    pad = lambda t: jnp.pad(t, ((0, 0), (0, 8 - DN_H)))
    gcol = pad(G.reshape(L, DN_H))
    grow = jnp.pad(jnp.transpose(G, (0, 2, 1)), ((0, 0), (0, 8 - DN_H), (0, 0)))
    return gcol, grow, pad(beta)


def dn_block_fwd(rin, cw, a_log, dt_bias, out_norm, tag, comm=None):
    gates, gates_vjp = jax.vjp(dn_gates, rin[:, REC_A0:REC_A0 + DN_H], rin[:, REC_A0 + DN_H:REC_IN], a_log, dt_bias)
    qkv = dn_prep_fwd(rin, cw, tag + "_prep")
    (o, shist, thist, solhist), got = _with_comm(dn_chunk_fwd(qkv, *gates, tag + "_chunk", comm=comm), comm)
    yd = dn_out_fwd(o, rin, out_norm.reshape(1, 128), tag + "_onorm")
    return yd, (qkv, gates, gates_vjp, o, shist, thist, solhist), got


def dn_block_bwd(dyd, res, rin, cw, out_norm, tag, comm=None):
    qkv, gates, gates_vjp, o, shist, thist, solhist = res
    do, dz, nsum = dn_out_bwd(dyd, o, rin, out_norm.reshape(1, 128), tag + "_donorm")
    (dqkv, dgc, dgr, db), got = _with_comm(dn_chunk_bwd(qkv, *gates, shist, thist, solhist, do, tag + "_dchunk",
                                                        comm=comm), comm)
    da, dbraw, g_alog, g_dtb = gates_vjp((dgc, dgr, db))
    dx, csum = dn_prep_bwd(rin, cw, dqkv, tag + "_dprep")
    grads = dict(conv=csum.reshape(4, 8, DN_NT * 128).sum(axis=1), a_log=g_alog, dt_bias=g_dtb,
                 out_norm=nsum.sum(axis=0).reshape(DN_H, 128).sum(axis=0))
    return dx, dz, da, dbraw, grads, got


_HBM = pl.BlockSpec(memory_space=pltpu.HBM)


def _mesh_pos():
    xi, yi, ci = lax.axis_index("x"), lax.axis_index("y"), lax.axis_index("c")
    return xi, yi, ci, 4 * xi + 2 * yi + ci


def _peer(xi, yi, ci, k):
    px = 1 - xi if (k >> 2) & 1 else xi
    py = 1 - yi if (k >> 1) & 1 else yi
    pc = 1 - ci if k & 1 else ci
    return (px, py, pc), 4 * px + 2 * py + pc


def _exchange(xs, gather, name):
    n = len(xs)

    def body(*refs):
        copies = _comm_copies(refs[:n], refs[n:2 * n], *refs[2 * n:], gather)
        for cp in copies:
            cp.start()
        for cp in copies:
            cp.wait()

    return pl.pallas_call(
        body, name=name, in_specs=[_HBM] * n, out_specs=tuple([_HBM] * n),
        out_shape=_comm_out_shapes(xs), scratch_shapes=_comm_sems(n))(*xs)


def _comm_out_shapes(xs):
    return tuple(jax.ShapeDtypeStruct((N_DEV,) + x.shape[-2:], x.dtype) for x in xs)


def _comm_sems(n):
    return [pltpu.SemaphoreType.DMA((n * (N_DEV - 1),)), pltpu.SemaphoreType.DMA((n * (N_DEV - 1),)),
            pltpu.SemaphoreType.DMA((n,))]


def _comm_copies(x_refs, o_refs, send_sems, recv_sems, lsems, gather):
    xi, yi, ci, me = _mesh_pos()
    copies = []
    for t in range(len(x_refs)):
        src_of = (lambda lin, t=t: x_refs[t]) if gather else (lambda lin, t=t: x_refs[t].at[lin])
        copies.append(pltpu.make_async_copy(src_of(me), o_refs[t].at[me], lsems.at[t]))
        for k in range(1, N_DEV):
            peer, lin = _peer(xi, yi, ci, k)
            s = t * (N_DEV - 1) + k - 1
            copies.append(pltpu.make_async_remote_copy(
                src_ref=src_of(lin), dst_ref=o_refs[t].at[me], send_sem=send_sems.at[s],
                recv_sem=recv_sems.at[s], device_id=peer, device_id_type=pl.DeviceIdType.MESH))
    return copies


def _call(body, args, *, name, grid, in_specs, out_specs, out_shape, scratch_shapes=(), sem, comm=None):
    if comm is None:
        return pl.pallas_call(body, name=name, grid=grid, in_specs=in_specs, out_specs=out_specs,
                              out_shape=out_shape, scratch_shapes=list(scratch_shapes),
                              compiler_params=_cparams(*sem))(*args)
    xs, gather = comm
    n = len(xs)
    single = not isinstance(out_shape, (tuple, list))
    outs_shape = (out_shape,) if single else tuple(out_shape)
    outs_specs = (out_specs,) if single else tuple(out_specs)
    n_in, n_out, n_scr = len(in_specs), len(outs_shape), len(scratch_shapes)

    def body2(*refs):
        ins, cx = refs[:n_in], refs[n_in:n_in + n]
        outs = refs[n_in + n:n_in + n + n_out]
        co = refs[n_in + n + n_out:n_in + 2 * n + n_out]
        scr = refs[n_in + 2 * n + n_out:n_in + 2 * n + n_out + n_scr]
        sems = refs[n_in + 2 * n + n_out + n_scr:]
        first = functools.reduce(jnp.logical_and, [pl.program_id(a) == 0 for a in range(len(grid))])
        last = functools.reduce(jnp.logical_and, [pl.program_id(a) == grid[a] - 1 for a in range(len(grid))])

        @pl.when(first)
        def _():
            for cp in _comm_copies(cx, co, *sems, gather):
                cp.start()

        body(*ins, *outs, *scr)

        @pl.when(last)
        def _():
            for cp in _comm_copies(cx, co, *sems, gather):
                cp.wait()

    res = pl.pallas_call(
        body2, name=name, grid=grid, in_specs=list(in_specs) + [_HBM] * n,
        out_specs=outs_specs + tuple([_HBM] * n), out_shape=outs_shape + _comm_out_shapes(xs),
        scratch_shapes=list(scratch_shapes) + _comm_sems(n),
        compiler_params=_cparams(*(["arbitrary"] * len(grid))))(*args, *xs)
    main = res[0] if single else tuple(res[:n_out])
    return main, list(res[n_out:])


def all_gather(x, name):
    return _exchange([x], True, name)[0]


def all_gather_many(xs, name):
    return _exchange(xs, True, name)


def all_to_all_many(xs, name):
    return _exchange(xs, False, name)


def reduce_adamw(gsrc, w, m, v, name, comm=None):
    parts = list(gsrc) if isinstance(gsrc, (list, tuple)) else [gsrc]
    S, R0, C = parts[0].shape
    R = R0 * len(parts)
    tr = _rtile(R0, max(16, min(256, (4 << 20) // (S * C * 4) // 16 * 16)), 16 if R0 % 16 == 0 else 8)
    n0 = R0 // tr
    c1 = 1.0 - ADAM_B1 ** ADAM_STEP
    c2 = 1.0 - ADAM_B2 ** ADAM_STEP

    def body(*refs):
        g_refs = refs[:len(parts)]
        w_ref, m_ref, v_ref, go_ref, d_ref, mo_ref, vo_ref = refs[len(parts):]
        for p, g_ref in enumerate(g_refs):
            @pl.when(pl.program_id(0) // n0 == p)
            def _(g_ref=g_ref):
                acc = g_ref[0].astype(F32)
                for s in range(1, S):
                    acc = acc + g_ref[s].astype(F32)
                go_ref[...] = acc
        g = go_ref[...]
        mn = ADAM_B1 * m_ref[...] + (1.0 - ADAM_B1) * g
        vn = ADAM_B2 * v_ref[...] + (1.0 - ADAM_B2) * (g * g)
        mo_ref[...] = mn
        vo_ref[...] = vn
        d_ref[...] = -ADAM_LR * ((mn / c1) / (jnp.sqrt(vn / c2) + ADAM_EPS) + ADAM_WD * w_ref[...])

    big = pl.BlockSpec((tr, C), lambda i: (i, 0))
    o = jax.ShapeDtypeStruct((R, C), F32)
    part_spec = lambda p: pl.BlockSpec((S, tr, C), lambda i: (0, jnp.clip(i - p * n0, 0, n0 - 1), 0))
    return _call(body, (*parts, w, m, v), name=name, grid=(R // tr,),
                 in_specs=[part_spec(p) for p in range(len(parts))] + [big, big, big],
                 out_specs=(big, big, big, big), out_shape=(o, o, o, o), sem=("parallel",), comm=comm)


def _to_slabs(g, ax):
    shp = g.shape
    g = g.reshape(shp[:ax] + (N_DEV, shp[ax] // N_DEV) + shp[ax + 1:])
    return jnp.moveaxis(g, ax, 0).reshape(N_DEV, -1)


def _from_slabs(s, ax, shp):
    s = s.reshape((N_DEV,) + shp[:ax] + (shp[ax] // N_DEV,) + shp[ax + 1:])
    return jnp.moveaxis(s, 0, ax).reshape(shp)


def _pack_rows(flat, width, row_mult):
    n = flat.shape[-1]
    per = width * row_mult
    tot = -(-n // per) * per
    flat = jnp.pad(flat, [(0, 0)] * (flat.ndim - 1) + [(0, tot - n)])
    return flat.reshape(flat.shape[:-1] + (tot // width, width))


def _offsets(sizes):
    offs, o = [], 0
    for s in sizes:
        offs.append(o)
        o += s
    return offs


WEIGHTS = ['ada_w', 'ada_b', 'norm_mix', 'norm_ffn', 'attn_w_in', 'attn_q_norm_a', 'attn_k_norm_a', 'attn_q_norm_b',
           'attn_k_norm_b', 'attn_sinks', 'attn_w_out', 'rec_w_in', 's5_lambda_re', 's5_lambda_im', 's5_log_dt',
           's5_b_re', 's5_b_im', 's5_c_re', 's5_c_im', 's5_d', 's5_glu_w', 's5_glu_b', 'dn_conv', 'dn_a_log',
           'dn_dt_bias', 'dn_out_norm', 'rec_w_out', 'ffn_w_up', 'ffn_conv', 'ffn_w_down']
BIG = [('attn_w_in', (D, ATTN_IN // N_DEV)), ('attn_w_out', (D // N_DEV, D)), ('rec_w_in', (D // N_DEV, REC_PAD)),
       ('s5_glu_w', (S5_W // N_DEV, S5_W)), ('rec_w_out', (D // N_DEV, D)), ('ffn_w_up', (2 * D, 2 * D_FF // N_DEV)),
       ('ffn_w_down', (2 * D_FF // N_DEV, D))]


def _shard2d(name, t):
    if name == 'rec_w_in':
        return jnp.pad(t[0], ((0, 0), (0, REC_PAD - REC_IN)))
    return t.reshape((-1, t.shape[-1]))


def _cols_to_slabs(g, k=N_DEV):
    r, n = g.shape
    return jnp.transpose(g.reshape(r, k, n // k), (1, 0, 2))


def _slabs_to_cols(s):
    k, r, c_ = s.shape
    return jnp.transpose(s, (1, 0, 2)).reshape(r, k * c_)
SMALL_SHARDED = [('s5_d', 1, (1, S5_W)), ('s5_glu_b', 1, (1, S5_W)), ('dn_conv', 2, (1, 4, 2304)),
                 ('ffn_conv', 2, (2, 3, 2 * D_FF))]
REPLICATED = [('ada_b', (2, 6 * D)), ('norm_mix', (2, D)), ('norm_ffn', (2, D)), ('attn_q_norm_a', (1, HD)),
              ('attn_k_norm_a', (1, HD)), ('attn_q_norm_b', (1, HD)), ('attn_k_norm_b', (1, HD)),
              ('attn_sinks', (1, 8)), ('s5_lambda_re', (1, 16, 64)), ('s5_lambda_im', (1, 16, 64)),
              ('s5_log_dt', (1, 16)), ('s5_b_re', (1, 16, 64, 16)), ('s5_b_im', (1, 16, 64, 16)),
              ('s5_c_re', (1, 16, 16, 64)), ('s5_c_im', (1, 16, 16, 64)), ('dn_a_log', (1, DN_H)),
              ('dn_dt_bias', (1, DN_H)), ('dn_out_norm', (1, 128))]


def _numel(shp):
    return int(np.prod(shp))


def kernel(x, c, ada_w, ada_b, norm_mix, norm_ffn, attn_w_in, attn_q_norm_a, attn_k_norm_a, attn_q_norm_b, attn_k_norm_b, attn_sinks, attn_w_out, rec_w_in, s5_lambda_re, s5_lambda_im, s5_log_dt, s5_b_re, s5_b_im, s5_c_re, s5_c_im, s5_d, s5_glu_w, s5_glu_b, dn_conv, dn_a_log, dn_dt_bias, dn_out_norm, rec_w_out, ffn_w_up, ffn_conv, ffn_w_down, loss_target, m_ada_w, m_ada_b, m_norm_mix, m_norm_ffn, m_attn_w_in, m_attn_q_norm_a, m_attn_k_norm_a, m_attn_q_norm_b, m_attn_k_norm_b, m_attn_sinks, m_attn_w_out, m_rec_w_in, m_s5_lambda_re, m_s5_lambda_im, m_s5_log_dt, m_s5_b_re, m_s5_b_im, m_s5_c_re, m_s5_c_im, m_s5_d, m_s5_glu_w, m_s5_glu_b, m_dn_conv, m_dn_a_log, m_dn_dt_bias, m_dn_out_norm, m_rec_w_out, m_ffn_w_up, m_ffn_conv, m_ffn_w_down, v_ada_w, v_ada_b, v_norm_mix, v_norm_ffn, v_attn_w_in, v_attn_q_norm_a, v_attn_k_norm_a, v_attn_q_norm_b, v_attn_k_norm_b, v_attn_sinks, v_attn_w_out, v_rec_w_in, v_s5_lambda_re, v_s5_lambda_im, v_s5_log_dt, v_s5_b_re, v_s5_b_im, v_s5_c_re, v_s5_c_im, v_s5_d, v_s5_glu_w, v_s5_glu_b, v_dn_conv, v_dn_a_log, v_dn_dt_bias, v_dn_out_norm, v_rec_w_out, v_ffn_w_up, v_ffn_conv, v_ffn_w_down):
    loc = locals()
    W = {n: loc[n] for n in WEIGHTS}
    M = {n: loc["m_" + n] for n in WEIGHTS}
    V = {n: loc["v_" + n] for n in WEIGHTS}
    _, _, _, me = _mesh_pos()
    L = x.shape[1]
    x0, tgt = x[0], loss_target[0]

    small_in = jnp.concatenate([c.reshape(-1)] + [W[n].reshape(-1) for n, _, _ in SMALL_SHARDED])
    si, att_in_all = all_gather_many([_pack_rows(small_in, 1024, 8), attn_w_in[0].astype(BF16)], "gather_first")
    si = si.reshape(N_DEV, -1)
    c_all = si[:, :D]
    off = D
    small_full = {}
    for n, ax, shp in SMALL_SHARDED:
        k = _numel(shp) // N_DEV
        small_full[n] = _from_slabs(si[:, off:off + k], ax, shp)
        off += k

    cond_all = jax.nn.silu(c_all)
    modp = jnp.concatenate([matmul([(cond_all, ada_w[l].astype(BF16))], "nn", f"ada{l}") for l in range(2)], axis=0)
    modp_all = all_gather(modp, "gather_mod")
    mods = []
    for l in range(2):
        row = lax.dynamic_index_in_dim(modp_all, l * N_DEV + me, axis=1, keepdims=False)
        mod = row.reshape(1, 6 * D) + ada_b[l].reshape(1, 6 * D)
        mods.append([mod[:, i * D:(i + 1) * D] for i in range(6)])

    w_att_in = _slabs_to_cols(att_in_all)
    bf = lambda t: t.astype(BF16)
    ffn_shards = [[bf(ffn_w_up[l]), bf(ffn_w_down[l])] for l in range(2)]
    rec_shards = [bf(_shard2d('rec_w_in', rec_w_in)), bf(s5_glu_w[0]), bf(rec_w_out[0])]
    ffn_cw = [small_full['ffn_conv'][l] for l in range(2)]
    dn_cw = small_full['dn_conv'][0]
    s5_dskip, glu_b = small_full['s5_d'], small_full['s5_glu_b']
    row = lambda t: t.reshape(1, -1)

    sh1, sc1, g1, sh2, sc2, g2 = mods[0]
    h1 = gate_norm_fwd(x0, None, None, row(norm_mix[0]), sh1, sc1, "l0_norm1")
    wvec, sinkvec = attn_vectors(attn_q_norm_a[0], attn_k_norm_a[0], attn_q_norm_b[0], attn_k_norm_b[0], attn_sinks[0])
    y0, res_att, got = attention_block_fwd(
        h1, w_att_in, wvec, sinkvec, None, "att",
        comms={'swa': ([ffn_shards[0][0][:D // 2]], True), 1: ([ffn_shards[0][0][D // 2:]], True),
               4: (ffn_shards[0][1:], True), 16: ([bf(attn_w_out[0])], True)})
    w_att_out = got['w_out']
    split_up = lambda up_all: (_slabs_to_cols(up_all[:4]), _slabs_to_cols(up_all[4:]))
    w_up = [split_up(jnp.concatenate([got['swa'][0], got[1][0]], axis=1))]
    w_down = [got[4][0].reshape(D_FF, D)]
    x1, h2 = gate_norm_fwd(x0, y0, g1, row(norm_ffn[0]), sh2, sc2, "l0_norm2")
    f0, res_f0, got_rec = ffn_block_fwd(h2, w_up[0][0], w_up[0][1], ffn_cw[0], w_down[0], "ffn0",
                                        comm=(rec_shards, True))
    w_rec_in = rec_cols_permute(got_rec[0].reshape(D, REC_PAD))
    glu_w, w_rec_out = got_rec[1].reshape(S5_W, S5_W), got_rec[2].reshape(D, D)
    w_rec_out = jnp.concatenate([w_rec_out[S5_W:], w_rec_out[:S5_W]], axis=0)
    t1, tc1, tg1, t2, tc2, tg2 = mods[1]
    x2, h3 = gate_norm_fwd(x1, f0, g2, row(norm_mix[1]), t1, tc1, "l1_norm1")
    rin = matmul([(h3, w_rec_in)], "nn", "rec_in")
    s5p, s5p_vjp = jax.vjp(s5_params, s5_lambda_re[0], s5_lambda_im[0], s5_log_dt[0], s5_b_re[0], s5_b_im[0],
                           s5_c_re[0], s5_c_im[0])
    u = rin[:, REC_U0:REC_A0]
    yc, res_s5 = s5_block_fwd(u, s5p, s5_dskip, glu_w, glu_b, "s5")
    yd, res_dn, got_ffn1 = dn_block_fwd(rin, dn_cw, dn_a_log[0], dn_dt_bias[0], dn_out_norm[0], "dn",
                                        comm=(ffn_shards[1], True))
    w_up.append(split_up(got_ffn1[0]))
    w_down.append(got_ffn1[1].reshape(D_FF, D))
    ycat = jnp.concatenate([yd, yc], axis=1)
    y1 = matmul([(ycat, w_rec_out)], "nn", "rec_out")
    x3, h4 = gate_norm_fwd(x2, y1, tg1, row(norm_ffn[1]), t2, tc2, "l1_norm2")
    f1, res_f1, _ = ffn_block_fwd(h4, w_up[1][0], w_up[1][1], ffn_cw[1], w_down[1], "ffn1")
    dx4, df1, lsum = final_loss(x3, f1, tg2, tgt, "loss")

    G = {}
    d_tg2 = lsum[8:16].sum(axis=0)
    dh4, gf1, _ = ffn_block_bwd(df1, res_f1, w_up[1][0], w_up[1][1], ffn_cw[1], w_down[1], "ffn1")
    ffn_slabs = lambda g: [g['w_up'], g['w_down'].reshape(N_DEV, D_FF // N_DEV, D)]
    dx3, dy1, s = gate_norm_bwd(x3, y1, tg1, row(norm_ffn[1]), tc2, dx4, dh4, "l1_dnorm2")
    s = s.reshape(4, 8, D).sum(axis=1)
    d_tg1, d_nffn1, d_t2, d_tc2 = s[0], s[1] * (1.0 + tc2[0]), s[2], s[1] * norm_ffn[1]
    g_rec_out = matmul([(ycat, dy1)], "tn", "rec_out_dw", out_dtype=BF16)
    g_rec_out = jnp.concatenate([g_rec_out[DN_W:], g_rec_out[:DN_W]], axis=0).reshape(N_DEV, D // N_DEV, D)
    dycat = matmul([(dy1, w_rec_out)], "nt", "rec_out_dx")
    du, s5cot, gs5 = s5_block_bwd(dycat, res_s5, s5p, s5_dskip, glu_w, glu_b, "s5", dout_col=DN_W // S5_W)
    s5g = s5p_vjp(s5cot)
    dqkv, dz, da, dbraw, gdn, recv_ffn1 = dn_block_bwd(dycat, res_dn, rin, dn_cw, dn_out_norm[0], "dn",
                                                       comm=(ffn_slabs(gf1), False))
    d_rest = jnp.concatenate([du.astype(BF16), da.astype(BF16), dbraw.astype(BF16),
                              jnp.zeros((L, REC_PAD - REC_IN), BF16)], axis=1)
    drin = ((dqkv, 0), (dz, 3 * DN_W), (d_rest, REC_U0))
    g_rec_in = jnp.concatenate([matmul([(h3, p)], "tn", f"rec_in_dw{i}", out_dtype=BF16)
                                for i, (p, _) in enumerate(drin)], axis=1)
    g_rec_in = rec_cols_restore(g_rec_in).reshape(N_DEV, D // N_DEV, REC_PAD)
    g_glu = gs5['glu_w'].astype(BF16).reshape(N_DEV, S5_W // N_DEV, S5_W)
    dh3 = matmul([(p, w_rec_in[:, c0:c0 + p.shape[1]]) for p, c0 in drin], "nt", "rec_in_dx")
    dx2, df0, s = gate_norm_bwd(x2, f0, g2, row(norm_mix[1]), tc1, dx3, dh3, "l1_dnorm1")
    s = s.reshape(4, 8, D).sum(axis=1)
    d_g2, d_nmix1, d_t1, d_tc1 = s[0], s[1] * (1.0 + tc1[0]), s[2], s[1] * norm_mix[1]
    dh2, gf0, recv_rec = ffn_block_bwd(df0, res_f0, w_up[0][0], w_up[0][1], ffn_cw[0], w_down[0], "ffn0",
                                       comm=([g_rec_in, g_glu, g_rec_out], False))
    dx1, dy0, s = gate_norm_bwd(x1, y0, g1, row(norm_ffn[0]), sc2, dx2, dh2, "l0_dnorm2")
    s = s.reshape(4, 8, D).sum(axis=1)
    d_g1, d_nffn0, d_sh2, d_sc2 = s[0], s[1] * (1.0 + sc2[0]), s[2], s[1] * norm_ffn[0]
    dh1, gatt, got_b = attention_block_bwd(dy0, res_att, w_att_in, wvec, sinkvec, w_att_out, "att",
                                           comms={'swa': ([gf0['w_up'][:, :D // 2]], False),
                                                  16: ([gf0['w_up'][:, D // 2:]], False),
                                                  1: (ffn_slabs(gf0)[1:], False)},
                                           send_w_out_on=4)
    recv_ffn0 = [jnp.concatenate([got_b['swa'][0], got_b[16][0]], axis=1), got_b[1][0]]
    (grad_x, s), recv_w_in = gate_norm_bwd(x0, None, None, row(norm_mix[0]), sc1, dx1, dh1, "l0_dnorm1",
                                           comm=([_cols_to_slabs(gatt['w_in'])], False))
    recv_att = [recv_w_in[0], got_b[4][0]]
    s = s.reshape(4, 8, D).sum(axis=1)
    d_nmix0, d_sh1, d_sc1 = s[1] * (1.0 + sc1[0]), s[2], s[1] * norm_mix[0]
    dmod = jnp.stack([jnp.concatenate([d_sh1, d_sc1, d_g1, d_sh2, d_sc2, d_g2]),
                      jnp.concatenate([d_t1, d_tc1, d_tg1, d_t2, d_tc2, d_tg2])])

    P = {'ada_b': dmod, 'norm_mix': jnp.stack([d_nmix0, d_nmix1]), 'norm_ffn': jnp.stack([d_nffn0, d_nffn1]),
         'attn_q_norm_a': gatt['q_norm_a'], 'attn_k_norm_a': gatt['k_norm_a'], 'attn_q_norm_b': gatt['q_norm_b'],
         'attn_k_norm_b': gatt['k_norm_b'], 'attn_sinks': gatt['sinks'],
         's5_lambda_re': s5g[0], 's5_lambda_im': s5g[1], 's5_log_dt': s5g[2], 's5_b_re': s5g[3], 's5_b_im': s5g[4],
         's5_c_re': s5g[5], 's5_c_im': s5g[6], 'dn_a_log': gdn['a_log'], 'dn_dt_bias': gdn['dt_bias'],
         'dn_out_norm': gdn['out_norm'],
         's5_d': gs5['dskip'], 's5_glu_b': gs5['glu_b'], 'dn_conv': gdn['conv'],
         'ffn_conv': jnp.stack([gf0['conv'], gf1['conv']])}

    out = {k: {} for k in ("g", "d", "m", "v")}
    keys = ("g", "d", "m", "v")
    recv = {'attn_w_in': recv_att[0], 'attn_w_out': recv_att[1], 'rec_w_in': recv_rec[0], 's5_glu_w': recv_rec[1],
            'rec_w_out': recv_rec[2]}
    for n, gr_ in recv.items():
        res4 = reduce_adamw(gr_, _shard2d(n, W[n]), _shard2d(n, M[n]), _shard2d(n, V[n]), "adamw_" + n)
        for key, t in zip(keys, res4):
            out[key][n] = (t[:, :REC_IN] if n == 'rec_w_in' else t).reshape(W[n].shape)
    rep_sizes = [_numel(shp) for _, shp in REPLICATED]
    ss_sizes = [_numel(shp) for _, _, shp in SMALL_SHARDED]
    rep_offs = _offsets(rep_sizes + ss_sizes + [1])
    parts = [P[n].reshape(-1) for n, _ in REPLICATED] + [P[n].reshape(-1) for n, _, _ in SMALL_SHARDED]
    parts.append(lsum[0:8].sum().reshape(1))
    spack = _pack_rows(jnp.concatenate(parts), 1024, 8)
    flat2d = lambda t: t.reshape(-1, t.shape[-1])
    sall = None
    for n, idx in (('ffn_w_up', 0), ('ffn_w_down', 1)):
        comm = ([spack], True) if sall is None else None
        res4, got_s = _with_comm(reduce_adamw([recv_ffn0[idx], recv_ffn1[idx]], flat2d(W[n]), flat2d(M[n]),
                                              flat2d(V[n]), "adamw_" + n, comm=comm), comm)
        if got_s is not None:
            sall = got_s[0]
        for key, t in zip(keys, res4):
            out[key][n] = t.reshape(W[n].shape)
    n_rest = sum(ss_sizes) + 1
    pk = lambda d: _pack_rows(jnp.concatenate([d[n].reshape(-1) for n, _ in REPLICATED]
                                              + [jnp.zeros((n_rest,), F32)]), 1024, 8)
    sg, sd_, sm, sv = [t.reshape(-1) for t in reduce_adamw(sall, pk(W), pk(M), pk(V), "adamw_small")]
    loss = 0.5 * sg[rep_offs[-1]] / D

    dmod_all = sall.reshape(N_DEV, -1)[:, :2 * 6 * D].reshape(N_DEV, 2, 6 * D)
    dmod_mine = lax.dynamic_slice_in_dim(dmod_all, me * (6 * D // N_DEV), 6 * D // N_DEV, axis=2)
    g_ada = [matmul([(cond_all, dmod_mine[:, l])], "tn", f"ada{l}_dw")[None] for l in range(2)]
    ada2d = lambda t: t.reshape(2 * D, 6 * D // N_DEV)
    for key, t in zip(("g", "d", "m", "v"), reduce_adamw(g_ada, ada2d(ada_w), ada2d(m_ada_w),
                                                          ada2d(v_ada_w), "adamw_ada_w")):
        out[key]['ada_w'] = t.reshape(ada_w.shape)
    own = []
    for (n, ax, shp), o in zip(SMALL_SHARDED, rep_offs[len(REPLICATED):]):
        slabs = _to_slabs(sg[o:o + _numel(shp)].reshape(shp), ax)
        own.append(lax.dynamic_index_in_dim(slabs, me, axis=0, keepdims=False))
    own_names = [n for n, _, _ in SMALL_SHARDED]
    pk = lambda d: _pack_rows(jnp.concatenate([d[n].reshape(-1) for n in own_names]), 1024, 8)
    og, od, om, ov = [t.reshape(-1) for t in reduce_adamw(_pack_rows(jnp.concatenate(own), 1024, 8)[None],
                                                          pk(W), pk(M), pk(V), "adamw_own")]

    def unpack(names_shapes, bufs):
        o = 0
        for n, shp in names_shapes:
            k = _numel(shp)
            for key, buf in zip(("g", "d", "m", "v"), bufs):
                out[key][n] = buf[o:o + k].reshape(shp)
            o += k

    unpack(REPLICATED, (sg, sd_, sm, sv))
    unpack([(n, W[n].shape) for n in own_names], (og, od, om, ov))
    return (loss, grad_x[None], *[out["g"][n] for n in WEIGHTS], *[out["d"][n] for n in WEIGHTS],
            *[out["m"][n] for n in WEIGHTS], *[out["v"][n] for n in WEIGHTS])
```

```python
import functools
import math

import numpy as np
import jax
import jax.numpy as jnp
from jax import lax
from jax.experimental import pallas as pl
from jax.experimental.pallas import tpu as pltpu

F32 = jnp.float32
BF16 = jnp.bfloat16

N_DEV = 8
D = 1024
HD = 64
BLK = 128
ATTN_IN = 2304
CB = ATTN_IN // 128
B_BRANCHES = ((128, 1), (512, 4), (2048, 16))
S5_W = 256
S5_P = 1024
DN_H = 6
DN_DK = 128
DN_C = 64
REC_IN = 3340
REC_PAD = 3456
D_FF = 2816
EPS = 1e-6
ADAM_LR, ADAM_B1, ADAM_B2, ADAM_EPS, ADAM_WD, ADAM_STEP = 0.001, 0.9, 0.999, 1e-8, 0.01, 10
VMEM_LIMIT = 48 * 1024 * 1024

ALIBI = np.asarray(2.0 ** (-8.0 * np.arange(1, 17) / 16), dtype=np.float32)


def _cparams(*sem):
    return pltpu.CompilerParams(dimension_semantics=tuple(sem), vmem_limit_bytes=VMEM_LIMIT)


def _tile(n, target):
    if n <= target:
        return n
    best = None
    for t in range(128, target + 1, 128):
        if n % t == 0:
            best = t
    assert best is not None, (n, target)
    return best


def _rtile(n, target, mult=8):
    if n <= target:
        return n
    best = None
    for t in range(mult, target + 1, mult):
        if n % t == 0:
            best = t
    assert best is not None, (n, target)
    return best


def _fold8(x):
    r, c = x.shape
    return x.reshape(r // 8, 8, c).sum(axis=0)


def _sigmoid(x):
    return 1.0 / (1.0 + jnp.exp(-x))


_DIMS = {"nn": (((1,), (0,)), ((), ())), "nt": (((1,), (1,)), ((), ())), "tn": (((0,), (0,)), ((), ()))}


MM_FULL_K = 3584


MM_VMEM_BUDGET = 40 << 20


def matmul(pairs, mode, name, out_dtype=F32, tm=1024, tn=1536, tk=1024):
    a0, b0 = pairs[0]
    if mode == "nn":
        (M, K), N = a0.shape, b0.shape[1]
    elif mode == "nt":
        (M, K), N = a0.shape, b0.shape[0]
    else:
        (K, M), N = a0.shape, b0.shape[1]
        tm = 1536
    tn = _tile(N, tn)
    tk = K if K <= MM_FULL_K else _tile(K, tk)
    nk = K // tk
    npair = len(pairs)
    dims = _DIMS[mode]
    kdim = 0 if mode == "tn" else 1
    tks = [a.shape[kdim] for a, _ in pairs]
    assert all(t == K for t in tks) or (nk == 1 and max(tks) <= MM_FULL_K), tks
    if nk > 1:
        tks = [tk] * npair

    def planned(tm_):
        ab = sum(tm_ * t * a.dtype.itemsize + t * tn * b.dtype.itemsize for (a, b), t in zip(pairs, tks))
        return 2 * ab + 2 * tm_ * tn * jnp.dtype(out_dtype).itemsize + (tm_ * tn * 4 if nk > 1 else 0)

    while True:
        tm_try = _rtile(M, tm) if M % 128 else _tile(M, tm)
        if planned(tm_try) <= MM_VMEM_BUDGET or tm <= 128:
            break
        tm //= 2
    tm = tm_try

    def body(*refs):
        o_ref = refs[2 * npair]
        tot = None
        for p in range(npair):
            part = lax.dot_general(refs[2 * p][...].astype(BF16), refs[2 * p + 1][...].astype(BF16),
                                   dims, preferred_element_type=F32)
            tot = part if tot is None else tot + part
        if nk == 1:
            o_ref[...] = tot.astype(o_ref.dtype)
            return
        acc_ref = refs[2 * npair + 1]
        k = pl.program_id(2)

        @pl.when(k == 0)
        def _():
            acc_ref[...] = tot

        @pl.when(k > 0)
        def _():
            acc_ref[...] += tot

        @pl.when(k == nk - 1)
        def _():
            o_ref[...] = acc_ref[...].astype(o_ref.dtype)

    def specs(t):
        if mode == "nn":
            return [pl.BlockSpec((tm, t), lambda j, i, k: (i, k)), pl.BlockSpec((t, tn), lambda j, i, k: (k, j))]
        if mode == "nt":
            return [pl.BlockSpec((tm, t), lambda j, i, k: (i, k)), pl.BlockSpec((tn, t), lambda j, i, k: (j, k))]
        return [pl.BlockSpec((t, tm), lambda j, i, k: (k, i)), pl.BlockSpec((t, tn), lambda j, i, k: (k, j))]

    flat = [t for pr in pairs for t in pr]
    return pl.pallas_call(
        body, name=name, grid=(N // tn, M // tm, nk),
        in_specs=[s for t in tks for s in specs(t)],
        out_specs=pl.BlockSpec((tm, tn), lambda j, i, k: (i, j)),
        out_shape=jax.ShapeDtypeStruct((M, N), out_dtype),
        scratch_shapes=[pltpu.VMEM((tm, tn), F32)] if nk > 1 else [],
        compiler_params=_cparams("parallel", "parallel", "arbitrary"),
    )(*flat)


def gate_norm_fwd(x, y, gate, nw, sh, sc, name):
    L, C = x.shape
    tl = _rtile(L, 512)
    has_gate = y is not None

    def body(*refs):
        if has_gate:
            x_ref, y_ref, g_ref, nw_ref, sh_ref, sc_ref, xn_ref, h_ref = refs
            xn = x_ref[...] + g_ref[...] * y_ref[...]
            xn_ref[...] = xn
        else:
            x_ref, nw_ref, sh_ref, sc_ref, h_ref = refs
            xn = x_ref[...]
        r = lax.rsqrt(jnp.mean(xn * xn, axis=-1, keepdims=True) + EPS)
        h = (xn * r * nw_ref[...]) * (1.0 + sc_ref[...]) + sh_ref[...]
        h_ref[...] = h.astype(BF16)

    big = pl.BlockSpec((tl, C), lambda i: (i, 0))
    vec = pl.BlockSpec((1, C), lambda i: (0, 0))
    if has_gate:
        ins, in_specs = (x, y, gate, nw, sh, sc), [big, big, vec, vec, vec, vec]
        out_shape = (jax.ShapeDtypeStruct((L, C), F32), jax.ShapeDtypeStruct((L, C), BF16))
        out_specs = (big, big)
    else:
        ins, in_specs = (x, nw, sh, sc), [big, vec, vec, vec]
        out_shape = jax.ShapeDtypeStruct((L, C), BF16)
        out_specs = big
    return pl.pallas_call(body, name=name, grid=(L // tl,), in_specs=in_specs, out_specs=out_specs,
                          out_shape=out_shape, compiler_params=_cparams("parallel"))(*ins)


def gate_norm_bwd(xn, y, gate, nw, sc, dxn_direct, dh, name, comm=None):
    L, C = xn.shape
    tl = _rtile(L, 256)
    has_gate = y is not None
    has_direct = dxn_direct is not None

    def body(*refs):
        refs = list(refs)
        xn_ref = refs.pop(0)
        y_ref = refs.pop(0) if has_gate else None
        g_ref = refs.pop(0) if has_gate else None
        nw_ref = refs.pop(0)
        sc_ref = refs.pop(0)
        dd_ref = refs.pop(0) if has_direct else None
        dh_ref = refs.pop(0)
        dxn_ref = refs.pop(0)
        dy_ref = refs.pop(0) if has_gate else None
        sums_ref = refs.pop(0)

        @pl.when(pl.program_id(0) == 0)
        def _():
            sums_ref[...] = jnp.zeros_like(sums_ref)

        xv = xn_ref[...]
        dh_v = dh_ref[...]
        r = lax.rsqrt(jnp.mean(xv * xv, axis=-1, keepdims=True) + EPS)
        n = xv * r
        a = nw_ref[...] * (1.0 + sc_ref[...])
        dn = dh_v * a
        dx = r * (dn - n * jnp.mean(dn * n, axis=-1, keepdims=True))
        if has_direct:
            dx = dx + dd_ref[...]
        dxn_ref[...] = dx
        sums_ref[8:16, :] += _fold8(dh_v * n)
        sums_ref[16:24, :] += _fold8(dh_v)
        if has_gate:
            dy_ref[...] = (dx * g_ref[...]).astype(BF16)
            sums_ref[0:8, :] += _fold8(dx * y_ref[...])

    big = pl.BlockSpec((tl, C), lambda i: (i, 0))
    vec = pl.BlockSpec((1, C), lambda i: (0, 0))
    ins, in_specs = [xn], [big]
    if has_gate:
        ins += [y, gate]
        in_specs += [big, vec]
    ins += [nw, sc]
    in_specs += [vec, vec]
    if has_direct:
        ins.append(dxn_direct)
        in_specs.append(big)
    ins.append(dh)
    in_specs.append(big)
    out_shape = [jax.ShapeDtypeStruct((L, C), F32)]
    out_specs = [big]
    if has_gate:
        out_shape.append(jax.ShapeDtypeStruct((L, C), BF16))
        out_specs.append(big)
    out_shape.append(jax.ShapeDtypeStruct((32, C), F32))
    out_specs.append(pl.BlockSpec((32, C), lambda i: (0, 0)))
    return _call(body, ins, name=name, grid=(L // tl,), in_specs=in_specs, out_specs=tuple(out_specs),
                 out_shape=tuple(out_shape), sem=("arbitrary",), comm=comm)


def final_loss(x, f, gate, target, name):
    L, C = x.shape
    tl = _rtile(L, 256)

    def body(x_ref, f_ref, g_ref, t_ref, dy_ref, df_ref, sums_ref):
        @pl.when(pl.program_id(0) == 0)
        def _():
            sums_ref[...] = jnp.zeros_like(sums_ref)

        fv = f_ref[...]
        err = x_ref[...] + g_ref[...] * fv - t_ref[...]
        dy = err * (1.0 / C)
        dy_ref[...] = dy
        df_ref[...] = (dy * g_ref[...]).astype(BF16)
        sums_ref[0:8, :] += _fold8(err * err)
        sums_ref[8:16, :] += _fold8(dy * fv)

    big = pl.BlockSpec((tl, C), lambda i: (i, 0))
    vec = pl.BlockSpec((1, C), lambda i: (0, 0))
    return pl.pallas_call(
        body, name=name, grid=(L // tl,), in_specs=[big, big, vec, big],
        out_specs=(big, big, pl.BlockSpec((16, C), lambda i: (0, 0))),
        out_shape=(jax.ShapeDtypeStruct((L, C), F32), jax.ShapeDtypeStruct((L, C), BF16),
                   jax.ShapeDtypeStruct((16, C), F32)),
        compiler_params=_cparams("arbitrary"))(x, f, gate, target)


def _seg_ones(seg):
    r = lax.broadcasted_iota(jnp.int32, (128, 128), 0) // seg
    c = lax.broadcasted_iota(jnp.int32, (128, 128), 1) // seg
    return (r == c).astype(BF16)


def _segsum(t, ones):
    hi = t.astype(BF16)
    lo = (t - hi.astype(F32)).astype(BF16)
    return (jnp.dot(hi, ones, preferred_element_type=F32) + jnp.dot(lo, ones, preferred_element_type=F32))


_NORMED_TILES = tuple(list(range(0, 5)) + list(range(6, 14)))


DIL = (4, 16)
B_COLS0, B_W = 768, 1536
DIL_TL = 256


def _to_dilated(scr_ref, out_ref, d, cast=None):
    nj, tl, _ = scr_ref.shape
    for r in range(d):
        for j in range(nj):
            piece = scr_ref[j, pl.ds(r, tl // d, stride=d), :]
            c0 = (r * nj + j) * 128
            out_ref[:, c0:c0 + 128] = piece if cast is None else piece.astype(cast)


def _from_dilated(in_ref, scr_ref, d):
    nj, tl, _ = scr_ref.shape
    for r in range(d):
        for j in range(nj):
            c0 = (r * nj + j) * 128
            scr_ref[j, pl.ds(r, tl // d, stride=d), :] = in_ref[:, c0:c0 + 128]


def _dil_spec(tl, d, width):
    return pl.BlockSpec((tl // d, d * width), lambda i: (i, 0))


def qknorm_fwd(qkv, wvec, name):
    L, C = qkv.shape
    tl = DIL_TL

    def body(x_ref, w_ref, o_ref, o4_ref, o16_ref, scr_ref):
        ones = _seg_ones(HD)
        for t in range(CB):
            cs = slice(t * 128, (t + 1) * 128)
            x = x_ref[:, cs]
            if t in _NORMED_TILES:
                ms = _segsum(x * x, ones) * (1.0 / HD)
                x = x * lax.rsqrt(ms + EPS) * w_ref[:, cs]
            o_ref[:, cs] = x.astype(BF16)
            if t * 128 >= B_COLS0:
                scr_ref[t - B_COLS0 // 128] = x
        _to_dilated(scr_ref, o4_ref, 4, BF16)
        _to_dilated(scr_ref, o16_ref, 16, BF16)

    return pl.pallas_call(
        body, name=name, grid=(L // tl,),
        in_specs=[pl.BlockSpec((tl, C), lambda i: (i, 0)), pl.BlockSpec((1, C), lambda i: (0, 0))],
        out_specs=(pl.BlockSpec((tl, C), lambda i: (i, 0)), _dil_spec(tl, 4, B_W), _dil_spec(tl, 16, B_W)),
        out_shape=(jax.ShapeDtypeStruct((L, C), BF16), jax.ShapeDtypeStruct((L // 4, 4 * B_W), BF16),
                   jax.ShapeDtypeStruct((L // 16, 16 * B_W), BF16)),
        scratch_shapes=[pltpu.VMEM((B_W // 128, tl, 128), F32)], compiler_params=_cparams("parallel"))(qkv, wvec)


def qknorm_bwd(qkv, wvec, d_a, d_b, name):
    L, C = qkv.shape
    tl = DIL_TL

    def body(x_ref, w_ref, dqa, dka, dva, q1, k1, v1, q4, k4, v4, q16, k16, v16, dx_ref, sums_ref,
             dy_ref, s4_ref, s16_ref):
        @pl.when(pl.program_id(0) == 0)
        def _():
            sums_ref[...] = jnp.zeros_like(sums_ref)

        dy_ref[:, 0:512] = dqa[...]
        for off, ref in ((512, dka), (640, dva)):
            for g in range(2):
                acc = ref[:, g * 256:g * 256 + HD]
                for h in range(1, 4):
                    acc = acc + ref[:, g * 256 + h * HD:g * 256 + (h + 1) * HD]
                dy_ref[:, off + g * HD:off + (g + 1) * HD] = acc
        for off, r1, r4, r16 in ((768, q1, q4, q16), (1280, k1, k4, k16), (1792, v1, v4, v16)):
            _from_dilated(r4, s4_ref, 4)
            _from_dilated(r16, s16_ref, 16)
            for j in range(4):
                dy_ref[:, off + j * 128:off + (j + 1) * 128] = r1[:, j * 128:(j + 1) * 128] + s4_ref[j] + s16_ref[j]

        ones = _seg_ones(HD)
        for t in range(CB):
            cs = slice(t * 128, (t + 1) * 128)
            d = dy_ref[:, cs]
            if t in _NORMED_TILES:
                x = x_ref[:, cs]
                r = lax.rsqrt(_segsum(x * x, ones) * (1.0 / HD) + EPS)
                n = x * r
                dn = d * w_ref[:, cs]
                dx_ref[:, cs] = (r * (dn - n * (_segsum(dn * n, ones) * (1.0 / HD)))).astype(BF16)
                sums_ref[:, cs] += _fold8(d * n)
            else:
                dx_ref[:, cs] = d.astype(BF16)

    big = pl.BlockSpec((tl, C), lambda i: (i, 0))
    p512 = pl.BlockSpec((tl, 512), lambda i: (i, 0))
    return pl.pallas_call(
        body, name=name, grid=(L // tl,),
        in_specs=[big, pl.BlockSpec((1, C), lambda i: (0, 0))] + [p512] * 6 + [_dil_spec(tl, 4, 512)] * 3
        + [_dil_spec(tl, 16, 512)] * 3,
        out_specs=(big, pl.BlockSpec((8, C), lambda i: (0, 0))),
        out_shape=(jax.ShapeDtypeStruct((L, C), BF16), jax.ShapeDtypeStruct((8, C), F32)),
        scratch_shapes=[pltpu.VMEM((tl, C), F32), pltpu.VMEM((4, tl, 128), F32), pltpu.VMEM((4, tl, 128), F32)],
        compiler_params=_cparams("arbitrary"))(qkv, wvec, *d_a, *d_b[0], *d_b[1], *d_b[2])


def _attn_biases(t, slopes, step, maxdist):
    qi = lax.broadcasted_iota(jnp.int32, (BLK, 2 * BLK), 0)
    sj = lax.broadcasted_iota(jnp.int32, (BLK, 2 * BLK), 1)
    dist = BLK + qi - sj
    valid = (dist >= 0) & (dist <= maxdist)
    distf = (step * dist).astype(F32)
    inner = [jnp.where(valid, (-sl) * distf, -jnp.inf) for sl in slopes]
    first = [jnp.where((t > 0) | (sj >= BLK), b, -jnp.inf) for b in inner]
    return inner, first


def _attn_scores(q, kw, bias):
    return lax.dot_general(q, kw, (((1,), (1,)), ((), ())), preferred_element_type=F32) + bias


ATT_NQ = 8


def _attn_operands(nq, hp, gqa, q_ref, kh_ref, kc_ref, vh_ref, vc_ref):
    ops = []
    for b in range(nq):
        rows = slice(b * BLK, (b + 1) * BLK)
        prev = slice((b - 1) * BLK, b * BLK)
        for e in range(2):
            cs = slice(e * HD, (e + 1) * HD)
            if gqa:
                ksel = lambda ref, r: jnp.where(hp >= 2, ref[r, 64:128], ref[r, 0:64])
            else:
                ksel = lambda ref, r, cs=cs: ref[r, cs]
            kprev = ksel(kh_ref, slice(0, BLK)) if b == 0 else ksel(kc_ref, prev)
            vprev = ksel(vh_ref, slice(0, BLK)) if b == 0 else ksel(vc_ref, prev)
            ops.append((b, e, rows, cs, q_ref[rows, cs] * (HD ** -0.5),
                        jnp.concatenate([kprev, ksel(kc_ref, rows)], axis=0),
                        jnp.concatenate([vprev, ksel(vc_ref, rows)], axis=0)))
    return ops


def _attn_specs(cb, q_off, k_off, v_off, gqa):
    kcol = (lambda r, hp: r * cb + k_off) if gqa else (lambda r, hp: r * cb + k_off + hp)
    vcol = (lambda r, hp: r * cb + v_off) if gqa else (lambda r, hp: r * cb + v_off + hp)
    return kcol, vcol


def attn_fwd(X, d, q_off, k_off, v_off, gqa, slope0, maxdist, name, comm=None):
    Ls = X.shape[0]
    nq = min(ATT_NQ, Ls // BLK)
    TQ = nq * BLK
    nt = Ls // TQ
    slopes = jnp.asarray(ALIBI)

    def body(sl_ref, q_ref, kh_ref, kc_ref, vh_ref, vc_ref, o_ref, lse_ref):
        hp, t = pl.program_id(1), pl.program_id(2)
        ops = _attn_operands(nq, hp, gqa, q_ref, kh_ref, kc_ref, vh_ref, vc_ref)
        inner, first = _attn_biases(t, [sl_ref[slope0 + 2 * hp + e] for e in range(2)], d, maxdist)
        s = [_attn_scores(q, kw, first[e] if b == 0 else inner[e]) for (b, e, rows, cs, q, kw, vw) in ops]
        m = [jnp.max(x, axis=-1, keepdims=True) for x in s]
        p = [jnp.exp(x - mm) for x, mm in zip(s, m)]
        l = [jnp.sum(x, axis=-1, keepdims=True) for x in p]
        o = [jnp.dot(x.astype(BF16), op[6], preferred_element_type=F32) / ll for x, op, ll in zip(p, ops, l)]
        for (b, e, rows, cs, q, kw, vw), oo, mm, ll in zip(ops, o, m, l):
            o_ref[rows, cs] = oo
            lse_ref[rows, cs] = jnp.broadcast_to(mm + jnp.log(ll), (BLK, HD))

    cb = X.shape[1] // (d * 128)
    kcol, vcol = _attn_specs(cb, q_off, k_off, v_off, gqa)
    tile, blk = (TQ, 128), (BLK, 128)
    halo = lambda t: jnp.maximum(t * nq - 1, 0)
    in_specs = [
        pl.BlockSpec(memory_space=pltpu.SMEM),
        pl.BlockSpec(tile, lambda r, hp, t: (t, r * cb + q_off + hp)),
        pl.BlockSpec(blk, lambda r, hp, t: (halo(t), kcol(r, hp))),
        pl.BlockSpec(tile, lambda r, hp, t: (t, kcol(r, hp))),
        pl.BlockSpec(blk, lambda r, hp, t: (halo(t), vcol(r, hp))),
        pl.BlockSpec(tile, lambda r, hp, t: (t, vcol(r, hp))),
    ]
    out_spec = pl.BlockSpec(tile, lambda r, hp, t: (t, r * 4 + hp))
    out = jax.ShapeDtypeStruct((Ls, d * 512), F32)
    return _call(body, (slopes, X, X, X, X, X), name=name, grid=(d, 4, nt), in_specs=in_specs,
                 out_specs=(out_spec, out_spec), out_shape=(out, out),
                 sem=("parallel", "parallel", "arbitrary"), comm=comm)


def attn_bwd(X, o, lse, do, dlse, d, q_off, k_off, v_off, gqa, slope0, maxdist, name, comm=None):
    Ls = X.shape[0]
    slopes = jnp.asarray(ALIBI)

    nq = min(ATT_NQ, Ls // BLK)
    TQ = nq * BLK
    nt = Ls // TQ
    nt_dims, tn_dims = (((1,), (1,)), ((), ())), (((0,), (0,)), ((), ()))

    def body(sl_ref, q_ref, kh_ref, kc_ref, vh_ref, vc_ref, o_ref, lse_ref, do_ref, dlse_ref,
             dq_ref, dk_ref, dv_ref, ak_ref, av_ref, pk_ref, pv_ref):
        hp, t = pl.program_id(1), pl.program_id(2)

        @pl.when(t == 0)
        def _():
            pk_ref[...] = jnp.zeros_like(pk_ref)
            pv_ref[...] = jnp.zeros_like(pv_ref)

        @pl.when(t < nt)
        def _():
            ops = _attn_operands(nq, hp, gqa, q_ref, kh_ref, kc_ref, vh_ref, vc_ref)
            inner, first = _attn_biases(t, [sl_ref[slope0 + 2 * hp + e] for e in range(2)], d, maxdist)
            sv = [_attn_scores(q, kw, first[e] if b == 0 else inner[e]) for (b, e, rows, cs, q, kw, vw) in ops]
            p = [jnp.exp(s - lse_ref[op[2], op[1] * HD:op[1] * HD + 1]) for s, op in zip(sv, ops)]
            dov = [do_ref[op[2], op[3]] for op in ops]
            delta = [jnp.sum(dd * o_ref[op[2], op[3]], axis=-1, keepdims=True) for dd, op in zip(dov, ops)]
            dob = [dd.astype(BF16) for dd in dov]
            dp = [lax.dot_general(dd, op[6], nt_dims, preferred_element_type=F32) for dd, op in zip(dob, ops)]
            ds = [(pp * (x - dl + dlse_ref[op[2], op[1] * HD:op[1] * HD + 1])).astype(BF16)
                  for pp, x, dl, op in zip(p, dp, delta, ops)]
            dq = [jnp.dot(x, op[5], preferred_element_type=F32) * (HD ** -0.5) for x, op in zip(ds, ops)]
            dkw = [lax.dot_general(x, op[4], tn_dims, preferred_element_type=F32) for x, op in zip(ds, ops)]
            dvw = [lax.dot_general(pp.astype(BF16), dd, tn_dims, preferred_element_type=F32)
                   for pp, dd in zip(p, dob)]
            ak_ref[...] = jnp.zeros_like(ak_ref)
            av_ref[...] = jnp.zeros_like(av_ref)
            for (b, e, rows, cs, q, kw, vw), x, yk, yv in zip(ops, dq, dkw, dvw):
                dq_ref[rows, cs] = x
                ak_ref[b * BLK:(b + 2) * BLK, cs] += yk
                av_ref[b * BLK:(b + 2) * BLK, cs] += yv
            if nt == 1:
                dk_ref[...] = ak_ref[BLK:, :]
                dv_ref[...] = av_ref[BLK:, :]
                return
            last = slice(TQ - BLK, TQ)
            dk_ref[...] = pk_ref[...]
            dv_ref[...] = pv_ref[...]
            dk_ref[last, :] += ak_ref[0:BLK, :]
            dv_ref[last, :] += av_ref[0:BLK, :]
            pk_ref[...] = ak_ref[BLK:, :]
            pv_ref[...] = av_ref[BLK:, :]

        @pl.when(t == nt)
        def _():
            dk_ref[...] = pk_ref[...]
            dv_ref[...] = pv_ref[...]

    cb = X.shape[1] // (d * 128)
    kcol, vcol = _attn_specs(cb, q_off, k_off, v_off, gqa)
    tile, blk = (TQ, 128), (BLK, 128)
    cur = lambda t: jnp.minimum(t, nt - 1)
    halo = lambda t: jnp.maximum(cur(t) * nq - 1, 0)
    ospec = pl.BlockSpec(tile, lambda r, hp, t: (cur(t), r * 4 + hp))
    in_specs = [
        pl.BlockSpec(memory_space=pltpu.SMEM),
        pl.BlockSpec(tile, lambda r, hp, t: (cur(t), r * cb + q_off + hp)),
        pl.BlockSpec(blk, lambda r, hp, t: (halo(t), kcol(r, hp))),
        pl.BlockSpec(tile, lambda r, hp, t: (cur(t), kcol(r, hp))),
        pl.BlockSpec(blk, lambda r, hp, t: (halo(t), vcol(r, hp))),
        pl.BlockSpec(tile, lambda r, hp, t: (cur(t), vcol(r, hp))),
        ospec, ospec, ospec, ospec,
    ]
    shifted = pl.BlockSpec(tile, lambda r, hp, t: (jnp.maximum(t - 1, 0), r * 4 + hp))
    out = jax.ShapeDtypeStruct((Ls, d * 512), F32)
    return _call(body, (slopes, X, X, X, X, X, o, lse, do, dlse), name=name, grid=(d, 4, nt + 1 if nt > 1 else 1),
                 in_specs=in_specs, out_specs=(ospec, shifted, shifted), out_shape=(out, out, out),
                 scratch_shapes=[pltpu.VMEM((TQ + BLK, 128), F32), pltpu.VMEM((TQ + BLK, 128), F32),
                                 pltpu.VMEM((TQ, 128), F32), pltpu.VMEM((TQ, 128), F32)],
                 sem=("parallel", "parallel", "arbitrary"), comm=comm)


def attn_merge_fwd(oa, la, sink, obs, lbs, name):
    L = oa.shape[0]
    tl = DIL_TL

    def body(oa_ref, la_ref, sk_ref, o1, o4, o16, l1, l4, l16, m_ref, so4, so16, sl4, sl16):
        m_ref[:, 0:512] = (oa_ref[...] * _sigmoid(la_ref[...] - sk_ref[...])).astype(BF16)
        for src, dst, d in ((o4, so4, 4), (o16, so16, 16), (l4, sl4, 4), (l16, sl16, 16)):
            _from_dilated(src, dst, d)
        for j in range(4):
            cs = slice(j * 128, (j + 1) * 128)
            a, b, c = l1[:, cs], sl4[j], sl16[j]
            mx = jnp.maximum(jnp.maximum(a, b), c)
            ea, eb, ec = jnp.exp(a - mx), jnp.exp(b - mx), jnp.exp(c - mx)
            inv = 1.0 / (ea + eb + ec)
            m_ref[:, 512 + j * 128:512 + (j + 1) * 128] = (
                (ea * inv) * o1[:, cs] + (eb * inv) * so4[j] + (ec * inv) * so16[j]).astype(BF16)

    big = pl.BlockSpec((tl, 512), lambda i: (i, 0))
    dil = [big, _dil_spec(tl, 4, 512), _dil_spec(tl, 16, 512)]
    return pl.pallas_call(
        body, name=name, grid=(L // tl,),
        in_specs=[big, big, pl.BlockSpec((1, 512), lambda i: (0, 0))] + dil + dil,
        out_specs=pl.BlockSpec((tl, 1024), lambda i: (i, 0)),
        out_shape=jax.ShapeDtypeStruct((L, 1024), BF16), scratch_shapes=[pltpu.VMEM((4, tl, 128), F32)] * 4,
        compiler_params=_cparams("parallel"),
    )(oa, la, sink, *obs, *lbs)


def attn_merge_bwd(dm, oa, la, sink, obs, lbs, name):
    L = oa.shape[0]
    tl = DIL_TL

    def body(dm_ref, oa_ref, la_ref, sk_ref, o1, o4, o16, l1, l4, l16,
             doa_ref, dla_ref, d1, d4, d16, g1, g4, g16, sums_ref, so4, so16, sl4, sl16, sd4, sd16, sg4, sg16):
        @pl.when(pl.program_id(0) == 0)
        def _():
            sums_ref[...] = jnp.zeros_like(sums_ref)

        for src, dst, d in ((o4, so4, 4), (o16, so16, 16), (l4, sl4, 4), (l16, sl16, 16)):
            _from_dilated(src, dst, d)
        ones = _seg_ones(HD)
        for t in range(4):
            cs = slice(t * 128, (t + 1) * 128)
            dma = dm_ref[:, cs]
            keep = _sigmoid(la_ref[:, cs] - sk_ref[:, cs])
            doa_ref[:, cs] = dma * keep
            tt = dma * oa_ref[:, cs] * keep * (1.0 - keep)
            dla_ref[:, cs] = _segsum(tt, ones)
            sums_ref[:, cs] += _fold8(-tt)
            dmb = dm_ref[:, 512 + t * 128:512 + (t + 1) * 128]
            a, b, c = l1[:, cs], sl4[t], sl16[t]
            mx = jnp.maximum(jnp.maximum(a, b), c)
            ea, eb, ec = jnp.exp(a - mx), jnp.exp(b - mx), jnp.exp(c - mx)
            inv = 1.0 / (ea + eb + ec)
            wa, wb, wc = ea * inv, eb * inv, ec * inv
            d1[:, cs] = wa * dmb
            sd4[t] = wb * dmb
            sd16[t] = wc * dmb
            sa = _segsum(dmb * o1[:, cs], ones)
            sb = _segsum(dmb * so4[t], ones)
            sc_ = _segsum(dmb * so16[t], ones)
            mean = wa * sa + wb * sb + wc * sc_
            g1[:, cs] = wa * (sa - mean)
            sg4[t] = wb * (sb - mean)
            sg16[t] = wc * (sc_ - mean)
        for src, dst, d in ((sd4, d4, 4), (sd16, d16, 16), (sg4, g4, 4), (sg16, g16, 16)):
            _to_dilated(src, dst, d)

    big = pl.BlockSpec((tl, 512), lambda i: (i, 0))
    dil = [big, _dil_spec(tl, 4, 512), _dil_spec(tl, 16, 512)]
    sd = jax.ShapeDtypeStruct
    shp = [sd((L, 512), F32), sd((L // 4, 4 * 512), F32), sd((L // 16, 16 * 512), F32)]
    return pl.pallas_call(
        body, name=name, grid=(L // tl,),
        in_specs=[pl.BlockSpec((tl, 1024), lambda i: (i, 0)), big, big,
                  pl.BlockSpec((1, 512), lambda i: (0, 0))] + dil + dil,
        out_specs=tuple([big, big] + dil + dil + [pl.BlockSpec((8, 512), lambda i: (0, 0))]),
        out_shape=tuple([shp[0], shp[0]] + shp + shp + [sd((8, 512), F32)]),
        scratch_shapes=[pltpu.VMEM((4, tl, 128), F32)] * 8, compiler_params=_cparams("arbitrary"),
    )(dm, oa, la, sink, *obs, *lbs)


def _shift_down(x, halo, k, first):
    rows = lax.broadcasted_iota(jnp.int32, (8, x.shape[1]), 0)
    out = pltpu.roll(x, k, axis=0)
    hrows = jnp.where(first, 0.0, pltpu.roll(halo, k, axis=0))
    top = jnp.where(rows < k, hrows, out[0:8, :])
    return jnp.concatenate([top, out[8:, :]], axis=0)


def _shift_up(x, nxt, k):
    tl = x.shape[0]
    rows = lax.broadcasted_iota(jnp.int32, (8, x.shape[1]), 0)
    out = pltpu.roll(x, tl - k, axis=0)
    bottom = jnp.where(rows >= 8 - k, pltpu.roll(nxt, 8 - k, axis=0), out[tl - 8:, :])
    return jnp.concatenate([out[:tl - 8, :], bottom], axis=0)


def _silu(x):
    return x * _sigmoid(x)


def _dsilu(x):
    s = _sigmoid(x)
    return s * (1.0 + x * (1.0 - s))


def ffn_act_fwd(ua, ub, cw, name, comm=None):
    L, F = ua.shape
    tl = _rtile(L, 256)
    tc = _tile(F, 1408)
    hb = tl // 8

    def body(ua_ref, uah_ref, ub_ref, ubh_ref, wa_ref, wb_ref, o_ref, ac_ref, bc_ref):
        first = pl.program_id(1) == 0

        def conv(x_ref, h_ref, w_ref):
            x = x_ref[...]
            h = h_ref[...]
            return (w_ref[2:3, :] * x + w_ref[1:2, :] * _shift_down(x, h, 1, first)
                    + w_ref[0:1, :] * _shift_down(x, h, 2, first))

        a = conv(ua_ref, uah_ref, wa_ref)
        b = conv(ub_ref, ubh_ref, wb_ref)
        ac_ref[...] = a
        bc_ref[...] = b
        o_ref[...] = (_silu(a) * b).astype(BF16)

    main = pl.BlockSpec((tl, tc), lambda j, i: (i, j))
    halo = pl.BlockSpec((8, tc), lambda j, i: (jnp.maximum(i * hb - 1, 0), j))
    wa = pl.BlockSpec((3, tc), lambda j, i: (0, j))
    wb = pl.BlockSpec((3, tc), lambda j, i: (0, j + F // tc))
    f32 = jax.ShapeDtypeStruct((L, F), F32)
    return _call(body, (ua, ua, ub, ub, cw, cw), name=name, grid=(F // tc, L // tl),
                 in_specs=[main, halo, main, halo, wa, wb], out_specs=(main, main, main),
                 out_shape=(jax.ShapeDtypeStruct((L, F), BF16), f32, f32), sem=("parallel", "parallel"), comm=comm)


def ffn_act_bwd(ua, ub, ac, bc, cw, dact, name, comm=None):
    L, F = ua.shape
    tl = _rtile(L, 256)
    tc = _tile(F, 1408)
    nrt = L // tl

    def body(ua_ref, ub_ref, ac_ref, bc_ref, wa_ref, wb_ref, da_ref, dua_ref, dub_ref, sums_ref, ca_ref, cb_ref):
        i = pl.program_id(1)

        @pl.when(i == 0)
        def _():
            sums_ref[...] = jnp.zeros_like(sums_ref)
            ca_ref[...] = jnp.zeros_like(ca_ref)
            cb_ref[...] = jnp.zeros_like(cb_ref)

        a, b = ac_ref[...], bc_ref[...]
        dact_v = da_ref[...]
        dya = dact_v * b * _dsilu(a)
        dyb = dact_v * _silu(a)
        for (dy, w_ref, c_ref, d_ref, x_ref, base) in ((dya, wa_ref, ca_ref, dua_ref, ua_ref, 0),
                                                        (dyb, wb_ref, cb_ref, dub_ref, ub_ref, 24)):
            nxt = c_ref[...]
            ups = (dy, _shift_up(dy, nxt, 1), _shift_up(dy, nxt, 2))
            d_ref[...] = (w_ref[2:3, :] * ups[0] + w_ref[1:2, :] * ups[1] + w_ref[0:1, :] * ups[2]).astype(BF16)
            c_ref[...] = dy[0:8, :]
            x = x_ref[...]
            for k in range(3):
                sums_ref[base + 8 * (2 - k):base + 8 * (2 - k) + 8, :] += _fold8(ups[k] * x)

    rev = lambda i: nrt - 1 - i
    main = pl.BlockSpec((tl, tc), lambda j, i: (rev(i), j))
    wa = pl.BlockSpec((3, tc), lambda j, i: (0, j))
    wb = pl.BlockSpec((3, tc), lambda j, i: (0, j + F // tc))
    ob = jax.ShapeDtypeStruct((L, F), BF16)
    return _call(body, (ua, ub, ac, bc, cw, cw, dact), name=name, grid=(F // tc, nrt),
                 in_specs=[main, main, main, main, wa, wb, main],
                 out_specs=(main, main, pl.BlockSpec((48, tc), lambda j, i: (0, j))),
                 out_shape=(ob, ob, jax.ShapeDtypeStruct((48, F), F32)),
                 scratch_shapes=[pltpu.VMEM((8, tc), F32), pltpu.VMEM((8, tc), F32)],
                 sem=("parallel", "arbitrary"), comm=comm)


def attn_vectors(qna, kna, qnb, knb, sinks):
    ones = jnp.ones((128,), F32)
    wvec = jnp.concatenate([jnp.tile(qna, 8), jnp.tile(kna, 2), ones, jnp.tile(qnb, 8), jnp.tile(knb, 8),
                            jnp.tile(ones, 4)]).reshape(1, ATTN_IN)
    return wvec, jnp.repeat(sinks, HD).reshape(1, 512)


def _with_comm(result, comm):
    return result if comm is not None else (result, None)


def attention_block_fwd(h, w_in, wvec, sinkvec, w_out, tag, comms=None):
    L = h.shape[0]
    comms = comms or {}
    got = {}
    qkv = matmul([(h, w_in)], "nn", tag + "_qkv")
    X, X4, X16 = qknorm_fwd(qkv, wvec, tag + "_qknorm")
    (oa, la), got['swa'] = _with_comm(attn_fwd(X, 1, 0, 4, 5, True, 0, BLK - 1, tag + "_swa",
                                               comm=comms.get('swa')), comms.get('swa'))
    views = {1: (X, 6, 10, 14), 4: (X4, 0, 4, 8), 16: (X16, 0, 4, 8)}
    obs, lbs = [], []
    for window, d in B_BRANCHES:
        xd, qo, ko, vo = views[d]
        (o, l), got[d] = _with_comm(attn_fwd(xd, d, qo, ko, vo, False, 8, window // d,
                                             tag + f"_dil{d}", comm=comms.get(d)), comms.get(d))
        obs.append(o)
        lbs.append(l)
    m = attn_merge_fwd(oa, la, sinkvec, obs, lbs, tag + "_merge")
    if w_out is None:
        w_out = got[16][0].reshape(D, D)
        got['w_out'] = w_out
    y = matmul([(m, w_out)], "nn", tag + "_out")
    return y, (h, qkv, views, oa, la, obs, lbs, m), got


def attention_block_bwd(dy, res, w_in, wvec, sinkvec, w_out, tag, comms=None, send_w_out_on=None):
    h, qkv, views, oa, la, obs, lbs, m = res
    comms = dict(comms or {})
    got = {}
    g_w_out = matmul([(m, dy)], "tn", tag + "_dwout", out_dtype=BF16)
    if send_w_out_on is not None:
        comms[send_w_out_on] = ([g_w_out.reshape(N_DEV, D // N_DEV, D)], False)
    dm = matmul([(dy, w_out)], "nt", tag + "_dm")
    doa, dla, d1, d2, d3, g1, g2, g3, sinksums = attn_merge_bwd(dm, oa, la, sinkvec, obs, lbs, tag + "_dmerge")
    d_a, got['swa'] = _with_comm(attn_bwd(views[1][0], oa, la, doa, dla, 1, 0, 4, 5, True, 0, BLK - 1,
                                          tag + "_dswa", comm=comms.get('swa')), comms.get('swa'))
    d_b = []
    for (window, d), o, l, do, dl in zip(B_BRANCHES, obs, lbs, (d1, d2, d3), (g1, g2, g3)):
        xd, qo, ko, vo = views[d]
        dqkv_d, got[d] = _with_comm(attn_bwd(xd, o, l, do, dl, d, qo, ko, vo, False, 8, window // d,
                                             tag + f"_ddil{d}", comm=comms.get(d)), comms.get(d))
        d_b.append(dqkv_d)
    dqkv, wsums = qknorm_bwd(qkv, wvec, d_a, d_b, tag + "_dqknorm")
    g_w_in = matmul([(h, dqkv)], "tn", tag + "_dwin", out_dtype=BF16)
    dh = matmul([(dqkv, w_in)], "nt", tag + "_dh")
    ws = wsums.sum(axis=0)
    grads = dict(
        w_in=g_w_in, w_out=g_w_out,
        q_norm_a=ws[0:512].reshape(8, HD).sum(axis=0), k_norm_a=ws[512:640].reshape(2, HD).sum(axis=0),
        q_norm_b=ws[768:1280].reshape(8, HD).sum(axis=0), k_norm_b=ws[1280:1792].reshape(8, HD).sum(axis=0),
        sinks=sinksums.sum(axis=0).reshape(8, HD).sum(axis=1))
    return dh, grads, got


def ffn_block_fwd(h, w_up_a, w_up_b, cw, w_down, tag, comm=None):
    ua = matmul([(h, w_up_a)], "nn", tag + "_upa")
    ub = matmul([(h, w_up_b)], "nn", tag + "_upb")
    (act, ac, bc), got = _with_comm(ffn_act_fwd(ua, ub, cw, tag + "_act", comm=comm), comm)
    f = matmul([(act, w_down)], "nn", tag + "_down")
    return f, (h, ua, ub, ac, bc, act), got


def ffn_block_bwd(df, res, w_up_a, w_up_b, cw, w_down, tag, comm=None):
    h, ua, ub, ac, bc, act = res
    g_down = matmul([(act, df)], "tn", tag + "_dwdown", out_dtype=BF16)
    dact = matmul([(df, w_down)], "nt", tag + "_dact")
    (dua, dub, sums), got = _with_comm(ffn_act_bwd(ua, ub, ac, bc, cw, dact, tag + "_dactk", comm=comm), comm)
    g_up = jnp.concatenate([_cols_to_slabs(matmul([(h, dua)], "tn", tag + "_dwupa", out_dtype=BF16), N_DEV // 2),
                            _cols_to_slabs(matmul([(h, dub)], "tn", tag + "_dwupb", out_dtype=BF16), N_DEV // 2)],
                           axis=0)
    dh = matmul([(dua, w_up_a), (dub, w_up_b)], "nt", tag + "_dh")
    s = sums.reshape(2, 3, 8, D_FF).sum(axis=2)
    g_conv = jnp.concatenate([s[0], s[1]], axis=1)
    return dh, dict(w_up=g_up, conv=g_conv, w_down=g_down), got


def s5_params(lam_re, lam_im, log_dt, b_re, b_im, c_re, c_im):
    dt = jnp.exp(log_dt)[:, None]
    mag, ang = jnp.exp(lam_re * dt), lam_im * dt
    a_re, a_im = mag * jnp.cos(ang), mag * jnp.sin(ang)
    nr, ni = a_re - 1.0, a_im
    den = lam_re * lam_re + lam_im * lam_im
    f_re = (nr * lam_re + ni * lam_im) / den
    f_im = (ni * lam_re - nr * lam_im) / den
    eye = jnp.eye(16, dtype=F32)[:, None, :, None]
    bd = lambda b: (eye * jnp.transpose(b, (0, 2, 1))[:, :, None, :]).reshape(S5_W, S5_P)
    cd = lambda c: (eye * jnp.transpose(c, (0, 2, 1))[:, :, None, :]).reshape(S5_P, S5_W)
    flat = lambda t: t.reshape(1, S5_P)
    return flat(a_re), flat(a_im), flat(f_re), flat(f_im), bd(b_re), bd(b_im), cd(c_re), cd(c_im)


def _scan_tables(a_re, a_im, reverse):
    pows = [(a_re, a_im)]
    for _ in range(7):
        pr, pi = pows[-1]
        pows.append((pr * a_re - pi * a_im, pr * a_im + pi * a_re))
    order = list(range(7, -1, -1)) if reverse else list(range(8))
    z = jnp.zeros_like(a_re)
    rows = [pows[0][0], pows[0][1], pows[1][0], pows[1][1], pows[3][0], pows[3][1], z, z]
    rows += [pows[k][0] for k in order] + [pows[k][1] for k in order]
    return jnp.concatenate(rows, axis=0)


def _block_scan(er, ei, tab_ref, cr, ci, reverse):
    rows = lax.broadcasted_iota(jnp.int32, er.shape, 0)
    for idx, s in enumerate((1, 2, 4)):
        if reverse:
            sr, si, keep = pltpu.roll(er, 8 - s, axis=0), pltpu.roll(ei, 8 - s, axis=0), rows < 8 - s
        else:
            sr, si, keep = pltpu.roll(er, s, axis=0), pltpu.roll(ei, s, axis=0), rows >= s
        sr, si = jnp.where(keep, sr, 0.0), jnp.where(keep, si, 0.0)
        ar, ai = tab_ref[2 * idx:2 * idx + 1, :], tab_ref[2 * idx + 1:2 * idx + 2, :]
        er, ei = er + ar * sr - ai * si, ei + ar * si + ai * sr
    pr, pi_ = tab_ref[8:16, :], tab_ref[16:24, :]
    er, ei = er + pr * cr - pi_ * ci, ei + pr * ci + pi_ * cr
    return er, ei


def s5_scan_fwd(bu_re, bu_im, a_re, a_im, f_re, f_im, name):
    L, P = bu_re.shape
    tl = _rtile(L, 512)
    tab = _scan_tables(a_re, a_im, False)
    fvec = jnp.concatenate([f_re, f_im] + [jnp.zeros_like(f_re)] * 6, axis=0)

    def body(br_ref, bi_ref, tab_ref, f_ref, xr_ref, xi_ref, c_ref):
        @pl.when(pl.program_id(0) == 0)
        def _():
            c_ref[...] = jnp.zeros_like(c_ref)

        def blk(i, carry):
            cr, ci = carry
            rows = pl.ds(pl.multiple_of(i * 8, 8), 8)
            br, bi = br_ref[rows, :], bi_ref[rows, :]
            fr, fi = f_ref[0:1, :], f_ref[1:2, :]
            er, ei = _block_scan(fr * br - fi * bi, fr * bi + fi * br, tab_ref, cr, ci, False)
            xr_ref[rows, :] = er
            xi_ref[rows, :] = ei
            return er[7:8, :], ei[7:8, :]

        cr, ci = lax.fori_loop(0, tl // 8, blk, (c_ref[0:1, :], c_ref[1:2, :]))
        c_ref[0:1, :] = cr
        c_ref[1:2, :] = ci

    big = pl.BlockSpec((tl, P), lambda i: (i, 0))
    out = jax.ShapeDtypeStruct((L, P), F32)
    return pl.pallas_call(
        body, name=name, grid=(L // tl,),
        in_specs=[big, big, pl.BlockSpec((24, P), lambda i: (0, 0)), pl.BlockSpec((8, P), lambda i: (0, 0))],
        out_specs=(big, big), out_shape=(out, out), scratch_shapes=[pltpu.VMEM((8, P), F32)],
        compiler_params=_cparams("arbitrary"))(bu_re, bu_im, tab, fvec)


def s5_scan_bwd(dx_re, dx_im, x_re, x_im, bu_re, bu_im, a_re, a_im, f_re, f_im, name):
    L, P = dx_re.shape
    tl = _rtile(L, 256)
    nt = L // tl
    tab = _scan_tables(a_re, -a_im, True)
    fvec = jnp.concatenate([f_re, f_im] + [jnp.zeros_like(f_re)] * 6, axis=0)

    def body(gr_ref, gi_ref, xr_ref, xi_ref, br_ref, bi_ref, tab_ref, f_ref, dbr_ref, dbi_ref, s_ref, c_ref):
        @pl.when(pl.program_id(0) == 0)
        def _():
            c_ref[...] = jnp.zeros_like(c_ref)
            s_ref[...] = jnp.zeros_like(s_ref)

        def blk(k, carry):
            cr, ci = carry
            i = tl // 8 - 1 - k
            rows = pl.ds(pl.multiple_of(i * 8, 8), 8)
            er, ei = _block_scan(gr_ref[rows, :], gi_ref[rows, :], tab_ref, cr, ci, True)
            rid = lax.broadcasted_iota(jnp.int32, er.shape, 0)
            sr = jnp.where(rid == 7, cr, pltpu.roll(er, 7, axis=0))
            si = jnp.where(rid == 7, ci, pltpu.roll(ei, 7, axis=0))
            xr, xi = xr_ref[rows, :], xi_ref[rows, :]
            s_ref[0:8, :] += sr * xr + si * xi
            s_ref[8:16, :] += si * xr - sr * xi
            br, bi = br_ref[rows, :], bi_ref[rows, :]
            s_ref[16:24, :] += er * br + ei * bi
            s_ref[24:32, :] += ei * br - er * bi
            fr, fi = f_ref[0:1, :], f_ref[1:2, :]
            dbr_ref[rows, :] = fr * er + fi * ei
            dbi_ref[rows, :] = fr * ei - fi * er
            return er[0:1, :], ei[0:1, :]

        cr, ci = lax.fori_loop(0, tl // 8, blk, (c_ref[0:1, :], c_ref[1:2, :]))
        c_ref[0:1, :] = cr
        c_ref[1:2, :] = ci

    big = pl.BlockSpec((tl, P), lambda i: (nt - 1 - i, 0))
    out = jax.ShapeDtypeStruct((L, P), F32)
    return pl.pallas_call(
        body, name=name, grid=(nt,),
        in_specs=[big] * 6 + [pl.BlockSpec((24, P), lambda i: (0, 0)), pl.BlockSpec((8, P), lambda i: (0, 0))],
        out_specs=(big, big, pl.BlockSpec((32, P), lambda i: (0, 0))),
        out_shape=(out, out, jax.ShapeDtypeStruct((32, P), F32)), scratch_shapes=[pltpu.VMEM((8, P), F32)],
        compiler_params=_cparams("arbitrary"))(dx_re, dx_im, x_re, x_im, bu_re, bu_im, tab, fvec)


_GK, _GC = math.sqrt(2.0 / math.pi), 0.044715


def _gelu(y):
    return 0.5 * y * (1.0 + jnp.tanh(_GK * (y + _GC * y * y * y)))


def _dgelu(y):
    t = jnp.tanh(_GK * (y + _GC * y * y * y))
    return 0.5 * (1.0 + t) + 0.5 * y * (1.0 - t * t) * _GK * (1.0 + 3.0 * _GC * y * y)


def s5_out_fwd(x_re, x_im, u, cd_re, cd_im, dskip, glu_w, glu_b, name):
    L = u.shape[0]
    tl = _rtile(L, 512)

    def body(xr_ref, xi_ref, u_ref, cr_ref, ci_ref, d_ref, w_ref, b_ref, y_ref, o_ref):
        y = (jnp.dot(xr_ref[...].astype(BF16), cr_ref[...], preferred_element_type=F32)
             - jnp.dot(xi_ref[...].astype(BF16), ci_ref[...], preferred_element_type=F32)
             + d_ref[...] * u_ref[...])
        y_ref[...] = y
        g = _gelu(y)
        z = jnp.dot(g.astype(BF16), w_ref[...], preferred_element_type=F32) + b_ref[...]
        o_ref[...] = (g * _sigmoid(z)).astype(BF16)

    big = pl.BlockSpec((tl, S5_P), lambda i: (i, 0))
    sm = pl.BlockSpec((tl, S5_W), lambda i: (i, 0))
    full = lambda r, c: pl.BlockSpec((r, c), lambda i: (0, 0))
    return pl.pallas_call(
        body, name=name, grid=(L // tl,),
        in_specs=[big, big, sm, full(S5_P, S5_W), full(S5_P, S5_W), full(1, S5_W), full(S5_W, S5_W), full(1, S5_W)],
        out_specs=(sm, sm),
        out_shape=(jax.ShapeDtypeStruct((L, S5_W), F32), jax.ShapeDtypeStruct((L, S5_W), BF16)),
        compiler_params=_cparams("parallel"))(x_re, x_im, u, cd_re, cd_im, dskip, glu_w, glu_b)


def s5_out_bwd(dout, y, u, x_re, x_im, cd_re, cd_im, dskip, glu_w, glu_b, name, dout_col=0):
    L = u.shape[0]
    tl = _rtile(L, 256)
    nt_dims = (((1,), (1,)), ((), ()))
    tn_dims = (((0,), (0,)), ((), ()))

    def body(do_ref, y_ref, u_ref, xr_ref, xi_ref, cr_ref, ci_ref, d_ref, w_ref, b_ref,
             dxr_ref, dxi_ref, du_ref, dcr_ref, dci_ref, dw_ref, s_ref):
        @pl.when(pl.program_id(0) == 0)
        def _():
            dcr_ref[...] = jnp.zeros_like(dcr_ref)
            dci_ref[...] = jnp.zeros_like(dci_ref)
            dw_ref[...] = jnp.zeros_like(dw_ref)
            s_ref[...] = jnp.zeros_like(s_ref)

        yv, dov = y_ref[...], do_ref[...]
        g = _gelu(yv)
        gb = g.astype(BF16)
        sg = _sigmoid(jnp.dot(gb, w_ref[...], preferred_element_type=F32) + b_ref[...])
        dz = dov * g * sg * (1.0 - sg)
        dzb = dz.astype(BF16)
        dg = dov * sg + lax.dot_general(dzb, w_ref[...], nt_dims, preferred_element_type=F32)
        dw_ref[...] += lax.dot_general(gb, dzb, tn_dims, preferred_element_type=F32)
        dy = dg * _dgelu(yv)
        dyb = dy.astype(BF16)
        s_ref[0:8, :] += _fold8(dy * u_ref[...])
        s_ref[8:16, :] += _fold8(dz)
        du_ref[...] = dy * d_ref[...]
        dxr_ref[...] = lax.dot_general(dyb, cr_ref[...], nt_dims, preferred_element_type=F32)
        dxi_ref[...] = -lax.dot_general(dyb, ci_ref[...], nt_dims, preferred_element_type=F32)
        dcr_ref[...] += lax.dot_general(xr_ref[...].astype(BF16), dyb, tn_dims, preferred_element_type=F32)
        dci_ref[...] -= lax.dot_general(xi_ref[...].astype(BF16), dyb, tn_dims, preferred_element_type=F32)

    big = pl.BlockSpec((tl, S5_P), lambda i: (i, 0))
    sm = pl.BlockSpec((tl, S5_W), lambda i: (i, 0))
    full = lambda r, c: pl.BlockSpec((r, c), lambda i: (0, 0))
    sd = jax.ShapeDtypeStruct
    return pl.pallas_call(
        body, name=name, grid=(L // tl,),
        in_specs=[pl.BlockSpec((tl, S5_W), lambda i: (i, dout_col)), sm, sm, big, big, full(S5_P, S5_W),
                  full(S5_P, S5_W), full(1, S5_W), full(S5_W, S5_W), full(1, S5_W)],
        out_specs=(big, big, sm, full(S5_P, S5_W), full(S5_P, S5_W), full(S5_W, S5_W), full(16, S5_W)),
        out_shape=(sd((L, S5_P), F32), sd((L, S5_P), F32), sd((L, S5_W), F32), sd((S5_P, S5_W), F32),
                   sd((S5_P, S5_W), F32), sd((S5_W, S5_W), F32), sd((16, S5_W), F32)),
        compiler_params=_cparams("arbitrary"))(dout, y, u, x_re, x_im, cd_re, cd_im, dskip, glu_w, glu_b)


def s5_block_fwd(u, params, dskip, glu_w, glu_b, tag):
    a_re, a_im, f_re, f_im, bd_re, bd_im, cd_re, cd_im = params
    bu_re = matmul([(u, bd_re.astype(BF16))], "nn", tag + "_bure")
    bu_im = matmul([(u, bd_im.astype(BF16))], "nn", tag + "_buim")
    x_re, x_im = s5_scan_fwd(bu_re, bu_im, a_re, a_im, f_re, f_im, tag + "_scan")
    y, out = s5_out_fwd(x_re, x_im, u, cd_re.astype(BF16), cd_im.astype(BF16), dskip, glu_w, glu_b, tag + "_out")
    return out, (u, bu_re, bu_im, x_re, x_im, y)


def s5_block_bwd(dout, res, params, dskip, glu_w, glu_b, tag, dout_col=0):
    u, bu_re, bu_im, x_re, x_im, y = res
    a_re, a_im, f_re, f_im, bd_re, bd_im, cd_re, cd_im = params
    dxr, dxi, du, dcr, dci, dglu_w, sums = s5_out_bwd(dout, y, u, x_re, x_im, cd_re.astype(BF16), cd_im.astype(BF16),
                                                      dskip, glu_w, glu_b, tag + "_dout", dout_col=dout_col)
    dbr, dbi, acc = s5_scan_bwd(dxr, dxi, x_re, x_im, bu_re, bu_im, a_re, a_im, f_re, f_im, tag + "_dscan")
    du = du + matmul([(dbr, bd_re.astype(BF16)), (dbi, bd_im.astype(BF16))], "nt", tag + "_du")
    dbd_re = matmul([(u, dbr)], "tn", tag + "_dbdre")
    dbd_im = matmul([(u, dbi)], "tn", tag + "_dbdim")
    acc = acc.reshape(4, 8, S5_P).sum(axis=1)
    s = sums.reshape(2, 8, S5_W).sum(axis=1)
    cot = (acc[0:1], acc[1:2], acc[2:3], acc[3:4], dbd_re, dbd_im, dcr, dci)
    return du, cot, dict(dskip=s[0], glu_w=dglu_w, glu_b=s[1])


DN_Z0, DN_NT = 18, 18
REC_U0, REC_A0 = 3072, 3328


def rec_cols_permute(w):
    return jnp.concatenate([w[..., S5_W:REC_A0], w[..., :S5_W], w[..., REC_A0:]], axis=-1)


def rec_cols_restore(w):
    return jnp.concatenate([w[..., REC_U0:REC_A0], w[..., :REC_U0], w[..., REC_A0:]], axis=-1)


DN_W = DN_H * DN_DK


def _dn_conv4(taps, w_ref):
    xc = w_ref[3:4, :] * taps[0]
    for k in range(1, 4):
        xc = xc + w_ref[3 - k:4 - k, :] * taps[k]
    return xc


def dn_prep_fwd(rin, cw, name):
    L = rin.shape[0]
    tl = _rtile(L, 256)
    hb = tl // 8

    def body(x_ref, h_ref, w_ref, o_ref):
        j = pl.program_id(0)
        first = pl.program_id(1) == 0
        x, h = x_ref[...], h_ref[...]
        s = _silu(_dn_conv4([x] + [_shift_down(x, h, k, first) for k in range(1, 4)], w_ref))
        scale = jnp.where(j == 0, DN_DK ** -0.5, 1.0)
        for hd in _HEADS:
            cs = slice(hd * 128, (hd + 1) * 128)
            sh = s[:, cs]
            r = lax.rsqrt(jnp.sum(sh * sh, axis=-1, keepdims=True) + EPS)
            o_ref[:, cs] = jnp.where(j < 2, sh * r * scale, sh)

    main = pl.BlockSpec((tl, DN_W), lambda j, i: (i, j))
    halo = pl.BlockSpec((8, DN_W), lambda j, i: (jnp.maximum(i * hb - 1, 0), j))
    return pl.pallas_call(
        body, name=name, grid=(3, L // tl),
        in_specs=[main, halo, pl.BlockSpec((4, DN_W), lambda j, i: (0, j))],
        out_specs=main, out_shape=jax.ShapeDtypeStruct((L, 3 * DN_W), F32),
        compiler_params=_cparams("parallel", "parallel"))(rin, rin, cw)


def dn_prep_bwd(rin, cw, dout, name):
    L = rin.shape[0]
    tl = _rtile(L, 256)
    hb = tl // 8
    nrt = L // tl

    def body(x_ref, h_ref, w_ref, d_ref, dx_ref, s_ref, c_ref):
        j = pl.program_id(0)
        i = pl.program_id(1)
        first = i == nrt - 1

        @pl.when(i == 0)
        def _():
            s_ref[...] = jnp.zeros_like(s_ref)
            c_ref[...] = jnp.zeros_like(c_ref)

        x, h = x_ref[...], h_ref[...]
        taps = [x] + [_shift_down(x, h, k, first) for k in range(1, 4)]
        xc = _dn_conv4(taps, w_ref)
        s = _silu(xc)
        scale = jnp.where(j == 0, DN_DK ** -0.5, 1.0)
        pieces = []
        for hd in _HEADS:
            cs = slice(hd * 128, (hd + 1) * 128)
            sh, d = s[:, cs], d_ref[:, cs]
            r = lax.rsqrt(jnp.sum(sh * sh, axis=-1, keepdims=True) + EPS)
            n = sh * r
            dn = d * scale
            pieces.append(jnp.where(j < 2, r * (dn - n * jnp.sum(dn * n, axis=-1, keepdims=True)), d))
        dxc = jnp.concatenate(pieces, axis=1) * _dsilu(xc)
        nxt = c_ref[...]
        dx_ref[...] = _dn_conv4([dxc] + [_shift_up(dxc, nxt, k) for k in range(1, 4)], w_ref).astype(BF16)
        c_ref[...] = dxc[0:8, :]
        for k in range(4):
            s_ref[8 * (3 - k):8 * (3 - k) + 8, :] += _fold8(dxc * taps[k])

    rev = lambda i: nrt - 1 - i
    main = pl.BlockSpec((tl, DN_W), lambda j, i: (rev(i), j))
    halo = pl.BlockSpec((8, DN_W), lambda j, i: (jnp.maximum(rev(i) * hb - 1, 0), j))
    return pl.pallas_call(
        body, name=name, grid=(3, nrt),
        in_specs=[main, halo, pl.BlockSpec((4, DN_W), lambda j, i: (0, j)), main],
        out_specs=(main, pl.BlockSpec((32, DN_W), lambda j, i: (0, j))),
        out_shape=(jax.ShapeDtypeStruct((L, 3 * DN_W), BF16), jax.ShapeDtypeStruct((32, 3 * DN_W), F32)),
        scratch_shapes=[pltpu.VMEM((8, DN_W), F32)],
        compiler_params=_cparams("parallel", "arbitrary"))(rin, rin, cw, dout)


_HI = lax.Precision.HIGH
_NT = (((1,), (1,)), ((), ()))
_TN = (((0,), (0,)), ((), ()))
_HEADS = tuple(range(DN_H))


def _mm(a, b, dims=(((1,), (0,)), ((), ())), hi=False):
    if hi:
        return lax.dot_general(a, b, dims, precision=_HI, preferred_element_type=F32)
    return lax.dot_general(a.astype(BF16), b.astype(BF16), dims, preferred_element_type=F32)


def _dn_masks():
    ri = lax.broadcasted_iota(jnp.int32, (DN_C, DN_C), 0)
    ci = lax.broadcasted_iota(jnp.int32, (DN_C, DN_C), 1)
    return ri >= ci, ri > ci, (ri == ci).astype(F32)


def _dn_decay(gc, gr, causal):
    gam = [jnp.where(causal, jnp.exp(jnp.where(causal, gc[h] - gr[h], 0.0)), 0.0) for h in _HEADS]
    eg = [jnp.exp(gc[h]) for h in _HEADS]
    el = [jnp.exp(gc[h][DN_C - 1:DN_C, :] - gc[h]) for h in _HEADS]
    gl = [jnp.exp(gc[h][DN_C - 1:DN_C, :]) for h in _HEADS]
    return gam, eg, el, gl


def _dn_solve(k, v, beta, gam, eg, kk, strict, eye):
    nmat = [jnp.where(strict, beta[h] * kk[h] * gam[h], 0.0) for h in _HEADS]
    t = [eye - nmat[h] for h in _HEADS]
    m = [_mm(nmat[h], nmat[h]) for h in _HEADS]
    for step in range(5):
        t = [t[h] + _mm(t[h], m[h]) for h in _HEADS]
        if step < 4:
            m = [_mm(m[h], m[h]) for h in _HEADS]
    res = [eye - t[h] - _mm(nmat[h], t[h], hi=True) for h in _HEADS]
    t = [t[h] + _mm(t[h], res[h]) for h in _HEADS]
    rhs = [jnp.concatenate([v[h] * beta[h], k[h] * (beta[h] * eg[h])], axis=1) for h in _HEADS]
    sol = [_mm(t[h], rhs[h], hi=True) for h in _HEADS]
    return t, sol


def dn_chunk_fwd(qkv, gcol, grow, bcol, name, comm=None):
    L = qkv.shape[0]
    C, W = DN_C, DN_H * DN_DK
    ncb = 8
    tl = ncb * C
    nchunks = L // C

    def body(q_ref, k_ref, v_ref, gc_ref, gr_ref, b_ref, o_ref, sh_ref, t_ref, sol_ref, s_ref):
        @pl.when(pl.program_id(0) == 0)
        def _():
            s_ref[...] = jnp.zeros_like(s_ref)

        causal, strict, eye = _dn_masks()

        def chunk(c, _):
            rows = pl.ds(pl.multiple_of(c * C, C), C)
            grow_c = gr_ref[c]
            hs = lambda h: slice(h * 128, (h + 1) * 128)
            q = [q_ref[rows, hs(h)] for h in _HEADS]
            k = [k_ref[rows, hs(h)] for h in _HEADS]
            v = [v_ref[rows, hs(h)] for h in _HEADS]
            gc = [gc_ref[rows, h:h + 1] for h in _HEADS]
            gr = [grow_c[h:h + 1, :] for h in _HEADS]
            beta = [b_ref[rows, h:h + 1] for h in _HEADS]
            gam, eg, el, gl = _dn_decay(gc, gr, causal)
            kk = [_mm(k[h], k[h], _NT) for h in _HEADS]
            t, sol = _dn_solve(k, v, beta, gam, eg, kk, strict, eye)
            qk = [_mm(q[h], k[h], _NT) * gam[h] for h in _HEADS]
            S = [s_ref[hs(h), :] for h in _HEADS]
            vn = [sol[h][:, :128] - _mm(sol[h][:, 128:], S[h]) for h in _HEADS]
            o = [_mm(q[h] * eg[h], S[h]) + _mm(qk[h], vn[h]) for h in _HEADS]
            Sn = [S[h] * gl[h] + _mm(k[h] * el[h], vn[h], _TN) for h in _HEADS]
            for h in _HEADS:
                sh_ref[c, hs(h), :] = S[h]
                s_ref[hs(h), :] = Sn[h]
                o_ref[rows, hs(h)] = o[h]
                t_ref[rows, h * C:(h + 1) * C] = t[h]
                sol_ref[rows, h * 256:(h + 1) * 256] = sol[h]
            return 0

        lax.fori_loop(0, ncb, chunk, 0)

    col = lambda b: pl.BlockSpec((tl, W), lambda i: (i, b))
    small = pl.BlockSpec((tl, 8), lambda i: (i, 0))
    rowblk = lambda w: pl.BlockSpec((tl, w), lambda i: (i, 0))
    sd = jax.ShapeDtypeStruct
    return _call(body, (qkv, qkv, qkv, gcol, grow, bcol), name=name, grid=(L // tl,),
                 in_specs=[col(0), col(1), col(2), small, pl.BlockSpec((ncb, 8, C), lambda i: (i, 0, 0)), small],
                 out_specs=(rowblk(W), pl.BlockSpec((ncb, W, 128), lambda i: (i, 0, 0)), rowblk(DN_H * C),
                            rowblk(DN_H * 256)),
                 out_shape=(sd((L, W), F32), sd((nchunks, W, 128), F32), sd((L, DN_H * C), F32),
                            sd((L, DN_H * 256), F32)),
                 scratch_shapes=[pltpu.VMEM((W, 128), F32)], sem=("arbitrary",), comm=comm)


def dn_chunk_bwd(qkv, gcol, grow, bcol, shist, thist, solhist, do, name, comm=None):
    L = qkv.shape[0]
    C, W = DN_C, DN_H * DN_DK
    ncb = 8
    tl = ncb * C
    nchunks = L // C
    nt = L // tl

    def body(q_ref, k_ref, v_ref, gc_ref, gr_ref, b_ref, sh_ref, t_ref, sol_ref, do_ref,
             dqkv_ref, dgc_ref, dgr_ref, db_ref, ds_ref):
        @pl.when(pl.program_id(0) == 0)
        def _():
            ds_ref[...] = jnp.zeros_like(ds_ref)

        lane8 = lax.broadcasted_iota(jnp.int32, (C, 8), 1)
        sub8 = lax.broadcasted_iota(jnp.int32, (8, C), 0)
        rowid = lax.broadcasted_iota(jnp.int32, (C, 1), 0)
        causal, strict, _ = _dn_masks()
        rsum = lambda a: jnp.sum(a, axis=1, keepdims=True)

        def chunk(cc, _):
            c = ncb - 1 - cc
            rows = pl.ds(pl.multiple_of(c * C, C), C)
            grow_c = gr_ref[c]
            hs = lambda h: slice(h * 128, (h + 1) * 128)
            q = [q_ref[rows, hs(h)] for h in _HEADS]
            k = [k_ref[rows, hs(h)] for h in _HEADS]
            v = [v_ref[rows, hs(h)] for h in _HEADS]
            gc = [gc_ref[rows, h:h + 1] for h in _HEADS]
            gr = [grow_c[h:h + 1, :] for h in _HEADS]
            beta = [b_ref[rows, h:h + 1] for h in _HEADS]
            t = [t_ref[rows, h * C:(h + 1) * C] for h in _HEADS]
            sol = [sol_ref[rows, h * 256:(h + 1) * 256] for h in _HEADS]
            S = [sh_ref[c, hs(h), :] for h in _HEADS]
            dS = [ds_ref[hs(h), :] for h in _HEADS]
            dov = [do_ref[rows, hs(h)] for h in _HEADS]
            gam, eg, el, gl = _dn_decay(gc, gr, causal)
            kk = [_mm(k[h], k[h], _NT) for h in _HEADS]
            qk_raw = [_mm(q[h], k[h], _NT) for h in _HEADS]
            w = [sol[h][:, 128:] for h in _HEADS]
            kd = [k[h] * el[h] for h in _HEADS]
            vn = [sol[h][:, :128] - _mm(w[h], S[h]) for h in _HEADS]
            dvn = [_mm(qk_raw[h] * gam[h], dov[h], _TN) + _mm(kd[h], dS[h]) for h in _HEADS]
            dqd = [_mm(dov[h], S[h], _NT) for h in _HEADS]
            dqk = [jnp.where(causal, _mm(dov[h], vn[h], _NT), 0.0) for h in _HEADS]
            dkd = [_mm(vn[h], dS[h], _NT) for h in _HEADS]
            dgl = [jnp.sum(rsum(dS[h] * S[h]), axis=0, keepdims=True) for h in _HEADS]
            dw = [-_mm(dvn[h], S[h], _NT) for h in _HEADS]
            dSn = [dS[h] * gl[h] + _mm(q[h] * eg[h], dov[h], _TN) - _mm(w[h], dvn[h], _TN) for h in _HEADS]
            drhs = [_mm(t[h], jnp.concatenate([dvn[h], dw[h]], axis=1), _TN) for h in _HEADS]
            dn = [jnp.where(strict, -_mm(drhs[h], sol[h], _NT), 0.0) for h in _HEADS]
            dgc_all = jnp.zeros((C, 8), F32)
            db_all = jnp.zeros((C, 8), F32)
            dgr_all = jnp.zeros((8, C), F32)
            for h in _HEADS:
                drv, drk = drhs[h][:, :128], drhs[h][:, 128:]
                t2 = rsum(drk * k[h])
                x = dn[h] * gam[h]
                dbeta = rsum(drv * v[h]) + t2 * eg[h] + rsum(x * kk[h])
                dkk = x * beta[h]
                draw = dqk[h] * gam[h]
                mm_ = (dn[h] * beta[h] * kk[h] + dqk[h] * qk_raw[h]) * gam[h]
                deg = t2 * beta[h] + rsum(dqd[h] * q[h])
                r_ = rsum(dkd[h] * k[h]) * el[h]
                dglast = jnp.sum(r_, axis=0, keepdims=True) + dgl[h] * gl[h]
                dgc = rsum(mm_) + deg * eg[h] - r_ + jnp.where(rowid == C - 1, dglast, 0.0)
                dgr = -jnp.sum(mm_, axis=0, keepdims=True)
                dqkv_ref[rows, hs(h)] = _mm(draw, k[h]) + dqd[h] * eg[h]
                dqkv_ref[rows, hs(DN_H + h)] = (drk * (beta[h] * eg[h]) + _mm(dkk, k[h]) + _mm(dkk, k[h], _TN)
                                                + _mm(draw, q[h], _TN) + dkd[h] * el[h])
                dqkv_ref[rows, hs(2 * DN_H + h)] = drv * beta[h]
                ds_ref[hs(h), :] = dSn[h]
                dgc_all = dgc_all + jnp.where(lane8 == h, dgc, 0.0)
                db_all = db_all + jnp.where(lane8 == h, dbeta, 0.0)
                dgr_all = dgr_all + jnp.where(sub8 == h, dgr, 0.0)
            dgc_ref[rows, :] = dgc_all
            db_ref[rows, :] = db_all
            dgr_ref[c] = dgr_all
            return 0

        lax.fori_loop(0, ncb, chunk, 0)

    rev = lambda i: nt - 1 - i
    col = lambda b: pl.BlockSpec((tl, W), lambda i: (rev(i), b))
    rowblk = lambda w: pl.BlockSpec((tl, w), lambda i: (rev(i), 0))
    small = pl.BlockSpec((tl, 8), lambda i: (rev(i), 0))
    g3 = pl.BlockSpec((ncb, 8, C), lambda i: (rev(i), 0, 0))
    sd = jax.ShapeDtypeStruct
    return _call(body, (qkv, qkv, qkv, gcol, grow, bcol, shist, thist, solhist, do), name=name, grid=(nt,),
                 in_specs=[col(0), col(1), col(2), small, g3, small,
                           pl.BlockSpec((ncb, W, 128), lambda i: (rev(i), 0, 0)), rowblk(DN_H * C),
                           rowblk(DN_H * 256), col(0)],
                 out_specs=(rowblk(3 * W), small, g3, small),
                 out_shape=(sd((L, 3 * W), F32), sd((L, 8), F32), sd((nchunks, 8, C), F32), sd((L, 8), F32)),
                 scratch_shapes=[pltpu.VMEM((W, 128), F32)], sem=("arbitrary",), comm=comm)


def dn_out_fwd(o, rin, nw, name):
    L = o.shape[0]
    tl = _rtile(L, 256)

    def body(o_ref, z_ref, w_ref, y_ref):
        for hd in _HEADS:
            cs = slice(hd * 128, (hd + 1) * 128)
            ov = o_ref[:, cs]
            r = lax.rsqrt(jnp.mean(ov * ov, axis=-1, keepdims=True) + EPS)
            y_ref[:, cs] = (ov * r * w_ref[...] * _silu(z_ref[:, cs])).astype(BF16)

    return pl.pallas_call(
        body, name=name, grid=(L // tl,),
        in_specs=[pl.BlockSpec((tl, DN_W), lambda i: (i, 0)), pl.BlockSpec((tl, DN_W), lambda i: (i, 3)),
                  pl.BlockSpec((1, 128), lambda i: (0, 0))],
        out_specs=pl.BlockSpec((tl, DN_W), lambda i: (i, 0)), out_shape=jax.ShapeDtypeStruct((L, DN_W), BF16),
        compiler_params=_cparams("parallel"))(o, rin, nw)


def dn_out_bwd(dycat, o, rin, nw, name):
    L = o.shape[0]
    tl = _rtile(L, 256)

    def body(dy_ref, o_ref, z_ref, w_ref, do_ref, dz_ref, s_ref):
        @pl.when(pl.program_id(0) == 0)
        def _():
            s_ref[...] = jnp.zeros_like(s_ref)

        for hd in _HEADS:
            cs = slice(hd * 128, (hd + 1) * 128)
            ov, zv, d = o_ref[:, cs], z_ref[:, cs], dy_ref[:, cs]
            r = lax.rsqrt(jnp.mean(ov * ov, axis=-1, keepdims=True) + EPS)
            n = ov * r
            dnw = d * _silu(zv)
            dz_ref[:, cs] = (d * n * w_ref[...] * _dsilu(zv)).astype(BF16)
            dn = dnw * w_ref[...]
            do_ref[:, cs] = r * (dn - n * jnp.mean(dn * n, axis=-1, keepdims=True))
            s_ref[:, cs] += _fold8(dnw * n)

    own = pl.BlockSpec((tl, DN_W), lambda i: (i, 0))
    sd = jax.ShapeDtypeStruct
    return pl.pallas_call(
        body, name=name, grid=(L // tl,),
        in_specs=[own, own, pl.BlockSpec((tl, DN_W), lambda i: (i, 3)), pl.BlockSpec((1, 128), lambda i: (0, 0))],
        out_specs=(own, own, pl.BlockSpec((8, DN_W), lambda i: (0, 0))),
        out_shape=(sd((L, DN_W), F32), sd((L, DN_W), BF16), sd((8, DN_W), F32)),
        compiler_params=_cparams("arbitrary"))(dycat, o, rin, nw)


def dn_gates(a, beta_raw, a_log, dt_bias):
    L = a.shape[0]
    beta = jax.nn.sigmoid(beta_raw)
    g = -jnp.exp(a_log) * jax.nn.softplus(a + dt_bias)
    G = jnp.cumsum(g.reshape(L // DN_C, DN_C, DN_H), axis=1)
    pad = lambda t: jnp.pad(t, ((0, 0), (0, 8 - DN_H)))
    gcol = pad(G.reshape(L, DN_H))
    grow = jnp.pad(jnp.transpose(G, (0, 2, 1)), ((0, 0), (0, 8 - DN_H), (0, 0)))
    return gcol, grow, pad(beta)


def dn_block_fwd(rin, cw, a_log, dt_bias, out_norm, tag, comm=None):
    gates, gates_vjp = jax.vjp(dn_gates, rin[:, REC_A0:REC_A0 + DN_H], rin[:, REC_A0 + DN_H:REC_IN], a_log, dt_bias)
    qkv = dn_prep_fwd(rin, cw, tag + "_prep")
    (o, shist, thist, solhist), got = _with_comm(dn_chunk_fwd(qkv, *gates, tag + "_chunk", comm=comm), comm)
    yd = dn_out_fwd(o, rin, out_norm.reshape(1, 128), tag + "_onorm")
    return yd, (qkv, gates, gates_vjp, o, shist, thist, solhist), got


def dn_block_bwd(dyd, res, rin, cw, out_norm, tag, comm=None):
    qkv, gates, gates_vjp, o, shist, thist, solhist = res
    do, dz, nsum = dn_out_bwd(dyd, o, rin, out_norm.reshape(1, 128), tag + "_donorm")
    (dqkv, dgc, dgr, db), got = _with_comm(dn_chunk_bwd(qkv, *gates, shist, thist, solhist, do, tag + "_dchunk",
                                                        comm=comm), comm)
    da, dbraw, g_alog, g_dtb = gates_vjp((dgc, dgr, db))
    dx, csum = dn_prep_bwd(rin, cw, dqkv, tag + "_dprep")
    grads = dict(conv=csum.reshape(4, 8, DN_NT * 128).sum(axis=1), a_log=g_alog, dt_bias=g_dtb,
                 out_norm=nsum.sum(axis=0).reshape(DN_H, 128).sum(axis=0))
    return dx, dz, da, dbraw, grads, got


_HBM = pl.BlockSpec(memory_space=pltpu.HBM)


def _mesh_pos():
    xi, yi, ci = lax.axis_index("x"), lax.axis_index("y"), lax.axis_index("c")
    return xi, yi, ci, 4 * xi + 2 * yi + ci


def _peer(xi, yi, ci, k):
    px = 1 - xi if (k >> 2) & 1 else xi
    py = 1 - yi if (k >> 1) & 1 else yi
    pc = 1 - ci if k & 1 else ci
    return (px, py, pc), 4 * px + 2 * py + pc


def _exchange(xs, gather, name):
    n = len(xs)

    def body(*refs):
        copies = _comm_copies(refs[:n], refs[n:2 * n], *refs[2 * n:], gather)
        for cp in copies:
            cp.start()
        for cp in copies:
            cp.wait()

    return pl.pallas_call(
        body, name=name, in_specs=[_HBM] * n, out_specs=tuple([_HBM] * n),
        out_shape=_comm_out_shapes(xs), scratch_shapes=_comm_sems(n))(*xs)


def _comm_out_shapes(xs):
    return tuple(jax.ShapeDtypeStruct((N_DEV,) + x.shape[-2:], x.dtype) for x in xs)


def _comm_sems(n):
    return [pltpu.SemaphoreType.DMA((n * (N_DEV - 1),)), pltpu.SemaphoreType.DMA((n * (N_DEV - 1),)),
            pltpu.SemaphoreType.DMA((n,))]


def _comm_copies(x_refs, o_refs, send_sems, recv_sems, lsems, gather):
    xi, yi, ci, me = _mesh_pos()
    copies = []
    for t in range(len(x_refs)):
        src_of = (lambda lin, t=t: x_refs[t]) if gather else (lambda lin, t=t: x_refs[t].at[lin])
        copies.append(pltpu.make_async_copy(src_of(me), o_refs[t].at[me], lsems.at[t]))
        for k in range(1, N_DEV):
            peer, lin = _peer(xi, yi, ci, k)
            s = t * (N_DEV - 1) + k - 1
            copies.append(pltpu.make_async_remote_copy(
                src_ref=src_of(lin), dst_ref=o_refs[t].at[me], send_sem=send_sems.at[s],
                recv_sem=recv_sems.at[s], device_id=peer, device_id_type=pl.DeviceIdType.MESH))
    return copies


def _call(body, args, *, name, grid, in_specs, out_specs, out_shape, scratch_shapes=(), sem, comm=None):
    if comm is None:
        return pl.pallas_call(body, name=name, grid=grid, in_specs=in_specs, out_specs=out_specs,
                              out_shape=out_shape, scratch_shapes=list(scratch_shapes),
                              compiler_params=_cparams(*sem))(*args)
    xs, gather = comm
    n = len(xs)
    single = not isinstance(out_shape, (tuple, list))
    outs_shape = (out_shape,) if single else tuple(out_shape)
    outs_specs = (out_specs,) if single else tuple(out_specs)
    n_in, n_out, n_scr = len(in_specs), len(outs_shape), len(scratch_shapes)

    def body2(*refs):
        ins, cx = refs[:n_in], refs[n_in:n_in + n]
        outs = refs[n_in + n:n_in + n + n_out]
        co = refs[n_in + n + n_out:n_in + 2 * n + n_out]
        scr = refs[n_in + 2 * n + n_out:n_in + 2 * n + n_out + n_scr]
        sems = refs[n_in + 2 * n + n_out + n_scr:]
        first = functools.reduce(jnp.logical_and, [pl.program_id(a) == 0 for a in range(len(grid))])
        last = functools.reduce(jnp.logical_and, [pl.program_id(a) == grid[a] - 1 for a in range(len(grid))])

        @pl.when(first)
        def _():
            for cp in _comm_copies(cx, co, *sems, gather):
                cp.start()

        body(*ins, *outs, *scr)

        @pl.when(last)
        def _():
            for cp in _comm_copies(cx, co, *sems, gather):
                cp.wait()

    res = pl.pallas_call(
        body2, name=name, grid=grid, in_specs=list(in_specs) + [_HBM] * n,
        out_specs=outs_specs + tuple([_HBM] * n), out_shape=outs_shape + _comm_out_shapes(xs),
        scratch_shapes=list(scratch_shapes) + _comm_sems(n),
        compiler_params=_cparams(*(["arbitrary"] * len(grid))))(*args, *xs)
    main = res[0] if single else tuple(res[:n_out])
    return main, list(res[n_out:])


def all_gather(x, name):
    return _exchange([x], True, name)[0]


def all_gather_many(xs, name):
    return _exchange(xs, True, name)


def all_to_all_many(xs, name):
    return _exchange(xs, False, name)


def reduce_adamw(gsrc, w, m, v, name, comm=None):
    parts = list(gsrc) if isinstance(gsrc, (list, tuple)) else [gsrc]
    S, R0, C = parts[0].shape
    R = R0 * len(parts)
    tr = _rtile(R0, max(16, min(256, (4 << 20) // (S * C * 4) // 16 * 16)), 16 if R0 % 16 == 0 else 8)
    n0 = R0 // tr
    c1 = 1.0 - ADAM_B1 ** ADAM_STEP
    c2 = 1.0 - ADAM_B2 ** ADAM_STEP

    def body(*refs):
        g_refs = refs[:len(parts)]
        w_ref, m_ref, v_ref, go_ref, d_ref, mo_ref, vo_ref = refs[len(parts):]
        for p, g_ref in enumerate(g_refs):
            @pl.when(pl.program_id(0) // n0 == p)
            def _(g_ref=g_ref):
                acc = g_ref[0].astype(F32)
                for s in range(1, S):
                    acc = acc + g_ref[s].astype(F32)
                go_ref[...] = acc
        g = go_ref[...]
        mn = ADAM_B1 * m_ref[...] + (1.0 - ADAM_B1) * g
        vn = ADAM_B2 * v_ref[...] + (1.0 - ADAM_B2) * (g * g)
        mo_ref[...] = mn
        vo_ref[...] = vn
        d_ref[...] = -ADAM_LR * ((mn / c1) / (jnp.sqrt(vn / c2) + ADAM_EPS) + ADAM_WD * w_ref[...])

    big = pl.BlockSpec((tr, C), lambda i: (i, 0))
    o = jax.ShapeDtypeStruct((R, C), F32)
    part_spec = lambda p: pl.BlockSpec((S, tr, C), lambda i: (0, jnp.clip(i - p * n0, 0, n0 - 1), 0))
    return _call(body, (*parts, w, m, v), name=name, grid=(R // tr,),
                 in_specs=[part_spec(p) for p in range(len(parts))] + [big, big, big],
                 out_specs=(big, big, big, big), out_shape=(o, o, o, o), sem=("parallel",), comm=comm)


def _to_slabs(g, ax):
    shp = g.shape
    g = g.reshape(shp[:ax] + (N_DEV, shp[ax] // N_DEV) + shp[ax + 1:])
    return jnp.moveaxis(g, ax, 0).reshape(N_DEV, -1)


def _from_slabs(s, ax, shp):
    s = s.reshape((N_DEV,) + shp[:ax] + (shp[ax] // N_DEV,) + shp[ax + 1:])
    return jnp.moveaxis(s, 0, ax).reshape(shp)


def _pack_rows(flat, width, row_mult):
    n = flat.shape[-1]
    per = width * row_mult
    tot = -(-n // per) * per
    flat = jnp.pad(flat, [(0, 0)] * (flat.ndim - 1) + [(0, tot - n)])
    return flat.reshape(flat.shape[:-1] + (tot // width, width))


def _offsets(sizes):
    offs, o = [], 0
    for s in sizes:
        offs.append(o)
        o += s
    return offs


WEIGHTS = ['ada_w', 'ada_b', 'norm_mix', 'norm_ffn', 'attn_w_in', 'attn_q_norm_a', 'attn_k_norm_a', 'attn_q_norm_b',
           'attn_k_norm_b', 'attn_sinks', 'attn_w_out', 'rec_w_in', 's5_lambda_re', 's5_lambda_im', 's5_log_dt',
           's5_b_re', 's5_b_im', 's5_c_re', 's5_c_im', 's5_d', 's5_glu_w', 's5_glu_b', 'dn_conv', 'dn_a_log',
           'dn_dt_bias', 'dn_out_norm', 'rec_w_out', 'ffn_w_up', 'ffn_conv', 'ffn_w_down']
BIG = [('attn_w_in', (D, ATTN_IN // N_DEV)), ('attn_w_out', (D // N_DEV, D)), ('rec_w_in', (D // N_DEV, REC_PAD)),
       ('s5_glu_w', (S5_W // N_DEV, S5_W)), ('rec_w_out', (D // N_DEV, D)), ('ffn_w_up', (2 * D, 2 * D_FF // N_DEV)),
       ('ffn_w_down', (2 * D_FF // N_DEV, D))]


def _shard2d(name, t):
    if name == 'rec_w_in':
        return jnp.pad(t[0], ((0, 0), (0, REC_PAD - REC_IN)))
    return t.reshape((-1, t.shape[-1]))


def _cols_to_slabs(g, k=N_DEV):
    r, n = g.shape
    return jnp.transpose(g.reshape(r, k, n // k), (1, 0, 2))


def _slabs_to_cols(s):
    k, r, c_ = s.shape
    return jnp.transpose(s, (1, 0, 2)).reshape(r, k * c_)
SMALL_SHARDED = [('s5_d', 1, (1, S5_W)), ('s5_glu_b', 1, (1, S5_W)), ('dn_conv', 2, (1, 4, 2304)),
                 ('ffn_conv', 2, (2, 3, 2 * D_FF))]
REPLICATED = [('ada_b', (2, 6 * D)), ('norm_mix', (2, D)), ('norm_ffn', (2, D)), ('attn_q_norm_a', (1, HD)),
              ('attn_k_norm_a', (1, HD)), ('attn_q_norm_b', (1, HD)), ('attn_k_norm_b', (1, HD)),
              ('attn_sinks', (1, 8)), ('s5_lambda_re', (1, 16, 64)), ('s5_lambda_im', (1, 16, 64)),
              ('s5_log_dt', (1, 16)), ('s5_b_re', (1, 16, 64, 16)), ('s5_b_im', (1, 16, 64, 16)),
              ('s5_c_re', (1, 16, 16, 64)), ('s5_c_im', (1, 16, 16, 64)), ('dn_a_log', (1, DN_H)),
              ('dn_dt_bias', (1, DN_H)), ('dn_out_norm', (1, 128))]


def _numel(shp):
    return int(np.prod(shp))


def kernel(x, c, ada_w, ada_b, norm_mix, norm_ffn, attn_w_in, attn_q_norm_a, attn_k_norm_a, attn_q_norm_b, attn_k_norm_b, attn_sinks, attn_w_out, rec_w_in, s5_lambda_re, s5_lambda_im, s5_log_dt, s5_b_re, s5_b_im, s5_c_re, s5_c_im, s5_d, s5_glu_w, s5_glu_b, dn_conv, dn_a_log, dn_dt_bias, dn_out_norm, rec_w_out, ffn_w_up, ffn_conv, ffn_w_down, loss_target, m_ada_w, m_ada_b, m_norm_mix, m_norm_ffn, m_attn_w_in, m_attn_q_norm_a, m_attn_k_norm_a, m_attn_q_norm_b, m_attn_k_norm_b, m_attn_sinks, m_attn_w_out, m_rec_w_in, m_s5_lambda_re, m_s5_lambda_im, m_s5_log_dt, m_s5_b_re, m_s5_b_im, m_s5_c_re, m_s5_c_im, m_s5_d, m_s5_glu_w, m_s5_glu_b, m_dn_conv, m_dn_a_log, m_dn_dt_bias, m_dn_out_norm, m_rec_w_out, m_ffn_w_up, m_ffn_conv, m_ffn_w_down, v_ada_w, v_ada_b, v_norm_mix, v_norm_ffn, v_attn_w_in, v_attn_q_norm_a, v_attn_k_norm_a, v_attn_q_norm_b, v_attn_k_norm_b, v_attn_sinks, v_attn_w_out, v_rec_w_in, v_s5_lambda_re, v_s5_lambda_im, v_s5_log_dt, v_s5_b_re, v_s5_b_im, v_s5_c_re, v_s5_c_im, v_s5_d, v_s5_glu_w, v_s5_glu_b, v_dn_conv, v_dn_a_log, v_dn_dt_bias, v_dn_out_norm, v_rec_w_out, v_ffn_w_up, v_ffn_conv, v_ffn_w_down):
    loc = locals()
    W = {n: loc[n] for n in WEIGHTS}
    M = {n: loc["m_" + n] for n in WEIGHTS}
    V = {n: loc["v_" + n] for n in WEIGHTS}
    _, _, _, me = _mesh_pos()
    L = x.shape[1]
    x0, tgt = x[0], loss_target[0]

    small_in = jnp.concatenate([c.reshape(-1)] + [W[n].reshape(-1) for n, _, _ in SMALL_SHARDED])
    si, att_in_all = all_gather_many([_pack_rows(small_in, 1024, 8), attn_w_in[0].astype(BF16)], "gather_first")
    si = si.reshape(N_DEV, -1)
    c_all = si[:, :D]
    off = D
    small_full = {}
    for n, ax, shp in SMALL_SHARDED:
        k = _numel(shp) // N_DEV
        small_full[n] = _from_slabs(si[:, off:off + k], ax, shp)
        off += k

    cond_all = jax.nn.silu(c_all)
    modp = jnp.concatenate([matmul([(cond_all, ada_w[l].astype(BF16))], "nn", f"ada{l}") for l in range(2)], axis=0)
    modp_all = all_gather(modp, "gather_mod")
    mods = []
    for l in range(2):
        row = lax.dynamic_index_in_dim(modp_all, l * N_DEV + me, axis=1, keepdims=False)
        mod = row.reshape(1, 6 * D) + ada_b[l].reshape(1, 6 * D)
        mods.append([mod[:, i * D:(i + 1) * D] for i in range(6)])

    w_att_in = _slabs_to_cols(att_in_all)
    bf = lambda t: t.astype(BF16)
    ffn_shards = [[bf(ffn_w_up[l]), bf(ffn_w_down[l])] for l in range(2)]
    rec_shards = [bf(_shard2d('rec_w_in', rec_w_in)), bf(s5_glu_w[0]), bf(rec_w_out[0])]
    ffn_cw = [small_full['ffn_conv'][l] for l in range(2)]
    dn_cw = small_full['dn_conv'][0]
    s5_dskip, glu_b = small_full['s5_d'], small_full['s5_glu_b']
    row = lambda t: t.reshape(1, -1)

    sh1, sc1, g1, sh2, sc2, g2 = mods[0]
    h1 = gate_norm_fwd(x0, None, None, row(norm_mix[0]), sh1, sc1, "l0_norm1")
    wvec, sinkvec = attn_vectors(attn_q_norm_a[0], attn_k_norm_a[0], attn_q_norm_b[0], attn_k_norm_b[0], attn_sinks[0])
    y0, res_att, got = attention_block_fwd(
        h1, w_att_in, wvec, sinkvec, None, "att",
        comms={'swa': ([ffn_shards[0][0][:D // 2]], True), 1: ([ffn_shards[0][0][D // 2:]], True),
               4: (ffn_shards[0][1:], True), 16: ([bf(attn_w_out[0])], True)})
    w_att_out = got['w_out']
    split_up = lambda up_all: (_slabs_to_cols(up_all[:4]), _slabs_to_cols(up_all[4:]))
    w_up = [split_up(jnp.concatenate([got['swa'][0], got[1][0]], axis=1))]
    w_down = [got[4][0].reshape(D_FF, D)]
    x1, h2 = gate_norm_fwd(x0, y0, g1, row(norm_ffn[0]), sh2, sc2, "l0_norm2")
    f0, res_f0, got_rec = ffn_block_fwd(h2, w_up[0][0], w_up[0][1], ffn_cw[0], w_down[0], "ffn0",
                                        comm=(rec_shards, True))
    w_rec_in = rec_cols_permute(got_rec[0].reshape(D, REC_PAD))
    glu_w, w_rec_out = got_rec[1].reshape(S5_W, S5_W), got_rec[2].reshape(D, D)
    w_rec_out = jnp.concatenate([w_rec_out[S5_W:], w_rec_out[:S5_W]], axis=0)
    t1, tc1, tg1, t2, tc2, tg2 = mods[1]
    x2, h3 = gate_norm_fwd(x1, f0, g2, row(norm_mix[1]), t1, tc1, "l1_norm1")
    rin = matmul([(h3, w_rec_in)], "nn", "rec_in")
    s5p, s5p_vjp = jax.vjp(s5_params, s5_lambda_re[0], s5_lambda_im[0], s5_log_dt[0], s5_b_re[0], s5_b_im[0],
                           s5_c_re[0], s5_c_im[0])
    u = rin[:, REC_U0:REC_A0]
    yc, res_s5 = s5_block_fwd(u, s5p, s5_dskip, glu_w, glu_b, "s5")
    yd, res_dn, got_ffn1 = dn_block_fwd(rin, dn_cw, dn_a_log[0], dn_dt_bias[0], dn_out_norm[0], "dn",
                                        comm=(ffn_shards[1], True))
    w_up.append(split_up(got_ffn1[0]))
    w_down.append(got_ffn1[1].reshape(D_FF, D))
    ycat = jnp.concatenate([yd, yc], axis=1)
    y1 = matmul([(ycat, w_rec_out)], "nn", "rec_out")
    x3, h4 = gate_norm_fwd(x2, y1, tg1, row(norm_ffn[1]), t2, tc2, "l1_norm2")
    f1, res_f1, _ = ffn_block_fwd(h4, w_up[1][0], w_up[1][1], ffn_cw[1], w_down[1], "ffn1")
    dx4, df1, lsum = final_loss(x3, f1, tg2, tgt, "loss")

    G = {}
    d_tg2 = lsum[8:16].sum(axis=0)
    dh4, gf1, _ = ffn_block_bwd(df1, res_f1, w_up[1][0], w_up[1][1], ffn_cw[1], w_down[1], "ffn1")
    ffn_slabs = lambda g: [g['w_up'], g['w_down'].reshape(N_DEV, D_FF // N_DEV, D)]
    dx3, dy1, s = gate_norm_bwd(x3, y1, tg1, row(norm_ffn[1]), tc2, dx4, dh4, "l1_dnorm2")
    s = s.reshape(4, 8, D).sum(axis=1)
    d_tg1, d_nffn1, d_t2, d_tc2 = s[0], s[1] * (1.0 + tc2[0]), s[2], s[1] * norm_ffn[1]
    g_rec_out = matmul([(ycat, dy1)], "tn", "rec_out_dw", out_dtype=BF16)
    g_rec_out = jnp.concatenate([g_rec_out[DN_W:], g_rec_out[:DN_W]], axis=0).reshape(N_DEV, D // N_DEV, D)
    dycat = matmul([(dy1, w_rec_out)], "nt", "rec_out_dx")
    du, s5cot, gs5 = s5_block_bwd(dycat, res_s5, s5p, s5_dskip, glu_w, glu_b, "s5", dout_col=DN_W // S5_W)
    s5g = s5p_vjp(s5cot)
    dqkv, dz, da, dbraw, gdn, recv_ffn1 = dn_block_bwd(dycat, res_dn, rin, dn_cw, dn_out_norm[0], "dn",
                                                       comm=(ffn_slabs(gf1), False))
    d_rest = jnp.concatenate([du.astype(BF16), da.astype(BF16), dbraw.astype(BF16),
                              jnp.zeros((L, REC_PAD - REC_IN), BF16)], axis=1)
    drin = ((dqkv, 0), (dz, 3 * DN_W), (d_rest, REC_U0))
    g_rec_in = jnp.concatenate([matmul([(h3, p)], "tn", f"rec_in_dw{i}", out_dtype=BF16)
                                for i, (p, _) in enumerate(drin)], axis=1)
    g_rec_in = rec_cols_restore(g_rec_in).reshape(N_DEV, D // N_DEV, REC_PAD)
    g_glu = gs5['glu_w'].astype(BF16).reshape(N_DEV, S5_W // N_DEV, S5_W)
    dh3 = matmul([(p, w_rec_in[:, c0:c0 + p.shape[1]]) for p, c0 in drin], "nt", "rec_in_dx")
    dx2, df0, s = gate_norm_bwd(x2, f0, g2, row(norm_mix[1]), tc1, dx3, dh3, "l1_dnorm1")
    s = s.reshape(4, 8, D).sum(axis=1)
    d_g2, d_nmix1, d_t1, d_tc1 = s[0], s[1] * (1.0 + tc1[0]), s[2], s[1] * norm_mix[1]
    dh2, gf0, recv_rec = ffn_block_bwd(df0, res_f0, w_up[0][0], w_up[0][1], ffn_cw[0], w_down[0], "ffn0",
                                       comm=([g_rec_in, g_glu, g_rec_out], False))
    dx1, dy0, s = gate_norm_bwd(x1, y0, g1, row(norm_ffn[0]), sc2, dx2, dh2, "l0_dnorm2")
    s = s.reshape(4, 8, D).sum(axis=1)
    d_g1, d_nffn0, d_sh2, d_sc2 = s[0], s[1] * (1.0 + sc2[0]), s[2], s[1] * norm_ffn[0]
    dh1, gatt, got_b = attention_block_bwd(dy0, res_att, w_att_in, wvec, sinkvec, w_att_out, "att",
                                           comms={'swa': ([gf0['w_up'][:, :D // 2]], False),
                                                  16: ([gf0['w_up'][:, D // 2:]], False),
                                                  1: (ffn_slabs(gf0)[1:], False)},
                                           send_w_out_on=4)
    recv_ffn0 = [jnp.concatenate([got_b['swa'][0], got_b[16][0]], axis=1), got_b[1][0]]
    (grad_x, s), recv_w_in = gate_norm_bwd(x0, None, None, row(norm_mix[0]), sc1, dx1, dh1, "l0_dnorm1",
                                           comm=([_cols_to_slabs(gatt['w_in'])], False))
    recv_att = [recv_w_in[0], got_b[4][0]]
    s = s.reshape(4, 8, D).sum(axis=1)
    d_nmix0, d_sh1, d_sc1 = s[1] * (1.0 + sc1[0]), s[2], s[1] * norm_mix[0]
    dmod = jnp.stack([jnp.concatenate([d_sh1, d_sc1, d_g1, d_sh2, d_sc2, d_g2]),
                      jnp.concatenate([d_t1, d_tc1, d_tg1, d_t2, d_tc2, d_tg2])])

    P = {'ada_b': dmod, 'norm_mix': jnp.stack([d_nmix0, d_nmix1]), 'norm_ffn': jnp.stack([d_nffn0, d_nffn1]),
         'attn_q_norm_a': gatt['q_norm_a'], 'attn_k_norm_a': gatt['k_norm_a'], 'attn_q_norm_b': gatt['q_norm_b'],
         'attn_k_norm_b': gatt['k_norm_b'], 'attn_sinks': gatt['sinks'],
         's5_lambda_re': s5g[0], 's5_lambda_im': s5g[1], 's5_log_dt': s5g[2], 's5_b_re': s5g[3], 's5_b_im': s5g[4],
         's5_c_re': s5g[5], 's5_c_im': s5g[6], 'dn_a_log': gdn['a_log'], 'dn_dt_bias': gdn['dt_bias'],
         'dn_out_norm': gdn['out_norm'],
         's5_d': gs5['dskip'], 's5_glu_b': gs5['glu_b'], 'dn_conv': gdn['conv'],
         'ffn_conv': jnp.stack([gf0['conv'], gf1['conv']])}

    out = {k: {} for k in ("g", "d", "m", "v")}
    keys = ("g", "d", "m", "v")
    recv = {'attn_w_in': recv_att[0], 'attn_w_out': recv_att[1], 'rec_w_in': recv_rec[0], 's5_glu_w': recv_rec[1],
            'rec_w_out': recv_rec[2]}
    for n, gr_ in recv.items():
        res4 = reduce_adamw(gr_, _shard2d(n, W[n]), _shard2d(n, M[n]), _shard2d(n, V[n]), "adamw_" + n)
        for key, t in zip(keys, res4):
            out[key][n] = (t[:, :REC_IN] if n == 'rec_w_in' else t).reshape(W[n].shape)
    rep_sizes = [_numel(shp) for _, shp in REPLICATED]
    ss_sizes = [_numel(shp) for _, _, shp in SMALL_SHARDED]
    rep_offs = _offsets(rep_sizes + ss_sizes + [1])
    parts = [P[n].reshape(-1) for n, _ in REPLICATED] + [P[n].reshape(-1) for n, _, _ in SMALL_SHARDED]
    parts.append(lsum[0:8].sum().reshape(1))
    spack = _pack_rows(jnp.concatenate(parts), 1024, 8)
    flat2d = lambda t: t.reshape(-1, t.shape[-1])
    sall = None
    for n, idx in (('ffn_w_up', 0), ('ffn_w_down', 1)):
        comm = ([spack], True) if sall is None else None
        res4, got_s = _with_comm(reduce_adamw([recv_ffn0[idx], recv_ffn1[idx]], flat2d(W[n]), flat2d(M[n]),
                                              flat2d(V[n]), "adamw_" + n, comm=comm), comm)
        if got_s is not None:
            sall = got_s[0]
        for key, t in zip(keys, res4):
            out[key][n] = t.reshape(W[n].shape)
    n_rest = sum(ss_sizes) + 1
    pk = lambda d: _pack_rows(jnp.concatenate([d[n].reshape(-1) for n, _ in REPLICATED]
                                              + [jnp.zeros((n_rest,), F32)]), 1024, 8)
    sg, sd_, sm, sv = [t.reshape(-1) for t in reduce_adamw(sall, pk(W), pk(M), pk(V), "adamw_small")]
    loss = 0.5 * sg[rep_offs[-1]] / D

    dmod_all = sall.reshape(N_DEV, -1)[:, :2 * 6 * D].reshape(N_DEV, 2, 6 * D)
    dmod_mine = lax.dynamic_slice_in_dim(dmod_all, me * (6 * D // N_DEV), 6 * D // N_DEV, axis=2)
    g_ada = [matmul([(cond_all, dmod_mine[:, l])], "tn", f"ada{l}_dw")[None] for l in range(2)]
    ada2d = lambda t: t.reshape(2 * D, 6 * D // N_DEV)
    for key, t in zip(("g", "d", "m", "v"), reduce_adamw(g_ada, ada2d(ada_w), ada2d(m_ada_w),
                                                          ada2d(v_ada_w), "adamw_ada_w")):
        out[key]['ada_w'] = t.reshape(ada_w.shape)
    own = []
    for (n, ax, shp), o in zip(SMALL_SHARDED, rep_offs[len(REPLICATED):]):
        slabs = _to_slabs(sg[o:o + _numel(shp)].reshape(shp), ax)
        own.append(lax.dynamic_index_in_dim(slabs, me, axis=0, keepdims=False))
    own_names = [n for n, _, _ in SMALL_SHARDED]
    pk = lambda d: _pack_rows(jnp.concatenate([d[n].reshape(-1) for n in own_names]), 1024, 8)
    og, od, om, ov = [t.reshape(-1) for t in reduce_adamw(_pack_rows(jnp.concatenate(own), 1024, 8)[None],
                                                          pk(W), pk(M), pk(V), "adamw_own")]

    def unpack(names_shapes, bufs):
        o = 0
        for n, shp in names_shapes:
            k = _numel(shp)
            for key, buf in zip(("g", "d", "m", "v"), bufs):
                out[key][n] = buf[o:o + k].reshape(shp)
            o += k

    unpack(REPLICATED, (sg, sd_, sm, sv))
    unpack([(n, W[n].shape) for n in own_names], (og, od, om, ov))
    return (loss, grad_x[None], *[out["g"][n] for n in WEIGHTS], *[out["d"][n] for n in WEIGHTS],
            *[out["m"][n] for n in WEIGHTS], *[out["v"][n] for n in WEIGHTS])
```

```python
import functools
import math

import numpy as np
import jax
import jax.numpy as jnp
from jax import lax
from jax.experimental import pallas as pl
from jax.experimental.pallas import tpu as pltpu

F32 = jnp.float32
BF16 = jnp.bfloat16

N_DEV = 8
D = 1024
HD = 64
BLK = 128
ATTN_IN = 2304
CB = ATTN_IN // 128
B_BRANCHES = ((128, 1), (512, 4), (2048, 16))
S5_W = 256
S5_P = 1024
DN_H = 6
DN_DK = 128
DN_C = 64
REC_IN = 3340
REC_PAD = 3456
D_FF = 2816
EPS = 1e-6
ADAM_LR, ADAM_B1, ADAM_B2, ADAM_EPS, ADAM_WD, ADAM_STEP = 0.001, 0.9, 0.999, 1e-8, 0.01, 10
VMEM_LIMIT = 48 * 1024 * 1024

ALIBI = np.asarray(2.0 ** (-8.0 * np.arange(1, 17) / 16), dtype=np.float32)


def _cparams(*sem):
    return pltpu.CompilerParams(dimension_semantics=tuple(sem), vmem_limit_bytes=VMEM_LIMIT)


def _tile(n, target):
    if n <= target:
        return n
    best = None
    for t in range(128, target + 1, 128):
        if n % t == 0:
            best = t
    assert best is not None, (n, target)
    return best


def _rtile(n, target, mult=8):
    if n <= target:
        return n
    best = None
    for t in range(mult, target + 1, mult):
        if n % t == 0:
            best = t
    assert best is not None, (n, target)
    return best


def _fold8(x):
    r, c = x.shape
    return x.reshape(r // 8, 8, c).sum(axis=0)


def _sigmoid(x):
    return 1.0 / (1.0 + jnp.exp(-x))


_DIMS = {"nn": (((1,), (0,)), ((), ())), "nt": (((1,), (1,)), ((), ())), "tn": (((0,), (0,)), ((), ()))}


MM_FULL_K = 3584


MM_VMEM_BUDGET = 40 << 20


def matmul(pairs, mode, name, out_dtype=F32, tm=1024, tn=1536, tk=1024):
    a0, b0 = pairs[0]
    if mode == "nn":
        (M, K), N = a0.shape, b0.shape[1]
    elif mode == "nt":
        (M, K), N = a0.shape, b0.shape[0]
    else:
        (K, M), N = a0.shape, b0.shape[1]
        tm = 1536
    tn = _tile(N, tn)
    tk = K if K <= MM_FULL_K else _tile(K, tk)
    nk = K // tk
    npair = len(pairs)
    dims = _DIMS[mode]
    kdim = 0 if mode == "tn" else 1
    tks = [a.shape[kdim] for a, _ in pairs]
    assert all(t == K for t in tks) or (nk == 1 and max(tks) <= MM_FULL_K), tks
    if nk > 1:
        tks = [tk] * npair

    def planned(tm_):
        ab = sum(tm_ * t * a.dtype.itemsize + t * tn * b.dtype.itemsize for (a, b), t in zip(pairs, tks))
        return 2 * ab + 2 * tm_ * tn * jnp.dtype(out_dtype).itemsize + (tm_ * tn * 4 if nk > 1 else 0)

    while True:
        tm_try = _rtile(M, tm) if M % 128 else _tile(M, tm)
        if planned(tm_try) <= MM_VMEM_BUDGET or tm <= 128:
            break
        tm //= 2
    tm = tm_try

    def body(*refs):
        o_ref = refs[2 * npair]
        tot = None
        for p in range(npair):
            part = lax.dot_general(refs[2 * p][...].astype(BF16), refs[2 * p + 1][...].astype(BF16),
                                   dims, preferred_element_type=F32)
            tot = part if tot is None else tot + part
        if nk == 1:
            o_ref[...] = tot.astype(o_ref.dtype)
            return
        acc_ref = refs[2 * npair + 1]
        k = pl.program_id(2)

        @pl.when(k == 0)
        def _():
            acc_ref[...] = tot

        @pl.when(k > 0)
        def _():
            acc_ref[...] += tot

        @pl.when(k == nk - 1)
        def _():
            o_ref[...] = acc_ref[...].astype(o_ref.dtype)

    def specs(t):
        if mode == "nn":
            return [pl.BlockSpec((tm, t), lambda j, i, k: (i, k)), pl.BlockSpec((t, tn), lambda j, i, k: (k, j))]
        if mode == "nt":
            return [pl.BlockSpec((tm, t), lambda j, i, k: (i, k)), pl.BlockSpec((tn, t), lambda j, i, k: (j, k))]
        return [pl.BlockSpec((t, tm), lambda j, i, k: (k, i)), pl.BlockSpec((t, tn), lambda j, i, k: (k, j))]

    flat = [t for pr in pairs for t in pr]
    return pl.pallas_call(
        body, name=name, grid=(N // tn, M // tm, nk),
        in_specs=[s for t in tks for s in specs(t)],
        out_specs=pl.BlockSpec((tm, tn), lambda j, i, k: (i, j)),
        out_shape=jax.ShapeDtypeStruct((M, N), out_dtype),
        scratch_shapes=[pltpu.VMEM((tm, tn), F32)] if nk > 1 else [],
        compiler_params=_cparams("parallel", "parallel", "arbitrary"),
    )(*flat)


def gate_norm_fwd(x, y, gate, nw, sh, sc, name):
    L, C = x.shape
    tl = _rtile(L, 512)
    has_gate = y is not None

    def body(*refs):
        if has_gate:
            x_ref, y_ref, g_ref, nw_ref, sh_ref, sc_ref, xn_ref, h_ref = refs
            xn = x_ref[...] + g_ref[...] * y_ref[...]
            xn_ref[...] = xn
        else:
            x_ref, nw_ref, sh_ref, sc_ref, h_ref = refs
            xn = x_ref[...]
        r = lax.rsqrt(jnp.mean(xn * xn, axis=-1, keepdims=True) + EPS)
        h = (xn * r * nw_ref[...]) * (1.0 + sc_ref[...]) + sh_ref[...]
        h_ref[...] = h.astype(BF16)

    big = pl.BlockSpec((tl, C), lambda i: (i, 0))
    vec = pl.BlockSpec((1, C), lambda i: (0, 0))
    if has_gate:
        ins, in_specs = (x, y, gate, nw, sh, sc), [big, big, vec, vec, vec, vec]
        out_shape = (jax.ShapeDtypeStruct((L, C), F32), jax.ShapeDtypeStruct((L, C), BF16))
        out_specs = (big, big)
    else:
        ins, in_specs = (x, nw, sh, sc), [big, vec, vec, vec]
        out_shape = jax.ShapeDtypeStruct((L, C), BF16)
        out_specs = big
    return pl.pallas_call(body, name=name, grid=(L // tl,), in_specs=in_specs, out_specs=out_specs,
                          out_shape=out_shape, compiler_params=_cparams("parallel"))(*ins)


def gate_norm_bwd(xn, y, gate, nw, sc, dxn_direct, dh, name, comm=None):
    L, C = xn.shape
    tl = _rtile(L, 256)
    has_gate = y is not None
    has_direct = dxn_direct is not None

    def body(*refs):
        refs = list(refs)
        xn_ref = refs.pop(0)
        y_ref = refs.pop(0) if has_gate else None
        g_ref = refs.pop(0) if has_gate else None
        nw_ref = refs.pop(0)
        sc_ref = refs.pop(0)
        dd_ref = refs.pop(0) if has_direct else None
        dh_ref = refs.pop(0)
        dxn_ref = refs.pop(0)
        dy_ref = refs.pop(0) if has_gate else None
        sums_ref = refs.pop(0)

        @pl.when(pl.program_id(0) == 0)
        def _():
            sums_ref[...] = jnp.zeros_like(sums_ref)

        xv = xn_ref[...]
        dh_v = dh_ref[...]
        r = lax.rsqrt(jnp.mean(xv * xv, axis=-1, keepdims=True) + EPS)
        n = xv * r
        a = nw_ref[...] * (1.0 + sc_ref[...])
        dn = dh_v * a
        dx = r * (dn - n * jnp.mean(dn * n, axis=-1, keepdims=True))
        if has_direct:
            dx = dx + dd_ref[...]
        dxn_ref[...] = dx
        sums_ref[8:16, :] += _fold8(dh_v * n)
        sums_ref[16:24, :] += _fold8(dh_v)
        if has_gate:
            dy_ref[...] = (dx * g_ref[...]).astype(BF16)
            sums_ref[0:8, :] += _fold8(dx * y_ref[...])

    big = pl.BlockSpec((tl, C), lambda i: (i, 0))
    vec = pl.BlockSpec((1, C), lambda i: (0, 0))
    ins, in_specs = [xn], [big]
    if has_gate:
        ins += [y, gate]
        in_specs += [big, vec]
    ins += [nw, sc]
    in_specs += [vec, vec]
    if has_direct:
        ins.append(dxn_direct)
        in_specs.append(big)
    ins.append(dh)
    in_specs.append(big)
    out_shape = [jax.ShapeDtypeStruct((L, C), F32)]
    out_specs = [big]
    if has_gate:
        out_shape.append(jax.ShapeDtypeStruct((L, C), BF16))
        out_specs.append(big)
    out_shape.append(jax.ShapeDtypeStruct((32, C), F32))
    out_specs.append(pl.BlockSpec((32, C), lambda i: (0, 0)))
    return _call(body, ins, name=name, grid=(L // tl,), in_specs=in_specs, out_specs=tuple(out_specs),
                 out_shape=tuple(out_shape), sem=("arbitrary",), comm=comm)


def final_loss(x, f, gate, target, name):
    L, C = x.shape
    tl = _rtile(L, 256)

    def body(x_ref, f_ref, g_ref, t_ref, dy_ref, df_ref, sums_ref):
        @pl.when(pl.program_id(0) == 0)
        def _():
            sums_ref[...] = jnp.zeros_like(sums_ref)

        fv = f_ref[...]
        err = x_ref[...] + g_ref[...] * fv - t_ref[...]
        dy = err * (1.0 / C)
        dy_ref[...] = dy
        df_ref[...] = (dy * g_ref[...]).astype(BF16)
        sums_ref[0:8, :] += _fold8(err * err)
        sums_ref[8:16, :] += _fold8(dy * fv)

    big = pl.BlockSpec((tl, C), lambda i: (i, 0))
    vec = pl.BlockSpec((1, C), lambda i: (0, 0))
    return pl.pallas_call(
        body, name=name, grid=(L // tl,), in_specs=[big, big, vec, big],
        out_specs=(big, big, pl.BlockSpec((16, C), lambda i: (0, 0))),
        out_shape=(jax.ShapeDtypeStruct((L, C), F32), jax.ShapeDtypeStruct((L, C), BF16),
                   jax.ShapeDtypeStruct((16, C), F32)),
        compiler_params=_cparams("arbitrary"))(x, f, gate, target)


def _seg_ones(seg):
    r = lax.broadcasted_iota(jnp.int32, (128, 128), 0) // seg
    c = lax.broadcasted_iota(jnp.int32, (128, 128), 1) // seg
    return (r == c).astype(BF16)


def _segsum(t, ones):
    hi = t.astype(BF16)
    lo = (t - hi.astype(F32)).astype(BF16)
    return (jnp.dot(hi, ones, preferred_element_type=F32) + jnp.dot(lo, ones, preferred_element_type=F32))


_NORMED_TILES = tuple(list(range(0, 5)) + list(range(6, 14)))


DIL = (4, 16)
B_COLS0, B_W = 768, 1536
DIL_TL = 256


def _to_dilated(scr_ref, out_ref, d, cast=None):
    nj, tl, _ = scr_ref.shape
    for r in range(d):
        for j in range(nj):
            piece = scr_ref[j, pl.ds(r, tl // d, stride=d), :]
            c0 = (r * nj + j) * 128
            out_ref[:, c0:c0 + 128] = piece if cast is None else piece.astype(cast)


def _from_dilated(in_ref, scr_ref, d):
    nj, tl, _ = scr_ref.shape
    for r in range(d):
        for j in range(nj):
            c0 = (r * nj + j) * 128
            scr_ref[j, pl.ds(r, tl // d, stride=d), :] = in_ref[:, c0:c0 + 128]


def _dil_spec(tl, d, width):
    return pl.BlockSpec((tl // d, d * width), lambda i: (i, 0))


def qknorm_fwd(qkv, wvec, name):
    L, C = qkv.shape
    tl = DIL_TL

    def body(x_ref, w_ref, o_ref, o4_ref, o16_ref, scr_ref):
        ones = _seg_ones(HD)
        for t in range(CB):
            cs = slice(t * 128, (t + 1) * 128)
            x = x_ref[:, cs]
            if t in _NORMED_TILES:
                ms = _segsum(x * x, ones) * (1.0 / HD)
                x = x * lax.rsqrt(ms + EPS) * w_ref[:, cs]
            o_ref[:, cs] = x.astype(BF16)
            if t * 128 >= B_COLS0:
                scr_ref[t - B_COLS0 // 128] = x
        _to_dilated(scr_ref, o4_ref, 4, BF16)
        _to_dilated(scr_ref, o16_ref, 16, BF16)

    return pl.pallas_call(
        body, name=name, grid=(L // tl,),
        in_specs=[pl.BlockSpec((tl, C), lambda i: (i, 0)), pl.BlockSpec((1, C), lambda i: (0, 0))],
        out_specs=(pl.BlockSpec((tl, C), lambda i: (i, 0)), _dil_spec(tl, 4, B_W), _dil_spec(tl, 16, B_W)),
        out_shape=(jax.ShapeDtypeStruct((L, C), BF16), jax.ShapeDtypeStruct((L // 4, 4 * B_W), BF16),
                   jax.ShapeDtypeStruct((L // 16, 16 * B_W), BF16)),
        scratch_shapes=[pltpu.VMEM((B_W // 128, tl, 128), F32)], compiler_params=_cparams("parallel"))(qkv, wvec)


def qknorm_bwd(qkv, wvec, d_a, d_b, name):
    L, C = qkv.shape
    tl = DIL_TL

    def body(x_ref, w_ref, dqa, dka, dva, q1, k1, v1, q4, k4, v4, q16, k16, v16, dx_ref, sums_ref,
             dy_ref, s4_ref, s16_ref):
        @pl.when(pl.program_id(0) == 0)
        def _():
            sums_ref[...] = jnp.zeros_like(sums_ref)

        dy_ref[:, 0:512] = dqa[...]
        for off, ref in ((512, dka), (640, dva)):
            for g in range(2):
                acc = ref[:, g * 256:g * 256 + HD]
                for h in range(1, 4):
                    acc = acc + ref[:, g * 256 + h * HD:g * 256 + (h + 1) * HD]
                dy_ref[:, off + g * HD:off + (g + 1) * HD] = acc
        for off, r1, r4, r16 in ((768, q1, q4, q16), (1280, k1, k4, k16), (1792, v1, v4, v16)):
            _from_dilated(r4, s4_ref, 4)
            _from_dilated(r16, s16_ref, 16)
            for j in range(4):
                dy_ref[:, off + j * 128:off + (j + 1) * 128] = r1[:, j * 128:(j + 1) * 128] + s4_ref[j] + s16_ref[j]

        ones = _seg_ones(HD)
        for t in range(CB):
            cs = slice(t * 128, (t + 1) * 128)
            d = dy_ref[:, cs]
            if t in _NORMED_TILES:
                x = x_ref[:, cs]
                r = lax.rsqrt(_segsum(x * x, ones) * (1.0 / HD) + EPS)
                n = x * r
                dn = d * w_ref[:, cs]
                dx_ref[:, cs] = (r * (dn - n * (_segsum(dn * n, ones) * (1.0 / HD)))).astype(BF16)
                sums_ref[:, cs] += _fold8(d * n)
            else:
                dx_ref[:, cs] = d.astype(BF16)

    big = pl.BlockSpec((tl, C), lambda i: (i, 0))
    p512 = pl.BlockSpec((tl, 512), lambda i: (i, 0))
    return pl.pallas_call(
        body, name=name, grid=(L // tl,),
        in_specs=[big, pl.BlockSpec((1, C), lambda i: (0, 0))] + [p512] * 6 + [_dil_spec(tl, 4, 512)] * 3
        + [_dil_spec(tl, 16, 512)] * 3,
        out_specs=(big, pl.BlockSpec((8, C), lambda i: (0, 0))),
        out_shape=(jax.ShapeDtypeStruct((L, C), BF16), jax.ShapeDtypeStruct((8, C), F32)),
        scratch_shapes=[pltpu.VMEM((tl, C), F32), pltpu.VMEM((4, tl, 128), F32), pltpu.VMEM((4, tl, 128), F32)],
        compiler_params=_cparams("arbitrary"))(qkv, wvec, *d_a, *d_b[0], *d_b[1], *d_b[2])


def _attn_biases(t, slopes, step, maxdist):
    qi = lax.broadcasted_iota(jnp.int32, (BLK, 2 * BLK), 0)
    sj = lax.broadcasted_iota(jnp.int32, (BLK, 2 * BLK), 1)
    dist = BLK + qi - sj
    valid = (dist >= 0) & (dist <= maxdist)
    distf = (step * dist).astype(F32)
    inner = [jnp.where(valid, (-sl) * distf, -jnp.inf) for sl in slopes]
    first = [jnp.where((t > 0) | (sj >= BLK), b, -jnp.inf) for b in inner]
    return inner, first


def _attn_scores(q, kw, bias):
    return lax.dot_general(q, kw, (((1,), (1,)), ((), ())), preferred_element_type=F32) + bias


ATT_NQ = 8


def _attn_operands(nq, hp, gqa, q_ref, kh_ref, kc_ref, vh_ref, vc_ref):
    ops = []
    for b in range(nq):
        rows = slice(b * BLK, (b + 1) * BLK)
        prev = slice((b - 1) * BLK, b * BLK)
        for e in range(2):
            cs = slice(e * HD, (e + 1) * HD)
            if gqa:
                ksel = lambda ref, r: jnp.where(hp >= 2, ref[r, 64:128], ref[r, 0:64])
            else:
                ksel = lambda ref, r, cs=cs: ref[r, cs]
            kprev = ksel(kh_ref, slice(0, BLK)) if b == 0 else ksel(kc_ref, prev)
            vprev = ksel(vh_ref, slice(0, BLK)) if b == 0 else ksel(vc_ref, prev)
            ops.append((b, e, rows, cs, q_ref[rows, cs] * (HD ** -0.5),
                        jnp.concatenate([kprev, ksel(kc_ref, rows)], axis=0),
                        jnp.concatenate([vprev, ksel(vc_ref, rows)], axis=0)))
    return ops


def _attn_specs(cb, q_off, k_off, v_off, gqa):
    kcol = (lambda r, hp: r * cb + k_off) if gqa else (lambda r, hp: r * cb + k_off + hp)
    vcol = (lambda r, hp: r * cb + v_off) if gqa else (lambda r, hp: r * cb + v_off + hp)
    return kcol, vcol


def attn_fwd(X, d, q_off, k_off, v_off, gqa, slope0, maxdist, name, comm=None):
    Ls = X.shape[0]
    nq = min(ATT_NQ, Ls // BLK)
    TQ = nq * BLK
    nt = Ls // TQ
    slopes = jnp.asarray(ALIBI)

    def body(sl_ref, q_ref, kh_ref, kc_ref, vh_ref, vc_ref, o_ref, lse_ref):
        hp, t = pl.program_id(1), pl.program_id(2)
        ops = _attn_operands(nq, hp, gqa, q_ref, kh_ref, kc_ref, vh_ref, vc_ref)
        inner, first = _attn_biases(t, [sl_ref[slope0 + 2 * hp + e] for e in range(2)], d, maxdist)
        s = [_attn_scores(q, kw, first[e] if b == 0 else inner[e]) for (b, e, rows, cs, q, kw, vw) in ops]
        m = [jnp.max(x, axis=-1, keepdims=True) for x in s]
        p = [jnp.exp(x - mm) for x, mm in zip(s, m)]
        l = [jnp.sum(x, axis=-1, keepdims=True) for x in p]
        o = [jnp.dot(x.astype(BF16), op[6], preferred_element_type=F32) / ll for x, op, ll in zip(p, ops, l)]
        for (b, e, rows, cs, q, kw, vw), oo, mm, ll in zip(ops, o, m, l):
            o_ref[rows, cs] = oo
            lse_ref[rows, cs] = jnp.broadcast_to(mm + jnp.log(ll), (BLK, HD))

    cb = X.shape[1] // (d * 128)
    kcol, vcol = _attn_specs(cb, q_off, k_off, v_off, gqa)
    tile, blk = (TQ, 128), (BLK, 128)
    halo = lambda t: jnp.maximum(t * nq - 1, 0)
    in_specs = [
        pl.BlockSpec(memory_space=pltpu.SMEM),
        pl.BlockSpec(tile, lambda r, hp, t: (t, r * cb + q_off + hp)),
        pl.BlockSpec(blk, lambda r, hp, t: (halo(t), kcol(r, hp))),
        pl.BlockSpec(tile, lambda r, hp, t: (t, kcol(r, hp))),
        pl.BlockSpec(blk, lambda r, hp, t: (halo(t), vcol(r, hp))),
        pl.BlockSpec(tile, lambda r, hp, t: (t, vcol(r, hp))),
    ]
    out_spec = pl.BlockSpec(tile, lambda r, hp, t: (t, r * 4 + hp))
    out = jax.ShapeDtypeStruct((Ls, d * 512), F32)
    return _call(body, (slopes, X, X, X, X, X), name=name, grid=(d, 4, nt), in_specs=in_specs,
                 out_specs=(out_spec, out_spec), out_shape=(out, out),
                 sem=("parallel", "parallel", "arbitrary"), comm=comm)


def attn_bwd(X, o, lse, do, dlse, d, q_off, k_off, v_off, gqa, slope0, maxdist, name, comm=None):
    Ls = X.shape[0]
    slopes = jnp.asarray(ALIBI)

    nq = min(ATT_NQ, Ls // BLK)
    TQ = nq * BLK
    nt = Ls // TQ
    nt_dims, tn_dims = (((1,), (1,)), ((), ())), (((0,), (0,)), ((), ()))

    def body(sl_ref, q_ref, kh_ref, kc_ref, vh_ref, vc_ref, o_ref, lse_ref, do_ref, dlse_ref,
             dq_ref, dk_ref, dv_ref, ak_ref, av_ref, pk_ref, pv_ref):
        hp, t = pl.program_id(1), pl.program_id(2)

        @pl.when(t == 0)
        def _():
            pk_ref[...] = jnp.zeros_like(pk_ref)
            pv_ref[...] = jnp.zeros_like(pv_ref)

        @pl.when(t < nt)
        def _():
            ops = _attn_operands(nq, hp, gqa, q_ref, kh_ref, kc_ref, vh_ref, vc_ref)
            inner, first = _attn_biases(t, [sl_ref[slope0 + 2 * hp + e] for e in range(2)], d, maxdist)
            sv = [_attn_scores(q, kw, first[e] if b == 0 else inner[e]) for (b, e, rows, cs, q, kw, vw) in ops]
            p = [jnp.exp(s - lse_ref[op[2], op[1] * HD:op[1] * HD + 1]) for s, op in zip(sv, ops)]
            dov = [do_ref[op[2], op[3]] for op in ops]
            delta = [jnp.sum(dd * o_ref[op[2], op[3]], axis=-1, keepdims=True) for dd, op in zip(dov, ops)]
            dob = [dd.astype(BF16) for dd in dov]
            dp = [lax.dot_general(dd, op[6], nt_dims, preferred_element_type=F32) for dd, op in zip(dob, ops)]
            ds = [(pp * (x - dl + dlse_ref[op[2], op[1] * HD:op[1] * HD + 1])).astype(BF16)
                  for pp, x, dl, op in zip(p, dp, delta, ops)]
            dq = [jnp.dot(x, op[5], preferred_element_type=F32) * (HD ** -0.5) for x, op in zip(ds, ops)]
            dkw = [lax.dot_general(x, op[4], tn_dims, preferred_element_type=F32) for x, op in zip(ds, ops)]
            dvw = [lax.dot_general(pp.astype(BF16), dd, tn_dims, preferred_element_type=F32)
                   for pp, dd in zip(p, dob)]
            ak_ref[...] = jnp.zeros_like(ak_ref)
            av_ref[...] = jnp.zeros_like(av_ref)
            for (b, e, rows, cs, q, kw, vw), x, yk, yv in zip(ops, dq, dkw, dvw):
                dq_ref[rows, cs] = x
                ak_ref[b * BLK:(b + 2) * BLK, cs] += yk
                av_ref[b * BLK:(b + 2) * BLK, cs] += yv
            if nt == 1:
                dk_ref[...] = ak_ref[BLK:, :]
                dv_ref[...] = av_ref[BLK:, :]
                return
            last = slice(TQ - BLK, TQ)
            dk_ref[...] = pk_ref[...]
            dv_ref[...] = pv_ref[...]
            dk_ref[last, :] += ak_ref[0:BLK, :]
            dv_ref[last, :] += av_ref[0:BLK, :]
            pk_ref[...] = ak_ref[BLK:, :]
            pv_ref[...] = av_ref[BLK:, :]

        @pl.when(t == nt)
        def _():
            dk_ref[...] = pk_ref[...]
            dv_ref[...] = pv_ref[...]

    cb = X.shape[1] // (d * 128)
    kcol, vcol = _attn_specs(cb, q_off, k_off, v_off, gqa)
    tile, blk = (TQ, 128), (BLK, 128)
    cur = lambda t: jnp.minimum(t, nt - 1)
    halo = lambda t: jnp.maximum(cur(t) * nq - 1, 0)
    ospec = pl.BlockSpec(tile, lambda r, hp, t: (cur(t), r * 4 + hp))
    in_specs = [
        pl.BlockSpec(memory_space=pltpu.SMEM),
        pl.BlockSpec(tile, lambda r, hp, t: (cur(t), r * cb + q_off + hp)),
        pl.BlockSpec(blk, lambda r, hp, t: (halo(t), kcol(r, hp))),
        pl.BlockSpec(tile, lambda r, hp, t: (cur(t), kcol(r, hp))),
        pl.BlockSpec(blk, lambda r, hp, t: (halo(t), vcol(r, hp))),
        pl.BlockSpec(tile, lambda r, hp, t: (cur(t), vcol(r, hp))),
        ospec, ospec, ospec, ospec,
    ]
    shifted = pl.BlockSpec(tile, lambda r, hp, t: (jnp.maximum(t - 1, 0), r * 4 + hp))
    out = jax.ShapeDtypeStruct((Ls, d * 512), F32)
    return _call(body, (slopes, X, X, X, X, X, o, lse, do, dlse), name=name, grid=(d, 4, nt + 1 if nt > 1 else 1),
                 in_specs=in_specs, out_specs=(ospec, shifted, shifted), out_shape=(out, out, out),
                 scratch_shapes=[pltpu.VMEM((TQ + BLK, 128), F32), pltpu.VMEM((TQ + BLK, 128), F32),
                                 pltpu.VMEM((TQ, 128), F32), pltpu.VMEM((TQ, 128), F32)],
                 sem=("parallel", "parallel", "arbitrary"), comm=comm)


def attn_merge_fwd(oa, la, sink, obs, lbs, name):
    L = oa.shape[0]
    tl = DIL_TL

    def body(oa_ref, la_ref, sk_ref, o1, o4, o16, l1, l4, l16, m_ref, so4, so16, sl4, sl16):
        m_ref[:, 0:512] = (oa_ref[...] * _sigmoid(la_ref[...] - sk_ref[...])).astype(BF16)
        for src, dst, d in ((o4, so4, 4), (o16, so16, 16), (l4, sl4, 4), (l16, sl16, 16)):
            _from_dilated(src, dst, d)
        for j in range(4):
            cs = slice(j * 128, (j + 1) * 128)
            a, b, c = l1[:, cs], sl4[j], sl16[j]
            mx = jnp.maximum(jnp.maximum(a, b), c)
            ea, eb, ec = jnp.exp(a - mx), jnp.exp(b - mx), jnp.exp(c - mx)
            inv = 1.0 / (ea + eb + ec)
            m_ref[:, 512 + j * 128:512 + (j + 1) * 128] = (
                (ea * inv) * o1[:, cs] + (eb * inv) * so4[j] + (ec * inv) * so16[j]).astype(BF16)

    big = pl.BlockSpec((tl, 512), lambda i: (i, 0))
    dil = [big, _dil_spec(tl, 4, 512), _dil_spec(tl, 16, 512)]
    return pl.pallas_call(
        body, name=name, grid=(L // tl,),
        in_specs=[big, big, pl.BlockSpec((1, 512), lambda i: (0, 0))] + dil + dil,
        out_specs=pl.BlockSpec((tl, 1024), lambda i: (i, 0)),
        out_shape=jax.ShapeDtypeStruct((L, 1024), BF16), scratch_shapes=[pltpu.VMEM((4, tl, 128), F32)] * 4,
        compiler_params=_cparams("parallel"),
    )(oa, la, sink, *obs, *lbs)


def attn_merge_bwd(dm, oa, la, sink, obs, lbs, name):
    L = oa.shape[0]
    tl = DIL_TL

    def body(dm_ref, oa_ref, la_ref, sk_ref, o1, o4, o16, l1, l4, l16,
             doa_ref, dla_ref, d1, d4, d16, g1, g4, g16, sums_ref, so4, so16, sl4, sl16, sd4, sd16, sg4, sg16):
        @pl.when(pl.program_id(0) == 0)
        def _():
            sums_ref[...] = jnp.zeros_like(sums_ref)

        for src, dst, d in ((o4, so4, 4), (o16, so16, 16), (l4, sl4, 4), (l16, sl16, 16)):
            _from_dilated(src, dst, d)
        ones = _seg_ones(HD)
        for t in range(4):
            cs = slice(t * 128, (t + 1) * 128)
            dma = dm_ref[:, cs]
            keep = _sigmoid(la_ref[:, cs] - sk_ref[:, cs])
            doa_ref[:, cs] = dma * keep
            tt = dma * oa_ref[:, cs] * keep * (1.0 - keep)
            dla_ref[:, cs] = _segsum(tt, ones)
            sums_ref[:, cs] += _fold8(-tt)
            dmb = dm_ref[:, 512 + t * 128:512 + (t + 1) * 128]
            a, b, c = l1[:, cs], sl4[t], sl16[t]
            mx = jnp.maximum(jnp.maximum(a, b), c)
            ea, eb, ec = jnp.exp(a - mx), jnp.exp(b - mx), jnp.exp(c - mx)
            inv = 1.0 / (ea + eb + ec)
            wa, wb, wc = ea * inv, eb * inv, ec * inv
            d1[:, cs] = wa * dmb
            sd4[t] = wb * dmb
            sd16[t] = wc * dmb
            sa = _segsum(dmb * o1[:, cs], ones)
            sb = _segsum(dmb * so4[t], ones)
            sc_ = _segsum(dmb * so16[t], ones)
            mean = wa * sa + wb * sb + wc * sc_
            g1[:, cs] = wa * (sa - mean)
            sg4[t] = wb * (sb - mean)
            sg16[t] = wc * (sc_ - mean)
        for src, dst, d in ((sd4, d4, 4), (sd16, d16, 16), (sg4, g4, 4), (sg16, g16, 16)):
            _to_dilated(src, dst, d)

    big = pl.BlockSpec((tl, 512), lambda i: (i, 0))
    dil = [big, _dil_spec(tl, 4, 512), _dil_spec(tl, 16, 512)]
    sd = jax.ShapeDtypeStruct
    shp = [sd((L, 512), F32), sd((L // 4, 4 * 512), F32), sd((L // 16, 16 * 512), F32)]
    return pl.pallas_call(
        body, name=name, grid=(L // tl,),
        in_specs=[pl.BlockSpec((tl, 1024), lambda i: (i, 0)), big, big,
                  pl.BlockSpec((1, 512), lambda i: (0, 0))] + dil + dil,
        out_specs=tuple([big, big] + dil + dil + [pl.BlockSpec((8, 512), lambda i: (0, 0))]),
        out_shape=tuple([shp[0], shp[0]] + shp + shp + [sd((8, 512), F32)]),
        scratch_shapes=[pltpu.VMEM((4, tl, 128), F32)] * 8, compiler_params=_cparams("arbitrary"),
    )(dm, oa, la, sink, *obs, *lbs)


def _shift_down(x, halo, k, first):
    rows = lax.broadcasted_iota(jnp.int32, (8, x.shape[1]), 0)
    out = pltpu.roll(x, k, axis=0)
    hrows = jnp.where(first, 0.0, pltpu.roll(halo, k, axis=0))
    top = jnp.where(rows < k, hrows, out[0:8, :])
    return jnp.concatenate([top, out[8:, :]], axis=0)


def _shift_up(x, nxt, k):
    tl = x.shape[0]
    rows = lax.broadcasted_iota(jnp.int32, (8, x.shape[1]), 0)
    out = pltpu.roll(x, tl - k, axis=0)
    bottom = jnp.where(rows >= 8 - k, pltpu.roll(nxt, 8 - k, axis=0), out[tl - 8:, :])
    return jnp.concatenate([out[:tl - 8, :], bottom], axis=0)


def _silu(x):
    return x * _sigmoid(x)


def _dsilu(x):
    s = _sigmoid(x)
    return s * (1.0 + x * (1.0 - s))


def ffn_act_fwd(ua, ub, cw, name, comm=None):
    L, F = ua.shape
    tl = _rtile(L, 256)
    tc = _tile(F, 1408)
    hb = tl // 8

    def body(ua_ref, uah_ref, ub_ref, ubh_ref, wa_ref, wb_ref, o_ref, ac_ref, bc_ref):
        first = pl.program_id(1) == 0

        def conv(x_ref, h_ref, w_ref):
            x = x_ref[...]
            h = h_ref[...]
            return (w_ref[2:3, :] * x + w_ref[1:2, :] * _shift_down(x, h, 1, first)
                    + w_ref[0:1, :] * _shift_down(x, h, 2, first))

        a = conv(ua_ref, uah_ref, wa_ref)
        b = conv(ub_ref, ubh_ref, wb_ref)
        ac_ref[...] = a
        bc_ref[...] = b
        o_ref[...] = (_silu(a) * b).astype(BF16)

    main = pl.BlockSpec((tl, tc), lambda j, i: (i, j))
    halo = pl.BlockSpec((8, tc), lambda j, i: (jnp.maximum(i * hb - 1, 0), j))
    wa = pl.BlockSpec((3, tc), lambda j, i: (0, j))
    wb = pl.BlockSpec((3, tc), lambda j, i: (0, j + F // tc))
    f32 = jax.ShapeDtypeStruct((L, F), F32)
    return _call(body, (ua, ua, ub, ub, cw, cw), name=name, grid=(F // tc, L // tl),
                 in_specs=[main, halo, main, halo, wa, wb], out_specs=(main, main, main),
                 out_shape=(jax.ShapeDtypeStruct((L, F), BF16), f32, f32), sem=("parallel", "parallel"), comm=comm)


def ffn_act_bwd(ua, ub, ac, bc, cw, dact, name, comm=None):
    L, F = ua.shape
    tl = _rtile(L, 256)
    tc = _tile(F, 1408)
    nrt = L // tl

    def body(ua_ref, ub_ref, ac_ref, bc_ref, wa_ref, wb_ref, da_ref, dua_ref, dub_ref, sums_ref, ca_ref, cb_ref):
        i = pl.program_id(1)

        @pl.when(i == 0)
        def _():
            sums_ref[...] = jnp.zeros_like(sums_ref)
            ca_ref[...] = jnp.zeros_like(ca_ref)
            cb_ref[...] = jnp.zeros_like(cb_ref)

        a, b = ac_ref[...], bc_ref[...]
        dact_v = da_ref[...]
        dya = dact_v * b * _dsilu(a)
        dyb = dact_v * _silu(a)
        for (dy, w_ref, c_ref, d_ref, x_ref, base) in ((dya, wa_ref, ca_ref, dua_ref, ua_ref, 0),
                                                        (dyb, wb_ref, cb_ref, dub_ref, ub_ref, 24)):
            nxt = c_ref[...]
            ups = (dy, _shift_up(dy, nxt, 1), _shift_up(dy, nxt, 2))
            d_ref[...] = (w_ref[2:3, :] * ups[0] + w_ref[1:2, :] * ups[1] + w_ref[0:1, :] * ups[2]).astype(BF16)
            c_ref[...] = dy[0:8, :]
            x = x_ref[...]
            for k in range(3):
                sums_ref[base + 8 * (2 - k):base + 8 * (2 - k) + 8, :] += _fold8(ups[k] * x)

    rev = lambda i: nrt - 1 - i
    main = pl.BlockSpec((tl, tc), lambda j, i: (rev(i), j))
    wa = pl.BlockSpec((3, tc), lambda j, i: (0, j))
    wb = pl.BlockSpec((3, tc), lambda j, i: (0, j + F // tc))
    ob = jax.ShapeDtypeStruct((L, F), BF16)
    return _call(body, (ua, ub, ac, bc, cw, cw, dact), name=name, grid=(F // tc, nrt),
                 in_specs=[main, main, main, main, wa, wb, main],
                 out_specs=(main, main, pl.BlockSpec((48, tc), lambda j, i: (0, j))),
                 out_shape=(ob, ob, jax.ShapeDtypeStruct((48, F), F32)),
                 scratch_shapes=[pltpu.VMEM((8, tc), F32), pltpu.VMEM((8, tc), F32)],
                 sem=("parallel", "arbitrary"), comm=comm)


def attn_vectors(qna, kna, qnb, knb, sinks):
    ones = jnp.ones((128,), F32)
    wvec = jnp.concatenate([jnp.tile(qna, 8), jnp.tile(kna, 2), ones, jnp.tile(qnb, 8), jnp.tile(knb, 8),
                            jnp.tile(ones, 4)]).reshape(1, ATTN_IN)
    return wvec, jnp.repeat(sinks, HD).reshape(1, 512)


def _with_comm(result, comm):
    return result if comm is not None else (result, None)


def attention_block_fwd(h, w_in, wvec, sinkvec, w_out, tag, comms=None):
    L = h.shape[0]
    comms = comms or {}
    got = {}
    qkv = matmul([(h, w_in)], "nn", tag + "_qkv")
    X, X4, X16 = qknorm_fwd(qkv, wvec, tag + "_qknorm")
    (oa, la), got['swa'] = _with_comm(attn_fwd(X, 1, 0, 4, 5, True, 0, BLK - 1, tag + "_swa",
                                               comm=comms.get('swa')), comms.get('swa'))
    views = {1: (X, 6, 10, 14), 4: (X4, 0, 4, 8), 16: (X16, 0, 4, 8)}
    obs, lbs = [], []
    for window, d in B_BRANCHES:
        xd, qo, ko, vo = views[d]
        (o, l), got[d] = _with_comm(attn_fwd(xd, d, qo, ko, vo, False, 8, window // d,
                                             tag + f"_dil{d}", comm=comms.get(d)), comms.get(d))
        obs.append(o)
        lbs.append(l)
    m = attn_merge_fwd(oa, la, sinkvec, obs, lbs, tag + "_merge")
    if w_out is None:
        w_out = got[16][0].reshape(D, D)
        got['w_out'] = w_out
    y = matmul([(m, w_out)], "nn", tag + "_out")
    return y, (h, qkv, views, oa, la, obs, lbs, m), got


def attention_block_bwd(dy, res, w_in, wvec, sinkvec, w_out, tag, comms=None, send_w_out_on=None):
    h, qkv, views, oa, la, obs, lbs, m = res
    comms = dict(comms or {})
    got = {}
    g_w_out = matmul([(m, dy)], "tn", tag + "_dwout", out_dtype=BF16)
    if send_w_out_on is not None:
        comms[send_w_out_on] = ([g_w_out.reshape(N_DEV, D // N_DEV, D)], False)
    dm = matmul([(dy, w_out)], "nt", tag + "_dm")
    doa, dla, d1, d2, d3, g1, g2, g3, sinksums = attn_merge_bwd(dm, oa, la, sinkvec, obs, lbs, tag + "_dmerge")
    d_a, got['swa'] = _with_comm(attn_bwd(views[1][0], oa, la, doa, dla, 1, 0, 4, 5, True, 0, BLK - 1,
                                          tag + "_dswa", comm=comms.get('swa')), comms.get('swa'))
    d_b = []
    for (window, d), o, l, do, dl in zip(B_BRANCHES, obs, lbs, (d1, d2, d3), (g1, g2, g3)):
        xd, qo, ko, vo = views[d]
        dqkv_d, got[d] = _with_comm(attn_bwd(xd, o, l, do, dl, d, qo, ko, vo, False, 8, window // d,
                                             tag + f"_ddil{d}", comm=comms.get(d)), comms.get(d))
        d_b.append(dqkv_d)
    dqkv, wsums = qknorm_bwd(qkv, wvec, d_a, d_b, tag + "_dqknorm")
    g_w_in = matmul([(h, dqkv)], "tn", tag + "_dwin", out_dtype=BF16)
    dh = matmul([(dqkv, w_in)], "nt", tag + "_dh")
    ws = wsums.sum(axis=0)
    grads = dict(
        w_in=g_w_in, w_out=g_w_out,
        q_norm_a=ws[0:512].reshape(8, HD).sum(axis=0), k_norm_a=ws[512:640].reshape(2, HD).sum(axis=0),
        q_norm_b=ws[768:1280].reshape(8, HD).sum(axis=0), k_norm_b=ws[1280:1792].reshape(8, HD).sum(axis=0),
        sinks=sinksums.sum(axis=0).reshape(8, HD).sum(axis=1))
    return dh, grads, got


def ffn_block_fwd(h, w_up_a, w_up_b, cw, w_down, tag, comm=None):
    ua = matmul([(h, w_up_a)], "nn", tag + "_upa")
    ub = matmul([(h, w_up_b)], "nn", tag + "_upb")
    (act, ac, bc), got = _with_comm(ffn_act_fwd(ua, ub, cw, tag + "_act", comm=comm), comm)
    f = matmul([(act, w_down)], "nn", tag + "_down")
    return f, (h, ua, ub, ac, bc, act), got


def ffn_block_bwd(df, res, w_up_a, w_up_b, cw, w_down, tag, comm=None):
    h, ua, ub, ac, bc, act = res
    g_down = matmul([(act, df)], "tn", tag + "_dwdown", out_dtype=BF16)
    dact = matmul([(df, w_down)], "nt", tag + "_dact")
    (dua, dub, sums), got = _with_comm(ffn_act_bwd(ua, ub, ac, bc, cw, dact, tag + "_dactk", comm=comm), comm)
    g_up = jnp.concatenate([_cols_to_slabs(matmul([(h, dua)], "tn", tag + "_dwupa", out_dtype=BF16), N_DEV // 2),
                            _cols_to_slabs(matmul([(h, dub)], "tn", tag + "_dwupb", out_dtype=BF16), N_DEV // 2)],
                           axis=0)
    dh = matmul([(dua, w_up_a), (dub, w_up_b)], "nt", tag + "_dh")
    s = sums.reshape(2, 3, 8, D_FF).sum(axis=2)
    g_conv = jnp.concatenate([s[0], s[1]], axis=1)
    return dh, dict(w_up=g_up, conv=g_conv, w_down=g_down), got


def s5_params(lam_re, lam_im, log_dt, b_re, b_im, c_re, c_im):
    dt = jnp.exp(log_dt)[:, None]
    mag, ang = jnp.exp(lam_re * dt), lam_im * dt
    a_re, a_im = mag * jnp.cos(ang), mag * jnp.sin(ang)
    nr, ni = a_re - 1.0, a_im
    den = lam_re * lam_re + lam_im * lam_im
    f_re = (nr * lam_re + ni * lam_im) / den
    f_im = (ni * lam_re - nr * lam_im) / den
    eye = jnp.eye(16, dtype=F32)[:, None, :, None]
    bd = lambda b: (eye * jnp.transpose(b, (0, 2, 1))[:, :, None, :]).reshape(S5_W, S5_P)
    cd = lambda c: (eye * jnp.transpose(c, (0, 2, 1))[:, :, None, :]).reshape(S5_P, S5_W)
    flat = lambda t: t.reshape(1, S5_P)
    return flat(a_re), flat(a_im), flat(f_re), flat(f_im), bd(b_re), bd(b_im), cd(c_re), cd(c_im)


def _scan_tables(a_re, a_im, reverse):
    pows = [(a_re, a_im)]
    for _ in range(7):
        pr, pi = pows[-1]
        pows.append((pr * a_re - pi * a_im, pr * a_im + pi * a_re))
    order = list(range(7, -1, -1)) if reverse else list(range(8))
    z = jnp.zeros_like(a_re)
    rows = [pows[0][0], pows[0][1], pows[1][0], pows[1][1], pows[3][0], pows[3][1], z, z]
    rows += [pows[k][0] for k in order] + [pows[k][1] for k in order]
    return jnp.concatenate(rows, axis=0)


def _block_scan(er, ei, tab_ref, cr, ci, reverse):
    rows = lax.broadcasted_iota(jnp.int32, er.shape, 0)
    for idx, s in enumerate((1, 2, 4)):
        if reverse:
            sr, si, keep = pltpu.roll(er, 8 - s, axis=0), pltpu.roll(ei, 8 - s, axis=0), rows < 8 - s
        else:
            sr, si, keep = pltpu.roll(er, s, axis=0), pltpu.roll(ei, s, axis=0), rows >= s
        sr, si = jnp.where(keep, sr, 0.0), jnp.where(keep, si, 0.0)
        ar, ai = tab_ref[2 * idx:2 * idx + 1, :], tab_ref[2 * idx + 1:2 * idx + 2, :]
        er, ei = er + ar * sr - ai * si, ei + ar * si + ai * sr
    pr, pi_ = tab_ref[8:16, :], tab_ref[16:24, :]
    er, ei = er + pr * cr - pi_ * ci, ei + pr * ci + pi_ * cr
    return er, ei


def s5_scan_fwd(bu_re, bu_im, a_re, a_im, f_re, f_im, name):
    L, P = bu_re.shape
    tl = _rtile(L, 512)
    tab = _scan_tables(a_re, a_im, False)
    fvec = jnp.concatenate([f_re, f_im] + [jnp.zeros_like(f_re)] * 6, axis=0)

    def body(br_ref, bi_ref, tab_ref, f_ref, xr_ref, xi_ref, c_ref):
        @pl.when(pl.program_id(0) == 0)
        def _():
            c_ref[...] = jnp.zeros_like(c_ref)

        def blk(i, carry):
            cr, ci = carry
            rows = pl.ds(pl.multiple_of(i * 8, 8), 8)
            br, bi = br_ref[rows, :], bi_ref[rows, :]
            fr, fi = f_ref[0:1, :], f_ref[1:2, :]
            er, ei = _block_scan(fr * br - fi * bi, fr * bi + fi * br, tab_ref, cr, ci, False)
            xr_ref[rows, :] = er
            xi_ref[rows, :] = ei
            return er[7:8, :], ei[7:8, :]

        cr, ci = lax.fori_loop(0, tl // 8, blk, (c_ref[0:1, :], c_ref[1:2, :]))
        c_ref[0:1, :] = cr
        c_ref[1:2, :] = ci

    big = pl.BlockSpec((tl, P), lambda i: (i, 0))
    out = jax.ShapeDtypeStruct((L, P), F32)
    return pl.pallas_call(
        body, name=name, grid=(L // tl,),
        in_specs=[big, big, pl.BlockSpec((24, P), lambda i: (0, 0)), pl.BlockSpec((8, P), lambda i: (0, 0))],
        out_specs=(big, big), out_shape=(out, out), scratch_shapes=[pltpu.VMEM((8, P), F32)],
        compiler_params=_cparams("arbitrary"))(bu_re, bu_im, tab, fvec)


def s5_scan_bwd(dx_re, dx_im, x_re, x_im, bu_re, bu_im, a_re, a_im, f_re, f_im, name):
    L, P = dx_re.shape
    tl = _rtile(L, 256)
    nt = L // tl
    tab = _scan_tables(a_re, -a_im, True)
    fvec = jnp.concatenate([f_re, f_im] + [jnp.zeros_like(f_re)] * 6, axis=0)

    def body(gr_ref, gi_ref, xr_ref, xi_ref, br_ref, bi_ref, tab_ref, f_ref, dbr_ref, dbi_ref, s_ref, c_ref):
        @pl.when(pl.program_id(0) == 0)
        def _():
            c_ref[...] = jnp.zeros_like(c_ref)
            s_ref[...] = jnp.zeros_like(s_ref)

        def blk(k, carry):
            cr, ci = carry
            i = tl // 8 - 1 - k
            rows = pl.ds(pl.multiple_of(i * 8, 8), 8)
            er, ei = _block_scan(gr_ref[rows, :], gi_ref[rows, :], tab_ref, cr, ci, True)
            rid = lax.broadcasted_iota(jnp.int32, er.shape, 0)
            sr = jnp.where(rid == 7, cr, pltpu.roll(er, 7, axis=0))
            si = jnp.where(rid == 7, ci, pltpu.roll(ei, 7, axis=0))
            xr, xi = xr_ref[rows, :], xi_ref[rows, :]
            s_ref[0:8, :] += sr * xr + si * xi
            s_ref[8:16, :] += si * xr - sr * xi
            br, bi = br_ref[rows, :], bi_ref[rows, :]
            s_ref[16:24, :] += er * br + ei * bi
            s_ref[24:32, :] += ei * br - er * bi
            fr, fi = f_ref[0:1, :], f_ref[1:2, :]
            dbr_ref[rows, :] = fr * er + fi * ei
            dbi_ref[rows, :] = fr * ei - fi * er
            return er[0:1, :], ei[0:1, :]

        cr, ci = lax.fori_loop(0, tl // 8, blk, (c_ref[0:1, :], c_ref[1:2, :]))
        c_ref[0:1, :] = cr
        c_ref[1:2, :] = ci

    big = pl.BlockSpec((tl, P), lambda i: (nt - 1 - i, 0))
    out = jax.ShapeDtypeStruct((L, P), F32)
    return pl.pallas_call(
        body, name=name, grid=(nt,),
        in_specs=[big] * 6 + [pl.BlockSpec((24, P), lambda i: (0, 0)), pl.BlockSpec((8, P), lambda i: (0, 0))],
        out_specs=(big, big, pl.BlockSpec((32, P), lambda i: (0, 0))),
        out_shape=(out, out, jax.ShapeDtypeStruct((32, P), F32)), scratch_shapes=[pltpu.VMEM((8, P), F32)],
        compiler_params=_cparams("arbitrary"))(dx_re, dx_im, x_re, x_im, bu_re, bu_im, tab, fvec)


_GK, _GC = math.sqrt(2.0 / math.pi), 0.044715


def _gelu(y):
    return 0.5 * y * (1.0 + jnp.tanh(_GK * (y + _GC * y * y * y)))


def _dgelu(y):
    t = jnp.tanh(_GK * (y + _GC * y * y * y))
    return 0.5 * (1.0 + t) + 0.5 * y * (1.0 - t * t) * _GK * (1.0 + 3.0 * _GC * y * y)


def s5_out_fwd(x_re, x_im, u, cd_re, cd_im, dskip, glu_w, glu_b, name):
    L = u.shape[0]
    tl = _rtile(L, 512)

    def body(xr_ref, xi_ref, u_ref, cr_ref, ci_ref, d_ref, w_ref, b_ref, y_ref, o_ref):
        y = (jnp.dot(xr_ref[...].astype(BF16), cr_ref[...], preferred_element_type=F32)
             - jnp.dot(xi_ref[...].astype(BF16), ci_ref[...], preferred_element_type=F32)
             + d_ref[...] * u_ref[...])
        y_ref[...] = y
        g = _gelu(y)
        z = jnp.dot(g.astype(BF16), w_ref[...], preferred_element_type=F32) + b_ref[...]
        o_ref[...] = (g * _sigmoid(z)).astype(BF16)

    big = pl.BlockSpec((tl, S5_P), lambda i: (i, 0))
    sm = pl.BlockSpec((tl, S5_W), lambda i: (i, 0))
    full = lambda r, c: pl.BlockSpec((r, c), lambda i: (0, 0))
    return pl.pallas_call(
        body, name=name, grid=(L // tl,),
        in_specs=[big, big, sm, full(S5_P, S5_W), full(S5_P, S5_W), full(1, S5_W), full(S5_W, S5_W), full(1, S5_W)],
        out_specs=(sm, sm),
        out_shape=(jax.ShapeDtypeStruct((L, S5_W), F32), jax.ShapeDtypeStruct((L, S5_W), BF16)),
        compiler_params=_cparams("parallel"))(x_re, x_im, u, cd_re, cd_im, dskip, glu_w, glu_b)


def s5_out_bwd(dout, y, u, x_re, x_im, cd_re, cd_im, dskip, glu_w, glu_b, name, dout_col=0):
    L = u.shape[0]
    tl = _rtile(L, 256)
    nt_dims = (((1,), (1,)), ((), ()))
    tn_dims = (((0,), (0,)), ((), ()))

    def body(do_ref, y_ref, u_ref, xr_ref, xi_ref, cr_ref, ci_ref, d_ref, w_ref, b_ref,
             dxr_ref, dxi_ref, du_ref, dcr_ref, dci_ref, dw_ref, s_ref):
        @pl.when(pl.program_id(0) == 0)
        def _():
            dcr_ref[...] = jnp.zeros_like(dcr_ref)
            dci_ref[...] = jnp.zeros_like(dci_ref)
            dw_ref[...] = jnp.zeros_like(dw_ref)
            s_ref[...] = jnp.zeros_like(s_ref)

        yv, dov = y_ref[...], do_ref[...]
        g = _gelu(yv)
        gb = g.astype(BF16)
        sg = _sigmoid(jnp.dot(gb, w_ref[...], preferred_element_type=F32) + b_ref[...])
        dz = dov * g * sg * (1.0 - sg)
        dzb = dz.astype(BF16)
        dg = dov * sg + lax.dot_general(dzb, w_ref[...], nt_dims, preferred_element_type=F32)
        dw_ref[...] += lax.dot_general(gb, dzb, tn_dims, preferred_element_type=F32)
        dy = dg * _dgelu(yv)
        dyb = dy.astype(BF16)
        s_ref[0:8, :] += _fold8(dy * u_ref[...])
        s_ref[8:16, :] += _fold8(dz)
        du_ref[...] = dy * d_ref[...]
        dxr_ref[...] = lax.dot_general(dyb, cr_ref[...], nt_dims, preferred_element_type=F32)
        dxi_ref[...] = -lax.dot_general(dyb, ci_ref[...], nt_dims, preferred_element_type=F32)
        dcr_ref[...] += lax.dot_general(xr_ref[...].astype(BF16), dyb, tn_dims, preferred_element_type=F32)
        dci_ref[...] -= lax.dot_general(xi_ref[...].astype(BF16), dyb, tn_dims, preferred_element_type=F32)

    big = pl.BlockSpec((tl, S5_P), lambda i: (i, 0))
    sm = pl.BlockSpec((tl, S5_W), lambda i: (i, 0))
    full = lambda r, c: pl.BlockSpec((r, c), lambda i: (0, 0))
    sd = jax.ShapeDtypeStruct
    return pl.pallas_call(
        body, name=name, grid=(L // tl,),
        in_specs=[pl.BlockSpec((tl, S5_W), lambda i: (i, dout_col)), sm, sm, big, big, full(S5_P, S5_W),
                  full(S5_P, S5_W), full(1, S5_W), full(S5_W, S5_W), full(1, S5_W)],
        out_specs=(big, big, sm, full(S5_P, S5_W), full(S5_P, S5_W), full(S5_W, S5_W), full(16, S5_W)),
        out_shape=(sd((L, S5_P), F32), sd((L, S5_P), F32), sd((L, S5_W), F32), sd((S5_P, S5_W), F32),
                   sd((S5_P, S5_W), F32), sd((S5_W, S5_W), F32), sd((16, S5_W), F32)),
        compiler_params=_cparams("arbitrary"))(dout, y, u, x_re, x_im, cd_re, cd_im, dskip, glu_w, glu_b)


def s5_block_fwd(u, params, dskip, glu_w, glu_b, tag):
    a_re, a_im, f_re, f_im, bd_re, bd_im, cd_re, cd_im = params
    bu_re = matmul([(u, bd_re.astype(BF16))], "nn", tag + "_bure")
    bu_im = matmul([(u, bd_im.astype(BF16))], "nn", tag + "_buim")
    x_re, x_im = s5_scan_fwd(bu_re, bu_im, a_re, a_im, f_re, f_im, tag + "_scan")
    y, out = s5_out_fwd(x_re, x_im, u, cd_re.astype(BF16), cd_im.astype(BF16), dskip, glu_w, glu_b, tag + "_out")
    return out, (u, bu_re, bu_im, x_re, x_im, y)


def s5_block_bwd(dout, res, params, dskip, glu_w, glu_b, tag, dout_col=0):
    u, bu_re, bu_im, x_re, x_im, y = res
    a_re, a_im, f_re, f_im, bd_re, bd_im, cd_re, cd_im = params
    dxr, dxi, du, dcr, dci, dglu_w, sums = s5_out_bwd(dout, y, u, x_re, x_im, cd_re.astype(BF16), cd_im.astype(BF16),
                                                      dskip, glu_w, glu_b, tag + "_dout", dout_col=dout_col)
    dbr, dbi, acc = s5_scan_bwd(dxr, dxi, x_re, x_im, bu_re, bu_im, a_re, a_im, f_re, f_im, tag + "_dscan")
    du = du + matmul([(dbr, bd_re.astype(BF16)), (dbi, bd_im.astype(BF16))], "nt", tag + "_du")
    dbd_re = matmul([(u, dbr)], "tn", tag + "_dbdre")
    dbd_im = matmul([(u, dbi)], "tn", tag + "_dbdim")
    acc = acc.reshape(4, 8, S5_P).sum(axis=1)
    s = sums.reshape(2, 8, S5_W).sum(axis=1)
    cot = (acc[0:1], acc[1:2], acc[2:3], acc[3:4], dbd_re, dbd_im, dcr, dci)
    return du, cot, dict(dskip=s[0], glu_w=dglu_w, glu_b=s[1])


DN_Z0, DN_NT = 18, 18
REC_U0, REC_A0 = 3072, 3328


def rec_cols_permute(w):
    return jnp.concatenate([w[..., S5_W:REC_A0], w[..., :S5_W], w[..., REC_A0:]], axis=-1)


def rec_cols_restore(w):
    return jnp.concatenate([w[..., REC_U0:REC_A0], w[..., :REC_U0], w[..., REC_A0:]], axis=-1)


DN_W = DN_H * DN_DK


def _dn_conv4(taps, w_ref):
    xc = w_ref[3:4, :] * taps[0]
    for k in range(1, 4):
        xc = xc + w_ref[3 - k:4 - k, :] * taps[k]
    return xc


def dn_prep_fwd(rin, cw, name):
    L = rin.shape[0]
    tl = _rtile(L, 256)
    hb = tl // 8

    def body(x_ref, h_ref, w_ref, o_ref):
        j = pl.program_id(0)
        first = pl.program_id(1) == 0
        x, h = x_ref[...], h_ref[...]
        s = _silu(_dn_conv4([x] + [_shift_down(x, h, k, first) for k in range(1, 4)], w_ref))
        scale = jnp.where(j == 0, DN_DK ** -0.5, 1.0)
        for hd in _HEADS:
            cs = slice(hd * 128, (hd + 1) * 128)
            sh = s[:, cs]
            r = lax.rsqrt(jnp.sum(sh * sh, axis=-1, keepdims=True) + EPS)
            o_ref[:, cs] = jnp.where(j < 2, sh * r * scale, sh)

    main = pl.BlockSpec((tl, DN_W), lambda j, i: (i, j))
    halo = pl.BlockSpec((8, DN_W), lambda j, i: (jnp.maximum(i * hb - 1, 0), j))
    return pl.pallas_call(
        body, name=name, grid=(3, L // tl),
        in_specs=[main, halo, pl.BlockSpec((4, DN_W), lambda j, i: (0, j))],
        out_specs=main, out_shape=jax.ShapeDtypeStruct((L, 3 * DN_W), F32),
        compiler_params=_cparams("parallel", "parallel"))(rin, rin, cw)


def dn_prep_bwd(rin, cw, dout, name):
    L = rin.shape[0]
    tl = _rtile(L, 256)
    hb = tl // 8
    nrt = L // tl

    def body(x_ref, h_ref, w_ref, d_ref, dx_ref, s_ref, c_ref):
        j = pl.program_id(0)
        i = pl.program_id(1)
        first = i == nrt - 1

        @pl.when(i == 0)
        def _():
            s_ref[...] = jnp.zeros_like(s_ref)
            c_ref[...] = jnp.zeros_like(c_ref)

        x, h = x_ref[...], h_ref[...]
        taps = [x] + [_shift_down(x, h, k, first) for k in range(1, 4)]
        xc = _dn_conv4(taps, w_ref)
        s = _silu(xc)
        scale = jnp.where(j == 0, DN_DK ** -0.5, 1.0)
        pieces = []
        for hd in _HEADS:
            cs = slice(hd * 128, (hd + 1) * 128)
            sh, d = s[:, cs], d_ref[:, cs]
            r = lax.rsqrt(jnp.sum(sh * sh, axis=-1, keepdims=True) + EPS)
            n = sh * r
            dn = d * scale
            pieces.append(jnp.where(j < 2, r * (dn - n * jnp.sum(dn * n, axis=-1, keepdims=True)), d))
        dxc = jnp.concatenate(pieces, axis=1) * _dsilu(xc)
        nxt = c_ref[...]
        dx_ref[...] = _dn_conv4([dxc] + [_shift_up(dxc, nxt, k) for k in range(1, 4)], w_ref).astype(BF16)
        c_ref[...] = dxc[0:8, :]
        for k in range(4):
            s_ref[8 * (3 - k):8 * (3 - k) + 8, :] += _fold8(dxc * taps[k])

    rev = lambda i: nrt - 1 - i
    main = pl.BlockSpec((tl, DN_W), lambda j, i: (rev(i), j))
    halo = pl.BlockSpec((8, DN_W), lambda j, i: (jnp.maximum(rev(i) * hb - 1, 0), j))
    return pl.pallas_call(
        body, name=name, grid=(3, nrt),
        in_specs=[main, halo, pl.BlockSpec((4, DN_W), lambda j, i: (0, j)), main],
        out_specs=(main, pl.BlockSpec((32, DN_W), lambda j, i: (0, j))),
        out_shape=(jax.ShapeDtypeStruct((L, 3 * DN_W), BF16), jax.ShapeDtypeStruct((32, 3 * DN_W), F32)),
        scratch_shapes=[pltpu.VMEM((8, DN_W), F32)],
        compiler_params=_cparams("parallel", "arbitrary"))(rin, rin, cw, dout)


_HI = lax.Precision.HIGH
_NT = (((1,), (1,)), ((), ()))
_TN = (((0,), (0,)), ((), ()))
_HEADS = tuple(range(DN_H))


def _mm(a, b, dims=(((1,), (0,)), ((), ())), hi=False):
    if hi:
        return lax.dot_general(a, b, dims, precision=_HI, preferred_element_type=F32)
    return lax.dot_general(a.astype(BF16), b.astype(BF16), dims, preferred_element_type=F32)


def _dn_masks():
    ri = lax.broadcasted_iota(jnp.int32, (DN_C, DN_C), 0)
    ci = lax.broadcasted_iota(jnp.int32, (DN_C, DN_C), 1)
    return ri >= ci, ri > ci, (ri == ci).astype(F32)


def _dn_decay(gc, gr, causal):
    gam = [jnp.where(causal, jnp.exp(jnp.where(causal, c - r, 0.0)), 0.0) for c, r in zip(gc, gr)]
    eg, el, gl = _dn_row_decay(gc)
    return gam, eg, el, gl


def _dn_row_decay(gc):
    eg = [jnp.exp(c) for c in gc]
    el = [jnp.exp(c[DN_C - 1:DN_C, :] - c) for c in gc]
    gl = [jnp.exp(c[DN_C - 1:DN_C, :]) for c in gc]
    return eg, el, gl


def _dn_solve(k, v, beta, gam, eg, kk, strict, eye):
    ids = range(len(k))
    nmat = [jnp.where(strict, beta[h] * kk[h] * gam[h], 0.0) for h in ids]
    t = [eye - nmat[h] for h in ids]
    m = [_mm(nmat[h], nmat[h]) for h in ids]
    for step in range(5):
        t = [t[h] + _mm(t[h], m[h]) for h in ids]
        if step < 4:
            m = [_mm(m[h], m[h]) for h in ids]
    res = [eye - t[h] - _mm(nmat[h], t[h], hi=True) for h in ids]
    t = [t[h] + _mm(t[h], res[h]) for h in ids]
    rhs = [jnp.concatenate([v[h] * beta[h], k[h] * (beta[h] * eg[h])], axis=1) for h in ids]
    sol = [_mm(t[h], rhs[h], hi=True) for h in ids]
    return t, sol


def dn_chunk_fwd(qkv, gcol, grow, bcol, name, comm=None):
    L = qkv.shape[0]
    C, W = DN_C, DN_H * DN_DK
    ncb = 8
    tl = ncb * C
    nchunks = L // C
    comm1, comm2 = comm if comm is not None else (None, None)
    hs = lambda h: slice(h * 128, (h + 1) * 128)

    def intra(q_ref, k_ref, v_ref, gc_ref, gr_ref, b_ref, t_ref, sol_ref, qk_ref):
        causal, strict, eye = _dn_masks()

        def pair(p, _):
            units = [(2 * p + j, h) for j in range(2) for h in _HEADS]
            rows = [pl.ds(pl.multiple_of(c * C, C), C) for c, _ in units]
            q = [q_ref[r, hs(h)] for r, (_, h) in zip(rows, units)]
            k = [k_ref[r, hs(h)] for r, (_, h) in zip(rows, units)]
            v = [v_ref[r, hs(h)] for r, (_, h) in zip(rows, units)]
            gc = [gc_ref[r, h:h + 1] for r, (_, h) in zip(rows, units)]
            gr = [gr_ref[c][h:h + 1, :] for c, h in units]
            beta = [b_ref[r, h:h + 1] for r, (_, h) in zip(rows, units)]
            gam, eg, _, _ = _dn_decay(gc, gr, causal)
            kk = [_mm(x, x, _NT) for x in k]
            t, sol = _dn_solve(k, v, beta, gam, eg, kk, strict, eye)
            qk = [_mm(a, b, _NT) * g for a, b, g in zip(q, k, gam)]
            for i, (r, (_, h)) in enumerate(zip(rows, units)):
                t_ref[r, h * C:(h + 1) * C] = t[i]
                sol_ref[r, h * 256:(h + 1) * 256] = sol[i]
                qk_ref[r, h * C:(h + 1) * C] = qk[i]
            return 0

        lax.fori_loop(0, ncb // 2, pair, 0)

    def scan(q_ref, k_ref, gc_ref, sol_ref, qk_ref, o_ref, sh_ref, s_ref):
        @pl.when(pl.program_id(0) == 0)
        def _():
            s_ref[...] = jnp.zeros_like(s_ref)

        def chunk(c, _):
            rows = pl.ds(pl.multiple_of(c * C, C), C)
            q = [q_ref[rows, hs(h)] for h in _HEADS]
            k = [k_ref[rows, hs(h)] for h in _HEADS]
            sol = [sol_ref[rows, h * 256:(h + 1) * 256] for h in _HEADS]
            qk = [qk_ref[rows, h * C:(h + 1) * C] for h in _HEADS]
            eg, el, gl = _dn_row_decay([gc_ref[rows, h:h + 1] for h in _HEADS])
            S = [s_ref[hs(h), :] for h in _HEADS]
            vn = [sol[h][:, :128] - _mm(sol[h][:, 128:], S[h]) for h in _HEADS]
            o = [_mm(q[h] * eg[h], S[h]) + _mm(qk[h], vn[h]) for h in _HEADS]
            Sn = [S[h] * gl[h] + _mm(k[h] * el[h], vn[h], _TN) for h in _HEADS]
            for h in _HEADS:
                sh_ref[c, hs(h), :] = S[h]
                s_ref[hs(h), :] = Sn[h]
                o_ref[rows, hs(h)] = o[h]
            return 0

        lax.fori_loop(0, ncb, chunk, 0)

    col = lambda b: pl.BlockSpec((tl, W), lambda i: (i, b))
    small = pl.BlockSpec((tl, 8), lambda i: (i, 0))
    rowblk = lambda w: pl.BlockSpec((tl, w), lambda i: (i, 0))
    sd = jax.ShapeDtypeStruct
    (thist, solhist, qk), got1 = _with_comm(_call(
        intra, (qkv, qkv, qkv, gcol, grow, bcol), name=name + "_intra", grid=(L // tl,),
        in_specs=[col(0), col(1), col(2), small, pl.BlockSpec((ncb, 8, C), lambda i: (i, 0, 0)), small],
        out_specs=(rowblk(DN_H * C), rowblk(DN_H * 256), rowblk(DN_H * C)),
        out_shape=(sd((L, DN_H * C), F32), sd((L, DN_H * 256), F32), sd((L, DN_H * C), F32)),
        sem=("parallel",), comm=comm1), comm1)
    (o, shist), got2 = _with_comm(_call(
        scan, (qkv, qkv, gcol, solhist, qk), name=name + "_scan", grid=(L // tl,),
        in_specs=[col(0), col(1), small, rowblk(DN_H * 256), rowblk(DN_H * C)],
        out_specs=(rowblk(W), pl.BlockSpec((ncb, W, 128), lambda i: (i, 0, 0))),
        out_shape=(sd((L, W), F32), sd((nchunks, W, 128), F32)),
        scratch_shapes=[pltpu.VMEM((W, 128), F32)], sem=("arbitrary",), comm=comm2), comm2)
    res = (o, shist, thist, solhist)
    return res if comm is None else (res, (got1 or []) + (got2 or []))


def dn_chunk_bwd(qkv, gcol, grow, bcol, shist, thist, solhist, do, name, comm=None):
    L = qkv.shape[0]
    C, W = DN_C, DN_H * DN_DK
    ncb = 8
    tl = ncb * C
    nchunks = L // C
    nt = L // tl

    def body(q_ref, k_ref, v_ref, gc_ref, gr_ref, b_ref, sh_ref, t_ref, sol_ref, do_ref,
             dqkv_ref, dgc_ref, dgr_ref, db_ref, ds_ref):
        @pl.when(pl.program_id(0) == 0)
        def _():
            ds_ref[...] = jnp.zeros_like(ds_ref)

        lane8 = lax.broadcasted_iota(jnp.int32, (C, 8), 1)
        sub8 = lax.broadcasted_iota(jnp.int32, (8, C), 0)
        rowid = lax.broadcasted_iota(jnp.int32, (C, 1), 0)
        causal, strict, _ = _dn_masks()
        rsum = lambda a: jnp.sum(a, axis=1, keepdims=True)

        def chunk(cc, _):
            c = ncb - 1 - cc
            rows = pl.ds(pl.multiple_of(c * C, C), C)
            grow_c = gr_ref[c]
            hs = lambda h: slice(h * 128, (h + 1) * 128)
            q = [q_ref[rows, hs(h)] for h in _HEADS]
            k = [k_ref[rows, hs(h)] for h in _HEADS]
            v = [v_ref[rows, hs(h)] for h in _HEADS]
            gc = [gc_ref[rows, h:h + 1] for h in _HEADS]
            gr = [grow_c[h:h + 1, :] for h in _HEADS]
            beta = [b_ref[rows, h:h + 1] for h in _HEADS]
            t = [t_ref[rows, h * C:(h + 1) * C] for h in _HEADS]
            sol = [sol_ref[rows, h * 256:(h + 1) * 256] for h in _HEADS]
            S = [sh_ref[c, hs(h), :] for h in _HEADS]
            dS = [ds_ref[hs(h), :] for h in _HEADS]
            dov = [do_ref[rows, hs(h)] for h in _HEADS]
            gam, eg, el, gl = _dn_decay(gc, gr, causal)
            kk = [_mm(k[h], k[h], _NT) for h in _HEADS]
            qk_raw = [_mm(q[h], k[h], _NT) for h in _HEADS]
            w = [sol[h][:, 128:] for h in _HEADS]
            kd = [k[h] * el[h] for h in _HEADS]
            vn = [sol[h][:, :128] - _mm(w[h], S[h]) for h in _HEADS]
            dvn = [_mm(qk_raw[h] * gam[h], dov[h], _TN) + _mm(kd[h], dS[h]) for h in _HEADS]
            dqd = [_mm(dov[h], S[h], _NT) for h in _HEADS]
            dqk = [jnp.where(causal, _mm(dov[h], vn[h], _NT), 0.0) for h in _HEADS]
            dkd = [_mm(vn[h], dS[h], _NT) for h in _HEADS]
            dgl = [jnp.sum(rsum(dS[h] * S[h]), axis=0, keepdims=True) for h in _HEADS]
            dw = [-_mm(dvn[h], S[h], _NT) for h in _HEADS]
            dSn = [dS[h] * gl[h] + _mm(q[h] * eg[h], dov[h], _TN) - _mm(w[h], dvn[h], _TN) for h in _HEADS]
            drhs = [_mm(t[h], jnp.concatenate([dvn[h], dw[h]], axis=1), _TN) for h in _HEADS]
            dn = [jnp.where(strict, -_mm(drhs[h], sol[h], _NT), 0.0) for h in _HEADS]
            dgc_all = jnp.zeros((C, 8), F32)
            db_all = jnp.zeros((C, 8), F32)
            dgr_all = jnp.zeros((8, C), F32)
            for h in _HEADS:
                drv, drk = drhs[h][:, :128], drhs[h][:, 128:]
                t2 = rsum(drk * k[h])
                x = dn[h] * gam[h]
                dbeta = rsum(drv * v[h]) + t2 * eg[h] + rsum(x * kk[h])
                dkk = x * beta[h]
                draw = dqk[h] * gam[h]
                mm_ = (dn[h] * beta[h] * kk[h] + dqk[h] * qk_raw[h]) * gam[h]
                deg = t2 * beta[h] + rsum(dqd[h] * q[h])
                r_ = rsum(dkd[h] * k[h]) * el[h]
                dglast = jnp.sum(r_, axis=0, keepdims=True) + dgl[h] * gl[h]
                dgc = rsum(mm_) + deg * eg[h] - r_ + jnp.where(rowid == C - 1, dglast, 0.0)
                dgr = -jnp.sum(mm_, axis=0, keepdims=True)
                dqkv_ref[rows, hs(h)] = _mm(draw, k[h]) + dqd[h] * eg[h]
                dqkv_ref[rows, hs(DN_H + h)] = (drk * (beta[h] * eg[h]) + _mm(dkk, k[h]) + _mm(dkk, k[h], _TN)
                                                + _mm(draw, q[h], _TN) + dkd[h] * el[h])
                dqkv_ref[rows, hs(2 * DN_H + h)] = drv * beta[h]
                ds_ref[hs(h), :] = dSn[h]
                dgc_all = dgc_all + jnp.where(lane8 == h, dgc, 0.0)
                db_all = db_all + jnp.where(lane8 == h, dbeta, 0.0)
                dgr_all = dgr_all + jnp.where(sub8 == h, dgr, 0.0)
            dgc_ref[rows, :] = dgc_all
            db_ref[rows, :] = db_all
            dgr_ref[c] = dgr_all
            return 0

        lax.fori_loop(0, ncb, chunk, 0)

    rev = lambda i: nt - 1 - i
    col = lambda b: pl.BlockSpec((tl, W), lambda i: (rev(i), b))
    rowblk = lambda w: pl.BlockSpec((tl, w), lambda i: (rev(i), 0))
    small = pl.BlockSpec((tl, 8), lambda i: (rev(i), 0))
    g3 = pl.BlockSpec((ncb, 8, C), lambda i: (rev(i), 0, 0))
    sd = jax.ShapeDtypeStruct
    return _call(body, (qkv, qkv, qkv, gcol, grow, bcol, shist, thist, solhist, do), name=name, grid=(nt,),
                 in_specs=[col(0), col(1), col(2), small, g3, small,
                           pl.BlockSpec((ncb, W, 128), lambda i: (rev(i), 0, 0)), rowblk(DN_H * C),
                           rowblk(DN_H * 256), col(0)],
                 out_specs=(rowblk(3 * W), small, g3, small),
                 out_shape=(sd((L, 3 * W), F32), sd((L, 8), F32), sd((nchunks, 8, C), F32), sd((L, 8), F32)),
                 scratch_shapes=[pltpu.VMEM((W, 128), F32)], sem=("arbitrary",), comm=comm)


def dn_out_fwd(o, rin, nw, name):
    L = o.shape[0]
    tl = _rtile(L, 256)

    def body(o_ref, z_ref, w_ref, y_ref):
        for hd in _HEADS:
            cs = slice(hd * 128, (hd + 1) * 128)
            ov = o_ref[:, cs]
            r = lax.rsqrt(jnp.mean(ov * ov, axis=-1, keepdims=True) + EPS)
            y_ref[:, cs] = (ov * r * w_ref[...] * _silu(z_ref[:, cs])).astype(BF16)

    return pl.pallas_call(
        body, name=name, grid=(L // tl,),
        in_specs=[pl.BlockSpec((tl, DN_W), lambda i: (i, 0)), pl.BlockSpec((tl, DN_W), lambda i: (i, 3)),
                  pl.BlockSpec((1, 128), lambda i: (0, 0))],
        out_specs=pl.BlockSpec((tl, DN_W), lambda i: (i, 0)), out_shape=jax.ShapeDtypeStruct((L, DN_W), BF16),
        compiler_params=_cparams("parallel"))(o, rin, nw)


def dn_out_bwd(dycat, o, rin, nw, name):
    L = o.shape[0]
    tl = _rtile(L, 256)

    def body(dy_ref, o_ref, z_ref, w_ref, do_ref, dz_ref, s_ref):
        @pl.when(pl.program_id(0) == 0)
        def _():
            s_ref[...] = jnp.zeros_like(s_ref)

        for hd in _HEADS:
            cs = slice(hd * 128, (hd + 1) * 128)
            ov, zv, d = o_ref[:, cs], z_ref[:, cs], dy_ref[:, cs]
            r = lax.rsqrt(jnp.mean(ov * ov, axis=-1, keepdims=True) + EPS)
            n = ov * r
            dnw = d * _silu(zv)
            dz_ref[:, cs] = (d * n * w_ref[...] * _dsilu(zv)).astype(BF16)
            dn = dnw * w_ref[...]
            do_ref[:, cs] = r * (dn - n * jnp.mean(dn * n, axis=-1, keepdims=True))
            s_ref[:, cs] += _fold8(dnw * n)

    own = pl.BlockSpec((tl, DN_W), lambda i: (i, 0))
    sd = jax.ShapeDtypeStruct
    return pl.pallas_call(
        body, name=name, grid=(L // tl,),
        in_specs=[own, own, pl.BlockSpec((tl, DN_W), lambda i: (i, 3)), pl.BlockSpec((1, 128), lambda i: (0, 0))],
        out_specs=(own, own, pl.BlockSpec((8, DN_W), lambda i: (0, 0))),
        out_shape=(sd((L, DN_W), F32), sd((L, DN_W), BF16), sd((8, DN_W), F32)),
        compiler_params=_cparams("arbitrary"))(dycat, o, rin, nw)


def dn_gates(a, beta_raw, a_log, dt_bias):
    L = a.shape[0]
    beta = jax.nn.sigmoid(beta_raw)
    g = -jnp.exp(a_log) * jax.nn.softplus(a + dt_bias)
    G = jnp.cumsum(g.reshape(L // DN_C, DN_C, DN_H), axis=1)
    pad = lambda t: jnp.pad(t, ((0, 0), (0, 8 - DN_H)))
    gcol = pad(G.reshape(L, DN_H))
    grow = jnp.pad(jnp.transpose(G, (0, 2, 1)), ((0, 0), (0, 8 - DN_H), (0, 0)))
    return gcol, grow, pad(beta)


def dn_block_fwd(rin, cw, a_log, dt_bias, out_norm, tag, comm=None):
    gates, gates_vjp = jax.vjp(dn_gates, rin[:, REC_A0:REC_A0 + DN_H], rin[:, REC_A0 + DN_H:REC_IN], a_log, dt_bias)
    qkv = dn_prep_fwd(rin, cw, tag + "_prep")
    (o, shist, thist, solhist), got = _with_comm(dn_chunk_fwd(qkv, *gates, tag + "_chunk", comm=comm), comm)
    yd = dn_out_fwd(o, rin, out_norm.reshape(1, 128), tag + "_onorm")
    return yd, (qkv, gates, gates_vjp, o, shist, thist, solhist), got


def dn_block_bwd(dyd, res, rin, cw, out_norm, tag, comm=None):
    qkv, gates, gates_vjp, o, shist, thist, solhist = res
    do, dz, nsum = dn_out_bwd(dyd, o, rin, out_norm.reshape(1, 128), tag + "_donorm")
    (dqkv, dgc, dgr, db), got = _with_comm(dn_chunk_bwd(qkv, *gates, shist, thist, solhist, do, tag + "_dchunk",
                                                        comm=comm), comm)
    da, dbraw, g_alog, g_dtb = gates_vjp((dgc, dgr, db))
    dx, csum = dn_prep_bwd(rin, cw, dqkv, tag + "_dprep")
    grads = dict(conv=csum.reshape(4, 8, DN_NT * 128).sum(axis=1), a_log=g_alog, dt_bias=g_dtb,
                 out_norm=nsum.sum(axis=0).reshape(DN_H, 128).sum(axis=0))
    return dx, dz, da, dbraw, grads, got


_HBM = pl.BlockSpec(memory_space=pltpu.HBM)


def _mesh_pos():
    xi, yi, ci = lax.axis_index("x"), lax.axis_index("y"), lax.axis_index("c")
    return xi, yi, ci, 4 * xi + 2 * yi + ci


def _peer(xi, yi, ci, k):
    px = 1 - xi if (k >> 2) & 1 else xi
    py = 1 - yi if (k >> 1) & 1 else yi
    pc = 1 - ci if k & 1 else ci
    return (px, py, pc), 4 * px + 2 * py + pc


def _exchange(xs, gather, name):
    n = len(xs)

    def body(*refs):
        copies = _comm_copies(refs[:n], refs[n:2 * n], *refs[2 * n:], gather)
        for cp in copies:
            cp.start()
        for cp in copies:
            cp.wait()

    return pl.pallas_call(
        body, name=name, in_specs=[_HBM] * n, out_specs=tuple([_HBM] * n),
        out_shape=_comm_out_shapes(xs), scratch_shapes=_comm_sems(n))(*xs)


def _comm_out_shapes(xs):
    return tuple(jax.ShapeDtypeStruct((N_DEV,) + x.shape[-2:], x.dtype) for x in xs)


def _comm_sems(n):
    return [pltpu.SemaphoreType.DMA((n * (N_DEV - 1),)), pltpu.SemaphoreType.DMA((n * (N_DEV - 1),)),
            pltpu.SemaphoreType.DMA((n,))]


def _comm_copies(x_refs, o_refs, send_sems, recv_sems, lsems, gather):
    xi, yi, ci, me = _mesh_pos()
    copies = []
    for t in range(len(x_refs)):
        src_of = (lambda lin, t=t: x_refs[t]) if gather else (lambda lin, t=t: x_refs[t].at[lin])
        copies.append(pltpu.make_async_copy(src_of(me), o_refs[t].at[me], lsems.at[t]))
        for k in range(1, N_DEV):
            peer, lin = _peer(xi, yi, ci, k)
            s = t * (N_DEV - 1) + k - 1
            copies.append(pltpu.make_async_remote_copy(
                src_ref=src_of(lin), dst_ref=o_refs[t].at[me], send_sem=send_sems.at[s],
                recv_sem=recv_sems.at[s], device_id=peer, device_id_type=pl.DeviceIdType.MESH))
    return copies


def _call(body, args, *, name, grid, in_specs, out_specs, out_shape, scratch_shapes=(), sem, comm=None):
    if comm is None:
        return pl.pallas_call(body, name=name, grid=grid, in_specs=in_specs, out_specs=out_specs,
                              out_shape=out_shape, scratch_shapes=list(scratch_shapes),
                              compiler_params=_cparams(*sem))(*args)
    xs, gather = comm
    n = len(xs)
    single = not isinstance(out_shape, (tuple, list))
    outs_shape = (out_shape,) if single else tuple(out_shape)
    outs_specs = (out_specs,) if single else tuple(out_specs)
    n_in, n_out, n_scr = len(in_specs), len(outs_shape), len(scratch_shapes)

    def body2(*refs):
        ins, cx = refs[:n_in], refs[n_in:n_in + n]
        outs = refs[n_in + n:n_in + n + n_out]
        co = refs[n_in + n + n_out:n_in + 2 * n + n_out]
        scr = refs[n_in + 2 * n + n_out:n_in + 2 * n + n_out + n_scr]
        sems = refs[n_in + 2 * n + n_out + n_scr:]
        first = functools.reduce(jnp.logical_and, [pl.program_id(a) == 0 for a in range(len(grid))])
        last = functools.reduce(jnp.logical_and, [pl.program_id(a) == grid[a] - 1 for a in range(len(grid))])

        @pl.when(first)
        def _():
            for cp in _comm_copies(cx, co, *sems, gather):
                cp.start()

        body(*ins, *outs, *scr)

        @pl.when(last)
        def _():
            for cp in _comm_copies(cx, co, *sems, gather):
                cp.wait()

    res = pl.pallas_call(
        body2, name=name, grid=grid, in_specs=list(in_specs) + [_HBM] * n,
        out_specs=outs_specs + tuple([_HBM] * n), out_shape=outs_shape + _comm_out_shapes(xs),
        scratch_shapes=list(scratch_shapes) + _comm_sems(n),
        compiler_params=_cparams(*(["arbitrary"] * len(grid))))(*args, *xs)
    main = res[0] if single else tuple(res[:n_out])
    return main, list(res[n_out:])


def all_gather(x, name):
    return _exchange([x], True, name)[0]


def all_gather_many(xs, name):
    return _exchange(xs, True, name)


def all_to_all_many(xs, name):
    return _exchange(xs, False, name)


def reduce_adamw(gsrc, w, m, v, name, comm=None):
    parts = list(gsrc) if isinstance(gsrc, (list, tuple)) else [gsrc]
    S, R0, C = parts[0].shape
    R = R0 * len(parts)
    tr = _rtile(R0, max(16, min(256, (4 << 20) // (S * C * 4) // 16 * 16)), 16 if R0 % 16 == 0 else 8)
    n0 = R0 // tr
    c1 = 1.0 - ADAM_B1 ** ADAM_STEP
    c2 = 1.0 - ADAM_B2 ** ADAM_STEP

    def body(*refs):
        g_refs = refs[:len(parts)]
        w_ref, m_ref, v_ref, go_ref, d_ref, mo_ref, vo_ref = refs[len(parts):]
        for p, g_ref in enumerate(g_refs):
            @pl.when(pl.program_id(0) // n0 == p)
            def _(g_ref=g_ref):
                acc = g_ref[0].astype(F32)
                for s in range(1, S):
                    acc = acc + g_ref[s].astype(F32)
                go_ref[...] = acc
        g = go_ref[...]
        mn = ADAM_B1 * m_ref[...] + (1.0 - ADAM_B1) * g
        vn = ADAM_B2 * v_ref[...] + (1.0 - ADAM_B2) * (g * g)
        mo_ref[...] = mn
        vo_ref[...] = vn
        d_ref[...] = -ADAM_LR * ((mn / c1) / (jnp.sqrt(vn / c2) + ADAM_EPS) + ADAM_WD * w_ref[...])

    big = pl.BlockSpec((tr, C), lambda i: (i, 0))
    o = jax.ShapeDtypeStruct((R, C), F32)
    part_spec = lambda p: pl.BlockSpec((S, tr, C), lambda i: (0, jnp.clip(i - p * n0, 0, n0 - 1), 0))
    return _call(body, (*parts, w, m, v), name=name, grid=(R // tr,),
                 in_specs=[part_spec(p) for p in range(len(parts))] + [big, big, big],
                 out_specs=(big, big, big, big), out_shape=(o, o, o, o), sem=("parallel",), comm=comm)


def _to_slabs(g, ax):
    shp = g.shape
    g = g.reshape(shp[:ax] + (N_DEV, shp[ax] // N_DEV) + shp[ax + 1:])
    return jnp.moveaxis(g, ax, 0).reshape(N_DEV, -1)


def _from_slabs(s, ax, shp):
    s = s.reshape((N_DEV,) + shp[:ax] + (shp[ax] // N_DEV,) + shp[ax + 1:])
    return jnp.moveaxis(s, 0, ax).reshape(shp)


def _pack_rows(flat, width, row_mult):
    n = flat.shape[-1]
    per = width * row_mult
    tot = -(-n // per) * per
    flat = jnp.pad(flat, [(0, 0)] * (flat.ndim - 1) + [(0, tot - n)])
    return flat.reshape(flat.shape[:-1] + (tot // width, width))


def _offsets(sizes):
    offs, o = [], 0
    for s in sizes:
        offs.append(o)
        o += s
    return offs


WEIGHTS = ['ada_w', 'ada_b', 'norm_mix', 'norm_ffn', 'attn_w_in', 'attn_q_norm_a', 'attn_k_norm_a', 'attn_q_norm_b',
           'attn_k_norm_b', 'attn_sinks', 'attn_w_out', 'rec_w_in', 's5_lambda_re', 's5_lambda_im', 's5_log_dt',
           's5_b_re', 's5_b_im', 's5_c_re', 's5_c_im', 's5_d', 's5_glu_w', 's5_glu_b', 'dn_conv', 'dn_a_log',
           'dn_dt_bias', 'dn_out_norm', 'rec_w_out', 'ffn_w_up', 'ffn_conv', 'ffn_w_down']
BIG = [('attn_w_in', (D, ATTN_IN // N_DEV)), ('attn_w_out', (D // N_DEV, D)), ('rec_w_in', (D // N_DEV, REC_PAD)),
       ('s5_glu_w', (S5_W // N_DEV, S5_W)), ('rec_w_out', (D // N_DEV, D)), ('ffn_w_up', (2 * D, 2 * D_FF // N_DEV)),
       ('ffn_w_down', (2 * D_FF // N_DEV, D))]


def _shard2d(name, t):
    if name == 'rec_w_in':
        return jnp.pad(t[0], ((0, 0), (0, REC_PAD - REC_IN)))
    return t.reshape((-1, t.shape[-1]))


def _cols_to_slabs(g, k=N_DEV):
    r, n = g.shape
    return jnp.transpose(g.reshape(r, k, n // k), (1, 0, 2))


def _slabs_to_cols(s):
    k, r, c_ = s.shape
    return jnp.transpose(s, (1, 0, 2)).reshape(r, k * c_)
SMALL_SHARDED = [('s5_d', 1, (1, S5_W)), ('s5_glu_b', 1, (1, S5_W)), ('dn_conv', 2, (1, 4, 2304)),
                 ('ffn_conv', 2, (2, 3, 2 * D_FF))]
REPLICATED = [('ada_b', (2, 6 * D)), ('norm_mix', (2, D)), ('norm_ffn', (2, D)), ('attn_q_norm_a', (1, HD)),
              ('attn_k_norm_a', (1, HD)), ('attn_q_norm_b', (1, HD)), ('attn_k_norm_b', (1, HD)),
              ('attn_sinks', (1, 8)), ('s5_lambda_re', (1, 16, 64)), ('s5_lambda_im', (1, 16, 64)),
              ('s5_log_dt', (1, 16)), ('s5_b_re', (1, 16, 64, 16)), ('s5_b_im', (1, 16, 64, 16)),
              ('s5_c_re', (1, 16, 16, 64)), ('s5_c_im', (1, 16, 16, 64)), ('dn_a_log', (1, DN_H)),
              ('dn_dt_bias', (1, DN_H)), ('dn_out_norm', (1, 128))]


def _numel(shp):
    return int(np.prod(shp))


def kernel(x, c, ada_w, ada_b, norm_mix, norm_ffn, attn_w_in, attn_q_norm_a, attn_k_norm_a, attn_q_norm_b, attn_k_norm_b, attn_sinks, attn_w_out, rec_w_in, s5_lambda_re, s5_lambda_im, s5_log_dt, s5_b_re, s5_b_im, s5_c_re, s5_c_im, s5_d, s5_glu_w, s5_glu_b, dn_conv, dn_a_log, dn_dt_bias, dn_out_norm, rec_w_out, ffn_w_up, ffn_conv, ffn_w_down, loss_target, m_ada_w, m_ada_b, m_norm_mix, m_norm_ffn, m_attn_w_in, m_attn_q_norm_a, m_attn_k_norm_a, m_attn_q_norm_b, m_attn_k_norm_b, m_attn_sinks, m_attn_w_out, m_rec_w_in, m_s5_lambda_re, m_s5_lambda_im, m_s5_log_dt, m_s5_b_re, m_s5_b_im, m_s5_c_re, m_s5_c_im, m_s5_d, m_s5_glu_w, m_s5_glu_b, m_dn_conv, m_dn_a_log, m_dn_dt_bias, m_dn_out_norm, m_rec_w_out, m_ffn_w_up, m_ffn_conv, m_ffn_w_down, v_ada_w, v_ada_b, v_norm_mix, v_norm_ffn, v_attn_w_in, v_attn_q_norm_a, v_attn_k_norm_a, v_attn_q_norm_b, v_attn_k_norm_b, v_attn_sinks, v_attn_w_out, v_rec_w_in, v_s5_lambda_re, v_s5_lambda_im, v_s5_log_dt, v_s5_b_re, v_s5_b_im, v_s5_c_re, v_s5_c_im, v_s5_d, v_s5_glu_w, v_s5_glu_b, v_dn_conv, v_dn_a_log, v_dn_dt_bias, v_dn_out_norm, v_rec_w_out, v_ffn_w_up, v_ffn_conv, v_ffn_w_down):
    loc = locals()
    W = {n: loc[n] for n in WEIGHTS}
    M = {n: loc["m_" + n] for n in WEIGHTS}
    V = {n: loc["v_" + n] for n in WEIGHTS}
    _, _, _, me = _mesh_pos()
    L = x.shape[1]
    x0, tgt = x[0], loss_target[0]

    small_in = jnp.concatenate([c.reshape(-1)] + [W[n].reshape(-1) for n, _, _ in SMALL_SHARDED])
    si, att_in_all = all_gather_many([_pack_rows(small_in, 1024, 8), attn_w_in[0].astype(BF16)], "gather_first")
    si = si.reshape(N_DEV, -1)
    c_all = si[:, :D]
    off = D
    small_full = {}
    for n, ax, shp in SMALL_SHARDED:
        k = _numel(shp) // N_DEV
        small_full[n] = _from_slabs(si[:, off:off + k], ax, shp)
        off += k

    cond_all = jax.nn.silu(c_all)
    modp = jnp.concatenate([matmul([(cond_all, ada_w[l].astype(BF16))], "nn", f"ada{l}") for l in range(2)], axis=0)
    modp_all = all_gather(modp, "gather_mod")
    mods = []
    for l in range(2):
        row = lax.dynamic_index_in_dim(modp_all, l * N_DEV + me, axis=1, keepdims=False)
        mod = row.reshape(1, 6 * D) + ada_b[l].reshape(1, 6 * D)
        mods.append([mod[:, i * D:(i + 1) * D] for i in range(6)])

    w_att_in = _slabs_to_cols(att_in_all)
    bf = lambda t: t.astype(BF16)
    ffn_shards = [[bf(ffn_w_up[l]), bf(ffn_w_down[l])] for l in range(2)]
    rec_shards = [bf(_shard2d('rec_w_in', rec_w_in)), bf(s5_glu_w[0]), bf(rec_w_out[0])]
    ffn_cw = [small_full['ffn_conv'][l] for l in range(2)]
    dn_cw = small_full['dn_conv'][0]
    s5_dskip, glu_b = small_full['s5_d'], small_full['s5_glu_b']
    row = lambda t: t.reshape(1, -1)

    sh1, sc1, g1, sh2, sc2, g2 = mods[0]
    h1 = gate_norm_fwd(x0, None, None, row(norm_mix[0]), sh1, sc1, "l0_norm1")
    wvec, sinkvec = attn_vectors(attn_q_norm_a[0], attn_k_norm_a[0], attn_q_norm_b[0], attn_k_norm_b[0], attn_sinks[0])
    y0, res_att, got = attention_block_fwd(
        h1, w_att_in, wvec, sinkvec, None, "att",
        comms={'swa': ([ffn_shards[0][0][:D // 2]], True), 1: ([ffn_shards[0][0][D // 2:]], True),
               4: (ffn_shards[0][1:], True), 16: ([bf(attn_w_out[0])], True)})
    w_att_out = got['w_out']
    split_up = lambda up_all: (_slabs_to_cols(up_all[:4]), _slabs_to_cols(up_all[4:]))
    w_up = [split_up(jnp.concatenate([got['swa'][0], got[1][0]], axis=1))]
    w_down = [got[4][0].reshape(D_FF, D)]
    x1, h2 = gate_norm_fwd(x0, y0, g1, row(norm_ffn[0]), sh2, sc2, "l0_norm2")
    f0, res_f0, got_rec = ffn_block_fwd(h2, w_up[0][0], w_up[0][1], ffn_cw[0], w_down[0], "ffn0",
                                        comm=(rec_shards, True))
    w_rec_in = rec_cols_permute(got_rec[0].reshape(D, REC_PAD))
    glu_w, w_rec_out = got_rec[1].reshape(S5_W, S5_W), got_rec[2].reshape(D, D)
    w_rec_out = jnp.concatenate([w_rec_out[S5_W:], w_rec_out[:S5_W]], axis=0)
    t1, tc1, tg1, t2, tc2, tg2 = mods[1]
    x2, h3 = gate_norm_fwd(x1, f0, g2, row(norm_mix[1]), t1, tc1, "l1_norm1")
    rin = matmul([(h3, w_rec_in)], "nn", "rec_in")
    s5p, s5p_vjp = jax.vjp(s5_params, s5_lambda_re[0], s5_lambda_im[0], s5_log_dt[0], s5_b_re[0], s5_b_im[0],
                           s5_c_re[0], s5_c_im[0])
    u = rin[:, REC_U0:REC_A0]
    yc, res_s5 = s5_block_fwd(u, s5p, s5_dskip, glu_w, glu_b, "s5")
    yd, res_dn, got_ffn1 = dn_block_fwd(rin, dn_cw, dn_a_log[0], dn_dt_bias[0], dn_out_norm[0], "dn",
                                        comm=(([ffn_shards[1][0][:D // 2], ffn_shards[1][1]], True),
                                              ([ffn_shards[1][0][D // 2:]], True)))
    w_up.append(split_up(jnp.concatenate([got_ffn1[0], got_ffn1[2]], axis=1)))
    w_down.append(got_ffn1[1].reshape(D_FF, D))
    ycat = jnp.concatenate([yd, yc], axis=1)
    y1 = matmul([(ycat, w_rec_out)], "nn", "rec_out")
    x3, h4 = gate_norm_fwd(x2, y1, tg1, row(norm_ffn[1]), t2, tc2, "l1_norm2")
    f1, res_f1, _ = ffn_block_fwd(h4, w_up[1][0], w_up[1][1], ffn_cw[1], w_down[1], "ffn1")
    dx4, df1, lsum = final_loss(x3, f1, tg2, tgt, "loss")

    G = {}
    d_tg2 = lsum[8:16].sum(axis=0)
    dh4, gf1, _ = ffn_block_bwd(df1, res_f1, w_up[1][0], w_up[1][1], ffn_cw[1], w_down[1], "ffn1")
    ffn_slabs = lambda g: [g['w_up'], g['w_down'].reshape(N_DEV, D_FF // N_DEV, D)]
    dx3, dy1, s = gate_norm_bwd(x3, y1, tg1, row(norm_ffn[1]), tc2, dx4, dh4, "l1_dnorm2")
    s = s.reshape(4, 8, D).sum(axis=1)
    d_tg1, d_nffn1, d_t2, d_tc2 = s[0], s[1] * (1.0 + tc2[0]), s[2], s[1] * norm_ffn[1]
    g_rec_out = matmul([(ycat, dy1)], "tn", "rec_out_dw", out_dtype=BF16)
    g_rec_out = jnp.concatenate([g_rec_out[DN_W:], g_rec_out[:DN_W]], axis=0).reshape(N_DEV, D // N_DEV, D)
    dycat = matmul([(dy1, w_rec_out)], "nt", "rec_out_dx")
    du, s5cot, gs5 = s5_block_bwd(dycat, res_s5, s5p, s5_dskip, glu_w, glu_b, "s5", dout_col=DN_W // S5_W)
    s5g = s5p_vjp(s5cot)
    dqkv, dz, da, dbraw, gdn, recv_ffn1 = dn_block_bwd(dycat, res_dn, rin, dn_cw, dn_out_norm[0], "dn",
                                                       comm=(ffn_slabs(gf1), False))
    d_rest = jnp.concatenate([du.astype(BF16), da.astype(BF16), dbraw.astype(BF16),
                              jnp.zeros((L, REC_PAD - REC_IN), BF16)], axis=1)
    drin = ((dqkv, 0), (dz, 3 * DN_W), (d_rest, REC_U0))
    g_rec_in = jnp.concatenate([matmul([(h3, p)], "tn", f"rec_in_dw{i}", out_dtype=BF16)
                                for i, (p, _) in enumerate(drin)], axis=1)
    g_rec_in = rec_cols_restore(g_rec_in).reshape(N_DEV, D // N_DEV, REC_PAD)
    g_glu = gs5['glu_w'].astype(BF16).reshape(N_DEV, S5_W // N_DEV, S5_W)
    dh3 = matmul([(p, w_rec_in[:, c0:c0 + p.shape[1]]) for p, c0 in drin], "nt", "rec_in_dx")
    dx2, df0, s = gate_norm_bwd(x2, f0, g2, row(norm_mix[1]), tc1, dx3, dh3, "l1_dnorm1")
    s = s.reshape(4, 8, D).sum(axis=1)
    d_g2, d_nmix1, d_t1, d_tc1 = s[0], s[1] * (1.0 + tc1[0]), s[2], s[1] * norm_mix[1]
    dh2, gf0, recv_rec = ffn_block_bwd(df0, res_f0, w_up[0][0], w_up[0][1], ffn_cw[0], w_down[0], "ffn0",
                                       comm=([g_rec_in, g_glu, g_rec_out], False))
    dx1, dy0, s = gate_norm_bwd(x1, y0, g1, row(norm_ffn[0]), sc2, dx2, dh2, "l0_dnorm2")
    s = s.reshape(4, 8, D).sum(axis=1)
    d_g1, d_nffn0, d_sh2, d_sc2 = s[0], s[1] * (1.0 + sc2[0]), s[2], s[1] * norm_ffn[0]
    dh1, gatt, got_b = attention_block_bwd(dy0, res_att, w_att_in, wvec, sinkvec, w_att_out, "att",
                                           comms={'swa': ([gf0['w_up'][:, :D // 2]], False),
                                                  16: ([gf0['w_up'][:, D // 2:]], False),
                                                  1: (ffn_slabs(gf0)[1:], False)},
                                           send_w_out_on=4)
    recv_ffn0 = [jnp.concatenate([got_b['swa'][0], got_b[16][0]], axis=1), got_b[1][0]]
    (grad_x, s), recv_w_in = gate_norm_bwd(x0, None, None, row(norm_mix[0]), sc1, dx1, dh1, "l0_dnorm1",
                                           comm=([_cols_to_slabs(gatt['w_in'])], False))
    recv_att = [recv_w_in[0], got_b[4][0]]
    s = s.reshape(4, 8, D).sum(axis=1)
    d_nmix0, d_sh1, d_sc1 = s[1] * (1.0 + sc1[0]), s[2], s[1] * norm_mix[0]
    dmod = jnp.stack([jnp.concatenate([d_sh1, d_sc1, d_g1, d_sh2, d_sc2, d_g2]),
                      jnp.concatenate([d_t1, d_tc1, d_tg1, d_t2, d_tc2, d_tg2])])

    P = {'ada_b': dmod, 'norm_mix': jnp.stack([d_nmix0, d_nmix1]), 'norm_ffn': jnp.stack([d_nffn0, d_nffn1]),
         'attn_q_norm_a': gatt['q_norm_a'], 'attn_k_norm_a': gatt['k_norm_a'], 'attn_q_norm_b': gatt['q_norm_b'],
         'attn_k_norm_b': gatt['k_norm_b'], 'attn_sinks': gatt['sinks'],
         's5_lambda_re': s5g[0], 's5_lambda_im': s5g[1], 's5_log_dt': s5g[2], 's5_b_re': s5g[3], 's5_b_im': s5g[4],
         's5_c_re': s5g[5], 's5_c_im': s5g[6], 'dn_a_log': gdn['a_log'], 'dn_dt_bias': gdn['dt_bias'],
         'dn_out_norm': gdn['out_norm'],
         's5_d': gs5['dskip'], 's5_glu_b': gs5['glu_b'], 'dn_conv': gdn['conv'],
         'ffn_conv': jnp.stack([gf0['conv'], gf1['conv']])}

    out = {k: {} for k in ("g", "d", "m", "v")}
    keys = ("g", "d", "m", "v")
    recv = {'attn_w_in': recv_att[0], 'attn_w_out': recv_att[1], 'rec_w_in': recv_rec[0], 's5_glu_w': recv_rec[1],
            'rec_w_out': recv_rec[2]}
    for n, gr_ in recv.items():
        res4 = reduce_adamw(gr_, _shard2d(n, W[n]), _shard2d(n, M[n]), _shard2d(n, V[n]), "adamw_" + n)
        for key, t in zip(keys, res4):
            out[key][n] = (t[:, :REC_IN] if n == 'rec_w_in' else t).reshape(W[n].shape)
    rep_sizes = [_numel(shp) for _, shp in REPLICATED]
    ss_sizes = [_numel(shp) for _, _, shp in SMALL_SHARDED]
    rep_offs = _offsets(rep_sizes + ss_sizes + [1])
    parts = [P[n].reshape(-1) for n, _ in REPLICATED] + [P[n].reshape(-1) for n, _, _ in SMALL_SHARDED]
    parts.append(lsum[0:8].sum().reshape(1))
    spack = _pack_rows(jnp.concatenate(parts), 1024, 8)
    flat2d = lambda t: t.reshape(-1, t.shape[-1])
    sall = None
    for n, idx in (('ffn_w_up', 0), ('ffn_w_down', 1)):
        comm = ([spack], True) if sall is None else None
        res4, got_s = _with_comm(reduce_adamw([recv_ffn0[idx], recv_ffn1[idx]], flat2d(W[n]), flat2d(M[n]),
                                              flat2d(V[n]), "adamw_" + n, comm=comm), comm)
        if got_s is not None:
            sall = got_s[0]
        for key, t in zip(keys, res4):
            out[key][n] = t.reshape(W[n].shape)
    n_rest = sum(ss_sizes) + 1
    pk = lambda d: _pack_rows(jnp.concatenate([d[n].reshape(-1) for n, _ in REPLICATED]
                                              + [jnp.zeros((n_rest,), F32)]), 1024, 8)
    sg, sd_, sm, sv = [t.reshape(-1) for t in reduce_adamw(sall, pk(W), pk(M), pk(V), "adamw_small")]
    loss = 0.5 * sg[rep_offs[-1]] / D

    dmod_all = sall.reshape(N_DEV, -1)[:, :2 * 6 * D].reshape(N_DEV, 2, 6 * D)
    dmod_mine = lax.dynamic_slice_in_dim(dmod_all, me * (6 * D // N_DEV), 6 * D // N_DEV, axis=2)
    g_ada = [matmul([(cond_all, dmod_mine[:, l])], "tn", f"ada{l}_dw")[None] for l in range(2)]
    ada2d = lambda t: t.reshape(2 * D, 6 * D // N_DEV)
    for key, t in zip(("g", "d", "m", "v"), reduce_adamw(g_ada, ada2d(ada_w), ada2d(m_ada_w),
                                                          ada2d(v_ada_w), "adamw_ada_w")):
        out[key]['ada_w'] = t.reshape(ada_w.shape)
    own = []
    for (n, ax, shp), o in zip(SMALL_SHARDED, rep_offs[len(REPLICATED):]):
        slabs = _to_slabs(sg[o:o + _numel(shp)].reshape(shp), ax)
        own.append(lax.dynamic_index_in_dim(slabs, me, axis=0, keepdims=False))
    own_names = [n for n, _, _ in SMALL_SHARDED]
    pk = lambda d: _pack_rows(jnp.concatenate([d[n].reshape(-1) for n in own_names]), 1024, 8)
    og, od, om, ov = [t.reshape(-1) for t in reduce_adamw(_pack_rows(jnp.concatenate(own), 1024, 8)[None],
                                                          pk(W), pk(M), pk(V), "adamw_own")]

    def unpack(names_shapes, bufs):
        o = 0
        for n, shp in names_shapes:
            k = _numel(shp)
            for key, buf in zip(("g", "d", "m", "v"), bufs):
                out[key][n] = buf[o:o + k].reshape(shp)
            o += k

    unpack(REPLICATED, (sg, sd_, sm, sv))
    unpack([(n, W[n].shape) for n in own_names], (og, od, om, ov))
    return (loss, grad_x[None], *[out["g"][n] for n in WEIGHTS], *[out["d"][n] for n in WEIGHTS],
            *[out["m"][n] for n in WEIGHTS], *[out["v"][n] for n in WEIGHTS])
```

```python
import functools
import math

import numpy as np
import jax
import jax.numpy as jnp
from jax import lax
from jax.experimental import pallas as pl
from jax.experimental.pallas import tpu as pltpu

F32 = jnp.float32
BF16 = jnp.bfloat16

N_DEV = 8
D = 1024
HD = 64
BLK = 128
ATTN_IN = 2304
CB = ATTN_IN // 128
B_BRANCHES = ((128, 1), (512, 4), (2048, 16))
S5_W = 256
S5_P = 1024
DN_H = 6
DN_DK = 128
DN_C = 64
REC_IN = 3340
REC_PAD = 3456
D_FF = 2816
EPS = 1e-6
ADAM_LR, ADAM_B1, ADAM_B2, ADAM_EPS, ADAM_WD, ADAM_STEP = 0.001, 0.9, 0.999, 1e-8, 0.01, 10
VMEM_LIMIT = 48 * 1024 * 1024

ALIBI = np.asarray(2.0 ** (-8.0 * np.arange(1, 17) / 16), dtype=np.float32)


def _cparams(*sem):
    return pltpu.CompilerParams(dimension_semantics=tuple(sem), vmem_limit_bytes=VMEM_LIMIT)


def _tile(n, target):
    if n <= target:
        return n
    best = None
    for t in range(128, target + 1, 128):
        if n % t == 0:
            best = t
    assert best is not None, (n, target)
    return best


def _rtile(n, target, mult=8):
    if n <= target:
        return n
    best = None
    for t in range(mult, target + 1, mult):
        if n % t == 0:
            best = t
    assert best is not None, (n, target)
    return best


def _fold8(x):
    r, c = x.shape
    return x.reshape(r // 8, 8, c).sum(axis=0)


def _sigmoid(x):
    return 1.0 / (1.0 + jnp.exp(-x))


_DIMS = {"nn": (((1,), (0,)), ((), ())), "nt": (((1,), (1,)), ((), ())), "tn": (((0,), (0,)), ((), ()))}


MM_FULL_K = 3584


MM_VMEM_BUDGET = 40 << 20


def matmul(pairs, mode, name, out_dtype=F32, tm=1024, tn=1536, tk=1024):
    a0, b0 = pairs[0]
    if mode == "nn":
        (M, K), N = a0.shape, b0.shape[1]
    elif mode == "nt":
        (M, K), N = a0.shape, b0.shape[0]
    else:
        (K, M), N = a0.shape, b0.shape[1]
        tm = 1536
    tn = _tile(N, tn)
    tk = K if K <= MM_FULL_K else _tile(K, tk)
    nk = K // tk
    npair = len(pairs)
    dims = _DIMS[mode]
    kdim = 0 if mode == "tn" else 1
    tks = [a.shape[kdim] for a, _ in pairs]
    assert all(t == K for t in tks) or (nk == 1 and max(tks) <= MM_FULL_K), tks
    if nk > 1:
        tks = [tk] * npair

    def planned(tm_):
        ab = sum(tm_ * t * a.dtype.itemsize + t * tn * b.dtype.itemsize for (a, b), t in zip(pairs, tks))
        return 2 * ab + 2 * tm_ * tn * jnp.dtype(out_dtype).itemsize + (tm_ * tn * 4 if nk > 1 else 0)

    while True:
        tm_try = _rtile(M, tm) if M % 128 else _tile(M, tm)
        if planned(tm_try) <= MM_VMEM_BUDGET or tm <= 128:
            break
        tm //= 2
    tm = tm_try

    def body(*refs):
        o_ref = refs[2 * npair]
        tot = None
        for p in range(npair):
            part = lax.dot_general(refs[2 * p][...].astype(BF16), refs[2 * p + 1][...].astype(BF16),
                                   dims, preferred_element_type=F32)
            tot = part if tot is None else tot + part
        if nk == 1:
            o_ref[...] = tot.astype(o_ref.dtype)
            return
        acc_ref = refs[2 * npair + 1]
        k = pl.program_id(2)

        @pl.when(k == 0)
        def _():
            acc_ref[...] = tot

        @pl.when(k > 0)
        def _():
            acc_ref[...] += tot

        @pl.when(k == nk - 1)
        def _():
            o_ref[...] = acc_ref[...].astype(o_ref.dtype)

    def specs(t):
        if mode == "nn":
            return [pl.BlockSpec((tm, t), lambda j, i, k: (i, k)), pl.BlockSpec((t, tn), lambda j, i, k: (k, j))]
        if mode == "nt":
            return [pl.BlockSpec((tm, t), lambda j, i, k: (i, k)), pl.BlockSpec((tn, t), lambda j, i, k: (j, k))]
        return [pl.BlockSpec((t, tm), lambda j, i, k: (k, i)), pl.BlockSpec((t, tn), lambda j, i, k: (k, j))]

    flat = [t for pr in pairs for t in pr]
    return pl.pallas_call(
        body, name=name, grid=(N // tn, M // tm, nk),
        in_specs=[s for t in tks for s in specs(t)],
        out_specs=pl.BlockSpec((tm, tn), lambda j, i, k: (i, j)),
        out_shape=jax.ShapeDtypeStruct((M, N), out_dtype),
        scratch_shapes=[pltpu.VMEM((tm, tn), F32)] if nk > 1 else [],
        compiler_params=_cparams("parallel", "parallel", "arbitrary"),
    )(*flat)


def gate_norm_fwd(x, y, gate, nw, sh, sc, name):
    L, C = x.shape
    tl = _rtile(L, 512)
    has_gate = y is not None

    def body(*refs):
        if has_gate:
            x_ref, y_ref, g_ref, nw_ref, sh_ref, sc_ref, xn_ref, h_ref = refs
            xn = x_ref[...] + g_ref[...] * y_ref[...]
            xn_ref[...] = xn
        else:
            x_ref, nw_ref, sh_ref, sc_ref, h_ref = refs
            xn = x_ref[...]
        r = lax.rsqrt(jnp.mean(xn * xn, axis=-1, keepdims=True) + EPS)
        h = (xn * r * nw_ref[...]) * (1.0 + sc_ref[...]) + sh_ref[...]
        h_ref[...] = h.astype(BF16)

    big = pl.BlockSpec((tl, C), lambda i: (i, 0))
    vec = pl.BlockSpec((1, C), lambda i: (0, 0))
    if has_gate:
        ins, in_specs = (x, y, gate, nw, sh, sc), [big, big, vec, vec, vec, vec]
        out_shape = (jax.ShapeDtypeStruct((L, C), F32), jax.ShapeDtypeStruct((L, C), BF16))
        out_specs = (big, big)
    else:
        ins, in_specs = (x, nw, sh, sc), [big, vec, vec, vec]
        out_shape = jax.ShapeDtypeStruct((L, C), BF16)
        out_specs = big
    return pl.pallas_call(body, name=name, grid=(L // tl,), in_specs=in_specs, out_specs=out_specs,
                          out_shape=out_shape, compiler_params=_cparams("parallel"))(*ins)


def gate_norm_bwd(xn, y, gate, nw, sc, dxn_direct, dh, name, comm=None):
    L, C = xn.shape
    tl = _rtile(L, 256)
    has_gate = y is not None
    has_direct = dxn_direct is not None

    def body(*refs):
        refs = list(refs)
        xn_ref = refs.pop(0)
        y_ref = refs.pop(0) if has_gate else None
        g_ref = refs.pop(0) if has_gate else None
        nw_ref = refs.pop(0)
        sc_ref = refs.pop(0)
        dd_ref = refs.pop(0) if has_direct else None
        dh_ref = refs.pop(0)
        dxn_ref = refs.pop(0)
        dy_ref = refs.pop(0) if has_gate else None
        sums_ref = refs.pop(0)

        @pl.when(pl.program_id(0) == 0)
        def _():
            sums_ref[...] = jnp.zeros_like(sums_ref)

        xv = xn_ref[...]
        dh_v = dh_ref[...]
        r = lax.rsqrt(jnp.mean(xv * xv, axis=-1, keepdims=True) + EPS)
        n = xv * r
        a = nw_ref[...] * (1.0 + sc_ref[...])
        dn = dh_v * a
        dx = r * (dn - n * jnp.mean(dn * n, axis=-1, keepdims=True))
        if has_direct:
            dx = dx + dd_ref[...]
        dxn_ref[...] = dx
        sums_ref[8:16, :] += _fold8(dh_v * n)
        sums_ref[16:24, :] += _fold8(dh_v)
        if has_gate:
            dy_ref[...] = (dx * g_ref[...]).astype(BF16)
            sums_ref[0:8, :] += _fold8(dx * y_ref[...])

    big = pl.BlockSpec((tl, C), lambda i: (i, 0))
    vec = pl.BlockSpec((1, C), lambda i: (0, 0))
    ins, in_specs = [xn], [big]
    if has_gate:
        ins += [y, gate]
        in_specs += [big, vec]
    ins += [nw, sc]
    in_specs += [vec, vec]
    if has_direct:
        ins.append(dxn_direct)
        in_specs.append(big)
    ins.append(dh)
    in_specs.append(big)
    out_shape = [jax.ShapeDtypeStruct((L, C), F32)]
    out_specs = [big]
    if has_gate:
        out_shape.append(jax.ShapeDtypeStruct((L, C), BF16))
        out_specs.append(big)
    out_shape.append(jax.ShapeDtypeStruct((32, C), F32))
    out_specs.append(pl.BlockSpec((32, C), lambda i: (0, 0)))
    return _call(body, ins, name=name, grid=(L // tl,), in_specs=in_specs, out_specs=tuple(out_specs),
                 out_shape=tuple(out_shape), sem=("arbitrary",), comm=comm)


def final_loss(x, f, gate, target, name):
    L, C = x.shape
    tl = _rtile(L, 256)

    def body(x_ref, f_ref, g_ref, t_ref, dy_ref, df_ref, sums_ref):
        @pl.when(pl.program_id(0) == 0)
        def _():
            sums_ref[...] = jnp.zeros_like(sums_ref)

        fv = f_ref[...]
        err = x_ref[...] + g_ref[...] * fv - t_ref[...]
        dy = err * (1.0 / C)
        dy_ref[...] = dy
        df_ref[...] = (dy * g_ref[...]).astype(BF16)
        sums_ref[0:8, :] += _fold8(err * err)
        sums_ref[8:16, :] += _fold8(dy * fv)

    big = pl.BlockSpec((tl, C), lambda i: (i, 0))
    vec = pl.BlockSpec((1, C), lambda i: (0, 0))
    return pl.pallas_call(
        body, name=name, grid=(L // tl,), in_specs=[big, big, vec, big],
        out_specs=(big, big, pl.BlockSpec((16, C), lambda i: (0, 0))),
        out_shape=(jax.ShapeDtypeStruct((L, C), F32), jax.ShapeDtypeStruct((L, C), BF16),
                   jax.ShapeDtypeStruct((16, C), F32)),
        compiler_params=_cparams("arbitrary"))(x, f, gate, target)


def _seg_ones(seg):
    r = lax.broadcasted_iota(jnp.int32, (128, 128), 0) // seg
    c = lax.broadcasted_iota(jnp.int32, (128, 128), 1) // seg
    return (r == c).astype(BF16)


def _segsum(t, ones):
    hi = t.astype(BF16)
    lo = (t - hi.astype(F32)).astype(BF16)
    return (jnp.dot(hi, ones, preferred_element_type=F32) + jnp.dot(lo, ones, preferred_element_type=F32))


_NORMED_TILES = tuple(list(range(0, 5)) + list(range(6, 14)))


DIL = (4, 16)
B_COLS0, B_W = 768, 1536
DIL_TL = 256


def _to_dilated(scr_ref, out_ref, d, cast=None):
    nj, tl, _ = scr_ref.shape
    for r in range(d):
        for j in range(nj):
            piece = scr_ref[j, pl.ds(r, tl // d, stride=d), :]
            c0 = (r * nj + j) * 128
            out_ref[:, c0:c0 + 128] = piece if cast is None else piece.astype(cast)


def _from_dilated(in_ref, scr_ref, d):
    nj, tl, _ = scr_ref.shape
    for r in range(d):
        for j in range(nj):
            c0 = (r * nj + j) * 128
            scr_ref[j, pl.ds(r, tl // d, stride=d), :] = in_ref[:, c0:c0 + 128]


def _dil_spec(tl, d, width):
    return pl.BlockSpec((tl // d, d * width), lambda i: (i, 0))


def qknorm_fwd(qkv, wvec, name):
    L, C = qkv.shape
    tl = DIL_TL

    def body(x_ref, w_ref, o_ref, o4_ref, o16_ref, scr_ref):
        ones = _seg_ones(HD)
        for t in range(CB):
            cs = slice(t * 128, (t + 1) * 128)
            x = x_ref[:, cs]
            if t in _NORMED_TILES:
                ms = _segsum(x * x, ones) * (1.0 / HD)
                x = x * lax.rsqrt(ms + EPS) * w_ref[:, cs]
            o_ref[:, cs] = x.astype(BF16)
            if t * 128 >= B_COLS0:
                scr_ref[t - B_COLS0 // 128] = x
        _to_dilated(scr_ref, o4_ref, 4, BF16)
        _to_dilated(scr_ref, o16_ref, 16, BF16)

    return pl.pallas_call(
        body, name=name, grid=(L // tl,),
        in_specs=[pl.BlockSpec((tl, C), lambda i: (i, 0)), pl.BlockSpec((1, C), lambda i: (0, 0))],
        out_specs=(pl.BlockSpec((tl, C), lambda i: (i, 0)), _dil_spec(tl, 4, B_W), _dil_spec(tl, 16, B_W)),
        out_shape=(jax.ShapeDtypeStruct((L, C), BF16), jax.ShapeDtypeStruct((L // 4, 4 * B_W), BF16),
                   jax.ShapeDtypeStruct((L // 16, 16 * B_W), BF16)),
        scratch_shapes=[pltpu.VMEM((B_W // 128, tl, 128), F32)], compiler_params=_cparams("parallel"))(qkv, wvec)


def qknorm_bwd(qkv, wvec, d_a, d_b, name):
    L, C = qkv.shape
    tl = DIL_TL

    def body(x_ref, w_ref, dqa, dka, dva, q1, k1, v1, q4, k4, v4, q16, k16, v16, dx_ref, sums_ref,
             dy_ref, s4_ref, s16_ref):
        @pl.when(pl.program_id(0) == 0)
        def _():
            sums_ref[...] = jnp.zeros_like(sums_ref)

        dy_ref[:, 0:512] = dqa[...]
        for off, ref in ((512, dka), (640, dva)):
            for g in range(2):
                acc = ref[:, g * 256:g * 256 + HD]
                for h in range(1, 4):
                    acc = acc + ref[:, g * 256 + h * HD:g * 256 + (h + 1) * HD]
                dy_ref[:, off + g * HD:off + (g + 1) * HD] = acc
        for off, r1, r4, r16 in ((768, q1, q4, q16), (1280, k1, k4, k16), (1792, v1, v4, v16)):
            _from_dilated(r4, s4_ref, 4)
            _from_dilated(r16, s16_ref, 16)
            for j in range(4):
                dy_ref[:, off + j * 128:off + (j + 1) * 128] = r1[:, j * 128:(j + 1) * 128] + s4_ref[j] + s16_ref[j]

        ones = _seg_ones(HD)
        for t in range(CB):
            cs = slice(t * 128, (t + 1) * 128)
            d = dy_ref[:, cs]
            if t in _NORMED_TILES:
                x = x_ref[:, cs]
                r = lax.rsqrt(_segsum(x * x, ones) * (1.0 / HD) + EPS)
                n = x * r
                dn = d * w_ref[:, cs]
                dx_ref[:, cs] = (r * (dn - n * (_segsum(dn * n, ones) * (1.0 / HD)))).astype(BF16)
                sums_ref[:, cs] += _fold8(d * n)
            else:
                dx_ref[:, cs] = d.astype(BF16)

    big = pl.BlockSpec((tl, C), lambda i: (i, 0))
    p512 = pl.BlockSpec((tl, 512), lambda i: (i, 0))
    return pl.pallas_call(
        body, name=name, grid=(L // tl,),
        in_specs=[big, pl.BlockSpec((1, C), lambda i: (0, 0))] + [p512] * 6 + [_dil_spec(tl, 4, 512)] * 3
        + [_dil_spec(tl, 16, 512)] * 3,
        out_specs=(big, pl.BlockSpec((8, C), lambda i: (0, 0))),
        out_shape=(jax.ShapeDtypeStruct((L, C), BF16), jax.ShapeDtypeStruct((8, C), F32)),
        scratch_shapes=[pltpu.VMEM((tl, C), F32), pltpu.VMEM((4, tl, 128), F32), pltpu.VMEM((4, tl, 128), F32)],
        compiler_params=_cparams("arbitrary"))(qkv, wvec, *d_a, *d_b[0], *d_b[1], *d_b[2])


def _attn_biases(t, slopes, step, maxdist):
    qi = lax.broadcasted_iota(jnp.int32, (BLK, 2 * BLK), 0)
    sj = lax.broadcasted_iota(jnp.int32, (BLK, 2 * BLK), 1)
    dist = BLK + qi - sj
    valid = (dist >= 0) & (dist <= maxdist)
    distf = (step * dist).astype(F32)
    inner = [jnp.where(valid, (-sl) * distf, -jnp.inf) for sl in slopes]
    first = [jnp.where((t > 0) | (sj >= BLK), b, -jnp.inf) for b in inner]
    return inner, first


def _attn_scores(q, kw, bias):
    return lax.dot_general(q, kw, (((1,), (1,)), ((), ())), preferred_element_type=F32) + bias


ATT_NQ = 8


def _attn_operands(nq, hp, gqa, q_ref, kh_ref, kc_ref, vh_ref, vc_ref):
    ops = []
    for b in range(nq):
        rows = slice(b * BLK, (b + 1) * BLK)
        prev = slice((b - 1) * BLK, b * BLK)
        for e in range(2):
            cs = slice(e * HD, (e + 1) * HD)
            if gqa:
                ksel = lambda ref, r: jnp.where(hp >= 2, ref[r, 64:128], ref[r, 0:64])
            else:
                ksel = lambda ref, r, cs=cs: ref[r, cs]
            kprev = ksel(kh_ref, slice(0, BLK)) if b == 0 else ksel(kc_ref, prev)
            vprev = ksel(vh_ref, slice(0, BLK)) if b == 0 else ksel(vc_ref, prev)
            ops.append((b, e, rows, cs, q_ref[rows, cs] * (HD ** -0.5),
                        jnp.concatenate([kprev, ksel(kc_ref, rows)], axis=0),
                        jnp.concatenate([vprev, ksel(vc_ref, rows)], axis=0)))
    return ops


def _attn_specs(cb, q_off, k_off, v_off, gqa):
    kcol = (lambda r, hp: r * cb + k_off) if gqa else (lambda r, hp: r * cb + k_off + hp)
    vcol = (lambda r, hp: r * cb + v_off) if gqa else (lambda r, hp: r * cb + v_off + hp)
    return kcol, vcol


def attn_fwd(X, d, q_off, k_off, v_off, gqa, slope0, maxdist, name, comm=None):
    Ls = X.shape[0]
    nq = min(ATT_NQ, Ls // BLK)
    TQ = nq * BLK
    nt = Ls // TQ
    slopes = jnp.asarray(ALIBI)

    def body(sl_ref, q_ref, kh_ref, kc_ref, vh_ref, vc_ref, o_ref, lse_ref):
        hp, t = pl.program_id(1), pl.program_id(2)
        ops = _attn_operands(nq, hp, gqa, q_ref, kh_ref, kc_ref, vh_ref, vc_ref)
        inner, first = _attn_biases(t, [sl_ref[slope0 + 2 * hp + e] for e in range(2)], d, maxdist)
        s = [_attn_scores(q, kw, first[e] if b == 0 else inner[e]) for (b, e, rows, cs, q, kw, vw) in ops]
        m = [jnp.max(x, axis=-1, keepdims=True) for x in s]
        p = [jnp.exp(x - mm) for x, mm in zip(s, m)]
        l = [jnp.sum(x, axis=-1, keepdims=True) for x in p]
        o = [jnp.dot(x.astype(BF16), op[6], preferred_element_type=F32) / ll for x, op, ll in zip(p, ops, l)]
        for (b, e, rows, cs, q, kw, vw), oo, mm, ll in zip(ops, o, m, l):
            o_ref[rows, cs] = oo
            lse_ref[rows, cs] = jnp.broadcast_to(mm + jnp.log(ll), (BLK, HD))

    cb = X.shape[1] // (d * 128)
    kcol, vcol = _attn_specs(cb, q_off, k_off, v_off, gqa)
    tile, blk = (TQ, 128), (BLK, 128)
    halo = lambda t: jnp.maximum(t * nq - 1, 0)
    in_specs = [
        pl.BlockSpec(memory_space=pltpu.SMEM),
        pl.BlockSpec(tile, lambda r, hp, t: (t, r * cb + q_off + hp)),
        pl.BlockSpec(blk, lambda r, hp, t: (halo(t), kcol(r, hp))),
        pl.BlockSpec(tile, lambda r, hp, t: (t, kcol(r, hp))),
        pl.BlockSpec(blk, lambda r, hp, t: (halo(t), vcol(r, hp))),
        pl.BlockSpec(tile, lambda r, hp, t: (t, vcol(r, hp))),
    ]
    out_spec = pl.BlockSpec(tile, lambda r, hp, t: (t, r * 4 + hp))
    out = jax.ShapeDtypeStruct((Ls, d * 512), F32)
    return _call(body, (slopes, X, X, X, X, X), name=name, grid=(d, 4, nt), in_specs=in_specs,
                 out_specs=(out_spec, out_spec), out_shape=(out, out),
                 sem=("parallel", "parallel", "arbitrary"), comm=comm)


def attn_bwd(X, o, lse, do, dlse, d, q_off, k_off, v_off, gqa, slope0, maxdist, name, comm=None):
    Ls = X.shape[0]
    slopes = jnp.asarray(ALIBI)

    nq = min(ATT_NQ, Ls // BLK)
    TQ = nq * BLK
    nt = Ls // TQ
    nt_dims, tn_dims = (((1,), (1,)), ((), ())), (((0,), (0,)), ((), ()))

    def body(sl_ref, q_ref, kh_ref, kc_ref, vh_ref, vc_ref, o_ref, lse_ref, do_ref, dlse_ref,
             dq_ref, dk_ref, dv_ref, ak_ref, av_ref, pk_ref, pv_ref):
        hp, t = pl.program_id(1), pl.program_id(2)

        @pl.when(t == 0)
        def _():
            pk_ref[...] = jnp.zeros_like(pk_ref)
            pv_ref[...] = jnp.zeros_like(pv_ref)

        @pl.when(t < nt)
        def _():
            ops = _attn_operands(nq, hp, gqa, q_ref, kh_ref, kc_ref, vh_ref, vc_ref)
            inner, first = _attn_biases(t, [sl_ref[slope0 + 2 * hp + e] for e in range(2)], d, maxdist)
            sv = [_attn_scores(q, kw, first[e] if b == 0 else inner[e]) for (b, e, rows, cs, q, kw, vw) in ops]
            p = [jnp.exp(s - lse_ref[op[2], op[1] * HD:op[1] * HD + 1]) for s, op in zip(sv, ops)]
            dov = [do_ref[op[2], op[3]] for op in ops]
            delta = [jnp.sum(dd * o_ref[op[2], op[3]], axis=-1, keepdims=True) for dd, op in zip(dov, ops)]
            dob = [dd.astype(BF16) for dd in dov]
            dp = [lax.dot_general(dd, op[6], nt_dims, preferred_element_type=F32) for dd, op in zip(dob, ops)]
            ds = [(pp * (x - dl + dlse_ref[op[2], op[1] * HD:op[1] * HD + 1])).astype(BF16)
                  for pp, x, dl, op in zip(p, dp, delta, ops)]
            dq = [jnp.dot(x, op[5], preferred_element_type=F32) * (HD ** -0.5) for x, op in zip(ds, ops)]
            dkw = [lax.dot_general(x, op[4], tn_dims, preferred_element_type=F32) for x, op in zip(ds, ops)]
            dvw = [lax.dot_general(pp.astype(BF16), dd, tn_dims, preferred_element_type=F32)
                   for pp, dd in zip(p, dob)]
            ak_ref[...] = jnp.zeros_like(ak_ref)
            av_ref[...] = jnp.zeros_like(av_ref)
            for (b, e, rows, cs, q, kw, vw), x, yk, yv in zip(ops, dq, dkw, dvw):
                dq_ref[rows, cs] = x
                ak_ref[b * BLK:(b + 2) * BLK, cs] += yk
                av_ref[b * BLK:(b + 2) * BLK, cs] += yv
            if nt == 1:
                dk_ref[...] = ak_ref[BLK:, :]
                dv_ref[...] = av_ref[BLK:, :]
                return
            last = slice(TQ - BLK, TQ)
            dk_ref[...] = pk_ref[...]
            dv_ref[...] = pv_ref[...]
            dk_ref[last, :] += ak_ref[0:BLK, :]
            dv_ref[last, :] += av_ref[0:BLK, :]
            pk_ref[...] = ak_ref[BLK:, :]
            pv_ref[...] = av_ref[BLK:, :]

        @pl.when(t == nt)
        def _():
            dk_ref[...] = pk_ref[...]
            dv_ref[...] = pv_ref[...]

    cb = X.shape[1] // (d * 128)
    kcol, vcol = _attn_specs(cb, q_off, k_off, v_off, gqa)
    tile, blk = (TQ, 128), (BLK, 128)
    cur = lambda t: jnp.minimum(t, nt - 1)
    halo = lambda t: jnp.maximum(cur(t) * nq - 1, 0)
    ospec = pl.BlockSpec(tile, lambda r, hp, t: (cur(t), r * 4 + hp))
    in_specs = [
        pl.BlockSpec(memory_space=pltpu.SMEM),
        pl.BlockSpec(tile, lambda r, hp, t: (cur(t), r * cb + q_off + hp)),
        pl.BlockSpec(blk, lambda r, hp, t: (halo(t), kcol(r, hp))),
        pl.BlockSpec(tile, lambda r, hp, t: (cur(t), kcol(r, hp))),
        pl.BlockSpec(blk, lambda r, hp, t: (halo(t), vcol(r, hp))),
        pl.BlockSpec(tile, lambda r, hp, t: (cur(t), vcol(r, hp))),
        ospec, ospec, ospec, ospec,
    ]
    shifted = pl.BlockSpec(tile, lambda r, hp, t: (jnp.maximum(t - 1, 0), r * 4 + hp))
    out = jax.ShapeDtypeStruct((Ls, d * 512), F32)
    return _call(body, (slopes, X, X, X, X, X, o, lse, do, dlse), name=name, grid=(d, 4, nt + 1 if nt > 1 else 1),
                 in_specs=in_specs, out_specs=(ospec, shifted, shifted), out_shape=(out, out, out),
                 scratch_shapes=[pltpu.VMEM((TQ + BLK, 128), F32), pltpu.VMEM((TQ + BLK, 128), F32),
                                 pltpu.VMEM((TQ, 128), F32), pltpu.VMEM((TQ, 128), F32)],
                 sem=("parallel", "parallel", "arbitrary"), comm=comm)


def attn_merge_fwd(oa, la, sink, obs, lbs, name):
    L = oa.shape[0]
    tl = DIL_TL

    def body(oa_ref, la_ref, sk_ref, o1, o4, o16, l1, l4, l16, m_ref, so4, so16, sl4, sl16):
        m_ref[:, 0:512] = (oa_ref[...] * _sigmoid(la_ref[...] - sk_ref[...])).astype(BF16)
        for src, dst, d in ((o4, so4, 4), (o16, so16, 16), (l4, sl4, 4), (l16, sl16, 16)):
            _from_dilated(src, dst, d)
        for j in range(4):
            cs = slice(j * 128, (j + 1) * 128)
            a, b, c = l1[:, cs], sl4[j], sl16[j]
            mx = jnp.maximum(jnp.maximum(a, b), c)
            ea, eb, ec = jnp.exp(a - mx), jnp.exp(b - mx), jnp.exp(c - mx)
            inv = 1.0 / (ea + eb + ec)
            m_ref[:, 512 + j * 128:512 + (j + 1) * 128] = (
                (ea * inv) * o1[:, cs] + (eb * inv) * so4[j] + (ec * inv) * so16[j]).astype(BF16)

    big = pl.BlockSpec((tl, 512), lambda i: (i, 0))
    dil = [big, _dil_spec(tl, 4, 512), _dil_spec(tl, 16, 512)]
    return pl.pallas_call(
        body, name=name, grid=(L // tl,),
        in_specs=[big, big, pl.BlockSpec((1, 512), lambda i: (0, 0))] + dil + dil,
        out_specs=pl.BlockSpec((tl, 1024), lambda i: (i, 0)),
        out_shape=jax.ShapeDtypeStruct((L, 1024), BF16), scratch_shapes=[pltpu.VMEM((4, tl, 128), F32)] * 4,
        compiler_params=_cparams("parallel"),
    )(oa, la, sink, *obs, *lbs)


def attn_merge_bwd(dm, oa, la, sink, obs, lbs, name):
    L = oa.shape[0]
    tl = DIL_TL

    def body(dm_ref, oa_ref, la_ref, sk_ref, o1, o4, o16, l1, l4, l16,
             doa_ref, dla_ref, d1, d4, d16, g1, g4, g16, sums_ref, so4, so16, sl4, sl16, sd4, sd16, sg4, sg16):
        @pl.when(pl.program_id(0) == 0)
        def _():
            sums_ref[...] = jnp.zeros_like(sums_ref)

        for src, dst, d in ((o4, so4, 4), (o16, so16, 16), (l4, sl4, 4), (l16, sl16, 16)):
            _from_dilated(src, dst, d)
        ones = _seg_ones(HD)
        for t in range(4):
            cs = slice(t * 128, (t + 1) * 128)
            dma = dm_ref[:, cs]
            keep = _sigmoid(la_ref[:, cs] - sk_ref[:, cs])
            doa_ref[:, cs] = dma * keep
            tt = dma * oa_ref[:, cs] * keep * (1.0 - keep)
            dla_ref[:, cs] = _segsum(tt, ones)
            sums_ref[:, cs] += _fold8(-tt)
            dmb = dm_ref[:, 512 + t * 128:512 + (t + 1) * 128]
            a, b, c = l1[:, cs], sl4[t], sl16[t]
            mx = jnp.maximum(jnp.maximum(a, b), c)
            ea, eb, ec = jnp.exp(a - mx), jnp.exp(b - mx), jnp.exp(c - mx)
            inv = 1.0 / (ea + eb + ec)
            wa, wb, wc = ea * inv, eb * inv, ec * inv
            d1[:, cs] = wa * dmb
            sd4[t] = wb * dmb
            sd16[t] = wc * dmb
            sa = _segsum(dmb * o1[:, cs], ones)
            sb = _segsum(dmb * so4[t], ones)
            sc_ = _segsum(dmb * so16[t], ones)
            mean = wa * sa + wb * sb + wc * sc_
            g1[:, cs] = wa * (sa - mean)
            sg4[t] = wb * (sb - mean)
            sg16[t] = wc * (sc_ - mean)
        for src, dst, d in ((sd4, d4, 4), (sd16, d16, 16), (sg4, g4, 4), (sg16, g16, 16)):
            _to_dilated(src, dst, d)

    big = pl.BlockSpec((tl, 512), lambda i: (i, 0))
    dil = [big, _dil_spec(tl, 4, 512), _dil_spec(tl, 16, 512)]
    sd = jax.ShapeDtypeStruct
    shp = [sd((L, 512), F32), sd((L // 4, 4 * 512), F32), sd((L // 16, 16 * 512), F32)]
    return pl.pallas_call(
        body, name=name, grid=(L // tl,),
        in_specs=[pl.BlockSpec((tl, 1024), lambda i: (i, 0)), big, big,
                  pl.BlockSpec((1, 512), lambda i: (0, 0))] + dil + dil,
        out_specs=tuple([big, big] + dil + dil + [pl.BlockSpec((8, 512), lambda i: (0, 0))]),
        out_shape=tuple([shp[0], shp[0]] + shp + shp + [sd((8, 512), F32)]),
        scratch_shapes=[pltpu.VMEM((4, tl, 128), F32)] * 8, compiler_params=_cparams("arbitrary"),
    )(dm, oa, la, sink, *obs, *lbs)


def _shift_down(x, halo, k, first):
    rows = lax.broadcasted_iota(jnp.int32, (8, x.shape[1]), 0)
    out = pltpu.roll(x, k, axis=0)
    hrows = jnp.where(first, 0.0, pltpu.roll(halo, k, axis=0))
    top = jnp.where(rows < k, hrows, out[0:8, :])
    return jnp.concatenate([top, out[8:, :]], axis=0)


def _shift_up(x, nxt, k):
    tl = x.shape[0]
    rows = lax.broadcasted_iota(jnp.int32, (8, x.shape[1]), 0)
    out = pltpu.roll(x, tl - k, axis=0)
    bottom = jnp.where(rows >= 8 - k, pltpu.roll(nxt, 8 - k, axis=0), out[tl - 8:, :])
    return jnp.concatenate([out[:tl - 8, :], bottom], axis=0)


def _silu(x):
    return x * _sigmoid(x)


def _dsilu(x):
    s = _sigmoid(x)
    return s * (1.0 + x * (1.0 - s))


def ffn_act_fwd(ua, ub, cw, name, comm=None):
    L, F = ua.shape
    tl = _rtile(L, 256)
    tc = _tile(F, 1408)
    hb = tl // 8

    def body(ua_ref, uah_ref, ub_ref, ubh_ref, wa_ref, wb_ref, o_ref, ac_ref, bc_ref):
        first = pl.program_id(1) == 0

        def conv(x_ref, h_ref, w_ref):
            x = x_ref[...]
            h = h_ref[...]
            return (w_ref[2:3, :] * x + w_ref[1:2, :] * _shift_down(x, h, 1, first)
                    + w_ref[0:1, :] * _shift_down(x, h, 2, first))

        a = conv(ua_ref, uah_ref, wa_ref)
        b = conv(ub_ref, ubh_ref, wb_ref)
        ac_ref[...] = a
        bc_ref[...] = b
        o_ref[...] = (_silu(a) * b).astype(BF16)

    main = pl.BlockSpec((tl, tc), lambda j, i: (i, j))
    halo = pl.BlockSpec((8, tc), lambda j, i: (jnp.maximum(i * hb - 1, 0), j))
    wa = pl.BlockSpec((3, tc), lambda j, i: (0, j))
    wb = pl.BlockSpec((3, tc), lambda j, i: (0, j + F // tc))
    f32 = jax.ShapeDtypeStruct((L, F), F32)
    return _call(body, (ua, ua, ub, ub, cw, cw), name=name, grid=(F // tc, L // tl),
                 in_specs=[main, halo, main, halo, wa, wb], out_specs=(main, main, main),
                 out_shape=(jax.ShapeDtypeStruct((L, F), BF16), f32, f32), sem=("parallel", "parallel"), comm=comm)


def ffn_act_bwd(ua, ub, ac, bc, cw, dact, name, comm=None):
    L, F = ua.shape
    tl = _rtile(L, 256)
    tc = _tile(F, 1408)
    nrt = L // tl

    def body(ua_ref, ub_ref, ac_ref, bc_ref, wa_ref, wb_ref, da_ref, dua_ref, dub_ref, sums_ref, ca_ref, cb_ref):
        i = pl.program_id(1)

        @pl.when(i == 0)
        def _():
            sums_ref[...] = jnp.zeros_like(sums_ref)
            ca_ref[...] = jnp.zeros_like(ca_ref)
            cb_ref[...] = jnp.zeros_like(cb_ref)

        a, b = ac_ref[...], bc_ref[...]
        dact_v = da_ref[...]
        dya = dact_v * b * _dsilu(a)
        dyb = dact_v * _silu(a)
        for (dy, w_ref, c_ref, d_ref, x_ref, base) in ((dya, wa_ref, ca_ref, dua_ref, ua_ref, 0),
                                                        (dyb, wb_ref, cb_ref, dub_ref, ub_ref, 24)):
            nxt = c_ref[...]
            ups = (dy, _shift_up(dy, nxt, 1), _shift_up(dy, nxt, 2))
            d_ref[...] = (w_ref[2:3, :] * ups[0] + w_ref[1:2, :] * ups[1] + w_ref[0:1, :] * ups[2]).astype(BF16)
            c_ref[...] = dy[0:8, :]
            x = x_ref[...]
            for k in range(3):
                sums_ref[base + 8 * (2 - k):base + 8 * (2 - k) + 8, :] += _fold8(ups[k] * x)

    rev = lambda i: nrt - 1 - i
    main = pl.BlockSpec((tl, tc), lambda j, i: (rev(i), j))
    wa = pl.BlockSpec((3, tc), lambda j, i: (0, j))
    wb = pl.BlockSpec((3, tc), lambda j, i: (0, j + F // tc))
    ob = jax.ShapeDtypeStruct((L, F), BF16)
    return _call(body, (ua, ub, ac, bc, cw, cw, dact), name=name, grid=(F // tc, nrt),
                 in_specs=[main, main, main, main, wa, wb, main],
                 out_specs=(main, main, pl.BlockSpec((48, tc), lambda j, i: (0, j))),
                 out_shape=(ob, ob, jax.ShapeDtypeStruct((48, F), F32)),
                 scratch_shapes=[pltpu.VMEM((8, tc), F32), pltpu.VMEM((8, tc), F32)],
                 sem=("parallel", "arbitrary"), comm=comm)


def attn_vectors(qna, kna, qnb, knb, sinks):
    ones = jnp.ones((128,), F32)
    wvec = jnp.concatenate([jnp.tile(qna, 8), jnp.tile(kna, 2), ones, jnp.tile(qnb, 8), jnp.tile(knb, 8),
                            jnp.tile(ones, 4)]).reshape(1, ATTN_IN)
    return wvec, jnp.repeat(sinks, HD).reshape(1, 512)


def _with_comm(result, comm):
    return result if comm is not None else (result, None)


def attention_block_fwd(h, w_in, wvec, sinkvec, w_out, tag, comms=None):
    L = h.shape[0]
    comms = comms or {}
    got = {}
    qkv = matmul([(h, w_in)], "nn", tag + "_qkv")
    X, X4, X16 = qknorm_fwd(qkv, wvec, tag + "_qknorm")
    (oa, la), got['swa'] = _with_comm(attn_fwd(X, 1, 0, 4, 5, True, 0, BLK - 1, tag + "_swa",
                                               comm=comms.get('swa')), comms.get('swa'))
    views = {1: (X, 6, 10, 14), 4: (X4, 0, 4, 8), 16: (X16, 0, 4, 8)}
    obs, lbs = [], []
    for window, d in B_BRANCHES:
        xd, qo, ko, vo = views[d]
        (o, l), got[d] = _with_comm(attn_fwd(xd, d, qo, ko, vo, False, 8, window // d,
                                             tag + f"_dil{d}", comm=comms.get(d)), comms.get(d))
        obs.append(o)
        lbs.append(l)
    m = attn_merge_fwd(oa, la, sinkvec, obs, lbs, tag + "_merge")
    if w_out is None:
        w_out = got[16][0].reshape(D, D)
        got['w_out'] = w_out
    y = matmul([(m, w_out)], "nn", tag + "_out")
    return y, (h, qkv, views, oa, la, obs, lbs, m), got


def attention_block_bwd(dy, res, w_in, wvec, sinkvec, w_out, tag, comms=None, send_w_out_on=None):
    h, qkv, views, oa, la, obs, lbs, m = res
    comms = dict(comms or {})
    got = {}
    g_w_out = matmul([(m, dy)], "tn", tag + "_dwout", out_dtype=BF16)
    if send_w_out_on is not None:
        comms[send_w_out_on] = ([g_w_out.reshape(N_DEV, D // N_DEV, D)], False)
    dm = matmul([(dy, w_out)], "nt", tag + "_dm")
    doa, dla, d1, d2, d3, g1, g2, g3, sinksums = attn_merge_bwd(dm, oa, la, sinkvec, obs, lbs, tag + "_dmerge")
    d_a, got['swa'] = _with_comm(attn_bwd(views[1][0], oa, la, doa, dla, 1, 0, 4, 5, True, 0, BLK - 1,
                                          tag + "_dswa", comm=comms.get('swa')), comms.get('swa'))
    d_b = []
    for (window, d), o, l, do, dl in zip(B_BRANCHES, obs, lbs, (d1, d2, d3), (g1, g2, g3)):
        xd, qo, ko, vo = views[d]
        dqkv_d, got[d] = _with_comm(attn_bwd(xd, o, l, do, dl, d, qo, ko, vo, False, 8, window // d,
                                             tag + f"_ddil{d}", comm=comms.get(d)), comms.get(d))
        d_b.append(dqkv_d)
    dqkv, wsums = qknorm_bwd(qkv, wvec, d_a, d_b, tag + "_dqknorm")
    g_w_in = matmul([(h, dqkv)], "tn", tag + "_dwin", out_dtype=BF16)
    dh = matmul([(dqkv, w_in)], "nt", tag + "_dh")
    ws = wsums.sum(axis=0)
    grads = dict(
        w_in=g_w_in, w_out=g_w_out,
        q_norm_a=ws[0:512].reshape(8, HD).sum(axis=0), k_norm_a=ws[512:640].reshape(2, HD).sum(axis=0),
        q_norm_b=ws[768:1280].reshape(8, HD).sum(axis=0), k_norm_b=ws[1280:1792].reshape(8, HD).sum(axis=0),
        sinks=sinksums.sum(axis=0).reshape(8, HD).sum(axis=1))
    return dh, grads, got


def ffn_block_fwd(h, w_up_a, w_up_b, cw, w_down, tag, comm=None):
    ua = matmul([(h, w_up_a)], "nn", tag + "_upa")
    ub = matmul([(h, w_up_b)], "nn", tag + "_upb")
    (act, ac, bc), got = _with_comm(ffn_act_fwd(ua, ub, cw, tag + "_act", comm=comm), comm)
    f = matmul([(act, w_down)], "nn", tag + "_down")
    return f, (h, ua, ub, ac, bc, act), got


def ffn_block_bwd(df, res, w_up_a, w_up_b, cw, w_down, tag, comm=None):
    h, ua, ub, ac, bc, act = res
    g_down = matmul([(act, df)], "tn", tag + "_dwdown", out_dtype=BF16)
    dact = matmul([(df, w_down)], "nt", tag + "_dact")
    (dua, dub, sums), got = _with_comm(ffn_act_bwd(ua, ub, ac, bc, cw, dact, tag + "_dactk", comm=comm), comm)
    g_up = jnp.concatenate([_cols_to_slabs(matmul([(h, dua)], "tn", tag + "_dwupa", out_dtype=BF16), N_DEV // 2),
                            _cols_to_slabs(matmul([(h, dub)], "tn", tag + "_dwupb", out_dtype=BF16), N_DEV // 2)],
                           axis=0)
    dh = matmul([(dua, w_up_a), (dub, w_up_b)], "nt", tag + "_dh")
    s = sums.reshape(2, 3, 8, D_FF).sum(axis=2)
    g_conv = jnp.concatenate([s[0], s[1]], axis=1)
    return dh, dict(w_up=g_up, conv=g_conv, w_down=g_down), got


def s5_params(lam_re, lam_im, log_dt, b_re, b_im, c_re, c_im):
    dt = jnp.exp(log_dt)[:, None]
    mag, ang = jnp.exp(lam_re * dt), lam_im * dt
    a_re, a_im = mag * jnp.cos(ang), mag * jnp.sin(ang)
    nr, ni = a_re - 1.0, a_im
    den = lam_re * lam_re + lam_im * lam_im
    f_re = (nr * lam_re + ni * lam_im) / den
    f_im = (ni * lam_re - nr * lam_im) / den
    eye = jnp.eye(16, dtype=F32)[:, None, :, None]
    bd = lambda b: (eye * jnp.transpose(b, (0, 2, 1))[:, :, None, :]).reshape(S5_W, S5_P)
    cd = lambda c: (eye * jnp.transpose(c, (0, 2, 1))[:, :, None, :]).reshape(S5_P, S5_W)
    flat = lambda t: t.reshape(1, S5_P)
    return flat(a_re), flat(a_im), flat(f_re), flat(f_im), bd(b_re), bd(b_im), cd(c_re), cd(c_im)


def _scan_tables(a_re, a_im, reverse):
    pows = [(a_re, a_im)]
    for _ in range(7):
        pr, pi = pows[-1]
        pows.append((pr * a_re - pi * a_im, pr * a_im + pi * a_re))
    order = list(range(7, -1, -1)) if reverse else list(range(8))
    z = jnp.zeros_like(a_re)
    rows = [pows[0][0], pows[0][1], pows[1][0], pows[1][1], pows[3][0], pows[3][1], z, z]
    rows += [pows[k][0] for k in order] + [pows[k][1] for k in order]
    return jnp.concatenate(rows, axis=0)


def _block_scan(er, ei, tab_ref, cr, ci, reverse):
    rows = lax.broadcasted_iota(jnp.int32, er.shape, 0)
    for idx, s in enumerate((1, 2, 4)):
        if reverse:
            sr, si, keep = pltpu.roll(er, 8 - s, axis=0), pltpu.roll(ei, 8 - s, axis=0), rows < 8 - s
        else:
            sr, si, keep = pltpu.roll(er, s, axis=0), pltpu.roll(ei, s, axis=0), rows >= s
        sr, si = jnp.where(keep, sr, 0.0), jnp.where(keep, si, 0.0)
        ar, ai = tab_ref[2 * idx:2 * idx + 1, :], tab_ref[2 * idx + 1:2 * idx + 2, :]
        er, ei = er + ar * sr - ai * si, ei + ar * si + ai * sr
    pr, pi_ = tab_ref[8:16, :], tab_ref[16:24, :]
    er, ei = er + pr * cr - pi_ * ci, ei + pr * ci + pi_ * cr
    return er, ei


def s5_scan_fwd(bu_re, bu_im, a_re, a_im, f_re, f_im, name):
    L, P = bu_re.shape
    tl = _rtile(L, 512)
    tab = _scan_tables(a_re, a_im, False)
    fvec = jnp.concatenate([f_re, f_im] + [jnp.zeros_like(f_re)] * 6, axis=0)

    def body(br_ref, bi_ref, tab_ref, f_ref, xr_ref, xi_ref, c_ref):
        @pl.when(pl.program_id(0) == 0)
        def _():
            c_ref[...] = jnp.zeros_like(c_ref)

        def blk(i, carry):
            cr, ci = carry
            rows = pl.ds(pl.multiple_of(i * 8, 8), 8)
            br, bi = br_ref[rows, :], bi_ref[rows, :]
            fr, fi = f_ref[0:1, :], f_ref[1:2, :]
            er, ei = _block_scan(fr * br - fi * bi, fr * bi + fi * br, tab_ref, cr, ci, False)
            xr_ref[rows, :] = er
            xi_ref[rows, :] = ei
            return er[7:8, :], ei[7:8, :]

        cr, ci = lax.fori_loop(0, tl // 8, blk, (c_ref[0:1, :], c_ref[1:2, :]))
        c_ref[0:1, :] = cr
        c_ref[1:2, :] = ci

    big = pl.BlockSpec((tl, P), lambda i: (i, 0))
    out = jax.ShapeDtypeStruct((L, P), F32)
    return pl.pallas_call(
        body, name=name, grid=(L // tl,),
        in_specs=[big, big, pl.BlockSpec((24, P), lambda i: (0, 0)), pl.BlockSpec((8, P), lambda i: (0, 0))],
        out_specs=(big, big), out_shape=(out, out), scratch_shapes=[pltpu.VMEM((8, P), F32)],
        compiler_params=_cparams("arbitrary"))(bu_re, bu_im, tab, fvec)


def s5_scan_bwd(dx_re, dx_im, x_re, x_im, bu_re, bu_im, a_re, a_im, f_re, f_im, name):
    L, P = dx_re.shape
    tl = _rtile(L, 256)
    nt = L // tl
    tab = _scan_tables(a_re, -a_im, True)
    fvec = jnp.concatenate([f_re, f_im] + [jnp.zeros_like(f_re)] * 6, axis=0)

    def body(gr_ref, gi_ref, xr_ref, xi_ref, br_ref, bi_ref, tab_ref, f_ref, dbr_ref, dbi_ref, s_ref, c_ref):
        @pl.when(pl.program_id(0) == 0)
        def _():
            c_ref[...] = jnp.zeros_like(c_ref)
            s_ref[...] = jnp.zeros_like(s_ref)

        def blk(k, carry):
            cr, ci = carry
            i = tl // 8 - 1 - k
            rows = pl.ds(pl.multiple_of(i * 8, 8), 8)
            er, ei = _block_scan(gr_ref[rows, :], gi_ref[rows, :], tab_ref, cr, ci, True)
            rid = lax.broadcasted_iota(jnp.int32, er.shape, 0)
            sr = jnp.where(rid == 7, cr, pltpu.roll(er, 7, axis=0))
            si = jnp.where(rid == 7, ci, pltpu.roll(ei, 7, axis=0))
            xr, xi = xr_ref[rows, :], xi_ref[rows, :]
            s_ref[0:8, :] += sr * xr + si * xi
            s_ref[8:16, :] += si * xr - sr * xi
            br, bi = br_ref[rows, :], bi_ref[rows, :]
            s_ref[16:24, :] += er * br + ei * bi
            s_ref[24:32, :] += ei * br - er * bi
            fr, fi = f_ref[0:1, :], f_ref[1:2, :]
            dbr_ref[rows, :] = fr * er + fi * ei
            dbi_ref[rows, :] = fr * ei - fi * er
            return er[0:1, :], ei[0:1, :]

        cr, ci = lax.fori_loop(0, tl // 8, blk, (c_ref[0:1, :], c_ref[1:2, :]))
        c_ref[0:1, :] = cr
        c_ref[1:2, :] = ci

    big = pl.BlockSpec((tl, P), lambda i: (nt - 1 - i, 0))
    out = jax.ShapeDtypeStruct((L, P), F32)
    return pl.pallas_call(
        body, name=name, grid=(nt,),
        in_specs=[big] * 6 + [pl.BlockSpec((24, P), lambda i: (0, 0)), pl.BlockSpec((8, P), lambda i: (0, 0))],
        out_specs=(big, big, pl.BlockSpec((32, P), lambda i: (0, 0))),
        out_shape=(out, out, jax.ShapeDtypeStruct((32, P), F32)), scratch_shapes=[pltpu.VMEM((8, P), F32)],
        compiler_params=_cparams("arbitrary"))(dx_re, dx_im, x_re, x_im, bu_re, bu_im, tab, fvec)


_GK, _GC = math.sqrt(2.0 / math.pi), 0.044715


def _gelu(y):
    return 0.5 * y * (1.0 + jnp.tanh(_GK * (y + _GC * y * y * y)))


def _dgelu(y):
    t = jnp.tanh(_GK * (y + _GC * y * y * y))
    return 0.5 * (1.0 + t) + 0.5 * y * (1.0 - t * t) * _GK * (1.0 + 3.0 * _GC * y * y)


def s5_out_fwd(x_re, x_im, u, cd_re, cd_im, dskip, glu_w, glu_b, name):
    L = u.shape[0]
    tl = _rtile(L, 512)

    def body(xr_ref, xi_ref, u_ref, cr_ref, ci_ref, d_ref, w_ref, b_ref, y_ref, o_ref):
        y = (jnp.dot(xr_ref[...].astype(BF16), cr_ref[...], preferred_element_type=F32)
             - jnp.dot(xi_ref[...].astype(BF16), ci_ref[...], preferred_element_type=F32)
             + d_ref[...] * u_ref[...])
        y_ref[...] = y
        g = _gelu(y)
        z = jnp.dot(g.astype(BF16), w_ref[...], preferred_element_type=F32) + b_ref[...]
        o_ref[...] = (g * _sigmoid(z)).astype(BF16)

    big = pl.BlockSpec((tl, S5_P), lambda i: (i, 0))
    sm = pl.BlockSpec((tl, S5_W), lambda i: (i, 0))
    full = lambda r, c: pl.BlockSpec((r, c), lambda i: (0, 0))
    return pl.pallas_call(
        body, name=name, grid=(L // tl,),
        in_specs=[big, big, sm, full(S5_P, S5_W), full(S5_P, S5_W), full(1, S5_W), full(S5_W, S5_W), full(1, S5_W)],
        out_specs=(sm, sm),
        out_shape=(jax.ShapeDtypeStruct((L, S5_W), F32), jax.ShapeDtypeStruct((L, S5_W), BF16)),
        compiler_params=_cparams("parallel"))(x_re, x_im, u, cd_re, cd_im, dskip, glu_w, glu_b)


def s5_out_bwd(dout, y, u, x_re, x_im, cd_re, cd_im, dskip, glu_w, glu_b, name, dout_col=0):
    L = u.shape[0]
    tl = _rtile(L, 256)
    nt_dims = (((1,), (1,)), ((), ()))
    tn_dims = (((0,), (0,)), ((), ()))

    def body(do_ref, y_ref, u_ref, xr_ref, xi_ref, cr_ref, ci_ref, d_ref, w_ref, b_ref,
             dxr_ref, dxi_ref, du_ref, dcr_ref, dci_ref, dw_ref, s_ref):
        @pl.when(pl.program_id(0) == 0)
        def _():
            dcr_ref[...] = jnp.zeros_like(dcr_ref)
            dci_ref[...] = jnp.zeros_like(dci_ref)
            dw_ref[...] = jnp.zeros_like(dw_ref)
            s_ref[...] = jnp.zeros_like(s_ref)

        yv, dov = y_ref[...], do_ref[...]
        g = _gelu(yv)
        gb = g.astype(BF16)
        sg = _sigmoid(jnp.dot(gb, w_ref[...], preferred_element_type=F32) + b_ref[...])
        dz = dov * g * sg * (1.0 - sg)
        dzb = dz.astype(BF16)
        dg = dov * sg + lax.dot_general(dzb, w_ref[...], nt_dims, preferred_element_type=F32)
        dw_ref[...] += lax.dot_general(gb, dzb, tn_dims, preferred_element_type=F32)
        dy = dg * _dgelu(yv)
        dyb = dy.astype(BF16)
        s_ref[0:8, :] += _fold8(dy * u_ref[...])
        s_ref[8:16, :] += _fold8(dz)
        du_ref[...] = dy * d_ref[...]
        dxr_ref[...] = lax.dot_general(dyb, cr_ref[...], nt_dims, preferred_element_type=F32)
        dxi_ref[...] = -lax.dot_general(dyb, ci_ref[...], nt_dims, preferred_element_type=F32)
        dcr_ref[...] += lax.dot_general(xr_ref[...].astype(BF16), dyb, tn_dims, preferred_element_type=F32)
        dci_ref[...] -= lax.dot_general(xi_ref[...].astype(BF16), dyb, tn_dims, preferred_element_type=F32)

    big = pl.BlockSpec((tl, S5_P), lambda i: (i, 0))
    sm = pl.BlockSpec((tl, S5_W), lambda i: (i, 0))
    full = lambda r, c: pl.BlockSpec((r, c), lambda i: (0, 0))
    sd = jax.ShapeDtypeStruct
    return pl.pallas_call(
        body, name=name, grid=(L // tl,),
        in_specs=[pl.BlockSpec((tl, S5_W), lambda i: (i, dout_col)), sm, sm, big, big, full(S5_P, S5_W),
                  full(S5_P, S5_W), full(1, S5_W), full(S5_W, S5_W), full(1, S5_W)],
        out_specs=(big, big, sm, full(S5_P, S5_W), full(S5_P, S5_W), full(S5_W, S5_W), full(16, S5_W)),
        out_shape=(sd((L, S5_P), F32), sd((L, S5_P), F32), sd((L, S5_W), F32), sd((S5_P, S5_W), F32),
                   sd((S5_P, S5_W), F32), sd((S5_W, S5_W), F32), sd((16, S5_W), F32)),
        compiler_params=_cparams("arbitrary"))(dout, y, u, x_re, x_im, cd_re, cd_im, dskip, glu_w, glu_b)


def s5_block_fwd(u, params, dskip, glu_w, glu_b, tag):
    a_re, a_im, f_re, f_im, bd_re, bd_im, cd_re, cd_im = params
    bu_re = matmul([(u, bd_re.astype(BF16))], "nn", tag + "_bure")
    bu_im = matmul([(u, bd_im.astype(BF16))], "nn", tag + "_buim")
    x_re, x_im = s5_scan_fwd(bu_re, bu_im, a_re, a_im, f_re, f_im, tag + "_scan")
    y, out = s5_out_fwd(x_re, x_im, u, cd_re.astype(BF16), cd_im.astype(BF16), dskip, glu_w, glu_b, tag + "_out")
    return out, (u, bu_re, bu_im, x_re, x_im, y)


def s5_block_bwd(dout, res, params, dskip, glu_w, glu_b, tag, dout_col=0):
    u, bu_re, bu_im, x_re, x_im, y = res
    a_re, a_im, f_re, f_im, bd_re, bd_im, cd_re, cd_im = params
    dxr, dxi, du, dcr, dci, dglu_w, sums = s5_out_bwd(dout, y, u, x_re, x_im, cd_re.astype(BF16), cd_im.astype(BF16),
                                                      dskip, glu_w, glu_b, tag + "_dout", dout_col=dout_col)
    dbr, dbi, acc = s5_scan_bwd(dxr, dxi, x_re, x_im, bu_re, bu_im, a_re, a_im, f_re, f_im, tag + "_dscan")
    du = du + matmul([(dbr, bd_re.astype(BF16)), (dbi, bd_im.astype(BF16))], "nt", tag + "_du")
    dbd_re = matmul([(u, dbr)], "tn", tag + "_dbdre")
    dbd_im = matmul([(u, dbi)], "tn", tag + "_dbdim")
    acc = acc.reshape(4, 8, S5_P).sum(axis=1)
    s = sums.reshape(2, 8, S5_W).sum(axis=1)
    cot = (acc[0:1], acc[1:2], acc[2:3], acc[3:4], dbd_re, dbd_im, dcr, dci)
    return du, cot, dict(dskip=s[0], glu_w=dglu_w, glu_b=s[1])


DN_Z0, DN_NT = 18, 18
REC_U0, REC_A0 = 3072, 3328


def rec_cols_permute(w):
    return jnp.concatenate([w[..., S5_W:REC_A0], w[..., :S5_W], w[..., REC_A0:]], axis=-1)


def rec_cols_restore(w):
    return jnp.concatenate([w[..., REC_U0:REC_A0], w[..., :REC_U0], w[..., REC_A0:]], axis=-1)


DN_W = DN_H * DN_DK
DN_NI = 4


def _dn_conv4(taps, w_ref):
    xc = w_ref[3:4, :] * taps[0]
    for k in range(1, 4):
        xc = xc + w_ref[3 - k:4 - k, :] * taps[k]
    return xc


def dn_prep_fwd(rin, cw, name):
    L = rin.shape[0]
    tl = _rtile(L, 256)
    hb = tl // 8

    def body(x_ref, h_ref, w_ref, o_ref):
        j = pl.program_id(0)
        first = pl.program_id(1) == 0
        x, h = x_ref[...], h_ref[...]
        s = _silu(_dn_conv4([x] + [_shift_down(x, h, k, first) for k in range(1, 4)], w_ref))
        scale = jnp.where(j == 0, DN_DK ** -0.5, 1.0)
        for hd in _HEADS:
            cs = slice(hd * 128, (hd + 1) * 128)
            sh = s[:, cs]
            r = lax.rsqrt(jnp.sum(sh * sh, axis=-1, keepdims=True) + EPS)
            o_ref[:, cs] = jnp.where(j < 2, sh * r * scale, sh)

    main = pl.BlockSpec((tl, DN_W), lambda j, i: (i, j))
    halo = pl.BlockSpec((8, DN_W), lambda j, i: (jnp.maximum(i * hb - 1, 0), j))
    return pl.pallas_call(
        body, name=name, grid=(3, L // tl),
        in_specs=[main, halo, pl.BlockSpec((4, DN_W), lambda j, i: (0, j))],
        out_specs=main, out_shape=jax.ShapeDtypeStruct((L, 3 * DN_W), F32),
        compiler_params=_cparams("parallel", "parallel"))(rin, rin, cw)


def dn_prep_bwd(rin, cw, dout, name):
    L = rin.shape[0]
    tl = _rtile(L, 256)
    hb = tl // 8
    nrt = L // tl

    def body(x_ref, h_ref, w_ref, d_ref, dx_ref, s_ref, c_ref):
        j = pl.program_id(0)
        i = pl.program_id(1)
        first = i == nrt - 1

        @pl.when(i == 0)
        def _():
            s_ref[...] = jnp.zeros_like(s_ref)
            c_ref[...] = jnp.zeros_like(c_ref)

        x, h = x_ref[...], h_ref[...]
        taps = [x] + [_shift_down(x, h, k, first) for k in range(1, 4)]
        xc = _dn_conv4(taps, w_ref)
        s = _silu(xc)
        scale = jnp.where(j == 0, DN_DK ** -0.5, 1.0)
        pieces = []
        for hd in _HEADS:
            cs = slice(hd * 128, (hd + 1) * 128)
            sh, d = s[:, cs], d_ref[:, cs]
            r = lax.rsqrt(jnp.sum(sh * sh, axis=-1, keepdims=True) + EPS)
            n = sh * r
            dn = d * scale
            pieces.append(jnp.where(j < 2, r * (dn - n * jnp.sum(dn * n, axis=-1, keepdims=True)), d))
        dxc = jnp.concatenate(pieces, axis=1) * _dsilu(xc)
        nxt = c_ref[...]
        dx_ref[...] = _dn_conv4([dxc] + [_shift_up(dxc, nxt, k) for k in range(1, 4)], w_ref).astype(BF16)
        c_ref[...] = dxc[0:8, :]
        for k in range(4):
            s_ref[8 * (3 - k):8 * (3 - k) + 8, :] += _fold8(dxc * taps[k])

    rev = lambda i: nrt - 1 - i
    main = pl.BlockSpec((tl, DN_W), lambda j, i: (rev(i), j))
    halo = pl.BlockSpec((8, DN_W), lambda j, i: (jnp.maximum(rev(i) * hb - 1, 0), j))
    return pl.pallas_call(
        body, name=name, grid=(3, nrt),
        in_specs=[main, halo, pl.BlockSpec((4, DN_W), lambda j, i: (0, j)), main],
        out_specs=(main, pl.BlockSpec((32, DN_W), lambda j, i: (0, j))),
        out_shape=(jax.ShapeDtypeStruct((L, 3 * DN_W), BF16), jax.ShapeDtypeStruct((32, 3 * DN_W), F32)),
        scratch_shapes=[pltpu.VMEM((8, DN_W), F32)],
        compiler_params=_cparams("parallel", "arbitrary"))(rin, rin, cw, dout)


_HI = lax.Precision.HIGH
_NT = (((1,), (1,)), ((), ()))
_TN = (((0,), (0,)), ((), ()))
_HEADS = tuple(range(DN_H))


def _mm(a, b, dims=(((1,), (0,)), ((), ())), hi=False):
    if hi:
        return lax.dot_general(a, b, dims, precision=_HI, preferred_element_type=F32)
    return lax.dot_general(a.astype(BF16), b.astype(BF16), dims, preferred_element_type=F32)


def _dn_masks():
    ri = lax.broadcasted_iota(jnp.int32, (DN_C, DN_C), 0)
    ci = lax.broadcasted_iota(jnp.int32, (DN_C, DN_C), 1)
    return ri >= ci, ri > ci, (ri == ci).astype(F32)


def _dn_decay(gc, gr, causal):
    gam = [jnp.where(causal, jnp.exp(jnp.where(causal, c - r, 0.0)), 0.0) for c, r in zip(gc, gr)]
    eg, el, gl = _dn_row_decay(gc)
    return gam, eg, el, gl


def _dn_row_decay(gc):
    eg = [jnp.exp(c) for c in gc]
    el = [jnp.exp(c[DN_C - 1:DN_C, :] - c) for c in gc]
    gl = [jnp.exp(c[DN_C - 1:DN_C, :]) for c in gc]
    return eg, el, gl


def _dn_solve(k, v, beta, gam, eg, kk, strict, eye):
    ids = range(len(k))
    nmat = [jnp.where(strict, beta[h] * kk[h] * gam[h], 0.0) for h in ids]
    t = [eye - nmat[h] for h in ids]
    m = [_mm(nmat[h], nmat[h]) for h in ids]
    for step in range(5):
        t = [t[h] + _mm(t[h], m[h]) for h in ids]
        if step < 4:
            m = [_mm(m[h], m[h]) for h in ids]
    res = [eye - t[h] - _mm(nmat[h], t[h], hi=True) for h in ids]
    t = [t[h] + _mm(t[h], res[h]) for h in ids]
    rhs = [jnp.concatenate([v[h] * beta[h], k[h] * (beta[h] * eg[h])], axis=1) for h in ids]
    sol = [_mm(t[h], rhs[h], hi=True) for h in ids]
    return t, sol


def dn_chunk_fwd(qkv, gcol, grow, bcol, name, comm=None):
    L = qkv.shape[0]
    C, W = DN_C, DN_H * DN_DK
    ncb = 8
    tl = ncb * C
    nchunks = L // C
    comm1, comm2 = comm if comm is not None else (None, None)
    hs = lambda h: slice(h * 128, (h + 1) * 128)

    def intra(q_ref, k_ref, v_ref, gc_ref, gr_ref, b_ref, t_ref, sol_ref, qk_ref):
        causal, strict, eye = _dn_masks()

        def pair(p, _):
            units = [(DN_NI * p + j, h) for j in range(DN_NI) for h in _HEADS]
            rows = [pl.ds(pl.multiple_of(c * C, C), C) for c, _ in units]
            q = [q_ref[r, hs(h)] for r, (_, h) in zip(rows, units)]
            k = [k_ref[r, hs(h)] for r, (_, h) in zip(rows, units)]
            v = [v_ref[r, hs(h)] for r, (_, h) in zip(rows, units)]
            gc = [gc_ref[r, h:h + 1] for r, (_, h) in zip(rows, units)]
            gr = [gr_ref[c][h:h + 1, :] for c, h in units]
            beta = [b_ref[r, h:h + 1] for r, (_, h) in zip(rows, units)]
            gam, eg, _, _ = _dn_decay(gc, gr, causal)
            kk = [_mm(x, x, _NT) for x in k]
            t, sol = _dn_solve(k, v, beta, gam, eg, kk, strict, eye)
            qk = [_mm(a, b, _NT) * g for a, b, g in zip(q, k, gam)]
            for i, (r, (_, h)) in enumerate(zip(rows, units)):
                t_ref[r, h * C:(h + 1) * C] = t[i]
                sol_ref[r, h * 256:(h + 1) * 256] = sol[i]
                qk_ref[r, h * C:(h + 1) * C] = qk[i]
            return 0

        lax.fori_loop(0, ncb // DN_NI, pair, 0)

    def scan(q_ref, k_ref, gc_ref, sol_ref, qk_ref, o_ref, sh_ref, s_ref):
        @pl.when(pl.program_id(0) == 0)
        def _():
            s_ref[...] = jnp.zeros_like(s_ref)

        def chunk(c, _):
            rows = pl.ds(pl.multiple_of(c * C, C), C)
            q = [q_ref[rows, hs(h)] for h in _HEADS]
            k = [k_ref[rows, hs(h)] for h in _HEADS]
            sol = [sol_ref[rows, h * 256:(h + 1) * 256] for h in _HEADS]
            qk = [qk_ref[rows, h * C:(h + 1) * C] for h in _HEADS]
            eg, el, gl = _dn_row_decay([gc_ref[rows, h:h + 1] for h in _HEADS])
            S = [s_ref[hs(h), :] for h in _HEADS]
            vn = [sol[h][:, :128] - _mm(sol[h][:, 128:], S[h]) for h in _HEADS]
            o = [_mm(q[h] * eg[h], S[h]) + _mm(qk[h], vn[h]) for h in _HEADS]
            Sn = [S[h] * gl[h] + _mm(k[h] * el[h], vn[h], _TN) for h in _HEADS]
            for h in _HEADS:
                sh_ref[c, hs(h), :] = S[h]
                s_ref[hs(h), :] = Sn[h]
                o_ref[rows, hs(h)] = o[h]
            return 0

        lax.fori_loop(0, ncb, chunk, 0)

    col = lambda b: pl.BlockSpec((tl, W), lambda i: (i, b))
    small = pl.BlockSpec((tl, 8), lambda i: (i, 0))
    rowblk = lambda w: pl.BlockSpec((tl, w), lambda i: (i, 0))
    sd = jax.ShapeDtypeStruct
    (thist, solhist, qk), got1 = _with_comm(_call(
        intra, (qkv, qkv, qkv, gcol, grow, bcol), name=name + "_intra", grid=(L // tl,),
        in_specs=[col(0), col(1), col(2), small, pl.BlockSpec((ncb, 8, C), lambda i: (i, 0, 0)), small],
        out_specs=(rowblk(DN_H * C), rowblk(DN_H * 256), rowblk(DN_H * C)),
        out_shape=(sd((L, DN_H * C), F32), sd((L, DN_H * 256), F32), sd((L, DN_H * C), F32)),
        sem=("parallel",), comm=comm1), comm1)
    (o, shist), got2 = _with_comm(_call(
        scan, (qkv, qkv, gcol, solhist, qk), name=name + "_scan", grid=(L // tl,),
        in_specs=[col(0), col(1), small, rowblk(DN_H * 256), rowblk(DN_H * C)],
        out_specs=(rowblk(W), pl.BlockSpec((ncb, W, 128), lambda i: (i, 0, 0))),
        out_shape=(sd((L, W), F32), sd((nchunks, W, 128), F32)),
        scratch_shapes=[pltpu.VMEM((W, 128), F32)], sem=("arbitrary",), comm=comm2), comm2)
    res = (o, shist, thist, solhist)
    return res if comm is None else (res, (got1 or []) + (got2 or []))


def dn_chunk_bwd(qkv, gcol, grow, bcol, shist, thist, solhist, do, name, comm=None):
    L = qkv.shape[0]
    C, W = DN_C, DN_H * DN_DK
    ncb = 8
    tl = ncb * C
    nchunks = L // C
    nt = L // tl

    def body(q_ref, k_ref, v_ref, gc_ref, gr_ref, b_ref, sh_ref, t_ref, sol_ref, do_ref,
             dqkv_ref, dgc_ref, dgr_ref, db_ref, ds_ref):
        @pl.when(pl.program_id(0) == 0)
        def _():
            ds_ref[...] = jnp.zeros_like(ds_ref)

        lane8 = lax.broadcasted_iota(jnp.int32, (C, 8), 1)
        sub8 = lax.broadcasted_iota(jnp.int32, (8, C), 0)
        rowid = lax.broadcasted_iota(jnp.int32, (C, 1), 0)
        causal, strict, _ = _dn_masks()
        rsum = lambda a: jnp.sum(a, axis=1, keepdims=True)

        def chunk(cc, _):
            c = ncb - 1 - cc
            rows = pl.ds(pl.multiple_of(c * C, C), C)
            grow_c = gr_ref[c]
            hs = lambda h: slice(h * 128, (h + 1) * 128)
            q = [q_ref[rows, hs(h)] for h in _HEADS]
            k = [k_ref[rows, hs(h)] for h in _HEADS]
            v = [v_ref[rows, hs(h)] for h in _HEADS]
            gc = [gc_ref[rows, h:h + 1] for h in _HEADS]
            gr = [grow_c[h:h + 1, :] for h in _HEADS]
            beta = [b_ref[rows, h:h + 1] for h in _HEADS]
            t = [t_ref[rows, h * C:(h + 1) * C] for h in _HEADS]
            sol = [sol_ref[rows, h * 256:(h + 1) * 256] for h in _HEADS]
            S = [sh_ref[c, hs(h), :] for h in _HEADS]
            dS = [ds_ref[hs(h), :] for h in _HEADS]
            dov = [do_ref[rows, hs(h)] for h in _HEADS]
            gam, eg, el, gl = _dn_decay(gc, gr, causal)
            kk = [_mm(k[h], k[h], _NT) for h in _HEADS]
            qk_raw = [_mm(q[h], k[h], _NT) for h in _HEADS]
            w = [sol[h][:, 128:] for h in _HEADS]
            kd = [k[h] * el[h] for h in _HEADS]
            vn = [sol[h][:, :128] - _mm(w[h], S[h]) for h in _HEADS]
            dvn = [_mm(qk_raw[h] * gam[h], dov[h], _TN) + _mm(kd[h], dS[h]) for h in _HEADS]
            dqd = [_mm(dov[h], S[h], _NT) for h in _HEADS]
            dqk = [jnp.where(causal, _mm(dov[h], vn[h], _NT), 0.0) for h in _HEADS]
            dkd = [_mm(vn[h], dS[h], _NT) for h in _HEADS]
            dgl = [jnp.sum(rsum(dS[h] * S[h]), axis=0, keepdims=True) for h in _HEADS]
            dw = [-_mm(dvn[h], S[h], _NT) for h in _HEADS]
            dSn = [dS[h] * gl[h] + _mm(q[h] * eg[h], dov[h], _TN) - _mm(w[h], dvn[h], _TN) for h in _HEADS]
            drhs = [_mm(t[h], jnp.concatenate([dvn[h], dw[h]], axis=1), _TN) for h in _HEADS]
            dn = [jnp.where(strict, -_mm(drhs[h], sol[h], _NT), 0.0) for h in _HEADS]
            dgc_all = jnp.zeros((C, 8), F32)
            db_all = jnp.zeros((C, 8), F32)
            dgr_all = jnp.zeros((8, C), F32)
            for h in _HEADS:
                drv, drk = drhs[h][:, :128], drhs[h][:, 128:]
                t2 = rsum(drk * k[h])
                x = dn[h] * gam[h]
                dbeta = rsum(drv * v[h]) + t2 * eg[h] + rsum(x * kk[h])
                dkk = x * beta[h]
                draw = dqk[h] * gam[h]
                mm_ = (dn[h] * beta[h] * kk[h] + dqk[h] * qk_raw[h]) * gam[h]
                deg = t2 * beta[h] + rsum(dqd[h] * q[h])
                r_ = rsum(dkd[h] * k[h]) * el[h]
                dglast = jnp.sum(r_, axis=0, keepdims=True) + dgl[h] * gl[h]
                dgc = rsum(mm_) + deg * eg[h] - r_ + jnp.where(rowid == C - 1, dglast, 0.0)
                dgr = -jnp.sum(mm_, axis=0, keepdims=True)
                dqkv_ref[rows, hs(h)] = _mm(draw, k[h]) + dqd[h] * eg[h]
                dqkv_ref[rows, hs(DN_H + h)] = (drk * (beta[h] * eg[h]) + _mm(dkk, k[h]) + _mm(dkk, k[h], _TN)
                                                + _mm(draw, q[h], _TN) + dkd[h] * el[h])
                dqkv_ref[rows, hs(2 * DN_H + h)] = drv * beta[h]
                ds_ref[hs(h), :] = dSn[h]
                dgc_all = dgc_all + jnp.where(lane8 == h, dgc, 0.0)
                db_all = db_all + jnp.where(lane8 == h, dbeta, 0.0)
                dgr_all = dgr_all + jnp.where(sub8 == h, dgr, 0.0)
            dgc_ref[rows, :] = dgc_all
            db_ref[rows, :] = db_all
            dgr_ref[c] = dgr_all
            return 0

        lax.fori_loop(0, ncb, chunk, 0)

    rev = lambda i: nt - 1 - i
    col = lambda b: pl.BlockSpec((tl, W), lambda i: (rev(i), b))
    rowblk = lambda w: pl.BlockSpec((tl, w), lambda i: (rev(i), 0))
    small = pl.BlockSpec((tl, 8), lambda i: (rev(i), 0))
    g3 = pl.BlockSpec((ncb, 8, C), lambda i: (rev(i), 0, 0))
    sd = jax.ShapeDtypeStruct
    return _call(body, (qkv, qkv, qkv, gcol, grow, bcol, shist, thist, solhist, do), name=name, grid=(nt,),
                 in_specs=[col(0), col(1), col(2), small, g3, small,
                           pl.BlockSpec((ncb, W, 128), lambda i: (rev(i), 0, 0)), rowblk(DN_H * C),
                           rowblk(DN_H * 256), col(0)],
                 out_specs=(rowblk(3 * W), small, g3, small),
                 out_shape=(sd((L, 3 * W), F32), sd((L, 8), F32), sd((nchunks, 8, C), F32), sd((L, 8), F32)),
                 scratch_shapes=[pltpu.VMEM((W, 128), F32)], sem=("arbitrary",), comm=comm)


def dn_out_fwd(o, rin, nw, name):
    L = o.shape[0]
    tl = _rtile(L, 256)

    def body(o_ref, z_ref, w_ref, y_ref):
        for hd in _HEADS:
            cs = slice(hd * 128, (hd + 1) * 128)
            ov = o_ref[:, cs]
            r = lax.rsqrt(jnp.mean(ov * ov, axis=-1, keepdims=True) + EPS)
            y_ref[:, cs] = (ov * r * w_ref[...] * _silu(z_ref[:, cs])).astype(BF16)

    return pl.pallas_call(
        body, name=name, grid=(L // tl,),
        in_specs=[pl.BlockSpec((tl, DN_W), lambda i: (i, 0)), pl.BlockSpec((tl, DN_W), lambda i: (i, 3)),
                  pl.BlockSpec((1, 128), lambda i: (0, 0))],
        out_specs=pl.BlockSpec((tl, DN_W), lambda i: (i, 0)), out_shape=jax.ShapeDtypeStruct((L, DN_W), BF16),
        compiler_params=_cparams("parallel"))(o, rin, nw)


def dn_out_bwd(dycat, o, rin, nw, name):
    L = o.shape[0]
    tl = _rtile(L, 256)

    def body(dy_ref, o_ref, z_ref, w_ref, do_ref, dz_ref, s_ref):
        @pl.when(pl.program_id(0) == 0)
        def _():
            s_ref[...] = jnp.zeros_like(s_ref)

        for hd in _HEADS:
            cs = slice(hd * 128, (hd + 1) * 128)
            ov, zv, d = o_ref[:, cs], z_ref[:, cs], dy_ref[:, cs]
            r = lax.rsqrt(jnp.mean(ov * ov, axis=-1, keepdims=True) + EPS)
            n = ov * r
            dnw = d * _silu(zv)
            dz_ref[:, cs] = (d * n * w_ref[...] * _dsilu(zv)).astype(BF16)
            dn = dnw * w_ref[...]
            do_ref[:, cs] = r * (dn - n * jnp.mean(dn * n, axis=-1, keepdims=True))
            s_ref[:, cs] += _fold8(dnw * n)

    own = pl.BlockSpec((tl, DN_W), lambda i: (i, 0))
    sd = jax.ShapeDtypeStruct
    return pl.pallas_call(
        body, name=name, grid=(L // tl,),
        in_specs=[own, own, pl.BlockSpec((tl, DN_W), lambda i: (i, 3)), pl.BlockSpec((1, 128), lambda i: (0, 0))],
        out_specs=(own, own, pl.BlockSpec((8, DN_W), lambda i: (0, 0))),
        out_shape=(sd((L, DN_W), F32), sd((L, DN_W), BF16), sd((8, DN_W), F32)),
        compiler_params=_cparams("arbitrary"))(dycat, o, rin, nw)


def dn_gates(a, beta_raw, a_log, dt_bias):
    L = a.shape[0]
    beta = jax.nn.sigmoid(beta_raw)
    g = -jnp.exp(a_log) * jax.nn.softplus(a + dt_bias)
    G = jnp.cumsum(g.reshape(L // DN_C, DN_C, DN_H), axis=1)
    pad = lambda t: jnp.pad(t, ((0, 0), (0, 8 - DN_H)))
    gcol = pad(G.reshape(L, DN_H))
    grow = jnp.pad(jnp.transpose(G, (0, 2, 1)), ((0, 0), (0, 8 - DN_H), (0, 0)))
    return gcol, grow, pad(beta)


def dn_block_fwd(rin, cw, a_log, dt_bias, out_norm, tag, comm=None):
    gates, gates_vjp = jax.vjp(dn_gates, rin[:, REC_A0:REC_A0 + DN_H], rin[:, REC_A0 + DN_H:REC_IN], a_log, dt_bias)
    qkv = dn_prep_fwd(rin, cw, tag + "_prep")
    (o, shist, thist, solhist), got = _with_comm(dn_chunk_fwd(qkv, *gates, tag + "_chunk", comm=comm), comm)
    yd = dn_out_fwd(o, rin, out_norm.reshape(1, 128), tag + "_onorm")
    return yd, (qkv, gates, gates_vjp, o, shist, thist, solhist), got


def dn_block_bwd(dyd, res, rin, cw, out_norm, tag, comm=None):
    qkv, gates, gates_vjp, o, shist, thist, solhist = res
    do, dz, nsum = dn_out_bwd(dyd, o, rin, out_norm.reshape(1, 128), tag + "_donorm")
    (dqkv, dgc, dgr, db), got = _with_comm(dn_chunk_bwd(qkv, *gates, shist, thist, solhist, do, tag + "_dchunk",
                                                        comm=comm), comm)
    da, dbraw, g_alog, g_dtb = gates_vjp((dgc, dgr, db))
    dx, csum = dn_prep_bwd(rin, cw, dqkv, tag + "_dprep")
    grads = dict(conv=csum.reshape(4, 8, DN_NT * 128).sum(axis=1), a_log=g_alog, dt_bias=g_dtb,
                 out_norm=nsum.sum(axis=0).reshape(DN_H, 128).sum(axis=0))
    return dx, dz, da, dbraw, grads, got


_HBM = pl.BlockSpec(memory_space=pltpu.HBM)


def _mesh_pos():
    xi, yi, ci = lax.axis_index("x"), lax.axis_index("y"), lax.axis_index("c")
    return xi, yi, ci, 4 * xi + 2 * yi + ci


def _peer(xi, yi, ci, k):
    px = 1 - xi if (k >> 2) & 1 else xi
    py = 1 - yi if (k >> 1) & 1 else yi
    pc = 1 - ci if k & 1 else ci
    return (px, py, pc), 4 * px + 2 * py + pc


def _exchange(xs, gather, name):
    n = len(xs)

    def body(*refs):
        copies = _comm_copies(refs[:n], refs[n:2 * n], *refs[2 * n:], gather)
        for cp in copies:
            cp.start()
        for cp in copies:
            cp.wait()

    return pl.pallas_call(
        body, name=name, in_specs=[_HBM] * n, out_specs=tuple([_HBM] * n),
        out_shape=_comm_out_shapes(xs), scratch_shapes=_comm_sems(n))(*xs)


def _comm_out_shapes(xs):
    return tuple(jax.ShapeDtypeStruct((N_DEV,) + x.shape[-2:], x.dtype) for x in xs)


def _comm_sems(n):
    return [pltpu.SemaphoreType.DMA((n * (N_DEV - 1),)), pltpu.SemaphoreType.DMA((n * (N_DEV - 1),)),
            pltpu.SemaphoreType.DMA((n,))]


def _comm_copies(x_refs, o_refs, send_sems, recv_sems, lsems, gather):
    xi, yi, ci, me = _mesh_pos()
    copies = []
    for t in range(len(x_refs)):
        src_of = (lambda lin, t=t: x_refs[t]) if gather else (lambda lin, t=t: x_refs[t].at[lin])
        copies.append(pltpu.make_async_copy(src_of(me), o_refs[t].at[me], lsems.at[t]))
        for k in range(1, N_DEV):
            peer, lin = _peer(xi, yi, ci, k)
            s = t * (N_DEV - 1) + k - 1
            copies.append(pltpu.make_async_remote_copy(
                src_ref=src_of(lin), dst_ref=o_refs[t].at[me], send_sem=send_sems.at[s],
                recv_sem=recv_sems.at[s], device_id=peer, device_id_type=pl.DeviceIdType.MESH))
    return copies


def _call(body, args, *, name, grid, in_specs, out_specs, out_shape, scratch_shapes=(), sem, comm=None):
    if comm is None:
        return pl.pallas_call(body, name=name, grid=grid, in_specs=in_specs, out_specs=out_specs,
                              out_shape=out_shape, scratch_shapes=list(scratch_shapes),
                              compiler_params=_cparams(*sem))(*args)
    xs, gather = comm
    n = len(xs)
    single = not isinstance(out_shape, (tuple, list))
    outs_shape = (out_shape,) if single else tuple(out_shape)
    outs_specs = (out_specs,) if single else tuple(out_specs)
    n_in, n_out, n_scr = len(in_specs), len(outs_shape), len(scratch_shapes)

    def body2(*refs):
        ins, cx = refs[:n_in], refs[n_in:n_in + n]
        outs = refs[n_in + n:n_in + n + n_out]
        co = refs[n_in + n + n_out:n_in + 2 * n + n_out]
        scr = refs[n_in + 2 * n + n_out:n_in + 2 * n + n_out + n_scr]
        sems = refs[n_in + 2 * n + n_out + n_scr:]
        first = functools.reduce(jnp.logical_and, [pl.program_id(a) == 0 for a in range(len(grid))])
        last = functools.reduce(jnp.logical_and, [pl.program_id(a) == grid[a] - 1 for a in range(len(grid))])

        @pl.when(first)
        def _():
            for cp in _comm_copies(cx, co, *sems, gather):
                cp.start()

        body(*ins, *outs, *scr)

        @pl.when(last)
        def _():
            for cp in _comm_copies(cx, co, *sems, gather):
                cp.wait()

    res = pl.pallas_call(
        body2, name=name, grid=grid, in_specs=list(in_specs) + [_HBM] * n,
        out_specs=outs_specs + tuple([_HBM] * n), out_shape=outs_shape + _comm_out_shapes(xs),
        scratch_shapes=list(scratch_shapes) + _comm_sems(n),
        compiler_params=_cparams(*(["arbitrary"] * len(grid))))(*args, *xs)
    main = res[0] if single else tuple(res[:n_out])
    return main, list(res[n_out:])


def all_gather(x, name):
    return _exchange([x], True, name)[0]


def all_gather_many(xs, name):
    return _exchange(xs, True, name)


def all_to_all_many(xs, name):
    return _exchange(xs, False, name)


def reduce_adamw(gsrc, w, m, v, name, comm=None):
    parts = list(gsrc) if isinstance(gsrc, (list, tuple)) else [gsrc]
    S, R0, C = parts[0].shape
    R = R0 * len(parts)
    tr = _rtile(R0, max(16, min(256, (4 << 20) // (S * C * 4) // 16 * 16)), 16 if R0 % 16 == 0 else 8)
    n0 = R0 // tr
    c1 = 1.0 - ADAM_B1 ** ADAM_STEP
    c2 = 1.0 - ADAM_B2 ** ADAM_STEP

    def body(*refs):
        g_refs = refs[:len(parts)]
        w_ref, m_ref, v_ref, go_ref, d_ref, mo_ref, vo_ref = refs[len(parts):]
        for p, g_ref in enumerate(g_refs):
            @pl.when(pl.program_id(0) // n0 == p)
            def _(g_ref=g_ref):
                acc = g_ref[0].astype(F32)
                for s in range(1, S):
                    acc = acc + g_ref[s].astype(F32)
                go_ref[...] = acc
        g = go_ref[...]
        mn = ADAM_B1 * m_ref[...] + (1.0 - ADAM_B1) * g
        vn = ADAM_B2 * v_ref[...] + (1.0 - ADAM_B2) * (g * g)
        mo_ref[...] = mn
        vo_ref[...] = vn
        d_ref[...] = -ADAM_LR * ((mn / c1) / (jnp.sqrt(vn / c2) + ADAM_EPS) + ADAM_WD * w_ref[...])

    big = pl.BlockSpec((tr, C), lambda i: (i, 0))
    o = jax.ShapeDtypeStruct((R, C), F32)
    part_spec = lambda p: pl.BlockSpec((S, tr, C), lambda i: (0, jnp.clip(i - p * n0, 0, n0 - 1), 0))
    return _call(body, (*parts, w, m, v), name=name, grid=(R // tr,),
                 in_specs=[part_spec(p) for p in range(len(parts))] + [big, big, big],
                 out_specs=(big, big, big, big), out_shape=(o, o, o, o), sem=("parallel",), comm=comm)


def _to_slabs(g, ax):
    shp = g.shape
    g = g.reshape(shp[:ax] + (N_DEV, shp[ax] // N_DEV) + shp[ax + 1:])
    return jnp.moveaxis(g, ax, 0).reshape(N_DEV, -1)


def _from_slabs(s, ax, shp):
    s = s.reshape((N_DEV,) + shp[:ax] + (shp[ax] // N_DEV,) + shp[ax + 1:])
    return jnp.moveaxis(s, 0, ax).reshape(shp)


def _pack_rows(flat, width, row_mult):
    n = flat.shape[-1]
    per = width * row_mult
    tot = -(-n // per) * per
    flat = jnp.pad(flat, [(0, 0)] * (flat.ndim - 1) + [(0, tot - n)])
    return flat.reshape(flat.shape[:-1] + (tot // width, width))


def _offsets(sizes):
    offs, o = [], 0
    for s in sizes:
        offs.append(o)
        o += s
    return offs


WEIGHTS = ['ada_w', 'ada_b', 'norm_mix', 'norm_ffn', 'attn_w_in', 'attn_q_norm_a', 'attn_k_norm_a', 'attn_q_norm_b',
           'attn_k_norm_b', 'attn_sinks', 'attn_w_out', 'rec_w_in', 's5_lambda_re', 's5_lambda_im', 's5_log_dt',
           's5_b_re', 's5_b_im', 's5_c_re', 's5_c_im', 's5_d', 's5_glu_w', 's5_glu_b', 'dn_conv', 'dn_a_log',
           'dn_dt_bias', 'dn_out_norm', 'rec_w_out', 'ffn_w_up', 'ffn_conv', 'ffn_w_down']
BIG = [('attn_w_in', (D, ATTN_IN // N_DEV)), ('attn_w_out', (D // N_DEV, D)), ('rec_w_in', (D // N_DEV, REC_PAD)),
       ('s5_glu_w', (S5_W // N_DEV, S5_W)), ('rec_w_out', (D // N_DEV, D)), ('ffn_w_up', (2 * D, 2 * D_FF // N_DEV)),
       ('ffn_w_down', (2 * D_FF // N_DEV, D))]


def _shard2d(name, t):
    if name == 'rec_w_in':
        return jnp.pad(t[0], ((0, 0), (0, REC_PAD - REC_IN)))
    return t.reshape((-1, t.shape[-1]))


def _cols_to_slabs(g, k=N_DEV):
    r, n = g.shape
    return jnp.transpose(g.reshape(r, k, n // k), (1, 0, 2))


def _slabs_to_cols(s):
    k, r, c_ = s.shape
    return jnp.transpose(s, (1, 0, 2)).reshape(r, k * c_)
SMALL_SHARDED = [('s5_d', 1, (1, S5_W)), ('s5_glu_b', 1, (1, S5_W)), ('dn_conv', 2, (1, 4, 2304)),
                 ('ffn_conv', 2, (2, 3, 2 * D_FF))]
REPLICATED = [('ada_b', (2, 6 * D)), ('norm_mix', (2, D)), ('norm_ffn', (2, D)), ('attn_q_norm_a', (1, HD)),
              ('attn_k_norm_a', (1, HD)), ('attn_q_norm_b', (1, HD)), ('attn_k_norm_b', (1, HD)),
              ('attn_sinks', (1, 8)), ('s5_lambda_re', (1, 16, 64)), ('s5_lambda_im', (1, 16, 64)),
              ('s5_log_dt', (1, 16)), ('s5_b_re', (1, 16, 64, 16)), ('s5_b_im', (1, 16, 64, 16)),
              ('s5_c_re', (1, 16, 16, 64)), ('s5_c_im', (1, 16, 16, 64)), ('dn_a_log', (1, DN_H)),
              ('dn_dt_bias', (1, DN_H)), ('dn_out_norm', (1, 128))]


def _numel(shp):
    return int(np.prod(shp))


def kernel(x, c, ada_w, ada_b, norm_mix, norm_ffn, attn_w_in, attn_q_norm_a, attn_k_norm_a, attn_q_norm_b, attn_k_norm_b, attn_sinks, attn_w_out, rec_w_in, s5_lambda_re, s5_lambda_im, s5_log_dt, s5_b_re, s5_b_im, s5_c_re, s5_c_im, s5_d, s5_glu_w, s5_glu_b, dn_conv, dn_a_log, dn_dt_bias, dn_out_norm, rec_w_out, ffn_w_up, ffn_conv, ffn_w_down, loss_target, m_ada_w, m_ada_b, m_norm_mix, m_norm_ffn, m_attn_w_in, m_attn_q_norm_a, m_attn_k_norm_a, m_attn_q_norm_b, m_attn_k_norm_b, m_attn_sinks, m_attn_w_out, m_rec_w_in, m_s5_lambda_re, m_s5_lambda_im, m_s5_log_dt, m_s5_b_re, m_s5_b_im, m_s5_c_re, m_s5_c_im, m_s5_d, m_s5_glu_w, m_s5_glu_b, m_dn_conv, m_dn_a_log, m_dn_dt_bias, m_dn_out_norm, m_rec_w_out, m_ffn_w_up, m_ffn_conv, m_ffn_w_down, v_ada_w, v_ada_b, v_norm_mix, v_norm_ffn, v_attn_w_in, v_attn_q_norm_a, v_attn_k_norm_a, v_attn_q_norm_b, v_attn_k_norm_b, v_attn_sinks, v_attn_w_out, v_rec_w_in, v_s5_lambda_re, v_s5_lambda_im, v_s5_log_dt, v_s5_b_re, v_s5_b_im, v_s5_c_re, v_s5_c_im, v_s5_d, v_s5_glu_w, v_s5_glu_b, v_dn_conv, v_dn_a_log, v_dn_dt_bias, v_dn_out_norm, v_rec_w_out, v_ffn_w_up, v_ffn_conv, v_ffn_w_down):
    loc = locals()
    W = {n: loc[n] for n in WEIGHTS}
    M = {n: loc["m_" + n] for n in WEIGHTS}
    V = {n: loc["v_" + n] for n in WEIGHTS}
    _, _, _, me = _mesh_pos()
    L = x.shape[1]
    x0, tgt = x[0], loss_target[0]

    small_in = jnp.concatenate([c.reshape(-1)] + [W[n].reshape(-1) for n, _, _ in SMALL_SHARDED])
    si, att_in_all = all_gather_many([_pack_rows(small_in, 1024, 8), attn_w_in[0].astype(BF16)], "gather_first")
    si = si.reshape(N_DEV, -1)
    c_all = si[:, :D]
    off = D
    small_full = {}
    for n, ax, shp in SMALL_SHARDED:
        k = _numel(shp) // N_DEV
        small_full[n] = _from_slabs(si[:, off:off + k], ax, shp)
        off += k

    cond_all = jax.nn.silu(c_all)
    modp = jnp.concatenate([matmul([(cond_all, ada_w[l].astype(BF16))], "nn", f"ada{l}") for l in range(2)], axis=0)
    modp_all = all_gather(modp, "gather_mod")
    mods = []
    for l in range(2):
        row = lax.dynamic_index_in_dim(modp_all, l * N_DEV + me, axis=1, keepdims=False)
        mod = row.reshape(1, 6 * D) + ada_b[l].reshape(1, 6 * D)
        mods.append([mod[:, i * D:(i + 1) * D] for i in range(6)])

    w_att_in = _slabs_to_cols(att_in_all)
    bf = lambda t: t.astype(BF16)
    ffn_shards = [[bf(ffn_w_up[l]), bf(ffn_w_down[l])] for l in range(2)]
    rec_shards = [bf(_shard2d('rec_w_in', rec_w_in)), bf(s5_glu_w[0]), bf(rec_w_out[0])]
    ffn_cw = [small_full['ffn_conv'][l] for l in range(2)]
    dn_cw = small_full['dn_conv'][0]
    s5_dskip, glu_b = small_full['s5_d'], small_full['s5_glu_b']
    row = lambda t: t.reshape(1, -1)

    sh1, sc1, g1, sh2, sc2, g2 = mods[0]
    h1 = gate_norm_fwd(x0, None, None, row(norm_mix[0]), sh1, sc1, "l0_norm1")
    wvec, sinkvec = attn_vectors(attn_q_norm_a[0], attn_k_norm_a[0], attn_q_norm_b[0], attn_k_norm_b[0], attn_sinks[0])
    y0, res_att, got = attention_block_fwd(
        h1, w_att_in, wvec, sinkvec, None, "att",
        comms={'swa': ([ffn_shards[0][0][:D // 2]], True), 1: ([ffn_shards[0][0][D // 2:]], True),
               4: (ffn_shards[0][1:], True), 16: ([bf(attn_w_out[0])], True)})
    w_att_out = got['w_out']
    split_up = lambda up_all: (_slabs_to_cols(up_all[:4]), _slabs_to_cols(up_all[4:]))
    w_up = [split_up(jnp.concatenate([got['swa'][0], got[1][0]], axis=1))]
    w_down = [got[4][0].reshape(D_FF, D)]
    x1, h2 = gate_norm_fwd(x0, y0, g1, row(norm_ffn[0]), sh2, sc2, "l0_norm2")
    f0, res_f0, got_rec = ffn_block_fwd(h2, w_up[0][0], w_up[0][1], ffn_cw[0], w_down[0], "ffn0",
                                        comm=(rec_shards, True))
    w_rec_in = rec_cols_permute(got_rec[0].reshape(D, REC_PAD))
    glu_w, w_rec_out = got_rec[1].reshape(S5_W, S5_W), got_rec[2].reshape(D, D)
    w_rec_out = jnp.concatenate([w_rec_out[S5_W:], w_rec_out[:S5_W]], axis=0)
    t1, tc1, tg1, t2, tc2, tg2 = mods[1]
    x2, h3 = gate_norm_fwd(x1, f0, g2, row(norm_mix[1]), t1, tc1, "l1_norm1")
    rin = matmul([(h3, w_rec_in)], "nn", "rec_in")
    s5p, s5p_vjp = jax.vjp(s5_params, s5_lambda_re[0], s5_lambda_im[0], s5_log_dt[0], s5_b_re[0], s5_b_im[0],
                           s5_c_re[0], s5_c_im[0])
    u = rin[:, REC_U0:REC_A0]
    yc, res_s5 = s5_block_fwd(u, s5p, s5_dskip, glu_w, glu_b, "s5")
    yd, res_dn, got_ffn1 = dn_block_fwd(rin, dn_cw, dn_a_log[0], dn_dt_bias[0], dn_out_norm[0], "dn",
                                        comm=(([ffn_shards[1][0][:D // 2], ffn_shards[1][1]], True),
                                              ([ffn_shards[1][0][D // 2:]], True)))
    w_up.append(split_up(jnp.concatenate([got_ffn1[0], got_ffn1[2]], axis=1)))
    w_down.append(got_ffn1[1].reshape(D_FF, D))
    ycat = jnp.concatenate([yd, yc], axis=1)
    y1 = matmul([(ycat, w_rec_out)], "nn", "rec_out")
    x3, h4 = gate_norm_fwd(x2, y1, tg1, row(norm_ffn[1]), t2, tc2, "l1_norm2")
    f1, res_f1, _ = ffn_block_fwd(h4, w_up[1][0], w_up[1][1], ffn_cw[1], w_down[1], "ffn1")
    dx4, df1, lsum = final_loss(x3, f1, tg2, tgt, "loss")

    G = {}
    d_tg2 = lsum[8:16].sum(axis=0)
    dh4, gf1, _ = ffn_block_bwd(df1, res_f1, w_up[1][0], w_up[1][1], ffn_cw[1], w_down[1], "ffn1")
    ffn_slabs = lambda g: [g['w_up'], g['w_down'].reshape(N_DEV, D_FF // N_DEV, D)]
    dx3, dy1, s = gate_norm_bwd(x3, y1, tg1, row(norm_ffn[1]), tc2, dx4, dh4, "l1_dnorm2")
    s = s.reshape(4, 8, D).sum(axis=1)
    d_tg1, d_nffn1, d_t2, d_tc2 = s[0], s[1] * (1.0 + tc2[0]), s[2], s[1] * norm_ffn[1]
    g_rec_out = matmul([(ycat, dy1)], "tn", "rec_out_dw", out_dtype=BF16)
    g_rec_out = jnp.concatenate([g_rec_out[DN_W:], g_rec_out[:DN_W]], axis=0).reshape(N_DEV, D // N_DEV, D)
    dycat = matmul([(dy1, w_rec_out)], "nt", "rec_out_dx")
    du, s5cot, gs5 = s5_block_bwd(dycat, res_s5, s5p, s5_dskip, glu_w, glu_b, "s5", dout_col=DN_W // S5_W)
    s5g = s5p_vjp(s5cot)
    dqkv, dz, da, dbraw, gdn, recv_ffn1 = dn_block_bwd(dycat, res_dn, rin, dn_cw, dn_out_norm[0], "dn",
                                                       comm=(ffn_slabs(gf1), False))
    d_rest = jnp.concatenate([du.astype(BF16), da.astype(BF16), dbraw.astype(BF16),
                              jnp.zeros((L, REC_PAD - REC_IN), BF16)], axis=1)
    drin = ((dqkv, 0), (dz, 3 * DN_W), (d_rest, REC_U0))
    g_rec_in = jnp.concatenate([matmul([(h3, p)], "tn", f"rec_in_dw{i}", out_dtype=BF16)
                                for i, (p, _) in enumerate(drin)], axis=1)
    g_rec_in = rec_cols_restore(g_rec_in).reshape(N_DEV, D // N_DEV, REC_PAD)
    g_glu = gs5['glu_w'].astype(BF16).reshape(N_DEV, S5_W // N_DEV, S5_W)
    dh3 = matmul([(p, w_rec_in[:, c0:c0 + p.shape[1]]) for p, c0 in drin], "nt", "rec_in_dx")
    dx2, df0, s = gate_norm_bwd(x2, f0, g2, row(norm_mix[1]), tc1, dx3, dh3, "l1_dnorm1")
    s = s.reshape(4, 8, D).sum(axis=1)
    d_g2, d_nmix1, d_t1, d_tc1 = s[0], s[1] * (1.0 + tc1[0]), s[2], s[1] * norm_mix[1]
    dh2, gf0, recv_rec = ffn_block_bwd(df0, res_f0, w_up[0][0], w_up[0][1], ffn_cw[0], w_down[0], "ffn0",
                                       comm=([g_rec_in, g_glu, g_rec_out], False))
    dx1, dy0, s = gate_norm_bwd(x1, y0, g1, row(norm_ffn[0]), sc2, dx2, dh2, "l0_dnorm2")
    s = s.reshape(4, 8, D).sum(axis=1)
    d_g1, d_nffn0, d_sh2, d_sc2 = s[0], s[1] * (1.0 + sc2[0]), s[2], s[1] * norm_ffn[0]
    dh1, gatt, got_b = attention_block_bwd(dy0, res_att, w_att_in, wvec, sinkvec, w_att_out, "att",
                                           comms={'swa': ([gf0['w_up'][:, :D // 2]], False),
                                                  16: ([gf0['w_up'][:, D // 2:]], False),
                                                  1: (ffn_slabs(gf0)[1:], False)},
                                           send_w_out_on=4)
    recv_ffn0 = [jnp.concatenate([got_b['swa'][0], got_b[16][0]], axis=1), got_b[1][0]]
    (grad_x, s), recv_w_in = gate_norm_bwd(x0, None, None, row(norm_mix[0]), sc1, dx1, dh1, "l0_dnorm1",
                                           comm=([_cols_to_slabs(gatt['w_in'])], False))
    recv_att = [recv_w_in[0], got_b[4][0]]
    s = s.reshape(4, 8, D).sum(axis=1)
    d_nmix0, d_sh1, d_sc1 = s[1] * (1.0 + sc1[0]), s[2], s[1] * norm_mix[0]
    dmod = jnp.stack([jnp.concatenate([d_sh1, d_sc1, d_g1, d_sh2, d_sc2, d_g2]),
                      jnp.concatenate([d_t1, d_tc1, d_tg1, d_t2, d_tc2, d_tg2])])

    P = {'ada_b': dmod, 'norm_mix': jnp.stack([d_nmix0, d_nmix1]), 'norm_ffn': jnp.stack([d_nffn0, d_nffn1]),
         'attn_q_norm_a': gatt['q_norm_a'], 'attn_k_norm_a': gatt['k_norm_a'], 'attn_q_norm_b': gatt['q_norm_b'],
         'attn_k_norm_b': gatt['k_norm_b'], 'attn_sinks': gatt['sinks'],
         's5_lambda_re': s5g[0], 's5_lambda_im': s5g[1], 's5_log_dt': s5g[2], 's5_b_re': s5g[3], 's5_b_im': s5g[4],
         's5_c_re': s5g[5], 's5_c_im': s5g[6], 'dn_a_log': gdn['a_log'], 'dn_dt_bias': gdn['dt_bias'],
         'dn_out_norm': gdn['out_norm'],
         's5_d': gs5['dskip'], 's5_glu_b': gs5['glu_b'], 'dn_conv': gdn['conv'],
         'ffn_conv': jnp.stack([gf0['conv'], gf1['conv']])}

    out = {k: {} for k in ("g", "d", "m", "v")}
    keys = ("g", "d", "m", "v")
    recv = {'attn_w_in': recv_att[0], 'attn_w_out': recv_att[1], 'rec_w_in': recv_rec[0], 's5_glu_w': recv_rec[1],
            'rec_w_out': recv_rec[2]}
    for n, gr_ in recv.items():
        res4 = reduce_adamw(gr_, _shard2d(n, W[n]), _shard2d(n, M[n]), _shard2d(n, V[n]), "adamw_" + n)
        for key, t in zip(keys, res4):
            out[key][n] = (t[:, :REC_IN] if n == 'rec_w_in' else t).reshape(W[n].shape)
    rep_sizes = [_numel(shp) for _, shp in REPLICATED]
    ss_sizes = [_numel(shp) for _, _, shp in SMALL_SHARDED]
    rep_offs = _offsets(rep_sizes + ss_sizes + [1])
    parts = [P[n].reshape(-1) for n, _ in REPLICATED] + [P[n].reshape(-1) for n, _, _ in SMALL_SHARDED]
    parts.append(lsum[0:8].sum().reshape(1))
    spack = _pack_rows(jnp.concatenate(parts), 1024, 8)
    flat2d = lambda t: t.reshape(-1, t.shape[-1])
    sall = None
    for n, idx in (('ffn_w_up', 0), ('ffn_w_down', 1)):
        comm = ([spack], True) if sall is None else None
        res4, got_s = _with_comm(reduce_adamw([recv_ffn0[idx], recv_ffn1[idx]], flat2d(W[n]), flat2d(M[n]),
                                              flat2d(V[n]), "adamw_" + n, comm=comm), comm)
        if got_s is not None:
            sall = got_s[0]
        for key, t in zip(keys, res4):
            out[key][n] = t.reshape(W[n].shape)
    n_rest = sum(ss_sizes) + 1
    pk = lambda d: _pack_rows(jnp.concatenate([d[n].reshape(-1) for n, _ in REPLICATED]
                                              + [jnp.zeros((n_rest,), F32)]), 1024, 8)
    sg, sd_, sm, sv = [t.reshape(-1) for t in reduce_adamw(sall, pk(W), pk(M), pk(V), "adamw_small")]
    loss = 0.5 * sg[rep_offs[-1]] / D

    dmod_all = sall.reshape(N_DEV, -1)[:, :2 * 6 * D].reshape(N_DEV, 2, 6 * D)
    dmod_mine = lax.dynamic_slice_in_dim(dmod_all, me * (6 * D // N_DEV), 6 * D // N_DEV, axis=2)
    g_ada = [matmul([(cond_all, dmod_mine[:, l])], "tn", f"ada{l}_dw")[None] for l in range(2)]
    ada2d = lambda t: t.reshape(2 * D, 6 * D // N_DEV)
    for key, t in zip(("g", "d", "m", "v"), reduce_adamw(g_ada, ada2d(ada_w), ada2d(m_ada_w),
                                                          ada2d(v_ada_w), "adamw_ada_w")):
        out[key]['ada_w'] = t.reshape(ada_w.shape)
    own = []
    for (n, ax, shp), o in zip(SMALL_SHARDED, rep_offs[len(REPLICATED):]):
        slabs = _to_slabs(sg[o:o + _numel(shp)].reshape(shp), ax)
        own.append(lax.dynamic_index_in_dim(slabs, me, axis=0, keepdims=False))
    own_names = [n for n, _, _ in SMALL_SHARDED]
    pk = lambda d: _pack_rows(jnp.concatenate([d[n].reshape(-1) for n in own_names]), 1024, 8)
    og, od, om, ov = [t.reshape(-1) for t in reduce_adamw(_pack_rows(jnp.concatenate(own), 1024, 8)[None],
                                                          pk(W), pk(M), pk(V), "adamw_own")]

    def unpack(names_shapes, bufs):
        o = 0
        for n, shp in names_shapes:
            k = _numel(shp)
            for key, buf in zip(("g", "d", "m", "v"), bufs):
                out[key][n] = buf[o:o + k].reshape(shp)
            o += k

    unpack(REPLICATED, (sg, sd_, sm, sv))
    unpack([(n, W[n].shape) for n in own_names], (og, od, om, ov))
    return (loss, grad_x[None], *[out["g"][n] for n in WEIGHTS], *[out["d"][n] for n in WEIGHTS],
            *[out["m"][n] for n in WEIGHTS], *[out["v"][n] for n in WEIGHTS])
```

```python
import functools
import math

import numpy as np
import jax
import jax.numpy as jnp
from jax import lax
from jax.experimental import pallas as pl
from jax.experimental.pallas import tpu as pltpu

F32 = jnp.float32
BF16 = jnp.bfloat16

N_DEV = 8
D = 1024
HD = 64
BLK = 128
ATTN_IN = 2304
CB = ATTN_IN // 128
B_BRANCHES = ((128, 1), (512, 4), (2048, 16))
S5_W = 256
S5_P = 1024
DN_H = 6
DN_DK = 128
DN_C = 64
REC_IN = 3340
REC_PAD = 3456
D_FF = 2816
EPS = 1e-6
ADAM_LR, ADAM_B1, ADAM_B2, ADAM_EPS, ADAM_WD, ADAM_STEP = 0.001, 0.9, 0.999, 1e-8, 0.01, 10
VMEM_LIMIT = 48 * 1024 * 1024

ALIBI = np.asarray(2.0 ** (-8.0 * np.arange(1, 17) / 16), dtype=np.float32)


def _cparams(*sem):
    return pltpu.CompilerParams(dimension_semantics=tuple(sem), vmem_limit_bytes=VMEM_LIMIT)


def _tile(n, target):
    if n <= target:
        return n
    best = None
    for t in range(128, target + 1, 128):
        if n % t == 0:
            best = t
    assert best is not None, (n, target)
    return best


def _rtile(n, target, mult=8):
    if n <= target:
        return n
    best = None
    for t in range(mult, target + 1, mult):
        if n % t == 0:
            best = t
    assert best is not None, (n, target)
    return best


def _fold8(x):
    r, c = x.shape
    return x.reshape(r // 8, 8, c).sum(axis=0)


def _sigmoid(x):
    return 1.0 / (1.0 + jnp.exp(-x))


_DIMS = {"nn": (((1,), (0,)), ((), ())), "nt": (((1,), (1,)), ((), ())), "tn": (((0,), (0,)), ((), ()))}


MM_FULL_K = 3584


MM_VMEM_BUDGET = 40 << 20


def matmul(pairs, mode, name, out_dtype=F32, tm=1024, tn=1536, tk=1024):
    a0, b0 = pairs[0]
    if mode == "nn":
        (M, K), N = a0.shape, b0.shape[1]
    elif mode == "nt":
        (M, K), N = a0.shape, b0.shape[0]
    else:
        (K, M), N = a0.shape, b0.shape[1]
        tm = 1536
    tn = _tile(N, tn)
    tk = K if K <= MM_FULL_K else _tile(K, tk)
    nk = K // tk
    npair = len(pairs)
    dims = _DIMS[mode]
    kdim = 0 if mode == "tn" else 1
    tks = [a.shape[kdim] for a, _ in pairs]
    assert all(t == K for t in tks) or (nk == 1 and max(tks) <= MM_FULL_K), tks
    if nk > 1:
        tks = [tk] * npair

    def planned(tm_):
        ab = sum(tm_ * t * a.dtype.itemsize + t * tn * b.dtype.itemsize for (a, b), t in zip(pairs, tks))
        return 2 * ab + 2 * tm_ * tn * jnp.dtype(out_dtype).itemsize + (tm_ * tn * 4 if nk > 1 else 0)

    while True:
        tm_try = _rtile(M, tm) if M % 128 else _tile(M, tm)
        if planned(tm_try) <= MM_VMEM_BUDGET or tm <= 128:
            break
        tm //= 2
    tm = tm_try

    def body(*refs):
        o_ref = refs[2 * npair]
        tot = None
        for p in range(npair):
            part = lax.dot_general(refs[2 * p][...].astype(BF16), refs[2 * p + 1][...].astype(BF16),
                                   dims, preferred_element_type=F32)
            tot = part if tot is None else tot + part
        if nk == 1:
            o_ref[...] = tot.astype(o_ref.dtype)
            return
        acc_ref = refs[2 * npair + 1]
        k = pl.program_id(2)

        @pl.when(k == 0)
        def _():
            acc_ref[...] = tot

        @pl.when(k > 0)
        def _():
            acc_ref[...] += tot

        @pl.when(k == nk - 1)
        def _():
            o_ref[...] = acc_ref[...].astype(o_ref.dtype)

    def specs(t):
        if mode == "nn":
            return [pl.BlockSpec((tm, t), lambda j, i, k: (i, k)), pl.BlockSpec((t, tn), lambda j, i, k: (k, j))]
        if mode == "nt":
            return [pl.BlockSpec((tm, t), lambda j, i, k: (i, k)), pl.BlockSpec((tn, t), lambda j, i, k: (j, k))]
        return [pl.BlockSpec((t, tm), lambda j, i, k: (k, i)), pl.BlockSpec((t, tn), lambda j, i, k: (k, j))]

    flat = [t for pr in pairs for t in pr]
    return pl.pallas_call(
        body, name=name, grid=(N // tn, M // tm, nk),
        in_specs=[s for t in tks for s in specs(t)],
        out_specs=pl.BlockSpec((tm, tn), lambda j, i, k: (i, j)),
        out_shape=jax.ShapeDtypeStruct((M, N), out_dtype),
        scratch_shapes=[pltpu.VMEM((tm, tn), F32)] if nk > 1 else [],
        compiler_params=_cparams("parallel", "parallel", "arbitrary"),
    )(*flat)


def gate_norm_fwd(x, y, gate, nw, sh, sc, name):
    L, C = x.shape
    tl = _rtile(L, 512)
    has_gate = y is not None

    def body(*refs):
        if has_gate:
            x_ref, y_ref, g_ref, nw_ref, sh_ref, sc_ref, xn_ref, h_ref = refs
            xn = x_ref[...] + g_ref[...] * y_ref[...]
            xn_ref[...] = xn
        else:
            x_ref, nw_ref, sh_ref, sc_ref, h_ref = refs
            xn = x_ref[...]
        r = lax.rsqrt(jnp.mean(xn * xn, axis=-1, keepdims=True) + EPS)
        h = (xn * r * nw_ref[...]) * (1.0 + sc_ref[...]) + sh_ref[...]
        h_ref[...] = h.astype(BF16)

    big = pl.BlockSpec((tl, C), lambda i: (i, 0))
    vec = pl.BlockSpec((1, C), lambda i: (0, 0))
    if has_gate:
        ins, in_specs = (x, y, gate, nw, sh, sc), [big, big, vec, vec, vec, vec]
        out_shape = (jax.ShapeDtypeStruct((L, C), F32), jax.ShapeDtypeStruct((L, C), BF16))
        out_specs = (big, big)
    else:
        ins, in_specs = (x, nw, sh, sc), [big, vec, vec, vec]
        out_shape = jax.ShapeDtypeStruct((L, C), BF16)
        out_specs = big
    return pl.pallas_call(body, name=name, grid=(L // tl,), in_specs=in_specs, out_specs=out_specs,
                          out_shape=out_shape, compiler_params=_cparams("parallel"))(*ins)


def gate_norm_bwd(xn, y, gate, nw, sc, dxn_direct, dh, name, comm=None):
    L, C = xn.shape
    tl = _rtile(L, 256)
    has_gate = y is not None
    has_direct = dxn_direct is not None

    def body(*refs):
        refs = list(refs)
        xn_ref = refs.pop(0)
        y_ref = refs.pop(0) if has_gate else None
        g_ref = refs.pop(0) if has_gate else None
        nw_ref = refs.pop(0)
        sc_ref = refs.pop(0)
        dd_ref = refs.pop(0) if has_direct else None
        dh_ref = refs.pop(0)
        dxn_ref = refs.pop(0)
        dy_ref = refs.pop(0) if has_gate else None
        sums_ref = refs.pop(0)

        @pl.when(pl.program_id(0) == 0)
        def _():
            sums_ref[...] = jnp.zeros_like(sums_ref)

        xv = xn_ref[...]
        dh_v = dh_ref[...]
        r = lax.rsqrt(jnp.mean(xv * xv, axis=-1, keepdims=True) + EPS)
        n = xv * r
        a = nw_ref[...] * (1.0 + sc_ref[...])
        dn = dh_v * a
        dx = r * (dn - n * jnp.mean(dn * n, axis=-1, keepdims=True))
        if has_direct:
            dx = dx + dd_ref[...]
        dxn_ref[...] = dx
        sums_ref[8:16, :] += _fold8(dh_v * n)
        sums_ref[16:24, :] += _fold8(dh_v)
        if has_gate:
            dy_ref[...] = (dx * g_ref[...]).astype(BF16)
            sums_ref[0:8, :] += _fold8(dx * y_ref[...])

    big = pl.BlockSpec((tl, C), lambda i: (i, 0))
    vec = pl.BlockSpec((1, C), lambda i: (0, 0))
    ins, in_specs = [xn], [big]
    if has_gate:
        ins += [y, gate]
        in_specs += [big, vec]
    ins += [nw, sc]
    in_specs += [vec, vec]
    if has_direct:
        ins.append(dxn_direct)
        in_specs.append(big)
    ins.append(dh)
    in_specs.append(big)
    out_shape = [jax.ShapeDtypeStruct((L, C), F32)]
    out_specs = [big]
    if has_gate:
        out_shape.append(jax.ShapeDtypeStruct((L, C), BF16))
        out_specs.append(big)
    out_shape.append(jax.ShapeDtypeStruct((32, C), F32))
    out_specs.append(pl.BlockSpec((32, C), lambda i: (0, 0)))
    return _call(body, ins, name=name, grid=(L // tl,), in_specs=in_specs, out_specs=tuple(out_specs),
                 out_shape=tuple(out_shape), sem=("arbitrary",), comm=comm)


def final_loss(x, f, gate, target, name):
    L, C = x.shape
    tl = _rtile(L, 256)

    def body(x_ref, f_ref, g_ref, t_ref, dy_ref, df_ref, sums_ref):
        @pl.when(pl.program_id(0) == 0)
        def _():
            sums_ref[...] = jnp.zeros_like(sums_ref)

        fv = f_ref[...]
        err = x_ref[...] + g_ref[...] * fv - t_ref[...]
        dy = err * (1.0 / C)
        dy_ref[...] = dy
        df_ref[...] = (dy * g_ref[...]).astype(BF16)
        sums_ref[0:8, :] += _fold8(err * err)
        sums_ref[8:16, :] += _fold8(dy * fv)

    big = pl.BlockSpec((tl, C), lambda i: (i, 0))
    vec = pl.BlockSpec((1, C), lambda i: (0, 0))
    return pl.pallas_call(
        body, name=name, grid=(L // tl,), in_specs=[big, big, vec, big],
        out_specs=(big, big, pl.BlockSpec((16, C), lambda i: (0, 0))),
        out_shape=(jax.ShapeDtypeStruct((L, C), F32), jax.ShapeDtypeStruct((L, C), BF16),
                   jax.ShapeDtypeStruct((16, C), F32)),
        compiler_params=_cparams("arbitrary"))(x, f, gate, target)


def _seg_ones(seg):
    r = lax.broadcasted_iota(jnp.int32, (128, 128), 0) // seg
    c = lax.broadcasted_iota(jnp.int32, (128, 128), 1) // seg
    return (r == c).astype(BF16)


def _segsum(t, ones):
    hi = t.astype(BF16)
    lo = (t - hi.astype(F32)).astype(BF16)
    return (jnp.dot(hi, ones, preferred_element_type=F32) + jnp.dot(lo, ones, preferred_element_type=F32))


_NORMED_TILES = tuple(list(range(0, 5)) + list(range(6, 14)))


DIL = (4, 16)
B_COLS0, B_W = 768, 1536
DIL_TL = 256


def _to_dilated(scr_ref, out_ref, d, cast=None):
    nj, tl, _ = scr_ref.shape
    for r in range(d):
        for j in range(nj):
            piece = scr_ref[j, pl.ds(r, tl // d, stride=d), :]
            c0 = (r * nj + j) * 128
            out_ref[:, c0:c0 + 128] = piece if cast is None else piece.astype(cast)


def _from_dilated(in_ref, scr_ref, d):
    nj, tl, _ = scr_ref.shape
    for r in range(d):
        for j in range(nj):
            c0 = (r * nj + j) * 128
            scr_ref[j, pl.ds(r, tl // d, stride=d), :] = in_ref[:, c0:c0 + 128]


def _dil_spec(tl, d, width):
    return pl.BlockSpec((tl // d, d * width), lambda i: (i, 0))


def qknorm_fwd(qkv, wvec, name):
    L, C = qkv.shape
    tl = DIL_TL

    def body(x_ref, w_ref, o_ref, o4_ref, o16_ref, scr_ref):
        ones = _seg_ones(HD)
        for t in range(CB):
            cs = slice(t * 128, (t + 1) * 128)
            x = x_ref[:, cs]
            if t in _NORMED_TILES:
                ms = _segsum(x * x, ones) * (1.0 / HD)
                x = x * lax.rsqrt(ms + EPS) * w_ref[:, cs]
            o_ref[:, cs] = x.astype(BF16)
            if t * 128 >= B_COLS0:
                scr_ref[t - B_COLS0 // 128] = x
        _to_dilated(scr_ref, o4_ref, 4, BF16)
        _to_dilated(scr_ref, o16_ref, 16, BF16)

    return pl.pallas_call(
        body, name=name, grid=(L // tl,),
        in_specs=[pl.BlockSpec((tl, C), lambda i: (i, 0)), pl.BlockSpec((1, C), lambda i: (0, 0))],
        out_specs=(pl.BlockSpec((tl, C), lambda i: (i, 0)), _dil_spec(tl, 4, B_W), _dil_spec(tl, 16, B_W)),
        out_shape=(jax.ShapeDtypeStruct((L, C), BF16), jax.ShapeDtypeStruct((L // 4, 4 * B_W), BF16),
                   jax.ShapeDtypeStruct((L // 16, 16 * B_W), BF16)),
        scratch_shapes=[pltpu.VMEM((B_W // 128, tl, 128), F32)], compiler_params=_cparams("parallel"))(qkv, wvec)


def qknorm_bwd(qkv, wvec, d_a, d_b, name):
    L, C = qkv.shape
    tl = DIL_TL

    def body(x_ref, w_ref, dqa, dka, dva, q1, k1, v1, q4, k4, v4, q16, k16, v16, dx_ref, sums_ref,
             dy_ref, s4_ref, s16_ref):
        @pl.when(pl.program_id(0) == 0)
        def _():
            sums_ref[...] = jnp.zeros_like(sums_ref)

        dy_ref[:, 0:512] = dqa[...]
        for off, ref in ((512, dka), (640, dva)):
            for g in range(2):
                acc = ref[:, g * 256:g * 256 + HD]
                for h in range(1, 4):
                    acc = acc + ref[:, g * 256 + h * HD:g * 256 + (h + 1) * HD]
                dy_ref[:, off + g * HD:off + (g + 1) * HD] = acc
        for off, r1, r4, r16 in ((768, q1, q4, q16), (1280, k1, k4, k16), (1792, v1, v4, v16)):
            _from_dilated(r4, s4_ref, 4)
            _from_dilated(r16, s16_ref, 16)
            for j in range(4):
                dy_ref[:, off + j * 128:off + (j + 1) * 128] = r1[:, j * 128:(j + 1) * 128] + s4_ref[j] + s16_ref[j]

        ones = _seg_ones(HD)
        for t in range(CB):
            cs = slice(t * 128, (t + 1) * 128)
            d = dy_ref[:, cs]
            if t in _NORMED_TILES:
                x = x_ref[:, cs]
                r = lax.rsqrt(_segsum(x * x, ones) * (1.0 / HD) + EPS)
                n = x * r
                dn = d * w_ref[:, cs]
                dx_ref[:, cs] = (r * (dn - n * (_segsum(dn * n, ones) * (1.0 / HD)))).astype(BF16)
                sums_ref[:, cs] += _fold8(d * n)
            else:
                dx_ref[:, cs] = d.astype(BF16)

    big = pl.BlockSpec((tl, C), lambda i: (i, 0))
    p512 = pl.BlockSpec((tl, 512), lambda i: (i, 0))
    return pl.pallas_call(
        body, name=name, grid=(L // tl,),
        in_specs=[big, pl.BlockSpec((1, C), lambda i: (0, 0))] + [p512] * 6 + [_dil_spec(tl, 4, 512)] * 3
        + [_dil_spec(tl, 16, 512)] * 3,
        out_specs=(big, pl.BlockSpec((8, C), lambda i: (0, 0))),
        out_shape=(jax.ShapeDtypeStruct((L, C), BF16), jax.ShapeDtypeStruct((8, C), F32)),
        scratch_shapes=[pltpu.VMEM((tl, C), F32), pltpu.VMEM((4, tl, 128), F32), pltpu.VMEM((4, tl, 128), F32)],
        compiler_params=_cparams("arbitrary"))(qkv, wvec, *d_a, *d_b[0], *d_b[1], *d_b[2])


def _attn_biases(t, slopes, step, maxdist):
    qi = lax.broadcasted_iota(jnp.int32, (BLK, 2 * BLK), 0)
    sj = lax.broadcasted_iota(jnp.int32, (BLK, 2 * BLK), 1)
    dist = BLK + qi - sj
    valid = (dist >= 0) & (dist <= maxdist)
    distf = (step * dist).astype(F32)
    inner = [jnp.where(valid, (-sl) * distf, -jnp.inf) for sl in slopes]
    first = [jnp.where((t > 0) | (sj >= BLK), b, -jnp.inf) for b in inner]
    return inner, first


def _attn_scores(q, kw, bias):
    return lax.dot_general(q, kw, (((1,), (1,)), ((), ())), preferred_element_type=F32) + bias


ATT_NQ = 8


def _attn_operands(nq, hp, gqa, q_ref, kh_ref, kc_ref, vh_ref, vc_ref):
    ops = []
    for b in range(nq):
        rows = slice(b * BLK, (b + 1) * BLK)
        prev = slice((b - 1) * BLK, b * BLK)
        for e in range(2):
            cs = slice(e * HD, (e + 1) * HD)
            if gqa:
                ksel = lambda ref, r: jnp.where(hp >= 2, ref[r, 64:128], ref[r, 0:64])
            else:
                ksel = lambda ref, r, cs=cs: ref[r, cs]
            kprev = ksel(kh_ref, slice(0, BLK)) if b == 0 else ksel(kc_ref, prev)
            vprev = ksel(vh_ref, slice(0, BLK)) if b == 0 else ksel(vc_ref, prev)
            ops.append((b, e, rows, cs, q_ref[rows, cs] * (HD ** -0.5),
                        jnp.concatenate([kprev, ksel(kc_ref, rows)], axis=0),
                        jnp.concatenate([vprev, ksel(vc_ref, rows)], axis=0)))
    return ops


def _attn_specs(cb, q_off, k_off, v_off, gqa):
    kcol = (lambda r, hp: r * cb + k_off) if gqa else (lambda r, hp: r * cb + k_off + hp)
    vcol = (lambda r, hp: r * cb + v_off) if gqa else (lambda r, hp: r * cb + v_off + hp)
    return kcol, vcol


def attn_fwd(X, d, q_off, k_off, v_off, gqa, slope0, maxdist, name, comm=None):
    Ls = X.shape[0]
    nq = min(ATT_NQ, Ls // BLK)
    TQ = nq * BLK
    nt = Ls // TQ
    slopes = jnp.asarray(ALIBI)

    def body(sl_ref, q_ref, kh_ref, kc_ref, vh_ref, vc_ref, o_ref, lse_ref):
        hp, t = pl.program_id(1), pl.program_id(2)
        ops = _attn_operands(nq, hp, gqa, q_ref, kh_ref, kc_ref, vh_ref, vc_ref)
        inner, first = _attn_biases(t, [sl_ref[slope0 + 2 * hp + e] for e in range(2)], d, maxdist)
        s = [_attn_scores(q, kw, first[e] if b == 0 else inner[e]) for (b, e, rows, cs, q, kw, vw) in ops]
        m = [jnp.max(x, axis=-1, keepdims=True) for x in s]
        p = [jnp.exp(x - mm) for x, mm in zip(s, m)]
        l = [jnp.sum(x, axis=-1, keepdims=True) for x in p]
        o = [jnp.dot(x.astype(BF16), op[6], preferred_element_type=F32) / ll for x, op, ll in zip(p, ops, l)]
        for (b, e, rows, cs, q, kw, vw), oo, mm, ll in zip(ops, o, m, l):
            o_ref[rows, cs] = oo
            lse_ref[rows, cs] = jnp.broadcast_to(mm + jnp.log(ll), (BLK, HD))

    cb = X.shape[1] // (d * 128)
    kcol, vcol = _attn_specs(cb, q_off, k_off, v_off, gqa)
    tile, blk = (TQ, 128), (BLK, 128)
    halo = lambda t: jnp.maximum(t * nq - 1, 0)
    in_specs = [
        pl.BlockSpec(memory_space=pltpu.SMEM),
        pl.BlockSpec(tile, lambda r, hp, t: (t, r * cb + q_off + hp)),
        pl.BlockSpec(blk, lambda r, hp, t: (halo(t), kcol(r, hp))),
        pl.BlockSpec(tile, lambda r, hp, t: (t, kcol(r, hp))),
        pl.BlockSpec(blk, lambda r, hp, t: (halo(t), vcol(r, hp))),
        pl.BlockSpec(tile, lambda r, hp, t: (t, vcol(r, hp))),
    ]
    out_spec = pl.BlockSpec(tile, lambda r, hp, t: (t, r * 4 + hp))
    out = jax.ShapeDtypeStruct((Ls, d * 512), F32)
    return _call(body, (slopes, X, X, X, X, X), name=name, grid=(d, 4, nt), in_specs=in_specs,
                 out_specs=(out_spec, out_spec), out_shape=(out, out),
                 sem=("parallel", "parallel", "arbitrary"), comm=comm)


def attn_bwd(X, o, lse, do, dlse, d, q_off, k_off, v_off, gqa, slope0, maxdist, name, comm=None):
    Ls = X.shape[0]
    slopes = jnp.asarray(ALIBI)

    nq = min(ATT_NQ, Ls // BLK)
    TQ = nq * BLK
    nt = Ls // TQ
    nt_dims, tn_dims = (((1,), (1,)), ((), ())), (((0,), (0,)), ((), ()))

    def body(sl_ref, q_ref, kh_ref, kc_ref, vh_ref, vc_ref, o_ref, lse_ref, do_ref, dlse_ref,
             dq_ref, dk_ref, dv_ref, ak_ref, av_ref, pk_ref, pv_ref):
        hp, t = pl.program_id(1), pl.program_id(2)

        @pl.when(t == 0)
        def _():
            pk_ref[...] = jnp.zeros_like(pk_ref)
            pv_ref[...] = jnp.zeros_like(pv_ref)

        @pl.when(t < nt)
        def _():
            ops = _attn_operands(nq, hp, gqa, q_ref, kh_ref, kc_ref, vh_ref, vc_ref)
            inner, first = _attn_biases(t, [sl_ref[slope0 + 2 * hp + e] for e in range(2)], d, maxdist)
            sv = [_attn_scores(q, kw, first[e] if b == 0 else inner[e]) for (b, e, rows, cs, q, kw, vw) in ops]
            p = [jnp.exp(s - lse_ref[op[2], op[1] * HD:op[1] * HD + 1]) for s, op in zip(sv, ops)]
            dov = [do_ref[op[2], op[3]] for op in ops]
            delta = [jnp.sum(dd * o_ref[op[2], op[3]], axis=-1, keepdims=True) for dd, op in zip(dov, ops)]
            dob = [dd.astype(BF16) for dd in dov]
            dp = [lax.dot_general(dd, op[6], nt_dims, preferred_element_type=F32) for dd, op in zip(dob, ops)]
            ds = [(pp * (x - dl + dlse_ref[op[2], op[1] * HD:op[1] * HD + 1])).astype(BF16)
                  for pp, x, dl, op in zip(p, dp, delta, ops)]
            dq = [jnp.dot(x, op[5], preferred_element_type=F32) * (HD ** -0.5) for x, op in zip(ds, ops)]
            dkw = [lax.dot_general(x, op[4], tn_dims, preferred_element_type=F32) for x, op in zip(ds, ops)]
            dvw = [lax.dot_general(pp.astype(BF16), dd, tn_dims, preferred_element_type=F32)
                   for pp, dd in zip(p, dob)]
            ak_ref[...] = jnp.zeros_like(ak_ref)
            av_ref[...] = jnp.zeros_like(av_ref)
            for (b, e, rows, cs, q, kw, vw), x, yk, yv in zip(ops, dq, dkw, dvw):
                dq_ref[rows, cs] = x
                ak_ref[b * BLK:(b + 2) * BLK, cs] += yk
                av_ref[b * BLK:(b + 2) * BLK, cs] += yv
            if nt == 1:
                dk_ref[...] = ak_ref[BLK:, :]
                dv_ref[...] = av_ref[BLK:, :]
                return
            last = slice(TQ - BLK, TQ)
            dk_ref[...] = pk_ref[...]
            dv_ref[...] = pv_ref[...]
            dk_ref[last, :] += ak_ref[0:BLK, :]
            dv_ref[last, :] += av_ref[0:BLK, :]
            pk_ref[...] = ak_ref[BLK:, :]
            pv_ref[...] = av_ref[BLK:, :]

        @pl.when(t == nt)
        def _():
            dk_ref[...] = pk_ref[...]
            dv_ref[...] = pv_ref[...]

    cb = X.shape[1] // (d * 128)
    kcol, vcol = _attn_specs(cb, q_off, k_off, v_off, gqa)
    tile, blk = (TQ, 128), (BLK, 128)
    cur = lambda t: jnp.minimum(t, nt - 1)
    halo = lambda t: jnp.maximum(cur(t) * nq - 1, 0)
    ospec = pl.BlockSpec(tile, lambda r, hp, t: (cur(t), r * 4 + hp))
    in_specs = [
        pl.BlockSpec(memory_space=pltpu.SMEM),
        pl.BlockSpec(tile, lambda r, hp, t: (cur(t), r * cb + q_off + hp)),
        pl.BlockSpec(blk, lambda r, hp, t: (halo(t), kcol(r, hp))),
        pl.BlockSpec(tile, lambda r, hp, t: (cur(t), kcol(r, hp))),
        pl.BlockSpec(blk, lambda r, hp, t: (halo(t), vcol(r, hp))),
        pl.BlockSpec(tile, lambda r, hp, t: (cur(t), vcol(r, hp))),
        ospec, ospec, ospec, ospec,
    ]
    shifted = pl.BlockSpec(tile, lambda r, hp, t: (jnp.maximum(t - 1, 0), r * 4 + hp))
    out = jax.ShapeDtypeStruct((Ls, d * 512), F32)
    return _call(body, (slopes, X, X, X, X, X, o, lse, do, dlse), name=name, grid=(d, 4, nt + 1 if nt > 1 else 1),
                 in_specs=in_specs, out_specs=(ospec, shifted, shifted), out_shape=(out, out, out),
                 scratch_shapes=[pltpu.VMEM((TQ + BLK, 128), F32), pltpu.VMEM((TQ + BLK, 128), F32),
                                 pltpu.VMEM((TQ, 128), F32), pltpu.VMEM((TQ, 128), F32)],
                 sem=("parallel", "parallel", "arbitrary"), comm=comm)


def attn_merge_fwd(oa, la, sink, obs, lbs, name):
    L = oa.shape[0]
    tl = DIL_TL

    def body(oa_ref, la_ref, sk_ref, o1, o4, o16, l1, l4, l16, m_ref, so4, so16, sl4, sl16):
        m_ref[:, 0:512] = (oa_ref[...] * _sigmoid(la_ref[...] - sk_ref[...])).astype(BF16)
        for src, dst, d in ((o4, so4, 4), (o16, so16, 16), (l4, sl4, 4), (l16, sl16, 16)):
            _from_dilated(src, dst, d)
        for j in range(4):
            cs = slice(j * 128, (j + 1) * 128)
            a, b, c = l1[:, cs], sl4[j], sl16[j]
            mx = jnp.maximum(jnp.maximum(a, b), c)
            ea, eb, ec = jnp.exp(a - mx), jnp.exp(b - mx), jnp.exp(c - mx)
            inv = 1.0 / (ea + eb + ec)
            m_ref[:, 512 + j * 128:512 + (j + 1) * 128] = (
                (ea * inv) * o1[:, cs] + (eb * inv) * so4[j] + (ec * inv) * so16[j]).astype(BF16)

    big = pl.BlockSpec((tl, 512), lambda i: (i, 0))
    dil = [big, _dil_spec(tl, 4, 512), _dil_spec(tl, 16, 512)]
    return pl.pallas_call(
        body, name=name, grid=(L // tl,),
        in_specs=[big, big, pl.BlockSpec((1, 512), lambda i: (0, 0))] + dil + dil,
        out_specs=pl.BlockSpec((tl, 1024), lambda i: (i, 0)),
        out_shape=jax.ShapeDtypeStruct((L, 1024), BF16), scratch_shapes=[pltpu.VMEM((4, tl, 128), F32)] * 4,
        compiler_params=_cparams("parallel"),
    )(oa, la, sink, *obs, *lbs)


def attn_merge_bwd(dm, oa, la, sink, obs, lbs, name):
    L = oa.shape[0]
    tl = DIL_TL

    def body(dm_ref, oa_ref, la_ref, sk_ref, o1, o4, o16, l1, l4, l16,
             doa_ref, dla_ref, d1, d4, d16, g1, g4, g16, sums_ref, so4, so16, sl4, sl16, sd4, sd16, sg4, sg16):
        @pl.when(pl.program_id(0) == 0)
        def _():
            sums_ref[...] = jnp.zeros_like(sums_ref)

        for src, dst, d in ((o4, so4, 4), (o16, so16, 16), (l4, sl4, 4), (l16, sl16, 16)):
            _from_dilated(src, dst, d)
        ones = _seg_ones(HD)
        for t in range(4):
            cs = slice(t * 128, (t + 1) * 128)
            dma = dm_ref[:, cs]
            keep = _sigmoid(la_ref[:, cs] - sk_ref[:, cs])
            doa_ref[:, cs] = dma * keep
            tt = dma * oa_ref[:, cs] * keep * (1.0 - keep)
            dla_ref[:, cs] = _segsum(tt, ones)
            sums_ref[:, cs] += _fold8(-tt)
            dmb = dm_ref[:, 512 + t * 128:512 + (t + 1) * 128]
            a, b, c = l1[:, cs], sl4[t], sl16[t]
            mx = jnp.maximum(jnp.maximum(a, b), c)
            ea, eb, ec = jnp.exp(a - mx), jnp.exp(b - mx), jnp.exp(c - mx)
            inv = 1.0 / (ea + eb + ec)
            wa, wb, wc = ea * inv, eb * inv, ec * inv
            d1[:, cs] = wa * dmb
            sd4[t] = wb * dmb
            sd16[t] = wc * dmb
            sa = _segsum(dmb * o1[:, cs], ones)
            sb = _segsum(dmb * so4[t], ones)
            sc_ = _segsum(dmb * so16[t], ones)
            mean = wa * sa + wb * sb + wc * sc_
            g1[:, cs] = wa * (sa - mean)
            sg4[t] = wb * (sb - mean)
            sg16[t] = wc * (sc_ - mean)
        for src, dst, d in ((sd4, d4, 4), (sd16, d16, 16), (sg4, g4, 4), (sg16, g16, 16)):
            _to_dilated(src, dst, d)

    big = pl.BlockSpec((tl, 512), lambda i: (i, 0))
    dil = [big, _dil_spec(tl, 4, 512), _dil_spec(tl, 16, 512)]
    sd = jax.ShapeDtypeStruct
    shp = [sd((L, 512), F32), sd((L // 4, 4 * 512), F32), sd((L // 16, 16 * 512), F32)]
    return pl.pallas_call(
        body, name=name, grid=(L // tl,),
        in_specs=[pl.BlockSpec((tl, 1024), lambda i: (i, 0)), big, big,
                  pl.BlockSpec((1, 512), lambda i: (0, 0))] + dil + dil,
        out_specs=tuple([big, big] + dil + dil + [pl.BlockSpec((8, 512), lambda i: (0, 0))]),
        out_shape=tuple([shp[0], shp[0]] + shp + shp + [sd((8, 512), F32)]),
        scratch_shapes=[pltpu.VMEM((4, tl, 128), F32)] * 8, compiler_params=_cparams("arbitrary"),
    )(dm, oa, la, sink, *obs, *lbs)


def _shift_down(x, halo, k, first):
    rows = lax.broadcasted_iota(jnp.int32, (8, x.shape[1]), 0)
    out = pltpu.roll(x, k, axis=0)
    hrows = jnp.where(first, 0.0, pltpu.roll(halo, k, axis=0))
    top = jnp.where(rows < k, hrows, out[0:8, :])
    return jnp.concatenate([top, out[8:, :]], axis=0)


def _shift_up(x, nxt, k):
    tl = x.shape[0]
    rows = lax.broadcasted_iota(jnp.int32, (8, x.shape[1]), 0)
    out = pltpu.roll(x, tl - k, axis=0)
    bottom = jnp.where(rows >= 8 - k, pltpu.roll(nxt, 8 - k, axis=0), out[tl - 8:, :])
    return jnp.concatenate([out[:tl - 8, :], bottom], axis=0)


def _silu(x):
    return x * _sigmoid(x)


def _dsilu(x):
    s = _sigmoid(x)
    return s * (1.0 + x * (1.0 - s))


def ffn_act_fwd(ua, ub, cw, name, comm=None):
    L, F = ua.shape
    tl = _rtile(L, 256)
    tc = _tile(F, 1408)
    hb = tl // 8

    def body(ua_ref, uah_ref, ub_ref, ubh_ref, wa_ref, wb_ref, o_ref, ac_ref, bc_ref):
        first = pl.program_id(1) == 0

        def conv(x_ref, h_ref, w_ref):
            x = x_ref[...]
            h = h_ref[...]
            return (w_ref[2:3, :] * x + w_ref[1:2, :] * _shift_down(x, h, 1, first)
                    + w_ref[0:1, :] * _shift_down(x, h, 2, first))

        a = conv(ua_ref, uah_ref, wa_ref)
        b = conv(ub_ref, ubh_ref, wb_ref)
        ac_ref[...] = a
        bc_ref[...] = b
        o_ref[...] = (_silu(a) * b).astype(BF16)

    main = pl.BlockSpec((tl, tc), lambda j, i: (i, j))
    halo = pl.BlockSpec((8, tc), lambda j, i: (jnp.maximum(i * hb - 1, 0), j))
    wa = pl.BlockSpec((3, tc), lambda j, i: (0, j))
    wb = pl.BlockSpec((3, tc), lambda j, i: (0, j + F // tc))
    f32 = jax.ShapeDtypeStruct((L, F), F32)
    return _call(body, (ua, ua, ub, ub, cw, cw), name=name, grid=(F // tc, L // tl),
                 in_specs=[main, halo, main, halo, wa, wb], out_specs=(main, main, main),
                 out_shape=(jax.ShapeDtypeStruct((L, F), BF16), f32, f32), sem=("parallel", "parallel"), comm=comm)


def ffn_act_bwd(ua, ub, ac, bc, cw, dact, name, comm=None):
    L, F = ua.shape
    tl = _rtile(L, 256)
    tc = _tile(F, 1408)
    nrt = L // tl

    def body(ua_ref, ub_ref, ac_ref, bc_ref, wa_ref, wb_ref, da_ref, dua_ref, dub_ref, sums_ref, ca_ref, cb_ref):
        i = pl.program_id(1)

        @pl.when(i == 0)
        def _():
            sums_ref[...] = jnp.zeros_like(sums_ref)
            ca_ref[...] = jnp.zeros_like(ca_ref)
            cb_ref[...] = jnp.zeros_like(cb_ref)

        a, b = ac_ref[...], bc_ref[...]
        dact_v = da_ref[...]
        dya = dact_v * b * _dsilu(a)
        dyb = dact_v * _silu(a)
        for (dy, w_ref, c_ref, d_ref, x_ref, base) in ((dya, wa_ref, ca_ref, dua_ref, ua_ref, 0),
                                                        (dyb, wb_ref, cb_ref, dub_ref, ub_ref, 24)):
            nxt = c_ref[...]
            ups = (dy, _shift_up(dy, nxt, 1), _shift_up(dy, nxt, 2))
            d_ref[...] = (w_ref[2:3, :] * ups[0] + w_ref[1:2, :] * ups[1] + w_ref[0:1, :] * ups[2]).astype(BF16)
            c_ref[...] = dy[0:8, :]
            x = x_ref[...]
            for k in range(3):
                sums_ref[base + 8 * (2 - k):base + 8 * (2 - k) + 8, :] += _fold8(ups[k] * x)

    rev = lambda i: nrt - 1 - i
    main = pl.BlockSpec((tl, tc), lambda j, i: (rev(i), j))
    wa = pl.BlockSpec((3, tc), lambda j, i: (0, j))
    wb = pl.BlockSpec((3, tc), lambda j, i: (0, j + F // tc))
    ob = jax.ShapeDtypeStruct((L, F), BF16)
    return _call(body, (ua, ub, ac, bc, cw, cw, dact), name=name, grid=(F // tc, nrt),
                 in_specs=[main, main, main, main, wa, wb, main],
                 out_specs=(main, main, pl.BlockSpec((48, tc), lambda j, i: (0, j))),
                 out_shape=(ob, ob, jax.ShapeDtypeStruct((48, F), F32)),
                 scratch_shapes=[pltpu.VMEM((8, tc), F32), pltpu.VMEM((8, tc), F32)],
                 sem=("parallel", "arbitrary"), comm=comm)


def attn_vectors(qna, kna, qnb, knb, sinks):
    ones = jnp.ones((128,), F32)
    wvec = jnp.concatenate([jnp.tile(qna, 8), jnp.tile(kna, 2), ones, jnp.tile(qnb, 8), jnp.tile(knb, 8),
                            jnp.tile(ones, 4)]).reshape(1, ATTN_IN)
    return wvec, jnp.repeat(sinks, HD).reshape(1, 512)


def _with_comm(result, comm):
    return result if comm is not None else (result, None)


def attention_block_fwd(h, w_in, wvec, sinkvec, w_out, tag, comms=None):
    L = h.shape[0]
    comms = comms or {}
    got = {}
    qkv = matmul([(h, w_in)], "nn", tag + "_qkv")
    X, X4, X16 = qknorm_fwd(qkv, wvec, tag + "_qknorm")
    (oa, la), got['swa'] = _with_comm(attn_fwd(X, 1, 0, 4, 5, True, 0, BLK - 1, tag + "_swa",
                                               comm=comms.get('swa')), comms.get('swa'))
    views = {1: (X, 6, 10, 14), 4: (X4, 0, 4, 8), 16: (X16, 0, 4, 8)}
    obs, lbs = [], []
    for window, d in B_BRANCHES:
        xd, qo, ko, vo = views[d]
        (o, l), got[d] = _with_comm(attn_fwd(xd, d, qo, ko, vo, False, 8, window // d,
                                             tag + f"_dil{d}", comm=comms.get(d)), comms.get(d))
        obs.append(o)
        lbs.append(l)
    m = attn_merge_fwd(oa, la, sinkvec, obs, lbs, tag + "_merge")
    if w_out is None:
        w_out = got[16][0].reshape(D, D)
        got['w_out'] = w_out
    y = matmul([(m, w_out)], "nn", tag + "_out")
    return y, (h, qkv, views, oa, la, obs, lbs, m), got


def attention_block_bwd(dy, res, w_in, wvec, sinkvec, w_out, tag, comms=None, send_w_out_on=None):
    h, qkv, views, oa, la, obs, lbs, m = res
    comms = dict(comms or {})
    got = {}
    g_w_out = matmul([(m, dy)], "tn", tag + "_dwout", out_dtype=BF16)
    if send_w_out_on is not None:
        comms[send_w_out_on] = ([g_w_out.reshape(N_DEV, D // N_DEV, D)], False)
    dm = matmul([(dy, w_out)], "nt", tag + "_dm")
    doa, dla, d1, d2, d3, g1, g2, g3, sinksums = attn_merge_bwd(dm, oa, la, sinkvec, obs, lbs, tag + "_dmerge")
    d_a, got['swa'] = _with_comm(attn_bwd(views[1][0], oa, la, doa, dla, 1, 0, 4, 5, True, 0, BLK - 1,
                                          tag + "_dswa", comm=comms.get('swa')), comms.get('swa'))
    d_b = []
    for (window, d), o, l, do, dl in zip(B_BRANCHES, obs, lbs, (d1, d2, d3), (g1, g2, g3)):
        xd, qo, ko, vo = views[d]
        dqkv_d, got[d] = _with_comm(attn_bwd(xd, o, l, do, dl, d, qo, ko, vo, False, 8, window // d,
                                             tag + f"_ddil{d}", comm=comms.get(d)), comms.get(d))
        d_b.append(dqkv_d)
    dqkv, wsums = qknorm_bwd(qkv, wvec, d_a, d_b, tag + "_dqknorm")
    g_w_in = matmul([(h, dqkv)], "tn", tag + "_dwin", out_dtype=BF16)
    dh = matmul([(dqkv, w_in)], "nt", tag + "_dh")
    ws = wsums.sum(axis=0)
    grads = dict(
        w_in=g_w_in, w_out=g_w_out,
        q_norm_a=ws[0:512].reshape(8, HD).sum(axis=0), k_norm_a=ws[512:640].reshape(2, HD).sum(axis=0),
        q_norm_b=ws[768:1280].reshape(8, HD).sum(axis=0), k_norm_b=ws[1280:1792].reshape(8, HD).sum(axis=0),
        sinks=sinksums.sum(axis=0).reshape(8, HD).sum(axis=1))
    return dh, grads, got


def ffn_block_fwd(h, w_up_a, w_up_b, cw, w_down, tag, comm=None):
    ua = matmul([(h, w_up_a)], "nn", tag + "_upa")
    ub = matmul([(h, w_up_b)], "nn", tag + "_upb")
    (act, ac, bc), got = _with_comm(ffn_act_fwd(ua, ub, cw, tag + "_act", comm=comm), comm)
    f = matmul([(act, w_down)], "nn", tag + "_down")
    return f, (h, ua, ub, ac, bc, act), got


def ffn_block_bwd(df, res, w_up_a, w_up_b, cw, w_down, tag, comm=None):
    h, ua, ub, ac, bc, act = res
    g_down = matmul([(act, df)], "tn", tag + "_dwdown", out_dtype=BF16)
    dact = matmul([(df, w_down)], "nt", tag + "_dact")
    (dua, dub, sums), got = _with_comm(ffn_act_bwd(ua, ub, ac, bc, cw, dact, tag + "_dactk", comm=comm), comm)
    g_up = jnp.concatenate([_cols_to_slabs(matmul([(h, dua)], "tn", tag + "_dwupa", out_dtype=BF16), N_DEV // 2),
                            _cols_to_slabs(matmul([(h, dub)], "tn", tag + "_dwupb", out_dtype=BF16), N_DEV // 2)],
                           axis=0)
    dh = matmul([(dua, w_up_a), (dub, w_up_b)], "nt", tag + "_dh")
    s = sums.reshape(2, 3, 8, D_FF).sum(axis=2)
    g_conv = jnp.concatenate([s[0], s[1]], axis=1)
    return dh, dict(w_up=g_up, conv=g_conv, w_down=g_down), got


def s5_params(lam_re, lam_im, log_dt, b_re, b_im, c_re, c_im):
    dt = jnp.exp(log_dt)[:, None]
    mag, ang = jnp.exp(lam_re * dt), lam_im * dt
    a_re, a_im = mag * jnp.cos(ang), mag * jnp.sin(ang)
    nr, ni = a_re - 1.0, a_im
    den = lam_re * lam_re + lam_im * lam_im
    f_re = (nr * lam_re + ni * lam_im) / den
    f_im = (ni * lam_re - nr * lam_im) / den
    eye = jnp.eye(16, dtype=F32)[:, None, :, None]
    bd = lambda b: (eye * jnp.transpose(b, (0, 2, 1))[:, :, None, :]).reshape(S5_W, S5_P)
    cd = lambda c: (eye * jnp.transpose(c, (0, 2, 1))[:, :, None, :]).reshape(S5_P, S5_W)
    flat = lambda t: t.reshape(1, S5_P)
    return flat(a_re), flat(a_im), flat(f_re), flat(f_im), bd(b_re), bd(b_im), cd(c_re), cd(c_im)


def _scan_tables(a_re, a_im, reverse):
    pows = [(a_re, a_im)]
    for _ in range(7):
        pr, pi = pows[-1]
        pows.append((pr * a_re - pi * a_im, pr * a_im + pi * a_re))
    order = list(range(7, -1, -1)) if reverse else list(range(8))
    z = jnp.zeros_like(a_re)
    rows = [pows[0][0], pows[0][1], pows[1][0], pows[1][1], pows[3][0], pows[3][1], z, z]
    rows += [pows[k][0] for k in order] + [pows[k][1] for k in order]
    return jnp.concatenate(rows, axis=0)


def _block_scan(er, ei, tab_ref, cr, ci, reverse):
    rows = lax.broadcasted_iota(jnp.int32, er.shape, 0)
    for idx, s in enumerate((1, 2, 4)):
        if reverse:
            sr, si, keep = pltpu.roll(er, 8 - s, axis=0), pltpu.roll(ei, 8 - s, axis=0), rows < 8 - s
        else:
            sr, si, keep = pltpu.roll(er, s, axis=0), pltpu.roll(ei, s, axis=0), rows >= s
        sr, si = jnp.where(keep, sr, 0.0), jnp.where(keep, si, 0.0)
        ar, ai = tab_ref[2 * idx:2 * idx + 1, :], tab_ref[2 * idx + 1:2 * idx + 2, :]
        er, ei = er + ar * sr - ai * si, ei + ar * si + ai * sr
    pr, pi_ = tab_ref[8:16, :], tab_ref[16:24, :]
    er, ei = er + pr * cr - pi_ * ci, ei + pr * ci + pi_ * cr
    return er, ei


def s5_scan_fwd(bu_re, bu_im, a_re, a_im, f_re, f_im, name):
    L, P = bu_re.shape
    tl = _rtile(L, 512)
    tab = _scan_tables(a_re, a_im, False)
    fvec = jnp.concatenate([f_re, f_im] + [jnp.zeros_like(f_re)] * 6, axis=0)

    def body(br_ref, bi_ref, tab_ref, f_ref, xr_ref, xi_ref, c_ref):
        @pl.when(pl.program_id(0) == 0)
        def _():
            c_ref[...] = jnp.zeros_like(c_ref)

        def blk(i, carry):
            cr, ci = carry
            rows = pl.ds(pl.multiple_of(i * 8, 8), 8)
            br, bi = br_ref[rows, :], bi_ref[rows, :]
            fr, fi = f_ref[0:1, :], f_ref[1:2, :]
            er, ei = _block_scan(fr * br - fi * bi, fr * bi + fi * br, tab_ref, cr, ci, False)
            xr_ref[rows, :] = er
            xi_ref[rows, :] = ei
            return er[7:8, :], ei[7:8, :]

        cr, ci = lax.fori_loop(0, tl // 8, blk, (c_ref[0:1, :], c_ref[1:2, :]))
        c_ref[0:1, :] = cr
        c_ref[1:2, :] = ci

    big = pl.BlockSpec((tl, P), lambda i: (i, 0))
    out = jax.ShapeDtypeStruct((L, P), F32)
    return pl.pallas_call(
        body, name=name, grid=(L // tl,),
        in_specs=[big, big, pl.BlockSpec((24, P), lambda i: (0, 0)), pl.BlockSpec((8, P), lambda i: (0, 0))],
        out_specs=(big, big), out_shape=(out, out), scratch_shapes=[pltpu.VMEM((8, P), F32)],
        compiler_params=_cparams("arbitrary"))(bu_re, bu_im, tab, fvec)


def s5_scan_bwd(dx_re, dx_im, x_re, x_im, bu_re, bu_im, a_re, a_im, f_re, f_im, name):
    L, P = dx_re.shape
    tl = _rtile(L, 256)
    nt = L // tl
    tab = _scan_tables(a_re, -a_im, True)
    fvec = jnp.concatenate([f_re, f_im] + [jnp.zeros_like(f_re)] * 6, axis=0)

    def body(gr_ref, gi_ref, xr_ref, xi_ref, br_ref, bi_ref, tab_ref, f_ref, dbr_ref, dbi_ref, s_ref, c_ref):
        @pl.when(pl.program_id(0) == 0)
        def _():
            c_ref[...] = jnp.zeros_like(c_ref)
            s_ref[...] = jnp.zeros_like(s_ref)

        def blk(k, carry):
            cr, ci = carry
            i = tl // 8 - 1 - k
            rows = pl.ds(pl.multiple_of(i * 8, 8), 8)
            er, ei = _block_scan(gr_ref[rows, :], gi_ref[rows, :], tab_ref, cr, ci, True)
            rid = lax.broadcasted_iota(jnp.int32, er.shape, 0)
            sr = jnp.where(rid == 7, cr, pltpu.roll(er, 7, axis=0))
            si = jnp.where(rid == 7, ci, pltpu.roll(ei, 7, axis=0))
            xr, xi = xr_ref[rows, :], xi_ref[rows, :]
            s_ref[0:8, :] += sr * xr + si * xi
            s_ref[8:16, :] += si * xr - sr * xi
            br, bi = br_ref[rows, :], bi_ref[rows, :]
            s_ref[16:24, :] += er * br + ei * bi
            s_ref[24:32, :] += ei * br - er * bi
            fr, fi = f_ref[0:1, :], f_ref[1:2, :]
            dbr_ref[rows, :] = fr * er + fi * ei
            dbi_ref[rows, :] = fr * ei - fi * er
            return er[0:1, :], ei[0:1, :]

        cr, ci = lax.fori_loop(0, tl // 8, blk, (c_ref[0:1, :], c_ref[1:2, :]))
        c_ref[0:1, :] = cr
        c_ref[1:2, :] = ci

    big = pl.BlockSpec((tl, P), lambda i: (nt - 1 - i, 0))
    out = jax.ShapeDtypeStruct((L, P), F32)
    return pl.pallas_call(
        body, name=name, grid=(nt,),
        in_specs=[big] * 6 + [pl.BlockSpec((24, P), lambda i: (0, 0)), pl.BlockSpec((8, P), lambda i: (0, 0))],
        out_specs=(big, big, pl.BlockSpec((32, P), lambda i: (0, 0))),
        out_shape=(out, out, jax.ShapeDtypeStruct((32, P), F32)), scratch_shapes=[pltpu.VMEM((8, P), F32)],
        compiler_params=_cparams("arbitrary"))(dx_re, dx_im, x_re, x_im, bu_re, bu_im, tab, fvec)


_GK, _GC = math.sqrt(2.0 / math.pi), 0.044715


def _gelu(y):
    return 0.5 * y * (1.0 + jnp.tanh(_GK * (y + _GC * y * y * y)))


def _dgelu(y):
    t = jnp.tanh(_GK * (y + _GC * y * y * y))
    return 0.5 * (1.0 + t) + 0.5 * y * (1.0 - t * t) * _GK * (1.0 + 3.0 * _GC * y * y)


def s5_out_fwd(x_re, x_im, u, cd_re, cd_im, dskip, glu_w, glu_b, name):
    L = u.shape[0]
    tl = _rtile(L, 512)

    def body(xr_ref, xi_ref, u_ref, cr_ref, ci_ref, d_ref, w_ref, b_ref, y_ref, o_ref):
        y = (jnp.dot(xr_ref[...].astype(BF16), cr_ref[...], preferred_element_type=F32)
             - jnp.dot(xi_ref[...].astype(BF16), ci_ref[...], preferred_element_type=F32)
             + d_ref[...] * u_ref[...])
        y_ref[...] = y
        g = _gelu(y)
        z = jnp.dot(g.astype(BF16), w_ref[...], preferred_element_type=F32) + b_ref[...]
        o_ref[...] = (g * _sigmoid(z)).astype(BF16)

    big = pl.BlockSpec((tl, S5_P), lambda i: (i, 0))
    sm = pl.BlockSpec((tl, S5_W), lambda i: (i, 0))
    full = lambda r, c: pl.BlockSpec((r, c), lambda i: (0, 0))
    return pl.pallas_call(
        body, name=name, grid=(L // tl,),
        in_specs=[big, big, sm, full(S5_P, S5_W), full(S5_P, S5_W), full(1, S5_W), full(S5_W, S5_W), full(1, S5_W)],
        out_specs=(sm, sm),
        out_shape=(jax.ShapeDtypeStruct((L, S5_W), F32), jax.ShapeDtypeStruct((L, S5_W), BF16)),
        compiler_params=_cparams("parallel"))(x_re, x_im, u, cd_re, cd_im, dskip, glu_w, glu_b)


def s5_out_bwd(dout, y, u, x_re, x_im, cd_re, cd_im, dskip, glu_w, glu_b, name, dout_col=0):
    L = u.shape[0]
    tl = _rtile(L, 256)
    nt_dims = (((1,), (1,)), ((), ()))
    tn_dims = (((0,), (0,)), ((), ()))

    def body(do_ref, y_ref, u_ref, xr_ref, xi_ref, cr_ref, ci_ref, d_ref, w_ref, b_ref,
             dxr_ref, dxi_ref, du_ref, dcr_ref, dci_ref, dw_ref, s_ref):
        @pl.when(pl.program_id(0) == 0)
        def _():
            dcr_ref[...] = jnp.zeros_like(dcr_ref)
            dci_ref[...] = jnp.zeros_like(dci_ref)
            dw_ref[...] = jnp.zeros_like(dw_ref)
            s_ref[...] = jnp.zeros_like(s_ref)

        yv, dov = y_ref[...], do_ref[...]
        g = _gelu(yv)
        gb = g.astype(BF16)
        sg = _sigmoid(jnp.dot(gb, w_ref[...], preferred_element_type=F32) + b_ref[...])
        dz = dov * g * sg * (1.0 - sg)
        dzb = dz.astype(BF16)
        dg = dov * sg + lax.dot_general(dzb, w_ref[...], nt_dims, preferred_element_type=F32)
        dw_ref[...] += lax.dot_general(gb, dzb, tn_dims, preferred_element_type=F32)
        dy = dg * _dgelu(yv)
        dyb = dy.astype(BF16)
        s_ref[0:8, :] += _fold8(dy * u_ref[...])
        s_ref[8:16, :] += _fold8(dz)
        du_ref[...] = dy * d_ref[...]
        dxr_ref[...] = lax.dot_general(dyb, cr_ref[...], nt_dims, preferred_element_type=F32)
        dxi_ref[...] = -lax.dot_general(dyb, ci_ref[...], nt_dims, preferred_element_type=F32)
        dcr_ref[...] += lax.dot_general(xr_ref[...].astype(BF16), dyb, tn_dims, preferred_element_type=F32)
        dci_ref[...] -= lax.dot_general(xi_ref[...].astype(BF16), dyb, tn_dims, preferred_element_type=F32)

    big = pl.BlockSpec((tl, S5_P), lambda i: (i, 0))
    sm = pl.BlockSpec((tl, S5_W), lambda i: (i, 0))
    full = lambda r, c: pl.BlockSpec((r, c), lambda i: (0, 0))
    sd = jax.ShapeDtypeStruct
    return pl.pallas_call(
        body, name=name, grid=(L // tl,),
        in_specs=[pl.BlockSpec((tl, S5_W), lambda i: (i, dout_col)), sm, sm, big, big, full(S5_P, S5_W),
                  full(S5_P, S5_W), full(1, S5_W), full(S5_W, S5_W), full(1, S5_W)],
        out_specs=(big, big, sm, full(S5_P, S5_W), full(S5_P, S5_W), full(S5_W, S5_W), full(16, S5_W)),
        out_shape=(sd((L, S5_P), F32), sd((L, S5_P), F32), sd((L, S5_W), F32), sd((S5_P, S5_W), F32),
                   sd((S5_P, S5_W), F32), sd((S5_W, S5_W), F32), sd((16, S5_W), F32)),
        compiler_params=_cparams("arbitrary"))(dout, y, u, x_re, x_im, cd_re, cd_im, dskip, glu_w, glu_b)


def s5_block_fwd(u, params, dskip, glu_w, glu_b, tag):
    a_re, a_im, f_re, f_im, bd_re, bd_im, cd_re, cd_im = params
    bu_re = matmul([(u, bd_re.astype(BF16))], "nn", tag + "_bure")
    bu_im = matmul([(u, bd_im.astype(BF16))], "nn", tag + "_buim")
    x_re, x_im = s5_scan_fwd(bu_re, bu_im, a_re, a_im, f_re, f_im, tag + "_scan")
    y, out = s5_out_fwd(x_re, x_im, u, cd_re.astype(BF16), cd_im.astype(BF16), dskip, glu_w, glu_b, tag + "_out")
    return out, (u, bu_re, bu_im, x_re, x_im, y)


def s5_block_bwd(dout, res, params, dskip, glu_w, glu_b, tag, dout_col=0):
    u, bu_re, bu_im, x_re, x_im, y = res
    a_re, a_im, f_re, f_im, bd_re, bd_im, cd_re, cd_im = params
    dxr, dxi, du, dcr, dci, dglu_w, sums = s5_out_bwd(dout, y, u, x_re, x_im, cd_re.astype(BF16), cd_im.astype(BF16),
                                                      dskip, glu_w, glu_b, tag + "_dout", dout_col=dout_col)
    dbr, dbi, acc = s5_scan_bwd(dxr, dxi, x_re, x_im, bu_re, bu_im, a_re, a_im, f_re, f_im, tag + "_dscan")
    du = du + matmul([(dbr, bd_re.astype(BF16)), (dbi, bd_im.astype(BF16))], "nt", tag + "_du")
    dbd_re = matmul([(u, dbr)], "tn", tag + "_dbdre")
    dbd_im = matmul([(u, dbi)], "tn", tag + "_dbdim")
    acc = acc.reshape(4, 8, S5_P).sum(axis=1)
    s = sums.reshape(2, 8, S5_W).sum(axis=1)
    cot = (acc[0:1], acc[1:2], acc[2:3], acc[3:4], dbd_re, dbd_im, dcr, dci)
    return du, cot, dict(dskip=s[0], glu_w=dglu_w, glu_b=s[1])


DN_Z0, DN_NT = 18, 18
REC_U0, REC_A0 = 3072, 3328


def rec_cols_permute(w):
    return jnp.concatenate([w[..., S5_W:REC_A0], w[..., :S5_W], w[..., REC_A0:]], axis=-1)


def rec_cols_restore(w):
    return jnp.concatenate([w[..., REC_U0:REC_A0], w[..., :REC_U0], w[..., REC_A0:]], axis=-1)


DN_W = DN_H * DN_DK
DN_NI = 4


def _dn_conv4(taps, w_ref):
    xc = w_ref[3:4, :] * taps[0]
    for k in range(1, 4):
        xc = xc + w_ref[3 - k:4 - k, :] * taps[k]
    return xc


def dn_prep_fwd(rin, cw, name, comm=None):
    L = rin.shape[0]
    tl = _rtile(L, 256)
    hb = tl // 8

    def body(x_ref, h_ref, w_ref, o_ref):
        j = pl.program_id(0)
        first = pl.program_id(1) == 0
        x, h = x_ref[...], h_ref[...]
        s = _silu(_dn_conv4([x] + [_shift_down(x, h, k, first) for k in range(1, 4)], w_ref))
        scale = jnp.where(j == 0, DN_DK ** -0.5, 1.0)
        for hd in _HEADS:
            cs = slice(hd * 128, (hd + 1) * 128)
            sh = s[:, cs]
            r = lax.rsqrt(jnp.sum(sh * sh, axis=-1, keepdims=True) + EPS)
            o_ref[:, cs] = jnp.where(j < 2, sh * r * scale, sh)

    main = pl.BlockSpec((tl, DN_W), lambda j, i: (i, j))
    halo = pl.BlockSpec((8, DN_W), lambda j, i: (jnp.maximum(i * hb - 1, 0), j))
    return _call(body, (rin, rin, cw), name=name, grid=(3, L // tl),
                 in_specs=[main, halo, pl.BlockSpec((4, DN_W), lambda j, i: (0, j))],
                 out_specs=main, out_shape=jax.ShapeDtypeStruct((L, 3 * DN_W), F32),
                 sem=("parallel", "parallel"), comm=comm)


def dn_prep_bwd(rin, cw, dout, name):
    L = rin.shape[0]
    tl = _rtile(L, 256)
    hb = tl // 8
    nrt = L // tl

    def body(x_ref, h_ref, w_ref, d_ref, dx_ref, s_ref, c_ref):
        j = pl.program_id(0)
        i = pl.program_id(1)
        first = i == nrt - 1

        @pl.when(i == 0)
        def _():
            s_ref[...] = jnp.zeros_like(s_ref)
            c_ref[...] = jnp.zeros_like(c_ref)

        x, h = x_ref[...], h_ref[...]
        taps = [x] + [_shift_down(x, h, k, first) for k in range(1, 4)]
        xc = _dn_conv4(taps, w_ref)
        s = _silu(xc)
        scale = jnp.where(j == 0, DN_DK ** -0.5, 1.0)
        pieces = []
        for hd in _HEADS:
            cs = slice(hd * 128, (hd + 1) * 128)
            sh, d = s[:, cs], d_ref[:, cs]
            r = lax.rsqrt(jnp.sum(sh * sh, axis=-1, keepdims=True) + EPS)
            n = sh * r
            dn = d * scale
            pieces.append(jnp.where(j < 2, r * (dn - n * jnp.sum(dn * n, axis=-1, keepdims=True)), d))
        dxc = jnp.concatenate(pieces, axis=1) * _dsilu(xc)
        nxt = c_ref[...]
        dx_ref[...] = _dn_conv4([dxc] + [_shift_up(dxc, nxt, k) for k in range(1, 4)], w_ref).astype(BF16)
        c_ref[...] = dxc[0:8, :]
        for k in range(4):
            s_ref[8 * (3 - k):8 * (3 - k) + 8, :] += _fold8(dxc * taps[k])

    rev = lambda i: nrt - 1 - i
    main = pl.BlockSpec((tl, DN_W), lambda j, i: (rev(i), j))
    halo = pl.BlockSpec((8, DN_W), lambda j, i: (jnp.maximum(rev(i) * hb - 1, 0), j))
    return pl.pallas_call(
        body, name=name, grid=(3, nrt),
        in_specs=[main, halo, pl.BlockSpec((4, DN_W), lambda j, i: (0, j)), main],
        out_specs=(main, pl.BlockSpec((32, DN_W), lambda j, i: (0, j))),
        out_shape=(jax.ShapeDtypeStruct((L, 3 * DN_W), BF16), jax.ShapeDtypeStruct((32, 3 * DN_W), F32)),
        scratch_shapes=[pltpu.VMEM((8, DN_W), F32)],
        compiler_params=_cparams("parallel", "arbitrary"))(rin, rin, cw, dout)


_HI = lax.Precision.HIGH
_NT = (((1,), (1,)), ((), ()))
_TN = (((0,), (0,)), ((), ()))
_HEADS = tuple(range(DN_H))


def _mm(a, b, dims=(((1,), (0,)), ((), ())), hi=False):
    if hi:
        return lax.dot_general(a, b, dims, precision=_HI, preferred_element_type=F32)
    return lax.dot_general(a.astype(BF16), b.astype(BF16), dims, preferred_element_type=F32)


def _dn_masks():
    ri = lax.broadcasted_iota(jnp.int32, (DN_C, DN_C), 0)
    ci = lax.broadcasted_iota(jnp.int32, (DN_C, DN_C), 1)
    return ri >= ci, ri > ci, (ri == ci).astype(F32)


def _dn_decay(gc, gr, causal):
    gam = [jnp.where(causal, jnp.exp(jnp.where(causal, c - r, 0.0)), 0.0) for c, r in zip(gc, gr)]
    eg, el, gl = _dn_row_decay(gc)
    return gam, eg, el, gl


def _dn_row_decay(gc):
    eg = [jnp.exp(c) for c in gc]
    el = [jnp.exp(c[DN_C - 1:DN_C, :] - c) for c in gc]
    gl = [jnp.exp(c[DN_C - 1:DN_C, :]) for c in gc]
    return eg, el, gl


def _dn_solve(k, v, beta, gam, eg, kk, strict, eye):
    ids = range(len(k))
    nmat = [jnp.where(strict, beta[h] * kk[h] * gam[h], 0.0) for h in ids]
    t = [eye - nmat[h] for h in ids]
    m = [_mm(nmat[h], nmat[h]) for h in ids]
    for step in range(5):
        t = [t[h] + _mm(t[h], m[h]) for h in ids]
        if step < 4:
            m = [_mm(m[h], m[h]) for h in ids]
    res = [eye - t[h] - _mm(nmat[h], t[h], hi=True) for h in ids]
    t = [t[h] + _mm(t[h], res[h]) for h in ids]
    rhs = [jnp.concatenate([v[h] * beta[h], k[h] * (beta[h] * eg[h])], axis=1) for h in ids]
    sol = [_mm(t[h], rhs[h], hi=True) for h in ids]
    return t, sol


def dn_chunk_fwd(qkv, gcol, grow, bcol, name, comm=None):
    L = qkv.shape[0]
    C, W = DN_C, DN_H * DN_DK
    ncb = 8
    tl = ncb * C
    nchunks = L // C
    comm1, comm2 = comm if comm is not None else (None, None)
    hs = lambda h: slice(h * 128, (h + 1) * 128)

    def intra(q_ref, k_ref, v_ref, gc_ref, gr_ref, b_ref, t_ref, sol_ref, qk_ref):
        causal, strict, eye = _dn_masks()

        def pair(p, _):
            units = [(DN_NI * p + j, h) for j in range(DN_NI) for h in _HEADS]
            rows = [pl.ds(pl.multiple_of(c * C, C), C) for c, _ in units]
            q = [q_ref[r, hs(h)] for r, (_, h) in zip(rows, units)]
            k = [k_ref[r, hs(h)] for r, (_, h) in zip(rows, units)]
            v = [v_ref[r, hs(h)] for r, (_, h) in zip(rows, units)]
            gc = [gc_ref[r, h:h + 1] for r, (_, h) in zip(rows, units)]
            gr = [gr_ref[c][h:h + 1, :] for c, h in units]
            beta = [b_ref[r, h:h + 1] for r, (_, h) in zip(rows, units)]
            gam, eg, _, _ = _dn_decay(gc, gr, causal)
            kk = [_mm(x, x, _NT) for x in k]
            t, sol = _dn_solve(k, v, beta, gam, eg, kk, strict, eye)
            qk = [_mm(a, b, _NT) * g for a, b, g in zip(q, k, gam)]
            for i, (r, (_, h)) in enumerate(zip(rows, units)):
                t_ref[r, h * C:(h + 1) * C] = t[i]
                sol_ref[r, h * 256:(h + 1) * 256] = sol[i]
                qk_ref[r, h * C:(h + 1) * C] = qk[i]
            return 0

        lax.fori_loop(0, ncb // DN_NI, pair, 0)

    def scan(q_ref, k_ref, gc_ref, sol_ref, qk_ref, o_ref, sh_ref, s_ref):
        @pl.when(pl.program_id(0) == 0)
        def _():
            s_ref[...] = jnp.zeros_like(s_ref)

        def chunk(c, _):
            rows = pl.ds(pl.multiple_of(c * C, C), C)
            q = [q_ref[rows, hs(h)] for h in _HEADS]
            k = [k_ref[rows, hs(h)] for h in _HEADS]
            sol = [sol_ref[rows, h * 256:(h + 1) * 256] for h in _HEADS]
            qk = [qk_ref[rows, h * C:(h + 1) * C] for h in _HEADS]
            eg, el, gl = _dn_row_decay([gc_ref[rows, h:h + 1] for h in _HEADS])
            S = [s_ref[hs(h), :] for h in _HEADS]
            vn = [sol[h][:, :128] - _mm(sol[h][:, 128:], S[h]) for h in _HEADS]
            o = [_mm(q[h] * eg[h], S[h]) + _mm(qk[h], vn[h]) for h in _HEADS]
            Sn = [S[h] * gl[h] + _mm(k[h] * el[h], vn[h], _TN) for h in _HEADS]
            for h in _HEADS:
                sh_ref[c, hs(h), :] = S[h]
                s_ref[hs(h), :] = Sn[h]
                o_ref[rows, hs(h)] = o[h]
            return 0

        lax.fori_loop(0, ncb, chunk, 0)

    col = lambda b: pl.BlockSpec((tl, W), lambda i: (i, b))
    small = pl.BlockSpec((tl, 8), lambda i: (i, 0))
    rowblk = lambda w: pl.BlockSpec((tl, w), lambda i: (i, 0))
    sd = jax.ShapeDtypeStruct
    (thist, solhist, qk), got1 = _with_comm(_call(
        intra, (qkv, qkv, qkv, gcol, grow, bcol), name=name + "_intra", grid=(L // tl,),
        in_specs=[col(0), col(1), col(2), small, pl.BlockSpec((ncb, 8, C), lambda i: (i, 0, 0)), small],
        out_specs=(rowblk(DN_H * C), rowblk(DN_H * 256), rowblk(DN_H * C)),
        out_shape=(sd((L, DN_H * C), F32), sd((L, DN_H * 256), F32), sd((L, DN_H * C), F32)),
        sem=("parallel",), comm=comm1), comm1)
    (o, shist), got2 = _with_comm(_call(
        scan, (qkv, qkv, gcol, solhist, qk), name=name + "_scan", grid=(L // tl,),
        in_specs=[col(0), col(1), small, rowblk(DN_H * 256), rowblk(DN_H * C)],
        out_specs=(rowblk(W), pl.BlockSpec((ncb, W, 128), lambda i: (i, 0, 0))),
        out_shape=(sd((L, W), F32), sd((nchunks, W, 128), F32)),
        scratch_shapes=[pltpu.VMEM((W, 128), F32)], sem=("arbitrary",), comm=comm2), comm2)
    res = (o, shist, thist, solhist)
    return res if comm is None else (res, (got1 or []) + (got2 or []))


def dn_chunk_bwd(qkv, gcol, grow, bcol, shist, thist, solhist, do, name, comm=None):
    L = qkv.shape[0]
    C, W = DN_C, DN_H * DN_DK
    ncb = 8
    tl = ncb * C
    nchunks = L // C
    nt = L // tl

    def body(q_ref, k_ref, v_ref, gc_ref, gr_ref, b_ref, sh_ref, t_ref, sol_ref, do_ref,
             dqkv_ref, dgc_ref, dgr_ref, db_ref, ds_ref):
        @pl.when(pl.program_id(0) == 0)
        def _():
            ds_ref[...] = jnp.zeros_like(ds_ref)

        lane8 = lax.broadcasted_iota(jnp.int32, (C, 8), 1)
        sub8 = lax.broadcasted_iota(jnp.int32, (8, C), 0)
        rowid = lax.broadcasted_iota(jnp.int32, (C, 1), 0)
        causal, strict, _ = _dn_masks()
        rsum = lambda a: jnp.sum(a, axis=1, keepdims=True)

        def chunk(cc, _):
            c = ncb - 1 - cc
            rows = pl.ds(pl.multiple_of(c * C, C), C)
            grow_c = gr_ref[c]
            hs = lambda h: slice(h * 128, (h + 1) * 128)
            q = [q_ref[rows, hs(h)] for h in _HEADS]
            k = [k_ref[rows, hs(h)] for h in _HEADS]
            v = [v_ref[rows, hs(h)] for h in _HEADS]
            gc = [gc_ref[rows, h:h + 1] for h in _HEADS]
            gr = [grow_c[h:h + 1, :] for h in _HEADS]
            beta = [b_ref[rows, h:h + 1] for h in _HEADS]
            t = [t_ref[rows, h * C:(h + 1) * C] for h in _HEADS]
            sol = [sol_ref[rows, h * 256:(h + 1) * 256] for h in _HEADS]
            S = [sh_ref[c, hs(h), :] for h in _HEADS]
            dS = [ds_ref[hs(h), :] for h in _HEADS]
            dov = [do_ref[rows, hs(h)] for h in _HEADS]
            gam, eg, el, gl = _dn_decay(gc, gr, causal)
            kk = [_mm(k[h], k[h], _NT) for h in _HEADS]
            qk_raw = [_mm(q[h], k[h], _NT) for h in _HEADS]
            w = [sol[h][:, 128:] for h in _HEADS]
            kd = [k[h] * el[h] for h in _HEADS]
            vn = [sol[h][:, :128] - _mm(w[h], S[h]) for h in _HEADS]
            dvn = [_mm(qk_raw[h] * gam[h], dov[h], _TN) + _mm(kd[h], dS[h]) for h in _HEADS]
            dqd = [_mm(dov[h], S[h], _NT) for h in _HEADS]
            dqk = [jnp.where(causal, _mm(dov[h], vn[h], _NT), 0.0) for h in _HEADS]
            dkd = [_mm(vn[h], dS[h], _NT) for h in _HEADS]
            dgl = [jnp.sum(rsum(dS[h] * S[h]), axis=0, keepdims=True) for h in _HEADS]
            dw = [-_mm(dvn[h], S[h], _NT) for h in _HEADS]
            dSn = [dS[h] * gl[h] + _mm(q[h] * eg[h], dov[h], _TN) - _mm(w[h], dvn[h], _TN) for h in _HEADS]
            drhs = [_mm(t[h], jnp.concatenate([dvn[h], dw[h]], axis=1), _TN) for h in _HEADS]
            dn = [jnp.where(strict, -_mm(drhs[h], sol[h], _NT), 0.0) for h in _HEADS]
            dgc_all = jnp.zeros((C, 8), F32)
            db_all = jnp.zeros((C, 8), F32)
            dgr_all = jnp.zeros((8, C), F32)
            for h in _HEADS:
                drv, drk = drhs[h][:, :128], drhs[h][:, 128:]
                t2 = rsum(drk * k[h])
                x = dn[h] * gam[h]
                dbeta = rsum(drv * v[h]) + t2 * eg[h] + rsum(x * kk[h])
                dkk = x * beta[h]
                draw = dqk[h] * gam[h]
                mm_ = (dn[h] * beta[h] * kk[h] + dqk[h] * qk_raw[h]) * gam[h]
                deg = t2 * beta[h] + rsum(dqd[h] * q[h])
                r_ = rsum(dkd[h] * k[h]) * el[h]
                dglast = jnp.sum(r_, axis=0, keepdims=True) + dgl[h] * gl[h]
                dgc = rsum(mm_) + deg * eg[h] - r_ + jnp.where(rowid == C - 1, dglast, 0.0)
                dgr = -jnp.sum(mm_, axis=0, keepdims=True)
                dqkv_ref[rows, hs(h)] = _mm(draw, k[h]) + dqd[h] * eg[h]
                dqkv_ref[rows, hs(DN_H + h)] = (drk * (beta[h] * eg[h]) + _mm(dkk, k[h]) + _mm(dkk, k[h], _TN)
                                                + _mm(draw, q[h], _TN) + dkd[h] * el[h])
                dqkv_ref[rows, hs(2 * DN_H + h)] = drv * beta[h]
                ds_ref[hs(h), :] = dSn[h]
                dgc_all = dgc_all + jnp.where(lane8 == h, dgc, 0.0)
                db_all = db_all + jnp.where(lane8 == h, dbeta, 0.0)
                dgr_all = dgr_all + jnp.where(sub8 == h, dgr, 0.0)
            dgc_ref[rows, :] = dgc_all
            db_ref[rows, :] = db_all
            dgr_ref[c] = dgr_all
            return 0

        lax.fori_loop(0, ncb, chunk, 0)

    rev = lambda i: nt - 1 - i
    col = lambda b: pl.BlockSpec((tl, W), lambda i: (rev(i), b))
    rowblk = lambda w: pl.BlockSpec((tl, w), lambda i: (rev(i), 0))
    small = pl.BlockSpec((tl, 8), lambda i: (rev(i), 0))
    g3 = pl.BlockSpec((ncb, 8, C), lambda i: (rev(i), 0, 0))
    sd = jax.ShapeDtypeStruct
    return _call(body, (qkv, qkv, qkv, gcol, grow, bcol, shist, thist, solhist, do), name=name, grid=(nt,),
                 in_specs=[col(0), col(1), col(2), small, g3, small,
                           pl.BlockSpec((ncb, W, 128), lambda i: (rev(i), 0, 0)), rowblk(DN_H * C),
                           rowblk(DN_H * 256), col(0)],
                 out_specs=(rowblk(3 * W), small, g3, small),
                 out_shape=(sd((L, 3 * W), F32), sd((L, 8), F32), sd((nchunks, 8, C), F32), sd((L, 8), F32)),
                 scratch_shapes=[pltpu.VMEM((W, 128), F32)], sem=("arbitrary",), comm=comm)


def dn_out_fwd(o, rin, nw, name):
    L = o.shape[0]
    tl = _rtile(L, 256)

    def body(o_ref, z_ref, w_ref, y_ref):
        for hd in _HEADS:
            cs = slice(hd * 128, (hd + 1) * 128)
            ov = o_ref[:, cs]
            r = lax.rsqrt(jnp.mean(ov * ov, axis=-1, keepdims=True) + EPS)
            y_ref[:, cs] = (ov * r * w_ref[...] * _silu(z_ref[:, cs])).astype(BF16)

    return pl.pallas_call(
        body, name=name, grid=(L // tl,),
        in_specs=[pl.BlockSpec((tl, DN_W), lambda i: (i, 0)), pl.BlockSpec((tl, DN_W), lambda i: (i, 3)),
                  pl.BlockSpec((1, 128), lambda i: (0, 0))],
        out_specs=pl.BlockSpec((tl, DN_W), lambda i: (i, 0)), out_shape=jax.ShapeDtypeStruct((L, DN_W), BF16),
        compiler_params=_cparams("parallel"))(o, rin, nw)


def dn_out_bwd(dycat, o, rin, nw, name):
    L = o.shape[0]
    tl = _rtile(L, 256)

    def body(dy_ref, o_ref, z_ref, w_ref, do_ref, dz_ref, s_ref):
        @pl.when(pl.program_id(0) == 0)
        def _():
            s_ref[...] = jnp.zeros_like(s_ref)

        for hd in _HEADS:
            cs = slice(hd * 128, (hd + 1) * 128)
            ov, zv, d = o_ref[:, cs], z_ref[:, cs], dy_ref[:, cs]
            r = lax.rsqrt(jnp.mean(ov * ov, axis=-1, keepdims=True) + EPS)
            n = ov * r
            dnw = d * _silu(zv)
            dz_ref[:, cs] = (d * n * w_ref[...] * _dsilu(zv)).astype(BF16)
            dn = dnw * w_ref[...]
            do_ref[:, cs] = r * (dn - n * jnp.mean(dn * n, axis=-1, keepdims=True))
            s_ref[:, cs] += _fold8(dnw * n)

    own = pl.BlockSpec((tl, DN_W), lambda i: (i, 0))
    sd = jax.ShapeDtypeStruct
    return pl.pallas_call(
        body, name=name, grid=(L // tl,),
        in_specs=[own, own, pl.BlockSpec((tl, DN_W), lambda i: (i, 3)), pl.BlockSpec((1, 128), lambda i: (0, 0))],
        out_specs=(own, own, pl.BlockSpec((8, DN_W), lambda i: (0, 0))),
        out_shape=(sd((L, DN_W), F32), sd((L, DN_W), BF16), sd((8, DN_W), F32)),
        compiler_params=_cparams("arbitrary"))(dycat, o, rin, nw)


def dn_gates(a, beta_raw, a_log, dt_bias):
    L = a.shape[0]
    beta = jax.nn.sigmoid(beta_raw)
    g = -jnp.exp(a_log) * jax.nn.softplus(a + dt_bias)
    G = jnp.cumsum(g.reshape(L // DN_C, DN_C, DN_H), axis=1)
    pad = lambda t: jnp.pad(t, ((0, 0), (0, 8 - DN_H)))
    gcol = pad(G.reshape(L, DN_H))
    grow = jnp.pad(jnp.transpose(G, (0, 2, 1)), ((0, 0), (0, 8 - DN_H), (0, 0)))
    return gcol, grow, pad(beta)


def dn_block_fwd(rin, cw, a_log, dt_bias, out_norm, tag, comm=None):
    gates, gates_vjp = jax.vjp(dn_gates, rin[:, REC_A0:REC_A0 + DN_H], rin[:, REC_A0 + DN_H:REC_IN], a_log, dt_bias)
    c0, c12 = (comm[0], comm[1:]) if comm is not None else (None, None)
    qkv, got0 = _with_comm(dn_prep_fwd(rin, cw, tag + "_prep", comm=c0), c0)
    (o, shist, thist, solhist), got = _with_comm(dn_chunk_fwd(qkv, *gates, tag + "_chunk", comm=c12), c12)
    yd = dn_out_fwd(o, rin, out_norm.reshape(1, 128), tag + "_onorm")
    return yd, (qkv, gates, gates_vjp, o, shist, thist, solhist), (got0 or []) + (got or [])


def dn_block_bwd(dyd, res, rin, cw, out_norm, tag, comm=None):
    qkv, gates, gates_vjp, o, shist, thist, solhist = res
    do, dz, nsum = dn_out_bwd(dyd, o, rin, out_norm.reshape(1, 128), tag + "_donorm")
    (dqkv, dgc, dgr, db), got = _with_comm(dn_chunk_bwd(qkv, *gates, shist, thist, solhist, do, tag + "_dchunk",
                                                        comm=comm), comm)
    da, dbraw, g_alog, g_dtb = gates_vjp((dgc, dgr, db))
    dx, csum = dn_prep_bwd(rin, cw, dqkv, tag + "_dprep")
    grads = dict(conv=csum.reshape(4, 8, DN_NT * 128).sum(axis=1), a_log=g_alog, dt_bias=g_dtb,
                 out_norm=nsum.sum(axis=0).reshape(DN_H, 128).sum(axis=0))
    return dx, dz, da, dbraw, grads, got


_HBM = pl.BlockSpec(memory_space=pltpu.HBM)


def _mesh_pos():
    xi, yi, ci = lax.axis_index("x"), lax.axis_index("y"), lax.axis_index("c")
    return xi, yi, ci, 4 * xi + 2 * yi + ci


def _peer(xi, yi, ci, k):
    px = 1 - xi if (k >> 2) & 1 else xi
    py = 1 - yi if (k >> 1) & 1 else yi
    pc = 1 - ci if k & 1 else ci
    return (px, py, pc), 4 * px + 2 * py + pc


def _exchange(xs, gather, name):
    n = len(xs)

    def body(*refs):
        copies = _comm_copies(refs[:n], refs[n:2 * n], *refs[2 * n:], gather)
        for cp in copies:
            cp.start()
        for cp in copies:
            cp.wait()

    return pl.pallas_call(
        body, name=name, in_specs=[_HBM] * n, out_specs=tuple([_HBM] * n),
        out_shape=_comm_out_shapes(xs), scratch_shapes=_comm_sems(n))(*xs)


def _comm_out_shapes(xs):
    return tuple(jax.ShapeDtypeStruct((N_DEV,) + x.shape[-2:], x.dtype) for x in xs)


def _comm_sems(n):
    return [pltpu.SemaphoreType.DMA((n * (N_DEV - 1),)), pltpu.SemaphoreType.DMA((n * (N_DEV - 1),)),
            pltpu.SemaphoreType.DMA((n,))]


def _comm_copies(x_refs, o_refs, send_sems, recv_sems, lsems, gather):
    xi, yi, ci, me = _mesh_pos()
    copies = []
    for t in range(len(x_refs)):
        src_of = (lambda lin, t=t: x_refs[t]) if gather else (lambda lin, t=t: x_refs[t].at[lin])
        copies.append(pltpu.make_async_copy(src_of(me), o_refs[t].at[me], lsems.at[t]))
        for k in range(1, N_DEV):
            peer, lin = _peer(xi, yi, ci, k)
            s = t * (N_DEV - 1) + k - 1
            copies.append(pltpu.make_async_remote_copy(
                src_ref=src_of(lin), dst_ref=o_refs[t].at[me], send_sem=send_sems.at[s],
                recv_sem=recv_sems.at[s], device_id=peer, device_id_type=pl.DeviceIdType.MESH))
    return copies


def _call(body, args, *, name, grid, in_specs, out_specs, out_shape, scratch_shapes=(), sem, comm=None):
    if comm is None:
        return pl.pallas_call(body, name=name, grid=grid, in_specs=in_specs, out_specs=out_specs,
                              out_shape=out_shape, scratch_shapes=list(scratch_shapes),
                              compiler_params=_cparams(*sem))(*args)
    xs, gather = comm
    n = len(xs)
    single = not isinstance(out_shape, (tuple, list))
    outs_shape = (out_shape,) if single else tuple(out_shape)
    outs_specs = (out_specs,) if single else tuple(out_specs)
    n_in, n_out, n_scr = len(in_specs), len(outs_shape), len(scratch_shapes)

    def body2(*refs):
        ins, cx = refs[:n_in], refs[n_in:n_in + n]
        outs = refs[n_in + n:n_in + n + n_out]
        co = refs[n_in + n + n_out:n_in + 2 * n + n_out]
        scr = refs[n_in + 2 * n + n_out:n_in + 2 * n + n_out + n_scr]
        sems = refs[n_in + 2 * n + n_out + n_scr:]
        first = functools.reduce(jnp.logical_and, [pl.program_id(a) == 0 for a in range(len(grid))])
        last = functools.reduce(jnp.logical_and, [pl.program_id(a) == grid[a] - 1 for a in range(len(grid))])

        @pl.when(first)
        def _():
            for cp in _comm_copies(cx, co, *sems, gather):
                cp.start()

        body(*ins, *outs, *scr)

        @pl.when(last)
        def _():
            for cp in _comm_copies(cx, co, *sems, gather):
                cp.wait()

    res = pl.pallas_call(
        body2, name=name, grid=grid, in_specs=list(in_specs) + [_HBM] * n,
        out_specs=outs_specs + tuple([_HBM] * n), out_shape=outs_shape + _comm_out_shapes(xs),
        scratch_shapes=list(scratch_shapes) + _comm_sems(n),
        compiler_params=_cparams(*(["arbitrary"] * len(grid))))(*args, *xs)
    main = res[0] if single else tuple(res[:n_out])
    return main, list(res[n_out:])


def all_gather(x, name):
    return _exchange([x], True, name)[0]


def all_gather_many(xs, name):
    return _exchange(xs, True, name)


def all_to_all_many(xs, name):
    return _exchange(xs, False, name)


def reduce_adamw(gsrc, w, m, v, name, comm=None):
    parts = list(gsrc) if isinstance(gsrc, (list, tuple)) else [gsrc]
    S, R0, C = parts[0].shape
    R = R0 * len(parts)
    tr = _rtile(R0, max(16, min(256, (4 << 20) // (S * C * 4) // 16 * 16)), 16 if R0 % 16 == 0 else 8)
    n0 = R0 // tr
    c1 = 1.0 - ADAM_B1 ** ADAM_STEP
    c2 = 1.0 - ADAM_B2 ** ADAM_STEP

    def body(*refs):
        g_refs = refs[:len(parts)]
        w_ref, m_ref, v_ref, go_ref, d_ref, mo_ref, vo_ref = refs[len(parts):]
        for p, g_ref in enumerate(g_refs):
            @pl.when(pl.program_id(0) // n0 == p)
            def _(g_ref=g_ref):
                acc = g_ref[0].astype(F32)
                for s in range(1, S):
                    acc = acc + g_ref[s].astype(F32)
                go_ref[...] = acc
        g = go_ref[...]
        mn = ADAM_B1 * m_ref[...] + (1.0 - ADAM_B1) * g
        vn = ADAM_B2 * v_ref[...] + (1.0 - ADAM_B2) * (g * g)
        mo_ref[...] = mn
        vo_ref[...] = vn
        d_ref[...] = -ADAM_LR * ((mn / c1) / (jnp.sqrt(vn / c2) + ADAM_EPS) + ADAM_WD * w_ref[...])

    big = pl.BlockSpec((tr, C), lambda i: (i, 0))
    o = jax.ShapeDtypeStruct((R, C), F32)
    part_spec = lambda p: pl.BlockSpec((S, tr, C), lambda i: (0, jnp.clip(i - p * n0, 0, n0 - 1), 0))
    return _call(body, (*parts, w, m, v), name=name, grid=(R // tr,),
                 in_specs=[part_spec(p) for p in range(len(parts))] + [big, big, big],
                 out_specs=(big, big, big, big), out_shape=(o, o, o, o), sem=("parallel",), comm=comm)


def _to_slabs(g, ax):
    shp = g.shape
    g = g.reshape(shp[:ax] + (N_DEV, shp[ax] // N_DEV) + shp[ax + 1:])
    return jnp.moveaxis(g, ax, 0).reshape(N_DEV, -1)


def _from_slabs(s, ax, shp):
    s = s.reshape((N_DEV,) + shp[:ax] + (shp[ax] // N_DEV,) + shp[ax + 1:])
    return jnp.moveaxis(s, 0, ax).reshape(shp)


def _pack_rows(flat, width, row_mult):
    n = flat.shape[-1]
    per = width * row_mult
    tot = -(-n // per) * per
    flat = jnp.pad(flat, [(0, 0)] * (flat.ndim - 1) + [(0, tot - n)])
    return flat.reshape(flat.shape[:-1] + (tot // width, width))


def _offsets(sizes):
    offs, o = [], 0
    for s in sizes:
        offs.append(o)
        o += s
    return offs


WEIGHTS = ['ada_w', 'ada_b', 'norm_mix', 'norm_ffn', 'attn_w_in', 'attn_q_norm_a', 'attn_k_norm_a', 'attn_q_norm_b',
           'attn_k_norm_b', 'attn_sinks', 'attn_w_out', 'rec_w_in', 's5_lambda_re', 's5_lambda_im', 's5_log_dt',
           's5_b_re', 's5_b_im', 's5_c_re', 's5_c_im', 's5_d', 's5_glu_w', 's5_glu_b', 'dn_conv', 'dn_a_log',
           'dn_dt_bias', 'dn_out_norm', 'rec_w_out', 'ffn_w_up', 'ffn_conv', 'ffn_w_down']
BIG = [('attn_w_in', (D, ATTN_IN // N_DEV)), ('attn_w_out', (D // N_DEV, D)), ('rec_w_in', (D // N_DEV, REC_PAD)),
       ('s5_glu_w', (S5_W // N_DEV, S5_W)), ('rec_w_out', (D // N_DEV, D)), ('ffn_w_up', (2 * D, 2 * D_FF // N_DEV)),
       ('ffn_w_down', (2 * D_FF // N_DEV, D))]


def _shard2d(name, t):
    if name == 'rec_w_in':
        return jnp.pad(t[0], ((0, 0), (0, REC_PAD - REC_IN)))
    return t.reshape((-1, t.shape[-1]))


def _cols_to_slabs(g, k=N_DEV):
    r, n = g.shape
    return jnp.transpose(g.reshape(r, k, n // k), (1, 0, 2))


def _slabs_to_cols(s):
    k, r, c_ = s.shape
    return jnp.transpose(s, (1, 0, 2)).reshape(r, k * c_)
SMALL_SHARDED = [('s5_d', 1, (1, S5_W)), ('s5_glu_b', 1, (1, S5_W)), ('dn_conv', 2, (1, 4, 2304)),
                 ('ffn_conv', 2, (2, 3, 2 * D_FF))]
REPLICATED = [('ada_b', (2, 6 * D)), ('norm_mix', (2, D)), ('norm_ffn', (2, D)), ('attn_q_norm_a', (1, HD)),
              ('attn_k_norm_a', (1, HD)), ('attn_q_norm_b', (1, HD)), ('attn_k_norm_b', (1, HD)),
              ('attn_sinks', (1, 8)), ('s5_lambda_re', (1, 16, 64)), ('s5_lambda_im', (1, 16, 64)),
              ('s5_log_dt', (1, 16)), ('s5_b_re', (1, 16, 64, 16)), ('s5_b_im', (1, 16, 64, 16)),
              ('s5_c_re', (1, 16, 16, 64)), ('s5_c_im', (1, 16, 16, 64)), ('dn_a_log', (1, DN_H)),
              ('dn_dt_bias', (1, DN_H)), ('dn_out_norm', (1, 128))]


def _numel(shp):
    return int(np.prod(shp))


def kernel(x, c, ada_w, ada_b, norm_mix, norm_ffn, attn_w_in, attn_q_norm_a, attn_k_norm_a, attn_q_norm_b, attn_k_norm_b, attn_sinks, attn_w_out, rec_w_in, s5_lambda_re, s5_lambda_im, s5_log_dt, s5_b_re, s5_b_im, s5_c_re, s5_c_im, s5_d, s5_glu_w, s5_glu_b, dn_conv, dn_a_log, dn_dt_bias, dn_out_norm, rec_w_out, ffn_w_up, ffn_conv, ffn_w_down, loss_target, m_ada_w, m_ada_b, m_norm_mix, m_norm_ffn, m_attn_w_in, m_attn_q_norm_a, m_attn_k_norm_a, m_attn_q_norm_b, m_attn_k_norm_b, m_attn_sinks, m_attn_w_out, m_rec_w_in, m_s5_lambda_re, m_s5_lambda_im, m_s5_log_dt, m_s5_b_re, m_s5_b_im, m_s5_c_re, m_s5_c_im, m_s5_d, m_s5_glu_w, m_s5_glu_b, m_dn_conv, m_dn_a_log, m_dn_dt_bias, m_dn_out_norm, m_rec_w_out, m_ffn_w_up, m_ffn_conv, m_ffn_w_down, v_ada_w, v_ada_b, v_norm_mix, v_norm_ffn, v_attn_w_in, v_attn_q_norm_a, v_attn_k_norm_a, v_attn_q_norm_b, v_attn_k_norm_b, v_attn_sinks, v_attn_w_out, v_rec_w_in, v_s5_lambda_re, v_s5_lambda_im, v_s5_log_dt, v_s5_b_re, v_s5_b_im, v_s5_c_re, v_s5_c_im, v_s5_d, v_s5_glu_w, v_s5_glu_b, v_dn_conv, v_dn_a_log, v_dn_dt_bias, v_dn_out_norm, v_rec_w_out, v_ffn_w_up, v_ffn_conv, v_ffn_w_down):
    loc = locals()
    W = {n: loc[n] for n in WEIGHTS}
    M = {n: loc["m_" + n] for n in WEIGHTS}
    V = {n: loc["v_" + n] for n in WEIGHTS}
    _, _, _, me = _mesh_pos()
    L = x.shape[1]
    x0, tgt = x[0], loss_target[0]

    small_in = jnp.concatenate([c.reshape(-1)] + [W[n].reshape(-1) for n, _, _ in SMALL_SHARDED])
    si, att_in_all = all_gather_many([_pack_rows(small_in, 1024, 8), attn_w_in[0].astype(BF16)], "gather_first")
    si = si.reshape(N_DEV, -1)
    c_all = si[:, :D]
    off = D
    small_full = {}
    for n, ax, shp in SMALL_SHARDED:
        k = _numel(shp) // N_DEV
        small_full[n] = _from_slabs(si[:, off:off + k], ax, shp)
        off += k

    cond_all = jax.nn.silu(c_all)
    modp = jnp.concatenate([matmul([(cond_all, ada_w[l].astype(BF16))], "nn", f"ada{l}") for l in range(2)], axis=0)
    modp_all = all_gather(modp, "gather_mod")
    mods = []
    for l in range(2):
        row = lax.dynamic_index_in_dim(modp_all, l * N_DEV + me, axis=1, keepdims=False)
        mod = row.reshape(1, 6 * D) + ada_b[l].reshape(1, 6 * D)
        mods.append([mod[:, i * D:(i + 1) * D] for i in range(6)])

    w_att_in = _slabs_to_cols(att_in_all)
    bf = lambda t: t.astype(BF16)
    ffn_shards = [[bf(ffn_w_up[l]), bf(ffn_w_down[l])] for l in range(2)]
    rec_shards = [bf(_shard2d('rec_w_in', rec_w_in)), bf(s5_glu_w[0]), bf(rec_w_out[0])]
    ffn_cw = [small_full['ffn_conv'][l] for l in range(2)]
    dn_cw = small_full['dn_conv'][0]
    s5_dskip, glu_b = small_full['s5_d'], small_full['s5_glu_b']
    row = lambda t: t.reshape(1, -1)

    sh1, sc1, g1, sh2, sc2, g2 = mods[0]
    h1 = gate_norm_fwd(x0, None, None, row(norm_mix[0]), sh1, sc1, "l0_norm1")
    wvec, sinkvec = attn_vectors(attn_q_norm_a[0], attn_k_norm_a[0], attn_q_norm_b[0], attn_k_norm_b[0], attn_sinks[0])
    y0, res_att, got = attention_block_fwd(
        h1, w_att_in, wvec, sinkvec, None, "att",
        comms={'swa': ([ffn_shards[0][0][:D // 2]], True), 1: ([ffn_shards[0][0][D // 2:]], True),
               4: (ffn_shards[0][1:], True), 16: ([bf(attn_w_out[0])], True)})
    w_att_out = got['w_out']
    split_up = lambda up_all: (_slabs_to_cols(up_all[:4]), _slabs_to_cols(up_all[4:]))
    w_up = [split_up(jnp.concatenate([got['swa'][0], got[1][0]], axis=1))]
    w_down = [got[4][0].reshape(D_FF, D)]
    x1, h2 = gate_norm_fwd(x0, y0, g1, row(norm_ffn[0]), sh2, sc2, "l0_norm2")
    f0, res_f0, got_rec = ffn_block_fwd(h2, w_up[0][0], w_up[0][1], ffn_cw[0], w_down[0], "ffn0",
                                        comm=(rec_shards, True))
    w_rec_in = rec_cols_permute(got_rec[0].reshape(D, REC_PAD))
    glu_w, w_rec_out = got_rec[1].reshape(S5_W, S5_W), got_rec[2].reshape(D, D)
    w_rec_out = jnp.concatenate([w_rec_out[S5_W:], w_rec_out[:S5_W]], axis=0)
    t1, tc1, tg1, t2, tc2, tg2 = mods[1]
    x2, h3 = gate_norm_fwd(x1, f0, g2, row(norm_mix[1]), t1, tc1, "l1_norm1")
    rin = matmul([(h3, w_rec_in)], "nn", "rec_in")
    s5p, s5p_vjp = jax.vjp(s5_params, s5_lambda_re[0], s5_lambda_im[0], s5_log_dt[0], s5_b_re[0], s5_b_im[0],
                           s5_c_re[0], s5_c_im[0])
    u = rin[:, REC_U0:REC_A0]
    yc, res_s5 = s5_block_fwd(u, s5p, s5_dskip, glu_w, glu_b, "s5")
    yd, res_dn, got_ffn1 = dn_block_fwd(rin, dn_cw, dn_a_log[0], dn_dt_bias[0], dn_out_norm[0], "dn",
                                        comm=(([ffn_shards[1][1]], True), ([ffn_shards[1][0][:D // 2]], True),
                                              ([ffn_shards[1][0][D // 2:]], True)))
    w_up.append(split_up(jnp.concatenate([got_ffn1[1], got_ffn1[2]], axis=1)))
    w_down.append(got_ffn1[0].reshape(D_FF, D))
    ycat = jnp.concatenate([yd, yc], axis=1)
    y1 = matmul([(ycat, w_rec_out)], "nn", "rec_out")
    x3, h4 = gate_norm_fwd(x2, y1, tg1, row(norm_ffn[1]), t2, tc2, "l1_norm2")
    f1, res_f1, _ = ffn_block_fwd(h4, w_up[1][0], w_up[1][1], ffn_cw[1], w_down[1], "ffn1")
    dx4, df1, lsum = final_loss(x3, f1, tg2, tgt, "loss")

    G = {}
    d_tg2 = lsum[8:16].sum(axis=0)
    dh4, gf1, _ = ffn_block_bwd(df1, res_f1, w_up[1][0], w_up[1][1], ffn_cw[1], w_down[1], "ffn1")
    ffn_slabs = lambda g: [g['w_up'], g['w_down'].reshape(N_DEV, D_FF // N_DEV, D)]
    dx3, dy1, s = gate_norm_bwd(x3, y1, tg1, row(norm_ffn[1]), tc2, dx4, dh4, "l1_dnorm2")
    s = s.reshape(4, 8, D).sum(axis=1)
    d_tg1, d_nffn1, d_t2, d_tc2 = s[0], s[1] * (1.0 + tc2[0]), s[2], s[1] * norm_ffn[1]
    g_rec_out = matmul([(ycat, dy1)], "tn", "rec_out_dw", out_dtype=BF16)
    g_rec_out = jnp.concatenate([g_rec_out[DN_W:], g_rec_out[:DN_W]], axis=0).reshape(N_DEV, D // N_DEV, D)
    dycat = matmul([(dy1, w_rec_out)], "nt", "rec_out_dx")
    du, s5cot, gs5 = s5_block_bwd(dycat, res_s5, s5p, s5_dskip, glu_w, glu_b, "s5", dout_col=DN_W // S5_W)
    s5g = s5p_vjp(s5cot)
    dqkv, dz, da, dbraw, gdn, recv_ffn1 = dn_block_bwd(dycat, res_dn, rin, dn_cw, dn_out_norm[0], "dn",
                                                       comm=(ffn_slabs(gf1), False))
    d_rest = jnp.concatenate([du.astype(BF16), da.astype(BF16), dbraw.astype(BF16),
                              jnp.zeros((L, REC_PAD - REC_IN), BF16)], axis=1)
    drin = ((dqkv, 0), (dz, 3 * DN_W), (d_rest, REC_U0))
    g_rec_in = jnp.concatenate([matmul([(h3, p)], "tn", f"rec_in_dw{i}", out_dtype=BF16)
                                for i, (p, _) in enumerate(drin)], axis=1)
    g_rec_in = rec_cols_restore(g_rec_in).reshape(N_DEV, D // N_DEV, REC_PAD)
    g_glu = gs5['glu_w'].astype(BF16).reshape(N_DEV, S5_W // N_DEV, S5_W)
    dh3 = matmul([(p, w_rec_in[:, c0:c0 + p.shape[1]]) for p, c0 in drin], "nt", "rec_in_dx")
    dx2, df0, s = gate_norm_bwd(x2, f0, g2, row(norm_mix[1]), tc1, dx3, dh3, "l1_dnorm1")
    s = s.reshape(4, 8, D).sum(axis=1)
    d_g2, d_nmix1, d_t1, d_tc1 = s[0], s[1] * (1.0 + tc1[0]), s[2], s[1] * norm_mix[1]
    dh2, gf0, recv_rec = ffn_block_bwd(df0, res_f0, w_up[0][0], w_up[0][1], ffn_cw[0], w_down[0], "ffn0",
                                       comm=([g_rec_in, g_glu, g_rec_out], False))
    dx1, dy0, s = gate_norm_bwd(x1, y0, g1, row(norm_ffn[0]), sc2, dx2, dh2, "l0_dnorm2")
    s = s.reshape(4, 8, D).sum(axis=1)
    d_g1, d_nffn0, d_sh2, d_sc2 = s[0], s[1] * (1.0 + sc2[0]), s[2], s[1] * norm_ffn[0]
    dh1, gatt, got_b = attention_block_bwd(dy0, res_att, w_att_in, wvec, sinkvec, w_att_out, "att",
                                           comms={'swa': ([gf0['w_up'][:, :D // 2]], False),
                                                  16: ([gf0['w_up'][:, D // 2:]], False),
                                                  1: (ffn_slabs(gf0)[1:], False)},
                                           send_w_out_on=4)
    recv_ffn0 = [jnp.concatenate([got_b['swa'][0], got_b[16][0]], axis=1), got_b[1][0]]
    (grad_x, s), recv_w_in = gate_norm_bwd(x0, None, None, row(norm_mix[0]), sc1, dx1, dh1, "l0_dnorm1",
                                           comm=([_cols_to_slabs(gatt['w_in'])], False))
    recv_att = [recv_w_in[0], got_b[4][0]]
    s = s.reshape(4, 8, D).sum(axis=1)
    d_nmix0, d_sh1, d_sc1 = s[1] * (1.0 + sc1[0]), s[2], s[1] * norm_mix[0]
    dmod = jnp.stack([jnp.concatenate([d_sh1, d_sc1, d_g1, d_sh2, d_sc2, d_g2]),
                      jnp.concatenate([d_t1, d_tc1, d_tg1, d_t2, d_tc2, d_tg2])])

    P = {'ada_b': dmod, 'norm_mix': jnp.stack([d_nmix0, d_nmix1]), 'norm_ffn': jnp.stack([d_nffn0, d_nffn1]),
         'attn_q_norm_a': gatt['q_norm_a'], 'attn_k_norm_a': gatt['k_norm_a'], 'attn_q_norm_b': gatt['q_norm_b'],
         'attn_k_norm_b': gatt['k_norm_b'], 'attn_sinks': gatt['sinks'],
         's5_lambda_re': s5g[0], 's5_lambda_im': s5g[1], 's5_log_dt': s5g[2], 's5_b_re': s5g[3], 's5_b_im': s5g[4],
         's5_c_re': s5g[5], 's5_c_im': s5g[6], 'dn_a_log': gdn['a_log'], 'dn_dt_bias': gdn['dt_bias'],
         'dn_out_norm': gdn['out_norm'],
         's5_d': gs5['dskip'], 's5_glu_b': gs5['glu_b'], 'dn_conv': gdn['conv'],
         'ffn_conv': jnp.stack([gf0['conv'], gf1['conv']])}

    out = {k: {} for k in ("g", "d", "m", "v")}
    keys = ("g", "d", "m", "v")
    recv = {'attn_w_in': recv_att[0], 'attn_w_out': recv_att[1], 'rec_w_in': recv_rec[0], 's5_glu_w': recv_rec[1],
            'rec_w_out': recv_rec[2]}
    for n, gr_ in recv.items():
        res4 = reduce_adamw(gr_, _shard2d(n, W[n]), _shard2d(n, M[n]), _shard2d(n, V[n]), "adamw_" + n)
        for key, t in zip(keys, res4):
            out[key][n] = (t[:, :REC_IN] if n == 'rec_w_in' else t).reshape(W[n].shape)
    rep_sizes = [_numel(shp) for _, shp in REPLICATED]
    ss_sizes = [_numel(shp) for _, _, shp in SMALL_SHARDED]
    rep_offs = _offsets(rep_sizes + ss_sizes + [1])
    parts = [P[n].reshape(-1) for n, _ in REPLICATED] + [P[n].reshape(-1) for n, _, _ in SMALL_SHARDED]
    parts.append(lsum[0:8].sum().reshape(1))
    spack = _pack_rows(jnp.concatenate(parts), 1024, 8)
    flat2d = lambda t: t.reshape(-1, t.shape[-1])
    sall = None
    for n, idx in (('ffn_w_up', 0), ('ffn_w_down', 1)):
        comm = ([spack], True) if sall is None else None
        res4, got_s = _with_comm(reduce_adamw([recv_ffn0[idx], recv_ffn1[idx]], flat2d(W[n]), flat2d(M[n]),
                                              flat2d(V[n]), "adamw_" + n, comm=comm), comm)
        if got_s is not None:
            sall = got_s[0]
        for key, t in zip(keys, res4):
            out[key][n] = t.reshape(W[n].shape)
    n_rest = sum(ss_sizes) + 1
    pk = lambda d: _pack_rows(jnp.concatenate([d[n].reshape(-1) for n, _ in REPLICATED]
                                              + [jnp.zeros((n_rest,), F32)]), 1024, 8)
    sg, sd_, sm, sv = [t.reshape(-1) for t in reduce_adamw(sall, pk(W), pk(M), pk(V), "adamw_small")]
    loss = 0.5 * sg[rep_offs[-1]] / D

    dmod_all = sall.reshape(N_DEV, -1)[:, :2 * 6 * D].reshape(N_DEV, 2, 6 * D)
    dmod_mine = lax.dynamic_slice_in_dim(dmod_all, me * (6 * D // N_DEV), 6 * D // N_DEV, axis=2)
    g_ada = [matmul([(cond_all, dmod_mine[:, l])], "tn", f"ada{l}_dw")[None] for l in range(2)]
    ada2d = lambda t: t.reshape(2 * D, 6 * D // N_DEV)
    for key, t in zip(("g", "d", "m", "v"), reduce_adamw(g_ada, ada2d(ada_w), ada2d(m_ada_w),
                                                          ada2d(v_ada_w), "adamw_ada_w")):
        out[key]['ada_w'] = t.reshape(ada_w.shape)
    own = []
    for (n, ax, shp), o in zip(SMALL_SHARDED, rep_offs[len(REPLICATED):]):
        slabs = _to_slabs(sg[o:o + _numel(shp)].reshape(shp), ax)
        own.append(lax.dynamic_index_in_dim(slabs, me, axis=0, keepdims=False))
    own_names = [n for n, _, _ in SMALL_SHARDED]
    pk = lambda d: _pack_rows(jnp.concatenate([d[n].reshape(-1) for n in own_names]), 1024, 8)
    og, od, om, ov = [t.reshape(-1) for t in reduce_adamw(_pack_rows(jnp.concatenate(own), 1024, 8)[None],
                                                          pk(W), pk(M), pk(V), "adamw_own")]

    def unpack(names_shapes, bufs):
        o = 0
        for n, shp in names_shapes:
            k = _numel(shp)
            for key, buf in zip(("g", "d", "m", "v"), bufs):
                out[key][n] = buf[o:o + k].reshape(shp)
            o += k

    unpack(REPLICATED, (sg, sd_, sm, sv))
    unpack([(n, W[n].shape) for n in own_names], (og, od, om, ov))
    return (loss, grad_x[None], *[out["g"][n] for n in WEIGHTS], *[out["d"][n] for n in WEIGHTS],
            *[out["m"][n] for n in WEIGHTS], *[out["v"][n] for n in WEIGHTS])
```

```python
import functools
import math

import numpy as np
import jax
import jax.numpy as jnp
from jax import lax
from jax.experimental import pallas as pl
from jax.experimental.pallas import tpu as pltpu

F32 = jnp.float32
BF16 = jnp.bfloat16

N_DEV = 8
D = 1024
HD = 64
BLK = 128
ATTN_IN = 2304
CB = ATTN_IN // 128
B_BRANCHES = ((128, 1), (512, 4), (2048, 16))
S5_W = 256
S5_P = 1024
DN_H = 6
DN_DK = 128
DN_C = 64
REC_IN = 3340
REC_PAD = 3456
D_FF = 2816
EPS = 1e-6
ADAM_LR, ADAM_B1, ADAM_B2, ADAM_EPS, ADAM_WD, ADAM_STEP = 0.001, 0.9, 0.999, 1e-8, 0.01, 10
VMEM_LIMIT = 48 * 1024 * 1024

ALIBI = np.asarray(2.0 ** (-8.0 * np.arange(1, 17) / 16), dtype=np.float32)


def _cparams(*sem):
    return pltpu.CompilerParams(dimension_semantics=tuple(sem), vmem_limit_bytes=VMEM_LIMIT)


def _tile(n, target):
    if n <= target:
        return n
    best = None
    for t in range(128, target + 1, 128):
        if n % t == 0:
            best = t
    assert best is not None, (n, target)
    return best


def _rtile(n, target, mult=8):
    if n <= target:
        return n
    best = None
    for t in range(mult, target + 1, mult):
        if n % t == 0:
            best = t
    assert best is not None, (n, target)
    return best


def _fold8(x):
    r, c = x.shape
    return x.reshape(r // 8, 8, c).sum(axis=0)


def _sigmoid(x):
    return 1.0 / (1.0 + jnp.exp(-x))


_DIMS = {"nn": (((1,), (0,)), ((), ())), "nt": (((1,), (1,)), ((), ())), "tn": (((0,), (0,)), ((), ()))}


MM_FULL_K = 3584


MM_VMEM_BUDGET = 40 << 20


def matmul(pairs, mode, name, out_dtype=F32, tm=1024, tn=1536, tk=1024):
    a0, b0 = pairs[0]
    if mode == "nn":
        (M, K), N = a0.shape, b0.shape[1]
    elif mode == "nt":
        (M, K), N = a0.shape, b0.shape[0]
    else:
        (K, M), N = a0.shape, b0.shape[1]
        tm = 1536
    tn = _tile(N, tn)
    tk = K if K <= MM_FULL_K else _tile(K, tk)
    nk = K // tk
    npair = len(pairs)
    dims = _DIMS[mode]
    kdim = 0 if mode == "tn" else 1
    tks = [a.shape[kdim] for a, _ in pairs]
    assert all(t == K for t in tks) or (nk == 1 and max(tks) <= MM_FULL_K), tks
    if nk > 1:
        tks = [tk] * npair

    def planned(tm_):
        ab = sum(tm_ * t * a.dtype.itemsize + t * tn * b.dtype.itemsize for (a, b), t in zip(pairs, tks))
        return 2 * ab + 2 * tm_ * tn * jnp.dtype(out_dtype).itemsize + (tm_ * tn * 4 if nk > 1 else 0)

    while True:
        tm_try = _rtile(M, tm) if M % 128 else _tile(M, tm)
        if planned(tm_try) <= MM_VMEM_BUDGET or tm <= 128:
            break
        tm //= 2
    tm = tm_try

    def body(*refs):
        o_ref = refs[2 * npair]
        tot = None
        for p in range(npair):
            part = lax.dot_general(refs[2 * p][...].astype(BF16), refs[2 * p + 1][...].astype(BF16),
                                   dims, preferred_element_type=F32)
            tot = part if tot is None else tot + part
        if nk == 1:
            o_ref[...] = tot.astype(o_ref.dtype)
            return
        acc_ref = refs[2 * npair + 1]
        k = pl.program_id(2)

        @pl.when(k == 0)
        def _():
            acc_ref[...] = tot

        @pl.when(k > 0)
        def _():
            acc_ref[...] += tot

        @pl.when(k == nk - 1)
        def _():
            o_ref[...] = acc_ref[...].astype(o_ref.dtype)

    def specs(t):
        if mode == "nn":
            return [pl.BlockSpec((tm, t), lambda j, i, k: (i, k)), pl.BlockSpec((t, tn), lambda j, i, k: (k, j))]
        if mode == "nt":
            return [pl.BlockSpec((tm, t), lambda j, i, k: (i, k)), pl.BlockSpec((tn, t), lambda j, i, k: (j, k))]
        return [pl.BlockSpec((t, tm), lambda j, i, k: (k, i)), pl.BlockSpec((t, tn), lambda j, i, k: (k, j))]

    flat = [t for pr in pairs for t in pr]
    return pl.pallas_call(
        body, name=name, grid=(N // tn, M // tm, nk),
        in_specs=[s for t in tks for s in specs(t)],
        out_specs=pl.BlockSpec((tm, tn), lambda j, i, k: (i, j)),
        out_shape=jax.ShapeDtypeStruct((M, N), out_dtype),
        scratch_shapes=[pltpu.VMEM((tm, tn), F32)] if nk > 1 else [],
        compiler_params=_cparams("parallel", "parallel", "arbitrary"),
    )(*flat)


def gate_norm_fwd(x, y, gate, nw, sh, sc, name):
    L, C = x.shape
    tl = _rtile(L, 512)
    has_gate = y is not None

    def body(*refs):
        if has_gate:
            x_ref, y_ref, g_ref, nw_ref, sh_ref, sc_ref, xn_ref, h_ref = refs
            xn = x_ref[...] + g_ref[...] * y_ref[...]
            xn_ref[...] = xn
        else:
            x_ref, nw_ref, sh_ref, sc_ref, h_ref = refs
            xn = x_ref[...]
        r = lax.rsqrt(jnp.mean(xn * xn, axis=-1, keepdims=True) + EPS)
        h = (xn * r * nw_ref[...]) * (1.0 + sc_ref[...]) + sh_ref[...]
        h_ref[...] = h.astype(BF16)

    big = pl.BlockSpec((tl, C), lambda i: (i, 0))
    vec = pl.BlockSpec((1, C), lambda i: (0, 0))
    if has_gate:
        ins, in_specs = (x, y, gate, nw, sh, sc), [big, big, vec, vec, vec, vec]
        out_shape = (jax.ShapeDtypeStruct((L, C), F32), jax.ShapeDtypeStruct((L, C), BF16))
        out_specs = (big, big)
    else:
        ins, in_specs = (x, nw, sh, sc), [big, vec, vec, vec]
        out_shape = jax.ShapeDtypeStruct((L, C), BF16)
        out_specs = big
    return pl.pallas_call(body, name=name, grid=(L // tl,), in_specs=in_specs, out_specs=out_specs,
                          out_shape=out_shape, compiler_params=_cparams("parallel"))(*ins)


def gate_norm_bwd(xn, y, gate, nw, sc, dxn_direct, dh, name, comm=None):
    L, C = xn.shape
    tl = _rtile(L, 256)
    has_gate = y is not None
    has_direct = dxn_direct is not None

    def body(*refs):
        refs = list(refs)
        xn_ref = refs.pop(0)
        y_ref = refs.pop(0) if has_gate else None
        g_ref = refs.pop(0) if has_gate else None
        nw_ref = refs.pop(0)
        sc_ref = refs.pop(0)
        dd_ref = refs.pop(0) if has_direct else None
        dh_ref = refs.pop(0)
        dxn_ref = refs.pop(0)
        dy_ref = refs.pop(0) if has_gate else None
        sums_ref = refs.pop(0)

        @pl.when(pl.program_id(0) == 0)
        def _():
            sums_ref[...] = jnp.zeros_like(sums_ref)

        xv = xn_ref[...]
        dh_v = dh_ref[...]
        r = lax.rsqrt(jnp.mean(xv * xv, axis=-1, keepdims=True) + EPS)
        n = xv * r
        a = nw_ref[...] * (1.0 + sc_ref[...])
        dn = dh_v * a
        dx = r * (dn - n * jnp.mean(dn * n, axis=-1, keepdims=True))
        if has_direct:
            dx = dx + dd_ref[...]
        dxn_ref[...] = dx
        sums_ref[8:16, :] += _fold8(dh_v * n)
        sums_ref[16:24, :] += _fold8(dh_v)
        if has_gate:
            dy_ref[...] = (dx * g_ref[...]).astype(BF16)
            sums_ref[0:8, :] += _fold8(dx * y_ref[...])

    big = pl.BlockSpec((tl, C), lambda i: (i, 0))
    vec = pl.BlockSpec((1, C), lambda i: (0, 0))
    ins, in_specs = [xn], [big]
    if has_gate:
        ins += [y, gate]
        in_specs += [big, vec]
    ins += [nw, sc]
    in_specs += [vec, vec]
    if has_direct:
        ins.append(dxn_direct)
        in_specs.append(big)
    ins.append(dh)
    in_specs.append(big)
    out_shape = [jax.ShapeDtypeStruct((L, C), F32)]
    out_specs = [big]
    if has_gate:
        out_shape.append(jax.ShapeDtypeStruct((L, C), BF16))
        out_specs.append(big)
    out_shape.append(jax.ShapeDtypeStruct((32, C), F32))
    out_specs.append(pl.BlockSpec((32, C), lambda i: (0, 0)))
    return _call(body, ins, name=name, grid=(L // tl,), in_specs=in_specs, out_specs=tuple(out_specs),
                 out_shape=tuple(out_shape), sem=("arbitrary",), comm=comm)


def final_loss(x, f, gate, target, name):
    L, C = x.shape
    tl = _rtile(L, 256)

    def body(x_ref, f_ref, g_ref, t_ref, dy_ref, df_ref, sums_ref):
        @pl.when(pl.program_id(0) == 0)
        def _():
            sums_ref[...] = jnp.zeros_like(sums_ref)

        fv = f_ref[...]
        err = x_ref[...] + g_ref[...] * fv - t_ref[...]
        dy = err * (1.0 / C)
        dy_ref[...] = dy
        df_ref[...] = (dy * g_ref[...]).astype(BF16)
        sums_ref[0:8, :] += _fold8(err * err)
        sums_ref[8:16, :] += _fold8(dy * fv)

    big = pl.BlockSpec((tl, C), lambda i: (i, 0))
    vec = pl.BlockSpec((1, C), lambda i: (0, 0))
    return pl.pallas_call(
        body, name=name, grid=(L // tl,), in_specs=[big, big, vec, big],
        out_specs=(big, big, pl.BlockSpec((16, C), lambda i: (0, 0))),
        out_shape=(jax.ShapeDtypeStruct((L, C), F32), jax.ShapeDtypeStruct((L, C), BF16),
                   jax.ShapeDtypeStruct((16, C), F32)),
        compiler_params=_cparams("arbitrary"))(x, f, gate, target)


def _seg_ones(seg):
    r = lax.broadcasted_iota(jnp.int32, (128, 128), 0) // seg
    c = lax.broadcasted_iota(jnp.int32, (128, 128), 1) // seg
    return (r == c).astype(BF16)


def _segsum(t, ones):
    hi = t.astype(BF16)
    lo = (t - hi.astype(F32)).astype(BF16)
    return (jnp.dot(hi, ones, preferred_element_type=F32) + jnp.dot(lo, ones, preferred_element_type=F32))


_NORMED_TILES = tuple(list(range(0, 5)) + list(range(6, 14)))


DIL = (4, 16)
B_COLS0, B_W = 768, 1536
DIL_TL = 256


def _to_dilated(scr_ref, out_ref, d, cast=None):
    nj, tl, _ = scr_ref.shape
    for r in range(d):
        for j in range(nj):
            piece = scr_ref[j, pl.ds(r, tl // d, stride=d), :]
            c0 = (r * nj + j) * 128
            out_ref[:, c0:c0 + 128] = piece if cast is None else piece.astype(cast)


def _from_dilated(in_ref, scr_ref, d):
    nj, tl, _ = scr_ref.shape
    for r in range(d):
        for j in range(nj):
            c0 = (r * nj + j) * 128
            scr_ref[j, pl.ds(r, tl // d, stride=d), :] = in_ref[:, c0:c0 + 128]


def _dil_spec(tl, d, width):
    return pl.BlockSpec((tl // d, d * width), lambda i: (i, 0))


def qknorm_fwd(qkv, wvec, name):
    L, C = qkv.shape
    tl = DIL_TL

    def body(x_ref, w_ref, o_ref, o4_ref, o16_ref, scr_ref):
        ones = _seg_ones(HD)
        for t in range(CB):
            cs = slice(t * 128, (t + 1) * 128)
            x = x_ref[:, cs]
            if t in _NORMED_TILES:
                ms = _segsum(x * x, ones) * (1.0 / HD)
                x = x * lax.rsqrt(ms + EPS) * w_ref[:, cs]
            o_ref[:, cs] = x.astype(BF16)
            if t * 128 >= B_COLS0:
                scr_ref[t - B_COLS0 // 128] = x
        _to_dilated(scr_ref, o4_ref, 4, BF16)
        _to_dilated(scr_ref, o16_ref, 16, BF16)

    return pl.pallas_call(
        body, name=name, grid=(L // tl,),
        in_specs=[pl.BlockSpec((tl, C), lambda i: (i, 0)), pl.BlockSpec((1, C), lambda i: (0, 0))],
        out_specs=(pl.BlockSpec((tl, C), lambda i: (i, 0)), _dil_spec(tl, 4, B_W), _dil_spec(tl, 16, B_W)),
        out_shape=(jax.ShapeDtypeStruct((L, C), BF16), jax.ShapeDtypeStruct((L // 4, 4 * B_W), BF16),
                   jax.ShapeDtypeStruct((L // 16, 16 * B_W), BF16)),
        scratch_shapes=[pltpu.VMEM((B_W // 128, tl, 128), F32)], compiler_params=_cparams("parallel"))(qkv, wvec)


def qknorm_bwd(qkv, wvec, d_a, d_b, name):
    L, C = qkv.shape
    tl = DIL_TL

    def body(x_ref, w_ref, dqa, dka, dva, q1, k1, v1, q4, k4, v4, q16, k16, v16, dx_ref, sums_ref,
             dy_ref, s4_ref, s16_ref):
        @pl.when(pl.program_id(0) == 0)
        def _():
            sums_ref[...] = jnp.zeros_like(sums_ref)

        dy_ref[:, 0:512] = dqa[...]
        for off, ref in ((512, dka), (640, dva)):
            for g in range(2):
                acc = ref[:, g * 256:g * 256 + HD]
                for h in range(1, 4):
                    acc = acc + ref[:, g * 256 + h * HD:g * 256 + (h + 1) * HD]
                dy_ref[:, off + g * HD:off + (g + 1) * HD] = acc
        for off, r1, r4, r16 in ((768, q1, q4, q16), (1280, k1, k4, k16), (1792, v1, v4, v16)):
            _from_dilated(r4, s4_ref, 4)
            _from_dilated(r16, s16_ref, 16)
            for j in range(4):
                dy_ref[:, off + j * 128:off + (j + 1) * 128] = r1[:, j * 128:(j + 1) * 128] + s4_ref[j] + s16_ref[j]

        ones = _seg_ones(HD)
        for t in range(CB):
            cs = slice(t * 128, (t + 1) * 128)
            d = dy_ref[:, cs]
            if t in _NORMED_TILES:
                x = x_ref[:, cs]
                r = lax.rsqrt(_segsum(x * x, ones) * (1.0 / HD) + EPS)
                n = x * r
                dn = d * w_ref[:, cs]
                dx_ref[:, cs] = (r * (dn - n * (_segsum(dn * n, ones) * (1.0 / HD)))).astype(BF16)
                sums_ref[:, cs] += _fold8(d * n)
            else:
                dx_ref[:, cs] = d.astype(BF16)

    big = pl.BlockSpec((tl, C), lambda i: (i, 0))
    p512 = pl.BlockSpec((tl, 512), lambda i: (i, 0))
    return pl.pallas_call(
        body, name=name, grid=(L // tl,),
        in_specs=[big, pl.BlockSpec((1, C), lambda i: (0, 0))] + [p512] * 6 + [_dil_spec(tl, 4, 512)] * 3
        + [_dil_spec(tl, 16, 512)] * 3,
        out_specs=(big, pl.BlockSpec((8, C), lambda i: (0, 0))),
        out_shape=(jax.ShapeDtypeStruct((L, C), BF16), jax.ShapeDtypeStruct((8, C), F32)),
        scratch_shapes=[pltpu.VMEM((tl, C), F32), pltpu.VMEM((4, tl, 128), F32), pltpu.VMEM((4, tl, 128), F32)],
        compiler_params=_cparams("arbitrary"))(qkv, wvec, *d_a, *d_b[0], *d_b[1], *d_b[2])


def _attn_biases(t, slopes, step, maxdist):
    qi = lax.broadcasted_iota(jnp.int32, (BLK, 2 * BLK), 0)
    sj = lax.broadcasted_iota(jnp.int32, (BLK, 2 * BLK), 1)
    dist = BLK + qi - sj
    valid = (dist >= 0) & (dist <= maxdist)
    distf = (step * dist).astype(F32)
    inner = [jnp.where(valid, (-sl) * distf, -jnp.inf) for sl in slopes]
    first = [jnp.where((t > 0) | (sj >= BLK), b, -jnp.inf) for b in inner]
    return inner, first


def _attn_scores(q, kw, bias):
    return lax.dot_general(q, kw, (((1,), (1,)), ((), ())), preferred_element_type=F32) + bias


ATT_NQ_ONE = 16
ATT_NQ = 8


def _attn_operands(nq, hp, gqa, q_ref, kh_ref, kc_ref, vh_ref, vc_ref):
    ops = []
    for b in range(nq):
        rows = slice(b * BLK, (b + 1) * BLK)
        prev = slice((b - 1) * BLK, b * BLK)
        for e in range(2):
            cs = slice(e * HD, (e + 1) * HD)
            if gqa:
                ksel = lambda ref, r: jnp.where(hp >= 2, ref[r, 64:128], ref[r, 0:64])
            else:
                ksel = lambda ref, r, cs=cs: ref[r, cs]
            kprev = ksel(kh_ref, slice(0, BLK)) if b == 0 else ksel(kc_ref, prev)
            vprev = ksel(vh_ref, slice(0, BLK)) if b == 0 else ksel(vc_ref, prev)
            ops.append((b, e, rows, cs, q_ref[rows, cs] * (HD ** -0.5),
                        jnp.concatenate([kprev, ksel(kc_ref, rows)], axis=0),
                        jnp.concatenate([vprev, ksel(vc_ref, rows)], axis=0)))
    return ops


def _attn_specs(cb, q_off, k_off, v_off, gqa):
    kcol = (lambda r, hp: r * cb + k_off) if gqa else (lambda r, hp: r * cb + k_off + hp)
    vcol = (lambda r, hp: r * cb + v_off) if gqa else (lambda r, hp: r * cb + v_off + hp)
    return kcol, vcol


def attn_fwd(X, d, q_off, k_off, v_off, gqa, slope0, maxdist, name, comm=None):
    Ls = X.shape[0]
    nq = min(ATT_NQ, Ls // BLK)
    TQ = nq * BLK
    nt = Ls // TQ
    slopes = jnp.asarray(ALIBI)

    def body(sl_ref, q_ref, kh_ref, kc_ref, vh_ref, vc_ref, o_ref, lse_ref):
        hp, t = pl.program_id(1), pl.program_id(2)
        ops = _attn_operands(nq, hp, gqa, q_ref, kh_ref, kc_ref, vh_ref, vc_ref)
        inner, first = _attn_biases(t, [sl_ref[slope0 + 2 * hp + e] for e in range(2)], d, maxdist)
        s = [_attn_scores(q, kw, first[e] if b == 0 else inner[e]) for (b, e, rows, cs, q, kw, vw) in ops]
        m = [jnp.max(x, axis=-1, keepdims=True) for x in s]
        p = [jnp.exp(x - mm) for x, mm in zip(s, m)]
        l = [jnp.sum(x, axis=-1, keepdims=True) for x in p]
        o = [jnp.dot(x.astype(BF16), op[6], preferred_element_type=F32) / ll for x, op, ll in zip(p, ops, l)]
        for (b, e, rows, cs, q, kw, vw), oo, mm, ll in zip(ops, o, m, l):
            o_ref[rows, cs] = oo
            lse_ref[rows, cs] = jnp.broadcast_to(mm + jnp.log(ll), (BLK, HD))

    cb = X.shape[1] // (d * 128)
    kcol, vcol = _attn_specs(cb, q_off, k_off, v_off, gqa)
    tile, blk = (TQ, 128), (BLK, 128)
    halo = lambda t: jnp.maximum(t * nq - 1, 0)
    in_specs = [
        pl.BlockSpec(memory_space=pltpu.SMEM),
        pl.BlockSpec(tile, lambda r, hp, t: (t, r * cb + q_off + hp)),
        pl.BlockSpec(blk, lambda r, hp, t: (halo(t), kcol(r, hp))),
        pl.BlockSpec(tile, lambda r, hp, t: (t, kcol(r, hp))),
        pl.BlockSpec(blk, lambda r, hp, t: (halo(t), vcol(r, hp))),
        pl.BlockSpec(tile, lambda r, hp, t: (t, vcol(r, hp))),
    ]
    out_spec = pl.BlockSpec(tile, lambda r, hp, t: (t, r * 4 + hp))
    out = jax.ShapeDtypeStruct((Ls, d * 512), F32)
    return _call(body, (slopes, X, X, X, X, X), name=name, grid=(d, 4, nt), in_specs=in_specs,
                 out_specs=(out_spec, out_spec), out_shape=(out, out),
                 sem=("parallel", "parallel", "arbitrary"), comm=comm)


def attn_bwd(X, o, lse, do, dlse, d, q_off, k_off, v_off, gqa, slope0, maxdist, name, comm=None):
    Ls = X.shape[0]
    slopes = jnp.asarray(ALIBI)

    nq = Ls // BLK if Ls // BLK <= ATT_NQ_ONE else ATT_NQ
    TQ = nq * BLK
    nt = Ls // TQ
    nt_dims, tn_dims = (((1,), (1,)), ((), ())), (((0,), (0,)), ((), ()))

    def body(sl_ref, q_ref, kh_ref, kc_ref, vh_ref, vc_ref, o_ref, lse_ref, do_ref, dlse_ref,
             dq_ref, dk_ref, dv_ref, ak_ref, av_ref, pk_ref, pv_ref):
        hp, t = pl.program_id(1), pl.program_id(2)

        @pl.when(t == 0)
        def _():
            pk_ref[...] = jnp.zeros_like(pk_ref)
            pv_ref[...] = jnp.zeros_like(pv_ref)

        @pl.when(t < nt)
        def _():
            ops = _attn_operands(nq, hp, gqa, q_ref, kh_ref, kc_ref, vh_ref, vc_ref)
            inner, first = _attn_biases(t, [sl_ref[slope0 + 2 * hp + e] for e in range(2)], d, maxdist)
            sv = [_attn_scores(q, kw, first[e] if b == 0 else inner[e]) for (b, e, rows, cs, q, kw, vw) in ops]
            p = [jnp.exp(s - lse_ref[op[2], op[1] * HD:op[1] * HD + 1]) for s, op in zip(sv, ops)]
            dov = [do_ref[op[2], op[3]] for op in ops]
            delta = [jnp.sum(dd * o_ref[op[2], op[3]], axis=-1, keepdims=True) for dd, op in zip(dov, ops)]
            dob = [dd.astype(BF16) for dd in dov]
            dp = [lax.dot_general(dd, op[6], nt_dims, preferred_element_type=F32) for dd, op in zip(dob, ops)]
            ds = [(pp * (x - dl + dlse_ref[op[2], op[1] * HD:op[1] * HD + 1])).astype(BF16)
                  for pp, x, dl, op in zip(p, dp, delta, ops)]
            dq = [jnp.dot(x, op[5], preferred_element_type=F32) * (HD ** -0.5) for x, op in zip(ds, ops)]
            dkw = [lax.dot_general(x, op[4], tn_dims, preferred_element_type=F32) for x, op in zip(ds, ops)]
            dvw = [lax.dot_general(pp.astype(BF16), dd, tn_dims, preferred_element_type=F32)
                   for pp, dd in zip(p, dob)]
            ak_ref[...] = jnp.zeros_like(ak_ref)
            av_ref[...] = jnp.zeros_like(av_ref)
            for (b, e, rows, cs, q, kw, vw), x, yk, yv in zip(ops, dq, dkw, dvw):
                dq_ref[rows, cs] = x
                ak_ref[b * BLK:(b + 2) * BLK, cs] += yk
                av_ref[b * BLK:(b + 2) * BLK, cs] += yv
            if nt == 1:
                dk_ref[...] = ak_ref[BLK:, :]
                dv_ref[...] = av_ref[BLK:, :]
                return
            last = slice(TQ - BLK, TQ)
            dk_ref[...] = pk_ref[...]
            dv_ref[...] = pv_ref[...]
            dk_ref[last, :] += ak_ref[0:BLK, :]
            dv_ref[last, :] += av_ref[0:BLK, :]
            pk_ref[...] = ak_ref[BLK:, :]
            pv_ref[...] = av_ref[BLK:, :]

        @pl.when(t == nt)
        def _():
            dk_ref[...] = pk_ref[...]
            dv_ref[...] = pv_ref[...]

    cb = X.shape[1] // (d * 128)
    kcol, vcol = _attn_specs(cb, q_off, k_off, v_off, gqa)
    tile, blk = (TQ, 128), (BLK, 128)
    cur = lambda t: jnp.minimum(t, nt - 1)
    halo = lambda t: jnp.maximum(cur(t) * nq - 1, 0)
    ospec = pl.BlockSpec(tile, lambda r, hp, t: (cur(t), r * 4 + hp))
    in_specs = [
        pl.BlockSpec(memory_space=pltpu.SMEM),
        pl.BlockSpec(tile, lambda r, hp, t: (cur(t), r * cb + q_off + hp)),
        pl.BlockSpec(blk, lambda r, hp, t: (halo(t), kcol(r, hp))),
        pl.BlockSpec(tile, lambda r, hp, t: (cur(t), kcol(r, hp))),
        pl.BlockSpec(blk, lambda r, hp, t: (halo(t), vcol(r, hp))),
        pl.BlockSpec(tile, lambda r, hp, t: (cur(t), vcol(r, hp))),
        ospec, ospec, ospec, ospec,
    ]
    shifted = pl.BlockSpec(tile, lambda r, hp, t: (jnp.maximum(t - 1, 0), r * 4 + hp))
    out = jax.ShapeDtypeStruct((Ls, d * 512), F32)
    return _call(body, (slopes, X, X, X, X, X, o, lse, do, dlse), name=name, grid=(d, 4, nt + 1 if nt > 1 else 1),
                 in_specs=in_specs, out_specs=(ospec, shifted, shifted), out_shape=(out, out, out),
                 scratch_shapes=[pltpu.VMEM((TQ + BLK, 128), F32), pltpu.VMEM((TQ + BLK, 128), F32),
                                 pltpu.VMEM((TQ, 128), F32), pltpu.VMEM((TQ, 128), F32)],
                 sem=("parallel", "parallel", "arbitrary"), comm=comm)


def attn_merge_fwd(oa, la, sink, obs, lbs, name):
    L = oa.shape[0]
    tl = DIL_TL

    def body(oa_ref, la_ref, sk_ref, o1, o4, o16, l1, l4, l16, m_ref, so4, so16, sl4, sl16):
        m_ref[:, 0:512] = (oa_ref[...] * _sigmoid(la_ref[...] - sk_ref[...])).astype(BF16)
        for src, dst, d in ((o4, so4, 4), (o16, so16, 16), (l4, sl4, 4), (l16, sl16, 16)):
            _from_dilated(src, dst, d)
        for j in range(4):
            cs = slice(j * 128, (j + 1) * 128)
            a, b, c = l1[:, cs], sl4[j], sl16[j]
            mx = jnp.maximum(jnp.maximum(a, b), c)
            ea, eb, ec = jnp.exp(a - mx), jnp.exp(b - mx), jnp.exp(c - mx)
            inv = 1.0 / (ea + eb + ec)
            m_ref[:, 512 + j * 128:512 + (j + 1) * 128] = (
                (ea * inv) * o1[:, cs] + (eb * inv) * so4[j] + (ec * inv) * so16[j]).astype(BF16)

    big = pl.BlockSpec((tl, 512), lambda i: (i, 0))
    dil = [big, _dil_spec(tl, 4, 512), _dil_spec(tl, 16, 512)]
    return pl.pallas_call(
        body, name=name, grid=(L // tl,),
        in_specs=[big, big, pl.BlockSpec((1, 512), lambda i: (0, 0))] + dil + dil,
        out_specs=pl.BlockSpec((tl, 1024), lambda i: (i, 0)),
        out_shape=jax.ShapeDtypeStruct((L, 1024), BF16), scratch_shapes=[pltpu.VMEM((4, tl, 128), F32)] * 4,
        compiler_params=_cparams("parallel"),
    )(oa, la, sink, *obs, *lbs)


def attn_merge_bwd(dm, oa, la, sink, obs, lbs, name):
    L = oa.shape[0]
    tl = DIL_TL

    def body(dm_ref, oa_ref, la_ref, sk_ref, o1, o4, o16, l1, l4, l16,
             doa_ref, dla_ref, d1, d4, d16, g1, g4, g16, sums_ref, so4, so16, sl4, sl16, sd4, sd16, sg4, sg16):
        @pl.when(pl.program_id(0) == 0)
        def _():
            sums_ref[...] = jnp.zeros_like(sums_ref)

        for src, dst, d in ((o4, so4, 4), (o16, so16, 16), (l4, sl4, 4), (l16, sl16, 16)):
            _from_dilated(src, dst, d)
        ones = _seg_ones(HD)
        for t in range(4):
            cs = slice(t * 128, (t + 1) * 128)
            dma = dm_ref[:, cs]
            keep = _sigmoid(la_ref[:, cs] - sk_ref[:, cs])
            doa_ref[:, cs] = dma * keep
            tt = dma * oa_ref[:, cs] * keep * (1.0 - keep)
            dla_ref[:, cs] = _segsum(tt, ones)
            sums_ref[:, cs] += _fold8(-tt)
            dmb = dm_ref[:, 512 + t * 128:512 + (t + 1) * 128]
            a, b, c = l1[:, cs], sl4[t], sl16[t]
            mx = jnp.maximum(jnp.maximum(a, b), c)
            ea, eb, ec = jnp.exp(a - mx), jnp.exp(b - mx), jnp.exp(c - mx)
            inv = 1.0 / (ea + eb + ec)
            wa, wb, wc = ea * inv, eb * inv, ec * inv
            d1[:, cs] = wa * dmb
            sd4[t] = wb * dmb
            sd16[t] = wc * dmb
            sa = _segsum(dmb * o1[:, cs], ones)
            sb = _segsum(dmb * so4[t], ones)
            sc_ = _segsum(dmb * so16[t], ones)
            mean = wa * sa + wb * sb + wc * sc_
            g1[:, cs] = wa * (sa - mean)
            sg4[t] = wb * (sb - mean)
            sg16[t] = wc * (sc_ - mean)
        for src, dst, d in ((sd4, d4, 4), (sd16, d16, 16), (sg4, g4, 4), (sg16, g16, 16)):
            _to_dilated(src, dst, d)

    big = pl.BlockSpec((tl, 512), lambda i: (i, 0))
    dil = [big, _dil_spec(tl, 4, 512), _dil_spec(tl, 16, 512)]
    sd = jax.ShapeDtypeStruct
    shp = [sd((L, 512), F32), sd((L // 4, 4 * 512), F32), sd((L // 16, 16 * 512), F32)]
    return pl.pallas_call(
        body, name=name, grid=(L // tl,),
        in_specs=[pl.BlockSpec((tl, 1024), lambda i: (i, 0)), big, big,
                  pl.BlockSpec((1, 512), lambda i: (0, 0))] + dil + dil,
        out_specs=tuple([big, big] + dil + dil + [pl.BlockSpec((8, 512), lambda i: (0, 0))]),
        out_shape=tuple([shp[0], shp[0]] + shp + shp + [sd((8, 512), F32)]),
        scratch_shapes=[pltpu.VMEM((4, tl, 128), F32)] * 8, compiler_params=_cparams("arbitrary"),
    )(dm, oa, la, sink, *obs, *lbs)


def _shift_down(x, halo, k, first):
    rows = lax.broadcasted_iota(jnp.int32, (8, x.shape[1]), 0)
    out = pltpu.roll(x, k, axis=0)
    hrows = jnp.where(first, 0.0, pltpu.roll(halo, k, axis=0))
    top = jnp.where(rows < k, hrows, out[0:8, :])
    return jnp.concatenate([top, out[8:, :]], axis=0)


def _shift_up(x, nxt, k):
    tl = x.shape[0]
    rows = lax.broadcasted_iota(jnp.int32, (8, x.shape[1]), 0)
    out = pltpu.roll(x, tl - k, axis=0)
    bottom = jnp.where(rows >= 8 - k, pltpu.roll(nxt, 8 - k, axis=0), out[tl - 8:, :])
    return jnp.concatenate([out[:tl - 8, :], bottom], axis=0)


def _silu(x):
    return x * _sigmoid(x)


def _dsilu(x):
    s = _sigmoid(x)
    return s * (1.0 + x * (1.0 - s))


def ffn_act_fwd(ua, ub, cw, name, comm=None):
    L, F = ua.shape
    tl = _rtile(L, 256)
    tc = _tile(F, 1408)
    hb = tl // 8

    def body(ua_ref, uah_ref, ub_ref, ubh_ref, wa_ref, wb_ref, o_ref, ac_ref, bc_ref):
        first = pl.program_id(1) == 0

        def conv(x_ref, h_ref, w_ref):
            x = x_ref[...]
            h = h_ref[...]
            return (w_ref[2:3, :] * x + w_ref[1:2, :] * _shift_down(x, h, 1, first)
                    + w_ref[0:1, :] * _shift_down(x, h, 2, first))

        a = conv(ua_ref, uah_ref, wa_ref)
        b = conv(ub_ref, ubh_ref, wb_ref)
        ac_ref[...] = a
        bc_ref[...] = b
        o_ref[...] = (_silu(a) * b).astype(BF16)

    main = pl.BlockSpec((tl, tc), lambda j, i: (i, j))
    halo = pl.BlockSpec((8, tc), lambda j, i: (jnp.maximum(i * hb - 1, 0), j))
    wa = pl.BlockSpec((3, tc), lambda j, i: (0, j))
    wb = pl.BlockSpec((3, tc), lambda j, i: (0, j + F // tc))
    f32 = jax.ShapeDtypeStruct((L, F), F32)
    return _call(body, (ua, ua, ub, ub, cw, cw), name=name, grid=(F // tc, L // tl),
                 in_specs=[main, halo, main, halo, wa, wb], out_specs=(main, main, main),
                 out_shape=(jax.ShapeDtypeStruct((L, F), BF16), f32, f32), sem=("parallel", "parallel"), comm=comm)


def ffn_act_bwd(ua, ub, ac, bc, cw, dact, name, comm=None):
    L, F = ua.shape
    tl = _rtile(L, 256)
    tc = _tile(F, 1408)
    nrt = L // tl

    def body(ua_ref, ub_ref, ac_ref, bc_ref, wa_ref, wb_ref, da_ref, dua_ref, dub_ref, sums_ref, ca_ref, cb_ref):
        i = pl.program_id(1)

        @pl.when(i == 0)
        def _():
            sums_ref[...] = jnp.zeros_like(sums_ref)
            ca_ref[...] = jnp.zeros_like(ca_ref)
            cb_ref[...] = jnp.zeros_like(cb_ref)

        a, b = ac_ref[...], bc_ref[...]
        dact_v = da_ref[...]
        dya = dact_v * b * _dsilu(a)
        dyb = dact_v * _silu(a)
        for (dy, w_ref, c_ref, d_ref, x_ref, base) in ((dya, wa_ref, ca_ref, dua_ref, ua_ref, 0),
                                                        (dyb, wb_ref, cb_ref, dub_ref, ub_ref, 24)):
            nxt = c_ref[...]
            ups = (dy, _shift_up(dy, nxt, 1), _shift_up(dy, nxt, 2))
            d_ref[...] = (w_ref[2:3, :] * ups[0] + w_ref[1:2, :] * ups[1] + w_ref[0:1, :] * ups[2]).astype(BF16)
            c_ref[...] = dy[0:8, :]
            x = x_ref[...]
            for k in range(3):
                sums_ref[base + 8 * (2 - k):base + 8 * (2 - k) + 8, :] += _fold8(ups[k] * x)

    rev = lambda i: nrt - 1 - i
    main = pl.BlockSpec((tl, tc), lambda j, i: (rev(i), j))
    wa = pl.BlockSpec((3, tc), lambda j, i: (0, j))
    wb = pl.BlockSpec((3, tc), lambda j, i: (0, j + F // tc))
    ob = jax.ShapeDtypeStruct((L, F), BF16)
    return _call(body, (ua, ub, ac, bc, cw, cw, dact), name=name, grid=(F // tc, nrt),
                 in_specs=[main, main, main, main, wa, wb, main],
                 out_specs=(main, main, pl.BlockSpec((48, tc), lambda j, i: (0, j))),
                 out_shape=(ob, ob, jax.ShapeDtypeStruct((48, F), F32)),
                 scratch_shapes=[pltpu.VMEM((8, tc), F32), pltpu.VMEM((8, tc), F32)],
                 sem=("parallel", "arbitrary"), comm=comm)


def attn_vectors(qna, kna, qnb, knb, sinks):
    ones = jnp.ones((128,), F32)
    wvec = jnp.concatenate([jnp.tile(qna, 8), jnp.tile(kna, 2), ones, jnp.tile(qnb, 8), jnp.tile(knb, 8),
                            jnp.tile(ones, 4)]).reshape(1, ATTN_IN)
    return wvec, jnp.repeat(sinks, HD).reshape(1, 512)


def _with_comm(result, comm):
    return result if comm is not None else (result, None)


def attention_block_fwd(h, w_in, wvec, sinkvec, w_out, tag, comms=None):
    L = h.shape[0]
    comms = comms or {}
    got = {}
    qkv = matmul([(h, w_in)], "nn", tag + "_qkv")
    X, X4, X16 = qknorm_fwd(qkv, wvec, tag + "_qknorm")
    (oa, la), got['swa'] = _with_comm(attn_fwd(X, 1, 0, 4, 5, True, 0, BLK - 1, tag + "_swa",
                                               comm=comms.get('swa')), comms.get('swa'))
    views = {1: (X, 6, 10, 14), 4: (X4, 0, 4, 8), 16: (X16, 0, 4, 8)}
    obs, lbs = [], []
    for window, d in B_BRANCHES:
        xd, qo, ko, vo = views[d]
        (o, l), got[d] = _with_comm(attn_fwd(xd, d, qo, ko, vo, False, 8, window // d,
                                             tag + f"_dil{d}", comm=comms.get(d)), comms.get(d))
        obs.append(o)
        lbs.append(l)
    m = attn_merge_fwd(oa, la, sinkvec, obs, lbs, tag + "_merge")
    if w_out is None:
        w_out = got[16][0].reshape(D, D)
        got['w_out'] = w_out
    y = matmul([(m, w_out)], "nn", tag + "_out")
    return y, (h, qkv, views, oa, la, obs, lbs, m), got


def attention_block_bwd(dy, res, w_in, wvec, sinkvec, w_out, tag, comms=None, send_w_out_on=None):
    h, qkv, views, oa, la, obs, lbs, m = res
    comms = dict(comms or {})
    got = {}
    g_w_out = matmul([(m, dy)], "tn", tag + "_dwout", out_dtype=BF16)
    if send_w_out_on is not None:
        comms[send_w_out_on] = ([g_w_out.reshape(N_DEV, D // N_DEV, D)], False)
    dm = matmul([(dy, w_out)], "nt", tag + "_dm")
    doa, dla, d1, d2, d3, g1, g2, g3, sinksums = attn_merge_bwd(dm, oa, la, sinkvec, obs, lbs, tag + "_dmerge")
    d_a, got['swa'] = _with_comm(attn_bwd(views[1][0], oa, la, doa, dla, 1, 0, 4, 5, True, 0, BLK - 1,
                                          tag + "_dswa", comm=comms.get('swa')), comms.get('swa'))
    d_b = []
    for (window, d), o, l, do, dl in zip(B_BRANCHES, obs, lbs, (d1, d2, d3), (g1, g2, g3)):
        xd, qo, ko, vo = views[d]
        dqkv_d, got[d] = _with_comm(attn_bwd(xd, o, l, do, dl, d, qo, ko, vo, False, 8, window // d,
                                             tag + f"_ddil{d}", comm=comms.get(d)), comms.get(d))
        d_b.append(dqkv_d)
    dqkv, wsums = qknorm_bwd(qkv, wvec, d_a, d_b, tag + "_dqknorm")
    g_w_in = matmul([(h, dqkv)], "tn", tag + "_dwin", out_dtype=BF16)
    dh = matmul([(dqkv, w_in)], "nt", tag + "_dh")
    ws = wsums.sum(axis=0)
    grads = dict(
        w_in=g_w_in, w_out=g_w_out,
        q_norm_a=ws[0:512].reshape(8, HD).sum(axis=0), k_norm_a=ws[512:640].reshape(2, HD).sum(axis=0),
        q_norm_b=ws[768:1280].reshape(8, HD).sum(axis=0), k_norm_b=ws[1280:1792].reshape(8, HD).sum(axis=0),
        sinks=sinksums.sum(axis=0).reshape(8, HD).sum(axis=1))
    return dh, grads, got


def ffn_block_fwd(h, w_up_a, w_up_b, cw, w_down, tag, comm=None):
    ua = matmul([(h, w_up_a)], "nn", tag + "_upa")
    ub = matmul([(h, w_up_b)], "nn", tag + "_upb")
    (act, ac, bc), got = _with_comm(ffn_act_fwd(ua, ub, cw, tag + "_act", comm=comm), comm)
    f = matmul([(act, w_down)], "nn", tag + "_down")
    return f, (h, ua, ub, ac, bc, act), got


def ffn_block_bwd(df, res, w_up_a, w_up_b, cw, w_down, tag, comm=None):
    h, ua, ub, ac, bc, act = res
    g_down = matmul([(act, df)], "tn", tag + "_dwdown", out_dtype=BF16)
    dact = matmul([(df, w_down)], "nt", tag + "_dact")
    (dua, dub, sums), got = _with_comm(ffn_act_bwd(ua, ub, ac, bc, cw, dact, tag + "_dactk", comm=comm), comm)
    g_up = jnp.concatenate([_cols_to_slabs(matmul([(h, dua)], "tn", tag + "_dwupa", out_dtype=BF16), N_DEV // 2),
                            _cols_to_slabs(matmul([(h, dub)], "tn", tag + "_dwupb", out_dtype=BF16), N_DEV // 2)],
                           axis=0)
    dh = matmul([(dua, w_up_a), (dub, w_up_b)], "nt", tag + "_dh")
    s = sums.reshape(2, 3, 8, D_FF).sum(axis=2)
    g_conv = jnp.concatenate([s[0], s[1]], axis=1)
    return dh, dict(w_up=g_up, conv=g_conv, w_down=g_down), got


def s5_params(lam_re, lam_im, log_dt, b_re, b_im, c_re, c_im):
    dt = jnp.exp(log_dt)[:, None]
    mag, ang = jnp.exp(lam_re * dt), lam_im * dt
    a_re, a_im = mag * jnp.cos(ang), mag * jnp.sin(ang)
    nr, ni = a_re - 1.0, a_im
    den = lam_re * lam_re + lam_im * lam_im
    f_re = (nr * lam_re + ni * lam_im) / den
    f_im = (ni * lam_re - nr * lam_im) / den
    eye = jnp.eye(16, dtype=F32)[:, None, :, None]
    bd = lambda b: (eye * jnp.transpose(b, (0, 2, 1))[:, :, None, :]).reshape(S5_W, S5_P)
    cd = lambda c: (eye * jnp.transpose(c, (0, 2, 1))[:, :, None, :]).reshape(S5_P, S5_W)
    flat = lambda t: t.reshape(1, S5_P)
    return flat(a_re), flat(a_im), flat(f_re), flat(f_im), bd(b_re), bd(b_im), cd(c_re), cd(c_im)


def _scan_tables(a_re, a_im, reverse):
    pows = [(a_re, a_im)]
    for _ in range(7):
        pr, pi = pows[-1]
        pows.append((pr * a_re - pi * a_im, pr * a_im + pi * a_re))
    order = list(range(7, -1, -1)) if reverse else list(range(8))
    z = jnp.zeros_like(a_re)
    rows = [pows[0][0], pows[0][1], pows[1][0], pows[1][1], pows[3][0], pows[3][1], z, z]
    rows += [pows[k][0] for k in order] + [pows[k][1] for k in order]
    return jnp.concatenate(rows, axis=0)


def _block_scan(er, ei, tab_ref, cr, ci, reverse):
    rows = lax.broadcasted_iota(jnp.int32, er.shape, 0)
    for idx, s in enumerate((1, 2, 4)):
        if reverse:
            sr, si, keep = pltpu.roll(er, 8 - s, axis=0), pltpu.roll(ei, 8 - s, axis=0), rows < 8 - s
        else:
            sr, si, keep = pltpu.roll(er, s, axis=0), pltpu.roll(ei, s, axis=0), rows >= s
        sr, si = jnp.where(keep, sr, 0.0), jnp.where(keep, si, 0.0)
        ar, ai = tab_ref[2 * idx:2 * idx + 1, :], tab_ref[2 * idx + 1:2 * idx + 2, :]
        er, ei = er + ar * sr - ai * si, ei + ar * si + ai * sr
    pr, pi_ = tab_ref[8:16, :], tab_ref[16:24, :]
    er, ei = er + pr * cr - pi_ * ci, ei + pr * ci + pi_ * cr
    return er, ei


def s5_scan_fwd(bu_re, bu_im, a_re, a_im, f_re, f_im, name):
    L, P = bu_re.shape
    tl = _rtile(L, 512)
    tab = _scan_tables(a_re, a_im, False)
    fvec = jnp.concatenate([f_re, f_im] + [jnp.zeros_like(f_re)] * 6, axis=0)

    def body(br_ref, bi_ref, tab_ref, f_ref, xr_ref, xi_ref, c_ref):
        @pl.when(pl.program_id(0) == 0)
        def _():
            c_ref[...] = jnp.zeros_like(c_ref)

        def blk(i, carry):
            cr, ci = carry
            rows = pl.ds(pl.multiple_of(i * 8, 8), 8)
            br, bi = br_ref[rows, :], bi_ref[rows, :]
            fr, fi = f_ref[0:1, :], f_ref[1:2, :]
            er, ei = _block_scan(fr * br - fi * bi, fr * bi + fi * br, tab_ref, cr, ci, False)
            xr_ref[rows, :] = er
            xi_ref[rows, :] = ei
            return er[7:8, :], ei[7:8, :]

        cr, ci = lax.fori_loop(0, tl // 8, blk, (c_ref[0:1, :], c_ref[1:2, :]))
        c_ref[0:1, :] = cr
        c_ref[1:2, :] = ci

    big = pl.BlockSpec((tl, P), lambda i: (i, 0))
    out = jax.ShapeDtypeStruct((L, P), F32)
    return pl.pallas_call(
        body, name=name, grid=(L // tl,),
        in_specs=[big, big, pl.BlockSpec((24, P), lambda i: (0, 0)), pl.BlockSpec((8, P), lambda i: (0, 0))],
        out_specs=(big, big), out_shape=(out, out), scratch_shapes=[pltpu.VMEM((8, P), F32)],
        compiler_params=_cparams("arbitrary"))(bu_re, bu_im, tab, fvec)


def s5_scan_bwd(dx_re, dx_im, x_re, x_im, bu_re, bu_im, a_re, a_im, f_re, f_im, name):
    L, P = dx_re.shape
    tl = _rtile(L, 256)
    nt = L // tl
    tab = _scan_tables(a_re, -a_im, True)
    fvec = jnp.concatenate([f_re, f_im] + [jnp.zeros_like(f_re)] * 6, axis=0)

    def body(gr_ref, gi_ref, xr_ref, xi_ref, br_ref, bi_ref, tab_ref, f_ref, dbr_ref, dbi_ref, s_ref, c_ref):
        @pl.when(pl.program_id(0) == 0)
        def _():
            c_ref[...] = jnp.zeros_like(c_ref)
            s_ref[...] = jnp.zeros_like(s_ref)

        def blk(k, carry):
            cr, ci = carry
            i = tl // 8 - 1 - k
            rows = pl.ds(pl.multiple_of(i * 8, 8), 8)
            er, ei = _block_scan(gr_ref[rows, :], gi_ref[rows, :], tab_ref, cr, ci, True)
            rid = lax.broadcasted_iota(jnp.int32, er.shape, 0)
            sr = jnp.where(rid == 7, cr, pltpu.roll(er, 7, axis=0))
            si = jnp.where(rid == 7, ci, pltpu.roll(ei, 7, axis=0))
            xr, xi = xr_ref[rows, :], xi_ref[rows, :]
            s_ref[0:8, :] += sr * xr + si * xi
            s_ref[8:16, :] += si * xr - sr * xi
            br, bi = br_ref[rows, :], bi_ref[rows, :]
            s_ref[16:24, :] += er * br + ei * bi
            s_ref[24:32, :] += ei * br - er * bi
            fr, fi = f_ref[0:1, :], f_ref[1:2, :]
            dbr_ref[rows, :] = fr * er + fi * ei
            dbi_ref[rows, :] = fr * ei - fi * er
            return er[0:1, :], ei[0:1, :]

        cr, ci = lax.fori_loop(0, tl // 8, blk, (c_ref[0:1, :], c_ref[1:2, :]))
        c_ref[0:1, :] = cr
        c_ref[1:2, :] = ci

    big = pl.BlockSpec((tl, P), lambda i: (nt - 1 - i, 0))
    out = jax.ShapeDtypeStruct((L, P), F32)
    return pl.pallas_call(
        body, name=name, grid=(nt,),
        in_specs=[big] * 6 + [pl.BlockSpec((24, P), lambda i: (0, 0)), pl.BlockSpec((8, P), lambda i: (0, 0))],
        out_specs=(big, big, pl.BlockSpec((32, P), lambda i: (0, 0))),
        out_shape=(out, out, jax.ShapeDtypeStruct((32, P), F32)), scratch_shapes=[pltpu.VMEM((8, P), F32)],
        compiler_params=_cparams("arbitrary"))(dx_re, dx_im, x_re, x_im, bu_re, bu_im, tab, fvec)


_GK, _GC = math.sqrt(2.0 / math.pi), 0.044715


def _gelu(y):
    return 0.5 * y * (1.0 + jnp.tanh(_GK * (y + _GC * y * y * y)))


def _dgelu(y):
    t = jnp.tanh(_GK * (y + _GC * y * y * y))
    return 0.5 * (1.0 + t) + 0.5 * y * (1.0 - t * t) * _GK * (1.0 + 3.0 * _GC * y * y)


def s5_out_fwd(x_re, x_im, u, cd_re, cd_im, dskip, glu_w, glu_b, name):
    L = u.shape[0]
    tl = _rtile(L, 512)

    def body(xr_ref, xi_ref, u_ref, cr_ref, ci_ref, d_ref, w_ref, b_ref, y_ref, o_ref):
        y = (jnp.dot(xr_ref[...].astype(BF16), cr_ref[...], preferred_element_type=F32)
             - jnp.dot(xi_ref[...].astype(BF16), ci_ref[...], preferred_element_type=F32)
             + d_ref[...] * u_ref[...])
        y_ref[...] = y
        g = _gelu(y)
        z = jnp.dot(g.astype(BF16), w_ref[...], preferred_element_type=F32) + b_ref[...]
        o_ref[...] = (g * _sigmoid(z)).astype(BF16)

    big = pl.BlockSpec((tl, S5_P), lambda i: (i, 0))
    sm = pl.BlockSpec((tl, S5_W), lambda i: (i, 0))
    full = lambda r, c: pl.BlockSpec((r, c), lambda i: (0, 0))
    return pl.pallas_call(
        body, name=name, grid=(L // tl,),
        in_specs=[big, big, sm, full(S5_P, S5_W), full(S5_P, S5_W), full(1, S5_W), full(S5_W, S5_W), full(1, S5_W)],
        out_specs=(sm, sm),
        out_shape=(jax.ShapeDtypeStruct((L, S5_W), F32), jax.ShapeDtypeStruct((L, S5_W), BF16)),
        compiler_params=_cparams("parallel"))(x_re, x_im, u, cd_re, cd_im, dskip, glu_w, glu_b)


def s5_out_bwd(dout, y, u, x_re, x_im, cd_re, cd_im, dskip, glu_w, glu_b, name, dout_col=0):
    L = u.shape[0]
    tl = _rtile(L, 256)
    nt_dims = (((1,), (1,)), ((), ()))
    tn_dims = (((0,), (0,)), ((), ()))

    def body(do_ref, y_ref, u_ref, xr_ref, xi_ref, cr_ref, ci_ref, d_ref, w_ref, b_ref,
             dxr_ref, dxi_ref, du_ref, dcr_ref, dci_ref, dw_ref, s_ref):
        @pl.when(pl.program_id(0) == 0)
        def _():
            dcr_ref[...] = jnp.zeros_like(dcr_ref)
            dci_ref[...] = jnp.zeros_like(dci_ref)
            dw_ref[...] = jnp.zeros_like(dw_ref)
            s_ref[...] = jnp.zeros_like(s_ref)

        yv, dov = y_ref[...], do_ref[...]
        g = _gelu(yv)
        gb = g.astype(BF16)
        sg = _sigmoid(jnp.dot(gb, w_ref[...], preferred_element_type=F32) + b_ref[...])
        dz = dov * g * sg * (1.0 - sg)
        dzb = dz.astype(BF16)
        dg = dov * sg + lax.dot_general(dzb, w_ref[...], nt_dims, preferred_element_type=F32)
        dw_ref[...] += lax.dot_general(gb, dzb, tn_dims, preferred_element_type=F32)
        dy = dg * _dgelu(yv)
        dyb = dy.astype(BF16)
        s_ref[0:8, :] += _fold8(dy * u_ref[...])
        s_ref[8:16, :] += _fold8(dz)
        du_ref[...] = dy * d_ref[...]
        dxr_ref[...] = lax.dot_general(dyb, cr_ref[...], nt_dims, preferred_element_type=F32)
        dxi_ref[...] = -lax.dot_general(dyb, ci_ref[...], nt_dims, preferred_element_type=F32)
        dcr_ref[...] += lax.dot_general(xr_ref[...].astype(BF16), dyb, tn_dims, preferred_element_type=F32)
        dci_ref[...] -= lax.dot_general(xi_ref[...].astype(BF16), dyb, tn_dims, preferred_element_type=F32)

    big = pl.BlockSpec((tl, S5_P), lambda i: (i, 0))
    sm = pl.BlockSpec((tl, S5_W), lambda i: (i, 0))
    full = lambda r, c: pl.BlockSpec((r, c), lambda i: (0, 0))
    sd = jax.ShapeDtypeStruct
    return pl.pallas_call(
        body, name=name, grid=(L // tl,),
        in_specs=[pl.BlockSpec((tl, S5_W), lambda i: (i, dout_col)), sm, sm, big, big, full(S5_P, S5_W),
                  full(S5_P, S5_W), full(1, S5_W), full(S5_W, S5_W), full(1, S5_W)],
        out_specs=(big, big, sm, full(S5_P, S5_W), full(S5_P, S5_W), full(S5_W, S5_W), full(16, S5_W)),
        out_shape=(sd((L, S5_P), F32), sd((L, S5_P), F32), sd((L, S5_W), F32), sd((S5_P, S5_W), F32),
                   sd((S5_P, S5_W), F32), sd((S5_W, S5_W), F32), sd((16, S5_W), F32)),
        compiler_params=_cparams("arbitrary"))(dout, y, u, x_re, x_im, cd_re, cd_im, dskip, glu_w, glu_b)


def s5_block_fwd(u, params, dskip, glu_w, glu_b, tag):
    a_re, a_im, f_re, f_im, bd_re, bd_im, cd_re, cd_im = params
    bu_re = matmul([(u, bd_re.astype(BF16))], "nn", tag + "_bure")
    bu_im = matmul([(u, bd_im.astype(BF16))], "nn", tag + "_buim")
    x_re, x_im = s5_scan_fwd(bu_re, bu_im, a_re, a_im, f_re, f_im, tag + "_scan")
    y, out = s5_out_fwd(x_re, x_im, u, cd_re.astype(BF16), cd_im.astype(BF16), dskip, glu_w, glu_b, tag + "_out")
    return out, (u, bu_re, bu_im, x_re, x_im, y)


def s5_block_bwd(dout, res, params, dskip, glu_w, glu_b, tag, dout_col=0):
    u, bu_re, bu_im, x_re, x_im, y = res
    a_re, a_im, f_re, f_im, bd_re, bd_im, cd_re, cd_im = params
    dxr, dxi, du, dcr, dci, dglu_w, sums = s5_out_bwd(dout, y, u, x_re, x_im, cd_re.astype(BF16), cd_im.astype(BF16),
                                                      dskip, glu_w, glu_b, tag + "_dout", dout_col=dout_col)
    dbr, dbi, acc = s5_scan_bwd(dxr, dxi, x_re, x_im, bu_re, bu_im, a_re, a_im, f_re, f_im, tag + "_dscan")
    du = du + matmul([(dbr, bd_re.astype(BF16)), (dbi, bd_im.astype(BF16))], "nt", tag + "_du")
    dbd_re = matmul([(u, dbr)], "tn", tag + "_dbdre")
    dbd_im = matmul([(u, dbi)], "tn", tag + "_dbdim")
    acc = acc.reshape(4, 8, S5_P).sum(axis=1)
    s = sums.reshape(2, 8, S5_W).sum(axis=1)
    cot = (acc[0:1], acc[1:2], acc[2:3], acc[3:4], dbd_re, dbd_im, dcr, dci)
    return du, cot, dict(dskip=s[0], glu_w=dglu_w, glu_b=s[1])


DN_Z0, DN_NT = 18, 18
REC_U0, REC_A0 = 3072, 3328


def rec_cols_permute(w):
    return jnp.concatenate([w[..., S5_W:REC_A0], w[..., :S5_W], w[..., REC_A0:]], axis=-1)


def rec_cols_restore(w):
    return jnp.concatenate([w[..., REC_U0:REC_A0], w[..., :REC_U0], w[..., REC_A0:]], axis=-1)


DN_W = DN_H * DN_DK
DN_NI = 4


def _dn_conv4(taps, w_ref):
    xc = w_ref[3:4, :] * taps[0]
    for k in range(1, 4):
        xc = xc + w_ref[3 - k:4 - k, :] * taps[k]
    return xc


def dn_prep_fwd(rin, cw, name, comm=None):
    L = rin.shape[0]
    tl = _rtile(L, 256)
    hb = tl // 8

    def body(x_ref, h_ref, w_ref, o_ref):
        j = pl.program_id(0)
        first = pl.program_id(1) == 0
        x, h = x_ref[...], h_ref[...]
        s = _silu(_dn_conv4([x] + [_shift_down(x, h, k, first) for k in range(1, 4)], w_ref))
        scale = jnp.where(j == 0, DN_DK ** -0.5, 1.0)
        for hd in _HEADS:
            cs = slice(hd * 128, (hd + 1) * 128)
            sh = s[:, cs]
            r = lax.rsqrt(jnp.sum(sh * sh, axis=-1, keepdims=True) + EPS)
            o_ref[:, cs] = jnp.where(j < 2, sh * r * scale, sh)

    main = pl.BlockSpec((tl, DN_W), lambda j, i: (i, j))
    halo = pl.BlockSpec((8, DN_W), lambda j, i: (jnp.maximum(i * hb - 1, 0), j))
    return _call(body, (rin, rin, cw), name=name, grid=(3, L // tl),
                 in_specs=[main, halo, pl.BlockSpec((4, DN_W), lambda j, i: (0, j))],
                 out_specs=main, out_shape=jax.ShapeDtypeStruct((L, 3 * DN_W), F32),
                 sem=("parallel", "parallel"), comm=comm)


def dn_prep_bwd(rin, cw, dout, name):
    L = rin.shape[0]
    tl = _rtile(L, 256)
    hb = tl // 8
    nrt = L // tl

    def body(x_ref, h_ref, w_ref, d_ref, dx_ref, s_ref, c_ref):
        j = pl.program_id(0)
        i = pl.program_id(1)
        first = i == nrt - 1

        @pl.when(i == 0)
        def _():
            s_ref[...] = jnp.zeros_like(s_ref)
            c_ref[...] = jnp.zeros_like(c_ref)

        x, h = x_ref[...], h_ref[...]
        taps = [x] + [_shift_down(x, h, k, first) for k in range(1, 4)]
        xc = _dn_conv4(taps, w_ref)
        s = _silu(xc)
        scale = jnp.where(j == 0, DN_DK ** -0.5, 1.0)
        pieces = []
        for hd in _HEADS:
            cs = slice(hd * 128, (hd + 1) * 128)
            sh, d = s[:, cs], d_ref[:, cs]
            r = lax.rsqrt(jnp.sum(sh * sh, axis=-1, keepdims=True) + EPS)
            n = sh * r
            dn = d * scale
            pieces.append(jnp.where(j < 2, r * (dn - n * jnp.sum(dn * n, axis=-1, keepdims=True)), d))
        dxc = jnp.concatenate(pieces, axis=1) * _dsilu(xc)
        nxt = c_ref[...]
        dx_ref[...] = _dn_conv4([dxc] + [_shift_up(dxc, nxt, k) for k in range(1, 4)], w_ref).astype(BF16)
        c_ref[...] = dxc[0:8, :]
        for k in range(4):
            s_ref[8 * (3 - k):8 * (3 - k) + 8, :] += _fold8(dxc * taps[k])

    rev = lambda i: nrt - 1 - i
    main = pl.BlockSpec((tl, DN_W), lambda j, i: (rev(i), j))
    halo = pl.BlockSpec((8, DN_W), lambda j, i: (jnp.maximum(rev(i) * hb - 1, 0), j))
    return pl.pallas_call(
        body, name=name, grid=(3, nrt),
        in_specs=[main, halo, pl.BlockSpec((4, DN_W), lambda j, i: (0, j)), main],
        out_specs=(main, pl.BlockSpec((32, DN_W), lambda j, i: (0, j))),
        out_shape=(jax.ShapeDtypeStruct((L, 3 * DN_W), BF16), jax.ShapeDtypeStruct((32, 3 * DN_W), F32)),
        scratch_shapes=[pltpu.VMEM((8, DN_W), F32)],
        compiler_params=_cparams("parallel", "arbitrary"))(rin, rin, cw, dout)


_HI = lax.Precision.HIGH
_NT = (((1,), (1,)), ((), ()))
_TN = (((0,), (0,)), ((), ()))
_HEADS = tuple(range(DN_H))


def _mm(a, b, dims=(((1,), (0,)), ((), ())), hi=False):
    if hi:
        return lax.dot_general(a, b, dims, precision=_HI, preferred_element_type=F32)
    return lax.dot_general(a.astype(BF16), b.astype(BF16), dims, preferred_element_type=F32)


def _dn_masks():
    ri = lax.broadcasted_iota(jnp.int32, (DN_C, DN_C), 0)
    ci = lax.broadcasted_iota(jnp.int32, (DN_C, DN_C), 1)
    return ri >= ci, ri > ci, (ri == ci).astype(F32)


def _dn_decay(gc, gr, causal):
    gam = [jnp.where(causal, jnp.exp(jnp.where(causal, c - r, 0.0)), 0.0) for c, r in zip(gc, gr)]
    eg, el, gl = _dn_row_decay(gc)
    return gam, eg, el, gl


def _dn_row_decay(gc):
    eg = [jnp.exp(c) for c in gc]
    el = [jnp.exp(c[DN_C - 1:DN_C, :] - c) for c in gc]
    gl = [jnp.exp(c[DN_C - 1:DN_C, :]) for c in gc]
    return eg, el, gl


def _dn_solve(k, v, beta, gam, eg, kk, strict, eye):
    ids = range(len(k))
    nmat = [jnp.where(strict, beta[h] * kk[h] * gam[h], 0.0) for h in ids]
    t = [eye - nmat[h] for h in ids]
    m = [_mm(nmat[h], nmat[h]) for h in ids]
    for step in range(5):
        t = [t[h] + _mm(t[h], m[h]) for h in ids]
        if step < 4:
            m = [_mm(m[h], m[h]) for h in ids]
    res = [eye - t[h] - _mm(nmat[h], t[h], hi=True) for h in ids]
    t = [t[h] + _mm(t[h], res[h]) for h in ids]
    rhs = [jnp.concatenate([v[h] * beta[h], k[h] * (beta[h] * eg[h])], axis=1) for h in ids]
    sol = [_mm(t[h], rhs[h], hi=True) for h in ids]
    return t, sol


def dn_chunk_fwd(qkv, gcol, grow, bcol, name, comm=None):
    L = qkv.shape[0]
    C, W = DN_C, DN_H * DN_DK
    ncb = 8
    tl = ncb * C
    nchunks = L // C
    comm1, comm2 = comm if comm is not None else (None, None)
    hs = lambda h: slice(h * 128, (h + 1) * 128)

    def intra(q_ref, k_ref, v_ref, gc_ref, gr_ref, b_ref, t_ref, sol_ref, qk_ref):
        causal, strict, eye = _dn_masks()

        def pair(p, _):
            units = [(DN_NI * p + j, h) for j in range(DN_NI) for h in _HEADS]
            rows = [pl.ds(pl.multiple_of(c * C, C), C) for c, _ in units]
            q = [q_ref[r, hs(h)] for r, (_, h) in zip(rows, units)]
            k = [k_ref[r, hs(h)] for r, (_, h) in zip(rows, units)]
            v = [v_ref[r, hs(h)] for r, (_, h) in zip(rows, units)]
            gc = [gc_ref[r, h:h + 1] for r, (_, h) in zip(rows, units)]
            gr = [gr_ref[c][h:h + 1, :] for c, h in units]
            beta = [b_ref[r, h:h + 1] for r, (_, h) in zip(rows, units)]
            gam, eg, _, _ = _dn_decay(gc, gr, causal)
            kk = [_mm(x, x, _NT) for x in k]
            t, sol = _dn_solve(k, v, beta, gam, eg, kk, strict, eye)
            qk = [_mm(a, b, _NT) * g for a, b, g in zip(q, k, gam)]
            for i, (r, (_, h)) in enumerate(zip(rows, units)):
                t_ref[r, h * C:(h + 1) * C] = t[i]
                sol_ref[r, h * 256:(h + 1) * 256] = sol[i]
                qk_ref[r, h * C:(h + 1) * C] = qk[i]
            return 0

        lax.fori_loop(0, ncb // DN_NI, pair, 0)

    def scan(q_ref, k_ref, gc_ref, sol_ref, qk_ref, o_ref, sh_ref, s_ref):
        @pl.when(pl.program_id(0) == 0)
        def _():
            s_ref[...] = jnp.zeros_like(s_ref)

        def chunk(c, _):
            rows = pl.ds(pl.multiple_of(c * C, C), C)
            q = [q_ref[rows, hs(h)] for h in _HEADS]
            k = [k_ref[rows, hs(h)] for h in _HEADS]
            sol = [sol_ref[rows, h * 256:(h + 1) * 256] for h in _HEADS]
            qk = [qk_ref[rows, h * C:(h + 1) * C] for h in _HEADS]
            eg, el, gl = _dn_row_decay([gc_ref[rows, h:h + 1] for h in _HEADS])
            S = [s_ref[hs(h), :] for h in _HEADS]
            vn = [sol[h][:, :128] - _mm(sol[h][:, 128:], S[h]) for h in _HEADS]
            o = [_mm(q[h] * eg[h], S[h]) + _mm(qk[h], vn[h]) for h in _HEADS]
            Sn = [S[h] * gl[h] + _mm(k[h] * el[h], vn[h], _TN) for h in _HEADS]
            for h in _HEADS:
                sh_ref[c, hs(h), :] = S[h]
                s_ref[hs(h), :] = Sn[h]
                o_ref[rows, hs(h)] = o[h]
            return 0

        lax.fori_loop(0, ncb, chunk, 0)

    col = lambda b: pl.BlockSpec((tl, W), lambda i: (i, b))
    small = pl.BlockSpec((tl, 8), lambda i: (i, 0))
    rowblk = lambda w: pl.BlockSpec((tl, w), lambda i: (i, 0))
    sd = jax.ShapeDtypeStruct
    (thist, solhist, qk), got1 = _with_comm(_call(
        intra, (qkv, qkv, qkv, gcol, grow, bcol), name=name + "_intra", grid=(L // tl,),
        in_specs=[col(0), col(1), col(2), small, pl.BlockSpec((ncb, 8, C), lambda i: (i, 0, 0)), small],
        out_specs=(rowblk(DN_H * C), rowblk(DN_H * 256), rowblk(DN_H * C)),
        out_shape=(sd((L, DN_H * C), F32), sd((L, DN_H * 256), F32), sd((L, DN_H * C), F32)),
        sem=("parallel",), comm=comm1), comm1)
    (o, shist), got2 = _with_comm(_call(
        scan, (qkv, qkv, gcol, solhist, qk), name=name + "_scan", grid=(L // tl,),
        in_specs=[col(0), col(1), small, rowblk(DN_H * 256), rowblk(DN_H * C)],
        out_specs=(rowblk(W), pl.BlockSpec((ncb, W, 128), lambda i: (i, 0, 0))),
        out_shape=(sd((L, W), F32), sd((nchunks, W, 128), F32)),
        scratch_shapes=[pltpu.VMEM((W, 128), F32)], sem=("arbitrary",), comm=comm2), comm2)
    res = (o, shist, thist, solhist)
    return res if comm is None else (res, (got1 or []) + (got2 or []))


def dn_chunk_bwd(qkv, gcol, grow, bcol, shist, thist, solhist, do, name, comm=None):
    L = qkv.shape[0]
    C, W = DN_C, DN_H * DN_DK
    ncb = 8
    tl = ncb * C
    nchunks = L // C
    nt = L // tl

    def body(q_ref, k_ref, v_ref, gc_ref, gr_ref, b_ref, sh_ref, t_ref, sol_ref, do_ref,
             dqkv_ref, dgc_ref, dgr_ref, db_ref, ds_ref):
        @pl.when(pl.program_id(0) == 0)
        def _():
            ds_ref[...] = jnp.zeros_like(ds_ref)

        lane8 = lax.broadcasted_iota(jnp.int32, (C, 8), 1)
        sub8 = lax.broadcasted_iota(jnp.int32, (8, C), 0)
        rowid = lax.broadcasted_iota(jnp.int32, (C, 1), 0)
        causal, strict, _ = _dn_masks()
        rsum = lambda a: jnp.sum(a, axis=1, keepdims=True)

        def chunk(cc, _):
            c = ncb - 1 - cc
            rows = pl.ds(pl.multiple_of(c * C, C), C)
            grow_c = gr_ref[c]
            hs = lambda h: slice(h * 128, (h + 1) * 128)
            q = [q_ref[rows, hs(h)] for h in _HEADS]
            k = [k_ref[rows, hs(h)] for h in _HEADS]
            v = [v_ref[rows, hs(h)] for h in _HEADS]
            gc = [gc_ref[rows, h:h + 1] for h in _HEADS]
            gr = [grow_c[h:h + 1, :] for h in _HEADS]
            beta = [b_ref[rows, h:h + 1] for h in _HEADS]
            t = [t_ref[rows, h * C:(h + 1) * C] for h in _HEADS]
            sol = [sol_ref[rows, h * 256:(h + 1) * 256] for h in _HEADS]
            S = [sh_ref[c, hs(h), :] for h in _HEADS]
            dS = [ds_ref[hs(h), :] for h in _HEADS]
            dov = [do_ref[rows, hs(h)] for h in _HEADS]
            gam, eg, el, gl = _dn_decay(gc, gr, causal)
            kk = [_mm(k[h], k[h], _NT) for h in _HEADS]
            qk_raw = [_mm(q[h], k[h], _NT) for h in _HEADS]
            w = [sol[h][:, 128:] for h in _HEADS]
            kd = [k[h] * el[h] for h in _HEADS]
            vn = [sol[h][:, :128] - _mm(w[h], S[h]) for h in _HEADS]
            dvn = [_mm(qk_raw[h] * gam[h], dov[h], _TN) + _mm(kd[h], dS[h]) for h in _HEADS]
            dqd = [_mm(dov[h], S[h], _NT) for h in _HEADS]
            dqk = [jnp.where(causal, _mm(dov[h], vn[h], _NT), 0.0) for h in _HEADS]
            dkd = [_mm(vn[h], dS[h], _NT) for h in _HEADS]
            dgl = [jnp.sum(rsum(dS[h] * S[h]), axis=0, keepdims=True) for h in _HEADS]
            dw = [-_mm(dvn[h], S[h], _NT) for h in _HEADS]
            dSn = [dS[h] * gl[h] + _mm(q[h] * eg[h], dov[h], _TN) - _mm(w[h], dvn[h], _TN) for h in _HEADS]
            drhs = [_mm(t[h], jnp.concatenate([dvn[h], dw[h]], axis=1), _TN) for h in _HEADS]
            dn = [jnp.where(strict, -_mm(drhs[h], sol[h], _NT), 0.0) for h in _HEADS]
            dgc_all = jnp.zeros((C, 8), F32)
            db_all = jnp.zeros((C, 8), F32)
            dgr_all = jnp.zeros((8, C), F32)
            for h in _HEADS:
                drv, drk = drhs[h][:, :128], drhs[h][:, 128:]
                t2 = rsum(drk * k[h])
                x = dn[h] * gam[h]
                dbeta = rsum(drv * v[h]) + t2 * eg[h] + rsum(x * kk[h])
                dkk = x * beta[h]
                draw = dqk[h] * gam[h]
                mm_ = (dn[h] * beta[h] * kk[h] + dqk[h] * qk_raw[h]) * gam[h]
                deg = t2 * beta[h] + rsum(dqd[h] * q[h])
                r_ = rsum(dkd[h] * k[h]) * el[h]
                dglast = jnp.sum(r_, axis=0, keepdims=True) + dgl[h] * gl[h]
                dgc = rsum(mm_) + deg * eg[h] - r_ + jnp.where(rowid == C - 1, dglast, 0.0)
                dgr = -jnp.sum(mm_, axis=0, keepdims=True)
                dqkv_ref[rows, hs(h)] = _mm(draw, k[h]) + dqd[h] * eg[h]
                dqkv_ref[rows, hs(DN_H + h)] = (drk * (beta[h] * eg[h]) + _mm(dkk, k[h]) + _mm(dkk, k[h], _TN)
                                                + _mm(draw, q[h], _TN) + dkd[h] * el[h])
                dqkv_ref[rows, hs(2 * DN_H + h)] = drv * beta[h]
                ds_ref[hs(h), :] = dSn[h]
                dgc_all = dgc_all + jnp.where(lane8 == h, dgc, 0.0)
                db_all = db_all + jnp.where(lane8 == h, dbeta, 0.0)
                dgr_all = dgr_all + jnp.where(sub8 == h, dgr, 0.0)
            dgc_ref[rows, :] = dgc_all
            db_ref[rows, :] = db_all
            dgr_ref[c] = dgr_all
            return 0

        lax.fori_loop(0, ncb, chunk, 0)

    rev = lambda i: nt - 1 - i
    col = lambda b: pl.BlockSpec((tl, W), lambda i: (rev(i), b))
    rowblk = lambda w: pl.BlockSpec((tl, w), lambda i: (rev(i), 0))
    small = pl.BlockSpec((tl, 8), lambda i: (rev(i), 0))
    g3 = pl.BlockSpec((ncb, 8, C), lambda i: (rev(i), 0, 0))
    sd = jax.ShapeDtypeStruct
    return _call(body, (qkv, qkv, qkv, gcol, grow, bcol, shist, thist, solhist, do), name=name, grid=(nt,),
                 in_specs=[col(0), col(1), col(2), small, g3, small,
                           pl.BlockSpec((ncb, W, 128), lambda i: (rev(i), 0, 0)), rowblk(DN_H * C),
                           rowblk(DN_H * 256), col(0)],
                 out_specs=(rowblk(3 * W), small, g3, small),
                 out_shape=(sd((L, 3 * W), F32), sd((L, 8), F32), sd((nchunks, 8, C), F32), sd((L, 8), F32)),
                 scratch_shapes=[pltpu.VMEM((W, 128), F32)], sem=("arbitrary",), comm=comm)


def dn_out_fwd(o, rin, nw, name):
    L = o.shape[0]
    tl = _rtile(L, 256)

    def body(o_ref, z_ref, w_ref, y_ref):
        for hd in _HEADS:
            cs = slice(hd * 128, (hd + 1) * 128)
            ov = o_ref[:, cs]
            r = lax.rsqrt(jnp.mean(ov * ov, axis=-1, keepdims=True) + EPS)
            y_ref[:, cs] = (ov * r * w_ref[...] * _silu(z_ref[:, cs])).astype(BF16)

    return pl.pallas_call(
        body, name=name, grid=(L // tl,),
        in_specs=[pl.BlockSpec((tl, DN_W), lambda i: (i, 0)), pl.BlockSpec((tl, DN_W), lambda i: (i, 3)),
                  pl.BlockSpec((1, 128), lambda i: (0, 0))],
        out_specs=pl.BlockSpec((tl, DN_W), lambda i: (i, 0)), out_shape=jax.ShapeDtypeStruct((L, DN_W), BF16),
        compiler_params=_cparams("parallel"))(o, rin, nw)


def dn_out_bwd(dycat, o, rin, nw, name):
    L = o.shape[0]
    tl = _rtile(L, 256)

    def body(dy_ref, o_ref, z_ref, w_ref, do_ref, dz_ref, s_ref):
        @pl.when(pl.program_id(0) == 0)
        def _():
            s_ref[...] = jnp.zeros_like(s_ref)

        for hd in _HEADS:
            cs = slice(hd * 128, (hd + 1) * 128)
            ov, zv, d = o_ref[:, cs], z_ref[:, cs], dy_ref[:, cs]
            r = lax.rsqrt(jnp.mean(ov * ov, axis=-1, keepdims=True) + EPS)
            n = ov * r
            dnw = d * _silu(zv)
            dz_ref[:, cs] = (d * n * w_ref[...] * _dsilu(zv)).astype(BF16)
            dn = dnw * w_ref[...]
            do_ref[:, cs] = r * (dn - n * jnp.mean(dn * n, axis=-1, keepdims=True))
            s_ref[:, cs] += _fold8(dnw * n)

    own = pl.BlockSpec((tl, DN_W), lambda i: (i, 0))
    sd = jax.ShapeDtypeStruct
    return pl.pallas_call(
        body, name=name, grid=(L // tl,),
        in_specs=[own, own, pl.BlockSpec((tl, DN_W), lambda i: (i, 3)), pl.BlockSpec((1, 128), lambda i: (0, 0))],
        out_specs=(own, own, pl.BlockSpec((8, DN_W), lambda i: (0, 0))),
        out_shape=(sd((L, DN_W), F32), sd((L, DN_W), BF16), sd((8, DN_W), F32)),
        compiler_params=_cparams("arbitrary"))(dycat, o, rin, nw)


def dn_gates(a, beta_raw, a_log, dt_bias):
    L = a.shape[0]
    beta = jax.nn.sigmoid(beta_raw)
    g = -jnp.exp(a_log) * jax.nn.softplus(a + dt_bias)
    G = jnp.cumsum(g.reshape(L // DN_C, DN_C, DN_H), axis=1)
    pad = lambda t: jnp.pad(t, ((0, 0), (0, 8 - DN_H)))
    gcol = pad(G.reshape(L, DN_H))
    grow = jnp.pad(jnp.transpose(G, (0, 2, 1)), ((0, 0), (0, 8 - DN_H), (0, 0)))
    return gcol, grow, pad(beta)


def dn_block_fwd(rin, cw, a_log, dt_bias, out_norm, tag, comm=None):
    gates, gates_vjp = jax.vjp(dn_gates, rin[:, REC_A0:REC_A0 + DN_H], rin[:, REC_A0 + DN_H:REC_IN], a_log, dt_bias)
    c0, c12 = (comm[0], comm[1:]) if comm is not None else (None, None)
    qkv, got0 = _with_comm(dn_prep_fwd(rin, cw, tag + "_prep", comm=c0), c0)
    (o, shist, thist, solhist), got = _with_comm(dn_chunk_fwd(qkv, *gates, tag + "_chunk", comm=c12), c12)
    yd = dn_out_fwd(o, rin, out_norm.reshape(1, 128), tag + "_onorm")
    return yd, (qkv, gates, gates_vjp, o, shist, thist, solhist), (got0 or []) + (got or [])


def dn_block_bwd(dyd, res, rin, cw, out_norm, tag, comm=None):
    qkv, gates, gates_vjp, o, shist, thist, solhist = res
    do, dz, nsum = dn_out_bwd(dyd, o, rin, out_norm.reshape(1, 128), tag + "_donorm")
    (dqkv, dgc, dgr, db), got = _with_comm(dn_chunk_bwd(qkv, *gates, shist, thist, solhist, do, tag + "_dchunk",
                                                        comm=comm), comm)
    da, dbraw, g_alog, g_dtb = gates_vjp((dgc, dgr, db))
    dx, csum = dn_prep_bwd(rin, cw, dqkv, tag + "_dprep")
    grads = dict(conv=csum.reshape(4, 8, DN_NT * 128).sum(axis=1), a_log=g_alog, dt_bias=g_dtb,
                 out_norm=nsum.sum(axis=0).reshape(DN_H, 128).sum(axis=0))
    return dx, dz, da, dbraw, grads, got


_HBM = pl.BlockSpec(memory_space=pltpu.HBM)


def _mesh_pos():
    xi, yi, ci = lax.axis_index("x"), lax.axis_index("y"), lax.axis_index("c")
    return xi, yi, ci, 4 * xi + 2 * yi + ci


def _peer(xi, yi, ci, k):
    px = 1 - xi if (k >> 2) & 1 else xi
    py = 1 - yi if (k >> 1) & 1 else yi
    pc = 1 - ci if k & 1 else ci
    return (px, py, pc), 4 * px + 2 * py + pc


def _exchange(xs, gather, name):
    n = len(xs)

    def body(*refs):
        copies = _comm_copies(refs[:n], refs[n:2 * n], *refs[2 * n:], gather)
        for cp in copies:
            cp.start()
        for cp in copies:
            cp.wait()

    return pl.pallas_call(
        body, name=name, in_specs=[_HBM] * n, out_specs=tuple([_HBM] * n),
        out_shape=_comm_out_shapes(xs), scratch_shapes=_comm_sems(n))(*xs)


def _comm_out_shapes(xs):
    return tuple(jax.ShapeDtypeStruct((N_DEV,) + x.shape[-2:], x.dtype) for x in xs)


def _comm_sems(n):
    return [pltpu.SemaphoreType.DMA((n * (N_DEV - 1),)), pltpu.SemaphoreType.DMA((n * (N_DEV - 1),)),
            pltpu.SemaphoreType.DMA((n,))]


def _comm_copies(x_refs, o_refs, send_sems, recv_sems, lsems, gather):
    xi, yi, ci, me = _mesh_pos()
    copies = []
    for t in range(len(x_refs)):
        src_of = (lambda lin, t=t: x_refs[t]) if gather else (lambda lin, t=t: x_refs[t].at[lin])
        copies.append(pltpu.make_async_copy(src_of(me), o_refs[t].at[me], lsems.at[t]))
        for k in range(1, N_DEV):
            peer, lin = _peer(xi, yi, ci, k)
            s = t * (N_DEV - 1) + k - 1
            copies.append(pltpu.make_async_remote_copy(
                src_ref=src_of(lin), dst_ref=o_refs[t].at[me], send_sem=send_sems.at[s],
                recv_sem=recv_sems.at[s], device_id=peer, device_id_type=pl.DeviceIdType.MESH))
    return copies


def _call(body, args, *, name, grid, in_specs, out_specs, out_shape, scratch_shapes=(), sem, comm=None):
    if comm is None:
        return pl.pallas_call(body, name=name, grid=grid, in_specs=in_specs, out_specs=out_specs,
                              out_shape=out_shape, scratch_shapes=list(scratch_shapes),
                              compiler_params=_cparams(*sem))(*args)
    xs, gather = comm
    n = len(xs)
    single = not isinstance(out_shape, (tuple, list))
    outs_shape = (out_shape,) if single else tuple(out_shape)
    outs_specs = (out_specs,) if single else tuple(out_specs)
    n_in, n_out, n_scr = len(in_specs), len(outs_shape), len(scratch_shapes)

    def body2(*refs):
        ins, cx = refs[:n_in], refs[n_in:n_in + n]
        outs = refs[n_in + n:n_in + n + n_out]
        co = refs[n_in + n + n_out:n_in + 2 * n + n_out]
        scr = refs[n_in + 2 * n + n_out:n_in + 2 * n + n_out + n_scr]
        sems = refs[n_in + 2 * n + n_out + n_scr:]
        first = functools.reduce(jnp.logical_and, [pl.program_id(a) == 0 for a in range(len(grid))])
        last = functools.reduce(jnp.logical_and, [pl.program_id(a) == grid[a] - 1 for a in range(len(grid))])

        @pl.when(first)
        def _():
            for cp in _comm_copies(cx, co, *sems, gather):
                cp.start()

        body(*ins, *outs, *scr)

        @pl.when(last)
        def _():
            for cp in _comm_copies(cx, co, *sems, gather):
                cp.wait()

    res = pl.pallas_call(
        body2, name=name, grid=grid, in_specs=list(in_specs) + [_HBM] * n,
        out_specs=outs_specs + tuple([_HBM] * n), out_shape=outs_shape + _comm_out_shapes(xs),
        scratch_shapes=list(scratch_shapes) + _comm_sems(n),
        compiler_params=_cparams(*(["arbitrary"] * len(grid))))(*args, *xs)
    main = res[0] if single else tuple(res[:n_out])
    return main, list(res[n_out:])


def all_gather(x, name):
    return _exchange([x], True, name)[0]


def all_gather_many(xs, name):
    return _exchange(xs, True, name)


def all_to_all_many(xs, name):
    return _exchange(xs, False, name)


def reduce_adamw(gsrc, w, m, v, name, comm=None):
    parts = list(gsrc) if isinstance(gsrc, (list, tuple)) else [gsrc]
    S, R0, C = parts[0].shape
    R = R0 * len(parts)
    tr = _rtile(R0, max(16, min(256, (4 << 20) // (S * C * 4) // 16 * 16)), 16 if R0 % 16 == 0 else 8)
    n0 = R0 // tr
    c1 = 1.0 - ADAM_B1 ** ADAM_STEP
    c2 = 1.0 - ADAM_B2 ** ADAM_STEP

    def body(*refs):
        g_refs = refs[:len(parts)]
        w_ref, m_ref, v_ref, go_ref, d_ref, mo_ref, vo_ref = refs[len(parts):]
        for p, g_ref in enumerate(g_refs):
            @pl.when(pl.program_id(0) // n0 == p)
            def _(g_ref=g_ref):
                acc = g_ref[0].astype(F32)
                for s in range(1, S):
                    acc = acc + g_ref[s].astype(F32)
                go_ref[...] = acc
        g = go_ref[...]
        mn = ADAM_B1 * m_ref[...] + (1.0 - ADAM_B1) * g
        vn = ADAM_B2 * v_ref[...] + (1.0 - ADAM_B2) * (g * g)
        mo_ref[...] = mn
        vo_ref[...] = vn
        d_ref[...] = -ADAM_LR * ((mn / c1) / (jnp.sqrt(vn / c2) + ADAM_EPS) + ADAM_WD * w_ref[...])

    big = pl.BlockSpec((tr, C), lambda i: (i, 0))
    o = jax.ShapeDtypeStruct((R, C), F32)
    part_spec = lambda p: pl.BlockSpec((S, tr, C), lambda i: (0, jnp.clip(i - p * n0, 0, n0 - 1), 0))
    return _call(body, (*parts, w, m, v), name=name, grid=(R // tr,),
                 in_specs=[part_spec(p) for p in range(len(parts))] + [big, big, big],
                 out_specs=(big, big, big, big), out_shape=(o, o, o, o), sem=("parallel",), comm=comm)


def _to_slabs(g, ax):
    shp = g.shape
    g = g.reshape(shp[:ax] + (N_DEV, shp[ax] // N_DEV) + shp[ax + 1:])
    return jnp.moveaxis(g, ax, 0).reshape(N_DEV, -1)


def _from_slabs(s, ax, shp):
    s = s.reshape((N_DEV,) + shp[:ax] + (shp[ax] // N_DEV,) + shp[ax + 1:])
    return jnp.moveaxis(s, 0, ax).reshape(shp)


def _pack_rows(flat, width, row_mult):
    n = flat.shape[-1]
    per = width * row_mult
    tot = -(-n // per) * per
    flat = jnp.pad(flat, [(0, 0)] * (flat.ndim - 1) + [(0, tot - n)])
    return flat.reshape(flat.shape[:-1] + (tot // width, width))


def _offsets(sizes):
    offs, o = [], 0
    for s in sizes:
        offs.append(o)
        o += s
    return offs


WEIGHTS = ['ada_w', 'ada_b', 'norm_mix', 'norm_ffn', 'attn_w_in', 'attn_q_norm_a', 'attn_k_norm_a', 'attn_q_norm_b',
           'attn_k_norm_b', 'attn_sinks', 'attn_w_out', 'rec_w_in', 's5_lambda_re', 's5_lambda_im', 's5_log_dt',
           's5_b_re', 's5_b_im', 's5_c_re', 's5_c_im', 's5_d', 's5_glu_w', 's5_glu_b', 'dn_conv', 'dn_a_log',
           'dn_dt_bias', 'dn_out_norm', 'rec_w_out', 'ffn_w_up', 'ffn_conv', 'ffn_w_down']
BIG = [('attn_w_in', (D, ATTN_IN // N_DEV)), ('attn_w_out', (D // N_DEV, D)), ('rec_w_in', (D // N_DEV, REC_PAD)),
       ('s5_glu_w', (S5_W // N_DEV, S5_W)), ('rec_w_out', (D // N_DEV, D)), ('ffn_w_up', (2 * D, 2 * D_FF // N_DEV)),
       ('ffn_w_down', (2 * D_FF // N_DEV, D))]


def _shard2d(name, t):
    if name == 'rec_w_in':
        return jnp.pad(t[0], ((0, 0), (0, REC_PAD - REC_IN)))
    return t.reshape((-1, t.shape[-1]))


def _cols_to_slabs(g, k=N_DEV):
    r, n = g.shape
    return jnp.transpose(g.reshape(r, k, n // k), (1, 0, 2))


def _slabs_to_cols(s):
    k, r, c_ = s.shape
    return jnp.transpose(s, (1, 0, 2)).reshape(r, k * c_)
SMALL_SHARDED = [('s5_d', 1, (1, S5_W)), ('s5_glu_b', 1, (1, S5_W)), ('dn_conv', 2, (1, 4, 2304)),
                 ('ffn_conv', 2, (2, 3, 2 * D_FF))]
REPLICATED = [('ada_b', (2, 6 * D)), ('norm_mix', (2, D)), ('norm_ffn', (2, D)), ('attn_q_norm_a', (1, HD)),
              ('attn_k_norm_a', (1, HD)), ('attn_q_norm_b', (1, HD)), ('attn_k_norm_b', (1, HD)),
              ('attn_sinks', (1, 8)), ('s5_lambda_re', (1, 16, 64)), ('s5_lambda_im', (1, 16, 64)),
              ('s5_log_dt', (1, 16)), ('s5_b_re', (1, 16, 64, 16)), ('s5_b_im', (1, 16, 64, 16)),
              ('s5_c_re', (1, 16, 16, 64)), ('s5_c_im', (1, 16, 16, 64)), ('dn_a_log', (1, DN_H)),
              ('dn_dt_bias', (1, DN_H)), ('dn_out_norm', (1, 128))]


def _numel(shp):
    return int(np.prod(shp))


def kernel(x, c, ada_w, ada_b, norm_mix, norm_ffn, attn_w_in, attn_q_norm_a, attn_k_norm_a, attn_q_norm_b, attn_k_norm_b, attn_sinks, attn_w_out, rec_w_in, s5_lambda_re, s5_lambda_im, s5_log_dt, s5_b_re, s5_b_im, s5_c_re, s5_c_im, s5_d, s5_glu_w, s5_glu_b, dn_conv, dn_a_log, dn_dt_bias, dn_out_norm, rec_w_out, ffn_w_up, ffn_conv, ffn_w_down, loss_target, m_ada_w, m_ada_b, m_norm_mix, m_norm_ffn, m_attn_w_in, m_attn_q_norm_a, m_attn_k_norm_a, m_attn_q_norm_b, m_attn_k_norm_b, m_attn_sinks, m_attn_w_out, m_rec_w_in, m_s5_lambda_re, m_s5_lambda_im, m_s5_log_dt, m_s5_b_re, m_s5_b_im, m_s5_c_re, m_s5_c_im, m_s5_d, m_s5_glu_w, m_s5_glu_b, m_dn_conv, m_dn_a_log, m_dn_dt_bias, m_dn_out_norm, m_rec_w_out, m_ffn_w_up, m_ffn_conv, m_ffn_w_down, v_ada_w, v_ada_b, v_norm_mix, v_norm_ffn, v_attn_w_in, v_attn_q_norm_a, v_attn_k_norm_a, v_attn_q_norm_b, v_attn_k_norm_b, v_attn_sinks, v_attn_w_out, v_rec_w_in, v_s5_lambda_re, v_s5_lambda_im, v_s5_log_dt, v_s5_b_re, v_s5_b_im, v_s5_c_re, v_s5_c_im, v_s5_d, v_s5_glu_w, v_s5_glu_b, v_dn_conv, v_dn_a_log, v_dn_dt_bias, v_dn_out_norm, v_rec_w_out, v_ffn_w_up, v_ffn_conv, v_ffn_w_down):
    loc = locals()
    W = {n: loc[n] for n in WEIGHTS}
    M = {n: loc["m_" + n] for n in WEIGHTS}
    V = {n: loc["v_" + n] for n in WEIGHTS}
    _, _, _, me = _mesh_pos()
    L = x.shape[1]
    x0, tgt = x[0], loss_target[0]

    small_in = jnp.concatenate([c.reshape(-1)] + [W[n].reshape(-1) for n, _, _ in SMALL_SHARDED])
    si, att_in_all = all_gather_many([_pack_rows(small_in, 1024, 8), attn_w_in[0].astype(BF16)], "gather_first")
    si = si.reshape(N_DEV, -1)
    c_all = si[:, :D]
    off = D
    small_full = {}
    for n, ax, shp in SMALL_SHARDED:
        k = _numel(shp) // N_DEV
        small_full[n] = _from_slabs(si[:, off:off + k], ax, shp)
        off += k

    cond_all = jax.nn.silu(c_all)
    modp = jnp.concatenate([matmul([(cond_all, ada_w[l].astype(BF16))], "nn", f"ada{l}") for l in range(2)], axis=0)
    modp_all = all_gather(modp, "gather_mod")
    mods = []
    for l in range(2):
        row = lax.dynamic_index_in_dim(modp_all, l * N_DEV + me, axis=1, keepdims=False)
        mod = row.reshape(1, 6 * D) + ada_b[l].reshape(1, 6 * D)
        mods.append([mod[:, i * D:(i + 1) * D] for i in range(6)])

    w_att_in = _slabs_to_cols(att_in_all)
    bf = lambda t: t.astype(BF16)
    ffn_shards = [[bf(ffn_w_up[l]), bf(ffn_w_down[l])] for l in range(2)]
    rec_shards = [bf(_shard2d('rec_w_in', rec_w_in)), bf(s5_glu_w[0]), bf(rec_w_out[0])]
    ffn_cw = [small_full['ffn_conv'][l] for l in range(2)]
    dn_cw = small_full['dn_conv'][0]
    s5_dskip, glu_b = small_full['s5_d'], small_full['s5_glu_b']
    row = lambda t: t.reshape(1, -1)

    sh1, sc1, g1, sh2, sc2, g2 = mods[0]
    h1 = gate_norm_fwd(x0, None, None, row(norm_mix[0]), sh1, sc1, "l0_norm1")
    wvec, sinkvec = attn_vectors(attn_q_norm_a[0], attn_k_norm_a[0], attn_q_norm_b[0], attn_k_norm_b[0], attn_sinks[0])
    y0, res_att, got = attention_block_fwd(
        h1, w_att_in, wvec, sinkvec, None, "att",
        comms={'swa': ([ffn_shards[0][0][:D // 2]], True), 1: ([ffn_shards[0][0][D // 2:]], True),
               4: (ffn_shards[0][1:], True), 16: ([bf(attn_w_out[0])], True)})
    w_att_out = got['w_out']
    split_up = lambda up_all: (_slabs_to_cols(up_all[:4]), _slabs_to_cols(up_all[4:]))
    w_up = [split_up(jnp.concatenate([got['swa'][0], got[1][0]], axis=1))]
    w_down = [got[4][0].reshape(D_FF, D)]
    x1, h2 = gate_norm_fwd(x0, y0, g1, row(norm_ffn[0]), sh2, sc2, "l0_norm2")
    f0, res_f0, got_rec = ffn_block_fwd(h2, w_up[0][0], w_up[0][1], ffn_cw[0], w_down[0], "ffn0",
                                        comm=(rec_shards, True))
    w_rec_in = rec_cols_permute(got_rec[0].reshape(D, REC_PAD))
    glu_w, w_rec_out = got_rec[1].reshape(S5_W, S5_W), got_rec[2].reshape(D, D)
    w_rec_out = jnp.concatenate([w_rec_out[S5_W:], w_rec_out[:S5_W]], axis=0)
    t1, tc1, tg1, t2, tc2, tg2 = mods[1]
    x2, h3 = gate_norm_fwd(x1, f0, g2, row(norm_mix[1]), t1, tc1, "l1_norm1")
    rin = matmul([(h3, w_rec_in)], "nn", "rec_in")
    s5p, s5p_vjp = jax.vjp(s5_params, s5_lambda_re[0], s5_lambda_im[0], s5_log_dt[0], s5_b_re[0], s5_b_im[0],
                           s5_c_re[0], s5_c_im[0])
    u = rin[:, REC_U0:REC_A0]
    yc, res_s5 = s5_block_fwd(u, s5p, s5_dskip, glu_w, glu_b, "s5")
    yd, res_dn, got_ffn1 = dn_block_fwd(rin, dn_cw, dn_a_log[0], dn_dt_bias[0], dn_out_norm[0], "dn",
                                        comm=(([ffn_shards[1][1]], True), ([ffn_shards[1][0][:D // 2]], True),
                                              ([ffn_shards[1][0][D // 2:]], True)))
    w_up.append(split_up(jnp.concatenate([got_ffn1[1], got_ffn1[2]], axis=1)))
    w_down.append(got_ffn1[0].reshape(D_FF, D))
    ycat = jnp.concatenate([yd, yc], axis=1)
    y1 = matmul([(ycat, w_rec_out)], "nn", "rec_out")
    x3, h4 = gate_norm_fwd(x2, y1, tg1, row(norm_ffn[1]), t2, tc2, "l1_norm2")
    f1, res_f1, _ = ffn_block_fwd(h4, w_up[1][0], w_up[1][1], ffn_cw[1], w_down[1], "ffn1")
    dx4, df1, lsum = final_loss(x3, f1, tg2, tgt, "loss")

    G = {}
    d_tg2 = lsum[8:16].sum(axis=0)
    dh4, gf1, _ = ffn_block_bwd(df1, res_f1, w_up[1][0], w_up[1][1], ffn_cw[1], w_down[1], "ffn1")
    ffn_slabs = lambda g: [g['w_up'], g['w_down'].reshape(N_DEV, D_FF // N_DEV, D)]
    dx3, dy1, s = gate_norm_bwd(x3, y1, tg1, row(norm_ffn[1]), tc2, dx4, dh4, "l1_dnorm2")
    s = s.reshape(4, 8, D).sum(axis=1)
    d_tg1, d_nffn1, d_t2, d_tc2 = s[0], s[1] * (1.0 + tc2[0]), s[2], s[1] * norm_ffn[1]
    g_rec_out = matmul([(ycat, dy1)], "tn", "rec_out_dw", out_dtype=BF16)
    g_rec_out = jnp.concatenate([g_rec_out[DN_W:], g_rec_out[:DN_W]], axis=0).reshape(N_DEV, D // N_DEV, D)
    dycat = matmul([(dy1, w_rec_out)], "nt", "rec_out_dx")
    du, s5cot, gs5 = s5_block_bwd(dycat, res_s5, s5p, s5_dskip, glu_w, glu_b, "s5", dout_col=DN_W // S5_W)
    s5g = s5p_vjp(s5cot)
    dqkv, dz, da, dbraw, gdn, recv_ffn1 = dn_block_bwd(dycat, res_dn, rin, dn_cw, dn_out_norm[0], "dn",
                                                       comm=(ffn_slabs(gf1), False))
    d_rest = jnp.concatenate([du.astype(BF16), da.astype(BF16), dbraw.astype(BF16),
                              jnp.zeros((L, REC_PAD - REC_IN), BF16)], axis=1)
    drin = ((dqkv, 0), (dz, 3 * DN_W), (d_rest, REC_U0))
    g_rec_in = jnp.concatenate([matmul([(h3, p)], "tn", f"rec_in_dw{i}", out_dtype=BF16)
                                for i, (p, _) in enumerate(drin)], axis=1)
    g_rec_in = rec_cols_restore(g_rec_in).reshape(N_DEV, D // N_DEV, REC_PAD)
    g_glu = gs5['glu_w'].astype(BF16).reshape(N_DEV, S5_W // N_DEV, S5_W)
    dh3 = matmul([(p, w_rec_in[:, c0:c0 + p.shape[1]]) for p, c0 in drin], "nt", "rec_in_dx")
    dx2, df0, s = gate_norm_bwd(x2, f0, g2, row(norm_mix[1]), tc1, dx3, dh3, "l1_dnorm1")
    s = s.reshape(4, 8, D).sum(axis=1)
    d_g2, d_nmix1, d_t1, d_tc1 = s[0], s[1] * (1.0 + tc1[0]), s[2], s[1] * norm_mix[1]
    dh2, gf0, recv_rec = ffn_block_bwd(df0, res_f0, w_up[0][0], w_up[0][1], ffn_cw[0], w_down[0], "ffn0",
                                       comm=([g_rec_in, g_glu, g_rec_out], False))
    dx1, dy0, s = gate_norm_bwd(x1, y0, g1, row(norm_ffn[0]), sc2, dx2, dh2, "l0_dnorm2")
    s = s.reshape(4, 8, D).sum(axis=1)
    d_g1, d_nffn0, d_sh2, d_sc2 = s[0], s[1] * (1.0 + sc2[0]), s[2], s[1] * norm_ffn[0]
    dh1, gatt, got_b = attention_block_bwd(dy0, res_att, w_att_in, wvec, sinkvec, w_att_out, "att",
                                           comms={'swa': ([gf0['w_up'][:, :D // 2]], False),
                                                  16: ([gf0['w_up'][:, D // 2:]], False),
                                                  1: (ffn_slabs(gf0)[1:], False)},
                                           send_w_out_on=4)
    recv_ffn0 = [jnp.concatenate([got_b['swa'][0], got_b[16][0]], axis=1), got_b[1][0]]
    (grad_x, s), recv_w_in = gate_norm_bwd(x0, None, None, row(norm_mix[0]), sc1, dx1, dh1, "l0_dnorm1",
                                           comm=([_cols_to_slabs(gatt['w_in'])], False))
    recv_att = [recv_w_in[0], got_b[4][0]]
    s = s.reshape(4, 8, D).sum(axis=1)
    d_nmix0, d_sh1, d_sc1 = s[1] * (1.0 + sc1[0]), s[2], s[1] * norm_mix[0]
    dmod = jnp.stack([jnp.concatenate([d_sh1, d_sc1, d_g1, d_sh2, d_sc2, d_g2]),
                      jnp.concatenate([d_t1, d_tc1, d_tg1, d_t2, d_tc2, d_tg2])])

    P = {'ada_b': dmod, 'norm_mix': jnp.stack([d_nmix0, d_nmix1]), 'norm_ffn': jnp.stack([d_nffn0, d_nffn1]),
         'attn_q_norm_a': gatt['q_norm_a'], 'attn_k_norm_a': gatt['k_norm_a'], 'attn_q_norm_b': gatt['q_norm_b'],
         'attn_k_norm_b': gatt['k_norm_b'], 'attn_sinks': gatt['sinks'],
         's5_lambda_re': s5g[0], 's5_lambda_im': s5g[1], 's5_log_dt': s5g[2], 's5_b_re': s5g[3], 's5_b_im': s5g[4],
         's5_c_re': s5g[5], 's5_c_im': s5g[6], 'dn_a_log': gdn['a_log'], 'dn_dt_bias': gdn['dt_bias'],
         'dn_out_norm': gdn['out_norm'],
         's5_d': gs5['dskip'], 's5_glu_b': gs5['glu_b'], 'dn_conv': gdn['conv'],
         'ffn_conv': jnp.stack([gf0['conv'], gf1['conv']])}

    out = {k: {} for k in ("g", "d", "m", "v")}
    keys = ("g", "d", "m", "v")
    recv = {'attn_w_in': recv_att[0], 'attn_w_out': recv_att[1], 'rec_w_in': recv_rec[0], 's5_glu_w': recv_rec[1],
            'rec_w_out': recv_rec[2]}
    for n, gr_ in recv.items():
        res4 = reduce_adamw(gr_, _shard2d(n, W[n]), _shard2d(n, M[n]), _shard2d(n, V[n]), "adamw_" + n)
        for key, t in zip(keys, res4):
            out[key][n] = (t[:, :REC_IN] if n == 'rec_w_in' else t).reshape(W[n].shape)
    rep_sizes = [_numel(shp) for _, shp in REPLICATED]
    ss_sizes = [_numel(shp) for _, _, shp in SMALL_SHARDED]
    rep_offs = _offsets(rep_sizes + ss_sizes + [1])
    parts = [P[n].reshape(-1) for n, _ in REPLICATED] + [P[n].reshape(-1) for n, _, _ in SMALL_SHARDED]
    parts.append(lsum[0:8].sum().reshape(1))
    spack = _pack_rows(jnp.concatenate(parts), 1024, 8)
    flat2d = lambda t: t.reshape(-1, t.shape[-1])
    sall = None
    for n, idx in (('ffn_w_up', 0), ('ffn_w_down', 1)):
        comm = ([spack], True) if sall is None else None
        res4, got_s = _with_comm(reduce_adamw([recv_ffn0[idx], recv_ffn1[idx]], flat2d(W[n]), flat2d(M[n]),
                                              flat2d(V[n]), "adamw_" + n, comm=comm), comm)
        if got_s is not None:
            sall = got_s[0]
        for key, t in zip(keys, res4):
            out[key][n] = t.reshape(W[n].shape)
    n_rest = sum(ss_sizes) + 1
    pk = lambda d: _pack_rows(jnp.concatenate([d[n].reshape(-1) for n, _ in REPLICATED]
                                              + [jnp.zeros((n_rest,), F32)]), 1024, 8)
    sg, sd_, sm, sv = [t.reshape(-1) for t in reduce_adamw(sall, pk(W), pk(M), pk(V), "adamw_small")]
    loss = 0.5 * sg[rep_offs[-1]] / D

    dmod_all = sall.reshape(N_DEV, -1)[:, :2 * 6 * D].reshape(N_DEV, 2, 6 * D)
    dmod_mine = lax.dynamic_slice_in_dim(dmod_all, me * (6 * D // N_DEV), 6 * D // N_DEV, axis=2)
    g_ada = [matmul([(cond_all, dmod_mine[:, l])], "tn", f"ada{l}_dw")[None] for l in range(2)]
    ada2d = lambda t: t.reshape(2 * D, 6 * D // N_DEV)
    for key, t in zip(("g", "d", "m", "v"), reduce_adamw(g_ada, ada2d(ada_w), ada2d(m_ada_w),
                                                          ada2d(v_ada_w), "adamw_ada_w")):
        out[key]['ada_w'] = t.reshape(ada_w.shape)
    own = []
    for (n, ax, shp), o in zip(SMALL_SHARDED, rep_offs[len(REPLICATED):]):
        slabs = _to_slabs(sg[o:o + _numel(shp)].reshape(shp), ax)
        own.append(lax.dynamic_index_in_dim(slabs, me, axis=0, keepdims=False))
    own_names = [n for n, _, _ in SMALL_SHARDED]
    pk = lambda d: _pack_rows(jnp.concatenate([d[n].reshape(-1) for n in own_names]), 1024, 8)
    og, od, om, ov = [t.reshape(-1) for t in reduce_adamw(_pack_rows(jnp.concatenate(own), 1024, 8)[None],
                                                          pk(W), pk(M), pk(V), "adamw_own")]

    def unpack(names_shapes, bufs):
        o = 0
        for n, shp in names_shapes:
            k = _numel(shp)
            for key, buf in zip(("g", "d", "m", "v"), bufs):
                out[key][n] = buf[o:o + k].reshape(shp)
            o += k

    unpack(REPLICATED, (sg, sd_, sm, sv))
    unpack([(n, W[n].shape) for n in own_names], (og, od, om, ov))
    return (loss, grad_x[None], *[out["g"][n] for n in WEIGHTS], *[out["d"][n] for n in WEIGHTS],
            *[out["m"][n] for n in WEIGHTS], *[out["v"][n] for n in WEIGHTS])
```

```python
import functools
import math

import numpy as np
import jax
import jax.numpy as jnp
from jax import lax
from jax.experimental import pallas as pl
from jax.experimental.pallas import tpu as pltpu

F32 = jnp.float32
BF16 = jnp.bfloat16

N_DEV = 8
D = 1024
HD = 64
BLK = 128
ATTN_IN = 2304
CB = ATTN_IN // 128
B_BRANCHES = ((128, 1), (512, 4), (2048, 16))
S5_W = 256
S5_P = 1024
DN_H = 6
DN_DK = 128
DN_C = 64
REC_IN = 3340
REC_PAD = 3456
D_FF = 2816
EPS = 1e-6
ADAM_LR, ADAM_B1, ADAM_B2, ADAM_EPS, ADAM_WD, ADAM_STEP = 0.001, 0.9, 0.999, 1e-8, 0.01, 10
VMEM_LIMIT = 48 * 1024 * 1024

ALIBI = np.asarray(2.0 ** (-8.0 * np.arange(1, 17) / 16), dtype=np.float32)


def _cparams(*sem):
    return pltpu.CompilerParams(dimension_semantics=tuple(sem), vmem_limit_bytes=VMEM_LIMIT)


def _tile(n, target):
    if n <= target:
        return n
    best = None
    for t in range(128, target + 1, 128):
        if n % t == 0:
            best = t
    assert best is not None, (n, target)
    return best


def _rtile(n, target, mult=8):
    if n <= target:
        return n
    best = None
    for t in range(mult, target + 1, mult):
        if n % t == 0:
            best = t
    assert best is not None, (n, target)
    return best


def _fold8(x):
    r, c = x.shape
    return x.reshape(r // 8, 8, c).sum(axis=0)


def _sigmoid(x):
    return 1.0 / (1.0 + jnp.exp(-x))


_DIMS = {"nn": (((1,), (0,)), ((), ())), "nt": (((1,), (1,)), ((), ())), "tn": (((0,), (0,)), ((), ()))}


MM_FULL_K = 3584


MM_VMEM_BUDGET = 40 << 20


def matmul(pairs, mode, name, out_dtype=F32, tm=1024, tn=1536, tk=1024):
    a0, b0 = pairs[0]
    if mode == "nn":
        (M, K), N = a0.shape, b0.shape[1]
    elif mode == "nt":
        (M, K), N = a0.shape, b0.shape[0]
    else:
        (K, M), N = a0.shape, b0.shape[1]
        tm = 1536
    tn = _tile(N, tn)
    tk = K if K <= MM_FULL_K else _tile(K, tk)
    nk = K // tk
    npair = len(pairs)
    dims = _DIMS[mode]
    kdim = 0 if mode == "tn" else 1
    tks = [a.shape[kdim] for a, _ in pairs]
    assert all(t == K for t in tks) or (nk == 1 and max(tks) <= MM_FULL_K), tks
    if nk > 1:
        tks = [tk] * npair

    def planned(tm_):
        ab = sum(tm_ * t * a.dtype.itemsize + t * tn * b.dtype.itemsize for (a, b), t in zip(pairs, tks))
        return 2 * ab + 2 * tm_ * tn * jnp.dtype(out_dtype).itemsize + (tm_ * tn * 4 if nk > 1 else 0)

    while True:
        tm_try = _rtile(M, tm) if M % 128 else _tile(M, tm)
        if planned(tm_try) <= MM_VMEM_BUDGET or tm <= 128:
            break
        tm //= 2
    tm = tm_try

    def body(*refs):
        o_ref = refs[2 * npair]
        tot = None
        for p in range(npair):
            part = lax.dot_general(refs[2 * p][...].astype(BF16), refs[2 * p + 1][...].astype(BF16),
                                   dims, preferred_element_type=F32)
            tot = part if tot is None else tot + part
        if nk == 1:
            o_ref[...] = tot.astype(o_ref.dtype)
            return
        acc_ref = refs[2 * npair + 1]
        k = pl.program_id(2)

        @pl.when(k == 0)
        def _():
            acc_ref[...] = tot

        @pl.when(k > 0)
        def _():
            acc_ref[...] += tot

        @pl.when(k == nk - 1)
        def _():
            o_ref[...] = acc_ref[...].astype(o_ref.dtype)

    def specs(t):
        if mode == "nn":
            return [pl.BlockSpec((tm, t), lambda j, i, k: (i, k)), pl.BlockSpec((t, tn), lambda j, i, k: (k, j))]
        if mode == "nt":
            return [pl.BlockSpec((tm, t), lambda j, i, k: (i, k)), pl.BlockSpec((tn, t), lambda j, i, k: (j, k))]
        return [pl.BlockSpec((t, tm), lambda j, i, k: (k, i)), pl.BlockSpec((t, tn), lambda j, i, k: (k, j))]

    flat = [t for pr in pairs for t in pr]
    return pl.pallas_call(
        body, name=name, grid=(N // tn, M // tm, nk),
        in_specs=[s for t in tks for s in specs(t)],
        out_specs=pl.BlockSpec((tm, tn), lambda j, i, k: (i, j)),
        out_shape=jax.ShapeDtypeStruct((M, N), out_dtype),
        scratch_shapes=[pltpu.VMEM((tm, tn), F32)] if nk > 1 else [],
        compiler_params=_cparams("parallel", "parallel", "arbitrary"),
    )(*flat)


def gate_norm_fwd(x, y, gate, nw, sh, sc, name):
    L, C = x.shape
    tl = _rtile(L, 512)
    has_gate = y is not None

    def body(*refs):
        if has_gate:
            x_ref, y_ref, g_ref, nw_ref, sh_ref, sc_ref, xn_ref, h_ref = refs
            xn = x_ref[...] + g_ref[...] * y_ref[...]
            xn_ref[...] = xn
        else:
            x_ref, nw_ref, sh_ref, sc_ref, h_ref = refs
            xn = x_ref[...]
        r = lax.rsqrt(jnp.mean(xn * xn, axis=-1, keepdims=True) + EPS)
        h = (xn * r * nw_ref[...]) * (1.0 + sc_ref[...]) + sh_ref[...]
        h_ref[...] = h.astype(BF16)

    big = pl.BlockSpec((tl, C), lambda i: (i, 0))
    vec = pl.BlockSpec((1, C), lambda i: (0, 0))
    if has_gate:
        ins, in_specs = (x, y, gate, nw, sh, sc), [big, big, vec, vec, vec, vec]
        out_shape = (jax.ShapeDtypeStruct((L, C), F32), jax.ShapeDtypeStruct((L, C), BF16))
        out_specs = (big, big)
    else:
        ins, in_specs = (x, nw, sh, sc), [big, vec, vec, vec]
        out_shape = jax.ShapeDtypeStruct((L, C), BF16)
        out_specs = big
    return pl.pallas_call(body, name=name, grid=(L // tl,), in_specs=in_specs, out_specs=out_specs,
                          out_shape=out_shape, compiler_params=_cparams("parallel"))(*ins)


def gate_norm_bwd(xn, y, gate, nw, sc, dxn_direct, dh, name, comm=None):
    L, C = xn.shape
    tl = _rtile(L, 256)
    has_gate = y is not None
    has_direct = dxn_direct is not None

    def body(*refs):
        refs = list(refs)
        xn_ref = refs.pop(0)
        y_ref = refs.pop(0) if has_gate else None
        g_ref = refs.pop(0) if has_gate else None
        nw_ref = refs.pop(0)
        sc_ref = refs.pop(0)
        dd_ref = refs.pop(0) if has_direct else None
        dh_ref = refs.pop(0)
        dxn_ref = refs.pop(0)
        dy_ref = refs.pop(0) if has_gate else None
        sums_ref = refs.pop(0)

        @pl.when(pl.program_id(0) == 0)
        def _():
            sums_ref[...] = jnp.zeros_like(sums_ref)

        xv = xn_ref[...]
        dh_v = dh_ref[...]
        r = lax.rsqrt(jnp.mean(xv * xv, axis=-1, keepdims=True) + EPS)
        n = xv * r
        a = nw_ref[...] * (1.0 + sc_ref[...])
        dn = dh_v * a
        dx = r * (dn - n * jnp.mean(dn * n, axis=-1, keepdims=True))
        if has_direct:
            dx = dx + dd_ref[...]
        dxn_ref[...] = dx
        sums_ref[8:16, :] += _fold8(dh_v * n)
        sums_ref[16:24, :] += _fold8(dh_v)
        if has_gate:
            dy_ref[...] = (dx * g_ref[...]).astype(BF16)
            sums_ref[0:8, :] += _fold8(dx * y_ref[...])

    big = pl.BlockSpec((tl, C), lambda i: (i, 0))
    vec = pl.BlockSpec((1, C), lambda i: (0, 0))
    ins, in_specs = [xn], [big]
    if has_gate:
        ins += [y, gate]
        in_specs += [big, vec]
    ins += [nw, sc]
    in_specs += [vec, vec]
    if has_direct:
        ins.append(dxn_direct)
        in_specs.append(big)
    ins.append(dh)
    in_specs.append(big)
    out_shape = [jax.ShapeDtypeStruct((L, C), F32)]
    out_specs = [big]
    if has_gate:
        out_shape.append(jax.ShapeDtypeStruct((L, C), BF16))
        out_specs.append(big)
    out_shape.append(jax.ShapeDtypeStruct((32, C), F32))
    out_specs.append(pl.BlockSpec((32, C), lambda i: (0, 0)))
    return _call(body, ins, name=name, grid=(L // tl,), in_specs=in_specs, out_specs=tuple(out_specs),
                 out_shape=tuple(out_shape), sem=("arbitrary",), comm=comm)


def final_loss(x, f, gate, target, name):
    L, C = x.shape
    tl = _rtile(L, 256)

    def body(x_ref, f_ref, g_ref, t_ref, dy_ref, df_ref, sums_ref):
        @pl.when(pl.program_id(0) == 0)
        def _():
            sums_ref[...] = jnp.zeros_like(sums_ref)

        fv = f_ref[...]
        err = x_ref[...] + g_ref[...] * fv - t_ref[...]
        dy = err * (1.0 / C)
        dy_ref[...] = dy
        df_ref[...] = (dy * g_ref[...]).astype(BF16)
        sums_ref[0:8, :] += _fold8(err * err)
        sums_ref[8:16, :] += _fold8(dy * fv)

    big = pl.BlockSpec((tl, C), lambda i: (i, 0))
    vec = pl.BlockSpec((1, C), lambda i: (0, 0))
    return pl.pallas_call(
        body, name=name, grid=(L // tl,), in_specs=[big, big, vec, big],
        out_specs=(big, big, pl.BlockSpec((16, C), lambda i: (0, 0))),
        out_shape=(jax.ShapeDtypeStruct((L, C), F32), jax.ShapeDtypeStruct((L, C), BF16),
                   jax.ShapeDtypeStruct((16, C), F32)),
        compiler_params=_cparams("arbitrary"))(x, f, gate, target)


def _seg_ones(seg):
    r = lax.broadcasted_iota(jnp.int32, (128, 128), 0) // seg
    c = lax.broadcasted_iota(jnp.int32, (128, 128), 1) // seg
    return (r == c).astype(BF16)


def _segsum(t, ones):
    hi = t.astype(BF16)
    lo = (t - hi.astype(F32)).astype(BF16)
    return (jnp.dot(hi, ones, preferred_element_type=F32) + jnp.dot(lo, ones, preferred_element_type=F32))


_NORMED_TILES = tuple(list(range(0, 5)) + list(range(6, 14)))


DIL = (4, 16)
B_COLS0, B_W = 768, 1536
DIL_TL = 256


def _to_dilated(scr_ref, out_ref, d, cast=None):
    nj, tl, _ = scr_ref.shape
    for r in range(d):
        for j in range(nj):
            piece = scr_ref[j, pl.ds(r, tl // d, stride=d), :]
            c0 = (r * nj + j) * 128
            out_ref[:, c0:c0 + 128] = piece if cast is None else piece.astype(cast)


def _from_dilated(in_ref, scr_ref, d):
    nj, tl, _ = scr_ref.shape
    for r in range(d):
        for j in range(nj):
            c0 = (r * nj + j) * 128
            scr_ref[j, pl.ds(r, tl // d, stride=d), :] = in_ref[:, c0:c0 + 128]


def _dil_spec(tl, d, width):
    return pl.BlockSpec((tl // d, d * width), lambda i: (i, 0))


def qknorm_fwd(qkv, wvec, name):
    L, C = qkv.shape
    tl = DIL_TL

    def body(x_ref, w_ref, o_ref, o4_ref, o16_ref, scr_ref):
        ones = _seg_ones(HD)
        for t in range(CB):
            cs = slice(t * 128, (t + 1) * 128)
            x = x_ref[:, cs]
            if t in _NORMED_TILES:
                ms = _segsum(x * x, ones) * (1.0 / HD)
                x = x * lax.rsqrt(ms + EPS) * w_ref[:, cs]
            o_ref[:, cs] = x.astype(BF16)
            if t * 128 >= B_COLS0:
                scr_ref[t - B_COLS0 // 128] = x
        _to_dilated(scr_ref, o4_ref, 4, BF16)
        _to_dilated(scr_ref, o16_ref, 16, BF16)

    return pl.pallas_call(
        body, name=name, grid=(L // tl,),
        in_specs=[pl.BlockSpec((tl, C), lambda i: (i, 0)), pl.BlockSpec((1, C), lambda i: (0, 0))],
        out_specs=(pl.BlockSpec((tl, C), lambda i: (i, 0)), _dil_spec(tl, 4, B_W), _dil_spec(tl, 16, B_W)),
        out_shape=(jax.ShapeDtypeStruct((L, C), BF16), jax.ShapeDtypeStruct((L // 4, 4 * B_W), BF16),
                   jax.ShapeDtypeStruct((L // 16, 16 * B_W), BF16)),
        scratch_shapes=[pltpu.VMEM((B_W // 128, tl, 128), F32)], compiler_params=_cparams("parallel"))(qkv, wvec)


def qknorm_bwd(qkv, wvec, d_a, d_b, name):
    L, C = qkv.shape
    tl = DIL_TL

    def body(x_ref, w_ref, dqa, dka, dva, q1, k1, v1, q4, k4, v4, q16, k16, v16, dx_ref, sums_ref,
             dy_ref, s4_ref, s16_ref):
        @pl.when(pl.program_id(0) == 0)
        def _():
            sums_ref[...] = jnp.zeros_like(sums_ref)

        dy_ref[:, 0:512] = dqa[...]
        for off, ref in ((512, dka), (640, dva)):
            for g in range(2):
                acc = ref[:, g * 256:g * 256 + HD]
                for h in range(1, 4):
                    acc = acc + ref[:, g * 256 + h * HD:g * 256 + (h + 1) * HD]
                dy_ref[:, off + g * HD:off + (g + 1) * HD] = acc
        for off, r1, r4, r16 in ((768, q1, q4, q16), (1280, k1, k4, k16), (1792, v1, v4, v16)):
            _from_dilated(r4, s4_ref, 4)
            _from_dilated(r16, s16_ref, 16)
            for j in range(4):
                dy_ref[:, off + j * 128:off + (j + 1) * 128] = r1[:, j * 128:(j + 1) * 128] + s4_ref[j] + s16_ref[j]

        ones = _seg_ones(HD)
        for t in range(CB):
            cs = slice(t * 128, (t + 1) * 128)
            d = dy_ref[:, cs]
            if t in _NORMED_TILES:
                x = x_ref[:, cs]
                r = lax.rsqrt(_segsum(x * x, ones) * (1.0 / HD) + EPS)
                n = x * r
                dn = d * w_ref[:, cs]
                dx_ref[:, cs] = (r * (dn - n * (_segsum(dn * n, ones) * (1.0 / HD)))).astype(BF16)
                sums_ref[:, cs] += _fold8(d * n)
            else:
                dx_ref[:, cs] = d.astype(BF16)

    big = pl.BlockSpec((tl, C), lambda i: (i, 0))
    p512 = pl.BlockSpec((tl, 512), lambda i: (i, 0))
    return pl.pallas_call(
        body, name=name, grid=(L // tl,),
        in_specs=[big, pl.BlockSpec((1, C), lambda i: (0, 0))] + [p512] * 6 + [_dil_spec(tl, 4, 512)] * 3
        + [_dil_spec(tl, 16, 512)] * 3,
        out_specs=(big, pl.BlockSpec((8, C), lambda i: (0, 0))),
        out_shape=(jax.ShapeDtypeStruct((L, C), BF16), jax.ShapeDtypeStruct((8, C), F32)),
        scratch_shapes=[pltpu.VMEM((tl, C), F32), pltpu.VMEM((4, tl, 128), F32), pltpu.VMEM((4, tl, 128), F32)],
        compiler_params=_cparams("arbitrary"))(qkv, wvec, *d_a, *d_b[0], *d_b[1], *d_b[2])


def _attn_biases(t, slopes, step, maxdist):
    qi = lax.broadcasted_iota(jnp.int32, (BLK, 2 * BLK), 0)
    sj = lax.broadcasted_iota(jnp.int32, (BLK, 2 * BLK), 1)
    dist = BLK + qi - sj
    valid = (dist >= 0) & (dist <= maxdist)
    distf = (step * dist).astype(F32)
    inner = [jnp.where(valid, (-sl) * distf, -jnp.inf) for sl in slopes]
    first = [jnp.where((t > 0) | (sj >= BLK), b, -jnp.inf) for b in inner]
    return inner, first


def _attn_scores(q, kw, bias):
    return lax.dot_general(q, kw, (((1,), (1,)), ((), ())), preferred_element_type=F32) + bias


ATT_NQ_BWD = 16
ATT_NQ = 8


def _attn_operands(nq, hp, gqa, q_ref, kh_ref, kc_ref, vh_ref, vc_ref):
    ops = []
    for b in range(nq):
        rows = slice(b * BLK, (b + 1) * BLK)
        prev = slice((b - 1) * BLK, b * BLK)
        for e in range(2):
            cs = slice(e * HD, (e + 1) * HD)
            if gqa:
                ksel = lambda ref, r: jnp.where(hp >= 2, ref[r, 64:128], ref[r, 0:64])
            else:
                ksel = lambda ref, r, cs=cs: ref[r, cs]
            kprev = ksel(kh_ref, slice(0, BLK)) if b == 0 else ksel(kc_ref, prev)
            vprev = ksel(vh_ref, slice(0, BLK)) if b == 0 else ksel(vc_ref, prev)
            ops.append((b, e, rows, cs, q_ref[rows, cs] * (HD ** -0.5),
                        jnp.concatenate([kprev, ksel(kc_ref, rows)], axis=0),
                        jnp.concatenate([vprev, ksel(vc_ref, rows)], axis=0)))
    return ops


def _attn_specs(cb, q_off, k_off, v_off, gqa):
    kcol = (lambda r, hp: r * cb + k_off) if gqa else (lambda r, hp: r * cb + k_off + hp)
    vcol = (lambda r, hp: r * cb + v_off) if gqa else (lambda r, hp: r * cb + v_off + hp)
    return kcol, vcol


def attn_fwd(X, d, q_off, k_off, v_off, gqa, slope0, maxdist, name, comm=None):
    Ls = X.shape[0]
    nq = min(ATT_NQ, Ls // BLK)
    TQ = nq * BLK
    nt = Ls // TQ
    slopes = jnp.asarray(ALIBI)

    def body(sl_ref, q_ref, kh_ref, kc_ref, vh_ref, vc_ref, o_ref, lse_ref):
        hp, t = pl.program_id(1), pl.program_id(2)
        ops = _attn_operands(nq, hp, gqa, q_ref, kh_ref, kc_ref, vh_ref, vc_ref)
        inner, first = _attn_biases(t, [sl_ref[slope0 + 2 * hp + e] for e in range(2)], d, maxdist)
        s = [_attn_scores(q, kw, first[e] if b == 0 else inner[e]) for (b, e, rows, cs, q, kw, vw) in ops]
        m = [jnp.max(x, axis=-1, keepdims=True) for x in s]
        p = [jnp.exp(x - mm) for x, mm in zip(s, m)]
        l = [jnp.sum(x, axis=-1, keepdims=True) for x in p]
        o = [jnp.dot(x.astype(BF16), op[6], preferred_element_type=F32) / ll for x, op, ll in zip(p, ops, l)]
        for (b, e, rows, cs, q, kw, vw), oo, mm, ll in zip(ops, o, m, l):
            o_ref[rows, cs] = oo
            lse_ref[rows, cs] = jnp.broadcast_to(mm + jnp.log(ll), (BLK, HD))

    cb = X.shape[1] // (d * 128)
    kcol, vcol = _attn_specs(cb, q_off, k_off, v_off, gqa)
    tile, blk = (TQ, 128), (BLK, 128)
    halo = lambda t: jnp.maximum(t * nq - 1, 0)
    in_specs = [
        pl.BlockSpec(memory_space=pltpu.SMEM),
        pl.BlockSpec(tile, lambda r, hp, t: (t, r * cb + q_off + hp)),
        pl.BlockSpec(blk, lambda r, hp, t: (halo(t), kcol(r, hp))),
        pl.BlockSpec(tile, lambda r, hp, t: (t, kcol(r, hp))),
        pl.BlockSpec(blk, lambda r, hp, t: (halo(t), vcol(r, hp))),
        pl.BlockSpec(tile, lambda r, hp, t: (t, vcol(r, hp))),
    ]
    out_spec = pl.BlockSpec(tile, lambda r, hp, t: (t, r * 4 + hp))
    out = jax.ShapeDtypeStruct((Ls, d * 512), F32)
    return _call(body, (slopes, X, X, X, X, X), name=name, grid=(d, 4, nt), in_specs=in_specs,
                 out_specs=(out_spec, out_spec), out_shape=(out, out),
                 sem=("parallel", "parallel", "arbitrary"), comm=comm)


def attn_bwd(X, o, lse, do, dlse, d, q_off, k_off, v_off, gqa, slope0, maxdist, name, comm=None):
    Ls = X.shape[0]
    slopes = jnp.asarray(ALIBI)

    nq = min(ATT_NQ_BWD, Ls // BLK)
    TQ = nq * BLK
    nt = Ls // TQ
    nt_dims, tn_dims = (((1,), (1,)), ((), ())), (((0,), (0,)), ((), ()))

    def body(sl_ref, q_ref, kh_ref, kc_ref, vh_ref, vc_ref, o_ref, lse_ref, do_ref, dlse_ref,
             dq_ref, dk_ref, dv_ref, ak_ref, av_ref, pk_ref, pv_ref):
        hp, t = pl.program_id(1), pl.program_id(2)

        @pl.when(t == 0)
        def _():
            pk_ref[...] = jnp.zeros_like(pk_ref)
            pv_ref[...] = jnp.zeros_like(pv_ref)

        @pl.when(t < nt)
        def _():
            ops = _attn_operands(nq, hp, gqa, q_ref, kh_ref, kc_ref, vh_ref, vc_ref)
            inner, first = _attn_biases(t, [sl_ref[slope0 + 2 * hp + e] for e in range(2)], d, maxdist)
            sv = [_attn_scores(q, kw, first[e] if b == 0 else inner[e]) for (b, e, rows, cs, q, kw, vw) in ops]
            p = [jnp.exp(s - lse_ref[op[2], op[1] * HD:op[1] * HD + 1]) for s, op in zip(sv, ops)]
            dov = [do_ref[op[2], op[3]] for op in ops]
            delta = [jnp.sum(dd * o_ref[op[2], op[3]], axis=-1, keepdims=True) for dd, op in zip(dov, ops)]
            dob = [dd.astype(BF16) for dd in dov]
            dp = [lax.dot_general(dd, op[6], nt_dims, preferred_element_type=F32) for dd, op in zip(dob, ops)]
            ds = [(pp * (x - dl + dlse_ref[op[2], op[1] * HD:op[1] * HD + 1])).astype(BF16)
                  for pp, x, dl, op in zip(p, dp, delta, ops)]
            dq = [jnp.dot(x, op[5], preferred_element_type=F32) * (HD ** -0.5) for x, op in zip(ds, ops)]
            dkw = [lax.dot_general(x, op[4], tn_dims, preferred_element_type=F32) for x, op in zip(ds, ops)]
            dvw = [lax.dot_general(pp.astype(BF16), dd, tn_dims, preferred_element_type=F32)
                   for pp, dd in zip(p, dob)]
            ak_ref[...] = jnp.zeros_like(ak_ref)
            av_ref[...] = jnp.zeros_like(av_ref)
            for (b, e, rows, cs, q, kw, vw), x, yk, yv in zip(ops, dq, dkw, dvw):
                dq_ref[rows, cs] = x
                ak_ref[b * BLK:(b + 2) * BLK, cs] += yk
                av_ref[b * BLK:(b + 2) * BLK, cs] += yv
            if nt == 1:
                dk_ref[...] = ak_ref[BLK:, :]
                dv_ref[...] = av_ref[BLK:, :]
                return
            last = slice(TQ - BLK, TQ)
            dk_ref[...] = pk_ref[...]
            dv_ref[...] = pv_ref[...]
            dk_ref[last, :] += ak_ref[0:BLK, :]
            dv_ref[last, :] += av_ref[0:BLK, :]
            pk_ref[...] = ak_ref[BLK:, :]
            pv_ref[...] = av_ref[BLK:, :]

        @pl.when(t == nt)
        def _():
            dk_ref[...] = pk_ref[...]
            dv_ref[...] = pv_ref[...]

    cb = X.shape[1] // (d * 128)
    kcol, vcol = _attn_specs(cb, q_off, k_off, v_off, gqa)
    tile, blk = (TQ, 128), (BLK, 128)
    cur = lambda t: jnp.minimum(t, nt - 1)
    halo = lambda t: jnp.maximum(cur(t) * nq - 1, 0)
    ospec = pl.BlockSpec(tile, lambda r, hp, t: (cur(t), r * 4 + hp))
    in_specs = [
        pl.BlockSpec(memory_space=pltpu.SMEM),
        pl.BlockSpec(tile, lambda r, hp, t: (cur(t), r * cb + q_off + hp)),
        pl.BlockSpec(blk, lambda r, hp, t: (halo(t), kcol(r, hp))),
        pl.BlockSpec(tile, lambda r, hp, t: (cur(t), kcol(r, hp))),
        pl.BlockSpec(blk, lambda r, hp, t: (halo(t), vcol(r, hp))),
        pl.BlockSpec(tile, lambda r, hp, t: (cur(t), vcol(r, hp))),
        ospec, ospec, ospec, ospec,
    ]
    shifted = pl.BlockSpec(tile, lambda r, hp, t: (jnp.maximum(t - 1, 0), r * 4 + hp))
    out = jax.ShapeDtypeStruct((Ls, d * 512), F32)
    return _call(body, (slopes, X, X, X, X, X, o, lse, do, dlse), name=name, grid=(d, 4, nt + 1 if nt > 1 else 1),
                 in_specs=in_specs, out_specs=(ospec, shifted, shifted), out_shape=(out, out, out),
                 scratch_shapes=[pltpu.VMEM((TQ + BLK, 128), F32), pltpu.VMEM((TQ + BLK, 128), F32),
                                 pltpu.VMEM((TQ, 128), F32), pltpu.VMEM((TQ, 128), F32)],
                 sem=("parallel", "parallel", "arbitrary"), comm=comm)


def attn_merge_fwd(oa, la, sink, obs, lbs, name):
    L = oa.shape[0]
    tl = DIL_TL

    def body(oa_ref, la_ref, sk_ref, o1, o4, o16, l1, l4, l16, m_ref, so4, so16, sl4, sl16):
        m_ref[:, 0:512] = (oa_ref[...] * _sigmoid(la_ref[...] - sk_ref[...])).astype(BF16)
        for src, dst, d in ((o4, so4, 4), (o16, so16, 16), (l4, sl4, 4), (l16, sl16, 16)):
            _from_dilated(src, dst, d)
        for j in range(4):
            cs = slice(j * 128, (j + 1) * 128)
            a, b, c = l1[:, cs], sl4[j], sl16[j]
            mx = jnp.maximum(jnp.maximum(a, b), c)
            ea, eb, ec = jnp.exp(a - mx), jnp.exp(b - mx), jnp.exp(c - mx)
            inv = 1.0 / (ea + eb + ec)
            m_ref[:, 512 + j * 128:512 + (j + 1) * 128] = (
                (ea * inv) * o1[:, cs] + (eb * inv) * so4[j] + (ec * inv) * so16[j]).astype(BF16)

    big = pl.BlockSpec((tl, 512), lambda i: (i, 0))
    dil = [big, _dil_spec(tl, 4, 512), _dil_spec(tl, 16, 512)]
    return pl.pallas_call(
        body, name=name, grid=(L // tl,),
        in_specs=[big, big, pl.BlockSpec((1, 512), lambda i: (0, 0))] + dil + dil,
        out_specs=pl.BlockSpec((tl, 1024), lambda i: (i, 0)),
        out_shape=jax.ShapeDtypeStruct((L, 1024), BF16), scratch_shapes=[pltpu.VMEM((4, tl, 128), F32)] * 4,
        compiler_params=_cparams("parallel"),
    )(oa, la, sink, *obs, *lbs)


def attn_merge_bwd(dm, oa, la, sink, obs, lbs, name):
    L = oa.shape[0]
    tl = DIL_TL

    def body(dm_ref, oa_ref, la_ref, sk_ref, o1, o4, o16, l1, l4, l16,
             doa_ref, dla_ref, d1, d4, d16, g1, g4, g16, sums_ref, so4, so16, sl4, sl16, sd4, sd16, sg4, sg16):
        @pl.when(pl.program_id(0) == 0)
        def _():
            sums_ref[...] = jnp.zeros_like(sums_ref)

        for src, dst, d in ((o4, so4, 4), (o16, so16, 16), (l4, sl4, 4), (l16, sl16, 16)):
            _from_dilated(src, dst, d)
        ones = _seg_ones(HD)
        for t in range(4):
            cs = slice(t * 128, (t + 1) * 128)
            dma = dm_ref[:, cs]
            keep = _sigmoid(la_ref[:, cs] - sk_ref[:, cs])
            doa_ref[:, cs] = dma * keep
            tt = dma * oa_ref[:, cs] * keep * (1.0 - keep)
            dla_ref[:, cs] = _segsum(tt, ones)
            sums_ref[:, cs] += _fold8(-tt)
            dmb = dm_ref[:, 512 + t * 128:512 + (t + 1) * 128]
            a, b, c = l1[:, cs], sl4[t], sl16[t]
            mx = jnp.maximum(jnp.maximum(a, b), c)
            ea, eb, ec = jnp.exp(a - mx), jnp.exp(b - mx), jnp.exp(c - mx)
            inv = 1.0 / (ea + eb + ec)
            wa, wb, wc = ea * inv, eb * inv, ec * inv
            d1[:, cs] = wa * dmb
            sd4[t] = wb * dmb
            sd16[t] = wc * dmb
            sa = _segsum(dmb * o1[:, cs], ones)
            sb = _segsum(dmb * so4[t], ones)
            sc_ = _segsum(dmb * so16[t], ones)
            mean = wa * sa + wb * sb + wc * sc_
            g1[:, cs] = wa * (sa - mean)
            sg4[t] = wb * (sb - mean)
            sg16[t] = wc * (sc_ - mean)
        for src, dst, d in ((sd4, d4, 4), (sd16, d16, 16), (sg4, g4, 4), (sg16, g16, 16)):
            _to_dilated(src, dst, d)

    big = pl.BlockSpec((tl, 512), lambda i: (i, 0))
    dil = [big, _dil_spec(tl, 4, 512), _dil_spec(tl, 16, 512)]
    sd = jax.ShapeDtypeStruct
    shp = [sd((L, 512), F32), sd((L // 4, 4 * 512), F32), sd((L // 16, 16 * 512), F32)]
    return pl.pallas_call(
        body, name=name, grid=(L // tl,),
        in_specs=[pl.BlockSpec((tl, 1024), lambda i: (i, 0)), big, big,
                  pl.BlockSpec((1, 512), lambda i: (0, 0))] + dil + dil,
        out_specs=tuple([big, big] + dil + dil + [pl.BlockSpec((8, 512), lambda i: (0, 0))]),
        out_shape=tuple([shp[0], shp[0]] + shp + shp + [sd((8, 512), F32)]),
        scratch_shapes=[pltpu.VMEM((4, tl, 128), F32)] * 8, compiler_params=_cparams("arbitrary"),
    )(dm, oa, la, sink, *obs, *lbs)


def _shift_down(x, halo, k, first):
    rows = lax.broadcasted_iota(jnp.int32, (8, x.shape[1]), 0)
    out = pltpu.roll(x, k, axis=0)
    hrows = jnp.where(first, 0.0, pltpu.roll(halo, k, axis=0))
    top = jnp.where(rows < k, hrows, out[0:8, :])
    return jnp.concatenate([top, out[8:, :]], axis=0)


def _shift_up(x, nxt, k):
    tl = x.shape[0]
    rows = lax.broadcasted_iota(jnp.int32, (8, x.shape[1]), 0)
    out = pltpu.roll(x, tl - k, axis=0)
    bottom = jnp.where(rows >= 8 - k, pltpu.roll(nxt, 8 - k, axis=0), out[tl - 8:, :])
    return jnp.concatenate([out[:tl - 8, :], bottom], axis=0)


def _silu(x):
    return x * _sigmoid(x)


def _dsilu(x):
    s = _sigmoid(x)
    return s * (1.0 + x * (1.0 - s))


def ffn_act_fwd(ua, ub, cw, name, comm=None):
    L, F = ua.shape
    tl = _rtile(L, 256)
    tc = _tile(F, 1408)
    hb = tl // 8

    def body(ua_ref, uah_ref, ub_ref, ubh_ref, wa_ref, wb_ref, o_ref, ac_ref, bc_ref):
        first = pl.program_id(1) == 0

        def conv(x_ref, h_ref, w_ref):
            x = x_ref[...]
            h = h_ref[...]
            return (w_ref[2:3, :] * x + w_ref[1:2, :] * _shift_down(x, h, 1, first)
                    + w_ref[0:1, :] * _shift_down(x, h, 2, first))

        a = conv(ua_ref, uah_ref, wa_ref)
        b = conv(ub_ref, ubh_ref, wb_ref)
        ac_ref[...] = a
        bc_ref[...] = b
        o_ref[...] = (_silu(a) * b).astype(BF16)

    main = pl.BlockSpec((tl, tc), lambda j, i: (i, j))
    halo = pl.BlockSpec((8, tc), lambda j, i: (jnp.maximum(i * hb - 1, 0), j))
    wa = pl.BlockSpec((3, tc), lambda j, i: (0, j))
    wb = pl.BlockSpec((3, tc), lambda j, i: (0, j + F // tc))
    f32 = jax.ShapeDtypeStruct((L, F), F32)
    return _call(body, (ua, ua, ub, ub, cw, cw), name=name, grid=(F // tc, L // tl),
                 in_specs=[main, halo, main, halo, wa, wb], out_specs=(main, main, main),
                 out_shape=(jax.ShapeDtypeStruct((L, F), BF16), f32, f32), sem=("parallel", "parallel"), comm=comm)


def ffn_act_bwd(ua, ub, ac, bc, cw, dact, name, comm=None):
    L, F = ua.shape
    tl = _rtile(L, 256)
    tc = _tile(F, 1408)
    nrt = L // tl

    def body(ua_ref, ub_ref, ac_ref, bc_ref, wa_ref, wb_ref, da_ref, dua_ref, dub_ref, sums_ref, ca_ref, cb_ref):
        i = pl.program_id(1)

        @pl.when(i == 0)
        def _():
            sums_ref[...] = jnp.zeros_like(sums_ref)
            ca_ref[...] = jnp.zeros_like(ca_ref)
            cb_ref[...] = jnp.zeros_like(cb_ref)

        a, b = ac_ref[...], bc_ref[...]
        dact_v = da_ref[...]
        dya = dact_v * b * _dsilu(a)
        dyb = dact_v * _silu(a)
        for (dy, w_ref, c_ref, d_ref, x_ref, base) in ((dya, wa_ref, ca_ref, dua_ref, ua_ref, 0),
                                                        (dyb, wb_ref, cb_ref, dub_ref, ub_ref, 24)):
            nxt = c_ref[...]
            ups = (dy, _shift_up(dy, nxt, 1), _shift_up(dy, nxt, 2))
            d_ref[...] = (w_ref[2:3, :] * ups[0] + w_ref[1:2, :] * ups[1] + w_ref[0:1, :] * ups[2]).astype(BF16)
            c_ref[...] = dy[0:8, :]
            x = x_ref[...]
            for k in range(3):
                sums_ref[base + 8 * (2 - k):base + 8 * (2 - k) + 8, :] += _fold8(ups[k] * x)

    rev = lambda i: nrt - 1 - i
    main = pl.BlockSpec((tl, tc), lambda j, i: (rev(i), j))
    wa = pl.BlockSpec((3, tc), lambda j, i: (0, j))
    wb = pl.BlockSpec((3, tc), lambda j, i: (0, j + F // tc))
    ob = jax.ShapeDtypeStruct((L, F), BF16)
    return _call(body, (ua, ub, ac, bc, cw, cw, dact), name=name, grid=(F // tc, nrt),
                 in_specs=[main, main, main, main, wa, wb, main],
                 out_specs=(main, main, pl.BlockSpec((48, tc), lambda j, i: (0, j))),
                 out_shape=(ob, ob, jax.ShapeDtypeStruct((48, F), F32)),
                 scratch_shapes=[pltpu.VMEM((8, tc), F32), pltpu.VMEM((8, tc), F32)],
                 sem=("parallel", "arbitrary"), comm=comm)


def attn_vectors(qna, kna, qnb, knb, sinks):
    ones = jnp.ones((128,), F32)
    wvec = jnp.concatenate([jnp.tile(qna, 8), jnp.tile(kna, 2), ones, jnp.tile(qnb, 8), jnp.tile(knb, 8),
                            jnp.tile(ones, 4)]).reshape(1, ATTN_IN)
    return wvec, jnp.repeat(sinks, HD).reshape(1, 512)


def _with_comm(result, comm):
    return result if comm is not None else (result, None)


def attention_block_fwd(h, w_in, wvec, sinkvec, w_out, tag, comms=None):
    L = h.shape[0]
    comms = comms or {}
    got = {}
    qkv = matmul([(h, w_in)], "nn", tag + "_qkv")
    X, X4, X16 = qknorm_fwd(qkv, wvec, tag + "_qknorm")
    (oa, la), got['swa'] = _with_comm(attn_fwd(X, 1, 0, 4, 5, True, 0, BLK - 1, tag + "_swa",
                                               comm=comms.get('swa')), comms.get('swa'))
    views = {1: (X, 6, 10, 14), 4: (X4, 0, 4, 8), 16: (X16, 0, 4, 8)}
    obs, lbs = [], []
    for window, d in B_BRANCHES:
        xd, qo, ko, vo = views[d]
        (o, l), got[d] = _with_comm(attn_fwd(xd, d, qo, ko, vo, False, 8, window // d,
                                             tag + f"_dil{d}", comm=comms.get(d)), comms.get(d))
        obs.append(o)
        lbs.append(l)
    m = attn_merge_fwd(oa, la, sinkvec, obs, lbs, tag + "_merge")
    if w_out is None:
        w_out = got[16][0].reshape(D, D)
        got['w_out'] = w_out
    y = matmul([(m, w_out)], "nn", tag + "_out")
    return y, (h, qkv, views, oa, la, obs, lbs, m), got


def attention_block_bwd(dy, res, w_in, wvec, sinkvec, w_out, tag, comms=None, send_w_out_on=None):
    h, qkv, views, oa, la, obs, lbs, m = res
    comms = dict(comms or {})
    got = {}
    g_w_out = matmul([(m, dy)], "tn", tag + "_dwout", out_dtype=BF16)
    if send_w_out_on is not None:
        comms[send_w_out_on] = ([g_w_out.reshape(N_DEV, D // N_DEV, D)], False)
    dm = matmul([(dy, w_out)], "nt", tag + "_dm")
    doa, dla, d1, d2, d3, g1, g2, g3, sinksums = attn_merge_bwd(dm, oa, la, sinkvec, obs, lbs, tag + "_dmerge")
    d_a, got['swa'] = _with_comm(attn_bwd(views[1][0], oa, la, doa, dla, 1, 0, 4, 5, True, 0, BLK - 1,
                                          tag + "_dswa", comm=comms.get('swa')), comms.get('swa'))
    d_b = []
    for (window, d), o, l, do, dl in zip(B_BRANCHES, obs, lbs, (d1, d2, d3), (g1, g2, g3)):
        xd, qo, ko, vo = views[d]
        dqkv_d, got[d] = _with_comm(attn_bwd(xd, o, l, do, dl, d, qo, ko, vo, False, 8, window // d,
                                             tag + f"_ddil{d}", comm=comms.get(d)), comms.get(d))
        d_b.append(dqkv_d)
    dqkv, wsums = qknorm_bwd(qkv, wvec, d_a, d_b, tag + "_dqknorm")
    g_w_in = matmul([(h, dqkv)], "tn", tag + "_dwin", out_dtype=BF16)
    dh = matmul([(dqkv, w_in)], "nt", tag + "_dh")
    ws = wsums.sum(axis=0)
    grads = dict(
        w_in=g_w_in, w_out=g_w_out,
        q_norm_a=ws[0:512].reshape(8, HD).sum(axis=0), k_norm_a=ws[512:640].reshape(2, HD).sum(axis=0),
        q_norm_b=ws[768:1280].reshape(8, HD).sum(axis=0), k_norm_b=ws[1280:1792].reshape(8, HD).sum(axis=0),
        sinks=sinksums.sum(axis=0).reshape(8, HD).sum(axis=1))
    return dh, grads, got


def ffn_block_fwd(h, w_up_a, w_up_b, cw, w_down, tag, comm=None):
    ua = matmul([(h, w_up_a)], "nn", tag + "_upa")
    ub = matmul([(h, w_up_b)], "nn", tag + "_upb")
    (act, ac, bc), got = _with_comm(ffn_act_fwd(ua, ub, cw, tag + "_act", comm=comm), comm)
    f = matmul([(act, w_down)], "nn", tag + "_down")
    return f, (h, ua, ub, ac, bc, act), got


def ffn_block_bwd(df, res, w_up_a, w_up_b, cw, w_down, tag, comm=None):
    h, ua, ub, ac, bc, act = res
    g_down = matmul([(act, df)], "tn", tag + "_dwdown", out_dtype=BF16)
    dact = matmul([(df, w_down)], "nt", tag + "_dact")
    (dua, dub, sums), got = _with_comm(ffn_act_bwd(ua, ub, ac, bc, cw, dact, tag + "_dactk", comm=comm), comm)
    g_up = jnp.concatenate([_cols_to_slabs(matmul([(h, dua)], "tn", tag + "_dwupa", out_dtype=BF16), N_DEV // 2),
                            _cols_to_slabs(matmul([(h, dub)], "tn", tag + "_dwupb", out_dtype=BF16), N_DEV // 2)],
                           axis=0)
    dh = matmul([(dua, w_up_a), (dub, w_up_b)], "nt", tag + "_dh")
    s = sums.reshape(2, 3, 8, D_FF).sum(axis=2)
    g_conv = jnp.concatenate([s[0], s[1]], axis=1)
    return dh, dict(w_up=g_up, conv=g_conv, w_down=g_down), got


def s5_params(lam_re, lam_im, log_dt, b_re, b_im, c_re, c_im):
    dt = jnp.exp(log_dt)[:, None]
    mag, ang = jnp.exp(lam_re * dt), lam_im * dt
    a_re, a_im = mag * jnp.cos(ang), mag * jnp.sin(ang)
    nr, ni = a_re - 1.0, a_im
    den = lam_re * lam_re + lam_im * lam_im
    f_re = (nr * lam_re + ni * lam_im) / den
    f_im = (ni * lam_re - nr * lam_im) / den
    eye = jnp.eye(16, dtype=F32)[:, None, :, None]
    bd = lambda b: (eye * jnp.transpose(b, (0, 2, 1))[:, :, None, :]).reshape(S5_W, S5_P)
    cd = lambda c: (eye * jnp.transpose(c, (0, 2, 1))[:, :, None, :]).reshape(S5_P, S5_W)
    flat = lambda t: t.reshape(1, S5_P)
    return flat(a_re), flat(a_im), flat(f_re), flat(f_im), bd(b_re), bd(b_im), cd(c_re), cd(c_im)


def _scan_tables(a_re, a_im, reverse):
    pows = [(a_re, a_im)]
    for _ in range(7):
        pr, pi = pows[-1]
        pows.append((pr * a_re - pi * a_im, pr * a_im + pi * a_re))
    order = list(range(7, -1, -1)) if reverse else list(range(8))
    z = jnp.zeros_like(a_re)
    rows = [pows[0][0], pows[0][1], pows[1][0], pows[1][1], pows[3][0], pows[3][1], z, z]
    rows += [pows[k][0] for k in order] + [pows[k][1] for k in order]
    return jnp.concatenate(rows, axis=0)


def _block_scan(er, ei, tab_ref, cr, ci, reverse):
    rows = lax.broadcasted_iota(jnp.int32, er.shape, 0)
    for idx, s in enumerate((1, 2, 4)):
        if reverse:
            sr, si, keep = pltpu.roll(er, 8 - s, axis=0), pltpu.roll(ei, 8 - s, axis=0), rows < 8 - s
        else:
            sr, si, keep = pltpu.roll(er, s, axis=0), pltpu.roll(ei, s, axis=0), rows >= s
        sr, si = jnp.where(keep, sr, 0.0), jnp.where(keep, si, 0.0)
        ar, ai = tab_ref[2 * idx:2 * idx + 1, :], tab_ref[2 * idx + 1:2 * idx + 2, :]
        er, ei = er + ar * sr - ai * si, ei + ar * si + ai * sr
    pr, pi_ = tab_ref[8:16, :], tab_ref[16:24, :]
    er, ei = er + pr * cr - pi_ * ci, ei + pr * ci + pi_ * cr
    return er, ei


def s5_scan_fwd(bu_re, bu_im, a_re, a_im, f_re, f_im, name):
    L, P = bu_re.shape
    tl = _rtile(L, 512)
    tab = _scan_tables(a_re, a_im, False)
    fvec = jnp.concatenate([f_re, f_im] + [jnp.zeros_like(f_re)] * 6, axis=0)

    def body(br_ref, bi_ref, tab_ref, f_ref, xr_ref, xi_ref, c_ref):
        @pl.when(pl.program_id(0) == 0)
        def _():
            c_ref[...] = jnp.zeros_like(c_ref)

        def blk(i, carry):
            cr, ci = carry
            rows = pl.ds(pl.multiple_of(i * 8, 8), 8)
            br, bi = br_ref[rows, :], bi_ref[rows, :]
            fr, fi = f_ref[0:1, :], f_ref[1:2, :]
            er, ei = _block_scan(fr * br - fi * bi, fr * bi + fi * br, tab_ref, cr, ci, False)
            xr_ref[rows, :] = er
            xi_ref[rows, :] = ei
            return er[7:8, :], ei[7:8, :]

        cr, ci = lax.fori_loop(0, tl // 8, blk, (c_ref[0:1, :], c_ref[1:2, :]))
        c_ref[0:1, :] = cr
        c_ref[1:2, :] = ci

    big = pl.BlockSpec((tl, P), lambda i: (i, 0))
    out = jax.ShapeDtypeStruct((L, P), F32)
    return pl.pallas_call(
        body, name=name, grid=(L // tl,),
        in_specs=[big, big, pl.BlockSpec((24, P), lambda i: (0, 0)), pl.BlockSpec((8, P), lambda i: (0, 0))],
        out_specs=(big, big), out_shape=(out, out), scratch_shapes=[pltpu.VMEM((8, P), F32)],
        compiler_params=_cparams("arbitrary"))(bu_re, bu_im, tab, fvec)


def s5_scan_bwd(dx_re, dx_im, x_re, x_im, bu_re, bu_im, a_re, a_im, f_re, f_im, name):
    L, P = dx_re.shape
    tl = _rtile(L, 256)
    nt = L // tl
    tab = _scan_tables(a_re, -a_im, True)
    fvec = jnp.concatenate([f_re, f_im] + [jnp.zeros_like(f_re)] * 6, axis=0)

    def body(gr_ref, gi_ref, xr_ref, xi_ref, br_ref, bi_ref, tab_ref, f_ref, dbr_ref, dbi_ref, s_ref, c_ref):
        @pl.when(pl.program_id(0) == 0)
        def _():
            c_ref[...] = jnp.zeros_like(c_ref)
            s_ref[...] = jnp.zeros_like(s_ref)

        def blk(k, carry):
            cr, ci = carry
            i = tl // 8 - 1 - k
            rows = pl.ds(pl.multiple_of(i * 8, 8), 8)
            er, ei = _block_scan(gr_ref[rows, :], gi_ref[rows, :], tab_ref, cr, ci, True)
            rid = lax.broadcasted_iota(jnp.int32, er.shape, 0)
            sr = jnp.where(rid == 7, cr, pltpu.roll(er, 7, axis=0))
            si = jnp.where(rid == 7, ci, pltpu.roll(ei, 7, axis=0))
            xr, xi = xr_ref[rows, :], xi_ref[rows, :]
            s_ref[0:8, :] += sr * xr + si * xi
            s_ref[8:16, :] += si * xr - sr * xi
            br, bi = br_ref[rows, :], bi_ref[rows, :]
            s_ref[16:24, :] += er * br + ei * bi
            s_ref[24:32, :] += ei * br - er * bi
            fr, fi = f_ref[0:1, :], f_ref[1:2, :]
            dbr_ref[rows, :] = fr * er + fi * ei
            dbi_ref[rows, :] = fr * ei - fi * er
            return er[0:1, :], ei[0:1, :]

        cr, ci = lax.fori_loop(0, tl // 8, blk, (c_ref[0:1, :], c_ref[1:2, :]))
        c_ref[0:1, :] = cr
        c_ref[1:2, :] = ci

    big = pl.BlockSpec((tl, P), lambda i: (nt - 1 - i, 0))
    out = jax.ShapeDtypeStruct((L, P), F32)
    return pl.pallas_call(
        body, name=name, grid=(nt,),
        in_specs=[big] * 6 + [pl.BlockSpec((24, P), lambda i: (0, 0)), pl.BlockSpec((8, P), lambda i: (0, 0))],
        out_specs=(big, big, pl.BlockSpec((32, P), lambda i: (0, 0))),
        out_shape=(out, out, jax.ShapeDtypeStruct((32, P), F32)), scratch_shapes=[pltpu.VMEM((8, P), F32)],
        compiler_params=_cparams("arbitrary"))(dx_re, dx_im, x_re, x_im, bu_re, bu_im, tab, fvec)


_GK, _GC = math.sqrt(2.0 / math.pi), 0.044715


def _gelu(y):
    return 0.5 * y * (1.0 + jnp.tanh(_GK * (y + _GC * y * y * y)))


def _dgelu(y):
    t = jnp.tanh(_GK * (y + _GC * y * y * y))
    return 0.5 * (1.0 + t) + 0.5 * y * (1.0 - t * t) * _GK * (1.0 + 3.0 * _GC * y * y)


def s5_out_fwd(x_re, x_im, u, cd_re, cd_im, dskip, glu_w, glu_b, name):
    L = u.shape[0]
    tl = _rtile(L, 512)

    def body(xr_ref, xi_ref, u_ref, cr_ref, ci_ref, d_ref, w_ref, b_ref, y_ref, o_ref):
        y = (jnp.dot(xr_ref[...].astype(BF16), cr_ref[...], preferred_element_type=F32)
             - jnp.dot(xi_ref[...].astype(BF16), ci_ref[...], preferred_element_type=F32)
             + d_ref[...] * u_ref[...])
        y_ref[...] = y
        g = _gelu(y)
        z = jnp.dot(g.astype(BF16), w_ref[...], preferred_element_type=F32) + b_ref[...]
        o_ref[...] = (g * _sigmoid(z)).astype(BF16)

    big = pl.BlockSpec((tl, S5_P), lambda i: (i, 0))
    sm = pl.BlockSpec((tl, S5_W), lambda i: (i, 0))
    full = lambda r, c: pl.BlockSpec((r, c), lambda i: (0, 0))
    return pl.pallas_call(
        body, name=name, grid=(L // tl,),
        in_specs=[big, big, sm, full(S5_P, S5_W), full(S5_P, S5_W), full(1, S5_W), full(S5_W, S5_W), full(1, S5_W)],
        out_specs=(sm, sm),
        out_shape=(jax.ShapeDtypeStruct((L, S5_W), F32), jax.ShapeDtypeStruct((L, S5_W), BF16)),
        compiler_params=_cparams("parallel"))(x_re, x_im, u, cd_re, cd_im, dskip, glu_w, glu_b)


def s5_out_bwd(dout, y, u, x_re, x_im, cd_re, cd_im, dskip, glu_w, glu_b, name, dout_col=0):
    L = u.shape[0]
    tl = _rtile(L, 256)
    nt_dims = (((1,), (1,)), ((), ()))
    tn_dims = (((0,), (0,)), ((), ()))

    def body(do_ref, y_ref, u_ref, xr_ref, xi_ref, cr_ref, ci_ref, d_ref, w_ref, b_ref,
             dxr_ref, dxi_ref, du_ref, dcr_ref, dci_ref, dw_ref, s_ref):
        @pl.when(pl.program_id(0) == 0)
        def _():
            dcr_ref[...] = jnp.zeros_like(dcr_ref)
            dci_ref[...] = jnp.zeros_like(dci_ref)
            dw_ref[...] = jnp.zeros_like(dw_ref)
            s_ref[...] = jnp.zeros_like(s_ref)

        yv, dov = y_ref[...], do_ref[...]
        g = _gelu(yv)
        gb = g.astype(BF16)
        sg = _sigmoid(jnp.dot(gb, w_ref[...], preferred_element_type=F32) + b_ref[...])
        dz = dov * g * sg * (1.0 - sg)
        dzb = dz.astype(BF16)
        dg = dov * sg + lax.dot_general(dzb, w_ref[...], nt_dims, preferred_element_type=F32)
        dw_ref[...] += lax.dot_general(gb, dzb, tn_dims, preferred_element_type=F32)
        dy = dg * _dgelu(yv)
        dyb = dy.astype(BF16)
        s_ref[0:8, :] += _fold8(dy * u_ref[...])
        s_ref[8:16, :] += _fold8(dz)
        du_ref[...] = dy * d_ref[...]
        dxr_ref[...] = lax.dot_general(dyb, cr_ref[...], nt_dims, preferred_element_type=F32)
        dxi_ref[...] = -lax.dot_general(dyb, ci_ref[...], nt_dims, preferred_element_type=F32)
        dcr_ref[...] += lax.dot_general(xr_ref[...].astype(BF16), dyb, tn_dims, preferred_element_type=F32)
        dci_ref[...] -= lax.dot_general(xi_ref[...].astype(BF16), dyb, tn_dims, preferred_element_type=F32)

    big = pl.BlockSpec((tl, S5_P), lambda i: (i, 0))
    sm = pl.BlockSpec((tl, S5_W), lambda i: (i, 0))
    full = lambda r, c: pl.BlockSpec((r, c), lambda i: (0, 0))
    sd = jax.ShapeDtypeStruct
    return pl.pallas_call(
        body, name=name, grid=(L // tl,),
        in_specs=[pl.BlockSpec((tl, S5_W), lambda i: (i, dout_col)), sm, sm, big, big, full(S5_P, S5_W),
                  full(S5_P, S5_W), full(1, S5_W), full(S5_W, S5_W), full(1, S5_W)],
        out_specs=(big, big, sm, full(S5_P, S5_W), full(S5_P, S5_W), full(S5_W, S5_W), full(16, S5_W)),
        out_shape=(sd((L, S5_P), F32), sd((L, S5_P), F32), sd((L, S5_W), F32), sd((S5_P, S5_W), F32),
                   sd((S5_P, S5_W), F32), sd((S5_W, S5_W), F32), sd((16, S5_W), F32)),
        compiler_params=_cparams("arbitrary"))(dout, y, u, x_re, x_im, cd_re, cd_im, dskip, glu_w, glu_b)


def s5_block_fwd(u, params, dskip, glu_w, glu_b, tag):
    a_re, a_im, f_re, f_im, bd_re, bd_im, cd_re, cd_im = params
    bu_re = matmul([(u, bd_re.astype(BF16))], "nn", tag + "_bure")
    bu_im = matmul([(u, bd_im.astype(BF16))], "nn", tag + "_buim")
    x_re, x_im = s5_scan_fwd(bu_re, bu_im, a_re, a_im, f_re, f_im, tag + "_scan")
    y, out = s5_out_fwd(x_re, x_im, u, cd_re.astype(BF16), cd_im.astype(BF16), dskip, glu_w, glu_b, tag + "_out")
    return out, (u, bu_re, bu_im, x_re, x_im, y)


def s5_block_bwd(dout, res, params, dskip, glu_w, glu_b, tag, dout_col=0):
    u, bu_re, bu_im, x_re, x_im, y = res
    a_re, a_im, f_re, f_im, bd_re, bd_im, cd_re, cd_im = params
    dxr, dxi, du, dcr, dci, dglu_w, sums = s5_out_bwd(dout, y, u, x_re, x_im, cd_re.astype(BF16), cd_im.astype(BF16),
                                                      dskip, glu_w, glu_b, tag + "_dout", dout_col=dout_col)
    dbr, dbi, acc = s5_scan_bwd(dxr, dxi, x_re, x_im, bu_re, bu_im, a_re, a_im, f_re, f_im, tag + "_dscan")
    du = du + matmul([(dbr, bd_re.astype(BF16)), (dbi, bd_im.astype(BF16))], "nt", tag + "_du")
    dbd_re = matmul([(u, dbr)], "tn", tag + "_dbdre")
    dbd_im = matmul([(u, dbi)], "tn", tag + "_dbdim")
    acc = acc.reshape(4, 8, S5_P).sum(axis=1)
    s = sums.reshape(2, 8, S5_W).sum(axis=1)
    cot = (acc[0:1], acc[1:2], acc[2:3], acc[3:4], dbd_re, dbd_im, dcr, dci)
    return du, cot, dict(dskip=s[0], glu_w=dglu_w, glu_b=s[1])


DN_Z0, DN_NT = 18, 18
REC_U0, REC_A0 = 3072, 3328


def rec_cols_permute(w):
    return jnp.concatenate([w[..., S5_W:REC_A0], w[..., :S5_W], w[..., REC_A0:]], axis=-1)


def rec_cols_restore(w):
    return jnp.concatenate([w[..., REC_U0:REC_A0], w[..., :REC_U0], w[..., REC_A0:]], axis=-1)


DN_W = DN_H * DN_DK
DN_NI = 4


def _dn_conv4(taps, w_ref):
    xc = w_ref[3:4, :] * taps[0]
    for k in range(1, 4):
        xc = xc + w_ref[3 - k:4 - k, :] * taps[k]
    return xc


def dn_prep_fwd(rin, cw, name, comm=None):
    L = rin.shape[0]
    tl = _rtile(L, 256)
    hb = tl // 8

    def body(x_ref, h_ref, w_ref, o_ref):
        j = pl.program_id(0)
        first = pl.program_id(1) == 0
        x, h = x_ref[...], h_ref[...]
        s = _silu(_dn_conv4([x] + [_shift_down(x, h, k, first) for k in range(1, 4)], w_ref))
        scale = jnp.where(j == 0, DN_DK ** -0.5, 1.0)
        for hd in _HEADS:
            cs = slice(hd * 128, (hd + 1) * 128)
            sh = s[:, cs]
            r = lax.rsqrt(jnp.sum(sh * sh, axis=-1, keepdims=True) + EPS)
            o_ref[:, cs] = jnp.where(j < 2, sh * r * scale, sh)

    main = pl.BlockSpec((tl, DN_W), lambda j, i: (i, j))
    halo = pl.BlockSpec((8, DN_W), lambda j, i: (jnp.maximum(i * hb - 1, 0), j))
    return _call(body, (rin, rin, cw), name=name, grid=(3, L // tl),
                 in_specs=[main, halo, pl.BlockSpec((4, DN_W), lambda j, i: (0, j))],
                 out_specs=main, out_shape=jax.ShapeDtypeStruct((L, 3 * DN_W), F32),
                 sem=("parallel", "parallel"), comm=comm)


def dn_prep_bwd(rin, cw, dout, name):
    L = rin.shape[0]
    tl = _rtile(L, 256)
    hb = tl // 8
    nrt = L // tl

    def body(x_ref, h_ref, w_ref, d_ref, dx_ref, s_ref, c_ref):
        j = pl.program_id(0)
        i = pl.program_id(1)
        first = i == nrt - 1

        @pl.when(i == 0)
        def _():
            s_ref[...] = jnp.zeros_like(s_ref)
            c_ref[...] = jnp.zeros_like(c_ref)

        x, h = x_ref[...], h_ref[...]
        taps = [x] + [_shift_down(x, h, k, first) for k in range(1, 4)]
        xc = _dn_conv4(taps, w_ref)
        s = _silu(xc)
        scale = jnp.where(j == 0, DN_DK ** -0.5, 1.0)
        pieces = []
        for hd in _HEADS:
            cs = slice(hd * 128, (hd + 1) * 128)
            sh, d = s[:, cs], d_ref[:, cs]
            r = lax.rsqrt(jnp.sum(sh * sh, axis=-1, keepdims=True) + EPS)
            n = sh * r
            dn = d * scale
            pieces.append(jnp.where(j < 2, r * (dn - n * jnp.sum(dn * n, axis=-1, keepdims=True)), d))
        dxc = jnp.concatenate(pieces, axis=1) * _dsilu(xc)
        nxt = c_ref[...]
        dx_ref[...] = _dn_conv4([dxc] + [_shift_up(dxc, nxt, k) for k in range(1, 4)], w_ref).astype(BF16)
        c_ref[...] = dxc[0:8, :]
        for k in range(4):
            s_ref[8 * (3 - k):8 * (3 - k) + 8, :] += _fold8(dxc * taps[k])

    rev = lambda i: nrt - 1 - i
    main = pl.BlockSpec((tl, DN_W), lambda j, i: (rev(i), j))
    halo = pl.BlockSpec((8, DN_W), lambda j, i: (jnp.maximum(rev(i) * hb - 1, 0), j))
    return pl.pallas_call(
        body, name=name, grid=(3, nrt),
        in_specs=[main, halo, pl.BlockSpec((4, DN_W), lambda j, i: (0, j)), main],
        out_specs=(main, pl.BlockSpec((32, DN_W), lambda j, i: (0, j))),
        out_shape=(jax.ShapeDtypeStruct((L, 3 * DN_W), BF16), jax.ShapeDtypeStruct((32, 3 * DN_W), F32)),
        scratch_shapes=[pltpu.VMEM((8, DN_W), F32)],
        compiler_params=_cparams("parallel", "arbitrary"))(rin, rin, cw, dout)


_HI = lax.Precision.HIGH
_NT = (((1,), (1,)), ((), ()))
_TN = (((0,), (0,)), ((), ()))
_HEADS = tuple(range(DN_H))


def _mm(a, b, dims=(((1,), (0,)), ((), ())), hi=False):
    if hi:
        return lax.dot_general(a, b, dims, precision=_HI, preferred_element_type=F32)
    return lax.dot_general(a.astype(BF16), b.astype(BF16), dims, preferred_element_type=F32)


def _dn_masks():
    ri = lax.broadcasted_iota(jnp.int32, (DN_C, DN_C), 0)
    ci = lax.broadcasted_iota(jnp.int32, (DN_C, DN_C), 1)
    return ri >= ci, ri > ci, (ri == ci).astype(F32)


def _dn_decay(gc, gr, causal):
    gam = [jnp.where(causal, jnp.exp(jnp.where(causal, c - r, 0.0)), 0.0) for c, r in zip(gc, gr)]
    eg, el, gl = _dn_row_decay(gc)
    return gam, eg, el, gl


def _dn_row_decay(gc):
    eg = [jnp.exp(c) for c in gc]
    el = [jnp.exp(c[DN_C - 1:DN_C, :] - c) for c in gc]
    gl = [jnp.exp(c[DN_C - 1:DN_C, :]) for c in gc]
    return eg, el, gl


def _dn_solve(k, v, beta, gam, eg, kk, strict, eye):
    ids = range(len(k))
    nmat = [jnp.where(strict, beta[h] * kk[h] * gam[h], 0.0) for h in ids]
    t = [eye - nmat[h] for h in ids]
    m = [_mm(nmat[h], nmat[h]) for h in ids]
    for step in range(5):
        t = [t[h] + _mm(t[h], m[h]) for h in ids]
        if step < 4:
            m = [_mm(m[h], m[h]) for h in ids]
    res = [eye - t[h] - _mm(nmat[h], t[h], hi=True) for h in ids]
    t = [t[h] + _mm(t[h], res[h]) for h in ids]
    rhs = [jnp.concatenate([v[h] * beta[h], k[h] * (beta[h] * eg[h])], axis=1) for h in ids]
    sol = [_mm(t[h], rhs[h], hi=True) for h in ids]
    return t, sol


def dn_chunk_fwd(qkv, gcol, grow, bcol, name, comm=None):
    L = qkv.shape[0]
    C, W = DN_C, DN_H * DN_DK
    ncb = 8
    tl = ncb * C
    nchunks = L // C
    comm1, comm2 = comm if comm is not None else (None, None)
    hs = lambda h: slice(h * 128, (h + 1) * 128)

    def intra(q_ref, k_ref, v_ref, gc_ref, gr_ref, b_ref, t_ref, sol_ref, qk_ref):
        causal, strict, eye = _dn_masks()

        def pair(p, _):
            units = [(DN_NI * p + j, h) for j in range(DN_NI) for h in _HEADS]
            rows = [pl.ds(pl.multiple_of(c * C, C), C) for c, _ in units]
            q = [q_ref[r, hs(h)] for r, (_, h) in zip(rows, units)]
            k = [k_ref[r, hs(h)] for r, (_, h) in zip(rows, units)]
            v = [v_ref[r, hs(h)] for r, (_, h) in zip(rows, units)]
            gc = [gc_ref[r, h:h + 1] for r, (_, h) in zip(rows, units)]
            gr = [gr_ref[c][h:h + 1, :] for c, h in units]
            beta = [b_ref[r, h:h + 1] for r, (_, h) in zip(rows, units)]
            gam, eg, _, _ = _dn_decay(gc, gr, causal)
            kk = [_mm(x, x, _NT) for x in k]
            t, sol = _dn_solve(k, v, beta, gam, eg, kk, strict, eye)
            qk = [_mm(a, b, _NT) * g for a, b, g in zip(q, k, gam)]
            for i, (r, (_, h)) in enumerate(zip(rows, units)):
                t_ref[r, h * C:(h + 1) * C] = t[i]
                sol_ref[r, h * 256:(h + 1) * 256] = sol[i]
                qk_ref[r, h * C:(h + 1) * C] = qk[i]
            return 0

        lax.fori_loop(0, ncb // DN_NI, pair, 0)

    def scan(q_ref, k_ref, gc_ref, sol_ref, qk_ref, o_ref, sh_ref, s_ref):
        @pl.when(pl.program_id(0) == 0)
        def _():
            s_ref[...] = jnp.zeros_like(s_ref)

        def chunk(c, _):
            rows = pl.ds(pl.multiple_of(c * C, C), C)
            q = [q_ref[rows, hs(h)] for h in _HEADS]
            k = [k_ref[rows, hs(h)] for h in _HEADS]
            sol = [sol_ref[rows, h * 256:(h + 1) * 256] for h in _HEADS]
            qk = [qk_ref[rows, h * C:(h + 1) * C] for h in _HEADS]
            eg, el, gl = _dn_row_decay([gc_ref[rows, h:h + 1] for h in _HEADS])
            S = [s_ref[hs(h), :] for h in _HEADS]
            vn = [sol[h][:, :128] - _mm(sol[h][:, 128:], S[h]) for h in _HEADS]
            o = [_mm(q[h] * eg[h], S[h]) + _mm(qk[h], vn[h]) for h in _HEADS]
            Sn = [S[h] * gl[h] + _mm(k[h] * el[h], vn[h], _TN) for h in _HEADS]
            for h in _HEADS:
                sh_ref[c, hs(h), :] = S[h]
                s_ref[hs(h), :] = Sn[h]
                o_ref[rows, hs(h)] = o[h]
            return 0

        lax.fori_loop(0, ncb, chunk, 0)

    col = lambda b: pl.BlockSpec((tl, W), lambda i: (i, b))
    small = pl.BlockSpec((tl, 8), lambda i: (i, 0))
    rowblk = lambda w: pl.BlockSpec((tl, w), lambda i: (i, 0))
    sd = jax.ShapeDtypeStruct
    (thist, solhist, qk), got1 = _with_comm(_call(
        intra, (qkv, qkv, qkv, gcol, grow, bcol), name=name + "_intra", grid=(L // tl,),
        in_specs=[col(0), col(1), col(2), small, pl.BlockSpec((ncb, 8, C), lambda i: (i, 0, 0)), small],
        out_specs=(rowblk(DN_H * C), rowblk(DN_H * 256), rowblk(DN_H * C)),
        out_shape=(sd((L, DN_H * C), F32), sd((L, DN_H * 256), F32), sd((L, DN_H * C), F32)),
        sem=("parallel",), comm=comm1), comm1)
    (o, shist), got2 = _with_comm(_call(
        scan, (qkv, qkv, gcol, solhist, qk), name=name + "_scan", grid=(L // tl,),
        in_specs=[col(0), col(1), small, rowblk(DN_H * 256), rowblk(DN_H * C)],
        out_specs=(rowblk(W), pl.BlockSpec((ncb, W, 128), lambda i: (i, 0, 0))),
        out_shape=(sd((L, W), F32), sd((nchunks, W, 128), F32)),
        scratch_shapes=[pltpu.VMEM((W, 128), F32)], sem=("arbitrary",), comm=comm2), comm2)
    res = (o, shist, thist, solhist)
    return res if comm is None else (res, (got1 or []) + (got2 or []))


def dn_chunk_bwd(qkv, gcol, grow, bcol, shist, thist, solhist, do, name, comm=None):
    L = qkv.shape[0]
    C, W = DN_C, DN_H * DN_DK
    ncb = 8
    tl = ncb * C
    nchunks = L // C
    nt = L // tl

    def body(q_ref, k_ref, v_ref, gc_ref, gr_ref, b_ref, sh_ref, t_ref, sol_ref, do_ref,
             dqkv_ref, dgc_ref, dgr_ref, db_ref, ds_ref):
        @pl.when(pl.program_id(0) == 0)
        def _():
            ds_ref[...] = jnp.zeros_like(ds_ref)

        lane8 = lax.broadcasted_iota(jnp.int32, (C, 8), 1)
        sub8 = lax.broadcasted_iota(jnp.int32, (8, C), 0)
        rowid = lax.broadcasted_iota(jnp.int32, (C, 1), 0)
        causal, strict, _ = _dn_masks()
        rsum = lambda a: jnp.sum(a, axis=1, keepdims=True)

        def chunk(cc, _):
            c = ncb - 1 - cc
            rows = pl.ds(pl.multiple_of(c * C, C), C)
            grow_c = gr_ref[c]
            hs = lambda h: slice(h * 128, (h + 1) * 128)
            q = [q_ref[rows, hs(h)] for h in _HEADS]
            k = [k_ref[rows, hs(h)] for h in _HEADS]
            v = [v_ref[rows, hs(h)] for h in _HEADS]
            gc = [gc_ref[rows, h:h + 1] for h in _HEADS]
            gr = [grow_c[h:h + 1, :] for h in _HEADS]
            beta = [b_ref[rows, h:h + 1] for h in _HEADS]
            t = [t_ref[rows, h * C:(h + 1) * C] for h in _HEADS]
            sol = [sol_ref[rows, h * 256:(h + 1) * 256] for h in _HEADS]
            S = [sh_ref[c, hs(h), :] for h in _HEADS]
            dS = [ds_ref[hs(h), :] for h in _HEADS]
            dov = [do_ref[rows, hs(h)] for h in _HEADS]
            gam, eg, el, gl = _dn_decay(gc, gr, causal)
            kk = [_mm(k[h], k[h], _NT) for h in _HEADS]
            qk_raw = [_mm(q[h], k[h], _NT) for h in _HEADS]
            w = [sol[h][:, 128:] for h in _HEADS]
            kd = [k[h] * el[h] for h in _HEADS]
            vn = [sol[h][:, :128] - _mm(w[h], S[h]) for h in _HEADS]
            dvn = [_mm(qk_raw[h] * gam[h], dov[h], _TN) + _mm(kd[h], dS[h]) for h in _HEADS]
            dqd = [_mm(dov[h], S[h], _NT) for h in _HEADS]
            dqk = [jnp.where(causal, _mm(dov[h], vn[h], _NT), 0.0) for h in _HEADS]
            dkd = [_mm(vn[h], dS[h], _NT) for h in _HEADS]
            dgl = [jnp.sum(rsum(dS[h] * S[h]), axis=0, keepdims=True) for h in _HEADS]
            dw = [-_mm(dvn[h], S[h], _NT) for h in _HEADS]
            dSn = [dS[h] * gl[h] + _mm(q[h] * eg[h], dov[h], _TN) - _mm(w[h], dvn[h], _TN) for h in _HEADS]
            drhs = [_mm(t[h], jnp.concatenate([dvn[h], dw[h]], axis=1), _TN) for h in _HEADS]
            dn = [jnp.where(strict, -_mm(drhs[h], sol[h], _NT), 0.0) for h in _HEADS]
            dgc_all = jnp.zeros((C, 8), F32)
            db_all = jnp.zeros((C, 8), F32)
            dgr_all = jnp.zeros((8, C), F32)
            for h in _HEADS:
                drv, drk = drhs[h][:, :128], drhs[h][:, 128:]
                t2 = rsum(drk * k[h])
                x = dn[h] * gam[h]
                dbeta = rsum(drv * v[h]) + t2 * eg[h] + rsum(x * kk[h])
                dkk = x * beta[h]
                draw = dqk[h] * gam[h]
                mm_ = (dn[h] * beta[h] * kk[h] + dqk[h] * qk_raw[h]) * gam[h]
                deg = t2 * beta[h] + rsum(dqd[h] * q[h])
                r_ = rsum(dkd[h] * k[h]) * el[h]
                dglast = jnp.sum(r_, axis=0, keepdims=True) + dgl[h] * gl[h]
                dgc = rsum(mm_) + deg * eg[h] - r_ + jnp.where(rowid == C - 1, dglast, 0.0)
                dgr = -jnp.sum(mm_, axis=0, keepdims=True)
                dqkv_ref[rows, hs(h)] = _mm(draw, k[h]) + dqd[h] * eg[h]
                dqkv_ref[rows, hs(DN_H + h)] = (drk * (beta[h] * eg[h]) + _mm(dkk, k[h]) + _mm(dkk, k[h], _TN)
                                                + _mm(draw, q[h], _TN) + dkd[h] * el[h])
                dqkv_ref[rows, hs(2 * DN_H + h)] = drv * beta[h]
                ds_ref[hs(h), :] = dSn[h]
                dgc_all = dgc_all + jnp.where(lane8 == h, dgc, 0.0)
                db_all = db_all + jnp.where(lane8 == h, dbeta, 0.0)
                dgr_all = dgr_all + jnp.where(sub8 == h, dgr, 0.0)
            dgc_ref[rows, :] = dgc_all
            db_ref[rows, :] = db_all
            dgr_ref[c] = dgr_all
            return 0

        lax.fori_loop(0, ncb, chunk, 0)

    rev = lambda i: nt - 1 - i
    col = lambda b: pl.BlockSpec((tl, W), lambda i: (rev(i), b))
    rowblk = lambda w: pl.BlockSpec((tl, w), lambda i: (rev(i), 0))
    small = pl.BlockSpec((tl, 8), lambda i: (rev(i), 0))
    g3 = pl.BlockSpec((ncb, 8, C), lambda i: (rev(i), 0, 0))
    sd = jax.ShapeDtypeStruct
    return _call(body, (qkv, qkv, qkv, gcol, grow, bcol, shist, thist, solhist, do), name=name, grid=(nt,),
                 in_specs=[col(0), col(1), col(2), small, g3, small,
                           pl.BlockSpec((ncb, W, 128), lambda i: (rev(i), 0, 0)), rowblk(DN_H * C),
                           rowblk(DN_H * 256), col(0)],
                 out_specs=(rowblk(3 * W), small, g3, small),
                 out_shape=(sd((L, 3 * W), F32), sd((L, 8), F32), sd((nchunks, 8, C), F32), sd((L, 8), F32)),
                 scratch_shapes=[pltpu.VMEM((W, 128), F32)], sem=("arbitrary",), comm=comm)


def dn_out_fwd(o, rin, nw, name):
    L = o.shape[0]
    tl = _rtile(L, 256)

    def body(o_ref, z_ref, w_ref, y_ref):
        for hd in _HEADS:
            cs = slice(hd * 128, (hd + 1) * 128)
            ov = o_ref[:, cs]
            r = lax.rsqrt(jnp.mean(ov * ov, axis=-1, keepdims=True) + EPS)
            y_ref[:, cs] = (ov * r * w_ref[...] * _silu(z_ref[:, cs])).astype(BF16)

    return pl.pallas_call(
        body, name=name, grid=(L // tl,),
        in_specs=[pl.BlockSpec((tl, DN_W), lambda i: (i, 0)), pl.BlockSpec((tl, DN_W), lambda i: (i, 3)),
                  pl.BlockSpec((1, 128), lambda i: (0, 0))],
        out_specs=pl.BlockSpec((tl, DN_W), lambda i: (i, 0)), out_shape=jax.ShapeDtypeStruct((L, DN_W), BF16),
        compiler_params=_cparams("parallel"))(o, rin, nw)


def dn_out_bwd(dycat, o, rin, nw, name):
    L = o.shape[0]
    tl = _rtile(L, 256)

    def body(dy_ref, o_ref, z_ref, w_ref, do_ref, dz_ref, s_ref):
        @pl.when(pl.program_id(0) == 0)
        def _():
            s_ref[...] = jnp.zeros_like(s_ref)

        for hd in _HEADS:
            cs = slice(hd * 128, (hd + 1) * 128)
            ov, zv, d = o_ref[:, cs], z_ref[:, cs], dy_ref[:, cs]
            r = lax.rsqrt(jnp.mean(ov * ov, axis=-1, keepdims=True) + EPS)
            n = ov * r
            dnw = d * _silu(zv)
            dz_ref[:, cs] = (d * n * w_ref[...] * _dsilu(zv)).astype(BF16)
            dn = dnw * w_ref[...]
            do_ref[:, cs] = r * (dn - n * jnp.mean(dn * n, axis=-1, keepdims=True))
            s_ref[:, cs] += _fold8(dnw * n)

    own = pl.BlockSpec((tl, DN_W), lambda i: (i, 0))
    sd = jax.ShapeDtypeStruct
    return pl.pallas_call(
        body, name=name, grid=(L // tl,),
        in_specs=[own, own, pl.BlockSpec((tl, DN_W), lambda i: (i, 3)), pl.BlockSpec((1, 128), lambda i: (0, 0))],
        out_specs=(own, own, pl.BlockSpec((8, DN_W), lambda i: (0, 0))),
        out_shape=(sd((L, DN_W), F32), sd((L, DN_W), BF16), sd((8, DN_W), F32)),
        compiler_params=_cparams("arbitrary"))(dycat, o, rin, nw)


def dn_gates(a, beta_raw, a_log, dt_bias):
    L = a.shape[0]
    beta = jax.nn.sigmoid(beta_raw)
    g = -jnp.exp(a_log) * jax.nn.softplus(a + dt_bias)
    G = jnp.cumsum(g.reshape(L // DN_C, DN_C, DN_H), axis=1)
    pad = lambda t: jnp.pad(t, ((0, 0), (0, 8 - DN_H)))
    gcol = pad(G.reshape(L, DN_H))
    grow = jnp.pad(jnp.transpose(G, (0, 2, 1)), ((0, 0), (0, 8 - DN_H), (0, 0)))
    return gcol, grow, pad(beta)


def dn_block_fwd(rin, cw, a_log, dt_bias, out_norm, tag, comm=None):
    gates, gates_vjp = jax.vjp(dn_gates, rin[:, REC_A0:REC_A0 + DN_H], rin[:, REC_A0 + DN_H:REC_IN], a_log, dt_bias)
    c0, c12 = (comm[0], comm[1:]) if comm is not None else (None, None)
    qkv, got0 = _with_comm(dn_prep_fwd(rin, cw, tag + "_prep", comm=c0), c0)
    (o, shist, thist, solhist), got = _with_comm(dn_chunk_fwd(qkv, *gates, tag + "_chunk", comm=c12), c12)
    yd = dn_out_fwd(o, rin, out_norm.reshape(1, 128), tag + "_onorm")
    return yd, (qkv, gates, gates_vjp, o, shist, thist, solhist), (got0 or []) + (got or [])


def dn_block_bwd(dyd, res, rin, cw, out_norm, tag, comm=None):
    qkv, gates, gates_vjp, o, shist, thist, solhist = res
    do, dz, nsum = dn_out_bwd(dyd, o, rin, out_norm.reshape(1, 128), tag + "_donorm")
    (dqkv, dgc, dgr, db), got = _with_comm(dn_chunk_bwd(qkv, *gates, shist, thist, solhist, do, tag + "_dchunk",
                                                        comm=comm), comm)
    da, dbraw, g_alog, g_dtb = gates_vjp((dgc, dgr, db))
    dx, csum = dn_prep_bwd(rin, cw, dqkv, tag + "_dprep")
    grads = dict(conv=csum.reshape(4, 8, DN_NT * 128).sum(axis=1), a_log=g_alog, dt_bias=g_dtb,
                 out_norm=nsum.sum(axis=0).reshape(DN_H, 128).sum(axis=0))
    return dx, dz, da, dbraw, grads, got


_HBM = pl.BlockSpec(memory_space=pltpu.HBM)


def _mesh_pos():
    xi, yi, ci = lax.axis_index("x"), lax.axis_index("y"), lax.axis_index("c")
    return xi, yi, ci, 4 * xi + 2 * yi + ci


def _peer(xi, yi, ci, k):
    px = 1 - xi if (k >> 2) & 1 else xi
    py = 1 - yi if (k >> 1) & 1 else yi
    pc = 1 - ci if k & 1 else ci
    return (px, py, pc), 4 * px + 2 * py + pc


def _exchange(xs, gather, name):
    n = len(xs)

    def body(*refs):
        copies = _comm_copies(refs[:n], refs[n:2 * n], *refs[2 * n:], gather)
        for cp in copies:
            cp.start()
        for cp in copies:
            cp.wait()

    return pl.pallas_call(
        body, name=name, in_specs=[_HBM] * n, out_specs=tuple([_HBM] * n),
        out_shape=_comm_out_shapes(xs), scratch_shapes=_comm_sems(n))(*xs)


def _comm_out_shapes(xs):
    return tuple(jax.ShapeDtypeStruct((N_DEV,) + x.shape[-2:], x.dtype) for x in xs)


def _comm_sems(n):
    return [pltpu.SemaphoreType.DMA((n * (N_DEV - 1),)), pltpu.SemaphoreType.DMA((n * (N_DEV - 1),)),
            pltpu.SemaphoreType.DMA((n,))]


def _comm_copies(x_refs, o_refs, send_sems, recv_sems, lsems, gather):
    xi, yi, ci, me = _mesh_pos()
    copies = []
    for t in range(len(x_refs)):
        src_of = (lambda lin, t=t: x_refs[t]) if gather else (lambda lin, t=t: x_refs[t].at[lin])
        copies.append(pltpu.make_async_copy(src_of(me), o_refs[t].at[me], lsems.at[t]))
        for k in range(1, N_DEV):
            peer, lin = _peer(xi, yi, ci, k)
            s = t * (N_DEV - 1) + k - 1
            copies.append(pltpu.make_async_remote_copy(
                src_ref=src_of(lin), dst_ref=o_refs[t].at[me], send_sem=send_sems.at[s],
                recv_sem=recv_sems.at[s], device_id=peer, device_id_type=pl.DeviceIdType.MESH))
    return copies


def _call(body, args, *, name, grid, in_specs, out_specs, out_shape, scratch_shapes=(), sem, comm=None):
    if comm is None:
        return pl.pallas_call(body, name=name, grid=grid, in_specs=in_specs, out_specs=out_specs,
                              out_shape=out_shape, scratch_shapes=list(scratch_shapes),
                              compiler_params=_cparams(*sem))(*args)
    xs, gather = comm
    n = len(xs)
    single = not isinstance(out_shape, (tuple, list))
    outs_shape = (out_shape,) if single else tuple(out_shape)
    outs_specs = (out_specs,) if single else tuple(out_specs)
    n_in, n_out, n_scr = len(in_specs), len(outs_shape), len(scratch_shapes)

    def body2(*refs):
        ins, cx = refs[:n_in], refs[n_in:n_in + n]
        outs = refs[n_in + n:n_in + n + n_out]
        co = refs[n_in + n + n_out:n_in + 2 * n + n_out]
        scr = refs[n_in + 2 * n + n_out:n_in + 2 * n + n_out + n_scr]
        sems = refs[n_in + 2 * n + n_out + n_scr:]
        first = functools.reduce(jnp.logical_and, [pl.program_id(a) == 0 for a in range(len(grid))])
        last = functools.reduce(jnp.logical_and, [pl.program_id(a) == grid[a] - 1 for a in range(len(grid))])

        @pl.when(first)
        def _():
            for cp in _comm_copies(cx, co, *sems, gather):
                cp.start()

        body(*ins, *outs, *scr)

        @pl.when(last)
        def _():
            for cp in _comm_copies(cx, co, *sems, gather):
                cp.wait()

    res = pl.pallas_call(
        body2, name=name, grid=grid, in_specs=list(in_specs) + [_HBM] * n,
        out_specs=outs_specs + tuple([_HBM] * n), out_shape=outs_shape + _comm_out_shapes(xs),
        scratch_shapes=list(scratch_shapes) + _comm_sems(n),
        compiler_params=_cparams(*(["arbitrary"] * len(grid))))(*args, *xs)
    main = res[0] if single else tuple(res[:n_out])
    return main, list(res[n_out:])


def all_gather(x, name):
    return _exchange([x], True, name)[0]


def all_gather_many(xs, name):
    return _exchange(xs, True, name)


def all_to_all_many(xs, name):
    return _exchange(xs, False, name)


def reduce_adamw(gsrc, w, m, v, name, comm=None):
    parts = list(gsrc) if isinstance(gsrc, (list, tuple)) else [gsrc]
    S, R0, C = parts[0].shape
    R = R0 * len(parts)
    tr = _rtile(R0, max(16, min(256, (4 << 20) // (S * C * 4) // 16 * 16)), 16 if R0 % 16 == 0 else 8)
    n0 = R0 // tr
    c1 = 1.0 - ADAM_B1 ** ADAM_STEP
    c2 = 1.0 - ADAM_B2 ** ADAM_STEP

    def body(*refs):
        g_refs = refs[:len(parts)]
        w_ref, m_ref, v_ref, go_ref, d_ref, mo_ref, vo_ref = refs[len(parts):]
        for p, g_ref in enumerate(g_refs):
            @pl.when(pl.program_id(0) // n0 == p)
            def _(g_ref=g_ref):
                acc = g_ref[0].astype(F32)
                for s in range(1, S):
                    acc = acc + g_ref[s].astype(F32)
                go_ref[...] = acc
        g = go_ref[...]
        mn = ADAM_B1 * m_ref[...] + (1.0 - ADAM_B1) * g
        vn = ADAM_B2 * v_ref[...] + (1.0 - ADAM_B2) * (g * g)
        mo_ref[...] = mn
        vo_ref[...] = vn
        d_ref[...] = -ADAM_LR * ((mn / c1) / (jnp.sqrt(vn / c2) + ADAM_EPS) + ADAM_WD * w_ref[...])

    big = pl.BlockSpec((tr, C), lambda i: (i, 0))
    o = jax.ShapeDtypeStruct((R, C), F32)
    part_spec = lambda p: pl.BlockSpec((S, tr, C), lambda i: (0, jnp.clip(i - p * n0, 0, n0 - 1), 0))
    return _call(body, (*parts, w, m, v), name=name, grid=(R // tr,),
                 in_specs=[part_spec(p) for p in range(len(parts))] + [big, big, big],
                 out_specs=(big, big, big, big), out_shape=(o, o, o, o), sem=("parallel",), comm=comm)


def _to_slabs(g, ax):
    shp = g.shape
    g = g.reshape(shp[:ax] + (N_DEV, shp[ax] // N_DEV) + shp[ax + 1:])
    return jnp.moveaxis(g, ax, 0).reshape(N_DEV, -1)


def _from_slabs(s, ax, shp):
    s = s.reshape((N_DEV,) + shp[:ax] + (shp[ax] // N_DEV,) + shp[ax + 1:])
    return jnp.moveaxis(s, 0, ax).reshape(shp)


def _pack_rows(flat, width, row_mult):
    n = flat.shape[-1]
    per = width * row_mult
    tot = -(-n // per) * per
    flat = jnp.pad(flat, [(0, 0)] * (flat.ndim - 1) + [(0, tot - n)])
    return flat.reshape(flat.shape[:-1] + (tot // width, width))


def _offsets(sizes):
    offs, o = [], 0
    for s in sizes:
        offs.append(o)
        o += s
    return offs


WEIGHTS = ['ada_w', 'ada_b', 'norm_mix', 'norm_ffn', 'attn_w_in', 'attn_q_norm_a', 'attn_k_norm_a', 'attn_q_norm_b',
           'attn_k_norm_b', 'attn_sinks', 'attn_w_out', 'rec_w_in', 's5_lambda_re', 's5_lambda_im', 's5_log_dt',
           's5_b_re', 's5_b_im', 's5_c_re', 's5_c_im', 's5_d', 's5_glu_w', 's5_glu_b', 'dn_conv', 'dn_a_log',
           'dn_dt_bias', 'dn_out_norm', 'rec_w_out', 'ffn_w_up', 'ffn_conv', 'ffn_w_down']
BIG = [('attn_w_in', (D, ATTN_IN // N_DEV)), ('attn_w_out', (D // N_DEV, D)), ('rec_w_in', (D // N_DEV, REC_PAD)),
       ('s5_glu_w', (S5_W // N_DEV, S5_W)), ('rec_w_out', (D // N_DEV, D)), ('ffn_w_up', (2 * D, 2 * D_FF // N_DEV)),
       ('ffn_w_down', (2 * D_FF // N_DEV, D))]


def _shard2d(name, t):
    if name == 'rec_w_in':
        return jnp.pad(t[0], ((0, 0), (0, REC_PAD - REC_IN)))
    return t.reshape((-1, t.shape[-1]))


def _cols_to_slabs(g, k=N_DEV):
    r, n = g.shape
    return jnp.transpose(g.reshape(r, k, n // k), (1, 0, 2))


def _slabs_to_cols(s):
    k, r, c_ = s.shape
    return jnp.transpose(s, (1, 0, 2)).reshape(r, k * c_)
SMALL_SHARDED = [('s5_d', 1, (1, S5_W)), ('s5_glu_b', 1, (1, S5_W)), ('dn_conv', 2, (1, 4, 2304)),
                 ('ffn_conv', 2, (2, 3, 2 * D_FF))]
REPLICATED = [('ada_b', (2, 6 * D)), ('norm_mix', (2, D)), ('norm_ffn', (2, D)), ('attn_q_norm_a', (1, HD)),
              ('attn_k_norm_a', (1, HD)), ('attn_q_norm_b', (1, HD)), ('attn_k_norm_b', (1, HD)),
              ('attn_sinks', (1, 8)), ('s5_lambda_re', (1, 16, 64)), ('s5_lambda_im', (1, 16, 64)),
              ('s5_log_dt', (1, 16)), ('s5_b_re', (1, 16, 64, 16)), ('s5_b_im', (1, 16, 64, 16)),
              ('s5_c_re', (1, 16, 16, 64)), ('s5_c_im', (1, 16, 16, 64)), ('dn_a_log', (1, DN_H)),
              ('dn_dt_bias', (1, DN_H)), ('dn_out_norm', (1, 128))]


def _numel(shp):
    return int(np.prod(shp))


def kernel(x, c, ada_w, ada_b, norm_mix, norm_ffn, attn_w_in, attn_q_norm_a, attn_k_norm_a, attn_q_norm_b, attn_k_norm_b, attn_sinks, attn_w_out, rec_w_in, s5_lambda_re, s5_lambda_im, s5_log_dt, s5_b_re, s5_b_im, s5_c_re, s5_c_im, s5_d, s5_glu_w, s5_glu_b, dn_conv, dn_a_log, dn_dt_bias, dn_out_norm, rec_w_out, ffn_w_up, ffn_conv, ffn_w_down, loss_target, m_ada_w, m_ada_b, m_norm_mix, m_norm_ffn, m_attn_w_in, m_attn_q_norm_a, m_attn_k_norm_a, m_attn_q_norm_b, m_attn_k_norm_b, m_attn_sinks, m_attn_w_out, m_rec_w_in, m_s5_lambda_re, m_s5_lambda_im, m_s5_log_dt, m_s5_b_re, m_s5_b_im, m_s5_c_re, m_s5_c_im, m_s5_d, m_s5_glu_w, m_s5_glu_b, m_dn_conv, m_dn_a_log, m_dn_dt_bias, m_dn_out_norm, m_rec_w_out, m_ffn_w_up, m_ffn_conv, m_ffn_w_down, v_ada_w, v_ada_b, v_norm_mix, v_norm_ffn, v_attn_w_in, v_attn_q_norm_a, v_attn_k_norm_a, v_attn_q_norm_b, v_attn_k_norm_b, v_attn_sinks, v_attn_w_out, v_rec_w_in, v_s5_lambda_re, v_s5_lambda_im, v_s5_log_dt, v_s5_b_re, v_s5_b_im, v_s5_c_re, v_s5_c_im, v_s5_d, v_s5_glu_w, v_s5_glu_b, v_dn_conv, v_dn_a_log, v_dn_dt_bias, v_dn_out_norm, v_rec_w_out, v_ffn_w_up, v_ffn_conv, v_ffn_w_down):
    loc = locals()
    W = {n: loc[n] for n in WEIGHTS}
    M = {n: loc["m_" + n] for n in WEIGHTS}
    V = {n: loc["v_" + n] for n in WEIGHTS}
    _, _, _, me = _mesh_pos()
    L = x.shape[1]
    x0, tgt = x[0], loss_target[0]

    small_in = jnp.concatenate([c.reshape(-1)] + [W[n].reshape(-1) for n, _, _ in SMALL_SHARDED])
    si, att_in_all = all_gather_many([_pack_rows(small_in, 1024, 8), attn_w_in[0].astype(BF16)], "gather_first")
    si = si.reshape(N_DEV, -1)
    c_all = si[:, :D]
    off = D
    small_full = {}
    for n, ax, shp in SMALL_SHARDED:
        k = _numel(shp) // N_DEV
        small_full[n] = _from_slabs(si[:, off:off + k], ax, shp)
        off += k

    cond_all = jax.nn.silu(c_all)
    modp = jnp.concatenate([matmul([(cond_all, ada_w[l].astype(BF16))], "nn", f"ada{l}") for l in range(2)], axis=0)
    modp_all = all_gather(modp, "gather_mod")
    mods = []
    for l in range(2):
        row = lax.dynamic_index_in_dim(modp_all, l * N_DEV + me, axis=1, keepdims=False)
        mod = row.reshape(1, 6 * D) + ada_b[l].reshape(1, 6 * D)
        mods.append([mod[:, i * D:(i + 1) * D] for i in range(6)])

    w_att_in = _slabs_to_cols(att_in_all)
    bf = lambda t: t.astype(BF16)
    ffn_shards = [[bf(ffn_w_up[l]), bf(ffn_w_down[l])] for l in range(2)]
    rec_shards = [bf(_shard2d('rec_w_in', rec_w_in)), bf(s5_glu_w[0]), bf(rec_w_out[0])]
    ffn_cw = [small_full['ffn_conv'][l] for l in range(2)]
    dn_cw = small_full['dn_conv'][0]
    s5_dskip, glu_b = small_full['s5_d'], small_full['s5_glu_b']
    row = lambda t: t.reshape(1, -1)

    sh1, sc1, g1, sh2, sc2, g2 = mods[0]
    h1 = gate_norm_fwd(x0, None, None, row(norm_mix[0]), sh1, sc1, "l0_norm1")
    wvec, sinkvec = attn_vectors(attn_q_norm_a[0], attn_k_norm_a[0], attn_q_norm_b[0], attn_k_norm_b[0], attn_sinks[0])
    y0, res_att, got = attention_block_fwd(
        h1, w_att_in, wvec, sinkvec, None, "att",
        comms={'swa': ([ffn_shards[0][0][:D // 2]], True), 1: ([ffn_shards[0][0][D // 2:]], True),
               4: (ffn_shards[0][1:], True), 16: ([bf(attn_w_out[0])], True)})
    w_att_out = got['w_out']
    split_up = lambda up_all: (_slabs_to_cols(up_all[:4]), _slabs_to_cols(up_all[4:]))
    w_up = [split_up(jnp.concatenate([got['swa'][0], got[1][0]], axis=1))]
    w_down = [got[4][0].reshape(D_FF, D)]
    x1, h2 = gate_norm_fwd(x0, y0, g1, row(norm_ffn[0]), sh2, sc2, "l0_norm2")
    f0, res_f0, got_rec = ffn_block_fwd(h2, w_up[0][0], w_up[0][1], ffn_cw[0], w_down[0], "ffn0",
                                        comm=(rec_shards, True))
    w_rec_in = rec_cols_permute(got_rec[0].reshape(D, REC_PAD))
    glu_w, w_rec_out = got_rec[1].reshape(S5_W, S5_W), got_rec[2].reshape(D, D)
    w_rec_out = jnp.concatenate([w_rec_out[S5_W:], w_rec_out[:S5_W]], axis=0)
    t1, tc1, tg1, t2, tc2, tg2 = mods[1]
    x2, h3 = gate_norm_fwd(x1, f0, g2, row(norm_mix[1]), t1, tc1, "l1_norm1")
    rin = matmul([(h3, w_rec_in)], "nn", "rec_in")
    s5p, s5p_vjp = jax.vjp(s5_params, s5_lambda_re[0], s5_lambda_im[0], s5_log_dt[0], s5_b_re[0], s5_b_im[0],
                           s5_c_re[0], s5_c_im[0])
    u = rin[:, REC_U0:REC_A0]
    yc, res_s5 = s5_block_fwd(u, s5p, s5_dskip, glu_w, glu_b, "s5")
    yd, res_dn, got_ffn1 = dn_block_fwd(rin, dn_cw, dn_a_log[0], dn_dt_bias[0], dn_out_norm[0], "dn",
                                        comm=(([ffn_shards[1][1]], True), ([ffn_shards[1][0][:D // 2]], True),
                                              ([ffn_shards[1][0][D // 2:]], True)))
    w_up.append(split_up(jnp.concatenate([got_ffn1[1], got_ffn1[2]], axis=1)))
    w_down.append(got_ffn1[0].reshape(D_FF, D))
    ycat = jnp.concatenate([yd, yc], axis=1)
    y1 = matmul([(ycat, w_rec_out)], "nn", "rec_out")
    x3, h4 = gate_norm_fwd(x2, y1, tg1, row(norm_ffn[1]), t2, tc2, "l1_norm2")
    f1, res_f1, _ = ffn_block_fwd(h4, w_up[1][0], w_up[1][1], ffn_cw[1], w_down[1], "ffn1")
    dx4, df1, lsum = final_loss(x3, f1, tg2, tgt, "loss")

    G = {}
    d_tg2 = lsum[8:16].sum(axis=0)
    dh4, gf1, _ = ffn_block_bwd(df1, res_f1, w_up[1][0], w_up[1][1], ffn_cw[1], w_down[1], "ffn1")
    ffn_slabs = lambda g: [g['w_up'], g['w_down'].reshape(N_DEV, D_FF // N_DEV, D)]
    dx3, dy1, s = gate_norm_bwd(x3, y1, tg1, row(norm_ffn[1]), tc2, dx4, dh4, "l1_dnorm2")
    s = s.reshape(4, 8, D).sum(axis=1)
    d_tg1, d_nffn1, d_t2, d_tc2 = s[0], s[1] * (1.0 + tc2[0]), s[2], s[1] * norm_ffn[1]
    g_rec_out = matmul([(ycat, dy1)], "tn", "rec_out_dw", out_dtype=BF16)
    g_rec_out = jnp.concatenate([g_rec_out[DN_W:], g_rec_out[:DN_W]], axis=0).reshape(N_DEV, D // N_DEV, D)
    dycat = matmul([(dy1, w_rec_out)], "nt", "rec_out_dx")
    du, s5cot, gs5 = s5_block_bwd(dycat, res_s5, s5p, s5_dskip, glu_w, glu_b, "s5", dout_col=DN_W // S5_W)
    s5g = s5p_vjp(s5cot)
    dqkv, dz, da, dbraw, gdn, recv_ffn1 = dn_block_bwd(dycat, res_dn, rin, dn_cw, dn_out_norm[0], "dn",
                                                       comm=(ffn_slabs(gf1), False))
    d_rest = jnp.concatenate([du.astype(BF16), da.astype(BF16), dbraw.astype(BF16),
                              jnp.zeros((L, REC_PAD - REC_IN), BF16)], axis=1)
    drin = ((dqkv, 0), (dz, 3 * DN_W), (d_rest, REC_U0))
    g_rec_in = jnp.concatenate([matmul([(h3, p)], "tn", f"rec_in_dw{i}", out_dtype=BF16)
                                for i, (p, _) in enumerate(drin)], axis=1)
    g_rec_in = rec_cols_restore(g_rec_in).reshape(N_DEV, D // N_DEV, REC_PAD)
    g_glu = gs5['glu_w'].astype(BF16).reshape(N_DEV, S5_W // N_DEV, S5_W)
    dh3 = matmul([(p, w_rec_in[:, c0:c0 + p.shape[1]]) for p, c0 in drin], "nt", "rec_in_dx")
    dx2, df0, s = gate_norm_bwd(x2, f0, g2, row(norm_mix[1]), tc1, dx3, dh3, "l1_dnorm1")
    s = s.reshape(4, 8, D).sum(axis=1)
    d_g2, d_nmix1, d_t1, d_tc1 = s[0], s[1] * (1.0 + tc1[0]), s[2], s[1] * norm_mix[1]
    dh2, gf0, recv_rec = ffn_block_bwd(df0, res_f0, w_up[0][0], w_up[0][1], ffn_cw[0], w_down[0], "ffn0",
                                       comm=([g_rec_in, g_glu, g_rec_out], False))
    dx1, dy0, s = gate_norm_bwd(x1, y0, g1, row(norm_ffn[0]), sc2, dx2, dh2, "l0_dnorm2")
    s = s.reshape(4, 8, D).sum(axis=1)
    d_g1, d_nffn0, d_sh2, d_sc2 = s[0], s[1] * (1.0 + sc2[0]), s[2], s[1] * norm_ffn[0]
    dh1, gatt, got_b = attention_block_bwd(dy0, res_att, w_att_in, wvec, sinkvec, w_att_out, "att",
                                           comms={'swa': ([gf0['w_up'][:, :D // 2]], False),
                                                  16: ([gf0['w_up'][:, D // 2:]], False),
                                                  1: (ffn_slabs(gf0)[1:], False)},
                                           send_w_out_on=4)
    recv_ffn0 = [jnp.concatenate([got_b['swa'][0], got_b[16][0]], axis=1), got_b[1][0]]
    (grad_x, s), recv_w_in = gate_norm_bwd(x0, None, None, row(norm_mix[0]), sc1, dx1, dh1, "l0_dnorm1",
                                           comm=([_cols_to_slabs(gatt['w_in'])], False))
    recv_att = [recv_w_in[0], got_b[4][0]]
    s = s.reshape(4, 8, D).sum(axis=1)
    d_nmix0, d_sh1, d_sc1 = s[1] * (1.0 + sc1[0]), s[2], s[1] * norm_mix[0]
    dmod = jnp.stack([jnp.concatenate([d_sh1, d_sc1, d_g1, d_sh2, d_sc2, d_g2]),
                      jnp.concatenate([d_t1, d_tc1, d_tg1, d_t2, d_tc2, d_tg2])])

    P = {'ada_b': dmod, 'norm_mix': jnp.stack([d_nmix0, d_nmix1]), 'norm_ffn': jnp.stack([d_nffn0, d_nffn1]),
         'attn_q_norm_a': gatt['q_norm_a'], 'attn_k_norm_a': gatt['k_norm_a'], 'attn_q_norm_b': gatt['q_norm_b'],
         'attn_k_norm_b': gatt['k_norm_b'], 'attn_sinks': gatt['sinks'],
         's5_lambda_re': s5g[0], 's5_lambda_im': s5g[1], 's5_log_dt': s5g[2], 's5_b_re': s5g[3], 's5_b_im': s5g[4],
         's5_c_re': s5g[5], 's5_c_im': s5g[6], 'dn_a_log': gdn['a_log'], 'dn_dt_bias': gdn['dt_bias'],
         'dn_out_norm': gdn['out_norm'],
         's5_d': gs5['dskip'], 's5_glu_b': gs5['glu_b'], 'dn_conv': gdn['conv'],
         'ffn_conv': jnp.stack([gf0['conv'], gf1['conv']])}

    out = {k: {} for k in ("g", "d", "m", "v")}
    keys = ("g", "d", "m", "v")
    recv = {'attn_w_in': recv_att[0], 'attn_w_out': recv_att[1], 'rec_w_in': recv_rec[0], 's5_glu_w': recv_rec[1],
            'rec_w_out': recv_rec[2]}
    for n, gr_ in recv.items():
        res4 = reduce_adamw(gr_, _shard2d(n, W[n]), _shard2d(n, M[n]), _shard2d(n, V[n]), "adamw_" + n)
        for key, t in zip(keys, res4):
            out[key][n] = (t[:, :REC_IN] if n == 'rec_w_in' else t).reshape(W[n].shape)
    rep_sizes = [_numel(shp) for _, shp in REPLICATED]
    ss_sizes = [_numel(shp) for _, _, shp in SMALL_SHARDED]
    rep_offs = _offsets(rep_sizes + ss_sizes + [1])
    parts = [P[n].reshape(-1) for n, _ in REPLICATED] + [P[n].reshape(-1) for n, _, _ in SMALL_SHARDED]
    parts.append(lsum[0:8].sum().reshape(1))
    spack = _pack_rows(jnp.concatenate(parts), 1024, 8)
    flat2d = lambda t: t.reshape(-1, t.shape[-1])
    sall = None
    for n, idx in (('ffn_w_up', 0), ('ffn_w_down', 1)):
        comm = ([spack], True) if sall is None else None
        res4, got_s = _with_comm(reduce_adamw([recv_ffn0[idx], recv_ffn1[idx]], flat2d(W[n]), flat2d(M[n]),
                                              flat2d(V[n]), "adamw_" + n, comm=comm), comm)
        if got_s is not None:
            sall = got_s[0]
        for key, t in zip(keys, res4):
            out[key][n] = t.reshape(W[n].shape)
    n_rest = sum(ss_sizes) + 1
    pk = lambda d: _pack_rows(jnp.concatenate([d[n].reshape(-1) for n, _ in REPLICATED]
                                              + [jnp.zeros((n_rest,), F32)]), 1024, 8)
    sg, sd_, sm, sv = [t.reshape(-1) for t in reduce_adamw(sall, pk(W), pk(M), pk(V), "adamw_small")]
    loss = 0.5 * sg[rep_offs[-1]] / D

    dmod_all = sall.reshape(N_DEV, -1)[:, :2 * 6 * D].reshape(N_DEV, 2, 6 * D)
    dmod_mine = lax.dynamic_slice_in_dim(dmod_all, me * (6 * D // N_DEV), 6 * D // N_DEV, axis=2)
    g_ada = [matmul([(cond_all, dmod_mine[:, l])], "tn", f"ada{l}_dw")[None] for l in range(2)]
    ada2d = lambda t: t.reshape(2 * D, 6 * D // N_DEV)
    for key, t in zip(("g", "d", "m", "v"), reduce_adamw(g_ada, ada2d(ada_w), ada2d(m_ada_w),
                                                          ada2d(v_ada_w), "adamw_ada_w")):
        out[key]['ada_w'] = t.reshape(ada_w.shape)
    own = []
    for (n, ax, shp), o in zip(SMALL_SHARDED, rep_offs[len(REPLICATED):]):
        slabs = _to_slabs(sg[o:o + _numel(shp)].reshape(shp), ax)
        own.append(lax.dynamic_index_in_dim(slabs, me, axis=0, keepdims=False))
    own_names = [n for n, _, _ in SMALL_SHARDED]
    pk = lambda d: _pack_rows(jnp.concatenate([d[n].reshape(-1) for n in own_names]), 1024, 8)
    og, od, om, ov = [t.reshape(-1) for t in reduce_adamw(_pack_rows(jnp.concatenate(own), 1024, 8)[None],
                                                          pk(W), pk(M), pk(V), "adamw_own")]

    def unpack(names_shapes, bufs):
        o = 0
        for n, shp in names_shapes:
            k = _numel(shp)
            for key, buf in zip(("g", "d", "m", "v"), bufs):
                out[key][n] = buf[o:o + k].reshape(shp)
            o += k

    unpack(REPLICATED, (sg, sd_, sm, sv))
    unpack([(n, W[n].shape) for n in own_names], (og, od, om, ov))
    return (loss, grad_x[None], *[out["g"][n] for n in WEIGHTS], *[out["d"][n] for n in WEIGHTS],
            *[out["m"][n] for n in WEIGHTS], *[out["v"][n] for n in WEIGHTS])
```

```python
import functools
import math

import numpy as np
import jax
import jax.numpy as jnp
from jax import lax
from jax.experimental import pallas as pl
from jax.experimental.pallas import tpu as pltpu

F32 = jnp.float32
BF16 = jnp.bfloat16

N_DEV = 8
D = 1024
HD = 64
BLK = 128
ATTN_IN = 2304
CB = ATTN_IN // 128
B_BRANCHES = ((128, 1), (512, 4), (2048, 16))
S5_W = 256
S5_P = 1024
DN_H = 6
DN_DK = 128
DN_C = 64
REC_IN = 3340
REC_PAD = 3456
D_FF = 2816
EPS = 1e-6
ADAM_LR, ADAM_B1, ADAM_B2, ADAM_EPS, ADAM_WD, ADAM_STEP = 0.001, 0.9, 0.999, 1e-8, 0.01, 10
VMEM_LIMIT = 48 * 1024 * 1024

ALIBI = np.asarray(2.0 ** (-8.0 * np.arange(1, 17) / 16), dtype=np.float32)


def _cparams(*sem):
    return pltpu.CompilerParams(dimension_semantics=tuple(sem), vmem_limit_bytes=VMEM_LIMIT)


def _tile(n, target):
    if n <= target:
        return n
    best = None
    for t in range(128, target + 1, 128):
        if n % t == 0:
            best = t
    assert best is not None, (n, target)
    return best


def _rtile(n, target, mult=8):
    if n <= target:
        return n
    best = None
    for t in range(mult, target + 1, mult):
        if n % t == 0:
            best = t
    assert best is not None, (n, target)
    return best


def _fold8(x):
    r, c = x.shape
    return x.reshape(r // 8, 8, c).sum(axis=0)


def _sigmoid(x):
    return 1.0 / (1.0 + jnp.exp(-x))


_DIMS = {"nn": (((1,), (0,)), ((), ())), "nt": (((1,), (1,)), ((), ())), "tn": (((0,), (0,)), ((), ()))}


MM_FULL_K = 3584


MM_VMEM_BUDGET = 40 << 20


def matmul(pairs, mode, name, out_dtype=F32, tm=1024, tn=1536, tk=1024):
    a0, b0 = pairs[0]
    if mode == "nn":
        (M, K), N = a0.shape, b0.shape[1]
    elif mode == "nt":
        (M, K), N = a0.shape, b0.shape[0]
    else:
        (K, M), N = a0.shape, b0.shape[1]
        tm = 1536
    tn = _tile(N, tn)
    tk = K if K <= MM_FULL_K else _tile(K, tk)
    nk = K // tk
    npair = len(pairs)
    dims = _DIMS[mode]
    kdim = 0 if mode == "tn" else 1
    tks = [a.shape[kdim] for a, _ in pairs]
    assert all(t == K for t in tks) or (nk == 1 and max(tks) <= MM_FULL_K), tks
    if nk > 1:
        tks = [tk] * npair

    def planned(tm_):
        ab = sum(tm_ * t * a.dtype.itemsize + t * tn * b.dtype.itemsize for (a, b), t in zip(pairs, tks))
        return 2 * ab + 2 * tm_ * tn * jnp.dtype(out_dtype).itemsize + (tm_ * tn * 4 if nk > 1 else 0)

    while True:
        tm_try = _rtile(M, tm) if M % 128 else _tile(M, tm)
        if planned(tm_try) <= MM_VMEM_BUDGET or tm <= 128:
            break
        tm //= 2
    tm = tm_try

    def body(*refs):
        o_ref = refs[2 * npair]
        tot = None
        for p in range(npair):
            part = lax.dot_general(refs[2 * p][...].astype(BF16), refs[2 * p + 1][...].astype(BF16),
                                   dims, preferred_element_type=F32)
            tot = part if tot is None else tot + part
        if nk == 1:
            o_ref[...] = tot.astype(o_ref.dtype)
            return
        acc_ref = refs[2 * npair + 1]
        k = pl.program_id(2)

        @pl.when(k == 0)
        def _():
            acc_ref[...] = tot

        @pl.when(k > 0)
        def _():
            acc_ref[...] += tot

        @pl.when(k == nk - 1)
        def _():
            o_ref[...] = acc_ref[...].astype(o_ref.dtype)

    def specs(t):
        if mode == "nn":
            return [pl.BlockSpec((tm, t), lambda j, i, k: (i, k)), pl.BlockSpec((t, tn), lambda j, i, k: (k, j))]
        if mode == "nt":
            return [pl.BlockSpec((tm, t), lambda j, i, k: (i, k)), pl.BlockSpec((tn, t), lambda j, i, k: (j, k))]
        return [pl.BlockSpec((t, tm), lambda j, i, k: (k, i)), pl.BlockSpec((t, tn), lambda j, i, k: (k, j))]

    flat = [t for pr in pairs for t in pr]
    return pl.pallas_call(
        body, name=name, grid=(N // tn, M // tm, nk),
        in_specs=[s for t in tks for s in specs(t)],
        out_specs=pl.BlockSpec((tm, tn), lambda j, i, k: (i, j)),
        out_shape=jax.ShapeDtypeStruct((M, N), out_dtype),
        scratch_shapes=[pltpu.VMEM((tm, tn), F32)] if nk > 1 else [],
        compiler_params=_cparams("parallel", "parallel", "arbitrary"),
    )(*flat)


def gate_norm_fwd(x, y, gate, nw, sh, sc, name):
    L, C = x.shape
    tl = _rtile(L, 512)
    has_gate = y is not None

    def body(*refs):
        if has_gate:
            x_ref, y_ref, g_ref, nw_ref, sh_ref, sc_ref, xn_ref, h_ref = refs
            xn = x_ref[...] + g_ref[...] * y_ref[...]
            xn_ref[...] = xn
        else:
            x_ref, nw_ref, sh_ref, sc_ref, h_ref = refs
            xn = x_ref[...]
        r = lax.rsqrt(jnp.mean(xn * xn, axis=-1, keepdims=True) + EPS)
        h = (xn * r * nw_ref[...]) * (1.0 + sc_ref[...]) + sh_ref[...]
        h_ref[...] = h.astype(BF16)

    big = pl.BlockSpec((tl, C), lambda i: (i, 0))
    vec = pl.BlockSpec((1, C), lambda i: (0, 0))
    if has_gate:
        ins, in_specs = (x, y, gate, nw, sh, sc), [big, big, vec, vec, vec, vec]
        out_shape = (jax.ShapeDtypeStruct((L, C), F32), jax.ShapeDtypeStruct((L, C), BF16))
        out_specs = (big, big)
    else:
        ins, in_specs = (x, nw, sh, sc), [big, vec, vec, vec]
        out_shape = jax.ShapeDtypeStruct((L, C), BF16)
        out_specs = big
    return pl.pallas_call(body, name=name, grid=(L // tl,), in_specs=in_specs, out_specs=out_specs,
                          out_shape=out_shape, compiler_params=_cparams("parallel"))(*ins)


def gate_norm_bwd(xn, y, gate, nw, sc, dxn_direct, dh, name, comm=None):
    L, C = xn.shape
    tl = _rtile(L, 256)
    has_gate = y is not None
    has_direct = dxn_direct is not None

    def body(*refs):
        refs = list(refs)
        xn_ref = refs.pop(0)
        y_ref = refs.pop(0) if has_gate else None
        g_ref = refs.pop(0) if has_gate else None
        nw_ref = refs.pop(0)
        sc_ref = refs.pop(0)
        dd_ref = refs.pop(0) if has_direct else None
        dh_ref = refs.pop(0)
        dxn_ref = refs.pop(0)
        dy_ref = refs.pop(0) if has_gate else None
        sums_ref = refs.pop(0)

        @pl.when(pl.program_id(0) == 0)
        def _():
            sums_ref[...] = jnp.zeros_like(sums_ref)

        xv = xn_ref[...]
        dh_v = dh_ref[...]
        r = lax.rsqrt(jnp.mean(xv * xv, axis=-1, keepdims=True) + EPS)
        n = xv * r
        a = nw_ref[...] * (1.0 + sc_ref[...])
        dn = dh_v * a
        dx = r * (dn - n * jnp.mean(dn * n, axis=-1, keepdims=True))
        if has_direct:
            dx = dx + dd_ref[...]
        dxn_ref[...] = dx
        sums_ref[8:16, :] += _fold8(dh_v * n)
        sums_ref[16:24, :] += _fold8(dh_v)
        if has_gate:
            dy_ref[...] = (dx * g_ref[...]).astype(BF16)
            sums_ref[0:8, :] += _fold8(dx * y_ref[...])

    big = pl.BlockSpec((tl, C), lambda i: (i, 0))
    vec = pl.BlockSpec((1, C), lambda i: (0, 0))
    ins, in_specs = [xn], [big]
    if has_gate:
        ins += [y, gate]
        in_specs += [big, vec]
    ins += [nw, sc]
    in_specs += [vec, vec]
    if has_direct:
        ins.append(dxn_direct)
        in_specs.append(big)
    ins.append(dh)
    in_specs.append(big)
    out_shape = [jax.ShapeDtypeStruct((L, C), F32)]
    out_specs = [big]
    if has_gate:
        out_shape.append(jax.ShapeDtypeStruct((L, C), BF16))
        out_specs.append(big)
    out_shape.append(jax.ShapeDtypeStruct((32, C), F32))
    out_specs.append(pl.BlockSpec((32, C), lambda i: (0, 0)))
    return _call(body, ins, name=name, grid=(L // tl,), in_specs=in_specs, out_specs=tuple(out_specs),
                 out_shape=tuple(out_shape), sem=("arbitrary",), comm=comm)


def final_loss(x, f, gate, target, name):
    L, C = x.shape
    tl = _rtile(L, 256)

    def body(x_ref, f_ref, g_ref, t_ref, dy_ref, df_ref, sums_ref):
        @pl.when(pl.program_id(0) == 0)
        def _():
            sums_ref[...] = jnp.zeros_like(sums_ref)

        fv = f_ref[...]
        err = x_ref[...] + g_ref[...] * fv - t_ref[...]
        dy = err * (1.0 / C)
        dy_ref[...] = dy
        df_ref[...] = (dy * g_ref[...]).astype(BF16)
        sums_ref[0:8, :] += _fold8(err * err)
        sums_ref[8:16, :] += _fold8(dy * fv)

    big = pl.BlockSpec((tl, C), lambda i: (i, 0))
    vec = pl.BlockSpec((1, C), lambda i: (0, 0))
    return pl.pallas_call(
        body, name=name, grid=(L // tl,), in_specs=[big, big, vec, big],
        out_specs=(big, big, pl.BlockSpec((16, C), lambda i: (0, 0))),
        out_shape=(jax.ShapeDtypeStruct((L, C), F32), jax.ShapeDtypeStruct((L, C), BF16),
                   jax.ShapeDtypeStruct((16, C), F32)),
        compiler_params=_cparams("arbitrary"))(x, f, gate, target)


def _seg_ones(seg):
    r = lax.broadcasted_iota(jnp.int32, (128, 128), 0) // seg
    c = lax.broadcasted_iota(jnp.int32, (128, 128), 1) // seg
    return (r == c).astype(BF16)


def _segsum(t, ones):
    hi = t.astype(BF16)
    lo = (t - hi.astype(F32)).astype(BF16)
    return (jnp.dot(hi, ones, preferred_element_type=F32) + jnp.dot(lo, ones, preferred_element_type=F32))


_NORMED_TILES = tuple(list(range(0, 5)) + list(range(6, 14)))


DIL = (4, 16)
B_COLS0, B_W = 768, 1536
DIL_TL = 256


def _to_dilated(scr_ref, out_ref, d, cast=None):
    nj, tl, _ = scr_ref.shape
    for r in range(d):
        for j in range(nj):
            piece = scr_ref[j, pl.ds(r, tl // d, stride=d), :]
            c0 = (r * nj + j) * 128
            out_ref[:, c0:c0 + 128] = piece if cast is None else piece.astype(cast)


def _from_dilated(in_ref, scr_ref, d):
    nj, tl, _ = scr_ref.shape
    for r in range(d):
        for j in range(nj):
            c0 = (r * nj + j) * 128
            scr_ref[j, pl.ds(r, tl // d, stride=d), :] = in_ref[:, c0:c0 + 128]


def _dil_spec(tl, d, width):
    return pl.BlockSpec((tl // d, d * width), lambda i: (i, 0))


def qknorm_fwd(qkv, wvec, name):
    L, C = qkv.shape
    tl = DIL_TL

    def body(x_ref, w_ref, o_ref, o4_ref, o16_ref, scr_ref):
        ones = _seg_ones(HD)
        for t in range(CB):
            cs = slice(t * 128, (t + 1) * 128)
            x = x_ref[:, cs]
            if t in _NORMED_TILES:
                ms = _segsum(x * x, ones) * (1.0 / HD)
                x = x * lax.rsqrt(ms + EPS) * w_ref[:, cs]
            o_ref[:, cs] = x.astype(BF16)
            if t * 128 >= B_COLS0:
                scr_ref[t - B_COLS0 // 128] = x
        _to_dilated(scr_ref, o4_ref, 4, BF16)
        _to_dilated(scr_ref, o16_ref, 16, BF16)

    return pl.pallas_call(
        body, name=name, grid=(L // tl,),
        in_specs=[pl.BlockSpec((tl, C), lambda i: (i, 0)), pl.BlockSpec((1, C), lambda i: (0, 0))],
        out_specs=(pl.BlockSpec((tl, C), lambda i: (i, 0)), _dil_spec(tl, 4, B_W), _dil_spec(tl, 16, B_W)),
        out_shape=(jax.ShapeDtypeStruct((L, C), BF16), jax.ShapeDtypeStruct((L // 4, 4 * B_W), BF16),
                   jax.ShapeDtypeStruct((L // 16, 16 * B_W), BF16)),
        scratch_shapes=[pltpu.VMEM((B_W // 128, tl, 128), F32)], compiler_params=_cparams("parallel"))(qkv, wvec)


def qknorm_bwd(qkv, wvec, d_a, d_b, name):
    L, C = qkv.shape
    tl = DIL_TL

    def body(x_ref, w_ref, dqa, dka, dva, q1, k1, v1, q4, k4, v4, q16, k16, v16, dx_ref, sums_ref,
             dy_ref, s4_ref, s16_ref):
        @pl.when(pl.program_id(0) == 0)
        def _():
            sums_ref[...] = jnp.zeros_like(sums_ref)

        dy_ref[:, 0:512] = dqa[...]
        for off, ref in ((512, dka), (640, dva)):
            for g in range(2):
                acc = ref[:, g * 256:g * 256 + HD]
                for h in range(1, 4):
                    acc = acc + ref[:, g * 256 + h * HD:g * 256 + (h + 1) * HD]
                dy_ref[:, off + g * HD:off + (g + 1) * HD] = acc
        for off, r1, r4, r16 in ((768, q1, q4, q16), (1280, k1, k4, k16), (1792, v1, v4, v16)):
            _from_dilated(r4, s4_ref, 4)
            _from_dilated(r16, s16_ref, 16)
            for j in range(4):
                dy_ref[:, off + j * 128:off + (j + 1) * 128] = r1[:, j * 128:(j + 1) * 128] + s4_ref[j] + s16_ref[j]

        ones = _seg_ones(HD)
        for t in range(CB):
            cs = slice(t * 128, (t + 1) * 128)
            d = dy_ref[:, cs]
            if t in _NORMED_TILES:
                x = x_ref[:, cs]
                r = lax.rsqrt(_segsum(x * x, ones) * (1.0 / HD) + EPS)
                n = x * r
                dn = d * w_ref[:, cs]
                dx_ref[:, cs] = (r * (dn - n * (_segsum(dn * n, ones) * (1.0 / HD)))).astype(BF16)
                sums_ref[:, cs] += _fold8(d * n)
            else:
                dx_ref[:, cs] = d.astype(BF16)

    big = pl.BlockSpec((tl, C), lambda i: (i, 0))
    p512 = pl.BlockSpec((tl, 512), lambda i: (i, 0))
    return pl.pallas_call(
        body, name=name, grid=(L // tl,),
        in_specs=[big, pl.BlockSpec((1, C), lambda i: (0, 0))] + [p512] * 6 + [_dil_spec(tl, 4, 512)] * 3
        + [_dil_spec(tl, 16, 512)] * 3,
        out_specs=(big, pl.BlockSpec((8, C), lambda i: (0, 0))),
        out_shape=(jax.ShapeDtypeStruct((L, C), BF16), jax.ShapeDtypeStruct((8, C), F32)),
        scratch_shapes=[pltpu.VMEM((tl, C), F32), pltpu.VMEM((4, tl, 128), F32), pltpu.VMEM((4, tl, 128), F32)],
        compiler_params=_cparams("arbitrary"))(qkv, wvec, *d_a, *d_b[0], *d_b[1], *d_b[2])


def _attn_biases(t, slopes, step, maxdist):
    qi = lax.broadcasted_iota(jnp.int32, (BLK, 2 * BLK), 0)
    sj = lax.broadcasted_iota(jnp.int32, (BLK, 2 * BLK), 1)
    dist = BLK + qi - sj
    valid = (dist >= 0) & (dist <= maxdist)
    distf = (step * dist).astype(F32)
    inner = [jnp.where(valid, (-sl) * distf, -jnp.inf) for sl in slopes]
    first = [jnp.where((t > 0) | (sj >= BLK), b, -jnp.inf) for b in inner]
    return inner, first


def _attn_scores(q, kw, bias):
    return lax.dot_general(q, kw, (((1,), (1,)), ((), ())), preferred_element_type=F32) + bias


ATT_NQ_ONE = 16
ATT_NQ = 8


def _attn_operands(nq, hp, gqa, q_ref, kh_ref, kc_ref, vh_ref, vc_ref):
    ops = []
    for b in range(nq):
        rows = slice(b * BLK, (b + 1) * BLK)
        prev = slice((b - 1) * BLK, b * BLK)
        for e in range(2):
            cs = slice(e * HD, (e + 1) * HD)
            if gqa:
                ksel = lambda ref, r: jnp.where(hp >= 2, ref[r, 64:128], ref[r, 0:64])
            else:
                ksel = lambda ref, r, cs=cs: ref[r, cs]
            kprev = ksel(kh_ref, slice(0, BLK)) if b == 0 else ksel(kc_ref, prev)
            vprev = ksel(vh_ref, slice(0, BLK)) if b == 0 else ksel(vc_ref, prev)
            ops.append((b, e, rows, cs, q_ref[rows, cs] * (HD ** -0.5),
                        jnp.concatenate([kprev, ksel(kc_ref, rows)], axis=0),
                        jnp.concatenate([vprev, ksel(vc_ref, rows)], axis=0)))
    return ops


def _attn_specs(cb, q_off, k_off, v_off, gqa):
    kcol = (lambda r, hp: r * cb + k_off) if gqa else (lambda r, hp: r * cb + k_off + hp)
    vcol = (lambda r, hp: r * cb + v_off) if gqa else (lambda r, hp: r * cb + v_off + hp)
    return kcol, vcol


def attn_fwd(X, d, q_off, k_off, v_off, gqa, slope0, maxdist, name, comm=None):
    Ls = X.shape[0]
    nq = min(ATT_NQ, Ls // BLK)
    TQ = nq * BLK
    nt = Ls // TQ
    slopes = jnp.asarray(ALIBI)

    def body(sl_ref, q_ref, kh_ref, kc_ref, vh_ref, vc_ref, o_ref, lse_ref):
        hp, t = pl.program_id(1), pl.program_id(2)
        ops = _attn_operands(nq, hp, gqa, q_ref, kh_ref, kc_ref, vh_ref, vc_ref)
        inner, first = _attn_biases(t, [sl_ref[slope0 + 2 * hp + e] for e in range(2)], d, maxdist)
        s = [_attn_scores(q, kw, first[e] if b == 0 else inner[e]) for (b, e, rows, cs, q, kw, vw) in ops]
        m = [jnp.max(x, axis=-1, keepdims=True) for x in s]
        p = [jnp.exp(x - mm) for x, mm in zip(s, m)]
        l = [jnp.sum(x, axis=-1, keepdims=True) for x in p]
        o = [jnp.dot(x.astype(BF16), op[6], preferred_element_type=F32) / ll for x, op, ll in zip(p, ops, l)]
        for (b, e, rows, cs, q, kw, vw), oo, mm, ll in zip(ops, o, m, l):
            o_ref[rows, cs] = oo
            lse_ref[rows, cs] = jnp.broadcast_to(mm + jnp.log(ll), (BLK, HD))

    cb = X.shape[1] // (d * 128)
    kcol, vcol = _attn_specs(cb, q_off, k_off, v_off, gqa)
    tile, blk = (TQ, 128), (BLK, 128)
    halo = lambda t: jnp.maximum(t * nq - 1, 0)
    in_specs = [
        pl.BlockSpec(memory_space=pltpu.SMEM),
        pl.BlockSpec(tile, lambda r, hp, t: (t, r * cb + q_off + hp)),
        pl.BlockSpec(blk, lambda r, hp, t: (halo(t), kcol(r, hp))),
        pl.BlockSpec(tile, lambda r, hp, t: (t, kcol(r, hp))),
        pl.BlockSpec(blk, lambda r, hp, t: (halo(t), vcol(r, hp))),
        pl.BlockSpec(tile, lambda r, hp, t: (t, vcol(r, hp))),
    ]
    out_spec = pl.BlockSpec(tile, lambda r, hp, t: (t, r * 4 + hp))
    out = jax.ShapeDtypeStruct((Ls, d * 512), F32)
    return _call(body, (slopes, X, X, X, X, X), name=name, grid=(d, 4, nt), in_specs=in_specs,
                 out_specs=(out_spec, out_spec), out_shape=(out, out),
                 sem=("parallel", "parallel", "arbitrary"), comm=comm)


def attn_bwd(X, o, lse, do, dlse, d, q_off, k_off, v_off, gqa, slope0, maxdist, name, comm=None):
    Ls = X.shape[0]
    slopes = jnp.asarray(ALIBI)

    nq = Ls // BLK if Ls // BLK <= ATT_NQ_ONE else ATT_NQ
    TQ = nq * BLK
    nt = Ls // TQ
    nt_dims, tn_dims = (((1,), (1,)), ((), ())), (((0,), (0,)), ((), ()))

    def body(sl_ref, q_ref, kh_ref, kc_ref, vh_ref, vc_ref, o_ref, lse_ref, do_ref, dlse_ref,
             dq_ref, dk_ref, dv_ref, ak_ref, av_ref, pk_ref, pv_ref):
        hp, t = pl.program_id(1), pl.program_id(2)

        @pl.when(t == 0)
        def _():
            pk_ref[...] = jnp.zeros_like(pk_ref)
            pv_ref[...] = jnp.zeros_like(pv_ref)

        @pl.when(t < nt)
        def _():
            ops = _attn_operands(nq, hp, gqa, q_ref, kh_ref, kc_ref, vh_ref, vc_ref)
            inner, first = _attn_biases(t, [sl_ref[slope0 + 2 * hp + e] for e in range(2)], d, maxdist)
            sv = [_attn_scores(q, kw, first[e] if b == 0 else inner[e]) for (b, e, rows, cs, q, kw, vw) in ops]
            p = [jnp.exp(s - lse_ref[op[2], op[1] * HD:op[1] * HD + 1]) for s, op in zip(sv, ops)]
            dov = [do_ref[op[2], op[3]] for op in ops]
            delta = [jnp.sum(dd * o_ref[op[2], op[3]], axis=-1, keepdims=True) for dd, op in zip(dov, ops)]
            dob = [dd.astype(BF16) for dd in dov]
            dp = [lax.dot_general(dd, op[6], nt_dims, preferred_element_type=F32) for dd, op in zip(dob, ops)]
            ds = [(pp * (x - dl + dlse_ref[op[2], op[1] * HD:op[1] * HD + 1])).astype(BF16)
                  for pp, x, dl, op in zip(p, dp, delta, ops)]
            dq = [jnp.dot(x, op[5], preferred_element_type=F32) * (HD ** -0.5) for x, op in zip(ds, ops)]
            dkw = [lax.dot_general(x, op[4], tn_dims, preferred_element_type=F32) for x, op in zip(ds, ops)]
            dvw = [lax.dot_general(pp.astype(BF16), dd, tn_dims, preferred_element_type=F32)
                   for pp, dd in zip(p, dob)]
            ak_ref[...] = jnp.zeros_like(ak_ref)
            av_ref[...] = jnp.zeros_like(av_ref)
            for (b, e, rows, cs, q, kw, vw), x, yk, yv in zip(ops, dq, dkw, dvw):
                dq_ref[rows, cs] = x
                ak_ref[b * BLK:(b + 2) * BLK, cs] += yk
                av_ref[b * BLK:(b + 2) * BLK, cs] += yv
            if nt == 1:
                dk_ref[...] = ak_ref[BLK:, :]
                dv_ref[...] = av_ref[BLK:, :]
                return
            last = slice(TQ - BLK, TQ)
            dk_ref[...] = pk_ref[...]
            dv_ref[...] = pv_ref[...]
            dk_ref[last, :] += ak_ref[0:BLK, :]
            dv_ref[last, :] += av_ref[0:BLK, :]
            pk_ref[...] = ak_ref[BLK:, :]
            pv_ref[...] = av_ref[BLK:, :]

        @pl.when(t == nt)
        def _():
            dk_ref[...] = pk_ref[...]
            dv_ref[...] = pv_ref[...]

    cb = X.shape[1] // (d * 128)
    kcol, vcol = _attn_specs(cb, q_off, k_off, v_off, gqa)
    tile, blk = (TQ, 128), (BLK, 128)
    cur = lambda t: jnp.minimum(t, nt - 1)
    halo = lambda t: jnp.maximum(cur(t) * nq - 1, 0)
    ospec = pl.BlockSpec(tile, lambda r, hp, t: (cur(t), r * 4 + hp))
    in_specs = [
        pl.BlockSpec(memory_space=pltpu.SMEM),
        pl.BlockSpec(tile, lambda r, hp, t: (cur(t), r * cb + q_off + hp)),
        pl.BlockSpec(blk, lambda r, hp, t: (halo(t), kcol(r, hp))),
        pl.BlockSpec(tile, lambda r, hp, t: (cur(t), kcol(r, hp))),
        pl.BlockSpec(blk, lambda r, hp, t: (halo(t), vcol(r, hp))),
        pl.BlockSpec(tile, lambda r, hp, t: (cur(t), vcol(r, hp))),
        ospec, ospec, ospec, ospec,
    ]
    shifted = pl.BlockSpec(tile, lambda r, hp, t: (jnp.maximum(t - 1, 0), r * 4 + hp))
    out = jax.ShapeDtypeStruct((Ls, d * 512), F32)
    return _call(body, (slopes, X, X, X, X, X, o, lse, do, dlse), name=name, grid=(d, 4, nt + 1 if nt > 1 else 1),
                 in_specs=in_specs, out_specs=(ospec, shifted, shifted), out_shape=(out, out, out),
                 scratch_shapes=[pltpu.VMEM((TQ + BLK, 128), F32), pltpu.VMEM((TQ + BLK, 128), F32),
                                 pltpu.VMEM((TQ, 128), F32), pltpu.VMEM((TQ, 128), F32)],
                 sem=("parallel", "parallel", "arbitrary"), comm=comm)


def attn_merge_fwd(oa, la, sink, obs, lbs, name):
    L = oa.shape[0]
    tl = DIL_TL

    def body(oa_ref, la_ref, sk_ref, o1, o4, o16, l1, l4, l16, m_ref, so4, so16, sl4, sl16):
        m_ref[:, 0:512] = (oa_ref[...] * _sigmoid(la_ref[...] - sk_ref[...])).astype(BF16)
        for src, dst, d in ((o4, so4, 4), (o16, so16, 16), (l4, sl4, 4), (l16, sl16, 16)):
            _from_dilated(src, dst, d)
        for j in range(4):
            cs = slice(j * 128, (j + 1) * 128)
            a, b, c = l1[:, cs], sl4[j], sl16[j]
            mx = jnp.maximum(jnp.maximum(a, b), c)
            ea, eb, ec = jnp.exp(a - mx), jnp.exp(b - mx), jnp.exp(c - mx)
            inv = 1.0 / (ea + eb + ec)
            m_ref[:, 512 + j * 128:512 + (j + 1) * 128] = (
                (ea * inv) * o1[:, cs] + (eb * inv) * so4[j] + (ec * inv) * so16[j]).astype(BF16)

    big = pl.BlockSpec((tl, 512), lambda i: (i, 0))
    dil = [big, _dil_spec(tl, 4, 512), _dil_spec(tl, 16, 512)]
    return pl.pallas_call(
        body, name=name, grid=(L // tl,),
        in_specs=[big, big, pl.BlockSpec((1, 512), lambda i: (0, 0))] + dil + dil,
        out_specs=pl.BlockSpec((tl, 1024), lambda i: (i, 0)),
        out_shape=jax.ShapeDtypeStruct((L, 1024), BF16), scratch_shapes=[pltpu.VMEM((4, tl, 128), F32)] * 4,
        compiler_params=_cparams("parallel"),
    )(oa, la, sink, *obs, *lbs)


def attn_merge_bwd(dm, oa, la, sink, obs, lbs, name):
    L = oa.shape[0]
    tl = DIL_TL

    def body(dm_ref, oa_ref, la_ref, sk_ref, o1, o4, o16, l1, l4, l16,
             doa_ref, dla_ref, d1, d4, d16, g1, g4, g16, sums_ref, so4, so16, sl4, sl16, sd4, sd16, sg4, sg16):
        @pl.when(pl.program_id(0) == 0)
        def _():
            sums_ref[...] = jnp.zeros_like(sums_ref)

        for src, dst, d in ((o4, so4, 4), (o16, so16, 16), (l4, sl4, 4), (l16, sl16, 16)):
            _from_dilated(src, dst, d)
        ones = _seg_ones(HD)
        for t in range(4):
            cs = slice(t * 128, (t + 1) * 128)
            dma = dm_ref[:, cs]
            keep = _sigmoid(la_ref[:, cs] - sk_ref[:, cs])
            doa_ref[:, cs] = dma * keep
            tt = dma * oa_ref[:, cs] * keep * (1.0 - keep)
            dla_ref[:, cs] = _segsum(tt, ones)
            sums_ref[:, cs] += _fold8(-tt)
            dmb = dm_ref[:, 512 + t * 128:512 + (t + 1) * 128]
            a, b, c = l1[:, cs], sl4[t], sl16[t]
            mx = jnp.maximum(jnp.maximum(a, b), c)
            ea, eb, ec = jnp.exp(a - mx), jnp.exp(b - mx), jnp.exp(c - mx)
            inv = 1.0 / (ea + eb + ec)
            wa, wb, wc = ea * inv, eb * inv, ec * inv
            d1[:, cs] = wa * dmb
            sd4[t] = wb * dmb
            sd16[t] = wc * dmb
            sa = _segsum(dmb * o1[:, cs], ones)
            sb = _segsum(dmb * so4[t], ones)
            sc_ = _segsum(dmb * so16[t], ones)
            mean = wa * sa + wb * sb + wc * sc_
            g1[:, cs] = wa * (sa - mean)
            sg4[t] = wb * (sb - mean)
            sg16[t] = wc * (sc_ - mean)
        for src, dst, d in ((sd4, d4, 4), (sd16, d16, 16), (sg4, g4, 4), (sg16, g16, 16)):
            _to_dilated(src, dst, d)

    big = pl.BlockSpec((tl, 512), lambda i: (i, 0))
    dil = [big, _dil_spec(tl, 4, 512), _dil_spec(tl, 16, 512)]
    sd = jax.ShapeDtypeStruct
    shp = [sd((L, 512), F32), sd((L // 4, 4 * 512), F32), sd((L // 16, 16 * 512), F32)]
    return pl.pallas_call(
        body, name=name, grid=(L // tl,),
        in_specs=[pl.BlockSpec((tl, 1024), lambda i: (i, 0)), big, big,
                  pl.BlockSpec((1, 512), lambda i: (0, 0))] + dil + dil,
        out_specs=tuple([big, big] + dil + dil + [pl.BlockSpec((8, 512), lambda i: (0, 0))]),
        out_shape=tuple([shp[0], shp[0]] + shp + shp + [sd((8, 512), F32)]),
        scratch_shapes=[pltpu.VMEM((4, tl, 128), F32)] * 8, compiler_params=_cparams("arbitrary"),
    )(dm, oa, la, sink, *obs, *lbs)


def _shift_down(x, halo, k, first):
    rows = lax.broadcasted_iota(jnp.int32, (8, x.shape[1]), 0)
    out = pltpu.roll(x, k, axis=0)
    hrows = jnp.where(first, 0.0, pltpu.roll(halo, k, axis=0))
    top = jnp.where(rows < k, hrows, out[0:8, :])
    return jnp.concatenate([top, out[8:, :]], axis=0)


def _shift_up(x, nxt, k):
    tl = x.shape[0]
    rows = lax.broadcasted_iota(jnp.int32, (8, x.shape[1]), 0)
    out = pltpu.roll(x, tl - k, axis=0)
    bottom = jnp.where(rows >= 8 - k, pltpu.roll(nxt, 8 - k, axis=0), out[tl - 8:, :])
    return jnp.concatenate([out[:tl - 8, :], bottom], axis=0)


def _silu(x):
    return x * _sigmoid(x)


def _dsilu(x):
    s = _sigmoid(x)
    return s * (1.0 + x * (1.0 - s))


def ffn_act_fwd(ua, ub, cw, name, comm=None):
    L, F = ua.shape
    tl = _rtile(L, 256)
    tc = _tile(F, 1408)
    hb = tl // 8

    def body(ua_ref, uah_ref, ub_ref, ubh_ref, wa_ref, wb_ref, o_ref, ac_ref, bc_ref):
        first = pl.program_id(1) == 0

        def conv(x_ref, h_ref, w_ref):
            x = x_ref[...]
            h = h_ref[...]
            return (w_ref[2:3, :] * x + w_ref[1:2, :] * _shift_down(x, h, 1, first)
                    + w_ref[0:1, :] * _shift_down(x, h, 2, first))

        a = conv(ua_ref, uah_ref, wa_ref)
        b = conv(ub_ref, ubh_ref, wb_ref)
        ac_ref[...] = a
        bc_ref[...] = b
        o_ref[...] = (_silu(a) * b).astype(BF16)

    main = pl.BlockSpec((tl, tc), lambda j, i: (i, j))
    halo = pl.BlockSpec((8, tc), lambda j, i: (jnp.maximum(i * hb - 1, 0), j))
    wa = pl.BlockSpec((3, tc), lambda j, i: (0, j))
    wb = pl.BlockSpec((3, tc), lambda j, i: (0, j + F // tc))
    f32 = jax.ShapeDtypeStruct((L, F), F32)
    return _call(body, (ua, ua, ub, ub, cw, cw), name=name, grid=(F // tc, L // tl),
                 in_specs=[main, halo, main, halo, wa, wb], out_specs=(main, main, main),
                 out_shape=(jax.ShapeDtypeStruct((L, F), BF16), f32, f32), sem=("parallel", "parallel"), comm=comm)


def ffn_act_bwd(ua, ub, ac, bc, cw, dact, name, comm=None):
    L, F = ua.shape
    tl = _rtile(L, 256)
    tc = _tile(F, 1408)
    nrt = L // tl

    def body(ua_ref, ub_ref, ac_ref, bc_ref, wa_ref, wb_ref, da_ref, dua_ref, dub_ref, sums_ref, ca_ref, cb_ref):
        i = pl.program_id(1)

        @pl.when(i == 0)
        def _():
            sums_ref[...] = jnp.zeros_like(sums_ref)
            ca_ref[...] = jnp.zeros_like(ca_ref)
            cb_ref[...] = jnp.zeros_like(cb_ref)

        a, b = ac_ref[...], bc_ref[...]
        dact_v = da_ref[...]
        dya = dact_v * b * _dsilu(a)
        dyb = dact_v * _silu(a)
        for (dy, w_ref, c_ref, d_ref, x_ref, base) in ((dya, wa_ref, ca_ref, dua_ref, ua_ref, 0),
                                                        (dyb, wb_ref, cb_ref, dub_ref, ub_ref, 24)):
            nxt = c_ref[...]
            ups = (dy, _shift_up(dy, nxt, 1), _shift_up(dy, nxt, 2))
            d_ref[...] = (w_ref[2:3, :] * ups[0] + w_ref[1:2, :] * ups[1] + w_ref[0:1, :] * ups[2]).astype(BF16)
            c_ref[...] = dy[0:8, :]
            x = x_ref[...]
            for k in range(3):
                sums_ref[base + 8 * (2 - k):base + 8 * (2 - k) + 8, :] += _fold8(ups[k] * x)

    rev = lambda i: nrt - 1 - i
    main = pl.BlockSpec((tl, tc), lambda j, i: (rev(i), j))
    wa = pl.BlockSpec((3, tc), lambda j, i: (0, j))
    wb = pl.BlockSpec((3, tc), lambda j, i: (0, j + F // tc))
    ob = jax.ShapeDtypeStruct((L, F), BF16)
    return _call(body, (ua, ub, ac, bc, cw, cw, dact), name=name, grid=(F // tc, nrt),
                 in_specs=[main, main, main, main, wa, wb, main],
                 out_specs=(main, main, pl.BlockSpec((48, tc), lambda j, i: (0, j))),
                 out_shape=(ob, ob, jax.ShapeDtypeStruct((48, F), F32)),
                 scratch_shapes=[pltpu.VMEM((8, tc), F32), pltpu.VMEM((8, tc), F32)],
                 sem=("parallel", "arbitrary"), comm=comm)


def attn_vectors(qna, kna, qnb, knb, sinks):
    ones = jnp.ones((128,), F32)
    wvec = jnp.concatenate([jnp.tile(qna, 8), jnp.tile(kna, 2), ones, jnp.tile(qnb, 8), jnp.tile(knb, 8),
                            jnp.tile(ones, 4)]).reshape(1, ATTN_IN)
    return wvec, jnp.repeat(sinks, HD).reshape(1, 512)


def _with_comm(result, comm):
    return result if comm is not None else (result, None)


def attention_block_fwd(h, w_in, wvec, sinkvec, w_out, tag, comms=None):
    L = h.shape[0]
    comms = comms or {}
    got = {}
    qkv = matmul([(h, w_in)], "nn", tag + "_qkv")
    X, X4, X16 = qknorm_fwd(qkv, wvec, tag + "_qknorm")
    (oa, la), got['swa'] = _with_comm(attn_fwd(X, 1, 0, 4, 5, True, 0, BLK - 1, tag + "_swa",
                                               comm=comms.get('swa')), comms.get('swa'))
    views = {1: (X, 6, 10, 14), 4: (X4, 0, 4, 8), 16: (X16, 0, 4, 8)}
    obs, lbs = [], []
    for window, d in B_BRANCHES:
        xd, qo, ko, vo = views[d]
        (o, l), got[d] = _with_comm(attn_fwd(xd, d, qo, ko, vo, False, 8, window // d,
                                             tag + f"_dil{d}", comm=comms.get(d)), comms.get(d))
        obs.append(o)
        lbs.append(l)
    m = attn_merge_fwd(oa, la, sinkvec, obs, lbs, tag + "_merge")
    if w_out is None:
        w_out = got[16][0].reshape(D, D)
        got['w_out'] = w_out
    y = matmul([(m, w_out)], "nn", tag + "_out")
    return y, (h, qkv, views, oa, la, obs, lbs, m), got


def attention_block_bwd(dy, res, w_in, wvec, sinkvec, w_out, tag, comms=None, send_w_out_on=None):
    h, qkv, views, oa, la, obs, lbs, m = res
    comms = dict(comms or {})
    got = {}
    g_w_out = matmul([(m, dy)], "tn", tag + "_dwout", out_dtype=BF16)
    if send_w_out_on is not None:
        comms[send_w_out_on] = ([g_w_out.reshape(N_DEV, D // N_DEV, D)], False)
    dm = matmul([(dy, w_out)], "nt", tag + "_dm")
    doa, dla, d1, d2, d3, g1, g2, g3, sinksums = attn_merge_bwd(dm, oa, la, sinkvec, obs, lbs, tag + "_dmerge")
    d_a, got['swa'] = _with_comm(attn_bwd(views[1][0], oa, la, doa, dla, 1, 0, 4, 5, True, 0, BLK - 1,
                                          tag + "_dswa", comm=comms.get('swa')), comms.get('swa'))
    d_b = []
    for (window, d), o, l, do, dl in zip(B_BRANCHES, obs, lbs, (d1, d2, d3), (g1, g2, g3)):
        xd, qo, ko, vo = views[d]
        dqkv_d, got[d] = _with_comm(attn_bwd(xd, o, l, do, dl, d, qo, ko, vo, False, 8, window // d,
                                             tag + f"_ddil{d}", comm=comms.get(d)), comms.get(d))
        d_b.append(dqkv_d)
    dqkv, wsums = qknorm_bwd(qkv, wvec, d_a, d_b, tag + "_dqknorm")
    g_w_in = matmul([(h, dqkv)], "tn", tag + "_dwin", out_dtype=BF16)
    dh = matmul([(dqkv, w_in)], "nt", tag + "_dh")
    ws = wsums.sum(axis=0)
    grads = dict(
        w_in=g_w_in, w_out=g_w_out,
        q_norm_a=ws[0:512].reshape(8, HD).sum(axis=0), k_norm_a=ws[512:640].reshape(2, HD).sum(axis=0),
        q_norm_b=ws[768:1280].reshape(8, HD).sum(axis=0), k_norm_b=ws[1280:1792].reshape(8, HD).sum(axis=0),
        sinks=sinksums.sum(axis=0).reshape(8, HD).sum(axis=1))
    return dh, grads, got


def ffn_block_fwd(h, w_up_a, w_up_b, cw, w_down, tag, comm=None):
    ua = matmul([(h, w_up_a)], "nn", tag + "_upa")
    ub = matmul([(h, w_up_b)], "nn", tag + "_upb")
    (act, ac, bc), got = _with_comm(ffn_act_fwd(ua, ub, cw, tag + "_act", comm=comm), comm)
    f = matmul([(act, w_down)], "nn", tag + "_down")
    return f, (h, ua, ub, ac, bc, act), got


def ffn_block_bwd(df, res, w_up_a, w_up_b, cw, w_down, tag, comm=None):
    h, ua, ub, ac, bc, act = res
    g_down = matmul([(act, df)], "tn", tag + "_dwdown", out_dtype=BF16)
    dact = matmul([(df, w_down)], "nt", tag + "_dact")
    (dua, dub, sums), got = _with_comm(ffn_act_bwd(ua, ub, ac, bc, cw, dact, tag + "_dactk", comm=comm), comm)
    g_up = jnp.concatenate([_cols_to_slabs(matmul([(h, dua)], "tn", tag + "_dwupa", out_dtype=BF16), N_DEV // 2),
                            _cols_to_slabs(matmul([(h, dub)], "tn", tag + "_dwupb", out_dtype=BF16), N_DEV // 2)],
                           axis=0)
    dh = matmul([(dua, w_up_a), (dub, w_up_b)], "nt", tag + "_dh")
    s = sums.reshape(2, 3, 8, D_FF).sum(axis=2)
    g_conv = jnp.concatenate([s[0], s[1]], axis=1)
    return dh, dict(w_up=g_up, conv=g_conv, w_down=g_down), got


def s5_params(lam_re, lam_im, log_dt, b_re, b_im, c_re, c_im):
    dt = jnp.exp(log_dt)[:, None]
    mag, ang = jnp.exp(lam_re * dt), lam_im * dt
    a_re, a_im = mag * jnp.cos(ang), mag * jnp.sin(ang)
    nr, ni = a_re - 1.0, a_im
    den = lam_re * lam_re + lam_im * lam_im
    f_re = (nr * lam_re + ni * lam_im) / den
    f_im = (ni * lam_re - nr * lam_im) / den
    eye = jnp.eye(16, dtype=F32)[:, None, :, None]
    bd = lambda b: (eye * jnp.transpose(b, (0, 2, 1))[:, :, None, :]).reshape(S5_W, S5_P)
    cd = lambda c: (eye * jnp.transpose(c, (0, 2, 1))[:, :, None, :]).reshape(S5_P, S5_W)
    flat = lambda t: t.reshape(1, S5_P)
    return flat(a_re), flat(a_im), flat(f_re), flat(f_im), bd(b_re), bd(b_im), cd(c_re), cd(c_im)


def _scan_tables(a_re, a_im, reverse):
    pows = [(a_re, a_im)]
    for _ in range(7):
        pr, pi = pows[-1]
        pows.append((pr * a_re - pi * a_im, pr * a_im + pi * a_re))
    order = list(range(7, -1, -1)) if reverse else list(range(8))
    rid = jnp.arange(8)[:, None]
    rows = []
    for s in (1, 2, 4):
        keep = (rid < 8 - s) if reverse else (rid >= s)
        rows += [jnp.where(keep, pows[s - 1][0], 0.0), jnp.where(keep, pows[s - 1][1], 0.0)]
    rows += [jnp.concatenate([pows[k][0] for k in order], axis=0), jnp.concatenate([pows[k][1] for k in order], axis=0)]
    return jnp.concatenate(rows, axis=0)


def _block_scan(er, ei, tab_ref, cr, ci, reverse):
    for idx, s in enumerate((1, 2, 4)):
        shift = 8 - s if reverse else s
        sr, si = pltpu.roll(er, shift, axis=0), pltpu.roll(ei, shift, axis=0)
        ar, ai = tab_ref[16 * idx:16 * idx + 8, :], tab_ref[16 * idx + 8:16 * idx + 16, :]
        er, ei = er + ar * sr - ai * si, ei + ar * si + ai * sr
    pr, pi_ = tab_ref[48:56, :], tab_ref[56:64, :]
    er, ei = er + pr * cr - pi_ * ci, ei + pr * ci + pi_ * cr
    return er, ei


def s5_scan_fwd(bu_re, bu_im, a_re, a_im, f_re, f_im, name):
    L, P = bu_re.shape
    tl = _rtile(L, 512)
    tab = _scan_tables(a_re, a_im, False)
    fvec = jnp.concatenate([f_re, f_im] + [jnp.zeros_like(f_re)] * 6, axis=0)

    def body(br_ref, bi_ref, tab_ref, f_ref, xr_ref, xi_ref, c_ref):
        @pl.when(pl.program_id(0) == 0)
        def _():
            c_ref[...] = jnp.zeros_like(c_ref)

        def blk(i, carry):
            cr, ci = carry
            rows = pl.ds(pl.multiple_of(i * 8, 8), 8)
            br, bi = br_ref[rows, :], bi_ref[rows, :]
            fr, fi = f_ref[0:1, :], f_ref[1:2, :]
            er, ei = _block_scan(fr * br - fi * bi, fr * bi + fi * br, tab_ref, cr, ci, False)
            xr_ref[rows, :] = er
            xi_ref[rows, :] = ei
            return er[7:8, :], ei[7:8, :]

        cr, ci = lax.fori_loop(0, tl // 8, blk, (c_ref[0:1, :], c_ref[1:2, :]))
        c_ref[0:1, :] = cr
        c_ref[1:2, :] = ci

    big = pl.BlockSpec((tl, P), lambda i: (i, 0))
    out = jax.ShapeDtypeStruct((L, P), F32)
    return pl.pallas_call(
        body, name=name, grid=(L // tl,),
        in_specs=[big, big, pl.BlockSpec((64, P), lambda i: (0, 0)), pl.BlockSpec((8, P), lambda i: (0, 0))],
        out_specs=(big, big), out_shape=(out, out), scratch_shapes=[pltpu.VMEM((8, P), F32)],
        compiler_params=_cparams("arbitrary"))(bu_re, bu_im, tab, fvec)


def s5_scan_bwd(dx_re, dx_im, x_re, x_im, bu_re, bu_im, a_re, a_im, f_re, f_im, name):
    L, P = dx_re.shape
    tl = _rtile(L, 256)
    nt = L // tl
    tab = _scan_tables(a_re, -a_im, True)
    fvec = jnp.concatenate([f_re, f_im] + [jnp.zeros_like(f_re)] * 6, axis=0)

    def body(gr_ref, gi_ref, xr_ref, xi_ref, br_ref, bi_ref, tab_ref, f_ref, dbr_ref, dbi_ref, s_ref, c_ref):
        @pl.when(pl.program_id(0) == 0)
        def _():
            c_ref[...] = jnp.zeros_like(c_ref)
            s_ref[...] = jnp.zeros_like(s_ref)

        def blk(k, carry):
            cr, ci = carry
            i = tl // 8 - 1 - k
            rows = pl.ds(pl.multiple_of(i * 8, 8), 8)
            er, ei = _block_scan(gr_ref[rows, :], gi_ref[rows, :], tab_ref, cr, ci, True)
            rid = lax.broadcasted_iota(jnp.int32, er.shape, 0)
            sr = jnp.where(rid == 7, cr, pltpu.roll(er, 7, axis=0))
            si = jnp.where(rid == 7, ci, pltpu.roll(ei, 7, axis=0))
            xr, xi = xr_ref[rows, :], xi_ref[rows, :]
            s_ref[0:8, :] += sr * xr + si * xi
            s_ref[8:16, :] += si * xr - sr * xi
            br, bi = br_ref[rows, :], bi_ref[rows, :]
            s_ref[16:24, :] += er * br + ei * bi
            s_ref[24:32, :] += ei * br - er * bi
            fr, fi = f_ref[0:1, :], f_ref[1:2, :]
            dbr_ref[rows, :] = fr * er + fi * ei
            dbi_ref[rows, :] = fr * ei - fi * er
            return er[0:1, :], ei[0:1, :]

        cr, ci = lax.fori_loop(0, tl // 8, blk, (c_ref[0:1, :], c_ref[1:2, :]))
        c_ref[0:1, :] = cr
        c_ref[1:2, :] = ci

    big = pl.BlockSpec((tl, P), lambda i: (nt - 1 - i, 0))
    out = jax.ShapeDtypeStruct((L, P), F32)
    return pl.pallas_call(
        body, name=name, grid=(nt,),
        in_specs=[big] * 6 + [pl.BlockSpec((64, P), lambda i: (0, 0)), pl.BlockSpec((8, P), lambda i: (0, 0))],
        out_specs=(big, big, pl.BlockSpec((32, P), lambda i: (0, 0))),
        out_shape=(out, out, jax.ShapeDtypeStruct((32, P), F32)), scratch_shapes=[pltpu.VMEM((8, P), F32)],
        compiler_params=_cparams("arbitrary"))(dx_re, dx_im, x_re, x_im, bu_re, bu_im, tab, fvec)


_GK, _GC = math.sqrt(2.0 / math.pi), 0.044715


def _gelu(y):
    return 0.5 * y * (1.0 + jnp.tanh(_GK * (y + _GC * y * y * y)))


def _dgelu(y):
    t = jnp.tanh(_GK * (y + _GC * y * y * y))
    return 0.5 * (1.0 + t) + 0.5 * y * (1.0 - t * t) * _GK * (1.0 + 3.0 * _GC * y * y)


def s5_out_fwd(x_re, x_im, u, cd_re, cd_im, dskip, glu_w, glu_b, name):
    L = u.shape[0]
    tl = _rtile(L, 512)

    def body(xr_ref, xi_ref, u_ref, cr_ref, ci_ref, d_ref, w_ref, b_ref, y_ref, o_ref):
        y = (jnp.dot(xr_ref[...].astype(BF16), cr_ref[...], preferred_element_type=F32)
             - jnp.dot(xi_ref[...].astype(BF16), ci_ref[...], preferred_element_type=F32)
             + d_ref[...] * u_ref[...])
        y_ref[...] = y
        g = _gelu(y)
        z = jnp.dot(g.astype(BF16), w_ref[...], preferred_element_type=F32) + b_ref[...]
        o_ref[...] = (g * _sigmoid(z)).astype(BF16)

    big = pl.BlockSpec((tl, S5_P), lambda i: (i, 0))
    sm = pl.BlockSpec((tl, S5_W), lambda i: (i, 0))
    full = lambda r, c: pl.BlockSpec((r, c), lambda i: (0, 0))
    return pl.pallas_call(
        body, name=name, grid=(L // tl,),
        in_specs=[big, big, sm, full(S5_P, S5_W), full(S5_P, S5_W), full(1, S5_W), full(S5_W, S5_W), full(1, S5_W)],
        out_specs=(sm, sm),
        out_shape=(jax.ShapeDtypeStruct((L, S5_W), F32), jax.ShapeDtypeStruct((L, S5_W), BF16)),
        compiler_params=_cparams("parallel"))(x_re, x_im, u, cd_re, cd_im, dskip, glu_w, glu_b)


def s5_out_bwd(dout, y, u, x_re, x_im, cd_re, cd_im, dskip, glu_w, glu_b, name, dout_col=0):
    L = u.shape[0]
    tl = _rtile(L, 256)
    nt_dims = (((1,), (1,)), ((), ()))
    tn_dims = (((0,), (0,)), ((), ()))

    def body(do_ref, y_ref, u_ref, xr_ref, xi_ref, cr_ref, ci_ref, d_ref, w_ref, b_ref,
             dxr_ref, dxi_ref, du_ref, dcr_ref, dci_ref, dw_ref, s_ref):
        @pl.when(pl.program_id(0) == 0)
        def _():
            dcr_ref[...] = jnp.zeros_like(dcr_ref)
            dci_ref[...] = jnp.zeros_like(dci_ref)
            dw_ref[...] = jnp.zeros_like(dw_ref)
            s_ref[...] = jnp.zeros_like(s_ref)

        yv, dov = y_ref[...], do_ref[...]
        g = _gelu(yv)
        gb = g.astype(BF16)
        sg = _sigmoid(jnp.dot(gb, w_ref[...], preferred_element_type=F32) + b_ref[...])
        dz = dov * g * sg * (1.0 - sg)
        dzb = dz.astype(BF16)
        dg = dov * sg + lax.dot_general(dzb, w_ref[...], nt_dims, preferred_element_type=F32)
        dw_ref[...] += lax.dot_general(gb, dzb, tn_dims, preferred_element_type=F32)
        dy = dg * _dgelu(yv)
        dyb = dy.astype(BF16)
        s_ref[0:8, :] += _fold8(dy * u_ref[...])
        s_ref[8:16, :] += _fold8(dz)
        du_ref[...] = dy * d_ref[...]
        dxr_ref[...] = lax.dot_general(dyb, cr_ref[...], nt_dims, preferred_element_type=F32)
        dxi_ref[...] = -lax.dot_general(dyb, ci_ref[...], nt_dims, preferred_element_type=F32)
        dcr_ref[...] += lax.dot_general(xr_ref[...].astype(BF16), dyb, tn_dims, preferred_element_type=F32)
        dci_ref[...] -= lax.dot_general(xi_ref[...].astype(BF16), dyb, tn_dims, preferred_element_type=F32)

    big = pl.BlockSpec((tl, S5_P), lambda i: (i, 0))
    sm = pl.BlockSpec((tl, S5_W), lambda i: (i, 0))
    full = lambda r, c: pl.BlockSpec((r, c), lambda i: (0, 0))
    sd = jax.ShapeDtypeStruct
    return pl.pallas_call(
        body, name=name, grid=(L // tl,),
        in_specs=[pl.BlockSpec((tl, S5_W), lambda i: (i, dout_col)), sm, sm, big, big, full(S5_P, S5_W),
                  full(S5_P, S5_W), full(1, S5_W), full(S5_W, S5_W), full(1, S5_W)],
        out_specs=(big, big, sm, full(S5_P, S5_W), full(S5_P, S5_W), full(S5_W, S5_W), full(16, S5_W)),
        out_shape=(sd((L, S5_P), F32), sd((L, S5_P), F32), sd((L, S5_W), F32), sd((S5_P, S5_W), F32),
                   sd((S5_P, S5_W), F32), sd((S5_W, S5_W), F32), sd((16, S5_W), F32)),
        compiler_params=_cparams("arbitrary"))(dout, y, u, x_re, x_im, cd_re, cd_im, dskip, glu_w, glu_b)


def s5_block_fwd(u, params, dskip, glu_w, glu_b, tag):
    a_re, a_im, f_re, f_im, bd_re, bd_im, cd_re, cd_im = params
    bu_re = matmul([(u, bd_re.astype(BF16))], "nn", tag + "_bure")
    bu_im = matmul([(u, bd_im.astype(BF16))], "nn", tag + "_buim")
    x_re, x_im = s5_scan_fwd(bu_re, bu_im, a_re, a_im, f_re, f_im, tag + "_scan")
    y, out = s5_out_fwd(x_re, x_im, u, cd_re.astype(BF16), cd_im.astype(BF16), dskip, glu_w, glu_b, tag + "_out")
    return out, (u, bu_re, bu_im, x_re, x_im, y)


def s5_block_bwd(dout, res, params, dskip, glu_w, glu_b, tag, dout_col=0):
    u, bu_re, bu_im, x_re, x_im, y = res
    a_re, a_im, f_re, f_im, bd_re, bd_im, cd_re, cd_im = params
    dxr, dxi, du, dcr, dci, dglu_w, sums = s5_out_bwd(dout, y, u, x_re, x_im, cd_re.astype(BF16), cd_im.astype(BF16),
                                                      dskip, glu_w, glu_b, tag + "_dout", dout_col=dout_col)
    dbr, dbi, acc = s5_scan_bwd(dxr, dxi, x_re, x_im, bu_re, bu_im, a_re, a_im, f_re, f_im, tag + "_dscan")
    du = du + matmul([(dbr, bd_re.astype(BF16)), (dbi, bd_im.astype(BF16))], "nt", tag + "_du")
    dbd_re = matmul([(u, dbr)], "tn", tag + "_dbdre")
    dbd_im = matmul([(u, dbi)], "tn", tag + "_dbdim")
    acc = acc.reshape(4, 8, S5_P).sum(axis=1)
    s = sums.reshape(2, 8, S5_W).sum(axis=1)
    cot = (acc[0:1], acc[1:2], acc[2:3], acc[3:4], dbd_re, dbd_im, dcr, dci)
    return du, cot, dict(dskip=s[0], glu_w=dglu_w, glu_b=s[1])


DN_Z0, DN_NT = 18, 18
REC_U0, REC_A0 = 3072, 3328


def rec_cols_permute(w):
    return jnp.concatenate([w[..., S5_W:REC_A0], w[..., :S5_W], w[..., REC_A0:]], axis=-1)


def rec_cols_restore(w):
    return jnp.concatenate([w[..., REC_U0:REC_A0], w[..., :REC_U0], w[..., REC_A0:]], axis=-1)


DN_W = DN_H * DN_DK
DN_NI = 4


def _dn_conv4(taps, w_ref):
    xc = w_ref[3:4, :] * taps[0]
    for k in range(1, 4):
        xc = xc + w_ref[3 - k:4 - k, :] * taps[k]
    return xc


def dn_prep_fwd(rin, cw, name, comm=None):
    L = rin.shape[0]
    tl = _rtile(L, 256)
    hb = tl // 8

    def body(x_ref, h_ref, w_ref, o_ref):
        j = pl.program_id(0)
        first = pl.program_id(1) == 0
        x, h = x_ref[...], h_ref[...]
        s = _silu(_dn_conv4([x] + [_shift_down(x, h, k, first) for k in range(1, 4)], w_ref))
        scale = jnp.where(j == 0, DN_DK ** -0.5, 1.0)
        for hd in _HEADS:
            cs = slice(hd * 128, (hd + 1) * 128)
            sh = s[:, cs]
            r = lax.rsqrt(jnp.sum(sh * sh, axis=-1, keepdims=True) + EPS)
            o_ref[:, cs] = jnp.where(j < 2, sh * r * scale, sh)

    main = pl.BlockSpec((tl, DN_W), lambda j, i: (i, j))
    halo = pl.BlockSpec((8, DN_W), lambda j, i: (jnp.maximum(i * hb - 1, 0), j))
    return _call(body, (rin, rin, cw), name=name, grid=(3, L // tl),
                 in_specs=[main, halo, pl.BlockSpec((4, DN_W), lambda j, i: (0, j))],
                 out_specs=main, out_shape=jax.ShapeDtypeStruct((L, 3 * DN_W), F32),
                 sem=("parallel", "parallel"), comm=comm)


def dn_prep_bwd(rin, cw, dout, name):
    L = rin.shape[0]
    tl = _rtile(L, 256)
    hb = tl // 8
    nrt = L // tl

    def body(x_ref, h_ref, w_ref, d_ref, dx_ref, s_ref, c_ref):
        j = pl.program_id(0)
        i = pl.program_id(1)
        first = i == nrt - 1

        @pl.when(i == 0)
        def _():
            s_ref[...] = jnp.zeros_like(s_ref)
            c_ref[...] = jnp.zeros_like(c_ref)

        x, h = x_ref[...], h_ref[...]
        taps = [x] + [_shift_down(x, h, k, first) for k in range(1, 4)]
        xc = _dn_conv4(taps, w_ref)
        s = _silu(xc)
        scale = jnp.where(j == 0, DN_DK ** -0.5, 1.0)
        pieces = []
        for hd in _HEADS:
            cs = slice(hd * 128, (hd + 1) * 128)
            sh, d = s[:, cs], d_ref[:, cs]
            r = lax.rsqrt(jnp.sum(sh * sh, axis=-1, keepdims=True) + EPS)
            n = sh * r
            dn = d * scale
            pieces.append(jnp.where(j < 2, r * (dn - n * jnp.sum(dn * n, axis=-1, keepdims=True)), d))
        dxc = jnp.concatenate(pieces, axis=1) * _dsilu(xc)
        nxt = c_ref[...]
        dx_ref[...] = _dn_conv4([dxc] + [_shift_up(dxc, nxt, k) for k in range(1, 4)], w_ref).astype(BF16)
        c_ref[...] = dxc[0:8, :]
        for k in range(4):
            s_ref[8 * (3 - k):8 * (3 - k) + 8, :] += _fold8(dxc * taps[k])

    rev = lambda i: nrt - 1 - i
    main = pl.BlockSpec((tl, DN_W), lambda j, i: (rev(i), j))
    halo = pl.BlockSpec((8, DN_W), lambda j, i: (jnp.maximum(rev(i) * hb - 1, 0), j))
    return pl.pallas_call(
        body, name=name, grid=(3, nrt),
        in_specs=[main, halo, pl.BlockSpec((4, DN_W), lambda j, i: (0, j)), main],
        out_specs=(main, pl.BlockSpec((32, DN_W), lambda j, i: (0, j))),
        out_shape=(jax.ShapeDtypeStruct((L, 3 * DN_W), BF16), jax.ShapeDtypeStruct((32, 3 * DN_W), F32)),
        scratch_shapes=[pltpu.VMEM((8, DN_W), F32)],
        compiler_params=_cparams("parallel", "arbitrary"))(rin, rin, cw, dout)


_HI = lax.Precision.HIGH
_NT = (((1,), (1,)), ((), ()))
_TN = (((0,), (0,)), ((), ()))
_HEADS = tuple(range(DN_H))


def _mm(a, b, dims=(((1,), (0,)), ((), ())), hi=False):
    if hi:
        return lax.dot_general(a, b, dims, precision=_HI, preferred_element_type=F32)
    return lax.dot_general(a.astype(BF16), b.astype(BF16), dims, preferred_element_type=F32)


def _dn_masks():
    ri = lax.broadcasted_iota(jnp.int32, (DN_C, DN_C), 0)
    ci = lax.broadcasted_iota(jnp.int32, (DN_C, DN_C), 1)
    return ri >= ci, ri > ci, (ri == ci).astype(F32)


def _dn_decay(gc, gr, causal):
    gam = [jnp.where(causal, jnp.exp(jnp.where(causal, c - r, 0.0)), 0.0) for c, r in zip(gc, gr)]
    eg, el, gl = _dn_row_decay(gc)
    return gam, eg, el, gl


def _dn_row_decay(gc):
    eg = [jnp.exp(c) for c in gc]
    el = [jnp.exp(c[DN_C - 1:DN_C, :] - c) for c in gc]
    gl = [jnp.exp(c[DN_C - 1:DN_C, :]) for c in gc]
    return eg, el, gl


def _dn_solve(k, v, beta, gam, eg, kk, strict, eye):
    ids = range(len(k))
    nmat = [jnp.where(strict, beta[h] * kk[h] * gam[h], 0.0) for h in ids]
    t = [eye - nmat[h] for h in ids]
    m = [_mm(nmat[h], nmat[h]) for h in ids]
    for step in range(5):
        t = [t[h] + _mm(t[h], m[h]) for h in ids]
        if step < 4:
            m = [_mm(m[h], m[h]) for h in ids]
    res = [eye - t[h] - _mm(nmat[h], t[h], hi=True) for h in ids]
    t = [t[h] + _mm(t[h], res[h]) for h in ids]
    rhs = [jnp.concatenate([v[h] * beta[h], k[h] * (beta[h] * eg[h])], axis=1) for h in ids]
    sol = [_mm(t[h], rhs[h], hi=True) for h in ids]
    return t, sol


def dn_chunk_fwd(qkv, gcol, grow, bcol, name, comm=None):
    L = qkv.shape[0]
    C, W = DN_C, DN_H * DN_DK
    ncb = 8
    tl = ncb * C
    nchunks = L // C
    comm1, comm2 = comm if comm is not None else (None, None)
    hs = lambda h: slice(h * 128, (h + 1) * 128)

    def intra(q_ref, k_ref, v_ref, gc_ref, gr_ref, b_ref, t_ref, sol_ref, qk_ref):
        causal, strict, eye = _dn_masks()

        def pair(p, _):
            units = [(DN_NI * p + j, h) for j in range(DN_NI) for h in _HEADS]
            rows = [pl.ds(pl.multiple_of(c * C, C), C) for c, _ in units]
            q = [q_ref[r, hs(h)] for r, (_, h) in zip(rows, units)]
            k = [k_ref[r, hs(h)] for r, (_, h) in zip(rows, units)]
            v = [v_ref[r, hs(h)] for r, (_, h) in zip(rows, units)]
            gc = [gc_ref[r, h:h + 1] for r, (_, h) in zip(rows, units)]
            gr = [gr_ref[c][h:h + 1, :] for c, h in units]
            beta = [b_ref[r, h:h + 1] for r, (_, h) in zip(rows, units)]
            gam, eg, _, _ = _dn_decay(gc, gr, causal)
            kk = [_mm(x, x, _NT) for x in k]
            t, sol = _dn_solve(k, v, beta, gam, eg, kk, strict, eye)
            qk = [_mm(a, b, _NT) * g for a, b, g in zip(q, k, gam)]
            for i, (r, (_, h)) in enumerate(zip(rows, units)):
                t_ref[r, h * C:(h + 1) * C] = t[i]
                sol_ref[r, h * 256:(h + 1) * 256] = sol[i]
                qk_ref[r, h * C:(h + 1) * C] = qk[i]
            return 0

        lax.fori_loop(0, ncb // DN_NI, pair, 0)

    def scan(q_ref, k_ref, gc_ref, sol_ref, qk_ref, o_ref, sh_ref, s_ref):
        @pl.when(pl.program_id(0) == 0)
        def _():
            s_ref[...] = jnp.zeros_like(s_ref)

        def chunk(c, _):
            rows = pl.ds(pl.multiple_of(c * C, C), C)
            q = [q_ref[rows, hs(h)] for h in _HEADS]
            k = [k_ref[rows, hs(h)] for h in _HEADS]
            sol = [sol_ref[rows, h * 256:(h + 1) * 256] for h in _HEADS]
            qk = [qk_ref[rows, h * C:(h + 1) * C] for h in _HEADS]
            eg, el, gl = _dn_row_decay([gc_ref[rows, h:h + 1] for h in _HEADS])
            S = [s_ref[hs(h), :] for h in _HEADS]
            vn = [sol[h][:, :128] - _mm(sol[h][:, 128:], S[h]) for h in _HEADS]
            o = [_mm(q[h] * eg[h], S[h]) + _mm(qk[h], vn[h]) for h in _HEADS]
            Sn = [S[h] * gl[h] + _mm(k[h] * el[h], vn[h], _TN) for h in _HEADS]
            for h in _HEADS:
                sh_ref[c, hs(h), :] = S[h]
                s_ref[hs(h), :] = Sn[h]
                o_ref[rows, hs(h)] = o[h]
            return 0

        lax.fori_loop(0, ncb, chunk, 0)

    col = lambda b: pl.BlockSpec((tl, W), lambda i: (i, b))
    small = pl.BlockSpec((tl, 8), lambda i: (i, 0))
    rowblk = lambda w: pl.BlockSpec((tl, w), lambda i: (i, 0))
    sd = jax.ShapeDtypeStruct
    (thist, solhist, qk), got1 = _with_comm(_call(
        intra, (qkv, qkv, qkv, gcol, grow, bcol), name=name + "_intra", grid=(L // tl,),
        in_specs=[col(0), col(1), col(2), small, pl.BlockSpec((ncb, 8, C), lambda i: (i, 0, 0)), small],
        out_specs=(rowblk(DN_H * C), rowblk(DN_H * 256), rowblk(DN_H * C)),
        out_shape=(sd((L, DN_H * C), F32), sd((L, DN_H * 256), F32), sd((L, DN_H * C), F32)),
        sem=("parallel",), comm=comm1), comm1)
    (o, shist), got2 = _with_comm(_call(
        scan, (qkv, qkv, gcol, solhist, qk), name=name + "_scan", grid=(L // tl,),
        in_specs=[col(0), col(1), small, rowblk(DN_H * 256), rowblk(DN_H * C)],
        out_specs=(rowblk(W), pl.BlockSpec((ncb, W, 128), lambda i: (i, 0, 0))),
        out_shape=(sd((L, W), F32), sd((nchunks, W, 128), F32)),
        scratch_shapes=[pltpu.VMEM((W, 128), F32)], sem=("arbitrary",), comm=comm2), comm2)
    res = (o, shist, thist, solhist)
    return res if comm is None else (res, (got1 or []) + (got2 or []))


def dn_chunk_bwd(qkv, gcol, grow, bcol, shist, thist, solhist, do, name, comm=None):
    L = qkv.shape[0]
    C, W = DN_C, DN_H * DN_DK
    ncb = 8
    tl = ncb * C
    nchunks = L // C
    nt = L // tl

    def body(q_ref, k_ref, v_ref, gc_ref, gr_ref, b_ref, sh_ref, t_ref, sol_ref, do_ref,
             dqkv_ref, dgc_ref, dgr_ref, db_ref, ds_ref):
        @pl.when(pl.program_id(0) == 0)
        def _():
            ds_ref[...] = jnp.zeros_like(ds_ref)

        lane8 = lax.broadcasted_iota(jnp.int32, (C, 8), 1)
        sub8 = lax.broadcasted_iota(jnp.int32, (8, C), 0)
        rowid = lax.broadcasted_iota(jnp.int32, (C, 1), 0)
        causal, strict, _ = _dn_masks()
        rsum = lambda a: jnp.sum(a, axis=1, keepdims=True)

        def chunk(cc, _):
            c = ncb - 1 - cc
            rows = pl.ds(pl.multiple_of(c * C, C), C)
            grow_c = gr_ref[c]
            hs = lambda h: slice(h * 128, (h + 1) * 128)
            q = [q_ref[rows, hs(h)] for h in _HEADS]
            k = [k_ref[rows, hs(h)] for h in _HEADS]
            v = [v_ref[rows, hs(h)] for h in _HEADS]
            gc = [gc_ref[rows, h:h + 1] for h in _HEADS]
            gr = [grow_c[h:h + 1, :] for h in _HEADS]
            beta = [b_ref[rows, h:h + 1] for h in _HEADS]
            t = [t_ref[rows, h * C:(h + 1) * C] for h in _HEADS]
            sol = [sol_ref[rows, h * 256:(h + 1) * 256] for h in _HEADS]
            S = [sh_ref[c, hs(h), :] for h in _HEADS]
            dS = [ds_ref[hs(h), :] for h in _HEADS]
            dov = [do_ref[rows, hs(h)] for h in _HEADS]
            gam, eg, el, gl = _dn_decay(gc, gr, causal)
            kk = [_mm(k[h], k[h], _NT) for h in _HEADS]
            qk_raw = [_mm(q[h], k[h], _NT) for h in _HEADS]
            w = [sol[h][:, 128:] for h in _HEADS]
            kd = [k[h] * el[h] for h in _HEADS]
            vn = [sol[h][:, :128] - _mm(w[h], S[h]) for h in _HEADS]
            dvn = [_mm(qk_raw[h] * gam[h], dov[h], _TN) + _mm(kd[h], dS[h]) for h in _HEADS]
            dqd = [_mm(dov[h], S[h], _NT) for h in _HEADS]
            dqk = [jnp.where(causal, _mm(dov[h], vn[h], _NT), 0.0) for h in _HEADS]
            dkd = [_mm(vn[h], dS[h], _NT) for h in _HEADS]
            dgl = [jnp.sum(rsum(dS[h] * S[h]), axis=0, keepdims=True) for h in _HEADS]
            dw = [-_mm(dvn[h], S[h], _NT) for h in _HEADS]
            dSn = [dS[h] * gl[h] + _mm(q[h] * eg[h], dov[h], _TN) - _mm(w[h], dvn[h], _TN) for h in _HEADS]
            drhs = [_mm(t[h], jnp.concatenate([dvn[h], dw[h]], axis=1), _TN) for h in _HEADS]
            dn = [jnp.where(strict, -_mm(drhs[h], sol[h], _NT), 0.0) for h in _HEADS]
            dgc_all = jnp.zeros((C, 8), F32)
            db_all = jnp.zeros((C, 8), F32)
            dgr_all = jnp.zeros((8, C), F32)
            for h in _HEADS:
                drv, drk = drhs[h][:, :128], drhs[h][:, 128:]
                t2 = rsum(drk * k[h])
                x = dn[h] * gam[h]
                dbeta = rsum(drv * v[h]) + t2 * eg[h] + rsum(x * kk[h])
                dkk = x * beta[h]
                draw = dqk[h] * gam[h]
                mm_ = (dn[h] * beta[h] * kk[h] + dqk[h] * qk_raw[h]) * gam[h]
                deg = t2 * beta[h] + rsum(dqd[h] * q[h])
                r_ = rsum(dkd[h] * k[h]) * el[h]
                dglast = jnp.sum(r_, axis=0, keepdims=True) + dgl[h] * gl[h]
                dgc = rsum(mm_) + deg * eg[h] - r_ + jnp.where(rowid == C - 1, dglast, 0.0)
                dgr = -jnp.sum(mm_, axis=0, keepdims=True)
                dqkv_ref[rows, hs(h)] = _mm(draw, k[h]) + dqd[h] * eg[h]
                dqkv_ref[rows, hs(DN_H + h)] = (drk * (beta[h] * eg[h]) + _mm(dkk, k[h]) + _mm(dkk, k[h], _TN)
                                                + _mm(draw, q[h], _TN) + dkd[h] * el[h])
                dqkv_ref[rows, hs(2 * DN_H + h)] = drv * beta[h]
                ds_ref[hs(h), :] = dSn[h]
                dgc_all = dgc_all + jnp.where(lane8 == h, dgc, 0.0)
                db_all = db_all + jnp.where(lane8 == h, dbeta, 0.0)
                dgr_all = dgr_all + jnp.where(sub8 == h, dgr, 0.0)
            dgc_ref[rows, :] = dgc_all
            db_ref[rows, :] = db_all
            dgr_ref[c] = dgr_all
            return 0

        lax.fori_loop(0, ncb, chunk, 0)

    rev = lambda i: nt - 1 - i
    col = lambda b: pl.BlockSpec((tl, W), lambda i: (rev(i), b))
    rowblk = lambda w: pl.BlockSpec((tl, w), lambda i: (rev(i), 0))
    small = pl.BlockSpec((tl, 8), lambda i: (rev(i), 0))
    g3 = pl.BlockSpec((ncb, 8, C), lambda i: (rev(i), 0, 0))
    sd = jax.ShapeDtypeStruct
    return _call(body, (qkv, qkv, qkv, gcol, grow, bcol, shist, thist, solhist, do), name=name, grid=(nt,),
                 in_specs=[col(0), col(1), col(2), small, g3, small,
                           pl.BlockSpec((ncb, W, 128), lambda i: (rev(i), 0, 0)), rowblk(DN_H * C),
                           rowblk(DN_H * 256), col(0)],
                 out_specs=(rowblk(3 * W), small, g3, small),
                 out_shape=(sd((L, 3 * W), F32), sd((L, 8), F32), sd((nchunks, 8, C), F32), sd((L, 8), F32)),
                 scratch_shapes=[pltpu.VMEM((W, 128), F32)], sem=("arbitrary",), comm=comm)


def dn_out_fwd(o, rin, nw, name):
    L = o.shape[0]
    tl = _rtile(L, 256)

    def body(o_ref, z_ref, w_ref, y_ref):
        for hd in _HEADS:
            cs = slice(hd * 128, (hd + 1) * 128)
            ov = o_ref[:, cs]
            r = lax.rsqrt(jnp.mean(ov * ov, axis=-1, keepdims=True) + EPS)
            y_ref[:, cs] = (ov * r * w_ref[...] * _silu(z_ref[:, cs])).astype(BF16)

    return pl.pallas_call(
        body, name=name, grid=(L // tl,),
        in_specs=[pl.BlockSpec((tl, DN_W), lambda i: (i, 0)), pl.BlockSpec((tl, DN_W), lambda i: (i, 3)),
                  pl.BlockSpec((1, 128), lambda i: (0, 0))],
        out_specs=pl.BlockSpec((tl, DN_W), lambda i: (i, 0)), out_shape=jax.ShapeDtypeStruct((L, DN_W), BF16),
        compiler_params=_cparams("parallel"))(o, rin, nw)


def dn_out_bwd(dycat, o, rin, nw, name):
    L = o.shape[0]
    tl = _rtile(L, 256)

    def body(dy_ref, o_ref, z_ref, w_ref, do_ref, dz_ref, s_ref):
        @pl.when(pl.program_id(0) == 0)
        def _():
            s_ref[...] = jnp.zeros_like(s_ref)

        for hd in _HEADS:
            cs = slice(hd * 128, (hd + 1) * 128)
            ov, zv, d = o_ref[:, cs], z_ref[:, cs], dy_ref[:, cs]
            r = lax.rsqrt(jnp.mean(ov * ov, axis=-1, keepdims=True) + EPS)
            n = ov * r
            dnw = d * _silu(zv)
            dz_ref[:, cs] = (d * n * w_ref[...] * _dsilu(zv)).astype(BF16)
            dn = dnw * w_ref[...]
            do_ref[:, cs] = r * (dn - n * jnp.mean(dn * n, axis=-1, keepdims=True))
            s_ref[:, cs] += _fold8(dnw * n)

    own = pl.BlockSpec((tl, DN_W), lambda i: (i, 0))
    sd = jax.ShapeDtypeStruct
    return pl.pallas_call(
        body, name=name, grid=(L // tl,),
        in_specs=[own, own, pl.BlockSpec((tl, DN_W), lambda i: (i, 3)), pl.BlockSpec((1, 128), lambda i: (0, 0))],
        out_specs=(own, own, pl.BlockSpec((8, DN_W), lambda i: (0, 0))),
        out_shape=(sd((L, DN_W), F32), sd((L, DN_W), BF16), sd((8, DN_W), F32)),
        compiler_params=_cparams("arbitrary"))(dycat, o, rin, nw)


def dn_gates(a, beta_raw, a_log, dt_bias):
    L = a.shape[0]
    beta = jax.nn.sigmoid(beta_raw)
    g = -jnp.exp(a_log) * jax.nn.softplus(a + dt_bias)
    G = jnp.cumsum(g.reshape(L // DN_C, DN_C, DN_H), axis=1)
    pad = lambda t: jnp.pad(t, ((0, 0), (0, 8 - DN_H)))
    gcol = pad(G.reshape(L, DN_H))
    grow = jnp.pad(jnp.transpose(G, (0, 2, 1)), ((0, 0), (0, 8 - DN_H), (0, 0)))
    return gcol, grow, pad(beta)


def dn_block_fwd(rin, cw, a_log, dt_bias, out_norm, tag, comm=None):
    gates, gates_vjp = jax.vjp(dn_gates, rin[:, REC_A0:REC_A0 + DN_H], rin[:, REC_A0 + DN_H:REC_IN], a_log, dt_bias)
    c0, c12 = (comm[0], comm[1:]) if comm is not None else (None, None)
    qkv, got0 = _with_comm(dn_prep_fwd(rin, cw, tag + "_prep", comm=c0), c0)
    (o, shist, thist, solhist), got = _with_comm(dn_chunk_fwd(qkv, *gates, tag + "_chunk", comm=c12), c12)
    yd = dn_out_fwd(o, rin, out_norm.reshape(1, 128), tag + "_onorm")
    return yd, (qkv, gates, gates_vjp, o, shist, thist, solhist), (got0 or []) + (got or [])


def dn_block_bwd(dyd, res, rin, cw, out_norm, tag, comm=None):
    qkv, gates, gates_vjp, o, shist, thist, solhist = res
    do, dz, nsum = dn_out_bwd(dyd, o, rin, out_norm.reshape(1, 128), tag + "_donorm")
    (dqkv, dgc, dgr, db), got = _with_comm(dn_chunk_bwd(qkv, *gates, shist, thist, solhist, do, tag + "_dchunk",
                                                        comm=comm), comm)
    da, dbraw, g_alog, g_dtb = gates_vjp((dgc, dgr, db))
    dx, csum = dn_prep_bwd(rin, cw, dqkv, tag + "_dprep")
    grads = dict(conv=csum.reshape(4, 8, DN_NT * 128).sum(axis=1), a_log=g_alog, dt_bias=g_dtb,
                 out_norm=nsum.sum(axis=0).reshape(DN_H, 128).sum(axis=0))
    return dx, dz, da, dbraw, grads, got


_HBM = pl.BlockSpec(memory_space=pltpu.HBM)


def _mesh_pos():
    xi, yi, ci = lax.axis_index("x"), lax.axis_index("y"), lax.axis_index("c")
    return xi, yi, ci, 4 * xi + 2 * yi + ci


def _peer(xi, yi, ci, k):
    px = 1 - xi if (k >> 2) & 1 else xi
    py = 1 - yi if (k >> 1) & 1 else yi
    pc = 1 - ci if k & 1 else ci
    return (px, py, pc), 4 * px + 2 * py + pc


def _exchange(xs, gather, name):
    n = len(xs)

    def body(*refs):
        copies = _comm_copies(refs[:n], refs[n:2 * n], *refs[2 * n:], gather)
        for cp in copies:
            cp.start()
        for cp in copies:
            cp.wait()

    return pl.pallas_call(
        body, name=name, in_specs=[_HBM] * n, out_specs=tuple([_HBM] * n),
        out_shape=_comm_out_shapes(xs), scratch_shapes=_comm_sems(n))(*xs)


def _comm_out_shapes(xs):
    return tuple(jax.ShapeDtypeStruct((N_DEV,) + x.shape[-2:], x.dtype) for x in xs)


def _comm_sems(n):
    return [pltpu.SemaphoreType.DMA((n * (N_DEV - 1),)), pltpu.SemaphoreType.DMA((n * (N_DEV - 1),)),
            pltpu.SemaphoreType.DMA((n,))]


def _comm_copies(x_refs, o_refs, send_sems, recv_sems, lsems, gather):
    xi, yi, ci, me = _mesh_pos()
    copies = []
    for t in range(len(x_refs)):
        src_of = (lambda lin, t=t: x_refs[t]) if gather else (lambda lin, t=t: x_refs[t].at[lin])
        copies.append(pltpu.make_async_copy(src_of(me), o_refs[t].at[me], lsems.at[t]))
        for k in range(1, N_DEV):
            peer, lin = _peer(xi, yi, ci, k)
            s = t * (N_DEV - 1) + k - 1
            copies.append(pltpu.make_async_remote_copy(
                src_ref=src_of(lin), dst_ref=o_refs[t].at[me], send_sem=send_sems.at[s],
                recv_sem=recv_sems.at[s], device_id=peer, device_id_type=pl.DeviceIdType.MESH))
    return copies


def _call(body, args, *, name, grid, in_specs, out_specs, out_shape, scratch_shapes=(), sem, comm=None):
    if comm is None:
        return pl.pallas_call(body, name=name, grid=grid, in_specs=in_specs, out_specs=out_specs,
                              out_shape=out_shape, scratch_shapes=list(scratch_shapes),
                              compiler_params=_cparams(*sem))(*args)
    xs, gather = comm
    n = len(xs)
    single = not isinstance(out_shape, (tuple, list))
    outs_shape = (out_shape,) if single else tuple(out_shape)
    outs_specs = (out_specs,) if single else tuple(out_specs)
    n_in, n_out, n_scr = len(in_specs), len(outs_shape), len(scratch_shapes)

    def body2(*refs):
        ins, cx = refs[:n_in], refs[n_in:n_in + n]
        outs = refs[n_in + n:n_in + n + n_out]
        co = refs[n_in + n + n_out:n_in + 2 * n + n_out]
        scr = refs[n_in + 2 * n + n_out:n_in + 2 * n + n_out + n_scr]
        sems = refs[n_in + 2 * n + n_out + n_scr:]
        first = functools.reduce(jnp.logical_and, [pl.program_id(a) == 0 for a in range(len(grid))])
        last = functools.reduce(jnp.logical_and, [pl.program_id(a) == grid[a] - 1 for a in range(len(grid))])

        @pl.when(first)
        def _():
            for cp in _comm_copies(cx, co, *sems, gather):
                cp.start()

        body(*ins, *outs, *scr)

        @pl.when(last)
        def _():
            for cp in _comm_copies(cx, co, *sems, gather):
                cp.wait()

    res = pl.pallas_call(
        body2, name=name, grid=grid, in_specs=list(in_specs) + [_HBM] * n,
        out_specs=outs_specs + tuple([_HBM] * n), out_shape=outs_shape + _comm_out_shapes(xs),
        scratch_shapes=list(scratch_shapes) + _comm_sems(n),
        compiler_params=_cparams(*(["arbitrary"] * len(grid))))(*args, *xs)
    main = res[0] if single else tuple(res[:n_out])
    return main, list(res[n_out:])


def all_gather(x, name):
    return _exchange([x], True, name)[0]


def all_gather_many(xs, name):
    return _exchange(xs, True, name)


def all_to_all_many(xs, name):
    return _exchange(xs, False, name)


def reduce_adamw(gsrc, w, m, v, name, comm=None):
    parts = list(gsrc) if isinstance(gsrc, (list, tuple)) else [gsrc]
    S, R0, C = parts[0].shape
    R = R0 * len(parts)
    tr = _rtile(R0, max(16, min(256, (4 << 20) // (S * C * 4) // 16 * 16)), 16 if R0 % 16 == 0 else 8)
    n0 = R0 // tr
    c1 = 1.0 - ADAM_B1 ** ADAM_STEP
    c2 = 1.0 - ADAM_B2 ** ADAM_STEP

    def body(*refs):
        g_refs = refs[:len(parts)]
        w_ref, m_ref, v_ref, go_ref, d_ref, mo_ref, vo_ref = refs[len(parts):]
        for p, g_ref in enumerate(g_refs):
            @pl.when(pl.program_id(0) // n0 == p)
            def _(g_ref=g_ref):
                acc = g_ref[0].astype(F32)
                for s in range(1, S):
                    acc = acc + g_ref[s].astype(F32)
                go_ref[...] = acc
        g = go_ref[...]
        mn = ADAM_B1 * m_ref[...] + (1.0 - ADAM_B1) * g
        vn = ADAM_B2 * v_ref[...] + (1.0 - ADAM_B2) * (g * g)
        mo_ref[...] = mn
        vo_ref[...] = vn
        d_ref[...] = -ADAM_LR * ((mn / c1) / (jnp.sqrt(vn / c2) + ADAM_EPS) + ADAM_WD * w_ref[...])

    big = pl.BlockSpec((tr, C), lambda i: (i, 0))
    o = jax.ShapeDtypeStruct((R, C), F32)
    part_spec = lambda p: pl.BlockSpec((S, tr, C), lambda i: (0, jnp.clip(i - p * n0, 0, n0 - 1), 0))
    return _call(body, (*parts, w, m, v), name=name, grid=(R // tr,),
                 in_specs=[part_spec(p) for p in range(len(parts))] + [big, big, big],
                 out_specs=(big, big, big, big), out_shape=(o, o, o, o), sem=("parallel",), comm=comm)


def _to_slabs(g, ax):
    shp = g.shape
    g = g.reshape(shp[:ax] + (N_DEV, shp[ax] // N_DEV) + shp[ax + 1:])
    return jnp.moveaxis(g, ax, 0).reshape(N_DEV, -1)


def _from_slabs(s, ax, shp):
    s = s.reshape((N_DEV,) + shp[:ax] + (shp[ax] // N_DEV,) + shp[ax + 1:])
    return jnp.moveaxis(s, 0, ax).reshape(shp)


def _pack_rows(flat, width, row_mult):
    n = flat.shape[-1]
    per = width * row_mult
    tot = -(-n // per) * per
    flat = jnp.pad(flat, [(0, 0)] * (flat.ndim - 1) + [(0, tot - n)])
    return flat.reshape(flat.shape[:-1] + (tot // width, width))


def _offsets(sizes):
    offs, o = [], 0
    for s in sizes:
        offs.append(o)
        o += s
    return offs


WEIGHTS = ['ada_w', 'ada_b', 'norm_mix', 'norm_ffn', 'attn_w_in', 'attn_q_norm_a', 'attn_k_norm_a', 'attn_q_norm_b',
           'attn_k_norm_b', 'attn_sinks', 'attn_w_out', 'rec_w_in', 's5_lambda_re', 's5_lambda_im', 's5_log_dt',
           's5_b_re', 's5_b_im', 's5_c_re', 's5_c_im', 's5_d', 's5_glu_w', 's5_glu_b', 'dn_conv', 'dn_a_log',
           'dn_dt_bias', 'dn_out_norm', 'rec_w_out', 'ffn_w_up', 'ffn_conv', 'ffn_w_down']
BIG = [('attn_w_in', (D, ATTN_IN // N_DEV)), ('attn_w_out', (D // N_DEV, D)), ('rec_w_in', (D // N_DEV, REC_PAD)),
       ('s5_glu_w', (S5_W // N_DEV, S5_W)), ('rec_w_out', (D // N_DEV, D)), ('ffn_w_up', (2 * D, 2 * D_FF // N_DEV)),
       ('ffn_w_down', (2 * D_FF // N_DEV, D))]


def _shard2d(name, t):
    if name == 'rec_w_in':
        return jnp.pad(t[0], ((0, 0), (0, REC_PAD - REC_IN)))
    return t.reshape((-1, t.shape[-1]))


def _cols_to_slabs(g, k=N_DEV):
    r, n = g.shape
    return jnp.transpose(g.reshape(r, k, n // k), (1, 0, 2))


def _slabs_to_cols(s):
    k, r, c_ = s.shape
    return jnp.transpose(s, (1, 0, 2)).reshape(r, k * c_)
SMALL_SHARDED = [('s5_d', 1, (1, S5_W)), ('s5_glu_b', 1, (1, S5_W)), ('dn_conv', 2, (1, 4, 2304)),
                 ('ffn_conv', 2, (2, 3, 2 * D_FF))]
REPLICATED = [('ada_b', (2, 6 * D)), ('norm_mix', (2, D)), ('norm_ffn', (2, D)), ('attn_q_norm_a', (1, HD)),
              ('attn_k_norm_a', (1, HD)), ('attn_q_norm_b', (1, HD)), ('attn_k_norm_b', (1, HD)),
              ('attn_sinks', (1, 8)), ('s5_lambda_re', (1, 16, 64)), ('s5_lambda_im', (1, 16, 64)),
              ('s5_log_dt', (1, 16)), ('s5_b_re', (1, 16, 64, 16)), ('s5_b_im', (1, 16, 64, 16)),
              ('s5_c_re', (1, 16, 16, 64)), ('s5_c_im', (1, 16, 16, 64)), ('dn_a_log', (1, DN_H)),
              ('dn_dt_bias', (1, DN_H)), ('dn_out_norm', (1, 128))]


def _numel(shp):
    return int(np.prod(shp))


def kernel(x, c, ada_w, ada_b, norm_mix, norm_ffn, attn_w_in, attn_q_norm_a, attn_k_norm_a, attn_q_norm_b, attn_k_norm_b, attn_sinks, attn_w_out, rec_w_in, s5_lambda_re, s5_lambda_im, s5_log_dt, s5_b_re, s5_b_im, s5_c_re, s5_c_im, s5_d, s5_glu_w, s5_glu_b, dn_conv, dn_a_log, dn_dt_bias, dn_out_norm, rec_w_out, ffn_w_up, ffn_conv, ffn_w_down, loss_target, m_ada_w, m_ada_b, m_norm_mix, m_norm_ffn, m_attn_w_in, m_attn_q_norm_a, m_attn_k_norm_a, m_attn_q_norm_b, m_attn_k_norm_b, m_attn_sinks, m_attn_w_out, m_rec_w_in, m_s5_lambda_re, m_s5_lambda_im, m_s5_log_dt, m_s5_b_re, m_s5_b_im, m_s5_c_re, m_s5_c_im, m_s5_d, m_s5_glu_w, m_s5_glu_b, m_dn_conv, m_dn_a_log, m_dn_dt_bias, m_dn_out_norm, m_rec_w_out, m_ffn_w_up, m_ffn_conv, m_ffn_w_down, v_ada_w, v_ada_b, v_norm_mix, v_norm_ffn, v_attn_w_in, v_attn_q_norm_a, v_attn_k_norm_a, v_attn_q_norm_b, v_attn_k_norm_b, v_attn_sinks, v_attn_w_out, v_rec_w_in, v_s5_lambda_re, v_s5_lambda_im, v_s5_log_dt, v_s5_b_re, v_s5_b_im, v_s5_c_re, v_s5_c_im, v_s5_d, v_s5_glu_w, v_s5_glu_b, v_dn_conv, v_dn_a_log, v_dn_dt_bias, v_dn_out_norm, v_rec_w_out, v_ffn_w_up, v_ffn_conv, v_ffn_w_down):
    loc = locals()
    W = {n: loc[n] for n in WEIGHTS}
    M = {n: loc["m_" + n] for n in WEIGHTS}
    V = {n: loc["v_" + n] for n in WEIGHTS}
    _, _, _, me = _mesh_pos()
    L = x.shape[1]
    x0, tgt = x[0], loss_target[0]

    small_in = jnp.concatenate([c.reshape(-1)] + [W[n].reshape(-1) for n, _, _ in SMALL_SHARDED])
    si, att_in_all = all_gather_many([_pack_rows(small_in, 1024, 8), attn_w_in[0].astype(BF16)], "gather_first")
    si = si.reshape(N_DEV, -1)
    c_all = si[:, :D]
    off = D
    small_full = {}
    for n, ax, shp in SMALL_SHARDED:
        k = _numel(shp) // N_DEV
        small_full[n] = _from_slabs(si[:, off:off + k], ax, shp)
        off += k

    cond_all = jax.nn.silu(c_all)
    modp = jnp.concatenate([matmul([(cond_all, ada_w[l].astype(BF16))], "nn", f"ada{l}") for l in range(2)], axis=0)
    modp_all = all_gather(modp, "gather_mod")
    mods = []
    for l in range(2):
        row = lax.dynamic_index_in_dim(modp_all, l * N_DEV + me, axis=1, keepdims=False)
        mod = row.reshape(1, 6 * D) + ada_b[l].reshape(1, 6 * D)
        mods.append([mod[:, i * D:(i + 1) * D] for i in range(6)])

    w_att_in = _slabs_to_cols(att_in_all)
    bf = lambda t: t.astype(BF16)
    ffn_shards = [[bf(ffn_w_up[l]), bf(ffn_w_down[l])] for l in range(2)]
    rec_shards = [bf(_shard2d('rec_w_in', rec_w_in)), bf(s5_glu_w[0]), bf(rec_w_out[0])]
    ffn_cw = [small_full['ffn_conv'][l] for l in range(2)]
    dn_cw = small_full['dn_conv'][0]
    s5_dskip, glu_b = small_full['s5_d'], small_full['s5_glu_b']
    row = lambda t: t.reshape(1, -1)

    sh1, sc1, g1, sh2, sc2, g2 = mods[0]
    h1 = gate_norm_fwd(x0, None, None, row(norm_mix[0]), sh1, sc1, "l0_norm1")
    wvec, sinkvec = attn_vectors(attn_q_norm_a[0], attn_k_norm_a[0], attn_q_norm_b[0], attn_k_norm_b[0], attn_sinks[0])
    y0, res_att, got = attention_block_fwd(
        h1, w_att_in, wvec, sinkvec, None, "att",
        comms={'swa': ([ffn_shards[0][0][:D // 2]], True), 1: ([ffn_shards[0][0][D // 2:]], True),
               4: (ffn_shards[0][1:], True), 16: ([bf(attn_w_out[0])], True)})
    w_att_out = got['w_out']
    split_up = lambda up_all: (_slabs_to_cols(up_all[:4]), _slabs_to_cols(up_all[4:]))
    w_up = [split_up(jnp.concatenate([got['swa'][0], got[1][0]], axis=1))]
    w_down = [got[4][0].reshape(D_FF, D)]
    x1, h2 = gate_norm_fwd(x0, y0, g1, row(norm_ffn[0]), sh2, sc2, "l0_norm2")
    f0, res_f0, got_rec = ffn_block_fwd(h2, w_up[0][0], w_up[0][1], ffn_cw[0], w_down[0], "ffn0",
                                        comm=(rec_shards, True))
    w_rec_in = rec_cols_permute(got_rec[0].reshape(D, REC_PAD))
    glu_w, w_rec_out = got_rec[1].reshape(S5_W, S5_W), got_rec[2].reshape(D, D)
    w_rec_out = jnp.concatenate([w_rec_out[S5_W:], w_rec_out[:S5_W]], axis=0)
    t1, tc1, tg1, t2, tc2, tg2 = mods[1]
    x2, h3 = gate_norm_fwd(x1, f0, g2, row(norm_mix[1]), t1, tc1, "l1_norm1")
    rin = matmul([(h3, w_rec_in)], "nn", "rec_in")
    s5p, s5p_vjp = jax.vjp(s5_params, s5_lambda_re[0], s5_lambda_im[0], s5_log_dt[0], s5_b_re[0], s5_b_im[0],
                           s5_c_re[0], s5_c_im[0])
    u = rin[:, REC_U0:REC_A0]
    yc, res_s5 = s5_block_fwd(u, s5p, s5_dskip, glu_w, glu_b, "s5")
    yd, res_dn, got_ffn1 = dn_block_fwd(rin, dn_cw, dn_a_log[0], dn_dt_bias[0], dn_out_norm[0], "dn",
                                        comm=(([ffn_shards[1][1]], True), ([ffn_shards[1][0][:D // 2]], True),
                                              ([ffn_shards[1][0][D // 2:]], True)))
    w_up.append(split_up(jnp.concatenate([got_ffn1[1], got_ffn1[2]], axis=1)))
    w_down.append(got_ffn1[0].reshape(D_FF, D))
    ycat = jnp.concatenate([yd, yc], axis=1)
    y1 = matmul([(ycat, w_rec_out)], "nn", "rec_out")
    x3, h4 = gate_norm_fwd(x2, y1, tg1, row(norm_ffn[1]), t2, tc2, "l1_norm2")
    f1, res_f1, _ = ffn_block_fwd(h4, w_up[1][0], w_up[1][1], ffn_cw[1], w_down[1], "ffn1")
    dx4, df1, lsum = final_loss(x3, f1, tg2, tgt, "loss")

    G = {}
    d_tg2 = lsum[8:16].sum(axis=0)
    dh4, gf1, _ = ffn_block_bwd(df1, res_f1, w_up[1][0], w_up[1][1], ffn_cw[1], w_down[1], "ffn1")
    ffn_slabs = lambda g: [g['w_up'], g['w_down'].reshape(N_DEV, D_FF // N_DEV, D)]
    dx3, dy1, s = gate_norm_bwd(x3, y1, tg1, row(norm_ffn[1]), tc2, dx4, dh4, "l1_dnorm2")
    s = s.reshape(4, 8, D).sum(axis=1)
    d_tg1, d_nffn1, d_t2, d_tc2 = s[0], s[1] * (1.0 + tc2[0]), s[2], s[1] * norm_ffn[1]
    g_rec_out = matmul([(ycat, dy1)], "tn", "rec_out_dw", out_dtype=BF16)
    g_rec_out = jnp.concatenate([g_rec_out[DN_W:], g_rec_out[:DN_W]], axis=0).reshape(N_DEV, D // N_DEV, D)
    dycat = matmul([(dy1, w_rec_out)], "nt", "rec_out_dx")
    du, s5cot, gs5 = s5_block_bwd(dycat, res_s5, s5p, s5_dskip, glu_w, glu_b, "s5", dout_col=DN_W // S5_W)
    s5g = s5p_vjp(s5cot)
    dqkv, dz, da, dbraw, gdn, recv_ffn1 = dn_block_bwd(dycat, res_dn, rin, dn_cw, dn_out_norm[0], "dn",
                                                       comm=(ffn_slabs(gf1), False))
    d_rest = jnp.concatenate([du.astype(BF16), da.astype(BF16), dbraw.astype(BF16),
                              jnp.zeros((L, REC_PAD - REC_IN), BF16)], axis=1)
    drin = ((dqkv, 0), (dz, 3 * DN_W), (d_rest, REC_U0))
    g_rec_in = jnp.concatenate([matmul([(h3, p)], "tn", f"rec_in_dw{i}", out_dtype=BF16)
                                for i, (p, _) in enumerate(drin)], axis=1)
    g_rec_in = rec_cols_restore(g_rec_in).reshape(N_DEV, D // N_DEV, REC_PAD)
    g_glu = gs5['glu_w'].astype(BF16).reshape(N_DEV, S5_W // N_DEV, S5_W)
    dh3 = matmul([(p, w_rec_in[:, c0:c0 + p.shape[1]]) for p, c0 in drin], "nt", "rec_in_dx")
    dx2, df0, s = gate_norm_bwd(x2, f0, g2, row(norm_mix[1]), tc1, dx3, dh3, "l1_dnorm1")
    s = s.reshape(4, 8, D).sum(axis=1)
    d_g2, d_nmix1, d_t1, d_tc1 = s[0], s[1] * (1.0 + tc1[0]), s[2], s[1] * norm_mix[1]
    dh2, gf0, recv_rec = ffn_block_bwd(df0, res_f0, w_up[0][0], w_up[0][1], ffn_cw[0], w_down[0], "ffn0",
                                       comm=([g_rec_in, g_glu, g_rec_out], False))
    dx1, dy0, s = gate_norm_bwd(x1, y0, g1, row(norm_ffn[0]), sc2, dx2, dh2, "l0_dnorm2")
    s = s.reshape(4, 8, D).sum(axis=1)
    d_g1, d_nffn0, d_sh2, d_sc2 = s[0], s[1] * (1.0 + sc2[0]), s[2], s[1] * norm_ffn[0]
    dh1, gatt, got_b = attention_block_bwd(dy0, res_att, w_att_in, wvec, sinkvec, w_att_out, "att",
                                           comms={'swa': ([gf0['w_up'][:, :D // 2]], False),
                                                  16: ([gf0['w_up'][:, D // 2:]], False),
                                                  1: (ffn_slabs(gf0)[1:], False)},
                                           send_w_out_on=4)
    recv_ffn0 = [jnp.concatenate([got_b['swa'][0], got_b[16][0]], axis=1), got_b[1][0]]
    (grad_x, s), recv_w_in = gate_norm_bwd(x0, None, None, row(norm_mix[0]), sc1, dx1, dh1, "l0_dnorm1",
                                           comm=([_cols_to_slabs(gatt['w_in'])], False))
    recv_att = [recv_w_in[0], got_b[4][0]]
    s = s.reshape(4, 8, D).sum(axis=1)
    d_nmix0, d_sh1, d_sc1 = s[1] * (1.0 + sc1[0]), s[2], s[1] * norm_mix[0]
    dmod = jnp.stack([jnp.concatenate([d_sh1, d_sc1, d_g1, d_sh2, d_sc2, d_g2]),
                      jnp.concatenate([d_t1, d_tc1, d_tg1, d_t2, d_tc2, d_tg2])])

    P = {'ada_b': dmod, 'norm_mix': jnp.stack([d_nmix0, d_nmix1]), 'norm_ffn': jnp.stack([d_nffn0, d_nffn1]),
         'attn_q_norm_a': gatt['q_norm_a'], 'attn_k_norm_a': gatt['k_norm_a'], 'attn_q_norm_b': gatt['q_norm_b'],
         'attn_k_norm_b': gatt['k_norm_b'], 'attn_sinks': gatt['sinks'],
         's5_lambda_re': s5g[0], 's5_lambda_im': s5g[1], 's5_log_dt': s5g[2], 's5_b_re': s5g[3], 's5_b_im': s5g[4],
         's5_c_re': s5g[5], 's5_c_im': s5g[6], 'dn_a_log': gdn['a_log'], 'dn_dt_bias': gdn['dt_bias'],
         'dn_out_norm': gdn['out_norm'],
         's5_d': gs5['dskip'], 's5_glu_b': gs5['glu_b'], 'dn_conv': gdn['conv'],
         'ffn_conv': jnp.stack([gf0['conv'], gf1['conv']])}

    out = {k: {} for k in ("g", "d", "m", "v")}
    keys = ("g", "d", "m", "v")
    recv = {'attn_w_in': recv_att[0], 'attn_w_out': recv_att[1], 'rec_w_in': recv_rec[0], 's5_glu_w': recv_rec[1],
            'rec_w_out': recv_rec[2]}
    for n, gr_ in recv.items():
        res4 = reduce_adamw(gr_, _shard2d(n, W[n]), _shard2d(n, M[n]), _shard2d(n, V[n]), "adamw_" + n)
        for key, t in zip(keys, res4):
            out[key][n] = (t[:, :REC_IN] if n == 'rec_w_in' else t).reshape(W[n].shape)
    rep_sizes = [_numel(shp) for _, shp in REPLICATED]
    ss_sizes = [_numel(shp) for _, _, shp in SMALL_SHARDED]
    rep_offs = _offsets(rep_sizes + ss_sizes + [1])
    parts = [P[n].reshape(-1) for n, _ in REPLICATED] + [P[n].reshape(-1) for n, _, _ in SMALL_SHARDED]
    parts.append(lsum[0:8].sum().reshape(1))
    spack = _pack_rows(jnp.concatenate(parts), 1024, 8)
    flat2d = lambda t: t.reshape(-1, t.shape[-1])
    sall = None
    for n, idx in (('ffn_w_up', 0), ('ffn_w_down', 1)):
        comm = ([spack], True) if sall is None else None
        res4, got_s = _with_comm(reduce_adamw([recv_ffn0[idx], recv_ffn1[idx]], flat2d(W[n]), flat2d(M[n]),
                                              flat2d(V[n]), "adamw_" + n, comm=comm), comm)
        if got_s is not None:
            sall = got_s[0]
        for key, t in zip(keys, res4):
            out[key][n] = t.reshape(W[n].shape)
    n_rest = sum(ss_sizes) + 1
    pk = lambda d: _pack_rows(jnp.concatenate([d[n].reshape(-1) for n, _ in REPLICATED]
                                              + [jnp.zeros((n_rest,), F32)]), 1024, 8)
    sg, sd_, sm, sv = [t.reshape(-1) for t in reduce_adamw(sall, pk(W), pk(M), pk(V), "adamw_small")]
    loss = 0.5 * sg[rep_offs[-1]] / D

    dmod_all = sall.reshape(N_DEV, -1)[:, :2 * 6 * D].reshape(N_DEV, 2, 6 * D)
    dmod_mine = lax.dynamic_slice_in_dim(dmod_all, me * (6 * D // N_DEV), 6 * D // N_DEV, axis=2)
    g_ada = [matmul([(cond_all, dmod_mine[:, l])], "tn", f"ada{l}_dw")[None] for l in range(2)]
    ada2d = lambda t: t.reshape(2 * D, 6 * D // N_DEV)
    for key, t in zip(("g", "d", "m", "v"), reduce_adamw(g_ada, ada2d(ada_w), ada2d(m_ada_w),
                                                          ada2d(v_ada_w), "adamw_ada_w")):
        out[key]['ada_w'] = t.reshape(ada_w.shape)
    own = []
    for (n, ax, shp), o in zip(SMALL_SHARDED, rep_offs[len(REPLICATED):]):
        slabs = _to_slabs(sg[o:o + _numel(shp)].reshape(shp), ax)
        own.append(lax.dynamic_index_in_dim(slabs, me, axis=0, keepdims=False))
    own_names = [n for n, _, _ in SMALL_SHARDED]
    pk = lambda d: _pack_rows(jnp.concatenate([d[n].reshape(-1) for n in own_names]), 1024, 8)
    og, od, om, ov = [t.reshape(-1) for t in reduce_adamw(_pack_rows(jnp.concatenate(own), 1024, 8)[None],
                                                          pk(W), pk(M), pk(V), "adamw_own")]

    def unpack(names_shapes, bufs):
        o = 0
        for n, shp in names_shapes:
            k = _numel(shp)
            for key, buf in zip(("g", "d", "m", "v"), bufs):
                out[key][n] = buf[o:o + k].reshape(shp)
            o += k

    unpack(REPLICATED, (sg, sd_, sm, sv))
    unpack([(n, W[n].shape) for n in own_names], (og, od, om, ov))
    return (loss, grad_x[None], *[out["g"][n] for n in WEIGHTS], *[out["d"][n] for n in WEIGHTS],
            *[out["m"][n] for n in WEIGHTS], *[out["v"][n] for n in WEIGHTS])
```
